```python
import math
import jax, jax.numpy as jnp
from jax import lax
import numpy as np

D_MODEL = 1024
BATCH = 8
SEQ = 8192
DEPTH = 1

N_MEM = 256
MIX_WIDTH = D_MODEL
CONV_WIDTH = MIX_WIDTH // 2
CONV_GROUPS = 8
CONV_K = 3
GLA_HEADS = 4
GLA_DV = (MIX_WIDTH - CONV_WIDTH) // GLA_HEADS
GLA_DK = GLA_DV // 2
GLA_K_TOTAL = GLA_HEADS * GLA_DK
GLA_V_TOTAL = GLA_HEADS * GLA_DV
GLA_LOWRANK = 16
GLA_GATE_NORM = 16.0
GLA_CHUNK = 64
XA_HEADS = 4
XA_HEAD_DIM = D_MODEL // XA_HEADS
D_FF = 4 * D_MODEL
EPS = 1e-6

SPLITS = [CONV_WIDTH, CONV_WIDTH, CONV_WIDTH,
          GLA_K_TOTAL, GLA_K_TOTAL, GLA_V_TOTAL, GLA_V_TOTAL,
          GLA_LOWRANK, GLA_LOWRANK]
W_IN_COLS = int(sum(SPLITS))

kernel_name = "hybrid_conv_gla_memxattn_encoder_block"


def rms_norm(x, g):
    xf = x.astype(jnp.float32)
    y = xf * lax.rsqrt(jnp.mean(xf * xf, axis=-1, keepdims=True) + EPS)
    return (y * g.astype(jnp.float32)).astype(x.dtype)


def split_cols(z):
    offs = np.cumsum(SPLITS)[:-1].tolist()
    return jnp.split(z, offs, axis=-1)


def short_conv_mixer(b_gate, c_gate, u, conv_w):
    h = c_gate * u
    hp = jnp.pad(h, ((0, 0), (1, 1), (0, 0)))
    y = (conv_w[0] * hp[:, :-2] + conv_w[1] * hp[:, 1:-1] + conv_w[2] * hp[:, 2:])
    return b_gate * y


def gla_chunked(q, k, v, log_a):
    bsz, s, h, dk = q.shape
    dv = v.shape[-1]
    nc = s // GLA_CHUNK

    def to_chunks(t):
        return t.reshape(bsz, nc, GLA_CHUNK, h, t.shape[-1]).transpose(0, 3, 1, 2, 4)

    q, k, v, la = to_chunks(q), to_chunks(k), to_chunks(v), to_chunks(log_a)
    b = jnp.cumsum(la, axis=3)
    q_t = q * jnp.exp(b)
    k_t = k * jnp.exp(-b)
    mask = jnp.tril(jnp.ones((GLA_CHUNK, GLA_CHUNK), dtype=bool))
    attn = jnp.einsum('bhncd,bhnsd->bhncs', q_t, k_t)
    attn = jnp.where(mask, attn, 0.0)
    o_intra = jnp.einsum('bhncs,bhnsv->bhncv', attn, v)

    g_tot = b[:, :, :, -1, :]
    k_hat = k * jnp.exp(g_tot[:, :, :, None, :] - b)
    u_chunk = jnp.einsum('bhncd,bhncv->bhndv', k_hat, v)

    def step(state, inp):
        g_n, u_n = inp
        new_state = jnp.exp(g_n)[..., None] * state + u_n
        return new_state, state

    s0 = jnp.zeros((bsz, h, dk, dv), jnp.float32)
    _, s_before = lax.scan(step, s0, (jnp.moveaxis(g_tot, 2, 0), jnp.moveaxis(u_chunk, 2, 0)))
    o_inter = jnp.einsum('bhncd,nbhdv->bhncv', q_t, s_before)
    o = o_intra + o_inter
    return o.transpose(0, 2, 3, 1, 4).reshape(bsz, s, h, dv)


def bidirectional_gla(q, k, v, la_f, la_b):
    fwd = gla_chunked(q, k, v, la_f)
    flip = lambda t: jnp.flip(t, axis=1)
    bwd = flip(gla_chunked(flip(q), flip(k), flip(v), flip(la_b)))
    diag = jnp.einsum('bshd,bshd->bsh', q, k)[..., None] * v
    return fwd + bwd - diag


def mixer_layer(x, mix_norm, w_in, conv_w, conv_norm, w_af, b_af, w_ab, b_ab, gla_norm, w_out):
    bsz, s, _ = x.shape
    h = rms_norm(x, mix_norm)
    z = h @ w_in
    cb, cc, cu, q, k, v, g, lr_f, lr_b = split_cols(z)

    y_conv = short_conv_mixer(cb, cc, cu, conv_w)
    y_conv = rms_norm(y_conv.reshape(bsz, s, CONV_GROUPS, CONV_WIDTH // CONV_GROUPS),
                      conv_norm.reshape(CONV_GROUPS, -1)).reshape(bsz, s, CONV_WIDTH)

    f32 = jnp.float32
    qh = q.astype(f32).reshape(bsz, s, GLA_HEADS, GLA_DK) * (GLA_DK ** -0.5)
    kh = k.astype(f32).reshape(bsz, s, GLA_HEADS, GLA_DK)
    vh = v.astype(f32).reshape(bsz, s, GLA_HEADS, GLA_DV)
    la_f = jax.nn.log_sigmoid((lr_f @ w_af + b_af).astype(f32)) / GLA_GATE_NORM
    la_b = jax.nn.log_sigmoid((lr_b @ w_ab + b_ab).astype(f32)) / GLA_GATE_NORM
    la_f = la_f.reshape(bsz, s, GLA_HEADS, GLA_DK)
    la_b = la_b.reshape(bsz, s, GLA_HEADS, GLA_DK)
    o = bidirectional_gla(qh, kh, vh, la_f, la_b)
    o = rms_norm(o, gla_norm).reshape(bsz, s, GLA_V_TOTAL).astype(x.dtype)
    y_gla = o * jax.nn.silu(g)

    y = jnp.concatenate([y_conv, y_gla], axis=-1)
    return y @ w_out


def memory_cross_attention(x, mem, xa_norm, mem_norm, w_xq, w_xkv, w_xo):
    bsz, s, _ = x.shape
    hq = rms_norm(x, xa_norm) @ w_xq
    kv = rms_norm(mem, mem_norm) @ w_xkv
    km, vm = jnp.split(kv, 2, axis=-1)
    qh = hq.reshape(bsz, s, XA_HEADS, XA_HEAD_DIM)
    kh = km.reshape(bsz, N_MEM, XA_HEADS, XA_HEAD_DIM)
    vh = vm.reshape(bsz, N_MEM, XA_HEADS, XA_HEAD_DIM)
    scores = jnp.einsum('bqhd,bmhd->bhqm', qh, kh).astype(jnp.float32) / math.sqrt(XA_HEAD_DIM)
    p = jax.nn.softmax(scores, axis=-1).astype(x.dtype)
    o = jnp.einsum('bhqm,bmhd->bqhd', p, vh).reshape(bsz, s, D_MODEL)
    return o @ w_xo


def sq_relu_mlp(x, mlp_norm, w_up, w_down):
    h = rms_norm(x, mlp_norm) @ w_up
    return jnp.square(jax.nn.relu(h)) @ w_down


def _fwd_setup_inputs(seed: int = 0) -> dict:
    key = jax.random.key(seed)
    ks = jax.random.split(key, 24)
    L, D = DEPTH, D_MODEL
    nrm = lambda k, shape, fan_in: jax.random.normal(k, shape, jnp.float32) * (fan_in ** -0.5)
    gain = lambda k, shape: 1.0 + 0.01 * jax.random.normal(k, shape, jnp.float32)
    bias = lambda k, shape: 0.1 * jax.random.normal(k, shape, jnp.float32)
    return {
        "x": jax.random.normal(ks[0], (BATCH, SEQ, D), jnp.float32),
        "mem": jax.random.normal(ks[1], (BATCH, N_MEM, D), jnp.float32),
        "mix_norm": gain(ks[2], (L, D)),
        "w_in": nrm(ks[3], (L, D, W_IN_COLS), D),
        "conv_w": nrm(ks[4], (L, CONV_K, CONV_WIDTH), CONV_K),
        "conv_norm": gain(ks[5], (L, CONV_WIDTH)),
        "w_af": nrm(ks[6], (L, GLA_LOWRANK, GLA_K_TOTAL), GLA_LOWRANK),
        "b_af": bias(ks[7], (L, GLA_K_TOTAL)),
        "w_ab": nrm(ks[8], (L, GLA_LOWRANK, GLA_K_TOTAL), GLA_LOWRANK),
        "b_ab": bias(ks[9], (L, GLA_K_TOTAL)),
        "gla_norm": gain(ks[10], (L, GLA_DV)),
        "w_out": nrm(ks[11], (L, MIX_WIDTH, D), MIX_WIDTH),
        "xa_norm": gain(ks[12], (L, D)),
        "mem_norm": gain(ks[13], (L, D)),
        "w_xq": nrm(ks[14], (L, D, D), D),
        "w_xkv": nrm(ks[15], (L, D, 2 * D), D),
        "w_xo": nrm(ks[16], (L, D, D), D),
        "mlp_norm": gain(ks[17], (L, D)),
        "w_up": nrm(ks[18], (L, D, D_FF), D),
        "w_down": nrm(ks[19], (L, D_FF, D), D_FF),
        "final_norm": gain(ks[20], (D,)),
    }


def _fwd_reference(x, mem, mix_norm, w_in, conv_w, conv_norm, w_af, b_af, w_ab, b_ab, gla_norm,
              w_out, xa_norm, mem_norm, w_xq, w_xkv, w_xo, mlp_norm, w_up, w_down, final_norm):
    for l in range(DEPTH):
        x = x + mixer_layer(x, mix_norm[l], w_in[l], conv_w[l], conv_norm[l], w_af[l], b_af[l],
                            w_ab[l], b_ab[l], gla_norm[l], w_out[l])
        x = x + memory_cross_attention(x, mem, xa_norm[l], mem_norm[l], w_xq[l], w_xkv[l], w_xo[l])
        x = x + sq_relu_mlp(x, mlp_norm[l], w_up[l], w_down[l])
    return rms_norm(x, final_norm)


import jax as _jax
import jax.numpy as _jnp

TWIN_FORMAT = 'train_step'
FWD_PARAMS = ['x', 'mem', 'mix_norm', 'w_in', 'conv_w', 'conv_norm', 'w_af', 'b_af', 'w_ab', 'b_ab', 'gla_norm', 'w_out', 'xa_norm', 'mem_norm', 'w_xq', 'w_xkv', 'w_xo', 'mlp_norm', 'w_up', 'w_down', 'final_norm']
TWIN_WEIGHTS = ['mix_norm', 'w_in', 'conv_w', 'conv_norm', 'w_af', 'b_af', 'w_ab', 'b_ab', 'gla_norm', 'w_out', 'xa_norm', 'mem_norm', 'w_xq', 'w_xkv', 'w_xo', 'mlp_norm', 'w_up', 'w_down', 'final_norm']
TWIN_DIFF_INPUT = 'x'
TWIN_INPUTS = ['x', 'mem', 'mix_norm', 'w_in', 'conv_w', 'conv_norm', 'w_af', 'b_af', 'w_ab', 'b_ab', 'gla_norm', 'w_out', 'xa_norm', 'mem_norm', 'w_xq', 'w_xkv', 'w_xo', 'mlp_norm', 'w_up', 'w_down', 'final_norm', 'loss_target', 'm_mix_norm', 'm_w_in', 'm_conv_w', 'm_conv_norm', 'm_w_af', 'm_b_af', 'm_w_ab', 'm_b_ab', 'm_gla_norm', 'm_w_out', 'm_xa_norm', 'm_mem_norm', 'm_w_xq', 'm_w_xkv', 'm_w_xo', 'm_mlp_norm', 'm_w_up', 'm_w_down', 'm_final_norm', 'v_mix_norm', 'v_w_in', 'v_conv_w', 'v_conv_norm', 'v_w_af', 'v_b_af', 'v_w_ab', 'v_b_ab', 'v_gla_norm', 'v_w_out', 'v_xa_norm', 'v_mem_norm', 'v_w_xq', 'v_w_xkv', 'v_w_xo', 'v_mlp_norm', 'v_w_up', 'v_w_down', 'v_final_norm']
TWIN_OUTPUTS = ['loss', 'grad_x', 'grad_mix_norm', 'grad_w_in', 'grad_conv_w', 'grad_conv_norm', 'grad_w_af', 'grad_b_af', 'grad_w_ab', 'grad_b_ab', 'grad_gla_norm', 'grad_w_out', 'grad_xa_norm', 'grad_mem_norm', 'grad_w_xq', 'grad_w_xkv', 'grad_w_xo', 'grad_mlp_norm', 'grad_w_up', 'grad_w_down', 'grad_final_norm', 'delta_mix_norm', 'delta_w_in', 'delta_conv_w', 'delta_conv_norm', 'delta_w_af', 'delta_b_af', 'delta_w_ab', 'delta_b_ab', 'delta_gla_norm', 'delta_w_out', 'delta_xa_norm', 'delta_mem_norm', 'delta_w_xq', 'delta_w_xkv', 'delta_w_xo', 'delta_mlp_norm', 'delta_w_up', 'delta_w_down', 'delta_final_norm', 'new_m_mix_norm', 'new_m_w_in', 'new_m_conv_w', 'new_m_conv_norm', 'new_m_w_af', 'new_m_b_af', 'new_m_w_ab', 'new_m_b_ab', 'new_m_gla_norm', 'new_m_w_out', 'new_m_xa_norm', 'new_m_mem_norm', 'new_m_w_xq', 'new_m_w_xkv', 'new_m_w_xo', 'new_m_mlp_norm', 'new_m_w_up', 'new_m_w_down', 'new_m_final_norm', 'new_v_mix_norm', 'new_v_w_in', 'new_v_conv_w', 'new_v_conv_norm', 'new_v_w_af', 'new_v_b_af', 'new_v_w_ab', 'new_v_b_ab', 'new_v_gla_norm', 'new_v_w_out', 'new_v_xa_norm', 'new_v_mem_norm', 'new_v_w_xq', 'new_v_w_xkv', 'new_v_w_xo', 'new_v_mlp_norm', 'new_v_w_up', 'new_v_w_down', 'new_v_final_norm']
TWIN_LEAF_KINDS = {'loss': 'loss', 'grad_x': 'grad_x', 'grad_mix_norm': 'grad_w', 'grad_w_in': 'grad_w', 'grad_conv_w': 'grad_w', 'grad_conv_norm': 'grad_w', 'grad_w_af': 'grad_w', 'grad_b_af': 'grad_w', 'grad_w_ab': 'grad_w', 'grad_b_ab': 'grad_w', 'grad_gla_norm': 'grad_w', 'grad_w_out': 'grad_w', 'grad_xa_norm': 'grad_w', 'grad_mem_norm': 'grad_w', 'grad_w_xq': 'grad_w', 'grad_w_xkv': 'grad_w', 'grad_w_xo': 'grad_w', 'grad_mlp_norm': 'grad_w', 'grad_w_up': 'grad_w', 'grad_w_down': 'grad_w', 'grad_final_norm': 'grad_w', 'delta_mix_norm': 'delta_w', 'delta_w_in': 'delta_w', 'delta_conv_w': 'delta_w', 'delta_conv_norm': 'delta_w', 'delta_w_af': 'delta_w', 'delta_b_af': 'delta_w', 'delta_w_ab': 'delta_w', 'delta_b_ab': 'delta_w', 'delta_gla_norm': 'delta_w', 'delta_w_out': 'delta_w', 'delta_xa_norm': 'delta_w', 'delta_mem_norm': 'delta_w', 'delta_w_xq': 'delta_w', 'delta_w_xkv': 'delta_w', 'delta_w_xo': 'delta_w', 'delta_mlp_norm': 'delta_w', 'delta_w_up': 'delta_w', 'delta_w_down': 'delta_w', 'delta_final_norm': 'delta_w', 'new_m_mix_norm': 'new_m', 'new_m_w_in': 'new_m', 'new_m_conv_w': 'new_m', 'new_m_conv_norm': 'new_m', 'new_m_w_af': 'new_m', 'new_m_b_af': 'new_m', 'new_m_w_ab': 'new_m', 'new_m_b_ab': 'new_m', 'new_m_gla_norm': 'new_m', 'new_m_w_out': 'new_m', 'new_m_xa_norm': 'new_m', 'new_m_mem_norm': 'new_m', 'new_m_w_xq': 'new_m', 'new_m_w_xkv': 'new_m', 'new_m_w_xo': 'new_m', 'new_m_mlp_norm': 'new_m', 'new_m_w_up': 'new_m', 'new_m_w_down': 'new_m', 'new_m_final_norm': 'new_m', 'new_v_mix_norm': 'new_v', 'new_v_w_in': 'new_v', 'new_v_conv_w': 'new_v', 'new_v_conv_norm': 'new_v', 'new_v_w_af': 'new_v', 'new_v_b_af': 'new_v', 'new_v_w_ab': 'new_v', 'new_v_b_ab': 'new_v', 'new_v_gla_norm': 'new_v', 'new_v_w_out': 'new_v', 'new_v_xa_norm': 'new_v', 'new_v_mem_norm': 'new_v', 'new_v_w_xq': 'new_v', 'new_v_w_xkv': 'new_v', 'new_v_w_xo': 'new_v', 'new_v_mlp_norm': 'new_v', 'new_v_w_up': 'new_v', 'new_v_w_down': 'new_v', 'new_v_final_norm': 'new_v'}


def _forward(args):
    return _fwd_reference(*[args[k] for k in FWD_PARAMS])


def _output_shape():
    def fwd():
        inp = _fwd_setup_inputs(0)
        return _fwd_reference(*[inp[k] for k in FWD_PARAMS])
    out = _jax.eval_shape(fwd)
    return out.shape, out.dtype

N_MICROBATCH = 1
ADAM_LR = 0.001
ADAM_B1 = 0.9
ADAM_B2 = 0.999
ADAM_EPS = 1e-08
ADAM_WD = 0.01
ADAM_STEP = 10
PER_EXAMPLE_BATCH_AXIS = {'x': 0, 'mem': 0, 'loss_target': 0}
SHARED_INPUTS = []
_WEIGHT_DTYPES = {'mix_norm': _jnp.float32, 'w_in': _jnp.float32, 'conv_w': _jnp.float32, 'conv_norm': _jnp.float32, 'w_af': _jnp.float32, 'b_af': _jnp.float32, 'w_ab': _jnp.float32, 'b_ab': _jnp.float32, 'gla_norm': _jnp.float32, 'w_out': _jnp.float32, 'xa_norm': _jnp.float32, 'mem_norm': _jnp.float32, 'w_xq': _jnp.float32, 'w_xkv': _jnp.float32, 'w_xo': _jnp.float32, 'mlp_norm': _jnp.float32, 'w_up': _jnp.float32, 'w_down': _jnp.float32, 'final_norm': _jnp.float32}
MOMENT_SCALE = {'mix_norm': 3.305808e-01, 'w_in': 1.852847e-01, 'conv_w': 2.433821e-01, 'conv_norm': 2.136451e-01, 'w_af': 1.379663e-02, 'b_af': 6.925647e-02, 'w_ab': 1.420639e-02, 'b_ab': 6.198017e-02, 'gla_norm': 2.599708e-01, 'w_out': 1.717412e-01, 'xa_norm': 2.371975e-02, 'mem_norm': 3.278734e-02, 'w_xq': 2.194061e-02, 'w_xkv': 2.219400e-02, 'w_xo': 2.233933e-02, 'mlp_norm': 2.174314e-01, 'w_up': 9.948577e-02, 'w_down': 1.829087e-01, 'final_norm': 6.442213e+01}


def _to_microbatches(a, axis):
    t = _jnp.moveaxis(a, axis, 0)
    t = t.reshape((N_MICROBATCH, t.shape[0] // N_MICROBATCH) + t.shape[1:])
    return _jnp.moveaxis(t, 1, axis + 1)


def setup_inputs(seed: int = 0) -> dict:
    inp = _fwd_setup_inputs(seed)
    key = _jax.random.fold_in(_jax.random.key(seed), 7919)
    shape, _ = _output_shape()
    out = dict(inp)
    out["loss_target"] = _jax.random.normal(_jax.random.fold_in(key, 0), shape, _jnp.float32)
    for i, name in enumerate(TWIN_WEIGHTS):
        w = inp[name].astype(_jnp.float32)
        if MOMENT_SCALE is None:
            s = _jnp.sqrt(_jnp.mean(_jnp.square(w)) + 1e-30)
        else:
            s = MOMENT_SCALE[name]
        km, kv = _jax.random.split(_jax.random.fold_in(key, i + 1))
        out[name] = w
        out["m_" + name] = s * _jax.random.normal(km, w.shape, _jnp.float32)
        out["v_" + name] = (s * s) * _jax.random.uniform(kv, w.shape, _jnp.float32, 0.5, 1.5)
    if N_MICROBATCH > 1:
        for name, axis in PER_EXAMPLE_BATCH_AXIS.items():
            out[name] = _to_microbatches(out[name], axis)
    return {'x': out['x'], 'mem': out['mem'], 'mix_norm': out['mix_norm'], 'w_in': out['w_in'], 'conv_w': out['conv_w'], 'conv_norm': out['conv_norm'], 'w_af': out['w_af'], 'b_af': out['b_af'], 'w_ab': out['w_ab'], 'b_ab': out['b_ab'], 'gla_norm': out['gla_norm'], 'w_out': out['w_out'], 'xa_norm': out['xa_norm'], 'mem_norm': out['mem_norm'], 'w_xq': out['w_xq'], 'w_xkv': out['w_xkv'], 'w_xo': out['w_xo'], 'mlp_norm': out['mlp_norm'], 'w_up': out['w_up'], 'w_down': out['w_down'], 'final_norm': out['final_norm'], 'loss_target': out['loss_target'], 'm_mix_norm': out['m_mix_norm'], 'm_w_in': out['m_w_in'], 'm_conv_w': out['m_conv_w'], 'm_conv_norm': out['m_conv_norm'], 'm_w_af': out['m_w_af'], 'm_b_af': out['m_b_af'], 'm_w_ab': out['m_w_ab'], 'm_b_ab': out['m_b_ab'], 'm_gla_norm': out['m_gla_norm'], 'm_w_out': out['m_w_out'], 'm_xa_norm': out['m_xa_norm'], 'm_mem_norm': out['m_mem_norm'], 'm_w_xq': out['m_w_xq'], 'm_w_xkv': out['m_w_xkv'], 'm_w_xo': out['m_w_xo'], 'm_mlp_norm': out['m_mlp_norm'], 'm_w_up': out['m_w_up'], 'm_w_down': out['m_w_down'], 'm_final_norm': out['m_final_norm'], 'v_mix_norm': out['v_mix_norm'], 'v_w_in': out['v_w_in'], 'v_conv_w': out['v_conv_w'], 'v_conv_norm': out['v_conv_norm'], 'v_w_af': out['v_w_af'], 'v_b_af': out['v_b_af'], 'v_w_ab': out['v_w_ab'], 'v_b_ab': out['v_b_ab'], 'v_gla_norm': out['v_gla_norm'], 'v_w_out': out['v_w_out'], 'v_xa_norm': out['v_xa_norm'], 'v_mem_norm': out['v_mem_norm'], 'v_w_xq': out['v_w_xq'], 'v_w_xkv': out['v_w_xkv'], 'v_w_xo': out['v_w_xo'], 'v_mlp_norm': out['v_mlp_norm'], 'v_w_up': out['v_w_up'], 'v_w_down': out['v_w_down'], 'v_final_norm': out['v_final_norm']}


def _loss(weights, diff, rest, loss_target):
    with _jax.named_scope("forward"):
        args = {**rest, TWIN_DIFF_INPUT: diff, **{k: w.astype(_WEIGHT_DTYPES[k]) for k, w in weights.items()}}
        y = _forward(args)
    with _jax.named_scope("loss_head"):
        err = _jnp.square(y.astype(_jnp.float32) - loss_target)
        return 0.5 * _jnp.sum(_jnp.mean(err, axis=-1)) if err.ndim else 0.5 * err


def _adamw(w, g, m, v):
    m = ADAM_B1 * m + (1.0 - ADAM_B1) * g
    v = ADAM_B2 * v + (1.0 - ADAM_B2) * _jnp.square(g)
    m_hat = m / (1.0 - ADAM_B1 ** ADAM_STEP)
    v_hat = v / (1.0 - ADAM_B2 ** ADAM_STEP)
    delta = -ADAM_LR * (m_hat / (_jnp.sqrt(v_hat) + ADAM_EPS) + ADAM_WD * w)
    return delta, m, v


def reference(x, mem, mix_norm, w_in, conv_w, conv_norm, w_af, b_af, w_ab, b_ab, gla_norm, w_out, xa_norm, mem_norm, w_xq, w_xkv, w_xo, mlp_norm, w_up, w_down, final_norm, loss_target, m_mix_norm, m_w_in, m_conv_w, m_conv_norm, m_w_af, m_b_af, m_w_ab, m_b_ab, m_gla_norm, m_w_out, m_xa_norm, m_mem_norm, m_w_xq, m_w_xkv, m_w_xo, m_mlp_norm, m_w_up, m_w_down, m_final_norm, v_mix_norm, v_w_in, v_conv_w, v_conv_norm, v_w_af, v_b_af, v_w_ab, v_b_ab, v_gla_norm, v_w_out, v_xa_norm, v_mem_norm, v_w_xq, v_w_xkv, v_w_xo, v_mlp_norm, v_w_up, v_w_down, v_final_norm):
    given = dict(x=x, mem=mem, mix_norm=mix_norm, w_in=w_in, conv_w=conv_w, conv_norm=conv_norm, w_af=w_af, b_af=b_af, w_ab=w_ab, b_ab=b_ab, gla_norm=gla_norm, w_out=w_out, xa_norm=xa_norm, mem_norm=mem_norm, w_xq=w_xq, w_xkv=w_xkv, w_xo=w_xo, mlp_norm=mlp_norm, w_up=w_up, w_down=w_down, final_norm=final_norm, loss_target=loss_target, m_mix_norm=m_mix_norm, m_w_in=m_w_in, m_conv_w=m_conv_w, m_conv_norm=m_conv_norm, m_w_af=m_w_af, m_b_af=m_b_af, m_w_ab=m_w_ab, m_b_ab=m_b_ab, m_gla_norm=m_gla_norm, m_w_out=m_w_out, m_xa_norm=m_xa_norm, m_mem_norm=m_mem_norm, m_w_xq=m_w_xq, m_w_xkv=m_w_xkv, m_w_xo=m_w_xo, m_mlp_norm=m_mlp_norm, m_w_up=m_w_up, m_w_down=m_w_down, m_final_norm=m_final_norm, v_mix_norm=v_mix_norm, v_w_in=v_w_in, v_conv_w=v_conv_w, v_conv_norm=v_conv_norm, v_w_af=v_w_af, v_b_af=v_b_af, v_w_ab=v_w_ab, v_b_ab=v_b_ab, v_gla_norm=v_gla_norm, v_w_out=v_w_out, v_xa_norm=v_xa_norm, v_mem_norm=v_mem_norm, v_w_xq=v_w_xq, v_w_xkv=v_w_xkv, v_w_xo=v_w_xo, v_mlp_norm=v_mlp_norm, v_w_up=v_w_up, v_w_down=v_w_down, v_final_norm=v_final_norm)
    weights = {n: given[n] for n in TWIN_WEIGHTS}
    shared = {n: given[n] for n in SHARED_INPUTS}
    per_example = {n: given[n] for n in ['x', 'mem']}
    grad_fn = _jax.value_and_grad(_loss, argnums=(0, 1))

    def one_microbatch(ex, loss_target):
        ex = dict(ex)
        diff = ex.pop(TWIN_DIFF_INPUT)
        return grad_fn(weights, diff, {**shared, **ex}, loss_target)

    if N_MICROBATCH == 1:
        loss, (grad_w, grad_x) = one_microbatch(per_example, given["loss_target"])
    else:
        def body(carry, xs):
            loss_sum, grad_sum = carry
            l_k, (gw_k, gx_k) = one_microbatch(xs[0], xs[1])
            with _jax.named_scope("update"):
                return (loss_sum + l_k, _jax.tree.map(_jnp.add, grad_sum, gw_k)), gx_k

        init = (_jnp.zeros((), _jnp.float32), _jax.tree.map(_jnp.zeros_like, weights))
        (loss, grad_w), grad_x = _jax.lax.scan(body, init, (per_example, given["loss_target"]))
    with _jax.named_scope("update"):
        delta_w, new_m, new_v = {}, {}, {}
        for n in TWIN_WEIGHTS:
            delta_w[n], new_m[n], new_v[n] = _adamw(weights[n], grad_w[n], given["m_" + n], given["v_" + n])
    return (loss, grad_x, *[grad_w[n] for n in TWIN_WEIGHTS], *[delta_w[n] for n in TWIN_WEIGHTS],
            *[new_m[n] for n in TWIN_WEIGHTS], *[new_v[n] for n in TWIN_WEIGHTS])
```

```python
import functools

import jax
import jax.numpy as jnp
from jax import lax
from jax.experimental import pallas as pl
from jax.experimental.pallas import tpu as pltpu

F32 = jnp.float32
BF16 = jnp.bfloat16

D = 1024
CW = 512
GK = 256
GV = 512
NH = 4
CH = 64
LR = 16
NMEM = 256
XD = 256
FF = 4096
ZW = 3104
ZC = 3200
EPS = 1e-6
NDEV = 8

ZB_CB, ZB_CC, ZB_CU, ZB_V, ZB_G = 0, 1, 2, 4, 5
ZB_Q, ZB_K = 6, 7
ZB_LR = 24

TM = 512
TM_MLP = 1024
TF = 512
TB = 256
TT = 512
VMEM_LIMIT = 56 * 1024 * 1024

ADAM_LR, ADAM_B1, ADAM_B2, ADAM_EPS, ADAM_WD, ADAM_STEP = 0.001, 0.9, 0.999, 1e-08, 0.01, 10

ROWS_WIN, ROWS_SQ, ROWS_XKV, ROWS_FF, ROWS_SMALL = 400, 128, 256, 512, 16
OFF_WIN = 0
OFF_WOUT = OFF_WIN + ROWS_WIN
OFF_WXQ = OFF_WOUT + ROWS_SQ
OFF_WXO = OFF_WXQ + ROWS_SQ
OFF_WXKV = OFF_WXO + ROWS_SQ
OFF_WUP = OFF_WXKV + ROWS_XKV
OFF_WDOWN = OFF_WUP + ROWS_FF
OFF_SMALL = OFF_WDOWN + ROWS_FF
NR_G = OFF_SMALL
NR_W = OFF_SMALL + ROWS_SMALL
SMALL_ROWS = 128

MESH = pl.DeviceIdType.MESH


def _cparams(sem):
    return pltpu.CompilerParams(dimension_semantics=sem, vmem_limit_bytes=VMEM_LIMIT)


def _dot(a, b):
    return jnp.dot(a.astype(BF16), b.astype(BF16), preferred_element_type=F32)


def _dot_nt(a, b):
    return lax.dot_general(a.astype(BF16), b.astype(BF16), (((1,), (1,)), ((), ())), preferred_element_type=F32)


def _dot_tn(a, b):
    return lax.dot_general(a.astype(BF16), b.astype(BF16), (((0,), (0,)), ((), ())), preferred_element_type=F32)


def _split(x, n):
    parts = []
    for _ in range(n):
        p = x.astype(BF16)
        parts.append(p)
        x = x - p.astype(F32)
    return parts


def _dot_exact_lhs(m, x, n):
    return functools.reduce(lambda a, b: a + b, [jnp.dot(m, p, preferred_element_type=F32) for p in _split(x, n)])


def _dot_exact_rhs(x, m, n):
    return functools.reduce(lambda a, b: a + b, [jnp.dot(p, m, preferred_element_type=F32) for p in _split(x, n)])


def _rms(x, g):
    r = lax.rsqrt(jnp.mean(x * x, axis=-1, keepdims=True) + EPS)
    return x * r * g, r


def _rms_bwd(x, r, g, dy):
    xr = x * r
    u = dy * g
    dx = r * (u - xr * jnp.mean(u * xr, axis=-1, keepdims=True))
    return dx, jnp.sum(dy * xr, axis=0, keepdims=True)


def _iota(shape, dim):
    return lax.broadcasted_iota(jnp.int32, shape, dim)


def _sigmoid(x):
    return 1.0 / (1.0 + jnp.exp(-x))


def _acc_rows(ref, row):
    ref[...] += jnp.broadcast_to(row, ref.shape)


def _inproj(x, g, w):
    t = x.shape[0]
    tm = min(TM, t)

    def body(x_ref, g_ref, w_ref, z_ref, h_ref):
        h, _ = _rms(x_ref[...], g_ref[...])
        hb = h.astype(BF16)
        h_ref[...] = hb
        z_ref[...] = jnp.dot(hb, w_ref[...], preferred_element_type=F32)

    return pl.pallas_call(
        body, name="inproj", grid=(t // tm,),
        in_specs=[pl.BlockSpec((tm, D), lambda i: (i, 0)), pl.BlockSpec((1, D), lambda i: (0, 0)),
                  pl.BlockSpec((D, ZC), lambda i: (0, 0))],
        out_specs=[pl.BlockSpec((tm, ZC), lambda i: (i, 0)), pl.BlockSpec((tm, D), lambda i: (i, 0))],
        out_shape=[jax.ShapeDtypeStruct((t, ZC), F32), jax.ShapeDtypeStruct((t, D), BF16)],
        compiler_params=_cparams(("arbitrary",)))(x, g, w)


def _kv_proj(mem, g, w):
    def body(m_ref, g_ref, w_ref, kv_ref, mn_ref):
        mn, _ = _rms(m_ref[...], g_ref[...])
        mb = mn.astype(BF16)
        mn_ref[...] = mb
        kv_ref[...] = jnp.dot(mb, w_ref[...], preferred_element_type=F32)

    return pl.pallas_call(
        body, name="kv_proj",
        out_shape=[jax.ShapeDtypeStruct((NMEM, 2 * D), F32), jax.ShapeDtypeStruct((NMEM, D), BF16)],
        compiler_params=pltpu.CompilerParams(vmem_limit_bytes=VMEM_LIMIT))(mem, g, w)


def _softmax_head(qb, kb):
    s = _dot_nt(qb, kb) * (1.0 / 16.0)
    e = jnp.exp(s - jnp.max(s, axis=-1, keepdims=True))
    return e / jnp.sum(e, axis=-1, keepdims=True)


def _attn_fwd(x, yb, w_out, g, w_xq, kb, vb, w_xo):
    t = x.shape[0]
    tm = min(TM, t)

    def body(x_ref, y_ref, wo_ref, g_ref, wq_ref, k_ref, v_ref, wx_ref, x1_ref, x2_ref, xn_ref, q_ref, a_ref):
        x1 = x_ref[...] + jnp.dot(y_ref[...], wo_ref[...], preferred_element_type=F32)
        x1_ref[...] = x1
        xn, _ = _rms(x1, g_ref[...])
        xb = xn.astype(BF16)
        xn_ref[...] = xb
        qb = jnp.dot(xb, wq_ref[...], preferred_element_type=F32).astype(BF16)
        q_ref[...] = qb
        for h in range(NH):
            hs = slice(h * XD, (h + 1) * XD)
            p = _softmax_head(qb[:, hs], k_ref[:, hs])
            a_ref[:, hs] = _dot(p, v_ref[:, hs]).astype(BF16)
        x2_ref[...] = x1 + jnp.dot(a_ref[...], wx_ref[...], preferred_element_type=F32)

    tok = lambda i: (i, 0)
    full = lambda i: (0, 0)
    return pl.pallas_call(
        body, name="attn_fwd", grid=(t // tm,),
        in_specs=[pl.BlockSpec((tm, D), tok), pl.BlockSpec((tm, D), tok), pl.BlockSpec((D, D), full),
                  pl.BlockSpec((1, D), full), pl.BlockSpec((D, D), full), pl.BlockSpec((NMEM, D), full),
                  pl.BlockSpec((NMEM, D), full), pl.BlockSpec((D, D), full)],
        out_specs=[pl.BlockSpec((tm, D), tok)] * 5,
        out_shape=[jax.ShapeDtypeStruct((t, D), F32), jax.ShapeDtypeStruct((t, D), F32),
                   jax.ShapeDtypeStruct((t, D), BF16), jax.ShapeDtypeStruct((t, D), BF16),
                   jax.ShapeDtypeStruct((t, D), BF16)],
        compiler_params=_cparams(("arbitrary",)))(x, yb, w_out, g, w_xq, kb, vb, w_xo)


def _mlp_fwd(x2, g, w_up, w_down, fg, target):
    t = x2.shape[0]
    tm = min(TM_MLP, t)
    nj = FF // TF

    def body(x_ref, g_ref, wu_ref, wd_ref, fg_ref, t_ref, h1_ref, xn_ref, dx_ref, dxb_ref, loss_ref, dfg_ref,
             acc, xnb):
        i, j = pl.program_id(0), pl.program_id(1)

        @pl.when(j == 0)
        def _():
            xn, _ = _rms(x_ref[...], g_ref[...])
            xnb[...] = xn.astype(BF16)
            xn_ref[...] = xnb[...]
            acc[...] = jnp.zeros_like(acc)

        @pl.when((i == 0) & (j == 0))
        def _():
            loss_ref[...] = jnp.zeros_like(loss_ref)
            dfg_ref[...] = jnp.zeros_like(dfg_ref)

        h1 = jnp.dot(xnb[...], wu_ref[...], preferred_element_type=F32)
        h1_ref[...] = h1
        hr = jnp.maximum(h1, 0.0)
        acc[...] += _dot(hr * hr, wd_ref[...])

        @pl.when(j == nj - 1)
        def _():
            x3 = x_ref[...] + acc[...]
            y, r = _rms(x3, fg_ref[...])
            e = y - t_ref[...]
            row = jnp.mean(e * e, axis=-1, keepdims=True)
            _acc_rows(loss_ref, 0.5 * jnp.sum(row, axis=0, keepdims=True))
            dx, dfg = _rms_bwd(x3, r, fg_ref[...], e * (1.0 / D))
            dx_ref[...] = dx
            dxb_ref[...] = dx.astype(BF16)
            _acc_rows(dfg_ref, dfg)

    tok = lambda i, j: (i, 0)
    full = lambda i, j: (0, 0)
    return pl.pallas_call(
        body, name="mlp_fwd", grid=(t // tm, nj),
        in_specs=[pl.BlockSpec((tm, D), tok), pl.BlockSpec((1, D), full), pl.BlockSpec((D, TF), lambda i, j: (0, j)),
                  pl.BlockSpec((TF, D), lambda i, j: (j, 0)), pl.BlockSpec((1, D), full), pl.BlockSpec((tm, D), tok)],
        out_specs=[pl.BlockSpec((tm, TF), lambda i, j: (i, j)), pl.BlockSpec((tm, D), tok), pl.BlockSpec((tm, D), tok),
                   pl.BlockSpec((tm, D), tok), pl.BlockSpec((8, 128), full), pl.BlockSpec((8, D), full)],
        out_shape=[jax.ShapeDtypeStruct((t, FF), F32), jax.ShapeDtypeStruct((t, D), BF16),
                   jax.ShapeDtypeStruct((t, D), F32), jax.ShapeDtypeStruct((t, D), BF16),
                   jax.ShapeDtypeStruct((8, 128), F32), jax.ShapeDtypeStruct((8, D), F32)],
        scratch_shapes=[pltpu.VMEM((tm, D), F32), pltpu.VMEM((tm, D), BF16)],
        compiler_params=_cparams(("arbitrary", "arbitrary")))(x2, g, w_up, w_down, fg, target)


def _mlp_bwd(dx3, dx3b, h1, w_down, w_up, x2, g):
    t = x2.shape[0]
    tm = min(TM_MLP, t)
    nj = FF // TF

    def body(dx_ref, dxb_ref, h1_ref, wd_ref, wu_ref, x_ref, g_ref, a_ref, dh_ref, dx2_ref, dx2b_ref, dg_ref, acc):
        i, j = pl.program_id(0), pl.program_id(1)

        @pl.when(j == 0)
        def _():
            acc[...] = jnp.zeros_like(acc)

        @pl.when((i == 0) & (j == 0))
        def _():
            dg_ref[...] = jnp.zeros_like(dg_ref)

        hr = jnp.maximum(h1_ref[...], 0.0)
        da = _dot_nt(dxb_ref[...], wd_ref[...])
        dh = (da * 2.0 * hr).astype(BF16)
        a_ref[...] = (hr * hr).astype(BF16)
        dh_ref[...] = dh
        acc[...] += _dot_nt(dh, wu_ref[...])

        @pl.when(j == nj - 1)
        def _():
            x = x_ref[...]
            r = lax.rsqrt(jnp.mean(x * x, axis=-1, keepdims=True) + EPS)
            dx, dg = _rms_bwd(x, r, g_ref[...], acc[...])
            dx2 = dx_ref[...] + dx
            dx2_ref[...] = dx2
            dx2b_ref[...] = dx2.astype(BF16)
            _acc_rows(dg_ref, dg)

    tok = lambda i, j: (i, 0)
    full = lambda i, j: (0, 0)
    hid = lambda i, j: (i, j)
    return pl.pallas_call(
        body, name="mlp_bwd", grid=(t // tm, nj),
        in_specs=[pl.BlockSpec((tm, D), tok), pl.BlockSpec((tm, D), tok), pl.BlockSpec((tm, TF), hid),
                  pl.BlockSpec((TF, D), lambda i, j: (j, 0)), pl.BlockSpec((D, TF), lambda i, j: (0, j)),
                  pl.BlockSpec((tm, D), tok), pl.BlockSpec((1, D), full)],
        out_specs=[pl.BlockSpec((tm, TF), hid), pl.BlockSpec((tm, TF), hid), pl.BlockSpec((tm, D), tok),
                   pl.BlockSpec((tm, D), tok), pl.BlockSpec((8, D), full)],
        out_shape=[jax.ShapeDtypeStruct((t, FF), BF16), jax.ShapeDtypeStruct((t, FF), BF16),
                   jax.ShapeDtypeStruct((t, D), F32), jax.ShapeDtypeStruct((t, D), BF16),
                   jax.ShapeDtypeStruct((8, D), F32)],
        scratch_shapes=[pltpu.VMEM((tm, D), F32)],
        compiler_params=_cparams(("arbitrary", "arbitrary")))(dx3, dx3b, h1, w_down, w_up, x2, g)


def _attn_bwd(x1, dx2, dx2b, qb, kb, vb, w_xo, w_xq, w_out, g):
    t = x1.shape[0]
    tm = min(TM, t)

    def body(x_ref, dx2_ref, dx2b_ref, q_ref, k_ref, v_ref, wx_ref, wq_ref, wo_ref, g_ref,
             dx1_ref, dx1b_ref, dy_ref, dq_ref, dkv_ref, dg_ref):
        @pl.when(pl.program_id(0) == 0)
        def _():
            dkv_ref[...] = jnp.zeros_like(dkv_ref)
            dg_ref[...] = jnp.zeros_like(dg_ref)

        datt = _dot_nt(dx2b_ref[...], wx_ref[...]).astype(BF16)
        for h in range(NH):
            hs = slice(h * XD, (h + 1) * XD)
            q_h, k_h, v_h, da_h = q_ref[:, hs], k_ref[:, hs], v_ref[:, hs], datt[:, hs]
            p = _softmax_head(q_h, k_h)
            dp = _dot_nt(da_h, v_h)
            ds = (p * (dp - jnp.sum(dp * p, axis=-1, keepdims=True)) * (1.0 / 16.0)).astype(BF16)
            dq_ref[:, hs] = _dot(ds, k_h).astype(BF16)
            dkv_ref[:, hs] += _dot_tn(ds, q_h)
            dkv_ref[:, D + h * XD:D + (h + 1) * XD] += _dot_tn(p, da_h)
        dxn = _dot_nt(dq_ref[...], wq_ref[...])
        x = x_ref[...]
        r = lax.rsqrt(jnp.mean(x * x, axis=-1, keepdims=True) + EPS)
        dx, dg = _rms_bwd(x, r, g_ref[...], dxn)
        dx1 = dx2_ref[...] + dx
        dx1_ref[...] = dx1
        dx1b = dx1.astype(BF16)
        dx1b_ref[...] = dx1b
        dy_ref[...] = _dot_nt(dx1b, wo_ref[...])
        _acc_rows(dg_ref, dg)

    tok = lambda i: (i, 0)
    full = lambda i: (0, 0)
    return pl.pallas_call(
        body, name="attn_bwd", grid=(t // tm,),
        in_specs=[pl.BlockSpec((tm, D), tok), pl.BlockSpec((tm, D), tok), pl.BlockSpec((tm, D), tok),
                  pl.BlockSpec((tm, D), tok), pl.BlockSpec((NMEM, D), full), pl.BlockSpec((NMEM, D), full),
                  pl.BlockSpec((D, D), full), pl.BlockSpec((D, D), full), pl.BlockSpec((D, D), full),
                  pl.BlockSpec((1, D), full)],
        out_specs=[pl.BlockSpec((tm, D), tok), pl.BlockSpec((tm, D), tok), pl.BlockSpec((tm, D), tok),
                   pl.BlockSpec((tm, D), tok), pl.BlockSpec((NMEM, 2 * D), full), pl.BlockSpec((8, D), full)],
        out_shape=[jax.ShapeDtypeStruct((t, D), F32), jax.ShapeDtypeStruct((t, D), BF16),
                   jax.ShapeDtypeStruct((t, D), F32), jax.ShapeDtypeStruct((t, D), BF16),
                   jax.ShapeDtypeStruct((NMEM, 2 * D), F32), jax.ShapeDtypeStruct((8, D), F32)],
        compiler_params=_cparams(("arbitrary",)))(x1, dx2, dx2b, qb, kb, vb, w_xo, w_xq, w_out, g)


def _kv_bwd(dkv, memn, mem, g, w):
    def body(dkv_ref, mn_ref, m_ref, g_ref, w_ref, dw_ref, dg_ref):
        dkvb = dkv_ref[...].astype(BF16)
        dw_ref[...] = _dot_tn(mn_ref[...], dkvb)
        dmn = _dot_nt(dkvb, w_ref[...])
        m = m_ref[...]
        r = lax.rsqrt(jnp.mean(m * m, axis=-1, keepdims=True) + EPS)
        dg_ref[...] = jnp.broadcast_to(jnp.sum(dmn * m * r, axis=0, keepdims=True), dg_ref.shape)

    return pl.pallas_call(
        body, name="kv_bwd",
        out_shape=[jax.ShapeDtypeStruct((D, 2 * D), F32), jax.ShapeDtypeStruct((8, D), F32)],
        compiler_params=pltpu.CompilerParams(vmem_limit_bytes=VMEM_LIMIT))(dkv, memn, mem, g, w)


def _inproj_bwd(dz, w, x, dx1, g):
    t = x.shape[0]
    tm = min(TM, t)

    def body(dz_ref, w_ref, x_ref, dx1_ref, g_ref, gx_ref, dg_ref):
        @pl.when(pl.program_id(0) == 0)
        def _():
            dg_ref[...] = jnp.zeros_like(dg_ref)

        dh = _dot_nt(dz_ref[...], w_ref[...])
        x = x_ref[...]
        r = lax.rsqrt(jnp.mean(x * x, axis=-1, keepdims=True) + EPS)
        dx, dg = _rms_bwd(x, r, g_ref[...], dh)
        gx_ref[...] = dx1_ref[...] + dx
        _acc_rows(dg_ref, dg)

    tok = lambda i: (i, 0)
    full = lambda i: (0, 0)
    return pl.pallas_call(
        body, name="inproj_bwd", grid=(t // tm,),
        in_specs=[pl.BlockSpec((tm, ZC), tok), pl.BlockSpec((D, ZC), full), pl.BlockSpec((tm, D), tok),
                  pl.BlockSpec((tm, D), tok), pl.BlockSpec((1, D), full)],
        out_specs=[pl.BlockSpec((tm, D), tok), pl.BlockSpec((8, D), full)],
        out_shape=[jax.ShapeDtypeStruct((t, D), F32), jax.ShapeDtypeStruct((8, D), F32)],
        compiler_params=_cparams(("arbitrary",)))(dz, w, x, dx1, g)


def _matmul_tn(a, b, name):
    t, k = a.shape
    n = b.shape[1]
    tk = min(k, 1024)
    tn = 640 if n % 1024 else 1024
    tt = min(TT, t)
    nt = t // tt

    def body(a_ref, b_ref, o_ref):
        @pl.when(pl.program_id(2) == 0)
        def _():
            o_ref[...] = jnp.zeros_like(o_ref)

        o_ref[...] += _dot_tn(a_ref[...], b_ref[...])

    return pl.pallas_call(
        body, name=name, grid=(k // tk, n // tn, nt),
        in_specs=[pl.BlockSpec((tt, tk), lambda i, j, s: (s, i)), pl.BlockSpec((tt, tn), lambda i, j, s: (s, j))],
        out_specs=pl.BlockSpec((tk, tn), lambda i, j, s: (i, j)),
        out_shape=jax.ShapeDtypeStruct((k, n), F32),
        compiler_params=_cparams(("arbitrary", "arbitrary", "arbitrary")))(a, b)


def _lane_head(shape, dim, shift):
    return _iota(shape, dim) >> shift


def _gla_recompute(q_raw, k, lr, wpad, bias, rev, tb):
    pre = _dot(lr, wpad) + bias
    la = (jnp.minimum(pre, 0.0) - jnp.log(1.0 + jnp.exp(-jnp.abs(pre)))) * (1.0 / 16.0)
    r, c = _iota((tb, tb), 0), _iota((tb, tb), 1)
    tri = (c >= r) if rev else (c <= r)
    cum = jnp.where(((r >> 6) == (c >> 6)) & tri, 1.0, 0.0).astype(BF16)
    b = _dot_exact_lhs(cum, la, 3)
    e, ei = jnp.exp(b), jnp.exp(-b)
    qt = (q_raw * 0.125) * e
    kt = k * ei
    return pre, b, e, ei, qt, kt


def _stack_heads(x, shift):
    head = _lane_head(x.shape, 1, shift)
    return jnp.concatenate([jnp.where(head == h, x, 0.0) for h in range(NH)], axis=0).astype(BF16)


def _fold_heads(x, shift):
    head = _lane_head((CH, x.shape[1]), 1, shift)
    return functools.reduce(lambda a, b: a + b,
                            [jnp.where(head == h, x[h * CH:(h + 1) * CH], 0.0) for h in range(NH)])


def _wide_mask(rev):
    r, s = _iota((CH, NH * CH), 0), _iota((CH, NH * CH), 1) & (CH - 1)
    return (s >= r) if rev else (s <= r)


def _state_mask():
    return (_iota((GV, GK), 0) >> 7) == (_iota((GV, GK), 1) >> 6)


def _state_expand(sd):
    head = _lane_head(sd.shape, 1, 6)
    return jnp.concatenate([jnp.where(head == h, sd, 0.0) for h in range(NH)], axis=0)


def _conv_parts(cb, cc, cu, ccp, cup, ccn, cun, cw_ref, first, last, tb):
    h = cc * cu
    hp = jnp.where(first, 0.0, ccp * cup)
    hn = jnp.where(last, 0.0, ccn * cun)
    rows = _iota(h.shape, 0)
    h_m1 = jnp.where(rows == 0, hp, pltpu.roll(h, 1, 0))
    h_p1 = jnp.where(rows == tb - 1, hn, pltpu.roll(h, tb - 1, 0))
    conv = cw_ref[pl.ds(0, 1), :] * h_m1 + cw_ref[pl.ds(1, 1), :] * h + cw_ref[pl.ds(2, 1), :] * h_p1
    return h, h_m1, h_p1, conv


def _group_ones():
    return jnp.where((_iota((CW, CW), 0) >> 6) == (_iota((CW, CW), 1) >> 6), 1.0, 0.0).astype(BF16)


def _head_norm(o):
    ons, rs = [], []
    for h in range(NH):
        slab = o[:, h * 128:(h + 1) * 128]
        r = lax.rsqrt(jnp.mean(slab * slab, axis=-1, keepdims=True) + EPS)
        ons.append(slab * r)
        rs.append(jnp.broadcast_to(r, slab.shape))
    return jnp.concatenate(ons, axis=1), jnp.concatenate(rs, axis=1)


def _zspec(tb, width, blk, jmap):
    return pl.BlockSpec((tb, width), lambda i: (jmap(i), blk))


def _halo_specs(tb, nblk, t, blk, jmap):
    prev = pl.BlockSpec((8, CW), lambda i: (jnp.maximum(jmap(i) * (tb // 8) - 1, 0), blk))
    nxt = pl.BlockSpec((8, CW), lambda i: (jnp.minimum((jmap(i) + 1) * (tb // 8), t // 8 - 1), blk))
    return prev, nxt


def _gla_fwd_sweep(z, wpad, bias, rev, finish_args=None):
    t = z.shape[0]
    tb = min(TB, t)
    nblk, nb = t // tb, tb // CH
    jmap = (lambda i: nblk - 1 - i) if rev else (lambda i: i)
    finish = finish_args is not None

    def body(*refs):
        if finish:
            (q_ref, k_ref, v_ref, lr_ref, w_ref, bias_ref, of_ref, g_ref, cb_ref, cc_ref, cu_ref, ccp_ref, ccn_ref,
             cup_ref, cun_ref, cw_ref, cn_ref, gn_ref, y_ref, opre_ref, sd_ref, st, b_scr, o_scr) = refs
        else:
            q_ref, k_ref, v_ref, lr_ref, w_ref, bias_ref, o_ref, sd_ref, st, b_scr = refs
            o_scr = o_ref
        i = pl.program_id(0)

        @pl.when(i == 0)
        def _():
            st[...] = jnp.zeros_like(st)

        q_raw, k, v = q_ref[...], k_ref[...], v_ref[...]
        _, b, _, _, qt, kt = _gla_recompute(q_raw, k, lr_ref[...], w_ref[...], bias_ref[...], rev, tb)
        b_scr[...] = b
        maskw, bd = _wide_mask(rev), _state_mask()
        for c in (reversed(range(nb)) if rev else range(nb)):
            sl = slice(c * CH, (c + 1) * CH)
            gdec = jnp.exp(b_scr[pl.ds(c * CH + (0 if rev else CH - 1), 1), :])
            qt_c, kt_c, v_c = qt[sl], kt[sl], v[sl]
            s_in = st[...]
            sd_ref[c] = s_in[0:128] + s_in[128:256] + s_in[256:384] + s_in[384:512]
            a = jnp.where(maskw, _dot_nt(qt_c, _stack_heads(kt_c, 6)), 0.0)
            o_scr[pl.ds(c * CH, CH), :] = _dot(a, _stack_heads(v_c, 7)) + _dot_nt(qt_c, s_in)
            st[...] = s_in * gdec + jnp.where(bd, _dot_tn(v_c, kt_c * gdec), 0.0)

        if finish:
            j = jmap(i)
            hsel = jnp.where((_iota((GK, GV), 0) >> 6) == (_iota((GK, GV), 1) >> 7), 1.0, 0.0).astype(BF16)
            sb = _dot_exact_rhs((q_raw * 0.125) * k, hsel, 2)
            o_pre = of_ref[...] + o_scr[...] - sb * v
            opre_ref[...] = o_pre
            on, _ = _head_norm(o_pre)
            g = g_ref[...]
            y_ref[:, CW:] = (on * gn_ref[...] * (g * _sigmoid(g))).astype(BF16)
            cb = cb_ref[...]
            _, _, _, conv = _conv_parts(cb, cc_ref[...], cu_ref[...], ccp_ref[pl.ds(7, 1), :], cup_ref[pl.ds(7, 1), :],
                                        ccn_ref[pl.ds(0, 1), :], cun_ref[pl.ds(0, 1), :], cw_ref, j == 0,
                                        j == nblk - 1, tb)
            yc = cb * conv
            gm = _dot_exact_rhs(yc * yc, _group_ones(), 2) * (1.0 / 64.0)
            y_ref[:, :CW] = (yc * lax.rsqrt(gm + EPS) * cn_ref[...]).astype(BF16)

    full = lambda i: (0, 0)
    in_specs = [_zspec(tb, GK, ZB_Q, jmap), _zspec(tb, GK, ZB_K, jmap), _zspec(tb, GV, ZB_V, jmap),
                _zspec(tb, 128, ZB_LR, jmap), pl.BlockSpec((128, GK), full), pl.BlockSpec((1, GK), full)]
    args = [z, z, z, z, wpad, bias]
    sd_spec = pl.BlockSpec((nb, 128, GK), lambda i: (jmap(i), 0, 0))
    sd_shape = jax.ShapeDtypeStruct((t // CH, 128, GK), F32)
    scratch = [pltpu.VMEM((GV, GK), F32), pltpu.VMEM((tb, GK), F32)]
    if finish:
        o_f, conv_w, conv_norm, gla_norm4 = finish_args
        ccp, ccn = _halo_specs(tb, nblk, t, ZB_CC, jmap)
        cup, cun = _halo_specs(tb, nblk, t, ZB_CU, jmap)
        in_specs += [pl.BlockSpec((tb, GV), lambda i: (jmap(i), 0)), _zspec(tb, GV, ZB_G, jmap),
                     _zspec(tb, CW, ZB_CB, jmap), _zspec(tb, CW, ZB_CC, jmap), _zspec(tb, CW, ZB_CU, jmap),
                     ccp, ccn, cup, cun, pl.BlockSpec((3, CW), full), pl.BlockSpec((1, CW), full),
                     pl.BlockSpec((1, GV), full)]
        args += [o_f, z, z, z, z, z, z, z, z, conv_w, conv_norm, gla_norm4]
        out_specs = [pl.BlockSpec((tb, D), lambda i: (jmap(i), 0)), pl.BlockSpec((tb, GV), lambda i: (jmap(i), 0)),
                     sd_spec]
        out_shape = [jax.ShapeDtypeStruct((t, D), BF16), jax.ShapeDtypeStruct((t, GV), F32), sd_shape]
        scratch.append(pltpu.VMEM((tb, GV), F32))
    else:
        out_specs = [pl.BlockSpec((tb, GV), lambda i: (jmap(i), 0)), sd_spec]
        out_shape = [jax.ShapeDtypeStruct((t, GV), F32), sd_shape]
    return pl.pallas_call(
        body, name="gla_fwd_rev" if rev else "gla_fwd", grid=(nblk,), in_specs=in_specs, out_specs=out_specs,
        out_shape=out_shape, scratch_shapes=scratch, compiler_params=_cparams(("arbitrary",)))(*args)


def _gla_bwd_chunks(do_ref, sd_ref, dst, b_scr, db_scr, dq_ref, dk_ref, dv_ref, qt, kt, e, ei, v, rev, nb):
    maskw, bd = _wide_mask(rev), _state_mask()
    for c in (range(nb) if rev else reversed(range(nb))):
        sl = slice(c * CH, (c + 1) * CH)
        grow = c * CH + (0 if rev else CH - 1)
        gdec = jnp.exp(b_scr[pl.ds(grow, 1), :])
        qt_c, kt_c, v_c, do_c = qt[sl], kt[sl], v[sl], do_ref[pl.ds(c * CH, CH), :]
        s_in = _state_expand(sd_ref[c])
        ds_out = dst[...]
        kbd, vbd = _stack_heads(kt_c, 6), _stack_heads(v_c, 7)
        a = jnp.where(maskw, _dot_nt(qt_c, kbd), 0.0)
        da = jnp.where(maskw, _dot_nt(do_c, vbd), 0.0)
        kh = kt_c * gdec
        dv_ref[pl.ds(c * CH, CH), :] = _fold_heads(_dot_tn(a, do_c), 7) + _dot_nt(kh, ds_out)
        dqt = _dot(da, kbd) + _dot(do_c, s_in)
        dkh = _dot(v_c, ds_out)
        dkt = _fold_heads(_dot_tn(da, qt_c), 6) + dkh * gdec
        dg = jnp.sum(ds_out * s_in, axis=0, keepdims=True) + jnp.sum(kt_c * dkh, axis=0, keepdims=True)
        db_scr[pl.ds(c * CH, CH), :] = dqt * qt_c - dkt * kt_c
        db_scr[pl.ds(grow, 1), :] += dg * gdec
        dq_ref[pl.ds(c * CH, CH), :] = dqt * e[sl] * 0.125
        dk_ref[pl.ds(c * CH, CH), :] = dkt * ei[sl]
        dst[...] = ds_out * gdec + jnp.where(bd, _dot_tn(do_c, qt_c), 0.0)


def _gate_bwd(db, pre, lr, wpad, rev, tb):
    r, c = _iota((tb, tb), 0), _iota((tb, tb), 1)
    tri = (c <= r) if rev else (c >= r)
    cum_t = jnp.where(((r >> 6) == (c >> 6)) & tri, 1.0, 0.0).astype(BF16)
    dla = _dot_exact_lhs(cum_t, db, 2)
    dpre = dla * (1.0 / 16.0) / (1.0 + jnp.exp(pre))
    return dpre, _dot_nt(dpre, wpad), _dot_tn(lr, dpre)


def _gla_bwd_first(z, dy, o_pre, sd, wpad, bias, conv_w, conv_norm, gla_norm4):
    t = z.shape[0]
    tb = min(TB, t)
    nblk, nb = t // tb, tb // CH
    jmap = lambda i: nblk - 1 - i

    def body(q_ref, k_ref, v_ref, lr_ref, g_ref, cb_ref, cc_ref, cu_ref, ccp_ref, ccn_ref, cup_ref, cun_ref,
             dy_ref, opre_ref, sd_ref, w_ref, bias_ref, cw_ref, cn_ref, gn_ref,
             do_ref, dq_ref, dk_ref, dv_ref, dlr_ref, dzg_ref, dzcb_ref, dconv_ref,
             dw_ref, dbias_ref, dcw_ref, dcn_ref, dgn_ref, dst, b_scr, db_scr):
        i = pl.program_id(0)
        j = jmap(i)

        @pl.when(i == 0)
        def _():
            dst[...] = jnp.zeros_like(dst)
            for ref in (dw_ref, dbias_ref, dcw_ref, dcn_ref, dgn_ref):
                ref[...] = jnp.zeros_like(ref)

        dyg = dy_ref[:, CW:]
        g = g_ref[...]
        sig = _sigmoid(g)
        on, rr = _head_norm(opre_ref[...])
        gn = gn_ref[...]
        dzg_ref[...] = (dyg * on * gn * (sig * (1.0 + g * (1.0 - sig)))).astype(BF16)
        don = dyg * (g * sig)
        _acc_rows(dgn_ref, jnp.sum(don * on, axis=0, keepdims=True))
        u = don * gn
        uo = u * on
        mean_uo = jnp.concatenate(
            [jnp.broadcast_to(jnp.mean(uo[:, h * 128:(h + 1) * 128], axis=-1, keepdims=True), (tb, 128))
             for h in range(NH)], axis=1)
        do_ref[...] = rr * (u - on * mean_uo)

        cb = cb_ref[...]
        h, h_m1, h_p1, conv = _conv_parts(cb, cc_ref[...], cu_ref[...], ccp_ref[pl.ds(7, 1), :],
                                          cup_ref[pl.ds(7, 1), :], ccn_ref[pl.ds(0, 1), :], cun_ref[pl.ds(0, 1), :],
                                          cw_ref, j == 0, j == nblk - 1, tb)
        yc = cb * conv
        ones = _group_ones()
        rc = lax.rsqrt(_dot_exact_rhs(yc * yc, ones, 2) * (1.0 / 64.0) + EPS)
        ycr = yc * rc
        dyn = dy_ref[:, :CW]
        _acc_rows(dcn_ref, jnp.sum(dyn * ycr, axis=0, keepdims=True))
        uc = dyn * cn_ref[...]
        dyc = rc * (uc - ycr * (_dot_exact_rhs(uc * ycr, ones, 2) * (1.0 / 64.0)))
        dzcb_ref[...] = (dyc * conv).astype(BF16)
        dconv = dyc * cb
        dconv_ref[...] = dconv
        dcw_ref[pl.ds(0, 1), :] += jnp.sum(dconv * h_m1, axis=0, keepdims=True)
        dcw_ref[pl.ds(1, 1), :] += jnp.sum(dconv * h, axis=0, keepdims=True)
        dcw_ref[pl.ds(2, 1), :] += jnp.sum(dconv * h_p1, axis=0, keepdims=True)

        lr, wp = lr_ref[...], w_ref[...]
        pre, b, e, ei, qt, kt = _gla_recompute(q_ref[...], k_ref[...], lr, wp, bias_ref[...], False, tb)
        b_scr[...] = b
        _gla_bwd_chunks(do_ref, sd_ref, dst, b_scr, db_scr, dq_ref, dk_ref, dv_ref, qt, kt, e, ei, v_ref[...],
                        False, nb)
        dpre, dlr, dw = _gate_bwd(db_scr[...], pre, lr, wp, False, tb)
        dlr_ref[...] = dlr
        dw_ref[...] += dw
        _acc_rows(dbias_ref, jnp.sum(dpre, axis=0, keepdims=True))

    full = lambda i: (0, 0)
    tokv = pl.BlockSpec((tb, GV), lambda i: (jmap(i), 0))
    tokk = pl.BlockSpec((tb, GK), lambda i: (jmap(i), 0))
    ccp, ccn = _halo_specs(tb, nblk, t, ZB_CC, jmap)
    cup, cun = _halo_specs(tb, nblk, t, ZB_CU, jmap)
    in_specs = [_zspec(tb, GK, ZB_Q, jmap), _zspec(tb, GK, ZB_K, jmap), _zspec(tb, GV, ZB_V, jmap),
                _zspec(tb, 128, ZB_LR, jmap), _zspec(tb, GV, ZB_G, jmap), _zspec(tb, CW, ZB_CB, jmap),
                _zspec(tb, CW, ZB_CC, jmap), _zspec(tb, CW, ZB_CU, jmap), ccp, ccn, cup, cun,
                pl.BlockSpec((tb, D), lambda i: (jmap(i), 0)), tokv,
                pl.BlockSpec((nb, 128, GK), lambda i: (jmap(i), 0, 0)), pl.BlockSpec((128, GK), full),
                pl.BlockSpec((1, GK), full), pl.BlockSpec((3, CW), full), pl.BlockSpec((1, CW), full),
                pl.BlockSpec((1, GV), full)]
    out_specs = [tokv, tokk, tokk, tokv, pl.BlockSpec((tb, 128), lambda i: (jmap(i), 0)), tokv, tokv, tokv,
                 pl.BlockSpec((128, GK), full), pl.BlockSpec((8, GK), full), pl.BlockSpec((8, CW), full),
                 pl.BlockSpec((8, CW), full), pl.BlockSpec((8, GV), full)]
    out_shape = [jax.ShapeDtypeStruct((t, GV), F32), jax.ShapeDtypeStruct((t, GK), F32),
                 jax.ShapeDtypeStruct((t, GK), F32), jax.ShapeDtypeStruct((t, GV), F32),
                 jax.ShapeDtypeStruct((t, 128), F32), jax.ShapeDtypeStruct((t, GV), BF16),
                 jax.ShapeDtypeStruct((t, CW), BF16), jax.ShapeDtypeStruct((t, CW), F32),
                 jax.ShapeDtypeStruct((128, GK), F32), jax.ShapeDtypeStruct((8, GK), F32),
                 jax.ShapeDtypeStruct((8, CW), F32), jax.ShapeDtypeStruct((8, CW), F32),
                 jax.ShapeDtypeStruct((8, GV), F32)]
    return pl.pallas_call(
        body, name="gla_bwd_first", grid=(nblk,), in_specs=in_specs, out_specs=out_specs, out_shape=out_shape,
        scratch_shapes=[pltpu.VMEM((GV, GK), F32), pltpu.VMEM((tb, GK), F32), pltpu.VMEM((tb, GK), F32)],
        compiler_params=_cparams(("arbitrary",)))(
            z, z, z, z, z, z, z, z, z, z, z, z, dy, o_pre, sd, wpad, bias, conv_w, conv_norm, gla_norm4)


def _gla_bwd_second(z, do, sd, wpad, bias, dqa, dka, dva, dlra, dzg, dzcb, dconv, conv_w):
    t = z.shape[0]
    tb = min(TB, t)
    nblk, nb = t // tb, tb // CH
    jmap = lambda i: i

    def body(q_ref, k_ref, v_ref, lr_ref, cc_ref, cu_ref, do_ref, sd_ref, w_ref, bias_ref, dqa_ref, dka_ref,
             dva_ref, dlra_ref, dzg_ref, dzcb_ref, dc_ref, dcp_ref, dcn_ref, cw_ref,
             dz_ref, dw_ref, dbias_ref, dst, b_scr, db_scr, dq_scr, dk_scr, dv_scr):
        i = pl.program_id(0)

        @pl.when(i == 0)
        def _():
            dst[...] = jnp.zeros_like(dst)
            dw_ref[...] = jnp.zeros_like(dw_ref)
            dbias_ref[...] = jnp.zeros_like(dbias_ref)

        q_raw, k, v, lr, wp = q_ref[...], k_ref[...], v_ref[...], lr_ref[...], w_ref[...]
        pre, b, e, ei, qt, kt = _gla_recompute(q_raw, k, lr, wp, bias_ref[...], True, tb)
        b_scr[...] = b
        _gla_bwd_chunks(do_ref, sd_ref, dst, b_scr, db_scr, dq_scr, dk_scr, dv_scr, qt, kt, e, ei, v, True, nb)
        dpre, dlr, dw = _gate_bwd(db_scr[...], pre, lr, wp, True, tb)
        dw_ref[...] += dw
        _acc_rows(dbias_ref, jnp.sum(dpre, axis=0, keepdims=True))

        do = do_ref[...]
        qs = q_raw * 0.125
        hsel = jnp.where((_iota((GK, GV), 0) >> 6) == (_iota((GK, GV), 1) >> 7), 1.0, 0.0).astype(BF16)
        hsel_t = jnp.where((_iota((GV, GK), 0) >> 7) == (_iota((GV, GK), 1) >> 6), 1.0, 0.0).astype(BF16)
        sb = _dot_exact_rhs(qs * k, hsel, 2)
        dsk = _dot_exact_rhs(do * v, hsel_t, 2)
        dz_ref[:, 1536:1792] = (dqa_ref[...] + dq_scr[...] - dsk * k * 0.125).astype(BF16)
        dz_ref[:, 1792:2048] = (dka_ref[...] + dk_scr[...] - dsk * qs).astype(BF16)
        dz_ref[:, 2048:2560] = (dva_ref[...] + dv_scr[...] - sb * do).astype(BF16)
        dz_ref[:, 2560:3072] = dzg_ref[...]
        dz_ref[:, 3072:3200] = (dlra_ref[...] + dlr).astype(BF16)

        dc = dc_ref[...]
        rows = _iota(dc.shape, 0)
        dprev = jnp.where(i == 0, 0.0, dcp_ref[pl.ds(7, 1), :])
        dnext = jnp.where(i == nblk - 1, 0.0, dcn_ref[pl.ds(0, 1), :])
        dc_m1 = jnp.where(rows == 0, dprev, pltpu.roll(dc, 1, 0))
        dc_p1 = jnp.where(rows == tb - 1, dnext, pltpu.roll(dc, tb - 1, 0))
        dh = cw_ref[pl.ds(0, 1), :] * dc_p1 + cw_ref[pl.ds(1, 1), :] * dc + cw_ref[pl.ds(2, 1), :] * dc_m1
        dz_ref[:, 0:512] = dzcb_ref[...]
        dz_ref[:, 512:1024] = (dh * cu_ref[...]).astype(BF16)
        dz_ref[:, 1024:1536] = (dh * cc_ref[...]).astype(BF16)

    full = lambda i: (0, 0)
    tokv = pl.BlockSpec((tb, GV), lambda i: (i, 0))
    tokk = pl.BlockSpec((tb, GK), lambda i: (i, 0))
    dcp = pl.BlockSpec((8, CW), lambda i: (jnp.maximum(i * (tb // 8) - 1, 0), 0))
    dcn = pl.BlockSpec((8, CW), lambda i: (jnp.minimum((i + 1) * (tb // 8), t // 8 - 1), 0))
    in_specs = [_zspec(tb, GK, ZB_Q, jmap), _zspec(tb, GK, ZB_K, jmap), _zspec(tb, GV, ZB_V, jmap),
                _zspec(tb, 128, ZB_LR, jmap), _zspec(tb, CW, ZB_CC, jmap), _zspec(tb, CW, ZB_CU, jmap), tokv,
                pl.BlockSpec((nb, 128, GK), lambda i: (i, 0, 0)), pl.BlockSpec((128, GK), full),
                pl.BlockSpec((1, GK), full), tokk, tokk, tokv, pl.BlockSpec((tb, 128), lambda i: (i, 0)), tokv, tokv,
                tokv, dcp, dcn, pl.BlockSpec((3, CW), full)]
    out_specs = [pl.BlockSpec((tb, ZC), lambda i: (i, 0)), pl.BlockSpec((128, GK), full), pl.BlockSpec((8, GK), full)]
    out_shape = [jax.ShapeDtypeStruct((t, ZC), BF16), jax.ShapeDtypeStruct((128, GK), F32),
                 jax.ShapeDtypeStruct((8, GK), F32)]
    return pl.pallas_call(
        body, name="gla_bwd_second", grid=(nblk,), in_specs=in_specs, out_specs=out_specs, out_shape=out_shape,
        scratch_shapes=[pltpu.VMEM((GV, GK), F32), pltpu.VMEM((tb, GK), F32), pltpu.VMEM((tb, GK), F32),
                        pltpu.VMEM((tb, GK), F32), pltpu.VMEM((tb, GK), F32), pltpu.VMEM((tb, GV), F32)],
        compiler_params=_cparams(("arbitrary",)))(
            z, z, z, z, z, z, do, sd, wpad, bias, dqa, dka, dva, dlra, dzg, dzcb, dconv, dconv, dconv, conv_w)


def _local_step(x, mem, target, p):
    zeros_lr = jnp.zeros((128 - LR, GK), BF16)
    waf_pad = jnp.concatenate([p["w_af"].astype(BF16), zeros_lr], axis=0)
    wab_pad = jnp.concatenate([jnp.zeros((LR, GK), BF16), p["w_ab"].astype(BF16), zeros_lr[:128 - 2 * LR]], axis=0)
    gla_norm4 = jnp.tile(p["gla_norm"], (1, NH))

    z, hb = _inproj(x, p["mix_norm"], p["w_in"])
    o_f, sd_f = _gla_fwd_sweep(z, waf_pad, p["b_af"], False)
    yb, o_pre, sd_b = _gla_fwd_sweep(z, wab_pad, p["b_ab"], True,
                                     (o_f, p["conv_w"], p["conv_norm"], gla_norm4))
    kv, memn = _kv_proj(mem, p["mem_norm"], p["w_xkv"])
    kb, vb = kv[:, :D].astype(BF16), kv[:, D:].astype(BF16)
    x1, x2, xn1, qb, attb = _attn_fwd(x, yb, p["w_out"], p["xa_norm"], p["w_xq"], kb, vb, p["w_xo"])
    h1, xn2, dx3, dx3b, loss8, dfinal = _mlp_fwd(x2, p["mlp_norm"], p["w_up"], p["w_down"], p["final_norm"], target)

    ab, dh1b, dx2, dx2b, dmlp = _mlp_bwd(dx3, dx3b, h1, p["w_down"], p["w_up"], x2, p["mlp_norm"])
    dx1, dx1b, dy, dqb, dkv, dxa = _attn_bwd(x1, dx2, dx2b, qb, kb, vb, p["w_xo"], p["w_xq"], p["w_out"],
                                             p["xa_norm"])
    dw_xkv, dmemn = _kv_bwd(dkv, memn, mem, p["mem_norm"], p["w_xkv"])
    (do, dqa, dka, dva, dlra, dzg, dzcb, dconv, dwaf, dbaf, dcw, dcn, dgn) = _gla_bwd_first(
        z, dy, o_pre, sd_f, waf_pad, p["b_af"], p["conv_w"], p["conv_norm"], gla_norm4)
    dz, dwab, dbab = _gla_bwd_second(z, do, sd_b, wab_pad, p["b_ab"], dqa, dka, dva, dlra, dzg, dzcb, dconv,
                                     p["conv_w"])
    grad_x, dmix = _inproj_bwd(dz, p["w_in"], x, dx1, p["mix_norm"])

    grads = {
        "w_in": _matmul_tn(hb, dz, "dw_in")[:, :ZW],
        "w_out": _matmul_tn(yb, dx1b, "dw_out"),
        "w_xq": _matmul_tn(xn1, dqb, "dw_xq"),
        "w_xo": _matmul_tn(attb, dx2b, "dw_xo"),
        "w_xkv": dw_xkv,
        "w_up": _matmul_tn(xn2, dh1b, "dw_up"),
        "w_down": _matmul_tn(ab, dx3b, "dw_down"),
        "mix_norm": dmix[0:1], "conv_w": dcw[0:3], "conv_norm": dcn[0:1],
        "w_af": dwaf[0:LR], "b_af": dbaf[0:1], "w_ab": dwab[LR:2 * LR], "b_ab": dbab[0:1],
        "gla_norm": (dgn[0:1, 0:128] + dgn[0:1, 128:256]) + (dgn[0:1, 256:384] + dgn[0:1, 384:512]),
        "xa_norm": dxa[0:1], "mem_norm": dmemn[0:1], "mlp_norm": dmlp[0:1], "final_norm": dfinal[0:1],
    }
    return loss8[0:1, 0:1], grad_x, grads


def _place():
    return lax.axis_index("x"), lax.axis_index("y"), lax.axis_index("c")


def _all_gather(blk):
    def body(x_ref, out_ref, send_sems, recv_sems, local_sem):
        x, y, c = _place()
        me, sibling = (x, y, c), (x, y, 1 - c)
        chips = [(1 - x, y), (x, 1 - y), (1 - x, 1 - y)]

        def rows(px, py, pc):
            return out_ref.at[4 * px + 2 * py + pc]

        def copy(k, block, to, src=None):
            return pltpu.make_async_remote_copy(
                src_ref=rows(*block) if src is None else src, dst_ref=rows(*block),
                send_sem=send_sems.at[k], recv_sem=recv_sems.at[k], device_id=to, device_id_type=MESH)

        mine = pltpu.make_async_copy(x_ref, rows(*me), local_sem)
        mine.start()
        first = [copy(0, me, sibling, src=x_ref)]
        first += [copy(1 + j, me, (*chip, c), src=x_ref) for j, chip in enumerate(chips)]
        for cp in first:
            cp.start()
        passed = [copy(4 + j, (*chip, c), sibling) for j, chip in enumerate(chips)]
        for j, chip in enumerate(chips):
            copy(1 + j, (*chip, c), me).wait_recv()
            passed[j].start()
        copy(0, sibling, me).wait_recv()
        for j, chip in enumerate(chips):
            copy(4 + j, (*chip, 1 - c), me).wait_recv()
        for cp in first + passed:
            cp.wait_send()
        mine.wait()

    return pl.pallas_call(
        body, name="weights_all_gather", out_shape=jax.ShapeDtypeStruct((NDEV,) + blk.shape, blk.dtype),
        in_specs=[pl.BlockSpec(memory_space=pltpu.HBM)], out_specs=pl.BlockSpec(memory_space=pltpu.HBM),
        scratch_shapes=[pltpu.SemaphoreType.DMA((7,)), pltpu.SemaphoreType.DMA((7,)), pltpu.SemaphoreType.DMA],
    )(blk)


def _rs_sibling(g4):
    def body(g_ref, out_ref, send_sems, recv_sems):
        x, y, c = _place()
        copies = [pltpu.make_async_remote_copy(
            src_ref=g_ref.at[k, 1 - c], dst_ref=out_ref.at[k], send_sem=send_sems.at[k], recv_sem=recv_sems.at[k],
            device_id=(x, y, 1 - c), device_id_type=MESH) for k in range(4)]
        for cp in copies:
            cp.start()
        for cp in copies:
            cp.wait()

    return pl.pallas_call(
        body, name="grads_to_sibling", out_shape=jax.ShapeDtypeStruct((4,) + g4.shape[2:], g4.dtype),
        in_specs=[pl.BlockSpec(memory_space=pltpu.HBM)], out_specs=pl.BlockSpec(memory_space=pltpu.HBM),
        scratch_shapes=[pltpu.SemaphoreType.DMA((4,)), pltpu.SemaphoreType.DMA((4,))],
    )(g4)


def _rs_chips(pb):
    def body(p_ref, out_ref, send_sems, recv_sems):
        x, y, c = _place()
        peers = [(1 - x, y), (x, 1 - y), (1 - x, 1 - y)]
        copies = [pltpu.make_async_remote_copy(
            src_ref=p_ref.at[2 * px + py], dst_ref=out_ref.at[k], send_sem=send_sems.at[k], recv_sem=recv_sems.at[k],
            device_id=(px, py, c), device_id_type=MESH) for k, (px, py) in enumerate(peers)]
        for cp in copies:
            cp.start()
        for cp in copies:
            cp.wait()

    return pl.pallas_call(
        body, name="grads_to_chips", out_shape=jax.ShapeDtypeStruct((3,) + pb.shape[1:], pb.dtype),
        in_specs=[pl.BlockSpec(memory_space=pltpu.HBM)], out_specs=pl.BlockSpec(memory_space=pltpu.HBM),
        scratch_shapes=[pltpu.SemaphoreType.DMA((3,)), pltpu.SemaphoreType.DMA((3,))],
    )(pb)


RS_ROWS = 688


def _rs_pair_sum(place, g4, r1):
    nr = g4.shape[2]

    def body(pl_ref, g_ref, r_ref, pb_ref, own_ref):
        s = g_ref[0, 0] + r_ref[0]
        pb_ref[0] = s.astype(BF16)

        @pl.when(pl.program_id(1) == pl_ref[0])
        def _():
            own_ref[...] = s

    grid_spec = pltpu.PrefetchScalarGridSpec(
        num_scalar_prefetch=1, grid=(nr // RS_ROWS, 4),
        in_specs=[pl.BlockSpec((1, 1, RS_ROWS, D), lambda r, k, p: (k, p[1], r, 0)),
                  pl.BlockSpec((1, RS_ROWS, D), lambda r, k, p: (k, r, 0))],
        out_specs=[pl.BlockSpec((1, RS_ROWS, D), lambda r, k, p: (k, r, 0)),
                   pl.BlockSpec((RS_ROWS, D), lambda r, k, p: (r, 0))])
    return pl.pallas_call(
        body, name="grads_pair_sum", grid_spec=grid_spec,
        out_shape=[jax.ShapeDtypeStruct((4, nr, D), BF16), jax.ShapeDtypeStruct((nr, D), F32)],
        compiler_params=_cparams(("arbitrary", "arbitrary")))(place, g4, r1)


def _rs_final_sum(own, r2):
    nr = own.shape[0]

    def body(o_ref, r_ref, g_ref):
        g_ref[...] = ((o_ref[...] + r_ref[0].astype(F32)) + r_ref[1].astype(F32)) + r_ref[2].astype(F32)

    return pl.pallas_call(
        body, name="grads_final_sum", grid=(nr // RS_ROWS,),
        in_specs=[pl.BlockSpec((RS_ROWS, D), lambda r: (r, 0)), pl.BlockSpec((3, RS_ROWS, D), lambda r: (0, r, 0))],
        out_specs=pl.BlockSpec((RS_ROWS, D), lambda r: (r, 0)),
        out_shape=jax.ShapeDtypeStruct((nr, D), F32), compiler_params=_cparams(("arbitrary",)))(own, r2)


def _small_all_reduce(vec):
    m_per = vec.shape[0]

    def body(x_ref, all_ref, sum_ref, send_sems, recv_sems, local_sem):
        x, y, c = _place()
        me, sibling = (x, y, c), (x, y, 1 - c)
        chips = [(1 - x, y), (x, 1 - y), (1 - x, 1 - y)]

        def rows(px, py, pc):
            return all_ref.at[4 * px + 2 * py + pc]

        def copy(k, block, to, src=None):
            return pltpu.make_async_remote_copy(
                src_ref=rows(*block) if src is None else src, dst_ref=rows(*block),
                send_sem=send_sems.at[k], recv_sem=recv_sems.at[k], device_id=to, device_id_type=MESH)

        mine = pltpu.make_async_copy(x_ref, rows(*me), local_sem)
        mine.start()
        first = [copy(0, me, sibling, src=x_ref)]
        first += [copy(1 + j, me, (*chip, c), src=x_ref) for j, chip in enumerate(chips)]
        for cp in first:
            cp.start()
        passed = [copy(4 + j, (*chip, c), sibling) for j, chip in enumerate(chips)]
        for j, chip in enumerate(chips):
            copy(1 + j, (*chip, c), me).wait_recv()
            passed[j].start()
        copy(0, sibling, me).wait_recv()
        for j, chip in enumerate(chips):
            copy(4 + j, (*chip, 1 - c), me).wait_recv()
        for cp in first + passed:
            cp.wait_send()
        mine.wait()
        total = all_ref[0]
        for d in range(1, NDEV):
            total = total + all_ref[d]
        sum_ref[...] = total

    return pl.pallas_call(
        body, name="small_all_reduce",
        out_shape=[jax.ShapeDtypeStruct((NDEV, m_per, 128), F32), jax.ShapeDtypeStruct((m_per, 128), F32)],
        in_specs=[pl.BlockSpec(memory_space=pltpu.VMEM)],
        out_specs=[pl.BlockSpec(memory_space=pltpu.VMEM), pl.BlockSpec(memory_space=pltpu.VMEM)],
        scratch_shapes=[pltpu.SemaphoreType.DMA((7,)), pltpu.SemaphoreType.DMA((7,)), pltpu.SemaphoreType.DMA],
    )(vec)[1]


def _adamw_math(w, g, m, v):
    m = ADAM_B1 * m + (1.0 - ADAM_B1) * g
    v = ADAM_B2 * v + (1.0 - ADAM_B2) * (g * g)
    m_hat = m / (1.0 - ADAM_B1 ** ADAM_STEP)
    v_hat = v / (1.0 - ADAM_B2 ** ADAM_STEP)
    delta = -ADAM_LR * (m_hat / (jnp.sqrt(v_hat) + ADAM_EPS) + ADAM_WD * w)
    return delta, m, v


def _adamw(w, g, m, v, name):
    r, c = w.shape
    tr = min(r, 256)

    def body(w_ref, g_ref, m_ref, v_ref, d_ref, nm_ref, nv_ref):
        d_ref[...], nm_ref[...], nv_ref[...] = _adamw_math(w_ref[...], g_ref[...], m_ref[...], v_ref[...])

    spec = pl.BlockSpec((tr, c), lambda i: (i, 0))
    return pl.pallas_call(
        body, name=name, grid=(r // tr,), in_specs=[spec] * 4, out_specs=[spec] * 3,
        out_shape=[jax.ShapeDtypeStruct((r, c), F32)] * 3, compiler_params=_cparams(("arbitrary",)))(w, g, m, v)


def _adamw_small(ws, gs, ms, vs):
    n = len(ws)

    def body(*refs):
        ins, outs = refs[:4 * n], refs[4 * n:]
        for i in range(n):
            d, m, v = _adamw_math(ins[i][...], ins[n + i][...], ins[2 * n + i][...], ins[3 * n + i][...])
            outs[i][...], outs[n + i][...], outs[2 * n + i][...] = d, m, v

    shapes = [jax.ShapeDtypeStruct(w.shape, F32) for w in ws]
    outs = pl.pallas_call(body, name="adamw_small", out_shape=shapes * 3)(*ws, *gs, *ms, *vs)
    return outs[:n], outs[n:2 * n], outs[2 * n:]


MATS = ("w_in", "w_out", "w_xq", "w_xo", "w_xkv", "w_up", "w_down")
SMALL = ("mix_norm", "conv_w", "conv_norm", "w_af", "b_af", "w_ab", "b_ab", "gla_norm", "xa_norm", "mem_norm",
         "mlp_norm", "final_norm")
WEIGHTS = ("mix_norm", "w_in", "conv_w", "conv_norm", "w_af", "b_af", "w_ab", "b_ab", "gla_norm", "w_out", "xa_norm",
           "mem_norm", "w_xq", "w_xkv", "w_xo", "mlp_norm", "w_up", "w_down", "final_norm")
MAT_ROWS = {"w_in": (OFF_WIN, 388), "w_out": (OFF_WOUT, 128), "w_xq": (OFF_WXQ, 128), "w_xo": (OFF_WXO, 128),
            "w_xkv": (OFF_WXKV, 256), "w_up": (OFF_WUP, 512), "w_down": (OFF_WDOWN, 512)}
COL_SHARDED = {"w_in": 388, "w_xkv": 256, "w_up": 512}
SMALL_SHARDED = {"conv_w": (3, 64), "w_af": (LR, 32), "w_ab": (LR, 32)}


def _pad_rows(a, rows):
    return jnp.pad(a, ((0, rows - a.shape[0]), (0, 0)))


def _cols_to_blocks(full, width):
    r = full.shape[0]
    return full.reshape(r, NDEV, width).transpose(1, 0, 2).reshape(NDEV, r * width // D, D)


def _blocks_to_cols(blocks, rows, width):
    return blocks.reshape(NDEV, rows, width).transpose(1, 0, 2).reshape(rows, NDEV * width)


def kernel(x, mem, mix_norm, w_in, conv_w, conv_norm, w_af, b_af, w_ab, b_ab, gla_norm, w_out, xa_norm, mem_norm, w_xq, w_xkv, w_xo, mlp_norm, w_up, w_down, final_norm, loss_target, m_mix_norm, m_w_in, m_conv_w, m_conv_norm, m_w_af, m_b_af, m_w_ab, m_b_ab, m_gla_norm, m_w_out, m_xa_norm, m_mem_norm, m_w_xq, m_w_xkv, m_w_xo, m_mlp_norm, m_w_up, m_w_down, m_final_norm, v_mix_norm, v_w_in, v_conv_w, v_conv_norm, v_w_af, v_b_af, v_w_ab, v_b_ab, v_gla_norm, v_w_out, v_xa_norm, v_mem_norm, v_w_xq, v_w_xkv, v_w_xo, v_mlp_norm, v_w_up, v_w_down, v_final_norm):
    w = dict(mix_norm=mix_norm, w_in=w_in, conv_w=conv_w, conv_norm=conv_norm, w_af=w_af, b_af=b_af, w_ab=w_ab,
             b_ab=b_ab, gla_norm=gla_norm, w_out=w_out, xa_norm=xa_norm, mem_norm=mem_norm, w_xq=w_xq, w_xkv=w_xkv,
             w_xo=w_xo, mlp_norm=mlp_norm, w_up=w_up, w_down=w_down, final_norm=final_norm)
    mom = dict(mix_norm=m_mix_norm, w_in=m_w_in, conv_w=m_conv_w, conv_norm=m_conv_norm, w_af=m_w_af, b_af=m_b_af,
               w_ab=m_w_ab, b_ab=m_b_ab, gla_norm=m_gla_norm, w_out=m_w_out, xa_norm=m_xa_norm, mem_norm=m_mem_norm,
               w_xq=m_w_xq, w_xkv=m_w_xkv, w_xo=m_w_xo, mlp_norm=m_mlp_norm, w_up=m_w_up, w_down=m_w_down,
               final_norm=m_final_norm)
    var = dict(mix_norm=v_mix_norm, w_in=v_w_in, conv_w=v_conv_w, conv_norm=v_conv_norm, w_af=v_w_af, b_af=v_b_af,
               w_ab=v_w_ab, b_ab=v_b_ab, gla_norm=v_gla_norm, w_out=v_w_out, xa_norm=v_xa_norm, mem_norm=v_mem_norm,
               w_xq=v_w_xq, w_xkv=v_w_xkv, w_xo=v_w_xo, mlp_norm=v_mlp_norm, w_up=v_w_up, w_down=v_w_down,
               final_norm=v_final_norm)
    xi, yi, ci = _place()
    me = 4 * xi + 2 * yi + ci
    two_d = lambda a: a.reshape(a.shape[-2:]) if a.ndim == 3 else a.reshape(1, a.shape[-1])

    small = jnp.concatenate([w[n].reshape(-1) for n in SMALL_SHARDED])
    small = lax.bitcast_convert_type(small, BF16).reshape(-1)
    parts = [_pad_rows(two_d(w[n]).astype(BF16).reshape(-1, D), MAT_ROWS[n][1] + (12 if n == "w_in" else 0))
             for n in MATS]
    parts.append(jnp.pad(small, (0, ROWS_SMALL * D - small.shape[0])).reshape(ROWS_SMALL, D))
    gathered = _all_gather(jnp.concatenate(parts, axis=0))

    p = {}
    for n in MATS:
        off, rows = MAT_ROWS[n]
        blocks = gathered[:, off:off + rows]
        p[n] = _blocks_to_cols(blocks, D, COL_SHARDED[n]) if n in COL_SHARDED else blocks.reshape(NDEV * rows, D)
    p["w_in"] = jnp.pad(p["w_in"], ((0, 0), (0, ZC - ZW)))
    n_small = sum(a * b for a, b in SMALL_SHARDED.values())
    small_all = gathered[:, OFF_SMALL:].reshape(NDEV, -1)[:, :2 * n_small].reshape(NDEV, n_small, 2)
    small_all = lax.bitcast_convert_type(small_all, F32)
    off = 0
    for n, (r, c) in SMALL_SHARDED.items():
        p[n] = small_all[:, off:off + r * c].reshape(NDEV, r, c).transpose(1, 0, 2).reshape(r, NDEV * c)
        off += r * c
    for n in SMALL:
        if n not in SMALL_SHARDED:
            p[n] = two_d(w[n])

    loss_part, grad_x, grads = _local_step(x[0], mem[0], loss_target[0], p)

    gparts = []
    for n in MATS:
        off, rows = MAT_ROWS[n]
        blocks = _cols_to_blocks(grads[n], COL_SHARDED[n]) if n in COL_SHARDED else grads[n].reshape(NDEV, rows, D)
        if n == "w_in":
            blocks = jnp.pad(blocks, ((0, 0), (0, ROWS_WIN - rows), (0, 0)))
        gparts.append(blocks)
    g4 = jnp.concatenate(gparts, axis=1).reshape(4, 2, NR_G, D)
    from_sibling = _rs_sibling(g4)
    place = jnp.stack([2 * xi + yi, ci]).astype(jnp.int32)
    pair_bf, own = _rs_pair_sum(place, g4, from_sibling)
    g_shard = _rs_final_sum(own, _rs_chips(pair_bf))

    order = [n for n in SMALL if n not in SMALL_SHARDED] + list(SMALL_SHARDED)
    flat = jnp.concatenate([grads[n].reshape(-1) for n in order] + [loss_part.reshape(-1)])
    n_flat = flat.shape[0]
    tot = _small_all_reduce(jnp.pad(flat, (0, SMALL_ROWS * 128 - n_flat)).reshape(SMALL_ROWS, 128)).reshape(-1)
    gsmall, off = {}, 0
    for n in order:
        size = grads[n].size
        full = tot[off:off + size].reshape(grads[n].shape)
        off += size
        if n in SMALL_SHARDED:
            r, c = SMALL_SHARDED[n]
            full = lax.dynamic_slice_in_dim(full, me * c, c, axis=1)
        gsmall[n] = full
    loss = tot[off]

    out_g, out_d, out_m, out_v = {}, {}, {}, {}
    for n in MATS:
        off, rows = MAT_ROWS[n]
        shape2 = two_d(w[n]).shape
        g = g_shard[off:off + rows].reshape(shape2)
        d, nm, nv = _adamw(two_d(w[n]), g, two_d(mom[n]), two_d(var[n]), "adamw_" + n)
        out_g[n], out_d[n], out_m[n], out_v[n] = [a.reshape(w[n].shape) for a in (g, d, nm, nv)]
    ds, nms, nvs = _adamw_small([two_d(w[n]) for n in SMALL], [gsmall[n] for n in SMALL],
                                [two_d(mom[n]) for n in SMALL], [two_d(var[n]) for n in SMALL])
    for i, n in enumerate(SMALL):
        out_g[n], out_d[n], out_m[n], out_v[n] = [a.reshape(w[n].shape) for a in (gsmall[n], ds[i], nms[i], nvs[i])]

    return (loss, grad_x[None], *[out_g[n] for n in WEIGHTS], *[out_d[n] for n in WEIGHTS],
            *[out_m[n] for n in WEIGHTS], *[out_v[n] for n in WEIGHTS])
```

```python
import functools

import jax
import jax.numpy as jnp
from jax import lax
from jax.experimental import pallas as pl
from jax.experimental.pallas import tpu as pltpu

F32 = jnp.float32
BF16 = jnp.bfloat16

D = 1024
CW = 512
GK = 256
GV = 512
NH = 4
CH = 64
LR = 16
NMEM = 256
XD = 256
FF = 4096
ZW = 3104
ZC = 3200
EPS = 1e-6
NDEV = 8

ZB_CB, ZB_CC, ZB_CU, ZB_V, ZB_G = 0, 1, 2, 4, 5
ZB_Q, ZB_K = 6, 7
ZB_LR = 24

TM = 512
TM_MLP = 1024
TF = 512
TB = 256
TT = 512
VMEM_LIMIT = 56 * 1024 * 1024

ADAM_LR, ADAM_B1, ADAM_B2, ADAM_EPS, ADAM_WD, ADAM_STEP = 0.001, 0.9, 0.999, 1e-08, 0.01, 10

WIN_SHARD, XKV_SHARD, UP_SHARD = ZW // NDEV, 2 * D // NDEV, FF // NDEV
SMALL_ROWS = 128

MESH = pl.DeviceIdType.MESH


def _cparams(sem):
    return pltpu.CompilerParams(dimension_semantics=sem, vmem_limit_bytes=VMEM_LIMIT)


def _dot(a, b):
    return jnp.dot(a.astype(BF16), b.astype(BF16), preferred_element_type=F32)


def _dot_nt(a, b):
    return lax.dot_general(a.astype(BF16), b.astype(BF16), (((1,), (1,)), ((), ())), preferred_element_type=F32)


def _dot_tn(a, b):
    return lax.dot_general(a.astype(BF16), b.astype(BF16), (((0,), (0,)), ((), ())), preferred_element_type=F32)


def _split(x, n):
    parts = []
    for _ in range(n):
        p = x.astype(BF16)
        parts.append(p)
        x = x - p.astype(F32)
    return parts


def _dot_exact_lhs(m, x, n):
    return functools.reduce(lambda a, b: a + b, [jnp.dot(m, p, preferred_element_type=F32) for p in _split(x, n)])


def _dot_exact_rhs(x, m, n):
    return functools.reduce(lambda a, b: a + b, [jnp.dot(p, m, preferred_element_type=F32) for p in _split(x, n)])


def _rms(x, g):
    r = lax.rsqrt(jnp.mean(x * x, axis=-1, keepdims=True) + EPS)
    return x * r * g, r


def _rms_bwd(x, r, g, dy):
    xr = x * r
    u = dy * g
    dx = r * (u - xr * jnp.mean(u * xr, axis=-1, keepdims=True))
    return dx, jnp.sum(dy * xr, axis=0, keepdims=True)


def _iota(shape, dim):
    return lax.broadcasted_iota(jnp.int32, shape, dim)


def _sigmoid(x):
    return 1.0 / (1.0 + jnp.exp(-x))


def _acc_rows(ref, row):
    ref[...] += jnp.broadcast_to(row, ref.shape)


def _inproj(x, g, w):
    t = x.shape[0]
    tm = min(TM, t)

    def body(x_ref, g_ref, w_ref, z_ref, h_ref):
        h, _ = _rms(x_ref[...], g_ref[...])
        hb = h.astype(BF16)
        h_ref[...] = hb
        z_ref[...] = jnp.dot(hb, w_ref[...], preferred_element_type=F32)

    return pl.pallas_call(
        body, name="inproj", grid=(t // tm,),
        in_specs=[pl.BlockSpec((tm, D), lambda i: (i, 0)), pl.BlockSpec((1, D), lambda i: (0, 0)),
                  pl.BlockSpec((D, ZC), lambda i: (0, 0))],
        out_specs=[pl.BlockSpec((tm, ZC), lambda i: (i, 0)), pl.BlockSpec((tm, D), lambda i: (i, 0))],
        out_shape=[jax.ShapeDtypeStruct((t, ZC), F32), jax.ShapeDtypeStruct((t, D), BF16)],
        compiler_params=_cparams(("arbitrary",)))(x, g, w)


def _kv_proj(mem, g, w):
    def body(m_ref, g_ref, w_ref, kv_ref, mn_ref):
        mn, _ = _rms(m_ref[...], g_ref[...])
        mb = mn.astype(BF16)
        mn_ref[...] = mb
        for j in range(NDEV):
            kv_ref[:, j * XKV_SHARD:(j + 1) * XKV_SHARD] = jnp.dot(mb, w_ref[j], preferred_element_type=F32)

    return pl.pallas_call(
        body, name="kv_proj",
        out_shape=[jax.ShapeDtypeStruct((NMEM, 2 * D), F32), jax.ShapeDtypeStruct((NMEM, D), BF16)],
        compiler_params=pltpu.CompilerParams(vmem_limit_bytes=VMEM_LIMIT))(mem, g, w)


def _softmax_head(qb, kb):
    s = _dot_nt(qb, kb) * (1.0 / 16.0)
    e = jnp.exp(s - jnp.max(s, axis=-1, keepdims=True))
    return e / jnp.sum(e, axis=-1, keepdims=True)


def _attn_fwd(x, yb, w_out, g, w_xq, kb, vb, w_xo):
    t = x.shape[0]
    tm = min(TM, t)

    def body(x_ref, y_ref, wo_ref, g_ref, wq_ref, k_ref, v_ref, wx_ref, x1_ref, x2_ref, xn_ref, q_ref, a_ref):
        x1 = x_ref[...] + jnp.dot(y_ref[...], wo_ref[...], preferred_element_type=F32)
        x1_ref[...] = x1
        xn, _ = _rms(x1, g_ref[...])
        xb = xn.astype(BF16)
        xn_ref[...] = xb
        qb = jnp.dot(xb, wq_ref[...], preferred_element_type=F32).astype(BF16)
        q_ref[...] = qb
        for h in range(NH):
            hs = slice(h * XD, (h + 1) * XD)
            p = _softmax_head(qb[:, hs], k_ref[:, hs])
            a_ref[:, hs] = _dot(p, v_ref[:, hs]).astype(BF16)
        x2_ref[...] = x1 + jnp.dot(a_ref[...], wx_ref[...], preferred_element_type=F32)

    tok = lambda i: (i, 0)
    full = lambda i: (0, 0)
    return pl.pallas_call(
        body, name="attn_fwd", grid=(t // tm,),
        in_specs=[pl.BlockSpec((tm, D), tok), pl.BlockSpec((tm, D), tok), pl.BlockSpec((D, D), full),
                  pl.BlockSpec((1, D), full), pl.BlockSpec((D, D), full), pl.BlockSpec((NMEM, D), full),
                  pl.BlockSpec((NMEM, D), full), pl.BlockSpec((D, D), full)],
        out_specs=[pl.BlockSpec((tm, D), tok)] * 5,
        out_shape=[jax.ShapeDtypeStruct((t, D), F32), jax.ShapeDtypeStruct((t, D), F32),
                   jax.ShapeDtypeStruct((t, D), BF16), jax.ShapeDtypeStruct((t, D), BF16),
                   jax.ShapeDtypeStruct((t, D), BF16)],
        compiler_params=_cparams(("arbitrary",)))(x, yb, w_out, g, w_xq, kb, vb, w_xo)


def _mlp_fwd(x2, g, w_up, w_down, fg, target):
    t = x2.shape[0]
    tm = min(TM_MLP, t)
    nj = FF // TF

    def body(x_ref, g_ref, wu_ref, wd_ref, fg_ref, t_ref, h1_ref, xn_ref, dx_ref, dxb_ref, loss_ref, dfg_ref,
             acc, xnb):
        i, j = pl.program_id(0), pl.program_id(1)

        @pl.when(j == 0)
        def _():
            xn, _ = _rms(x_ref[...], g_ref[...])
            xnb[...] = xn.astype(BF16)
            xn_ref[...] = xnb[...]
            acc[...] = jnp.zeros_like(acc)

        @pl.when((i == 0) & (j == 0))
        def _():
            loss_ref[...] = jnp.zeros_like(loss_ref)
            dfg_ref[...] = jnp.zeros_like(dfg_ref)

        h1 = jnp.dot(xnb[...], wu_ref[...], preferred_element_type=F32)
        h1_ref[...] = h1
        hr = jnp.maximum(h1, 0.0)
        acc[...] += _dot(hr * hr, wd_ref[...])

        @pl.when(j == nj - 1)
        def _():
            x3 = x_ref[...] + acc[...]
            y, r = _rms(x3, fg_ref[...])
            e = y - t_ref[...]
            row = jnp.mean(e * e, axis=-1, keepdims=True)
            _acc_rows(loss_ref, 0.5 * jnp.sum(row, axis=0, keepdims=True))
            dx, dfg = _rms_bwd(x3, r, fg_ref[...], e * (1.0 / D))
            dx_ref[...] = dx
            dxb_ref[...] = dx.astype(BF16)
            _acc_rows(dfg_ref, dfg)

    tok = lambda i, j: (i, 0)
    full = lambda i, j: (0, 0)
    return pl.pallas_call(
        body, name="mlp_fwd", grid=(t // tm, nj),
        in_specs=[pl.BlockSpec((tm, D), tok), pl.BlockSpec((1, D), full),
                  pl.BlockSpec((None, D, TF), lambda i, j: (j, 0, 0)),
                  pl.BlockSpec((TF, D), lambda i, j: (j, 0)), pl.BlockSpec((1, D), full), pl.BlockSpec((tm, D), tok)],
        out_specs=[pl.BlockSpec((tm, TF), lambda i, j: (i, j)), pl.BlockSpec((tm, D), tok), pl.BlockSpec((tm, D), tok),
                   pl.BlockSpec((tm, D), tok), pl.BlockSpec((8, 128), full), pl.BlockSpec((8, D), full)],
        out_shape=[jax.ShapeDtypeStruct((t, FF), F32), jax.ShapeDtypeStruct((t, D), BF16),
                   jax.ShapeDtypeStruct((t, D), F32), jax.ShapeDtypeStruct((t, D), BF16),
                   jax.ShapeDtypeStruct((8, 128), F32), jax.ShapeDtypeStruct((8, D), F32)],
        scratch_shapes=[pltpu.VMEM((tm, D), F32), pltpu.VMEM((tm, D), BF16)],
        compiler_params=_cparams(("arbitrary", "arbitrary")))(x2, g, w_up, w_down, fg, target)


def _mlp_bwd(dx3, dx3b, h1, w_down, w_up, x2, g):
    t = x2.shape[0]
    tm = min(TM_MLP, t)
    nj = FF // TF

    def body(dx_ref, dxb_ref, h1_ref, wd_ref, wu_ref, x_ref, g_ref, a_ref, dh_ref, dx2_ref, dx2b_ref, dg_ref, acc):
        i, j = pl.program_id(0), pl.program_id(1)

        @pl.when(j == 0)
        def _():
            acc[...] = jnp.zeros_like(acc)

        @pl.when((i == 0) & (j == 0))
        def _():
            dg_ref[...] = jnp.zeros_like(dg_ref)

        hr = jnp.maximum(h1_ref[...], 0.0)
        da = _dot_nt(dxb_ref[...], wd_ref[...])
        dh = (da * 2.0 * hr).astype(BF16)
        a_ref[...] = (hr * hr).astype(BF16)
        dh_ref[...] = dh
        acc[...] += _dot_nt(dh, wu_ref[...])

        @pl.when(j == nj - 1)
        def _():
            x = x_ref[...]
            r = lax.rsqrt(jnp.mean(x * x, axis=-1, keepdims=True) + EPS)
            dx, dg = _rms_bwd(x, r, g_ref[...], acc[...])
            dx2 = dx_ref[...] + dx
            dx2_ref[...] = dx2
            dx2b_ref[...] = dx2.astype(BF16)
            _acc_rows(dg_ref, dg)

    tok = lambda i, j: (i, 0)
    full = lambda i, j: (0, 0)
    hid = lambda i, j: (i, j)
    return pl.pallas_call(
        body, name="mlp_bwd", grid=(t // tm, nj),
        in_specs=[pl.BlockSpec((tm, D), tok), pl.BlockSpec((tm, D), tok), pl.BlockSpec((tm, TF), hid),
                  pl.BlockSpec((TF, D), lambda i, j: (j, 0)), pl.BlockSpec((None, D, TF), lambda i, j: (j, 0, 0)),
                  pl.BlockSpec((tm, D), tok), pl.BlockSpec((1, D), full)],
        out_specs=[pl.BlockSpec((tm, TF), hid), pl.BlockSpec((tm, TF), hid), pl.BlockSpec((tm, D), tok),
                   pl.BlockSpec((tm, D), tok), pl.BlockSpec((8, D), full)],
        out_shape=[jax.ShapeDtypeStruct((t, FF), BF16), jax.ShapeDtypeStruct((t, FF), BF16),
                   jax.ShapeDtypeStruct((t, D), F32), jax.ShapeDtypeStruct((t, D), BF16),
                   jax.ShapeDtypeStruct((8, D), F32)],
        scratch_shapes=[pltpu.VMEM((tm, D), F32)],
        compiler_params=_cparams(("arbitrary", "arbitrary")))(dx3, dx3b, h1, w_down, w_up, x2, g)


def _attn_bwd(x1, dx2, dx2b, qb, kb, vb, w_xo, w_xq, w_out, g):
    t = x1.shape[0]
    tm = min(TM, t)

    def body(x_ref, dx2_ref, dx2b_ref, q_ref, k_ref, v_ref, wx_ref, wq_ref, wo_ref, g_ref,
             dx1_ref, dx1b_ref, dy_ref, dq_ref, dkv_ref, dg_ref):
        @pl.when(pl.program_id(0) == 0)
        def _():
            dkv_ref[...] = jnp.zeros_like(dkv_ref)
            dg_ref[...] = jnp.zeros_like(dg_ref)

        datt = _dot_nt(dx2b_ref[...], wx_ref[...]).astype(BF16)
        for h in range(NH):
            hs = slice(h * XD, (h + 1) * XD)
            q_h, k_h, v_h, da_h = q_ref[:, hs], k_ref[:, hs], v_ref[:, hs], datt[:, hs]
            p = _softmax_head(q_h, k_h)
            dp = _dot_nt(da_h, v_h)
            ds = (p * (dp - jnp.sum(dp * p, axis=-1, keepdims=True)) * (1.0 / 16.0)).astype(BF16)
            dq_ref[:, hs] = _dot(ds, k_h).astype(BF16)
            dkv_ref[:, hs] += _dot_tn(ds, q_h)
            dkv_ref[:, D + h * XD:D + (h + 1) * XD] += _dot_tn(p, da_h)
        dxn = _dot_nt(dq_ref[...], wq_ref[...])
        x = x_ref[...]
        r = lax.rsqrt(jnp.mean(x * x, axis=-1, keepdims=True) + EPS)
        dx, dg = _rms_bwd(x, r, g_ref[...], dxn)
        dx1 = dx2_ref[...] + dx
        dx1_ref[...] = dx1
        dx1b = dx1.astype(BF16)
        dx1b_ref[...] = dx1b
        dy_ref[...] = _dot_nt(dx1b, wo_ref[...])
        _acc_rows(dg_ref, dg)

    tok = lambda i: (i, 0)
    full = lambda i: (0, 0)
    return pl.pallas_call(
        body, name="attn_bwd", grid=(t // tm,),
        in_specs=[pl.BlockSpec((tm, D), tok), pl.BlockSpec((tm, D), tok), pl.BlockSpec((tm, D), tok),
                  pl.BlockSpec((tm, D), tok), pl.BlockSpec((NMEM, D), full), pl.BlockSpec((NMEM, D), full),
                  pl.BlockSpec((D, D), full), pl.BlockSpec((D, D), full), pl.BlockSpec((D, D), full),
                  pl.BlockSpec((1, D), full)],
        out_specs=[pl.BlockSpec((tm, D), tok), pl.BlockSpec((tm, D), tok), pl.BlockSpec((tm, D), tok),
                   pl.BlockSpec((tm, D), tok), pl.BlockSpec((NMEM, 2 * D), full), pl.BlockSpec((8, D), full)],
        out_shape=[jax.ShapeDtypeStruct((t, D), F32), jax.ShapeDtypeStruct((t, D), BF16),
                   jax.ShapeDtypeStruct((t, D), F32), jax.ShapeDtypeStruct((t, D), BF16),
                   jax.ShapeDtypeStruct((NMEM, 2 * D), F32), jax.ShapeDtypeStruct((8, D), F32)],
        compiler_params=_cparams(("arbitrary",)))(x1, dx2, dx2b, qb, kb, vb, w_xo, w_xq, w_out, g)


def _kv_bwd(dkv, memn, mem, g, w):
    def body(dkv_ref, mn_ref, m_ref, g_ref, w_ref, dw_ref, dg_ref):
        dkvb = dkv_ref[...].astype(BF16)
        dmn = jnp.zeros((NMEM, D), F32)
        for j in range(NDEV):
            cols = slice(j * XKV_SHARD, (j + 1) * XKV_SHARD)
            dw_ref[j] = _dot_tn(mn_ref[...], dkvb[:, cols])
            dmn += _dot_nt(dkvb[:, cols], w_ref[j])
        m = m_ref[...]
        r = lax.rsqrt(jnp.mean(m * m, axis=-1, keepdims=True) + EPS)
        dg_ref[...] = jnp.broadcast_to(jnp.sum(dmn * m * r, axis=0, keepdims=True), dg_ref.shape)

    return pl.pallas_call(
        body, name="kv_bwd",
        out_shape=[jax.ShapeDtypeStruct((NDEV, D, XKV_SHARD), F32), jax.ShapeDtypeStruct((8, D), F32)],
        compiler_params=pltpu.CompilerParams(vmem_limit_bytes=VMEM_LIMIT))(dkv, memn, mem, g, w)


def _inproj_bwd(dz, w, x, dx1, g):
    t = x.shape[0]
    tm = min(TM, t)

    def body(dz_ref, w_ref, x_ref, dx1_ref, g_ref, gx_ref, dg_ref):
        @pl.when(pl.program_id(0) == 0)
        def _():
            dg_ref[...] = jnp.zeros_like(dg_ref)

        dh = _dot_nt(dz_ref[...], w_ref[...])
        x = x_ref[...]
        r = lax.rsqrt(jnp.mean(x * x, axis=-1, keepdims=True) + EPS)
        dx, dg = _rms_bwd(x, r, g_ref[...], dh)
        gx_ref[...] = dx1_ref[...] + dx
        _acc_rows(dg_ref, dg)

    tok = lambda i: (i, 0)
    full = lambda i: (0, 0)
    return pl.pallas_call(
        body, name="inproj_bwd", grid=(t // tm,),
        in_specs=[pl.BlockSpec((tm, ZC), tok), pl.BlockSpec((D, ZC), full), pl.BlockSpec((tm, D), tok),
                  pl.BlockSpec((tm, D), tok), pl.BlockSpec((1, D), full)],
        out_specs=[pl.BlockSpec((tm, D), tok), pl.BlockSpec((8, D), full)],
        out_shape=[jax.ShapeDtypeStruct((t, D), F32), jax.ShapeDtypeStruct((8, D), F32)],
        compiler_params=_cparams(("arbitrary",)))(dz, w, x, dx1, g)


def _matmul_tn(a, b, name, col_block=None):
    t, k = a.shape
    n = b.shape[1]
    tk = min(k, 1024)
    tn = col_block or (640 if n % 1024 else 1024)
    tt = min(TT, t)
    nt = t // tt
    if col_block:
        out_spec = pl.BlockSpec((None, tk, tn), lambda i, j, s: (j, i, 0))
        out_shape = jax.ShapeDtypeStruct((n // tn, k, tn), F32)
    else:
        out_spec = pl.BlockSpec((tk, tn), lambda i, j, s: (i, j))
        out_shape = jax.ShapeDtypeStruct((k, n), F32)

    def body(a_ref, b_ref, o_ref):
        @pl.when(pl.program_id(2) == 0)
        def _():
            o_ref[...] = jnp.zeros_like(o_ref)

        o_ref[...] += _dot_tn(a_ref[...], b_ref[...])

    return pl.pallas_call(
        body, name=name, grid=(k // tk, n // tn, nt),
        in_specs=[pl.BlockSpec((tt, tk), lambda i, j, s: (s, i)), pl.BlockSpec((tt, tn), lambda i, j, s: (s, j))],
        out_specs=out_spec, out_shape=out_shape,
        compiler_params=_cparams(("arbitrary", "arbitrary", "arbitrary")))(a, b)


def _lane_head(shape, dim, shift):
    return _iota(shape, dim) >> shift


def _gla_recompute(q_raw, k, lr, wpad, bias, rev, tb):
    pre = _dot(lr, wpad) + bias
    la = (jnp.minimum(pre, 0.0) - jnp.log(1.0 + jnp.exp(-jnp.abs(pre)))) * (1.0 / 16.0)
    r, c = _iota((tb, tb), 0), _iota((tb, tb), 1)
    tri = (c >= r) if rev else (c <= r)
    cum = jnp.where(((r >> 6) == (c >> 6)) & tri, 1.0, 0.0).astype(BF16)
    b = _dot_exact_lhs(cum, la, 3)
    e, ei = jnp.exp(b), jnp.exp(-b)
    qt = (q_raw * 0.125) * e
    kt = k * ei
    return pre, b, e, ei, qt, kt


def _stack_heads(x, shift):
    head = _lane_head(x.shape, 1, shift)
    return jnp.concatenate([jnp.where(head == h, x, 0.0) for h in range(NH)], axis=0).astype(BF16)


def _fold_heads(x, shift):
    head = _lane_head((CH, x.shape[1]), 1, shift)
    return functools.reduce(lambda a, b: a + b,
                            [jnp.where(head == h, x[h * CH:(h + 1) * CH], 0.0) for h in range(NH)])


def _wide_mask(rev):
    r, s = _iota((CH, NH * CH), 0), _iota((CH, NH * CH), 1) & (CH - 1)
    return (s >= r) if rev else (s <= r)


def _state_mask():
    return (_iota((GV, GK), 0) >> 7) == (_iota((GV, GK), 1) >> 6)


def _state_expand(sd):
    head = _lane_head(sd.shape, 1, 6)
    return jnp.concatenate([jnp.where(head == h, sd, 0.0) for h in range(NH)], axis=0)


def _conv_parts(cb, cc, cu, ccp, cup, ccn, cun, cw_ref, first, last, tb):
    h = cc * cu
    hp = jnp.where(first, 0.0, ccp * cup)
    hn = jnp.where(last, 0.0, ccn * cun)
    rows = _iota(h.shape, 0)
    h_m1 = jnp.where(rows == 0, hp, pltpu.roll(h, 1, 0))
    h_p1 = jnp.where(rows == tb - 1, hn, pltpu.roll(h, tb - 1, 0))
    conv = cw_ref[pl.ds(0, 1), :] * h_m1 + cw_ref[pl.ds(1, 1), :] * h + cw_ref[pl.ds(2, 1), :] * h_p1
    return h, h_m1, h_p1, conv


def _group_ones():
    return jnp.where((_iota((CW, CW), 0) >> 6) == (_iota((CW, CW), 1) >> 6), 1.0, 0.0).astype(BF16)


def _head_norm(o):
    ons, rs = [], []
    for h in range(NH):
        slab = o[:, h * 128:(h + 1) * 128]
        r = lax.rsqrt(jnp.mean(slab * slab, axis=-1, keepdims=True) + EPS)
        ons.append(slab * r)
        rs.append(jnp.broadcast_to(r, slab.shape))
    return jnp.concatenate(ons, axis=1), jnp.concatenate(rs, axis=1)


def _zspec(tb, width, blk, jmap):
    return pl.BlockSpec((tb, width), lambda i: (jmap(i), blk))


def _halo_specs(tb, nblk, t, blk, jmap):
    prev = pl.BlockSpec((8, CW), lambda i: (jnp.maximum(jmap(i) * (tb // 8) - 1, 0), blk))
    nxt = pl.BlockSpec((8, CW), lambda i: (jnp.minimum((jmap(i) + 1) * (tb // 8), t // 8 - 1), blk))
    return prev, nxt


def _gla_fwd_sweep(z, wpad, bias, rev, finish_args=None):
    t = z.shape[0]
    tb = min(TB, t)
    nblk, nb = t // tb, tb // CH
    jmap = (lambda i: nblk - 1 - i) if rev else (lambda i: i)
    finish = finish_args is not None

    def body(*refs):
        if finish:
            (q_ref, k_ref, v_ref, lr_ref, w_ref, bias_ref, of_ref, g_ref, cb_ref, cc_ref, cu_ref, ccp_ref, ccn_ref,
             cup_ref, cun_ref, cw_ref, cn_ref, gn_ref, y_ref, opre_ref, sd_ref, st, b_scr, o_scr) = refs
        else:
            q_ref, k_ref, v_ref, lr_ref, w_ref, bias_ref, o_ref, sd_ref, st, b_scr = refs
            o_scr = o_ref
        i = pl.program_id(0)

        @pl.when(i == 0)
        def _():
            st[...] = jnp.zeros_like(st)

        q_raw, k, v = q_ref[...], k_ref[...], v_ref[...]
        _, b, _, _, qt, kt = _gla_recompute(q_raw, k, lr_ref[...], w_ref[...], bias_ref[...], rev, tb)
        b_scr[...] = b
        maskw, bd = _wide_mask(rev), _state_mask()
        for c in (reversed(range(nb)) if rev else range(nb)):
            sl = slice(c * CH, (c + 1) * CH)
            gdec = jnp.exp(b_scr[pl.ds(c * CH + (0 if rev else CH - 1), 1), :])
            qt_c, kt_c, v_c = qt[sl], kt[sl], v[sl]
            s_in = st[...]
            sd_ref[c] = s_in[0:128] + s_in[128:256] + s_in[256:384] + s_in[384:512]
            a = jnp.where(maskw, _dot_nt(qt_c, _stack_heads(kt_c, 6)), 0.0)
            o_scr[pl.ds(c * CH, CH), :] = _dot(a, _stack_heads(v_c, 7)) + _dot_nt(qt_c, s_in)
            st[...] = s_in * gdec + jnp.where(bd, _dot_tn(v_c, kt_c * gdec), 0.0)

        if finish:
            j = jmap(i)
            hsel = jnp.where((_iota((GK, GV), 0) >> 6) == (_iota((GK, GV), 1) >> 7), 1.0, 0.0).astype(BF16)
            sb = _dot_exact_rhs((q_raw * 0.125) * k, hsel, 2)
            o_pre = of_ref[...] + o_scr[...] - sb * v
            opre_ref[...] = o_pre
            on, _ = _head_norm(o_pre)
            g = g_ref[...]
            y_ref[:, CW:] = (on * gn_ref[...] * (g * _sigmoid(g))).astype(BF16)
            cb = cb_ref[...]
            _, _, _, conv = _conv_parts(cb, cc_ref[...], cu_ref[...], ccp_ref[pl.ds(7, 1), :], cup_ref[pl.ds(7, 1), :],
                                        ccn_ref[pl.ds(0, 1), :], cun_ref[pl.ds(0, 1), :], cw_ref, j == 0,
                                        j == nblk - 1, tb)
            yc = cb * conv
            gm = _dot_exact_rhs(yc * yc, _group_ones(), 2) * (1.0 / 64.0)
            y_ref[:, :CW] = (yc * lax.rsqrt(gm + EPS) * cn_ref[...]).astype(BF16)

    full = lambda i: (0, 0)
    in_specs = [_zspec(tb, GK, ZB_Q, jmap), _zspec(tb, GK, ZB_K, jmap), _zspec(tb, GV, ZB_V, jmap),
                _zspec(tb, 128, ZB_LR, jmap), pl.BlockSpec((128, GK), full), pl.BlockSpec((1, GK), full)]
    args = [z, z, z, z, wpad, bias]
    sd_spec = pl.BlockSpec((nb, 128, GK), lambda i: (jmap(i), 0, 0))
    sd_shape = jax.ShapeDtypeStruct((t // CH, 128, GK), F32)
    scratch = [pltpu.VMEM((GV, GK), F32), pltpu.VMEM((tb, GK), F32)]
    if finish:
        o_f, conv_w, conv_norm, gla_norm4 = finish_args
        ccp, ccn = _halo_specs(tb, nblk, t, ZB_CC, jmap)
        cup, cun = _halo_specs(tb, nblk, t, ZB_CU, jmap)
        in_specs += [pl.BlockSpec((tb, GV), lambda i: (jmap(i), 0)), _zspec(tb, GV, ZB_G, jmap),
                     _zspec(tb, CW, ZB_CB, jmap), _zspec(tb, CW, ZB_CC, jmap), _zspec(tb, CW, ZB_CU, jmap),
                     ccp, ccn, cup, cun, pl.BlockSpec((3, CW), full), pl.BlockSpec((1, CW), full),
                     pl.BlockSpec((1, GV), full)]
        args += [o_f, z, z, z, z, z, z, z, z, conv_w, conv_norm, gla_norm4]
        out_specs = [pl.BlockSpec((tb, D), lambda i: (jmap(i), 0)), pl.BlockSpec((tb, GV), lambda i: (jmap(i), 0)),
                     sd_spec]
        out_shape = [jax.ShapeDtypeStruct((t, D), BF16), jax.ShapeDtypeStruct((t, GV), F32), sd_shape]
        scratch.append(pltpu.VMEM((tb, GV), F32))
    else:
        out_specs = [pl.BlockSpec((tb, GV), lambda i: (jmap(i), 0)), sd_spec]
        out_shape = [jax.ShapeDtypeStruct((t, GV), F32), sd_shape]
    return pl.pallas_call(
        body, name="gla_fwd_rev" if rev else "gla_fwd", grid=(nblk,), in_specs=in_specs, out_specs=out_specs,
        out_shape=out_shape, scratch_shapes=scratch, compiler_params=_cparams(("arbitrary",)))(*args)


def _gla_bwd_chunks(do_ref, sd_ref, dst, b_scr, db_scr, dq_ref, dk_ref, dv_ref, qt, kt, e, ei, v, rev, nb):
    maskw, bd = _wide_mask(rev), _state_mask()
    for c in (range(nb) if rev else reversed(range(nb))):
        sl = slice(c * CH, (c + 1) * CH)
        grow = c * CH + (0 if rev else CH - 1)
        gdec = jnp.exp(b_scr[pl.ds(grow, 1), :])
        qt_c, kt_c, v_c, do_c = qt[sl], kt[sl], v[sl], do_ref[pl.ds(c * CH, CH), :]
        s_in = _state_expand(sd_ref[c])
        ds_out = dst[...]
        kbd, vbd = _stack_heads(kt_c, 6), _stack_heads(v_c, 7)
        a = jnp.where(maskw, _dot_nt(qt_c, kbd), 0.0)
        da = jnp.where(maskw, _dot_nt(do_c, vbd), 0.0)
        kh = kt_c * gdec
        dv_ref[pl.ds(c * CH, CH), :] = _fold_heads(_dot_tn(a, do_c), 7) + _dot_nt(kh, ds_out)
        dqt = _dot(da, kbd) + _dot(do_c, s_in)
        dkh = _dot(v_c, ds_out)
        dkt = _fold_heads(_dot_tn(da, qt_c), 6) + dkh * gdec
        dg = jnp.sum(ds_out * s_in, axis=0, keepdims=True) + jnp.sum(kt_c * dkh, axis=0, keepdims=True)
        db_scr[pl.ds(c * CH, CH), :] = dqt * qt_c - dkt * kt_c
        db_scr[pl.ds(grow, 1), :] += dg * gdec
        dq_ref[pl.ds(c * CH, CH), :] = dqt * e[sl] * 0.125
        dk_ref[pl.ds(c * CH, CH), :] = dkt * ei[sl]
        dst[...] = ds_out * gdec + jnp.where(bd, _dot_tn(do_c, qt_c), 0.0)


def _gate_bwd(db, pre, lr, wpad, rev, tb):
    r, c = _iota((tb, tb), 0), _iota((tb, tb), 1)
    tri = (c <= r) if rev else (c >= r)
    cum_t = jnp.where(((r >> 6) == (c >> 6)) & tri, 1.0, 0.0).astype(BF16)
    dla = _dot_exact_lhs(cum_t, db, 2)
    dpre = dla * (1.0 / 16.0) / (1.0 + jnp.exp(pre))
    return dpre, _dot_nt(dpre, wpad), _dot_tn(lr, dpre)


def _gla_bwd_first(z, dy, o_pre, sd, wpad, bias, conv_w, conv_norm, gla_norm4):
    t = z.shape[0]
    tb = min(TB, t)
    nblk, nb = t // tb, tb // CH
    jmap = lambda i: nblk - 1 - i

    def body(q_ref, k_ref, v_ref, lr_ref, g_ref, cb_ref, cc_ref, cu_ref, ccp_ref, ccn_ref, cup_ref, cun_ref,
             dy_ref, opre_ref, sd_ref, w_ref, bias_ref, cw_ref, cn_ref, gn_ref,
             do_ref, dq_ref, dk_ref, dv_ref, dlr_ref, dzg_ref, dzcb_ref, dconv_ref,
             dw_ref, dbias_ref, dcw_ref, dcn_ref, dgn_ref, dst, b_scr, db_scr):
        i = pl.program_id(0)
        j = jmap(i)

        @pl.when(i == 0)
        def _():
            dst[...] = jnp.zeros_like(dst)
            for ref in (dw_ref, dbias_ref, dcw_ref, dcn_ref, dgn_ref):
                ref[...] = jnp.zeros_like(ref)

        dyg = dy_ref[:, CW:]
        g = g_ref[...]
        sig = _sigmoid(g)
        on, rr = _head_norm(opre_ref[...])
        gn = gn_ref[...]
        dzg_ref[...] = (dyg * on * gn * (sig * (1.0 + g * (1.0 - sig)))).astype(BF16)
        don = dyg * (g * sig)
        _acc_rows(dgn_ref, jnp.sum(don * on, axis=0, keepdims=True))
        u = don * gn
        uo = u * on
        mean_uo = jnp.concatenate(
            [jnp.broadcast_to(jnp.mean(uo[:, h * 128:(h + 1) * 128], axis=-1, keepdims=True), (tb, 128))
             for h in range(NH)], axis=1)
        do_ref[...] = rr * (u - on * mean_uo)

        cb = cb_ref[...]
        h, h_m1, h_p1, conv = _conv_parts(cb, cc_ref[...], cu_ref[...], ccp_ref[pl.ds(7, 1), :],
                                          cup_ref[pl.ds(7, 1), :], ccn_ref[pl.ds(0, 1), :], cun_ref[pl.ds(0, 1), :],
                                          cw_ref, j == 0, j == nblk - 1, tb)
        yc = cb * conv
        ones = _group_ones()
        rc = lax.rsqrt(_dot_exact_rhs(yc * yc, ones, 2) * (1.0 / 64.0) + EPS)
        ycr = yc * rc
        dyn = dy_ref[:, :CW]
        _acc_rows(dcn_ref, jnp.sum(dyn * ycr, axis=0, keepdims=True))
        uc = dyn * cn_ref[...]
        dyc = rc * (uc - ycr * (_dot_exact_rhs(uc * ycr, ones, 2) * (1.0 / 64.0)))
        dzcb_ref[...] = (dyc * conv).astype(BF16)
        dconv = dyc * cb
        dconv_ref[...] = dconv
        dcw_ref[pl.ds(0, 1), :] += jnp.sum(dconv * h_m1, axis=0, keepdims=True)
        dcw_ref[pl.ds(1, 1), :] += jnp.sum(dconv * h, axis=0, keepdims=True)
        dcw_ref[pl.ds(2, 1), :] += jnp.sum(dconv * h_p1, axis=0, keepdims=True)

        lr, wp = lr_ref[...], w_ref[...]
        pre, b, e, ei, qt, kt = _gla_recompute(q_ref[...], k_ref[...], lr, wp, bias_ref[...], False, tb)
        b_scr[...] = b
        _gla_bwd_chunks(do_ref, sd_ref, dst, b_scr, db_scr, dq_ref, dk_ref, dv_ref, qt, kt, e, ei, v_ref[...],
                        False, nb)
        dpre, dlr, dw = _gate_bwd(db_scr[...], pre, lr, wp, False, tb)
        dlr_ref[...] = dlr
        dw_ref[...] += dw
        _acc_rows(dbias_ref, jnp.sum(dpre, axis=0, keepdims=True))

    full = lambda i: (0, 0)
    tokv = pl.BlockSpec((tb, GV), lambda i: (jmap(i), 0))
    tokk = pl.BlockSpec((tb, GK), lambda i: (jmap(i), 0))
    ccp, ccn = _halo_specs(tb, nblk, t, ZB_CC, jmap)
    cup, cun = _halo_specs(tb, nblk, t, ZB_CU, jmap)
    in_specs = [_zspec(tb, GK, ZB_Q, jmap), _zspec(tb, GK, ZB_K, jmap), _zspec(tb, GV, ZB_V, jmap),
                _zspec(tb, 128, ZB_LR, jmap), _zspec(tb, GV, ZB_G, jmap), _zspec(tb, CW, ZB_CB, jmap),
                _zspec(tb, CW, ZB_CC, jmap), _zspec(tb, CW, ZB_CU, jmap), ccp, ccn, cup, cun,
                pl.BlockSpec((tb, D), lambda i: (jmap(i), 0)), tokv,
                pl.BlockSpec((nb, 128, GK), lambda i: (jmap(i), 0, 0)), pl.BlockSpec((128, GK), full),
                pl.BlockSpec((1, GK), full), pl.BlockSpec((3, CW), full), pl.BlockSpec((1, CW), full),
                pl.BlockSpec((1, GV), full)]
    out_specs = [tokv, tokk, tokk, tokv, pl.BlockSpec((tb, 128), lambda i: (jmap(i), 0)), tokv, tokv, tokv,
                 pl.BlockSpec((128, GK), full), pl.BlockSpec((8, GK), full), pl.BlockSpec((8, CW), full),
                 pl.BlockSpec((8, CW), full), pl.BlockSpec((8, GV), full)]
    out_shape = [jax.ShapeDtypeStruct((t, GV), F32), jax.ShapeDtypeStruct((t, GK), F32),
                 jax.ShapeDtypeStruct((t, GK), F32), jax.ShapeDtypeStruct((t, GV), F32),
                 jax.ShapeDtypeStruct((t, 128), F32), jax.ShapeDtypeStruct((t, GV), BF16),
                 jax.ShapeDtypeStruct((t, CW), BF16), jax.ShapeDtypeStruct((t, CW), F32),
                 jax.ShapeDtypeStruct((128, GK), F32), jax.ShapeDtypeStruct((8, GK), F32),
                 jax.ShapeDtypeStruct((8, CW), F32), jax.ShapeDtypeStruct((8, CW), F32),
                 jax.ShapeDtypeStruct((8, GV), F32)]
    return pl.pallas_call(
        body, name="gla_bwd_first", grid=(nblk,), in_specs=in_specs, out_specs=out_specs, out_shape=out_shape,
        scratch_shapes=[pltpu.VMEM((GV, GK), F32), pltpu.VMEM((tb, GK), F32), pltpu.VMEM((tb, GK), F32)],
        compiler_params=_cparams(("arbitrary",)))(
            z, z, z, z, z, z, z, z, z, z, z, z, dy, o_pre, sd, wpad, bias, conv_w, conv_norm, gla_norm4)


def _gla_bwd_second(z, do, sd, wpad, bias, dqa, dka, dva, dlra, dzg, dzcb, dconv, conv_w):
    t = z.shape[0]
    tb = min(TB, t)
    nblk, nb = t // tb, tb // CH
    jmap = lambda i: i

    def body(q_ref, k_ref, v_ref, lr_ref, cc_ref, cu_ref, do_ref, sd_ref, w_ref, bias_ref, dqa_ref, dka_ref,
             dva_ref, dlra_ref, dzg_ref, dzcb_ref, dc_ref, dcp_ref, dcn_ref, cw_ref,
             dz_ref, dw_ref, dbias_ref, dst, b_scr, db_scr, dq_scr, dk_scr, dv_scr):
        i = pl.program_id(0)

        @pl.when(i == 0)
        def _():
            dst[...] = jnp.zeros_like(dst)
            dw_ref[...] = jnp.zeros_like(dw_ref)
            dbias_ref[...] = jnp.zeros_like(dbias_ref)

        q_raw, k, v, lr, wp = q_ref[...], k_ref[...], v_ref[...], lr_ref[...], w_ref[...]
        pre, b, e, ei, qt, kt = _gla_recompute(q_raw, k, lr, wp, bias_ref[...], True, tb)
        b_scr[...] = b
        _gla_bwd_chunks(do_ref, sd_ref, dst, b_scr, db_scr, dq_scr, dk_scr, dv_scr, qt, kt, e, ei, v, True, nb)
        dpre, dlr, dw = _gate_bwd(db_scr[...], pre, lr, wp, True, tb)
        dw_ref[...] += dw
        _acc_rows(dbias_ref, jnp.sum(dpre, axis=0, keepdims=True))

        do = do_ref[...]
        qs = q_raw * 0.125
        hsel = jnp.where((_iota((GK, GV), 0) >> 6) == (_iota((GK, GV), 1) >> 7), 1.0, 0.0).astype(BF16)
        hsel_t = jnp.where((_iota((GV, GK), 0) >> 7) == (_iota((GV, GK), 1) >> 6), 1.0, 0.0).astype(BF16)
        sb = _dot_exact_rhs(qs * k, hsel, 2)
        dsk = _dot_exact_rhs(do * v, hsel_t, 2)
        dz_ref[:, 1536:1792] = (dqa_ref[...] + dq_scr[...] - dsk * k * 0.125).astype(BF16)
        dz_ref[:, 1792:2048] = (dka_ref[...] + dk_scr[...] - dsk * qs).astype(BF16)
        dz_ref[:, 2048:2560] = (dva_ref[...] + dv_scr[...] - sb * do).astype(BF16)
        dz_ref[:, 2560:3072] = dzg_ref[...]
        dz_ref[:, 3072:3200] = (dlra_ref[...] + dlr).astype(BF16)

        dc = dc_ref[...]
        rows = _iota(dc.shape, 0)
        dprev = jnp.where(i == 0, 0.0, dcp_ref[pl.ds(7, 1), :])
        dnext = jnp.where(i == nblk - 1, 0.0, dcn_ref[pl.ds(0, 1), :])
        dc_m1 = jnp.where(rows == 0, dprev, pltpu.roll(dc, 1, 0))
        dc_p1 = jnp.where(rows == tb - 1, dnext, pltpu.roll(dc, tb - 1, 0))
        dh = cw_ref[pl.ds(0, 1), :] * dc_p1 + cw_ref[pl.ds(1, 1), :] * dc + cw_ref[pl.ds(2, 1), :] * dc_m1
        dz_ref[:, 0:512] = dzcb_ref[...]
        dz_ref[:, 512:1024] = (dh * cu_ref[...]).astype(BF16)
        dz_ref[:, 1024:1536] = (dh * cc_ref[...]).astype(BF16)

    full = lambda i: (0, 0)
    tokv = pl.BlockSpec((tb, GV), lambda i: (i, 0))
    tokk = pl.BlockSpec((tb, GK), lambda i: (i, 0))
    dcp = pl.BlockSpec((8, CW), lambda i: (jnp.maximum(i * (tb // 8) - 1, 0), 0))
    dcn = pl.BlockSpec((8, CW), lambda i: (jnp.minimum((i + 1) * (tb // 8), t // 8 - 1), 0))
    in_specs = [_zspec(tb, GK, ZB_Q, jmap), _zspec(tb, GK, ZB_K, jmap), _zspec(tb, GV, ZB_V, jmap),
                _zspec(tb, 128, ZB_LR, jmap), _zspec(tb, CW, ZB_CC, jmap), _zspec(tb, CW, ZB_CU, jmap), tokv,
                pl.BlockSpec((nb, 128, GK), lambda i: (i, 0, 0)), pl.BlockSpec((128, GK), full),
                pl.BlockSpec((1, GK), full), tokk, tokk, tokv, pl.BlockSpec((tb, 128), lambda i: (i, 0)), tokv, tokv,
                tokv, dcp, dcn, pl.BlockSpec((3, CW), full)]
    out_specs = [pl.BlockSpec((tb, ZC), lambda i: (i, 0)), pl.BlockSpec((128, GK), full), pl.BlockSpec((8, GK), full)]
    out_shape = [jax.ShapeDtypeStruct((t, ZC), BF16), jax.ShapeDtypeStruct((128, GK), F32),
                 jax.ShapeDtypeStruct((8, GK), F32)]
    return pl.pallas_call(
        body, name="gla_bwd_second", grid=(nblk,), in_specs=in_specs, out_specs=out_specs, out_shape=out_shape,
        scratch_shapes=[pltpu.VMEM((GV, GK), F32), pltpu.VMEM((tb, GK), F32), pltpu.VMEM((tb, GK), F32),
                        pltpu.VMEM((tb, GK), F32), pltpu.VMEM((tb, GK), F32), pltpu.VMEM((tb, GV), F32)],
        compiler_params=_cparams(("arbitrary",)))(
            z, z, z, z, z, z, do, sd, wpad, bias, dqa, dka, dva, dlra, dzg, dzcb, dconv, dconv, dconv, conv_w)


def _local_step(x, mem, target, p):
    assert TF == UP_SHARD
    zeros_lr = jnp.zeros((128 - LR, GK), BF16)
    waf_pad = jnp.concatenate([p["w_af"].astype(BF16), zeros_lr], axis=0)
    wab_pad = jnp.concatenate([jnp.zeros((LR, GK), BF16), p["w_ab"].astype(BF16), zeros_lr[:128 - 2 * LR]], axis=0)
    gla_norm4 = jnp.tile(p["gla_norm"], (1, NH))

    z, hb = _inproj(x, p["mix_norm"], p["w_in"])
    o_f, sd_f = _gla_fwd_sweep(z, waf_pad, p["b_af"], False)
    yb, o_pre, sd_b = _gla_fwd_sweep(z, wab_pad, p["b_ab"], True,
                                     (o_f, p["conv_w"], p["conv_norm"], gla_norm4))
    kv, memn = _kv_proj(mem, p["mem_norm"], p["w_xkv"])
    kb, vb = kv[:, :D].astype(BF16), kv[:, D:].astype(BF16)
    x1, x2, xn1, qb, attb = _attn_fwd(x, yb, p["w_out"], p["xa_norm"], p["w_xq"], kb, vb, p["w_xo"])
    h1, xn2, dx3, dx3b, loss8, dfinal = _mlp_fwd(x2, p["mlp_norm"], p["w_up"], p["w_down"], p["final_norm"], target)

    ab, dh1b, dx2, dx2b, dmlp = _mlp_bwd(dx3, dx3b, h1, p["w_down"], p["w_up"], x2, p["mlp_norm"])
    dx1, dx1b, dy, dqb, dkv, dxa = _attn_bwd(x1, dx2, dx2b, qb, kb, vb, p["w_xo"], p["w_xq"], p["w_out"],
                                             p["xa_norm"])
    dw_xkv, dmemn = _kv_bwd(dkv, memn, mem, p["mem_norm"], p["w_xkv"])
    (do, dqa, dka, dva, dlra, dzg, dzcb, dconv, dwaf, dbaf, dcw, dcn, dgn) = _gla_bwd_first(
        z, dy, o_pre, sd_f, waf_pad, p["b_af"], p["conv_w"], p["conv_norm"], gla_norm4)
    dz, dwab, dbab = _gla_bwd_second(z, do, sd_b, wab_pad, p["b_ab"], dqa, dka, dva, dlra, dzg, dzcb, dconv,
                                     p["conv_w"])
    grad_x, dmix = _inproj_bwd(dz, p["w_in"], x, dx1, p["mix_norm"])

    grads = {
        "w_in": _matmul_tn(hb, dz, "dw_in"),
        "w_out": _matmul_tn(yb, dx1b, "dw_out"),
        "w_xq": _matmul_tn(xn1, dqb, "dw_xq"),
        "w_xo": _matmul_tn(attb, dx2b, "dw_xo"),
        "w_xkv": dw_xkv,
        "w_up": _matmul_tn(xn2, dh1b, "dw_up", col_block=UP_SHARD),
        "w_down": _matmul_tn(ab, dx3b, "dw_down"),
        "mix_norm": dmix[0:1], "conv_w": dcw[0:3], "conv_norm": dcn[0:1],
        "w_af": dwaf[0:LR], "b_af": dbaf[0:1], "w_ab": dwab[LR:2 * LR], "b_ab": dbab[0:1],
        "gla_norm": (dgn[0:1, 0:128] + dgn[0:1, 128:256]) + (dgn[0:1, 256:384] + dgn[0:1, 384:512]),
        "xa_norm": dxa[0:1], "mem_norm": dmemn[0:1], "mlp_norm": dmlp[0:1], "final_norm": dfinal[0:1],
    }
    return loss8[0:1, 0:1], grad_x, grads


def _place():
    return lax.axis_index("x"), lax.axis_index("y"), lax.axis_index("c")


class _Rider:
    def __init__(self, arrays, out_shape, scratch, start, finish):
        self.arrays, self.out_shape, self.scratch, self.start, self.finish = arrays, out_shape, scratch, start, finish


def _gather_rider(blks):
    n = len(blks)

    def plan(in_refs, out_refs, sems):
        send_sems, recv_sems, local_sems = sems
        x, y, c = _place()
        me, sibling = (x, y, c), (x, y, 1 - c)
        chips = [(1 - x, y), (x, 1 - y), (1 - x, 1 - y)]

        def copy(a, k, block, to, own=False):
            px, py, pc = block
            dst = out_refs[a].at[4 * px + 2 * py + pc]
            return pltpu.make_async_remote_copy(
                src_ref=in_refs[a] if own else dst, dst_ref=dst, send_sem=send_sems.at[k, a],
                recv_sem=recv_sems.at[k, a], device_id=to, device_id_type=MESH)

        mine = [pltpu.make_async_copy(in_refs[a], out_refs[a].at[4 * x + 2 * y + c], local_sems.at[a])
                for a in range(n)]
        first = [[copy(a, 0, me, sibling, own=True)] + [copy(a, 1 + j, me, (*chip, c), own=True)
                                                        for j, chip in enumerate(chips)] for a in range(n)]
        passed = [[copy(a, 4 + j, (*chip, c), sibling) for j, chip in enumerate(chips)] for a in range(n)]
        landed = [[copy(a, 1 + j, (*chip, c), me) for j, chip in enumerate(chips)] for a in range(n)]
        from_sibling = [[copy(a, 0, sibling, me)] + [copy(a, 4 + j, (*chip, 1 - c), me)
                                                     for j, chip in enumerate(chips)] for a in range(n)]
        return mine, first, passed, landed, from_sibling

    def start(in_refs, out_refs, sems):
        mine, first, _, _, _ = plan(in_refs, out_refs, sems)
        for a in range(n):
            mine[a].start()
            for cp in first[a]:
                cp.start()

    def finish(in_refs, out_refs, sems):
        mine, first, passed, landed, from_sibling = plan(in_refs, out_refs, sems)
        for j in range(3):
            for a in range(n):
                landed[a][j].wait_recv()
                passed[a][j].start()
        for a in range(n):
            for cp in from_sibling[a]:
                cp.wait_recv()
            for cp in first[a] + passed[a]:
                cp.wait_send()
            mine[a].wait()

    return _Rider(blks, [jax.ShapeDtypeStruct((NDEV,) + b.shape, b.dtype) for b in blks],
                  [pltpu.SemaphoreType.DMA((7, n)), pltpu.SemaphoreType.DMA((7, n)), pltpu.SemaphoreType.DMA((n,))],
                  start, finish)


def _sibling_rider(g4s):
    n = len(g4s)

    def copies(in_refs, out_refs, sems):
        send_sems, recv_sems = sems
        x, y, c = _place()
        return [pltpu.make_async_remote_copy(
            src_ref=in_refs[a].at[k, 1 - c], dst_ref=out_refs[a].at[k], send_sem=send_sems.at[k, a],
            recv_sem=recv_sems.at[k, a], device_id=(x, y, 1 - c), device_id_type=MESH)
            for a in range(n) for k in range(4)]

    def start(in_refs, out_refs, sems):
        for cp in copies(in_refs, out_refs, sems):
            cp.start()

    def finish(in_refs, out_refs, sems):
        for cp in copies(in_refs, out_refs, sems):
            cp.wait()

    return _Rider(g4s, [jax.ShapeDtypeStruct((4,) + g.shape[2:], g.dtype) for g in g4s],
                  [pltpu.SemaphoreType.DMA((4, n)), pltpu.SemaphoreType.DMA((4, n))], start, finish)


def _chips_rider(pbs):
    n = len(pbs)

    def copies(in_refs, out_refs, sems):
        send_sems, recv_sems = sems
        x, y, c = _place()
        peers = [(1 - x, y), (x, 1 - y), (1 - x, 1 - y)]
        return [pltpu.make_async_remote_copy(
            src_ref=in_refs[a].at[2 * px + py], dst_ref=out_refs[a].at[k], send_sem=send_sems.at[k, a],
            recv_sem=recv_sems.at[k, a], device_id=(px, py, c), device_id_type=MESH)
            for a in range(n) for k, (px, py) in enumerate(peers)]

    def start(in_refs, out_refs, sems):
        for cp in copies(in_refs, out_refs, sems):
            cp.start()

    def finish(in_refs, out_refs, sems):
        for cp in copies(in_refs, out_refs, sems):
            cp.wait()

    return _Rider(pbs, [jax.ShapeDtypeStruct((3,) + p.shape[1:], p.dtype) for p in pbs],
                  [pltpu.SemaphoreType.DMA((3, n)), pltpu.SemaphoreType.DMA((3, n))], start, finish)


def _exchange(rider, name):
    n_in, n_out = len(rider.arrays), len(rider.out_shape)

    def body(*refs):
        ins, outs, sems = refs[:n_in], refs[n_in:n_in + n_out], refs[n_in + n_out:]
        rider.start(ins, outs, sems)
        rider.finish(ins, outs, sems)

    hbm = pl.BlockSpec(memory_space=pltpu.HBM)
    return pl.pallas_call(body, name=name, out_shape=rider.out_shape, in_specs=[hbm] * n_in,
                          out_specs=[hbm] * n_out, scratch_shapes=rider.scratch)(*rider.arrays)


def _rs_pair_sum(place, g4, r1, name):
    rows, cols = g4.shape[2:]
    tr = min(rows, 512)

    def body(pl_ref, g_ref, r_ref, pb_ref, own_ref):
        s = g_ref[0, 0] + r_ref[0]
        pb_ref[0] = s.astype(BF16)

        @pl.when(pl.program_id(1) == pl_ref[0])
        def _():
            own_ref[...] = s

    grid_spec = pltpu.PrefetchScalarGridSpec(
        num_scalar_prefetch=1, grid=(rows // tr, 4),
        in_specs=[pl.BlockSpec((1, 1, tr, cols), lambda r, k, p: (k, p[1], r, 0)),
                  pl.BlockSpec((1, tr, cols), lambda r, k, p: (k, r, 0))],
        out_specs=[pl.BlockSpec((1, tr, cols), lambda r, k, p: (k, r, 0)),
                   pl.BlockSpec((tr, cols), lambda r, k, p: (r, 0))])
    return pl.pallas_call(
        body, name=name, grid_spec=grid_spec,
        out_shape=[jax.ShapeDtypeStruct((4, rows, cols), BF16), jax.ShapeDtypeStruct((rows, cols), F32)],
        compiler_params=_cparams(("arbitrary", "arbitrary")))(place, g4, r1)


def _small_all_reduce(vec):
    m_per = vec.shape[0]

    def body(x_ref, all_ref, sum_ref, send_sems, recv_sems, local_sem):
        x, y, c = _place()
        me, sibling = (x, y, c), (x, y, 1 - c)
        chips = [(1 - x, y), (x, 1 - y), (1 - x, 1 - y)]

        def rows(px, py, pc):
            return all_ref.at[4 * px + 2 * py + pc]

        def copy(k, block, to, src=None):
            return pltpu.make_async_remote_copy(
                src_ref=rows(*block) if src is None else src, dst_ref=rows(*block),
                send_sem=send_sems.at[k], recv_sem=recv_sems.at[k], device_id=to, device_id_type=MESH)

        mine = pltpu.make_async_copy(x_ref, rows(*me), local_sem)
        mine.start()
        first = [copy(0, me, sibling, src=x_ref)]
        first += [copy(1 + j, me, (*chip, c), src=x_ref) for j, chip in enumerate(chips)]
        for cp in first:
            cp.start()
        passed = [copy(4 + j, (*chip, c), sibling) for j, chip in enumerate(chips)]
        for j, chip in enumerate(chips):
            copy(1 + j, (*chip, c), me).wait_recv()
            passed[j].start()
        copy(0, sibling, me).wait_recv()
        for j, chip in enumerate(chips):
            copy(4 + j, (*chip, 1 - c), me).wait_recv()
        for cp in first + passed:
            cp.wait_send()
        mine.wait()
        total = all_ref[0]
        for d in range(1, NDEV):
            total = total + all_ref[d]
        sum_ref[...] = total

    return pl.pallas_call(
        body, name="small_all_reduce",
        out_shape=[jax.ShapeDtypeStruct((NDEV, m_per, 128), F32), jax.ShapeDtypeStruct((m_per, 128), F32)],
        in_specs=[pl.BlockSpec(memory_space=pltpu.VMEM)],
        out_specs=[pl.BlockSpec(memory_space=pltpu.VMEM), pl.BlockSpec(memory_space=pltpu.VMEM)],
        scratch_shapes=[pltpu.SemaphoreType.DMA((7,)), pltpu.SemaphoreType.DMA((7,)), pltpu.SemaphoreType.DMA],
    )(vec)[1]


def _adamw_math(w, g, m, v):
    m = ADAM_B1 * m + (1.0 - ADAM_B1) * g
    v = ADAM_B2 * v + (1.0 - ADAM_B2) * (g * g)
    m_hat = m / (1.0 - ADAM_B1 ** ADAM_STEP)
    v_hat = v / (1.0 - ADAM_B2 ** ADAM_STEP)
    delta = -ADAM_LR * (m_hat / (jnp.sqrt(v_hat) + ADAM_EPS) + ADAM_WD * w)
    return delta, m, v


def _adamw(w, own, r2, m, v, name):
    r, c = w.shape
    tr = min(r, 256)

    def body(w_ref, o_ref, r_ref, m_ref, v_ref, g_ref, d_ref, nm_ref, nv_ref):
        g = ((o_ref[...] + r_ref[0].astype(F32)) + r_ref[1].astype(F32)) + r_ref[2].astype(F32)
        g_ref[...] = g
        d_ref[...], nm_ref[...], nv_ref[...] = _adamw_math(w_ref[...], g, m_ref[...], v_ref[...])

    spec = pl.BlockSpec((tr, c), lambda i: (i, 0))
    return pl.pallas_call(
        body, name=name, grid=(r // tr,),
        in_specs=[spec, spec, pl.BlockSpec((3, tr, c), lambda i: (0, i, 0)), spec, spec], out_specs=[spec] * 4,
        out_shape=[jax.ShapeDtypeStruct((r, c), F32)] * 4,
        compiler_params=_cparams(("arbitrary",)))(w, own, r2, m, v)


def _adamw_small(ws, gs, ms, vs):
    n = len(ws)

    def body(*refs):
        ins, outs = refs[:4 * n], refs[4 * n:]
        for i in range(n):
            d, m, v = _adamw_math(ins[i][...], ins[n + i][...], ins[2 * n + i][...], ins[3 * n + i][...])
            outs[i][...], outs[n + i][...], outs[2 * n + i][...] = d, m, v

    shapes = [jax.ShapeDtypeStruct(w.shape, F32) for w in ws]
    outs = pl.pallas_call(body, name="adamw_small", out_shape=shapes * 3)(*ws, *gs, *ms, *vs)
    return outs[:n], outs[n:2 * n], outs[2 * n:]


MATS = ("w_in", "w_out", "w_xq", "w_xo", "w_xkv", "w_up", "w_down")
SMALL = ("mix_norm", "conv_w", "conv_norm", "w_af", "b_af", "w_ab", "b_ab", "gla_norm", "xa_norm", "mem_norm",
         "mlp_norm", "final_norm")
WEIGHTS = ("mix_norm", "w_in", "conv_w", "conv_norm", "w_af", "b_af", "w_ab", "b_ab", "gla_norm", "w_out", "xa_norm",
           "mem_norm", "w_xq", "w_xkv", "w_xo", "mlp_norm", "w_up", "w_down", "final_norm")
COL_SHARDED = ("w_in", "w_xkv", "w_up")
SMALL_SHARDED = {"conv_w": (3, 64), "w_af": (LR, 32), "w_ab": (LR, 32)}
SMALL_PACK_ROWS = 16


def kernel(x, mem, mix_norm, w_in, conv_w, conv_norm, w_af, b_af, w_ab, b_ab, gla_norm, w_out, xa_norm, mem_norm, w_xq, w_xkv, w_xo, mlp_norm, w_up, w_down, final_norm, loss_target, m_mix_norm, m_w_in, m_conv_w, m_conv_norm, m_w_af, m_b_af, m_w_ab, m_b_ab, m_gla_norm, m_w_out, m_xa_norm, m_mem_norm, m_w_xq, m_w_xkv, m_w_xo, m_mlp_norm, m_w_up, m_w_down, m_final_norm, v_mix_norm, v_w_in, v_conv_w, v_conv_norm, v_w_af, v_b_af, v_w_ab, v_b_ab, v_gla_norm, v_w_out, v_xa_norm, v_mem_norm, v_w_xq, v_w_xkv, v_w_xo, v_mlp_norm, v_w_up, v_w_down, v_final_norm):
    w = dict(mix_norm=mix_norm, w_in=w_in, conv_w=conv_w, conv_norm=conv_norm, w_af=w_af, b_af=b_af, w_ab=w_ab,
             b_ab=b_ab, gla_norm=gla_norm, w_out=w_out, xa_norm=xa_norm, mem_norm=mem_norm, w_xq=w_xq, w_xkv=w_xkv,
             w_xo=w_xo, mlp_norm=mlp_norm, w_up=w_up, w_down=w_down, final_norm=final_norm)
    mom = dict(mix_norm=m_mix_norm, w_in=m_w_in, conv_w=m_conv_w, conv_norm=m_conv_norm, w_af=m_w_af, b_af=m_b_af,
               w_ab=m_w_ab, b_ab=m_b_ab, gla_norm=m_gla_norm, w_out=m_w_out, xa_norm=m_xa_norm, mem_norm=m_mem_norm,
               w_xq=m_w_xq, w_xkv=m_w_xkv, w_xo=m_w_xo, mlp_norm=m_mlp_norm, w_up=m_w_up, w_down=m_w_down,
               final_norm=m_final_norm)
    var = dict(mix_norm=v_mix_norm, w_in=v_w_in, conv_w=v_conv_w, conv_norm=v_conv_norm, w_af=v_w_af, b_af=v_b_af,
               w_ab=v_w_ab, b_ab=v_b_ab, gla_norm=v_gla_norm, w_out=v_w_out, xa_norm=v_xa_norm, mem_norm=v_mem_norm,
               w_xq=v_w_xq, w_xkv=v_w_xkv, w_xo=v_w_xo, mlp_norm=v_mlp_norm, w_up=v_w_up, w_down=v_w_down,
               final_norm=v_final_norm)
    xi, yi, ci = _place()
    me = 4 * xi + 2 * yi + ci
    two_d = lambda a: a.reshape(a.shape[-2:]) if a.ndim == 3 else a.reshape(1, a.shape[-1])

    small = jnp.concatenate([w[n].reshape(-1) for n in SMALL_SHARDED])
    small = jnp.pad(small, (0, SMALL_PACK_ROWS * 128 - small.shape[0])).reshape(SMALL_PACK_ROWS, 128)
    gathered = _exchange(_gather_rider([two_d(w[n]).astype(BF16) for n in MATS] + [small]), "weights_all_gather")
    p = dict(zip(MATS, gathered))
    for n in MATS:
        if n not in COL_SHARDED:
            p[n] = p[n].reshape(-1, D)
    p["w_in"] = jnp.pad(p["w_in"].transpose(1, 0, 2).reshape(D, ZW), ((0, 0), (0, ZC - ZW)))
    small_all = gathered[-1].reshape(NDEV, -1)
    off = 0
    for n, (r, c) in SMALL_SHARDED.items():
        p[n] = small_all[:, off:off + r * c].reshape(NDEV, r, c).transpose(1, 0, 2).reshape(r, NDEV * c)
        off += r * c
    for n in SMALL:
        if n not in SMALL_SHARDED:
            p[n] = two_d(w[n])

    loss_part, grad_x, grads = _local_step(x[0], mem[0], loss_target[0], p)

    grads["w_in"] = grads["w_in"][:, :ZW].reshape(D, NDEV, WIN_SHARD).transpose(1, 0, 2)
    g4 = [grads[n].reshape((4, 2) + two_d(w[n]).shape) for n in MATS]
    from_sibling = _exchange(_sibling_rider(g4), "grads_to_sibling")
    place = jnp.stack([2 * xi + yi, ci]).astype(jnp.int32)
    pairs = [_rs_pair_sum(place, g4[a], from_sibling[a], "pair_sum_" + n) for a, n in enumerate(MATS)]
    from_chips = _exchange(_chips_rider([pb for pb, _ in pairs]), "grads_to_chips")

    order = [n for n in SMALL if n not in SMALL_SHARDED] + list(SMALL_SHARDED)
    flat = jnp.concatenate([grads[n].reshape(-1) for n in order] + [loss_part.reshape(-1)])
    n_flat = flat.shape[0]
    tot = _small_all_reduce(jnp.pad(flat, (0, SMALL_ROWS * 128 - n_flat)).reshape(SMALL_ROWS, 128)).reshape(-1)
    gsmall, off = {}, 0
    for n in order:
        size = grads[n].size
        full = tot[off:off + size].reshape(grads[n].shape)
        off += size
        if n in SMALL_SHARDED:
            r, c = SMALL_SHARDED[n]
            full = lax.dynamic_slice_in_dim(full, me * c, c, axis=1)
        gsmall[n] = full
    loss = tot[off]

    out_g, out_d, out_m, out_v = {}, {}, {}, {}
    for a, n in enumerate(MATS):
        res = _adamw(two_d(w[n]), pairs[a][1], from_chips[a], two_d(mom[n]), two_d(var[n]), "adamw_" + n)
        out_g[n], out_d[n], out_m[n], out_v[n] = [o.reshape(w[n].shape) for o in res]
    ds, nms, nvs = _adamw_small([two_d(w[n]) for n in SMALL], [gsmall[n] for n in SMALL],
                                [two_d(mom[n]) for n in SMALL], [two_d(var[n]) for n in SMALL])
    for i, n in enumerate(SMALL):
        out_g[n], out_d[n], out_m[n], out_v[n] = [a.reshape(w[n].shape) for a in (gsmall[n], ds[i], nms[i], nvs[i])]

    return (loss, grad_x[None], *[out_g[n] for n in WEIGHTS], *[out_d[n] for n in WEIGHTS],
            *[out_m[n] for n in WEIGHTS], *[out_v[n] for n in WEIGHTS])
```

```python
import functools

import jax
import jax.numpy as jnp
from jax import lax
from jax.experimental import pallas as pl
from jax.experimental.pallas import tpu as pltpu

F32 = jnp.float32
BF16 = jnp.bfloat16

D = 1024
CW = 512
GK = 256
GV = 512
NH = 4
CH = 64
LR = 16
NMEM = 256
XD = 256
FF = 4096
ZW = 3104
ZC = 3200
EPS = 1e-6
NDEV = 8

ZB_CB, ZB_CC, ZB_CU, ZB_V, ZB_G = 0, 1, 2, 4, 5
ZB_Q, ZB_K = 6, 7
ZB_LR = 24

TM = 512
TM_MLP = 1024
TF = 512
TB = 256
TT = 512
VMEM_LIMIT = 56 * 1024 * 1024

ADAM_LR, ADAM_B1, ADAM_B2, ADAM_EPS, ADAM_WD, ADAM_STEP = 0.001, 0.9, 0.999, 1e-08, 0.01, 10

WIN_SHARD, XKV_SHARD, UP_SHARD = ZW // NDEV, 2 * D // NDEV, FF // NDEV
SMALL_ROWS = 128

MESH = pl.DeviceIdType.MESH


def _cparams(sem):
    return pltpu.CompilerParams(dimension_semantics=sem, vmem_limit_bytes=VMEM_LIMIT)


def _call(body, name, grid, in_specs, out_specs, out_shape, scratch, args, riders=()):
    n_in, n_out, n_scr = len(in_specs), len(out_specs), len(scratch)
    counts = [(len(r.arrays), len(r.out_shape), len(r.scratch)) for r in riders]

    def take(refs, pos, sizes):
        groups = []
        for size in sizes:
            groups.append(refs[pos:pos + size])
            pos += size
        return groups, pos

    def wrapped(*refs):
        ins, pos = refs[:n_in], n_in
        r_ins, pos = take(refs, pos, [c[0] for c in counts])
        outs, pos = refs[pos:pos + n_out], pos + n_out
        r_outs, pos = take(refs, pos, [c[1] for c in counts])
        scr, pos = refs[pos:pos + n_scr], pos + n_scr
        r_scr, pos = take(refs, pos, [c[2] for c in counts])
        ids = [pl.program_id(d) for d in range(len(grid))]
        first = functools.reduce(lambda a, b: a & b, [i == 0 for i in ids])
        last = functools.reduce(lambda a, b: a & b, [i == g - 1 for i, g in zip(ids, grid)])

        @pl.when(first)
        def _():
            for r, a, b, c in zip(riders, r_ins, r_outs, r_scr):
                r.start(a, b, c)

        body(*ins, *outs, *scr)

        @pl.when(last)
        def _():
            for r, a, b, c in zip(riders, r_ins, r_outs, r_scr):
                r.finish(a, b, c)

    hbm = pl.BlockSpec(memory_space=pltpu.HBM)
    r_args = [a for r in riders for a in r.arrays]
    r_shapes = [s for r in riders for s in r.out_shape]
    return pl.pallas_call(
        wrapped if riders else body, name=name, grid=grid, in_specs=list(in_specs) + [hbm] * len(r_args),
        out_specs=list(out_specs) + [hbm] * len(r_shapes), out_shape=list(out_shape) + r_shapes,
        scratch_shapes=list(scratch) + [s for r in riders for s in r.scratch],
        compiler_params=_cparams(("arbitrary",) * len(grid)))(*args, *r_args)


def _dot(a, b):
    return jnp.dot(a.astype(BF16), b.astype(BF16), preferred_element_type=F32)


def _dot_nt(a, b):
    return lax.dot_general(a.astype(BF16), b.astype(BF16), (((1,), (1,)), ((), ())), preferred_element_type=F32)


def _dot_tn(a, b):
    return lax.dot_general(a.astype(BF16), b.astype(BF16), (((0,), (0,)), ((), ())), preferred_element_type=F32)


def _split(x, n):
    parts = []
    for _ in range(n):
        p = x.astype(BF16)
        parts.append(p)
        x = x - p.astype(F32)
    return parts


def _dot_exact_lhs(m, x, n):
    return functools.reduce(lambda a, b: a + b, [jnp.dot(m, p, preferred_element_type=F32) for p in _split(x, n)])


def _dot_exact_rhs(x, m, n):
    return functools.reduce(lambda a, b: a + b, [jnp.dot(p, m, preferred_element_type=F32) for p in _split(x, n)])


def _rms(x, g):
    r = lax.rsqrt(jnp.mean(x * x, axis=-1, keepdims=True) + EPS)
    return x * r * g, r


def _rms_bwd(x, r, g, dy):
    xr = x * r
    u = dy * g
    dx = r * (u - xr * jnp.mean(u * xr, axis=-1, keepdims=True))
    return dx, jnp.sum(dy * xr, axis=0, keepdims=True)


def _iota(shape, dim):
    return lax.broadcasted_iota(jnp.int32, shape, dim)


def _sigmoid(x):
    return 1.0 / (1.0 + jnp.exp(-x))


def _acc_rows(ref, row):
    ref[...] += jnp.broadcast_to(row, ref.shape)


def _inproj(x, g, w, riders=()):
    t = x.shape[0]
    tm = min(TM, t)

    def body(x_ref, g_ref, w_ref, z_ref, h_ref):
        h, _ = _rms(x_ref[...], g_ref[...])
        hb = h.astype(BF16)
        h_ref[...] = hb
        z_ref[...] = jnp.dot(hb, w_ref[...], preferred_element_type=F32)

    return _call(
        body, "inproj", (t // tm,),
        [pl.BlockSpec((tm, D), lambda i: (i, 0)), pl.BlockSpec((1, D), lambda i: (0, 0)),
         pl.BlockSpec((D, ZC), lambda i: (0, 0))],
        [pl.BlockSpec((tm, ZC), lambda i: (i, 0)), pl.BlockSpec((tm, D), lambda i: (i, 0))],
        [jax.ShapeDtypeStruct((t, ZC), F32), jax.ShapeDtypeStruct((t, D), BF16)], [], (x, g, w), riders)


def _kv_proj(mem, g, w):
    def body(m_ref, g_ref, w_ref, kv_ref, mn_ref):
        mn, _ = _rms(m_ref[...], g_ref[...])
        mb = mn.astype(BF16)
        mn_ref[...] = mb
        for j in range(NDEV):
            kv_ref[:, j * XKV_SHARD:(j + 1) * XKV_SHARD] = jnp.dot(mb, w_ref[j], preferred_element_type=F32)

    return pl.pallas_call(
        body, name="kv_proj",
        out_shape=[jax.ShapeDtypeStruct((NMEM, 2 * D), F32), jax.ShapeDtypeStruct((NMEM, D), BF16)],
        compiler_params=pltpu.CompilerParams(vmem_limit_bytes=VMEM_LIMIT))(mem, g, w)


def _softmax_head(qb, kb):
    s = _dot_nt(qb, kb) * (1.0 / 16.0)
    e = jnp.exp(s - jnp.max(s, axis=-1, keepdims=True))
    return e / jnp.sum(e, axis=-1, keepdims=True)


def _attn_fwd(x, yb, w_out, g, w_xq, kb, vb, w_xo):
    t = x.shape[0]
    tm = min(TM, t)

    def body(x_ref, y_ref, wo_ref, g_ref, wq_ref, k_ref, v_ref, wx_ref, x1_ref, x2_ref, xn_ref, q_ref, a_ref):
        x1 = x_ref[...] + jnp.dot(y_ref[...], wo_ref[...], preferred_element_type=F32)
        x1_ref[...] = x1
        xn, _ = _rms(x1, g_ref[...])
        xb = xn.astype(BF16)
        xn_ref[...] = xb
        qb = jnp.dot(xb, wq_ref[...], preferred_element_type=F32).astype(BF16)
        q_ref[...] = qb
        for h in range(NH):
            hs = slice(h * XD, (h + 1) * XD)
            p = _softmax_head(qb[:, hs], k_ref[:, hs])
            a_ref[:, hs] = _dot(p, v_ref[:, hs]).astype(BF16)
        x2_ref[...] = x1 + jnp.dot(a_ref[...], wx_ref[...], preferred_element_type=F32)

    tok = lambda i: (i, 0)
    full = lambda i: (0, 0)
    return pl.pallas_call(
        body, name="attn_fwd", grid=(t // tm,),
        in_specs=[pl.BlockSpec((tm, D), tok), pl.BlockSpec((tm, D), tok), pl.BlockSpec((D, D), full),
                  pl.BlockSpec((1, D), full), pl.BlockSpec((D, D), full), pl.BlockSpec((NMEM, D), full),
                  pl.BlockSpec((NMEM, D), full), pl.BlockSpec((D, D), full)],
        out_specs=[pl.BlockSpec((tm, D), tok)] * 5,
        out_shape=[jax.ShapeDtypeStruct((t, D), F32), jax.ShapeDtypeStruct((t, D), F32),
                   jax.ShapeDtypeStruct((t, D), BF16), jax.ShapeDtypeStruct((t, D), BF16),
                   jax.ShapeDtypeStruct((t, D), BF16)],
        compiler_params=_cparams(("arbitrary",)))(x, yb, w_out, g, w_xq, kb, vb, w_xo)


def _mlp_fwd(x2, g, w_up, w_down, fg, target):
    t = x2.shape[0]
    tm = min(TM_MLP, t)
    nj = FF // TF

    def body(x_ref, g_ref, wu_ref, wd_ref, fg_ref, t_ref, h1_ref, xn_ref, dx_ref, dxb_ref, loss_ref, dfg_ref,
             acc, xnb):
        i, j = pl.program_id(0), pl.program_id(1)

        @pl.when(j == 0)
        def _():
            xn, _ = _rms(x_ref[...], g_ref[...])
            xnb[...] = xn.astype(BF16)
            xn_ref[...] = xnb[...]
            acc[...] = jnp.zeros_like(acc)

        @pl.when((i == 0) & (j == 0))
        def _():
            loss_ref[...] = jnp.zeros_like(loss_ref)
            dfg_ref[...] = jnp.zeros_like(dfg_ref)

        h1 = jnp.dot(xnb[...], wu_ref[...], preferred_element_type=F32)
        h1_ref[...] = h1
        hr = jnp.maximum(h1, 0.0)
        acc[...] += _dot(hr * hr, wd_ref[...])

        @pl.when(j == nj - 1)
        def _():
            x3 = x_ref[...] + acc[...]
            y, r = _rms(x3, fg_ref[...])
            e = y - t_ref[...]
            row = jnp.mean(e * e, axis=-1, keepdims=True)
            _acc_rows(loss_ref, 0.5 * jnp.sum(row, axis=0, keepdims=True))
            dx, dfg = _rms_bwd(x3, r, fg_ref[...], e * (1.0 / D))
            dx_ref[...] = dx
            dxb_ref[...] = dx.astype(BF16)
            _acc_rows(dfg_ref, dfg)

    tok = lambda i, j: (i, 0)
    full = lambda i, j: (0, 0)
    return pl.pallas_call(
        body, name="mlp_fwd", grid=(t // tm, nj),
        in_specs=[pl.BlockSpec((tm, D), tok), pl.BlockSpec((1, D), full),
                  pl.BlockSpec((None, D, TF), lambda i, j: (j, 0, 0)),
                  pl.BlockSpec((TF, D), lambda i, j: (j, 0)), pl.BlockSpec((1, D), full), pl.BlockSpec((tm, D), tok)],
        out_specs=[pl.BlockSpec((tm, TF), lambda i, j: (i, j)), pl.BlockSpec((tm, D), tok), pl.BlockSpec((tm, D), tok),
                   pl.BlockSpec((tm, D), tok), pl.BlockSpec((8, 128), full), pl.BlockSpec((8, D), full)],
        out_shape=[jax.ShapeDtypeStruct((t, FF), F32), jax.ShapeDtypeStruct((t, D), BF16),
                   jax.ShapeDtypeStruct((t, D), F32), jax.ShapeDtypeStruct((t, D), BF16),
                   jax.ShapeDtypeStruct((8, 128), F32), jax.ShapeDtypeStruct((8, D), F32)],
        scratch_shapes=[pltpu.VMEM((tm, D), F32), pltpu.VMEM((tm, D), BF16)],
        compiler_params=_cparams(("arbitrary", "arbitrary")))(x2, g, w_up, w_down, fg, target)


def _mlp_bwd(dx3, dx3b, h1, w_down, w_up, x2, g):
    t = x2.shape[0]
    tm = min(TM_MLP, t)
    nj = FF // TF

    def body(dx_ref, dxb_ref, h1_ref, wd_ref, wu_ref, x_ref, g_ref, a_ref, dh_ref, dx2_ref, dx2b_ref, dg_ref, acc):
        i, j = pl.program_id(0), pl.program_id(1)

        @pl.when(j == 0)
        def _():
            acc[...] = jnp.zeros_like(acc)

        @pl.when((i == 0) & (j == 0))
        def _():
            dg_ref[...] = jnp.zeros_like(dg_ref)

        hr = jnp.maximum(h1_ref[...], 0.0)
        da = _dot_nt(dxb_ref[...], wd_ref[...])
        dh = (da * 2.0 * hr).astype(BF16)
        a_ref[...] = (hr * hr).astype(BF16)
        dh_ref[...] = dh
        acc[...] += _dot_nt(dh, wu_ref[...])

        @pl.when(j == nj - 1)
        def _():
            x = x_ref[...]
            r = lax.rsqrt(jnp.mean(x * x, axis=-1, keepdims=True) + EPS)
            dx, dg = _rms_bwd(x, r, g_ref[...], acc[...])
            dx2 = dx_ref[...] + dx
            dx2_ref[...] = dx2
            dx2b_ref[...] = dx2.astype(BF16)
            _acc_rows(dg_ref, dg)

    tok = lambda i, j: (i, 0)
    full = lambda i, j: (0, 0)
    hid = lambda i, j: (i, j)
    return pl.pallas_call(
        body, name="mlp_bwd", grid=(t // tm, nj),
        in_specs=[pl.BlockSpec((tm, D), tok), pl.BlockSpec((tm, D), tok), pl.BlockSpec((tm, TF), hid),
                  pl.BlockSpec((TF, D), lambda i, j: (j, 0)), pl.BlockSpec((None, D, TF), lambda i, j: (j, 0, 0)),
                  pl.BlockSpec((tm, D), tok), pl.BlockSpec((1, D), full)],
        out_specs=[pl.BlockSpec((tm, TF), hid), pl.BlockSpec((tm, TF), hid), pl.BlockSpec((tm, D), tok),
                   pl.BlockSpec((tm, D), tok), pl.BlockSpec((8, D), full)],
        out_shape=[jax.ShapeDtypeStruct((t, FF), BF16), jax.ShapeDtypeStruct((t, FF), BF16),
                   jax.ShapeDtypeStruct((t, D), F32), jax.ShapeDtypeStruct((t, D), BF16),
                   jax.ShapeDtypeStruct((8, D), F32)],
        scratch_shapes=[pltpu.VMEM((tm, D), F32)],
        compiler_params=_cparams(("arbitrary", "arbitrary")))(dx3, dx3b, h1, w_down, w_up, x2, g)


def _attn_bwd(x1, dx2, dx2b, qb, kb, vb, w_xo, w_xq, w_out, g):
    t = x1.shape[0]
    tm = min(TM, t)

    def body(x_ref, dx2_ref, dx2b_ref, q_ref, k_ref, v_ref, wx_ref, wq_ref, wo_ref, g_ref,
             dx1_ref, dx1b_ref, dy_ref, dq_ref, dkv_ref, dg_ref):
        @pl.when(pl.program_id(0) == 0)
        def _():
            dkv_ref[...] = jnp.zeros_like(dkv_ref)
            dg_ref[...] = jnp.zeros_like(dg_ref)

        datt = _dot_nt(dx2b_ref[...], wx_ref[...]).astype(BF16)
        for h in range(NH):
            hs = slice(h * XD, (h + 1) * XD)
            q_h, k_h, v_h, da_h = q_ref[:, hs], k_ref[:, hs], v_ref[:, hs], datt[:, hs]
            p = _softmax_head(q_h, k_h)
            dp = _dot_nt(da_h, v_h)
            ds = (p * (dp - jnp.sum(dp * p, axis=-1, keepdims=True)) * (1.0 / 16.0)).astype(BF16)
            dq_ref[:, hs] = _dot(ds, k_h).astype(BF16)
            dkv_ref[:, hs] += _dot_tn(ds, q_h)
            dkv_ref[:, D + h * XD:D + (h + 1) * XD] += _dot_tn(p, da_h)
        dxn = _dot_nt(dq_ref[...], wq_ref[...])
        x = x_ref[...]
        r = lax.rsqrt(jnp.mean(x * x, axis=-1, keepdims=True) + EPS)
        dx, dg = _rms_bwd(x, r, g_ref[...], dxn)
        dx1 = dx2_ref[...] + dx
        dx1_ref[...] = dx1
        dx1b = dx1.astype(BF16)
        dx1b_ref[...] = dx1b
        dy_ref[...] = _dot_nt(dx1b, wo_ref[...])
        _acc_rows(dg_ref, dg)

    tok = lambda i: (i, 0)
    full = lambda i: (0, 0)
    return pl.pallas_call(
        body, name="attn_bwd", grid=(t // tm,),
        in_specs=[pl.BlockSpec((tm, D), tok), pl.BlockSpec((tm, D), tok), pl.BlockSpec((tm, D), tok),
                  pl.BlockSpec((tm, D), tok), pl.BlockSpec((NMEM, D), full), pl.BlockSpec((NMEM, D), full),
                  pl.BlockSpec((D, D), full), pl.BlockSpec((D, D), full), pl.BlockSpec((D, D), full),
                  pl.BlockSpec((1, D), full)],
        out_specs=[pl.BlockSpec((tm, D), tok), pl.BlockSpec((tm, D), tok), pl.BlockSpec((tm, D), tok),
                   pl.BlockSpec((tm, D), tok), pl.BlockSpec((NMEM, 2 * D), full), pl.BlockSpec((8, D), full)],
        out_shape=[jax.ShapeDtypeStruct((t, D), F32), jax.ShapeDtypeStruct((t, D), BF16),
                   jax.ShapeDtypeStruct((t, D), F32), jax.ShapeDtypeStruct((t, D), BF16),
                   jax.ShapeDtypeStruct((NMEM, 2 * D), F32), jax.ShapeDtypeStruct((8, D), F32)],
        compiler_params=_cparams(("arbitrary",)))(x1, dx2, dx2b, qb, kb, vb, w_xo, w_xq, w_out, g)


def _kv_bwd(dkv, memn, mem, g, w):
    def body(dkv_ref, mn_ref, m_ref, g_ref, w_ref, dw_ref, dg_ref):
        dkvb = dkv_ref[...].astype(BF16)
        dmn = jnp.zeros((NMEM, D), F32)
        for j in range(NDEV):
            cols = slice(j * XKV_SHARD, (j + 1) * XKV_SHARD)
            dw_ref[j] = _dot_tn(mn_ref[...], dkvb[:, cols])
            dmn += _dot_nt(dkvb[:, cols], w_ref[j])
        m = m_ref[...]
        r = lax.rsqrt(jnp.mean(m * m, axis=-1, keepdims=True) + EPS)
        dg_ref[...] = jnp.broadcast_to(jnp.sum(dmn * m * r, axis=0, keepdims=True), dg_ref.shape)

    return pl.pallas_call(
        body, name="kv_bwd",
        out_shape=[jax.ShapeDtypeStruct((NDEV, D, XKV_SHARD), F32), jax.ShapeDtypeStruct((8, D), F32)],
        compiler_params=pltpu.CompilerParams(vmem_limit_bytes=VMEM_LIMIT))(dkv, memn, mem, g, w)


def _inproj_bwd(dz, w, x, dx1, g, riders=()):
    t = x.shape[0]
    tm = min(TM, t)

    def body(dz_ref, w_ref, x_ref, dx1_ref, g_ref, gx_ref, dg_ref):
        @pl.when(pl.program_id(0) == 0)
        def _():
            dg_ref[...] = jnp.zeros_like(dg_ref)

        dh = _dot_nt(dz_ref[...], w_ref[...])
        x = x_ref[...]
        r = lax.rsqrt(jnp.mean(x * x, axis=-1, keepdims=True) + EPS)
        dx, dg = _rms_bwd(x, r, g_ref[...], dh)
        gx_ref[...] = dx1_ref[...] + dx
        _acc_rows(dg_ref, dg)

    tok = lambda i: (i, 0)
    full = lambda i: (0, 0)
    return _call(
        body, "inproj_bwd", (t // tm,),
        [pl.BlockSpec((tm, ZC), tok), pl.BlockSpec((D, ZC), full), pl.BlockSpec((tm, D), tok),
         pl.BlockSpec((tm, D), tok), pl.BlockSpec((1, D), full)],
        [pl.BlockSpec((tm, D), tok), pl.BlockSpec((8, D), full)],
        [jax.ShapeDtypeStruct((t, D), F32), jax.ShapeDtypeStruct((8, D), F32)], [], (dz, w, x, dx1, g), riders)


def _matmul_tn(a, b, name, col_block=None, riders=()):
    t, k = a.shape
    n = b.shape[1]
    tk = min(k, 1024)
    tn = 640 if n % 1024 else 1024
    tt = min(TT, t)
    nt = t // tt
    if col_block:
        per = tn // col_block
        out_spec = pl.BlockSpec((per, tk, col_block), lambda i, j, s: (j, i, 0))
        out_shape = jax.ShapeDtypeStruct((n // col_block, k, col_block), F32)
    else:
        out_spec = pl.BlockSpec((tk, tn), lambda i, j, s: (i, j))
        out_shape = jax.ShapeDtypeStruct((k, n), F32)

    def body(a_ref, b_ref, o_ref):
        @pl.when(pl.program_id(2) == 0)
        def _():
            o_ref[...] = jnp.zeros_like(o_ref)

        if col_block:
            for q in range(per):
                o_ref[q] += _dot_tn(a_ref[...], b_ref[:, q * col_block:(q + 1) * col_block])
        else:
            o_ref[...] += _dot_tn(a_ref[...], b_ref[...])

    return _call(
        body, name, (k // tk, n // tn, nt),
        [pl.BlockSpec((tt, tk), lambda i, j, s: (s, i)), pl.BlockSpec((tt, tn), lambda i, j, s: (s, j))],
        [out_spec], [out_shape], [], (a, b), riders)


def _lane_head(shape, dim, shift):
    return _iota(shape, dim) >> shift


def _gla_recompute(q_raw, k, lr, wpad, bias, rev, tb):
    pre = _dot(lr, wpad) + bias
    la = (jnp.minimum(pre, 0.0) - jnp.log(1.0 + jnp.exp(-jnp.abs(pre)))) * (1.0 / 16.0)
    r, c = _iota((tb, tb), 0), _iota((tb, tb), 1)
    tri = (c >= r) if rev else (c <= r)
    cum = jnp.where(((r >> 6) == (c >> 6)) & tri, 1.0, 0.0).astype(BF16)
    b = _dot_exact_lhs(cum, la, 3)
    e, ei = jnp.exp(b), jnp.exp(-b)
    qt = (q_raw * 0.125) * e
    kt = k * ei
    return pre, b, e, ei, qt, kt


def _stack_heads(x, shift):
    head = _lane_head(x.shape, 1, shift)
    return jnp.concatenate([jnp.where(head == h, x, 0.0) for h in range(NH)], axis=0).astype(BF16)


def _fold_heads(x, shift):
    head = _lane_head((CH, x.shape[1]), 1, shift)
    return functools.reduce(lambda a, b: a + b,
                            [jnp.where(head == h, x[h * CH:(h + 1) * CH], 0.0) for h in range(NH)])


def _wide_mask(rev):
    r, s = _iota((CH, NH * CH), 0), _iota((CH, NH * CH), 1) & (CH - 1)
    return (s >= r) if rev else (s <= r)


def _state_mask():
    return (_iota((GV, GK), 0) >> 7) == (_iota((GV, GK), 1) >> 6)


def _state_expand(sd):
    head = _lane_head(sd.shape, 1, 6)
    return jnp.concatenate([jnp.where(head == h, sd, 0.0) for h in range(NH)], axis=0)


def _conv_parts(cb, cc, cu, ccp, cup, ccn, cun, cw_ref, first, last, tb):
    h = cc * cu
    hp = jnp.where(first, 0.0, ccp * cup)
    hn = jnp.where(last, 0.0, ccn * cun)
    rows = _iota(h.shape, 0)
    h_m1 = jnp.where(rows == 0, hp, pltpu.roll(h, 1, 0))
    h_p1 = jnp.where(rows == tb - 1, hn, pltpu.roll(h, tb - 1, 0))
    conv = cw_ref[pl.ds(0, 1), :] * h_m1 + cw_ref[pl.ds(1, 1), :] * h + cw_ref[pl.ds(2, 1), :] * h_p1
    return h, h_m1, h_p1, conv


def _group_ones():
    return jnp.where((_iota((CW, CW), 0) >> 6) == (_iota((CW, CW), 1) >> 6), 1.0, 0.0).astype(BF16)


def _head_norm(o):
    ons, rs = [], []
    for h in range(NH):
        slab = o[:, h * 128:(h + 1) * 128]
        r = lax.rsqrt(jnp.mean(slab * slab, axis=-1, keepdims=True) + EPS)
        ons.append(slab * r)
        rs.append(jnp.broadcast_to(r, slab.shape))
    return jnp.concatenate(ons, axis=1), jnp.concatenate(rs, axis=1)


def _zspec(tb, width, blk, jmap):
    return pl.BlockSpec((tb, width), lambda i: (jmap(i), blk))


def _halo_specs(tb, nblk, t, blk, jmap):
    prev = pl.BlockSpec((8, CW), lambda i: (jnp.maximum(jmap(i) * (tb // 8) - 1, 0), blk))
    nxt = pl.BlockSpec((8, CW), lambda i: (jnp.minimum((jmap(i) + 1) * (tb // 8), t // 8 - 1), blk))
    return prev, nxt


def _gla_fwd_sweep(z, wpad, bias, rev, finish_args=None, riders=()):
    t = z.shape[0]
    tb = min(TB, t)
    nblk, nb = t // tb, tb // CH
    jmap = (lambda i: nblk - 1 - i) if rev else (lambda i: i)
    finish = finish_args is not None

    def body(*refs):
        if finish:
            (q_ref, k_ref, v_ref, lr_ref, w_ref, bias_ref, of_ref, g_ref, cb_ref, cc_ref, cu_ref, ccp_ref, ccn_ref,
             cup_ref, cun_ref, cw_ref, cn_ref, gn_ref, y_ref, opre_ref, sd_ref, st, b_scr, o_scr) = refs
        else:
            q_ref, k_ref, v_ref, lr_ref, w_ref, bias_ref, o_ref, sd_ref, st, b_scr = refs
            o_scr = o_ref
        i = pl.program_id(0)

        @pl.when(i == 0)
        def _():
            st[...] = jnp.zeros_like(st)

        q_raw, k, v = q_ref[...], k_ref[...], v_ref[...]
        _, b, _, _, qt, kt = _gla_recompute(q_raw, k, lr_ref[...], w_ref[...], bias_ref[...], rev, tb)
        b_scr[...] = b
        maskw, bd = _wide_mask(rev), _state_mask()
        for c in (reversed(range(nb)) if rev else range(nb)):
            sl = slice(c * CH, (c + 1) * CH)
            gdec = jnp.exp(b_scr[pl.ds(c * CH + (0 if rev else CH - 1), 1), :])
            qt_c, kt_c, v_c = qt[sl], kt[sl], v[sl]
            s_in = st[...]
            sd_ref[c] = s_in[0:128] + s_in[128:256] + s_in[256:384] + s_in[384:512]
            a = jnp.where(maskw, _dot_nt(qt_c, _stack_heads(kt_c, 6)), 0.0)
            o_scr[pl.ds(c * CH, CH), :] = _dot(a, _stack_heads(v_c, 7)) + _dot_nt(qt_c, s_in)
            st[...] = s_in * gdec + jnp.where(bd, _dot_tn(v_c, kt_c * gdec), 0.0)

        if finish:
            j = jmap(i)
            hsel = jnp.where((_iota((GK, GV), 0) >> 6) == (_iota((GK, GV), 1) >> 7), 1.0, 0.0).astype(BF16)
            sb = _dot_exact_rhs((q_raw * 0.125) * k, hsel, 2)
            o_pre = of_ref[...] + o_scr[...] - sb * v
            opre_ref[...] = o_pre
            on, _ = _head_norm(o_pre)
            g = g_ref[...]
            y_ref[:, CW:] = (on * gn_ref[...] * (g * _sigmoid(g))).astype(BF16)
            cb = cb_ref[...]
            _, _, _, conv = _conv_parts(cb, cc_ref[...], cu_ref[...], ccp_ref[pl.ds(7, 1), :], cup_ref[pl.ds(7, 1), :],
                                        ccn_ref[pl.ds(0, 1), :], cun_ref[pl.ds(0, 1), :], cw_ref, j == 0,
                                        j == nblk - 1, tb)
            yc = cb * conv
            gm = _dot_exact_rhs(yc * yc, _group_ones(), 2) * (1.0 / 64.0)
            y_ref[:, :CW] = (yc * lax.rsqrt(gm + EPS) * cn_ref[...]).astype(BF16)

    full = lambda i: (0, 0)
    in_specs = [_zspec(tb, GK, ZB_Q, jmap), _zspec(tb, GK, ZB_K, jmap), _zspec(tb, GV, ZB_V, jmap),
                _zspec(tb, 128, ZB_LR, jmap), pl.BlockSpec((128, GK), full), pl.BlockSpec((1, GK), full)]
    args = [z, z, z, z, wpad, bias]
    sd_spec = pl.BlockSpec((nb, 128, GK), lambda i: (jmap(i), 0, 0))
    sd_shape = jax.ShapeDtypeStruct((t // CH, 128, GK), F32)
    scratch = [pltpu.VMEM((GV, GK), F32), pltpu.VMEM((tb, GK), F32)]
    if finish:
        o_f, conv_w, conv_norm, gla_norm4 = finish_args
        ccp, ccn = _halo_specs(tb, nblk, t, ZB_CC, jmap)
        cup, cun = _halo_specs(tb, nblk, t, ZB_CU, jmap)
        in_specs += [pl.BlockSpec((tb, GV), lambda i: (jmap(i), 0)), _zspec(tb, GV, ZB_G, jmap),
                     _zspec(tb, CW, ZB_CB, jmap), _zspec(tb, CW, ZB_CC, jmap), _zspec(tb, CW, ZB_CU, jmap),
                     ccp, ccn, cup, cun, pl.BlockSpec((3, CW), full), pl.BlockSpec((1, CW), full),
                     pl.BlockSpec((1, GV), full)]
        args += [o_f, z, z, z, z, z, z, z, z, conv_w, conv_norm, gla_norm4]
        out_specs = [pl.BlockSpec((tb, D), lambda i: (jmap(i), 0)), pl.BlockSpec((tb, GV), lambda i: (jmap(i), 0)),
                     sd_spec]
        out_shape = [jax.ShapeDtypeStruct((t, D), BF16), jax.ShapeDtypeStruct((t, GV), F32), sd_shape]
        scratch.append(pltpu.VMEM((tb, GV), F32))
    else:
        out_specs = [pl.BlockSpec((tb, GV), lambda i: (jmap(i), 0)), sd_spec]
        out_shape = [jax.ShapeDtypeStruct((t, GV), F32), sd_shape]
    return _call(body, "gla_fwd_rev" if rev else "gla_fwd", (nblk,), in_specs, out_specs, out_shape, scratch, args,
                 riders)


def _gla_bwd_chunks(do_ref, sd_ref, dst, b_scr, db_scr, dq_ref, dk_ref, dv_ref, qt, kt, e, ei, v, rev, nb):
    maskw, bd = _wide_mask(rev), _state_mask()
    for c in (range(nb) if rev else reversed(range(nb))):
        sl = slice(c * CH, (c + 1) * CH)
        grow = c * CH + (0 if rev else CH - 1)
        gdec = jnp.exp(b_scr[pl.ds(grow, 1), :])
        qt_c, kt_c, v_c, do_c = qt[sl], kt[sl], v[sl], do_ref[pl.ds(c * CH, CH), :]
        s_in = _state_expand(sd_ref[c])
        ds_out = dst[...]
        kbd, vbd = _stack_heads(kt_c, 6), _stack_heads(v_c, 7)
        a = jnp.where(maskw, _dot_nt(qt_c, kbd), 0.0)
        da = jnp.where(maskw, _dot_nt(do_c, vbd), 0.0)
        kh = kt_c * gdec
        dv_ref[pl.ds(c * CH, CH), :] = _fold_heads(_dot_tn(a, do_c), 7) + _dot_nt(kh, ds_out)
        dqt = _dot(da, kbd) + _dot(do_c, s_in)
        dkh = _dot(v_c, ds_out)
        dkt = _fold_heads(_dot_tn(da, qt_c), 6) + dkh * gdec
        dg = jnp.sum(ds_out * s_in, axis=0, keepdims=True) + jnp.sum(kt_c * dkh, axis=0, keepdims=True)
        db_scr[pl.ds(c * CH, CH), :] = dqt * qt_c - dkt * kt_c
        db_scr[pl.ds(grow, 1), :] += dg * gdec
        dq_ref[pl.ds(c * CH, CH), :] = dqt * e[sl] * 0.125
        dk_ref[pl.ds(c * CH, CH), :] = dkt * ei[sl]
        dst[...] = ds_out * gdec + jnp.where(bd, _dot_tn(do_c, qt_c), 0.0)


def _gate_bwd(db, pre, lr, wpad, rev, tb):
    r, c = _iota((tb, tb), 0), _iota((tb, tb), 1)
    tri = (c <= r) if rev else (c >= r)
    cum_t = jnp.where(((r >> 6) == (c >> 6)) & tri, 1.0, 0.0).astype(BF16)
    dla = _dot_exact_lhs(cum_t, db, 2)
    dpre = dla * (1.0 / 16.0) / (1.0 + jnp.exp(pre))
    return dpre, _dot_nt(dpre, wpad), _dot_tn(lr, dpre)


def _gla_bwd_first(z, dy, o_pre, sd, wpad, bias, conv_w, conv_norm, gla_norm4, riders=()):
    t = z.shape[0]
    tb = min(TB, t)
    nblk, nb = t // tb, tb // CH
    jmap = lambda i: nblk - 1 - i

    def body(q_ref, k_ref, v_ref, lr_ref, g_ref, cb_ref, cc_ref, cu_ref, ccp_ref, ccn_ref, cup_ref, cun_ref,
             dy_ref, opre_ref, sd_ref, w_ref, bias_ref, cw_ref, cn_ref, gn_ref,
             do_ref, dq_ref, dk_ref, dv_ref, dlr_ref, dzg_ref, dzcb_ref, dconv_ref,
             dw_ref, dbias_ref, dcw_ref, dcn_ref, dgn_ref, dst, b_scr, db_scr):
        i = pl.program_id(0)
        j = jmap(i)

        @pl.when(i == 0)
        def _():
            dst[...] = jnp.zeros_like(dst)
            for ref in (dw_ref, dbias_ref, dcw_ref, dcn_ref, dgn_ref):
                ref[...] = jnp.zeros_like(ref)

        dyg = dy_ref[:, CW:]
        g = g_ref[...]
        sig = _sigmoid(g)
        on, rr = _head_norm(opre_ref[...])
        gn = gn_ref[...]
        dzg_ref[...] = (dyg * on * gn * (sig * (1.0 + g * (1.0 - sig)))).astype(BF16)
        don = dyg * (g * sig)
        _acc_rows(dgn_ref, jnp.sum(don * on, axis=0, keepdims=True))
        u = don * gn
        uo = u * on
        mean_uo = jnp.concatenate(
            [jnp.broadcast_to(jnp.mean(uo[:, h * 128:(h + 1) * 128], axis=-1, keepdims=True), (tb, 128))
             for h in range(NH)], axis=1)
        do_ref[...] = rr * (u - on * mean_uo)

        cb = cb_ref[...]
        h, h_m1, h_p1, conv = _conv_parts(cb, cc_ref[...], cu_ref[...], ccp_ref[pl.ds(7, 1), :],
                                          cup_ref[pl.ds(7, 1), :], ccn_ref[pl.ds(0, 1), :], cun_ref[pl.ds(0, 1), :],
                                          cw_ref, j == 0, j == nblk - 1, tb)
        yc = cb * conv
        ones = _group_ones()
        rc = lax.rsqrt(_dot_exact_rhs(yc * yc, ones, 2) * (1.0 / 64.0) + EPS)
        ycr = yc * rc
        dyn = dy_ref[:, :CW]
        _acc_rows(dcn_ref, jnp.sum(dyn * ycr, axis=0, keepdims=True))
        uc = dyn * cn_ref[...]
        dyc = rc * (uc - ycr * (_dot_exact_rhs(uc * ycr, ones, 2) * (1.0 / 64.0)))
        dzcb_ref[...] = (dyc * conv).astype(BF16)
        dconv = dyc * cb
        dconv_ref[...] = dconv
        dcw_ref[pl.ds(0, 1), :] += jnp.sum(dconv * h_m1, axis=0, keepdims=True)
        dcw_ref[pl.ds(1, 1), :] += jnp.sum(dconv * h, axis=0, keepdims=True)
        dcw_ref[pl.ds(2, 1), :] += jnp.sum(dconv * h_p1, axis=0, keepdims=True)

        lr, wp = lr_ref[...], w_ref[...]
        pre, b, e, ei, qt, kt = _gla_recompute(q_ref[...], k_ref[...], lr, wp, bias_ref[...], False, tb)
        b_scr[...] = b
        _gla_bwd_chunks(do_ref, sd_ref, dst, b_scr, db_scr, dq_ref, dk_ref, dv_ref, qt, kt, e, ei, v_ref[...],
                        False, nb)
        dpre, dlr, dw = _gate_bwd(db_scr[...], pre, lr, wp, False, tb)
        dlr_ref[...] = dlr
        dw_ref[...] += dw
        _acc_rows(dbias_ref, jnp.sum(dpre, axis=0, keepdims=True))

    full = lambda i: (0, 0)
    tokv = pl.BlockSpec((tb, GV), lambda i: (jmap(i), 0))
    tokk = pl.BlockSpec((tb, GK), lambda i: (jmap(i), 0))
    ccp, ccn = _halo_specs(tb, nblk, t, ZB_CC, jmap)
    cup, cun = _halo_specs(tb, nblk, t, ZB_CU, jmap)
    in_specs = [_zspec(tb, GK, ZB_Q, jmap), _zspec(tb, GK, ZB_K, jmap), _zspec(tb, GV, ZB_V, jmap),
                _zspec(tb, 128, ZB_LR, jmap), _zspec(tb, GV, ZB_G, jmap), _zspec(tb, CW, ZB_CB, jmap),
                _zspec(tb, CW, ZB_CC, jmap), _zspec(tb, CW, ZB_CU, jmap), ccp, ccn, cup, cun,
                pl.BlockSpec((tb, D), lambda i: (jmap(i), 0)), tokv,
                pl.BlockSpec((nb, 128, GK), lambda i: (jmap(i), 0, 0)), pl.BlockSpec((128, GK), full),
                pl.BlockSpec((1, GK), full), pl.BlockSpec((3, CW), full), pl.BlockSpec((1, CW), full),
                pl.BlockSpec((1, GV), full)]
    out_specs = [tokv, tokk, tokk, tokv, pl.BlockSpec((tb, 128), lambda i: (jmap(i), 0)), tokv, tokv, tokv,
                 pl.BlockSpec((128, GK), full), pl.BlockSpec((8, GK), full), pl.BlockSpec((8, CW), full),
                 pl.BlockSpec((8, CW), full), pl.BlockSpec((8, GV), full)]
    out_shape = [jax.ShapeDtypeStruct((t, GV), F32), jax.ShapeDtypeStruct((t, GK), F32),
                 jax.ShapeDtypeStruct((t, GK), F32), jax.ShapeDtypeStruct((t, GV), F32),
                 jax.ShapeDtypeStruct((t, 128), F32), jax.ShapeDtypeStruct((t, GV), BF16),
                 jax.ShapeDtypeStruct((t, CW), BF16), jax.ShapeDtypeStruct((t, CW), F32),
                 jax.ShapeDtypeStruct((128, GK), F32), jax.ShapeDtypeStruct((8, GK), F32),
                 jax.ShapeDtypeStruct((8, CW), F32), jax.ShapeDtypeStruct((8, CW), F32),
                 jax.ShapeDtypeStruct((8, GV), F32)]
    return _call(
        body, "gla_bwd_first", (nblk,), in_specs, out_specs, out_shape,
        [pltpu.VMEM((GV, GK), F32), pltpu.VMEM((tb, GK), F32), pltpu.VMEM((tb, GK), F32)],
        (z, z, z, z, z, z, z, z, z, z, z, z, dy, o_pre, sd, wpad, bias, conv_w, conv_norm, gla_norm4), riders)


def _gla_bwd_second(z, do, sd, wpad, bias, dqa, dka, dva, dlra, dzg, dzcb, dconv, conv_w, riders=()):
    t = z.shape[0]
    tb = min(TB, t)
    nblk, nb = t // tb, tb // CH
    jmap = lambda i: i

    def body(q_ref, k_ref, v_ref, lr_ref, cc_ref, cu_ref, do_ref, sd_ref, w_ref, bias_ref, dqa_ref, dka_ref,
             dva_ref, dlra_ref, dzg_ref, dzcb_ref, dc_ref, dcp_ref, dcn_ref, cw_ref,
             dz_ref, dw_ref, dbias_ref, dst, b_scr, db_scr, dq_scr, dk_scr, dv_scr):
        i = pl.program_id(0)

        @pl.when(i == 0)
        def _():
            dst[...] = jnp.zeros_like(dst)
            dw_ref[...] = jnp.zeros_like(dw_ref)
            dbias_ref[...] = jnp.zeros_like(dbias_ref)

        q_raw, k, v, lr, wp = q_ref[...], k_ref[...], v_ref[...], lr_ref[...], w_ref[...]
        pre, b, e, ei, qt, kt = _gla_recompute(q_raw, k, lr, wp, bias_ref[...], True, tb)
        b_scr[...] = b
        _gla_bwd_chunks(do_ref, sd_ref, dst, b_scr, db_scr, dq_scr, dk_scr, dv_scr, qt, kt, e, ei, v, True, nb)
        dpre, dlr, dw = _gate_bwd(db_scr[...], pre, lr, wp, True, tb)
        dw_ref[...] += dw
        _acc_rows(dbias_ref, jnp.sum(dpre, axis=0, keepdims=True))

        do = do_ref[...]
        qs = q_raw * 0.125
        hsel = jnp.where((_iota((GK, GV), 0) >> 6) == (_iota((GK, GV), 1) >> 7), 1.0, 0.0).astype(BF16)
        hsel_t = jnp.where((_iota((GV, GK), 0) >> 7) == (_iota((GV, GK), 1) >> 6), 1.0, 0.0).astype(BF16)
        sb = _dot_exact_rhs(qs * k, hsel, 2)
        dsk = _dot_exact_rhs(do * v, hsel_t, 2)
        dz_ref[:, 1536:1792] = (dqa_ref[...] + dq_scr[...] - dsk * k * 0.125).astype(BF16)
        dz_ref[:, 1792:2048] = (dka_ref[...] + dk_scr[...] - dsk * qs).astype(BF16)
        dz_ref[:, 2048:2560] = (dva_ref[...] + dv_scr[...] - sb * do).astype(BF16)
        dz_ref[:, 2560:3072] = dzg_ref[...]
        dz_ref[:, 3072:3200] = (dlra_ref[...] + dlr).astype(BF16)

        dc = dc_ref[...]
        rows = _iota(dc.shape, 0)
        dprev = jnp.where(i == 0, 0.0, dcp_ref[pl.ds(7, 1), :])
        dnext = jnp.where(i == nblk - 1, 0.0, dcn_ref[pl.ds(0, 1), :])
        dc_m1 = jnp.where(rows == 0, dprev, pltpu.roll(dc, 1, 0))
        dc_p1 = jnp.where(rows == tb - 1, dnext, pltpu.roll(dc, tb - 1, 0))
        dh = cw_ref[pl.ds(0, 1), :] * dc_p1 + cw_ref[pl.ds(1, 1), :] * dc + cw_ref[pl.ds(2, 1), :] * dc_m1
        dz_ref[:, 0:512] = dzcb_ref[...]
        dz_ref[:, 512:1024] = (dh * cu_ref[...]).astype(BF16)
        dz_ref[:, 1024:1536] = (dh * cc_ref[...]).astype(BF16)

    full = lambda i: (0, 0)
    tokv = pl.BlockSpec((tb, GV), lambda i: (i, 0))
    tokk = pl.BlockSpec((tb, GK), lambda i: (i, 0))
    dcp = pl.BlockSpec((8, CW), lambda i: (jnp.maximum(i * (tb // 8) - 1, 0), 0))
    dcn = pl.BlockSpec((8, CW), lambda i: (jnp.minimum((i + 1) * (tb // 8), t // 8 - 1), 0))
    in_specs = [_zspec(tb, GK, ZB_Q, jmap), _zspec(tb, GK, ZB_K, jmap), _zspec(tb, GV, ZB_V, jmap),
                _zspec(tb, 128, ZB_LR, jmap), _zspec(tb, CW, ZB_CC, jmap), _zspec(tb, CW, ZB_CU, jmap), tokv,
                pl.BlockSpec((nb, 128, GK), lambda i: (i, 0, 0)), pl.BlockSpec((128, GK), full),
                pl.BlockSpec((1, GK), full), tokk, tokk, tokv, pl.BlockSpec((tb, 128), lambda i: (i, 0)), tokv, tokv,
                tokv, dcp, dcn, pl.BlockSpec((3, CW), full)]
    out_specs = [pl.BlockSpec((tb, ZC), lambda i: (i, 0)), pl.BlockSpec((128, GK), full), pl.BlockSpec((8, GK), full)]
    out_shape = [jax.ShapeDtypeStruct((t, ZC), BF16), jax.ShapeDtypeStruct((128, GK), F32),
                 jax.ShapeDtypeStruct((8, GK), F32)]
    return _call(
        body, "gla_bwd_second", (nblk,), in_specs, out_specs, out_shape,
        [pltpu.VMEM((GV, GK), F32), pltpu.VMEM((tb, GK), F32), pltpu.VMEM((tb, GK), F32),
         pltpu.VMEM((tb, GK), F32), pltpu.VMEM((tb, GK), F32), pltpu.VMEM((tb, GV), F32)],
        (z, z, z, z, z, z, do, sd, wpad, bias, dqa, dka, dva, dlra, dzg, dzcb, dconv, dconv, dconv, conv_w), riders)


def _step(x, mem, target, shard, small_pack, vec, place):
    assert TF == UP_SHARD
    own, from_chips = {}, {}

    def pair_sums(names, g4, from_sibling):
        pbs = []
        for n, g, s in zip(names, g4, from_sibling):
            pb, own[n] = _rs_pair_sum(place, g, s, "pair_sum_" + n)
            pbs.append(pb)
        return pbs

    def by_dest(g, n):
        return g.reshape((4, 2) + shard[n].shape)

    w_in, small_all = _exchange(_gather_rider([shard["w_in"], small_pack]), "gather_w_in")
    w_in = jnp.pad(w_in.transpose(1, 0, 2).reshape(D, ZW), ((0, 0), (0, ZC - ZW)))
    small_all = small_all.reshape(NDEV, -1)
    p, off = {}, 0
    for n, (r, c) in SMALL_SHARDED.items():
        p[n] = small_all[:, off:off + r * c].reshape(NDEV, r, c).transpose(1, 0, 2).reshape(r, NDEV * c)
        off += r * c
    zeros_lr = jnp.zeros((128 - LR, GK), BF16)
    waf_pad = jnp.concatenate([p["w_af"].astype(BF16), zeros_lr], axis=0)
    wab_pad = jnp.concatenate([jnp.zeros((LR, GK), BF16), p["w_ab"].astype(BF16), zeros_lr[:128 - 2 * LR]], axis=0)
    gla_norm4 = jnp.tile(vec["gla_norm"], (1, NH))

    z, hb, w_out, w_xq, w_xo, w_xkv = _inproj(
        x, vec["mix_norm"], w_in, [_gather_rider([shard[n] for n in ("w_out", "w_xq", "w_xo", "w_xkv")])])
    w_out, w_xq, w_xo = [a.reshape(D, D) for a in (w_out, w_xq, w_xo)]
    o_f, sd_f, w_up = _gla_fwd_sweep(z, waf_pad, vec["b_af"], False, riders=[_gather_rider([shard["w_up"]])])
    yb, o_pre, sd_b, w_down = _gla_fwd_sweep(z, wab_pad, vec["b_ab"], True,
                                             (o_f, p["conv_w"], vec["conv_norm"], gla_norm4),
                                             riders=[_gather_rider([shard["w_down"]])])
    w_down = w_down.reshape(FF, D)
    kv, memn = _kv_proj(mem, vec["mem_norm"], w_xkv)
    kb, vb = kv[:, :D].astype(BF16), kv[:, D:].astype(BF16)
    x1, x2, xn1, qb, attb = _attn_fwd(x, yb, w_out, vec["xa_norm"], w_xq, kb, vb, w_xo)
    h1, xn2, dx3, dx3b, loss8, dfinal = _mlp_fwd(x2, vec["mlp_norm"], w_up, w_down, vec["final_norm"], target)

    ab, dh1b, dx2, dx2b, dmlp = _mlp_bwd(dx3, dx3b, h1, w_down, w_up, x2, vec["mlp_norm"])
    g_mlp = [by_dest(_matmul_tn(ab, dx3b, "dw_down")[0], "w_down"),
             by_dest(_matmul_tn(xn2, dh1b, "dw_up", col_block=UP_SHARD)[0], "w_up")]
    dx1, dx1b, dy, dqb, dkv, dxa = _attn_bwd(x1, dx2, dx2b, qb, kb, vb, w_xo, w_xq, w_out, vec["xa_norm"])
    dw_xo, *s_mlp = _matmul_tn(attb, dx2b, "dw_xo", riders=[_sibling_rider(g_mlp)])
    pb_mlp = pair_sums(("w_down", "w_up"), g_mlp, s_mlp)
    dw_xkv, dmemn = _kv_bwd(dkv, memn, mem, vec["mem_norm"], w_xkv)
    att_names = ("w_xo", "w_xq", "w_out", "w_xkv")
    g_att = [by_dest(g, n) for g, n in zip(
        (dw_xo, _matmul_tn(xn1, dqb, "dw_xq")[0], _matmul_tn(yb, dx1b, "dw_out")[0], dw_xkv), att_names)]
    res = _gla_bwd_first(z, dy, o_pre, sd_f, waf_pad, vec["b_af"], p["conv_w"], vec["conv_norm"], gla_norm4,
                         riders=[_chips_rider(pb_mlp), _sibling_rider(g_att)])
    do, dqa, dka, dva, dlra, dzg, dzcb, dconv, dwaf, dbaf, dcw, dcn, dgn = res[:13]
    from_chips["w_down"], from_chips["w_up"] = res[13:15]
    pb_att = pair_sums(att_names, g_att, res[15:])
    dz, dwab, dbab, *c_att = _gla_bwd_second(z, do, sd_b, wab_pad, vec["b_ab"], dqa, dka, dva, dlra, dzg, dzcb, dconv,
                                             p["conv_w"], riders=[_chips_rider(pb_att)])
    from_chips.update(zip(att_names, c_att))
    dw_in = _matmul_tn(hb, dz, "dw_in")[0][:, :ZW].reshape(D, NDEV, WIN_SHARD).transpose(1, 0, 2)
    g_in = [by_dest(dw_in, "w_in")]
    grad_x, dmix, s_in = _inproj_bwd(dz, w_in, x, dx1, vec["mix_norm"], riders=[_sibling_rider(g_in)])
    from_chips["w_in"], = _exchange(_chips_rider(pair_sums(("w_in",), g_in, [s_in])), "grads_to_chips_w_in")

    small_grads = {
        "mix_norm": dmix[0:1], "conv_w": dcw[0:3], "conv_norm": dcn[0:1],
        "w_af": dwaf[0:LR], "b_af": dbaf[0:1], "w_ab": dwab[LR:2 * LR], "b_ab": dbab[0:1],
        "gla_norm": (dgn[0:1, 0:128] + dgn[0:1, 128:256]) + (dgn[0:1, 256:384] + dgn[0:1, 384:512]),
        "xa_norm": dxa[0:1], "mem_norm": dmemn[0:1], "mlp_norm": dmlp[0:1], "final_norm": dfinal[0:1],
    }
    return loss8[0:1, 0:1], grad_x, small_grads, own, from_chips


def _place():
    return lax.axis_index("x"), lax.axis_index("y"), lax.axis_index("c")


class _Rider:
    def __init__(self, arrays, out_shape, scratch, start, finish):
        self.arrays, self.out_shape, self.scratch, self.start, self.finish = arrays, out_shape, scratch, start, finish


def _gather_rider(blks):
    n = len(blks)

    def plan(in_refs, out_refs, sems):
        send_sems, recv_sems, local_sems = sems
        x, y, c = _place()
        me, sibling = (x, y, c), (x, y, 1 - c)
        chips = [(1 - x, y, c), (x, 1 - y, c), (1 - x, 1 - y, c)]

        def copy(a, k, block, to, own=False):
            px, py, pc = block
            dst = out_refs[a].at[4 * px + 2 * py + pc]
            return pltpu.make_async_remote_copy(
                src_ref=in_refs[a] if own else dst, dst_ref=dst, send_sem=send_sems.at[k, a],
                recv_sem=recv_sems.at[k, a], device_id=to, device_id_type=MESH)

        def local(a):
            return pltpu.make_async_copy(in_refs[a], out_refs[a].at[4 * x + 2 * y + c], local_sems.at[a])

        def own_sends(a):
            return [copy(a, 0, me, sibling, own=True)] + [copy(a, 1 + j, me, chip, own=True)
                                                          for j, chip in enumerate(chips)]

        return copy, local, own_sends, me, sibling, chips

    def start(in_refs, out_refs, sems):
        _, local, own_sends, _, _, _ = plan(in_refs, out_refs, sems)
        for a in range(n):
            local(a).start()
            for cp in own_sends(a):
                cp.start()

    def finish(in_refs, out_refs, sems):
        copy, local, own_sends, me, sibling, chips = plan(in_refs, out_refs, sems)
        for j, chip in enumerate(chips):
            for a in range(n):
                copy(a, 1 + j, chip, me).wait_recv()
                copy(a, 4 + j, chip, sibling).start()
        for a in range(n):
            copy(a, 0, sibling, me).wait_recv()
            for j, (px, py, pc) in enumerate(chips):
                copy(a, 4 + j, (px, py, 1 - pc), me).wait_recv()
            for cp in own_sends(a) + [copy(a, 4 + j, chip, sibling) for j, chip in enumerate(chips)]:
                cp.wait_send()
            local(a).wait()

    return _Rider(blks, [jax.ShapeDtypeStruct((NDEV,) + b.shape, b.dtype) for b in blks],
                  [pltpu.SemaphoreType.DMA((7, n)), pltpu.SemaphoreType.DMA((7, n)), pltpu.SemaphoreType.DMA((n,))],
                  start, finish)


def _sibling_rider(g4s):
    n = len(g4s)

    def copies(in_refs, out_refs, sems):
        send_sems, recv_sems = sems
        x, y, c = _place()
        return [pltpu.make_async_remote_copy(
            src_ref=in_refs[a].at[k, 1 - c], dst_ref=out_refs[a].at[k], send_sem=send_sems.at[k, a],
            recv_sem=recv_sems.at[k, a], device_id=(x, y, 1 - c), device_id_type=MESH)
            for a in range(n) for k in range(4)]

    def start(in_refs, out_refs, sems):
        for cp in copies(in_refs, out_refs, sems):
            cp.start()

    def finish(in_refs, out_refs, sems):
        for cp in copies(in_refs, out_refs, sems):
            cp.wait()

    return _Rider(g4s, [jax.ShapeDtypeStruct((4,) + g.shape[2:], g.dtype) for g in g4s],
                  [pltpu.SemaphoreType.DMA((4, n)), pltpu.SemaphoreType.DMA((4, n))], start, finish)


def _chips_rider(pbs):
    n = len(pbs)

    def copies(in_refs, out_refs, sems):
        send_sems, recv_sems = sems
        x, y, c = _place()
        peers = [(1 - x, y), (x, 1 - y), (1 - x, 1 - y)]
        return [pltpu.make_async_remote_copy(
            src_ref=in_refs[a].at[2 * px + py], dst_ref=out_refs[a].at[k], send_sem=send_sems.at[k, a],
            recv_sem=recv_sems.at[k, a], device_id=(px, py, c), device_id_type=MESH)
            for a in range(n) for k, (px, py) in enumerate(peers)]

    def start(in_refs, out_refs, sems):
        for cp in copies(in_refs, out_refs, sems):
            cp.start()

    def finish(in_refs, out_refs, sems):
        for cp in copies(in_refs, out_refs, sems):
            cp.wait()

    return _Rider(pbs, [jax.ShapeDtypeStruct((3,) + p.shape[1:], p.dtype) for p in pbs],
                  [pltpu.SemaphoreType.DMA((3, n)), pltpu.SemaphoreType.DMA((3, n))], start, finish)


def _exchange(rider, name):
    n_in, n_out = len(rider.arrays), len(rider.out_shape)

    def body(*refs):
        ins, outs, sems = refs[:n_in], refs[n_in:n_in + n_out], refs[n_in + n_out:]
        rider.start(ins, outs, sems)
        rider.finish(ins, outs, sems)

    hbm = pl.BlockSpec(memory_space=pltpu.HBM)
    return pl.pallas_call(body, name=name, out_shape=rider.out_shape, in_specs=[hbm] * n_in,
                          out_specs=[hbm] * n_out, scratch_shapes=rider.scratch)(*rider.arrays)


def _rs_pair_sum(place, g4, r1, name):
    rows, cols = g4.shape[2:]
    tr = min(rows, 512)

    def body(pl_ref, g_ref, r_ref, pb_ref, own_ref):
        s = g_ref[0, 0] + r_ref[0]
        pb_ref[0] = s.astype(BF16)

        @pl.when(pl.program_id(1) == pl_ref[0])
        def _():
            own_ref[...] = s

    grid_spec = pltpu.PrefetchScalarGridSpec(
        num_scalar_prefetch=1, grid=(rows // tr, 4),
        in_specs=[pl.BlockSpec((1, 1, tr, cols), lambda r, k, p: (k, p[1], r, 0)),
                  pl.BlockSpec((1, tr, cols), lambda r, k, p: (k, r, 0))],
        out_specs=[pl.BlockSpec((1, tr, cols), lambda r, k, p: (k, r, 0)),
                   pl.BlockSpec((tr, cols), lambda r, k, p: (r, 0))])
    return pl.pallas_call(
        body, name=name, grid_spec=grid_spec,
        out_shape=[jax.ShapeDtypeStruct((4, rows, cols), BF16), jax.ShapeDtypeStruct((rows, cols), F32)],
        compiler_params=_cparams(("arbitrary", "arbitrary")))(place, g4, r1)


def _small_all_reduce(vec):
    m_per = vec.shape[0]

    def body(x_ref, all_ref, sum_ref, send_sems, recv_sems, local_sem):
        x, y, c = _place()
        me, sibling = (x, y, c), (x, y, 1 - c)
        chips = [(1 - x, y), (x, 1 - y), (1 - x, 1 - y)]

        def rows(px, py, pc):
            return all_ref.at[4 * px + 2 * py + pc]

        def copy(k, block, to, src=None):
            return pltpu.make_async_remote_copy(
                src_ref=rows(*block) if src is None else src, dst_ref=rows(*block),
                send_sem=send_sems.at[k], recv_sem=recv_sems.at[k], device_id=to, device_id_type=MESH)

        mine = pltpu.make_async_copy(x_ref, rows(*me), local_sem)
        mine.start()
        first = [copy(0, me, sibling, src=x_ref)]
        first += [copy(1 + j, me, (*chip, c), src=x_ref) for j, chip in enumerate(chips)]
        for cp in first:
            cp.start()
        passed = [copy(4 + j, (*chip, c), sibling) for j, chip in enumerate(chips)]
        for j, chip in enumerate(chips):
            copy(1 + j, (*chip, c), me).wait_recv()
            passed[j].start()
        copy(0, sibling, me).wait_recv()
        for j, chip in enumerate(chips):
            copy(4 + j, (*chip, 1 - c), me).wait_recv()
        for cp in first + passed:
            cp.wait_send()
        mine.wait()
        total = all_ref[0]
        for d in range(1, NDEV):
            total = total + all_ref[d]
        sum_ref[...] = total

    return pl.pallas_call(
        body, name="small_all_reduce",
        out_shape=[jax.ShapeDtypeStruct((NDEV, m_per, 128), F32), jax.ShapeDtypeStruct((m_per, 128), F32)],
        in_specs=[pl.BlockSpec(memory_space=pltpu.VMEM)],
        out_specs=[pl.BlockSpec(memory_space=pltpu.VMEM), pl.BlockSpec(memory_space=pltpu.VMEM)],
        scratch_shapes=[pltpu.SemaphoreType.DMA((7,)), pltpu.SemaphoreType.DMA((7,)), pltpu.SemaphoreType.DMA],
    )(vec)[1]


def _adamw_math(w, g, m, v):
    m = ADAM_B1 * m + (1.0 - ADAM_B1) * g
    v = ADAM_B2 * v + (1.0 - ADAM_B2) * (g * g)
    m_hat = m / (1.0 - ADAM_B1 ** ADAM_STEP)
    v_hat = v / (1.0 - ADAM_B2 ** ADAM_STEP)
    delta = -ADAM_LR * (m_hat / (jnp.sqrt(v_hat) + ADAM_EPS) + ADAM_WD * w)
    return delta, m, v


def _adamw(w, own, r2, m, v, name):
    r, c = w.shape
    tr = min(r, 256)

    def body(w_ref, o_ref, r_ref, m_ref, v_ref, g_ref, d_ref, nm_ref, nv_ref):
        g = ((o_ref[...] + r_ref[0].astype(F32)) + r_ref[1].astype(F32)) + r_ref[2].astype(F32)
        g_ref[...] = g
        d_ref[...], nm_ref[...], nv_ref[...] = _adamw_math(w_ref[...], g, m_ref[...], v_ref[...])

    spec = pl.BlockSpec((tr, c), lambda i: (i, 0))
    return pl.pallas_call(
        body, name=name, grid=(r // tr,),
        in_specs=[spec, spec, pl.BlockSpec((3, tr, c), lambda i: (0, i, 0)), spec, spec], out_specs=[spec] * 4,
        out_shape=[jax.ShapeDtypeStruct((r, c), F32)] * 4,
        compiler_params=_cparams(("arbitrary",)))(w, own, r2, m, v)


def _adamw_small(ws, gs, ms, vs):
    n = len(ws)

    def body(*refs):
        ins, outs = refs[:4 * n], refs[4 * n:]
        for i in range(n):
            d, m, v = _adamw_math(ins[i][...], ins[n + i][...], ins[2 * n + i][...], ins[3 * n + i][...])
            outs[i][...], outs[n + i][...], outs[2 * n + i][...] = d, m, v

    shapes = [jax.ShapeDtypeStruct(w.shape, F32) for w in ws]
    outs = pl.pallas_call(body, name="adamw_small", out_shape=shapes * 3)(*ws, *gs, *ms, *vs)
    return outs[:n], outs[n:2 * n], outs[2 * n:]


MATS = ("w_in", "w_out", "w_xq", "w_xo", "w_xkv", "w_up", "w_down")
SMALL = ("mix_norm", "conv_w", "conv_norm", "w_af", "b_af", "w_ab", "b_ab", "gla_norm", "xa_norm", "mem_norm",
         "mlp_norm", "final_norm")
WEIGHTS = ("mix_norm", "w_in", "conv_w", "conv_norm", "w_af", "b_af", "w_ab", "b_ab", "gla_norm", "w_out", "xa_norm",
           "mem_norm", "w_xq", "w_xkv", "w_xo", "mlp_norm", "w_up", "w_down", "final_norm")
COL_SHARDED = ("w_in", "w_xkv", "w_up")
SMALL_SHARDED = {"conv_w": (3, 64), "w_af": (LR, 32), "w_ab": (LR, 32)}
SMALL_PACK_ROWS = 16


def kernel(x, mem, mix_norm, w_in, conv_w, conv_norm, w_af, b_af, w_ab, b_ab, gla_norm, w_out, xa_norm, mem_norm, w_xq, w_xkv, w_xo, mlp_norm, w_up, w_down, final_norm, loss_target, m_mix_norm, m_w_in, m_conv_w, m_conv_norm, m_w_af, m_b_af, m_w_ab, m_b_ab, m_gla_norm, m_w_out, m_xa_norm, m_mem_norm, m_w_xq, m_w_xkv, m_w_xo, m_mlp_norm, m_w_up, m_w_down, m_final_norm, v_mix_norm, v_w_in, v_conv_w, v_conv_norm, v_w_af, v_b_af, v_w_ab, v_b_ab, v_gla_norm, v_w_out, v_xa_norm, v_mem_norm, v_w_xq, v_w_xkv, v_w_xo, v_mlp_norm, v_w_up, v_w_down, v_final_norm):
    w = dict(mix_norm=mix_norm, w_in=w_in, conv_w=conv_w, conv_norm=conv_norm, w_af=w_af, b_af=b_af, w_ab=w_ab,
             b_ab=b_ab, gla_norm=gla_norm, w_out=w_out, xa_norm=xa_norm, mem_norm=mem_norm, w_xq=w_xq, w_xkv=w_xkv,
             w_xo=w_xo, mlp_norm=mlp_norm, w_up=w_up, w_down=w_down, final_norm=final_norm)
    mom = dict(mix_norm=m_mix_norm, w_in=m_w_in, conv_w=m_conv_w, conv_norm=m_conv_norm, w_af=m_w_af, b_af=m_b_af,
               w_ab=m_w_ab, b_ab=m_b_ab, gla_norm=m_gla_norm, w_out=m_w_out, xa_norm=m_xa_norm, mem_norm=m_mem_norm,
               w_xq=m_w_xq, w_xkv=m_w_xkv, w_xo=m_w_xo, mlp_norm=m_mlp_norm, w_up=m_w_up, w_down=m_w_down,
               final_norm=m_final_norm)
    var = dict(mix_norm=v_mix_norm, w_in=v_w_in, conv_w=v_conv_w, conv_norm=v_conv_norm, w_af=v_w_af, b_af=v_b_af,
               w_ab=v_w_ab, b_ab=v_b_ab, gla_norm=v_gla_norm, w_out=v_w_out, xa_norm=v_xa_norm, mem_norm=v_mem_norm,
               w_xq=v_w_xq, w_xkv=v_w_xkv, w_xo=v_w_xo, mlp_norm=v_mlp_norm, w_up=v_w_up, w_down=v_w_down,
               final_norm=v_final_norm)
    xi, yi, ci = _place()
    me = 4 * xi + 2 * yi + ci
    two_d = lambda a: a.reshape(a.shape[-2:]) if a.ndim == 3 else a.reshape(1, a.shape[-1])

    small = jnp.concatenate([w[n].reshape(-1) for n in SMALL_SHARDED])
    small = jnp.pad(small, (0, SMALL_PACK_ROWS * 128 - small.shape[0])).reshape(SMALL_PACK_ROWS, 128)
    shard = {n: two_d(w[n]).astype(BF16) for n in MATS}
    vec = {n: two_d(w[n]) for n in SMALL if n not in SMALL_SHARDED}
    place = jnp.stack([2 * xi + yi, ci]).astype(jnp.int32)
    loss_part, grad_x, grads, own, from_chips = _step(x[0], mem[0], loss_target[0], shard, small, vec, place)

    order = [n for n in SMALL if n not in SMALL_SHARDED] + list(SMALL_SHARDED)
    flat = jnp.concatenate([grads[n].reshape(-1) for n in order] + [loss_part.reshape(-1)])
    n_flat = flat.shape[0]
    tot = _small_all_reduce(jnp.pad(flat, (0, SMALL_ROWS * 128 - n_flat)).reshape(SMALL_ROWS, 128)).reshape(-1)
    gsmall, off = {}, 0
    for n in order:
        size = grads[n].size
        full = tot[off:off + size].reshape(grads[n].shape)
        off += size
        if n in SMALL_SHARDED:
            r, c = SMALL_SHARDED[n]
            full = lax.dynamic_slice_in_dim(full, me * c, c, axis=1)
        gsmall[n] = full
    loss = tot[off]

    out_g, out_d, out_m, out_v = {}, {}, {}, {}
    for n in MATS:
        res = _adamw(two_d(w[n]), own[n], from_chips[n], two_d(mom[n]), two_d(var[n]), "adamw_" + n)
        out_g[n], out_d[n], out_m[n], out_v[n] = [o.reshape(w[n].shape) for o in res]
    ds, nms, nvs = _adamw_small([two_d(w[n]) for n in SMALL], [gsmall[n] for n in SMALL],
                                [two_d(mom[n]) for n in SMALL], [two_d(var[n]) for n in SMALL])
    for i, n in enumerate(SMALL):
        out_g[n], out_d[n], out_m[n], out_v[n] = [a.reshape(w[n].shape) for a in (gsmall[n], ds[i], nms[i], nvs[i])]

    return (loss, grad_x[None], *[out_g[n] for n in WEIGHTS], *[out_d[n] for n in WEIGHTS],
            *[out_m[n] for n in WEIGHTS], *[out_v[n] for n in WEIGHTS])
```

```python
import functools

import jax
import jax.numpy as jnp
from jax import lax
from jax.experimental import pallas as pl
from jax.experimental.pallas import tpu as pltpu

F32 = jnp.float32
BF16 = jnp.bfloat16

D = 1024
CW = 512
GK = 256
GV = 512
NH = 4
CH = 64
LR = 16
NMEM = 256
XD = 256
FF = 4096
ZW = 3104
ZC = 3200
EPS = 1e-6
NDEV = 8

ZB_CB, ZB_CC, ZB_CU, ZB_V, ZB_G = 0, 1, 2, 4, 5
ZB_Q, ZB_K = 6, 7
ZB_LR = 24

TM = 512
TM_MLP = 512
TF = 1024
TB = 256
TT = 2048
VMEM_LIMIT = 56 * 1024 * 1024

ADAM_LR, ADAM_B1, ADAM_B2, ADAM_EPS, ADAM_WD, ADAM_STEP = 0.001, 0.9, 0.999, 1e-08, 0.01, 10

WIN_SHARD, XKV_SHARD, UP_SHARD = ZW // NDEV, 2 * D // NDEV, FF // NDEV
SMALL_ROWS = 128

MESH = pl.DeviceIdType.MESH


def _cparams(sem):
    return pltpu.CompilerParams(dimension_semantics=sem, vmem_limit_bytes=VMEM_LIMIT)


def _call(body, name, grid, in_specs, out_specs, out_shape, scratch, args, riders=()):
    n_in, n_out, n_scr = len(in_specs), len(out_specs), len(scratch)
    counts = [(len(r.arrays), len(r.out_shape), len(r.scratch)) for r in riders]

    def take(refs, pos, sizes):
        groups = []
        for size in sizes:
            groups.append(refs[pos:pos + size])
            pos += size
        return groups, pos

    def wrapped(*refs):
        ins, pos = refs[:n_in], n_in
        r_ins, pos = take(refs, pos, [c[0] for c in counts])
        outs, pos = refs[pos:pos + n_out], pos + n_out
        r_outs, pos = take(refs, pos, [c[1] for c in counts])
        scr, pos = refs[pos:pos + n_scr], pos + n_scr
        r_scr, pos = take(refs, pos, [c[2] for c in counts])
        ids = [pl.program_id(d) for d in range(len(grid))]
        first = functools.reduce(lambda a, b: a & b, [i == 0 for i in ids])
        last = functools.reduce(lambda a, b: a & b, [i == g - 1 for i, g in zip(ids, grid)])

        @pl.when(first)
        def _():
            for r, a, b, c in zip(riders, r_ins, r_outs, r_scr):
                r.start(a, b, c)

        body(*ins, *outs, *scr)

        @pl.when(last)
        def _():
            for r, a, b, c in zip(riders, r_ins, r_outs, r_scr):
                r.finish(a, b, c)

    hbm = pl.BlockSpec(memory_space=pltpu.HBM)
    r_args = [a for r in riders for a in r.arrays]
    r_shapes = [s for r in riders for s in r.out_shape]
    return pl.pallas_call(
        wrapped if riders else body, name=name, grid=grid, in_specs=list(in_specs) + [hbm] * len(r_args),
        out_specs=list(out_specs) + [hbm] * len(r_shapes), out_shape=list(out_shape) + r_shapes,
        scratch_shapes=list(scratch) + [s for r in riders for s in r.scratch],
        compiler_params=_cparams(("arbitrary",) * len(grid)))(*args, *r_args)


def _dot(a, b):
    return jnp.dot(a.astype(BF16), b.astype(BF16), preferred_element_type=F32)


def _dot_nt(a, b):
    return lax.dot_general(a.astype(BF16), b.astype(BF16), (((1,), (1,)), ((), ())), preferred_element_type=F32)


def _dot_tn(a, b):
    return lax.dot_general(a.astype(BF16), b.astype(BF16), (((0,), (0,)), ((), ())), preferred_element_type=F32)


def _split(x, n):
    parts = []
    for _ in range(n):
        p = x.astype(BF16)
        parts.append(p)
        x = x - p.astype(F32)
    return parts


def _dot_exact_lhs(m, x, n):
    return functools.reduce(lambda a, b: a + b, [jnp.dot(m, p, preferred_element_type=F32) for p in _split(x, n)])


def _dot_exact_rhs(x, m, n):
    return functools.reduce(lambda a, b: a + b, [jnp.dot(p, m, preferred_element_type=F32) for p in _split(x, n)])


def _rms(x, g):
    r = lax.rsqrt(jnp.mean(x * x, axis=-1, keepdims=True) + EPS)
    return x * r * g, r


def _rms_bwd(x, r, g, dy):
    xr = x * r
    u = dy * g
    dx = r * (u - xr * jnp.mean(u * xr, axis=-1, keepdims=True))
    return dx, jnp.sum(dy * xr, axis=0, keepdims=True)


def _iota(shape, dim):
    return lax.broadcasted_iota(jnp.int32, shape, dim)


def _sigmoid(x):
    return 1.0 / (1.0 + jnp.exp(-x))


def _acc_rows(ref, row):
    ref[...] += jnp.broadcast_to(row, ref.shape)


def _inproj(x, g, w, riders=()):
    t = x.shape[0]
    tm = min(TM, t)

    def body(x_ref, g_ref, w_ref, z_ref, h_ref):
        h, _ = _rms(x_ref[...], g_ref[...])
        hb = h.astype(BF16)
        h_ref[...] = hb
        z_ref[...] = jnp.dot(hb, w_ref[...], preferred_element_type=F32)

    return _call(
        body, "inproj", (t // tm,),
        [pl.BlockSpec((tm, D), lambda i: (i, 0)), pl.BlockSpec((1, D), lambda i: (0, 0)),
         pl.BlockSpec((D, ZC), lambda i: (0, 0))],
        [pl.BlockSpec((tm, ZC), lambda i: (i, 0)), pl.BlockSpec((tm, D), lambda i: (i, 0))],
        [jax.ShapeDtypeStruct((t, ZC), F32), jax.ShapeDtypeStruct((t, D), BF16)], [], (x, g, w), riders)


def _kv_proj(mem, g, w):
    def body(m_ref, g_ref, w_ref, kv_ref, mn_ref):
        mn, _ = _rms(m_ref[...], g_ref[...])
        mb = mn.astype(BF16)
        mn_ref[...] = mb
        for j in range(NDEV):
            kv_ref[:, j * XKV_SHARD:(j + 1) * XKV_SHARD] = jnp.dot(mb, w_ref[j], preferred_element_type=F32)

    return pl.pallas_call(
        body, name="kv_proj",
        out_shape=[jax.ShapeDtypeStruct((NMEM, 2 * D), F32), jax.ShapeDtypeStruct((NMEM, D), BF16)],
        compiler_params=pltpu.CompilerParams(vmem_limit_bytes=VMEM_LIMIT))(mem, g, w)


def _softmax_head(qb, kb):
    s = _dot_nt(qb, kb) * (1.0 / 16.0)
    e = jnp.exp(s - jnp.max(s, axis=-1, keepdims=True))
    return e / jnp.sum(e, axis=-1, keepdims=True)


def _attn_fwd(x, yb, w_out, g, w_xq, kb, vb, w_xo):
    t = x.shape[0]
    tm = min(TM, t)

    def body(x_ref, y_ref, wo_ref, g_ref, wq_ref, k_ref, v_ref, wx_ref, x1_ref, x2_ref, xn_ref, q_ref, a_ref):
        x1 = x_ref[...] + jnp.dot(y_ref[...], wo_ref[...], preferred_element_type=F32)
        x1_ref[...] = x1
        xn, _ = _rms(x1, g_ref[...])
        xb = xn.astype(BF16)
        xn_ref[...] = xb
        qb = jnp.dot(xb, wq_ref[...], preferred_element_type=F32).astype(BF16)
        q_ref[...] = qb
        for h in range(NH):
            hs = slice(h * XD, (h + 1) * XD)
            p = _softmax_head(qb[:, hs], k_ref[:, hs])
            a_ref[:, hs] = _dot(p, v_ref[:, hs]).astype(BF16)
        x2_ref[...] = x1 + jnp.dot(a_ref[...], wx_ref[...], preferred_element_type=F32)

    tok = lambda i: (i, 0)
    full = lambda i: (0, 0)
    return pl.pallas_call(
        body, name="attn_fwd", grid=(t // tm,),
        in_specs=[pl.BlockSpec((tm, D), tok), pl.BlockSpec((tm, D), tok), pl.BlockSpec((D, D), full),
                  pl.BlockSpec((1, D), full), pl.BlockSpec((D, D), full), pl.BlockSpec((NMEM, D), full),
                  pl.BlockSpec((NMEM, D), full), pl.BlockSpec((D, D), full)],
        out_specs=[pl.BlockSpec((tm, D), tok)] * 5,
        out_shape=[jax.ShapeDtypeStruct((t, D), F32), jax.ShapeDtypeStruct((t, D), F32),
                   jax.ShapeDtypeStruct((t, D), BF16), jax.ShapeDtypeStruct((t, D), BF16),
                   jax.ShapeDtypeStruct((t, D), BF16)],
        compiler_params=_cparams(("arbitrary",)))(x, yb, w_out, g, w_xq, kb, vb, w_xo)


def _mlp_fwd(x2, g, w_up, w_down, fg, target):
    t = x2.shape[0]
    tm = min(TM_MLP, t)
    nj = FF // TF

    def body(x_ref, g_ref, wu_ref, wd_ref, fg_ref, t_ref, h1_ref, xn_ref, dx_ref, dxb_ref, loss_ref, dfg_ref,
             acc, xnb, ab):
        i, j = pl.program_id(0), pl.program_id(1)

        @pl.when(j == 0)
        def _():
            xn, _ = _rms(x_ref[...], g_ref[...])
            xnb[...] = xn.astype(BF16)
            xn_ref[...] = xnb[...]
            acc[...] = jnp.zeros_like(acc)

        @pl.when((i == 0) & (j == 0))
        def _():
            loss_ref[...] = jnp.zeros_like(loss_ref)
            dfg_ref[...] = jnp.zeros_like(dfg_ref)

        for q in range(TF // UP_SHARD):
            cols = slice(q * UP_SHARD, (q + 1) * UP_SHARD)
            h1 = jnp.dot(xnb[...], wu_ref[q], preferred_element_type=F32)
            h1_ref[:, cols] = h1
            hr = jnp.maximum(h1, 0.0)
            ab[:, cols] = (hr * hr).astype(BF16)
        acc[...] += jnp.dot(ab[...], wd_ref[...], preferred_element_type=F32)

        @pl.when(j == nj - 1)
        def _():
            x3 = x_ref[...] + acc[...]
            y, r = _rms(x3, fg_ref[...])
            e = y - t_ref[...]
            row = jnp.mean(e * e, axis=-1, keepdims=True)
            _acc_rows(loss_ref, 0.5 * jnp.sum(row, axis=0, keepdims=True))
            dx, dfg = _rms_bwd(x3, r, fg_ref[...], e * (1.0 / D))
            dx_ref[...] = dx
            dxb_ref[...] = dx.astype(BF16)
            _acc_rows(dfg_ref, dfg)

    tok = lambda i, j: (i, 0)
    full = lambda i, j: (0, 0)
    return pl.pallas_call(
        body, name="mlp_fwd", grid=(t // tm, nj),
        in_specs=[pl.BlockSpec((tm, D), tok), pl.BlockSpec((1, D), full),
                  pl.BlockSpec((TF // UP_SHARD, D, UP_SHARD), lambda i, j: (j, 0, 0)),
                  pl.BlockSpec((TF, D), lambda i, j: (j, 0)), pl.BlockSpec((1, D), full), pl.BlockSpec((tm, D), tok)],
        out_specs=[pl.BlockSpec((tm, TF), lambda i, j: (i, j)), pl.BlockSpec((tm, D), tok), pl.BlockSpec((tm, D), tok),
                   pl.BlockSpec((tm, D), tok), pl.BlockSpec((8, 128), full), pl.BlockSpec((8, D), full)],
        out_shape=[jax.ShapeDtypeStruct((t, FF), F32), jax.ShapeDtypeStruct((t, D), BF16),
                   jax.ShapeDtypeStruct((t, D), F32), jax.ShapeDtypeStruct((t, D), BF16),
                   jax.ShapeDtypeStruct((8, 128), F32), jax.ShapeDtypeStruct((8, D), F32)],
        scratch_shapes=[pltpu.VMEM((tm, D), F32), pltpu.VMEM((tm, D), BF16), pltpu.VMEM((tm, TF), BF16)],
        compiler_params=_cparams(("arbitrary", "arbitrary")))(x2, g, w_up, w_down, fg, target)


def _mlp_bwd(dx3, dx3b, h1, w_down, w_up, x2, g):
    t = x2.shape[0]
    tm = min(TM_MLP, t)
    nj = FF // TF

    def body(dx_ref, dxb_ref, h1_ref, wd_ref, wu_ref, x_ref, g_ref, a_ref, dh_ref, dx2_ref, dx2b_ref, dg_ref, acc):
        i, j = pl.program_id(0), pl.program_id(1)

        @pl.when(j == 0)
        def _():
            acc[...] = jnp.zeros_like(acc)

        @pl.when((i == 0) & (j == 0))
        def _():
            dg_ref[...] = jnp.zeros_like(dg_ref)

        hr = jnp.maximum(h1_ref[...], 0.0)
        da = _dot_nt(dxb_ref[...], wd_ref[...])
        dh = (da * 2.0 * hr).astype(BF16)
        a_ref[...] = (hr * hr).astype(BF16)
        dh_ref[...] = dh
        acc[...] += functools.reduce(lambda a, b: a + b, [
            _dot_nt(dh[:, q * UP_SHARD:(q + 1) * UP_SHARD], wu_ref[q]) for q in range(TF // UP_SHARD)])

        @pl.when(j == nj - 1)
        def _():
            x = x_ref[...]
            r = lax.rsqrt(jnp.mean(x * x, axis=-1, keepdims=True) + EPS)
            dx, dg = _rms_bwd(x, r, g_ref[...], acc[...])
            dx2 = dx_ref[...] + dx
            dx2_ref[...] = dx2
            dx2b_ref[...] = dx2.astype(BF16)
            _acc_rows(dg_ref, dg)

    tok = lambda i, j: (i, 0)
    full = lambda i, j: (0, 0)
    hid = lambda i, j: (i, j)
    return pl.pallas_call(
        body, name="mlp_bwd", grid=(t // tm, nj),
        in_specs=[pl.BlockSpec((tm, D), tok), pl.BlockSpec((tm, D), tok), pl.BlockSpec((tm, TF), hid),
                  pl.BlockSpec((TF, D), lambda i, j: (j, 0)),
                  pl.BlockSpec((TF // UP_SHARD, D, UP_SHARD), lambda i, j: (j, 0, 0)),
                  pl.BlockSpec((tm, D), tok), pl.BlockSpec((1, D), full)],
        out_specs=[pl.BlockSpec((tm, TF), hid), pl.BlockSpec((tm, TF), hid), pl.BlockSpec((tm, D), tok),
                   pl.BlockSpec((tm, D), tok), pl.BlockSpec((8, D), full)],
        out_shape=[jax.ShapeDtypeStruct((t, FF), BF16), jax.ShapeDtypeStruct((t, FF), BF16),
                   jax.ShapeDtypeStruct((t, D), F32), jax.ShapeDtypeStruct((t, D), BF16),
                   jax.ShapeDtypeStruct((8, D), F32)],
        scratch_shapes=[pltpu.VMEM((tm, D), F32)],
        compiler_params=_cparams(("arbitrary", "arbitrary")))(dx3, dx3b, h1, w_down, w_up, x2, g)


def _attn_bwd(x1, dx2, dx2b, qb, kb, vb, w_xo, w_xq, w_out, g):
    t = x1.shape[0]
    tm = min(TM, t)

    def body(x_ref, dx2_ref, dx2b_ref, q_ref, k_ref, v_ref, wx_ref, wq_ref, wo_ref, g_ref,
             dx1_ref, dx1b_ref, dy_ref, dq_ref, dkv_ref, dg_ref):
        @pl.when(pl.program_id(0) == 0)
        def _():
            dkv_ref[...] = jnp.zeros_like(dkv_ref)
            dg_ref[...] = jnp.zeros_like(dg_ref)

        datt = _dot_nt(dx2b_ref[...], wx_ref[...]).astype(BF16)
        for h in range(NH):
            hs = slice(h * XD, (h + 1) * XD)
            q_h, k_h, v_h, da_h = q_ref[:, hs], k_ref[:, hs], v_ref[:, hs], datt[:, hs]
            p = _softmax_head(q_h, k_h)
            dp = _dot_nt(da_h, v_h)
            ds = (p * (dp - jnp.sum(dp * p, axis=-1, keepdims=True)) * (1.0 / 16.0)).astype(BF16)
            dq_ref[:, hs] = _dot(ds, k_h).astype(BF16)
            dkv_ref[:, hs] += _dot_tn(ds, q_h)
            dkv_ref[:, D + h * XD:D + (h + 1) * XD] += _dot_tn(p, da_h)
        dxn = _dot_nt(dq_ref[...], wq_ref[...])
        x = x_ref[...]
        r = lax.rsqrt(jnp.mean(x * x, axis=-1, keepdims=True) + EPS)
        dx, dg = _rms_bwd(x, r, g_ref[...], dxn)
        dx1 = dx2_ref[...] + dx
        dx1_ref[...] = dx1
        dx1b = dx1.astype(BF16)
        dx1b_ref[...] = dx1b
        dy_ref[...] = _dot_nt(dx1b, wo_ref[...])
        _acc_rows(dg_ref, dg)

    tok = lambda i: (i, 0)
    full = lambda i: (0, 0)
    return pl.pallas_call(
        body, name="attn_bwd", grid=(t // tm,),
        in_specs=[pl.BlockSpec((tm, D), tok), pl.BlockSpec((tm, D), tok), pl.BlockSpec((tm, D), tok),
                  pl.BlockSpec((tm, D), tok), pl.BlockSpec((NMEM, D), full), pl.BlockSpec((NMEM, D), full),
                  pl.BlockSpec((D, D), full), pl.BlockSpec((D, D), full), pl.BlockSpec((D, D), full),
                  pl.BlockSpec((1, D), full)],
        out_specs=[pl.BlockSpec((tm, D), tok), pl.BlockSpec((tm, D), tok), pl.BlockSpec((tm, D), tok),
                   pl.BlockSpec((tm, D), tok), pl.BlockSpec((NMEM, 2 * D), full), pl.BlockSpec((8, D), full)],
        out_shape=[jax.ShapeDtypeStruct((t, D), F32), jax.ShapeDtypeStruct((t, D), BF16),
                   jax.ShapeDtypeStruct((t, D), F32), jax.ShapeDtypeStruct((t, D), BF16),
                   jax.ShapeDtypeStruct((NMEM, 2 * D), F32), jax.ShapeDtypeStruct((8, D), F32)],
        compiler_params=_cparams(("arbitrary",)))(x1, dx2, dx2b, qb, kb, vb, w_xo, w_xq, w_out, g)


def _kv_bwd(dkv, memn, mem, g, w):
    def body(dkv_ref, mn_ref, m_ref, g_ref, w_ref, dw_ref, dg_ref):
        dkvb = dkv_ref[...].astype(BF16)
        dmn = jnp.zeros((NMEM, D), F32)
        for j in range(NDEV):
            cols = slice(j * XKV_SHARD, (j + 1) * XKV_SHARD)
            dw_ref[j] = _dot_tn(mn_ref[...], dkvb[:, cols])
            dmn += _dot_nt(dkvb[:, cols], w_ref[j])
        m = m_ref[...]
        r = lax.rsqrt(jnp.mean(m * m, axis=-1, keepdims=True) + EPS)
        dg_ref[...] = jnp.broadcast_to(jnp.sum(dmn * m * r, axis=0, keepdims=True), dg_ref.shape)

    return pl.pallas_call(
        body, name="kv_bwd",
        out_shape=[jax.ShapeDtypeStruct((NDEV, D, XKV_SHARD), F32), jax.ShapeDtypeStruct((8, D), F32)],
        compiler_params=pltpu.CompilerParams(vmem_limit_bytes=VMEM_LIMIT))(dkv, memn, mem, g, w)


def _inproj_bwd(dz, w, x, dx1, g, riders=()):
    t = x.shape[0]
    tm = min(TM, t)

    def body(dz_ref, w_ref, x_ref, dx1_ref, g_ref, gx_ref, dg_ref):
        @pl.when(pl.program_id(0) == 0)
        def _():
            dg_ref[...] = jnp.zeros_like(dg_ref)

        dh = _dot_nt(dz_ref[...], w_ref[...])
        x = x_ref[...]
        r = lax.rsqrt(jnp.mean(x * x, axis=-1, keepdims=True) + EPS)
        dx, dg = _rms_bwd(x, r, g_ref[...], dh)
        gx_ref[...] = dx1_ref[...] + dx
        _acc_rows(dg_ref, dg)

    tok = lambda i: (i, 0)
    full = lambda i: (0, 0)
    return _call(
        body, "inproj_bwd", (t // tm,),
        [pl.BlockSpec((tm, ZC), tok), pl.BlockSpec((D, ZC), full), pl.BlockSpec((tm, D), tok),
         pl.BlockSpec((tm, D), tok), pl.BlockSpec((1, D), full)],
        [pl.BlockSpec((tm, D), tok), pl.BlockSpec((8, D), full)],
        [jax.ShapeDtypeStruct((t, D), F32), jax.ShapeDtypeStruct((8, D), F32)], [], (dz, w, x, dx1, g), riders)


def _matmul_tn(a, b, name, col_block=None, riders=()):
    t, k = a.shape
    n = b.shape[1]
    tk = min(k, 1024)
    tn = 640 if n % 1024 else 1024
    tt = min(TT, t)
    nt = t // tt
    if col_block:
        per = tn // col_block
        out_spec = pl.BlockSpec((per, tk, col_block), lambda i, j, s: (j, i, 0))
        out_shape = jax.ShapeDtypeStruct((n // col_block, k, col_block), F32)
    else:
        out_spec = pl.BlockSpec((tk, tn), lambda i, j, s: (i, j))
        out_shape = jax.ShapeDtypeStruct((k, n), F32)

    def body(a_ref, b_ref, o_ref):
        @pl.when(pl.program_id(2) == 0)
        def _():
            o_ref[...] = jnp.zeros_like(o_ref)

        if col_block:
            for q in range(per):
                o_ref[q] += _dot_tn(a_ref[...], b_ref[:, q * col_block:(q + 1) * col_block])
        else:
            o_ref[...] += _dot_tn(a_ref[...], b_ref[...])

    return _call(
        body, name, (k // tk, n // tn, nt),
        [pl.BlockSpec((tt, tk), lambda i, j, s: (s, i)), pl.BlockSpec((tt, tn), lambda i, j, s: (s, j))],
        [out_spec], [out_shape], [], (a, b), riders)


def _lane_head(shape, dim, shift):
    return _iota(shape, dim) >> shift


def _gla_recompute(q_raw, k, lr, wpad, bias, rev, tb):
    pre = _dot(lr, wpad) + bias
    la = (jnp.minimum(pre, 0.0) - jnp.log(1.0 + jnp.exp(-jnp.abs(pre)))) * (1.0 / 16.0)
    r, c = _iota((tb, tb), 0), _iota((tb, tb), 1)
    tri = (c >= r) if rev else (c <= r)
    cum = jnp.where(((r >> 6) == (c >> 6)) & tri, 1.0, 0.0).astype(BF16)
    b = _dot_exact_lhs(cum, la, 3)
    e, ei = jnp.exp(b), jnp.exp(-b)
    qt = (q_raw * 0.125) * e
    kt = k * ei
    return pre, b, e, ei, qt, kt


def _stack_heads(x, shift):
    head = _lane_head(x.shape, 1, shift)
    return jnp.concatenate([jnp.where(head == h, x, 0.0) for h in range(NH)], axis=0).astype(BF16)


def _fold_heads(x, shift):
    head = _lane_head((CH, x.shape[1]), 1, shift)
    return functools.reduce(lambda a, b: a + b,
                            [jnp.where(head == h, x[h * CH:(h + 1) * CH], 0.0) for h in range(NH)])


def _wide_mask(rev):
    r, s = _iota((CH, NH * CH), 0), _iota((CH, NH * CH), 1) & (CH - 1)
    return (s >= r) if rev else (s <= r)


def _state_mask():
    return (_iota((GV, GK), 0) >> 7) == (_iota((GV, GK), 1) >> 6)


def _state_expand(sd):
    head = _lane_head(sd.shape, 1, 6)
    return jnp.concatenate([jnp.where(head == h, sd, 0.0) for h in range(NH)], axis=0)


def _conv_parts(cb, cc, cu, ccp, cup, ccn, cun, cw_ref, first, last, tb):
    h = cc * cu
    hp = jnp.where(first, 0.0, ccp * cup)
    hn = jnp.where(last, 0.0, ccn * cun)
    rows = _iota(h.shape, 0)
    h_m1 = jnp.where(rows == 0, hp, pltpu.roll(h, 1, 0))
    h_p1 = jnp.where(rows == tb - 1, hn, pltpu.roll(h, tb - 1, 0))
    conv = cw_ref[pl.ds(0, 1), :] * h_m1 + cw_ref[pl.ds(1, 1), :] * h + cw_ref[pl.ds(2, 1), :] * h_p1
    return h, h_m1, h_p1, conv


def _group_ones():
    return jnp.where((_iota((CW, CW), 0) >> 6) == (_iota((CW, CW), 1) >> 6), 1.0, 0.0).astype(BF16)


def _head_norm(o):
    ons, rs = [], []
    for h in range(NH):
        slab = o[:, h * 128:(h + 1) * 128]
        r = lax.rsqrt(jnp.mean(slab * slab, axis=-1, keepdims=True) + EPS)
        ons.append(slab * r)
        rs.append(jnp.broadcast_to(r, slab.shape))
    return jnp.concatenate(ons, axis=1), jnp.concatenate(rs, axis=1)


def _zspec(tb, width, blk, jmap):
    return pl.BlockSpec((tb, width), lambda i: (jmap(i), blk))


def _halo_specs(tb, nblk, t, blk, jmap):
    prev = pl.BlockSpec((8, CW), lambda i: (jnp.maximum(jmap(i) * (tb // 8) - 1, 0), blk))
    nxt = pl.BlockSpec((8, CW), lambda i: (jnp.minimum((jmap(i) + 1) * (tb // 8), t // 8 - 1), blk))
    return prev, nxt


def _gla_fwd_sweep(z, wpad, bias, rev, finish_args=None, riders=()):
    t = z.shape[0]
    tb = min(TB, t)
    nblk, nb = t // tb, tb // CH
    jmap = (lambda i: nblk - 1 - i) if rev else (lambda i: i)
    finish = finish_args is not None

    def body(*refs):
        if finish:
            (q_ref, k_ref, v_ref, lr_ref, w_ref, bias_ref, of_ref, g_ref, cb_ref, cc_ref, cu_ref, ccp_ref, ccn_ref,
             cup_ref, cun_ref, cw_ref, cn_ref, gn_ref, y_ref, opre_ref, sd_ref, st, b_scr, o_scr) = refs
        else:
            q_ref, k_ref, v_ref, lr_ref, w_ref, bias_ref, o_ref, sd_ref, st, b_scr = refs
            o_scr = o_ref
        i = pl.program_id(0)

        @pl.when(i == 0)
        def _():
            st[...] = jnp.zeros_like(st)

        q_raw, k, v = q_ref[...], k_ref[...], v_ref[...]
        _, b, _, _, qt, kt = _gla_recompute(q_raw, k, lr_ref[...], w_ref[...], bias_ref[...], rev, tb)
        b_scr[...] = b
        maskw, bd = _wide_mask(rev), _state_mask()
        for c in (reversed(range(nb)) if rev else range(nb)):
            sl = slice(c * CH, (c + 1) * CH)
            gdec = jnp.exp(b_scr[pl.ds(c * CH + (0 if rev else CH - 1), 1), :])
            qt_c, kt_c, v_c = qt[sl], kt[sl], v[sl]
            s_in = st[...]
            sd_ref[c] = s_in[0:128] + s_in[128:256] + s_in[256:384] + s_in[384:512]
            a = jnp.where(maskw, _dot_nt(qt_c, _stack_heads(kt_c, 6)), 0.0)
            o_scr[pl.ds(c * CH, CH), :] = _dot(a, _stack_heads(v_c, 7)) + _dot_nt(qt_c, s_in)
            st[...] = s_in * gdec + jnp.where(bd, _dot_tn(v_c, kt_c * gdec), 0.0)

        if finish:
            j = jmap(i)
            hsel = jnp.where((_iota((GK, GV), 0) >> 6) == (_iota((GK, GV), 1) >> 7), 1.0, 0.0).astype(BF16)
            sb = _dot_exact_rhs((q_raw * 0.125) * k, hsel, 2)
            o_pre = of_ref[...] + o_scr[...] - sb * v
            opre_ref[...] = o_pre
            on, _ = _head_norm(o_pre)
            g = g_ref[...]
            y_ref[:, CW:] = (on * gn_ref[...] * (g * _sigmoid(g))).astype(BF16)
            cb = cb_ref[...]
            _, _, _, conv = _conv_parts(cb, cc_ref[...], cu_ref[...], ccp_ref[pl.ds(7, 1), :], cup_ref[pl.ds(7, 1), :],
                                        ccn_ref[pl.ds(0, 1), :], cun_ref[pl.ds(0, 1), :], cw_ref, j == 0,
                                        j == nblk - 1, tb)
            yc = cb * conv
            gm = _dot_exact_rhs(yc * yc, _group_ones(), 2) * (1.0 / 64.0)
            y_ref[:, :CW] = (yc * lax.rsqrt(gm + EPS) * cn_ref[...]).astype(BF16)

    full = lambda i: (0, 0)
    in_specs = [_zspec(tb, GK, ZB_Q, jmap), _zspec(tb, GK, ZB_K, jmap), _zspec(tb, GV, ZB_V, jmap),
                _zspec(tb, 128, ZB_LR, jmap), pl.BlockSpec((128, GK), full), pl.BlockSpec((1, GK), full)]
    args = [z, z, z, z, wpad, bias]
    sd_spec = pl.BlockSpec((nb, 128, GK), lambda i: (jmap(i), 0, 0))
    sd_shape = jax.ShapeDtypeStruct((t // CH, 128, GK), F32)
    scratch = [pltpu.VMEM((GV, GK), F32), pltpu.VMEM((tb, GK), F32)]
    if finish:
        o_f, conv_w, conv_norm, gla_norm4 = finish_args
        ccp, ccn = _halo_specs(tb, nblk, t, ZB_CC, jmap)
        cup, cun = _halo_specs(tb, nblk, t, ZB_CU, jmap)
        in_specs += [pl.BlockSpec((tb, GV), lambda i: (jmap(i), 0)), _zspec(tb, GV, ZB_G, jmap),
                     _zspec(tb, CW, ZB_CB, jmap), _zspec(tb, CW, ZB_CC, jmap), _zspec(tb, CW, ZB_CU, jmap),
                     ccp, ccn, cup, cun, pl.BlockSpec((3, CW), full), pl.BlockSpec((1, CW), full),
                     pl.BlockSpec((1, GV), full)]
        args += [o_f, z, z, z, z, z, z, z, z, conv_w, conv_norm, gla_norm4]
        out_specs = [pl.BlockSpec((tb, D), lambda i: (jmap(i), 0)), pl.BlockSpec((tb, GV), lambda i: (jmap(i), 0)),
                     sd_spec]
        out_shape = [jax.ShapeDtypeStruct((t, D), BF16), jax.ShapeDtypeStruct((t, GV), F32), sd_shape]
        scratch.append(pltpu.VMEM((tb, GV), F32))
    else:
        out_specs = [pl.BlockSpec((tb, GV), lambda i: (jmap(i), 0)), sd_spec]
        out_shape = [jax.ShapeDtypeStruct((t, GV), F32), sd_shape]
    return _call(body, "gla_fwd_rev" if rev else "gla_fwd", (nblk,), in_specs, out_specs, out_shape, scratch, args,
                 riders)


def _gla_bwd_chunks(do_ref, sd_ref, dst, b_scr, db_scr, dq_ref, dk_ref, dv_ref, qt, kt, e, ei, v, rev, nb):
    maskw, bd = _wide_mask(rev), _state_mask()
    for c in (range(nb) if rev else reversed(range(nb))):
        sl = slice(c * CH, (c + 1) * CH)
        grow = c * CH + (0 if rev else CH - 1)
        gdec = jnp.exp(b_scr[pl.ds(grow, 1), :])
        qt_c, kt_c, v_c, do_c = qt[sl], kt[sl], v[sl], do_ref[pl.ds(c * CH, CH), :]
        s_in = _state_expand(sd_ref[c])
        ds_out = dst[...]
        kbd, vbd = _stack_heads(kt_c, 6), _stack_heads(v_c, 7)
        a = jnp.where(maskw, _dot_nt(qt_c, kbd), 0.0)
        da = jnp.where(maskw, _dot_nt(do_c, vbd), 0.0)
        kh = kt_c * gdec
        dv_ref[pl.ds(c * CH, CH), :] = _fold_heads(_dot_tn(a, do_c), 7) + _dot_nt(kh, ds_out)
        dqt = _dot(da, kbd) + _dot(do_c, s_in)
        dkh = _dot(v_c, ds_out)
        dkt = _fold_heads(_dot_tn(da, qt_c), 6) + dkh * gdec
        dg = jnp.sum(ds_out * s_in, axis=0, keepdims=True) + jnp.sum(kt_c * dkh, axis=0, keepdims=True)
        db_scr[pl.ds(c * CH, CH), :] = dqt * qt_c - dkt * kt_c
        db_scr[pl.ds(grow, 1), :] += dg * gdec
        dq_ref[pl.ds(c * CH, CH), :] = dqt * e[sl] * 0.125
        dk_ref[pl.ds(c * CH, CH), :] = dkt * ei[sl]
        dst[...] = ds_out * gdec + jnp.where(bd, _dot_tn(do_c, qt_c), 0.0)


def _gate_bwd(db, pre, lr, wpad, rev, tb):
    r, c = _iota((tb, tb), 0), _iota((tb, tb), 1)
    tri = (c <= r) if rev else (c >= r)
    cum_t = jnp.where(((r >> 6) == (c >> 6)) & tri, 1.0, 0.0).astype(BF16)
    dla = _dot_exact_lhs(cum_t, db, 2)
    dpre = dla * (1.0 / 16.0) / (1.0 + jnp.exp(pre))
    return dpre, _dot_nt(dpre, wpad), _dot_tn(lr, dpre)


def _gla_bwd_first(z, dy, o_pre, sd, wpad, bias, conv_w, conv_norm, gla_norm4, riders=()):
    t = z.shape[0]
    tb = min(TB, t)
    nblk, nb = t // tb, tb // CH
    jmap = lambda i: nblk - 1 - i

    def body(q_ref, k_ref, v_ref, lr_ref, g_ref, cb_ref, cc_ref, cu_ref, ccp_ref, ccn_ref, cup_ref, cun_ref,
             dy_ref, opre_ref, sd_ref, w_ref, bias_ref, cw_ref, cn_ref, gn_ref,
             do_ref, dq_ref, dk_ref, dv_ref, dlr_ref, dzg_ref, dzcb_ref, dconv_ref,
             dw_ref, dbias_ref, dcw_ref, dcn_ref, dgn_ref, dst, b_scr, db_scr):
        i = pl.program_id(0)
        j = jmap(i)

        @pl.when(i == 0)
        def _():
            dst[...] = jnp.zeros_like(dst)
            for ref in (dw_ref, dbias_ref, dcw_ref, dcn_ref, dgn_ref):
                ref[...] = jnp.zeros_like(ref)

        dyg = dy_ref[:, CW:]
        g = g_ref[...]
        sig = _sigmoid(g)
        on, rr = _head_norm(opre_ref[...])
        gn = gn_ref[...]
        dzg_ref[...] = (dyg * on * gn * (sig * (1.0 + g * (1.0 - sig)))).astype(BF16)
        don = dyg * (g * sig)
        _acc_rows(dgn_ref, jnp.sum(don * on, axis=0, keepdims=True))
        u = don * gn
        uo = u * on
        mean_uo = jnp.concatenate(
            [jnp.broadcast_to(jnp.mean(uo[:, h * 128:(h + 1) * 128], axis=-1, keepdims=True), (tb, 128))
             for h in range(NH)], axis=1)
        do_ref[...] = rr * (u - on * mean_uo)

        cb = cb_ref[...]
        h, h_m1, h_p1, conv = _conv_parts(cb, cc_ref[...], cu_ref[...], ccp_ref[pl.ds(7, 1), :],
                                          cup_ref[pl.ds(7, 1), :], ccn_ref[pl.ds(0, 1), :], cun_ref[pl.ds(0, 1), :],
                                          cw_ref, j == 0, j == nblk - 1, tb)
        yc = cb * conv
        ones = _group_ones()
        rc = lax.rsqrt(_dot_exact_rhs(yc * yc, ones, 2) * (1.0 / 64.0) + EPS)
        ycr = yc * rc
        dyn = dy_ref[:, :CW]
        _acc_rows(dcn_ref, jnp.sum(dyn * ycr, axis=0, keepdims=True))
        uc = dyn * cn_ref[...]
        dyc = rc * (uc - ycr * (_dot_exact_rhs(uc * ycr, ones, 2) * (1.0 / 64.0)))
        dzcb_ref[...] = (dyc * conv).astype(BF16)
        dconv = dyc * cb
        dconv_ref[...] = dconv
        dcw_ref[pl.ds(0, 1), :] += jnp.sum(dconv * h_m1, axis=0, keepdims=True)
        dcw_ref[pl.ds(1, 1), :] += jnp.sum(dconv * h, axis=0, keepdims=True)
        dcw_ref[pl.ds(2, 1), :] += jnp.sum(dconv * h_p1, axis=0, keepdims=True)

        lr, wp = lr_ref[...], w_ref[...]
        pre, b, e, ei, qt, kt = _gla_recompute(q_ref[...], k_ref[...], lr, wp, bias_ref[...], False, tb)
        b_scr[...] = b
        _gla_bwd_chunks(do_ref, sd_ref, dst, b_scr, db_scr, dq_ref, dk_ref, dv_ref, qt, kt, e, ei, v_ref[...],
                        False, nb)
        dpre, dlr, dw = _gate_bwd(db_scr[...], pre, lr, wp, False, tb)
        dlr_ref[...] = dlr
        dw_ref[...] += dw
        _acc_rows(dbias_ref, jnp.sum(dpre, axis=0, keepdims=True))

    full = lambda i: (0, 0)
    tokv = pl.BlockSpec((tb, GV), lambda i: (jmap(i), 0))
    tokk = pl.BlockSpec((tb, GK), lambda i: (jmap(i), 0))
    ccp, ccn = _halo_specs(tb, nblk, t, ZB_CC, jmap)
    cup, cun = _halo_specs(tb, nblk, t, ZB_CU, jmap)
    in_specs = [_zspec(tb, GK, ZB_Q, jmap), _zspec(tb, GK, ZB_K, jmap), _zspec(tb, GV, ZB_V, jmap),
                _zspec(tb, 128, ZB_LR, jmap), _zspec(tb, GV, ZB_G, jmap), _zspec(tb, CW, ZB_CB, jmap),
                _zspec(tb, CW, ZB_CC, jmap), _zspec(tb, CW, ZB_CU, jmap), ccp, ccn, cup, cun,
                pl.BlockSpec((tb, D), lambda i: (jmap(i), 0)), tokv,
                pl.BlockSpec((nb, 128, GK), lambda i: (jmap(i), 0, 0)), pl.BlockSpec((128, GK), full),
                pl.BlockSpec((1, GK), full), pl.BlockSpec((3, CW), full), pl.BlockSpec((1, CW), full),
                pl.BlockSpec((1, GV), full)]
    out_specs = [tokv, tokk, tokk, tokv, pl.BlockSpec((tb, 128), lambda i: (jmap(i), 0)), tokv, tokv, tokv,
                 pl.BlockSpec((128, GK), full), pl.BlockSpec((8, GK), full), pl.BlockSpec((8, CW), full),
                 pl.BlockSpec((8, CW), full), pl.BlockSpec((8, GV), full)]
    out_shape = [jax.ShapeDtypeStruct((t, GV), F32), jax.ShapeDtypeStruct((t, GK), F32),
                 jax.ShapeDtypeStruct((t, GK), F32), jax.ShapeDtypeStruct((t, GV), F32),
                 jax.ShapeDtypeStruct((t, 128), F32), jax.ShapeDtypeStruct((t, GV), BF16),
                 jax.ShapeDtypeStruct((t, CW), BF16), jax.ShapeDtypeStruct((t, CW), F32),
                 jax.ShapeDtypeStruct((128, GK), F32), jax.ShapeDtypeStruct((8, GK), F32),
                 jax.ShapeDtypeStruct((8, CW), F32), jax.ShapeDtypeStruct((8, CW), F32),
                 jax.ShapeDtypeStruct((8, GV), F32)]
    return _call(
        body, "gla_bwd_first", (nblk,), in_specs, out_specs, out_shape,
        [pltpu.VMEM((GV, GK), F32), pltpu.VMEM((tb, GK), F32), pltpu.VMEM((tb, GK), F32)],
        (z, z, z, z, z, z, z, z, z, z, z, z, dy, o_pre, sd, wpad, bias, conv_w, conv_norm, gla_norm4), riders)


def _gla_bwd_second(z, do, sd, wpad, bias, dqa, dka, dva, dlra, dzg, dzcb, dconv, conv_w, riders=()):
    t = z.shape[0]
    tb = min(TB, t)
    nblk, nb = t // tb, tb // CH
    jmap = lambda i: i

    def body(q_ref, k_ref, v_ref, lr_ref, cc_ref, cu_ref, do_ref, sd_ref, w_ref, bias_ref, dqa_ref, dka_ref,
             dva_ref, dlra_ref, dzg_ref, dzcb_ref, dc_ref, dcp_ref, dcn_ref, cw_ref,
             dz_ref, dw_ref, dbias_ref, dst, b_scr, db_scr, dq_scr, dk_scr, dv_scr):
        i = pl.program_id(0)

        @pl.when(i == 0)
        def _():
            dst[...] = jnp.zeros_like(dst)
            dw_ref[...] = jnp.zeros_like(dw_ref)
            dbias_ref[...] = jnp.zeros_like(dbias_ref)

        q_raw, k, v, lr, wp = q_ref[...], k_ref[...], v_ref[...], lr_ref[...], w_ref[...]
        pre, b, e, ei, qt, kt = _gla_recompute(q_raw, k, lr, wp, bias_ref[...], True, tb)
        b_scr[...] = b
        _gla_bwd_chunks(do_ref, sd_ref, dst, b_scr, db_scr, dq_scr, dk_scr, dv_scr, qt, kt, e, ei, v, True, nb)
        dpre, dlr, dw = _gate_bwd(db_scr[...], pre, lr, wp, True, tb)
        dw_ref[...] += dw
        _acc_rows(dbias_ref, jnp.sum(dpre, axis=0, keepdims=True))

        do = do_ref[...]
        qs = q_raw * 0.125
        hsel = jnp.where((_iota((GK, GV), 0) >> 6) == (_iota((GK, GV), 1) >> 7), 1.0, 0.0).astype(BF16)
        hsel_t = jnp.where((_iota((GV, GK), 0) >> 7) == (_iota((GV, GK), 1) >> 6), 1.0, 0.0).astype(BF16)
        sb = _dot_exact_rhs(qs * k, hsel, 2)
        dsk = _dot_exact_rhs(do * v, hsel_t, 2)
        dz_ref[:, 1536:1792] = (dqa_ref[...] + dq_scr[...] - dsk * k * 0.125).astype(BF16)
        dz_ref[:, 1792:2048] = (dka_ref[...] + dk_scr[...] - dsk * qs).astype(BF16)
        dz_ref[:, 2048:2560] = (dva_ref[...] + dv_scr[...] - sb * do).astype(BF16)
        dz_ref[:, 2560:3072] = dzg_ref[...]
        dz_ref[:, 3072:3200] = (dlra_ref[...] + dlr).astype(BF16)

        dc = dc_ref[...]
        rows = _iota(dc.shape, 0)
        dprev = jnp.where(i == 0, 0.0, dcp_ref[pl.ds(7, 1), :])
        dnext = jnp.where(i == nblk - 1, 0.0, dcn_ref[pl.ds(0, 1), :])
        dc_m1 = jnp.where(rows == 0, dprev, pltpu.roll(dc, 1, 0))
        dc_p1 = jnp.where(rows == tb - 1, dnext, pltpu.roll(dc, tb - 1, 0))
        dh = cw_ref[pl.ds(0, 1), :] * dc_p1 + cw_ref[pl.ds(1, 1), :] * dc + cw_ref[pl.ds(2, 1), :] * dc_m1
        dz_ref[:, 0:512] = dzcb_ref[...]
        dz_ref[:, 512:1024] = (dh * cu_ref[...]).astype(BF16)
        dz_ref[:, 1024:1536] = (dh * cc_ref[...]).astype(BF16)

    full = lambda i: (0, 0)
    tokv = pl.BlockSpec((tb, GV), lambda i: (i, 0))
    tokk = pl.BlockSpec((tb, GK), lambda i: (i, 0))
    dcp = pl.BlockSpec((8, CW), lambda i: (jnp.maximum(i * (tb // 8) - 1, 0), 0))
    dcn = pl.BlockSpec((8, CW), lambda i: (jnp.minimum((i + 1) * (tb // 8), t // 8 - 1), 0))
    in_specs = [_zspec(tb, GK, ZB_Q, jmap), _zspec(tb, GK, ZB_K, jmap), _zspec(tb, GV, ZB_V, jmap),
                _zspec(tb, 128, ZB_LR, jmap), _zspec(tb, CW, ZB_CC, jmap), _zspec(tb, CW, ZB_CU, jmap), tokv,
                pl.BlockSpec((nb, 128, GK), lambda i: (i, 0, 0)), pl.BlockSpec((128, GK), full),
                pl.BlockSpec((1, GK), full), tokk, tokk, tokv, pl.BlockSpec((tb, 128), lambda i: (i, 0)), tokv, tokv,
                tokv, dcp, dcn, pl.BlockSpec((3, CW), full)]
    out_specs = [pl.BlockSpec((tb, ZC), lambda i: (i, 0)), pl.BlockSpec((128, GK), full), pl.BlockSpec((8, GK), full)]
    out_shape = [jax.ShapeDtypeStruct((t, ZC), BF16), jax.ShapeDtypeStruct((128, GK), F32),
                 jax.ShapeDtypeStruct((8, GK), F32)]
    return _call(
        body, "gla_bwd_second", (nblk,), in_specs, out_specs, out_shape,
        [pltpu.VMEM((GV, GK), F32), pltpu.VMEM((tb, GK), F32), pltpu.VMEM((tb, GK), F32),
         pltpu.VMEM((tb, GK), F32), pltpu.VMEM((tb, GK), F32), pltpu.VMEM((tb, GV), F32)],
        (z, z, z, z, z, z, do, sd, wpad, bias, dqa, dka, dva, dlra, dzg, dzcb, dconv, dconv, dconv, conv_w), riders)


def _step(x, mem, target, shard, small_pack, vec, place):
    assert TF % UP_SHARD == 0
    own, from_chips = {}, {}

    def pair_sums(names, g4, from_sibling):
        pbs = []
        for n, g, s in zip(names, g4, from_sibling):
            pb, own[n] = _rs_pair_sum(place, g, s, "pair_sum_" + n)
            pbs.append(pb)
        return pbs

    def by_dest(g, n):
        return g.reshape((4, 2) + shard[n].shape)

    w_in, small_all = _exchange(_gather_rider([shard["w_in"], small_pack]), "gather_w_in")
    w_in = jnp.pad(w_in.transpose(1, 0, 2).reshape(D, ZW), ((0, 0), (0, ZC - ZW)))
    small_all = small_all.reshape(NDEV, -1)
    p, off = {}, 0
    for n, (r, c) in SMALL_SHARDED.items():
        p[n] = small_all[:, off:off + r * c].reshape(NDEV, r, c).transpose(1, 0, 2).reshape(r, NDEV * c)
        off += r * c
    zeros_lr = jnp.zeros((128 - LR, GK), BF16)
    waf_pad = jnp.concatenate([p["w_af"].astype(BF16), zeros_lr], axis=0)
    wab_pad = jnp.concatenate([jnp.zeros((LR, GK), BF16), p["w_ab"].astype(BF16), zeros_lr[:128 - 2 * LR]], axis=0)
    gla_norm4 = jnp.tile(vec["gla_norm"], (1, NH))

    z, hb, w_out, w_xq, w_xo, w_xkv = _inproj(
        x, vec["mix_norm"], w_in, [_gather_rider([shard[n] for n in ("w_out", "w_xq", "w_xo", "w_xkv")])])
    w_out, w_xq, w_xo = [a.reshape(D, D) for a in (w_out, w_xq, w_xo)]
    o_f, sd_f, w_up = _gla_fwd_sweep(z, waf_pad, vec["b_af"], False, riders=[_gather_rider([shard["w_up"]])])
    yb, o_pre, sd_b, w_down = _gla_fwd_sweep(z, wab_pad, vec["b_ab"], True,
                                             (o_f, p["conv_w"], vec["conv_norm"], gla_norm4),
                                             riders=[_gather_rider([shard["w_down"]])])
    w_down = w_down.reshape(FF, D)
    kv, memn = _kv_proj(mem, vec["mem_norm"], w_xkv)
    kb, vb = kv[:, :D].astype(BF16), kv[:, D:].astype(BF16)
    x1, x2, xn1, qb, attb = _attn_fwd(x, yb, w_out, vec["xa_norm"], w_xq, kb, vb, w_xo)
    h1, xn2, dx3, dx3b, loss8, dfinal = _mlp_fwd(x2, vec["mlp_norm"], w_up, w_down, vec["final_norm"], target)

    ab, dh1b, dx2, dx2b, dmlp = _mlp_bwd(dx3, dx3b, h1, w_down, w_up, x2, vec["mlp_norm"])
    g_mlp = [by_dest(_matmul_tn(ab, dx3b, "dw_down")[0], "w_down"),
             by_dest(_matmul_tn(xn2, dh1b, "dw_up", col_block=UP_SHARD)[0], "w_up")]
    dx1, dx1b, dy, dqb, dkv, dxa = _attn_bwd(x1, dx2, dx2b, qb, kb, vb, w_xo, w_xq, w_out, vec["xa_norm"])
    dw_xo, *s_mlp = _matmul_tn(attb, dx2b, "dw_xo", riders=[_sibling_rider(g_mlp)])
    pb_mlp = pair_sums(("w_down", "w_up"), g_mlp, s_mlp)
    dw_xkv, dmemn = _kv_bwd(dkv, memn, mem, vec["mem_norm"], w_xkv)
    att_names = ("w_xo", "w_xq", "w_out", "w_xkv")
    g_att = [by_dest(g, n) for g, n in zip(
        (dw_xo, _matmul_tn(xn1, dqb, "dw_xq")[0], _matmul_tn(yb, dx1b, "dw_out")[0], dw_xkv), att_names)]
    res = _gla_bwd_first(z, dy, o_pre, sd_f, waf_pad, vec["b_af"], p["conv_w"], vec["conv_norm"], gla_norm4,
                         riders=[_chips_rider(pb_mlp), _sibling_rider(g_att)])
    do, dqa, dka, dva, dlra, dzg, dzcb, dconv, dwaf, dbaf, dcw, dcn, dgn = res[:13]
    from_chips["w_down"], from_chips["w_up"] = res[13:15]
    pb_att = pair_sums(att_names, g_att, res[15:])
    dz, dwab, dbab, *c_att = _gla_bwd_second(z, do, sd_b, wab_pad, vec["b_ab"], dqa, dka, dva, dlra, dzg, dzcb, dconv,
                                             p["conv_w"], riders=[_chips_rider(pb_att)])
    from_chips.update(zip(att_names, c_att))
    dw_in = _matmul_tn(hb, dz, "dw_in")[0][:, :ZW].reshape(D, NDEV, WIN_SHARD).transpose(1, 0, 2)
    g_in = [by_dest(dw_in, "w_in")]
    pb_in = pair_sums(("w_in",), g_in, _exchange(_sibling_rider(g_in), "grads_to_sibling_w_in"))
    grad_x, dmix, from_chips["w_in"] = _inproj_bwd(dz, w_in, x, dx1, vec["mix_norm"], riders=[_chips_rider(pb_in)])

    small_grads = {
        "mix_norm": dmix[0:1], "conv_w": dcw[0:3], "conv_norm": dcn[0:1],
        "w_af": dwaf[0:LR], "b_af": dbaf[0:1], "w_ab": dwab[LR:2 * LR], "b_ab": dbab[0:1],
        "gla_norm": (dgn[0:1, 0:128] + dgn[0:1, 128:256]) + (dgn[0:1, 256:384] + dgn[0:1, 384:512]),
        "xa_norm": dxa[0:1], "mem_norm": dmemn[0:1], "mlp_norm": dmlp[0:1], "final_norm": dfinal[0:1],
    }
    return loss8[0:1, 0:1], grad_x, small_grads, own, from_chips


def _place():
    return lax.axis_index("x"), lax.axis_index("y"), lax.axis_index("c")


class _Rider:
    def __init__(self, arrays, out_shape, scratch, start, finish):
        self.arrays, self.out_shape, self.scratch, self.start, self.finish = arrays, out_shape, scratch, start, finish


def _gather_rider(blks):
    n = len(blks)

    def plan(in_refs, out_refs, sems):
        send_sems, recv_sems, local_sems = sems
        x, y, c = _place()
        me, sibling = (x, y, c), (x, y, 1 - c)
        chips = [(1 - x, y, c), (x, 1 - y, c), (1 - x, 1 - y, c)]

        def copy(a, k, block, to, own=False):
            px, py, pc = block
            dst = out_refs[a].at[4 * px + 2 * py + pc]
            return pltpu.make_async_remote_copy(
                src_ref=in_refs[a] if own else dst, dst_ref=dst, send_sem=send_sems.at[k, a],
                recv_sem=recv_sems.at[k, a], device_id=to, device_id_type=MESH)

        def local(a):
            return pltpu.make_async_copy(in_refs[a], out_refs[a].at[4 * x + 2 * y + c], local_sems.at[a])

        def own_sends(a):
            return [copy(a, 0, me, sibling, own=True)] + [copy(a, 1 + j, me, chip, own=True)
                                                          for j, chip in enumerate(chips)]

        return copy, local, own_sends, me, sibling, chips

    def start(in_refs, out_refs, sems):
        _, local, own_sends, _, _, _ = plan(in_refs, out_refs, sems)
        for a in range(n):
            local(a).start()
            for cp in own_sends(a):
                cp.start()

    def finish(in_refs, out_refs, sems):
        copy, local, own_sends, me, sibling, chips = plan(in_refs, out_refs, sems)
        for j, chip in enumerate(chips):
            for a in range(n):
                copy(a, 1 + j, chip, me).wait_recv()
                copy(a, 4 + j, chip, sibling).start()
        for a in range(n):
            copy(a, 0, sibling, me).wait_recv()
            for j, (px, py, pc) in enumerate(chips):
                copy(a, 4 + j, (px, py, 1 - pc), me).wait_recv()
            for cp in own_sends(a) + [copy(a, 4 + j, chip, sibling) for j, chip in enumerate(chips)]:
                cp.wait_send()
            local(a).wait()

    return _Rider(blks, [jax.ShapeDtypeStruct((NDEV,) + b.shape, b.dtype) for b in blks],
                  [pltpu.SemaphoreType.DMA((7, n)), pltpu.SemaphoreType.DMA((7, n)), pltpu.SemaphoreType.DMA((n,))],
                  start, finish)


def _sibling_rider(g4s):
    n = len(g4s)

    def copies(in_refs, out_refs, sems):
        send_sems, recv_sems = sems
        x, y, c = _place()
        return [pltpu.make_async_remote_copy(
            src_ref=in_refs[a].at[k, 1 - c], dst_ref=out_refs[a].at[k], send_sem=send_sems.at[k, a],
            recv_sem=recv_sems.at[k, a], device_id=(x, y, 1 - c), device_id_type=MESH)
            for a in range(n) for k in range(4)]

    def start(in_refs, out_refs, sems):
        for cp in copies(in_refs, out_refs, sems):
            cp.start()

    def finish(in_refs, out_refs, sems):
        for cp in copies(in_refs, out_refs, sems):
            cp.wait()

    return _Rider(g4s, [jax.ShapeDtypeStruct((4,) + g.shape[2:], g.dtype) for g in g4s],
                  [pltpu.SemaphoreType.DMA((4, n)), pltpu.SemaphoreType.DMA((4, n))], start, finish)


def _chips_rider(pbs):
    n = len(pbs)

    def copies(in_refs, out_refs, sems):
        send_sems, recv_sems = sems
        x, y, c = _place()
        peers = [(1 - x, y), (x, 1 - y), (1 - x, 1 - y)]
        return [pltpu.make_async_remote_copy(
            src_ref=in_refs[a].at[2 * px + py], dst_ref=out_refs[a].at[k], send_sem=send_sems.at[k, a],
            recv_sem=recv_sems.at[k, a], device_id=(px, py, c), device_id_type=MESH)
            for a in range(n) for k, (px, py) in enumerate(peers)]

    def start(in_refs, out_refs, sems):
        for cp in copies(in_refs, out_refs, sems):
            cp.start()

    def finish(in_refs, out_refs, sems):
        for cp in copies(in_refs, out_refs, sems):
            cp.wait()

    return _Rider(pbs, [jax.ShapeDtypeStruct((3,) + p.shape[1:], p.dtype) for p in pbs],
                  [pltpu.SemaphoreType.DMA((3, n)), pltpu.SemaphoreType.DMA((3, n))], start, finish)


def _exchange(rider, name):
    n_in, n_out = len(rider.arrays), len(rider.out_shape)

    def body(*refs):
        ins, outs, sems = refs[:n_in], refs[n_in:n_in + n_out], refs[n_in + n_out:]
        rider.start(ins, outs, sems)
        rider.finish(ins, outs, sems)

    hbm = pl.BlockSpec(memory_space=pltpu.HBM)
    return pl.pallas_call(body, name=name, out_shape=rider.out_shape, in_specs=[hbm] * n_in,
                          out_specs=[hbm] * n_out, scratch_shapes=rider.scratch)(*rider.arrays)


def _rs_pair_sum(place, g4, r1, name):
    rows, cols = g4.shape[2:]
    tr = min(rows, 512)

    def body(pl_ref, g_ref, r_ref, pb_ref, own_ref):
        s = g_ref[0, 0] + r_ref[0]
        pb_ref[0] = s.astype(BF16)

        @pl.when(pl.program_id(1) == pl_ref[0])
        def _():
            own_ref[...] = s

    grid_spec = pltpu.PrefetchScalarGridSpec(
        num_scalar_prefetch=1, grid=(rows // tr, 4),
        in_specs=[pl.BlockSpec((1, 1, tr, cols), lambda r, k, p: (k, p[1], r, 0)),
                  pl.BlockSpec((1, tr, cols), lambda r, k, p: (k, r, 0))],
        out_specs=[pl.BlockSpec((1, tr, cols), lambda r, k, p: (k, r, 0)),
                   pl.BlockSpec((tr, cols), lambda r, k, p: (r, 0))])
    return pl.pallas_call(
        body, name=name, grid_spec=grid_spec,
        out_shape=[jax.ShapeDtypeStruct((4, rows, cols), BF16), jax.ShapeDtypeStruct((rows, cols), F32)],
        compiler_params=_cparams(("arbitrary", "arbitrary")))(place, g4, r1)


def _small_all_reduce(vec):
    m_per = vec.shape[0]

    def body(x_ref, all_ref, sum_ref, send_sems, recv_sems, local_sem):
        x, y, c = _place()
        me, sibling = (x, y, c), (x, y, 1 - c)
        chips = [(1 - x, y), (x, 1 - y), (1 - x, 1 - y)]

        def rows(px, py, pc):
            return all_ref.at[4 * px + 2 * py + pc]

        def copy(k, block, to, src=None):
            return pltpu.make_async_remote_copy(
                src_ref=rows(*block) if src is None else src, dst_ref=rows(*block),
                send_sem=send_sems.at[k], recv_sem=recv_sems.at[k], device_id=to, device_id_type=MESH)

        mine = pltpu.make_async_copy(x_ref, rows(*me), local_sem)
        mine.start()
        first = [copy(0, me, sibling, src=x_ref)]
        first += [copy(1 + j, me, (*chip, c), src=x_ref) for j, chip in enumerate(chips)]
        for cp in first:
            cp.start()
        passed = [copy(4 + j, (*chip, c), sibling) for j, chip in enumerate(chips)]
        for j, chip in enumerate(chips):
            copy(1 + j, (*chip, c), me).wait_recv()
            passed[j].start()
        copy(0, sibling, me).wait_recv()
        for j, chip in enumerate(chips):
            copy(4 + j, (*chip, 1 - c), me).wait_recv()
        for cp in first + passed:
            cp.wait_send()
        mine.wait()
        total = all_ref[0]
        for d in range(1, NDEV):
            total = total + all_ref[d]
        sum_ref[...] = total

    return pl.pallas_call(
        body, name="small_all_reduce",
        out_shape=[jax.ShapeDtypeStruct((NDEV, m_per, 128), F32), jax.ShapeDtypeStruct((m_per, 128), F32)],
        in_specs=[pl.BlockSpec(memory_space=pltpu.VMEM)],
        out_specs=[pl.BlockSpec(memory_space=pltpu.VMEM), pl.BlockSpec(memory_space=pltpu.VMEM)],
        scratch_shapes=[pltpu.SemaphoreType.DMA((7,)), pltpu.SemaphoreType.DMA((7,)), pltpu.SemaphoreType.DMA],
    )(vec)[1]


def _adamw_math(w, g, m, v):
    m = ADAM_B1 * m + (1.0 - ADAM_B1) * g
    v = ADAM_B2 * v + (1.0 - ADAM_B2) * (g * g)
    m_hat = m / (1.0 - ADAM_B1 ** ADAM_STEP)
    v_hat = v / (1.0 - ADAM_B2 ** ADAM_STEP)
    delta = -ADAM_LR * (m_hat / (jnp.sqrt(v_hat) + ADAM_EPS) + ADAM_WD * w)
    return delta, m, v


def _adamw(w, own, r2, m, v, name):
    _, r, c = w.shape
    tr = min(r, 256)

    def body(w_ref, o_ref, r_ref, m_ref, v_ref, g_ref, d_ref, nm_ref, nv_ref):
        g = ((o_ref[...] + r_ref[0].astype(F32)) + r_ref[1].astype(F32)) + r_ref[2].astype(F32)
        g_ref[...] = g
        d_ref[...], nm_ref[...], nv_ref[...] = _adamw_math(w_ref[...], g, m_ref[...], v_ref[...])

    spec = pl.BlockSpec((None, tr, c), lambda i: (0, i, 0))
    return pl.pallas_call(
        body, name=name, grid=(r // tr,),
        in_specs=[spec, pl.BlockSpec((tr, c), lambda i: (i, 0)), pl.BlockSpec((3, tr, c), lambda i: (0, i, 0)),
                  spec, spec],
        out_specs=[spec] * 4, out_shape=[jax.ShapeDtypeStruct((1, r, c), F32)] * 4,
        compiler_params=_cparams(("arbitrary",)))(w, own, r2, m, v)


def _adamw_small(ws, gs, ms, vs):
    n = len(ws)

    def body(*refs):
        ins, outs = refs[:4 * n], refs[4 * n:]
        for i in range(n):
            d, m, v = _adamw_math(ins[i][...], ins[n + i][...], ins[2 * n + i][...], ins[3 * n + i][...])
            outs[i][...], outs[n + i][...], outs[2 * n + i][...] = d, m, v

    shapes = [jax.ShapeDtypeStruct(w.shape, F32) for w in ws]
    outs = pl.pallas_call(body, name="adamw_small", out_shape=shapes * 3)(*ws, *gs, *ms, *vs)
    return outs[:n], outs[n:2 * n], outs[2 * n:]


MATS = ("w_in", "w_out", "w_xq", "w_xo", "w_xkv", "w_up", "w_down")
SMALL = ("mix_norm", "conv_w", "conv_norm", "w_af", "b_af", "w_ab", "b_ab", "gla_norm", "xa_norm", "mem_norm",
         "mlp_norm", "final_norm")
WEIGHTS = ("mix_norm", "w_in", "conv_w", "conv_norm", "w_af", "b_af", "w_ab", "b_ab", "gla_norm", "w_out", "xa_norm",
           "mem_norm", "w_xq", "w_xkv", "w_xo", "mlp_norm", "w_up", "w_down", "final_norm")
COL_SHARDED = ("w_in", "w_xkv", "w_up")
SMALL_SHARDED = {"conv_w": (3, 64), "w_af": (LR, 32), "w_ab": (LR, 32)}
SMALL_PACK_ROWS = 16


def kernel(x, mem, mix_norm, w_in, conv_w, conv_norm, w_af, b_af, w_ab, b_ab, gla_norm, w_out, xa_norm, mem_norm, w_xq, w_xkv, w_xo, mlp_norm, w_up, w_down, final_norm, loss_target, m_mix_norm, m_w_in, m_conv_w, m_conv_norm, m_w_af, m_b_af, m_w_ab, m_b_ab, m_gla_norm, m_w_out, m_xa_norm, m_mem_norm, m_w_xq, m_w_xkv, m_w_xo, m_mlp_norm, m_w_up, m_w_down, m_final_norm, v_mix_norm, v_w_in, v_conv_w, v_conv_norm, v_w_af, v_b_af, v_w_ab, v_b_ab, v_gla_norm, v_w_out, v_xa_norm, v_mem_norm, v_w_xq, v_w_xkv, v_w_xo, v_mlp_norm, v_w_up, v_w_down, v_final_norm):
    w = dict(mix_norm=mix_norm, w_in=w_in, conv_w=conv_w, conv_norm=conv_norm, w_af=w_af, b_af=b_af, w_ab=w_ab,
             b_ab=b_ab, gla_norm=gla_norm, w_out=w_out, xa_norm=xa_norm, mem_norm=mem_norm, w_xq=w_xq, w_xkv=w_xkv,
             w_xo=w_xo, mlp_norm=mlp_norm, w_up=w_up, w_down=w_down, final_norm=final_norm)
    mom = dict(mix_norm=m_mix_norm, w_in=m_w_in, conv_w=m_conv_w, conv_norm=m_conv_norm, w_af=m_w_af, b_af=m_b_af,
               w_ab=m_w_ab, b_ab=m_b_ab, gla_norm=m_gla_norm, w_out=m_w_out, xa_norm=m_xa_norm, mem_norm=m_mem_norm,
               w_xq=m_w_xq, w_xkv=m_w_xkv, w_xo=m_w_xo, mlp_norm=m_mlp_norm, w_up=m_w_up, w_down=m_w_down,
               final_norm=m_final_norm)
    var = dict(mix_norm=v_mix_norm, w_in=v_w_in, conv_w=v_conv_w, conv_norm=v_conv_norm, w_af=v_w_af, b_af=v_b_af,
               w_ab=v_w_ab, b_ab=v_b_ab, gla_norm=v_gla_norm, w_out=v_w_out, xa_norm=v_xa_norm, mem_norm=v_mem_norm,
               w_xq=v_w_xq, w_xkv=v_w_xkv, w_xo=v_w_xo, mlp_norm=v_mlp_norm, w_up=v_w_up, w_down=v_w_down,
               final_norm=v_final_norm)
    xi, yi, ci = _place()
    me = 4 * xi + 2 * yi + ci
    two_d = lambda a: a.reshape(a.shape[-2:]) if a.ndim == 3 else a.reshape(1, a.shape[-1])

    small = jnp.concatenate([w[n].reshape(-1) for n in SMALL_SHARDED])
    small = jnp.pad(small, (0, SMALL_PACK_ROWS * 128 - small.shape[0])).reshape(SMALL_PACK_ROWS, 128)
    shard = {n: two_d(w[n]).astype(BF16) for n in MATS}
    vec = {n: two_d(w[n]) for n in SMALL if n not in SMALL_SHARDED}
    place = jnp.stack([2 * xi + yi, ci]).astype(jnp.int32)
    loss_part, grad_x, grads, own, from_chips = _step(x[0], mem[0], loss_target[0], shard, small, vec, place)

    order = [n for n in SMALL if n not in SMALL_SHARDED] + list(SMALL_SHARDED)
    flat = jnp.concatenate([grads[n].reshape(-1) for n in order] + [loss_part.reshape(-1)])
    n_flat = flat.shape[0]
    tot = _small_all_reduce(jnp.pad(flat, (0, SMALL_ROWS * 128 - n_flat)).reshape(SMALL_ROWS, 128)).reshape(-1)
    gsmall, off = {}, 0
    for n in order:
        size = grads[n].size
        full = tot[off:off + size].reshape(grads[n].shape)
        off += size
        if n in SMALL_SHARDED:
            r, c = SMALL_SHARDED[n]
            full = lax.dynamic_slice_in_dim(full, me * c, c, axis=1)
        gsmall[n] = full
    loss = tot[off]

    out_g, out_d, out_m, out_v = {}, {}, {}, {}
    for n in MATS:
        out_g[n], out_d[n], out_m[n], out_v[n] = _adamw(w[n], own[n], from_chips[n], mom[n], var[n], "adamw_" + n)
    ds, nms, nvs = _adamw_small([two_d(w[n]) for n in SMALL], [gsmall[n] for n in SMALL],
                                [two_d(mom[n]) for n in SMALL], [two_d(var[n]) for n in SMALL])
    for i, n in enumerate(SMALL):
        out_g[n], out_d[n], out_m[n], out_v[n] = [a.reshape(w[n].shape) for a in (gsmall[n], ds[i], nms[i], nvs[i])]

    return (loss, grad_x[None], *[out_g[n] for n in WEIGHTS], *[out_d[n] for n in WEIGHTS],
            *[out_m[n] for n in WEIGHTS], *[out_v[n] for n in WEIGHTS])
```

```python
import functools

import jax
import jax.numpy as jnp
from jax import lax
from jax.experimental import pallas as pl
from jax.experimental.pallas import tpu as pltpu

F32 = jnp.float32
BF16 = jnp.bfloat16

D = 1024
CW = 512
GK = 256
GV = 512
NH = 4
CH = 64
LR = 16
NMEM = 256
XD = 256
FF = 4096
ZW = 3104
ZC = 3200
EPS = 1e-6
NDEV = 8

ZB_CB, ZB_CC, ZB_CU, ZB_V, ZB_G = 0, 1, 2, 4, 5
ZB_Q, ZB_K = 6, 7
ZB_LR = 24

TM = 512
TM_MLP = 256
TF = 512
TB = 256
TT = 2048
VMEM_LIMIT = 56 * 1024 * 1024

ADAM_LR, ADAM_B1, ADAM_B2, ADAM_EPS, ADAM_WD, ADAM_STEP = 0.001, 0.9, 0.999, 1e-08, 0.01, 10

WIN_SHARD, XKV_SHARD, UP_SHARD = ZW // NDEV, 2 * D // NDEV, FF // NDEV
SMALL_ROWS = 128

MESH = pl.DeviceIdType.MESH


def _cparams(sem):
    return pltpu.CompilerParams(dimension_semantics=sem, vmem_limit_bytes=VMEM_LIMIT)


def _call(body, name, grid, in_specs, out_specs, out_shape, scratch, args, riders=()):
    n_in, n_out, n_scr = len(in_specs), len(out_specs), len(scratch)
    counts = [(len(r.arrays), len(r.out_shape), len(r.scratch)) for r in riders]

    def take(refs, pos, sizes):
        groups = []
        for size in sizes:
            groups.append(refs[pos:pos + size])
            pos += size
        return groups, pos

    def wrapped(*refs):
        ins, pos = refs[:n_in], n_in
        r_ins, pos = take(refs, pos, [c[0] for c in counts])
        outs, pos = refs[pos:pos + n_out], pos + n_out
        r_outs, pos = take(refs, pos, [c[1] for c in counts])
        scr, pos = refs[pos:pos + n_scr], pos + n_scr
        r_scr, pos = take(refs, pos, [c[2] for c in counts])
        ids = [pl.program_id(d) for d in range(len(grid))]
        first = functools.reduce(lambda a, b: a & b, [i == 0 for i in ids])
        last = functools.reduce(lambda a, b: a & b, [i == g - 1 for i, g in zip(ids, grid)])

        @pl.when(first)
        def _():
            for r, a, b, c in zip(riders, r_ins, r_outs, r_scr):
                r.start(a, b, c)

        body(*ins, *outs, *scr)

        @pl.when(last)
        def _():
            for r, a, b, c in zip(riders, r_ins, r_outs, r_scr):
                r.finish(a, b, c)

    hbm = pl.BlockSpec(memory_space=pltpu.HBM)
    r_args = [a for r in riders for a in r.arrays]
    r_shapes = [s for r in riders for s in r.out_shape]
    return pl.pallas_call(
        wrapped if riders else body, name=name, grid=grid, in_specs=list(in_specs) + [hbm] * len(r_args),
        out_specs=list(out_specs) + [hbm] * len(r_shapes), out_shape=list(out_shape) + r_shapes,
        scratch_shapes=list(scratch) + [s for r in riders for s in r.scratch],
        compiler_params=_cparams(("arbitrary",) * len(grid)))(*args, *r_args)


def _dot(a, b):
    return jnp.dot(a.astype(BF16), b.astype(BF16), preferred_element_type=F32)


def _dot_nt(a, b):
    return lax.dot_general(a.astype(BF16), b.astype(BF16), (((1,), (1,)), ((), ())), preferred_element_type=F32)


def _dot_tn(a, b):
    return lax.dot_general(a.astype(BF16), b.astype(BF16), (((0,), (0,)), ((), ())), preferred_element_type=F32)


def _split(x, n):
    parts = []
    for _ in range(n):
        p = x.astype(BF16)
        parts.append(p)
        x = x - p.astype(F32)
    return parts


def _dot_exact_lhs(m, x, n):
    return functools.reduce(lambda a, b: a + b, [jnp.dot(m, p, preferred_element_type=F32) for p in _split(x, n)])


def _dot_exact_rhs(x, m, n):
    return functools.reduce(lambda a, b: a + b, [jnp.dot(p, m, preferred_element_type=F32) for p in _split(x, n)])


def _rms(x, g):
    r = lax.rsqrt(jnp.mean(x * x, axis=-1, keepdims=True) + EPS)
    return x * r * g, r


def _rms_bwd(x, r, g, dy):
    xr = x * r
    u = dy * g
    dx = r * (u - xr * jnp.mean(u * xr, axis=-1, keepdims=True))
    return dx, jnp.sum(dy * xr, axis=0, keepdims=True)


def _iota(shape, dim):
    return lax.broadcasted_iota(jnp.int32, shape, dim)


def _sigmoid(x):
    return 1.0 / (1.0 + jnp.exp(-x))


def _acc_rows(ref, row):
    ref[...] += jnp.broadcast_to(row, ref.shape)


def _inproj(x, g, w, riders=()):
    t = x.shape[0]
    tm = min(TM, t)

    def body(x_ref, g_ref, w_ref, z_ref, h_ref):
        h, _ = _rms(x_ref[...], g_ref[...])
        hb = h.astype(BF16)
        h_ref[...] = hb
        z_ref[...] = jnp.dot(hb, w_ref[...], preferred_element_type=F32)

    return _call(
        body, "inproj", (t // tm,),
        [pl.BlockSpec((tm, D), lambda i: (i, 0)), pl.BlockSpec((1, D), lambda i: (0, 0)),
         pl.BlockSpec((D, ZC), lambda i: (0, 0))],
        [pl.BlockSpec((tm, ZC), lambda i: (i, 0)), pl.BlockSpec((tm, D), lambda i: (i, 0))],
        [jax.ShapeDtypeStruct((t, ZC), F32), jax.ShapeDtypeStruct((t, D), BF16)], [], (x, g, w), riders)


def _kv_proj(mem, g, w):
    def body(m_ref, g_ref, w_ref, kv_ref, mn_ref):
        mn, _ = _rms(m_ref[...], g_ref[...])
        mb = mn.astype(BF16)
        mn_ref[...] = mb
        for j in range(NDEV):
            kv_ref[:, j * XKV_SHARD:(j + 1) * XKV_SHARD] = jnp.dot(mb, w_ref[j], preferred_element_type=F32)

    return pl.pallas_call(
        body, name="kv_proj",
        out_shape=[jax.ShapeDtypeStruct((NMEM, 2 * D), F32), jax.ShapeDtypeStruct((NMEM, D), BF16)],
        compiler_params=pltpu.CompilerParams(vmem_limit_bytes=VMEM_LIMIT))(mem, g, w)


def _softmax_head(qb, kb):
    s = _dot_nt(qb, kb) * (1.0 / 16.0)
    e = jnp.exp(s - jnp.max(s, axis=-1, keepdims=True))
    return e / jnp.sum(e, axis=-1, keepdims=True)


def _attn_fwd(x, yb, w_out, g, w_xq, kb, vb, w_xo):
    t = x.shape[0]
    tm = min(TM, t)

    def body(x_ref, y_ref, wo_ref, g_ref, wq_ref, k_ref, v_ref, wx_ref, x1_ref, x2_ref, xn_ref, q_ref, a_ref):
        x1 = x_ref[...] + jnp.dot(y_ref[...], wo_ref[...], preferred_element_type=F32)
        x1_ref[...] = x1
        xn, _ = _rms(x1, g_ref[...])
        xb = xn.astype(BF16)
        xn_ref[...] = xb
        qb = jnp.dot(xb, wq_ref[...], preferred_element_type=F32).astype(BF16)
        q_ref[...] = qb
        for h in range(NH):
            hs = slice(h * XD, (h + 1) * XD)
            p = _softmax_head(qb[:, hs], k_ref[:, hs])
            a_ref[:, hs] = _dot(p, v_ref[:, hs]).astype(BF16)
        x2_ref[...] = x1 + jnp.dot(a_ref[...], wx_ref[...], preferred_element_type=F32)

    tok = lambda i: (i, 0)
    full = lambda i: (0, 0)
    return pl.pallas_call(
        body, name="attn_fwd", grid=(t // tm,),
        in_specs=[pl.BlockSpec((tm, D), tok), pl.BlockSpec((tm, D), tok), pl.BlockSpec((D, D), full),
                  pl.BlockSpec((1, D), full), pl.BlockSpec((D, D), full), pl.BlockSpec((NMEM, D), full),
                  pl.BlockSpec((NMEM, D), full), pl.BlockSpec((D, D), full)],
        out_specs=[pl.BlockSpec((tm, D), tok)] * 5,
        out_shape=[jax.ShapeDtypeStruct((t, D), F32), jax.ShapeDtypeStruct((t, D), F32),
                   jax.ShapeDtypeStruct((t, D), BF16), jax.ShapeDtypeStruct((t, D), BF16),
                   jax.ShapeDtypeStruct((t, D), BF16)],
        compiler_params=_cparams(("arbitrary",)))(x, yb, w_out, g, w_xq, kb, vb, w_xo)


def _mlp_fwd(x2, g, w_up_t, w_down, fg, target):
    t = x2.shape[0]
    tm = min(TM_MLP, t)

    def body(x_ref, g_ref, wu_ref, wd_ref, fg_ref, t_ref, h1_ref, xn_ref, dx_ref, dxb_ref, loss_ref, dfg_ref, ab):
        @pl.when(pl.program_id(0) == 0)
        def _():
            loss_ref[...] = jnp.zeros_like(loss_ref)
            dfg_ref[...] = jnp.zeros_like(dfg_ref)

        x = x_ref[...]
        xn, _ = _rms(x, g_ref[...])
        xnb = xn.astype(BF16)
        xn_ref[...] = xnb
        for q in range(FF // TF):
            cols = slice(q * TF, (q + 1) * TF)
            h1 = _dot_nt(xnb, wu_ref[cols, :])
            h1_ref[:, cols] = h1.astype(BF16)
            hr = jnp.maximum(h1, 0.0)
            ab[:, cols] = (hr * hr).astype(BF16)
        x3 = x + jnp.dot(ab[...], wd_ref[...], preferred_element_type=F32)
        y, r = _rms(x3, fg_ref[...])
        e = y - t_ref[...]
        row = jnp.mean(e * e, axis=-1, keepdims=True)
        _acc_rows(loss_ref, 0.5 * jnp.sum(row, axis=0, keepdims=True))
        dx, dfg = _rms_bwd(x3, r, fg_ref[...], e * (1.0 / D))
        dx_ref[...] = dx
        dxb_ref[...] = dx.astype(BF16)
        _acc_rows(dfg_ref, dfg)

    tok = lambda i: (i, 0)
    full = lambda i: (0, 0)
    once = pl.Buffered(1)
    return pl.pallas_call(
        body, name="mlp_fwd", grid=(t // tm,),
        in_specs=[pl.BlockSpec((tm, D), tok), pl.BlockSpec((1, D), full),
                  pl.BlockSpec((FF, D), full, pipeline_mode=once), pl.BlockSpec((FF, D), full, pipeline_mode=once),
                  pl.BlockSpec((1, D), full), pl.BlockSpec((tm, D), tok)],
        out_specs=[pl.BlockSpec((tm, FF), tok), pl.BlockSpec((tm, D), tok), pl.BlockSpec((tm, D), tok),
                   pl.BlockSpec((tm, D), tok), pl.BlockSpec((8, 128), full), pl.BlockSpec((8, D), full)],
        out_shape=[jax.ShapeDtypeStruct((t, FF), BF16), jax.ShapeDtypeStruct((t, D), BF16),
                   jax.ShapeDtypeStruct((t, D), F32), jax.ShapeDtypeStruct((t, D), BF16),
                   jax.ShapeDtypeStruct((8, 128), F32), jax.ShapeDtypeStruct((8, D), F32)],
        scratch_shapes=[pltpu.VMEM((tm, FF), BF16)],
        compiler_params=_cparams(("arbitrary",)))(x2, g, w_up_t, w_down, fg, target)


def _mlp_bwd(dx3, dx3b, h1b, w_down, w_up_t, x2, g):
    t = x2.shape[0]
    tm = min(TM_MLP, t)

    def body(dx_ref, dxb_ref, h1_ref, wd_ref, wu_ref, x_ref, g_ref, a_ref, dh_ref, dx2_ref, dx2b_ref, dg_ref):
        @pl.when(pl.program_id(0) == 0)
        def _():
            dg_ref[...] = jnp.zeros_like(dg_ref)

        for q in range(FF // TF):
            cols = slice(q * TF, (q + 1) * TF)
            hr = jnp.maximum(h1_ref[:, cols].astype(F32), 0.0)
            da = _dot_nt(dxb_ref[...], wd_ref[cols, :])
            a_ref[:, cols] = (hr * hr).astype(BF16)
            dh_ref[:, cols] = (da * 2.0 * hr).astype(BF16)
        dxn = jnp.dot(dh_ref[...], wu_ref[...], preferred_element_type=F32)
        x = x_ref[...]
        r = lax.rsqrt(jnp.mean(x * x, axis=-1, keepdims=True) + EPS)
        dx, dg = _rms_bwd(x, r, g_ref[...], dxn)
        dx2 = dx_ref[...] + dx
        dx2_ref[...] = dx2
        dx2b_ref[...] = dx2.astype(BF16)
        _acc_rows(dg_ref, dg)

    tok = lambda i: (i, 0)
    full = lambda i: (0, 0)
    once = pl.Buffered(1)
    return pl.pallas_call(
        body, name="mlp_bwd", grid=(t // tm,),
        in_specs=[pl.BlockSpec((tm, D), tok), pl.BlockSpec((tm, D), tok), pl.BlockSpec((tm, FF), tok),
                  pl.BlockSpec((FF, D), full, pipeline_mode=once), pl.BlockSpec((FF, D), full, pipeline_mode=once),
                  pl.BlockSpec((tm, D), tok), pl.BlockSpec((1, D), full)],
        out_specs=[pl.BlockSpec((tm, FF), tok), pl.BlockSpec((tm, FF), tok), pl.BlockSpec((tm, D), tok),
                   pl.BlockSpec((tm, D), tok), pl.BlockSpec((8, D), full)],
        out_shape=[jax.ShapeDtypeStruct((t, FF), BF16), jax.ShapeDtypeStruct((t, FF), BF16),
                   jax.ShapeDtypeStruct((t, D), F32), jax.ShapeDtypeStruct((t, D), BF16),
                   jax.ShapeDtypeStruct((8, D), F32)],
        compiler_params=_cparams(("arbitrary",)))(dx3, dx3b, h1b, w_down, w_up_t, x2, g)


def _attn_bwd(x1, dx2, dx2b, qb, kb, vb, w_xo, w_xq, w_out, g):
    t = x1.shape[0]
    tm = min(TM, t)

    def body(x_ref, dx2_ref, dx2b_ref, q_ref, k_ref, v_ref, wx_ref, wq_ref, wo_ref, g_ref,
             dx1_ref, dx1b_ref, dy_ref, dq_ref, dkv_ref, dg_ref):
        @pl.when(pl.program_id(0) == 0)
        def _():
            dkv_ref[...] = jnp.zeros_like(dkv_ref)
            dg_ref[...] = jnp.zeros_like(dg_ref)

        datt = _dot_nt(dx2b_ref[...], wx_ref[...]).astype(BF16)
        for h in range(NH):
            hs = slice(h * XD, (h + 1) * XD)
            q_h, k_h, v_h, da_h = q_ref[:, hs], k_ref[:, hs], v_ref[:, hs], datt[:, hs]
            p = _softmax_head(q_h, k_h)
            dp = _dot_nt(da_h, v_h)
            ds = (p * (dp - jnp.sum(dp * p, axis=-1, keepdims=True)) * (1.0 / 16.0)).astype(BF16)
            dq_ref[:, hs] = _dot(ds, k_h).astype(BF16)
            dkv_ref[:, hs] += _dot_tn(ds, q_h)
            dkv_ref[:, D + h * XD:D + (h + 1) * XD] += _dot_tn(p, da_h)
        dxn = _dot_nt(dq_ref[...], wq_ref[...])
        x = x_ref[...]
        r = lax.rsqrt(jnp.mean(x * x, axis=-1, keepdims=True) + EPS)
        dx, dg = _rms_bwd(x, r, g_ref[...], dxn)
        dx1 = dx2_ref[...] + dx
        dx1_ref[...] = dx1
        dx1b = dx1.astype(BF16)
        dx1b_ref[...] = dx1b
        dy_ref[...] = _dot_nt(dx1b, wo_ref[...])
        _acc_rows(dg_ref, dg)

    tok = lambda i: (i, 0)
    full = lambda i: (0, 0)
    return pl.pallas_call(
        body, name="attn_bwd", grid=(t // tm,),
        in_specs=[pl.BlockSpec((tm, D), tok), pl.BlockSpec((tm, D), tok), pl.BlockSpec((tm, D), tok),
                  pl.BlockSpec((tm, D), tok), pl.BlockSpec((NMEM, D), full), pl.BlockSpec((NMEM, D), full),
                  pl.BlockSpec((D, D), full), pl.BlockSpec((D, D), full), pl.BlockSpec((D, D), full),
                  pl.BlockSpec((1, D), full)],
        out_specs=[pl.BlockSpec((tm, D), tok), pl.BlockSpec((tm, D), tok), pl.BlockSpec((tm, D), tok),
                   pl.BlockSpec((tm, D), tok), pl.BlockSpec((NMEM, 2 * D), full), pl.BlockSpec((8, D), full)],
        out_shape=[jax.ShapeDtypeStruct((t, D), F32), jax.ShapeDtypeStruct((t, D), BF16),
                   jax.ShapeDtypeStruct((t, D), F32), jax.ShapeDtypeStruct((t, D), BF16),
                   jax.ShapeDtypeStruct((NMEM, 2 * D), F32), jax.ShapeDtypeStruct((8, D), F32)],
        compiler_params=_cparams(("arbitrary",)))(x1, dx2, dx2b, qb, kb, vb, w_xo, w_xq, w_out, g)


def _kv_bwd(dkv, memn, mem, g, w):
    def body(dkv_ref, mn_ref, m_ref, g_ref, w_ref, dw_ref, dg_ref):
        dkvb = dkv_ref[...].astype(BF16)
        dmn = jnp.zeros((NMEM, D), F32)
        for j in range(NDEV):
            cols = slice(j * XKV_SHARD, (j + 1) * XKV_SHARD)
            dw_ref[j] = _dot_tn(mn_ref[...], dkvb[:, cols])
            dmn += _dot_nt(dkvb[:, cols], w_ref[j])
        m = m_ref[...]
        r = lax.rsqrt(jnp.mean(m * m, axis=-1, keepdims=True) + EPS)
        dg_ref[...] = jnp.broadcast_to(jnp.sum(dmn * m * r, axis=0, keepdims=True), dg_ref.shape)

    return pl.pallas_call(
        body, name="kv_bwd",
        out_shape=[jax.ShapeDtypeStruct((NDEV, D, XKV_SHARD), F32), jax.ShapeDtypeStruct((8, D), F32)],
        compiler_params=pltpu.CompilerParams(vmem_limit_bytes=VMEM_LIMIT))(dkv, memn, mem, g, w)


def _inproj_bwd(dz, w, x, dx1, g, riders=()):
    t = x.shape[0]
    tm = min(TM, t)

    def body(dz_ref, w_ref, x_ref, dx1_ref, g_ref, gx_ref, dg_ref):
        @pl.when(pl.program_id(0) == 0)
        def _():
            dg_ref[...] = jnp.zeros_like(dg_ref)

        dh = _dot_nt(dz_ref[...], w_ref[...])
        x = x_ref[...]
        r = lax.rsqrt(jnp.mean(x * x, axis=-1, keepdims=True) + EPS)
        dx, dg = _rms_bwd(x, r, g_ref[...], dh)
        gx_ref[...] = dx1_ref[...] + dx
        _acc_rows(dg_ref, dg)

    tok = lambda i: (i, 0)
    full = lambda i: (0, 0)
    return _call(
        body, "inproj_bwd", (t // tm,),
        [pl.BlockSpec((tm, ZC), tok), pl.BlockSpec((D, ZC), full), pl.BlockSpec((tm, D), tok),
         pl.BlockSpec((tm, D), tok), pl.BlockSpec((1, D), full)],
        [pl.BlockSpec((tm, D), tok), pl.BlockSpec((8, D), full)],
        [jax.ShapeDtypeStruct((t, D), F32), jax.ShapeDtypeStruct((8, D), F32)], [], (dz, w, x, dx1, g), riders)


def _matmul_tn(a, b, name, riders=()):
    t, k = a.shape
    n = b.shape[1]
    tk = min(k, 1024)
    tn = 640 if n % 1024 else 1024
    tt = min(TT, t)

    def body(a_ref, b_ref, o_ref):
        @pl.when(pl.program_id(2) == 0)
        def _():
            o_ref[...] = jnp.zeros_like(o_ref)

        o_ref[...] += _dot_tn(a_ref[...], b_ref[...])

    return _call(
        body, name, (k // tk, n // tn, t // tt),
        [pl.BlockSpec((tt, tk), lambda i, j, s: (s, i)), pl.BlockSpec((tt, tn), lambda i, j, s: (s, j))],
        [pl.BlockSpec((tk, tn), lambda i, j, s: (i, j))], [jax.ShapeDtypeStruct((k, n), F32)], [], (a, b), riders)


def _lane_head(shape, dim, shift):
    return _iota(shape, dim) >> shift


def _gla_recompute(q_raw, k, lr, wpad, bias, rev, tb):
    pre = _dot(lr, wpad) + bias
    la = (jnp.minimum(pre, 0.0) - jnp.log(1.0 + jnp.exp(-jnp.abs(pre)))) * (1.0 / 16.0)
    r, c = _iota((tb, tb), 0), _iota((tb, tb), 1)
    tri = (c >= r) if rev else (c <= r)
    cum = jnp.where(((r >> 6) == (c >> 6)) & tri, 1.0, 0.0).astype(BF16)
    b = _dot_exact_lhs(cum, la, 3)
    e, ei = jnp.exp(b), jnp.exp(-b)
    qt = (q_raw * 0.125) * e
    kt = k * ei
    return pre, b, e, ei, qt, kt


def _stack_heads(x, shift):
    head = _lane_head(x.shape, 1, shift)
    return jnp.concatenate([jnp.where(head == h, x, 0.0) for h in range(NH)], axis=0).astype(BF16)


def _fold_heads(x, shift):
    head = _lane_head((CH, x.shape[1]), 1, shift)
    return functools.reduce(lambda a, b: a + b,
                            [jnp.where(head == h, x[h * CH:(h + 1) * CH], 0.0) for h in range(NH)])


def _wide_mask(rev):
    r, s = _iota((CH, NH * CH), 0), _iota((CH, NH * CH), 1) & (CH - 1)
    return (s >= r) if rev else (s <= r)


def _state_mask():
    return (_iota((GV, GK), 0) >> 7) == (_iota((GV, GK), 1) >> 6)


def _state_expand(sd):
    head = _lane_head(sd.shape, 1, 6)
    return jnp.concatenate([jnp.where(head == h, sd, 0.0) for h in range(NH)], axis=0)


def _conv_parts(cb, cc, cu, ccp, cup, ccn, cun, cw_ref, first, last, tb):
    h = cc * cu
    hp = jnp.where(first, 0.0, ccp * cup)
    hn = jnp.where(last, 0.0, ccn * cun)
    rows = _iota(h.shape, 0)
    h_m1 = jnp.where(rows == 0, hp, pltpu.roll(h, 1, 0))
    h_p1 = jnp.where(rows == tb - 1, hn, pltpu.roll(h, tb - 1, 0))
    conv = cw_ref[pl.ds(0, 1), :] * h_m1 + cw_ref[pl.ds(1, 1), :] * h + cw_ref[pl.ds(2, 1), :] * h_p1
    return h, h_m1, h_p1, conv


def _group_ones():
    return jnp.where((_iota((CW, CW), 0) >> 6) == (_iota((CW, CW), 1) >> 6), 1.0, 0.0).astype(BF16)


def _head_norm(o):
    ons, rs = [], []
    for h in range(NH):
        slab = o[:, h * 128:(h + 1) * 128]
        r = lax.rsqrt(jnp.mean(slab * slab, axis=-1, keepdims=True) + EPS)
        ons.append(slab * r)
        rs.append(jnp.broadcast_to(r, slab.shape))
    return jnp.concatenate(ons, axis=1), jnp.concatenate(rs, axis=1)


def _zspec(tb, width, blk, jmap):
    return pl.BlockSpec((tb, width), lambda i: (jmap(i), blk))


def _halo_specs(tb, nblk, t, blk, jmap):
    prev = pl.BlockSpec((8, CW), lambda i: (jnp.maximum(jmap(i) * (tb // 8) - 1, 0), blk))
    nxt = pl.BlockSpec((8, CW), lambda i: (jnp.minimum((jmap(i) + 1) * (tb // 8), t // 8 - 1), blk))
    return prev, nxt


def _gla_fwd_sweep(z, wpad, bias, rev, finish_args=None, riders=()):
    t = z.shape[0]
    tb = min(TB, t)
    nblk, nb = t // tb, tb // CH
    jmap = (lambda i: nblk - 1 - i) if rev else (lambda i: i)
    finish = finish_args is not None

    def body(*refs):
        if finish:
            (q_ref, k_ref, v_ref, lr_ref, w_ref, bias_ref, of_ref, g_ref, cb_ref, cc_ref, cu_ref, ccp_ref, ccn_ref,
             cup_ref, cun_ref, cw_ref, cn_ref, gn_ref, y_ref, opre_ref, sd_ref, st, b_scr, o_scr) = refs
        else:
            q_ref, k_ref, v_ref, lr_ref, w_ref, bias_ref, o_ref, sd_ref, st, b_scr = refs
            o_scr = o_ref
        i = pl.program_id(0)

        @pl.when(i == 0)
        def _():
            st[...] = jnp.zeros_like(st)

        q_raw, k, v = q_ref[...], k_ref[...], v_ref[...]
        _, b, _, _, qt, kt = _gla_recompute(q_raw, k, lr_ref[...], w_ref[...], bias_ref[...], rev, tb)
        b_scr[...] = b
        maskw, bd = _wide_mask(rev), _state_mask()
        order = list(reversed(range(nb))) if rev else list(range(nb))
        rows = [slice(c * CH, (c + 1) * CH) for c in range(nb)]
        gdec = {c: jnp.exp(b_scr[pl.ds(c * CH + (0 if rev else CH - 1), 1), :]) for c in order}
        upd = {c: jnp.where(bd, _dot_tn(v[rows[c]], kt[rows[c]] * gdec[c]), 0.0) for c in order}
        o_intra = {c: _dot(jnp.where(maskw, _dot_nt(qt[rows[c]], _stack_heads(kt[rows[c]], 6)), 0.0),
                           _stack_heads(v[rows[c]], 7)) for c in order}
        s_in, state = {}, st[...]
        for c in order:
            s_in[c] = state
            state = state * gdec[c] + upd[c]
        st[...] = state
        for c in order:
            sd_ref[c] = s_in[c][0:128] + s_in[c][128:256] + s_in[c][256:384] + s_in[c][384:512]
            o_scr[pl.ds(c * CH, CH), :] = o_intra[c] + _dot_nt(qt[rows[c]], s_in[c])

        if finish:
            j = jmap(i)
            hsel = jnp.where((_iota((GK, GV), 0) >> 6) == (_iota((GK, GV), 1) >> 7), 1.0, 0.0).astype(BF16)
            sb = _dot_exact_rhs((q_raw * 0.125) * k, hsel, 2)
            o_pre = of_ref[...] + o_scr[...] - sb * v
            opre_ref[...] = o_pre
            on, _ = _head_norm(o_pre)
            g = g_ref[...]
            y_ref[:, CW:] = (on * gn_ref[...] * (g * _sigmoid(g))).astype(BF16)
            cb = cb_ref[...]
            _, _, _, conv = _conv_parts(cb, cc_ref[...], cu_ref[...], ccp_ref[pl.ds(7, 1), :], cup_ref[pl.ds(7, 1), :],
                                        ccn_ref[pl.ds(0, 1), :], cun_ref[pl.ds(0, 1), :], cw_ref, j == 0,
                                        j == nblk - 1, tb)
            yc = cb * conv
            gm = _dot_exact_rhs(yc * yc, _group_ones(), 2) * (1.0 / 64.0)
            y_ref[:, :CW] = (yc * lax.rsqrt(gm + EPS) * cn_ref[...]).astype(BF16)

    full = lambda i: (0, 0)
    in_specs = [_zspec(tb, GK, ZB_Q, jmap), _zspec(tb, GK, ZB_K, jmap), _zspec(tb, GV, ZB_V, jmap),
                _zspec(tb, 128, ZB_LR, jmap), pl.BlockSpec((128, GK), full), pl.BlockSpec((1, GK), full)]
    args = [z, z, z, z, wpad, bias]
    sd_spec = pl.BlockSpec((nb, 128, GK), lambda i: (jmap(i), 0, 0))
    sd_shape = jax.ShapeDtypeStruct((t // CH, 128, GK), F32)
    scratch = [pltpu.VMEM((GV, GK), F32), pltpu.VMEM((tb, GK), F32)]
    if finish:
        o_f, conv_w, conv_norm, gla_norm4 = finish_args
        ccp, ccn = _halo_specs(tb, nblk, t, ZB_CC, jmap)
        cup, cun = _halo_specs(tb, nblk, t, ZB_CU, jmap)
        in_specs += [pl.BlockSpec((tb, GV), lambda i: (jmap(i), 0)), _zspec(tb, GV, ZB_G, jmap),
                     _zspec(tb, CW, ZB_CB, jmap), _zspec(tb, CW, ZB_CC, jmap), _zspec(tb, CW, ZB_CU, jmap),
                     ccp, ccn, cup, cun, pl.BlockSpec((3, CW), full), pl.BlockSpec((1, CW), full),
                     pl.BlockSpec((1, GV), full)]
        args += [o_f, z, z, z, z, z, z, z, z, conv_w, conv_norm, gla_norm4]
        out_specs = [pl.BlockSpec((tb, D), lambda i: (jmap(i), 0)), pl.BlockSpec((tb, GV), lambda i: (jmap(i), 0)),
                     sd_spec]
        out_shape = [jax.ShapeDtypeStruct((t, D), BF16), jax.ShapeDtypeStruct((t, GV), F32), sd_shape]
        scratch.append(pltpu.VMEM((tb, GV), F32))
    else:
        out_specs = [pl.BlockSpec((tb, GV), lambda i: (jmap(i), 0)), sd_spec]
        out_shape = [jax.ShapeDtypeStruct((t, GV), F32), sd_shape]
    return _call(body, "gla_fwd_rev" if rev else "gla_fwd", (nblk,), in_specs, out_specs, out_shape, scratch, args,
                 riders)


def _gla_bwd_chunks(do_ref, sd_ref, dst, b_scr, db_scr, dq_ref, dk_ref, dv_ref, qt, kt, e, ei, v, rev, nb):
    maskw, bd = _wide_mask(rev), _state_mask()
    order = list(range(nb)) if rev else list(reversed(range(nb)))
    rows = [slice(c * CH, (c + 1) * CH) for c in range(nb)]
    grow = [c * CH + (0 if rev else CH - 1) for c in range(nb)]
    gdec, do, s_in, dqt, dv, dkt, upd = {}, {}, {}, {}, {}, {}, {}
    for c in order:
        qt_c, kt_c, v_c = qt[rows[c]], kt[rows[c]], v[rows[c]]
        gdec[c] = jnp.exp(b_scr[pl.ds(grow[c], 1), :])
        do[c] = do_ref[pl.ds(c * CH, CH), :]
        s_in[c] = _state_expand(sd_ref[c])
        kbd, vbd = _stack_heads(kt_c, 6), _stack_heads(v_c, 7)
        a = jnp.where(maskw, _dot_nt(qt_c, kbd), 0.0)
        da = jnp.where(maskw, _dot_nt(do[c], vbd), 0.0)
        dv[c] = _fold_heads(_dot_tn(a, do[c]), 7)
        dqt[c] = _dot(da, kbd) + _dot(do[c], s_in[c])
        dkt[c] = _fold_heads(_dot_tn(da, qt_c), 6)
        upd[c] = jnp.where(bd, _dot_tn(do[c], qt_c), 0.0)
    ds_out, state = {}, dst[...]
    for c in order:
        ds_out[c] = state
        state = state * gdec[c] + upd[c]
    dst[...] = state
    for c in order:
        qt_c, kt_c, v_c = qt[rows[c]], kt[rows[c]], v[rows[c]]
        dv_ref[pl.ds(c * CH, CH), :] = dv[c] + _dot_nt(kt_c * gdec[c], ds_out[c])
        dkh = _dot(v_c, ds_out[c])
        dkt_c = dkt[c] + dkh * gdec[c]
        dg = jnp.sum(ds_out[c] * s_in[c], axis=0, keepdims=True) + jnp.sum(kt_c * dkh, axis=0, keepdims=True)
        db_scr[pl.ds(c * CH, CH), :] = dqt[c] * qt_c - dkt_c * kt_c
        db_scr[pl.ds(grow[c], 1), :] += dg * gdec[c]
        dq_ref[pl.ds(c * CH, CH), :] = dqt[c] * e[rows[c]] * 0.125
        dk_ref[pl.ds(c * CH, CH), :] = dkt_c * ei[rows[c]]


def _gate_bwd(db, pre, lr, wpad, rev, tb):
    r, c = _iota((tb, tb), 0), _iota((tb, tb), 1)
    tri = (c <= r) if rev else (c >= r)
    cum_t = jnp.where(((r >> 6) == (c >> 6)) & tri, 1.0, 0.0).astype(BF16)
    dla = _dot_exact_lhs(cum_t, db, 2)
    dpre = dla * (1.0 / 16.0) / (1.0 + jnp.exp(pre))
    return dpre, _dot_nt(dpre, wpad), _dot_tn(lr, dpre)


def _gla_bwd_first(z, dy, o_pre, sd, wpad, bias, conv_w, conv_norm, gla_norm4, riders=()):
    t = z.shape[0]
    tb = min(TB, t)
    nblk, nb = t // tb, tb // CH
    jmap = lambda i: nblk - 1 - i

    def body(q_ref, k_ref, v_ref, lr_ref, g_ref, cb_ref, cc_ref, cu_ref, ccp_ref, ccn_ref, cup_ref, cun_ref,
             dy_ref, opre_ref, sd_ref, w_ref, bias_ref, cw_ref, cn_ref, gn_ref,
             do_ref, dq_ref, dk_ref, dv_ref, dlr_ref, dzg_ref, dzcb_ref, dconv_ref,
             dw_ref, dbias_ref, dcw_ref, dcn_ref, dgn_ref, dst, b_scr, db_scr):
        i = pl.program_id(0)
        j = jmap(i)

        @pl.when(i == 0)
        def _():
            dst[...] = jnp.zeros_like(dst)
            for ref in (dw_ref, dbias_ref, dcw_ref, dcn_ref, dgn_ref):
                ref[...] = jnp.zeros_like(ref)

        dyg = dy_ref[:, CW:]
        g = g_ref[...]
        sig = _sigmoid(g)
        on, rr = _head_norm(opre_ref[...])
        gn = gn_ref[...]
        dzg_ref[...] = (dyg * on * gn * (sig * (1.0 + g * (1.0 - sig)))).astype(BF16)
        don = dyg * (g * sig)
        _acc_rows(dgn_ref, jnp.sum(don * on, axis=0, keepdims=True))
        u = don * gn
        uo = u * on
        mean_uo = jnp.concatenate(
            [jnp.broadcast_to(jnp.mean(uo[:, h * 128:(h + 1) * 128], axis=-1, keepdims=True), (tb, 128))
             for h in range(NH)], axis=1)
        do_ref[...] = rr * (u - on * mean_uo)

        cb = cb_ref[...]
        h, h_m1, h_p1, conv = _conv_parts(cb, cc_ref[...], cu_ref[...], ccp_ref[pl.ds(7, 1), :],
                                          cup_ref[pl.ds(7, 1), :], ccn_ref[pl.ds(0, 1), :], cun_ref[pl.ds(0, 1), :],
                                          cw_ref, j == 0, j == nblk - 1, tb)
        yc = cb * conv
        ones = _group_ones()
        rc = lax.rsqrt(_dot_exact_rhs(yc * yc, ones, 2) * (1.0 / 64.0) + EPS)
        ycr = yc * rc
        dyn = dy_ref[:, :CW]
        _acc_rows(dcn_ref, jnp.sum(dyn * ycr, axis=0, keepdims=True))
        uc = dyn * cn_ref[...]
        dyc = rc * (uc - ycr * (_dot_exact_rhs(uc * ycr, ones, 2) * (1.0 / 64.0)))
        dzcb_ref[...] = (dyc * conv).astype(BF16)
        dconv = dyc * cb
        dconv_ref[...] = dconv
        dcw_ref[pl.ds(0, 1), :] += jnp.sum(dconv * h_m1, axis=0, keepdims=True)
        dcw_ref[pl.ds(1, 1), :] += jnp.sum(dconv * h, axis=0, keepdims=True)
        dcw_ref[pl.ds(2, 1), :] += jnp.sum(dconv * h_p1, axis=0, keepdims=True)

        lr, wp = lr_ref[...], w_ref[...]
        pre, b, e, ei, qt, kt = _gla_recompute(q_ref[...], k_ref[...], lr, wp, bias_ref[...], False, tb)
        b_scr[...] = b
        _gla_bwd_chunks(do_ref, sd_ref, dst, b_scr, db_scr, dq_ref, dk_ref, dv_ref, qt, kt, e, ei, v_ref[...],
                        False, nb)
        dpre, dlr, dw = _gate_bwd(db_scr[...], pre, lr, wp, False, tb)
        dlr_ref[...] = dlr
        dw_ref[...] += dw
        _acc_rows(dbias_ref, jnp.sum(dpre, axis=0, keepdims=True))

    full = lambda i: (0, 0)
    tokv = pl.BlockSpec((tb, GV), lambda i: (jmap(i), 0))
    tokk = pl.BlockSpec((tb, GK), lambda i: (jmap(i), 0))
    ccp, ccn = _halo_specs(tb, nblk, t, ZB_CC, jmap)
    cup, cun = _halo_specs(tb, nblk, t, ZB_CU, jmap)
    in_specs = [_zspec(tb, GK, ZB_Q, jmap), _zspec(tb, GK, ZB_K, jmap), _zspec(tb, GV, ZB_V, jmap),
                _zspec(tb, 128, ZB_LR, jmap), _zspec(tb, GV, ZB_G, jmap), _zspec(tb, CW, ZB_CB, jmap),
                _zspec(tb, CW, ZB_CC, jmap), _zspec(tb, CW, ZB_CU, jmap), ccp, ccn, cup, cun,
                pl.BlockSpec((tb, D), lambda i: (jmap(i), 0)), tokv,
                pl.BlockSpec((nb, 128, GK), lambda i: (jmap(i), 0, 0)), pl.BlockSpec((128, GK), full),
                pl.BlockSpec((1, GK), full), pl.BlockSpec((3, CW), full), pl.BlockSpec((1, CW), full),
                pl.BlockSpec((1, GV), full)]
    out_specs = [tokv, tokk, tokk, tokv, pl.BlockSpec((tb, 128), lambda i: (jmap(i), 0)), tokv, tokv, tokv,
                 pl.BlockSpec((128, GK), full), pl.BlockSpec((8, GK), full), pl.BlockSpec((8, CW), full),
                 pl.BlockSpec((8, CW), full), pl.BlockSpec((8, GV), full)]
    out_shape = [jax.ShapeDtypeStruct((t, GV), F32), jax.ShapeDtypeStruct((t, GK), F32),
                 jax.ShapeDtypeStruct((t, GK), F32), jax.ShapeDtypeStruct((t, GV), F32),
                 jax.ShapeDtypeStruct((t, 128), F32), jax.ShapeDtypeStruct((t, GV), BF16),
                 jax.ShapeDtypeStruct((t, CW), BF16), jax.ShapeDtypeStruct((t, CW), F32),
                 jax.ShapeDtypeStruct((128, GK), F32), jax.ShapeDtypeStruct((8, GK), F32),
                 jax.ShapeDtypeStruct((8, CW), F32), jax.ShapeDtypeStruct((8, CW), F32),
                 jax.ShapeDtypeStruct((8, GV), F32)]
    return _call(
        body, "gla_bwd_first", (nblk,), in_specs, out_specs, out_shape,
        [pltpu.VMEM((GV, GK), F32), pltpu.VMEM((tb, GK), F32), pltpu.VMEM((tb, GK), F32)],
        (z, z, z, z, z, z, z, z, z, z, z, z, dy, o_pre, sd, wpad, bias, conv_w, conv_norm, gla_norm4), riders)


def _gla_bwd_second(z, do, sd, wpad, bias, dqa, dka, dva, dlra, dzg, dzcb, dconv, conv_w, riders=()):
    t = z.shape[0]
    tb = min(TB, t)
    nblk, nb = t // tb, tb // CH
    jmap = lambda i: i

    def body(q_ref, k_ref, v_ref, lr_ref, cc_ref, cu_ref, do_ref, sd_ref, w_ref, bias_ref, dqa_ref, dka_ref,
             dva_ref, dlra_ref, dzg_ref, dzcb_ref, dc_ref, dcp_ref, dcn_ref, cw_ref,
             dz_ref, dw_ref, dbias_ref, dst, b_scr, db_scr, dq_scr, dk_scr, dv_scr):
        i = pl.program_id(0)

        @pl.when(i == 0)
        def _():
            dst[...] = jnp.zeros_like(dst)
            dw_ref[...] = jnp.zeros_like(dw_ref)
            dbias_ref[...] = jnp.zeros_like(dbias_ref)

        q_raw, k, v, lr, wp = q_ref[...], k_ref[...], v_ref[...], lr_ref[...], w_ref[...]
        pre, b, e, ei, qt, kt = _gla_recompute(q_raw, k, lr, wp, bias_ref[...], True, tb)
        b_scr[...] = b
        _gla_bwd_chunks(do_ref, sd_ref, dst, b_scr, db_scr, dq_scr, dk_scr, dv_scr, qt, kt, e, ei, v, True, nb)
        dpre, dlr, dw = _gate_bwd(db_scr[...], pre, lr, wp, True, tb)
        dw_ref[...] += dw
        _acc_rows(dbias_ref, jnp.sum(dpre, axis=0, keepdims=True))

        do = do_ref[...]
        qs = q_raw * 0.125
        hsel = jnp.where((_iota((GK, GV), 0) >> 6) == (_iota((GK, GV), 1) >> 7), 1.0, 0.0).astype(BF16)
        hsel_t = jnp.where((_iota((GV, GK), 0) >> 7) == (_iota((GV, GK), 1) >> 6), 1.0, 0.0).astype(BF16)
        sb = _dot_exact_rhs(qs * k, hsel, 2)
        dsk = _dot_exact_rhs(do * v, hsel_t, 2)
        dz_ref[:, 1536:1792] = (dqa_ref[...] + dq_scr[...] - dsk * k * 0.125).astype(BF16)
        dz_ref[:, 1792:2048] = (dka_ref[...] + dk_scr[...] - dsk * qs).astype(BF16)
        dz_ref[:, 2048:2560] = (dva_ref[...] + dv_scr[...] - sb * do).astype(BF16)
        dz_ref[:, 2560:3072] = dzg_ref[...]
        dz_ref[:, 3072:3200] = (dlra_ref[...] + dlr).astype(BF16)

        dc = dc_ref[...]
        rows = _iota(dc.shape, 0)
        dprev = jnp.where(i == 0, 0.0, dcp_ref[pl.ds(7, 1), :])
        dnext = jnp.where(i == nblk - 1, 0.0, dcn_ref[pl.ds(0, 1), :])
        dc_m1 = jnp.where(rows == 0, dprev, pltpu.roll(dc, 1, 0))
        dc_p1 = jnp.where(rows == tb - 1, dnext, pltpu.roll(dc, tb - 1, 0))
        dh = cw_ref[pl.ds(0, 1), :] * dc_p1 + cw_ref[pl.ds(1, 1), :] * dc + cw_ref[pl.ds(2, 1), :] * dc_m1
        dz_ref[:, 0:512] = dzcb_ref[...]
        dz_ref[:, 512:1024] = (dh * cu_ref[...]).astype(BF16)
        dz_ref[:, 1024:1536] = (dh * cc_ref[...]).astype(BF16)

    full = lambda i: (0, 0)
    tokv = pl.BlockSpec((tb, GV), lambda i: (i, 0))
    tokk = pl.BlockSpec((tb, GK), lambda i: (i, 0))
    dcp = pl.BlockSpec((8, CW), lambda i: (jnp.maximum(i * (tb // 8) - 1, 0), 0))
    dcn = pl.BlockSpec((8, CW), lambda i: (jnp.minimum((i + 1) * (tb // 8), t // 8 - 1), 0))
    in_specs = [_zspec(tb, GK, ZB_Q, jmap), _zspec(tb, GK, ZB_K, jmap), _zspec(tb, GV, ZB_V, jmap),
                _zspec(tb, 128, ZB_LR, jmap), _zspec(tb, CW, ZB_CC, jmap), _zspec(tb, CW, ZB_CU, jmap), tokv,
                pl.BlockSpec((nb, 128, GK), lambda i: (i, 0, 0)), pl.BlockSpec((128, GK), full),
                pl.BlockSpec((1, GK), full), tokk, tokk, tokv, pl.BlockSpec((tb, 128), lambda i: (i, 0)), tokv, tokv,
                tokv, dcp, dcn, pl.BlockSpec((3, CW), full)]
    out_specs = [pl.BlockSpec((tb, ZC), lambda i: (i, 0)), pl.BlockSpec((128, GK), full), pl.BlockSpec((8, GK), full)]
    out_shape = [jax.ShapeDtypeStruct((t, ZC), BF16), jax.ShapeDtypeStruct((128, GK), F32),
                 jax.ShapeDtypeStruct((8, GK), F32)]
    return _call(
        body, "gla_bwd_second", (nblk,), in_specs, out_specs, out_shape,
        [pltpu.VMEM((GV, GK), F32), pltpu.VMEM((tb, GK), F32), pltpu.VMEM((tb, GK), F32),
         pltpu.VMEM((tb, GK), F32), pltpu.VMEM((tb, GK), F32), pltpu.VMEM((tb, GV), F32)],
        (z, z, z, z, z, z, do, sd, wpad, bias, dqa, dka, dva, dlra, dzg, dzcb, dconv, dconv, dconv, conv_w), riders)


def _step(x, mem, target, shard, small_pack, vec, place):
    own, from_chips = {}, {}

    def pair_sums(names, g4, from_sibling):
        pbs = []
        for n, g, s in zip(names, g4, from_sibling):
            pb, own[n] = _rs_pair_sum(place, g, s, "pair_sum_" + n)
            pbs.append(pb)
        return pbs

    def by_dest(g, n):
        return g.reshape((4, 2) + shard[n].shape)

    w_in, small_all = _exchange(_gather_rider([shard["w_in"], small_pack]), "gather_w_in")
    w_in = jnp.pad(w_in.transpose(1, 0, 2).reshape(D, ZW), ((0, 0), (0, ZC - ZW)))
    small_all = small_all.reshape(NDEV, -1)
    p, off = {}, 0
    for n, (r, c) in SMALL_SHARDED.items():
        p[n] = small_all[:, off:off + r * c].reshape(NDEV, r, c).transpose(1, 0, 2).reshape(r, NDEV * c)
        off += r * c
    zeros_lr = jnp.zeros((128 - LR, GK), BF16)
    waf_pad = jnp.concatenate([p["w_af"].astype(BF16), zeros_lr], axis=0)
    wab_pad = jnp.concatenate([jnp.zeros((LR, GK), BF16), p["w_ab"].astype(BF16), zeros_lr[:128 - 2 * LR]], axis=0)
    gla_norm4 = jnp.tile(vec["gla_norm"], (1, NH))

    z, hb, w_out, w_xq, w_xo, w_xkv = _inproj(
        x, vec["mix_norm"], w_in, [_gather_rider([shard[n] for n in ("w_out", "w_xq", "w_xo", "w_xkv")])])
    w_out, w_xq, w_xo = [a.reshape(D, D) for a in (w_out, w_xq, w_xo)]
    o_f, sd_f, w_up_t = _gla_fwd_sweep(z, waf_pad, vec["b_af"], False, riders=[_gather_rider([shard["w_up"]])])
    w_up_t = w_up_t.reshape(FF, D)
    yb, o_pre, sd_b, w_down = _gla_fwd_sweep(z, wab_pad, vec["b_ab"], True,
                                             (o_f, p["conv_w"], vec["conv_norm"], gla_norm4),
                                             riders=[_gather_rider([shard["w_down"]])])
    w_down = w_down.reshape(FF, D)
    kv, memn = _kv_proj(mem, vec["mem_norm"], w_xkv)
    kb, vb = kv[:, :D].astype(BF16), kv[:, D:].astype(BF16)
    x1, x2, xn1, qb, attb = _attn_fwd(x, yb, w_out, vec["xa_norm"], w_xq, kb, vb, w_xo)
    h1b, xn2, dx3, dx3b, loss8, dfinal = _mlp_fwd(x2, vec["mlp_norm"], w_up_t, w_down, vec["final_norm"], target)

    ab, dh1b, dx2, dx2b, dmlp = _mlp_bwd(dx3, dx3b, h1b, w_down, w_up_t, x2, vec["mlp_norm"])
    g_mlp = [by_dest(_matmul_tn(ab, dx3b, "dw_down")[0], "w_down"),
             by_dest(_matmul_tn(dh1b, xn2, "dw_up")[0], "w_up")]
    dx1, dx1b, dy, dqb, dkv, dxa = _attn_bwd(x1, dx2, dx2b, qb, kb, vb, w_xo, w_xq, w_out, vec["xa_norm"])
    dw_xo, *s_mlp = _matmul_tn(attb, dx2b, "dw_xo", riders=[_sibling_rider(g_mlp)])
    pb_mlp = pair_sums(("w_down", "w_up"), g_mlp, s_mlp)
    dw_xkv, dmemn = _kv_bwd(dkv, memn, mem, vec["mem_norm"], w_xkv)
    att_names = ("w_xo", "w_xq", "w_out", "w_xkv")
    g_att = [by_dest(g, n) for g, n in zip(
        (dw_xo, _matmul_tn(xn1, dqb, "dw_xq")[0], _matmul_tn(yb, dx1b, "dw_out")[0], dw_xkv), att_names)]
    res = _gla_bwd_first(z, dy, o_pre, sd_f, waf_pad, vec["b_af"], p["conv_w"], vec["conv_norm"], gla_norm4,
                         riders=[_chips_rider(pb_mlp), _sibling_rider(g_att)])
    do, dqa, dka, dva, dlra, dzg, dzcb, dconv, dwaf, dbaf, dcw, dcn, dgn = res[:13]
    from_chips["w_down"], from_chips["w_up"] = res[13:15]
    pb_att = pair_sums(att_names, g_att, res[15:])
    dz, dwab, dbab, *c_att = _gla_bwd_second(z, do, sd_b, wab_pad, vec["b_ab"], dqa, dka, dva, dlra, dzg, dzcb, dconv,
                                             p["conv_w"], riders=[_chips_rider(pb_att)])
    from_chips.update(zip(att_names, c_att))
    dw_in = _matmul_tn(hb, dz, "dw_in")[0][:, :ZW].reshape(D, NDEV, WIN_SHARD).transpose(1, 0, 2)
    g_in = [by_dest(dw_in, "w_in")]
    pb_in = pair_sums(("w_in",), g_in, _exchange(_sibling_rider(g_in), "grads_to_sibling_w_in"))
    grad_x, dmix, from_chips["w_in"] = _inproj_bwd(dz, w_in, x, dx1, vec["mix_norm"], riders=[_chips_rider(pb_in)])

    small_grads = {
        "mix_norm": dmix[0:1], "conv_w": dcw[0:3], "conv_norm": dcn[0:1],
        "w_af": dwaf[0:LR], "b_af": dbaf[0:1], "w_ab": dwab[LR:2 * LR], "b_ab": dbab[0:1],
        "gla_norm": (dgn[0:1, 0:128] + dgn[0:1, 128:256]) + (dgn[0:1, 256:384] + dgn[0:1, 384:512]),
        "xa_norm": dxa[0:1], "mem_norm": dmemn[0:1], "mlp_norm": dmlp[0:1], "final_norm": dfinal[0:1],
    }
    return loss8[0:1, 0:1], grad_x, small_grads, own, from_chips


def _place():
    return lax.axis_index("x"), lax.axis_index("y"), lax.axis_index("c")


class _Rider:
    def __init__(self, arrays, out_shape, scratch, start, finish):
        self.arrays, self.out_shape, self.scratch, self.start, self.finish = arrays, out_shape, scratch, start, finish


def _gather_rider(blks):
    n = len(blks)

    def plan(in_refs, out_refs, sems):
        send_sems, recv_sems, local_sems = sems
        x, y, c = _place()
        me, sibling = (x, y, c), (x, y, 1 - c)
        chips = [(1 - x, y, c), (x, 1 - y, c), (1 - x, 1 - y, c)]

        def copy(a, k, block, to, own=False):
            px, py, pc = block
            dst = out_refs[a].at[4 * px + 2 * py + pc]
            return pltpu.make_async_remote_copy(
                src_ref=in_refs[a] if own else dst, dst_ref=dst, send_sem=send_sems.at[k, a],
                recv_sem=recv_sems.at[k, a], device_id=to, device_id_type=MESH)

        def local(a):
            return pltpu.make_async_copy(in_refs[a], out_refs[a].at[4 * x + 2 * y + c], local_sems.at[a])

        def own_sends(a):
            return [copy(a, 0, me, sibling, own=True)] + [copy(a, 1 + j, me, chip, own=True)
                                                          for j, chip in enumerate(chips)]

        return copy, local, own_sends, me, sibling, chips

    def start(in_refs, out_refs, sems):
        _, local, own_sends, _, _, _ = plan(in_refs, out_refs, sems)
        for a in range(n):
            local(a).start()
            for cp in own_sends(a):
                cp.start()

    def finish(in_refs, out_refs, sems):
        copy, local, own_sends, me, sibling, chips = plan(in_refs, out_refs, sems)
        for j, chip in enumerate(chips):
            for a in range(n):
                copy(a, 1 + j, chip, me).wait_recv()
                copy(a, 4 + j, chip, sibling).start()
        for a in range(n):
            copy(a, 0, sibling, me).wait_recv()
            for j, (px, py, pc) in enumerate(chips):
                copy(a, 4 + j, (px, py, 1 - pc), me).wait_recv()
            for cp in own_sends(a) + [copy(a, 4 + j, chip, sibling) for j, chip in enumerate(chips)]:
                cp.wait_send()
            local(a).wait()

    return _Rider(blks, [jax.ShapeDtypeStruct((NDEV,) + b.shape, b.dtype) for b in blks],
                  [pltpu.SemaphoreType.DMA((7, n)), pltpu.SemaphoreType.DMA((7, n)), pltpu.SemaphoreType.DMA((n,))],
                  start, finish)


def _sibling_rider(g4s):
    n = len(g4s)

    def copies(in_refs, out_refs, sems):
        send_sems, recv_sems = sems
        x, y, c = _place()
        return [pltpu.make_async_remote_copy(
            src_ref=in_refs[a].at[k, 1 - c], dst_ref=out_refs[a].at[k], send_sem=send_sems.at[k, a],
            recv_sem=recv_sems.at[k, a], device_id=(x, y, 1 - c), device_id_type=MESH)
            for a in range(n) for k in range(4)]

    def start(in_refs, out_refs, sems):
        for cp in copies(in_refs, out_refs, sems):
            cp.start()

    def finish(in_refs, out_refs, sems):
        for cp in copies(in_refs, out_refs, sems):
            cp.wait()

    return _Rider(g4s, [jax.ShapeDtypeStruct((4,) + g.shape[2:], g.dtype) for g in g4s],
                  [pltpu.SemaphoreType.DMA((4, n)), pltpu.SemaphoreType.DMA((4, n))], start, finish)


def _chips_rider(pbs):
    n = len(pbs)

    def copies(in_refs, out_refs, sems):
        send_sems, recv_sems = sems
        x, y, c = _place()
        peers = [(1 - x, y), (x, 1 - y), (1 - x, 1 - y)]
        return [pltpu.make_async_remote_copy(
            src_ref=in_refs[a].at[2 * px + py], dst_ref=out_refs[a].at[k], send_sem=send_sems.at[k, a],
            recv_sem=recv_sems.at[k, a], device_id=(px, py, c), device_id_type=MESH)
            for a in range(n) for k, (px, py) in enumerate(peers)]

    def start(in_refs, out_refs, sems):
        for cp in copies(in_refs, out_refs, sems):
            cp.start()

    def finish(in_refs, out_refs, sems):
        for cp in copies(in_refs, out_refs, sems):
            cp.wait()

    return _Rider(pbs, [jax.ShapeDtypeStruct((3,) + p.shape[1:], p.dtype) for p in pbs],
                  [pltpu.SemaphoreType.DMA((3, n)), pltpu.SemaphoreType.DMA((3, n))], start, finish)


def _exchange(rider, name):
    n_in, n_out = len(rider.arrays), len(rider.out_shape)

    def body(*refs):
        ins, outs, sems = refs[:n_in], refs[n_in:n_in + n_out], refs[n_in + n_out:]
        rider.start(ins, outs, sems)
        rider.finish(ins, outs, sems)

    hbm = pl.BlockSpec(memory_space=pltpu.HBM)
    return pl.pallas_call(body, name=name, out_shape=rider.out_shape, in_specs=[hbm] * n_in,
                          out_specs=[hbm] * n_out, scratch_shapes=rider.scratch)(*rider.arrays)


def _rs_pair_sum(place, g4, r1, name):
    rows, cols = g4.shape[2:]
    tr = min(rows, 512)

    def body(pl_ref, g_ref, r_ref, pb_ref, own_ref):
        s = g_ref[0, 0] + r_ref[0]
        pb_ref[0] = s.astype(BF16)

        @pl.when(pl.program_id(1) == pl_ref[0])
        def _():
            own_ref[...] = s

    grid_spec = pltpu.PrefetchScalarGridSpec(
        num_scalar_prefetch=1, grid=(rows // tr, 4),
        in_specs=[pl.BlockSpec((1, 1, tr, cols), lambda r, k, p: (k, p[1], r, 0)),
                  pl.BlockSpec((1, tr, cols), lambda r, k, p: (k, r, 0))],
        out_specs=[pl.BlockSpec((1, tr, cols), lambda r, k, p: (k, r, 0)),
                   pl.BlockSpec((tr, cols), lambda r, k, p: (r, 0))])
    return pl.pallas_call(
        body, name=name, grid_spec=grid_spec,
        out_shape=[jax.ShapeDtypeStruct((4, rows, cols), BF16), jax.ShapeDtypeStruct((rows, cols), F32)],
        compiler_params=_cparams(("arbitrary", "arbitrary")))(place, g4, r1)


def _small_all_reduce(vec):
    m_per = vec.shape[0]

    def body(x_ref, all_ref, sum_ref, send_sems, recv_sems, local_sem):
        x, y, c = _place()
        me, sibling = (x, y, c), (x, y, 1 - c)
        chips = [(1 - x, y), (x, 1 - y), (1 - x, 1 - y)]

        def rows(px, py, pc):
            return all_ref.at[4 * px + 2 * py + pc]

        def copy(k, block, to, src=None):
            return pltpu.make_async_remote_copy(
                src_ref=rows(*block) if src is None else src, dst_ref=rows(*block),
                send_sem=send_sems.at[k], recv_sem=recv_sems.at[k], device_id=to, device_id_type=MESH)

        mine = pltpu.make_async_copy(x_ref, rows(*me), local_sem)
        mine.start()
        first = [copy(0, me, sibling, src=x_ref)]
        first += [copy(1 + j, me, (*chip, c), src=x_ref) for j, chip in enumerate(chips)]
        for cp in first:
            cp.start()
        passed = [copy(4 + j, (*chip, c), sibling) for j, chip in enumerate(chips)]
        for j, chip in enumerate(chips):
            copy(1 + j, (*chip, c), me).wait_recv()
            passed[j].start()
        copy(0, sibling, me).wait_recv()
        for j, chip in enumerate(chips):
            copy(4 + j, (*chip, 1 - c), me).wait_recv()
        for cp in first + passed:
            cp.wait_send()
        mine.wait()
        total = all_ref[0]
        for d in range(1, NDEV):
            total = total + all_ref[d]
        sum_ref[...] = total

    return pl.pallas_call(
        body, name="small_all_reduce",
        out_shape=[jax.ShapeDtypeStruct((NDEV, m_per, 128), F32), jax.ShapeDtypeStruct((m_per, 128), F32)],
        in_specs=[pl.BlockSpec(memory_space=pltpu.VMEM)],
        out_specs=[pl.BlockSpec(memory_space=pltpu.VMEM), pl.BlockSpec(memory_space=pltpu.VMEM)],
        scratch_shapes=[pltpu.SemaphoreType.DMA((7,)), pltpu.SemaphoreType.DMA((7,)), pltpu.SemaphoreType.DMA],
    )(vec)[1]


def _adamw_math(w, g, m, v):
    m = ADAM_B1 * m + (1.0 - ADAM_B1) * g
    v = ADAM_B2 * v + (1.0 - ADAM_B2) * (g * g)
    m_hat = m / (1.0 - ADAM_B1 ** ADAM_STEP)
    v_hat = v / (1.0 - ADAM_B2 ** ADAM_STEP)
    delta = -ADAM_LR * (m_hat / (jnp.sqrt(v_hat) + ADAM_EPS) + ADAM_WD * w)
    return delta, m, v


def _adamw(w, own, r2, m, v, name):
    _, r, c = w.shape
    tr = min(r, 256)

    def body(w_ref, o_ref, r_ref, m_ref, v_ref, g_ref, d_ref, nm_ref, nv_ref):
        g = ((o_ref[...] + r_ref[0].astype(F32)) + r_ref[1].astype(F32)) + r_ref[2].astype(F32)
        g_ref[...] = g
        d_ref[...], nm_ref[...], nv_ref[...] = _adamw_math(w_ref[...], g, m_ref[...], v_ref[...])

    spec = pl.BlockSpec((None, tr, c), lambda i: (0, i, 0))
    return pl.pallas_call(
        body, name=name, grid=(r // tr,),
        in_specs=[spec, pl.BlockSpec((tr, c), lambda i: (i, 0)), pl.BlockSpec((3, tr, c), lambda i: (0, i, 0)),
                  spec, spec],
        out_specs=[spec] * 4, out_shape=[jax.ShapeDtypeStruct((1, r, c), F32)] * 4,
        compiler_params=_cparams(("arbitrary",)))(w, own, r2, m, v)


def _adamw_small(ws, gs, ms, vs):
    n = len(ws)

    def body(*refs):
        ins, outs = refs[:4 * n], refs[4 * n:]
        for i in range(n):
            d, m, v = _adamw_math(ins[i][...], ins[n + i][...], ins[2 * n + i][...], ins[3 * n + i][...])
            outs[i][...], outs[n + i][...], outs[2 * n + i][...] = d, m, v

    shapes = [jax.ShapeDtypeStruct(w.shape, F32) for w in ws]
    outs = pl.pallas_call(body, name="adamw_small", out_shape=shapes * 3)(*ws, *gs, *ms, *vs)
    return outs[:n], outs[n:2 * n], outs[2 * n:]


MATS = ("w_in", "w_out", "w_xq", "w_xo", "w_xkv", "w_up", "w_down")
SMALL = ("mix_norm", "conv_w", "conv_norm", "w_af", "b_af", "w_ab", "b_ab", "gla_norm", "xa_norm", "mem_norm",
         "mlp_norm", "final_norm")
WEIGHTS = ("mix_norm", "w_in", "conv_w", "conv_norm", "w_af", "b_af", "w_ab", "b_ab", "gla_norm", "w_out", "xa_norm",
           "mem_norm", "w_xq", "w_xkv", "w_xo", "mlp_norm", "w_up", "w_down", "final_norm")
COL_SHARDED = ("w_in", "w_xkv", "w_up")
SMALL_SHARDED = {"conv_w": (3, 64), "w_af": (LR, 32), "w_ab": (LR, 32)}
SMALL_PACK_ROWS = 16


def kernel(x, mem, mix_norm, w_in, conv_w, conv_norm, w_af, b_af, w_ab, b_ab, gla_norm, w_out, xa_norm, mem_norm, w_xq, w_xkv, w_xo, mlp_norm, w_up, w_down, final_norm, loss_target, m_mix_norm, m_w_in, m_conv_w, m_conv_norm, m_w_af, m_b_af, m_w_ab, m_b_ab, m_gla_norm, m_w_out, m_xa_norm, m_mem_norm, m_w_xq, m_w_xkv, m_w_xo, m_mlp_norm, m_w_up, m_w_down, m_final_norm, v_mix_norm, v_w_in, v_conv_w, v_conv_norm, v_w_af, v_b_af, v_w_ab, v_b_ab, v_gla_norm, v_w_out, v_xa_norm, v_mem_norm, v_w_xq, v_w_xkv, v_w_xo, v_mlp_norm, v_w_up, v_w_down, v_final_norm):
    w = dict(mix_norm=mix_norm, w_in=w_in, conv_w=conv_w, conv_norm=conv_norm, w_af=w_af, b_af=b_af, w_ab=w_ab,
             b_ab=b_ab, gla_norm=gla_norm, w_out=w_out, xa_norm=xa_norm, mem_norm=mem_norm, w_xq=w_xq, w_xkv=w_xkv,
             w_xo=w_xo, mlp_norm=mlp_norm, w_up=w_up, w_down=w_down, final_norm=final_norm)
    mom = dict(mix_norm=m_mix_norm, w_in=m_w_in, conv_w=m_conv_w, conv_norm=m_conv_norm, w_af=m_w_af, b_af=m_b_af,
               w_ab=m_w_ab, b_ab=m_b_ab, gla_norm=m_gla_norm, w_out=m_w_out, xa_norm=m_xa_norm, mem_norm=m_mem_norm,
               w_xq=m_w_xq, w_xkv=m_w_xkv, w_xo=m_w_xo, mlp_norm=m_mlp_norm, w_up=m_w_up, w_down=m_w_down,
               final_norm=m_final_norm)
    var = dict(mix_norm=v_mix_norm, w_in=v_w_in, conv_w=v_conv_w, conv_norm=v_conv_norm, w_af=v_w_af, b_af=v_b_af,
               w_ab=v_w_ab, b_ab=v_b_ab, gla_norm=v_gla_norm, w_out=v_w_out, xa_norm=v_xa_norm, mem_norm=v_mem_norm,
               w_xq=v_w_xq, w_xkv=v_w_xkv, w_xo=v_w_xo, mlp_norm=v_mlp_norm, w_up=v_w_up, w_down=v_w_down,
               final_norm=v_final_norm)
    xi, yi, ci = _place()
    me = 4 * xi + 2 * yi + ci
    two_d = lambda a: a.reshape(a.shape[-2:]) if a.ndim == 3 else a.reshape(1, a.shape[-1])

    small = jnp.concatenate([w[n].reshape(-1) for n in SMALL_SHARDED])
    small = jnp.pad(small, (0, SMALL_PACK_ROWS * 128 - small.shape[0])).reshape(SMALL_PACK_ROWS, 128)
    shard = {n: two_d(w[n]).astype(BF16) for n in MATS}
    shard["w_up"] = shard["w_up"].T
    vec = {n: two_d(w[n]) for n in SMALL if n not in SMALL_SHARDED}
    place = jnp.stack([2 * xi + yi, ci]).astype(jnp.int32)
    loss_part, grad_x, grads, own, from_chips = _step(x[0], mem[0], loss_target[0], shard, small, vec, place)

    order = [n for n in SMALL if n not in SMALL_SHARDED] + list(SMALL_SHARDED)
    flat = jnp.concatenate([grads[n].reshape(-1) for n in order] + [loss_part.reshape(-1)])
    n_flat = flat.shape[0]
    tot = _small_all_reduce(jnp.pad(flat, (0, SMALL_ROWS * 128 - n_flat)).reshape(SMALL_ROWS, 128)).reshape(-1)
    gsmall, off = {}, 0
    for n in order:
        size = grads[n].size
        full = tot[off:off + size].reshape(grads[n].shape)
        off += size
        if n in SMALL_SHARDED:
            r, c = SMALL_SHARDED[n]
            full = lax.dynamic_slice_in_dim(full, me * c, c, axis=1)
        gsmall[n] = full
    loss = tot[off]

    out_g, out_d, out_m, out_v = {}, {}, {}, {}
    own["w_up"], from_chips["w_up"] = own["w_up"].T, from_chips["w_up"].transpose(0, 2, 1)
    for n in MATS:
        out_g[n], out_d[n], out_m[n], out_v[n] = _adamw(w[n], own[n], from_chips[n], mom[n], var[n], "adamw_" + n)
    ds, nms, nvs = _adamw_small([two_d(w[n]) for n in SMALL], [gsmall[n] for n in SMALL],
                                [two_d(mom[n]) for n in SMALL], [two_d(var[n]) for n in SMALL])
    for i, n in enumerate(SMALL):
        out_g[n], out_d[n], out_m[n], out_v[n] = [a.reshape(w[n].shape) for a in (gsmall[n], ds[i], nms[i], nvs[i])]

    return (loss, grad_x[None], *[out_g[n] for n in WEIGHTS], *[out_d[n] for n in WEIGHTS],
            *[out_m[n] for n in WEIGHTS], *[out_v[n] for n in WEIGHTS])
```

```python
import functools

import jax
import jax.numpy as jnp
from jax import lax
from jax.experimental import pallas as pl
from jax.experimental.pallas import tpu as pltpu

F32 = jnp.float32
BF16 = jnp.bfloat16

D = 1024
CW = 512
GK = 256
GV = 512
NH = 4
CH = 64
LR = 16
NMEM = 256
XD = 256
FF = 4096
ZW = 3104
ZC = 3200
EPS = 1e-6
NDEV = 8

ZB_CB, ZB_CC, ZB_CU, ZB_V, ZB_G = 0, 1, 2, 4, 5
ZB_Q, ZB_K = 6, 7
ZB_LR = 24

TM = 512
TM_MLP = 256
TF = 512
TB = 256
TT = 2048
VMEM_LIMIT = 56 * 1024 * 1024

ADAM_LR, ADAM_B1, ADAM_B2, ADAM_EPS, ADAM_WD, ADAM_STEP = 0.001, 0.9, 0.999, 1e-08, 0.01, 10

XKV_SHARD = 2 * D // NDEV
SMALL_ROWS = 128

MESH = pl.DeviceIdType.MESH


def _cparams(sem):
    return pltpu.CompilerParams(dimension_semantics=sem, vmem_limit_bytes=VMEM_LIMIT)


def _call(body, name, grid, in_specs, out_specs, out_shape, scratch, args, riders=()):
    n_in, n_out, n_scr = len(in_specs), len(out_specs), len(scratch)
    counts = [(len(r.arrays), len(r.out_shape), len(r.scratch)) for r in riders]

    def take(refs, pos, sizes):
        groups = []
        for size in sizes:
            groups.append(refs[pos:pos + size])
            pos += size
        return groups, pos

    def wrapped(*refs):
        ins, pos = refs[:n_in], n_in
        r_ins, pos = take(refs, pos, [c[0] for c in counts])
        outs, pos = refs[pos:pos + n_out], pos + n_out
        r_outs, pos = take(refs, pos, [c[1] for c in counts])
        scr, pos = refs[pos:pos + n_scr], pos + n_scr
        r_scr, pos = take(refs, pos, [c[2] for c in counts])
        ids = [pl.program_id(d) for d in range(len(grid))]
        first = functools.reduce(lambda a, b: a & b, [i == 0 for i in ids])
        last = functools.reduce(lambda a, b: a & b, [i == g - 1 for i, g in zip(ids, grid)])

        @pl.when(first)
        def _():
            for r, a, b, c in zip(riders, r_ins, r_outs, r_scr):
                r.start(a, b, c)

        body(*ins, *outs, *scr)

        @pl.when(last)
        def _():
            for r, a, b, c in zip(riders, r_ins, r_outs, r_scr):
                r.finish(a, b, c)

    hbm = pl.BlockSpec(memory_space=pltpu.HBM)
    r_args = [a for r in riders for a in r.arrays]
    r_shapes = [s for r in riders for s in r.out_shape]
    return pl.pallas_call(
        wrapped if riders else body, name=name, grid=grid, in_specs=list(in_specs) + [hbm] * len(r_args),
        out_specs=list(out_specs) + [hbm] * len(r_shapes), out_shape=list(out_shape) + r_shapes,
        scratch_shapes=list(scratch) + [s for r in riders for s in r.scratch],
        compiler_params=_cparams(("arbitrary",) * len(grid)))(*args, *r_args)


def _dot(a, b):
    return jnp.dot(a.astype(BF16), b.astype(BF16), preferred_element_type=F32)


def _dot_nt(a, b):
    return lax.dot_general(a.astype(BF16), b.astype(BF16), (((1,), (1,)), ((), ())), preferred_element_type=F32)


def _dot_tn(a, b):
    return lax.dot_general(a.astype(BF16), b.astype(BF16), (((0,), (0,)), ((), ())), preferred_element_type=F32)


def _split(x, n):
    parts = []
    for _ in range(n):
        p = x.astype(BF16)
        parts.append(p)
        x = x - p.astype(F32)
    return parts


def _dot_exact_lhs(m, x, n):
    return functools.reduce(lambda a, b: a + b, [jnp.dot(m, p, preferred_element_type=F32) for p in _split(x, n)])


def _dot_exact_rhs(x, m, n):
    return functools.reduce(lambda a, b: a + b, [jnp.dot(p, m, preferred_element_type=F32) for p in _split(x, n)])


def _rms(x, g):
    r = lax.rsqrt(jnp.mean(x * x, axis=-1, keepdims=True) + EPS)
    return x * r * g, r


def _rms_bwd(x, r, g, dy):
    xr = x * r
    u = dy * g
    dx = r * (u - xr * jnp.mean(u * xr, axis=-1, keepdims=True))
    return dx, jnp.sum(dy * xr, axis=0, keepdims=True)


def _iota(shape, dim):
    return lax.broadcasted_iota(jnp.int32, shape, dim)


def _sigmoid(x):
    return 1.0 / (1.0 + jnp.exp(-x))


def _acc_rows(ref, row):
    ref[...] += jnp.broadcast_to(row, ref.shape)


def _inproj(x, g, w_t, riders=()):
    t = x.shape[0]
    tm = min(TM, t)

    def body(x_ref, g_ref, w_ref, z_ref, h_ref):
        h, _ = _rms(x_ref[...], g_ref[...])
        hb = h.astype(BF16)
        h_ref[...] = hb
        z_ref[...] = _dot_nt(hb, w_ref[...])

    return _call(
        body, "inproj", (t // tm,),
        [pl.BlockSpec((tm, D), lambda i: (i, 0)), pl.BlockSpec((1, D), lambda i: (0, 0)),
         pl.BlockSpec((ZC, D), lambda i: (0, 0))],
        [pl.BlockSpec((tm, ZC), lambda i: (i, 0)), pl.BlockSpec((tm, D), lambda i: (i, 0))],
        [jax.ShapeDtypeStruct((t, ZC), F32), jax.ShapeDtypeStruct((t, D), BF16)], [], (x, g, w_t), riders)


def _kv_proj(mem, g, w):
    def body(m_ref, g_ref, w_ref, kv_ref, mn_ref):
        mn, _ = _rms(m_ref[...], g_ref[...])
        mb = mn.astype(BF16)
        mn_ref[...] = mb
        for j in range(NDEV):
            kv_ref[:, j * XKV_SHARD:(j + 1) * XKV_SHARD] = jnp.dot(mb, w_ref[j], preferred_element_type=F32)

    return pl.pallas_call(
        body, name="kv_proj",
        out_shape=[jax.ShapeDtypeStruct((NMEM, 2 * D), F32), jax.ShapeDtypeStruct((NMEM, D), BF16)],
        compiler_params=pltpu.CompilerParams(vmem_limit_bytes=VMEM_LIMIT))(mem, g, w)


def _softmax_head(qb, kb):
    s = _dot_nt(qb, kb) * (1.0 / 16.0)
    e = jnp.exp(s - jnp.max(s, axis=-1, keepdims=True))
    return e / jnp.sum(e, axis=-1, keepdims=True)


def _attn_fwd(x, yb, w_out, g, w_xq, kb, vb, w_xo):
    t = x.shape[0]
    tm = min(TM, t)

    def body(x_ref, y_ref, wo_ref, g_ref, wq_ref, k_ref, v_ref, wx_ref, x1_ref, x2_ref, xn_ref, q_ref, a_ref):
        x1 = x_ref[...] + jnp.dot(y_ref[...], wo_ref[...], preferred_element_type=F32)
        x1_ref[...] = x1
        xn, _ = _rms(x1, g_ref[...])
        xb = xn.astype(BF16)
        xn_ref[...] = xb
        qb = jnp.dot(xb, wq_ref[...], preferred_element_type=F32).astype(BF16)
        q_ref[...] = qb
        for h in range(NH):
            hs = slice(h * XD, (h + 1) * XD)
            p = _softmax_head(qb[:, hs], k_ref[:, hs])
            a_ref[:, hs] = _dot(p, v_ref[:, hs]).astype(BF16)
        x2_ref[...] = x1 + jnp.dot(a_ref[...], wx_ref[...], preferred_element_type=F32)

    tok = lambda i: (i, 0)
    full = lambda i: (0, 0)
    return pl.pallas_call(
        body, name="attn_fwd", grid=(t // tm,),
        in_specs=[pl.BlockSpec((tm, D), tok), pl.BlockSpec((tm, D), tok), pl.BlockSpec((D, D), full),
                  pl.BlockSpec((1, D), full), pl.BlockSpec((D, D), full), pl.BlockSpec((NMEM, D), full),
                  pl.BlockSpec((NMEM, D), full), pl.BlockSpec((D, D), full)],
        out_specs=[pl.BlockSpec((tm, D), tok)] * 5,
        out_shape=[jax.ShapeDtypeStruct((t, D), F32), jax.ShapeDtypeStruct((t, D), F32),
                   jax.ShapeDtypeStruct((t, D), BF16), jax.ShapeDtypeStruct((t, D), BF16),
                   jax.ShapeDtypeStruct((t, D), BF16)],
        compiler_params=_cparams(("arbitrary",)))(x, yb, w_out, g, w_xq, kb, vb, w_xo)


def _mlp_fwd(x2, g, w_up_t, w_down, fg, target):
    t = x2.shape[0]
    tm = min(TM_MLP, t)

    def body(x_ref, g_ref, wu_ref, wd_ref, fg_ref, t_ref, h1_ref, xn_ref, dx_ref, dxb_ref, loss_ref, dfg_ref, ab):
        @pl.when(pl.program_id(0) == 0)
        def _():
            loss_ref[...] = jnp.zeros_like(loss_ref)
            dfg_ref[...] = jnp.zeros_like(dfg_ref)

        x = x_ref[...]
        xn, _ = _rms(x, g_ref[...])
        xnb = xn.astype(BF16)
        xn_ref[...] = xnb
        for q in range(FF // TF):
            cols = slice(q * TF, (q + 1) * TF)
            h1 = _dot_nt(xnb, wu_ref[cols, :])
            h1_ref[:, cols] = h1.astype(BF16)
            hr = jnp.maximum(h1, 0.0)
            ab[:, cols] = (hr * hr).astype(BF16)
        x3 = x + jnp.dot(ab[...], wd_ref[...], preferred_element_type=F32)
        y, r = _rms(x3, fg_ref[...])
        e = y - t_ref[...]
        row = jnp.mean(e * e, axis=-1, keepdims=True)
        _acc_rows(loss_ref, 0.5 * jnp.sum(row, axis=0, keepdims=True))
        dx, dfg = _rms_bwd(x3, r, fg_ref[...], e * (1.0 / D))
        dx_ref[...] = dx
        dxb_ref[...] = dx.astype(BF16)
        _acc_rows(dfg_ref, dfg)

    tok = lambda i: (i, 0)
    full = lambda i: (0, 0)
    once = pl.Buffered(1)
    return pl.pallas_call(
        body, name="mlp_fwd", grid=(t // tm,),
        in_specs=[pl.BlockSpec((tm, D), tok), pl.BlockSpec((1, D), full),
                  pl.BlockSpec((FF, D), full, pipeline_mode=once), pl.BlockSpec((FF, D), full, pipeline_mode=once),
                  pl.BlockSpec((1, D), full), pl.BlockSpec((tm, D), tok)],
        out_specs=[pl.BlockSpec((tm, FF), tok), pl.BlockSpec((tm, D), tok), pl.BlockSpec((tm, D), tok),
                   pl.BlockSpec((tm, D), tok), pl.BlockSpec((8, 128), full), pl.BlockSpec((8, D), full)],
        out_shape=[jax.ShapeDtypeStruct((t, FF), BF16), jax.ShapeDtypeStruct((t, D), BF16),
                   jax.ShapeDtypeStruct((t, D), F32), jax.ShapeDtypeStruct((t, D), BF16),
                   jax.ShapeDtypeStruct((8, 128), F32), jax.ShapeDtypeStruct((8, D), F32)],
        scratch_shapes=[pltpu.VMEM((tm, FF), BF16)],
        compiler_params=_cparams(("arbitrary",)))(x2, g, w_up_t, w_down, fg, target)


def _mlp_bwd(dx3, dx3b, h1b, w_down, w_up_t, x2, g):
    t = x2.shape[0]
    tm = min(TM_MLP, t)

    def body(dx_ref, dxb_ref, h1_ref, wd_ref, wu_ref, x_ref, g_ref, a_ref, dh_ref, dx2_ref, dx2b_ref, dg_ref):
        @pl.when(pl.program_id(0) == 0)
        def _():
            dg_ref[...] = jnp.zeros_like(dg_ref)

        for q in range(FF // TF):
            cols = slice(q * TF, (q + 1) * TF)
            hr = jnp.maximum(h1_ref[:, cols].astype(F32), 0.0)
            da = _dot_nt(dxb_ref[...], wd_ref[cols, :])
            a_ref[:, cols] = (hr * hr).astype(BF16)
            dh_ref[:, cols] = (da * 2.0 * hr).astype(BF16)
        dxn = jnp.dot(dh_ref[...], wu_ref[...], preferred_element_type=F32)
        x = x_ref[...]
        r = lax.rsqrt(jnp.mean(x * x, axis=-1, keepdims=True) + EPS)
        dx, dg = _rms_bwd(x, r, g_ref[...], dxn)
        dx2 = dx_ref[...] + dx
        dx2_ref[...] = dx2
        dx2b_ref[...] = dx2.astype(BF16)
        _acc_rows(dg_ref, dg)

    tok = lambda i: (i, 0)
    full = lambda i: (0, 0)
    once = pl.Buffered(1)
    return pl.pallas_call(
        body, name="mlp_bwd", grid=(t // tm,),
        in_specs=[pl.BlockSpec((tm, D), tok), pl.BlockSpec((tm, D), tok), pl.BlockSpec((tm, FF), tok),
                  pl.BlockSpec((FF, D), full, pipeline_mode=once), pl.BlockSpec((FF, D), full, pipeline_mode=once),
                  pl.BlockSpec((tm, D), tok), pl.BlockSpec((1, D), full)],
        out_specs=[pl.BlockSpec((tm, FF), tok), pl.BlockSpec((tm, FF), tok), pl.BlockSpec((tm, D), tok),
                   pl.BlockSpec((tm, D), tok), pl.BlockSpec((8, D), full)],
        out_shape=[jax.ShapeDtypeStruct((t, FF), BF16), jax.ShapeDtypeStruct((t, FF), BF16),
                   jax.ShapeDtypeStruct((t, D), F32), jax.ShapeDtypeStruct((t, D), BF16),
                   jax.ShapeDtypeStruct((8, D), F32)],
        compiler_params=_cparams(("arbitrary",)))(dx3, dx3b, h1b, w_down, w_up_t, x2, g)


def _attn_bwd(x1, dx2, dx2b, qb, kb, vb, w_xo, w_xq, w_out, g):
    t = x1.shape[0]
    tm = min(TM, t)

    def body(x_ref, dx2_ref, dx2b_ref, q_ref, k_ref, v_ref, wx_ref, wq_ref, wo_ref, g_ref,
             dx1_ref, dx1b_ref, dy_ref, dq_ref, dkv_ref, dg_ref):
        @pl.when(pl.program_id(0) == 0)
        def _():
            dkv_ref[...] = jnp.zeros_like(dkv_ref)
            dg_ref[...] = jnp.zeros_like(dg_ref)

        datt = _dot_nt(dx2b_ref[...], wx_ref[...]).astype(BF16)
        for h in range(NH):
            hs = slice(h * XD, (h + 1) * XD)
            q_h, k_h, v_h, da_h = q_ref[:, hs], k_ref[:, hs], v_ref[:, hs], datt[:, hs]
            p = _softmax_head(q_h, k_h)
            dp = _dot_nt(da_h, v_h)
            ds = (p * (dp - jnp.sum(dp * p, axis=-1, keepdims=True)) * (1.0 / 16.0)).astype(BF16)
            dq_ref[:, hs] = _dot(ds, k_h).astype(BF16)
            dkv_ref[:, hs] += _dot_tn(ds, q_h)
            dkv_ref[:, D + h * XD:D + (h + 1) * XD] += _dot_tn(p, da_h)
        dxn = _dot_nt(dq_ref[...], wq_ref[...])
        x = x_ref[...]
        r = lax.rsqrt(jnp.mean(x * x, axis=-1, keepdims=True) + EPS)
        dx, dg = _rms_bwd(x, r, g_ref[...], dxn)
        dx1 = dx2_ref[...] + dx
        dx1_ref[...] = dx1
        dx1b = dx1.astype(BF16)
        dx1b_ref[...] = dx1b
        dy_ref[...] = _dot_nt(dx1b, wo_ref[...])
        _acc_rows(dg_ref, dg)

    tok = lambda i: (i, 0)
    full = lambda i: (0, 0)
    return pl.pallas_call(
        body, name="attn_bwd", grid=(t // tm,),
        in_specs=[pl.BlockSpec((tm, D), tok), pl.BlockSpec((tm, D), tok), pl.BlockSpec((tm, D), tok),
                  pl.BlockSpec((tm, D), tok), pl.BlockSpec((NMEM, D), full), pl.BlockSpec((NMEM, D), full),
                  pl.BlockSpec((D, D), full), pl.BlockSpec((D, D), full), pl.BlockSpec((D, D), full),
                  pl.BlockSpec((1, D), full)],
        out_specs=[pl.BlockSpec((tm, D), tok), pl.BlockSpec((tm, D), tok), pl.BlockSpec((tm, D), tok),
                   pl.BlockSpec((tm, D), tok), pl.BlockSpec((NMEM, 2 * D), full), pl.BlockSpec((8, D), full)],
        out_shape=[jax.ShapeDtypeStruct((t, D), F32), jax.ShapeDtypeStruct((t, D), BF16),
                   jax.ShapeDtypeStruct((t, D), F32), jax.ShapeDtypeStruct((t, D), BF16),
                   jax.ShapeDtypeStruct((NMEM, 2 * D), F32), jax.ShapeDtypeStruct((8, D), F32)],
        compiler_params=_cparams(("arbitrary",)))(x1, dx2, dx2b, qb, kb, vb, w_xo, w_xq, w_out, g)


def _kv_bwd(dkv, memn, mem, g, w):
    def body(dkv_ref, mn_ref, m_ref, g_ref, w_ref, dw_ref, dg_ref):
        dkvb = dkv_ref[...].astype(BF16)
        dmn = jnp.zeros((NMEM, D), F32)
        for j in range(NDEV):
            cols = slice(j * XKV_SHARD, (j + 1) * XKV_SHARD)
            dw_ref[j] = _dot_tn(mn_ref[...], dkvb[:, cols])
            dmn += _dot_nt(dkvb[:, cols], w_ref[j])
        m = m_ref[...]
        r = lax.rsqrt(jnp.mean(m * m, axis=-1, keepdims=True) + EPS)
        dg_ref[...] = jnp.broadcast_to(jnp.sum(dmn * m * r, axis=0, keepdims=True), dg_ref.shape)

    return pl.pallas_call(
        body, name="kv_bwd",
        out_shape=[jax.ShapeDtypeStruct((NDEV, D, XKV_SHARD), F32), jax.ShapeDtypeStruct((8, D), F32)],
        compiler_params=pltpu.CompilerParams(vmem_limit_bytes=VMEM_LIMIT))(dkv, memn, mem, g, w)


def _inproj_bwd(dz, w_t, x, dx1, g, riders=()):
    t = x.shape[0]
    tm = min(TM, t)

    def body(dz_ref, w_ref, x_ref, dx1_ref, g_ref, gx_ref, dg_ref):
        @pl.when(pl.program_id(0) == 0)
        def _():
            dg_ref[...] = jnp.zeros_like(dg_ref)

        dh = jnp.dot(dz_ref[...], w_ref[...], preferred_element_type=F32)
        x = x_ref[...]
        r = lax.rsqrt(jnp.mean(x * x, axis=-1, keepdims=True) + EPS)
        dx, dg = _rms_bwd(x, r, g_ref[...], dh)
        gx_ref[...] = dx1_ref[...] + dx
        _acc_rows(dg_ref, dg)

    tok = lambda i: (i, 0)
    full = lambda i: (0, 0)
    return _call(
        body, "inproj_bwd", (t // tm,),
        [pl.BlockSpec((tm, ZC), tok), pl.BlockSpec((ZC, D), full), pl.BlockSpec((tm, D), tok),
         pl.BlockSpec((tm, D), tok), pl.BlockSpec((1, D), full)],
        [pl.BlockSpec((tm, D), tok), pl.BlockSpec((8, D), full)],
        [jax.ShapeDtypeStruct((t, D), F32), jax.ShapeDtypeStruct((8, D), F32)], [], (dz, w_t, x, dx1, g), riders)


def _matmul_tn(a, b, name, riders=()):
    t, k = a.shape
    n = b.shape[1]
    tk, tn = [1024 if size % 1024 == 0 else 640 for size in (k, n)]
    tt = min(TT, t)

    def body(a_ref, b_ref, o_ref):
        @pl.when(pl.program_id(2) == 0)
        def _():
            o_ref[...] = jnp.zeros_like(o_ref)

        o_ref[...] += _dot_tn(a_ref[...], b_ref[...])

    return _call(
        body, name, (k // tk, n // tn, t // tt),
        [pl.BlockSpec((tt, tk), lambda i, j, s: (s, i)), pl.BlockSpec((tt, tn), lambda i, j, s: (s, j))],
        [pl.BlockSpec((tk, tn), lambda i, j, s: (i, j))], [jax.ShapeDtypeStruct((k, n), F32)], [], (a, b), riders)


def _lane_head(shape, dim, shift):
    return _iota(shape, dim) >> shift


def _gla_recompute(q_raw, k, lr, wpad, bias, rev, tb):
    pre = _dot(lr, wpad) + bias
    la = (jnp.minimum(pre, 0.0) - jnp.log(1.0 + jnp.exp(-jnp.abs(pre)))) * (1.0 / 16.0)
    r, c = _iota((tb, tb), 0), _iota((tb, tb), 1)
    tri = (c >= r) if rev else (c <= r)
    cum = jnp.where(((r >> 6) == (c >> 6)) & tri, 1.0, 0.0).astype(BF16)
    b = _dot_exact_lhs(cum, la, 3)
    e, ei = jnp.exp(b), jnp.exp(-b)
    qt = (q_raw * 0.125) * e
    kt = k * ei
    return pre, b, e, ei, qt, kt


def _stack_heads(x, shift):
    head = _lane_head(x.shape, 1, shift)
    return jnp.concatenate([jnp.where(head == h, x, 0.0) for h in range(NH)], axis=0).astype(BF16)


def _fold_heads(x, shift):
    head = _lane_head((CH, x.shape[1]), 1, shift)
    return functools.reduce(lambda a, b: a + b,
                            [jnp.where(head == h, x[h * CH:(h + 1) * CH], 0.0) for h in range(NH)])


def _wide_mask(rev):
    r, s = _iota((CH, NH * CH), 0), _iota((CH, NH * CH), 1) & (CH - 1)
    return (s >= r) if rev else (s <= r)


def _state_mask():
    return (_iota((GV, GK), 0) >> 7) == (_iota((GV, GK), 1) >> 6)


def _state_expand(sd):
    head = _lane_head(sd.shape, 1, 6)
    return jnp.concatenate([jnp.where(head == h, sd, 0.0) for h in range(NH)], axis=0)


def _conv_parts(cb, cc, cu, ccp, cup, ccn, cun, cw_ref, first, last, tb):
    h = cc * cu
    hp = jnp.where(first, 0.0, ccp * cup)
    hn = jnp.where(last, 0.0, ccn * cun)
    rows = _iota(h.shape, 0)
    h_m1 = jnp.where(rows == 0, hp, pltpu.roll(h, 1, 0))
    h_p1 = jnp.where(rows == tb - 1, hn, pltpu.roll(h, tb - 1, 0))
    conv = cw_ref[pl.ds(0, 1), :] * h_m1 + cw_ref[pl.ds(1, 1), :] * h + cw_ref[pl.ds(2, 1), :] * h_p1
    return h, h_m1, h_p1, conv


def _group_ones():
    return jnp.where((_iota((CW, CW), 0) >> 6) == (_iota((CW, CW), 1) >> 6), 1.0, 0.0).astype(BF16)


def _head_norm(o):
    ons, rs = [], []
    for h in range(NH):
        slab = o[:, h * 128:(h + 1) * 128]
        r = lax.rsqrt(jnp.mean(slab * slab, axis=-1, keepdims=True) + EPS)
        ons.append(slab * r)
        rs.append(jnp.broadcast_to(r, slab.shape))
    return jnp.concatenate(ons, axis=1), jnp.concatenate(rs, axis=1)


def _zspec(tb, width, blk, jmap):
    return pl.BlockSpec((tb, width), lambda i: (jmap(i), blk))


def _halo_specs(tb, nblk, t, blk, jmap):
    prev = pl.BlockSpec((8, CW), lambda i: (jnp.maximum(jmap(i) * (tb // 8) - 1, 0), blk))
    nxt = pl.BlockSpec((8, CW), lambda i: (jnp.minimum((jmap(i) + 1) * (tb // 8), t // 8 - 1), blk))
    return prev, nxt


def _gla_fwd_sweep(z, wpad, bias, rev, finish_args=None, riders=()):
    t = z.shape[0]
    tb = min(TB, t)
    nblk, nb = t // tb, tb // CH
    jmap = (lambda i: nblk - 1 - i) if rev else (lambda i: i)
    finish = finish_args is not None

    def body(*refs):
        if finish:
            (q_ref, k_ref, v_ref, lr_ref, w_ref, bias_ref, of_ref, g_ref, cb_ref, cc_ref, cu_ref, ccp_ref, ccn_ref,
             cup_ref, cun_ref, cw_ref, cn_ref, gn_ref, y_ref, opre_ref, sd_ref, st, b_scr, o_scr) = refs
        else:
            q_ref, k_ref, v_ref, lr_ref, w_ref, bias_ref, o_ref, sd_ref, st, b_scr = refs
            o_scr = o_ref
        i = pl.program_id(0)

        @pl.when(i == 0)
        def _():
            st[...] = jnp.zeros_like(st)

        q_raw, k, v = q_ref[...], k_ref[...], v_ref[...]
        _, b, _, _, qt, kt = _gla_recompute(q_raw, k, lr_ref[...], w_ref[...], bias_ref[...], rev, tb)
        b_scr[...] = b
        maskw, bd = _wide_mask(rev), _state_mask()
        order = list(reversed(range(nb))) if rev else list(range(nb))
        rows = [slice(c * CH, (c + 1) * CH) for c in range(nb)]
        gdec = {c: jnp.exp(b_scr[pl.ds(c * CH + (0 if rev else CH - 1), 1), :]) for c in order}
        upd = {c: jnp.where(bd, _dot_tn(v[rows[c]], kt[rows[c]] * gdec[c]), 0.0) for c in order}
        o_intra = {c: _dot(jnp.where(maskw, _dot_nt(qt[rows[c]], _stack_heads(kt[rows[c]], 6)), 0.0),
                           _stack_heads(v[rows[c]], 7)) for c in order}
        s_in, state = {}, st[...]
        for c in order:
            s_in[c] = state
            state = state * gdec[c] + upd[c]
        st[...] = state
        for c in order:
            sd_ref[c] = s_in[c][0:128] + s_in[c][128:256] + s_in[c][256:384] + s_in[c][384:512]
            o_scr[pl.ds(c * CH, CH), :] = o_intra[c] + _dot_nt(qt[rows[c]], s_in[c])

        if finish:
            j = jmap(i)
            hsel = jnp.where((_iota((GK, GV), 0) >> 6) == (_iota((GK, GV), 1) >> 7), 1.0, 0.0).astype(BF16)
            sb = _dot_exact_rhs((q_raw * 0.125) * k, hsel, 2)
            o_pre = of_ref[...] + o_scr[...] - sb * v
            opre_ref[...] = o_pre
            on, _ = _head_norm(o_pre)
            g = g_ref[...]
            y_ref[:, CW:] = (on * gn_ref[...] * (g * _sigmoid(g))).astype(BF16)
            cb = cb_ref[...]
            _, _, _, conv = _conv_parts(cb, cc_ref[...], cu_ref[...], ccp_ref[pl.ds(7, 1), :], cup_ref[pl.ds(7, 1), :],
                                        ccn_ref[pl.ds(0, 1), :], cun_ref[pl.ds(0, 1), :], cw_ref, j == 0,
                                        j == nblk - 1, tb)
            yc = cb * conv
            gm = _dot_exact_rhs(yc * yc, _group_ones(), 2) * (1.0 / 64.0)
            y_ref[:, :CW] = (yc * lax.rsqrt(gm + EPS) * cn_ref[...]).astype(BF16)

    full = lambda i: (0, 0)
    in_specs = [_zspec(tb, GK, ZB_Q, jmap), _zspec(tb, GK, ZB_K, jmap), _zspec(tb, GV, ZB_V, jmap),
                _zspec(tb, 128, ZB_LR, jmap), pl.BlockSpec((128, GK), full), pl.BlockSpec((1, GK), full)]
    args = [z, z, z, z, wpad, bias]
    sd_spec = pl.BlockSpec((nb, 128, GK), lambda i: (jmap(i), 0, 0))
    sd_shape = jax.ShapeDtypeStruct((t // CH, 128, GK), F32)
    scratch = [pltpu.VMEM((GV, GK), F32), pltpu.VMEM((tb, GK), F32)]
    if finish:
        o_f, conv_w, conv_norm, gla_norm4 = finish_args
        ccp, ccn = _halo_specs(tb, nblk, t, ZB_CC, jmap)
        cup, cun = _halo_specs(tb, nblk, t, ZB_CU, jmap)
        in_specs += [pl.BlockSpec((tb, GV), lambda i: (jmap(i), 0)), _zspec(tb, GV, ZB_G, jmap),
                     _zspec(tb, CW, ZB_CB, jmap), _zspec(tb, CW, ZB_CC, jmap), _zspec(tb, CW, ZB_CU, jmap),
                     ccp, ccn, cup, cun, pl.BlockSpec((3, CW), full), pl.BlockSpec((1, CW), full),
                     pl.BlockSpec((1, GV), full)]
        args += [o_f, z, z, z, z, z, z, z, z, conv_w, conv_norm, gla_norm4]
        out_specs = [pl.BlockSpec((tb, D), lambda i: (jmap(i), 0)), pl.BlockSpec((tb, GV), lambda i: (jmap(i), 0)),
                     sd_spec]
        out_shape = [jax.ShapeDtypeStruct((t, D), BF16), jax.ShapeDtypeStruct((t, GV), F32), sd_shape]
        scratch.append(pltpu.VMEM((tb, GV), F32))
    else:
        out_specs = [pl.BlockSpec((tb, GV), lambda i: (jmap(i), 0)), sd_spec]
        out_shape = [jax.ShapeDtypeStruct((t, GV), F32), sd_shape]
    return _call(body, "gla_fwd_rev" if rev else "gla_fwd", (nblk,), in_specs, out_specs, out_shape, scratch, args,
                 riders)


def _gla_bwd_chunks(do_ref, sd_ref, dst, b_scr, db_scr, dq_ref, dk_ref, dv_ref, qt, kt, e, ei, v, rev, nb):
    maskw, bd = _wide_mask(rev), _state_mask()
    for c in (range(nb) if rev else reversed(range(nb))):
        sl = slice(c * CH, (c + 1) * CH)
        grow = c * CH + (0 if rev else CH - 1)
        gdec = jnp.exp(b_scr[pl.ds(grow, 1), :])
        qt_c, kt_c, v_c, do_c = qt[sl], kt[sl], v[sl], do_ref[pl.ds(c * CH, CH), :]
        s_in = _state_expand(sd_ref[c])
        ds_out = dst[...]
        kbd, vbd = _stack_heads(kt_c, 6), _stack_heads(v_c, 7)
        a = jnp.where(maskw, _dot_nt(qt_c, kbd), 0.0)
        da = jnp.where(maskw, _dot_nt(do_c, vbd), 0.0)
        dv_ref[pl.ds(c * CH, CH), :] = _fold_heads(_dot_tn(a, do_c), 7) + _dot_nt(kt_c * gdec, ds_out)
        dqt = _dot(da, kbd) + _dot(do_c, s_in)
        dkh = _dot(v_c, ds_out)
        dkt = _fold_heads(_dot_tn(da, qt_c), 6) + dkh * gdec
        dg = jnp.sum(ds_out * s_in, axis=0, keepdims=True) + jnp.sum(kt_c * dkh, axis=0, keepdims=True)
        db_scr[pl.ds(c * CH, CH), :] = dqt * qt_c - dkt * kt_c
        db_scr[pl.ds(grow, 1), :] += dg * gdec
        dq_ref[pl.ds(c * CH, CH), :] = dqt * e[sl] * 0.125
        dk_ref[pl.ds(c * CH, CH), :] = dkt * ei[sl]
        dst[...] = ds_out * gdec + jnp.where(bd, _dot_tn(do_c, qt_c), 0.0)


def _gate_bwd(db, pre, lr, wpad, rev, tb):
    r, c = _iota((tb, tb), 0), _iota((tb, tb), 1)
    tri = (c <= r) if rev else (c >= r)
    cum_t = jnp.where(((r >> 6) == (c >> 6)) & tri, 1.0, 0.0).astype(BF16)
    dla = _dot_exact_lhs(cum_t, db, 2)
    dpre = dla * (1.0 / 16.0) / (1.0 + jnp.exp(pre))
    return dpre, _dot_nt(dpre, wpad), _dot_tn(lr, dpre)


def _gla_bwd_first(z, dy, o_pre, sd, wpad, bias, conv_w, conv_norm, gla_norm4, riders=()):
    t = z.shape[0]
    tb = min(TB, t)
    nblk, nb = t // tb, tb // CH
    jmap = lambda i: nblk - 1 - i

    def body(q_ref, k_ref, v_ref, lr_ref, g_ref, cb_ref, cc_ref, cu_ref, ccp_ref, ccn_ref, cup_ref, cun_ref,
             dy_ref, opre_ref, sd_ref, w_ref, bias_ref, cw_ref, cn_ref, gn_ref,
             do_ref, dq_ref, dk_ref, dv_ref, dlr_ref, dzg_ref, dzcb_ref, dconv_ref,
             dw_ref, dbias_ref, dcw_ref, dcn_ref, dgn_ref, dst, b_scr, db_scr):
        i = pl.program_id(0)
        j = jmap(i)

        @pl.when(i == 0)
        def _():
            dst[...] = jnp.zeros_like(dst)
            for ref in (dw_ref, dbias_ref, dcw_ref, dcn_ref, dgn_ref):
                ref[...] = jnp.zeros_like(ref)

        dyg = dy_ref[:, CW:]
        g = g_ref[...]
        sig = _sigmoid(g)
        on, rr = _head_norm(opre_ref[...])
        gn = gn_ref[...]
        dzg_ref[...] = (dyg * on * gn * (sig * (1.0 + g * (1.0 - sig)))).astype(BF16)
        don = dyg * (g * sig)
        _acc_rows(dgn_ref, jnp.sum(don * on, axis=0, keepdims=True))
        u = don * gn
        uo = u * on
        mean_uo = jnp.concatenate(
            [jnp.broadcast_to(jnp.mean(uo[:, h * 128:(h + 1) * 128], axis=-1, keepdims=True), (tb, 128))
             for h in range(NH)], axis=1)
        do_ref[...] = rr * (u - on * mean_uo)

        cb = cb_ref[...]
        h, h_m1, h_p1, conv = _conv_parts(cb, cc_ref[...], cu_ref[...], ccp_ref[pl.ds(7, 1), :],
                                          cup_ref[pl.ds(7, 1), :], ccn_ref[pl.ds(0, 1), :], cun_ref[pl.ds(0, 1), :],
                                          cw_ref, j == 0, j == nblk - 1, tb)
        yc = cb * conv
        ones = _group_ones()
        rc = lax.rsqrt(_dot_exact_rhs(yc * yc, ones, 2) * (1.0 / 64.0) + EPS)
        ycr = yc * rc
        dyn = dy_ref[:, :CW]
        _acc_rows(dcn_ref, jnp.sum(dyn * ycr, axis=0, keepdims=True))
        uc = dyn * cn_ref[...]
        dyc = rc * (uc - ycr * (_dot_exact_rhs(uc * ycr, ones, 2) * (1.0 / 64.0)))
        dzcb_ref[...] = (dyc * conv).astype(BF16)
        dconv = dyc * cb
        dconv_ref[...] = dconv
        dcw_ref[pl.ds(0, 1), :] += jnp.sum(dconv * h_m1, axis=0, keepdims=True)
        dcw_ref[pl.ds(1, 1), :] += jnp.sum(dconv * h, axis=0, keepdims=True)
        dcw_ref[pl.ds(2, 1), :] += jnp.sum(dconv * h_p1, axis=0, keepdims=True)

        lr, wp = lr_ref[...], w_ref[...]
        pre, b, e, ei, qt, kt = _gla_recompute(q_ref[...], k_ref[...], lr, wp, bias_ref[...], False, tb)
        b_scr[...] = b
        _gla_bwd_chunks(do_ref, sd_ref, dst, b_scr, db_scr, dq_ref, dk_ref, dv_ref, qt, kt, e, ei, v_ref[...],
                        False, nb)
        dpre, dlr, dw = _gate_bwd(db_scr[...], pre, lr, wp, False, tb)
        dlr_ref[...] = dlr
        dw_ref[...] += dw
        _acc_rows(dbias_ref, jnp.sum(dpre, axis=0, keepdims=True))

    full = lambda i: (0, 0)
    tokv = pl.BlockSpec((tb, GV), lambda i: (jmap(i), 0))
    tokk = pl.BlockSpec((tb, GK), lambda i: (jmap(i), 0))
    ccp, ccn = _halo_specs(tb, nblk, t, ZB_CC, jmap)
    cup, cun = _halo_specs(tb, nblk, t, ZB_CU, jmap)
    in_specs = [_zspec(tb, GK, ZB_Q, jmap), _zspec(tb, GK, ZB_K, jmap), _zspec(tb, GV, ZB_V, jmap),
                _zspec(tb, 128, ZB_LR, jmap), _zspec(tb, GV, ZB_G, jmap), _zspec(tb, CW, ZB_CB, jmap),
                _zspec(tb, CW, ZB_CC, jmap), _zspec(tb, CW, ZB_CU, jmap), ccp, ccn, cup, cun,
                pl.BlockSpec((tb, D), lambda i: (jmap(i), 0)), tokv,
                pl.BlockSpec((nb, 128, GK), lambda i: (jmap(i), 0, 0)), pl.BlockSpec((128, GK), full),
                pl.BlockSpec((1, GK), full), pl.BlockSpec((3, CW), full), pl.BlockSpec((1, CW), full),
                pl.BlockSpec((1, GV), full)]
    out_specs = [tokv, tokk, tokk, tokv, pl.BlockSpec((tb, 128), lambda i: (jmap(i), 0)), tokv, tokv, tokv,
                 pl.BlockSpec((128, GK), full), pl.BlockSpec((8, GK), full), pl.BlockSpec((8, CW), full),
                 pl.BlockSpec((8, CW), full), pl.BlockSpec((8, GV), full)]
    out_shape = [jax.ShapeDtypeStruct((t, GV), F32), jax.ShapeDtypeStruct((t, GK), F32),
                 jax.ShapeDtypeStruct((t, GK), F32), jax.ShapeDtypeStruct((t, GV), F32),
                 jax.ShapeDtypeStruct((t, 128), F32), jax.ShapeDtypeStruct((t, GV), BF16),
                 jax.ShapeDtypeStruct((t, CW), BF16), jax.ShapeDtypeStruct((t, CW), F32),
                 jax.ShapeDtypeStruct((128, GK), F32), jax.ShapeDtypeStruct((8, GK), F32),
                 jax.ShapeDtypeStruct((8, CW), F32), jax.ShapeDtypeStruct((8, CW), F32),
                 jax.ShapeDtypeStruct((8, GV), F32)]
    return _call(
        body, "gla_bwd_first", (nblk,), in_specs, out_specs, out_shape,
        [pltpu.VMEM((GV, GK), F32), pltpu.VMEM((tb, GK), F32), pltpu.VMEM((tb, GK), F32)],
        (z, z, z, z, z, z, z, z, z, z, z, z, dy, o_pre, sd, wpad, bias, conv_w, conv_norm, gla_norm4), riders)


def _gla_bwd_second(z, do, sd, wpad, bias, dqa, dka, dva, dlra, dzg, dzcb, dconv, conv_w, riders=()):
    t = z.shape[0]
    tb = min(TB, t)
    nblk, nb = t // tb, tb // CH
    jmap = lambda i: i

    def body(q_ref, k_ref, v_ref, lr_ref, cc_ref, cu_ref, do_ref, sd_ref, w_ref, bias_ref, dqa_ref, dka_ref,
             dva_ref, dlra_ref, dzg_ref, dzcb_ref, dc_ref, dcp_ref, dcn_ref, cw_ref,
             dz_ref, dw_ref, dbias_ref, dst, b_scr, db_scr, dq_scr, dk_scr, dv_scr):
        i = pl.program_id(0)

        @pl.when(i == 0)
        def _():
            dst[...] = jnp.zeros_like(dst)
            dw_ref[...] = jnp.zeros_like(dw_ref)
            dbias_ref[...] = jnp.zeros_like(dbias_ref)

        q_raw, k, v, lr, wp = q_ref[...], k_ref[...], v_ref[...], lr_ref[...], w_ref[...]
        pre, b, e, ei, qt, kt = _gla_recompute(q_raw, k, lr, wp, bias_ref[...], True, tb)
        b_scr[...] = b
        _gla_bwd_chunks(do_ref, sd_ref, dst, b_scr, db_scr, dq_scr, dk_scr, dv_scr, qt, kt, e, ei, v, True, nb)
        dpre, dlr, dw = _gate_bwd(db_scr[...], pre, lr, wp, True, tb)
        dw_ref[...] += dw
        _acc_rows(dbias_ref, jnp.sum(dpre, axis=0, keepdims=True))

        do = do_ref[...]
        qs = q_raw * 0.125
        hsel = jnp.where((_iota((GK, GV), 0) >> 6) == (_iota((GK, GV), 1) >> 7), 1.0, 0.0).astype(BF16)
        hsel_t = jnp.where((_iota((GV, GK), 0) >> 7) == (_iota((GV, GK), 1) >> 6), 1.0, 0.0).astype(BF16)
        sb = _dot_exact_rhs(qs * k, hsel, 2)
        dsk = _dot_exact_rhs(do * v, hsel_t, 2)
        dz_ref[:, 1536:1792] = (dqa_ref[...] + dq_scr[...] - dsk * k * 0.125).astype(BF16)
        dz_ref[:, 1792:2048] = (dka_ref[...] + dk_scr[...] - dsk * qs).astype(BF16)
        dz_ref[:, 2048:2560] = (dva_ref[...] + dv_scr[...] - sb * do).astype(BF16)
        dz_ref[:, 2560:3072] = dzg_ref[...]
        dz_ref[:, 3072:3200] = (dlra_ref[...] + dlr).astype(BF16)

        dc = dc_ref[...]
        rows = _iota(dc.shape, 0)
        dprev = jnp.where(i == 0, 0.0, dcp_ref[pl.ds(7, 1), :])
        dnext = jnp.where(i == nblk - 1, 0.0, dcn_ref[pl.ds(0, 1), :])
        dc_m1 = jnp.where(rows == 0, dprev, pltpu.roll(dc, 1, 0))
        dc_p1 = jnp.where(rows == tb - 1, dnext, pltpu.roll(dc, tb - 1, 0))
        dh = cw_ref[pl.ds(0, 1), :] * dc_p1 + cw_ref[pl.ds(1, 1), :] * dc + cw_ref[pl.ds(2, 1), :] * dc_m1
        dz_ref[:, 0:512] = dzcb_ref[...]
        dz_ref[:, 512:1024] = (dh * cu_ref[...]).astype(BF16)
        dz_ref[:, 1024:1536] = (dh * cc_ref[...]).astype(BF16)

    full = lambda i: (0, 0)
    tokv = pl.BlockSpec((tb, GV), lambda i: (i, 0))
    tokk = pl.BlockSpec((tb, GK), lambda i: (i, 0))
    dcp = pl.BlockSpec((8, CW), lambda i: (jnp.maximum(i * (tb // 8) - 1, 0), 0))
    dcn = pl.BlockSpec((8, CW), lambda i: (jnp.minimum((i + 1) * (tb // 8), t // 8 - 1), 0))
    in_specs = [_zspec(tb, GK, ZB_Q, jmap), _zspec(tb, GK, ZB_K, jmap), _zspec(tb, GV, ZB_V, jmap),
                _zspec(tb, 128, ZB_LR, jmap), _zspec(tb, CW, ZB_CC, jmap), _zspec(tb, CW, ZB_CU, jmap), tokv,
                pl.BlockSpec((nb, 128, GK), lambda i: (i, 0, 0)), pl.BlockSpec((128, GK), full),
                pl.BlockSpec((1, GK), full), tokk, tokk, tokv, pl.BlockSpec((tb, 128), lambda i: (i, 0)), tokv, tokv,
                tokv, dcp, dcn, pl.BlockSpec((3, CW), full)]
    out_specs = [pl.BlockSpec((tb, ZC), lambda i: (i, 0)), pl.BlockSpec((128, GK), full), pl.BlockSpec((8, GK), full)]
    out_shape = [jax.ShapeDtypeStruct((t, ZC), BF16), jax.ShapeDtypeStruct((128, GK), F32),
                 jax.ShapeDtypeStruct((8, GK), F32)]
    return _call(
        body, "gla_bwd_second", (nblk,), in_specs, out_specs, out_shape,
        [pltpu.VMEM((GV, GK), F32), pltpu.VMEM((tb, GK), F32), pltpu.VMEM((tb, GK), F32),
         pltpu.VMEM((tb, GK), F32), pltpu.VMEM((tb, GK), F32), pltpu.VMEM((tb, GV), F32)],
        (z, z, z, z, z, z, do, sd, wpad, bias, dqa, dka, dva, dlra, dzg, dzcb, dconv, dconv, dconv, conv_w), riders)


def _step(x, mem, target, shard, small_pack, vec, place):
    own, from_chips = {}, {}

    def pair_sums(names, g4, from_sibling):
        pbs = []
        for n, g, s in zip(names, g4, from_sibling):
            pb, own[n] = _rs_pair_sum(place, g, s, "pair_sum_" + n)
            pbs.append(pb)
        return pbs

    def by_dest(g, n):
        return g.reshape((4, 2) + shard[n].shape)

    w_in, small_all = _exchange(_gather_rider([shard["w_in"], small_pack]), "gather_w_in")
    w_in = jnp.pad(w_in.reshape(ZW, D), ((0, ZC - ZW), (0, 0)))
    small_all = small_all.reshape(NDEV, -1)
    p, off = {}, 0
    for n, (r, c) in SMALL_SHARDED.items():
        p[n] = small_all[:, off:off + r * c].reshape(NDEV, r, c).transpose(1, 0, 2).reshape(r, NDEV * c)
        off += r * c
    zeros_lr = jnp.zeros((128 - LR, GK), BF16)
    waf_pad = jnp.concatenate([p["w_af"].astype(BF16), zeros_lr], axis=0)
    wab_pad = jnp.concatenate([jnp.zeros((LR, GK), BF16), p["w_ab"].astype(BF16), zeros_lr[:128 - 2 * LR]], axis=0)
    gla_norm4 = jnp.tile(vec["gla_norm"], (1, NH))

    z, hb, w_out, w_xq, w_xo, w_xkv = _inproj(
        x, vec["mix_norm"], w_in, [_gather_rider([shard[n] for n in ("w_out", "w_xq", "w_xo", "w_xkv")])])
    w_out, w_xq, w_xo = [a.reshape(D, D) for a in (w_out, w_xq, w_xo)]
    o_f, sd_f, w_up_t = _gla_fwd_sweep(z, waf_pad, vec["b_af"], False, riders=[_gather_rider([shard["w_up"]])])
    w_up_t = w_up_t.reshape(FF, D)
    yb, o_pre, sd_b, w_down = _gla_fwd_sweep(z, wab_pad, vec["b_ab"], True,
                                             (o_f, p["conv_w"], vec["conv_norm"], gla_norm4),
                                             riders=[_gather_rider([shard["w_down"]])])
    w_down = w_down.reshape(FF, D)
    kv, memn = _kv_proj(mem, vec["mem_norm"], w_xkv)
    kb, vb = kv[:, :D].astype(BF16), kv[:, D:].astype(BF16)
    x1, x2, xn1, qb, attb = _attn_fwd(x, yb, w_out, vec["xa_norm"], w_xq, kb, vb, w_xo)
    h1b, xn2, dx3, dx3b, loss8, dfinal = _mlp_fwd(x2, vec["mlp_norm"], w_up_t, w_down, vec["final_norm"], target)

    ab, dh1b, dx2, dx2b, dmlp = _mlp_bwd(dx3, dx3b, h1b, w_down, w_up_t, x2, vec["mlp_norm"])
    g_mlp = [by_dest(_matmul_tn(ab, dx3b, "dw_down")[0], "w_down"),
             by_dest(_matmul_tn(dh1b, xn2, "dw_up")[0], "w_up")]
    dx1, dx1b, dy, dqb, dkv, dxa = _attn_bwd(x1, dx2, dx2b, qb, kb, vb, w_xo, w_xq, w_out, vec["xa_norm"])
    dw_xo, *s_mlp = _matmul_tn(attb, dx2b, "dw_xo", riders=[_sibling_rider(g_mlp)])
    pb_mlp = pair_sums(("w_down", "w_up"), g_mlp, s_mlp)
    dw_xkv, dmemn = _kv_bwd(dkv, memn, mem, vec["mem_norm"], w_xkv)
    att_names = ("w_xo", "w_xq", "w_out", "w_xkv")
    g_att = [by_dest(g, n) for g, n in zip(
        (dw_xo, _matmul_tn(xn1, dqb, "dw_xq")[0], _matmul_tn(yb, dx1b, "dw_out")[0], dw_xkv), att_names)]
    res = _gla_bwd_first(z, dy, o_pre, sd_f, waf_pad, vec["b_af"], p["conv_w"], vec["conv_norm"], gla_norm4,
                         riders=[_chips_rider(pb_mlp), _sibling_rider(g_att)])
    do, dqa, dka, dva, dlra, dzg, dzcb, dconv, dwaf, dbaf, dcw, dcn, dgn = res[:13]
    from_chips["w_down"], from_chips["w_up"] = res[13:15]
    pb_att = pair_sums(att_names, g_att, res[15:])
    dz, dwab, dbab, *c_att = _gla_bwd_second(z, do, sd_b, wab_pad, vec["b_ab"], dqa, dka, dva, dlra, dzg, dzcb, dconv,
                                             p["conv_w"], riders=[_chips_rider(pb_att)])
    from_chips.update(zip(att_names, c_att))
    g_in = [by_dest(_matmul_tn(dz, hb, "dw_in")[0][:ZW], "w_in")]
    pb_in = pair_sums(("w_in",), g_in, _exchange(_sibling_rider(g_in), "grads_to_sibling_w_in"))
    grad_x, dmix, from_chips["w_in"] = _inproj_bwd(dz, w_in, x, dx1, vec["mix_norm"], riders=[_chips_rider(pb_in)])

    small_grads = {
        "mix_norm": dmix[0:1], "conv_w": dcw[0:3], "conv_norm": dcn[0:1],
        "w_af": dwaf[0:LR], "b_af": dbaf[0:1], "w_ab": dwab[LR:2 * LR], "b_ab": dbab[0:1],
        "gla_norm": (dgn[0:1, 0:128] + dgn[0:1, 128:256]) + (dgn[0:1, 256:384] + dgn[0:1, 384:512]),
        "xa_norm": dxa[0:1], "mem_norm": dmemn[0:1], "mlp_norm": dmlp[0:1], "final_norm": dfinal[0:1],
    }
    return loss8[0:1, 0:1], grad_x, small_grads, own, from_chips


def _place():
    return lax.axis_index("x"), lax.axis_index("y"), lax.axis_index("c")


class _Rider:
    def __init__(self, arrays, out_shape, scratch, start, finish):
        self.arrays, self.out_shape, self.scratch, self.start, self.finish = arrays, out_shape, scratch, start, finish


def _gather_rider(blks):
    n = len(blks)

    def plan(in_refs, out_refs, sems):
        send_sems, recv_sems, local_sems = sems
        x, y, c = _place()
        me, sibling = (x, y, c), (x, y, 1 - c)
        chips = [(1 - x, y, c), (x, 1 - y, c), (1 - x, 1 - y, c)]

        def copy(a, k, block, to, own=False):
            px, py, pc = block
            dst = out_refs[a].at[4 * px + 2 * py + pc]
            return pltpu.make_async_remote_copy(
                src_ref=in_refs[a] if own else dst, dst_ref=dst, send_sem=send_sems.at[k, a],
                recv_sem=recv_sems.at[k, a], device_id=to, device_id_type=MESH)

        def local(a):
            return pltpu.make_async_copy(in_refs[a], out_refs[a].at[4 * x + 2 * y + c], local_sems.at[a])

        def own_sends(a):
            return [copy(a, 0, me, sibling, own=True)] + [copy(a, 1 + j, me, chip, own=True)
                                                          for j, chip in enumerate(chips)]

        return copy, local, own_sends, me, sibling, chips

    def start(in_refs, out_refs, sems):
        _, local, own_sends, _, _, _ = plan(in_refs, out_refs, sems)
        for a in range(n):
            local(a).start()
            for cp in own_sends(a):
                cp.start()

    def finish(in_refs, out_refs, sems):
        copy, local, own_sends, me, sibling, chips = plan(in_refs, out_refs, sems)
        for j, chip in enumerate(chips):
            for a in range(n):
                copy(a, 1 + j, chip, me).wait_recv()
                copy(a, 4 + j, chip, sibling).start()
        for a in range(n):
            copy(a, 0, sibling, me).wait_recv()
            for j, (px, py, pc) in enumerate(chips):
                copy(a, 4 + j, (px, py, 1 - pc), me).wait_recv()
            for cp in own_sends(a) + [copy(a, 4 + j, chip, sibling) for j, chip in enumerate(chips)]:
                cp.wait_send()
            local(a).wait()

    return _Rider(blks, [jax.ShapeDtypeStruct((NDEV,) + b.shape, b.dtype) for b in blks],
                  [pltpu.SemaphoreType.DMA((7, n)), pltpu.SemaphoreType.DMA((7, n)), pltpu.SemaphoreType.DMA((n,))],
                  start, finish)


def _sibling_rider(g4s):
    n = len(g4s)

    def copies(in_refs, out_refs, sems):
        send_sems, recv_sems = sems
        x, y, c = _place()
        return [pltpu.make_async_remote_copy(
            src_ref=in_refs[a].at[k, 1 - c], dst_ref=out_refs[a].at[k], send_sem=send_sems.at[k, a],
            recv_sem=recv_sems.at[k, a], device_id=(x, y, 1 - c), device_id_type=MESH)
            for a in range(n) for k in range(4)]

    def start(in_refs, out_refs, sems):
        for cp in copies(in_refs, out_refs, sems):
            cp.start()

    def finish(in_refs, out_refs, sems):
        for cp in copies(in_refs, out_refs, sems):
            cp.wait()

    return _Rider(g4s, [jax.ShapeDtypeStruct((4,) + g.shape[2:], g.dtype) for g in g4s],
                  [pltpu.SemaphoreType.DMA((4, n)), pltpu.SemaphoreType.DMA((4, n))], start, finish)


def _chips_rider(pbs):
    n = len(pbs)

    def copies(in_refs, out_refs, sems):
        send_sems, recv_sems = sems
        x, y, c = _place()
        peers = [(1 - x, y), (x, 1 - y), (1 - x, 1 - y)]
        return [pltpu.make_async_remote_copy(
            src_ref=in_refs[a].at[2 * px + py], dst_ref=out_refs[a].at[k], send_sem=send_sems.at[k, a],
            recv_sem=recv_sems.at[k, a], device_id=(px, py, c), device_id_type=MESH)
            for a in range(n) for k, (px, py) in enumerate(peers)]

    def start(in_refs, out_refs, sems):
        for cp in copies(in_refs, out_refs, sems):
            cp.start()

    def finish(in_refs, out_refs, sems):
        for cp in copies(in_refs, out_refs, sems):
            cp.wait()

    return _Rider(pbs, [jax.ShapeDtypeStruct((3,) + p.shape[1:], p.dtype) for p in pbs],
                  [pltpu.SemaphoreType.DMA((3, n)), pltpu.SemaphoreType.DMA((3, n))], start, finish)


def _exchange(rider, name):
    n_in, n_out = len(rider.arrays), len(rider.out_shape)

    def body(*refs):
        ins, outs, sems = refs[:n_in], refs[n_in:n_in + n_out], refs[n_in + n_out:]
        rider.start(ins, outs, sems)
        rider.finish(ins, outs, sems)

    hbm = pl.BlockSpec(memory_space=pltpu.HBM)
    return pl.pallas_call(body, name=name, out_shape=rider.out_shape, in_specs=[hbm] * n_in,
                          out_specs=[hbm] * n_out, scratch_shapes=rider.scratch)(*rider.arrays)


def _rs_pair_sum(place, g4, r1, name):
    rows, cols = g4.shape[2:]
    tr = min(rows, 512)

    def body(pl_ref, g_ref, r_ref, pb_ref, own_ref):
        s = g_ref[0, 0] + r_ref[0]
        pb_ref[0] = s.astype(BF16)

        @pl.when(pl.program_id(1) == pl_ref[0])
        def _():
            own_ref[...] = s

    grid_spec = pltpu.PrefetchScalarGridSpec(
        num_scalar_prefetch=1, grid=(rows // tr, 4),
        in_specs=[pl.BlockSpec((1, 1, tr, cols), lambda r, k, p: (k, p[1], r, 0)),
                  pl.BlockSpec((1, tr, cols), lambda r, k, p: (k, r, 0))],
        out_specs=[pl.BlockSpec((1, tr, cols), lambda r, k, p: (k, r, 0)),
                   pl.BlockSpec((tr, cols), lambda r, k, p: (r, 0))])
    return pl.pallas_call(
        body, name=name, grid_spec=grid_spec,
        out_shape=[jax.ShapeDtypeStruct((4, rows, cols), BF16), jax.ShapeDtypeStruct((rows, cols), F32)],
        compiler_params=_cparams(("arbitrary", "arbitrary")))(place, g4, r1)


def _small_all_reduce(vec):
    m_per = vec.shape[0]

    def body(x_ref, all_ref, sum_ref, send_sems, recv_sems, local_sem):
        x, y, c = _place()
        me, sibling = (x, y, c), (x, y, 1 - c)
        chips = [(1 - x, y), (x, 1 - y), (1 - x, 1 - y)]

        def rows(px, py, pc):
            return all_ref.at[4 * px + 2 * py + pc]

        def copy(k, block, to, src=None):
            return pltpu.make_async_remote_copy(
                src_ref=rows(*block) if src is None else src, dst_ref=rows(*block),
                send_sem=send_sems.at[k], recv_sem=recv_sems.at[k], device_id=to, device_id_type=MESH)

        mine = pltpu.make_async_copy(x_ref, rows(*me), local_sem)
        mine.start()
        first = [copy(0, me, sibling, src=x_ref)]
        first += [copy(1 + j, me, (*chip, c), src=x_ref) for j, chip in enumerate(chips)]
        for cp in first:
            cp.start()
        passed = [copy(4 + j, (*chip, c), sibling) for j, chip in enumerate(chips)]
        for j, chip in enumerate(chips):
            copy(1 + j, (*chip, c), me).wait_recv()
            passed[j].start()
        copy(0, sibling, me).wait_recv()
        for j, chip in enumerate(chips):
            copy(4 + j, (*chip, 1 - c), me).wait_recv()
        for cp in first + passed:
            cp.wait_send()
        mine.wait()
        total = all_ref[0]
        for d in range(1, NDEV):
            total = total + all_ref[d]
        sum_ref[...] = total

    return pl.pallas_call(
        body, name="small_all_reduce",
        out_shape=[jax.ShapeDtypeStruct((NDEV, m_per, 128), F32), jax.ShapeDtypeStruct((m_per, 128), F32)],
        in_specs=[pl.BlockSpec(memory_space=pltpu.VMEM)],
        out_specs=[pl.BlockSpec(memory_space=pltpu.VMEM), pl.BlockSpec(memory_space=pltpu.VMEM)],
        scratch_shapes=[pltpu.SemaphoreType.DMA((7,)), pltpu.SemaphoreType.DMA((7,)), pltpu.SemaphoreType.DMA],
    )(vec)[1]


def _adamw_math(w, g, m, v):
    m = ADAM_B1 * m + (1.0 - ADAM_B1) * g
    v = ADAM_B2 * v + (1.0 - ADAM_B2) * (g * g)
    m_hat = m / (1.0 - ADAM_B1 ** ADAM_STEP)
    v_hat = v / (1.0 - ADAM_B2 ** ADAM_STEP)
    delta = -ADAM_LR * (m_hat / (jnp.sqrt(v_hat) + ADAM_EPS) + ADAM_WD * w)
    return delta, m, v


def _adamw(w, m, v, own, r2, name):
    _, r, c = w.shape
    tr = 256 if r % 256 == 0 else r

    def body(w_ref, m_ref, v_ref, o_ref, r_ref, g_ref, d_ref, nm_ref, nv_ref):
        g = ((o_ref[...] + r_ref[0].astype(F32)) + r_ref[1].astype(F32)) + r_ref[2].astype(F32)
        g_ref[...] = g
        d_ref[...], nm_ref[...], nv_ref[...] = _adamw_math(w_ref[...], g, m_ref[...], v_ref[...])

    spec = pl.BlockSpec((None, tr, c), lambda i: (0, i, 0))
    return pl.pallas_call(
        body, name=name, grid=(r // tr,),
        in_specs=[spec, spec, spec, pl.BlockSpec((tr, c), lambda i: (i, 0)),
                  pl.BlockSpec((3, tr, c), lambda i: (0, i, 0))],
        out_specs=[spec] * 4, out_shape=[jax.ShapeDtypeStruct((1, r, c), F32)] * 4,
        compiler_params=_cparams(("arbitrary",)))(w, m, v, own, r2)


def _adamw_small(ws, gs, ms, vs):
    n = len(ws)

    def body(*refs):
        ins, outs = refs[:4 * n], refs[4 * n:]
        for i in range(n):
            d, m, v = _adamw_math(ins[i][...], ins[n + i][...], ins[2 * n + i][...], ins[3 * n + i][...])
            outs[i][...], outs[n + i][...], outs[2 * n + i][...] = d, m, v

    shapes = [jax.ShapeDtypeStruct(w.shape, F32) for w in ws]
    outs = pl.pallas_call(body, name="adamw_small", out_shape=shapes * 3)(*ws, *gs, *ms, *vs)
    return outs[:n], outs[n:2 * n], outs[2 * n:]


MATS = ("w_in", "w_out", "w_xq", "w_xo", "w_xkv", "w_up", "w_down")
SMALL = ("mix_norm", "conv_w", "conv_norm", "w_af", "b_af", "w_ab", "b_ab", "gla_norm", "xa_norm", "mem_norm",
         "mlp_norm", "final_norm")
WEIGHTS = ("mix_norm", "w_in", "conv_w", "conv_norm", "w_af", "b_af", "w_ab", "b_ab", "gla_norm", "w_out", "xa_norm",
           "mem_norm", "w_xq", "w_xkv", "w_xo", "mlp_norm", "w_up", "w_down", "final_norm")
SMALL_SHARDED = {"conv_w": (3, 64), "w_af": (LR, 32), "w_ab": (LR, 32)}
SMALL_PACK_ROWS = 16


def kernel(x, mem, mix_norm, w_in, conv_w, conv_norm, w_af, b_af, w_ab, b_ab, gla_norm, w_out, xa_norm, mem_norm, w_xq, w_xkv, w_xo, mlp_norm, w_up, w_down, final_norm, loss_target, m_mix_norm, m_w_in, m_conv_w, m_conv_norm, m_w_af, m_b_af, m_w_ab, m_b_ab, m_gla_norm, m_w_out, m_xa_norm, m_mem_norm, m_w_xq, m_w_xkv, m_w_xo, m_mlp_norm, m_w_up, m_w_down, m_final_norm, v_mix_norm, v_w_in, v_conv_w, v_conv_norm, v_w_af, v_b_af, v_w_ab, v_b_ab, v_gla_norm, v_w_out, v_xa_norm, v_mem_norm, v_w_xq, v_w_xkv, v_w_xo, v_mlp_norm, v_w_up, v_w_down, v_final_norm):
    w = dict(mix_norm=mix_norm, w_in=w_in, conv_w=conv_w, conv_norm=conv_norm, w_af=w_af, b_af=b_af, w_ab=w_ab,
             b_ab=b_ab, gla_norm=gla_norm, w_out=w_out, xa_norm=xa_norm, mem_norm=mem_norm, w_xq=w_xq, w_xkv=w_xkv,
             w_xo=w_xo, mlp_norm=mlp_norm, w_up=w_up, w_down=w_down, final_norm=final_norm)
    mom = dict(mix_norm=m_mix_norm, w_in=m_w_in, conv_w=m_conv_w, conv_norm=m_conv_norm, w_af=m_w_af, b_af=m_b_af,
               w_ab=m_w_ab, b_ab=m_b_ab, gla_norm=m_gla_norm, w_out=m_w_out, xa_norm=m_xa_norm, mem_norm=m_mem_norm,
               w_xq=m_w_xq, w_xkv=m_w_xkv, w_xo=m_w_xo, mlp_norm=m_mlp_norm, w_up=m_w_up, w_down=m_w_down,
               final_norm=m_final_norm)
    var = dict(mix_norm=v_mix_norm, w_in=v_w_in, conv_w=v_conv_w, conv_norm=v_conv_norm, w_af=v_w_af, b_af=v_b_af,
               w_ab=v_w_ab, b_ab=v_b_ab, gla_norm=v_gla_norm, w_out=v_w_out, xa_norm=v_xa_norm, mem_norm=v_mem_norm,
               w_xq=v_w_xq, w_xkv=v_w_xkv, w_xo=v_w_xo, mlp_norm=v_mlp_norm, w_up=v_w_up, w_down=v_w_down,
               final_norm=v_final_norm)
    xi, yi, ci = _place()
    me = 4 * xi + 2 * yi + ci
    two_d = lambda a: a.reshape(a.shape[-2:]) if a.ndim == 3 else a.reshape(1, a.shape[-1])

    small = jnp.concatenate([w[n].reshape(-1) for n in SMALL_SHARDED])
    small = jnp.pad(small, (0, SMALL_PACK_ROWS * 128 - small.shape[0])).reshape(SMALL_PACK_ROWS, 128)
    shard = {n: two_d(w[n]).astype(BF16) for n in MATS}
    for n in ("w_in", "w_up"):
        shard[n] = shard[n].T
    vec = {n: two_d(w[n]) for n in SMALL if n not in SMALL_SHARDED}
    place = jnp.stack([2 * xi + yi, ci]).astype(jnp.int32)
    loss_part, grad_x, grads, own, from_chips = _step(x[0], mem[0], loss_target[0], shard, small, vec, place)

    order = [n for n in SMALL if n not in SMALL_SHARDED] + list(SMALL_SHARDED)
    flat = jnp.concatenate([grads[n].reshape(-1) for n in order] + [loss_part.reshape(-1)])
    n_flat = flat.shape[0]
    tot = _small_all_reduce(jnp.pad(flat, (0, SMALL_ROWS * 128 - n_flat)).reshape(SMALL_ROWS, 128)).reshape(-1)
    gsmall, off = {}, 0
    for n in order:
        size = grads[n].size
        full = tot[off:off + size].reshape(grads[n].shape)
        off += size
        if n in SMALL_SHARDED:
            r, c = SMALL_SHARDED[n]
            full = lax.dynamic_slice_in_dim(full, me * c, c, axis=1)
        gsmall[n] = full
    loss = tot[off]

    out_g, out_d, out_m, out_v = {}, {}, {}, {}
    own["w_up"], from_chips["w_up"] = own["w_up"].T, from_chips["w_up"].transpose(0, 2, 1)
    for n in MATS:
        if n == "w_in":
            res = _adamw(*[a.transpose(0, 2, 1) for a in (w[n], mom[n], var[n])], own[n], from_chips[n], "adamw_" + n)
            res = [a.transpose(0, 2, 1) for a in res]
        else:
            res = _adamw(w[n], mom[n], var[n], own[n], from_chips[n], "adamw_" + n)
        out_g[n], out_d[n], out_m[n], out_v[n] = res
    ds, nms, nvs = _adamw_small([two_d(w[n]) for n in SMALL], [gsmall[n] for n in SMALL],
                                [two_d(mom[n]) for n in SMALL], [two_d(var[n]) for n in SMALL])
    for i, n in enumerate(SMALL):
        out_g[n], out_d[n], out_m[n], out_v[n] = [a.reshape(w[n].shape) for a in (gsmall[n], ds[i], nms[i], nvs[i])]

    return (loss, grad_x[None], *[out_g[n] for n in WEIGHTS], *[out_d[n] for n in WEIGHTS],
            *[out_m[n] for n in WEIGHTS], *[out_v[n] for n in WEIGHTS])
```

```python
import functools

import jax
import jax.numpy as jnp
from jax import lax
from jax.experimental import pallas as pl
from jax.experimental.pallas import tpu as pltpu

F32 = jnp.float32
BF16 = jnp.bfloat16

D = 1024
CW = 512
GK = 256
GV = 512
NH = 4
CH = 64
LR = 16
NMEM = 256
XD = 256
FF = 4096
ZW = 3104
ZC = 3200
EPS = 1e-6
NDEV = 8

ZB_CB, ZB_CC, ZB_CU, ZB_V, ZB_G = 0, 1, 2, 4, 5
ZB_Q, ZB_K = 6, 7
ZB_LR = 24

TM = 512
TM_MLP = 256
TF = 512
TB = 256
TB_BWD = 512
TT = 2048
VMEM_LIMIT = 56 * 1024 * 1024

ADAM_LR, ADAM_B1, ADAM_B2, ADAM_EPS, ADAM_WD, ADAM_STEP = 0.001, 0.9, 0.999, 1e-08, 0.01, 10

XKV_SHARD = 2 * D // NDEV
SMALL_ROWS = 128

MESH = pl.DeviceIdType.MESH


def _cparams(sem):
    return pltpu.CompilerParams(dimension_semantics=sem, vmem_limit_bytes=VMEM_LIMIT)


def _call(body, name, grid, in_specs, out_specs, out_shape, scratch, args, riders=()):
    n_in, n_out, n_scr = len(in_specs), len(out_specs), len(scratch)
    counts = [(len(r.arrays), len(r.out_shape), len(r.scratch)) for r in riders]

    def take(refs, pos, sizes):
        groups = []
        for size in sizes:
            groups.append(refs[pos:pos + size])
            pos += size
        return groups, pos

    def wrapped(*refs):
        ins, pos = refs[:n_in], n_in
        r_ins, pos = take(refs, pos, [c[0] for c in counts])
        outs, pos = refs[pos:pos + n_out], pos + n_out
        r_outs, pos = take(refs, pos, [c[1] for c in counts])
        scr, pos = refs[pos:pos + n_scr], pos + n_scr
        r_scr, pos = take(refs, pos, [c[2] for c in counts])
        ids = [pl.program_id(d) for d in range(len(grid))]
        first = functools.reduce(lambda a, b: a & b, [i == 0 for i in ids])
        last = functools.reduce(lambda a, b: a & b, [i == g - 1 for i, g in zip(ids, grid)])

        @pl.when(first)
        def _():
            for r, a, b, c in zip(riders, r_ins, r_outs, r_scr):
                r.start(a, b, c)

        body(*ins, *outs, *scr)

        @pl.when(last)
        def _():
            for r, a, b, c in zip(riders, r_ins, r_outs, r_scr):
                r.finish(a, b, c)

    hbm = pl.BlockSpec(memory_space=pltpu.HBM)
    r_args = [a for r in riders for a in r.arrays]
    r_shapes = [s for r in riders for s in r.out_shape]
    return pl.pallas_call(
        wrapped if riders else body, name=name, grid=grid, in_specs=list(in_specs) + [hbm] * len(r_args),
        out_specs=list(out_specs) + [hbm] * len(r_shapes), out_shape=list(out_shape) + r_shapes,
        scratch_shapes=list(scratch) + [s for r in riders for s in r.scratch],
        compiler_params=_cparams(("arbitrary",) * len(grid)))(*args, *r_args)


def _dot(a, b):
    return jnp.dot(a.astype(BF16), b.astype(BF16), preferred_element_type=F32)


def _dot_nt(a, b):
    return lax.dot_general(a.astype(BF16), b.astype(BF16), (((1,), (1,)), ((), ())), preferred_element_type=F32)


def _dot_tn(a, b):
    return lax.dot_general(a.astype(BF16), b.astype(BF16), (((0,), (0,)), ((), ())), preferred_element_type=F32)


def _split(x, n):
    parts = []
    for _ in range(n):
        p = x.astype(BF16)
        parts.append(p)
        x = x - p.astype(F32)
    return parts


def _dot_exact_lhs(m, x, n):
    return functools.reduce(lambda a, b: a + b, [jnp.dot(m, p, preferred_element_type=F32) for p in _split(x, n)])


def _dot_exact_rhs(x, m, n):
    return functools.reduce(lambda a, b: a + b, [jnp.dot(p, m, preferred_element_type=F32) for p in _split(x, n)])


def _rms(x, g):
    r = lax.rsqrt(jnp.mean(x * x, axis=-1, keepdims=True) + EPS)
    return x * r * g, r


def _rms_bwd(x, r, g, dy):
    xr = x * r
    u = dy * g
    dx = r * (u - xr * jnp.mean(u * xr, axis=-1, keepdims=True))
    return dx, jnp.sum(dy * xr, axis=0, keepdims=True)


def _iota(shape, dim):
    return lax.broadcasted_iota(jnp.int32, shape, dim)


def _sigmoid(x):
    return 1.0 / (1.0 + jnp.exp(-x))


def _acc_rows(ref, row):
    ref[...] += jnp.broadcast_to(row, ref.shape)


def _inproj(x, g, w_t, riders=()):
    t = x.shape[0]
    tm = min(TM, t)

    def body(x_ref, g_ref, w_ref, z_ref, h_ref):
        h, _ = _rms(x_ref[...], g_ref[...])
        hb = h.astype(BF16)
        h_ref[...] = hb
        z_ref[...] = _dot_nt(hb, w_ref[...])

    return _call(
        body, "inproj", (t // tm,),
        [pl.BlockSpec((tm, D), lambda i: (i, 0)), pl.BlockSpec((1, D), lambda i: (0, 0)),
         pl.BlockSpec((ZC, D), lambda i: (0, 0))],
        [pl.BlockSpec((tm, ZC), lambda i: (i, 0)), pl.BlockSpec((tm, D), lambda i: (i, 0))],
        [jax.ShapeDtypeStruct((t, ZC), F32), jax.ShapeDtypeStruct((t, D), BF16)], [], (x, g, w_t), riders)


def _kv_proj(mem, g, w):
    def body(m_ref, g_ref, w_ref, kv_ref, mn_ref):
        mn, _ = _rms(m_ref[...], g_ref[...])
        mb = mn.astype(BF16)
        mn_ref[...] = mb
        for j in range(NDEV):
            kv_ref[:, j * XKV_SHARD:(j + 1) * XKV_SHARD] = jnp.dot(mb, w_ref[j], preferred_element_type=F32)

    return pl.pallas_call(
        body, name="kv_proj",
        out_shape=[jax.ShapeDtypeStruct((NMEM, 2 * D), F32), jax.ShapeDtypeStruct((NMEM, D), BF16)],
        compiler_params=pltpu.CompilerParams(vmem_limit_bytes=VMEM_LIMIT))(mem, g, w)


def _softmax_head(qb, kb):
    s = _dot_nt(qb, kb) * (1.0 / 16.0)
    e = jnp.exp(s - jnp.max(s, axis=-1, keepdims=True))
    return e / jnp.sum(e, axis=-1, keepdims=True)


def _attn_fwd(x, yb, w_out, g, w_xq, kb, vb, w_xo):
    t = x.shape[0]
    tm = min(TM, t)

    def body(x_ref, y_ref, wo_ref, g_ref, wq_ref, k_ref, v_ref, wx_ref, x1_ref, x2_ref, xn_ref, q_ref, a_ref):
        x1 = x_ref[...] + jnp.dot(y_ref[...], wo_ref[...], preferred_element_type=F32)
        x1_ref[...] = x1
        xn, _ = _rms(x1, g_ref[...])
        xb = xn.astype(BF16)
        xn_ref[...] = xb
        qb = jnp.dot(xb, wq_ref[...], preferred_element_type=F32).astype(BF16)
        q_ref[...] = qb
        for h in range(NH):
            hs = slice(h * XD, (h + 1) * XD)
            p = _softmax_head(qb[:, hs], k_ref[:, hs])
            a_ref[:, hs] = _dot(p, v_ref[:, hs]).astype(BF16)
        x2_ref[...] = x1 + jnp.dot(a_ref[...], wx_ref[...], preferred_element_type=F32)

    tok = lambda i: (i, 0)
    full = lambda i: (0, 0)
    return pl.pallas_call(
        body, name="attn_fwd", grid=(t // tm,),
        in_specs=[pl.BlockSpec((tm, D), tok), pl.BlockSpec((tm, D), tok), pl.BlockSpec((D, D), full),
                  pl.BlockSpec((1, D), full), pl.BlockSpec((D, D), full), pl.BlockSpec((NMEM, D), full),
                  pl.BlockSpec((NMEM, D), full), pl.BlockSpec((D, D), full)],
        out_specs=[pl.BlockSpec((tm, D), tok)] * 5,
        out_shape=[jax.ShapeDtypeStruct((t, D), F32), jax.ShapeDtypeStruct((t, D), F32),
                   jax.ShapeDtypeStruct((t, D), BF16), jax.ShapeDtypeStruct((t, D), BF16),
                   jax.ShapeDtypeStruct((t, D), BF16)],
        compiler_params=_cparams(("arbitrary",)))(x, yb, w_out, g, w_xq, kb, vb, w_xo)


def _mlp_fwd(x2, g, w_up_t, w_down, fg, target):
    t = x2.shape[0]
    tm = min(TM_MLP, t)

    def body(x_ref, g_ref, wu_ref, wd_ref, fg_ref, t_ref, h1_ref, xn_ref, dx_ref, dxb_ref, loss_ref, dfg_ref, ab):
        @pl.when(pl.program_id(0) == 0)
        def _():
            loss_ref[...] = jnp.zeros_like(loss_ref)
            dfg_ref[...] = jnp.zeros_like(dfg_ref)

        x = x_ref[...]
        xn, _ = _rms(x, g_ref[...])
        xnb = xn.astype(BF16)
        xn_ref[...] = xnb
        for q in range(FF // TF):
            cols = slice(q * TF, (q + 1) * TF)
            h1 = _dot_nt(xnb, wu_ref[cols, :])
            h1_ref[:, cols] = h1.astype(BF16)
            hr = jnp.maximum(h1, 0.0)
            ab[:, cols] = (hr * hr).astype(BF16)
        x3 = x + jnp.dot(ab[...], wd_ref[...], preferred_element_type=F32)
        y, r = _rms(x3, fg_ref[...])
        e = y - t_ref[...]
        row = jnp.mean(e * e, axis=-1, keepdims=True)
        _acc_rows(loss_ref, 0.5 * jnp.sum(row, axis=0, keepdims=True))
        dx, dfg = _rms_bwd(x3, r, fg_ref[...], e * (1.0 / D))
        dx_ref[...] = dx
        dxb_ref[...] = dx.astype(BF16)
        _acc_rows(dfg_ref, dfg)

    tok = lambda i: (i, 0)
    full = lambda i: (0, 0)
    once = pl.Buffered(1)
    return pl.pallas_call(
        body, name="mlp_fwd", grid=(t // tm,),
        in_specs=[pl.BlockSpec((tm, D), tok), pl.BlockSpec((1, D), full),
                  pl.BlockSpec((FF, D), full, pipeline_mode=once), pl.BlockSpec((FF, D), full, pipeline_mode=once),
                  pl.BlockSpec((1, D), full), pl.BlockSpec((tm, D), tok)],
        out_specs=[pl.BlockSpec((tm, FF), tok), pl.BlockSpec((tm, D), tok), pl.BlockSpec((tm, D), tok),
                   pl.BlockSpec((tm, D), tok), pl.BlockSpec((8, 128), full), pl.BlockSpec((8, D), full)],
        out_shape=[jax.ShapeDtypeStruct((t, FF), BF16), jax.ShapeDtypeStruct((t, D), BF16),
                   jax.ShapeDtypeStruct((t, D), F32), jax.ShapeDtypeStruct((t, D), BF16),
                   jax.ShapeDtypeStruct((8, 128), F32), jax.ShapeDtypeStruct((8, D), F32)],
        scratch_shapes=[pltpu.VMEM((tm, FF), BF16)],
        compiler_params=_cparams(("arbitrary",)))(x2, g, w_up_t, w_down, fg, target)


def _mlp_bwd(dx3, dx3b, h1b, w_down, w_up_t, x2, g):
    t = x2.shape[0]
    tm = min(TM_MLP, t)

    def body(dx_ref, dxb_ref, h1_ref, wd_ref, wu_ref, x_ref, g_ref, a_ref, dh_ref, dx2_ref, dx2b_ref, dg_ref):
        @pl.when(pl.program_id(0) == 0)
        def _():
            dg_ref[...] = jnp.zeros_like(dg_ref)

        for q in range(FF // TF):
            cols = slice(q * TF, (q + 1) * TF)
            hr = jnp.maximum(h1_ref[:, cols].astype(F32), 0.0)
            da = _dot_nt(dxb_ref[...], wd_ref[cols, :])
            a_ref[:, cols] = (hr * hr).astype(BF16)
            dh_ref[:, cols] = (da * 2.0 * hr).astype(BF16)
        dxn = jnp.dot(dh_ref[...], wu_ref[...], preferred_element_type=F32)
        x = x_ref[...]
        r = lax.rsqrt(jnp.mean(x * x, axis=-1, keepdims=True) + EPS)
        dx, dg = _rms_bwd(x, r, g_ref[...], dxn)
        dx2 = dx_ref[...] + dx
        dx2_ref[...] = dx2
        dx2b_ref[...] = dx2.astype(BF16)
        _acc_rows(dg_ref, dg)

    tok = lambda i: (i, 0)
    full = lambda i: (0, 0)
    once = pl.Buffered(1)
    return pl.pallas_call(
        body, name="mlp_bwd", grid=(t // tm,),
        in_specs=[pl.BlockSpec((tm, D), tok), pl.BlockSpec((tm, D), tok), pl.BlockSpec((tm, FF), tok),
                  pl.BlockSpec((FF, D), full, pipeline_mode=once), pl.BlockSpec((FF, D), full, pipeline_mode=once),
                  pl.BlockSpec((tm, D), tok), pl.BlockSpec((1, D), full)],
        out_specs=[pl.BlockSpec((tm, FF), tok), pl.BlockSpec((tm, FF), tok), pl.BlockSpec((tm, D), tok),
                   pl.BlockSpec((tm, D), tok), pl.BlockSpec((8, D), full)],
        out_shape=[jax.ShapeDtypeStruct((t, FF), BF16), jax.ShapeDtypeStruct((t, FF), BF16),
                   jax.ShapeDtypeStruct((t, D), F32), jax.ShapeDtypeStruct((t, D), BF16),
                   jax.ShapeDtypeStruct((8, D), F32)],
        compiler_params=_cparams(("arbitrary",)))(dx3, dx3b, h1b, w_down, w_up_t, x2, g)


def _attn_bwd(x1, dx2, dx2b, qb, kb, vb, w_xo, w_xq, w_out, g):
    t = x1.shape[0]
    tm = min(TM, t)

    def body(x_ref, dx2_ref, dx2b_ref, q_ref, k_ref, v_ref, wx_ref, wq_ref, wo_ref, g_ref,
             dx1_ref, dx1b_ref, dy_ref, dq_ref, dkv_ref, dg_ref):
        @pl.when(pl.program_id(0) == 0)
        def _():
            dkv_ref[...] = jnp.zeros_like(dkv_ref)
            dg_ref[...] = jnp.zeros_like(dg_ref)

        datt = _dot_nt(dx2b_ref[...], wx_ref[...]).astype(BF16)
        for h in range(NH):
            hs = slice(h * XD, (h + 1) * XD)
            q_h, k_h, v_h, da_h = q_ref[:, hs], k_ref[:, hs], v_ref[:, hs], datt[:, hs]
            p = _softmax_head(q_h, k_h)
            dp = _dot_nt(da_h, v_h)
            ds = (p * (dp - jnp.sum(dp * p, axis=-1, keepdims=True)) * (1.0 / 16.0)).astype(BF16)
            dq_ref[:, hs] = _dot(ds, k_h).astype(BF16)
            dkv_ref[:, hs] += _dot_tn(ds, q_h)
            dkv_ref[:, D + h * XD:D + (h + 1) * XD] += _dot_tn(p, da_h)
        dxn = _dot_nt(dq_ref[...], wq_ref[...])
        x = x_ref[...]
        r = lax.rsqrt(jnp.mean(x * x, axis=-1, keepdims=True) + EPS)
        dx, dg = _rms_bwd(x, r, g_ref[...], dxn)
        dx1 = dx2_ref[...] + dx
        dx1_ref[...] = dx1
        dx1b = dx1.astype(BF16)
        dx1b_ref[...] = dx1b
        dy_ref[...] = _dot_nt(dx1b, wo_ref[...])
        _acc_rows(dg_ref, dg)

    tok = lambda i: (i, 0)
    full = lambda i: (0, 0)
    return pl.pallas_call(
        body, name="attn_bwd", grid=(t // tm,),
        in_specs=[pl.BlockSpec((tm, D), tok), pl.BlockSpec((tm, D), tok), pl.BlockSpec((tm, D), tok),
                  pl.BlockSpec((tm, D), tok), pl.BlockSpec((NMEM, D), full), pl.BlockSpec((NMEM, D), full),
                  pl.BlockSpec((D, D), full), pl.BlockSpec((D, D), full), pl.BlockSpec((D, D), full),
                  pl.BlockSpec((1, D), full)],
        out_specs=[pl.BlockSpec((tm, D), tok), pl.BlockSpec((tm, D), tok), pl.BlockSpec((tm, D), tok),
                   pl.BlockSpec((tm, D), tok), pl.BlockSpec((NMEM, 2 * D), full), pl.BlockSpec((8, D), full)],
        out_shape=[jax.ShapeDtypeStruct((t, D), F32), jax.ShapeDtypeStruct((t, D), BF16),
                   jax.ShapeDtypeStruct((t, D), F32), jax.ShapeDtypeStruct((t, D), BF16),
                   jax.ShapeDtypeStruct((NMEM, 2 * D), F32), jax.ShapeDtypeStruct((8, D), F32)],
        compiler_params=_cparams(("arbitrary",)))(x1, dx2, dx2b, qb, kb, vb, w_xo, w_xq, w_out, g)


def _kv_bwd(dkv, memn, mem, g, w):
    def body(dkv_ref, mn_ref, m_ref, g_ref, w_ref, dw_ref, dg_ref):
        dkvb = dkv_ref[...].astype(BF16)
        dmn = jnp.zeros((NMEM, D), F32)
        for j in range(NDEV):
            cols = slice(j * XKV_SHARD, (j + 1) * XKV_SHARD)
            dw_ref[j] = _dot_tn(mn_ref[...], dkvb[:, cols])
            dmn += _dot_nt(dkvb[:, cols], w_ref[j])
        m = m_ref[...]
        r = lax.rsqrt(jnp.mean(m * m, axis=-1, keepdims=True) + EPS)
        dg_ref[...] = jnp.broadcast_to(jnp.sum(dmn * m * r, axis=0, keepdims=True), dg_ref.shape)

    return pl.pallas_call(
        body, name="kv_bwd",
        out_shape=[jax.ShapeDtypeStruct((NDEV, D, XKV_SHARD), F32), jax.ShapeDtypeStruct((8, D), F32)],
        compiler_params=pltpu.CompilerParams(vmem_limit_bytes=VMEM_LIMIT))(dkv, memn, mem, g, w)


def _inproj_bwd(dz, w_t, x, dx1, g, riders=()):
    t = x.shape[0]
    tm = min(TM, t)

    def body(dz_ref, w_ref, x_ref, dx1_ref, g_ref, gx_ref, dg_ref):
        @pl.when(pl.program_id(0) == 0)
        def _():
            dg_ref[...] = jnp.zeros_like(dg_ref)

        dh = jnp.dot(dz_ref[...], w_ref[...], preferred_element_type=F32)
        x = x_ref[...]
        r = lax.rsqrt(jnp.mean(x * x, axis=-1, keepdims=True) + EPS)
        dx, dg = _rms_bwd(x, r, g_ref[...], dh)
        gx_ref[...] = dx1_ref[...] + dx
        _acc_rows(dg_ref, dg)

    tok = lambda i: (i, 0)
    full = lambda i: (0, 0)
    return _call(
        body, "inproj_bwd", (t // tm,),
        [pl.BlockSpec((tm, ZC), tok), pl.BlockSpec((ZC, D), full), pl.BlockSpec((tm, D), tok),
         pl.BlockSpec((tm, D), tok), pl.BlockSpec((1, D), full)],
        [pl.BlockSpec((tm, D), tok), pl.BlockSpec((8, D), full)],
        [jax.ShapeDtypeStruct((t, D), F32), jax.ShapeDtypeStruct((8, D), F32)], [], (dz, w_t, x, dx1, g), riders)


def _matmul_tn(a, b, name, rows=None, riders=()):
    t, k = a.shape
    n = b.shape[1]
    tk, tn = [1024 if size % 1024 == 0 else 640 for size in (k, n)]
    tt = min(TT, t)
    rows = rows or k

    def body(a_ref, b_ref, o_ref):
        @pl.when(pl.program_id(2) == 0)
        def _():
            o_ref[...] = jnp.zeros_like(o_ref)

        o_ref[...] += _dot_tn(a_ref[...], b_ref[...])

    return _call(
        body, name, (k // tk, n // tn, t // tt),
        [pl.BlockSpec((tt, tk), lambda i, j, s: (s, i)), pl.BlockSpec((tt, tn), lambda i, j, s: (s, j))],
        [pl.BlockSpec((tk, tn), lambda i, j, s: (i, j))], [jax.ShapeDtypeStruct((rows, n), F32)], [], (a, b), riders)


def _lane_head(shape, dim, shift):
    return _iota(shape, dim) >> shift


def _gla_recompute(q_raw, k, lr, wpad, bias, rev, tb):
    pre = _dot(lr, wpad) + bias
    la = (jnp.minimum(pre, 0.0) - jnp.log(1.0 + jnp.exp(-jnp.abs(pre)))) * (1.0 / 16.0)
    r, c = _iota((tb, tb), 0), _iota((tb, tb), 1)
    tri = (c >= r) if rev else (c <= r)
    cum = jnp.where(((r >> 6) == (c >> 6)) & tri, 1.0, 0.0).astype(BF16)
    b = _dot_exact_lhs(cum, la, 3)
    e, ei = jnp.exp(b), jnp.exp(-b)
    qt = (q_raw * 0.125) * e
    kt = k * ei
    return pre, b, e, ei, qt, kt


def _stack_heads(x, shift):
    head = _lane_head(x.shape, 1, shift)
    return jnp.concatenate([jnp.where(head == h, x, 0.0) for h in range(NH)], axis=0).astype(BF16)


def _fold_heads(x, shift):
    head = _lane_head((CH, x.shape[1]), 1, shift)
    return functools.reduce(lambda a, b: a + b,
                            [jnp.where(head == h, x[h * CH:(h + 1) * CH], 0.0) for h in range(NH)])


def _wide_mask(rev):
    r, s = _iota((CH, NH * CH), 0), _iota((CH, NH * CH), 1) & (CH - 1)
    return (s >= r) if rev else (s <= r)


def _state_mask():
    return (_iota((GV, GK), 0) >> 7) == (_iota((GV, GK), 1) >> 6)


def _state_expand(sd):
    head = _lane_head(sd.shape, 1, 6)
    return jnp.concatenate([jnp.where(head == h, sd, 0.0) for h in range(NH)], axis=0)


def _conv_parts(cb, cc, cu, ccp, cup, ccn, cun, cw_ref, first, last, tb):
    h = cc * cu
    hp = jnp.where(first, 0.0, ccp * cup)
    hn = jnp.where(last, 0.0, ccn * cun)
    rows = _iota(h.shape, 0)
    h_m1 = jnp.where(rows == 0, hp, pltpu.roll(h, 1, 0))
    h_p1 = jnp.where(rows == tb - 1, hn, pltpu.roll(h, tb - 1, 0))
    conv = cw_ref[pl.ds(0, 1), :] * h_m1 + cw_ref[pl.ds(1, 1), :] * h + cw_ref[pl.ds(2, 1), :] * h_p1
    return h, h_m1, h_p1, conv


def _group_ones():
    return jnp.where((_iota((CW, CW), 0) >> 6) == (_iota((CW, CW), 1) >> 6), 1.0, 0.0).astype(BF16)


def _head_norm(o):
    ons, rs = [], []
    for h in range(NH):
        slab = o[:, h * 128:(h + 1) * 128]
        r = lax.rsqrt(jnp.mean(slab * slab, axis=-1, keepdims=True) + EPS)
        ons.append(slab * r)
        rs.append(jnp.broadcast_to(r, slab.shape))
    return jnp.concatenate(ons, axis=1), jnp.concatenate(rs, axis=1)


def _zspec(tb, width, blk, jmap):
    return pl.BlockSpec((tb, width), lambda i: (jmap(i), blk))


def _halo_specs(tb, nblk, t, blk, jmap):
    prev = pl.BlockSpec((8, CW), lambda i: (jnp.maximum(jmap(i) * (tb // 8) - 1, 0), blk))
    nxt = pl.BlockSpec((8, CW), lambda i: (jnp.minimum((jmap(i) + 1) * (tb // 8), t // 8 - 1), blk))
    return prev, nxt


def _gla_fwd_sweep(z, wpad, bias, rev, finish_args=None, riders=()):
    t = z.shape[0]
    tb = min(TB, t)
    nblk, nb = t // tb, tb // CH
    jmap = (lambda i: nblk - 1 - i) if rev else (lambda i: i)
    finish = finish_args is not None

    def body(*refs):
        if finish:
            (q_ref, k_ref, v_ref, lr_ref, w_ref, bias_ref, of_ref, g_ref, cb_ref, cc_ref, cu_ref, ccp_ref, ccn_ref,
             cup_ref, cun_ref, cw_ref, cn_ref, gn_ref, y_ref, opre_ref, sd_ref, st, b_scr, o_scr) = refs
        else:
            q_ref, k_ref, v_ref, lr_ref, w_ref, bias_ref, o_ref, sd_ref, st, b_scr = refs
            o_scr = o_ref
        i = pl.program_id(0)

        @pl.when(i == 0)
        def _():
            st[...] = jnp.zeros_like(st)

        q_raw, k, v = q_ref[...], k_ref[...], v_ref[...]
        _, b, _, _, qt, kt = _gla_recompute(q_raw, k, lr_ref[...], w_ref[...], bias_ref[...], rev, tb)
        b_scr[...] = b
        maskw, bd = _wide_mask(rev), _state_mask()
        order = list(reversed(range(nb))) if rev else list(range(nb))
        rows = [slice(c * CH, (c + 1) * CH) for c in range(nb)]
        gdec = {c: jnp.exp(b_scr[pl.ds(c * CH + (0 if rev else CH - 1), 1), :]) for c in order}
        upd = {c: jnp.where(bd, _dot_tn(v[rows[c]], kt[rows[c]] * gdec[c]), 0.0) for c in order}
        o_intra = {c: _dot(jnp.where(maskw, _dot_nt(qt[rows[c]], _stack_heads(kt[rows[c]], 6)), 0.0),
                           _stack_heads(v[rows[c]], 7)) for c in order}
        s_in, state = {}, st[...]
        for c in order:
            s_in[c] = state
            state = state * gdec[c] + upd[c]
        st[...] = state
        for c in order:
            sd_ref[c] = s_in[c][0:128] + s_in[c][128:256] + s_in[c][256:384] + s_in[c][384:512]
            o_scr[pl.ds(c * CH, CH), :] = o_intra[c] + _dot_nt(qt[rows[c]], s_in[c])

        if finish:
            j = jmap(i)
            hsel = jnp.where((_iota((GK, GV), 0) >> 6) == (_iota((GK, GV), 1) >> 7), 1.0, 0.0).astype(BF16)
            sb = _dot_exact_rhs((q_raw * 0.125) * k, hsel, 2)
            o_pre = of_ref[...] + o_scr[...] - sb * v
            opre_ref[...] = o_pre
            on, _ = _head_norm(o_pre)
            g = g_ref[...]
            y_ref[:, CW:] = (on * gn_ref[...] * (g * _sigmoid(g))).astype(BF16)
            cb = cb_ref[...]
            _, _, _, conv = _conv_parts(cb, cc_ref[...], cu_ref[...], ccp_ref[pl.ds(7, 1), :], cup_ref[pl.ds(7, 1), :],
                                        ccn_ref[pl.ds(0, 1), :], cun_ref[pl.ds(0, 1), :], cw_ref, j == 0,
                                        j == nblk - 1, tb)
            yc = cb * conv
            gm = _dot_exact_rhs(yc * yc, _group_ones(), 2) * (1.0 / 64.0)
            y_ref[:, :CW] = (yc * lax.rsqrt(gm + EPS) * cn_ref[...]).astype(BF16)

    full = lambda i: (0, 0)
    in_specs = [_zspec(tb, GK, ZB_Q, jmap), _zspec(tb, GK, ZB_K, jmap), _zspec(tb, GV, ZB_V, jmap),
                _zspec(tb, 128, ZB_LR, jmap), pl.BlockSpec((128, GK), full), pl.BlockSpec((1, GK), full)]
    args = [z, z, z, z, wpad, bias]
    sd_spec = pl.BlockSpec((nb, 128, GK), lambda i: (jmap(i), 0, 0))
    sd_shape = jax.ShapeDtypeStruct((t // CH, 128, GK), F32)
    scratch = [pltpu.VMEM((GV, GK), F32), pltpu.VMEM((tb, GK), F32)]
    if finish:
        o_f, conv_w, conv_norm, gla_norm4 = finish_args
        ccp, ccn = _halo_specs(tb, nblk, t, ZB_CC, jmap)
        cup, cun = _halo_specs(tb, nblk, t, ZB_CU, jmap)
        in_specs += [pl.BlockSpec((tb, GV), lambda i: (jmap(i), 0)), _zspec(tb, GV, ZB_G, jmap),
                     _zspec(tb, CW, ZB_CB, jmap), _zspec(tb, CW, ZB_CC, jmap), _zspec(tb, CW, ZB_CU, jmap),
                     ccp, ccn, cup, cun, pl.BlockSpec((3, CW), full), pl.BlockSpec((1, CW), full),
                     pl.BlockSpec((1, GV), full)]
        args += [o_f, z, z, z, z, z, z, z, z, conv_w, conv_norm, gla_norm4]
        out_specs = [pl.BlockSpec((tb, D), lambda i: (jmap(i), 0)), pl.BlockSpec((tb, GV), lambda i: (jmap(i), 0)),
                     sd_spec]
        out_shape = [jax.ShapeDtypeStruct((t, D), BF16), jax.ShapeDtypeStruct((t, GV), F32), sd_shape]
        scratch.append(pltpu.VMEM((tb, GV), F32))
    else:
        out_specs = [pl.BlockSpec((tb, GV), lambda i: (jmap(i), 0)), sd_spec]
        out_shape = [jax.ShapeDtypeStruct((t, GV), F32), sd_shape]
    return _call(body, "gla_fwd_rev" if rev else "gla_fwd", (nblk,), in_specs, out_specs, out_shape, scratch, args,
                 riders)


def _gla_bwd_chunks(do_ref, sd_ref, dst, b_scr, db_scr, dq_ref, dk_ref, dv_ref, qt, kt, e, ei, v, rev, nb):
    maskw, bd = _wide_mask(rev), _state_mask()
    for c in (range(nb) if rev else reversed(range(nb))):
        sl = slice(c * CH, (c + 1) * CH)
        grow = c * CH + (0 if rev else CH - 1)
        gdec = jnp.exp(b_scr[pl.ds(grow, 1), :])
        qt_c, kt_c, v_c, do_c = qt[sl], kt[sl], v[sl], do_ref[pl.ds(c * CH, CH), :]
        s_in = _state_expand(sd_ref[c])
        ds_out = dst[...]
        kbd, vbd = _stack_heads(kt_c, 6), _stack_heads(v_c, 7)
        a = jnp.where(maskw, _dot_nt(qt_c, kbd), 0.0)
        da = jnp.where(maskw, _dot_nt(do_c, vbd), 0.0)
        dv_ref[pl.ds(c * CH, CH), :] = _fold_heads(_dot_tn(a, do_c), 7) + _dot_nt(kt_c * gdec, ds_out)
        dqt = _dot(da, kbd) + _dot(do_c, s_in)
        dkh = _dot(v_c, ds_out)
        dkt = _fold_heads(_dot_tn(da, qt_c), 6) + dkh * gdec
        dg = jnp.sum(ds_out * s_in, axis=0, keepdims=True) + jnp.sum(kt_c * dkh, axis=0, keepdims=True)
        db_scr[pl.ds(c * CH, CH), :] = dqt * qt_c - dkt * kt_c
        db_scr[pl.ds(grow, 1), :] += dg * gdec
        dq_ref[pl.ds(c * CH, CH), :] = dqt * e[sl] * 0.125
        dk_ref[pl.ds(c * CH, CH), :] = dkt * ei[sl]
        dst[...] = ds_out * gdec + jnp.where(bd, _dot_tn(do_c, qt_c), 0.0)


def _gate_bwd(db, pre, lr, wpad, rev, tb):
    r, c = _iota((tb, tb), 0), _iota((tb, tb), 1)
    tri = (c <= r) if rev else (c >= r)
    cum_t = jnp.where(((r >> 6) == (c >> 6)) & tri, 1.0, 0.0).astype(BF16)
    dla = _dot_exact_lhs(cum_t, db, 2)
    dpre = dla * (1.0 / 16.0) / (1.0 + jnp.exp(pre))
    return dpre, _dot_nt(dpre, wpad), _dot_tn(lr, dpre)


def _gla_bwd_first(z, dy, o_pre, sd, wpad, bias, conv_w, conv_norm, gla_norm4, riders=()):
    t = z.shape[0]
    tb = min(TB_BWD, t)
    nblk, nb = t // tb, tb // CH
    jmap = lambda i: nblk - 1 - i

    def body(q_ref, k_ref, v_ref, lr_ref, g_ref, cb_ref, cc_ref, cu_ref, ccp_ref, ccn_ref, cup_ref, cun_ref,
             dy_ref, opre_ref, sd_ref, w_ref, bias_ref, cw_ref, cn_ref, gn_ref,
             do_ref, dq_ref, dk_ref, dv_ref, dlr_ref, dzg_ref, dzcb_ref, dconv_ref,
             dw_ref, dbias_ref, dcw_ref, dcn_ref, dgn_ref, dst, b_scr, db_scr):
        i = pl.program_id(0)
        j = jmap(i)

        @pl.when(i == 0)
        def _():
            dst[...] = jnp.zeros_like(dst)
            for ref in (dw_ref, dbias_ref, dcw_ref, dcn_ref, dgn_ref):
                ref[...] = jnp.zeros_like(ref)

        dyg = dy_ref[:, CW:]
        g = g_ref[...]
        sig = _sigmoid(g)
        on, rr = _head_norm(opre_ref[...])
        gn = gn_ref[...]
        dzg_ref[...] = (dyg * on * gn * (sig * (1.0 + g * (1.0 - sig)))).astype(BF16)
        don = dyg * (g * sig)
        _acc_rows(dgn_ref, jnp.sum(don * on, axis=0, keepdims=True))
        u = don * gn
        uo = u * on
        mean_uo = jnp.concatenate(
            [jnp.broadcast_to(jnp.mean(uo[:, h * 128:(h + 1) * 128], axis=-1, keepdims=True), (tb, 128))
             for h in range(NH)], axis=1)
        do_ref[...] = rr * (u - on * mean_uo)

        cb = cb_ref[...]
        h, h_m1, h_p1, conv = _conv_parts(cb, cc_ref[...], cu_ref[...], ccp_ref[pl.ds(7, 1), :],
                                          cup_ref[pl.ds(7, 1), :], ccn_ref[pl.ds(0, 1), :], cun_ref[pl.ds(0, 1), :],
                                          cw_ref, j == 0, j == nblk - 1, tb)
        yc = cb * conv
        ones = _group_ones()
        rc = lax.rsqrt(_dot_exact_rhs(yc * yc, ones, 2) * (1.0 / 64.0) + EPS)
        ycr = yc * rc
        dyn = dy_ref[:, :CW]
        _acc_rows(dcn_ref, jnp.sum(dyn * ycr, axis=0, keepdims=True))
        uc = dyn * cn_ref[...]
        dyc = rc * (uc - ycr * (_dot_exact_rhs(uc * ycr, ones, 2) * (1.0 / 64.0)))
        dzcb_ref[...] = (dyc * conv).astype(BF16)
        dconv = dyc * cb
        dconv_ref[...] = dconv
        dcw_ref[pl.ds(0, 1), :] += jnp.sum(dconv * h_m1, axis=0, keepdims=True)
        dcw_ref[pl.ds(1, 1), :] += jnp.sum(dconv * h, axis=0, keepdims=True)
        dcw_ref[pl.ds(2, 1), :] += jnp.sum(dconv * h_p1, axis=0, keepdims=True)

        lr, wp = lr_ref[...], w_ref[...]
        pre, b, e, ei, qt, kt = _gla_recompute(q_ref[...], k_ref[...], lr, wp, bias_ref[...], False, tb)
        b_scr[...] = b
        _gla_bwd_chunks(do_ref, sd_ref, dst, b_scr, db_scr, dq_ref, dk_ref, dv_ref, qt, kt, e, ei, v_ref[...],
                        False, nb)
        dpre, dlr, dw = _gate_bwd(db_scr[...], pre, lr, wp, False, tb)
        dlr_ref[...] = dlr
        dw_ref[...] += dw
        _acc_rows(dbias_ref, jnp.sum(dpre, axis=0, keepdims=True))

    full = lambda i: (0, 0)
    tokv = pl.BlockSpec((tb, GV), lambda i: (jmap(i), 0))
    tokk = pl.BlockSpec((tb, GK), lambda i: (jmap(i), 0))
    ccp, ccn = _halo_specs(tb, nblk, t, ZB_CC, jmap)
    cup, cun = _halo_specs(tb, nblk, t, ZB_CU, jmap)
    in_specs = [_zspec(tb, GK, ZB_Q, jmap), _zspec(tb, GK, ZB_K, jmap), _zspec(tb, GV, ZB_V, jmap),
                _zspec(tb, 128, ZB_LR, jmap), _zspec(tb, GV, ZB_G, jmap), _zspec(tb, CW, ZB_CB, jmap),
                _zspec(tb, CW, ZB_CC, jmap), _zspec(tb, CW, ZB_CU, jmap), ccp, ccn, cup, cun,
                pl.BlockSpec((tb, D), lambda i: (jmap(i), 0)), tokv,
                pl.BlockSpec((nb, 128, GK), lambda i: (jmap(i), 0, 0)), pl.BlockSpec((128, GK), full),
                pl.BlockSpec((1, GK), full), pl.BlockSpec((3, CW), full), pl.BlockSpec((1, CW), full),
                pl.BlockSpec((1, GV), full)]
    out_specs = [tokv, tokk, tokk, tokv, pl.BlockSpec((tb, 128), lambda i: (jmap(i), 0)), tokv, tokv, tokv,
                 pl.BlockSpec((128, GK), full), pl.BlockSpec((8, GK), full), pl.BlockSpec((8, CW), full),
                 pl.BlockSpec((8, CW), full), pl.BlockSpec((8, GV), full)]
    out_shape = [jax.ShapeDtypeStruct((t, GV), F32), jax.ShapeDtypeStruct((t, GK), F32),
                 jax.ShapeDtypeStruct((t, GK), F32), jax.ShapeDtypeStruct((t, GV), F32),
                 jax.ShapeDtypeStruct((t, 128), F32), jax.ShapeDtypeStruct((t, GV), BF16),
                 jax.ShapeDtypeStruct((t, CW), BF16), jax.ShapeDtypeStruct((t, CW), F32),
                 jax.ShapeDtypeStruct((128, GK), F32), jax.ShapeDtypeStruct((8, GK), F32),
                 jax.ShapeDtypeStruct((8, CW), F32), jax.ShapeDtypeStruct((8, CW), F32),
                 jax.ShapeDtypeStruct((8, GV), F32)]
    return _call(
        body, "gla_bwd_first", (nblk,), in_specs, out_specs, out_shape,
        [pltpu.VMEM((GV, GK), F32), pltpu.VMEM((tb, GK), F32), pltpu.VMEM((tb, GK), F32)],
        (z, z, z, z, z, z, z, z, z, z, z, z, dy, o_pre, sd, wpad, bias, conv_w, conv_norm, gla_norm4), riders)


def _gla_bwd_second(z, do, sd, wpad, bias, dqa, dka, dva, dlra, dzg, dzcb, dconv, conv_w, riders=()):
    t = z.shape[0]
    tb = min(TB_BWD, t)
    nblk, nb = t // tb, tb // CH
    jmap = lambda i: i

    def body(q_ref, k_ref, v_ref, lr_ref, cc_ref, cu_ref, do_ref, sd_ref, w_ref, bias_ref, dqa_ref, dka_ref,
             dva_ref, dlra_ref, dzg_ref, dzcb_ref, dc_ref, dcp_ref, dcn_ref, cw_ref,
             dz_ref, dw_ref, dbias_ref, dst, b_scr, db_scr, dq_scr, dk_scr, dv_scr):
        i = pl.program_id(0)

        @pl.when(i == 0)
        def _():
            dst[...] = jnp.zeros_like(dst)
            dw_ref[...] = jnp.zeros_like(dw_ref)
            dbias_ref[...] = jnp.zeros_like(dbias_ref)

        q_raw, k, v, lr, wp = q_ref[...], k_ref[...], v_ref[...], lr_ref[...], w_ref[...]
        pre, b, e, ei, qt, kt = _gla_recompute(q_raw, k, lr, wp, bias_ref[...], True, tb)
        b_scr[...] = b
        _gla_bwd_chunks(do_ref, sd_ref, dst, b_scr, db_scr, dq_scr, dk_scr, dv_scr, qt, kt, e, ei, v, True, nb)
        dpre, dlr, dw = _gate_bwd(db_scr[...], pre, lr, wp, True, tb)
        dw_ref[...] += dw
        _acc_rows(dbias_ref, jnp.sum(dpre, axis=0, keepdims=True))

        do = do_ref[...]
        qs = q_raw * 0.125
        hsel = jnp.where((_iota((GK, GV), 0) >> 6) == (_iota((GK, GV), 1) >> 7), 1.0, 0.0).astype(BF16)
        hsel_t = jnp.where((_iota((GV, GK), 0) >> 7) == (_iota((GV, GK), 1) >> 6), 1.0, 0.0).astype(BF16)
        sb = _dot_exact_rhs(qs * k, hsel, 2)
        dsk = _dot_exact_rhs(do * v, hsel_t, 2)
        dz_ref[:, 1536:1792] = (dqa_ref[...] + dq_scr[...] - dsk * k * 0.125).astype(BF16)
        dz_ref[:, 1792:2048] = (dka_ref[...] + dk_scr[...] - dsk * qs).astype(BF16)
        dz_ref[:, 2048:2560] = (dva_ref[...] + dv_scr[...] - sb * do).astype(BF16)
        dz_ref[:, 2560:3072] = dzg_ref[...]
        dz_ref[:, 3072:3200] = (dlra_ref[...] + dlr).astype(BF16)

        dc = dc_ref[...]
        rows = _iota(dc.shape, 0)
        dprev = jnp.where(i == 0, 0.0, dcp_ref[pl.ds(7, 1), :])
        dnext = jnp.where(i == nblk - 1, 0.0, dcn_ref[pl.ds(0, 1), :])
        dc_m1 = jnp.where(rows == 0, dprev, pltpu.roll(dc, 1, 0))
        dc_p1 = jnp.where(rows == tb - 1, dnext, pltpu.roll(dc, tb - 1, 0))
        dh = cw_ref[pl.ds(0, 1), :] * dc_p1 + cw_ref[pl.ds(1, 1), :] * dc + cw_ref[pl.ds(2, 1), :] * dc_m1
        dz_ref[:, 0:512] = dzcb_ref[...]
        dz_ref[:, 512:1024] = (dh * cu_ref[...]).astype(BF16)
        dz_ref[:, 1024:1536] = (dh * cc_ref[...]).astype(BF16)

    full = lambda i: (0, 0)
    tokv = pl.BlockSpec((tb, GV), lambda i: (i, 0))
    tokk = pl.BlockSpec((tb, GK), lambda i: (i, 0))
    dcp = pl.BlockSpec((8, CW), lambda i: (jnp.maximum(i * (tb // 8) - 1, 0), 0))
    dcn = pl.BlockSpec((8, CW), lambda i: (jnp.minimum((i + 1) * (tb // 8), t // 8 - 1), 0))
    in_specs = [_zspec(tb, GK, ZB_Q, jmap), _zspec(tb, GK, ZB_K, jmap), _zspec(tb, GV, ZB_V, jmap),
                _zspec(tb, 128, ZB_LR, jmap), _zspec(tb, CW, ZB_CC, jmap), _zspec(tb, CW, ZB_CU, jmap), tokv,
                pl.BlockSpec((nb, 128, GK), lambda i: (i, 0, 0)), pl.BlockSpec((128, GK), full),
                pl.BlockSpec((1, GK), full), tokk, tokk, tokv, pl.BlockSpec((tb, 128), lambda i: (i, 0)), tokv, tokv,
                tokv, dcp, dcn, pl.BlockSpec((3, CW), full)]
    out_specs = [pl.BlockSpec((tb, ZC), lambda i: (i, 0)), pl.BlockSpec((128, GK), full), pl.BlockSpec((8, GK), full)]
    out_shape = [jax.ShapeDtypeStruct((t, ZC), BF16), jax.ShapeDtypeStruct((128, GK), F32),
                 jax.ShapeDtypeStruct((8, GK), F32)]
    return _call(
        body, "gla_bwd_second", (nblk,), in_specs, out_specs, out_shape,
        [pltpu.VMEM((GV, GK), F32), pltpu.VMEM((tb, GK), F32), pltpu.VMEM((tb, GK), F32),
         pltpu.VMEM((tb, GK), F32), pltpu.VMEM((tb, GK), F32), pltpu.VMEM((tb, GV), F32)],
        (z, z, z, z, z, z, do, sd, wpad, bias, dqa, dka, dva, dlra, dzg, dzcb, dconv, dconv, dconv, conv_w), riders)


def _step(x, mem, target, shard, small_pack, vec, place):
    own, from_chips = {}, {}

    def pair_sums(names, g4, from_sibling):
        pbs = []
        for n, g, s in zip(names, g4, from_sibling):
            pb, own[n] = _rs_pair_sum(place, g, s, "pair_sum_" + n)
            pbs.append(pb)
        return pbs

    def by_dest(g, n):
        return g.reshape((4, 2) + shard[n].shape)

    w_in, small_all = _exchange(_gather_rider([shard["w_in"], small_pack]), "gather_w_in")
    w_in = jnp.pad(w_in.reshape(ZW, D), ((0, ZC - ZW), (0, 0)))
    small_all = small_all.reshape(NDEV, -1)
    p, off = {}, 0
    for n, (r, c) in SMALL_SHARDED.items():
        p[n] = small_all[:, off:off + r * c].reshape(NDEV, r, c).transpose(1, 0, 2).reshape(r, NDEV * c)
        off += r * c
    zeros_lr = jnp.zeros((128 - LR, GK), BF16)
    waf_pad = jnp.concatenate([p["w_af"].astype(BF16), zeros_lr], axis=0)
    wab_pad = jnp.concatenate([jnp.zeros((LR, GK), BF16), p["w_ab"].astype(BF16), zeros_lr[:128 - 2 * LR]], axis=0)
    gla_norm4 = jnp.tile(vec["gla_norm"], (1, NH))

    z, hb, w_out, w_xq, w_xo, w_xkv = _inproj(
        x, vec["mix_norm"], w_in, [_gather_rider([shard[n] for n in ("w_out", "w_xq", "w_xo", "w_xkv")])])
    w_out, w_xq, w_xo = [a.reshape(D, D) for a in (w_out, w_xq, w_xo)]
    o_f, sd_f, w_up_t = _gla_fwd_sweep(z, waf_pad, vec["b_af"], False, riders=[_gather_rider([shard["w_up"]])])
    w_up_t = w_up_t.reshape(FF, D)
    yb, o_pre, sd_b, w_down = _gla_fwd_sweep(z, wab_pad, vec["b_ab"], True,
                                             (o_f, p["conv_w"], vec["conv_norm"], gla_norm4),
                                             riders=[_gather_rider([shard["w_down"]])])
    w_down = w_down.reshape(FF, D)
    kv, memn = _kv_proj(mem, vec["mem_norm"], w_xkv)
    kb, vb = kv[:, :D].astype(BF16), kv[:, D:].astype(BF16)
    x1, x2, xn1, qb, attb = _attn_fwd(x, yb, w_out, vec["xa_norm"], w_xq, kb, vb, w_xo)
    h1b, xn2, dx3, dx3b, loss8, dfinal = _mlp_fwd(x2, vec["mlp_norm"], w_up_t, w_down, vec["final_norm"], target)

    ab, dh1b, dx2, dx2b, dmlp = _mlp_bwd(dx3, dx3b, h1b, w_down, w_up_t, x2, vec["mlp_norm"])
    g_mlp = [by_dest(_matmul_tn(ab, dx3b, "dw_down")[0], "w_down"),
             by_dest(_matmul_tn(dh1b, xn2, "dw_up")[0], "w_up")]
    dx1, dx1b, dy, dqb, dkv, dxa = _attn_bwd(x1, dx2, dx2b, qb, kb, vb, w_xo, w_xq, w_out, vec["xa_norm"])
    dw_xo, *s_mlp = _matmul_tn(attb, dx2b, "dw_xo", riders=[_sibling_rider(g_mlp)])
    pb_mlp = pair_sums(("w_down", "w_up"), g_mlp, s_mlp)
    dw_xkv, dmemn = _kv_bwd(dkv, memn, mem, vec["mem_norm"], w_xkv)
    att_names = ("w_xo", "w_xq", "w_out", "w_xkv")
    g_att = [by_dest(g, n) for g, n in zip(
        (dw_xo, _matmul_tn(xn1, dqb, "dw_xq")[0], _matmul_tn(yb, dx1b, "dw_out")[0], dw_xkv), att_names)]
    res = _gla_bwd_first(z, dy, o_pre, sd_f, waf_pad, vec["b_af"], p["conv_w"], vec["conv_norm"], gla_norm4,
                         riders=[_chips_rider(pb_mlp), _sibling_rider(g_att)])
    do, dqa, dka, dva, dlra, dzg, dzcb, dconv, dwaf, dbaf, dcw, dcn, dgn = res[:13]
    from_chips["w_down"], from_chips["w_up"] = res[13:15]
    pb_att = pair_sums(att_names, g_att, res[15:])
    dz, dwab, dbab, *c_att = _gla_bwd_second(z, do, sd_b, wab_pad, vec["b_ab"], dqa, dka, dva, dlra, dzg, dzcb, dconv,
                                             p["conv_w"], riders=[_chips_rider(pb_att)])
    from_chips.update(zip(att_names, c_att))
    g_in = [by_dest(_matmul_tn(dz, hb, "dw_in", rows=ZW)[0], "w_in")]
    pb_in = pair_sums(("w_in",), g_in, _exchange(_sibling_rider(g_in), "grads_to_sibling_w_in"))
    grad_x, dmix, from_chips["w_in"] = _inproj_bwd(dz, w_in, x, dx1, vec["mix_norm"], riders=[_chips_rider(pb_in)])

    small_grads = {
        "mix_norm": dmix[0:1], "conv_w": dcw[0:3], "conv_norm": dcn[0:1],
        "w_af": dwaf[0:LR], "b_af": dbaf[0:1], "w_ab": dwab[LR:2 * LR], "b_ab": dbab[0:1],
        "gla_norm": (dgn[0:1, 0:128] + dgn[0:1, 128:256]) + (dgn[0:1, 256:384] + dgn[0:1, 384:512]),
        "xa_norm": dxa[0:1], "mem_norm": dmemn[0:1], "mlp_norm": dmlp[0:1], "final_norm": dfinal[0:1],
    }
    return loss8[0:1, 0:1], grad_x, small_grads, own, from_chips


def _place():
    return lax.axis_index("x"), lax.axis_index("y"), lax.axis_index("c")


class _Rider:
    def __init__(self, arrays, out_shape, scratch, start, finish):
        self.arrays, self.out_shape, self.scratch, self.start, self.finish = arrays, out_shape, scratch, start, finish


def _gather_rider(blks):
    n = len(blks)

    def plan(in_refs, out_refs, sems):
        send_sems, recv_sems, local_sems = sems
        x, y, c = _place()
        me, sibling = (x, y, c), (x, y, 1 - c)
        chips = [(1 - x, y, c), (x, 1 - y, c), (1 - x, 1 - y, c)]

        def copy(a, k, block, to, own=False):
            px, py, pc = block
            dst = out_refs[a].at[4 * px + 2 * py + pc]
            return pltpu.make_async_remote_copy(
                src_ref=in_refs[a] if own else dst, dst_ref=dst, send_sem=send_sems.at[k, a],
                recv_sem=recv_sems.at[k, a], device_id=to, device_id_type=MESH)

        def local(a):
            return pltpu.make_async_copy(in_refs[a], out_refs[a].at[4 * x + 2 * y + c], local_sems.at[a])

        def own_sends(a):
            return [copy(a, 0, me, sibling, own=True)] + [copy(a, 1 + j, me, chip, own=True)
                                                          for j, chip in enumerate(chips)]

        return copy, local, own_sends, me, sibling, chips

    def start(in_refs, out_refs, sems):
        _, local, own_sends, _, _, _ = plan(in_refs, out_refs, sems)
        for a in range(n):
            local(a).start()
            for cp in own_sends(a):
                cp.start()

    def finish(in_refs, out_refs, sems):
        copy, local, own_sends, me, sibling, chips = plan(in_refs, out_refs, sems)
        for j, chip in enumerate(chips):
            for a in range(n):
                copy(a, 1 + j, chip, me).wait_recv()
                copy(a, 4 + j, chip, sibling).start()
        for a in range(n):
            copy(a, 0, sibling, me).wait_recv()
            for j, (px, py, pc) in enumerate(chips):
                copy(a, 4 + j, (px, py, 1 - pc), me).wait_recv()
            for cp in own_sends(a) + [copy(a, 4 + j, chip, sibling) for j, chip in enumerate(chips)]:
                cp.wait_send()
            local(a).wait()

    return _Rider(blks, [jax.ShapeDtypeStruct((NDEV,) + b.shape, b.dtype) for b in blks],
                  [pltpu.SemaphoreType.DMA((7, n)), pltpu.SemaphoreType.DMA((7, n)), pltpu.SemaphoreType.DMA((n,))],
                  start, finish)


def _sibling_rider(g4s):
    n = len(g4s)

    def copies(in_refs, out_refs, sems):
        send_sems, recv_sems = sems
        x, y, c = _place()
        return [pltpu.make_async_remote_copy(
            src_ref=in_refs[a].at[k, 1 - c], dst_ref=out_refs[a].at[k], send_sem=send_sems.at[k, a],
            recv_sem=recv_sems.at[k, a], device_id=(x, y, 1 - c), device_id_type=MESH)
            for a in range(n) for k in range(4)]

    def start(in_refs, out_refs, sems):
        for cp in copies(in_refs, out_refs, sems):
            cp.start()

    def finish(in_refs, out_refs, sems):
        for cp in copies(in_refs, out_refs, sems):
            cp.wait()

    return _Rider(g4s, [jax.ShapeDtypeStruct((4,) + g.shape[2:], g.dtype) for g in g4s],
                  [pltpu.SemaphoreType.DMA((4, n)), pltpu.SemaphoreType.DMA((4, n))], start, finish)


def _chips_rider(pbs):
    n = len(pbs)

    def copies(in_refs, out_refs, sems):
        send_sems, recv_sems = sems
        x, y, c = _place()
        peers = [(1 - x, y), (x, 1 - y), (1 - x, 1 - y)]
        return [pltpu.make_async_remote_copy(
            src_ref=in_refs[a].at[2 * px + py], dst_ref=out_refs[a].at[k], send_sem=send_sems.at[k, a],
            recv_sem=recv_sems.at[k, a], device_id=(px, py, c), device_id_type=MESH)
            for a in range(n) for k, (px, py) in enumerate(peers)]

    def start(in_refs, out_refs, sems):
        for cp in copies(in_refs, out_refs, sems):
            cp.start()

    def finish(in_refs, out_refs, sems):
        for cp in copies(in_refs, out_refs, sems):
            cp.wait()

    return _Rider(pbs, [jax.ShapeDtypeStruct((3,) + p.shape[1:], p.dtype) for p in pbs],
                  [pltpu.SemaphoreType.DMA((3, n)), pltpu.SemaphoreType.DMA((3, n))], start, finish)


def _exchange(rider, name):
    n_in, n_out = len(rider.arrays), len(rider.out_shape)

    def body(*refs):
        ins, outs, sems = refs[:n_in], refs[n_in:n_in + n_out], refs[n_in + n_out:]
        rider.start(ins, outs, sems)
        rider.finish(ins, outs, sems)

    hbm = pl.BlockSpec(memory_space=pltpu.HBM)
    return pl.pallas_call(body, name=name, out_shape=rider.out_shape, in_specs=[hbm] * n_in,
                          out_specs=[hbm] * n_out, scratch_shapes=rider.scratch)(*rider.arrays)


def _rs_pair_sum(place, g4, r1, name):
    rows, cols = g4.shape[2:]
    tr = min(rows, 512)

    def body(pl_ref, g_ref, r_ref, pb_ref, own_ref):
        s = g_ref[0, 0] + r_ref[0]
        pb_ref[0] = s.astype(BF16)

        @pl.when(pl.program_id(1) == pl_ref[0])
        def _():
            own_ref[...] = s

    grid_spec = pltpu.PrefetchScalarGridSpec(
        num_scalar_prefetch=1, grid=(rows // tr, 4),
        in_specs=[pl.BlockSpec((1, 1, tr, cols), lambda r, k, p: (k, p[1], r, 0)),
                  pl.BlockSpec((1, tr, cols), lambda r, k, p: (k, r, 0))],
        out_specs=[pl.BlockSpec((1, tr, cols), lambda r, k, p: (k, r, 0)),
                   pl.BlockSpec((tr, cols), lambda r, k, p: (r, 0))])
    return pl.pallas_call(
        body, name=name, grid_spec=grid_spec,
        out_shape=[jax.ShapeDtypeStruct((4, rows, cols), BF16), jax.ShapeDtypeStruct((rows, cols), F32)],
        compiler_params=_cparams(("arbitrary", "arbitrary")))(place, g4, r1)


def _small_all_reduce(vec):
    m_per = vec.shape[0]

    def body(x_ref, all_ref, sum_ref, send_sems, recv_sems, local_sem):
        x, y, c = _place()
        me, sibling = (x, y, c), (x, y, 1 - c)
        chips = [(1 - x, y), (x, 1 - y), (1 - x, 1 - y)]

        def rows(px, py, pc):
            return all_ref.at[4 * px + 2 * py + pc]

        def copy(k, block, to, src=None):
            return pltpu.make_async_remote_copy(
                src_ref=rows(*block) if src is None else src, dst_ref=rows(*block),
                send_sem=send_sems.at[k], recv_sem=recv_sems.at[k], device_id=to, device_id_type=MESH)

        mine = pltpu.make_async_copy(x_ref, rows(*me), local_sem)
        mine.start()
        first = [copy(0, me, sibling, src=x_ref)]
        first += [copy(1 + j, me, (*chip, c), src=x_ref) for j, chip in enumerate(chips)]
        for cp in first:
            cp.start()
        passed = [copy(4 + j, (*chip, c), sibling) for j, chip in enumerate(chips)]
        for j, chip in enumerate(chips):
            copy(1 + j, (*chip, c), me).wait_recv()
            passed[j].start()
        copy(0, sibling, me).wait_recv()
        for j, chip in enumerate(chips):
            copy(4 + j, (*chip, 1 - c), me).wait_recv()
        for cp in first + passed:
            cp.wait_send()
        mine.wait()
        total = all_ref[0]
        for d in range(1, NDEV):
            total = total + all_ref[d]
        sum_ref[...] = total

    return pl.pallas_call(
        body, name="small_all_reduce",
        out_shape=[jax.ShapeDtypeStruct((NDEV, m_per, 128), F32), jax.ShapeDtypeStruct((m_per, 128), F32)],
        in_specs=[pl.BlockSpec(memory_space=pltpu.VMEM)],
        out_specs=[pl.BlockSpec(memory_space=pltpu.VMEM), pl.BlockSpec(memory_space=pltpu.VMEM)],
        scratch_shapes=[pltpu.SemaphoreType.DMA((7,)), pltpu.SemaphoreType.DMA((7,)), pltpu.SemaphoreType.DMA],
    )(vec)[1]


def _adamw_math(w, g, m, v):
    m = ADAM_B1 * m + (1.0 - ADAM_B1) * g
    v = ADAM_B2 * v + (1.0 - ADAM_B2) * (g * g)
    m_hat = m / (1.0 - ADAM_B1 ** ADAM_STEP)
    v_hat = v / (1.0 - ADAM_B2 ** ADAM_STEP)
    delta = -ADAM_LR * (m_hat / (jnp.sqrt(v_hat) + ADAM_EPS) + ADAM_WD * w)
    return delta, m, v


def _adamw(w, m, v, own, r2, name):
    _, r, c = w.shape
    tr = 256 if r % 256 == 0 else r

    def body(w_ref, m_ref, v_ref, o_ref, r_ref, g_ref, d_ref, nm_ref, nv_ref):
        g = ((o_ref[...] + r_ref[0].astype(F32)) + r_ref[1].astype(F32)) + r_ref[2].astype(F32)
        g_ref[...] = g
        d_ref[...], nm_ref[...], nv_ref[...] = _adamw_math(w_ref[...], g, m_ref[...], v_ref[...])

    spec = pl.BlockSpec((None, tr, c), lambda i: (0, i, 0))
    return pl.pallas_call(
        body, name=name, grid=(r // tr,),
        in_specs=[spec, spec, spec, pl.BlockSpec((tr, c), lambda i: (i, 0)),
                  pl.BlockSpec((3, tr, c), lambda i: (0, i, 0))],
        out_specs=[spec] * 4, out_shape=[jax.ShapeDtypeStruct((1, r, c), F32)] * 4,
        compiler_params=_cparams(("arbitrary",)))(w, m, v, own, r2)


def _adamw_small(ws, gs, ms, vs):
    n = len(ws)

    def body(*refs):
        ins, outs = refs[:4 * n], refs[4 * n:]
        for i in range(n):
            d, m, v = _adamw_math(ins[i][...], ins[n + i][...], ins[2 * n + i][...], ins[3 * n + i][...])
            outs[i][...], outs[n + i][...], outs[2 * n + i][...] = d, m, v

    shapes = [jax.ShapeDtypeStruct(w.shape, F32) for w in ws]
    outs = pl.pallas_call(body, name="adamw_small", out_shape=shapes * 3)(*ws, *gs, *ms, *vs)
    return outs[:n], outs[n:2 * n], outs[2 * n:]


MATS = ("w_in", "w_out", "w_xq", "w_xo", "w_xkv", "w_up", "w_down")
SMALL = ("mix_norm", "conv_w", "conv_norm", "w_af", "b_af", "w_ab", "b_ab", "gla_norm", "xa_norm", "mem_norm",
         "mlp_norm", "final_norm")
WEIGHTS = ("mix_norm", "w_in", "conv_w", "conv_norm", "w_af", "b_af", "w_ab", "b_ab", "gla_norm", "w_out", "xa_norm",
           "mem_norm", "w_xq", "w_xkv", "w_xo", "mlp_norm", "w_up", "w_down", "final_norm")
SMALL_SHARDED = {"conv_w": (3, 64), "w_af": (LR, 32), "w_ab": (LR, 32)}
SMALL_PACK_ROWS = 16


def kernel(x, mem, mix_norm, w_in, conv_w, conv_norm, w_af, b_af, w_ab, b_ab, gla_norm, w_out, xa_norm, mem_norm, w_xq, w_xkv, w_xo, mlp_norm, w_up, w_down, final_norm, loss_target, m_mix_norm, m_w_in, m_conv_w, m_conv_norm, m_w_af, m_b_af, m_w_ab, m_b_ab, m_gla_norm, m_w_out, m_xa_norm, m_mem_norm, m_w_xq, m_w_xkv, m_w_xo, m_mlp_norm, m_w_up, m_w_down, m_final_norm, v_mix_norm, v_w_in, v_conv_w, v_conv_norm, v_w_af, v_b_af, v_w_ab, v_b_ab, v_gla_norm, v_w_out, v_xa_norm, v_mem_norm, v_w_xq, v_w_xkv, v_w_xo, v_mlp_norm, v_w_up, v_w_down, v_final_norm):
    w = dict(mix_norm=mix_norm, w_in=w_in, conv_w=conv_w, conv_norm=conv_norm, w_af=w_af, b_af=b_af, w_ab=w_ab,
             b_ab=b_ab, gla_norm=gla_norm, w_out=w_out, xa_norm=xa_norm, mem_norm=mem_norm, w_xq=w_xq, w_xkv=w_xkv,
             w_xo=w_xo, mlp_norm=mlp_norm, w_up=w_up, w_down=w_down, final_norm=final_norm)
    mom = dict(mix_norm=m_mix_norm, w_in=m_w_in, conv_w=m_conv_w, conv_norm=m_conv_norm, w_af=m_w_af, b_af=m_b_af,
               w_ab=m_w_ab, b_ab=m_b_ab, gla_norm=m_gla_norm, w_out=m_w_out, xa_norm=m_xa_norm, mem_norm=m_mem_norm,
               w_xq=m_w_xq, w_xkv=m_w_xkv, w_xo=m_w_xo, mlp_norm=m_mlp_norm, w_up=m_w_up, w_down=m_w_down,
               final_norm=m_final_norm)
    var = dict(mix_norm=v_mix_norm, w_in=v_w_in, conv_w=v_conv_w, conv_norm=v_conv_norm, w_af=v_w_af, b_af=v_b_af,
               w_ab=v_w_ab, b_ab=v_b_ab, gla_norm=v_gla_norm, w_out=v_w_out, xa_norm=v_xa_norm, mem_norm=v_mem_norm,
               w_xq=v_w_xq, w_xkv=v_w_xkv, w_xo=v_w_xo, mlp_norm=v_mlp_norm, w_up=v_w_up, w_down=v_w_down,
               final_norm=v_final_norm)
    xi, yi, ci = _place()
    me = 4 * xi + 2 * yi + ci
    two_d = lambda a: a.reshape(a.shape[-2:]) if a.ndim == 3 else a.reshape(1, a.shape[-1])

    small = jnp.concatenate([w[n].reshape(-1) for n in SMALL_SHARDED])
    small = jnp.pad(small, (0, SMALL_PACK_ROWS * 128 - small.shape[0])).reshape(SMALL_PACK_ROWS, 128)
    shard = {n: two_d(w[n]).astype(BF16) for n in MATS}
    for n in ("w_in", "w_up"):
        shard[n] = shard[n].T
    vec = {n: two_d(w[n]) for n in SMALL if n not in SMALL_SHARDED}
    place = jnp.stack([2 * xi + yi, ci]).astype(jnp.int32)
    loss_part, grad_x, grads, own, from_chips = _step(x[0], mem[0], loss_target[0], shard, small, vec, place)

    order = [n for n in SMALL if n not in SMALL_SHARDED] + list(SMALL_SHARDED)
    flat = jnp.concatenate([grads[n].reshape(-1) for n in order] + [loss_part.reshape(-1)])
    n_flat = flat.shape[0]
    tot = _small_all_reduce(jnp.pad(flat, (0, SMALL_ROWS * 128 - n_flat)).reshape(SMALL_ROWS, 128)).reshape(-1)
    gsmall, off = {}, 0
    for n in order:
        size = grads[n].size
        full = tot[off:off + size].reshape(grads[n].shape)
        off += size
        if n in SMALL_SHARDED:
            r, c = SMALL_SHARDED[n]
            full = lax.dynamic_slice_in_dim(full, me * c, c, axis=1)
        gsmall[n] = full
    loss = tot[off]

    out_g, out_d, out_m, out_v = {}, {}, {}, {}
    own["w_up"], from_chips["w_up"] = own["w_up"].T, from_chips["w_up"].transpose(0, 2, 1)
    for n in MATS:
        if n == "w_in":
            res = _adamw(*[a.transpose(0, 2, 1) for a in (w[n], mom[n], var[n])], own[n], from_chips[n], "adamw_" + n)
            res = [a.transpose(0, 2, 1) for a in res]
        else:
            res = _adamw(w[n], mom[n], var[n], own[n], from_chips[n], "adamw_" + n)
        out_g[n], out_d[n], out_m[n], out_v[n] = res
    ds, nms, nvs = _adamw_small([two_d(w[n]) for n in SMALL], [gsmall[n] for n in SMALL],
                                [two_d(mom[n]) for n in SMALL], [two_d(var[n]) for n in SMALL])
    for i, n in enumerate(SMALL):
        out_g[n], out_d[n], out_m[n], out_v[n] = [a.reshape(w[n].shape) for a in (gsmall[n], ds[i], nms[i], nvs[i])]

    return (loss, grad_x[None], *[out_g[n] for n in WEIGHTS], *[out_d[n] for n in WEIGHTS],
            *[out_m[n] for n in WEIGHTS], *[out_v[n] for n in WEIGHTS])
```

```python
import functools
import itertools

import jax
import jax.numpy as jnp
from jax import lax
from jax.experimental import pallas as pl
from jax.experimental.pallas import tpu as pltpu

F32 = jnp.float32
BF16 = jnp.bfloat16

D = 1024
CW = 512
GK = 256
GV = 512
NH = 4
CH = 64
LR = 16
NMEM = 256
XD = 256
FF = 4096
ZW = 3104
ZC = 3200
EPS = 1e-6
NDEV = 8

ZB_CB, ZB_CC, ZB_CU, ZB_V, ZB_G = 0, 1, 2, 4, 5
ZB_Q, ZB_K = 6, 7
ZB_LR = 24

TM = 512
TM_MLP = 256
TF = 512
TB = 512
TB_BWD = 512
TT = 2048
VMEM_LIMIT = 56 * 1024 * 1024

ADAM_LR, ADAM_B1, ADAM_B2, ADAM_EPS, ADAM_WD, ADAM_STEP = 0.001, 0.9, 0.999, 1e-08, 0.01, 10

XKV_SHARD = 2 * D // NDEV
SMALL_ROWS = 128

MESH = pl.DeviceIdType.MESH


def _cparams(sem):
    return pltpu.CompilerParams(dimension_semantics=sem, vmem_limit_bytes=VMEM_LIMIT)


def _call(body, name, grid, in_specs, out_specs, out_shape, scratch, args, riders=()):
    n_in, n_out, n_scr = len(in_specs), len(out_specs), len(scratch)
    counts = [(len(r.arrays), len(r.out_shape), len(r.scratch)) for r in riders]

    def take(refs, pos, sizes):
        groups = []
        for size in sizes:
            groups.append(refs[pos:pos + size])
            pos += size
        return groups, pos

    def wrapped(*refs):
        ins, pos = refs[:n_in], n_in
        r_ins, pos = take(refs, pos, [c[0] for c in counts])
        outs, pos = refs[pos:pos + n_out], pos + n_out
        r_outs, pos = take(refs, pos, [c[1] for c in counts])
        scr, pos = refs[pos:pos + n_scr], pos + n_scr
        r_scr, pos = take(refs, pos, [c[2] for c in counts])
        ids = [pl.program_id(d) for d in range(len(grid))]
        first = functools.reduce(lambda a, b: a & b, [i == 0 for i in ids])
        last = functools.reduce(lambda a, b: a & b, [i == g - 1 for i, g in zip(ids, grid)])

        @pl.when(first)
        def _():
            for r, a, b, c in zip(riders, r_ins, r_outs, r_scr):
                r.start(a, b, c)

        body(*ins, *outs, *scr)

        @pl.when(last)
        def _():
            for r, a, b, c in zip(riders, r_ins, r_outs, r_scr):
                r.finish(a, b, c)

    hbm = pl.BlockSpec(memory_space=pltpu.HBM)
    r_args = [a for r in riders for a in r.arrays]
    r_shapes = [s for r in riders for s in r.out_shape]
    return pl.pallas_call(
        wrapped if riders else body, name=name, grid=grid, in_specs=list(in_specs) + [hbm] * len(r_args),
        out_specs=list(out_specs) + [hbm] * len(r_shapes), out_shape=list(out_shape) + r_shapes,
        scratch_shapes=list(scratch) + [s for r in riders for s in r.scratch],
        compiler_params=_cparams(("arbitrary",) * len(grid)))(*args, *r_args)


def _dot(a, b):
    return jnp.dot(a.astype(BF16), b.astype(BF16), preferred_element_type=F32)


def _dot_nt(a, b):
    return lax.dot_general(a.astype(BF16), b.astype(BF16), (((1,), (1,)), ((), ())), preferred_element_type=F32)


def _dot_tn(a, b):
    return lax.dot_general(a.astype(BF16), b.astype(BF16), (((0,), (0,)), ((), ())), preferred_element_type=F32)


def _split(x, n):
    parts = []
    for _ in range(n):
        p = x.astype(BF16)
        parts.append(p)
        x = x - p.astype(F32)
    return parts


def _dot_exact_lhs(m, x, n):
    return functools.reduce(lambda a, b: a + b, [jnp.dot(m, p, preferred_element_type=F32) for p in _split(x, n)])


def _dot_exact_rhs(x, m, n):
    return functools.reduce(lambda a, b: a + b, [jnp.dot(p, m, preferred_element_type=F32) for p in _split(x, n)])


def _rms(x, g):
    r = lax.rsqrt(jnp.mean(x * x, axis=-1, keepdims=True) + EPS)
    return x * r * g, r


def _rms_bwd(x, r, g, dy):
    xr = x * r
    u = dy * g
    dx = r * (u - xr * jnp.mean(u * xr, axis=-1, keepdims=True))
    return dx, jnp.sum(dy * xr, axis=0, keepdims=True)


def _iota(shape, dim):
    return lax.broadcasted_iota(jnp.int32, shape, dim)


def _sigmoid(x):
    return 1.0 / (1.0 + jnp.exp(-x))


def _acc_rows(ref, row):
    ref[...] += jnp.broadcast_to(row, ref.shape)


def _inproj(x, g, w_t, riders=()):
    t = x.shape[0]
    tm = min(TM, t)

    def body(x_ref, g_ref, w_ref, z_ref, h_ref):
        h, _ = _rms(x_ref[...], g_ref[...])
        hb = h.astype(BF16)
        h_ref[...] = hb
        z_ref[...] = _dot_nt(hb, w_ref[...])

    return _call(
        body, "inproj", (t // tm,),
        [pl.BlockSpec((tm, D), lambda i: (i, 0)), pl.BlockSpec((1, D), lambda i: (0, 0)),
         pl.BlockSpec((ZC, D), lambda i: (0, 0))],
        [pl.BlockSpec((tm, ZC), lambda i: (i, 0)), pl.BlockSpec((tm, D), lambda i: (i, 0))],
        [jax.ShapeDtypeStruct((t, ZC), F32), jax.ShapeDtypeStruct((t, D), BF16)], [], (x, g, w_t), riders)


def _kv_proj(mem, g, w):
    def body(m_ref, g_ref, w_ref, kv_ref, mn_ref):
        mn, _ = _rms(m_ref[...], g_ref[...])
        mb = mn.astype(BF16)
        mn_ref[...] = mb
        for j in range(NDEV):
            kv_ref[:, j * XKV_SHARD:(j + 1) * XKV_SHARD] = jnp.dot(mb, w_ref[j], preferred_element_type=F32)

    return pl.pallas_call(
        body, name="kv_proj",
        out_shape=[jax.ShapeDtypeStruct((NMEM, 2 * D), F32), jax.ShapeDtypeStruct((NMEM, D), BF16)],
        compiler_params=pltpu.CompilerParams(vmem_limit_bytes=VMEM_LIMIT))(mem, g, w)


def _softmax_head(qb, kb):
    s = _dot_nt(qb, kb) * (1.0 / 16.0)
    e = jnp.exp(s - jnp.max(s, axis=-1, keepdims=True))
    return e / jnp.sum(e, axis=-1, keepdims=True)


def _attn_fwd(x, yb, w_out, g, w_xq, kb, vb, w_xo):
    t = x.shape[0]
    tm = min(TM, t)

    def body(x_ref, y_ref, wo_ref, g_ref, wq_ref, k_ref, v_ref, wx_ref, x1_ref, x2_ref, xn_ref, q_ref, a_ref):
        x1 = x_ref[...] + jnp.dot(y_ref[...], wo_ref[...], preferred_element_type=F32)
        x1_ref[...] = x1
        xn, _ = _rms(x1, g_ref[...])
        xb = xn.astype(BF16)
        xn_ref[...] = xb
        qb = jnp.dot(xb, wq_ref[...], preferred_element_type=F32).astype(BF16)
        q_ref[...] = qb
        for h in range(NH):
            hs = slice(h * XD, (h + 1) * XD)
            p = _softmax_head(qb[:, hs], k_ref[:, hs])
            a_ref[:, hs] = _dot(p, v_ref[:, hs]).astype(BF16)
        x2_ref[...] = x1 + jnp.dot(a_ref[...], wx_ref[...], preferred_element_type=F32)

    tok = lambda i: (i, 0)
    full = lambda i: (0, 0)
    return pl.pallas_call(
        body, name="attn_fwd", grid=(t // tm,),
        in_specs=[pl.BlockSpec((tm, D), tok), pl.BlockSpec((tm, D), tok), pl.BlockSpec((D, D), full),
                  pl.BlockSpec((1, D), full), pl.BlockSpec((D, D), full), pl.BlockSpec((NMEM, D), full),
                  pl.BlockSpec((NMEM, D), full), pl.BlockSpec((D, D), full)],
        out_specs=[pl.BlockSpec((tm, D), tok)] * 5,
        out_shape=[jax.ShapeDtypeStruct((t, D), F32), jax.ShapeDtypeStruct((t, D), F32),
                   jax.ShapeDtypeStruct((t, D), BF16), jax.ShapeDtypeStruct((t, D), BF16),
                   jax.ShapeDtypeStruct((t, D), BF16)],
        compiler_params=_cparams(("arbitrary",)))(x, yb, w_out, g, w_xq, kb, vb, w_xo)


def _mlp_fwd(x2, g, w_up_t, w_down, fg, target):
    t = x2.shape[0]
    tm = min(TM_MLP, t)

    def body(x_ref, g_ref, wu_ref, wd_ref, fg_ref, t_ref, h1_ref, xn_ref, dx_ref, dxb_ref, loss_ref, dfg_ref, ab):
        @pl.when(pl.program_id(0) == 0)
        def _():
            loss_ref[...] = jnp.zeros_like(loss_ref)
            dfg_ref[...] = jnp.zeros_like(dfg_ref)

        x = x_ref[...]
        xn, _ = _rms(x, g_ref[...])
        xnb = xn.astype(BF16)
        xn_ref[...] = xnb
        for q in range(FF // TF):
            cols = slice(q * TF, (q + 1) * TF)
            h1 = _dot_nt(xnb, wu_ref[cols, :])
            h1_ref[:, cols] = h1.astype(BF16)
            hr = jnp.maximum(h1, 0.0)
            ab[:, cols] = (hr * hr).astype(BF16)
        x3 = x + jnp.dot(ab[...], wd_ref[...], preferred_element_type=F32)
        y, r = _rms(x3, fg_ref[...])
        e = y - t_ref[...]
        row = jnp.mean(e * e, axis=-1, keepdims=True)
        _acc_rows(loss_ref, 0.5 * jnp.sum(row, axis=0, keepdims=True))
        dx, dfg = _rms_bwd(x3, r, fg_ref[...], e * (1.0 / D))
        dx_ref[...] = dx
        dxb_ref[...] = dx.astype(BF16)
        _acc_rows(dfg_ref, dfg)

    tok = lambda i: (i, 0)
    full = lambda i: (0, 0)
    once = pl.Buffered(1)
    return pl.pallas_call(
        body, name="mlp_fwd", grid=(t // tm,),
        in_specs=[pl.BlockSpec((tm, D), tok), pl.BlockSpec((1, D), full),
                  pl.BlockSpec((FF, D), full, pipeline_mode=once), pl.BlockSpec((FF, D), full, pipeline_mode=once),
                  pl.BlockSpec((1, D), full), pl.BlockSpec((tm, D), tok)],
        out_specs=[pl.BlockSpec((tm, FF), tok), pl.BlockSpec((tm, D), tok), pl.BlockSpec((tm, D), tok),
                   pl.BlockSpec((tm, D), tok), pl.BlockSpec((8, 128), full), pl.BlockSpec((8, D), full)],
        out_shape=[jax.ShapeDtypeStruct((t, FF), BF16), jax.ShapeDtypeStruct((t, D), BF16),
                   jax.ShapeDtypeStruct((t, D), F32), jax.ShapeDtypeStruct((t, D), BF16),
                   jax.ShapeDtypeStruct((8, 128), F32), jax.ShapeDtypeStruct((8, D), F32)],
        scratch_shapes=[pltpu.VMEM((tm, FF), BF16)],
        compiler_params=_cparams(("arbitrary",)))(x2, g, w_up_t, w_down, fg, target)


def _mlp_bwd(dx3, dx3b, h1b, w_down, w_up_t, x2, g):
    t = x2.shape[0]
    tm = min(TM_MLP, t)

    def body(dx_ref, dxb_ref, h1_ref, wd_ref, wu_ref, x_ref, g_ref, a_ref, dh_ref, dx2_ref, dx2b_ref, dg_ref):
        @pl.when(pl.program_id(0) == 0)
        def _():
            dg_ref[...] = jnp.zeros_like(dg_ref)

        for q in range(FF // TF):
            cols = slice(q * TF, (q + 1) * TF)
            hr = jnp.maximum(h1_ref[:, cols].astype(F32), 0.0)
            da = _dot_nt(dxb_ref[...], wd_ref[cols, :])
            a_ref[:, cols] = (hr * hr).astype(BF16)
            dh_ref[:, cols] = (da * 2.0 * hr).astype(BF16)
        dxn = jnp.dot(dh_ref[...], wu_ref[...], preferred_element_type=F32)
        x = x_ref[...]
        r = lax.rsqrt(jnp.mean(x * x, axis=-1, keepdims=True) + EPS)
        dx, dg = _rms_bwd(x, r, g_ref[...], dxn)
        dx2 = dx_ref[...] + dx
        dx2_ref[...] = dx2
        dx2b_ref[...] = dx2.astype(BF16)
        _acc_rows(dg_ref, dg)

    tok = lambda i: (i, 0)
    full = lambda i: (0, 0)
    once = pl.Buffered(1)
    return pl.pallas_call(
        body, name="mlp_bwd", grid=(t // tm,),
        in_specs=[pl.BlockSpec((tm, D), tok), pl.BlockSpec((tm, D), tok), pl.BlockSpec((tm, FF), tok),
                  pl.BlockSpec((FF, D), full, pipeline_mode=once), pl.BlockSpec((FF, D), full, pipeline_mode=once),
                  pl.BlockSpec((tm, D), tok), pl.BlockSpec((1, D), full)],
        out_specs=[pl.BlockSpec((tm, FF), tok), pl.BlockSpec((tm, FF), tok), pl.BlockSpec((tm, D), tok),
                   pl.BlockSpec((tm, D), tok), pl.BlockSpec((8, D), full)],
        out_shape=[jax.ShapeDtypeStruct((t, FF), BF16), jax.ShapeDtypeStruct((t, FF), BF16),
                   jax.ShapeDtypeStruct((t, D), F32), jax.ShapeDtypeStruct((t, D), BF16),
                   jax.ShapeDtypeStruct((8, D), F32)],
        compiler_params=_cparams(("arbitrary",)))(dx3, dx3b, h1b, w_down, w_up_t, x2, g)


def _attn_bwd(x1, dx2, dx2b, qb, kb, vb, w_xo, w_xq, w_out, g):
    t = x1.shape[0]
    tm = min(TM, t)

    def body(x_ref, dx2_ref, dx2b_ref, q_ref, k_ref, v_ref, wx_ref, wq_ref, wo_ref, g_ref,
             dx1_ref, dx1b_ref, dy_ref, dq_ref, dkv_ref, dg_ref):
        @pl.when(pl.program_id(0) == 0)
        def _():
            dkv_ref[...] = jnp.zeros_like(dkv_ref)
            dg_ref[...] = jnp.zeros_like(dg_ref)

        datt = _dot_nt(dx2b_ref[...], wx_ref[...]).astype(BF16)
        for h in range(NH):
            hs = slice(h * XD, (h + 1) * XD)
            q_h, k_h, v_h, da_h = q_ref[:, hs], k_ref[:, hs], v_ref[:, hs], datt[:, hs]
            p = _softmax_head(q_h, k_h)
            dp = _dot_nt(da_h, v_h)
            ds = (p * (dp - jnp.sum(dp * p, axis=-1, keepdims=True)) * (1.0 / 16.0)).astype(BF16)
            dq_ref[:, hs] = _dot(ds, k_h).astype(BF16)
            dkv_ref[:, hs] += _dot_tn(ds, q_h)
            dkv_ref[:, D + h * XD:D + (h + 1) * XD] += _dot_tn(p, da_h)
        dxn = _dot_nt(dq_ref[...], wq_ref[...])
        x = x_ref[...]
        r = lax.rsqrt(jnp.mean(x * x, axis=-1, keepdims=True) + EPS)
        dx, dg = _rms_bwd(x, r, g_ref[...], dxn)
        dx1 = dx2_ref[...] + dx
        dx1_ref[...] = dx1
        dx1b = dx1.astype(BF16)
        dx1b_ref[...] = dx1b
        dy_ref[...] = _dot_nt(dx1b, wo_ref[...])
        _acc_rows(dg_ref, dg)

    tok = lambda i: (i, 0)
    full = lambda i: (0, 0)
    return pl.pallas_call(
        body, name="attn_bwd", grid=(t // tm,),
        in_specs=[pl.BlockSpec((tm, D), tok), pl.BlockSpec((tm, D), tok), pl.BlockSpec((tm, D), tok),
                  pl.BlockSpec((tm, D), tok), pl.BlockSpec((NMEM, D), full), pl.BlockSpec((NMEM, D), full),
                  pl.BlockSpec((D, D), full), pl.BlockSpec((D, D), full), pl.BlockSpec((D, D), full),
                  pl.BlockSpec((1, D), full)],
        out_specs=[pl.BlockSpec((tm, D), tok), pl.BlockSpec((tm, D), tok), pl.BlockSpec((tm, D), tok),
                   pl.BlockSpec((tm, D), tok), pl.BlockSpec((NMEM, 2 * D), full), pl.BlockSpec((8, D), full)],
        out_shape=[jax.ShapeDtypeStruct((t, D), F32), jax.ShapeDtypeStruct((t, D), BF16),
                   jax.ShapeDtypeStruct((t, D), F32), jax.ShapeDtypeStruct((t, D), BF16),
                   jax.ShapeDtypeStruct((NMEM, 2 * D), F32), jax.ShapeDtypeStruct((8, D), F32)],
        compiler_params=_cparams(("arbitrary",)))(x1, dx2, dx2b, qb, kb, vb, w_xo, w_xq, w_out, g)


def _kv_bwd(dkv, memn, mem, g, w):
    def body(dkv_ref, mn_ref, m_ref, g_ref, w_ref, dw_ref, dg_ref):
        dkvb = dkv_ref[...].astype(BF16)
        dmn = jnp.zeros((NMEM, D), F32)
        for j in range(NDEV):
            cols = slice(j * XKV_SHARD, (j + 1) * XKV_SHARD)
            dw_ref[j] = _dot_tn(mn_ref[...], dkvb[:, cols])
            dmn += _dot_nt(dkvb[:, cols], w_ref[j])
        m = m_ref[...]
        r = lax.rsqrt(jnp.mean(m * m, axis=-1, keepdims=True) + EPS)
        dg_ref[...] = jnp.broadcast_to(jnp.sum(dmn * m * r, axis=0, keepdims=True), dg_ref.shape)

    return pl.pallas_call(
        body, name="kv_bwd",
        out_shape=[jax.ShapeDtypeStruct((NDEV, D, XKV_SHARD), F32), jax.ShapeDtypeStruct((8, D), F32)],
        compiler_params=pltpu.CompilerParams(vmem_limit_bytes=VMEM_LIMIT))(dkv, memn, mem, g, w)


def _inproj_bwd(dz, w_t, x, dx1, g, riders=()):
    t = x.shape[0]
    tm = min(TM, t)

    def body(dz_ref, w_ref, x_ref, dx1_ref, g_ref, gx_ref, dg_ref):
        @pl.when(pl.program_id(0) == 0)
        def _():
            dg_ref[...] = jnp.zeros_like(dg_ref)

        dh = jnp.dot(dz_ref[...], w_ref[...], preferred_element_type=F32)
        x = x_ref[...]
        r = lax.rsqrt(jnp.mean(x * x, axis=-1, keepdims=True) + EPS)
        dx, dg = _rms_bwd(x, r, g_ref[...], dh)
        gx_ref[...] = dx1_ref[...] + dx
        _acc_rows(dg_ref, dg)

    tok = lambda i: (i, 0)
    full = lambda i: (0, 0)
    return _call(
        body, "inproj_bwd", (t // tm,),
        [pl.BlockSpec((tm, ZC), tok), pl.BlockSpec((ZC, D), full), pl.BlockSpec((tm, D), tok),
         pl.BlockSpec((tm, D), tok), pl.BlockSpec((1, D), full)],
        [pl.BlockSpec((tm, D), tok), pl.BlockSpec((8, D), full)],
        [jax.ShapeDtypeStruct((t, D), F32), jax.ShapeDtypeStruct((8, D), F32)], [], (dz, w_t, x, dx1, g), riders)


def _matmul_tn(a, b, name, rows=None, riders=()):
    t, k = a.shape
    n = b.shape[1]
    tk, tn = [1024 if size % 1024 == 0 else 640 for size in (k, n)]
    tt = min(TT, t)
    rows = rows or k

    def body(a_ref, b_ref, o_ref):
        @pl.when(pl.program_id(2) == 0)
        def _():
            o_ref[...] = jnp.zeros_like(o_ref)

        o_ref[...] += _dot_tn(a_ref[...], b_ref[...])

    return _call(
        body, name, (k // tk, n // tn, t // tt),
        [pl.BlockSpec((tt, tk), lambda i, j, s: (s, i)), pl.BlockSpec((tt, tn), lambda i, j, s: (s, j))],
        [pl.BlockSpec((tk, tn), lambda i, j, s: (i, j))], [jax.ShapeDtypeStruct((rows, n), F32)], [], (a, b), riders)


def _lane_head(shape, dim, shift):
    return _iota(shape, dim) >> shift


def _gla_recompute(q_raw, k, lr, wpad, bias, rev, tb):
    pre = _dot(lr, wpad) + bias
    la = (jnp.minimum(pre, 0.0) - jnp.log(1.0 + jnp.exp(-jnp.abs(pre)))) * (1.0 / 16.0)
    r, c = _iota((tb, tb), 0), _iota((tb, tb), 1)
    tri = (c >= r) if rev else (c <= r)
    cum = jnp.where(((r >> 6) == (c >> 6)) & tri, 1.0, 0.0).astype(BF16)
    b = _dot_exact_lhs(cum, la, 3)
    e, ei = jnp.exp(b), jnp.exp(-b)
    qt = (q_raw * 0.125) * e
    kt = k * ei
    return pre, b, e, ei, qt, kt


def _stack_heads(x, shift):
    head = _lane_head(x.shape, 1, shift)
    return jnp.concatenate([jnp.where(head == h, x, 0.0) for h in range(NH)], axis=0).astype(BF16)


def _fold_heads(x, shift):
    head = _lane_head((CH, x.shape[1]), 1, shift)
    return functools.reduce(lambda a, b: a + b,
                            [jnp.where(head == h, x[h * CH:(h + 1) * CH], 0.0) for h in range(NH)])


def _wide_mask(rev):
    r, s = _iota((CH, NH * CH), 0), _iota((CH, NH * CH), 1) & (CH - 1)
    return (s >= r) if rev else (s <= r)


def _state_mask():
    return (_iota((GV, GK), 0) >> 7) == (_iota((GV, GK), 1) >> 6)


def _state_expand(sd):
    head = _lane_head(sd.shape, 1, 6)
    return jnp.concatenate([jnp.where(head == h, sd, 0.0) for h in range(NH)], axis=0)


def _conv_parts(cb, cc, cu, ccp, cup, ccn, cun, cw_ref, first, last, tb):
    h = cc * cu
    hp = jnp.where(first, 0.0, ccp * cup)
    hn = jnp.where(last, 0.0, ccn * cun)
    rows = _iota(h.shape, 0)
    h_m1 = jnp.where(rows == 0, hp, pltpu.roll(h, 1, 0))
    h_p1 = jnp.where(rows == tb - 1, hn, pltpu.roll(h, tb - 1, 0))
    conv = cw_ref[pl.ds(0, 1), :] * h_m1 + cw_ref[pl.ds(1, 1), :] * h + cw_ref[pl.ds(2, 1), :] * h_p1
    return h, h_m1, h_p1, conv


def _group_ones():
    return jnp.where((_iota((CW, CW), 0) >> 6) == (_iota((CW, CW), 1) >> 6), 1.0, 0.0).astype(BF16)


def _head_norm(o):
    ons, rs = [], []
    for h in range(NH):
        slab = o[:, h * 128:(h + 1) * 128]
        r = lax.rsqrt(jnp.mean(slab * slab, axis=-1, keepdims=True) + EPS)
        ons.append(slab * r)
        rs.append(jnp.broadcast_to(r, slab.shape))
    return jnp.concatenate(ons, axis=1), jnp.concatenate(rs, axis=1)


def _zspec(tb, width, blk, jmap):
    return pl.BlockSpec((tb, width), lambda i: (jmap(i), blk))


def _halo_specs(tb, nblk, t, blk, jmap):
    prev = pl.BlockSpec((8, CW), lambda i: (jnp.maximum(jmap(i) * (tb // 8) - 1, 0), blk))
    nxt = pl.BlockSpec((8, CW), lambda i: (jnp.minimum((jmap(i) + 1) * (tb // 8), t // 8 - 1), blk))
    return prev, nxt


def _gla_fwd_block(q_ref, k_ref, v_ref, lr_ref, w_ref, bias_ref, o_ref, sd_ref, st, b_scr, rev, tb):
    nb = tb // CH
    _, b, _, _, qt, kt = _gla_recompute(q_ref[...], k_ref[...], lr_ref[...], w_ref[...], bias_ref[...], rev, tb)
    v = v_ref[...]
    b_scr[...] = b
    yield
    maskw, bd = _wide_mask(rev), _state_mask()
    order = list(reversed(range(nb))) if rev else list(range(nb))
    rows = [slice(c * CH, (c + 1) * CH) for c in range(nb)]
    state = st[...]
    for c in order:
        gdec = jnp.exp(b_scr[pl.ds(c * CH + (0 if rev else CH - 1), 1), :])
        sd_ref[c] = state[0:128] + state[128:256] + state[256:384] + state[384:512]
        a = jnp.where(maskw, _dot_nt(qt[rows[c]], _stack_heads(kt[rows[c]], 6)), 0.0)
        o_ref[pl.ds(c * CH, CH), :] = _dot(a, _stack_heads(v[rows[c]], 7)) + _dot_nt(qt[rows[c]], state)
        state = state * gdec + jnp.where(bd, _dot_tn(v[rows[c]], kt[rows[c]] * gdec), 0.0)
        yield
    st[...] = state
    yield


def _gla_fwd(z, waf_pad, b_af, wab_pad, b_ab, riders=()):
    t = z.shape[0]
    tb = min(TB, t)
    nblk, nb = t // tb, tb // CH
    jmaps = (lambda i: i, lambda i: nblk - 1 - i)

    def body(qf, kf, vf, lrf, qr, kr, vr, lrr, wf, bf, wr, br, of_ref, sdf_ref, or_ref, sdr_ref,
             st_f, st_r, b_f, b_r):
        @pl.when(pl.program_id(0) == 0)
        def _():
            st_f[...] = jnp.zeros_like(st_f)
            st_r[...] = jnp.zeros_like(st_r)

        for _ in zip(_gla_fwd_block(qf, kf, vf, lrf, wf, bf, of_ref, sdf_ref, st_f, b_f, False, tb),
                     _gla_fwd_block(qr, kr, vr, lrr, wr, br, or_ref, sdr_ref, st_r, b_r, True, tb)):
            pass

    full = lambda i: (0, 0)
    zspecs = [s for jm in jmaps for s in (_zspec(tb, GK, ZB_Q, jm), _zspec(tb, GK, ZB_K, jm),
                                         _zspec(tb, GV, ZB_V, jm), _zspec(tb, 128, ZB_LR, jm))]
    wspecs = [pl.BlockSpec((128, GK), full), pl.BlockSpec((1, GK), full)] * 2
    out_specs = [s for jm in jmaps for s in (pl.BlockSpec((tb, GV), lambda i, jm=jm: (jm(i), 0)),
                                             pl.BlockSpec((nb, 128, GK), lambda i, jm=jm: (jm(i), 0, 0)))]
    out_shape = [jax.ShapeDtypeStruct((t, GV), F32), jax.ShapeDtypeStruct((t // CH, 128, GK), F32)] * 2
    scratch = [pltpu.VMEM((GV, GK), F32), pltpu.VMEM((GV, GK), F32), pltpu.VMEM((tb, GK), F32),
               pltpu.VMEM((tb, GK), F32)]
    return _call(body, "gla_fwd", (nblk,), zspecs + wspecs, out_specs, out_shape, scratch,
                 [z] * 8 + [waf_pad, b_af, wab_pad, b_ab], riders)


def _mixer_finish(z, o_f, o_b, conv_w, conv_norm, gla_norm4):
    t = z.shape[0]
    tb = min(TM, t)
    nblk = t // tb
    jmap = lambda i: i

    def body(q_ref, k_ref, v_ref, g_ref, cb_ref, cc_ref, cu_ref, ccp_ref, ccn_ref, cup_ref, cun_ref, of_ref, ob_ref,
             cw_ref, cn_ref, gn_ref, y_ref, opre_ref):
        j = pl.program_id(0)
        v = v_ref[...]
        hsel = jnp.where((_iota((GK, GV), 0) >> 6) == (_iota((GK, GV), 1) >> 7), 1.0, 0.0).astype(BF16)
        sb = _dot_exact_rhs((q_ref[...] * 0.125) * k_ref[...], hsel, 2)
        o_pre = of_ref[...] + ob_ref[...] - sb * v
        opre_ref[...] = o_pre
        on, _ = _head_norm(o_pre)
        g = g_ref[...]
        y_ref[:, CW:] = (on * gn_ref[...] * (g * _sigmoid(g))).astype(BF16)
        cb = cb_ref[...]
        _, _, _, conv = _conv_parts(cb, cc_ref[...], cu_ref[...], ccp_ref[pl.ds(7, 1), :], cup_ref[pl.ds(7, 1), :],
                                    ccn_ref[pl.ds(0, 1), :], cun_ref[pl.ds(0, 1), :], cw_ref, j == 0,
                                    j == nblk - 1, tb)
        yc = cb * conv
        gm = _dot_exact_rhs(yc * yc, _group_ones(), 2) * (1.0 / 64.0)
        y_ref[:, :CW] = (yc * lax.rsqrt(gm + EPS) * cn_ref[...]).astype(BF16)

    full = lambda i: (0, 0)
    tokv = pl.BlockSpec((tb, GV), lambda i: (i, 0))
    ccp, ccn = _halo_specs(tb, nblk, t, ZB_CC, jmap)
    cup, cun = _halo_specs(tb, nblk, t, ZB_CU, jmap)
    in_specs = [_zspec(tb, GK, ZB_Q, jmap), _zspec(tb, GK, ZB_K, jmap), _zspec(tb, GV, ZB_V, jmap),
                _zspec(tb, GV, ZB_G, jmap), _zspec(tb, CW, ZB_CB, jmap), _zspec(tb, CW, ZB_CC, jmap),
                _zspec(tb, CW, ZB_CU, jmap), ccp, ccn, cup, cun, tokv, tokv,
                pl.BlockSpec((3, CW), full), pl.BlockSpec((1, CW), full), pl.BlockSpec((1, GV), full)]
    return pl.pallas_call(
        body, name="mixer_finish", grid=(nblk,), in_specs=in_specs,
        out_specs=[pl.BlockSpec((tb, D), lambda i: (i, 0)), tokv],
        out_shape=[jax.ShapeDtypeStruct((t, D), BF16), jax.ShapeDtypeStruct((t, GV), F32)],
        compiler_params=_cparams(("arbitrary",)))(
            z, z, z, z, z, z, z, z, z, z, z, o_f, o_b, conv_w, conv_norm, gla_norm4)


def _gla_bwd_chunks(do_ref, sd_ref, dst, b_scr, db_scr, dq_ref, dk_ref, dv_ref, qt, kt, e, ei, v, rev, nb):
    maskw, bd = _wide_mask(rev), _state_mask()
    for c in (range(nb) if rev else reversed(range(nb))):
        sl = slice(c * CH, (c + 1) * CH)
        grow = c * CH + (0 if rev else CH - 1)
        gdec = jnp.exp(b_scr[pl.ds(grow, 1), :])
        qt_c, kt_c, v_c, do_c = qt[sl], kt[sl], v[sl], do_ref[pl.ds(c * CH, CH), :]
        s_in = _state_expand(sd_ref[c])
        ds_out = dst[...]
        kbd, vbd = _stack_heads(kt_c, 6), _stack_heads(v_c, 7)
        a = jnp.where(maskw, _dot_nt(qt_c, kbd), 0.0)
        da = jnp.where(maskw, _dot_nt(do_c, vbd), 0.0)
        dv_ref[pl.ds(c * CH, CH), :] = _fold_heads(_dot_tn(a, do_c), 7) + _dot_nt(kt_c * gdec, ds_out)
        dqt = _dot(da, kbd) + _dot(do_c, s_in)
        dkh = _dot(v_c, ds_out)
        dkt = _fold_heads(_dot_tn(da, qt_c), 6) + dkh * gdec
        dg = jnp.sum(ds_out * s_in, axis=0, keepdims=True) + jnp.sum(kt_c * dkh, axis=0, keepdims=True)
        db_scr[pl.ds(c * CH, CH), :] = dqt * qt_c - dkt * kt_c
        db_scr[pl.ds(grow, 1), :] += dg * gdec
        dq_ref[pl.ds(c * CH, CH), :] = dqt * e[sl] * 0.125
        dk_ref[pl.ds(c * CH, CH), :] = dkt * ei[sl]
        dst[...] = ds_out * gdec + jnp.where(bd, _dot_tn(do_c, qt_c), 0.0)
        yield


def _gate_bwd(db, pre, lr, wpad, rev, tb):
    r, c = _iota((tb, tb), 0), _iota((tb, tb), 1)
    tri = (c <= r) if rev else (c >= r)
    cum_t = jnp.where(((r >> 6) == (c >> 6)) & tri, 1.0, 0.0).astype(BF16)
    dla = _dot_exact_lhs(cum_t, db, 2)
    dpre = dla * (1.0 / 16.0) / (1.0 + jnp.exp(pre))
    return dpre, _dot_nt(dpre, wpad), _dot_tn(lr, dpre)


def _gla_bwd_first(z, dy, o_pre, sd, wpad, bias, conv_w, conv_norm, gla_norm4, riders=()):
    t = z.shape[0]
    tb = min(TB_BWD, t)
    nblk, nb = t // tb, tb // CH
    jmap = lambda i: nblk - 1 - i

    def body(q_ref, k_ref, v_ref, lr_ref, g_ref, cb_ref, cc_ref, cu_ref, ccp_ref, ccn_ref, cup_ref, cun_ref,
             dy_ref, opre_ref, sd_ref, w_ref, bias_ref, cw_ref, cn_ref, gn_ref,
             do_ref, dq_ref, dk_ref, dv_ref, dlr_ref, dzg_ref, dzcb_ref, dconv_ref,
             dw_ref, dbias_ref, dcw_ref, dcn_ref, dgn_ref, dst, b_scr, db_scr):
        i = pl.program_id(0)
        j = jmap(i)

        @pl.when(i == 0)
        def _():
            dst[...] = jnp.zeros_like(dst)
            for ref in (dw_ref, dbias_ref, dcw_ref, dcn_ref, dgn_ref):
                ref[...] = jnp.zeros_like(ref)

        dyg = dy_ref[:, CW:]
        g = g_ref[...]
        sig = _sigmoid(g)
        on, rr = _head_norm(opre_ref[...])
        gn = gn_ref[...]
        dzg_ref[...] = (dyg * on * gn * (sig * (1.0 + g * (1.0 - sig)))).astype(BF16)
        don = dyg * (g * sig)
        _acc_rows(dgn_ref, jnp.sum(don * on, axis=0, keepdims=True))
        u = don * gn
        uo = u * on
        mean_uo = jnp.concatenate(
            [jnp.broadcast_to(jnp.mean(uo[:, h * 128:(h + 1) * 128], axis=-1, keepdims=True), (tb, 128))
             for h in range(NH)], axis=1)
        do_ref[...] = rr * (u - on * mean_uo)

        def conv_branch():
            cb = cb_ref[...]
            h, h_m1, h_p1, conv = _conv_parts(cb, cc_ref[...], cu_ref[...], ccp_ref[pl.ds(7, 1), :],
                                              cup_ref[pl.ds(7, 1), :], ccn_ref[pl.ds(0, 1), :],
                                              cun_ref[pl.ds(0, 1), :], cw_ref, j == 0, j == nblk - 1, tb)
            yc = cb * conv
            yield
            ones = _group_ones()
            rc = lax.rsqrt(_dot_exact_rhs(yc * yc, ones, 2) * (1.0 / 64.0) + EPS)
            ycr = yc * rc
            yield
            dyn = dy_ref[:, :CW]
            _acc_rows(dcn_ref, jnp.sum(dyn * ycr, axis=0, keepdims=True))
            uc = dyn * cn_ref[...]
            yield
            dyc = rc * (uc - ycr * (_dot_exact_rhs(uc * ycr, ones, 2) * (1.0 / 64.0)))
            dzcb_ref[...] = (dyc * conv).astype(BF16)
            yield
            dconv = dyc * cb
            dconv_ref[...] = dconv
            yield
            dcw_ref[pl.ds(0, 1), :] += jnp.sum(dconv * h_m1, axis=0, keepdims=True)
            dcw_ref[pl.ds(1, 1), :] += jnp.sum(dconv * h, axis=0, keepdims=True)
            dcw_ref[pl.ds(2, 1), :] += jnp.sum(dconv * h_p1, axis=0, keepdims=True)
            yield

        lr, wp = lr_ref[...], w_ref[...]
        pre, b, e, ei, qt, kt = _gla_recompute(q_ref[...], k_ref[...], lr, wp, bias_ref[...], False, tb)
        b_scr[...] = b
        for _ in itertools.zip_longest(
                _gla_bwd_chunks(do_ref, sd_ref, dst, b_scr, db_scr, dq_ref, dk_ref, dv_ref, qt, kt, e, ei, v_ref[...],
                                False, nb), conv_branch()):
            pass
        dpre, dlr, dw = _gate_bwd(db_scr[...], pre, lr, wp, False, tb)
        dlr_ref[...] = dlr
        dw_ref[...] += dw
        _acc_rows(dbias_ref, jnp.sum(dpre, axis=0, keepdims=True))

    full = lambda i: (0, 0)
    tokv = pl.BlockSpec((tb, GV), lambda i: (jmap(i), 0))
    tokk = pl.BlockSpec((tb, GK), lambda i: (jmap(i), 0))
    ccp, ccn = _halo_specs(tb, nblk, t, ZB_CC, jmap)
    cup, cun = _halo_specs(tb, nblk, t, ZB_CU, jmap)
    in_specs = [_zspec(tb, GK, ZB_Q, jmap), _zspec(tb, GK, ZB_K, jmap), _zspec(tb, GV, ZB_V, jmap),
                _zspec(tb, 128, ZB_LR, jmap), _zspec(tb, GV, ZB_G, jmap), _zspec(tb, CW, ZB_CB, jmap),
                _zspec(tb, CW, ZB_CC, jmap), _zspec(tb, CW, ZB_CU, jmap), ccp, ccn, cup, cun,
                pl.BlockSpec((tb, D), lambda i: (jmap(i), 0)), tokv,
                pl.BlockSpec((nb, 128, GK), lambda i: (jmap(i), 0, 0)), pl.BlockSpec((128, GK), full),
                pl.BlockSpec((1, GK), full), pl.BlockSpec((3, CW), full), pl.BlockSpec((1, CW), full),
                pl.BlockSpec((1, GV), full)]
    out_specs = [tokv, tokk, tokk, tokv, pl.BlockSpec((tb, 128), lambda i: (jmap(i), 0)), tokv, tokv, tokv,
                 pl.BlockSpec((128, GK), full), pl.BlockSpec((8, GK), full), pl.BlockSpec((8, CW), full),
                 pl.BlockSpec((8, CW), full), pl.BlockSpec((8, GV), full)]
    out_shape = [jax.ShapeDtypeStruct((t, GV), F32), jax.ShapeDtypeStruct((t, GK), F32),
                 jax.ShapeDtypeStruct((t, GK), F32), jax.ShapeDtypeStruct((t, GV), F32),
                 jax.ShapeDtypeStruct((t, 128), F32), jax.ShapeDtypeStruct((t, GV), BF16),
                 jax.ShapeDtypeStruct((t, CW), BF16), jax.ShapeDtypeStruct((t, CW), F32),
                 jax.ShapeDtypeStruct((128, GK), F32), jax.ShapeDtypeStruct((8, GK), F32),
                 jax.ShapeDtypeStruct((8, CW), F32), jax.ShapeDtypeStruct((8, CW), F32),
                 jax.ShapeDtypeStruct((8, GV), F32)]
    return _call(
        body, "gla_bwd_first", (nblk,), in_specs, out_specs, out_shape,
        [pltpu.VMEM((GV, GK), F32), pltpu.VMEM((tb, GK), F32), pltpu.VMEM((tb, GK), F32)],
        (z, z, z, z, z, z, z, z, z, z, z, z, dy, o_pre, sd, wpad, bias, conv_w, conv_norm, gla_norm4), riders)


def _gla_bwd_second(z, do, sd, wpad, bias, dqa, dka, dva, dlra, dzg, dzcb, dconv, conv_w, riders=()):
    t = z.shape[0]
    tb = min(TB_BWD, t)
    nblk, nb = t // tb, tb // CH
    jmap = lambda i: i

    def body(q_ref, k_ref, v_ref, lr_ref, cc_ref, cu_ref, do_ref, sd_ref, w_ref, bias_ref, dqa_ref, dka_ref,
             dva_ref, dlra_ref, dzg_ref, dzcb_ref, dc_ref, dcp_ref, dcn_ref, cw_ref,
             dz_ref, dw_ref, dbias_ref, dst, b_scr, db_scr, dq_scr, dk_scr, dv_scr, sb_scr, dsk_scr):
        i = pl.program_id(0)

        @pl.when(i == 0)
        def _():
            dst[...] = jnp.zeros_like(dst)
            dw_ref[...] = jnp.zeros_like(dw_ref)
            dbias_ref[...] = jnp.zeros_like(dbias_ref)

        q_raw, k, v, lr, wp = q_ref[...], k_ref[...], v_ref[...], lr_ref[...], w_ref[...]
        pre, b, e, ei, qt, kt = _gla_recompute(q_raw, k, lr, wp, bias_ref[...], True, tb)
        b_scr[...] = b

        def token_local():
            dc = dc_ref[...]
            rows = _iota(dc.shape, 0)
            dprev = jnp.where(i == 0, 0.0, dcp_ref[pl.ds(7, 1), :])
            dnext = jnp.where(i == nblk - 1, 0.0, dcn_ref[pl.ds(0, 1), :])
            dc_m1 = jnp.where(rows == 0, dprev, pltpu.roll(dc, 1, 0))
            dc_p1 = jnp.where(rows == tb - 1, dnext, pltpu.roll(dc, tb - 1, 0))
            yield
            dh = cw_ref[pl.ds(0, 1), :] * dc_p1 + cw_ref[pl.ds(1, 1), :] * dc + cw_ref[pl.ds(2, 1), :] * dc_m1
            dz_ref[:, 0:512] = dzcb_ref[...]
            yield
            dz_ref[:, 512:1024] = (dh * cu_ref[...]).astype(BF16)
            dz_ref[:, 1024:1536] = (dh * cc_ref[...]).astype(BF16)
            dz_ref[:, 2560:3072] = dzg_ref[...]
            yield
            hsel = jnp.where((_iota((GK, GV), 0) >> 6) == (_iota((GK, GV), 1) >> 7), 1.0, 0.0).astype(BF16)
            sb_scr[...] = _dot_exact_rhs((q_raw * 0.125) * k, hsel, 2)
            yield
            hsel_t = jnp.where((_iota((GV, GK), 0) >> 7) == (_iota((GV, GK), 1) >> 6), 1.0, 0.0).astype(BF16)
            dsk_scr[...] = _dot_exact_rhs(do_ref[...] * v, hsel_t, 2)
            yield

        for _ in itertools.zip_longest(
                _gla_bwd_chunks(do_ref, sd_ref, dst, b_scr, db_scr, dq_scr, dk_scr, dv_scr, qt, kt, e, ei, v, True, nb),
                token_local()):
            pass
        dpre, dlr, dw = _gate_bwd(db_scr[...], pre, lr, wp, True, tb)
        dw_ref[...] += dw
        _acc_rows(dbias_ref, jnp.sum(dpre, axis=0, keepdims=True))
        dsk = dsk_scr[...]
        dz_ref[:, 1536:1792] = (dqa_ref[...] + dq_scr[...] - dsk * k * 0.125).astype(BF16)
        dz_ref[:, 1792:2048] = (dka_ref[...] + dk_scr[...] - dsk * (q_raw * 0.125)).astype(BF16)
        dz_ref[:, 2048:2560] = (dva_ref[...] + dv_scr[...] - sb_scr[...] * do_ref[...]).astype(BF16)
        dz_ref[:, 3072:3200] = (dlra_ref[...] + dlr).astype(BF16)

    full = lambda i: (0, 0)
    tokv = pl.BlockSpec((tb, GV), lambda i: (i, 0))
    tokk = pl.BlockSpec((tb, GK), lambda i: (i, 0))
    dcp = pl.BlockSpec((8, CW), lambda i: (jnp.maximum(i * (tb // 8) - 1, 0), 0))
    dcn = pl.BlockSpec((8, CW), lambda i: (jnp.minimum((i + 1) * (tb // 8), t // 8 - 1), 0))
    in_specs = [_zspec(tb, GK, ZB_Q, jmap), _zspec(tb, GK, ZB_K, jmap), _zspec(tb, GV, ZB_V, jmap),
                _zspec(tb, 128, ZB_LR, jmap), _zspec(tb, CW, ZB_CC, jmap), _zspec(tb, CW, ZB_CU, jmap), tokv,
                pl.BlockSpec((nb, 128, GK), lambda i: (i, 0, 0)), pl.BlockSpec((128, GK), full),
                pl.BlockSpec((1, GK), full), tokk, tokk, tokv, pl.BlockSpec((tb, 128), lambda i: (i, 0)), tokv, tokv,
                tokv, dcp, dcn, pl.BlockSpec((3, CW), full)]
    out_specs = [pl.BlockSpec((tb, ZC), lambda i: (i, 0)), pl.BlockSpec((128, GK), full), pl.BlockSpec((8, GK), full)]
    out_shape = [jax.ShapeDtypeStruct((t, ZC), BF16), jax.ShapeDtypeStruct((128, GK), F32),
                 jax.ShapeDtypeStruct((8, GK), F32)]
    return _call(
        body, "gla_bwd_second", (nblk,), in_specs, out_specs, out_shape,
        [pltpu.VMEM((GV, GK), F32), pltpu.VMEM((tb, GK), F32), pltpu.VMEM((tb, GK), F32),
         pltpu.VMEM((tb, GK), F32), pltpu.VMEM((tb, GK), F32), pltpu.VMEM((tb, GV), F32),
         pltpu.VMEM((tb, GV), F32), pltpu.VMEM((tb, GK), F32)],
        (z, z, z, z, z, z, do, sd, wpad, bias, dqa, dka, dva, dlra, dzg, dzcb, dconv, dconv, dconv, conv_w), riders)


def _step(x, mem, target, shard, small_pack, vec, place):
    own, from_chips = {}, {}

    def pair_sums(names, g4, from_sibling):
        pbs = []
        for n, g, s in zip(names, g4, from_sibling):
            pb, own[n] = _rs_pair_sum(place, g, s, "pair_sum_" + n)
            pbs.append(pb)
        return pbs

    def by_dest(g, n):
        return g.reshape((4, 2) + shard[n].shape)

    w_in, small_all = _exchange(_gather_rider([shard["w_in"], small_pack]), "gather_w_in")
    w_in = jnp.pad(w_in.reshape(ZW, D), ((0, ZC - ZW), (0, 0)))
    small_all = small_all.reshape(NDEV, -1)
    p, off = {}, 0
    for n, (r, c) in SMALL_SHARDED.items():
        p[n] = small_all[:, off:off + r * c].reshape(NDEV, r, c).transpose(1, 0, 2).reshape(r, NDEV * c)
        off += r * c
    zeros_lr = jnp.zeros((128 - LR, GK), BF16)
    waf_pad = jnp.concatenate([p["w_af"].astype(BF16), zeros_lr], axis=0)
    wab_pad = jnp.concatenate([jnp.zeros((LR, GK), BF16), p["w_ab"].astype(BF16), zeros_lr[:128 - 2 * LR]], axis=0)
    gla_norm4 = jnp.tile(vec["gla_norm"], (1, NH))

    z, hb, w_out, w_xq, w_xo, w_xkv = _inproj(
        x, vec["mix_norm"], w_in, [_gather_rider([shard[n] for n in ("w_out", "w_xq", "w_xo", "w_xkv")])])
    w_out, w_xq, w_xo = [a.reshape(D, D) for a in (w_out, w_xq, w_xo)]
    o_f, sd_f, o_b, sd_b, w_up_t, w_down = _gla_fwd(
        z, waf_pad, vec["b_af"], wab_pad, vec["b_ab"], [_gather_rider([shard["w_up"], shard["w_down"]])])
    w_up_t, w_down = w_up_t.reshape(FF, D), w_down.reshape(FF, D)
    yb, o_pre = _mixer_finish(z, o_f, o_b, p["conv_w"], vec["conv_norm"], gla_norm4)
    kv, memn = _kv_proj(mem, vec["mem_norm"], w_xkv)
    kb, vb = kv[:, :D].astype(BF16), kv[:, D:].astype(BF16)
    x1, x2, xn1, qb, attb = _attn_fwd(x, yb, w_out, vec["xa_norm"], w_xq, kb, vb, w_xo)
    h1b, xn2, dx3, dx3b, loss8, dfinal = _mlp_fwd(x2, vec["mlp_norm"], w_up_t, w_down, vec["final_norm"], target)

    ab, dh1b, dx2, dx2b, dmlp = _mlp_bwd(dx3, dx3b, h1b, w_down, w_up_t, x2, vec["mlp_norm"])
    g_mlp = [by_dest(_matmul_tn(ab, dx3b, "dw_down")[0], "w_down"),
             by_dest(_matmul_tn(dh1b, xn2, "dw_up")[0], "w_up")]
    dx1, dx1b, dy, dqb, dkv, dxa = _attn_bwd(x1, dx2, dx2b, qb, kb, vb, w_xo, w_xq, w_out, vec["xa_norm"])
    dw_xo, *s_mlp = _matmul_tn(attb, dx2b, "dw_xo", riders=[_sibling_rider(g_mlp)])
    pb_mlp = pair_sums(("w_down", "w_up"), g_mlp, s_mlp)
    dw_xkv, dmemn = _kv_bwd(dkv, memn, mem, vec["mem_norm"], w_xkv)
    att_names = ("w_xo", "w_xq", "w_out", "w_xkv")
    g_att = [by_dest(g, n) for g, n in zip(
        (dw_xo, _matmul_tn(xn1, dqb, "dw_xq")[0], _matmul_tn(yb, dx1b, "dw_out")[0], dw_xkv), att_names)]
    res = _gla_bwd_first(z, dy, o_pre, sd_f, waf_pad, vec["b_af"], p["conv_w"], vec["conv_norm"], gla_norm4,
                         riders=[_chips_rider(pb_mlp), _sibling_rider(g_att)])
    do, dqa, dka, dva, dlra, dzg, dzcb, dconv, dwaf, dbaf, dcw, dcn, dgn = res[:13]
    from_chips["w_down"], from_chips["w_up"] = res[13:15]
    pb_att = pair_sums(att_names, g_att, res[15:])
    dz, dwab, dbab, *c_att = _gla_bwd_second(z, do, sd_b, wab_pad, vec["b_ab"], dqa, dka, dva, dlra, dzg, dzcb, dconv,
                                             p["conv_w"], riders=[_chips_rider(pb_att)])
    from_chips.update(zip(att_names, c_att))
    g_in = [by_dest(_matmul_tn(dz, hb, "dw_in", rows=ZW)[0], "w_in")]
    pb_in = pair_sums(("w_in",), g_in, _exchange(_sibling_rider(g_in), "grads_to_sibling_w_in"))
    grad_x, dmix, from_chips["w_in"] = _inproj_bwd(dz, w_in, x, dx1, vec["mix_norm"], riders=[_chips_rider(pb_in)])

    small_grads = {
        "mix_norm": dmix[0:1], "conv_w": dcw[0:3], "conv_norm": dcn[0:1],
        "w_af": dwaf[0:LR], "b_af": dbaf[0:1], "w_ab": dwab[LR:2 * LR], "b_ab": dbab[0:1],
        "gla_norm": (dgn[0:1, 0:128] + dgn[0:1, 128:256]) + (dgn[0:1, 256:384] + dgn[0:1, 384:512]),
        "xa_norm": dxa[0:1], "mem_norm": dmemn[0:1], "mlp_norm": dmlp[0:1], "final_norm": dfinal[0:1],
    }
    return loss8[0:1, 0:1], grad_x, small_grads, own, from_chips


def _place():
    return lax.axis_index("x"), lax.axis_index("y"), lax.axis_index("c")


class _Rider:
    def __init__(self, arrays, out_shape, scratch, start, finish):
        self.arrays, self.out_shape, self.scratch, self.start, self.finish = arrays, out_shape, scratch, start, finish


def _gather_rider(blks):
    n = len(blks)

    def plan(in_refs, out_refs, sems):
        send_sems, recv_sems, local_sems = sems
        x, y, c = _place()
        me, sibling = (x, y, c), (x, y, 1 - c)
        chips = [(1 - x, y, c), (x, 1 - y, c), (1 - x, 1 - y, c)]

        def copy(a, k, block, to, own=False):
            px, py, pc = block
            dst = out_refs[a].at[4 * px + 2 * py + pc]
            return pltpu.make_async_remote_copy(
                src_ref=in_refs[a] if own else dst, dst_ref=dst, send_sem=send_sems.at[k, a],
                recv_sem=recv_sems.at[k, a], device_id=to, device_id_type=MESH)

        def local(a):
            return pltpu.make_async_copy(in_refs[a], out_refs[a].at[4 * x + 2 * y + c], local_sems.at[a])

        def own_sends(a):
            return [copy(a, 0, me, sibling, own=True)] + [copy(a, 1 + j, me, chip, own=True)
                                                          for j, chip in enumerate(chips)]

        return copy, local, own_sends, me, sibling, chips

    def start(in_refs, out_refs, sems):
        _, local, own_sends, _, _, _ = plan(in_refs, out_refs, sems)
        for a in range(n):
            local(a).start()
            for cp in own_sends(a):
                cp.start()

    def finish(in_refs, out_refs, sems):
        copy, local, own_sends, me, sibling, chips = plan(in_refs, out_refs, sems)
        for j, chip in enumerate(chips):
            for a in range(n):
                copy(a, 1 + j, chip, me).wait_recv()
                copy(a, 4 + j, chip, sibling).start()
        for a in range(n):
            copy(a, 0, sibling, me).wait_recv()
            for j, (px, py, pc) in enumerate(chips):
                copy(a, 4 + j, (px, py, 1 - pc), me).wait_recv()
            for cp in own_sends(a) + [copy(a, 4 + j, chip, sibling) for j, chip in enumerate(chips)]:
                cp.wait_send()
            local(a).wait()

    return _Rider(blks, [jax.ShapeDtypeStruct((NDEV,) + b.shape, b.dtype) for b in blks],
                  [pltpu.SemaphoreType.DMA((7, n)), pltpu.SemaphoreType.DMA((7, n)), pltpu.SemaphoreType.DMA((n,))],
                  start, finish)


def _sibling_rider(g4s):
    n = len(g4s)

    def copies(in_refs, out_refs, sems):
        send_sems, recv_sems = sems
        x, y, c = _place()
        return [pltpu.make_async_remote_copy(
            src_ref=in_refs[a].at[k, 1 - c], dst_ref=out_refs[a].at[k], send_sem=send_sems.at[k, a],
            recv_sem=recv_sems.at[k, a], device_id=(x, y, 1 - c), device_id_type=MESH)
            for a in range(n) for k in range(4)]

    def start(in_refs, out_refs, sems):
        for cp in copies(in_refs, out_refs, sems):
            cp.start()

    def finish(in_refs, out_refs, sems):
        for cp in copies(in_refs, out_refs, sems):
            cp.wait()

    return _Rider(g4s, [jax.ShapeDtypeStruct((4,) + g.shape[2:], g.dtype) for g in g4s],
                  [pltpu.SemaphoreType.DMA((4, n)), pltpu.SemaphoreType.DMA((4, n))], start, finish)


def _chips_rider(pbs):
    n = len(pbs)

    def copies(in_refs, out_refs, sems):
        send_sems, recv_sems = sems
        x, y, c = _place()
        peers = [(1 - x, y), (x, 1 - y), (1 - x, 1 - y)]
        return [pltpu.make_async_remote_copy(
            src_ref=in_refs[a].at[2 * px + py], dst_ref=out_refs[a].at[k], send_sem=send_sems.at[k, a],
            recv_sem=recv_sems.at[k, a], device_id=(px, py, c), device_id_type=MESH)
            for a in range(n) for k, (px, py) in enumerate(peers)]

    def start(in_refs, out_refs, sems):
        for cp in copies(in_refs, out_refs, sems):
            cp.start()

    def finish(in_refs, out_refs, sems):
        for cp in copies(in_refs, out_refs, sems):
            cp.wait()

    return _Rider(pbs, [jax.ShapeDtypeStruct((3,) + p.shape[1:], p.dtype) for p in pbs],
                  [pltpu.SemaphoreType.DMA((3, n)), pltpu.SemaphoreType.DMA((3, n))], start, finish)


def _exchange(rider, name):
    n_in, n_out = len(rider.arrays), len(rider.out_shape)

    def body(*refs):
        ins, outs, sems = refs[:n_in], refs[n_in:n_in + n_out], refs[n_in + n_out:]
        rider.start(ins, outs, sems)
        rider.finish(ins, outs, sems)

    hbm = pl.BlockSpec(memory_space=pltpu.HBM)
    return pl.pallas_call(body, name=name, out_shape=rider.out_shape, in_specs=[hbm] * n_in,
                          out_specs=[hbm] * n_out, scratch_shapes=rider.scratch)(*rider.arrays)


def _rs_pair_sum(place, g4, r1, name):
    rows, cols = g4.shape[2:]
    tr = min(rows, 512)

    def body(pl_ref, g_ref, r_ref, pb_ref, own_ref):
        s = g_ref[0, 0] + r_ref[0]
        pb_ref[0] = s.astype(BF16)

        @pl.when(pl.program_id(1) == pl_ref[0])
        def _():
            own_ref[...] = s

    grid_spec = pltpu.PrefetchScalarGridSpec(
        num_scalar_prefetch=1, grid=(rows // tr, 4),
        in_specs=[pl.BlockSpec((1, 1, tr, cols), lambda r, k, p: (k, p[1], r, 0)),
                  pl.BlockSpec((1, tr, cols), lambda r, k, p: (k, r, 0))],
        out_specs=[pl.BlockSpec((1, tr, cols), lambda r, k, p: (k, r, 0)),
                   pl.BlockSpec((tr, cols), lambda r, k, p: (r, 0))])
    return pl.pallas_call(
        body, name=name, grid_spec=grid_spec,
        out_shape=[jax.ShapeDtypeStruct((4, rows, cols), BF16), jax.ShapeDtypeStruct((rows, cols), F32)],
        compiler_params=_cparams(("arbitrary", "arbitrary")))(place, g4, r1)


def _small_all_reduce(vec):
    m_per = vec.shape[0]

    def body(x_ref, all_ref, sum_ref, send_sems, recv_sems, local_sem):
        x, y, c = _place()
        me, sibling = (x, y, c), (x, y, 1 - c)
        chips = [(1 - x, y), (x, 1 - y), (1 - x, 1 - y)]

        def rows(px, py, pc):
            return all_ref.at[4 * px + 2 * py + pc]

        def copy(k, block, to, src=None):
            return pltpu.make_async_remote_copy(
                src_ref=rows(*block) if src is None else src, dst_ref=rows(*block),
                send_sem=send_sems.at[k], recv_sem=recv_sems.at[k], device_id=to, device_id_type=MESH)

        mine = pltpu.make_async_copy(x_ref, rows(*me), local_sem)
        mine.start()
        first = [copy(0, me, sibling, src=x_ref)]
        first += [copy(1 + j, me, (*chip, c), src=x_ref) for j, chip in enumerate(chips)]
        for cp in first:
            cp.start()
        passed = [copy(4 + j, (*chip, c), sibling) for j, chip in enumerate(chips)]
        for j, chip in enumerate(chips):
            copy(1 + j, (*chip, c), me).wait_recv()
            passed[j].start()
        copy(0, sibling, me).wait_recv()
        for j, chip in enumerate(chips):
            copy(4 + j, (*chip, 1 - c), me).wait_recv()
        for cp in first + passed:
            cp.wait_send()
        mine.wait()
        total = all_ref[0]
        for d in range(1, NDEV):
            total = total + all_ref[d]
        sum_ref[...] = total

    return pl.pallas_call(
        body, name="small_all_reduce",
        out_shape=[jax.ShapeDtypeStruct((NDEV, m_per, 128), F32), jax.ShapeDtypeStruct((m_per, 128), F32)],
        in_specs=[pl.BlockSpec(memory_space=pltpu.VMEM)],
        out_specs=[pl.BlockSpec(memory_space=pltpu.VMEM), pl.BlockSpec(memory_space=pltpu.VMEM)],
        scratch_shapes=[pltpu.SemaphoreType.DMA((7,)), pltpu.SemaphoreType.DMA((7,)), pltpu.SemaphoreType.DMA],
    )(vec)[1]


def _adamw_math(w, g, m, v):
    m = ADAM_B1 * m + (1.0 - ADAM_B1) * g
    v = ADAM_B2 * v + (1.0 - ADAM_B2) * (g * g)
    m_hat = m / (1.0 - ADAM_B1 ** ADAM_STEP)
    v_hat = v / (1.0 - ADAM_B2 ** ADAM_STEP)
    delta = -ADAM_LR * (m_hat / (jnp.sqrt(v_hat) + ADAM_EPS) + ADAM_WD * w)
    return delta, m, v


def _adamw(w, m, v, own, r2, name):
    _, r, c = w.shape
    tr = 256 if r % 256 == 0 else r

    def body(w_ref, m_ref, v_ref, o_ref, r_ref, g_ref, d_ref, nm_ref, nv_ref):
        g = ((o_ref[...] + r_ref[0].astype(F32)) + r_ref[1].astype(F32)) + r_ref[2].astype(F32)
        g_ref[...] = g
        d_ref[...], nm_ref[...], nv_ref[...] = _adamw_math(w_ref[...], g, m_ref[...], v_ref[...])

    spec = pl.BlockSpec((None, tr, c), lambda i: (0, i, 0))
    return pl.pallas_call(
        body, name=name, grid=(r // tr,),
        in_specs=[spec, spec, spec, pl.BlockSpec((tr, c), lambda i: (i, 0)),
                  pl.BlockSpec((3, tr, c), lambda i: (0, i, 0))],
        out_specs=[spec] * 4, out_shape=[jax.ShapeDtypeStruct((1, r, c), F32)] * 4,
        compiler_params=_cparams(("arbitrary",)))(w, m, v, own, r2)


def _adamw_small(ws, gs, ms, vs):
    n = len(ws)

    def body(*refs):
        ins, outs = refs[:4 * n], refs[4 * n:]
        for i in range(n):
            d, m, v = _adamw_math(ins[i][...], ins[n + i][...], ins[2 * n + i][...], ins[3 * n + i][...])
            outs[i][...], outs[n + i][...], outs[2 * n + i][...] = d, m, v

    shapes = [jax.ShapeDtypeStruct(w.shape, F32) for w in ws]
    outs = pl.pallas_call(body, name="adamw_small", out_shape=shapes * 3)(*ws, *gs, *ms, *vs)
    return outs[:n], outs[n:2 * n], outs[2 * n:]


MATS = ("w_in", "w_out", "w_xq", "w_xo", "w_xkv", "w_up", "w_down")
SMALL = ("mix_norm", "conv_w", "conv_norm", "w_af", "b_af", "w_ab", "b_ab", "gla_norm", "xa_norm", "mem_norm",
         "mlp_norm", "final_norm")
WEIGHTS = ("mix_norm", "w_in", "conv_w", "conv_norm", "w_af", "b_af", "w_ab", "b_ab", "gla_norm", "w_out", "xa_norm",
           "mem_norm", "w_xq", "w_xkv", "w_xo", "mlp_norm", "w_up", "w_down", "final_norm")
SMALL_SHARDED = {"conv_w": (3, 64), "w_af": (LR, 32), "w_ab": (LR, 32)}
SMALL_PACK_ROWS = 16


def kernel(x, mem, mix_norm, w_in, conv_w, conv_norm, w_af, b_af, w_ab, b_ab, gla_norm, w_out, xa_norm, mem_norm, w_xq, w_xkv, w_xo, mlp_norm, w_up, w_down, final_norm, loss_target, m_mix_norm, m_w_in, m_conv_w, m_conv_norm, m_w_af, m_b_af, m_w_ab, m_b_ab, m_gla_norm, m_w_out, m_xa_norm, m_mem_norm, m_w_xq, m_w_xkv, m_w_xo, m_mlp_norm, m_w_up, m_w_down, m_final_norm, v_mix_norm, v_w_in, v_conv_w, v_conv_norm, v_w_af, v_b_af, v_w_ab, v_b_ab, v_gla_norm, v_w_out, v_xa_norm, v_mem_norm, v_w_xq, v_w_xkv, v_w_xo, v_mlp_norm, v_w_up, v_w_down, v_final_norm):
    w = dict(mix_norm=mix_norm, w_in=w_in, conv_w=conv_w, conv_norm=conv_norm, w_af=w_af, b_af=b_af, w_ab=w_ab,
             b_ab=b_ab, gla_norm=gla_norm, w_out=w_out, xa_norm=xa_norm, mem_norm=mem_norm, w_xq=w_xq, w_xkv=w_xkv,
             w_xo=w_xo, mlp_norm=mlp_norm, w_up=w_up, w_down=w_down, final_norm=final_norm)
    mom = dict(mix_norm=m_mix_norm, w_in=m_w_in, conv_w=m_conv_w, conv_norm=m_conv_norm, w_af=m_w_af, b_af=m_b_af,
               w_ab=m_w_ab, b_ab=m_b_ab, gla_norm=m_gla_norm, w_out=m_w_out, xa_norm=m_xa_norm, mem_norm=m_mem_norm,
               w_xq=m_w_xq, w_xkv=m_w_xkv, w_xo=m_w_xo, mlp_norm=m_mlp_norm, w_up=m_w_up, w_down=m_w_down,
               final_norm=m_final_norm)
    var = dict(mix_norm=v_mix_norm, w_in=v_w_in, conv_w=v_conv_w, conv_norm=v_conv_norm, w_af=v_w_af, b_af=v_b_af,
               w_ab=v_w_ab, b_ab=v_b_ab, gla_norm=v_gla_norm, w_out=v_w_out, xa_norm=v_xa_norm, mem_norm=v_mem_norm,
               w_xq=v_w_xq, w_xkv=v_w_xkv, w_xo=v_w_xo, mlp_norm=v_mlp_norm, w_up=v_w_up, w_down=v_w_down,
               final_norm=v_final_norm)
    xi, yi, ci = _place()
    me = 4 * xi + 2 * yi + ci
    two_d = lambda a: a.reshape(a.shape[-2:]) if a.ndim == 3 else a.reshape(1, a.shape[-1])

    small = jnp.concatenate([w[n].reshape(-1) for n in SMALL_SHARDED])
    small = jnp.pad(small, (0, SMALL_PACK_ROWS * 128 - small.shape[0])).reshape(SMALL_PACK_ROWS, 128)
    shard = {n: two_d(w[n]).astype(BF16) for n in MATS}
    for n in ("w_in", "w_up"):
        shard[n] = shard[n].T
    vec = {n: two_d(w[n]) for n in SMALL if n not in SMALL_SHARDED}
    place = jnp.stack([2 * xi + yi, ci]).astype(jnp.int32)
    loss_part, grad_x, grads, own, from_chips = _step(x[0], mem[0], loss_target[0], shard, small, vec, place)

    order = [n for n in SMALL if n not in SMALL_SHARDED] + list(SMALL_SHARDED)
    flat = jnp.concatenate([grads[n].reshape(-1) for n in order] + [loss_part.reshape(-1)])
    n_flat = flat.shape[0]
    tot = _small_all_reduce(jnp.pad(flat, (0, SMALL_ROWS * 128 - n_flat)).reshape(SMALL_ROWS, 128)).reshape(-1)
    gsmall, off = {}, 0
    for n in order:
        size = grads[n].size
        full = tot[off:off + size].reshape(grads[n].shape)
        off += size
        if n in SMALL_SHARDED:
            r, c = SMALL_SHARDED[n]
            full = lax.dynamic_slice_in_dim(full, me * c, c, axis=1)
        gsmall[n] = full
    loss = tot[off]

    out_g, out_d, out_m, out_v = {}, {}, {}, {}
    own["w_up"], from_chips["w_up"] = own["w_up"].T, from_chips["w_up"].transpose(0, 2, 1)
    for n in MATS:
        if n == "w_in":
            res = _adamw(*[a.transpose(0, 2, 1) for a in (w[n], mom[n], var[n])], own[n], from_chips[n], "adamw_" + n)
            res = [a.transpose(0, 2, 1) for a in res]
        else:
            res = _adamw(w[n], mom[n], var[n], own[n], from_chips[n], "adamw_" + n)
        out_g[n], out_d[n], out_m[n], out_v[n] = res
    ds, nms, nvs = _adamw_small([two_d(w[n]) for n in SMALL], [gsmall[n] for n in SMALL],
                                [two_d(mom[n]) for n in SMALL], [two_d(var[n]) for n in SMALL])
    for i, n in enumerate(SMALL):
        out_g[n], out_d[n], out_m[n], out_v[n] = [a.reshape(w[n].shape) for a in (gsmall[n], ds[i], nms[i], nvs[i])]

    return (loss, grad_x[None], *[out_g[n] for n in WEIGHTS], *[out_d[n] for n in WEIGHTS],
            *[out_m[n] for n in WEIGHTS], *[out_v[n] for n in WEIGHTS])
```

```python
import functools
import itertools

import jax
import jax.numpy as jnp
from jax import lax
from jax.experimental import pallas as pl
from jax.experimental.pallas import tpu as pltpu

F32 = jnp.float32
BF16 = jnp.bfloat16

D = 1024
CW = 512
GK = 256
GV = 512
NH = 4
CH = 64
LR = 16
NMEM = 256
XD = 256
FF = 4096
ZW = 3104
ZC = 3200
EPS = 1e-6
NDEV = 8

ZB_CB, ZB_CC, ZB_CU, ZB_V, ZB_G = 0, 1, 2, 4, 5
ZB_Q, ZB_K = 6, 7
ZB_LR = 24

TM = 512
TM_MLP = 256
TF = 512
TB = 512
TB_BWD = 512
TT = 2048
VMEM_LIMIT = 56 * 1024 * 1024

ADAM_LR, ADAM_B1, ADAM_B2, ADAM_EPS, ADAM_WD, ADAM_STEP = 0.001, 0.9, 0.999, 1e-08, 0.01, 10

XKV_SHARD = 2 * D // NDEV

MESH = pl.DeviceIdType.MESH


def _cparams(sem):
    return pltpu.CompilerParams(dimension_semantics=sem, vmem_limit_bytes=VMEM_LIMIT)


def _call(body, name, grid, in_specs, out_specs, out_shape, scratch, args, riders=()):
    n_in, n_out, n_scr = len(in_specs), len(out_specs), len(scratch)
    counts = [(len(r.arrays), len(r.out_shape), len(r.scratch)) for r in riders]

    def take(refs, pos, sizes):
        groups = []
        for size in sizes:
            groups.append(refs[pos:pos + size])
            pos += size
        return groups, pos

    def wrapped(*refs):
        ins, pos = refs[:n_in], n_in
        r_ins, pos = take(refs, pos, [c[0] for c in counts])
        outs, pos = refs[pos:pos + n_out], pos + n_out
        r_outs, pos = take(refs, pos, [c[1] for c in counts])
        scr, pos = refs[pos:pos + n_scr], pos + n_scr
        r_scr, pos = take(refs, pos, [c[2] for c in counts])
        ids = [pl.program_id(d) for d in range(len(grid))]
        first = functools.reduce(lambda a, b: a & b, [i == 0 for i in ids])
        last = functools.reduce(lambda a, b: a & b, [i == g - 1 for i, g in zip(ids, grid)])

        @pl.when(first)
        def _():
            for r, a, b, c in zip(riders, r_ins, r_outs, r_scr):
                r.start(a, b, c)

        body(*ins, *outs, *scr)

        @pl.when(last)
        def _():
            for r, a, b, c in zip(riders, r_ins, r_outs, r_scr):
                r.finish(a, b, c)

    hbm = pl.BlockSpec(memory_space=pltpu.HBM)
    r_args = [a for r in riders for a in r.arrays]
    r_shapes = [s for r in riders for s in r.out_shape]
    return pl.pallas_call(
        wrapped if riders else body, name=name, grid=grid, in_specs=list(in_specs) + [hbm] * len(r_args),
        out_specs=list(out_specs) + [hbm] * len(r_shapes), out_shape=list(out_shape) + r_shapes,
        scratch_shapes=list(scratch) + [s for r in riders for s in r.scratch],
        compiler_params=_cparams(("arbitrary",) * len(grid)))(*args, *r_args)


def _dot(a, b):
    return jnp.dot(a.astype(BF16), b.astype(BF16), preferred_element_type=F32)


def _dot_nt(a, b):
    return lax.dot_general(a.astype(BF16), b.astype(BF16), (((1,), (1,)), ((), ())), preferred_element_type=F32)


def _dot_tn(a, b):
    return lax.dot_general(a.astype(BF16), b.astype(BF16), (((0,), (0,)), ((), ())), preferred_element_type=F32)


def _split(x, n):
    parts = []
    for _ in range(n):
        p = x.astype(BF16)
        parts.append(p)
        x = x - p.astype(F32)
    return parts


def _dot_exact_lhs(m, x, n):
    return functools.reduce(lambda a, b: a + b, [jnp.dot(m, p, preferred_element_type=F32) for p in _split(x, n)])


def _dot_exact_rhs(x, m, n):
    return functools.reduce(lambda a, b: a + b, [jnp.dot(p, m, preferred_element_type=F32) for p in _split(x, n)])


def _rms(x, g):
    r = lax.rsqrt(jnp.mean(x * x, axis=-1, keepdims=True) + EPS)
    return x * r * g, r


def _rms_bwd(x, r, g, dy):
    xr = x * r
    u = dy * g
    dx = r * (u - xr * jnp.mean(u * xr, axis=-1, keepdims=True))
    return dx, jnp.sum(dy * xr, axis=0, keepdims=True)


def _iota(shape, dim):
    return lax.broadcasted_iota(jnp.int32, shape, dim)


def _sigmoid(x):
    return 1.0 / (1.0 + jnp.exp(-x))


def _acc_rows(ref, row):
    ref[...] += jnp.broadcast_to(row, ref.shape)


def _inproj(x, g, w_t, riders=()):
    t = x.shape[0]
    tm = min(TM, t)

    def body(x_ref, g_ref, w_ref, z_ref, h_ref):
        h, _ = _rms(x_ref[...], g_ref[...])
        hb = h.astype(BF16)
        h_ref[...] = hb
        z_ref[...] = _dot_nt(hb, w_ref[...])

    return _call(
        body, "inproj", (t // tm,),
        [pl.BlockSpec((tm, D), lambda i: (i, 0)), pl.BlockSpec((1, D), lambda i: (0, 0)),
         pl.BlockSpec((ZC, D), lambda i: (0, 0))],
        [pl.BlockSpec((tm, ZC), lambda i: (i, 0)), pl.BlockSpec((tm, D), lambda i: (i, 0))],
        [jax.ShapeDtypeStruct((t, ZC), F32), jax.ShapeDtypeStruct((t, D), BF16)], [], (x, g, w_t), riders)


def _kv_proj(mem, g, w):
    def body(m_ref, g_ref, w_ref, kv_ref, mn_ref):
        mn, _ = _rms(m_ref[...], g_ref[...])
        mb = mn.astype(BF16)
        mn_ref[...] = mb
        for j in range(NDEV):
            kv_ref[:, j * XKV_SHARD:(j + 1) * XKV_SHARD] = jnp.dot(mb, w_ref[j], preferred_element_type=F32)

    return pl.pallas_call(
        body, name="kv_proj",
        out_shape=[jax.ShapeDtypeStruct((NMEM, 2 * D), F32), jax.ShapeDtypeStruct((NMEM, D), BF16)],
        compiler_params=pltpu.CompilerParams(vmem_limit_bytes=VMEM_LIMIT))(mem, g, w)


def _softmax_head(qb, kb):
    s = _dot_nt(qb, kb) * (1.0 / 16.0)
    e = jnp.exp(s - jnp.max(s, axis=-1, keepdims=True))
    return e / jnp.sum(e, axis=-1, keepdims=True)


def _attn_fwd(x, yb, w_out, g, w_xq, kb, vb, w_xo):
    t = x.shape[0]
    tm = min(TM, t)

    def body(x_ref, y_ref, wo_ref, g_ref, wq_ref, k_ref, v_ref, wx_ref, x1_ref, x2_ref, xn_ref, q_ref, a_ref):
        x1 = x_ref[...] + jnp.dot(y_ref[...], wo_ref[...], preferred_element_type=F32)
        x1_ref[...] = x1
        xn, _ = _rms(x1, g_ref[...])
        xb = xn.astype(BF16)
        xn_ref[...] = xb
        qb = jnp.dot(xb, wq_ref[...], preferred_element_type=F32).astype(BF16)
        q_ref[...] = qb
        for h in range(NH):
            hs = slice(h * XD, (h + 1) * XD)
            p = _softmax_head(qb[:, hs], k_ref[:, hs])
            a_ref[:, hs] = _dot(p, v_ref[:, hs]).astype(BF16)
        x2_ref[...] = x1 + jnp.dot(a_ref[...], wx_ref[...], preferred_element_type=F32)

    tok = lambda i: (i, 0)
    full = lambda i: (0, 0)
    return pl.pallas_call(
        body, name="attn_fwd", grid=(t // tm,),
        in_specs=[pl.BlockSpec((tm, D), tok), pl.BlockSpec((tm, D), tok), pl.BlockSpec((D, D), full),
                  pl.BlockSpec((1, D), full), pl.BlockSpec((D, D), full), pl.BlockSpec((NMEM, D), full),
                  pl.BlockSpec((NMEM, D), full), pl.BlockSpec((D, D), full)],
        out_specs=[pl.BlockSpec((tm, D), tok)] * 5,
        out_shape=[jax.ShapeDtypeStruct((t, D), F32), jax.ShapeDtypeStruct((t, D), F32),
                   jax.ShapeDtypeStruct((t, D), BF16), jax.ShapeDtypeStruct((t, D), BF16),
                   jax.ShapeDtypeStruct((t, D), BF16)],
        compiler_params=_cparams(("arbitrary",)))(x, yb, w_out, g, w_xq, kb, vb, w_xo)


def _mlp_fwd(x2, g, w_up_t, w_down, fg, target):
    t = x2.shape[0]
    tm = min(TM_MLP, t)

    def body(x_ref, g_ref, wu_ref, wd_ref, fg_ref, t_ref, h1_ref, xn_ref, dx_ref, dxb_ref, loss_ref, dfg_ref, ab):
        @pl.when(pl.program_id(0) == 0)
        def _():
            loss_ref[...] = jnp.zeros_like(loss_ref)
            dfg_ref[...] = jnp.zeros_like(dfg_ref)

        x = x_ref[...]
        xn, _ = _rms(x, g_ref[...])
        xnb = xn.astype(BF16)
        xn_ref[...] = xnb
        for q in range(FF // TF):
            cols = slice(q * TF, (q + 1) * TF)
            h1 = _dot_nt(xnb, wu_ref[cols, :])
            h1_ref[:, cols] = h1.astype(BF16)
            hr = jnp.maximum(h1, 0.0)
            ab[:, cols] = (hr * hr).astype(BF16)
        x3 = x + jnp.dot(ab[...], wd_ref[...], preferred_element_type=F32)
        y, r = _rms(x3, fg_ref[...])
        e = y - t_ref[...]
        row = jnp.mean(e * e, axis=-1, keepdims=True)
        _acc_rows(loss_ref, 0.5 * jnp.sum(row, axis=0, keepdims=True))
        dx, dfg = _rms_bwd(x3, r, fg_ref[...], e * (1.0 / D))
        dx_ref[...] = dx
        dxb_ref[...] = dx.astype(BF16)
        _acc_rows(dfg_ref, dfg)

    tok = lambda i: (i, 0)
    full = lambda i: (0, 0)
    once = pl.Buffered(1)
    return pl.pallas_call(
        body, name="mlp_fwd", grid=(t // tm,),
        in_specs=[pl.BlockSpec((tm, D), tok), pl.BlockSpec((1, D), full),
                  pl.BlockSpec((FF, D), full, pipeline_mode=once), pl.BlockSpec((FF, D), full, pipeline_mode=once),
                  pl.BlockSpec((1, D), full), pl.BlockSpec((tm, D), tok)],
        out_specs=[pl.BlockSpec((tm, FF), tok), pl.BlockSpec((tm, D), tok), pl.BlockSpec((tm, D), tok),
                   pl.BlockSpec((tm, D), tok), pl.BlockSpec((8, 128), full), pl.BlockSpec((8, D), full)],
        out_shape=[jax.ShapeDtypeStruct((t, FF), BF16), jax.ShapeDtypeStruct((t, D), BF16),
                   jax.ShapeDtypeStruct((t, D), F32), jax.ShapeDtypeStruct((t, D), BF16),
                   jax.ShapeDtypeStruct((8, 128), F32), jax.ShapeDtypeStruct((8, D), F32)],
        scratch_shapes=[pltpu.VMEM((tm, FF), BF16)],
        compiler_params=_cparams(("arbitrary",)))(x2, g, w_up_t, w_down, fg, target)


def _mlp_bwd(dx3, dx3b, h1b, w_down, w_up_t, x2, g):
    t = x2.shape[0]
    tm = min(TM_MLP, t)

    def body(dx_ref, dxb_ref, h1_ref, wd_ref, wu_ref, x_ref, g_ref, a_ref, dh_ref, dx2_ref, dx2b_ref, dg_ref):
        @pl.when(pl.program_id(0) == 0)
        def _():
            dg_ref[...] = jnp.zeros_like(dg_ref)

        for q in range(FF // TF):
            cols = slice(q * TF, (q + 1) * TF)
            hr = jnp.maximum(h1_ref[:, cols].astype(F32), 0.0)
            da = _dot_nt(dxb_ref[...], wd_ref[cols, :])
            a_ref[:, cols] = (hr * hr).astype(BF16)
            dh_ref[:, cols] = (da * 2.0 * hr).astype(BF16)
        dxn = jnp.dot(dh_ref[...], wu_ref[...], preferred_element_type=F32)
        x = x_ref[...]
        r = lax.rsqrt(jnp.mean(x * x, axis=-1, keepdims=True) + EPS)
        dx, dg = _rms_bwd(x, r, g_ref[...], dxn)
        dx2 = dx_ref[...] + dx
        dx2_ref[...] = dx2
        dx2b_ref[...] = dx2.astype(BF16)
        _acc_rows(dg_ref, dg)

    tok = lambda i: (i, 0)
    full = lambda i: (0, 0)
    once = pl.Buffered(1)
    return pl.pallas_call(
        body, name="mlp_bwd", grid=(t // tm,),
        in_specs=[pl.BlockSpec((tm, D), tok), pl.BlockSpec((tm, D), tok), pl.BlockSpec((tm, FF), tok),
                  pl.BlockSpec((FF, D), full, pipeline_mode=once), pl.BlockSpec((FF, D), full, pipeline_mode=once),
                  pl.BlockSpec((tm, D), tok), pl.BlockSpec((1, D), full)],
        out_specs=[pl.BlockSpec((tm, FF), tok), pl.BlockSpec((tm, FF), tok), pl.BlockSpec((tm, D), tok),
                   pl.BlockSpec((tm, D), tok), pl.BlockSpec((8, D), full)],
        out_shape=[jax.ShapeDtypeStruct((t, FF), BF16), jax.ShapeDtypeStruct((t, FF), BF16),
                   jax.ShapeDtypeStruct((t, D), F32), jax.ShapeDtypeStruct((t, D), BF16),
                   jax.ShapeDtypeStruct((8, D), F32)],
        compiler_params=_cparams(("arbitrary",)))(dx3, dx3b, h1b, w_down, w_up_t, x2, g)


def _attn_bwd(x1, dx2, dx2b, qb, kb, vb, w_xo, w_xq, w_out, g):
    t = x1.shape[0]
    tm = min(TM, t)

    def body(x_ref, dx2_ref, dx2b_ref, q_ref, k_ref, v_ref, wx_ref, wq_ref, wo_ref, g_ref,
             dx1_ref, dx1b_ref, dy_ref, dq_ref, dkv_ref, dg_ref):
        @pl.when(pl.program_id(0) == 0)
        def _():
            dkv_ref[...] = jnp.zeros_like(dkv_ref)
            dg_ref[...] = jnp.zeros_like(dg_ref)

        datt = _dot_nt(dx2b_ref[...], wx_ref[...]).astype(BF16)
        for h in range(NH):
            hs = slice(h * XD, (h + 1) * XD)
            q_h, k_h, v_h, da_h = q_ref[:, hs], k_ref[:, hs], v_ref[:, hs], datt[:, hs]
            p = _softmax_head(q_h, k_h)
            dp = _dot_nt(da_h, v_h)
            ds = (p * (dp - jnp.sum(dp * p, axis=-1, keepdims=True)) * (1.0 / 16.0)).astype(BF16)
            dq_ref[:, hs] = _dot(ds, k_h).astype(BF16)
            dkv_ref[:, hs] += _dot_tn(ds, q_h)
            dkv_ref[:, D + h * XD:D + (h + 1) * XD] += _dot_tn(p, da_h)
        dxn = _dot_nt(dq_ref[...], wq_ref[...])
        x = x_ref[...]
        r = lax.rsqrt(jnp.mean(x * x, axis=-1, keepdims=True) + EPS)
        dx, dg = _rms_bwd(x, r, g_ref[...], dxn)
        dx1 = dx2_ref[...] + dx
        dx1_ref[...] = dx1
        dx1b = dx1.astype(BF16)
        dx1b_ref[...] = dx1b
        dy_ref[...] = _dot_nt(dx1b, wo_ref[...])
        _acc_rows(dg_ref, dg)

    tok = lambda i: (i, 0)
    full = lambda i: (0, 0)
    return pl.pallas_call(
        body, name="attn_bwd", grid=(t // tm,),
        in_specs=[pl.BlockSpec((tm, D), tok), pl.BlockSpec((tm, D), tok), pl.BlockSpec((tm, D), tok),
                  pl.BlockSpec((tm, D), tok), pl.BlockSpec((NMEM, D), full), pl.BlockSpec((NMEM, D), full),
                  pl.BlockSpec((D, D), full), pl.BlockSpec((D, D), full), pl.BlockSpec((D, D), full),
                  pl.BlockSpec((1, D), full)],
        out_specs=[pl.BlockSpec((tm, D), tok), pl.BlockSpec((tm, D), tok), pl.BlockSpec((tm, D), tok),
                   pl.BlockSpec((tm, D), tok), pl.BlockSpec((NMEM, 2 * D), full), pl.BlockSpec((8, D), full)],
        out_shape=[jax.ShapeDtypeStruct((t, D), F32), jax.ShapeDtypeStruct((t, D), BF16),
                   jax.ShapeDtypeStruct((t, D), F32), jax.ShapeDtypeStruct((t, D), BF16),
                   jax.ShapeDtypeStruct((NMEM, 2 * D), F32), jax.ShapeDtypeStruct((8, D), F32)],
        compiler_params=_cparams(("arbitrary",)))(x1, dx2, dx2b, qb, kb, vb, w_xo, w_xq, w_out, g)


def _kv_bwd(dkv, memn, mem, g, w):
    def body(dkv_ref, mn_ref, m_ref, g_ref, w_ref, dw_ref, dg_ref):
        dkvb = dkv_ref[...].astype(BF16)
        dmn = jnp.zeros((NMEM, D), F32)
        for j in range(NDEV):
            cols = slice(j * XKV_SHARD, (j + 1) * XKV_SHARD)
            dw_ref[j] = _dot_tn(mn_ref[...], dkvb[:, cols])
            dmn += _dot_nt(dkvb[:, cols], w_ref[j])
        m = m_ref[...]
        r = lax.rsqrt(jnp.mean(m * m, axis=-1, keepdims=True) + EPS)
        dg_ref[...] = jnp.broadcast_to(jnp.sum(dmn * m * r, axis=0, keepdims=True), dg_ref.shape)

    return pl.pallas_call(
        body, name="kv_bwd",
        out_shape=[jax.ShapeDtypeStruct((NDEV, D, XKV_SHARD), F32), jax.ShapeDtypeStruct((8, D), F32)],
        compiler_params=pltpu.CompilerParams(vmem_limit_bytes=VMEM_LIMIT))(dkv, memn, mem, g, w)


def _inproj_bwd(dz, w_t, x, dx1, g, riders=()):
    t = x.shape[0]
    tm = min(TM, t)

    def body(dz_ref, w_ref, x_ref, dx1_ref, g_ref, gx_ref, dg_ref):
        @pl.when(pl.program_id(0) == 0)
        def _():
            dg_ref[...] = jnp.zeros_like(dg_ref)

        dh = jnp.dot(dz_ref[...], w_ref[...], preferred_element_type=F32)
        x = x_ref[...]
        r = lax.rsqrt(jnp.mean(x * x, axis=-1, keepdims=True) + EPS)
        dx, dg = _rms_bwd(x, r, g_ref[...], dh)
        gx_ref[...] = dx1_ref[...] + dx
        _acc_rows(dg_ref, dg)

    tok = lambda i: (i, 0)
    full = lambda i: (0, 0)
    return _call(
        body, "inproj_bwd", (t // tm,),
        [pl.BlockSpec((tm, ZC), tok), pl.BlockSpec((ZC, D), full), pl.BlockSpec((tm, D), tok),
         pl.BlockSpec((tm, D), tok), pl.BlockSpec((1, D), full)],
        [pl.BlockSpec((tm, D), tok), pl.BlockSpec((8, D), full)],
        [jax.ShapeDtypeStruct((t, D), F32), jax.ShapeDtypeStruct((8, D), F32)], [], (dz, w_t, x, dx1, g), riders)


def _matmul_tn(a, b, name, rows=None, riders=()):
    t, k = a.shape
    n = b.shape[1]
    tk, tn = [1024 if size % 1024 == 0 else 640 for size in (k, n)]
    tt = min(TT, t)
    rows = rows or k

    def body(a_ref, b_ref, o_ref):
        @pl.when(pl.program_id(2) == 0)
        def _():
            o_ref[...] = jnp.zeros_like(o_ref)

        o_ref[...] += _dot_tn(a_ref[...], b_ref[...])

    return _call(
        body, name, (k // tk, n // tn, t // tt),
        [pl.BlockSpec((tt, tk), lambda i, j, s: (s, i)), pl.BlockSpec((tt, tn), lambda i, j, s: (s, j))],
        [pl.BlockSpec((tk, tn), lambda i, j, s: (i, j))], [jax.ShapeDtypeStruct((rows, n), F32)], [], (a, b), riders)


def _lane_head(shape, dim, shift):
    return _iota(shape, dim) >> shift


def _gla_recompute(q_raw, k, lr, wpad, bias, rev, tb):
    pre = _dot(lr, wpad) + bias
    la = (jnp.minimum(pre, 0.0) - jnp.log(1.0 + jnp.exp(-jnp.abs(pre)))) * (1.0 / 16.0)
    r, c = _iota((tb, tb), 0), _iota((tb, tb), 1)
    tri = (c >= r) if rev else (c <= r)
    cum = jnp.where(((r >> 6) == (c >> 6)) & tri, 1.0, 0.0).astype(BF16)
    b = _dot_exact_lhs(cum, la, 3)
    e, ei = jnp.exp(b), jnp.exp(-b)
    qt = (q_raw * 0.125) * e
    kt = k * ei
    return pre, b, e, ei, qt, kt


def _stack_heads(x, shift):
    head = _lane_head(x.shape, 1, shift)
    return jnp.concatenate([jnp.where(head == h, x, 0.0) for h in range(NH)], axis=0).astype(BF16)


def _fold_heads(x, shift):
    head = _lane_head((CH, x.shape[1]), 1, shift)
    return functools.reduce(lambda a, b: a + b,
                            [jnp.where(head == h, x[h * CH:(h + 1) * CH], 0.0) for h in range(NH)])


def _wide_mask(rev):
    r, s = _iota((CH, NH * CH), 0), _iota((CH, NH * CH), 1) & (CH - 1)
    return (s >= r) if rev else (s <= r)


def _state_mask():
    return (_iota((GV, GK), 0) >> 7) == (_iota((GV, GK), 1) >> 6)


def _state_expand(sd):
    head = _lane_head(sd.shape, 1, 6)
    return jnp.concatenate([jnp.where(head == h, sd, 0.0) for h in range(NH)], axis=0)


def _conv_parts(cb, cc, cu, ccp, cup, ccn, cun, cw_ref, first, last, tb):
    h = cc * cu
    hp = jnp.where(first, 0.0, ccp * cup)
    hn = jnp.where(last, 0.0, ccn * cun)
    rows = _iota(h.shape, 0)
    h_m1 = jnp.where(rows == 0, hp, pltpu.roll(h, 1, 0))
    h_p1 = jnp.where(rows == tb - 1, hn, pltpu.roll(h, tb - 1, 0))
    conv = cw_ref[pl.ds(0, 1), :] * h_m1 + cw_ref[pl.ds(1, 1), :] * h + cw_ref[pl.ds(2, 1), :] * h_p1
    return h, h_m1, h_p1, conv


def _group_ones():
    return jnp.where((_iota((CW, CW), 0) >> 6) == (_iota((CW, CW), 1) >> 6), 1.0, 0.0).astype(BF16)


def _head_norm(o):
    ons, rs = [], []
    for h in range(NH):
        slab = o[:, h * 128:(h + 1) * 128]
        r = lax.rsqrt(jnp.mean(slab * slab, axis=-1, keepdims=True) + EPS)
        ons.append(slab * r)
        rs.append(jnp.broadcast_to(r, slab.shape))
    return jnp.concatenate(ons, axis=1), jnp.concatenate(rs, axis=1)


def _zspec(tb, width, blk, jmap):
    return pl.BlockSpec((tb, width), lambda i: (jmap(i), blk))


def _halo_specs(tb, nblk, t, blk, jmap):
    prev = pl.BlockSpec((8, CW), lambda i: (jnp.maximum(jmap(i) * (tb // 8) - 1, 0), blk))
    nxt = pl.BlockSpec((8, CW), lambda i: (jnp.minimum((jmap(i) + 1) * (tb // 8), t // 8 - 1), blk))
    return prev, nxt


def _gla_fwd_block(q_ref, k_ref, v_ref, lr_ref, w_ref, bias_ref, o_ref, sd_ref, st, b_scr, rev, tb):
    nb = tb // CH
    _, b, _, _, qt, kt = _gla_recompute(q_ref[...], k_ref[...], lr_ref[...], w_ref[...], bias_ref[...], rev, tb)
    v = v_ref[...]
    b_scr[...] = b
    yield
    maskw, bd = _wide_mask(rev), _state_mask()
    order = list(reversed(range(nb))) if rev else list(range(nb))
    rows = [slice(c * CH, (c + 1) * CH) for c in range(nb)]
    state = st[...]
    for c in order:
        gdec = jnp.exp(b_scr[pl.ds(c * CH + (0 if rev else CH - 1), 1), :])
        sd_ref[c] = state[0:128] + state[128:256] + state[256:384] + state[384:512]
        a = jnp.where(maskw, _dot_nt(qt[rows[c]], _stack_heads(kt[rows[c]], 6)), 0.0)
        o_ref[pl.ds(c * CH, CH), :] = _dot(a, _stack_heads(v[rows[c]], 7)) + _dot_nt(qt[rows[c]], state)
        state = state * gdec + jnp.where(bd, _dot_tn(v[rows[c]], kt[rows[c]] * gdec), 0.0)
        yield
    st[...] = state
    yield


def _gla_fwd(z, waf_pad, b_af, wab_pad, b_ab, riders=()):
    t = z.shape[0]
    tb = min(TB, t)
    nblk, nb = t // tb, tb // CH
    jmaps = (lambda i: i, lambda i: nblk - 1 - i)

    def body(qf, kf, vf, lrf, qr, kr, vr, lrr, wf, bf, wr, br, of_ref, sdf_ref, or_ref, sdr_ref,
             st_f, st_r, b_f, b_r):
        @pl.when(pl.program_id(0) == 0)
        def _():
            st_f[...] = jnp.zeros_like(st_f)
            st_r[...] = jnp.zeros_like(st_r)

        for _ in zip(_gla_fwd_block(qf, kf, vf, lrf, wf, bf, of_ref, sdf_ref, st_f, b_f, False, tb),
                     _gla_fwd_block(qr, kr, vr, lrr, wr, br, or_ref, sdr_ref, st_r, b_r, True, tb)):
            pass

    full = lambda i: (0, 0)
    zspecs = [s for jm in jmaps for s in (_zspec(tb, GK, ZB_Q, jm), _zspec(tb, GK, ZB_K, jm),
                                         _zspec(tb, GV, ZB_V, jm), _zspec(tb, 128, ZB_LR, jm))]
    wspecs = [pl.BlockSpec((128, GK), full), pl.BlockSpec((1, GK), full)] * 2
    out_specs = [s for jm in jmaps for s in (pl.BlockSpec((tb, GV), lambda i, jm=jm: (jm(i), 0)),
                                             pl.BlockSpec((nb, 128, GK), lambda i, jm=jm: (jm(i), 0, 0)))]
    out_shape = [jax.ShapeDtypeStruct((t, GV), F32), jax.ShapeDtypeStruct((t // CH, 128, GK), F32)] * 2
    scratch = [pltpu.VMEM((GV, GK), F32), pltpu.VMEM((GV, GK), F32), pltpu.VMEM((tb, GK), F32),
               pltpu.VMEM((tb, GK), F32)]
    return _call(body, "gla_fwd", (nblk,), zspecs + wspecs, out_specs, out_shape, scratch,
                 [z] * 8 + [waf_pad, b_af, wab_pad, b_ab], riders)


def _mixer_finish(z, o_f, o_b, conv_w, conv_norm, gla_norm4):
    t = z.shape[0]
    tb = min(TM, t)
    nblk = t // tb
    jmap = lambda i: i

    def body(q_ref, k_ref, v_ref, g_ref, cb_ref, cc_ref, cu_ref, ccp_ref, ccn_ref, cup_ref, cun_ref, of_ref, ob_ref,
             cw_ref, cn_ref, gn_ref, y_ref, opre_ref):
        j = pl.program_id(0)
        v = v_ref[...]
        hsel = jnp.where((_iota((GK, GV), 0) >> 6) == (_iota((GK, GV), 1) >> 7), 1.0, 0.0).astype(BF16)
        sb = _dot_exact_rhs((q_ref[...] * 0.125) * k_ref[...], hsel, 2)
        o_pre = of_ref[...] + ob_ref[...] - sb * v
        opre_ref[...] = o_pre
        on, _ = _head_norm(o_pre)
        g = g_ref[...]
        y_ref[:, CW:] = (on * gn_ref[...] * (g * _sigmoid(g))).astype(BF16)
        cb = cb_ref[...]
        _, _, _, conv = _conv_parts(cb, cc_ref[...], cu_ref[...], ccp_ref[pl.ds(7, 1), :], cup_ref[pl.ds(7, 1), :],
                                    ccn_ref[pl.ds(0, 1), :], cun_ref[pl.ds(0, 1), :], cw_ref, j == 0,
                                    j == nblk - 1, tb)
        yc = cb * conv
        gm = _dot_exact_rhs(yc * yc, _group_ones(), 2) * (1.0 / 64.0)
        y_ref[:, :CW] = (yc * lax.rsqrt(gm + EPS) * cn_ref[...]).astype(BF16)

    full = lambda i: (0, 0)
    tokv = pl.BlockSpec((tb, GV), lambda i: (i, 0))
    ccp, ccn = _halo_specs(tb, nblk, t, ZB_CC, jmap)
    cup, cun = _halo_specs(tb, nblk, t, ZB_CU, jmap)
    in_specs = [_zspec(tb, GK, ZB_Q, jmap), _zspec(tb, GK, ZB_K, jmap), _zspec(tb, GV, ZB_V, jmap),
                _zspec(tb, GV, ZB_G, jmap), _zspec(tb, CW, ZB_CB, jmap), _zspec(tb, CW, ZB_CC, jmap),
                _zspec(tb, CW, ZB_CU, jmap), ccp, ccn, cup, cun, tokv, tokv,
                pl.BlockSpec((3, CW), full), pl.BlockSpec((1, CW), full), pl.BlockSpec((1, GV), full)]
    return pl.pallas_call(
        body, name="mixer_finish", grid=(nblk,), in_specs=in_specs,
        out_specs=[pl.BlockSpec((tb, D), lambda i: (i, 0)), tokv],
        out_shape=[jax.ShapeDtypeStruct((t, D), BF16), jax.ShapeDtypeStruct((t, GV), F32)],
        compiler_params=_cparams(("arbitrary",)))(
            z, z, z, z, z, z, z, z, z, z, z, o_f, o_b, conv_w, conv_norm, gla_norm4)


def _gla_bwd_chunks(do_ref, sd_ref, dst, b_scr, db_scr, dq_ref, dk_ref, dv_ref, qt, kt, e, ei, v, rev, nb):
    maskw, bd = _wide_mask(rev), _state_mask()
    for c in (range(nb) if rev else reversed(range(nb))):
        sl = slice(c * CH, (c + 1) * CH)
        grow = c * CH + (0 if rev else CH - 1)
        gdec = jnp.exp(b_scr[pl.ds(grow, 1), :])
        qt_c, kt_c, v_c, do_c = qt[sl], kt[sl], v[sl], do_ref[pl.ds(c * CH, CH), :]
        s_in = _state_expand(sd_ref[c])
        ds_out = dst[...]
        kbd, vbd = _stack_heads(kt_c, 6), _stack_heads(v_c, 7)
        a = jnp.where(maskw, _dot_nt(qt_c, kbd), 0.0)
        da = jnp.where(maskw, _dot_nt(do_c, vbd), 0.0)
        dv_ref[pl.ds(c * CH, CH), :] = _fold_heads(_dot_tn(a, do_c), 7) + _dot_nt(kt_c * gdec, ds_out)
        da_do = jnp.concatenate([da.astype(BF16), do_c.astype(BF16)], axis=1)
        dqt = _dot(da_do, jnp.concatenate([kbd, s_in.astype(BF16)], axis=0))
        dkh = _dot(v_c, ds_out)
        both = _dot_tn(da_do, qt_c)
        dkt = _fold_heads(both[:NH * CH], 6) + dkh * gdec
        dg = jnp.sum(ds_out * s_in, axis=0, keepdims=True) + jnp.sum(kt_c * dkh, axis=0, keepdims=True)
        db_scr[pl.ds(c * CH, CH), :] = dqt * qt_c - dkt * kt_c
        db_scr[pl.ds(grow, 1), :] += dg * gdec
        dq_ref[pl.ds(c * CH, CH), :] = dqt * e[sl] * 0.125
        dk_ref[pl.ds(c * CH, CH), :] = dkt * ei[sl]
        dst[...] = ds_out * gdec + jnp.where(bd, both[NH * CH:], 0.0)
        yield


def _gate_bwd(db, pre, lr, wpad, rev, tb):
    r, c = _iota((tb, tb), 0), _iota((tb, tb), 1)
    tri = (c <= r) if rev else (c >= r)
    cum_t = jnp.where(((r >> 6) == (c >> 6)) & tri, 1.0, 0.0).astype(BF16)
    dla = _dot_exact_lhs(cum_t, db, 2)
    dpre = dla * (1.0 / 16.0) / (1.0 + jnp.exp(pre))
    return dpre, _dot_nt(dpre, wpad), _dot_tn(lr, dpre)


def _gla_bwd_first(z, dy, o_pre, sd, wpad, bias, conv_w, conv_norm, gla_norm4, riders=()):
    t = z.shape[0]
    tb = min(TB_BWD, t)
    nblk, nb = t // tb, tb // CH
    jmap = lambda i: nblk - 1 - i

    def body(q_ref, k_ref, v_ref, lr_ref, g_ref, cb_ref, cc_ref, cu_ref, ccp_ref, ccn_ref, cup_ref, cun_ref,
             dy_ref, opre_ref, sd_ref, w_ref, bias_ref, cw_ref, cn_ref, gn_ref,
             do_ref, dq_ref, dk_ref, dv_ref, dlr_ref, dzg_ref, dzcb_ref, dconv_ref,
             dw_ref, dbias_ref, dcw_ref, dcn_ref, dgn_ref, dst, b_scr, db_scr):
        i = pl.program_id(0)
        j = jmap(i)

        @pl.when(i == 0)
        def _():
            dst[...] = jnp.zeros_like(dst)
            for ref in (dw_ref, dbias_ref, dcw_ref, dcn_ref, dgn_ref):
                ref[...] = jnp.zeros_like(ref)

        dyg = dy_ref[:, CW:]
        g = g_ref[...]
        sig = _sigmoid(g)
        on, rr = _head_norm(opre_ref[...])
        gn = gn_ref[...]
        dzg_ref[...] = (dyg * on * gn * (sig * (1.0 + g * (1.0 - sig)))).astype(BF16)
        don = dyg * (g * sig)
        _acc_rows(dgn_ref, jnp.sum(don * on, axis=0, keepdims=True))
        u = don * gn
        uo = u * on
        mean_uo = jnp.concatenate(
            [jnp.broadcast_to(jnp.mean(uo[:, h * 128:(h + 1) * 128], axis=-1, keepdims=True), (tb, 128))
             for h in range(NH)], axis=1)
        do_ref[...] = rr * (u - on * mean_uo)

        def conv_branch():
            cb = cb_ref[...]
            h, h_m1, h_p1, conv = _conv_parts(cb, cc_ref[...], cu_ref[...], ccp_ref[pl.ds(7, 1), :],
                                              cup_ref[pl.ds(7, 1), :], ccn_ref[pl.ds(0, 1), :],
                                              cun_ref[pl.ds(0, 1), :], cw_ref, j == 0, j == nblk - 1, tb)
            yc = cb * conv
            yield
            ones = _group_ones()
            rc = lax.rsqrt(_dot_exact_rhs(yc * yc, ones, 2) * (1.0 / 64.0) + EPS)
            ycr = yc * rc
            yield
            dyn = dy_ref[:, :CW]
            _acc_rows(dcn_ref, jnp.sum(dyn * ycr, axis=0, keepdims=True))
            uc = dyn * cn_ref[...]
            yield
            dyc = rc * (uc - ycr * (_dot_exact_rhs(uc * ycr, ones, 2) * (1.0 / 64.0)))
            dzcb_ref[...] = (dyc * conv).astype(BF16)
            yield
            dconv = dyc * cb
            dconv_ref[...] = dconv
            yield
            dcw_ref[pl.ds(0, 1), :] += jnp.sum(dconv * h_m1, axis=0, keepdims=True)
            dcw_ref[pl.ds(1, 1), :] += jnp.sum(dconv * h, axis=0, keepdims=True)
            dcw_ref[pl.ds(2, 1), :] += jnp.sum(dconv * h_p1, axis=0, keepdims=True)
            yield

        lr, wp = lr_ref[...], w_ref[...]
        pre, b, e, ei, qt, kt = _gla_recompute(q_ref[...], k_ref[...], lr, wp, bias_ref[...], False, tb)
        b_scr[...] = b
        for _ in itertools.zip_longest(
                _gla_bwd_chunks(do_ref, sd_ref, dst, b_scr, db_scr, dq_ref, dk_ref, dv_ref, qt, kt, e, ei, v_ref[...],
                                False, nb), conv_branch()):
            pass
        dpre, dlr, dw = _gate_bwd(db_scr[...], pre, lr, wp, False, tb)
        dlr_ref[...] = dlr
        dw_ref[...] += dw
        _acc_rows(dbias_ref, jnp.sum(dpre, axis=0, keepdims=True))

    full = lambda i: (0, 0)
    tokv = pl.BlockSpec((tb, GV), lambda i: (jmap(i), 0))
    tokk = pl.BlockSpec((tb, GK), lambda i: (jmap(i), 0))
    ccp, ccn = _halo_specs(tb, nblk, t, ZB_CC, jmap)
    cup, cun = _halo_specs(tb, nblk, t, ZB_CU, jmap)
    in_specs = [_zspec(tb, GK, ZB_Q, jmap), _zspec(tb, GK, ZB_K, jmap), _zspec(tb, GV, ZB_V, jmap),
                _zspec(tb, 128, ZB_LR, jmap), _zspec(tb, GV, ZB_G, jmap), _zspec(tb, CW, ZB_CB, jmap),
                _zspec(tb, CW, ZB_CC, jmap), _zspec(tb, CW, ZB_CU, jmap), ccp, ccn, cup, cun,
                pl.BlockSpec((tb, D), lambda i: (jmap(i), 0)), tokv,
                pl.BlockSpec((nb, 128, GK), lambda i: (jmap(i), 0, 0)), pl.BlockSpec((128, GK), full),
                pl.BlockSpec((1, GK), full), pl.BlockSpec((3, CW), full), pl.BlockSpec((1, CW), full),
                pl.BlockSpec((1, GV), full)]
    out_specs = [tokv, tokk, tokk, tokv, pl.BlockSpec((tb, 128), lambda i: (jmap(i), 0)), tokv, tokv, tokv,
                 pl.BlockSpec((128, GK), full), pl.BlockSpec((8, GK), full), pl.BlockSpec((8, CW), full),
                 pl.BlockSpec((8, CW), full), pl.BlockSpec((8, GV), full)]
    out_shape = [jax.ShapeDtypeStruct((t, GV), F32), jax.ShapeDtypeStruct((t, GK), F32),
                 jax.ShapeDtypeStruct((t, GK), F32), jax.ShapeDtypeStruct((t, GV), F32),
                 jax.ShapeDtypeStruct((t, 128), F32), jax.ShapeDtypeStruct((t, GV), BF16),
                 jax.ShapeDtypeStruct((t, CW), BF16), jax.ShapeDtypeStruct((t, CW), F32),
                 jax.ShapeDtypeStruct((128, GK), F32), jax.ShapeDtypeStruct((8, GK), F32),
                 jax.ShapeDtypeStruct((8, CW), F32), jax.ShapeDtypeStruct((8, CW), F32),
                 jax.ShapeDtypeStruct((8, GV), F32)]
    return _call(
        body, "gla_bwd_first", (nblk,), in_specs, out_specs, out_shape,
        [pltpu.VMEM((GV, GK), F32), pltpu.VMEM((tb, GK), F32), pltpu.VMEM((tb, GK), F32)],
        (z, z, z, z, z, z, z, z, z, z, z, z, dy, o_pre, sd, wpad, bias, conv_w, conv_norm, gla_norm4), riders)


def _gla_bwd_second(z, do, sd, wpad, bias, dqa, dka, dva, dlra, dzg, dzcb, dconv, conv_w, riders=()):
    t = z.shape[0]
    tb = min(TB_BWD, t)
    nblk, nb = t // tb, tb // CH
    jmap = lambda i: i

    def body(q_ref, k_ref, v_ref, lr_ref, cc_ref, cu_ref, do_ref, sd_ref, w_ref, bias_ref, dqa_ref, dka_ref,
             dva_ref, dlra_ref, dzg_ref, dzcb_ref, dc_ref, dcp_ref, dcn_ref, cw_ref,
             dz_ref, dw_ref, dbias_ref, dst, b_scr, db_scr, dq_scr, dk_scr, dv_scr, sb_scr, dsk_scr):
        i = pl.program_id(0)

        @pl.when(i == 0)
        def _():
            dst[...] = jnp.zeros_like(dst)
            dw_ref[...] = jnp.zeros_like(dw_ref)
            dbias_ref[...] = jnp.zeros_like(dbias_ref)

        q_raw, k, v, lr, wp = q_ref[...], k_ref[...], v_ref[...], lr_ref[...], w_ref[...]
        pre, b, e, ei, qt, kt = _gla_recompute(q_raw, k, lr, wp, bias_ref[...], True, tb)
        b_scr[...] = b

        def token_local():
            dc = dc_ref[...]
            rows = _iota(dc.shape, 0)
            dprev = jnp.where(i == 0, 0.0, dcp_ref[pl.ds(7, 1), :])
            dnext = jnp.where(i == nblk - 1, 0.0, dcn_ref[pl.ds(0, 1), :])
            dc_m1 = jnp.where(rows == 0, dprev, pltpu.roll(dc, 1, 0))
            dc_p1 = jnp.where(rows == tb - 1, dnext, pltpu.roll(dc, tb - 1, 0))
            yield
            dh = cw_ref[pl.ds(0, 1), :] * dc_p1 + cw_ref[pl.ds(1, 1), :] * dc + cw_ref[pl.ds(2, 1), :] * dc_m1
            dz_ref[:, 0:512] = dzcb_ref[...]
            yield
            dz_ref[:, 512:1024] = (dh * cu_ref[...]).astype(BF16)
            dz_ref[:, 1024:1536] = (dh * cc_ref[...]).astype(BF16)
            dz_ref[:, 2560:3072] = dzg_ref[...]
            yield
            hsel = jnp.where((_iota((GK, GV), 0) >> 6) == (_iota((GK, GV), 1) >> 7), 1.0, 0.0).astype(BF16)
            sb_scr[...] = _dot_exact_rhs((q_raw * 0.125) * k, hsel, 2)
            yield
            hsel_t = jnp.where((_iota((GV, GK), 0) >> 7) == (_iota((GV, GK), 1) >> 6), 1.0, 0.0).astype(BF16)
            dsk_scr[...] = _dot_exact_rhs(do_ref[...] * v, hsel_t, 2)
            yield

        for _ in itertools.zip_longest(
                _gla_bwd_chunks(do_ref, sd_ref, dst, b_scr, db_scr, dq_scr, dk_scr, dv_scr, qt, kt, e, ei, v, True, nb),
                token_local()):
            pass
        dpre, dlr, dw = _gate_bwd(db_scr[...], pre, lr, wp, True, tb)
        dw_ref[...] += dw
        _acc_rows(dbias_ref, jnp.sum(dpre, axis=0, keepdims=True))
        dsk = dsk_scr[...]
        dz_ref[:, 1536:1792] = (dqa_ref[...] + dq_scr[...] - dsk * k * 0.125).astype(BF16)
        dz_ref[:, 1792:2048] = (dka_ref[...] + dk_scr[...] - dsk * (q_raw * 0.125)).astype(BF16)
        dz_ref[:, 2048:2560] = (dva_ref[...] + dv_scr[...] - sb_scr[...] * do_ref[...]).astype(BF16)
        dz_ref[:, 3072:3200] = (dlra_ref[...] + dlr).astype(BF16)

    full = lambda i: (0, 0)
    tokv = pl.BlockSpec((tb, GV), lambda i: (i, 0))
    tokk = pl.BlockSpec((tb, GK), lambda i: (i, 0))
    dcp = pl.BlockSpec((8, CW), lambda i: (jnp.maximum(i * (tb // 8) - 1, 0), 0))
    dcn = pl.BlockSpec((8, CW), lambda i: (jnp.minimum((i + 1) * (tb // 8), t // 8 - 1), 0))
    in_specs = [_zspec(tb, GK, ZB_Q, jmap), _zspec(tb, GK, ZB_K, jmap), _zspec(tb, GV, ZB_V, jmap),
                _zspec(tb, 128, ZB_LR, jmap), _zspec(tb, CW, ZB_CC, jmap), _zspec(tb, CW, ZB_CU, jmap), tokv,
                pl.BlockSpec((nb, 128, GK), lambda i: (i, 0, 0)), pl.BlockSpec((128, GK), full),
                pl.BlockSpec((1, GK), full), tokk, tokk, tokv, pl.BlockSpec((tb, 128), lambda i: (i, 0)), tokv, tokv,
                tokv, dcp, dcn, pl.BlockSpec((3, CW), full)]
    out_specs = [pl.BlockSpec((tb, ZC), lambda i: (i, 0)), pl.BlockSpec((128, GK), full), pl.BlockSpec((8, GK), full)]
    out_shape = [jax.ShapeDtypeStruct((t, ZC), BF16), jax.ShapeDtypeStruct((128, GK), F32),
                 jax.ShapeDtypeStruct((8, GK), F32)]
    return _call(
        body, "gla_bwd_second", (nblk,), in_specs, out_specs, out_shape,
        [pltpu.VMEM((GV, GK), F32), pltpu.VMEM((tb, GK), F32), pltpu.VMEM((tb, GK), F32),
         pltpu.VMEM((tb, GK), F32), pltpu.VMEM((tb, GK), F32), pltpu.VMEM((tb, GV), F32),
         pltpu.VMEM((tb, GV), F32), pltpu.VMEM((tb, GK), F32)],
        (z, z, z, z, z, z, do, sd, wpad, bias, dqa, dka, dva, dlra, dzg, dzcb, dconv, dconv, dconv, conv_w), riders)


def _step(x, mem, target, shard, small_pack, vec, place):
    own, from_chips = {}, {}

    def pair_sums(names, g4, from_sibling):
        pbs = []
        for n, g, s in zip(names, g4, from_sibling):
            pb, own[n] = _rs_pair_sum(place, g, s, "pair_sum_" + n)
            pbs.append(pb)
        return pbs

    def by_dest(g, n):
        return g.reshape((4, 2) + shard[n].shape)

    w_in, small_all = _exchange(_gather_rider([shard["w_in"], small_pack]), "gather_w_in")
    w_in = jnp.pad(w_in.reshape(ZW, D), ((0, ZC - ZW), (0, 0)))
    small_all = small_all.reshape(NDEV, -1)
    p, off = {}, 0
    for n, (r, c) in SMALL_SHARDED.items():
        p[n] = small_all[:, off:off + r * c].reshape(NDEV, r, c).transpose(1, 0, 2).reshape(r, NDEV * c)
        off += r * c
    zeros_lr = jnp.zeros((128 - LR, GK), BF16)
    waf_pad = jnp.concatenate([p["w_af"].astype(BF16), zeros_lr], axis=0)
    wab_pad = jnp.concatenate([jnp.zeros((LR, GK), BF16), p["w_ab"].astype(BF16), zeros_lr[:128 - 2 * LR]], axis=0)
    gla_norm4 = jnp.tile(vec["gla_norm"], (1, NH))

    z, hb, w_out, w_xq, w_xo, w_xkv = _inproj(
        x, vec["mix_norm"], w_in, [_gather_rider([shard[n] for n in ("w_out", "w_xq", "w_xo", "w_xkv")])])
    w_out, w_xq, w_xo = [a.reshape(D, D) for a in (w_out, w_xq, w_xo)]
    o_f, sd_f, o_b, sd_b, w_up_t, w_down = _gla_fwd(
        z, waf_pad, vec["b_af"], wab_pad, vec["b_ab"], [_gather_rider([shard["w_up"], shard["w_down"]])])
    w_up_t, w_down = w_up_t.reshape(FF, D), w_down.reshape(FF, D)
    yb, o_pre = _mixer_finish(z, o_f, o_b, p["conv_w"], vec["conv_norm"], gla_norm4)
    kv, memn = _kv_proj(mem, vec["mem_norm"], w_xkv)
    kb, vb = kv[:, :D].astype(BF16), kv[:, D:].astype(BF16)
    x1, x2, xn1, qb, attb = _attn_fwd(x, yb, w_out, vec["xa_norm"], w_xq, kb, vb, w_xo)
    h1b, xn2, dx3, dx3b, loss8, dfinal = _mlp_fwd(x2, vec["mlp_norm"], w_up_t, w_down, vec["final_norm"], target)

    ab, dh1b, dx2, dx2b, dmlp = _mlp_bwd(dx3, dx3b, h1b, w_down, w_up_t, x2, vec["mlp_norm"])
    g_mlp = [by_dest(_matmul_tn(ab, dx3b, "dw_down")[0], "w_down"),
             by_dest(_matmul_tn(dh1b, xn2, "dw_up")[0], "w_up")]
    dx1, dx1b, dy, dqb, dkv, dxa = _attn_bwd(x1, dx2, dx2b, qb, kb, vb, w_xo, w_xq, w_out, vec["xa_norm"])
    dw_xo, *s_mlp = _matmul_tn(attb, dx2b, "dw_xo", riders=[_sibling_rider(g_mlp)])
    pb_mlp = pair_sums(("w_down", "w_up"), g_mlp, s_mlp)
    dw_xkv, dmemn = _kv_bwd(dkv, memn, mem, vec["mem_norm"], w_xkv)
    att_names = ("w_xo", "w_xq", "w_out", "w_xkv")
    g_att = [by_dest(g, n) for g, n in zip(
        (dw_xo, _matmul_tn(xn1, dqb, "dw_xq")[0], _matmul_tn(yb, dx1b, "dw_out")[0], dw_xkv), att_names)]
    res = _gla_bwd_first(z, dy, o_pre, sd_f, waf_pad, vec["b_af"], p["conv_w"], vec["conv_norm"], gla_norm4,
                         riders=[_chips_rider(pb_mlp), _sibling_rider(g_att)])
    do, dqa, dka, dva, dlra, dzg, dzcb, dconv, dwaf, dbaf, dcw, dcn, dgn = res[:13]
    from_chips["w_down"], from_chips["w_up"] = res[13:15]
    pb_att = pair_sums(att_names, g_att, res[15:])
    dz, dwab, dbab, *c_att = _gla_bwd_second(z, do, sd_b, wab_pad, vec["b_ab"], dqa, dka, dva, dlra, dzg, dzcb, dconv,
                                             p["conv_w"], riders=[_chips_rider(pb_att)])
    from_chips.update(zip(att_names, c_att))
    g_in = [by_dest(_matmul_tn(dz, hb, "dw_in", rows=ZW)[0], "w_in")]
    pb_in = pair_sums(("w_in",), g_in, _exchange(_sibling_rider(g_in), "grads_to_sibling_w_in"))
    grad_x, dmix, from_chips["w_in"] = _inproj_bwd(dz, w_in, x, dx1, vec["mix_norm"], riders=[_chips_rider(pb_in)])

    small_acc = dict(mix_norm=dmix, conv_w=dcw, conv_norm=dcn, w_af=dwaf, b_af=dbaf, w_ab=dwab, b_ab=dbab,
                     gla_norm=dgn, xa_norm=dxa, mem_norm=dmemn, mlp_norm=dmlp, final_norm=dfinal)
    return loss8, grad_x, small_acc, own, from_chips


def _place():
    return lax.axis_index("x"), lax.axis_index("y"), lax.axis_index("c")


class _Rider:
    def __init__(self, arrays, out_shape, scratch, start, finish):
        self.arrays, self.out_shape, self.scratch, self.start, self.finish = arrays, out_shape, scratch, start, finish


def _gather_rider(blks):
    n = len(blks)

    def plan(in_refs, out_refs, sems):
        send_sems, recv_sems, local_sems = sems
        x, y, c = _place()
        me, sibling = (x, y, c), (x, y, 1 - c)
        chips = [(1 - x, y, c), (x, 1 - y, c), (1 - x, 1 - y, c)]

        def copy(a, k, block, to, own=False):
            px, py, pc = block
            dst = out_refs[a].at[4 * px + 2 * py + pc]
            return pltpu.make_async_remote_copy(
                src_ref=in_refs[a] if own else dst, dst_ref=dst, send_sem=send_sems.at[k, a],
                recv_sem=recv_sems.at[k, a], device_id=to, device_id_type=MESH)

        def local(a):
            return pltpu.make_async_copy(in_refs[a], out_refs[a].at[4 * x + 2 * y + c], local_sems.at[a])

        def own_sends(a):
            return [copy(a, 0, me, sibling, own=True)] + [copy(a, 1 + j, me, chip, own=True)
                                                          for j, chip in enumerate(chips)]

        return copy, local, own_sends, me, sibling, chips

    def start(in_refs, out_refs, sems):
        _, local, own_sends, _, _, _ = plan(in_refs, out_refs, sems)
        for a in range(n):
            local(a).start()
            for cp in own_sends(a):
                cp.start()

    def finish(in_refs, out_refs, sems):
        copy, local, own_sends, me, sibling, chips = plan(in_refs, out_refs, sems)
        for j, chip in enumerate(chips):
            for a in range(n):
                copy(a, 1 + j, chip, me).wait_recv()
                copy(a, 4 + j, chip, sibling).start()
        for a in range(n):
            copy(a, 0, sibling, me).wait_recv()
            for j, (px, py, pc) in enumerate(chips):
                copy(a, 4 + j, (px, py, 1 - pc), me).wait_recv()
            for cp in own_sends(a) + [copy(a, 4 + j, chip, sibling) for j, chip in enumerate(chips)]:
                cp.wait_send()
            local(a).wait()

    return _Rider(blks, [jax.ShapeDtypeStruct((NDEV,) + b.shape, b.dtype) for b in blks],
                  [pltpu.SemaphoreType.DMA((7, n)), pltpu.SemaphoreType.DMA((7, n)), pltpu.SemaphoreType.DMA((n,))],
                  start, finish)


def _sibling_rider(g4s):
    n = len(g4s)

    def copies(in_refs, out_refs, sems):
        send_sems, recv_sems = sems
        x, y, c = _place()
        return [pltpu.make_async_remote_copy(
            src_ref=in_refs[a].at[k, 1 - c], dst_ref=out_refs[a].at[k], send_sem=send_sems.at[k, a],
            recv_sem=recv_sems.at[k, a], device_id=(x, y, 1 - c), device_id_type=MESH)
            for a in range(n) for k in range(4)]

    def start(in_refs, out_refs, sems):
        for cp in copies(in_refs, out_refs, sems):
            cp.start()

    def finish(in_refs, out_refs, sems):
        for cp in copies(in_refs, out_refs, sems):
            cp.wait()

    return _Rider(g4s, [jax.ShapeDtypeStruct((4,) + g.shape[2:], g.dtype) for g in g4s],
                  [pltpu.SemaphoreType.DMA((4, n)), pltpu.SemaphoreType.DMA((4, n))], start, finish)


def _chips_rider(pbs):
    n = len(pbs)

    def copies(in_refs, out_refs, sems):
        send_sems, recv_sems = sems
        x, y, c = _place()
        peers = [(1 - x, y), (x, 1 - y), (1 - x, 1 - y)]
        return [pltpu.make_async_remote_copy(
            src_ref=in_refs[a].at[2 * px + py], dst_ref=out_refs[a].at[k], send_sem=send_sems.at[k, a],
            recv_sem=recv_sems.at[k, a], device_id=(px, py, c), device_id_type=MESH)
            for a in range(n) for k, (px, py) in enumerate(peers)]

    def start(in_refs, out_refs, sems):
        for cp in copies(in_refs, out_refs, sems):
            cp.start()

    def finish(in_refs, out_refs, sems):
        for cp in copies(in_refs, out_refs, sems):
            cp.wait()

    return _Rider(pbs, [jax.ShapeDtypeStruct((3,) + p.shape[1:], p.dtype) for p in pbs],
                  [pltpu.SemaphoreType.DMA((3, n)), pltpu.SemaphoreType.DMA((3, n))], start, finish)


def _exchange(rider, name):
    n_in, n_out = len(rider.arrays), len(rider.out_shape)

    def body(*refs):
        ins, outs, sems = refs[:n_in], refs[n_in:n_in + n_out], refs[n_in + n_out:]
        rider.start(ins, outs, sems)
        rider.finish(ins, outs, sems)

    hbm = pl.BlockSpec(memory_space=pltpu.HBM)
    return pl.pallas_call(body, name=name, out_shape=rider.out_shape, in_specs=[hbm] * n_in,
                          out_specs=[hbm] * n_out, scratch_shapes=rider.scratch)(*rider.arrays)


def _rs_pair_sum(place, g4, r1, name):
    rows, cols = g4.shape[2:]
    tr = min(rows, 512)

    def body(pl_ref, g_ref, r_ref, pb_ref, own_ref):
        s = g_ref[0, 0] + r_ref[0]
        pb_ref[0] = s.astype(BF16)

        @pl.when(pl.program_id(1) == pl_ref[0])
        def _():
            own_ref[...] = s

    grid_spec = pltpu.PrefetchScalarGridSpec(
        num_scalar_prefetch=1, grid=(rows // tr, 4),
        in_specs=[pl.BlockSpec((1, 1, tr, cols), lambda r, k, p: (k, p[1], r, 0)),
                  pl.BlockSpec((1, tr, cols), lambda r, k, p: (k, r, 0))],
        out_specs=[pl.BlockSpec((1, tr, cols), lambda r, k, p: (k, r, 0)),
                   pl.BlockSpec((tr, cols), lambda r, k, p: (r, 0))])
    return pl.pallas_call(
        body, name=name, grid_spec=grid_spec,
        out_shape=[jax.ShapeDtypeStruct((4, rows, cols), BF16), jax.ShapeDtypeStruct((rows, cols), F32)],
        compiler_params=_cparams(("arbitrary", "arbitrary")))(place, g4, r1)


PACK_ROWS = 32
VEC_ROW = {"mix_norm": 0, "conv_norm": 1, "b_af": 2, "b_ab": 3, "gla_norm": 4, "xa_norm": 5, "mem_norm": 6,
           "mlp_norm": 7, "final_norm": 8}
LOSS_ROW, MAT_ROW = 9, 16
MAT_LANE = {"w_af": 0, "w_ab": GK, "conv_w": 2 * GK}
MAT_SRC_ROW = {"w_af": 0, "w_ab": LR, "conv_w": 0}


def _small_step(acc, loss8, ws, ms, vs):
    names = list(SMALL)
    n = len(names)
    widths = {k: ws[k].shape[-1] for k in names}

    def body(*refs):
        acc_refs = dict(zip(names, refs[:n]))
        loss_ref = refs[n]
        w_refs, m_refs, v_refs = [dict(zip(names, refs[n + 1 + q * n:n + 1 + (q + 1) * n])) for q in range(3)]
        outs = refs[1 + 4 * n:2 + 8 * n]
        lossout_ref = outs[0]
        g_out, d_out, m_out, v_out = [dict(zip(names, outs[1 + q * n:1 + (q + 1) * n])) for q in range(4)]
        pk, all_ref, tot, cut, send_sems, recv_sems, local_sem = refs[2 + 8 * n:]

        pk[...] = jnp.zeros_like(pk)
        for k, row in VEC_ROW.items():
            if k == "gla_norm":
                g = functools.reduce(lambda a, b: a + b, [acc_refs[k][pl.ds(0, 1), pl.ds(h * 128, 128)]
                                                          for h in range(NH)])
            else:
                g = acc_refs[k][pl.ds(0, 1), :]
            pk[pl.ds(row, 1), pl.ds(0, widths[k])] = g
        pk[pl.ds(LOSS_ROW, 1), pl.ds(0, 128)] = loss_ref[pl.ds(0, 1), :]
        for k, lane in MAT_LANE.items():
            rows, cols = (3, CW) if k == "conv_w" else (LR, GK)
            pk[pl.ds(MAT_ROW, rows), pl.ds(lane, cols)] = acc_refs[k][pl.ds(MAT_SRC_ROW[k], rows), :]

        x, y, c = _place()
        me, sibling = (x, y, c), (x, y, 1 - c)
        chips = [(1 - x, y, c), (x, 1 - y, c), (1 - x, 1 - y, c)]

        def copy(k, block, to, own=False):
            px, py, pc = block
            dst = all_ref.at[4 * px + 2 * py + pc]
            return pltpu.make_async_remote_copy(
                src_ref=pk if own else dst, dst_ref=dst, send_sem=send_sems.at[k], recv_sem=recv_sems.at[k],
                device_id=to, device_id_type=MESH)

        mine = pltpu.make_async_copy(pk, all_ref.at[4 * x + 2 * y + c], local_sem)
        mine.start()
        first = [copy(0, me, sibling, own=True)] + [copy(1 + j, me, chip, own=True) for j, chip in enumerate(chips)]
        for cp in first:
            cp.start()
        passed = [copy(4 + j, chip, sibling) for j, chip in enumerate(chips)]
        for j, chip in enumerate(chips):
            copy(1 + j, chip, me).wait_recv()
            passed[j].start()
        copy(0, sibling, me).wait_recv()
        for j, (px, py, pc) in enumerate(chips):
            copy(4 + j, (px, py, 1 - pc), me).wait_recv()
        for cp in first + passed:
            cp.wait_send()
        mine.wait()
        total = all_ref[0]
        for d in range(1, NDEV):
            total = total + all_ref[d]
        tot[...] = total
        lossout_ref[...] = jnp.broadcast_to(tot[pl.ds(LOSS_ROW, 1), pl.ds(0, 128)], lossout_ref.shape)

        dev = 4 * x + 2 * y + c
        for k in names:
            if k in VEC_ROW:
                g = tot[pl.ds(VEC_ROW[k], 1), pl.ds(0, widths[k])]
            else:
                rows, cols = (3, CW) if k == "conv_w" else (LR, GK)
                wd = widths[k]
                sel = jnp.where(_iota((cols, wd), 0) == dev * wd + _iota((cols, wd), 1), 1.0, 0.0).astype(BF16)
                cut[:, pl.ds(0, wd)] = _dot_exact_rhs(tot[pl.ds(MAT_ROW, LR), pl.ds(MAT_LANE[k], cols)], sel, 3)
                g = cut[pl.ds(0, rows), pl.ds(0, wd)]
            g_out[k][...] = g
            d_out[k][...], m_out[k][...], v_out[k][...] = _adamw_math(w_refs[k][...], g, m_refs[k][...],
                                                                       v_refs[k][...])

    shapes = [jax.ShapeDtypeStruct(ws[k].shape, F32) for k in names]
    res = pl.pallas_call(
        body, name="small_step", out_shape=[jax.ShapeDtypeStruct((8, 128), F32)] + shapes * 4,
        scratch_shapes=[pltpu.VMEM((PACK_ROWS, D), F32), pltpu.VMEM((NDEV, PACK_ROWS, D), F32),
                        pltpu.VMEM((PACK_ROWS, D), F32), pltpu.VMEM((LR, 128), F32),
                        pltpu.SemaphoreType.DMA((7,)), pltpu.SemaphoreType.DMA((7,)), pltpu.SemaphoreType.DMA],
    )(*[acc[k] for k in names], loss8, *[ws[k] for k in names], *[ms[k] for k in names], *[vs[k] for k in names])
    out = {k: tuple(res[1 + q * n + i] for q in range(4)) for i, k in enumerate(names)}
    return res[0], out


def _adamw_math(w, g, m, v):
    m = ADAM_B1 * m + (1.0 - ADAM_B1) * g
    v = ADAM_B2 * v + (1.0 - ADAM_B2) * (g * g)
    m_hat = m / (1.0 - ADAM_B1 ** ADAM_STEP)
    v_hat = v / (1.0 - ADAM_B2 ** ADAM_STEP)
    delta = -ADAM_LR * (m_hat / (jnp.sqrt(v_hat) + ADAM_EPS) + ADAM_WD * w)
    return delta, m, v


def _adamw(w, m, v, own, r2, name):
    _, r, c = w.shape
    tr = 256 if r % 256 == 0 else r

    def body(w_ref, m_ref, v_ref, o_ref, r_ref, g_ref, d_ref, nm_ref, nv_ref):
        g = ((o_ref[...] + r_ref[0].astype(F32)) + r_ref[1].astype(F32)) + r_ref[2].astype(F32)
        g_ref[...] = g
        d_ref[...], nm_ref[...], nv_ref[...] = _adamw_math(w_ref[...], g, m_ref[...], v_ref[...])

    spec = pl.BlockSpec((None, tr, c), lambda i: (0, i, 0))
    return pl.pallas_call(
        body, name=name, grid=(r // tr,),
        in_specs=[spec, spec, spec, pl.BlockSpec((tr, c), lambda i: (i, 0)),
                  pl.BlockSpec((3, tr, c), lambda i: (0, i, 0))],
        out_specs=[spec] * 4, out_shape=[jax.ShapeDtypeStruct((1, r, c), F32)] * 4,
        compiler_params=_cparams(("arbitrary",)))(w, m, v, own, r2)


MATS = ("w_in", "w_out", "w_xq", "w_xo", "w_xkv", "w_up", "w_down")
SMALL = ("mix_norm", "conv_w", "conv_norm", "w_af", "b_af", "w_ab", "b_ab", "gla_norm", "xa_norm", "mem_norm",
         "mlp_norm", "final_norm")
WEIGHTS = ("mix_norm", "w_in", "conv_w", "conv_norm", "w_af", "b_af", "w_ab", "b_ab", "gla_norm", "w_out", "xa_norm",
           "mem_norm", "w_xq", "w_xkv", "w_xo", "mlp_norm", "w_up", "w_down", "final_norm")
SMALL_SHARDED = {"conv_w": (3, 64), "w_af": (LR, 32), "w_ab": (LR, 32)}
SMALL_PACK_ROWS = 16


def kernel(x, mem, mix_norm, w_in, conv_w, conv_norm, w_af, b_af, w_ab, b_ab, gla_norm, w_out, xa_norm, mem_norm, w_xq, w_xkv, w_xo, mlp_norm, w_up, w_down, final_norm, loss_target, m_mix_norm, m_w_in, m_conv_w, m_conv_norm, m_w_af, m_b_af, m_w_ab, m_b_ab, m_gla_norm, m_w_out, m_xa_norm, m_mem_norm, m_w_xq, m_w_xkv, m_w_xo, m_mlp_norm, m_w_up, m_w_down, m_final_norm, v_mix_norm, v_w_in, v_conv_w, v_conv_norm, v_w_af, v_b_af, v_w_ab, v_b_ab, v_gla_norm, v_w_out, v_xa_norm, v_mem_norm, v_w_xq, v_w_xkv, v_w_xo, v_mlp_norm, v_w_up, v_w_down, v_final_norm):
    w = dict(mix_norm=mix_norm, w_in=w_in, conv_w=conv_w, conv_norm=conv_norm, w_af=w_af, b_af=b_af, w_ab=w_ab,
             b_ab=b_ab, gla_norm=gla_norm, w_out=w_out, xa_norm=xa_norm, mem_norm=mem_norm, w_xq=w_xq, w_xkv=w_xkv,
             w_xo=w_xo, mlp_norm=mlp_norm, w_up=w_up, w_down=w_down, final_norm=final_norm)
    mom = dict(mix_norm=m_mix_norm, w_in=m_w_in, conv_w=m_conv_w, conv_norm=m_conv_norm, w_af=m_w_af, b_af=m_b_af,
               w_ab=m_w_ab, b_ab=m_b_ab, gla_norm=m_gla_norm, w_out=m_w_out, xa_norm=m_xa_norm, mem_norm=m_mem_norm,
               w_xq=m_w_xq, w_xkv=m_w_xkv, w_xo=m_w_xo, mlp_norm=m_mlp_norm, w_up=m_w_up, w_down=m_w_down,
               final_norm=m_final_norm)
    var = dict(mix_norm=v_mix_norm, w_in=v_w_in, conv_w=v_conv_w, conv_norm=v_conv_norm, w_af=v_w_af, b_af=v_b_af,
               w_ab=v_w_ab, b_ab=v_b_ab, gla_norm=v_gla_norm, w_out=v_w_out, xa_norm=v_xa_norm, mem_norm=v_mem_norm,
               w_xq=v_w_xq, w_xkv=v_w_xkv, w_xo=v_w_xo, mlp_norm=v_mlp_norm, w_up=v_w_up, w_down=v_w_down,
               final_norm=v_final_norm)
    xi, yi, ci = _place()
    two_d = lambda a: a.reshape(a.shape[-2:]) if a.ndim == 3 else a.reshape(1, a.shape[-1])

    small = jnp.concatenate([w[n].reshape(-1) for n in SMALL_SHARDED])
    small = jnp.pad(small, (0, SMALL_PACK_ROWS * 128 - small.shape[0])).reshape(SMALL_PACK_ROWS, 128)
    shard = {n: two_d(w[n]).astype(BF16) for n in MATS}
    for n in ("w_in", "w_up"):
        shard[n] = shard[n].T
    vec = {n: two_d(w[n]) for n in SMALL if n not in SMALL_SHARDED}
    place = jnp.stack([2 * xi + yi, ci]).astype(jnp.int32)
    loss8, grad_x, small_acc, own, from_chips = _step(x[0], mem[0], loss_target[0], shard, small, vec, place)

    loss_all, small_out = _small_step(small_acc, loss8, *[{n: two_d(d[n]) for n in SMALL} for d in (w, mom, var)])
    loss = loss_all[0, 0]

    out_g, out_d, out_m, out_v = {}, {}, {}, {}
    own["w_up"], from_chips["w_up"] = own["w_up"].T, from_chips["w_up"].transpose(0, 2, 1)
    for n in MATS:
        if n == "w_in":
            res = _adamw(*[a.transpose(0, 2, 1) for a in (w[n], mom[n], var[n])], own[n], from_chips[n], "adamw_" + n)
            res = [a.transpose(0, 2, 1) for a in res]
        else:
            res = _adamw(w[n], mom[n], var[n], own[n], from_chips[n], "adamw_" + n)
        out_g[n], out_d[n], out_m[n], out_v[n] = res
    for n in SMALL:
        out_g[n], out_d[n], out_m[n], out_v[n] = [a.reshape(w[n].shape) for a in small_out[n]]

    return (loss, grad_x[None], *[out_g[n] for n in WEIGHTS], *[out_d[n] for n in WEIGHTS],
            *[out_m[n] for n in WEIGHTS], *[out_v[n] for n in WEIGHTS])
```

```python
import functools
import itertools

import jax
import jax.numpy as jnp
from jax import lax
from jax.experimental import pallas as pl
from jax.experimental.pallas import tpu as pltpu

F32 = jnp.float32
BF16 = jnp.bfloat16

D = 1024
CW = 512
GK = 256
GV = 512
NH = 4
CH = 64
LR = 16
NMEM = 256
XD = 256
FF = 4096
ZW = 3104
ZC = 3200
EPS = 1e-6
NDEV = 8

ZB_CB, ZB_CC, ZB_CU, ZB_V, ZB_G = 0, 1, 2, 4, 5
ZB_Q, ZB_K = 6, 7
ZB_LR = 24

TM = 512
TM_MLP = 256
TF = 512
TB = 512
TB_BWD = 512
TT = 2048
VMEM_LIMIT = 56 * 1024 * 1024

ADAM_LR, ADAM_B1, ADAM_B2, ADAM_EPS, ADAM_WD, ADAM_STEP = 0.001, 0.9, 0.999, 1e-08, 0.01, 10

XKV_SHARD = 2 * D // NDEV

MESH = pl.DeviceIdType.MESH


def _cparams(sem):
    return pltpu.CompilerParams(dimension_semantics=sem, vmem_limit_bytes=VMEM_LIMIT)


def _call(body, name, grid, in_specs, out_specs, out_shape, scratch, args, riders=()):
    n_in, n_out, n_scr = len(in_specs), len(out_specs), len(scratch)
    counts = [(len(r.arrays), len(r.out_shape), len(r.scratch)) for r in riders]

    def take(refs, pos, sizes):
        groups = []
        for size in sizes:
            groups.append(refs[pos:pos + size])
            pos += size
        return groups, pos

    def wrapped(*refs):
        ins, pos = refs[:n_in], n_in
        r_ins, pos = take(refs, pos, [c[0] for c in counts])
        outs, pos = refs[pos:pos + n_out], pos + n_out
        r_outs, pos = take(refs, pos, [c[1] for c in counts])
        scr, pos = refs[pos:pos + n_scr], pos + n_scr
        r_scr, pos = take(refs, pos, [c[2] for c in counts])
        ids = [pl.program_id(d) for d in range(len(grid))]
        first = functools.reduce(lambda a, b: a & b, [i == 0 for i in ids])
        last = functools.reduce(lambda a, b: a & b, [i == g - 1 for i, g in zip(ids, grid)])

        @pl.when(first)
        def _():
            for r, a, b, c in zip(riders, r_ins, r_outs, r_scr):
                r.start(a, b, c)

        body(*ins, *outs, *scr)

        @pl.when(last)
        def _():
            for r, a, b, c in zip(riders, r_ins, r_outs, r_scr):
                r.finish(a, b, c)

    hbm = pl.BlockSpec(memory_space=pltpu.HBM)
    r_args = [a for r in riders for a in r.arrays]
    r_shapes = [s for r in riders for s in r.out_shape]
    return pl.pallas_call(
        wrapped if riders else body, name=name, grid=grid, in_specs=list(in_specs) + [hbm] * len(r_args),
        out_specs=list(out_specs) + [hbm] * len(r_shapes), out_shape=list(out_shape) + r_shapes,
        scratch_shapes=list(scratch) + [s for r in riders for s in r.scratch],
        compiler_params=_cparams(("arbitrary",) * len(grid)))(*args, *r_args)


def _dot(a, b):
    return jnp.dot(a.astype(BF16), b.astype(BF16), preferred_element_type=F32)


def _dot_nt(a, b):
    return lax.dot_general(a.astype(BF16), b.astype(BF16), (((1,), (1,)), ((), ())), preferred_element_type=F32)


def _dot_tn(a, b):
    return lax.dot_general(a.astype(BF16), b.astype(BF16), (((0,), (0,)), ((), ())), preferred_element_type=F32)


def _split(x, n):
    parts = []
    for _ in range(n):
        p = x.astype(BF16)
        parts.append(p)
        x = x - p.astype(F32)
    return parts


def _dot_exact_lhs(m, x, n):
    return functools.reduce(lambda a, b: a + b, [jnp.dot(m, p, preferred_element_type=F32) for p in _split(x, n)])


def _dot_exact_rhs(x, m, n):
    return functools.reduce(lambda a, b: a + b, [jnp.dot(p, m, preferred_element_type=F32) for p in _split(x, n)])


def _rms(x, g):
    r = lax.rsqrt(jnp.mean(x * x, axis=-1, keepdims=True) + EPS)
    return x * r * g, r


def _rms_bwd(x, r, g, dy):
    xr = x * r
    u = dy * g
    dx = r * (u - xr * jnp.mean(u * xr, axis=-1, keepdims=True))
    return dx, jnp.sum(dy * xr, axis=0, keepdims=True)


def _iota(shape, dim):
    return lax.broadcasted_iota(jnp.int32, shape, dim)


def _sigmoid(x):
    return 1.0 / (1.0 + jnp.exp(-x))


def _acc_rows(ref, row):
    ref[...] += jnp.broadcast_to(row, ref.shape)


def _inproj(x, g, w_t, riders=()):
    t = x.shape[0]
    tm = min(TM, t)

    def body(x_ref, g_ref, w_ref, z_ref, h_ref):
        h, _ = _rms(x_ref[...], g_ref[...])
        hb = h.astype(BF16)
        h_ref[...] = hb
        z_ref[...] = _dot_nt(hb, w_ref[...])

    return _call(
        body, "inproj", (t // tm,),
        [pl.BlockSpec((tm, D), lambda i: (i, 0)), pl.BlockSpec((1, D), lambda i: (0, 0)),
         pl.BlockSpec((ZC, D), lambda i: (0, 0))],
        [pl.BlockSpec((tm, ZC), lambda i: (i, 0)), pl.BlockSpec((tm, D), lambda i: (i, 0))],
        [jax.ShapeDtypeStruct((t, ZC), F32), jax.ShapeDtypeStruct((t, D), BF16)], [], (x, g, w_t), riders)


def _kv_proj(mem, g, w):
    def body(m_ref, g_ref, w_ref, kv_ref, mn_ref):
        mn, _ = _rms(m_ref[...], g_ref[...])
        mb = mn.astype(BF16)
        mn_ref[...] = mb
        for j in range(NDEV):
            kv_ref[:, j * XKV_SHARD:(j + 1) * XKV_SHARD] = jnp.dot(mb, w_ref[j], preferred_element_type=F32)

    return pl.pallas_call(
        body, name="kv_proj",
        out_shape=[jax.ShapeDtypeStruct((NMEM, 2 * D), F32), jax.ShapeDtypeStruct((NMEM, D), BF16)],
        compiler_params=pltpu.CompilerParams(vmem_limit_bytes=VMEM_LIMIT))(mem, g, w)


def _softmax_head(qb, kb):
    s = _dot_nt(qb, kb) * (1.0 / 16.0)
    e = jnp.exp(s - jnp.max(s, axis=-1, keepdims=True))
    return e / jnp.sum(e, axis=-1, keepdims=True)


def _attn_fwd(x, yb, w_out, g, w_xq, kb, vb, w_xo):
    t = x.shape[0]
    tm = min(TM, t)

    def body(x_ref, y_ref, wo_ref, g_ref, wq_ref, k_ref, v_ref, wx_ref, x1_ref, x2_ref, xn_ref, q_ref, a_ref):
        x1 = x_ref[...] + jnp.dot(y_ref[...], wo_ref[...], preferred_element_type=F32)
        x1_ref[...] = x1
        xn, _ = _rms(x1, g_ref[...])
        xb = xn.astype(BF16)
        xn_ref[...] = xb
        qb = jnp.dot(xb, wq_ref[...], preferred_element_type=F32).astype(BF16)
        q_ref[...] = qb
        for h in range(NH):
            hs = slice(h * XD, (h + 1) * XD)
            p = _softmax_head(qb[:, hs], k_ref[:, hs])
            a_ref[:, hs] = _dot(p, v_ref[:, hs]).astype(BF16)
        x2_ref[...] = x1 + jnp.dot(a_ref[...], wx_ref[...], preferred_element_type=F32)

    tok = lambda i: (i, 0)
    full = lambda i: (0, 0)
    return pl.pallas_call(
        body, name="attn_fwd", grid=(t // tm,),
        in_specs=[pl.BlockSpec((tm, D), tok), pl.BlockSpec((tm, D), tok), pl.BlockSpec((D, D), full),
                  pl.BlockSpec((1, D), full), pl.BlockSpec((D, D), full), pl.BlockSpec((NMEM, D), full),
                  pl.BlockSpec((NMEM, D), full), pl.BlockSpec((D, D), full)],
        out_specs=[pl.BlockSpec((tm, D), tok)] * 5,
        out_shape=[jax.ShapeDtypeStruct((t, D), F32), jax.ShapeDtypeStruct((t, D), F32),
                   jax.ShapeDtypeStruct((t, D), BF16), jax.ShapeDtypeStruct((t, D), BF16),
                   jax.ShapeDtypeStruct((t, D), BF16)],
        compiler_params=_cparams(("arbitrary",)))(x, yb, w_out, g, w_xq, kb, vb, w_xo)


def _mlp_fwd(x2, g, w_up_t, w_down, fg, target):
    t = x2.shape[0]
    tm = min(TM_MLP, t)

    def body(x_ref, g_ref, wu_ref, wd_ref, fg_ref, t_ref, h1_ref, xn_ref, dx_ref, dxb_ref, loss_ref, dfg_ref, ab):
        @pl.when(pl.program_id(0) == 0)
        def _():
            loss_ref[...] = jnp.zeros_like(loss_ref)
            dfg_ref[...] = jnp.zeros_like(dfg_ref)

        x = x_ref[...]
        xn, _ = _rms(x, g_ref[...])
        xnb = xn.astype(BF16)
        xn_ref[...] = xnb
        for q in range(FF // TF):
            cols = slice(q * TF, (q + 1) * TF)
            h1 = _dot_nt(xnb, wu_ref[cols, :])
            h1_ref[:, cols] = h1.astype(BF16)
            hr = jnp.maximum(h1, 0.0)
            ab[:, cols] = (hr * hr).astype(BF16)
        x3 = x + jnp.dot(ab[...], wd_ref[...], preferred_element_type=F32)
        y, r = _rms(x3, fg_ref[...])
        e = y - t_ref[...]
        row = jnp.mean(e * e, axis=-1, keepdims=True)
        _acc_rows(loss_ref, 0.5 * jnp.sum(row, axis=0, keepdims=True))
        dx, dfg = _rms_bwd(x3, r, fg_ref[...], e * (1.0 / D))
        dx_ref[...] = dx
        dxb_ref[...] = dx.astype(BF16)
        _acc_rows(dfg_ref, dfg)

    tok = lambda i: (i, 0)
    full = lambda i: (0, 0)
    once = pl.Buffered(1)
    return pl.pallas_call(
        body, name="mlp_fwd", grid=(t // tm,),
        in_specs=[pl.BlockSpec((tm, D), tok), pl.BlockSpec((1, D), full),
                  pl.BlockSpec((FF, D), full, pipeline_mode=once), pl.BlockSpec((FF, D), full, pipeline_mode=once),
                  pl.BlockSpec((1, D), full), pl.BlockSpec((tm, D), tok)],
        out_specs=[pl.BlockSpec((tm, FF), tok), pl.BlockSpec((tm, D), tok), pl.BlockSpec((tm, D), tok),
                   pl.BlockSpec((tm, D), tok), pl.BlockSpec((8, 128), full), pl.BlockSpec((8, D), full)],
        out_shape=[jax.ShapeDtypeStruct((t, FF), BF16), jax.ShapeDtypeStruct((t, D), BF16),
                   jax.ShapeDtypeStruct((t, D), F32), jax.ShapeDtypeStruct((t, D), BF16),
                   jax.ShapeDtypeStruct((8, 128), F32), jax.ShapeDtypeStruct((8, D), F32)],
        scratch_shapes=[pltpu.VMEM((tm, FF), BF16)],
        compiler_params=_cparams(("arbitrary",)))(x2, g, w_up_t, w_down, fg, target)


def _mlp_bwd(dx3, dx3b, h1b, w_down, w_up_t, x2, g):
    t = x2.shape[0]
    tm = min(TM_MLP, t)

    def body(dx_ref, dxb_ref, h1_ref, wd_ref, wu_ref, x_ref, g_ref, a_ref, dh_ref, dx2_ref, dx2b_ref, dg_ref):
        @pl.when(pl.program_id(0) == 0)
        def _():
            dg_ref[...] = jnp.zeros_like(dg_ref)

        for q in range(FF // TF):
            cols = slice(q * TF, (q + 1) * TF)
            hr = jnp.maximum(h1_ref[:, cols].astype(F32), 0.0)
            da = _dot_nt(dxb_ref[...], wd_ref[cols, :])
            a_ref[:, cols] = (hr * hr).astype(BF16)
            dh_ref[:, cols] = (da * 2.0 * hr).astype(BF16)
        dxn = jnp.dot(dh_ref[...], wu_ref[...], preferred_element_type=F32)
        x = x_ref[...]
        r = lax.rsqrt(jnp.mean(x * x, axis=-1, keepdims=True) + EPS)
        dx, dg = _rms_bwd(x, r, g_ref[...], dxn)
        dx2 = dx_ref[...] + dx
        dx2_ref[...] = dx2
        dx2b_ref[...] = dx2.astype(BF16)
        _acc_rows(dg_ref, dg)

    tok = lambda i: (i, 0)
    full = lambda i: (0, 0)
    once = pl.Buffered(1)
    return pl.pallas_call(
        body, name="mlp_bwd", grid=(t // tm,),
        in_specs=[pl.BlockSpec((tm, D), tok), pl.BlockSpec((tm, D), tok), pl.BlockSpec((tm, FF), tok),
                  pl.BlockSpec((FF, D), full, pipeline_mode=once), pl.BlockSpec((FF, D), full, pipeline_mode=once),
                  pl.BlockSpec((tm, D), tok), pl.BlockSpec((1, D), full)],
        out_specs=[pl.BlockSpec((tm, FF), tok), pl.BlockSpec((tm, FF), tok), pl.BlockSpec((tm, D), tok),
                   pl.BlockSpec((tm, D), tok), pl.BlockSpec((8, D), full)],
        out_shape=[jax.ShapeDtypeStruct((t, FF), BF16), jax.ShapeDtypeStruct((t, FF), BF16),
                   jax.ShapeDtypeStruct((t, D), F32), jax.ShapeDtypeStruct((t, D), BF16),
                   jax.ShapeDtypeStruct((8, D), F32)],
        compiler_params=_cparams(("arbitrary",)))(dx3, dx3b, h1b, w_down, w_up_t, x2, g)


def _attn_bwd(x1, dx2, dx2b, qb, kb, vb, w_xo, w_xq, w_out, g):
    t = x1.shape[0]
    tm = min(TM, t)

    def body(x_ref, dx2_ref, dx2b_ref, q_ref, k_ref, v_ref, wx_ref, wq_ref, wo_ref, g_ref,
             dx1_ref, dx1b_ref, dy_ref, dq_ref, dkv_ref, dg_ref):
        @pl.when(pl.program_id(0) == 0)
        def _():
            dkv_ref[...] = jnp.zeros_like(dkv_ref)
            dg_ref[...] = jnp.zeros_like(dg_ref)

        datt = _dot_nt(dx2b_ref[...], wx_ref[...]).astype(BF16)
        for h in range(NH):
            hs = slice(h * XD, (h + 1) * XD)
            q_h, k_h, v_h, da_h = q_ref[:, hs], k_ref[:, hs], v_ref[:, hs], datt[:, hs]
            p = _softmax_head(q_h, k_h)
            dp = _dot_nt(da_h, v_h)
            ds = (p * (dp - jnp.sum(dp * p, axis=-1, keepdims=True)) * (1.0 / 16.0)).astype(BF16)
            dq_ref[:, hs] = _dot(ds, k_h).astype(BF16)
            dkv_ref[:, hs] += _dot_tn(ds, q_h)
            dkv_ref[:, D + h * XD:D + (h + 1) * XD] += _dot_tn(p, da_h)
        dxn = _dot_nt(dq_ref[...], wq_ref[...])
        x = x_ref[...]
        r = lax.rsqrt(jnp.mean(x * x, axis=-1, keepdims=True) + EPS)
        dx, dg = _rms_bwd(x, r, g_ref[...], dxn)
        dx1 = dx2_ref[...] + dx
        dx1_ref[...] = dx1
        dx1b = dx1.astype(BF16)
        dx1b_ref[...] = dx1b
        dy_ref[...] = _dot_nt(dx1b, wo_ref[...])
        _acc_rows(dg_ref, dg)

    tok = lambda i: (i, 0)
    full = lambda i: (0, 0)
    return pl.pallas_call(
        body, name="attn_bwd", grid=(t // tm,),
        in_specs=[pl.BlockSpec((tm, D), tok), pl.BlockSpec((tm, D), tok), pl.BlockSpec((tm, D), tok),
                  pl.BlockSpec((tm, D), tok), pl.BlockSpec((NMEM, D), full), pl.BlockSpec((NMEM, D), full),
                  pl.BlockSpec((D, D), full), pl.BlockSpec((D, D), full), pl.BlockSpec((D, D), full),
                  pl.BlockSpec((1, D), full)],
        out_specs=[pl.BlockSpec((tm, D), tok), pl.BlockSpec((tm, D), tok), pl.BlockSpec((tm, D), tok),
                   pl.BlockSpec((tm, D), tok), pl.BlockSpec((NMEM, 2 * D), full), pl.BlockSpec((8, D), full)],
        out_shape=[jax.ShapeDtypeStruct((t, D), F32), jax.ShapeDtypeStruct((t, D), BF16),
                   jax.ShapeDtypeStruct((t, D), F32), jax.ShapeDtypeStruct((t, D), BF16),
                   jax.ShapeDtypeStruct((NMEM, 2 * D), F32), jax.ShapeDtypeStruct((8, D), F32)],
        compiler_params=_cparams(("arbitrary",)))(x1, dx2, dx2b, qb, kb, vb, w_xo, w_xq, w_out, g)


def _kv_bwd(dkv, memn, mem, g, w):
    def body(dkv_ref, mn_ref, m_ref, g_ref, w_ref, dw_ref, dg_ref):
        dkvb = dkv_ref[...].astype(BF16)
        dmn = jnp.zeros((NMEM, D), F32)
        for j in range(NDEV):
            cols = slice(j * XKV_SHARD, (j + 1) * XKV_SHARD)
            dw_ref[j] = _dot_tn(mn_ref[...], dkvb[:, cols])
            dmn += _dot_nt(dkvb[:, cols], w_ref[j])
        m = m_ref[...]
        r = lax.rsqrt(jnp.mean(m * m, axis=-1, keepdims=True) + EPS)
        dg_ref[...] = jnp.broadcast_to(jnp.sum(dmn * m * r, axis=0, keepdims=True), dg_ref.shape)

    return pl.pallas_call(
        body, name="kv_bwd",
        out_shape=[jax.ShapeDtypeStruct((NDEV, D, XKV_SHARD), F32), jax.ShapeDtypeStruct((8, D), F32)],
        compiler_params=pltpu.CompilerParams(vmem_limit_bytes=VMEM_LIMIT))(dkv, memn, mem, g, w)


def _inproj_bwd(dz, w_t, x, dx1, g, riders=()):
    t = x.shape[0]
    tm = min(TM, t)

    def body(dz_ref, w_ref, x_ref, dx1_ref, g_ref, gx_ref, dg_ref):
        @pl.when(pl.program_id(0) == 0)
        def _():
            dg_ref[...] = jnp.zeros_like(dg_ref)

        dh = jnp.dot(dz_ref[...], w_ref[...], preferred_element_type=F32)
        x = x_ref[...]
        r = lax.rsqrt(jnp.mean(x * x, axis=-1, keepdims=True) + EPS)
        dx, dg = _rms_bwd(x, r, g_ref[...], dh)
        gx_ref[...] = dx1_ref[...] + dx
        _acc_rows(dg_ref, dg)

    tok = lambda i: (i, 0)
    full = lambda i: (0, 0)
    return _call(
        body, "inproj_bwd", (t // tm,),
        [pl.BlockSpec((tm, ZC), tok), pl.BlockSpec((ZC, D), full), pl.BlockSpec((tm, D), tok),
         pl.BlockSpec((tm, D), tok), pl.BlockSpec((1, D), full)],
        [pl.BlockSpec((tm, D), tok), pl.BlockSpec((8, D), full)],
        [jax.ShapeDtypeStruct((t, D), F32), jax.ShapeDtypeStruct((8, D), F32)], [], (dz, w_t, x, dx1, g), riders)


def _matmul_tn(a, b, name, rows=None, riders=()):
    t, k = a.shape
    n = b.shape[1]
    tk, tn = [1024 if size % 1024 == 0 else 640 for size in (k, n)]
    tt = min(TT, t)
    rows = rows or k

    def body(a_ref, b_ref, o_ref):
        @pl.when(pl.program_id(2) == 0)
        def _():
            o_ref[...] = jnp.zeros_like(o_ref)

        o_ref[...] += _dot_tn(a_ref[...], b_ref[...])

    return _call(
        body, name, (k // tk, n // tn, t // tt),
        [pl.BlockSpec((tt, tk), lambda i, j, s: (s, i)), pl.BlockSpec((tt, tn), lambda i, j, s: (s, j))],
        [pl.BlockSpec((tk, tn), lambda i, j, s: (i, j))], [jax.ShapeDtypeStruct((rows, n), F32)], [], (a, b), riders)


def _lane_head(shape, dim, shift):
    return _iota(shape, dim) >> shift


def _gla_recompute(q_raw, k, lr, wpad, bias, rev, tb):
    pre = _dot(lr, wpad) + bias
    la = (jnp.minimum(pre, 0.0) - jnp.log(1.0 + jnp.exp(-jnp.abs(pre)))) * (1.0 / 16.0)
    r, c = _iota((tb, tb), 0), _iota((tb, tb), 1)
    tri = (c >= r) if rev else (c <= r)
    cum = jnp.where(((r >> 6) == (c >> 6)) & tri, 1.0, 0.0).astype(BF16)
    b = _dot_exact_lhs(cum, la, 3)
    e, ei = jnp.exp(b), jnp.exp(-b)
    qt = (q_raw * 0.125) * e
    kt = k * ei
    return pre, b, e, ei, qt, kt


def _stack_heads(x, shift):
    head = _lane_head(x.shape, 1, shift)
    return jnp.concatenate([jnp.where(head == h, x, 0.0) for h in range(NH)], axis=0).astype(BF16)


def _fold_heads(x, shift):
    head = _lane_head((CH, x.shape[1]), 1, shift)
    return functools.reduce(lambda a, b: a + b,
                            [jnp.where(head == h, x[h * CH:(h + 1) * CH], 0.0) for h in range(NH)])


def _wide_mask(rev):
    r, s = _iota((CH, NH * CH), 0), _iota((CH, NH * CH), 1) & (CH - 1)
    return (s >= r) if rev else (s <= r)


def _state_mask():
    return (_iota((GV, GK), 0) >> 7) == (_iota((GV, GK), 1) >> 6)


def _state_expand(sd):
    head = _lane_head(sd.shape, 1, 6)
    return jnp.concatenate([jnp.where(head == h, sd, 0.0) for h in range(NH)], axis=0)


def _conv_parts(cb, cc, cu, ccp, cup, ccn, cun, cw_ref, first, last, tb):
    h = cc * cu
    hp = jnp.where(first, 0.0, ccp * cup)
    hn = jnp.where(last, 0.0, ccn * cun)
    rows = _iota(h.shape, 0)
    h_m1 = jnp.where(rows == 0, hp, pltpu.roll(h, 1, 0))
    h_p1 = jnp.where(rows == tb - 1, hn, pltpu.roll(h, tb - 1, 0))
    conv = cw_ref[pl.ds(0, 1), :] * h_m1 + cw_ref[pl.ds(1, 1), :] * h + cw_ref[pl.ds(2, 1), :] * h_p1
    return h, h_m1, h_p1, conv


def _group_ones():
    return jnp.where((_iota((CW, CW), 0) >> 6) == (_iota((CW, CW), 1) >> 6), 1.0, 0.0).astype(BF16)


def _head_norm(o):
    ons, rs = [], []
    for h in range(NH):
        slab = o[:, h * 128:(h + 1) * 128]
        r = lax.rsqrt(jnp.mean(slab * slab, axis=-1, keepdims=True) + EPS)
        ons.append(slab * r)
        rs.append(jnp.broadcast_to(r, slab.shape))
    return jnp.concatenate(ons, axis=1), jnp.concatenate(rs, axis=1)


def _zspec(tb, width, blk, jmap):
    return pl.BlockSpec((tb, width), lambda i: (jmap(i), blk))


def _halo_specs(tb, nblk, t, blk, jmap):
    prev = pl.BlockSpec((8, CW), lambda i: (jnp.maximum(jmap(i) * (tb // 8) - 1, 0), blk))
    nxt = pl.BlockSpec((8, CW), lambda i: (jnp.minimum((jmap(i) + 1) * (tb // 8), t // 8 - 1), blk))
    return prev, nxt


def _gla_fwd_block(q_ref, k_ref, v_ref, lr_ref, w_ref, bias_ref, o_ref, sd_ref, st, b_scr, rev, tb):
    nb = tb // CH
    _, b, _, _, qt, kt = _gla_recompute(q_ref[...], k_ref[...], lr_ref[...], w_ref[...], bias_ref[...], rev, tb)
    v = v_ref[...]
    b_scr[...] = b
    yield
    maskw, bd = _wide_mask(rev), _state_mask()
    order = list(reversed(range(nb))) if rev else list(range(nb))
    rows = [slice(c * CH, (c + 1) * CH) for c in range(nb)]
    state = st[...]
    for c in order:
        gdec = jnp.exp(b_scr[pl.ds(c * CH + (0 if rev else CH - 1), 1), :])
        sd_ref[c] = state[0:128] + state[128:256] + state[256:384] + state[384:512]
        a = jnp.where(maskw, _dot_nt(qt[rows[c]], _stack_heads(kt[rows[c]], 6)), 0.0)
        o_ref[pl.ds(c * CH, CH), :] = _dot(a, _stack_heads(v[rows[c]], 7)) + _dot_nt(qt[rows[c]], state)
        state = state * gdec + jnp.where(bd, _dot_tn(v[rows[c]], kt[rows[c]] * gdec), 0.0)
        yield
    st[...] = state
    yield


def _gla_fwd(z, waf_pad, b_af, wab_pad, b_ab, riders=()):
    t = z.shape[0]
    tb = min(TB, t)
    nblk, nb = t // tb, tb // CH
    jmaps = (lambda i: i, lambda i: nblk - 1 - i)

    def body(qf, kf, vf, lrf, qr, kr, vr, lrr, wf, bf, wr, br, of_ref, sdf_ref, or_ref, sdr_ref,
             st_f, st_r, b_f, b_r):
        @pl.when(pl.program_id(0) == 0)
        def _():
            st_f[...] = jnp.zeros_like(st_f)
            st_r[...] = jnp.zeros_like(st_r)

        for _ in zip(_gla_fwd_block(qf, kf, vf, lrf, wf, bf, of_ref, sdf_ref, st_f, b_f, False, tb),
                     _gla_fwd_block(qr, kr, vr, lrr, wr, br, or_ref, sdr_ref, st_r, b_r, True, tb)):
            pass

    full = lambda i: (0, 0)
    zspecs = [s for jm in jmaps for s in (_zspec(tb, GK, ZB_Q, jm), _zspec(tb, GK, ZB_K, jm),
                                         _zspec(tb, GV, ZB_V, jm), _zspec(tb, 128, ZB_LR, jm))]
    wspecs = [pl.BlockSpec((128, GK), full), pl.BlockSpec((1, GK), full)] * 2
    out_specs = [s for jm in jmaps for s in (pl.BlockSpec((tb, GV), lambda i, jm=jm: (jm(i), 0)),
                                             pl.BlockSpec((nb, 128, GK), lambda i, jm=jm: (jm(i), 0, 0)))]
    out_shape = [jax.ShapeDtypeStruct((t, GV), F32), jax.ShapeDtypeStruct((t // CH, 128, GK), F32)] * 2
    scratch = [pltpu.VMEM((GV, GK), F32), pltpu.VMEM((GV, GK), F32), pltpu.VMEM((tb, GK), F32),
               pltpu.VMEM((tb, GK), F32)]
    return _call(body, "gla_fwd", (nblk,), zspecs + wspecs, out_specs, out_shape, scratch,
                 [z] * 8 + [waf_pad, b_af, wab_pad, b_ab], riders)


def _mixer_finish(z, o_f, o_b, conv_w, conv_norm, gla_norm4):
    t = z.shape[0]
    tb = min(TM, t)
    nblk = t // tb
    jmap = lambda i: i

    def body(q_ref, k_ref, v_ref, g_ref, cb_ref, cc_ref, cu_ref, ccp_ref, ccn_ref, cup_ref, cun_ref, of_ref, ob_ref,
             cw_ref, cn_ref, gn_ref, y_ref, opre_ref):
        j = pl.program_id(0)
        v = v_ref[...]
        hsel = jnp.where((_iota((GK, GV), 0) >> 6) == (_iota((GK, GV), 1) >> 7), 1.0, 0.0).astype(BF16)
        sb = _dot_exact_rhs((q_ref[...] * 0.125) * k_ref[...], hsel, 2)
        o_pre = of_ref[...] + ob_ref[...] - sb * v
        opre_ref[...] = o_pre
        on, _ = _head_norm(o_pre)
        g = g_ref[...]
        y_ref[:, CW:] = (on * gn_ref[...] * (g * _sigmoid(g))).astype(BF16)
        cb = cb_ref[...]
        _, _, _, conv = _conv_parts(cb, cc_ref[...], cu_ref[...], ccp_ref[pl.ds(7, 1), :], cup_ref[pl.ds(7, 1), :],
                                    ccn_ref[pl.ds(0, 1), :], cun_ref[pl.ds(0, 1), :], cw_ref, j == 0,
                                    j == nblk - 1, tb)
        yc = cb * conv
        gm = _dot_exact_rhs(yc * yc, _group_ones(), 2) * (1.0 / 64.0)
        y_ref[:, :CW] = (yc * lax.rsqrt(gm + EPS) * cn_ref[...]).astype(BF16)

    full = lambda i: (0, 0)
    tokv = pl.BlockSpec((tb, GV), lambda i: (i, 0))
    ccp, ccn = _halo_specs(tb, nblk, t, ZB_CC, jmap)
    cup, cun = _halo_specs(tb, nblk, t, ZB_CU, jmap)
    in_specs = [_zspec(tb, GK, ZB_Q, jmap), _zspec(tb, GK, ZB_K, jmap), _zspec(tb, GV, ZB_V, jmap),
                _zspec(tb, GV, ZB_G, jmap), _zspec(tb, CW, ZB_CB, jmap), _zspec(tb, CW, ZB_CC, jmap),
                _zspec(tb, CW, ZB_CU, jmap), ccp, ccn, cup, cun, tokv, tokv,
                pl.BlockSpec((3, CW), full), pl.BlockSpec((1, CW), full), pl.BlockSpec((1, GV), full)]
    return pl.pallas_call(
        body, name="mixer_finish", grid=(nblk,), in_specs=in_specs,
        out_specs=[pl.BlockSpec((tb, D), lambda i: (i, 0)), tokv],
        out_shape=[jax.ShapeDtypeStruct((t, D), BF16), jax.ShapeDtypeStruct((t, GV), F32)],
        compiler_params=_cparams(("arbitrary",)))(
            z, z, z, z, z, z, z, z, z, z, z, o_f, o_b, conv_w, conv_norm, gla_norm4)


def _gla_bwd_chunks(do_ref, sd_ref, dst, b_scr, db_scr, dq_ref, dk_ref, dv_ref, qt, kt, e, ei, v, rev, nb):
    maskw, bd = _wide_mask(rev), _state_mask()
    for c in (range(nb) if rev else reversed(range(nb))):
        sl = slice(c * CH, (c + 1) * CH)
        grow = c * CH + (0 if rev else CH - 1)
        gdec = jnp.exp(b_scr[pl.ds(grow, 1), :])
        qt_c, kt_c, v_c, do_c = qt[sl], kt[sl], v[sl], do_ref[pl.ds(c * CH, CH), :]
        s_in = _state_expand(sd_ref[c])
        ds_out = dst[...]
        kbd, vbd = _stack_heads(kt_c, 6), _stack_heads(v_c, 7)
        a = jnp.where(maskw, _dot_nt(qt_c, kbd), 0.0)
        da = jnp.where(maskw, _dot_nt(do_c, vbd), 0.0)
        dv_ref[pl.ds(c * CH, CH), :] = _fold_heads(_dot_tn(a, do_c), 7) + _dot_nt(kt_c * gdec, ds_out)
        da_do = jnp.concatenate([da.astype(BF16), do_c.astype(BF16)], axis=1)
        dqt = _dot(da_do, jnp.concatenate([kbd, s_in.astype(BF16)], axis=0))
        dkh = _dot(v_c, ds_out)
        both = _dot_tn(da_do, qt_c)
        dkt = _fold_heads(both[:NH * CH], 6) + dkh * gdec
        dg = jnp.sum(ds_out * s_in, axis=0, keepdims=True) + jnp.sum(kt_c * dkh, axis=0, keepdims=True)
        db_scr[pl.ds(c * CH, CH), :] = dqt * qt_c - dkt * kt_c
        db_scr[pl.ds(grow, 1), :] += dg * gdec
        dq_ref[pl.ds(c * CH, CH), :] = dqt * e[sl] * 0.125
        dk_ref[pl.ds(c * CH, CH), :] = dkt * ei[sl]
        dst[...] = ds_out * gdec + jnp.where(bd, both[NH * CH:], 0.0)
        yield


def _gate_bwd(db, pre, lr, wpad, rev, tb):
    r, c = _iota((tb, tb), 0), _iota((tb, tb), 1)
    tri = (c <= r) if rev else (c >= r)
    cum_t = jnp.where(((r >> 6) == (c >> 6)) & tri, 1.0, 0.0).astype(BF16)
    dla = _dot_exact_lhs(cum_t, db, 2)
    dpre = dla * (1.0 / 16.0) / (1.0 + jnp.exp(pre))
    return dpre, _dot_nt(dpre, wpad), _dot_tn(lr, dpre)


def _gla_bwd_first(z, dy, o_pre, sd, wpad, bias, conv_w, conv_norm, gla_norm4, riders=()):
    t = z.shape[0]
    tb = min(TB_BWD, t)
    nblk, nb = t // tb, tb // CH
    jmap = lambda i: nblk - 1 - i

    def body(q_ref, k_ref, v_ref, lr_ref, g_ref, cb_ref, cc_ref, cu_ref, ccp_ref, ccn_ref, cup_ref, cun_ref,
             dy_ref, opre_ref, sd_ref, w_ref, bias_ref, cw_ref, cn_ref, gn_ref,
             do_ref, dq_ref, dk_ref, dv_ref, dlr_ref, dzg_ref, dzcb_ref, dconv_ref,
             dw_ref, dbias_ref, dcw_ref, dcn_ref, dgn_ref, dst, b_scr, db_scr):
        i = pl.program_id(0)
        j = jmap(i)

        @pl.when(i == 0)
        def _():
            dst[...] = jnp.zeros_like(dst)
            for ref in (dw_ref, dbias_ref, dcw_ref, dcn_ref, dgn_ref):
                ref[...] = jnp.zeros_like(ref)

        dyg = dy_ref[:, CW:]
        g = g_ref[...]
        sig = _sigmoid(g)
        on, rr = _head_norm(opre_ref[...])
        gn = gn_ref[...]
        dzg_ref[...] = (dyg * on * gn * (sig * (1.0 + g * (1.0 - sig)))).astype(BF16)
        don = dyg * (g * sig)
        _acc_rows(dgn_ref, jnp.sum(don * on, axis=0, keepdims=True))
        u = don * gn
        uo = u * on
        mean_uo = jnp.concatenate(
            [jnp.broadcast_to(jnp.mean(uo[:, h * 128:(h + 1) * 128], axis=-1, keepdims=True), (tb, 128))
             for h in range(NH)], axis=1)
        do_ref[...] = rr * (u - on * mean_uo)

        def conv_branch():
            cb = cb_ref[...]
            h, h_m1, h_p1, conv = _conv_parts(cb, cc_ref[...], cu_ref[...], ccp_ref[pl.ds(7, 1), :],
                                              cup_ref[pl.ds(7, 1), :], ccn_ref[pl.ds(0, 1), :],
                                              cun_ref[pl.ds(0, 1), :], cw_ref, j == 0, j == nblk - 1, tb)
            yc = cb * conv
            yield
            ones = _group_ones()
            rc = lax.rsqrt(_dot_exact_rhs(yc * yc, ones, 2) * (1.0 / 64.0) + EPS)
            ycr = yc * rc
            yield
            dyn = dy_ref[:, :CW]
            _acc_rows(dcn_ref, jnp.sum(dyn * ycr, axis=0, keepdims=True))
            uc = dyn * cn_ref[...]
            yield
            dyc = rc * (uc - ycr * (_dot_exact_rhs(uc * ycr, ones, 2) * (1.0 / 64.0)))
            dzcb_ref[...] = (dyc * conv).astype(BF16)
            yield
            dconv = dyc * cb
            dconv_ref[...] = dconv
            yield
            dcw_ref[pl.ds(0, 1), :] += jnp.sum(dconv * h_m1, axis=0, keepdims=True)
            dcw_ref[pl.ds(1, 1), :] += jnp.sum(dconv * h, axis=0, keepdims=True)
            dcw_ref[pl.ds(2, 1), :] += jnp.sum(dconv * h_p1, axis=0, keepdims=True)
            yield

        lr, wp = lr_ref[...], w_ref[...]
        pre, b, e, ei, qt, kt = _gla_recompute(q_ref[...], k_ref[...], lr, wp, bias_ref[...], False, tb)
        b_scr[...] = b
        for _ in itertools.zip_longest(
                _gla_bwd_chunks(do_ref, sd_ref, dst, b_scr, db_scr, dq_ref, dk_ref, dv_ref, qt, kt, e, ei, v_ref[...],
                                False, nb), conv_branch()):
            pass
        dpre, dlr, dw = _gate_bwd(db_scr[...], pre, lr, wp, False, tb)
        dlr_ref[...] = dlr
        dw_ref[...] += dw
        _acc_rows(dbias_ref, jnp.sum(dpre, axis=0, keepdims=True))

    full = lambda i: (0, 0)
    tokv = pl.BlockSpec((tb, GV), lambda i: (jmap(i), 0))
    tokk = pl.BlockSpec((tb, GK), lambda i: (jmap(i), 0))
    ccp, ccn = _halo_specs(tb, nblk, t, ZB_CC, jmap)
    cup, cun = _halo_specs(tb, nblk, t, ZB_CU, jmap)
    in_specs = [_zspec(tb, GK, ZB_Q, jmap), _zspec(tb, GK, ZB_K, jmap), _zspec(tb, GV, ZB_V, jmap),
                _zspec(tb, 128, ZB_LR, jmap), _zspec(tb, GV, ZB_G, jmap), _zspec(tb, CW, ZB_CB, jmap),
                _zspec(tb, CW, ZB_CC, jmap), _zspec(tb, CW, ZB_CU, jmap), ccp, ccn, cup, cun,
                pl.BlockSpec((tb, D), lambda i: (jmap(i), 0)), tokv,
                pl.BlockSpec((nb, 128, GK), lambda i: (jmap(i), 0, 0)), pl.BlockSpec((128, GK), full),
                pl.BlockSpec((1, GK), full), pl.BlockSpec((3, CW), full), pl.BlockSpec((1, CW), full),
                pl.BlockSpec((1, GV), full)]
    out_specs = [tokv, tokk, tokk, tokv, pl.BlockSpec((tb, 128), lambda i: (jmap(i), 0)), tokv, tokv, tokv,
                 pl.BlockSpec((128, GK), full), pl.BlockSpec((8, GK), full), pl.BlockSpec((8, CW), full),
                 pl.BlockSpec((8, CW), full), pl.BlockSpec((8, GV), full)]
    out_shape = [jax.ShapeDtypeStruct((t, GV), F32), jax.ShapeDtypeStruct((t, GK), F32),
                 jax.ShapeDtypeStruct((t, GK), F32), jax.ShapeDtypeStruct((t, GV), F32),
                 jax.ShapeDtypeStruct((t, 128), F32), jax.ShapeDtypeStruct((t, GV), BF16),
                 jax.ShapeDtypeStruct((t, CW), BF16), jax.ShapeDtypeStruct((t, CW), F32),
                 jax.ShapeDtypeStruct((128, GK), F32), jax.ShapeDtypeStruct((8, GK), F32),
                 jax.ShapeDtypeStruct((8, CW), F32), jax.ShapeDtypeStruct((8, CW), F32),
                 jax.ShapeDtypeStruct((8, GV), F32)]
    return _call(
        body, "gla_bwd_first", (nblk,), in_specs, out_specs, out_shape,
        [pltpu.VMEM((GV, GK), F32), pltpu.VMEM((tb, GK), F32), pltpu.VMEM((tb, GK), F32)],
        (z, z, z, z, z, z, z, z, z, z, z, z, dy, o_pre, sd, wpad, bias, conv_w, conv_norm, gla_norm4), riders)


def _gla_bwd_second(z, do, sd, wpad, bias, dqa, dka, dva, dlra, dzg, dzcb, dconv, conv_w, riders=()):
    t = z.shape[0]
    tb = min(TB_BWD, t)
    nblk, nb = t // tb, tb // CH
    jmap = lambda i: i

    def body(q_ref, k_ref, v_ref, lr_ref, cc_ref, cu_ref, do_ref, sd_ref, w_ref, bias_ref, dqa_ref, dka_ref,
             dva_ref, dlra_ref, dzg_ref, dzcb_ref, dc_ref, dcp_ref, dcn_ref, cw_ref,
             dz_ref, dw_ref, dbias_ref, dst, b_scr, db_scr, dq_scr, dk_scr, dv_scr, sb_scr, dsk_scr):
        i = pl.program_id(0)

        @pl.when(i == 0)
        def _():
            dst[...] = jnp.zeros_like(dst)
            dw_ref[...] = jnp.zeros_like(dw_ref)
            dbias_ref[...] = jnp.zeros_like(dbias_ref)

        q_raw, k, v, lr, wp = q_ref[...], k_ref[...], v_ref[...], lr_ref[...], w_ref[...]
        pre, b, e, ei, qt, kt = _gla_recompute(q_raw, k, lr, wp, bias_ref[...], True, tb)
        b_scr[...] = b

        def token_local():
            dc = dc_ref[...]
            rows = _iota(dc.shape, 0)
            dprev = jnp.where(i == 0, 0.0, dcp_ref[pl.ds(7, 1), :])
            dnext = jnp.where(i == nblk - 1, 0.0, dcn_ref[pl.ds(0, 1), :])
            dc_m1 = jnp.where(rows == 0, dprev, pltpu.roll(dc, 1, 0))
            dc_p1 = jnp.where(rows == tb - 1, dnext, pltpu.roll(dc, tb - 1, 0))
            yield
            dh = cw_ref[pl.ds(0, 1), :] * dc_p1 + cw_ref[pl.ds(1, 1), :] * dc + cw_ref[pl.ds(2, 1), :] * dc_m1
            dz_ref[:, 0:512] = dzcb_ref[...]
            yield
            dz_ref[:, 512:1024] = (dh * cu_ref[...]).astype(BF16)
            dz_ref[:, 1024:1536] = (dh * cc_ref[...]).astype(BF16)
            dz_ref[:, 2560:3072] = dzg_ref[...]
            yield
            hsel = jnp.where((_iota((GK, GV), 0) >> 6) == (_iota((GK, GV), 1) >> 7), 1.0, 0.0).astype(BF16)
            sb_scr[...] = _dot_exact_rhs((q_raw * 0.125) * k, hsel, 2)
            yield
            hsel_t = jnp.where((_iota((GV, GK), 0) >> 7) == (_iota((GV, GK), 1) >> 6), 1.0, 0.0).astype(BF16)
            dsk_scr[...] = _dot_exact_rhs(do_ref[...] * v, hsel_t, 2)
            yield

        for _ in itertools.zip_longest(
                _gla_bwd_chunks(do_ref, sd_ref, dst, b_scr, db_scr, dq_scr, dk_scr, dv_scr, qt, kt, e, ei, v, True, nb),
                token_local()):
            pass
        dpre, dlr, dw = _gate_bwd(db_scr[...], pre, lr, wp, True, tb)
        dw_ref[...] += dw
        _acc_rows(dbias_ref, jnp.sum(dpre, axis=0, keepdims=True))
        dsk = dsk_scr[...]
        dz_ref[:, 1536:1792] = (dqa_ref[...] + dq_scr[...] - dsk * k * 0.125).astype(BF16)
        dz_ref[:, 1792:2048] = (dka_ref[...] + dk_scr[...] - dsk * (q_raw * 0.125)).astype(BF16)
        dz_ref[:, 2048:2560] = (dva_ref[...] + dv_scr[...] - sb_scr[...] * do_ref[...]).astype(BF16)
        dz_ref[:, 3072:3200] = (dlra_ref[...] + dlr).astype(BF16)

    full = lambda i: (0, 0)
    tokv = pl.BlockSpec((tb, GV), lambda i: (i, 0))
    tokk = pl.BlockSpec((tb, GK), lambda i: (i, 0))
    dcp = pl.BlockSpec((8, CW), lambda i: (jnp.maximum(i * (tb // 8) - 1, 0), 0))
    dcn = pl.BlockSpec((8, CW), lambda i: (jnp.minimum((i + 1) * (tb // 8), t // 8 - 1), 0))
    in_specs = [_zspec(tb, GK, ZB_Q, jmap), _zspec(tb, GK, ZB_K, jmap), _zspec(tb, GV, ZB_V, jmap),
                _zspec(tb, 128, ZB_LR, jmap), _zspec(tb, CW, ZB_CC, jmap), _zspec(tb, CW, ZB_CU, jmap), tokv,
                pl.BlockSpec((nb, 128, GK), lambda i: (i, 0, 0)), pl.BlockSpec((128, GK), full),
                pl.BlockSpec((1, GK), full), tokk, tokk, tokv, pl.BlockSpec((tb, 128), lambda i: (i, 0)), tokv, tokv,
                tokv, dcp, dcn, pl.BlockSpec((3, CW), full)]
    out_specs = [pl.BlockSpec((tb, ZC), lambda i: (i, 0)), pl.BlockSpec((128, GK), full), pl.BlockSpec((8, GK), full)]
    out_shape = [jax.ShapeDtypeStruct((t, ZC), BF16), jax.ShapeDtypeStruct((128, GK), F32),
                 jax.ShapeDtypeStruct((8, GK), F32)]
    return _call(
        body, "gla_bwd_second", (nblk,), in_specs, out_specs, out_shape,
        [pltpu.VMEM((GV, GK), F32), pltpu.VMEM((tb, GK), F32), pltpu.VMEM((tb, GK), F32),
         pltpu.VMEM((tb, GK), F32), pltpu.VMEM((tb, GK), F32), pltpu.VMEM((tb, GV), F32),
         pltpu.VMEM((tb, GV), F32), pltpu.VMEM((tb, GK), F32)],
        (z, z, z, z, z, z, do, sd, wpad, bias, dqa, dka, dva, dlra, dzg, dzcb, dconv, dconv, dconv, conv_w), riders)


def _step(x, mem, target, shard, small_pack, vec, place):
    own, from_chips = {}, {}

    def pair_sums(names, g4, from_sibling):
        pbs = []
        for n, g, s in zip(names, g4, from_sibling):
            pb, own[n] = _rs_pair_sum(place, g, s, "pair_sum_" + n)
            pbs.append(pb)
        return pbs

    def by_dest(g, n):
        return g.reshape((4, 2) + shard[n].shape)

    w_in, small_all = _exchange(_gather_rider([shard["w_in"], small_pack]), "gather_w_in")
    w_in = jnp.pad(w_in.reshape(ZW, D), ((0, ZC - ZW), (0, 0)))
    small_all = small_all.reshape(NDEV, -1)
    p, off = {}, 0
    for n, (r, c) in SMALL_SHARDED.items():
        p[n] = small_all[:, off:off + r * c].reshape(NDEV, r, c).transpose(1, 0, 2).reshape(r, NDEV * c)
        off += r * c
    zeros_lr = jnp.zeros((128 - LR, GK), BF16)
    waf_pad = jnp.concatenate([p["w_af"].astype(BF16), zeros_lr], axis=0)
    wab_pad = jnp.concatenate([jnp.zeros((LR, GK), BF16), p["w_ab"].astype(BF16), zeros_lr[:128 - 2 * LR]], axis=0)
    gla_norm4 = jnp.tile(vec["gla_norm"], (1, NH))

    z, hb, w_out, w_xq, w_xo, w_xkv = _inproj(
        x, vec["mix_norm"], w_in, [_gather_rider([shard[n] for n in ("w_out", "w_xq", "w_xo", "w_xkv")])])
    w_out, w_xq, w_xo = [a.reshape(D, D) for a in (w_out, w_xq, w_xo)]
    o_f, sd_f, o_b, sd_b, w_up_t, w_down = _gla_fwd(
        z, waf_pad, vec["b_af"], wab_pad, vec["b_ab"], [_gather_rider([shard["w_up"], shard["w_down"]])])
    w_up_t, w_down = w_up_t.reshape(FF, D), w_down.reshape(FF, D)
    yb, o_pre = _mixer_finish(z, o_f, o_b, p["conv_w"], vec["conv_norm"], gla_norm4)
    kv, memn = _kv_proj(mem, vec["mem_norm"], w_xkv)
    kb, vb = kv[:, :D].astype(BF16), kv[:, D:].astype(BF16)
    x1, x2, xn1, qb, attb = _attn_fwd(x, yb, w_out, vec["xa_norm"], w_xq, kb, vb, w_xo)
    h1b, xn2, dx3, dx3b, loss8, dfinal = _mlp_fwd(x2, vec["mlp_norm"], w_up_t, w_down, vec["final_norm"], target)

    ab, dh1b, dx2, dx2b, dmlp = _mlp_bwd(dx3, dx3b, h1b, w_down, w_up_t, x2, vec["mlp_norm"])
    g_mlp = [by_dest(_matmul_tn(ab, dx3b, "dw_down")[0], "w_down"),
             by_dest(_matmul_tn(dh1b, xn2, "dw_up")[0], "w_up")]
    dx1, dx1b, dy, dqb, dkv, dxa = _attn_bwd(x1, dx2, dx2b, qb, kb, vb, w_xo, w_xq, w_out, vec["xa_norm"])
    dw_xo, *s_mlp = _matmul_tn(attb, dx2b, "dw_xo", riders=[_sibling_rider(g_mlp)])
    pb_mlp = pair_sums(("w_down", "w_up"), g_mlp, s_mlp)
    dw_xkv, dmemn = _kv_bwd(dkv, memn, mem, vec["mem_norm"], w_xkv)
    att_names = ("w_xo", "w_xq", "w_out", "w_xkv")
    g_att = [by_dest(g, n) for g, n in zip(
        (dw_xo, _matmul_tn(xn1, dqb, "dw_xq")[0], _matmul_tn(yb, dx1b, "dw_out")[0], dw_xkv), att_names)]
    res = _gla_bwd_first(z, dy, o_pre, sd_f, waf_pad, vec["b_af"], p["conv_w"], vec["conv_norm"], gla_norm4,
                         riders=[_chips_rider(pb_mlp), _sibling_rider(g_att)])
    do, dqa, dka, dva, dlra, dzg, dzcb, dconv, dwaf, dbaf, dcw, dcn, dgn = res[:13]
    from_chips["w_down"], from_chips["w_up"] = res[13:15]
    pb_att = pair_sums(att_names, g_att, res[15:])
    dz, dwab, dbab, *c_att = _gla_bwd_second(z, do, sd_b, wab_pad, vec["b_ab"], dqa, dka, dva, dlra, dzg, dzcb, dconv,
                                             p["conv_w"], riders=[_chips_rider(pb_att)])
    from_chips.update(zip(att_names, c_att))
    g_in = [by_dest(_matmul_tn(dz, hb, "dw_in", rows=ZW)[0], "w_in")]
    pb_in = pair_sums(("w_in",), g_in, _exchange(_sibling_rider(g_in), "grads_to_sibling_w_in"))
    grad_x, dmix, from_chips["w_in"] = _inproj_bwd(dz, w_in, x, dx1, vec["mix_norm"], riders=[_chips_rider(pb_in)])

    small_acc = dict(mix_norm=dmix, conv_w=dcw, conv_norm=dcn, w_af=dwaf, b_af=dbaf, w_ab=dwab, b_ab=dbab,
                     gla_norm=dgn, xa_norm=dxa, mem_norm=dmemn, mlp_norm=dmlp, final_norm=dfinal)
    return loss8, grad_x, small_acc, own, from_chips


def _place():
    return lax.axis_index("x"), lax.axis_index("y"), lax.axis_index("c")


class _Rider:
    def __init__(self, arrays, out_shape, scratch, start, finish):
        self.arrays, self.out_shape, self.scratch, self.start, self.finish = arrays, out_shape, scratch, start, finish


def _gather_rider(blks):
    n = len(blks)

    def plan(in_refs, out_refs, sems):
        send_sems, recv_sems, local_sems = sems
        x, y, c = _place()
        me, sibling = (x, y, c), (x, y, 1 - c)
        chips = [(1 - x, y, c), (x, 1 - y, c), (1 - x, 1 - y, c)]

        def copy(a, k, block, to, own=False):
            px, py, pc = block
            dst = out_refs[a].at[4 * px + 2 * py + pc]
            return pltpu.make_async_remote_copy(
                src_ref=in_refs[a] if own else dst, dst_ref=dst, send_sem=send_sems.at[k, a],
                recv_sem=recv_sems.at[k, a], device_id=to, device_id_type=MESH)

        def local(a):
            return pltpu.make_async_copy(in_refs[a], out_refs[a].at[4 * x + 2 * y + c], local_sems.at[a])

        def own_sends(a):
            return [copy(a, 0, me, sibling, own=True)] + [copy(a, 1 + j, me, chip, own=True)
                                                          for j, chip in enumerate(chips)]

        return copy, local, own_sends, me, sibling, chips

    def start(in_refs, out_refs, sems):
        _, local, own_sends, _, _, _ = plan(in_refs, out_refs, sems)
        for a in range(n):
            local(a).start()
            for cp in own_sends(a):
                cp.start()

    def finish(in_refs, out_refs, sems):
        copy, local, own_sends, me, sibling, chips = plan(in_refs, out_refs, sems)
        for j, chip in enumerate(chips):
            for a in range(n):
                copy(a, 1 + j, chip, me).wait_recv()
                copy(a, 4 + j, chip, sibling).start()
        for a in range(n):
            copy(a, 0, sibling, me).wait_recv()
            for j, (px, py, pc) in enumerate(chips):
                copy(a, 4 + j, (px, py, 1 - pc), me).wait_recv()
            for cp in own_sends(a) + [copy(a, 4 + j, chip, sibling) for j, chip in enumerate(chips)]:
                cp.wait_send()
            local(a).wait()

    return _Rider(blks, [jax.ShapeDtypeStruct((NDEV,) + b.shape, b.dtype) for b in blks],
                  [pltpu.SemaphoreType.DMA((7, n)), pltpu.SemaphoreType.DMA((7, n)), pltpu.SemaphoreType.DMA((n,))],
                  start, finish)


def _sibling_rider(g4s):
    n = len(g4s)

    def copies(in_refs, out_refs, sems):
        send_sems, recv_sems = sems
        x, y, c = _place()
        return [pltpu.make_async_remote_copy(
            src_ref=in_refs[a].at[k, 1 - c], dst_ref=out_refs[a].at[k], send_sem=send_sems.at[k, a],
            recv_sem=recv_sems.at[k, a], device_id=(x, y, 1 - c), device_id_type=MESH)
            for a in range(n) for k in range(4)]

    def start(in_refs, out_refs, sems):
        for cp in copies(in_refs, out_refs, sems):
            cp.start()

    def finish(in_refs, out_refs, sems):
        for cp in copies(in_refs, out_refs, sems):
            cp.wait()

    return _Rider(g4s, [jax.ShapeDtypeStruct((4,) + g.shape[2:], g.dtype) for g in g4s],
                  [pltpu.SemaphoreType.DMA((4, n)), pltpu.SemaphoreType.DMA((4, n))], start, finish)


def _chips_rider(pbs):
    n = len(pbs)

    def copies(in_refs, out_refs, sems):
        send_sems, recv_sems = sems
        x, y, c = _place()
        peers = [(1 - x, y), (x, 1 - y), (1 - x, 1 - y)]
        return [pltpu.make_async_remote_copy(
            src_ref=in_refs[a].at[2 * px + py], dst_ref=out_refs[a].at[k], send_sem=send_sems.at[k, a],
            recv_sem=recv_sems.at[k, a], device_id=(px, py, c), device_id_type=MESH)
            for a in range(n) for k, (px, py) in enumerate(peers)]

    def start(in_refs, out_refs, sems):
        for cp in copies(in_refs, out_refs, sems):
            cp.start()

    def finish(in_refs, out_refs, sems):
        for cp in copies(in_refs, out_refs, sems):
            cp.wait()

    return _Rider(pbs, [jax.ShapeDtypeStruct((3,) + p.shape[1:], p.dtype) for p in pbs],
                  [pltpu.SemaphoreType.DMA((3, n)), pltpu.SemaphoreType.DMA((3, n))], start, finish)


def _exchange(rider, name):
    n_in, n_out = len(rider.arrays), len(rider.out_shape)

    def body(*refs):
        ins, outs, sems = refs[:n_in], refs[n_in:n_in + n_out], refs[n_in + n_out:]
        rider.start(ins, outs, sems)
        rider.finish(ins, outs, sems)

    hbm = pl.BlockSpec(memory_space=pltpu.HBM)
    return pl.pallas_call(body, name=name, out_shape=rider.out_shape, in_specs=[hbm] * n_in,
                          out_specs=[hbm] * n_out, scratch_shapes=rider.scratch)(*rider.arrays)


def _rs_pair_sum(place, g4, r1, name):
    rows, cols = g4.shape[2:]
    tr = min(rows, 512)

    def body(pl_ref, g_ref, r_ref, pb_ref, own_ref):
        s = g_ref[0, 0] + r_ref[0]
        pb_ref[0] = s.astype(BF16)

        @pl.when(pl.program_id(1) == pl_ref[0])
        def _():
            own_ref[...] = s

    grid_spec = pltpu.PrefetchScalarGridSpec(
        num_scalar_prefetch=1, grid=(rows // tr, 4),
        in_specs=[pl.BlockSpec((1, 1, tr, cols), lambda r, k, p: (k, p[1], r, 0)),
                  pl.BlockSpec((1, tr, cols), lambda r, k, p: (k, r, 0))],
        out_specs=[pl.BlockSpec((1, tr, cols), lambda r, k, p: (k, r, 0)),
                   pl.BlockSpec((tr, cols), lambda r, k, p: (r, 0))])
    return pl.pallas_call(
        body, name=name, grid_spec=grid_spec,
        out_shape=[jax.ShapeDtypeStruct((4, rows, cols), BF16), jax.ShapeDtypeStruct((rows, cols), F32)],
        compiler_params=_cparams(("arbitrary", "arbitrary")))(place, g4, r1)


PACK_ROWS = 32
VEC_ROW = {"mix_norm": 0, "conv_norm": 1, "b_af": 2, "b_ab": 3, "gla_norm": 4, "xa_norm": 5, "mem_norm": 6,
           "mlp_norm": 7, "final_norm": 8}
LOSS_ROW, MAT_ROW = 9, 16
MAT_LANE = {"w_af": 0, "w_ab": GK, "conv_w": 2 * GK}
MAT_SRC_ROW = {"w_af": 0, "w_ab": LR, "conv_w": 0}


SMALL_WIDTH = {"mix_norm": D, "conv_w": 64, "conv_norm": CW, "w_af": 32, "b_af": GK, "w_ab": 32, "b_ab": GK,
               "gla_norm": 128, "xa_norm": D, "mem_norm": D, "mlp_norm": D, "final_norm": D}


def _small_reduce(acc, loss8):
    names = list(SMALL)
    n = len(names)
    widths = SMALL_WIDTH

    def body(*refs):
        acc_refs = dict(zip(names, refs[:n]))
        loss_ref, tot = refs[n], refs[n + 1]
        pk, all_ref, send_sems, recv_sems, local_sem = refs[n + 2:]

        pk[...] = jnp.zeros_like(pk)
        for k, row in VEC_ROW.items():
            if k == "gla_norm":
                g = functools.reduce(lambda a, b: a + b, [acc_refs[k][pl.ds(0, 1), pl.ds(h * 128, 128)]
                                                          for h in range(NH)])
            else:
                g = acc_refs[k][pl.ds(0, 1), :]
            pk[pl.ds(row, 1), pl.ds(0, widths[k])] = g
        pk[pl.ds(LOSS_ROW, 1), pl.ds(0, 128)] = loss_ref[pl.ds(0, 1), :]
        for k, lane in MAT_LANE.items():
            rows, cols = (3, CW) if k == "conv_w" else (LR, GK)
            pk[pl.ds(MAT_ROW, rows), pl.ds(lane, cols)] = acc_refs[k][pl.ds(MAT_SRC_ROW[k], rows), :]

        x, y, c = _place()
        me, sibling = (x, y, c), (x, y, 1 - c)
        chips = [(1 - x, y, c), (x, 1 - y, c), (1 - x, 1 - y, c)]

        def copy(k, block, to, own=False):
            px, py, pc = block
            dst = all_ref.at[4 * px + 2 * py + pc]
            return pltpu.make_async_remote_copy(
                src_ref=pk if own else dst, dst_ref=dst, send_sem=send_sems.at[k], recv_sem=recv_sems.at[k],
                device_id=to, device_id_type=MESH)

        mine = pltpu.make_async_copy(pk, all_ref.at[4 * x + 2 * y + c], local_sem)
        mine.start()
        first = [copy(0, me, sibling, own=True)] + [copy(1 + j, me, chip, own=True) for j, chip in enumerate(chips)]
        for cp in first:
            cp.start()
        passed = [copy(4 + j, chip, sibling) for j, chip in enumerate(chips)]
        for j, chip in enumerate(chips):
            copy(1 + j, chip, me).wait_recv()
            passed[j].start()
        copy(0, sibling, me).wait_recv()
        for j, (px, py, pc) in enumerate(chips):
            copy(4 + j, (px, py, 1 - pc), me).wait_recv()
        for cp in first + passed:
            cp.wait_send()
        mine.wait()
        total = all_ref[0]
        for d in range(1, NDEV):
            total = total + all_ref[d]
        tot[...] = total

    return pl.pallas_call(
        body, name="small_reduce", out_shape=jax.ShapeDtypeStruct((PACK_ROWS, D), F32),
        scratch_shapes=[pltpu.VMEM((PACK_ROWS, D), F32), pltpu.VMEM((NDEV, PACK_ROWS, D), F32),
                        pltpu.SemaphoreType.DMA((7,)), pltpu.SemaphoreType.DMA((7,)), pltpu.SemaphoreType.DMA],
    )(*[acc[k] for k in names], loss8)


def _small_adamw(tot, ws, ms, vs):
    names = list(SMALL)
    n = len(names)
    widths = SMALL_WIDTH

    def body(*refs):
        tot = refs[0]
        w_refs, m_refs, v_refs = [dict(zip(names, refs[1 + q * n:1 + (q + 1) * n])) for q in range(3)]
        outs = refs[1 + 3 * n:1 + 7 * n]
        g_out, d_out, m_out, v_out = [dict(zip(names, outs[q * n:(q + 1) * n])) for q in range(4)]
        cut = refs[1 + 7 * n]
        x, y, c = _place()
        dev = 4 * x + 2 * y + c
        for k in names:
            if k in VEC_ROW:
                g = tot[pl.ds(VEC_ROW[k], 1), pl.ds(0, widths[k])]
            else:
                rows, cols = (3, CW) if k == "conv_w" else (LR, GK)
                wd = widths[k]
                sel = jnp.where(_iota((cols, wd), 0) == dev * wd + _iota((cols, wd), 1), 1.0, 0.0).astype(BF16)
                cut[:, pl.ds(0, wd)] = _dot_exact_rhs(tot[pl.ds(MAT_ROW, LR), pl.ds(MAT_LANE[k], cols)], sel, 3)
                g = cut[pl.ds(0, rows), pl.ds(0, wd)]
            g_out[k][...] = g
            d_out[k][...], m_out[k][...], v_out[k][...] = _adamw_math(w_refs[k][...], g, m_refs[k][...],
                                                                       v_refs[k][...])

    shapes = [jax.ShapeDtypeStruct(ws[k].shape, F32) for k in names]
    res = pl.pallas_call(
        body, name="small_adamw", out_shape=shapes * 4, scratch_shapes=[pltpu.VMEM((LR, 128), F32)],
    )(tot, *[ws[k] for k in names], *[ms[k] for k in names], *[vs[k] for k in names])
    return {k: tuple(res[q * n + i] for q in range(4)) for i, k in enumerate(names)}


def _adamw_math(w, g, m, v):
    m = ADAM_B1 * m + (1.0 - ADAM_B1) * g
    v = ADAM_B2 * v + (1.0 - ADAM_B2) * (g * g)
    m_hat = m / (1.0 - ADAM_B1 ** ADAM_STEP)
    v_hat = v / (1.0 - ADAM_B2 ** ADAM_STEP)
    delta = -ADAM_LR * (m_hat / (jnp.sqrt(v_hat) + ADAM_EPS) + ADAM_WD * w)
    return delta, m, v


def _adamw(w, m, v, own, r2, name):
    _, r, c = w.shape
    tr = 256 if r % 256 == 0 else r

    def body(w_ref, m_ref, v_ref, o_ref, r_ref, g_ref, d_ref, nm_ref, nv_ref):
        g = ((o_ref[...] + r_ref[0].astype(F32)) + r_ref[1].astype(F32)) + r_ref[2].astype(F32)
        g_ref[...] = g
        d_ref[...], nm_ref[...], nv_ref[...] = _adamw_math(w_ref[...], g, m_ref[...], v_ref[...])

    spec = pl.BlockSpec((None, tr, c), lambda i: (0, i, 0))
    return pl.pallas_call(
        body, name=name, grid=(r // tr,),
        in_specs=[spec, spec, spec, pl.BlockSpec((tr, c), lambda i: (i, 0)),
                  pl.BlockSpec((3, tr, c), lambda i: (0, i, 0))],
        out_specs=[spec] * 4, out_shape=[jax.ShapeDtypeStruct((1, r, c), F32)] * 4,
        compiler_params=_cparams(("arbitrary",)))(w, m, v, own, r2)


MATS = ("w_in", "w_out", "w_xq", "w_xo", "w_xkv", "w_up", "w_down")
SMALL = ("mix_norm", "conv_w", "conv_norm", "w_af", "b_af", "w_ab", "b_ab", "gla_norm", "xa_norm", "mem_norm",
         "mlp_norm", "final_norm")
WEIGHTS = ("mix_norm", "w_in", "conv_w", "conv_norm", "w_af", "b_af", "w_ab", "b_ab", "gla_norm", "w_out", "xa_norm",
           "mem_norm", "w_xq", "w_xkv", "w_xo", "mlp_norm", "w_up", "w_down", "final_norm")
SMALL_SHARDED = {"conv_w": (3, 64), "w_af": (LR, 32), "w_ab": (LR, 32)}
SMALL_PACK_ROWS = 16


def kernel(x, mem, mix_norm, w_in, conv_w, conv_norm, w_af, b_af, w_ab, b_ab, gla_norm, w_out, xa_norm, mem_norm, w_xq, w_xkv, w_xo, mlp_norm, w_up, w_down, final_norm, loss_target, m_mix_norm, m_w_in, m_conv_w, m_conv_norm, m_w_af, m_b_af, m_w_ab, m_b_ab, m_gla_norm, m_w_out, m_xa_norm, m_mem_norm, m_w_xq, m_w_xkv, m_w_xo, m_mlp_norm, m_w_up, m_w_down, m_final_norm, v_mix_norm, v_w_in, v_conv_w, v_conv_norm, v_w_af, v_b_af, v_w_ab, v_b_ab, v_gla_norm, v_w_out, v_xa_norm, v_mem_norm, v_w_xq, v_w_xkv, v_w_xo, v_mlp_norm, v_w_up, v_w_down, v_final_norm):
    w = dict(mix_norm=mix_norm, w_in=w_in, conv_w=conv_w, conv_norm=conv_norm, w_af=w_af, b_af=b_af, w_ab=w_ab,
             b_ab=b_ab, gla_norm=gla_norm, w_out=w_out, xa_norm=xa_norm, mem_norm=mem_norm, w_xq=w_xq, w_xkv=w_xkv,
             w_xo=w_xo, mlp_norm=mlp_norm, w_up=w_up, w_down=w_down, final_norm=final_norm)
    mom = dict(mix_norm=m_mix_norm, w_in=m_w_in, conv_w=m_conv_w, conv_norm=m_conv_norm, w_af=m_w_af, b_af=m_b_af,
               w_ab=m_w_ab, b_ab=m_b_ab, gla_norm=m_gla_norm, w_out=m_w_out, xa_norm=m_xa_norm, mem_norm=m_mem_norm,
               w_xq=m_w_xq, w_xkv=m_w_xkv, w_xo=m_w_xo, mlp_norm=m_mlp_norm, w_up=m_w_up, w_down=m_w_down,
               final_norm=m_final_norm)
    var = dict(mix_norm=v_mix_norm, w_in=v_w_in, conv_w=v_conv_w, conv_norm=v_conv_norm, w_af=v_w_af, b_af=v_b_af,
               w_ab=v_w_ab, b_ab=v_b_ab, gla_norm=v_gla_norm, w_out=v_w_out, xa_norm=v_xa_norm, mem_norm=v_mem_norm,
               w_xq=v_w_xq, w_xkv=v_w_xkv, w_xo=v_w_xo, mlp_norm=v_mlp_norm, w_up=v_w_up, w_down=v_w_down,
               final_norm=v_final_norm)
    xi, yi, ci = _place()
    two_d = lambda a: a.reshape(a.shape[-2:]) if a.ndim == 3 else a.reshape(1, a.shape[-1])

    small = jnp.concatenate([w[n].reshape(-1) for n in SMALL_SHARDED])
    small = jnp.pad(small, (0, SMALL_PACK_ROWS * 128 - small.shape[0])).reshape(SMALL_PACK_ROWS, 128)
    shard = {n: two_d(w[n]).astype(BF16) for n in MATS}
    for n in ("w_in", "w_up"):
        shard[n] = shard[n].T
    vec = {n: two_d(w[n]) for n in SMALL if n not in SMALL_SHARDED}
    place = jnp.stack([2 * xi + yi, ci]).astype(jnp.int32)
    loss8, grad_x, small_acc, own, from_chips = _step(x[0], mem[0], loss_target[0], shard, small, vec, place)

    tot = _small_reduce(small_acc, loss8)
    small_out = _small_adamw(tot, *[{n: two_d(d[n]) for n in SMALL} for d in (w, mom, var)])
    loss = tot[LOSS_ROW, 0]

    out_g, out_d, out_m, out_v = {}, {}, {}, {}
    own["w_up"], from_chips["w_up"] = own["w_up"].T, from_chips["w_up"].transpose(0, 2, 1)
    for n in MATS:
        if n == "w_in":
            res = _adamw(*[a.transpose(0, 2, 1) for a in (w[n], mom[n], var[n])], own[n], from_chips[n], "adamw_" + n)
            res = [a.transpose(0, 2, 1) for a in res]
        else:
            res = _adamw(w[n], mom[n], var[n], own[n], from_chips[n], "adamw_" + n)
        out_g[n], out_d[n], out_m[n], out_v[n] = res
    for n in SMALL:
        out_g[n], out_d[n], out_m[n], out_v[n] = [a.reshape(w[n].shape) for a in small_out[n]]

    return (loss, grad_x[None], *[out_g[n] for n in WEIGHTS], *[out_d[n] for n in WEIGHTS],
            *[out_m[n] for n in WEIGHTS], *[out_v[n] for n in WEIGHTS])
```

```python
import functools
import itertools

import jax
import jax.numpy as jnp
from jax import lax
from jax.experimental import pallas as pl
from jax.experimental.pallas import tpu as pltpu

F32 = jnp.float32
BF16 = jnp.bfloat16

D = 1024
CW = 512
GK = 256
GV = 512
NH = 4
CH = 64
LR = 16
NMEM = 256
XD = 256
FF = 4096
ZW = 3104
ZC = 3200
EPS = 1e-6
NDEV = 8

ZB_CB, ZB_CC, ZB_CU, ZB_V, ZB_G = 0, 1, 2, 4, 5
ZB_Q, ZB_K = 6, 7
ZB_LR = 24

TM = 512
TM_MLP = 256
TM_MLP_FWD = 512
TF = 512
TB = 512
TB_BWD = 512
TT = 2048
VMEM_LIMIT = 56 * 1024 * 1024

ADAM_LR, ADAM_B1, ADAM_B2, ADAM_EPS, ADAM_WD, ADAM_STEP = 0.001, 0.9, 0.999, 1e-08, 0.01, 10

XKV_SHARD = 2 * D // NDEV

MESH = pl.DeviceIdType.MESH


def _cparams(sem):
    return pltpu.CompilerParams(dimension_semantics=sem, vmem_limit_bytes=VMEM_LIMIT)


def _call(body, name, grid, in_specs, out_specs, out_shape, scratch, args, riders=()):
    n_in, n_out, n_scr = len(in_specs), len(out_specs), len(scratch)
    counts = [(len(r.arrays), len(r.out_shape), len(r.scratch)) for r in riders]

    def take(refs, pos, sizes):
        groups = []
        for size in sizes:
            groups.append(refs[pos:pos + size])
            pos += size
        return groups, pos

    def wrapped(*refs):
        ins, pos = refs[:n_in], n_in
        r_ins, pos = take(refs, pos, [c[0] for c in counts])
        outs, pos = refs[pos:pos + n_out], pos + n_out
        r_outs, pos = take(refs, pos, [c[1] for c in counts])
        scr, pos = refs[pos:pos + n_scr], pos + n_scr
        r_scr, pos = take(refs, pos, [c[2] for c in counts])
        ids = [pl.program_id(d) for d in range(len(grid))]
        first = functools.reduce(lambda a, b: a & b, [i == 0 for i in ids])
        last = functools.reduce(lambda a, b: a & b, [i == g - 1 for i, g in zip(ids, grid)])

        @pl.when(first)
        def _():
            for r, a, b, c in zip(riders, r_ins, r_outs, r_scr):
                r.start(a, b, c)

        body(*ins, *outs, *scr)

        @pl.when(last)
        def _():
            for r, a, b, c in zip(riders, r_ins, r_outs, r_scr):
                r.finish(a, b, c)

    hbm = pl.BlockSpec(memory_space=pltpu.HBM)
    r_args = [a for r in riders for a in r.arrays]
    r_shapes = [s for r in riders for s in r.out_shape]
    return pl.pallas_call(
        wrapped if riders else body, name=name, grid=grid, in_specs=list(in_specs) + [hbm] * len(r_args),
        out_specs=list(out_specs) + [hbm] * len(r_shapes), out_shape=list(out_shape) + r_shapes,
        scratch_shapes=list(scratch) + [s for r in riders for s in r.scratch],
        compiler_params=_cparams(("arbitrary",) * len(grid)))(*args, *r_args)


def _dot(a, b):
    return jnp.dot(a.astype(BF16), b.astype(BF16), preferred_element_type=F32)


def _dot_nt(a, b):
    return lax.dot_general(a.astype(BF16), b.astype(BF16), (((1,), (1,)), ((), ())), preferred_element_type=F32)


def _dot_tn(a, b):
    return lax.dot_general(a.astype(BF16), b.astype(BF16), (((0,), (0,)), ((), ())), preferred_element_type=F32)


def _split(x, n):
    parts = []
    for _ in range(n):
        p = x.astype(BF16)
        parts.append(p)
        x = x - p.astype(F32)
    return parts


def _dot_exact_lhs(m, x, n):
    return functools.reduce(lambda a, b: a + b, [jnp.dot(m, p, preferred_element_type=F32) for p in _split(x, n)])


def _dot_exact_rhs(x, m, n):
    return functools.reduce(lambda a, b: a + b, [jnp.dot(p, m, preferred_element_type=F32) for p in _split(x, n)])


def _rms(x, g):
    r = lax.rsqrt(jnp.mean(x * x, axis=-1, keepdims=True) + EPS)
    return x * r * g, r


def _rms_bwd(x, r, g, dy):
    xr = x * r
    u = dy * g
    dx = r * (u - xr * jnp.mean(u * xr, axis=-1, keepdims=True))
    return dx, jnp.sum(dy * xr, axis=0, keepdims=True)


def _iota(shape, dim):
    return lax.broadcasted_iota(jnp.int32, shape, dim)


def _sigmoid(x):
    return 1.0 / (1.0 + jnp.exp(-x))


def _acc_rows(ref, row):
    ref[...] += jnp.broadcast_to(row, ref.shape)


def _inproj(x, g, w_t, riders=()):
    t = x.shape[0]
    tm = min(TM, t)

    def body(x_ref, g_ref, w_ref, z_ref, h_ref):
        h, _ = _rms(x_ref[...], g_ref[...])
        hb = h.astype(BF16)
        h_ref[...] = hb
        z_ref[...] = _dot_nt(hb, w_ref[...])

    return _call(
        body, "inproj", (t // tm,),
        [pl.BlockSpec((tm, D), lambda i: (i, 0)), pl.BlockSpec((1, D), lambda i: (0, 0)),
         pl.BlockSpec((ZC, D), lambda i: (0, 0))],
        [pl.BlockSpec((tm, ZC), lambda i: (i, 0)), pl.BlockSpec((tm, D), lambda i: (i, 0))],
        [jax.ShapeDtypeStruct((t, ZC), F32), jax.ShapeDtypeStruct((t, D), BF16)], [], (x, g, w_t), riders)


def _kv_proj(mem, g, w):
    def body(m_ref, g_ref, w_ref, kv_ref, mn_ref):
        mn, _ = _rms(m_ref[...], g_ref[...])
        mb = mn.astype(BF16)
        mn_ref[...] = mb
        for j in range(NDEV):
            kv_ref[:, j * XKV_SHARD:(j + 1) * XKV_SHARD] = jnp.dot(mb, w_ref[j], preferred_element_type=F32)

    return pl.pallas_call(
        body, name="kv_proj",
        out_shape=[jax.ShapeDtypeStruct((NMEM, 2 * D), F32), jax.ShapeDtypeStruct((NMEM, D), BF16)],
        compiler_params=pltpu.CompilerParams(vmem_limit_bytes=VMEM_LIMIT))(mem, g, w)


def _softmax_head(qb, kb):
    s = _dot_nt(qb, kb) * (1.0 / 16.0)
    e = jnp.exp(s - jnp.max(s, axis=-1, keepdims=True))
    return e / jnp.sum(e, axis=-1, keepdims=True)


def _attn_fwd(x, z, o_f, o_b, conv_w, conv_norm, gla_norm4, w_out, g, w_xq, kb, vb, w_xo):
    t = x.shape[0]
    tm = min(TM, t)
    nblk = t // tm
    jmap = lambda i: i

    def body(x_ref, zq_ref, zk_ref, zv_ref, zg_ref, cb_ref, cc_ref, cu_ref, ccp_ref, ccn_ref, cup_ref, cun_ref,
             of_ref, ob_ref, cw_ref, cn_ref, gn_ref, wo_ref, g_ref, wq_ref, k_ref, v_ref, wx_ref,
             x1_ref, x2_ref, xn_ref, q_ref, a_ref, y_ref, opre_ref):
        j = pl.program_id(0)
        zv = zv_ref[...]
        hsel = jnp.where((_iota((GK, GV), 0) >> 6) == (_iota((GK, GV), 1) >> 7), 1.0, 0.0).astype(BF16)
        sb = _dot_exact_rhs((zq_ref[...] * 0.125) * zk_ref[...], hsel, 2)
        o_pre = of_ref[...] + ob_ref[...] - sb * zv
        opre_ref[...] = o_pre
        on, _ = _head_norm(o_pre)
        zg = zg_ref[...]
        y_ref[:, CW:] = (on * gn_ref[...] * (zg * _sigmoid(zg))).astype(BF16)
        cb = cb_ref[...]
        _, _, _, conv = _conv_parts(cb, cc_ref[...], cu_ref[...], ccp_ref[pl.ds(7, 1), :], cup_ref[pl.ds(7, 1), :],
                                    ccn_ref[pl.ds(0, 1), :], cun_ref[pl.ds(0, 1), :], cw_ref, j == 0,
                                    j == nblk - 1, tm)
        yc = cb * conv
        gm = _dot_exact_rhs(yc * yc, _group_ones(), 2) * (1.0 / 64.0)
        y_ref[:, :CW] = (yc * lax.rsqrt(gm + EPS) * cn_ref[...]).astype(BF16)

        x1 = x_ref[...] + jnp.dot(y_ref[...], wo_ref[...], preferred_element_type=F32)
        x1_ref[...] = x1
        xn, _ = _rms(x1, g_ref[...])
        xb = xn.astype(BF16)
        xn_ref[...] = xb
        qb = jnp.dot(xb, wq_ref[...], preferred_element_type=F32).astype(BF16)
        q_ref[...] = qb
        for h in range(NH):
            hs = slice(h * XD, (h + 1) * XD)
            p = _softmax_head(qb[:, hs], k_ref[:, hs])
            a_ref[:, hs] = _dot(p, v_ref[:, hs]).astype(BF16)
        x2_ref[...] = x1 + jnp.dot(a_ref[...], wx_ref[...], preferred_element_type=F32)

    tok = lambda i: (i, 0)
    full = lambda i: (0, 0)
    once = pl.Buffered(1)
    tokd, tokv = pl.BlockSpec((tm, D), tok), pl.BlockSpec((tm, GV), tok)
    weight = pl.BlockSpec((D, D), full, pipeline_mode=once)
    ccp, ccn = _halo_specs(tm, nblk, t, ZB_CC, jmap)
    cup, cun = _halo_specs(tm, nblk, t, ZB_CU, jmap)
    in_specs = [tokd, _zspec(tm, GK, ZB_Q, jmap), _zspec(tm, GK, ZB_K, jmap), _zspec(tm, GV, ZB_V, jmap),
                _zspec(tm, GV, ZB_G, jmap), _zspec(tm, CW, ZB_CB, jmap), _zspec(tm, CW, ZB_CC, jmap),
                _zspec(tm, CW, ZB_CU, jmap), ccp, ccn, cup, cun, tokv, tokv,
                pl.BlockSpec((3, CW), full), pl.BlockSpec((1, CW), full), pl.BlockSpec((1, GV), full),
                weight, pl.BlockSpec((1, D), full), weight, pl.BlockSpec((NMEM, D), full),
                pl.BlockSpec((NMEM, D), full), weight]
    return pl.pallas_call(
        body, name="attn_fwd", grid=(nblk,), in_specs=in_specs, out_specs=[tokd] * 6 + [tokv],
        out_shape=[jax.ShapeDtypeStruct((t, D), F32), jax.ShapeDtypeStruct((t, D), F32),
                   jax.ShapeDtypeStruct((t, D), BF16), jax.ShapeDtypeStruct((t, D), BF16),
                   jax.ShapeDtypeStruct((t, D), BF16), jax.ShapeDtypeStruct((t, D), BF16),
                   jax.ShapeDtypeStruct((t, GV), F32)],
        compiler_params=_cparams(("arbitrary",)))(
            x, z, z, z, z, z, z, z, z, z, z, z, o_f, o_b, conv_w, conv_norm, gla_norm4, w_out, g, w_xq, kb, vb, w_xo)


def _mlp_fwd(x2, g, w_up_t, w_down, fg, target):
    t = x2.shape[0]
    tm = min(TM_MLP_FWD, t)

    def body(x_ref, g_ref, wu_ref, wd_ref, fg_ref, t_ref, h1_ref, xn_ref, dx_ref, dxb_ref, loss_ref, dfg_ref, ab):
        @pl.when(pl.program_id(0) == 0)
        def _():
            loss_ref[...] = jnp.zeros_like(loss_ref)
            dfg_ref[...] = jnp.zeros_like(dfg_ref)

        x = x_ref[...]
        xn, _ = _rms(x, g_ref[...])
        xnb = xn.astype(BF16)
        xn_ref[...] = xnb
        for q in range(FF // TF):
            cols = slice(q * TF, (q + 1) * TF)
            h1 = _dot_nt(xnb, wu_ref[cols, :])
            h1_ref[:, cols] = h1.astype(BF16)
            hr = jnp.maximum(h1, 0.0)
            ab[:, cols] = (hr * hr).astype(BF16)
        x3 = x + jnp.dot(ab[...], wd_ref[...], preferred_element_type=F32)
        y, r = _rms(x3, fg_ref[...])
        e = y - t_ref[...]
        row = jnp.mean(e * e, axis=-1, keepdims=True)
        _acc_rows(loss_ref, 0.5 * jnp.sum(row, axis=0, keepdims=True))
        dx, dfg = _rms_bwd(x3, r, fg_ref[...], e * (1.0 / D))
        dx_ref[...] = dx
        dxb_ref[...] = dx.astype(BF16)
        _acc_rows(dfg_ref, dfg)

    tok = lambda i: (i, 0)
    full = lambda i: (0, 0)
    once = pl.Buffered(1)
    return pl.pallas_call(
        body, name="mlp_fwd", grid=(t // tm,),
        in_specs=[pl.BlockSpec((tm, D), tok), pl.BlockSpec((1, D), full),
                  pl.BlockSpec((FF, D), full, pipeline_mode=once), pl.BlockSpec((FF, D), full, pipeline_mode=once),
                  pl.BlockSpec((1, D), full), pl.BlockSpec((tm, D), tok)],
        out_specs=[pl.BlockSpec((tm, FF), tok), pl.BlockSpec((tm, D), tok), pl.BlockSpec((tm, D), tok),
                   pl.BlockSpec((tm, D), tok), pl.BlockSpec((8, 128), full), pl.BlockSpec((8, D), full)],
        out_shape=[jax.ShapeDtypeStruct((t, FF), BF16), jax.ShapeDtypeStruct((t, D), BF16),
                   jax.ShapeDtypeStruct((t, D), F32), jax.ShapeDtypeStruct((t, D), BF16),
                   jax.ShapeDtypeStruct((8, 128), F32), jax.ShapeDtypeStruct((8, D), F32)],
        scratch_shapes=[pltpu.VMEM((tm, FF), BF16)],
        compiler_params=_cparams(("arbitrary",)))(x2, g, w_up_t, w_down, fg, target)


def _mlp_bwd(dx3, dx3b, h1b, w_down, w_up_t, x2, g):
    t = x2.shape[0]
    tm = min(TM_MLP, t)

    def body(dx_ref, dxb_ref, h1_ref, wd_ref, wu_ref, x_ref, g_ref, a_ref, dh_ref, dx2_ref, dx2b_ref, dg_ref):
        @pl.when(pl.program_id(0) == 0)
        def _():
            dg_ref[...] = jnp.zeros_like(dg_ref)

        for q in range(FF // TF):
            cols = slice(q * TF, (q + 1) * TF)
            hr = jnp.maximum(h1_ref[:, cols].astype(F32), 0.0)
            da = _dot_nt(dxb_ref[...], wd_ref[cols, :])
            a_ref[:, cols] = (hr * hr).astype(BF16)
            dh_ref[:, cols] = (da * 2.0 * hr).astype(BF16)
        dxn = jnp.dot(dh_ref[...], wu_ref[...], preferred_element_type=F32)
        x = x_ref[...]
        r = lax.rsqrt(jnp.mean(x * x, axis=-1, keepdims=True) + EPS)
        dx, dg = _rms_bwd(x, r, g_ref[...], dxn)
        dx2 = dx_ref[...] + dx
        dx2_ref[...] = dx2
        dx2b_ref[...] = dx2.astype(BF16)
        _acc_rows(dg_ref, dg)

    tok = lambda i: (i, 0)
    full = lambda i: (0, 0)
    once = pl.Buffered(1)
    return pl.pallas_call(
        body, name="mlp_bwd", grid=(t // tm,),
        in_specs=[pl.BlockSpec((tm, D), tok), pl.BlockSpec((tm, D), tok), pl.BlockSpec((tm, FF), tok),
                  pl.BlockSpec((FF, D), full, pipeline_mode=once), pl.BlockSpec((FF, D), full, pipeline_mode=once),
                  pl.BlockSpec((tm, D), tok), pl.BlockSpec((1, D), full)],
        out_specs=[pl.BlockSpec((tm, FF), tok), pl.BlockSpec((tm, FF), tok), pl.BlockSpec((tm, D), tok),
                   pl.BlockSpec((tm, D), tok), pl.BlockSpec((8, D), full)],
        out_shape=[jax.ShapeDtypeStruct((t, FF), BF16), jax.ShapeDtypeStruct((t, FF), BF16),
                   jax.ShapeDtypeStruct((t, D), F32), jax.ShapeDtypeStruct((t, D), BF16),
                   jax.ShapeDtypeStruct((8, D), F32)],
        compiler_params=_cparams(("arbitrary",)))(dx3, dx3b, h1b, w_down, w_up_t, x2, g)


def _attn_bwd(x1, dx2, dx2b, qb, kb, vb, w_xo, w_xq, w_out, g):
    t = x1.shape[0]
    tm = min(TM, t)

    def body(x_ref, dx2_ref, dx2b_ref, q_ref, k_ref, v_ref, wx_ref, wq_ref, wo_ref, g_ref,
             dx1_ref, dx1b_ref, dy_ref, dq_ref, dkv_ref, dg_ref):
        @pl.when(pl.program_id(0) == 0)
        def _():
            dkv_ref[...] = jnp.zeros_like(dkv_ref)
            dg_ref[...] = jnp.zeros_like(dg_ref)

        datt = _dot_nt(dx2b_ref[...], wx_ref[...]).astype(BF16)
        for h in range(NH):
            hs = slice(h * XD, (h + 1) * XD)
            q_h, k_h, v_h, da_h = q_ref[:, hs], k_ref[:, hs], v_ref[:, hs], datt[:, hs]
            p = _softmax_head(q_h, k_h)
            dp = _dot_nt(da_h, v_h)
            ds = (p * (dp - jnp.sum(dp * p, axis=-1, keepdims=True)) * (1.0 / 16.0)).astype(BF16)
            dq_ref[:, hs] = _dot(ds, k_h).astype(BF16)
            dkv_ref[:, hs] += _dot_tn(ds, q_h)
            dkv_ref[:, D + h * XD:D + (h + 1) * XD] += _dot_tn(p, da_h)
        dxn = _dot_nt(dq_ref[...], wq_ref[...])
        x = x_ref[...]
        r = lax.rsqrt(jnp.mean(x * x, axis=-1, keepdims=True) + EPS)
        dx, dg = _rms_bwd(x, r, g_ref[...], dxn)
        dx1 = dx2_ref[...] + dx
        dx1_ref[...] = dx1
        dx1b = dx1.astype(BF16)
        dx1b_ref[...] = dx1b
        dy_ref[...] = _dot_nt(dx1b, wo_ref[...])
        _acc_rows(dg_ref, dg)

    tok = lambda i: (i, 0)
    full = lambda i: (0, 0)
    return pl.pallas_call(
        body, name="attn_bwd", grid=(t // tm,),
        in_specs=[pl.BlockSpec((tm, D), tok), pl.BlockSpec((tm, D), tok), pl.BlockSpec((tm, D), tok),
                  pl.BlockSpec((tm, D), tok), pl.BlockSpec((NMEM, D), full), pl.BlockSpec((NMEM, D), full),
                  pl.BlockSpec((D, D), full), pl.BlockSpec((D, D), full), pl.BlockSpec((D, D), full),
                  pl.BlockSpec((1, D), full)],
        out_specs=[pl.BlockSpec((tm, D), tok), pl.BlockSpec((tm, D), tok), pl.BlockSpec((tm, D), tok),
                   pl.BlockSpec((tm, D), tok), pl.BlockSpec((NMEM, 2 * D), full), pl.BlockSpec((8, D), full)],
        out_shape=[jax.ShapeDtypeStruct((t, D), F32), jax.ShapeDtypeStruct((t, D), BF16),
                   jax.ShapeDtypeStruct((t, D), F32), jax.ShapeDtypeStruct((t, D), BF16),
                   jax.ShapeDtypeStruct((NMEM, 2 * D), F32), jax.ShapeDtypeStruct((8, D), F32)],
        compiler_params=_cparams(("arbitrary",)))(x1, dx2, dx2b, qb, kb, vb, w_xo, w_xq, w_out, g)


def _kv_bwd(dkv, memn, mem, g, w):
    def body(dkv_ref, mn_ref, m_ref, g_ref, w_ref, dw_ref, dg_ref):
        dkvb = dkv_ref[...].astype(BF16)
        dmn = jnp.zeros((NMEM, D), F32)
        for j in range(NDEV):
            cols = slice(j * XKV_SHARD, (j + 1) * XKV_SHARD)
            dw_ref[j] = _dot_tn(mn_ref[...], dkvb[:, cols])
            dmn += _dot_nt(dkvb[:, cols], w_ref[j])
        m = m_ref[...]
        r = lax.rsqrt(jnp.mean(m * m, axis=-1, keepdims=True) + EPS)
        dg_ref[...] = jnp.broadcast_to(jnp.sum(dmn * m * r, axis=0, keepdims=True), dg_ref.shape)

    return pl.pallas_call(
        body, name="kv_bwd",
        out_shape=[jax.ShapeDtypeStruct((NDEV, D, XKV_SHARD), F32), jax.ShapeDtypeStruct((8, D), F32)],
        compiler_params=pltpu.CompilerParams(vmem_limit_bytes=VMEM_LIMIT))(dkv, memn, mem, g, w)


def _inproj_bwd(dz, w_t, x, dx1, g, riders=()):
    t = x.shape[0]
    tm = min(TM, t)

    def body(dz_ref, w_ref, x_ref, dx1_ref, g_ref, gx_ref, dg_ref):
        @pl.when(pl.program_id(0) == 0)
        def _():
            dg_ref[...] = jnp.zeros_like(dg_ref)

        dh = jnp.dot(dz_ref[...], w_ref[...], preferred_element_type=F32)
        x = x_ref[...]
        r = lax.rsqrt(jnp.mean(x * x, axis=-1, keepdims=True) + EPS)
        dx, dg = _rms_bwd(x, r, g_ref[...], dh)
        gx_ref[...] = dx1_ref[...] + dx
        _acc_rows(dg_ref, dg)

    tok = lambda i: (i, 0)
    full = lambda i: (0, 0)
    return _call(
        body, "inproj_bwd", (t // tm,),
        [pl.BlockSpec((tm, ZC), tok), pl.BlockSpec((ZC, D), full), pl.BlockSpec((tm, D), tok),
         pl.BlockSpec((tm, D), tok), pl.BlockSpec((1, D), full)],
        [pl.BlockSpec((tm, D), tok), pl.BlockSpec((8, D), full)],
        [jax.ShapeDtypeStruct((t, D), F32), jax.ShapeDtypeStruct((8, D), F32)], [], (dz, w_t, x, dx1, g), riders)


def _matmul_tn(a, b, name, rows=None, riders=()):
    t, k = a.shape
    n = b.shape[1]
    tk, tn = [1024 if size % 1024 == 0 else 640 for size in (k, n)]
    tt = min(TT, t)
    rows = rows or k

    def body(a_ref, b_ref, o_ref):
        @pl.when(pl.program_id(2) == 0)
        def _():
            o_ref[...] = jnp.zeros_like(o_ref)

        o_ref[...] += _dot_tn(a_ref[...], b_ref[...])

    return _call(
        body, name, (k // tk, n // tn, t // tt),
        [pl.BlockSpec((tt, tk), lambda i, j, s: (s, i)), pl.BlockSpec((tt, tn), lambda i, j, s: (s, j))],
        [pl.BlockSpec((tk, tn), lambda i, j, s: (i, j))], [jax.ShapeDtypeStruct((rows, n), F32)], [], (a, b), riders)


def _lane_head(shape, dim, shift):
    return _iota(shape, dim) >> shift


def _gla_recompute(q_raw, k, lr, wpad, bias, rev, tb):
    pre = _dot(lr, wpad) + bias
    la = (jnp.minimum(pre, 0.0) - jnp.log(1.0 + jnp.exp(-jnp.abs(pre)))) * (1.0 / 16.0)
    r, c = _iota((tb, tb), 0), _iota((tb, tb), 1)
    tri = (c >= r) if rev else (c <= r)
    cum = jnp.where(((r >> 6) == (c >> 6)) & tri, 1.0, 0.0).astype(BF16)
    b = _dot_exact_lhs(cum, la, 3)
    e, ei = jnp.exp(b), jnp.exp(-b)
    qt = (q_raw * 0.125) * e
    kt = k * ei
    return pre, b, e, ei, qt, kt


def _stack_heads(x, shift):
    head = _lane_head(x.shape, 1, shift)
    return jnp.concatenate([jnp.where(head == h, x, 0.0) for h in range(NH)], axis=0).astype(BF16)


def _fold_heads(x, shift):
    head = _lane_head((CH, x.shape[1]), 1, shift)
    return functools.reduce(lambda a, b: a + b,
                            [jnp.where(head == h, x[h * CH:(h + 1) * CH], 0.0) for h in range(NH)])


def _wide_mask(rev):
    r, s = _iota((CH, NH * CH), 0), _iota((CH, NH * CH), 1) & (CH - 1)
    return (s >= r) if rev else (s <= r)


def _state_mask():
    return (_iota((GV, GK), 0) >> 7) == (_iota((GV, GK), 1) >> 6)


def _state_expand(sd):
    head = _lane_head(sd.shape, 1, 6)
    return jnp.concatenate([jnp.where(head == h, sd, 0.0) for h in range(NH)], axis=0)


def _conv_parts(cb, cc, cu, ccp, cup, ccn, cun, cw_ref, first, last, tb):
    h = cc * cu
    hp = jnp.where(first, 0.0, ccp * cup)
    hn = jnp.where(last, 0.0, ccn * cun)
    rows = _iota(h.shape, 0)
    h_m1 = jnp.where(rows == 0, hp, pltpu.roll(h, 1, 0))
    h_p1 = jnp.where(rows == tb - 1, hn, pltpu.roll(h, tb - 1, 0))
    conv = cw_ref[pl.ds(0, 1), :] * h_m1 + cw_ref[pl.ds(1, 1), :] * h + cw_ref[pl.ds(2, 1), :] * h_p1
    return h, h_m1, h_p1, conv


def _group_ones():
    return jnp.where((_iota((CW, CW), 0) >> 6) == (_iota((CW, CW), 1) >> 6), 1.0, 0.0).astype(BF16)


def _head_norm(o):
    ons, rs = [], []
    for h in range(NH):
        slab = o[:, h * 128:(h + 1) * 128]
        r = lax.rsqrt(jnp.mean(slab * slab, axis=-1, keepdims=True) + EPS)
        ons.append(slab * r)
        rs.append(jnp.broadcast_to(r, slab.shape))
    return jnp.concatenate(ons, axis=1), jnp.concatenate(rs, axis=1)


def _zspec(tb, width, blk, jmap):
    return pl.BlockSpec((tb, width), lambda i: (jmap(i), blk))


def _halo_specs(tb, nblk, t, blk, jmap):
    prev = pl.BlockSpec((8, CW), lambda i: (jnp.maximum(jmap(i) * (tb // 8) - 1, 0), blk))
    nxt = pl.BlockSpec((8, CW), lambda i: (jnp.minimum((jmap(i) + 1) * (tb // 8), t // 8 - 1), blk))
    return prev, nxt


def _gla_fwd_block(q_ref, k_ref, v_ref, lr_ref, w_ref, bias_ref, o_ref, sd_ref, st, b_scr, rev, tb):
    nb = tb // CH
    _, b, _, _, qt, kt = _gla_recompute(q_ref[...], k_ref[...], lr_ref[...], w_ref[...], bias_ref[...], rev, tb)
    v = v_ref[...]
    b_scr[...] = b
    yield
    maskw, bd = _wide_mask(rev), _state_mask()
    order = list(reversed(range(nb))) if rev else list(range(nb))
    rows = [slice(c * CH, (c + 1) * CH) for c in range(nb)]
    state = st[...]
    for c in order:
        gdec = jnp.exp(b_scr[pl.ds(c * CH + (0 if rev else CH - 1), 1), :])
        sd_ref[c] = state[0:128] + state[128:256] + state[256:384] + state[384:512]
        a = jnp.where(maskw, _dot_nt(qt[rows[c]], _stack_heads(kt[rows[c]], 6)), 0.0)
        o_ref[pl.ds(c * CH, CH), :] = _dot(a, _stack_heads(v[rows[c]], 7)) + _dot_nt(qt[rows[c]], state)
        state = state * gdec + jnp.where(bd, _dot_tn(v[rows[c]], kt[rows[c]] * gdec), 0.0)
        yield
    st[...] = state
    yield


def _gla_fwd(z, waf_pad, b_af, wab_pad, b_ab, riders=()):
    t = z.shape[0]
    tb = min(TB, t)
    nblk, nb = t // tb, tb // CH
    jmaps = (lambda i: i, lambda i: nblk - 1 - i)

    def body(qf, kf, vf, lrf, qr, kr, vr, lrr, wf, bf, wr, br, of_ref, sdf_ref, or_ref, sdr_ref,
             st_f, st_r, b_f, b_r):
        @pl.when(pl.program_id(0) == 0)
        def _():
            st_f[...] = jnp.zeros_like(st_f)
            st_r[...] = jnp.zeros_like(st_r)

        for _ in zip(_gla_fwd_block(qf, kf, vf, lrf, wf, bf, of_ref, sdf_ref, st_f, b_f, False, tb),
                     _gla_fwd_block(qr, kr, vr, lrr, wr, br, or_ref, sdr_ref, st_r, b_r, True, tb)):
            pass

    full = lambda i: (0, 0)
    zspecs = [s for jm in jmaps for s in (_zspec(tb, GK, ZB_Q, jm), _zspec(tb, GK, ZB_K, jm),
                                         _zspec(tb, GV, ZB_V, jm), _zspec(tb, 128, ZB_LR, jm))]
    wspecs = [pl.BlockSpec((128, GK), full), pl.BlockSpec((1, GK), full)] * 2
    out_specs = [s for jm in jmaps for s in (pl.BlockSpec((tb, GV), lambda i, jm=jm: (jm(i), 0)),
                                             pl.BlockSpec((nb, 128, GK), lambda i, jm=jm: (jm(i), 0, 0)))]
    out_shape = [jax.ShapeDtypeStruct((t, GV), F32), jax.ShapeDtypeStruct((t // CH, 128, GK), F32)] * 2
    scratch = [pltpu.VMEM((GV, GK), F32), pltpu.VMEM((GV, GK), F32), pltpu.VMEM((tb, GK), F32),
               pltpu.VMEM((tb, GK), F32)]
    return _call(body, "gla_fwd", (nblk,), zspecs + wspecs, out_specs, out_shape, scratch,
                 [z] * 8 + [waf_pad, b_af, wab_pad, b_ab], riders)


def _gla_bwd_chunks(do_ref, sd_ref, dst, b_scr, db_scr, dq_ref, dk_ref, dv_ref, qt, kt, e, ei, v, rev, nb):
    maskw, bd = _wide_mask(rev), _state_mask()
    for c in (range(nb) if rev else reversed(range(nb))):
        sl = slice(c * CH, (c + 1) * CH)
        grow = c * CH + (0 if rev else CH - 1)
        gdec = jnp.exp(b_scr[pl.ds(grow, 1), :])
        qt_c, kt_c, v_c, do_c = qt[sl], kt[sl], v[sl], do_ref[pl.ds(c * CH, CH), :]
        s_in = _state_expand(sd_ref[c])
        ds_out = dst[...]
        kbd, vbd = _stack_heads(kt_c, 6), _stack_heads(v_c, 7)
        a = jnp.where(maskw, _dot_nt(qt_c, kbd), 0.0)
        da = jnp.where(maskw, _dot_nt(do_c, vbd), 0.0)
        dv_ref[pl.ds(c * CH, CH), :] = _fold_heads(_dot_tn(a, do_c), 7) + _dot_nt(kt_c * gdec, ds_out)
        da_do = jnp.concatenate([da.astype(BF16), do_c.astype(BF16)], axis=1)
        dqt = _dot(da_do, jnp.concatenate([kbd, s_in.astype(BF16)], axis=0))
        dkh = _dot(v_c, ds_out)
        both = _dot_tn(da_do, qt_c)
        dkt = _fold_heads(both[:NH * CH], 6) + dkh * gdec
        dg = jnp.sum(ds_out * s_in, axis=0, keepdims=True) + jnp.sum(kt_c * dkh, axis=0, keepdims=True)
        db_scr[pl.ds(c * CH, CH), :] = dqt * qt_c - dkt * kt_c
        db_scr[pl.ds(grow, 1), :] += dg * gdec
        dq_ref[pl.ds(c * CH, CH), :] = dqt * e[sl] * 0.125
        dk_ref[pl.ds(c * CH, CH), :] = dkt * ei[sl]
        dst[...] = ds_out * gdec + jnp.where(bd, both[NH * CH:], 0.0)
        yield


def _gate_bwd(db, pre, lr, wpad, rev, tb):
    r, c = _iota((tb, tb), 0), _iota((tb, tb), 1)
    tri = (c <= r) if rev else (c >= r)
    cum_t = jnp.where(((r >> 6) == (c >> 6)) & tri, 1.0, 0.0).astype(BF16)
    dla = _dot_exact_lhs(cum_t, db, 2)
    dpre = dla * (1.0 / 16.0) / (1.0 + jnp.exp(pre))
    return dpre, _dot_nt(dpre, wpad), _dot_tn(lr, dpre)


def _gla_bwd_first(z, dy, o_pre, sd, wpad, bias, conv_w, conv_norm, gla_norm4, riders=()):
    t = z.shape[0]
    tb = min(TB_BWD, t)
    nblk, nb = t // tb, tb // CH
    jmap = lambda i: nblk - 1 - i

    def body(q_ref, k_ref, v_ref, lr_ref, g_ref, cb_ref, cc_ref, cu_ref, ccp_ref, ccn_ref, cup_ref, cun_ref,
             dy_ref, opre_ref, sd_ref, w_ref, bias_ref, cw_ref, cn_ref, gn_ref,
             do_ref, dq_ref, dk_ref, dv_ref, dlr_ref, dzg_ref, dzcb_ref, dconv_ref,
             dw_ref, dbias_ref, dcw_ref, dcn_ref, dgn_ref, dst, b_scr, db_scr):
        i = pl.program_id(0)
        j = jmap(i)

        @pl.when(i == 0)
        def _():
            dst[...] = jnp.zeros_like(dst)
            for ref in (dw_ref, dbias_ref, dcw_ref, dcn_ref, dgn_ref):
                ref[...] = jnp.zeros_like(ref)

        dyg = dy_ref[:, CW:]
        g = g_ref[...]
        sig = _sigmoid(g)
        on, rr = _head_norm(opre_ref[...])
        gn = gn_ref[...]
        dzg_ref[...] = (dyg * on * gn * (sig * (1.0 + g * (1.0 - sig)))).astype(BF16)
        don = dyg * (g * sig)
        _acc_rows(dgn_ref, jnp.sum(don * on, axis=0, keepdims=True))
        u = don * gn
        uo = u * on
        mean_uo = jnp.concatenate(
            [jnp.broadcast_to(jnp.mean(uo[:, h * 128:(h + 1) * 128], axis=-1, keepdims=True), (tb, 128))
             for h in range(NH)], axis=1)
        do_ref[...] = rr * (u - on * mean_uo)

        def conv_branch():
            cb = cb_ref[...]
            h, h_m1, h_p1, conv = _conv_parts(cb, cc_ref[...], cu_ref[...], ccp_ref[pl.ds(7, 1), :],
                                              cup_ref[pl.ds(7, 1), :], ccn_ref[pl.ds(0, 1), :],
                                              cun_ref[pl.ds(0, 1), :], cw_ref, j == 0, j == nblk - 1, tb)
            yc = cb * conv
            yield
            ones = _group_ones()
            rc = lax.rsqrt(_dot_exact_rhs(yc * yc, ones, 2) * (1.0 / 64.0) + EPS)
            ycr = yc * rc
            yield
            dyn = dy_ref[:, :CW]
            _acc_rows(dcn_ref, jnp.sum(dyn * ycr, axis=0, keepdims=True))
            uc = dyn * cn_ref[...]
            yield
            dyc = rc * (uc - ycr * (_dot_exact_rhs(uc * ycr, ones, 2) * (1.0 / 64.0)))
            dzcb_ref[...] = (dyc * conv).astype(BF16)
            yield
            dconv = dyc * cb
            dconv_ref[...] = dconv
            yield
            dcw_ref[pl.ds(0, 1), :] += jnp.sum(dconv * h_m1, axis=0, keepdims=True)
            dcw_ref[pl.ds(1, 1), :] += jnp.sum(dconv * h, axis=0, keepdims=True)
            dcw_ref[pl.ds(2, 1), :] += jnp.sum(dconv * h_p1, axis=0, keepdims=True)
            yield

        lr, wp = lr_ref[...], w_ref[...]
        pre, b, e, ei, qt, kt = _gla_recompute(q_ref[...], k_ref[...], lr, wp, bias_ref[...], False, tb)
        b_scr[...] = b
        for _ in itertools.zip_longest(
                _gla_bwd_chunks(do_ref, sd_ref, dst, b_scr, db_scr, dq_ref, dk_ref, dv_ref, qt, kt, e, ei, v_ref[...],
                                False, nb), conv_branch()):
            pass
        dpre, dlr, dw = _gate_bwd(db_scr[...], pre, lr, wp, False, tb)
        dlr_ref[...] = dlr
        dw_ref[...] += dw
        _acc_rows(dbias_ref, jnp.sum(dpre, axis=0, keepdims=True))

    full = lambda i: (0, 0)
    tokv = pl.BlockSpec((tb, GV), lambda i: (jmap(i), 0))
    tokk = pl.BlockSpec((tb, GK), lambda i: (jmap(i), 0))
    ccp, ccn = _halo_specs(tb, nblk, t, ZB_CC, jmap)
    cup, cun = _halo_specs(tb, nblk, t, ZB_CU, jmap)
    in_specs = [_zspec(tb, GK, ZB_Q, jmap), _zspec(tb, GK, ZB_K, jmap), _zspec(tb, GV, ZB_V, jmap),
                _zspec(tb, 128, ZB_LR, jmap), _zspec(tb, GV, ZB_G, jmap), _zspec(tb, CW, ZB_CB, jmap),
                _zspec(tb, CW, ZB_CC, jmap), _zspec(tb, CW, ZB_CU, jmap), ccp, ccn, cup, cun,
                pl.BlockSpec((tb, D), lambda i: (jmap(i), 0)), tokv,
                pl.BlockSpec((nb, 128, GK), lambda i: (jmap(i), 0, 0)), pl.BlockSpec((128, GK), full),
                pl.BlockSpec((1, GK), full), pl.BlockSpec((3, CW), full), pl.BlockSpec((1, CW), full),
                pl.BlockSpec((1, GV), full)]
    out_specs = [tokv, tokk, tokk, tokv, pl.BlockSpec((tb, 128), lambda i: (jmap(i), 0)), tokv, tokv, tokv,
                 pl.BlockSpec((128, GK), full), pl.BlockSpec((8, GK), full), pl.BlockSpec((8, CW), full),
                 pl.BlockSpec((8, CW), full), pl.BlockSpec((8, GV), full)]
    out_shape = [jax.ShapeDtypeStruct((t, GV), F32), jax.ShapeDtypeStruct((t, GK), F32),
                 jax.ShapeDtypeStruct((t, GK), F32), jax.ShapeDtypeStruct((t, GV), F32),
                 jax.ShapeDtypeStruct((t, 128), F32), jax.ShapeDtypeStruct((t, GV), BF16),
                 jax.ShapeDtypeStruct((t, CW), BF16), jax.ShapeDtypeStruct((t, CW), F32),
                 jax.ShapeDtypeStruct((128, GK), F32), jax.ShapeDtypeStruct((8, GK), F32),
                 jax.ShapeDtypeStruct((8, CW), F32), jax.ShapeDtypeStruct((8, CW), F32),
                 jax.ShapeDtypeStruct((8, GV), F32)]
    return _call(
        body, "gla_bwd_first", (nblk,), in_specs, out_specs, out_shape,
        [pltpu.VMEM((GV, GK), F32), pltpu.VMEM((tb, GK), F32), pltpu.VMEM((tb, GK), F32)],
        (z, z, z, z, z, z, z, z, z, z, z, z, dy, o_pre, sd, wpad, bias, conv_w, conv_norm, gla_norm4), riders)


def _gla_bwd_second(z, do, sd, wpad, bias, dqa, dka, dva, dlra, dzg, dzcb, dconv, conv_w, riders=()):
    t = z.shape[0]
    tb = min(TB_BWD, t)
    nblk, nb = t // tb, tb // CH
    jmap = lambda i: i

    def body(q_ref, k_ref, v_ref, lr_ref, cc_ref, cu_ref, do_ref, sd_ref, w_ref, bias_ref, dqa_ref, dka_ref,
             dva_ref, dlra_ref, dzg_ref, dzcb_ref, dc_ref, dcp_ref, dcn_ref, cw_ref,
             dz_ref, dw_ref, dbias_ref, dst, b_scr, db_scr, dq_scr, dk_scr, dv_scr, sb_scr, dsk_scr):
        i = pl.program_id(0)

        @pl.when(i == 0)
        def _():
            dst[...] = jnp.zeros_like(dst)
            dw_ref[...] = jnp.zeros_like(dw_ref)
            dbias_ref[...] = jnp.zeros_like(dbias_ref)

        q_raw, k, v, lr, wp = q_ref[...], k_ref[...], v_ref[...], lr_ref[...], w_ref[...]
        pre, b, e, ei, qt, kt = _gla_recompute(q_raw, k, lr, wp, bias_ref[...], True, tb)
        b_scr[...] = b

        def token_local():
            dc = dc_ref[...]
            rows = _iota(dc.shape, 0)
            dprev = jnp.where(i == 0, 0.0, dcp_ref[pl.ds(7, 1), :])
            dnext = jnp.where(i == nblk - 1, 0.0, dcn_ref[pl.ds(0, 1), :])
            dc_m1 = jnp.where(rows == 0, dprev, pltpu.roll(dc, 1, 0))
            dc_p1 = jnp.where(rows == tb - 1, dnext, pltpu.roll(dc, tb - 1, 0))
            yield
            dh = cw_ref[pl.ds(0, 1), :] * dc_p1 + cw_ref[pl.ds(1, 1), :] * dc + cw_ref[pl.ds(2, 1), :] * dc_m1
            dz_ref[:, 0:512] = dzcb_ref[...]
            yield
            dz_ref[:, 512:1024] = (dh * cu_ref[...]).astype(BF16)
            dz_ref[:, 1024:1536] = (dh * cc_ref[...]).astype(BF16)
            dz_ref[:, 2560:3072] = dzg_ref[...]
            yield
            hsel = jnp.where((_iota((GK, GV), 0) >> 6) == (_iota((GK, GV), 1) >> 7), 1.0, 0.0).astype(BF16)
            sb_scr[...] = _dot_exact_rhs((q_raw * 0.125) * k, hsel, 2)
            yield
            hsel_t = jnp.where((_iota((GV, GK), 0) >> 7) == (_iota((GV, GK), 1) >> 6), 1.0, 0.0).astype(BF16)
            dsk_scr[...] = _dot_exact_rhs(do_ref[...] * v, hsel_t, 2)
            yield

        for _ in itertools.zip_longest(
                _gla_bwd_chunks(do_ref, sd_ref, dst, b_scr, db_scr, dq_scr, dk_scr, dv_scr, qt, kt, e, ei, v, True, nb),
                token_local()):
            pass
        dpre, dlr, dw = _gate_bwd(db_scr[...], pre, lr, wp, True, tb)
        dw_ref[...] += dw
        _acc_rows(dbias_ref, jnp.sum(dpre, axis=0, keepdims=True))
        dsk = dsk_scr[...]
        dz_ref[:, 1536:1792] = (dqa_ref[...] + dq_scr[...] - dsk * k * 0.125).astype(BF16)
        dz_ref[:, 1792:2048] = (dka_ref[...] + dk_scr[...] - dsk * (q_raw * 0.125)).astype(BF16)
        dz_ref[:, 2048:2560] = (dva_ref[...] + dv_scr[...] - sb_scr[...] * do_ref[...]).astype(BF16)
        dz_ref[:, 3072:3200] = (dlra_ref[...] + dlr).astype(BF16)

    full = lambda i: (0, 0)
    tokv = pl.BlockSpec((tb, GV), lambda i: (i, 0))
    tokk = pl.BlockSpec((tb, GK), lambda i: (i, 0))
    dcp = pl.BlockSpec((8, CW), lambda i: (jnp.maximum(i * (tb // 8) - 1, 0), 0))
    dcn = pl.BlockSpec((8, CW), lambda i: (jnp.minimum((i + 1) * (tb // 8), t // 8 - 1), 0))
    in_specs = [_zspec(tb, GK, ZB_Q, jmap), _zspec(tb, GK, ZB_K, jmap), _zspec(tb, GV, ZB_V, jmap),
                _zspec(tb, 128, ZB_LR, jmap), _zspec(tb, CW, ZB_CC, jmap), _zspec(tb, CW, ZB_CU, jmap), tokv,
                pl.BlockSpec((nb, 128, GK), lambda i: (i, 0, 0)), pl.BlockSpec((128, GK), full),
                pl.BlockSpec((1, GK), full), tokk, tokk, tokv, pl.BlockSpec((tb, 128), lambda i: (i, 0)), tokv, tokv,
                tokv, dcp, dcn, pl.BlockSpec((3, CW), full)]
    out_specs = [pl.BlockSpec((tb, ZC), lambda i: (i, 0)), pl.BlockSpec((128, GK), full), pl.BlockSpec((8, GK), full)]
    out_shape = [jax.ShapeDtypeStruct((t, ZC), BF16), jax.ShapeDtypeStruct((128, GK), F32),
                 jax.ShapeDtypeStruct((8, GK), F32)]
    return _call(
        body, "gla_bwd_second", (nblk,), in_specs, out_specs, out_shape,
        [pltpu.VMEM((GV, GK), F32), pltpu.VMEM((tb, GK), F32), pltpu.VMEM((tb, GK), F32),
         pltpu.VMEM((tb, GK), F32), pltpu.VMEM((tb, GK), F32), pltpu.VMEM((tb, GV), F32),
         pltpu.VMEM((tb, GV), F32), pltpu.VMEM((tb, GK), F32)],
        (z, z, z, z, z, z, do, sd, wpad, bias, dqa, dka, dva, dlra, dzg, dzcb, dconv, dconv, dconv, conv_w), riders)


def _step(x, mem, target, shard, small_pack, vec, place):
    own, from_chips = {}, {}

    def pair_sums(names, g4, from_sibling):
        pbs = []
        for n, g, s in zip(names, g4, from_sibling):
            pb, own[n] = _rs_pair_sum(place, g, s, "pair_sum_" + n)
            pbs.append(pb)
        return pbs

    def by_dest(g, n):
        return g.reshape((4, 2) + shard[n].shape)

    w_in, small_all = _exchange(_gather_rider([shard["w_in"], small_pack]), "gather_w_in")
    w_in = jnp.pad(w_in.reshape(ZW, D), ((0, ZC - ZW), (0, 0)))
    small_all = small_all.reshape(NDEV, -1)
    p, off = {}, 0
    for n, (r, c) in SMALL_SHARDED.items():
        p[n] = small_all[:, off:off + r * c].reshape(NDEV, r, c).transpose(1, 0, 2).reshape(r, NDEV * c)
        off += r * c
    zeros_lr = jnp.zeros((128 - LR, GK), BF16)
    waf_pad = jnp.concatenate([p["w_af"].astype(BF16), zeros_lr], axis=0)
    wab_pad = jnp.concatenate([jnp.zeros((LR, GK), BF16), p["w_ab"].astype(BF16), zeros_lr[:128 - 2 * LR]], axis=0)
    gla_norm4 = jnp.tile(vec["gla_norm"], (1, NH))

    z, hb, w_out, w_xq, w_xo, w_xkv = _inproj(
        x, vec["mix_norm"], w_in, [_gather_rider([shard[n] for n in ("w_out", "w_xq", "w_xo", "w_xkv")])])
    w_out, w_xq, w_xo = [a.reshape(D, D) for a in (w_out, w_xq, w_xo)]
    o_f, sd_f, o_b, sd_b, w_up_t, w_down = _gla_fwd(
        z, waf_pad, vec["b_af"], wab_pad, vec["b_ab"], [_gather_rider([shard["w_up"], shard["w_down"]])])
    w_up_t, w_down = w_up_t.reshape(FF, D), w_down.reshape(FF, D)
    kv, memn = _kv_proj(mem, vec["mem_norm"], w_xkv)
    kb, vb = kv[:, :D].astype(BF16), kv[:, D:].astype(BF16)
    x1, x2, xn1, qb, attb, yb, o_pre = _attn_fwd(x, z, o_f, o_b, p["conv_w"], vec["conv_norm"], gla_norm4, w_out,
                                                 vec["xa_norm"], w_xq, kb, vb, w_xo)
    h1b, xn2, dx3, dx3b, loss8, dfinal = _mlp_fwd(x2, vec["mlp_norm"], w_up_t, w_down, vec["final_norm"], target)

    ab, dh1b, dx2, dx2b, dmlp = _mlp_bwd(dx3, dx3b, h1b, w_down, w_up_t, x2, vec["mlp_norm"])
    g_mlp = [by_dest(_matmul_tn(ab, dx3b, "dw_down")[0], "w_down"),
             by_dest(_matmul_tn(dh1b, xn2, "dw_up")[0], "w_up")]
    dx1, dx1b, dy, dqb, dkv, dxa = _attn_bwd(x1, dx2, dx2b, qb, kb, vb, w_xo, w_xq, w_out, vec["xa_norm"])
    dw_xo, *s_mlp = _matmul_tn(attb, dx2b, "dw_xo", riders=[_sibling_rider(g_mlp)])
    pb_mlp = pair_sums(("w_down", "w_up"), g_mlp, s_mlp)
    dw_xkv, dmemn = _kv_bwd(dkv, memn, mem, vec["mem_norm"], w_xkv)
    att_names = ("w_xo", "w_xq", "w_out", "w_xkv")
    g_att = [by_dest(g, n) for g, n in zip(
        (dw_xo, _matmul_tn(xn1, dqb, "dw_xq")[0], _matmul_tn(yb, dx1b, "dw_out")[0], dw_xkv), att_names)]
    res = _gla_bwd_first(z, dy, o_pre, sd_f, waf_pad, vec["b_af"], p["conv_w"], vec["conv_norm"], gla_norm4,
                         riders=[_chips_rider(pb_mlp), _sibling_rider(g_att)])
    do, dqa, dka, dva, dlra, dzg, dzcb, dconv, dwaf, dbaf, dcw, dcn, dgn = res[:13]
    from_chips["w_down"], from_chips["w_up"] = res[13:15]
    pb_att = pair_sums(att_names, g_att, res[15:])
    dz, dwab, dbab, *c_att = _gla_bwd_second(z, do, sd_b, wab_pad, vec["b_ab"], dqa, dka, dva, dlra, dzg, dzcb, dconv,
                                             p["conv_w"], riders=[_chips_rider(pb_att)])
    from_chips.update(zip(att_names, c_att))
    g_in = [by_dest(_matmul_tn(dz, hb, "dw_in", rows=ZW)[0], "w_in")]
    pb_in = pair_sums(("w_in",), g_in, _exchange(_sibling_rider(g_in), "grads_to_sibling_w_in"))
    grad_x, dmix, from_chips["w_in"] = _inproj_bwd(dz, w_in, x, dx1, vec["mix_norm"], riders=[_chips_rider(pb_in)])

    small_acc = dict(mix_norm=dmix, conv_w=dcw, conv_norm=dcn, w_af=dwaf, b_af=dbaf, w_ab=dwab, b_ab=dbab,
                     gla_norm=dgn, xa_norm=dxa, mem_norm=dmemn, mlp_norm=dmlp, final_norm=dfinal)
    return loss8, grad_x, small_acc, own, from_chips


def _place():
    return lax.axis_index("x"), lax.axis_index("y"), lax.axis_index("c")


class _Rider:
    def __init__(self, arrays, out_shape, scratch, start, finish):
        self.arrays, self.out_shape, self.scratch, self.start, self.finish = arrays, out_shape, scratch, start, finish


def _gather_rider(blks):
    n = len(blks)

    def plan(in_refs, out_refs, sems):
        send_sems, recv_sems, local_sems = sems
        x, y, c = _place()
        me, sibling = (x, y, c), (x, y, 1 - c)
        chips = [(1 - x, y, c), (x, 1 - y, c), (1 - x, 1 - y, c)]

        def copy(a, k, block, to, own=False):
            px, py, pc = block
            dst = out_refs[a].at[4 * px + 2 * py + pc]
            return pltpu.make_async_remote_copy(
                src_ref=in_refs[a] if own else dst, dst_ref=dst, send_sem=send_sems.at[k, a],
                recv_sem=recv_sems.at[k, a], device_id=to, device_id_type=MESH)

        def local(a):
            return pltpu.make_async_copy(in_refs[a], out_refs[a].at[4 * x + 2 * y + c], local_sems.at[a])

        def own_sends(a):
            return [copy(a, 0, me, sibling, own=True)] + [copy(a, 1 + j, me, chip, own=True)
                                                          for j, chip in enumerate(chips)]

        return copy, local, own_sends, me, sibling, chips

    def start(in_refs, out_refs, sems):
        _, local, own_sends, _, _, _ = plan(in_refs, out_refs, sems)
        for a in range(n):
            local(a).start()
            for cp in own_sends(a):
                cp.start()

    def finish(in_refs, out_refs, sems):
        copy, local, own_sends, me, sibling, chips = plan(in_refs, out_refs, sems)
        for j, chip in enumerate(chips):
            for a in range(n):
                copy(a, 1 + j, chip, me).wait_recv()
                copy(a, 4 + j, chip, sibling).start()
        for a in range(n):
            copy(a, 0, sibling, me).wait_recv()
            for j, (px, py, pc) in enumerate(chips):
                copy(a, 4 + j, (px, py, 1 - pc), me).wait_recv()
            for cp in own_sends(a) + [copy(a, 4 + j, chip, sibling) for j, chip in enumerate(chips)]:
                cp.wait_send()
            local(a).wait()

    return _Rider(blks, [jax.ShapeDtypeStruct((NDEV,) + b.shape, b.dtype) for b in blks],
                  [pltpu.SemaphoreType.DMA((7, n)), pltpu.SemaphoreType.DMA((7, n)), pltpu.SemaphoreType.DMA((n,))],
                  start, finish)


def _sibling_rider(g4s):
    n = len(g4s)

    def copies(in_refs, out_refs, sems):
        send_sems, recv_sems = sems
        x, y, c = _place()
        return [pltpu.make_async_remote_copy(
            src_ref=in_refs[a].at[k, 1 - c], dst_ref=out_refs[a].at[k], send_sem=send_sems.at[k, a],
            recv_sem=recv_sems.at[k, a], device_id=(x, y, 1 - c), device_id_type=MESH)
            for a in range(n) for k in range(4)]

    def start(in_refs, out_refs, sems):
        for cp in copies(in_refs, out_refs, sems):
            cp.start()

    def finish(in_refs, out_refs, sems):
        for cp in copies(in_refs, out_refs, sems):
            cp.wait()

    return _Rider(g4s, [jax.ShapeDtypeStruct((4,) + g.shape[2:], g.dtype) for g in g4s],
                  [pltpu.SemaphoreType.DMA((4, n)), pltpu.SemaphoreType.DMA((4, n))], start, finish)


def _chips_rider(pbs):
    n = len(pbs)

    def copies(in_refs, out_refs, sems):
        send_sems, recv_sems = sems
        x, y, c = _place()
        peers = [(1 - x, y), (x, 1 - y), (1 - x, 1 - y)]
        return [pltpu.make_async_remote_copy(
            src_ref=in_refs[a].at[2 * px + py], dst_ref=out_refs[a].at[k], send_sem=send_sems.at[k, a],
            recv_sem=recv_sems.at[k, a], device_id=(px, py, c), device_id_type=MESH)
            for a in range(n) for k, (px, py) in enumerate(peers)]

    def start(in_refs, out_refs, sems):
        for cp in copies(in_refs, out_refs, sems):
            cp.start()

    def finish(in_refs, out_refs, sems):
        for cp in copies(in_refs, out_refs, sems):
            cp.wait()

    return _Rider(pbs, [jax.ShapeDtypeStruct((3,) + p.shape[1:], p.dtype) for p in pbs],
                  [pltpu.SemaphoreType.DMA((3, n)), pltpu.SemaphoreType.DMA((3, n))], start, finish)


def _exchange(rider, name):
    n_in, n_out = len(rider.arrays), len(rider.out_shape)

    def body(*refs):
        ins, outs, sems = refs[:n_in], refs[n_in:n_in + n_out], refs[n_in + n_out:]
        rider.start(ins, outs, sems)
        rider.finish(ins, outs, sems)

    hbm = pl.BlockSpec(memory_space=pltpu.HBM)
    return pl.pallas_call(body, name=name, out_shape=rider.out_shape, in_specs=[hbm] * n_in,
                          out_specs=[hbm] * n_out, scratch_shapes=rider.scratch)(*rider.arrays)


def _rs_pair_sum(place, g4, r1, name):
    rows, cols = g4.shape[2:]
    tr = min(rows, 512)

    def body(pl_ref, g_ref, r_ref, pb_ref, own_ref):
        s = g_ref[0, 0] + r_ref[0]
        pb_ref[0] = s.astype(BF16)

        @pl.when(pl.program_id(1) == pl_ref[0])
        def _():
            own_ref[...] = s

    grid_spec = pltpu.PrefetchScalarGridSpec(
        num_scalar_prefetch=1, grid=(rows // tr, 4),
        in_specs=[pl.BlockSpec((1, 1, tr, cols), lambda r, k, p: (k, p[1], r, 0)),
                  pl.BlockSpec((1, tr, cols), lambda r, k, p: (k, r, 0))],
        out_specs=[pl.BlockSpec((1, tr, cols), lambda r, k, p: (k, r, 0)),
                   pl.BlockSpec((tr, cols), lambda r, k, p: (r, 0))])
    return pl.pallas_call(
        body, name=name, grid_spec=grid_spec,
        out_shape=[jax.ShapeDtypeStruct((4, rows, cols), BF16), jax.ShapeDtypeStruct((rows, cols), F32)],
        compiler_params=_cparams(("arbitrary", "arbitrary")))(place, g4, r1)


PACK_ROWS = 32
VEC_ROW = {"mix_norm": 0, "conv_norm": 1, "b_af": 2, "b_ab": 3, "gla_norm": 4, "xa_norm": 5, "mem_norm": 6,
           "mlp_norm": 7, "final_norm": 8}
LOSS_ROW, MAT_ROW = 9, 16
MAT_LANE = {"w_af": 0, "w_ab": GK, "conv_w": 2 * GK}
MAT_SRC_ROW = {"w_af": 0, "w_ab": LR, "conv_w": 0}


SMALL_WIDTH = {"mix_norm": D, "conv_w": 64, "conv_norm": CW, "w_af": 32, "b_af": GK, "w_ab": 32, "b_ab": GK,
               "gla_norm": 128, "xa_norm": D, "mem_norm": D, "mlp_norm": D, "final_norm": D}


def _small_reduce(acc, loss8):
    names = list(SMALL)
    n = len(names)
    widths = SMALL_WIDTH

    def body(*refs):
        acc_refs = dict(zip(names, refs[:n]))
        loss_ref, tot = refs[n], refs[n + 1]
        pk, all_ref, send_sems, recv_sems, local_sem = refs[n + 2:]

        pk[...] = jnp.zeros_like(pk)
        for k, row in VEC_ROW.items():
            if k == "gla_norm":
                g = functools.reduce(lambda a, b: a + b, [acc_refs[k][pl.ds(0, 1), pl.ds(h * 128, 128)]
                                                          for h in range(NH)])
            else:
                g = acc_refs[k][pl.ds(0, 1), :]
            pk[pl.ds(row, 1), pl.ds(0, widths[k])] = g
        pk[pl.ds(LOSS_ROW, 1), pl.ds(0, 128)] = loss_ref[pl.ds(0, 1), :]
        for k, lane in MAT_LANE.items():
            rows, cols = (3, CW) if k == "conv_w" else (LR, GK)
            pk[pl.ds(MAT_ROW, rows), pl.ds(lane, cols)] = acc_refs[k][pl.ds(MAT_SRC_ROW[k], rows), :]

        x, y, c = _place()
        me, sibling = (x, y, c), (x, y, 1 - c)
        chips = [(1 - x, y, c), (x, 1 - y, c), (1 - x, 1 - y, c)]

        def copy(k, block, to, own=False):
            px, py, pc = block
            dst = all_ref.at[4 * px + 2 * py + pc]
            return pltpu.make_async_remote_copy(
                src_ref=pk if own else dst, dst_ref=dst, send_sem=send_sems.at[k], recv_sem=recv_sems.at[k],
                device_id=to, device_id_type=MESH)

        mine = pltpu.make_async_copy(pk, all_ref.at[4 * x + 2 * y + c], local_sem)
        mine.start()
        first = [copy(0, me, sibling, own=True)] + [copy(1 + j, me, chip, own=True) for j, chip in enumerate(chips)]
        for cp in first:
            cp.start()
        passed = [copy(4 + j, chip, sibling) for j, chip in enumerate(chips)]
        for j, chip in enumerate(chips):
            copy(1 + j, chip, me).wait_recv()
            passed[j].start()
        copy(0, sibling, me).wait_recv()
        for j, (px, py, pc) in enumerate(chips):
            copy(4 + j, (px, py, 1 - pc), me).wait_recv()
        for cp in first + passed:
            cp.wait_send()
        mine.wait()
        total = all_ref[0]
        for d in range(1, NDEV):
            total = total + all_ref[d]
        tot[...] = total

    return pl.pallas_call(
        body, name="small_reduce", out_shape=jax.ShapeDtypeStruct((PACK_ROWS, D), F32),
        scratch_shapes=[pltpu.VMEM((PACK_ROWS, D), F32), pltpu.VMEM((NDEV, PACK_ROWS, D), F32),
                        pltpu.SemaphoreType.DMA((7,)), pltpu.SemaphoreType.DMA((7,)), pltpu.SemaphoreType.DMA],
    )(*[acc[k] for k in names], loss8)


def _small_adamw(tot, ws, ms, vs):
    names = list(SMALL)
    n = len(names)
    widths = SMALL_WIDTH

    def body(*refs):
        tot = refs[0]
        w_refs, m_refs, v_refs = [dict(zip(names, refs[1 + q * n:1 + (q + 1) * n])) for q in range(3)]
        outs = refs[1 + 3 * n:1 + 7 * n]
        g_out, d_out, m_out, v_out = [dict(zip(names, outs[q * n:(q + 1) * n])) for q in range(4)]
        cut = refs[1 + 7 * n]
        x, y, c = _place()
        dev = 4 * x + 2 * y + c
        for k in names:
            if k in VEC_ROW:
                g = tot[pl.ds(VEC_ROW[k], 1), pl.ds(0, widths[k])]
            else:
                rows, cols = (3, CW) if k == "conv_w" else (LR, GK)
                wd = widths[k]
                sel = jnp.where(_iota((cols, wd), 0) == dev * wd + _iota((cols, wd), 1), 1.0, 0.0).astype(BF16)
                cut[:, pl.ds(0, wd)] = _dot_exact_rhs(tot[pl.ds(MAT_ROW, LR), pl.ds(MAT_LANE[k], cols)], sel, 3)
                g = cut[pl.ds(0, rows), pl.ds(0, wd)]
            g_out[k][...] = g
            d_out[k][...], m_out[k][...], v_out[k][...] = _adamw_math(w_refs[k][...], g, m_refs[k][...],
                                                                       v_refs[k][...])

    shapes = [jax.ShapeDtypeStruct(ws[k].shape, F32) for k in names]
    res = pl.pallas_call(
        body, name="small_adamw", out_shape=shapes * 4, scratch_shapes=[pltpu.VMEM((LR, 128), F32)],
    )(tot, *[ws[k] for k in names], *[ms[k] for k in names], *[vs[k] for k in names])
    return {k: tuple(res[q * n + i] for q in range(4)) for i, k in enumerate(names)}


def _adamw_math(w, g, m, v):
    m = ADAM_B1 * m + (1.0 - ADAM_B1) * g
    v = ADAM_B2 * v + (1.0 - ADAM_B2) * (g * g)
    m_hat = m / (1.0 - ADAM_B1 ** ADAM_STEP)
    v_hat = v / (1.0 - ADAM_B2 ** ADAM_STEP)
    delta = -ADAM_LR * (m_hat / (jnp.sqrt(v_hat) + ADAM_EPS) + ADAM_WD * w)
    return delta, m, v


def _adamw(w, m, v, own, r2, name):
    _, r, c = w.shape
    tr = 256 if r % 256 == 0 else r

    def body(w_ref, m_ref, v_ref, o_ref, r_ref, g_ref, d_ref, nm_ref, nv_ref):
        g = ((o_ref[...] + r_ref[0].astype(F32)) + r_ref[1].astype(F32)) + r_ref[2].astype(F32)
        g_ref[...] = g
        d_ref[...], nm_ref[...], nv_ref[...] = _adamw_math(w_ref[...], g, m_ref[...], v_ref[...])

    spec = pl.BlockSpec((None, tr, c), lambda i: (0, i, 0))
    return pl.pallas_call(
        body, name=name, grid=(r // tr,),
        in_specs=[spec, spec, spec, pl.BlockSpec((tr, c), lambda i: (i, 0)),
                  pl.BlockSpec((3, tr, c), lambda i: (0, i, 0))],
        out_specs=[spec] * 4, out_shape=[jax.ShapeDtypeStruct((1, r, c), F32)] * 4,
        compiler_params=_cparams(("arbitrary",)))(w, m, v, own, r2)


MATS = ("w_in", "w_out", "w_xq", "w_xo", "w_xkv", "w_up", "w_down")
SMALL = ("mix_norm", "conv_w", "conv_norm", "w_af", "b_af", "w_ab", "b_ab", "gla_norm", "xa_norm", "mem_norm",
         "mlp_norm", "final_norm")
WEIGHTS = ("mix_norm", "w_in", "conv_w", "conv_norm", "w_af", "b_af", "w_ab", "b_ab", "gla_norm", "w_out", "xa_norm",
           "mem_norm", "w_xq", "w_xkv", "w_xo", "mlp_norm", "w_up", "w_down", "final_norm")
SMALL_SHARDED = {"conv_w": (3, 64), "w_af": (LR, 32), "w_ab": (LR, 32)}
SMALL_PACK_ROWS = 16


def kernel(x, mem, mix_norm, w_in, conv_w, conv_norm, w_af, b_af, w_ab, b_ab, gla_norm, w_out, xa_norm, mem_norm, w_xq, w_xkv, w_xo, mlp_norm, w_up, w_down, final_norm, loss_target, m_mix_norm, m_w_in, m_conv_w, m_conv_norm, m_w_af, m_b_af, m_w_ab, m_b_ab, m_gla_norm, m_w_out, m_xa_norm, m_mem_norm, m_w_xq, m_w_xkv, m_w_xo, m_mlp_norm, m_w_up, m_w_down, m_final_norm, v_mix_norm, v_w_in, v_conv_w, v_conv_norm, v_w_af, v_b_af, v_w_ab, v_b_ab, v_gla_norm, v_w_out, v_xa_norm, v_mem_norm, v_w_xq, v_w_xkv, v_w_xo, v_mlp_norm, v_w_up, v_w_down, v_final_norm):
    w = dict(mix_norm=mix_norm, w_in=w_in, conv_w=conv_w, conv_norm=conv_norm, w_af=w_af, b_af=b_af, w_ab=w_ab,
             b_ab=b_ab, gla_norm=gla_norm, w_out=w_out, xa_norm=xa_norm, mem_norm=mem_norm, w_xq=w_xq, w_xkv=w_xkv,
             w_xo=w_xo, mlp_norm=mlp_norm, w_up=w_up, w_down=w_down, final_norm=final_norm)
    mom = dict(mix_norm=m_mix_norm, w_in=m_w_in, conv_w=m_conv_w, conv_norm=m_conv_norm, w_af=m_w_af, b_af=m_b_af,
               w_ab=m_w_ab, b_ab=m_b_ab, gla_norm=m_gla_norm, w_out=m_w_out, xa_norm=m_xa_norm, mem_norm=m_mem_norm,
               w_xq=m_w_xq, w_xkv=m_w_xkv, w_xo=m_w_xo, mlp_norm=m_mlp_norm, w_up=m_w_up, w_down=m_w_down,
               final_norm=m_final_norm)
    var = dict(mix_norm=v_mix_norm, w_in=v_w_in, conv_w=v_conv_w, conv_norm=v_conv_norm, w_af=v_w_af, b_af=v_b_af,
               w_ab=v_w_ab, b_ab=v_b_ab, gla_norm=v_gla_norm, w_out=v_w_out, xa_norm=v_xa_norm, mem_norm=v_mem_norm,
               w_xq=v_w_xq, w_xkv=v_w_xkv, w_xo=v_w_xo, mlp_norm=v_mlp_norm, w_up=v_w_up, w_down=v_w_down,
               final_norm=v_final_norm)
    xi, yi, ci = _place()
    two_d = lambda a: a.reshape(a.shape[-2:]) if a.ndim == 3 else a.reshape(1, a.shape[-1])

    small = jnp.concatenate([w[n].reshape(-1) for n in SMALL_SHARDED])
    small = jnp.pad(small, (0, SMALL_PACK_ROWS * 128 - small.shape[0])).reshape(SMALL_PACK_ROWS, 128)
    shard = {n: two_d(w[n]).astype(BF16) for n in MATS}
    for n in ("w_in", "w_up"):
        shard[n] = shard[n].T
    vec = {n: two_d(w[n]) for n in SMALL if n not in SMALL_SHARDED}
    place = jnp.stack([2 * xi + yi, ci]).astype(jnp.int32)
    loss8, grad_x, small_acc, own, from_chips = _step(x[0], mem[0], loss_target[0], shard, small, vec, place)

    tot = _small_reduce(small_acc, loss8)
    small_out = _small_adamw(tot, *[{n: two_d(d[n]) for n in SMALL} for d in (w, mom, var)])
    loss = tot[LOSS_ROW, 0]

    out_g, out_d, out_m, out_v = {}, {}, {}, {}
    own["w_up"], from_chips["w_up"] = own["w_up"].T, from_chips["w_up"].transpose(0, 2, 1)
    for n in MATS:
        if n == "w_in":
            res = _adamw(*[a.transpose(0, 2, 1) for a in (w[n], mom[n], var[n])], own[n], from_chips[n], "adamw_" + n)
            res = [a.transpose(0, 2, 1) for a in res]
        else:
            res = _adamw(w[n], mom[n], var[n], own[n], from_chips[n], "adamw_" + n)
        out_g[n], out_d[n], out_m[n], out_v[n] = res
    for n in SMALL:
        out_g[n], out_d[n], out_m[n], out_v[n] = [a.reshape(w[n].shape) for a in small_out[n]]

    return (loss, grad_x[None], *[out_g[n] for n in WEIGHTS], *[out_d[n] for n in WEIGHTS],
            *[out_m[n] for n in WEIGHTS], *[out_v[n] for n in WEIGHTS])
```

```python
import functools
import itertools

import jax
import jax.numpy as jnp
from jax import lax
from jax.experimental import pallas as pl
from jax.experimental.pallas import tpu as pltpu

F32 = jnp.float32
BF16 = jnp.bfloat16

D = 1024
CW = 512
GK = 256
GV = 512
NH = 4
CH = 64
LR = 16
NMEM = 256
XD = 256
FF = 4096
ZW = 3104
ZC = 3200
EPS = 1e-6
NDEV = 8

ZB_CB, ZB_CC, ZB_CU, ZB_V, ZB_G = 0, 1, 2, 4, 5
ZB_Q, ZB_K = 6, 7
ZB_LR = 24

TM = 512
TM_MLP = 256
TM_MLP_FWD = 512
TF = 512
TB = 512
TB_BWD = 512
TT = 2048
VMEM_LIMIT = 56 * 1024 * 1024

ADAM_LR, ADAM_B1, ADAM_B2, ADAM_EPS, ADAM_WD, ADAM_STEP = 0.001, 0.9, 0.999, 1e-08, 0.01, 10

XKV_SHARD = 2 * D // NDEV

MESH = pl.DeviceIdType.MESH


def _cparams(sem):
    return pltpu.CompilerParams(dimension_semantics=sem, vmem_limit_bytes=VMEM_LIMIT)


def _call(body, name, grid, in_specs, out_specs, out_shape, scratch, args, riders=()):
    n_in, n_out, n_scr = len(in_specs), len(out_specs), len(scratch)
    counts = [(len(r.arrays), len(r.out_shape), len(r.scratch)) for r in riders]

    def take(refs, pos, sizes):
        groups = []
        for size in sizes:
            groups.append(refs[pos:pos + size])
            pos += size
        return groups, pos

    def wrapped(*refs):
        ins, pos = refs[:n_in], n_in
        r_ins, pos = take(refs, pos, [c[0] for c in counts])
        outs, pos = refs[pos:pos + n_out], pos + n_out
        r_outs, pos = take(refs, pos, [c[1] for c in counts])
        scr, pos = refs[pos:pos + n_scr], pos + n_scr
        r_scr, pos = take(refs, pos, [c[2] for c in counts])
        ids = [pl.program_id(d) for d in range(len(grid))]
        first = functools.reduce(lambda a, b: a & b, [i == 0 for i in ids])
        last = functools.reduce(lambda a, b: a & b, [i == g - 1 for i, g in zip(ids, grid)])

        @pl.when(first)
        def _():
            for r, a, b, c in zip(riders, r_ins, r_outs, r_scr):
                r.start(a, b, c)

        body(*ins, *outs, *scr)

        @pl.when(last)
        def _():
            for r, a, b, c in zip(riders, r_ins, r_outs, r_scr):
                r.finish(a, b, c)

    hbm = pl.BlockSpec(memory_space=pltpu.HBM)
    r_args = [a for r in riders for a in r.arrays]
    r_shapes = [s for r in riders for s in r.out_shape]
    return pl.pallas_call(
        wrapped if riders else body, name=name, grid=grid, in_specs=list(in_specs) + [hbm] * len(r_args),
        out_specs=list(out_specs) + [hbm] * len(r_shapes), out_shape=list(out_shape) + r_shapes,
        scratch_shapes=list(scratch) + [s for r in riders for s in r.scratch],
        compiler_params=_cparams(("arbitrary",) * len(grid)))(*args, *r_args)


def _dot(a, b):
    return jnp.dot(a.astype(BF16), b.astype(BF16), preferred_element_type=F32)


def _dot_nt(a, b):
    return lax.dot_general(a.astype(BF16), b.astype(BF16), (((1,), (1,)), ((), ())), preferred_element_type=F32)


def _dot_tn(a, b):
    return lax.dot_general(a.astype(BF16), b.astype(BF16), (((0,), (0,)), ((), ())), preferred_element_type=F32)


def _split(x, n):
    parts = []
    for _ in range(n):
        p = x.astype(BF16)
        parts.append(p)
        x = x - p.astype(F32)
    return parts


def _dot_exact_lhs(m, x, n):
    return functools.reduce(lambda a, b: a + b, [jnp.dot(m, p, preferred_element_type=F32) for p in _split(x, n)])


def _dot_exact_rhs(x, m, n):
    return functools.reduce(lambda a, b: a + b, [jnp.dot(p, m, preferred_element_type=F32) for p in _split(x, n)])


def _rms(x, g):
    r = lax.rsqrt(jnp.mean(x * x, axis=-1, keepdims=True) + EPS)
    return x * r * g, r


def _rms_bwd(x, r, g, dy):
    xr = x * r
    u = dy * g
    dx = r * (u - xr * jnp.mean(u * xr, axis=-1, keepdims=True))
    return dx, jnp.sum(dy * xr, axis=0, keepdims=True)


def _iota(shape, dim):
    return lax.broadcasted_iota(jnp.int32, shape, dim)


def _sigmoid(x):
    return 1.0 / (1.0 + jnp.exp(-x))


def _acc_rows(ref, row):
    ref[...] += jnp.broadcast_to(row, ref.shape)


def _inproj(x, g, w_t, riders=()):
    t = x.shape[0]
    tm = min(TM, t)

    def body(x_ref, g_ref, w_ref, z_ref, h_ref):
        h, _ = _rms(x_ref[...], g_ref[...])
        hb = h.astype(BF16)
        h_ref[...] = hb
        z_ref[...] = _dot_nt(hb, w_ref[...])

    return _call(
        body, "inproj", (t // tm,),
        [pl.BlockSpec((tm, D), lambda i: (i, 0)), pl.BlockSpec((1, D), lambda i: (0, 0)),
         pl.BlockSpec((ZC, D), lambda i: (0, 0))],
        [pl.BlockSpec((tm, ZC), lambda i: (i, 0)), pl.BlockSpec((tm, D), lambda i: (i, 0))],
        [jax.ShapeDtypeStruct((t, ZC), F32), jax.ShapeDtypeStruct((t, D), BF16)], [], (x, g, w_t), riders)


def _kv_proj(mem, g, w):
    def body(m_ref, g_ref, w_ref, kv_ref, mn_ref):
        mn, _ = _rms(m_ref[...], g_ref[...])
        mb = mn.astype(BF16)
        mn_ref[...] = mb
        for j in range(NDEV):
            kv_ref[:, j * XKV_SHARD:(j + 1) * XKV_SHARD] = jnp.dot(mb, w_ref[j], preferred_element_type=F32)

    return pl.pallas_call(
        body, name="kv_proj",
        out_shape=[jax.ShapeDtypeStruct((NMEM, 2 * D), F32), jax.ShapeDtypeStruct((NMEM, D), BF16)],
        compiler_params=pltpu.CompilerParams(vmem_limit_bytes=VMEM_LIMIT))(mem, g, w)


def _softmax_head(qb, kb):
    s = _dot_nt(qb, kb) * (1.0 / 16.0)
    e = jnp.exp(s - jnp.max(s, axis=-1, keepdims=True))
    return e / jnp.sum(e, axis=-1, keepdims=True)


def _attn_fwd(x, z, o_f, o_b, conv_w, conv_norm, gla_norm4, w_out, g, w_xq, kb, vb, w_xo):
    t = x.shape[0]
    tm = min(TM, t)
    nblk = t // tm
    jmap = lambda i: i

    def body(x_ref, zq_ref, zk_ref, zv_ref, zg_ref, cb_ref, cc_ref, cu_ref, ccp_ref, ccn_ref, cup_ref, cun_ref,
             of_ref, ob_ref, cw_ref, cn_ref, gn_ref, wo_ref, g_ref, wq_ref, k_ref, v_ref, wx_ref,
             x1_ref, x2_ref, xn_ref, q_ref, a_ref, y_ref, opre_ref):
        j = pl.program_id(0)
        zv = zv_ref[...]
        sb = _head_sum((zq_ref[...] * 0.125) * zk_ref[...], 64, 128)
        o_pre = of_ref[...] + ob_ref[...] - sb * zv
        opre_ref[...] = o_pre
        on, _ = _head_norm(o_pre)
        zg = zg_ref[...]
        y_ref[:, CW:] = (on * gn_ref[...] * (zg * _sigmoid(zg))).astype(BF16)
        cb = cb_ref[...]
        _, _, _, conv = _conv_parts(cb, cc_ref[...], cu_ref[...], ccp_ref[pl.ds(7, 1), :], cup_ref[pl.ds(7, 1), :],
                                    ccn_ref[pl.ds(0, 1), :], cun_ref[pl.ds(0, 1), :], cw_ref, j == 0,
                                    j == nblk - 1, tm)
        yc = cb * conv
        gm = _group_sum(yc * yc) * (1.0 / 64.0)
        y_ref[:, :CW] = (yc * lax.rsqrt(gm + EPS) * cn_ref[...]).astype(BF16)

        x1 = x_ref[...] + jnp.dot(y_ref[...], wo_ref[...], preferred_element_type=F32)
        x1_ref[...] = x1
        xn, _ = _rms(x1, g_ref[...])
        xb = xn.astype(BF16)
        xn_ref[...] = xb
        qb = jnp.dot(xb, wq_ref[...], preferred_element_type=F32).astype(BF16)
        q_ref[...] = qb
        for h in range(NH):
            hs = slice(h * XD, (h + 1) * XD)
            p = _softmax_head(qb[:, hs], k_ref[:, hs])
            a_ref[:, hs] = _dot(p, v_ref[:, hs]).astype(BF16)
        x2_ref[...] = x1 + jnp.dot(a_ref[...], wx_ref[...], preferred_element_type=F32)

    tok = lambda i: (i, 0)
    full = lambda i: (0, 0)
    once = pl.Buffered(1)
    tokd, tokv = pl.BlockSpec((tm, D), tok), pl.BlockSpec((tm, GV), tok)
    weight = pl.BlockSpec((D, D), full, pipeline_mode=once)
    ccp, ccn = _halo_specs(tm, nblk, t, ZB_CC, jmap)
    cup, cun = _halo_specs(tm, nblk, t, ZB_CU, jmap)
    in_specs = [tokd, _zspec(tm, GK, ZB_Q, jmap), _zspec(tm, GK, ZB_K, jmap), _zspec(tm, GV, ZB_V, jmap),
                _zspec(tm, GV, ZB_G, jmap), _zspec(tm, CW, ZB_CB, jmap), _zspec(tm, CW, ZB_CC, jmap),
                _zspec(tm, CW, ZB_CU, jmap), ccp, ccn, cup, cun, tokv, tokv,
                pl.BlockSpec((3, CW), full), pl.BlockSpec((1, CW), full), pl.BlockSpec((1, GV), full),
                weight, pl.BlockSpec((1, D), full), weight, pl.BlockSpec((NMEM, D), full),
                pl.BlockSpec((NMEM, D), full), weight]
    return pl.pallas_call(
        body, name="attn_fwd", grid=(nblk,), in_specs=in_specs, out_specs=[tokd] * 6 + [tokv],
        out_shape=[jax.ShapeDtypeStruct((t, D), F32), jax.ShapeDtypeStruct((t, D), F32),
                   jax.ShapeDtypeStruct((t, D), BF16), jax.ShapeDtypeStruct((t, D), BF16),
                   jax.ShapeDtypeStruct((t, D), BF16), jax.ShapeDtypeStruct((t, D), BF16),
                   jax.ShapeDtypeStruct((t, GV), F32)],
        compiler_params=_cparams(("arbitrary",)))(
            x, z, z, z, z, z, z, z, z, z, z, z, o_f, o_b, conv_w, conv_norm, gla_norm4, w_out, g, w_xq, kb, vb, w_xo)


def _mlp_fwd(x2, g, w_up_t, w_down, fg, target):
    t = x2.shape[0]
    tm = min(TM_MLP_FWD, t)

    def body(x_ref, g_ref, wu_ref, wd_ref, fg_ref, t_ref, h1_ref, xn_ref, dx_ref, dxb_ref, loss_ref, dfg_ref, ab):
        @pl.when(pl.program_id(0) == 0)
        def _():
            loss_ref[...] = jnp.zeros_like(loss_ref)
            dfg_ref[...] = jnp.zeros_like(dfg_ref)

        x = x_ref[...]
        xn, _ = _rms(x, g_ref[...])
        xnb = xn.astype(BF16)
        xn_ref[...] = xnb
        for q in range(FF // TF):
            cols = slice(q * TF, (q + 1) * TF)
            h1 = _dot_nt(xnb, wu_ref[cols, :])
            h1_ref[:, cols] = h1.astype(BF16)
            hr = jnp.maximum(h1, 0.0)
            ab[:, cols] = (hr * hr).astype(BF16)
        x3 = x + jnp.dot(ab[...], wd_ref[...], preferred_element_type=F32)
        y, r = _rms(x3, fg_ref[...])
        e = y - t_ref[...]
        row = jnp.mean(e * e, axis=-1, keepdims=True)
        _acc_rows(loss_ref, 0.5 * jnp.sum(row, axis=0, keepdims=True))
        dx, dfg = _rms_bwd(x3, r, fg_ref[...], e * (1.0 / D))
        dx_ref[...] = dx
        dxb_ref[...] = dx.astype(BF16)
        _acc_rows(dfg_ref, dfg)

    tok = lambda i: (i, 0)
    full = lambda i: (0, 0)
    once = pl.Buffered(1)
    return pl.pallas_call(
        body, name="mlp_fwd", grid=(t // tm,),
        in_specs=[pl.BlockSpec((tm, D), tok), pl.BlockSpec((1, D), full),
                  pl.BlockSpec((FF, D), full, pipeline_mode=once), pl.BlockSpec((FF, D), full, pipeline_mode=once),
                  pl.BlockSpec((1, D), full), pl.BlockSpec((tm, D), tok)],
        out_specs=[pl.BlockSpec((tm, FF), tok), pl.BlockSpec((tm, D), tok), pl.BlockSpec((tm, D), tok),
                   pl.BlockSpec((tm, D), tok), pl.BlockSpec((8, 128), full), pl.BlockSpec((8, D), full)],
        out_shape=[jax.ShapeDtypeStruct((t, FF), BF16), jax.ShapeDtypeStruct((t, D), BF16),
                   jax.ShapeDtypeStruct((t, D), F32), jax.ShapeDtypeStruct((t, D), BF16),
                   jax.ShapeDtypeStruct((8, 128), F32), jax.ShapeDtypeStruct((8, D), F32)],
        scratch_shapes=[pltpu.VMEM((tm, FF), BF16)],
        compiler_params=_cparams(("arbitrary",)))(x2, g, w_up_t, w_down, fg, target)


def _mlp_bwd(dx3, dx3b, h1b, w_down, w_up_t, x2, g):
    t = x2.shape[0]
    tm = min(TM_MLP, t)

    def body(dx_ref, dxb_ref, h1_ref, wd_ref, wu_ref, x_ref, g_ref, a_ref, dh_ref, dx2_ref, dx2b_ref, dg_ref):
        @pl.when(pl.program_id(0) == 0)
        def _():
            dg_ref[...] = jnp.zeros_like(dg_ref)

        for q in range(FF // TF):
            cols = slice(q * TF, (q + 1) * TF)
            hr = jnp.maximum(h1_ref[:, cols].astype(F32), 0.0)
            da = _dot_nt(dxb_ref[...], wd_ref[cols, :])
            a_ref[:, cols] = (hr * hr).astype(BF16)
            dh_ref[:, cols] = (da * 2.0 * hr).astype(BF16)
        dxn = jnp.dot(dh_ref[...], wu_ref[...], preferred_element_type=F32)
        x = x_ref[...]
        r = lax.rsqrt(jnp.mean(x * x, axis=-1, keepdims=True) + EPS)
        dx, dg = _rms_bwd(x, r, g_ref[...], dxn)
        dx2 = dx_ref[...] + dx
        dx2_ref[...] = dx2
        dx2b_ref[...] = dx2.astype(BF16)
        _acc_rows(dg_ref, dg)

    tok = lambda i: (i, 0)
    full = lambda i: (0, 0)
    once = pl.Buffered(1)
    return pl.pallas_call(
        body, name="mlp_bwd", grid=(t // tm,),
        in_specs=[pl.BlockSpec((tm, D), tok), pl.BlockSpec((tm, D), tok), pl.BlockSpec((tm, FF), tok),
                  pl.BlockSpec((FF, D), full, pipeline_mode=once), pl.BlockSpec((FF, D), full, pipeline_mode=once),
                  pl.BlockSpec((tm, D), tok), pl.BlockSpec((1, D), full)],
        out_specs=[pl.BlockSpec((tm, FF), tok), pl.BlockSpec((tm, FF), tok), pl.BlockSpec((tm, D), tok),
                   pl.BlockSpec((tm, D), tok), pl.BlockSpec((8, D), full)],
        out_shape=[jax.ShapeDtypeStruct((t, FF), BF16), jax.ShapeDtypeStruct((t, FF), BF16),
                   jax.ShapeDtypeStruct((t, D), F32), jax.ShapeDtypeStruct((t, D), BF16),
                   jax.ShapeDtypeStruct((8, D), F32)],
        compiler_params=_cparams(("arbitrary",)))(dx3, dx3b, h1b, w_down, w_up_t, x2, g)


def _attn_bwd(x1, dx2, dx2b, qb, kb, vb, w_xo, w_xq, w_out, g):
    t = x1.shape[0]
    tm = min(TM, t)

    def body(x_ref, dx2_ref, dx2b_ref, q_ref, k_ref, v_ref, wx_ref, wq_ref, wo_ref, g_ref,
             dx1_ref, dx1b_ref, dy_ref, dq_ref, dkv_ref, dg_ref):
        @pl.when(pl.program_id(0) == 0)
        def _():
            dkv_ref[...] = jnp.zeros_like(dkv_ref)
            dg_ref[...] = jnp.zeros_like(dg_ref)

        datt = _dot_nt(dx2b_ref[...], wx_ref[...]).astype(BF16)
        for h in range(NH):
            hs = slice(h * XD, (h + 1) * XD)
            q_h, k_h, v_h, da_h = q_ref[:, hs], k_ref[:, hs], v_ref[:, hs], datt[:, hs]
            p = _softmax_head(q_h, k_h)
            dp = _dot_nt(da_h, v_h)
            ds = (p * (dp - jnp.sum(dp * p, axis=-1, keepdims=True)) * (1.0 / 16.0)).astype(BF16)
            dq_ref[:, hs] = _dot(ds, k_h).astype(BF16)
            dkv_ref[:, hs] += _dot_tn(ds, q_h)
            dkv_ref[:, D + h * XD:D + (h + 1) * XD] += _dot_tn(p, da_h)
        dxn = _dot_nt(dq_ref[...], wq_ref[...])
        x = x_ref[...]
        r = lax.rsqrt(jnp.mean(x * x, axis=-1, keepdims=True) + EPS)
        dx, dg = _rms_bwd(x, r, g_ref[...], dxn)
        dx1 = dx2_ref[...] + dx
        dx1_ref[...] = dx1
        dx1b = dx1.astype(BF16)
        dx1b_ref[...] = dx1b
        dy_ref[...] = _dot_nt(dx1b, wo_ref[...])
        _acc_rows(dg_ref, dg)

    tok = lambda i: (i, 0)
    full = lambda i: (0, 0)
    return pl.pallas_call(
        body, name="attn_bwd", grid=(t // tm,),
        in_specs=[pl.BlockSpec((tm, D), tok), pl.BlockSpec((tm, D), tok), pl.BlockSpec((tm, D), tok),
                  pl.BlockSpec((tm, D), tok), pl.BlockSpec((NMEM, D), full), pl.BlockSpec((NMEM, D), full),
                  pl.BlockSpec((D, D), full), pl.BlockSpec((D, D), full), pl.BlockSpec((D, D), full),
                  pl.BlockSpec((1, D), full)],
        out_specs=[pl.BlockSpec((tm, D), tok), pl.BlockSpec((tm, D), tok), pl.BlockSpec((tm, D), tok),
                   pl.BlockSpec((tm, D), tok), pl.BlockSpec((NMEM, 2 * D), full), pl.BlockSpec((8, D), full)],
        out_shape=[jax.ShapeDtypeStruct((t, D), F32), jax.ShapeDtypeStruct((t, D), BF16),
                   jax.ShapeDtypeStruct((t, D), F32), jax.ShapeDtypeStruct((t, D), BF16),
                   jax.ShapeDtypeStruct((NMEM, 2 * D), F32), jax.ShapeDtypeStruct((8, D), F32)],
        compiler_params=_cparams(("arbitrary",)))(x1, dx2, dx2b, qb, kb, vb, w_xo, w_xq, w_out, g)


def _kv_bwd(dkv, memn, mem, g, w):
    def body(dkv_ref, mn_ref, m_ref, g_ref, w_ref, dw_ref, dg_ref):
        dkvb = dkv_ref[...].astype(BF16)
        dmn = jnp.zeros((NMEM, D), F32)
        for j in range(NDEV):
            cols = slice(j * XKV_SHARD, (j + 1) * XKV_SHARD)
            dw_ref[j] = _dot_tn(mn_ref[...], dkvb[:, cols])
            dmn += _dot_nt(dkvb[:, cols], w_ref[j])
        m = m_ref[...]
        r = lax.rsqrt(jnp.mean(m * m, axis=-1, keepdims=True) + EPS)
        dg_ref[...] = jnp.broadcast_to(jnp.sum(dmn * m * r, axis=0, keepdims=True), dg_ref.shape)

    return pl.pallas_call(
        body, name="kv_bwd",
        out_shape=[jax.ShapeDtypeStruct((NDEV, D, XKV_SHARD), F32), jax.ShapeDtypeStruct((8, D), F32)],
        compiler_params=pltpu.CompilerParams(vmem_limit_bytes=VMEM_LIMIT))(dkv, memn, mem, g, w)


def _inproj_bwd(dz, w_t, x, dx1, g, riders=()):
    t = x.shape[0]
    tm = min(TM, t)

    def body(dz_ref, w_ref, x_ref, dx1_ref, g_ref, gx_ref, dg_ref):
        @pl.when(pl.program_id(0) == 0)
        def _():
            dg_ref[...] = jnp.zeros_like(dg_ref)

        dh = jnp.dot(dz_ref[...], w_ref[...], preferred_element_type=F32)
        x = x_ref[...]
        r = lax.rsqrt(jnp.mean(x * x, axis=-1, keepdims=True) + EPS)
        dx, dg = _rms_bwd(x, r, g_ref[...], dh)
        gx_ref[...] = dx1_ref[...] + dx
        _acc_rows(dg_ref, dg)

    tok = lambda i: (i, 0)
    full = lambda i: (0, 0)
    return _call(
        body, "inproj_bwd", (t // tm,),
        [pl.BlockSpec((tm, ZC), tok), pl.BlockSpec((ZC, D), full), pl.BlockSpec((tm, D), tok),
         pl.BlockSpec((tm, D), tok), pl.BlockSpec((1, D), full)],
        [pl.BlockSpec((tm, D), tok), pl.BlockSpec((8, D), full)],
        [jax.ShapeDtypeStruct((t, D), F32), jax.ShapeDtypeStruct((8, D), F32)], [], (dz, w_t, x, dx1, g), riders)


def _matmul_tn(a, b, name, rows=None, riders=()):
    t, k = a.shape
    n = b.shape[1]
    tk, tn = [1024 if size % 1024 == 0 else 640 for size in (k, n)]
    tt = min(TT, t)
    rows = rows or k

    def body(a_ref, b_ref, o_ref):
        @pl.when(pl.program_id(2) == 0)
        def _():
            o_ref[...] = jnp.zeros_like(o_ref)

        o_ref[...] += _dot_tn(a_ref[...], b_ref[...])

    return _call(
        body, name, (k // tk, n // tn, t // tt),
        [pl.BlockSpec((tt, tk), lambda i, j, s: (s, i)), pl.BlockSpec((tt, tn), lambda i, j, s: (s, j))],
        [pl.BlockSpec((tk, tn), lambda i, j, s: (i, j))], [jax.ShapeDtypeStruct((rows, n), F32)], [], (a, b), riders)


def _lane_head(shape, dim, shift):
    return _iota(shape, dim) >> shift


CUM_ROWS = 128


def _chunk_cumsum(x, upper, n):
    r, c = _iota((CUM_ROWS, CUM_ROWS), 0), _iota((CUM_ROWS, CUM_ROWS), 1)
    tri = (c >= r) if upper else (c <= r)
    cum = jnp.where(((r >> 6) == (c >> 6)) & tri, 1.0, 0.0).astype(BF16)
    return jnp.concatenate([_dot_exact_lhs(cum, x[g:g + CUM_ROWS], n) for g in range(0, x.shape[0], CUM_ROWS)],
                           axis=0)


def _gla_recompute(q_raw, k, lr, wpad, bias, rev, tb):
    pre = _dot(lr, wpad) + bias
    la = (jnp.minimum(pre, 0.0) - jnp.log(1.0 + jnp.exp(-jnp.abs(pre)))) * (1.0 / 16.0)
    b = _chunk_cumsum(la, rev, 3)
    e, ei = jnp.exp(b), jnp.exp(-b)
    qt = (q_raw * 0.125) * e
    kt = k * ei
    return pre, b, e, ei, qt, kt


def _stack_heads(x, shift):
    head = _lane_head(x.shape, 1, shift)
    return jnp.concatenate([jnp.where(head == h, x, 0.0) for h in range(NH)], axis=0).astype(BF16)


def _fold_heads(x, shift):
    head = _lane_head((CH, x.shape[1]), 1, shift)
    return functools.reduce(lambda a, b: a + b,
                            [jnp.where(head == h, x[h * CH:(h + 1) * CH], 0.0) for h in range(NH)])


def _wide_mask(rev):
    r, s = _iota((CH, NH * CH), 0), _iota((CH, NH * CH), 1) & (CH - 1)
    return (s >= r) if rev else (s <= r)


def _state_mask():
    return (_iota((GV, GK), 0) >> 7) == (_iota((GV, GK), 1) >> 6)


def _state_expand(sd):
    head = _lane_head(sd.shape, 1, 6)
    return jnp.concatenate([jnp.where(head == h, sd, 0.0) for h in range(NH)], axis=0)


def _conv_parts(cb, cc, cu, ccp, cup, ccn, cun, cw_ref, first, last, tb):
    h = cc * cu
    hp = jnp.where(first, 0.0, ccp * cup)
    hn = jnp.where(last, 0.0, ccn * cun)
    rows = _iota(h.shape, 0)
    h_m1 = jnp.where(rows == 0, hp, pltpu.roll(h, 1, 0))
    h_p1 = jnp.where(rows == tb - 1, hn, pltpu.roll(h, tb - 1, 0))
    conv = cw_ref[pl.ds(0, 1), :] * h_m1 + cw_ref[pl.ds(1, 1), :] * h + cw_ref[pl.ds(2, 1), :] * h_p1
    return h, h_m1, h_p1, conv


def _head_sum(x, w_in, w_out):
    shape, sh_in, sh_out = (2 * w_in, 2 * w_out), w_in.bit_length() - 1, w_out.bit_length() - 1
    sel = jnp.where((_iota(shape, 0) >> sh_in) == (_iota(shape, 1) >> sh_out), 1.0, 0.0).astype(BF16)
    return jnp.concatenate([_dot_exact_rhs(x[:, s:s + 2 * w_in], sel, 2) for s in range(0, NH * w_in, 2 * w_in)],
                           axis=1)


def _group_sum(x):
    ones = jnp.where((_iota((128, 128), 0) >> 6) == (_iota((128, 128), 1) >> 6), 1.0, 0.0).astype(BF16)
    return jnp.concatenate([_dot_exact_rhs(x[:, s:s + 128], ones, 2) for s in range(0, x.shape[1], 128)], axis=1)


def _head_norm(o):
    ons, rs = [], []
    for h in range(NH):
        slab = o[:, h * 128:(h + 1) * 128]
        r = lax.rsqrt(jnp.mean(slab * slab, axis=-1, keepdims=True) + EPS)
        ons.append(slab * r)
        rs.append(jnp.broadcast_to(r, slab.shape))
    return jnp.concatenate(ons, axis=1), jnp.concatenate(rs, axis=1)


def _zspec(tb, width, blk, jmap):
    return pl.BlockSpec((tb, width), lambda i: (jmap(i), blk))


def _halo_specs(tb, nblk, t, blk, jmap):
    prev = pl.BlockSpec((8, CW), lambda i: (jnp.maximum(jmap(i) * (tb // 8) - 1, 0), blk))
    nxt = pl.BlockSpec((8, CW), lambda i: (jnp.minimum((jmap(i) + 1) * (tb // 8), t // 8 - 1), blk))
    return prev, nxt


def _gla_fwd_block(q_ref, k_ref, v_ref, lr_ref, w_ref, bias_ref, o_ref, sd_ref, st, b_scr, rev, tb):
    nb = tb // CH
    _, b, _, _, qt, kt = _gla_recompute(q_ref[...], k_ref[...], lr_ref[...], w_ref[...], bias_ref[...], rev, tb)
    v = v_ref[...]
    b_scr[...] = b
    yield
    maskw, bd = _wide_mask(rev), _state_mask()
    order = list(reversed(range(nb))) if rev else list(range(nb))
    rows = [slice(c * CH, (c + 1) * CH) for c in range(nb)]
    state = st[...]
    for c in order:
        gdec = jnp.exp(b_scr[pl.ds(c * CH + (0 if rev else CH - 1), 1), :])
        sd_ref[c] = state[0:128] + state[128:256] + state[256:384] + state[384:512]
        a = jnp.where(maskw, _dot_nt(qt[rows[c]], _stack_heads(kt[rows[c]], 6)), 0.0)
        o_ref[pl.ds(c * CH, CH), :] = _dot(a, _stack_heads(v[rows[c]], 7)) + _dot_nt(qt[rows[c]], state)
        state = state * gdec + jnp.where(bd, _dot_tn(v[rows[c]], kt[rows[c]] * gdec), 0.0)
        yield
    st[...] = state
    yield


def _gla_fwd(z, waf_pad, b_af, wab_pad, b_ab, riders=()):
    t = z.shape[0]
    tb = min(TB, t)
    nblk, nb = t // tb, tb // CH
    jmaps = (lambda i: i, lambda i: nblk - 1 - i)

    def body(qf, kf, vf, lrf, qr, kr, vr, lrr, wf, bf, wr, br, of_ref, sdf_ref, or_ref, sdr_ref,
             st_f, st_r, b_f, b_r):
        @pl.when(pl.program_id(0) == 0)
        def _():
            st_f[...] = jnp.zeros_like(st_f)
            st_r[...] = jnp.zeros_like(st_r)

        for _ in zip(_gla_fwd_block(qf, kf, vf, lrf, wf, bf, of_ref, sdf_ref, st_f, b_f, False, tb),
                     _gla_fwd_block(qr, kr, vr, lrr, wr, br, or_ref, sdr_ref, st_r, b_r, True, tb)):
            pass

    full = lambda i: (0, 0)
    zspecs = [s for jm in jmaps for s in (_zspec(tb, GK, ZB_Q, jm), _zspec(tb, GK, ZB_K, jm),
                                         _zspec(tb, GV, ZB_V, jm), _zspec(tb, 128, ZB_LR, jm))]
    wspecs = [pl.BlockSpec((128, GK), full), pl.BlockSpec((1, GK), full)] * 2
    out_specs = [s for jm in jmaps for s in (pl.BlockSpec((tb, GV), lambda i, jm=jm: (jm(i), 0)),
                                             pl.BlockSpec((nb, 128, GK), lambda i, jm=jm: (jm(i), 0, 0)))]
    out_shape = [jax.ShapeDtypeStruct((t, GV), F32), jax.ShapeDtypeStruct((t // CH, 128, GK), F32)] * 2
    scratch = [pltpu.VMEM((GV, GK), F32), pltpu.VMEM((GV, GK), F32), pltpu.VMEM((tb, GK), F32),
               pltpu.VMEM((tb, GK), F32)]
    return _call(body, "gla_fwd", (nblk,), zspecs + wspecs, out_specs, out_shape, scratch,
                 [z] * 8 + [waf_pad, b_af, wab_pad, b_ab], riders)


def _gla_bwd_chunks(do_ref, sd_ref, dst, b_scr, db_scr, dq_ref, dk_ref, dv_ref, qt, kt, e, ei, v, rev, nb):
    maskw, bd = _wide_mask(rev), _state_mask()
    for c in (range(nb) if rev else reversed(range(nb))):
        sl = slice(c * CH, (c + 1) * CH)
        grow = c * CH + (0 if rev else CH - 1)
        gdec = jnp.exp(b_scr[pl.ds(grow, 1), :])
        qt_c, kt_c, v_c, do_c = qt[sl], kt[sl], v[sl], do_ref[pl.ds(c * CH, CH), :]
        s_in = _state_expand(sd_ref[c])
        ds_out = dst[...]
        kbd, vbd = _stack_heads(kt_c, 6), _stack_heads(v_c, 7)
        a = jnp.where(maskw, _dot_nt(qt_c, kbd), 0.0)
        da = jnp.where(maskw, _dot_nt(do_c, vbd), 0.0)
        dv_ref[pl.ds(c * CH, CH), :] = _fold_heads(_dot_tn(a, do_c), 7) + _dot_nt(kt_c * gdec, ds_out)
        da_do = jnp.concatenate([da.astype(BF16), do_c.astype(BF16)], axis=1)
        dqt = _dot(da_do, jnp.concatenate([kbd, s_in.astype(BF16)], axis=0))
        dkh = _dot(v_c, ds_out)
        both = _dot_tn(da_do, qt_c)
        dkt = _fold_heads(both[:NH * CH], 6) + dkh * gdec
        dg = jnp.sum(ds_out * s_in, axis=0, keepdims=True) + jnp.sum(kt_c * dkh, axis=0, keepdims=True)
        db_scr[pl.ds(c * CH, CH), :] = dqt * qt_c - dkt * kt_c
        db_scr[pl.ds(grow, 1), :] += dg * gdec
        dq_ref[pl.ds(c * CH, CH), :] = dqt * e[sl] * 0.125
        dk_ref[pl.ds(c * CH, CH), :] = dkt * ei[sl]
        dst[...] = ds_out * gdec + jnp.where(bd, both[NH * CH:], 0.0)
        yield


def _gate_bwd(db, pre, lr, wpad, rev, tb):
    dla = _chunk_cumsum(db, not rev, 2)
    dpre = dla * (1.0 / 16.0) / (1.0 + jnp.exp(pre))
    return dpre, _dot_nt(dpre, wpad), _dot_tn(lr, dpre)


def _gla_bwd_first(z, dy, o_pre, sd, wpad, bias, conv_w, conv_norm, gla_norm4, riders=()):
    t = z.shape[0]
    tb = min(TB_BWD, t)
    nblk, nb = t // tb, tb // CH
    jmap = lambda i: nblk - 1 - i

    def body(q_ref, k_ref, v_ref, lr_ref, g_ref, cb_ref, cc_ref, cu_ref, ccp_ref, ccn_ref, cup_ref, cun_ref,
             dy_ref, opre_ref, sd_ref, w_ref, bias_ref, cw_ref, cn_ref, gn_ref,
             do_ref, dq_ref, dk_ref, dv_ref, dlr_ref, dzg_ref, dzcb_ref, dconv_ref,
             dw_ref, dbias_ref, dcw_ref, dcn_ref, dgn_ref, dst, b_scr, db_scr):
        i = pl.program_id(0)
        j = jmap(i)

        @pl.when(i == 0)
        def _():
            dst[...] = jnp.zeros_like(dst)
            for ref in (dw_ref, dbias_ref, dcw_ref, dcn_ref, dgn_ref):
                ref[...] = jnp.zeros_like(ref)

        dyg = dy_ref[:, CW:]
        g = g_ref[...]
        sig = _sigmoid(g)
        on, rr = _head_norm(opre_ref[...])
        gn = gn_ref[...]
        dzg_ref[...] = (dyg * on * gn * (sig * (1.0 + g * (1.0 - sig)))).astype(BF16)
        don = dyg * (g * sig)
        _acc_rows(dgn_ref, jnp.sum(don * on, axis=0, keepdims=True))
        u = don * gn
        uo = u * on
        mean_uo = jnp.concatenate(
            [jnp.broadcast_to(jnp.mean(uo[:, h * 128:(h + 1) * 128], axis=-1, keepdims=True), (tb, 128))
             for h in range(NH)], axis=1)
        do_ref[...] = rr * (u - on * mean_uo)

        def conv_branch():
            cb = cb_ref[...]
            h, h_m1, h_p1, conv = _conv_parts(cb, cc_ref[...], cu_ref[...], ccp_ref[pl.ds(7, 1), :],
                                              cup_ref[pl.ds(7, 1), :], ccn_ref[pl.ds(0, 1), :],
                                              cun_ref[pl.ds(0, 1), :], cw_ref, j == 0, j == nblk - 1, tb)
            yc = cb * conv
            yield
            rc = lax.rsqrt(_group_sum(yc * yc) * (1.0 / 64.0) + EPS)
            ycr = yc * rc
            yield
            dyn = dy_ref[:, :CW]
            _acc_rows(dcn_ref, jnp.sum(dyn * ycr, axis=0, keepdims=True))
            uc = dyn * cn_ref[...]
            yield
            dyc = rc * (uc - ycr * (_group_sum(uc * ycr) * (1.0 / 64.0)))
            dzcb_ref[...] = (dyc * conv).astype(BF16)
            yield
            dconv = dyc * cb
            dconv_ref[...] = dconv
            yield
            dcw_ref[pl.ds(0, 1), :] += jnp.sum(dconv * h_m1, axis=0, keepdims=True)
            dcw_ref[pl.ds(1, 1), :] += jnp.sum(dconv * h, axis=0, keepdims=True)
            dcw_ref[pl.ds(2, 1), :] += jnp.sum(dconv * h_p1, axis=0, keepdims=True)
            yield

        lr, wp = lr_ref[...], w_ref[...]
        pre, b, e, ei, qt, kt = _gla_recompute(q_ref[...], k_ref[...], lr, wp, bias_ref[...], False, tb)
        b_scr[...] = b
        for _ in itertools.zip_longest(
                _gla_bwd_chunks(do_ref, sd_ref, dst, b_scr, db_scr, dq_ref, dk_ref, dv_ref, qt, kt, e, ei, v_ref[...],
                                False, nb), conv_branch()):
            pass
        dpre, dlr, dw = _gate_bwd(db_scr[...], pre, lr, wp, False, tb)
        dlr_ref[...] = dlr
        dw_ref[...] += dw
        _acc_rows(dbias_ref, jnp.sum(dpre, axis=0, keepdims=True))

    full = lambda i: (0, 0)
    tokv = pl.BlockSpec((tb, GV), lambda i: (jmap(i), 0))
    tokk = pl.BlockSpec((tb, GK), lambda i: (jmap(i), 0))
    ccp, ccn = _halo_specs(tb, nblk, t, ZB_CC, jmap)
    cup, cun = _halo_specs(tb, nblk, t, ZB_CU, jmap)
    in_specs = [_zspec(tb, GK, ZB_Q, jmap), _zspec(tb, GK, ZB_K, jmap), _zspec(tb, GV, ZB_V, jmap),
                _zspec(tb, 128, ZB_LR, jmap), _zspec(tb, GV, ZB_G, jmap), _zspec(tb, CW, ZB_CB, jmap),
                _zspec(tb, CW, ZB_CC, jmap), _zspec(tb, CW, ZB_CU, jmap), ccp, ccn, cup, cun,
                pl.BlockSpec((tb, D), lambda i: (jmap(i), 0)), tokv,
                pl.BlockSpec((nb, 128, GK), lambda i: (jmap(i), 0, 0)), pl.BlockSpec((128, GK), full),
                pl.BlockSpec((1, GK), full), pl.BlockSpec((3, CW), full), pl.BlockSpec((1, CW), full),
                pl.BlockSpec((1, GV), full)]
    out_specs = [tokv, tokk, tokk, tokv, pl.BlockSpec((tb, 128), lambda i: (jmap(i), 0)), tokv, tokv, tokv,
                 pl.BlockSpec((128, GK), full), pl.BlockSpec((8, GK), full), pl.BlockSpec((8, CW), full),
                 pl.BlockSpec((8, CW), full), pl.BlockSpec((8, GV), full)]
    out_shape = [jax.ShapeDtypeStruct((t, GV), F32), jax.ShapeDtypeStruct((t, GK), F32),
                 jax.ShapeDtypeStruct((t, GK), F32), jax.ShapeDtypeStruct((t, GV), F32),
                 jax.ShapeDtypeStruct((t, 128), F32), jax.ShapeDtypeStruct((t, GV), BF16),
                 jax.ShapeDtypeStruct((t, CW), BF16), jax.ShapeDtypeStruct((t, CW), F32),
                 jax.ShapeDtypeStruct((128, GK), F32), jax.ShapeDtypeStruct((8, GK), F32),
                 jax.ShapeDtypeStruct((8, CW), F32), jax.ShapeDtypeStruct((8, CW), F32),
                 jax.ShapeDtypeStruct((8, GV), F32)]
    return _call(
        body, "gla_bwd_first", (nblk,), in_specs, out_specs, out_shape,
        [pltpu.VMEM((GV, GK), F32), pltpu.VMEM((tb, GK), F32), pltpu.VMEM((tb, GK), F32)],
        (z, z, z, z, z, z, z, z, z, z, z, z, dy, o_pre, sd, wpad, bias, conv_w, conv_norm, gla_norm4), riders)


def _gla_bwd_second(z, do, sd, wpad, bias, dqa, dka, dva, dlra, dzg, dzcb, dconv, conv_w, riders=()):
    t = z.shape[0]
    tb = min(TB_BWD, t)
    nblk, nb = t // tb, tb // CH
    jmap = lambda i: i

    def body(q_ref, k_ref, v_ref, lr_ref, cc_ref, cu_ref, do_ref, sd_ref, w_ref, bias_ref, dqa_ref, dka_ref,
             dva_ref, dlra_ref, dzg_ref, dzcb_ref, dc_ref, dcp_ref, dcn_ref, cw_ref,
             dz_ref, dw_ref, dbias_ref, dst, b_scr, db_scr, dq_scr, dk_scr, dv_scr, sb_scr, dsk_scr):
        i = pl.program_id(0)

        @pl.when(i == 0)
        def _():
            dst[...] = jnp.zeros_like(dst)
            dw_ref[...] = jnp.zeros_like(dw_ref)
            dbias_ref[...] = jnp.zeros_like(dbias_ref)

        q_raw, k, v, lr, wp = q_ref[...], k_ref[...], v_ref[...], lr_ref[...], w_ref[...]
        pre, b, e, ei, qt, kt = _gla_recompute(q_raw, k, lr, wp, bias_ref[...], True, tb)
        b_scr[...] = b

        def token_local():
            dc = dc_ref[...]
            rows = _iota(dc.shape, 0)
            dprev = jnp.where(i == 0, 0.0, dcp_ref[pl.ds(7, 1), :])
            dnext = jnp.where(i == nblk - 1, 0.0, dcn_ref[pl.ds(0, 1), :])
            dc_m1 = jnp.where(rows == 0, dprev, pltpu.roll(dc, 1, 0))
            dc_p1 = jnp.where(rows == tb - 1, dnext, pltpu.roll(dc, tb - 1, 0))
            yield
            dh = cw_ref[pl.ds(0, 1), :] * dc_p1 + cw_ref[pl.ds(1, 1), :] * dc + cw_ref[pl.ds(2, 1), :] * dc_m1
            dz_ref[:, 0:512] = dzcb_ref[...]
            yield
            dz_ref[:, 512:1024] = (dh * cu_ref[...]).astype(BF16)
            dz_ref[:, 1024:1536] = (dh * cc_ref[...]).astype(BF16)
            dz_ref[:, 2560:3072] = dzg_ref[...]
            yield
            sb_scr[...] = _head_sum((q_raw * 0.125) * k, 64, 128)
            yield
            dsk_scr[...] = _head_sum(do_ref[...] * v, 128, 64)
            yield

        for _ in itertools.zip_longest(
                _gla_bwd_chunks(do_ref, sd_ref, dst, b_scr, db_scr, dq_scr, dk_scr, dv_scr, qt, kt, e, ei, v, True, nb),
                token_local()):
            pass
        dpre, dlr, dw = _gate_bwd(db_scr[...], pre, lr, wp, True, tb)
        dw_ref[...] += dw
        _acc_rows(dbias_ref, jnp.sum(dpre, axis=0, keepdims=True))
        dsk = dsk_scr[...]
        dz_ref[:, 1536:1792] = (dqa_ref[...] + dq_scr[...] - dsk * k * 0.125).astype(BF16)
        dz_ref[:, 1792:2048] = (dka_ref[...] + dk_scr[...] - dsk * (q_raw * 0.125)).astype(BF16)
        dz_ref[:, 2048:2560] = (dva_ref[...] + dv_scr[...] - sb_scr[...] * do_ref[...]).astype(BF16)
        dz_ref[:, 3072:3200] = (dlra_ref[...] + dlr).astype(BF16)

    full = lambda i: (0, 0)
    tokv = pl.BlockSpec((tb, GV), lambda i: (i, 0))
    tokk = pl.BlockSpec((tb, GK), lambda i: (i, 0))
    dcp = pl.BlockSpec((8, CW), lambda i: (jnp.maximum(i * (tb // 8) - 1, 0), 0))
    dcn = pl.BlockSpec((8, CW), lambda i: (jnp.minimum((i + 1) * (tb // 8), t // 8 - 1), 0))
    in_specs = [_zspec(tb, GK, ZB_Q, jmap), _zspec(tb, GK, ZB_K, jmap), _zspec(tb, GV, ZB_V, jmap),
                _zspec(tb, 128, ZB_LR, jmap), _zspec(tb, CW, ZB_CC, jmap), _zspec(tb, CW, ZB_CU, jmap), tokv,
                pl.BlockSpec((nb, 128, GK), lambda i: (i, 0, 0)), pl.BlockSpec((128, GK), full),
                pl.BlockSpec((1, GK), full), tokk, tokk, tokv, pl.BlockSpec((tb, 128), lambda i: (i, 0)), tokv, tokv,
                tokv, dcp, dcn, pl.BlockSpec((3, CW), full)]
    out_specs = [pl.BlockSpec((tb, ZC), lambda i: (i, 0)), pl.BlockSpec((128, GK), full), pl.BlockSpec((8, GK), full)]
    out_shape = [jax.ShapeDtypeStruct((t, ZC), BF16), jax.ShapeDtypeStruct((128, GK), F32),
                 jax.ShapeDtypeStruct((8, GK), F32)]
    return _call(
        body, "gla_bwd_second", (nblk,), in_specs, out_specs, out_shape,
        [pltpu.VMEM((GV, GK), F32), pltpu.VMEM((tb, GK), F32), pltpu.VMEM((tb, GK), F32),
         pltpu.VMEM((tb, GK), F32), pltpu.VMEM((tb, GK), F32), pltpu.VMEM((tb, GV), F32),
         pltpu.VMEM((tb, GV), F32), pltpu.VMEM((tb, GK), F32)],
        (z, z, z, z, z, z, do, sd, wpad, bias, dqa, dka, dva, dlra, dzg, dzcb, dconv, dconv, dconv, conv_w), riders)


def _step(x, mem, target, shard, small_pack, vec, place):
    own, from_chips = {}, {}

    def pair_sums(names, g4, from_sibling):
        pbs = []
        for n, g, s in zip(names, g4, from_sibling):
            pb, own[n] = _rs_pair_sum(place, g, s, "pair_sum_" + n)
            pbs.append(pb)
        return pbs

    def by_dest(g, n):
        return g.reshape((4, 2) + shard[n].shape)

    w_in, small_all = _exchange(_gather_rider([shard["w_in"], small_pack]), "gather_w_in")
    w_in = jnp.pad(w_in.reshape(ZW, D), ((0, ZC - ZW), (0, 0)))
    small_all = small_all.reshape(NDEV, -1)
    p, off = {}, 0
    for n, (r, c) in SMALL_SHARDED.items():
        p[n] = small_all[:, off:off + r * c].reshape(NDEV, r, c).transpose(1, 0, 2).reshape(r, NDEV * c)
        off += r * c
    zeros_lr = jnp.zeros((128 - LR, GK), BF16)
    waf_pad = jnp.concatenate([p["w_af"].astype(BF16), zeros_lr], axis=0)
    wab_pad = jnp.concatenate([jnp.zeros((LR, GK), BF16), p["w_ab"].astype(BF16), zeros_lr[:128 - 2 * LR]], axis=0)
    gla_norm4 = jnp.tile(vec["gla_norm"], (1, NH))

    z, hb, w_out, w_xq, w_xo, w_xkv = _inproj(
        x, vec["mix_norm"], w_in, [_gather_rider([shard[n] for n in ("w_out", "w_xq", "w_xo", "w_xkv")])])
    w_out, w_xq, w_xo = [a.reshape(D, D) for a in (w_out, w_xq, w_xo)]
    o_f, sd_f, o_b, sd_b, w_up_t, w_down = _gla_fwd(
        z, waf_pad, vec["b_af"], wab_pad, vec["b_ab"], [_gather_rider([shard["w_up"], shard["w_down"]])])
    w_up_t, w_down = w_up_t.reshape(FF, D), w_down.reshape(FF, D)
    kv, memn = _kv_proj(mem, vec["mem_norm"], w_xkv)
    kb, vb = kv[:, :D].astype(BF16), kv[:, D:].astype(BF16)
    x1, x2, xn1, qb, attb, yb, o_pre = _attn_fwd(x, z, o_f, o_b, p["conv_w"], vec["conv_norm"], gla_norm4, w_out,
                                                 vec["xa_norm"], w_xq, kb, vb, w_xo)
    h1b, xn2, dx3, dx3b, loss8, dfinal = _mlp_fwd(x2, vec["mlp_norm"], w_up_t, w_down, vec["final_norm"], target)

    ab, dh1b, dx2, dx2b, dmlp = _mlp_bwd(dx3, dx3b, h1b, w_down, w_up_t, x2, vec["mlp_norm"])
    g_mlp = [by_dest(_matmul_tn(ab, dx3b, "dw_down")[0], "w_down"),
             by_dest(_matmul_tn(dh1b, xn2, "dw_up")[0], "w_up")]
    dx1, dx1b, dy, dqb, dkv, dxa = _attn_bwd(x1, dx2, dx2b, qb, kb, vb, w_xo, w_xq, w_out, vec["xa_norm"])
    dw_xo, *s_mlp = _matmul_tn(attb, dx2b, "dw_xo", riders=[_sibling_rider(g_mlp)])
    pb_mlp = pair_sums(("w_down", "w_up"), g_mlp, s_mlp)
    dw_xkv, dmemn = _kv_bwd(dkv, memn, mem, vec["mem_norm"], w_xkv)
    att_names = ("w_xo", "w_xq", "w_out", "w_xkv")
    g_att = [by_dest(g, n) for g, n in zip(
        (dw_xo, _matmul_tn(xn1, dqb, "dw_xq")[0], _matmul_tn(yb, dx1b, "dw_out")[0], dw_xkv), att_names)]
    res = _gla_bwd_first(z, dy, o_pre, sd_f, waf_pad, vec["b_af"], p["conv_w"], vec["conv_norm"], gla_norm4,
                         riders=[_chips_rider(pb_mlp), _sibling_rider(g_att)])
    do, dqa, dka, dva, dlra, dzg, dzcb, dconv, dwaf, dbaf, dcw, dcn, dgn = res[:13]
    from_chips["w_down"], from_chips["w_up"] = res[13:15]
    pb_att = pair_sums(att_names, g_att, res[15:])
    dz, dwab, dbab, *c_att = _gla_bwd_second(z, do, sd_b, wab_pad, vec["b_ab"], dqa, dka, dva, dlra, dzg, dzcb, dconv,
                                             p["conv_w"], riders=[_chips_rider(pb_att)])
    from_chips.update(zip(att_names, c_att))
    g_in = [by_dest(_matmul_tn(dz, hb, "dw_in", rows=ZW)[0], "w_in")]
    pb_in = pair_sums(("w_in",), g_in, _exchange(_sibling_rider(g_in), "grads_to_sibling_w_in"))
    grad_x, dmix, from_chips["w_in"] = _inproj_bwd(dz, w_in, x, dx1, vec["mix_norm"], riders=[_chips_rider(pb_in)])

    small_acc = dict(mix_norm=dmix, conv_w=dcw, conv_norm=dcn, w_af=dwaf, b_af=dbaf, w_ab=dwab, b_ab=dbab,
                     gla_norm=dgn, xa_norm=dxa, mem_norm=dmemn, mlp_norm=dmlp, final_norm=dfinal)
    return loss8, grad_x, small_acc, own, from_chips


def _place():
    return lax.axis_index("x"), lax.axis_index("y"), lax.axis_index("c")


class _Rider:
    def __init__(self, arrays, out_shape, scratch, start, finish):
        self.arrays, self.out_shape, self.scratch, self.start, self.finish = arrays, out_shape, scratch, start, finish


def _gather_rider(blks):
    n = len(blks)

    def plan(in_refs, out_refs, sems):
        send_sems, recv_sems, local_sems = sems
        x, y, c = _place()
        me, sibling = (x, y, c), (x, y, 1 - c)
        chips = [(1 - x, y, c), (x, 1 - y, c), (1 - x, 1 - y, c)]

        def copy(a, k, block, to, own=False):
            px, py, pc = block
            dst = out_refs[a].at[4 * px + 2 * py + pc]
            return pltpu.make_async_remote_copy(
                src_ref=in_refs[a] if own else dst, dst_ref=dst, send_sem=send_sems.at[k, a],
                recv_sem=recv_sems.at[k, a], device_id=to, device_id_type=MESH)

        def local(a):
            return pltpu.make_async_copy(in_refs[a], out_refs[a].at[4 * x + 2 * y + c], local_sems.at[a])

        def own_sends(a):
            return [copy(a, 0, me, sibling, own=True)] + [copy(a, 1 + j, me, chip, own=True)
                                                          for j, chip in enumerate(chips)]

        return copy, local, own_sends, me, sibling, chips

    def start(in_refs, out_refs, sems):
        _, local, own_sends, _, _, _ = plan(in_refs, out_refs, sems)
        for a in range(n):
            local(a).start()
            for cp in own_sends(a):
                cp.start()

    def finish(in_refs, out_refs, sems):
        copy, local, own_sends, me, sibling, chips = plan(in_refs, out_refs, sems)
        for j, chip in enumerate(chips):
            for a in range(n):
                copy(a, 1 + j, chip, me).wait_recv()
                copy(a, 4 + j, chip, sibling).start()
        for a in range(n):
            copy(a, 0, sibling, me).wait_recv()
            for j, (px, py, pc) in enumerate(chips):
                copy(a, 4 + j, (px, py, 1 - pc), me).wait_recv()
            for cp in own_sends(a) + [copy(a, 4 + j, chip, sibling) for j, chip in enumerate(chips)]:
                cp.wait_send()
            local(a).wait()

    return _Rider(blks, [jax.ShapeDtypeStruct((NDEV,) + b.shape, b.dtype) for b in blks],
                  [pltpu.SemaphoreType.DMA((7, n)), pltpu.SemaphoreType.DMA((7, n)), pltpu.SemaphoreType.DMA((n,))],
                  start, finish)


def _sibling_rider(g4s):
    n = len(g4s)

    def copies(in_refs, out_refs, sems):
        send_sems, recv_sems = sems
        x, y, c = _place()
        return [pltpu.make_async_remote_copy(
            src_ref=in_refs[a].at[k, 1 - c], dst_ref=out_refs[a].at[k], send_sem=send_sems.at[k, a],
            recv_sem=recv_sems.at[k, a], device_id=(x, y, 1 - c), device_id_type=MESH)
            for a in range(n) for k in range(4)]

    def start(in_refs, out_refs, sems):
        for cp in copies(in_refs, out_refs, sems):
            cp.start()

    def finish(in_refs, out_refs, sems):
        for cp in copies(in_refs, out_refs, sems):
            cp.wait()

    return _Rider(g4s, [jax.ShapeDtypeStruct((4,) + g.shape[2:], g.dtype) for g in g4s],
                  [pltpu.SemaphoreType.DMA((4, n)), pltpu.SemaphoreType.DMA((4, n))], start, finish)


def _chips_rider(pbs):
    n = len(pbs)

    def copies(in_refs, out_refs, sems):
        send_sems, recv_sems = sems
        x, y, c = _place()
        peers = [(1 - x, y), (x, 1 - y), (1 - x, 1 - y)]
        return [pltpu.make_async_remote_copy(
            src_ref=in_refs[a].at[2 * px + py], dst_ref=out_refs[a].at[k], send_sem=send_sems.at[k, a],
            recv_sem=recv_sems.at[k, a], device_id=(px, py, c), device_id_type=MESH)
            for a in range(n) for k, (px, py) in enumerate(peers)]

    def start(in_refs, out_refs, sems):
        for cp in copies(in_refs, out_refs, sems):
            cp.start()

    def finish(in_refs, out_refs, sems):
        for cp in copies(in_refs, out_refs, sems):
            cp.wait()

    return _Rider(pbs, [jax.ShapeDtypeStruct((3,) + p.shape[1:], p.dtype) for p in pbs],
                  [pltpu.SemaphoreType.DMA((3, n)), pltpu.SemaphoreType.DMA((3, n))], start, finish)


def _exchange(rider, name):
    n_in, n_out = len(rider.arrays), len(rider.out_shape)

    def body(*refs):
        ins, outs, sems = refs[:n_in], refs[n_in:n_in + n_out], refs[n_in + n_out:]
        rider.start(ins, outs, sems)
        rider.finish(ins, outs, sems)

    hbm = pl.BlockSpec(memory_space=pltpu.HBM)
    return pl.pallas_call(body, name=name, out_shape=rider.out_shape, in_specs=[hbm] * n_in,
                          out_specs=[hbm] * n_out, scratch_shapes=rider.scratch)(*rider.arrays)


def _rs_pair_sum(place, g4, r1, name):
    rows, cols = g4.shape[2:]
    tr = min(rows, 512)

    def body(pl_ref, g_ref, r_ref, pb_ref, own_ref):
        s = g_ref[0, 0] + r_ref[0]
        pb_ref[0] = s.astype(BF16)

        @pl.when(pl.program_id(1) == pl_ref[0])
        def _():
            own_ref[...] = s

    grid_spec = pltpu.PrefetchScalarGridSpec(
        num_scalar_prefetch=1, grid=(rows // tr, 4),
        in_specs=[pl.BlockSpec((1, 1, tr, cols), lambda r, k, p: (k, p[1], r, 0)),
                  pl.BlockSpec((1, tr, cols), lambda r, k, p: (k, r, 0))],
        out_specs=[pl.BlockSpec((1, tr, cols), lambda r, k, p: (k, r, 0)),
                   pl.BlockSpec((tr, cols), lambda r, k, p: (r, 0))])
    return pl.pallas_call(
        body, name=name, grid_spec=grid_spec,
        out_shape=[jax.ShapeDtypeStruct((4, rows, cols), BF16), jax.ShapeDtypeStruct((rows, cols), F32)],
        compiler_params=_cparams(("arbitrary", "arbitrary")))(place, g4, r1)


PACK_ROWS = 32
VEC_ROW = {"mix_norm": 0, "conv_norm": 1, "b_af": 2, "b_ab": 3, "gla_norm": 4, "xa_norm": 5, "mem_norm": 6,
           "mlp_norm": 7, "final_norm": 8}
LOSS_ROW, MAT_ROW = 9, 16
MAT_LANE = {"w_af": 0, "w_ab": GK, "conv_w": 2 * GK}
MAT_SRC_ROW = {"w_af": 0, "w_ab": LR, "conv_w": 0}


SMALL_WIDTH = {"mix_norm": D, "conv_w": 64, "conv_norm": CW, "w_af": 32, "b_af": GK, "w_ab": 32, "b_ab": GK,
               "gla_norm": 128, "xa_norm": D, "mem_norm": D, "mlp_norm": D, "final_norm": D}


def _small_reduce(acc, loss8):
    names = list(SMALL)
    n = len(names)
    widths = SMALL_WIDTH

    def body(*refs):
        acc_refs = dict(zip(names, refs[:n]))
        loss_ref, tot = refs[n], refs[n + 1]
        pk, all_ref, send_sems, recv_sems, local_sem = refs[n + 2:]

        pk[...] = jnp.zeros_like(pk)
        for k, row in VEC_ROW.items():
            if k == "gla_norm":
                g = functools.reduce(lambda a, b: a + b, [acc_refs[k][pl.ds(0, 1), pl.ds(h * 128, 128)]
                                                          for h in range(NH)])
            else:
                g = acc_refs[k][pl.ds(0, 1), :]
            pk[pl.ds(row, 1), pl.ds(0, widths[k])] = g
        pk[pl.ds(LOSS_ROW, 1), pl.ds(0, 128)] = loss_ref[pl.ds(0, 1), :]
        for k, lane in MAT_LANE.items():
            rows, cols = (3, CW) if k == "conv_w" else (LR, GK)
            pk[pl.ds(MAT_ROW, rows), pl.ds(lane, cols)] = acc_refs[k][pl.ds(MAT_SRC_ROW[k], rows), :]

        x, y, c = _place()
        me, sibling = (x, y, c), (x, y, 1 - c)
        chips = [(1 - x, y, c), (x, 1 - y, c), (1 - x, 1 - y, c)]

        def copy(k, block, to, own=False):
            px, py, pc = block
            dst = all_ref.at[4 * px + 2 * py + pc]
            return pltpu.make_async_remote_copy(
                src_ref=pk if own else dst, dst_ref=dst, send_sem=send_sems.at[k], recv_sem=recv_sems.at[k],
                device_id=to, device_id_type=MESH)

        mine = pltpu.make_async_copy(pk, all_ref.at[4 * x + 2 * y + c], local_sem)
        mine.start()
        first = [copy(0, me, sibling, own=True)] + [copy(1 + j, me, chip, own=True) for j, chip in enumerate(chips)]
        for cp in first:
            cp.start()
        passed = [copy(4 + j, chip, sibling) for j, chip in enumerate(chips)]
        for j, chip in enumerate(chips):
            copy(1 + j, chip, me).wait_recv()
            passed[j].start()
        copy(0, sibling, me).wait_recv()
        for j, (px, py, pc) in enumerate(chips):
            copy(4 + j, (px, py, 1 - pc), me).wait_recv()
        for cp in first + passed:
            cp.wait_send()
        mine.wait()
        total = all_ref[0]
        for d in range(1, NDEV):
            total = total + all_ref[d]
        tot[...] = total

    return pl.pallas_call(
        body, name="small_reduce", out_shape=jax.ShapeDtypeStruct((PACK_ROWS, D), F32),
        scratch_shapes=[pltpu.VMEM((PACK_ROWS, D), F32), pltpu.VMEM((NDEV, PACK_ROWS, D), F32),
                        pltpu.SemaphoreType.DMA((7,)), pltpu.SemaphoreType.DMA((7,)), pltpu.SemaphoreType.DMA],
    )(*[acc[k] for k in names], loss8)


def _small_adamw(tot, ws, ms, vs):
    names = list(SMALL)
    n = len(names)
    widths = SMALL_WIDTH

    def body(*refs):
        tot = refs[0]
        w_refs, m_refs, v_refs = [dict(zip(names, refs[1 + q * n:1 + (q + 1) * n])) for q in range(3)]
        outs = refs[1 + 3 * n:1 + 7 * n]
        g_out, d_out, m_out, v_out = [dict(zip(names, outs[q * n:(q + 1) * n])) for q in range(4)]
        cut = refs[1 + 7 * n]
        x, y, c = _place()
        dev = 4 * x + 2 * y + c
        for k in names:
            if k in VEC_ROW:
                g = tot[pl.ds(VEC_ROW[k], 1), pl.ds(0, widths[k])]
            else:
                rows, cols = (3, CW) if k == "conv_w" else (LR, GK)
                wd = widths[k]
                sel = jnp.where(_iota((cols, wd), 0) == dev * wd + _iota((cols, wd), 1), 1.0, 0.0).astype(BF16)
                cut[:, pl.ds(0, wd)] = _dot_exact_rhs(tot[pl.ds(MAT_ROW, LR), pl.ds(MAT_LANE[k], cols)], sel, 3)
                g = cut[pl.ds(0, rows), pl.ds(0, wd)]
            g_out[k][...] = g
            d_out[k][...], m_out[k][...], v_out[k][...] = _adamw_math(w_refs[k][...], g, m_refs[k][...],
                                                                       v_refs[k][...])

    shapes = [jax.ShapeDtypeStruct(ws[k].shape, F32) for k in names]
    res = pl.pallas_call(
        body, name="small_adamw", out_shape=shapes * 4, scratch_shapes=[pltpu.VMEM((LR, 128), F32)],
    )(tot, *[ws[k] for k in names], *[ms[k] for k in names], *[vs[k] for k in names])
    return {k: tuple(res[q * n + i] for q in range(4)) for i, k in enumerate(names)}


def _adamw_math(w, g, m, v):
    m = ADAM_B1 * m + (1.0 - ADAM_B1) * g
    v = ADAM_B2 * v + (1.0 - ADAM_B2) * (g * g)
    m_hat = m / (1.0 - ADAM_B1 ** ADAM_STEP)
    v_hat = v / (1.0 - ADAM_B2 ** ADAM_STEP)
    delta = -ADAM_LR * (m_hat / (jnp.sqrt(v_hat) + ADAM_EPS) + ADAM_WD * w)
    return delta, m, v


def _adamw(w, m, v, own, r2, name):
    _, r, c = w.shape
    tr = 256 if r % 256 == 0 else r

    def body(w_ref, m_ref, v_ref, o_ref, r_ref, g_ref, d_ref, nm_ref, nv_ref):
        g = ((o_ref[...] + r_ref[0].astype(F32)) + r_ref[1].astype(F32)) + r_ref[2].astype(F32)
        g_ref[...] = g
        d_ref[...], nm_ref[...], nv_ref[...] = _adamw_math(w_ref[...], g, m_ref[...], v_ref[...])

    spec = pl.BlockSpec((None, tr, c), lambda i: (0, i, 0))
    return pl.pallas_call(
        body, name=name, grid=(r // tr,),
        in_specs=[spec, spec, spec, pl.BlockSpec((tr, c), lambda i: (i, 0)),
                  pl.BlockSpec((3, tr, c), lambda i: (0, i, 0))],
        out_specs=[spec] * 4, out_shape=[jax.ShapeDtypeStruct((1, r, c), F32)] * 4,
        compiler_params=_cparams(("arbitrary",)))(w, m, v, own, r2)


MATS = ("w_in", "w_out", "w_xq", "w_xo", "w_xkv", "w_up", "w_down")
SMALL = ("mix_norm", "conv_w", "conv_norm", "w_af", "b_af", "w_ab", "b_ab", "gla_norm", "xa_norm", "mem_norm",
         "mlp_norm", "final_norm")
WEIGHTS = ("mix_norm", "w_in", "conv_w", "conv_norm", "w_af", "b_af", "w_ab", "b_ab", "gla_norm", "w_out", "xa_norm",
           "mem_norm", "w_xq", "w_xkv", "w_xo", "mlp_norm", "w_up", "w_down", "final_norm")
SMALL_SHARDED = {"conv_w": (3, 64), "w_af": (LR, 32), "w_ab": (LR, 32)}
SMALL_PACK_ROWS = 16


def kernel(x, mem, mix_norm, w_in, conv_w, conv_norm, w_af, b_af, w_ab, b_ab, gla_norm, w_out, xa_norm, mem_norm, w_xq, w_xkv, w_xo, mlp_norm, w_up, w_down, final_norm, loss_target, m_mix_norm, m_w_in, m_conv_w, m_conv_norm, m_w_af, m_b_af, m_w_ab, m_b_ab, m_gla_norm, m_w_out, m_xa_norm, m_mem_norm, m_w_xq, m_w_xkv, m_w_xo, m_mlp_norm, m_w_up, m_w_down, m_final_norm, v_mix_norm, v_w_in, v_conv_w, v_conv_norm, v_w_af, v_b_af, v_w_ab, v_b_ab, v_gla_norm, v_w_out, v_xa_norm, v_mem_norm, v_w_xq, v_w_xkv, v_w_xo, v_mlp_norm, v_w_up, v_w_down, v_final_norm):
    w = dict(mix_norm=mix_norm, w_in=w_in, conv_w=conv_w, conv_norm=conv_norm, w_af=w_af, b_af=b_af, w_ab=w_ab,
             b_ab=b_ab, gla_norm=gla_norm, w_out=w_out, xa_norm=xa_norm, mem_norm=mem_norm, w_xq=w_xq, w_xkv=w_xkv,
             w_xo=w_xo, mlp_norm=mlp_norm, w_up=w_up, w_down=w_down, final_norm=final_norm)
    mom = dict(mix_norm=m_mix_norm, w_in=m_w_in, conv_w=m_conv_w, conv_norm=m_conv_norm, w_af=m_w_af, b_af=m_b_af,
               w_ab=m_w_ab, b_ab=m_b_ab, gla_norm=m_gla_norm, w_out=m_w_out, xa_norm=m_xa_norm, mem_norm=m_mem_norm,
               w_xq=m_w_xq, w_xkv=m_w_xkv, w_xo=m_w_xo, mlp_norm=m_mlp_norm, w_up=m_w_up, w_down=m_w_down,
               final_norm=m_final_norm)
    var = dict(mix_norm=v_mix_norm, w_in=v_w_in, conv_w=v_conv_w, conv_norm=v_conv_norm, w_af=v_w_af, b_af=v_b_af,
               w_ab=v_w_ab, b_ab=v_b_ab, gla_norm=v_gla_norm, w_out=v_w_out, xa_norm=v_xa_norm, mem_norm=v_mem_norm,
               w_xq=v_w_xq, w_xkv=v_w_xkv, w_xo=v_w_xo, mlp_norm=v_mlp_norm, w_up=v_w_up, w_down=v_w_down,
               final_norm=v_final_norm)
    xi, yi, ci = _place()
    two_d = lambda a: a.reshape(a.shape[-2:]) if a.ndim == 3 else a.reshape(1, a.shape[-1])

    small = jnp.concatenate([w[n].reshape(-1) for n in SMALL_SHARDED])
    small = jnp.pad(small, (0, SMALL_PACK_ROWS * 128 - small.shape[0])).reshape(SMALL_PACK_ROWS, 128)
    shard = {n: two_d(w[n]).astype(BF16) for n in MATS}
    for n in ("w_in", "w_up"):
        shard[n] = shard[n].T
    vec = {n: two_d(w[n]) for n in SMALL if n not in SMALL_SHARDED}
    place = jnp.stack([2 * xi + yi, ci]).astype(jnp.int32)
    loss8, grad_x, small_acc, own, from_chips = _step(x[0], mem[0], loss_target[0], shard, small, vec, place)

    tot = _small_reduce(small_acc, loss8)
    small_out = _small_adamw(tot, *[{n: two_d(d[n]) for n in SMALL} for d in (w, mom, var)])
    loss = tot[LOSS_ROW, 0]

    out_g, out_d, out_m, out_v = {}, {}, {}, {}
    own["w_up"], from_chips["w_up"] = own["w_up"].T, from_chips["w_up"].transpose(0, 2, 1)
    for n in MATS:
        if n == "w_in":
            res = _adamw(*[a.transpose(0, 2, 1) for a in (w[n], mom[n], var[n])], own[n], from_chips[n], "adamw_" + n)
            res = [a.transpose(0, 2, 1) for a in res]
        else:
            res = _adamw(w[n], mom[n], var[n], own[n], from_chips[n], "adamw_" + n)
        out_g[n], out_d[n], out_m[n], out_v[n] = res
    for n in SMALL:
        out_g[n], out_d[n], out_m[n], out_v[n] = [a.reshape(w[n].shape) for a in small_out[n]]

    return (loss, grad_x[None], *[out_g[n] for n in WEIGHTS], *[out_d[n] for n in WEIGHTS],
            *[out_m[n] for n in WEIGHTS], *[out_v[n] for n in WEIGHTS])
```

```python
import functools
import itertools

import jax
import jax.numpy as jnp
from jax import lax
from jax.experimental import pallas as pl
from jax.experimental.pallas import tpu as pltpu

F32 = jnp.float32
BF16 = jnp.bfloat16

D = 1024
CW = 512
GK = 256
GV = 512
NH = 4
CH = 64
LR = 16
NMEM = 256
XD = 256
FF = 4096
ZW = 3104
ZC = 3200
EPS = 1e-6
NDEV = 8

ZB_CB, ZB_CC, ZB_CU, ZB_V, ZB_G = 0, 1, 2, 4, 5
ZB_Q, ZB_K = 6, 7
ZB_LR = 24

TM = 512
TM_MLP = 256
TM_MLP_FWD = 512
TF = 512
TB = 512
TB_BWD = 512
TT = 2048
VMEM_LIMIT = 56 * 1024 * 1024

ADAM_LR, ADAM_B1, ADAM_B2, ADAM_EPS, ADAM_WD, ADAM_STEP = 0.001, 0.9, 0.999, 1e-08, 0.01, 10

XKV_SHARD = 2 * D // NDEV

MESH = pl.DeviceIdType.MESH


def _cparams(sem):
    return pltpu.CompilerParams(dimension_semantics=sem, vmem_limit_bytes=VMEM_LIMIT)


def _call(body, name, grid, in_specs, out_specs, out_shape, scratch, args, riders=()):
    n_in, n_out, n_scr = len(in_specs), len(out_specs), len(scratch)
    counts = [(len(r.arrays), len(r.out_shape), len(r.scratch)) for r in riders]

    def take(refs, pos, sizes):
        groups = []
        for size in sizes:
            groups.append(refs[pos:pos + size])
            pos += size
        return groups, pos

    def wrapped(*refs):
        ins, pos = refs[:n_in], n_in
        r_ins, pos = take(refs, pos, [c[0] for c in counts])
        outs, pos = refs[pos:pos + n_out], pos + n_out
        r_outs, pos = take(refs, pos, [c[1] for c in counts])
        scr, pos = refs[pos:pos + n_scr], pos + n_scr
        r_scr, pos = take(refs, pos, [c[2] for c in counts])
        ids = [pl.program_id(d) for d in range(len(grid))]
        first = functools.reduce(lambda a, b: a & b, [i == 0 for i in ids])
        last = functools.reduce(lambda a, b: a & b, [i == g - 1 for i, g in zip(ids, grid)])

        @pl.when(first)
        def _():
            for r, a, b, c in zip(riders, r_ins, r_outs, r_scr):
                r.start(a, b, c)

        body(*ins, *outs, *scr)

        @pl.when(last)
        def _():
            for r, a, b, c in zip(riders, r_ins, r_outs, r_scr):
                r.finish(a, b, c)

    hbm = pl.BlockSpec(memory_space=pltpu.HBM)
    r_args = [a for r in riders for a in r.arrays]
    r_shapes = [s for r in riders for s in r.out_shape]
    return pl.pallas_call(
        wrapped if riders else body, name=name, grid=grid, in_specs=list(in_specs) + [hbm] * len(r_args),
        out_specs=list(out_specs) + [hbm] * len(r_shapes), out_shape=list(out_shape) + r_shapes,
        scratch_shapes=list(scratch) + [s for r in riders for s in r.scratch],
        compiler_params=_cparams(("arbitrary",) * len(grid)))(*args, *r_args)


def _dot(a, b):
    return jnp.dot(a.astype(BF16), b.astype(BF16), preferred_element_type=F32)


def _dot_nt(a, b):
    return lax.dot_general(a.astype(BF16), b.astype(BF16), (((1,), (1,)), ((), ())), preferred_element_type=F32)


def _dot_tn(a, b):
    return lax.dot_general(a.astype(BF16), b.astype(BF16), (((0,), (0,)), ((), ())), preferred_element_type=F32)


def _split(x, n):
    parts = []
    for _ in range(n):
        p = x.astype(BF16)
        parts.append(p)
        x = x - p.astype(F32)
    return parts


def _dot_exact_lhs(m, x, n):
    return functools.reduce(lambda a, b: a + b, [jnp.dot(m, p, preferred_element_type=F32) for p in _split(x, n)])


def _dot_exact_rhs(x, m, n):
    return functools.reduce(lambda a, b: a + b, [jnp.dot(p, m, preferred_element_type=F32) for p in _split(x, n)])


def _rms(x, g):
    r = lax.rsqrt(jnp.mean(x * x, axis=-1, keepdims=True) + EPS)
    return x * r * g, r


def _rms_bwd(x, r, g, dy):
    xr = x * r
    u = dy * g
    dx = r * (u - xr * jnp.mean(u * xr, axis=-1, keepdims=True))
    return dx, jnp.sum(dy * xr, axis=0, keepdims=True)


def _iota(shape, dim):
    return lax.broadcasted_iota(jnp.int32, shape, dim)


def _sigmoid(x):
    return 1.0 / (1.0 + jnp.exp(-x))


def _acc_rows(ref, row):
    ref[...] += jnp.broadcast_to(row, ref.shape)


def _inproj(x, g, w_t, riders=()):
    t = x.shape[0]
    tm = min(TM, t)

    def body(x_ref, g_ref, w_ref, z_ref, h_ref):
        h, _ = _rms(x_ref[...], g_ref[...])
        hb = h.astype(BF16)
        h_ref[...] = hb
        z_ref[...] = _dot_nt(hb, w_ref[...])

    return _call(
        body, "inproj", (t // tm,),
        [pl.BlockSpec((tm, D), lambda i: (i, 0)), pl.BlockSpec((1, D), lambda i: (0, 0)),
         pl.BlockSpec((ZC, D), lambda i: (0, 0))],
        [pl.BlockSpec((tm, ZC), lambda i: (i, 0)), pl.BlockSpec((tm, D), lambda i: (i, 0))],
        [jax.ShapeDtypeStruct((t, ZC), F32), jax.ShapeDtypeStruct((t, D), BF16)], [], (x, g, w_t), riders)


def _kv_proj(mem, g, w):
    def body(m_ref, g_ref, w_ref, kv_ref, mn_ref):
        mn, _ = _rms(m_ref[...], g_ref[...])
        mb = mn.astype(BF16)
        mn_ref[...] = mb
        for j in range(NDEV):
            kv_ref[:, j * XKV_SHARD:(j + 1) * XKV_SHARD] = jnp.dot(mb, w_ref[j], preferred_element_type=F32)

    return pl.pallas_call(
        body, name="kv_proj",
        out_shape=[jax.ShapeDtypeStruct((NMEM, 2 * D), F32), jax.ShapeDtypeStruct((NMEM, D), BF16)],
        compiler_params=pltpu.CompilerParams(vmem_limit_bytes=VMEM_LIMIT))(mem, g, w)


def _softmax_head(qb, kb):
    s = _dot_nt(qb, kb) * (1.0 / 16.0)
    e = jnp.exp(s - jnp.max(s, axis=-1, keepdims=True))
    return e / jnp.sum(e, axis=-1, keepdims=True)


def _attn_fwd(x, z, o_f, o_b, conv_w, conv_norm, gla_norm4, w_out, g, w_xq, kb, vb, w_xo):
    t = x.shape[0]
    tm = min(TM, t)
    nblk = t // tm
    jmap = lambda i: i

    def body(x_ref, zq_ref, zk_ref, zv_ref, zg_ref, cb_ref, cc_ref, cu_ref, ccp_ref, ccn_ref, cup_ref, cun_ref,
             of_ref, ob_ref, cw_ref, cn_ref, gn_ref, wo_ref, g_ref, wq_ref, k_ref, v_ref, wx_ref,
             x1_ref, x2_ref, xn_ref, q_ref, a_ref, y_ref, opre_ref):
        j = pl.program_id(0)
        zv = zv_ref[...]
        sb = _head_sum((zq_ref[...] * 0.125) * zk_ref[...], 64, 128)
        o_pre = of_ref[...] + ob_ref[...] - sb * zv
        opre_ref[...] = o_pre
        on, _ = _head_norm(o_pre)
        zg = zg_ref[...]
        y_ref[:, CW:] = (on * gn_ref[...] * (zg * _sigmoid(zg))).astype(BF16)
        cb = cb_ref[...]
        _, _, _, conv = _conv_parts(cb, cc_ref[...], cu_ref[...], ccp_ref[pl.ds(7, 1), :], cup_ref[pl.ds(7, 1), :],
                                    ccn_ref[pl.ds(0, 1), :], cun_ref[pl.ds(0, 1), :], cw_ref, j == 0,
                                    j == nblk - 1, tm)
        yc = cb * conv
        gm = _group_sum(yc * yc) * (1.0 / 64.0)
        y_ref[:, :CW] = (yc * lax.rsqrt(gm + EPS) * cn_ref[...]).astype(BF16)

        x1 = x_ref[...] + jnp.dot(y_ref[...], wo_ref[...], preferred_element_type=F32)
        x1_ref[...] = x1
        xn, _ = _rms(x1, g_ref[...])
        xb = xn.astype(BF16)
        xn_ref[...] = xb
        qb = jnp.dot(xb, wq_ref[...], preferred_element_type=F32).astype(BF16)
        q_ref[...] = qb
        for h in range(NH):
            hs = slice(h * XD, (h + 1) * XD)
            p = _softmax_head(qb[:, hs], k_ref[:, hs])
            a_ref[:, hs] = _dot(p, v_ref[:, hs]).astype(BF16)
        x2_ref[...] = x1 + jnp.dot(a_ref[...], wx_ref[...], preferred_element_type=F32)

    tok = lambda i: (i, 0)
    full = lambda i: (0, 0)
    once = pl.Buffered(1)
    tokd, tokv = pl.BlockSpec((tm, D), tok), pl.BlockSpec((tm, GV), tok)
    weight = pl.BlockSpec((D, D), full, pipeline_mode=once)
    ccp, ccn = _halo_specs(tm, nblk, t, ZB_CC, jmap)
    cup, cun = _halo_specs(tm, nblk, t, ZB_CU, jmap)
    in_specs = [tokd, _zspec(tm, GK, ZB_Q, jmap), _zspec(tm, GK, ZB_K, jmap), _zspec(tm, GV, ZB_V, jmap),
                _zspec(tm, GV, ZB_G, jmap), _zspec(tm, CW, ZB_CB, jmap), _zspec(tm, CW, ZB_CC, jmap),
                _zspec(tm, CW, ZB_CU, jmap), ccp, ccn, cup, cun, tokv, tokv,
                pl.BlockSpec((3, CW), full), pl.BlockSpec((1, CW), full), pl.BlockSpec((1, GV), full),
                weight, pl.BlockSpec((1, D), full), weight, pl.BlockSpec((NMEM, D), full),
                pl.BlockSpec((NMEM, D), full), weight]
    return pl.pallas_call(
        body, name="attn_fwd", grid=(nblk,), in_specs=in_specs, out_specs=[tokd] * 6 + [tokv],
        out_shape=[jax.ShapeDtypeStruct((t, D), F32), jax.ShapeDtypeStruct((t, D), F32),
                   jax.ShapeDtypeStruct((t, D), BF16), jax.ShapeDtypeStruct((t, D), BF16),
                   jax.ShapeDtypeStruct((t, D), BF16), jax.ShapeDtypeStruct((t, D), BF16),
                   jax.ShapeDtypeStruct((t, GV), F32)],
        compiler_params=_cparams(("arbitrary",)))(
            x, z, z, z, z, z, z, z, z, z, z, z, o_f, o_b, conv_w, conv_norm, gla_norm4, w_out, g, w_xq, kb, vb, w_xo)


def _mlp_fwd(x2, g, w_up_t, w_down, fg, target):
    t = x2.shape[0]
    tm = min(TM_MLP_FWD, t)

    def body(x_ref, g_ref, wu_ref, wd_ref, fg_ref, t_ref, h1_ref, xn_ref, dx_ref, dxb_ref, loss_ref, dfg_ref, ab):
        @pl.when(pl.program_id(0) == 0)
        def _():
            loss_ref[...] = jnp.zeros_like(loss_ref)
            dfg_ref[...] = jnp.zeros_like(dfg_ref)

        x = x_ref[...]
        xn, _ = _rms(x, g_ref[...])
        xnb = xn.astype(BF16)
        xn_ref[...] = xnb
        for q in range(FF // TF):
            cols = slice(q * TF, (q + 1) * TF)
            h1 = _dot_nt(xnb, wu_ref[cols, :])
            h1_ref[:, cols] = h1.astype(BF16)
            hr = jnp.maximum(h1, 0.0)
            ab[:, cols] = (hr * hr).astype(BF16)
        x3 = x + jnp.dot(ab[...], wd_ref[...], preferred_element_type=F32)
        y, r = _rms(x3, fg_ref[...])
        e = y - t_ref[...]
        row = jnp.mean(e * e, axis=-1, keepdims=True)
        _acc_rows(loss_ref, 0.5 * jnp.sum(row, axis=0, keepdims=True))
        dx, dfg = _rms_bwd(x3, r, fg_ref[...], e * (1.0 / D))
        dx_ref[...] = dx
        dxb_ref[...] = dx.astype(BF16)
        _acc_rows(dfg_ref, dfg)

    tok = lambda i: (i, 0)
    full = lambda i: (0, 0)
    once = pl.Buffered(1)
    return pl.pallas_call(
        body, name="mlp_fwd", grid=(t // tm,),
        in_specs=[pl.BlockSpec((tm, D), tok), pl.BlockSpec((1, D), full),
                  pl.BlockSpec((FF, D), full, pipeline_mode=once), pl.BlockSpec((FF, D), full, pipeline_mode=once),
                  pl.BlockSpec((1, D), full), pl.BlockSpec((tm, D), tok)],
        out_specs=[pl.BlockSpec((tm, FF), tok), pl.BlockSpec((tm, D), tok), pl.BlockSpec((tm, D), tok),
                   pl.BlockSpec((tm, D), tok), pl.BlockSpec((8, 128), full), pl.BlockSpec((8, D), full)],
        out_shape=[jax.ShapeDtypeStruct((t, FF), BF16), jax.ShapeDtypeStruct((t, D), BF16),
                   jax.ShapeDtypeStruct((t, D), F32), jax.ShapeDtypeStruct((t, D), BF16),
                   jax.ShapeDtypeStruct((8, 128), F32), jax.ShapeDtypeStruct((8, D), F32)],
        scratch_shapes=[pltpu.VMEM((tm, FF), BF16)],
        compiler_params=_cparams(("arbitrary",)))(x2, g, w_up_t, w_down, fg, target)


def _mlp_bwd(dx3, dx3b, h1b, w_down, w_up_t, x2, g):
    t = x2.shape[0]
    tm = min(TM_MLP, t)

    def body(dx_ref, dxb_ref, h1_ref, wd_ref, wu_ref, x_ref, g_ref, a_ref, dh_ref, dx2_ref, dx2b_ref, dg_ref):
        @pl.when(pl.program_id(0) == 0)
        def _():
            dg_ref[...] = jnp.zeros_like(dg_ref)

        for q in range(FF // TF):
            cols = slice(q * TF, (q + 1) * TF)
            hr = jnp.maximum(h1_ref[:, cols].astype(F32), 0.0)
            da = _dot_nt(dxb_ref[...], wd_ref[cols, :])
            a_ref[:, cols] = (hr * hr).astype(BF16)
            dh_ref[:, cols] = (da * 2.0 * hr).astype(BF16)
        dxn = jnp.dot(dh_ref[...], wu_ref[...], preferred_element_type=F32)
        x = x_ref[...]
        r = lax.rsqrt(jnp.mean(x * x, axis=-1, keepdims=True) + EPS)
        dx, dg = _rms_bwd(x, r, g_ref[...], dxn)
        dx2 = dx_ref[...] + dx
        dx2_ref[...] = dx2
        dx2b_ref[...] = dx2.astype(BF16)
        _acc_rows(dg_ref, dg)

    tok = lambda i: (i, 0)
    full = lambda i: (0, 0)
    once = pl.Buffered(1)
    return pl.pallas_call(
        body, name="mlp_bwd", grid=(t // tm,),
        in_specs=[pl.BlockSpec((tm, D), tok), pl.BlockSpec((tm, D), tok), pl.BlockSpec((tm, FF), tok),
                  pl.BlockSpec((FF, D), full, pipeline_mode=once), pl.BlockSpec((FF, D), full, pipeline_mode=once),
                  pl.BlockSpec((tm, D), tok), pl.BlockSpec((1, D), full)],
        out_specs=[pl.BlockSpec((tm, FF), tok), pl.BlockSpec((tm, FF), tok), pl.BlockSpec((tm, D), tok),
                   pl.BlockSpec((tm, D), tok), pl.BlockSpec((8, D), full)],
        out_shape=[jax.ShapeDtypeStruct((t, FF), BF16), jax.ShapeDtypeStruct((t, FF), BF16),
                   jax.ShapeDtypeStruct((t, D), F32), jax.ShapeDtypeStruct((t, D), BF16),
                   jax.ShapeDtypeStruct((8, D), F32)],
        compiler_params=_cparams(("arbitrary",)))(dx3, dx3b, h1b, w_down, w_up_t, x2, g)


def _attn_bwd(x1, dx2, dx2b, qb, kb, vb, w_xo, w_xq, w_out, g):
    t = x1.shape[0]
    tm = min(TM, t)

    def body(x_ref, dx2_ref, dx2b_ref, q_ref, k_ref, v_ref, wx_ref, wq_ref, wo_ref, g_ref,
             dx1_ref, dx1b_ref, dy_ref, dq_ref, dkv_ref, dg_ref):
        @pl.when(pl.program_id(0) == 0)
        def _():
            dkv_ref[...] = jnp.zeros_like(dkv_ref)
            dg_ref[...] = jnp.zeros_like(dg_ref)

        datt = _dot_nt(dx2b_ref[...], wx_ref[...]).astype(BF16)
        for h in range(NH):
            hs = slice(h * XD, (h + 1) * XD)
            q_h, k_h, v_h, da_h = q_ref[:, hs], k_ref[:, hs], v_ref[:, hs], datt[:, hs]
            p = _softmax_head(q_h, k_h)
            dp = _dot_nt(da_h, v_h)
            ds = (p * (dp - jnp.sum(dp * p, axis=-1, keepdims=True)) * (1.0 / 16.0)).astype(BF16)
            dq_ref[:, hs] = _dot(ds, k_h).astype(BF16)
            dkv_ref[:, hs] += _dot_tn(ds, q_h)
            dkv_ref[:, D + h * XD:D + (h + 1) * XD] += _dot_tn(p, da_h)
        dxn = _dot_nt(dq_ref[...], wq_ref[...])
        x = x_ref[...]
        r = lax.rsqrt(jnp.mean(x * x, axis=-1, keepdims=True) + EPS)
        dx, dg = _rms_bwd(x, r, g_ref[...], dxn)
        dx1 = dx2_ref[...] + dx
        dx1_ref[...] = dx1
        dx1b = dx1.astype(BF16)
        dx1b_ref[...] = dx1b
        dy_ref[...] = _dot_nt(dx1b, wo_ref[...])
        _acc_rows(dg_ref, dg)

    tok = lambda i: (i, 0)
    full = lambda i: (0, 0)
    return pl.pallas_call(
        body, name="attn_bwd", grid=(t // tm,),
        in_specs=[pl.BlockSpec((tm, D), tok), pl.BlockSpec((tm, D), tok), pl.BlockSpec((tm, D), tok),
                  pl.BlockSpec((tm, D), tok), pl.BlockSpec((NMEM, D), full), pl.BlockSpec((NMEM, D), full),
                  pl.BlockSpec((D, D), full), pl.BlockSpec((D, D), full), pl.BlockSpec((D, D), full),
                  pl.BlockSpec((1, D), full)],
        out_specs=[pl.BlockSpec((tm, D), tok), pl.BlockSpec((tm, D), tok), pl.BlockSpec((tm, D), tok),
                   pl.BlockSpec((tm, D), tok), pl.BlockSpec((NMEM, 2 * D), full), pl.BlockSpec((8, D), full)],
        out_shape=[jax.ShapeDtypeStruct((t, D), F32), jax.ShapeDtypeStruct((t, D), BF16),
                   jax.ShapeDtypeStruct((t, D), F32), jax.ShapeDtypeStruct((t, D), BF16),
                   jax.ShapeDtypeStruct((NMEM, 2 * D), F32), jax.ShapeDtypeStruct((8, D), F32)],
        compiler_params=_cparams(("arbitrary",)))(x1, dx2, dx2b, qb, kb, vb, w_xo, w_xq, w_out, g)


def _kv_bwd(dkv, memn, mem, g, w):
    def body(dkv_ref, mn_ref, m_ref, g_ref, w_ref, dw_ref, dg_ref):
        dkvb = dkv_ref[...].astype(BF16)
        dmn = jnp.zeros((NMEM, D), F32)
        for j in range(NDEV):
            cols = slice(j * XKV_SHARD, (j + 1) * XKV_SHARD)
            dw_ref[j] = _dot_tn(mn_ref[...], dkvb[:, cols])
            dmn += _dot_nt(dkvb[:, cols], w_ref[j])
        m = m_ref[...]
        r = lax.rsqrt(jnp.mean(m * m, axis=-1, keepdims=True) + EPS)
        dg_ref[...] = jnp.broadcast_to(jnp.sum(dmn * m * r, axis=0, keepdims=True), dg_ref.shape)

    return pl.pallas_call(
        body, name="kv_bwd",
        out_shape=[jax.ShapeDtypeStruct((NDEV, D, XKV_SHARD), F32), jax.ShapeDtypeStruct((8, D), F32)],
        compiler_params=pltpu.CompilerParams(vmem_limit_bytes=VMEM_LIMIT))(dkv, memn, mem, g, w)


def _inproj_bwd(dz, w_t, x, dx1, g, riders=()):
    t = x.shape[0]
    tm = min(TM, t)

    def body(dz_ref, w_ref, x_ref, dx1_ref, g_ref, gx_ref, dg_ref):
        @pl.when(pl.program_id(0) == 0)
        def _():
            dg_ref[...] = jnp.zeros_like(dg_ref)

        dh = jnp.dot(dz_ref[...], w_ref[...], preferred_element_type=F32)
        x = x_ref[...]
        r = lax.rsqrt(jnp.mean(x * x, axis=-1, keepdims=True) + EPS)
        dx, dg = _rms_bwd(x, r, g_ref[...], dh)
        gx_ref[...] = dx1_ref[...] + dx
        _acc_rows(dg_ref, dg)

    tok = lambda i: (i, 0)
    full = lambda i: (0, 0)
    return _call(
        body, "inproj_bwd", (t // tm,),
        [pl.BlockSpec((tm, ZC), tok), pl.BlockSpec((ZC, D), full), pl.BlockSpec((tm, D), tok),
         pl.BlockSpec((tm, D), tok), pl.BlockSpec((1, D), full)],
        [pl.BlockSpec((tm, D), tok), pl.BlockSpec((8, D), full)],
        [jax.ShapeDtypeStruct((t, D), F32), jax.ShapeDtypeStruct((8, D), F32)], [], (dz, w_t, x, dx1, g), riders)


def _matmul_tn(a, b, name, rows=None, riders=()):
    t, k = a.shape
    n = b.shape[1]
    tk, tn = [1024 if size % 1024 == 0 else 640 for size in (k, n)]
    tt = min(TT, t)
    rows = rows or k

    def body(a_ref, b_ref, o_ref):
        @pl.when(pl.program_id(2) == 0)
        def _():
            o_ref[...] = jnp.zeros_like(o_ref)

        o_ref[...] += _dot_tn(a_ref[...], b_ref[...])

    return _call(
        body, name, (k // tk, n // tn, t // tt),
        [pl.BlockSpec((tt, tk), lambda i, j, s: (s, i)), pl.BlockSpec((tt, tn), lambda i, j, s: (s, j))],
        [pl.BlockSpec((tk, tn), lambda i, j, s: (i, j))], [jax.ShapeDtypeStruct((rows, n), F32)], [], (a, b), riders)


def _lane_head(shape, dim, shift):
    return _iota(shape, dim) >> shift


CUM_ROWS = 128


def _chunk_cumsum(x, upper, n):
    r, c = _iota((CUM_ROWS, CUM_ROWS), 0), _iota((CUM_ROWS, CUM_ROWS), 1)
    tri = (c >= r) if upper else (c <= r)
    cum = jnp.where(((r >> 6) == (c >> 6)) & tri, 1.0, 0.0).astype(BF16)
    return jnp.concatenate([_dot_exact_lhs(cum, x[g:g + CUM_ROWS], n) for g in range(0, x.shape[0], CUM_ROWS)],
                           axis=0)


def _gla_recompute(q_raw, k, lr, wpad, bias, rev, tb):
    pre = _dot(lr, wpad) + bias
    la = (jnp.minimum(pre, 0.0) - jnp.log(1.0 + jnp.exp(-jnp.abs(pre)))) * (1.0 / 16.0)
    b = _chunk_cumsum(la, rev, 3)
    e, ei = jnp.exp(b), jnp.exp(-b)
    qt = (q_raw * 0.125) * e
    kt = k * ei
    return pre, b, e, ei, qt, kt


def _stack_heads(x, shift):
    head = _lane_head(x.shape, 1, shift)
    return jnp.concatenate([jnp.where(head == h, x, 0.0) for h in range(NH)], axis=0).astype(BF16)


def _fold_heads(x, shift):
    head = _lane_head((CH, x.shape[1]), 1, shift)
    return functools.reduce(lambda a, b: a + b,
                            [jnp.where(head == h, x[h * CH:(h + 1) * CH], 0.0) for h in range(NH)])


def _wide_mask(rev):
    r, s = _iota((CH, NH * CH), 0), _iota((CH, NH * CH), 1) & (CH - 1)
    return (s >= r) if rev else (s <= r)


def _rows_by_head(x):
    w = x.shape[1] // NH
    return jnp.concatenate([x[:, h * w:(h + 1) * w] for h in range(NH)], axis=0)


def _lanes_by_head(x):
    return jnp.concatenate([x[h * CH:(h + 1) * CH] for h in range(NH)], axis=1)


def _state_compact(xt):
    head = _lane_head((128, GK), 1, 6)
    return functools.reduce(lambda a, b: a + b,
                            [jnp.where(head == h, xt[h * 128:(h + 1) * 128], 0.0) for h in range(NH)])


def _conv_parts(cb, cc, cu, ccp, cup, ccn, cun, cw_ref, first, last, tb):
    h = cc * cu
    hp = jnp.where(first, 0.0, ccp * cup)
    hn = jnp.where(last, 0.0, ccn * cun)
    rows = _iota(h.shape, 0)
    h_m1 = jnp.where(rows == 0, hp, pltpu.roll(h, 1, 0))
    h_p1 = jnp.where(rows == tb - 1, hn, pltpu.roll(h, tb - 1, 0))
    conv = cw_ref[pl.ds(0, 1), :] * h_m1 + cw_ref[pl.ds(1, 1), :] * h + cw_ref[pl.ds(2, 1), :] * h_p1
    return h, h_m1, h_p1, conv


def _head_sum(x, w_in, w_out):
    shape, sh_in, sh_out = (2 * w_in, 2 * w_out), w_in.bit_length() - 1, w_out.bit_length() - 1
    sel = jnp.where((_iota(shape, 0) >> sh_in) == (_iota(shape, 1) >> sh_out), 1.0, 0.0).astype(BF16)
    return jnp.concatenate([_dot_exact_rhs(x[:, s:s + 2 * w_in], sel, 2) for s in range(0, NH * w_in, 2 * w_in)],
                           axis=1)


def _group_sum(x):
    ones = jnp.where((_iota((128, 128), 0) >> 6) == (_iota((128, 128), 1) >> 6), 1.0, 0.0).astype(BF16)
    return jnp.concatenate([_dot_exact_rhs(x[:, s:s + 128], ones, 2) for s in range(0, x.shape[1], 128)], axis=1)


def _head_norm(o):
    ons, rs = [], []
    for h in range(NH):
        slab = o[:, h * 128:(h + 1) * 128]
        r = lax.rsqrt(jnp.mean(slab * slab, axis=-1, keepdims=True) + EPS)
        ons.append(slab * r)
        rs.append(jnp.broadcast_to(r, slab.shape))
    return jnp.concatenate(ons, axis=1), jnp.concatenate(rs, axis=1)


def _zspec(tb, width, blk, jmap):
    return pl.BlockSpec((tb, width), lambda i: (jmap(i), blk))


def _halo_specs(tb, nblk, t, blk, jmap):
    prev = pl.BlockSpec((8, CW), lambda i: (jnp.maximum(jmap(i) * (tb // 8) - 1, 0), blk))
    nxt = pl.BlockSpec((8, CW), lambda i: (jnp.minimum((jmap(i) + 1) * (tb // 8), t // 8 - 1), blk))
    return prev, nxt


def _gla_fwd_block(q_ref, k_ref, v_ref, lr_ref, w_ref, bias_ref, o_ref, sd_ref, st, b_scr, rev, tb):
    nb = tb // CH
    _, b, _, _, qt, kt = _gla_recompute(q_ref[...], k_ref[...], lr_ref[...], w_ref[...], bias_ref[...], rev, tb)
    v = v_ref[...]
    b_scr[...] = b
    yield
    maskw = _wide_mask(rev)
    order = list(reversed(range(nb))) if rev else list(range(nb))
    rows = [slice(c * CH, (c + 1) * CH) for c in range(nb)]
    state = st[...]
    for c in order:
        gdec = jnp.exp(b_scr[pl.ds(c * CH + (0 if rev else CH - 1), 1), :])
        sd_ref[c] = state
        a = jnp.where(maskw, _dot_nt(qt[rows[c]], _stack_heads(kt[rows[c]], 6)), 0.0)
        o_inter = _lanes_by_head(_dot_nt(_stack_heads(qt[rows[c]], 6), state))
        o_ref[pl.ds(c * CH, CH), :] = _dot(a, _stack_heads(v[rows[c]], 7)) + o_inter
        state = state * gdec + _state_compact(_dot_tn(v[rows[c]], kt[rows[c]] * gdec))
        yield
    st[...] = state
    yield


def _gla_fwd(z, waf_pad, b_af, wab_pad, b_ab, riders=()):
    t = z.shape[0]
    tb = min(TB, t)
    nblk, nb = t // tb, tb // CH
    jmaps = (lambda i: i, lambda i: nblk - 1 - i)

    def body(qf, kf, vf, lrf, qr, kr, vr, lrr, wf, bf, wr, br, of_ref, sdf_ref, or_ref, sdr_ref,
             st_f, st_r, b_f, b_r):
        @pl.when(pl.program_id(0) == 0)
        def _():
            st_f[...] = jnp.zeros_like(st_f)
            st_r[...] = jnp.zeros_like(st_r)

        for _ in zip(_gla_fwd_block(qf, kf, vf, lrf, wf, bf, of_ref, sdf_ref, st_f, b_f, False, tb),
                     _gla_fwd_block(qr, kr, vr, lrr, wr, br, or_ref, sdr_ref, st_r, b_r, True, tb)):
            pass

    full = lambda i: (0, 0)
    zspecs = [s for jm in jmaps for s in (_zspec(tb, GK, ZB_Q, jm), _zspec(tb, GK, ZB_K, jm),
                                         _zspec(tb, GV, ZB_V, jm), _zspec(tb, 128, ZB_LR, jm))]
    wspecs = [pl.BlockSpec((128, GK), full), pl.BlockSpec((1, GK), full)] * 2
    out_specs = [s for jm in jmaps for s in (pl.BlockSpec((tb, GV), lambda i, jm=jm: (jm(i), 0)),
                                             pl.BlockSpec((nb, 128, GK), lambda i, jm=jm: (jm(i), 0, 0)))]
    out_shape = [jax.ShapeDtypeStruct((t, GV), F32), jax.ShapeDtypeStruct((t // CH, 128, GK), F32)] * 2
    scratch = [pltpu.VMEM((128, GK), F32), pltpu.VMEM((128, GK), F32), pltpu.VMEM((tb, GK), F32),
               pltpu.VMEM((tb, GK), F32)]
    return _call(body, "gla_fwd", (nblk,), zspecs + wspecs, out_specs, out_shape, scratch,
                 [z] * 8 + [waf_pad, b_af, wab_pad, b_ab], riders)


def _gla_bwd_chunks(do_ref, sd_ref, dst, b_scr, db_scr, dq_ref, dk_ref, dv_ref, qt, kt, e, ei, v, rev, nb):
    maskw = _wide_mask(rev)
    for c in (range(nb) if rev else reversed(range(nb))):
        sl = slice(c * CH, (c + 1) * CH)
        grow = c * CH + (0 if rev else CH - 1)
        gdec = jnp.exp(b_scr[pl.ds(grow, 1), :])
        qt_c, kt_c, v_c, do_c = qt[sl], kt[sl], v[sl], do_ref[pl.ds(c * CH, CH), :]
        s_in, ds_out = sd_ref[c], dst[...]
        kbd, vbd = _stack_heads(kt_c, 6), _stack_heads(v_c, 7)
        a = jnp.where(maskw, _dot_nt(qt_c, kbd), 0.0)
        da = jnp.where(maskw, _dot_nt(do_c, vbd), 0.0)
        dv_ref[pl.ds(c * CH, CH), :] = (_fold_heads(_dot_tn(a, do_c), 7)
                                        + _lanes_by_head(_dot_nt(_stack_heads(kt_c * gdec, 6), ds_out)))
        dqt = _dot(da, kbd) + _fold_heads(_dot(_rows_by_head(do_c), s_in), 6)
        dkh = _fold_heads(_dot(_rows_by_head(v_c), ds_out), 6)
        da_do = jnp.concatenate([da.astype(BF16), do_c.astype(BF16)], axis=1)
        both = _dot_tn(da_do, qt_c)
        dkt = _fold_heads(both[:NH * CH], 6) + dkh * gdec
        dg = jnp.sum(ds_out * s_in, axis=0, keepdims=True) + jnp.sum(kt_c * dkh, axis=0, keepdims=True)
        db_scr[pl.ds(c * CH, CH), :] = dqt * qt_c - dkt * kt_c
        db_scr[pl.ds(grow, 1), :] += dg * gdec
        dq_ref[pl.ds(c * CH, CH), :] = dqt * e[sl] * 0.125
        dk_ref[pl.ds(c * CH, CH), :] = dkt * ei[sl]
        dst[...] = ds_out * gdec + _state_compact(both[NH * CH:])
        yield


def _gate_bwd(db, pre, lr, wpad, rev, tb):
    dla = _chunk_cumsum(db, not rev, 2)
    dpre = dla * (1.0 / 16.0) / (1.0 + jnp.exp(pre))
    return dpre, _dot_nt(dpre, wpad), _dot_tn(lr, dpre)


def _gla_bwd_first(z, dy, o_pre, sd, wpad, bias, conv_w, conv_norm, gla_norm4, riders=()):
    t = z.shape[0]
    tb = min(TB_BWD, t)
    nblk, nb = t // tb, tb // CH
    jmap = lambda i: nblk - 1 - i

    def body(q_ref, k_ref, v_ref, lr_ref, g_ref, cb_ref, cc_ref, cu_ref, ccp_ref, ccn_ref, cup_ref, cun_ref,
             dy_ref, opre_ref, sd_ref, w_ref, bias_ref, cw_ref, cn_ref, gn_ref,
             do_ref, dq_ref, dk_ref, dv_ref, dlr_ref, dzg_ref, dzcb_ref, dconv_ref,
             dw_ref, dbias_ref, dcw_ref, dcn_ref, dgn_ref, dst, b_scr, db_scr):
        i = pl.program_id(0)
        j = jmap(i)

        @pl.when(i == 0)
        def _():
            dst[...] = jnp.zeros_like(dst)
            for ref in (dw_ref, dbias_ref, dcw_ref, dcn_ref, dgn_ref):
                ref[...] = jnp.zeros_like(ref)

        dyg = dy_ref[:, CW:]
        g = g_ref[...]
        sig = _sigmoid(g)
        on, rr = _head_norm(opre_ref[...])
        gn = gn_ref[...]
        dzg_ref[...] = (dyg * on * gn * (sig * (1.0 + g * (1.0 - sig)))).astype(BF16)
        don = dyg * (g * sig)
        _acc_rows(dgn_ref, jnp.sum(don * on, axis=0, keepdims=True))
        u = don * gn
        uo = u * on
        mean_uo = jnp.concatenate(
            [jnp.broadcast_to(jnp.mean(uo[:, h * 128:(h + 1) * 128], axis=-1, keepdims=True), (tb, 128))
             for h in range(NH)], axis=1)
        do_ref[...] = rr * (u - on * mean_uo)

        def conv_branch():
            cb = cb_ref[...]
            h, h_m1, h_p1, conv = _conv_parts(cb, cc_ref[...], cu_ref[...], ccp_ref[pl.ds(7, 1), :],
                                              cup_ref[pl.ds(7, 1), :], ccn_ref[pl.ds(0, 1), :],
                                              cun_ref[pl.ds(0, 1), :], cw_ref, j == 0, j == nblk - 1, tb)
            yc = cb * conv
            yield
            rc = lax.rsqrt(_group_sum(yc * yc) * (1.0 / 64.0) + EPS)
            ycr = yc * rc
            yield
            dyn = dy_ref[:, :CW]
            _acc_rows(dcn_ref, jnp.sum(dyn * ycr, axis=0, keepdims=True))
            uc = dyn * cn_ref[...]
            yield
            dyc = rc * (uc - ycr * (_group_sum(uc * ycr) * (1.0 / 64.0)))
            dzcb_ref[...] = (dyc * conv).astype(BF16)
            yield
            dconv = dyc * cb
            dconv_ref[...] = dconv
            yield
            dcw_ref[pl.ds(0, 1), :] += jnp.sum(dconv * h_m1, axis=0, keepdims=True)
            dcw_ref[pl.ds(1, 1), :] += jnp.sum(dconv * h, axis=0, keepdims=True)
            dcw_ref[pl.ds(2, 1), :] += jnp.sum(dconv * h_p1, axis=0, keepdims=True)
            yield

        lr, wp = lr_ref[...], w_ref[...]
        pre, b, e, ei, qt, kt = _gla_recompute(q_ref[...], k_ref[...], lr, wp, bias_ref[...], False, tb)
        b_scr[...] = b
        for _ in itertools.zip_longest(
                _gla_bwd_chunks(do_ref, sd_ref, dst, b_scr, db_scr, dq_ref, dk_ref, dv_ref, qt, kt, e, ei, v_ref[...],
                                False, nb), conv_branch()):
            pass
        dpre, dlr, dw = _gate_bwd(db_scr[...], pre, lr, wp, False, tb)
        dlr_ref[...] = dlr
        dw_ref[...] += dw
        _acc_rows(dbias_ref, jnp.sum(dpre, axis=0, keepdims=True))

    full = lambda i: (0, 0)
    tokv = pl.BlockSpec((tb, GV), lambda i: (jmap(i), 0))
    tokk = pl.BlockSpec((tb, GK), lambda i: (jmap(i), 0))
    ccp, ccn = _halo_specs(tb, nblk, t, ZB_CC, jmap)
    cup, cun = _halo_specs(tb, nblk, t, ZB_CU, jmap)
    in_specs = [_zspec(tb, GK, ZB_Q, jmap), _zspec(tb, GK, ZB_K, jmap), _zspec(tb, GV, ZB_V, jmap),
                _zspec(tb, 128, ZB_LR, jmap), _zspec(tb, GV, ZB_G, jmap), _zspec(tb, CW, ZB_CB, jmap),
                _zspec(tb, CW, ZB_CC, jmap), _zspec(tb, CW, ZB_CU, jmap), ccp, ccn, cup, cun,
                pl.BlockSpec((tb, D), lambda i: (jmap(i), 0)), tokv,
                pl.BlockSpec((nb, 128, GK), lambda i: (jmap(i), 0, 0)), pl.BlockSpec((128, GK), full),
                pl.BlockSpec((1, GK), full), pl.BlockSpec((3, CW), full), pl.BlockSpec((1, CW), full),
                pl.BlockSpec((1, GV), full)]
    out_specs = [tokv, tokk, tokk, tokv, pl.BlockSpec((tb, 128), lambda i: (jmap(i), 0)), tokv, tokv, tokv,
                 pl.BlockSpec((128, GK), full), pl.BlockSpec((8, GK), full), pl.BlockSpec((8, CW), full),
                 pl.BlockSpec((8, CW), full), pl.BlockSpec((8, GV), full)]
    out_shape = [jax.ShapeDtypeStruct((t, GV), F32), jax.ShapeDtypeStruct((t, GK), F32),
                 jax.ShapeDtypeStruct((t, GK), F32), jax.ShapeDtypeStruct((t, GV), F32),
                 jax.ShapeDtypeStruct((t, 128), F32), jax.ShapeDtypeStruct((t, GV), BF16),
                 jax.ShapeDtypeStruct((t, CW), BF16), jax.ShapeDtypeStruct((t, CW), F32),
                 jax.ShapeDtypeStruct((128, GK), F32), jax.ShapeDtypeStruct((8, GK), F32),
                 jax.ShapeDtypeStruct((8, CW), F32), jax.ShapeDtypeStruct((8, CW), F32),
                 jax.ShapeDtypeStruct((8, GV), F32)]
    return _call(
        body, "gla_bwd_first", (nblk,), in_specs, out_specs, out_shape,
        [pltpu.VMEM((128, GK), F32), pltpu.VMEM((tb, GK), F32), pltpu.VMEM((tb, GK), F32)],
        (z, z, z, z, z, z, z, z, z, z, z, z, dy, o_pre, sd, wpad, bias, conv_w, conv_norm, gla_norm4), riders)


def _gla_bwd_second(z, do, sd, wpad, bias, dqa, dka, dva, dlra, dzg, dzcb, dconv, conv_w, riders=()):
    t = z.shape[0]
    tb = min(TB_BWD, t)
    nblk, nb = t // tb, tb // CH
    jmap = lambda i: i

    def body(q_ref, k_ref, v_ref, lr_ref, cc_ref, cu_ref, do_ref, sd_ref, w_ref, bias_ref, dqa_ref, dka_ref,
             dva_ref, dlra_ref, dzg_ref, dzcb_ref, dc_ref, dcp_ref, dcn_ref, cw_ref,
             dz_ref, dw_ref, dbias_ref, dst, b_scr, db_scr, dq_scr, dk_scr, dv_scr, sb_scr, dsk_scr):
        i = pl.program_id(0)

        @pl.when(i == 0)
        def _():
            dst[...] = jnp.zeros_like(dst)
            dw_ref[...] = jnp.zeros_like(dw_ref)
            dbias_ref[...] = jnp.zeros_like(dbias_ref)

        q_raw, k, v, lr, wp = q_ref[...], k_ref[...], v_ref[...], lr_ref[...], w_ref[...]
        pre, b, e, ei, qt, kt = _gla_recompute(q_raw, k, lr, wp, bias_ref[...], True, tb)
        b_scr[...] = b

        def token_local():
            dc = dc_ref[...]
            rows = _iota(dc.shape, 0)
            dprev = jnp.where(i == 0, 0.0, dcp_ref[pl.ds(7, 1), :])
            dnext = jnp.where(i == nblk - 1, 0.0, dcn_ref[pl.ds(0, 1), :])
            dc_m1 = jnp.where(rows == 0, dprev, pltpu.roll(dc, 1, 0))
            dc_p1 = jnp.where(rows == tb - 1, dnext, pltpu.roll(dc, tb - 1, 0))
            yield
            dh = cw_ref[pl.ds(0, 1), :] * dc_p1 + cw_ref[pl.ds(1, 1), :] * dc + cw_ref[pl.ds(2, 1), :] * dc_m1
            dz_ref[:, 0:512] = dzcb_ref[...]
            yield
            dz_ref[:, 512:1024] = (dh * cu_ref[...]).astype(BF16)
            dz_ref[:, 1024:1536] = (dh * cc_ref[...]).astype(BF16)
            dz_ref[:, 2560:3072] = dzg_ref[...]
            yield
            sb_scr[...] = _head_sum((q_raw * 0.125) * k, 64, 128)
            yield
            dsk_scr[...] = _head_sum(do_ref[...] * v, 128, 64)
            yield

        for _ in itertools.zip_longest(
                _gla_bwd_chunks(do_ref, sd_ref, dst, b_scr, db_scr, dq_scr, dk_scr, dv_scr, qt, kt, e, ei, v, True, nb),
                token_local()):
            pass
        dpre, dlr, dw = _gate_bwd(db_scr[...], pre, lr, wp, True, tb)
        dw_ref[...] += dw
        _acc_rows(dbias_ref, jnp.sum(dpre, axis=0, keepdims=True))
        dsk = dsk_scr[...]
        dz_ref[:, 1536:1792] = (dqa_ref[...] + dq_scr[...] - dsk * k * 0.125).astype(BF16)
        dz_ref[:, 1792:2048] = (dka_ref[...] + dk_scr[...] - dsk * (q_raw * 0.125)).astype(BF16)
        dz_ref[:, 2048:2560] = (dva_ref[...] + dv_scr[...] - sb_scr[...] * do_ref[...]).astype(BF16)
        dz_ref[:, 3072:3200] = (dlra_ref[...] + dlr).astype(BF16)

    full = lambda i: (0, 0)
    tokv = pl.BlockSpec((tb, GV), lambda i: (i, 0))
    tokk = pl.BlockSpec((tb, GK), lambda i: (i, 0))
    dcp = pl.BlockSpec((8, CW), lambda i: (jnp.maximum(i * (tb // 8) - 1, 0), 0))
    dcn = pl.BlockSpec((8, CW), lambda i: (jnp.minimum((i + 1) * (tb // 8), t // 8 - 1), 0))
    in_specs = [_zspec(tb, GK, ZB_Q, jmap), _zspec(tb, GK, ZB_K, jmap), _zspec(tb, GV, ZB_V, jmap),
                _zspec(tb, 128, ZB_LR, jmap), _zspec(tb, CW, ZB_CC, jmap), _zspec(tb, CW, ZB_CU, jmap), tokv,
                pl.BlockSpec((nb, 128, GK), lambda i: (i, 0, 0)), pl.BlockSpec((128, GK), full),
                pl.BlockSpec((1, GK), full), tokk, tokk, tokv, pl.BlockSpec((tb, 128), lambda i: (i, 0)), tokv, tokv,
                tokv, dcp, dcn, pl.BlockSpec((3, CW), full)]
    out_specs = [pl.BlockSpec((tb, ZC), lambda i: (i, 0)), pl.BlockSpec((128, GK), full), pl.BlockSpec((8, GK), full)]
    out_shape = [jax.ShapeDtypeStruct((t, ZC), BF16), jax.ShapeDtypeStruct((128, GK), F32),
                 jax.ShapeDtypeStruct((8, GK), F32)]
    return _call(
        body, "gla_bwd_second", (nblk,), in_specs, out_specs, out_shape,
        [pltpu.VMEM((128, GK), F32), pltpu.VMEM((tb, GK), F32), pltpu.VMEM((tb, GK), F32),
         pltpu.VMEM((tb, GK), F32), pltpu.VMEM((tb, GK), F32), pltpu.VMEM((tb, GV), F32),
         pltpu.VMEM((tb, GV), F32), pltpu.VMEM((tb, GK), F32)],
        (z, z, z, z, z, z, do, sd, wpad, bias, dqa, dka, dva, dlra, dzg, dzcb, dconv, dconv, dconv, conv_w), riders)


def _step(x, mem, target, shard, small_pack, vec, place):
    own, from_chips = {}, {}

    def pair_sums(names, g4, from_sibling):
        pbs = []
        for n, g, s in zip(names, g4, from_sibling):
            pb, own[n] = _rs_pair_sum(place, g, s, "pair_sum_" + n)
            pbs.append(pb)
        return pbs

    def by_dest(g, n):
        return g.reshape((4, 2) + shard[n].shape)

    w_in, small_all = _exchange(_gather_rider([shard["w_in"], small_pack]), "gather_w_in")
    w_in = jnp.pad(w_in.reshape(ZW, D), ((0, ZC - ZW), (0, 0)))
    small_all = small_all.reshape(NDEV, -1)
    p, off = {}, 0
    for n, (r, c) in SMALL_SHARDED.items():
        p[n] = small_all[:, off:off + r * c].reshape(NDEV, r, c).transpose(1, 0, 2).reshape(r, NDEV * c)
        off += r * c
    zeros_lr = jnp.zeros((128 - LR, GK), BF16)
    waf_pad = jnp.concatenate([p["w_af"].astype(BF16), zeros_lr], axis=0)
    wab_pad = jnp.concatenate([jnp.zeros((LR, GK), BF16), p["w_ab"].astype(BF16), zeros_lr[:128 - 2 * LR]], axis=0)
    gla_norm4 = jnp.tile(vec["gla_norm"], (1, NH))

    z, hb, w_out, w_xq, w_xo, w_xkv = _inproj(
        x, vec["mix_norm"], w_in, [_gather_rider([shard[n] for n in ("w_out", "w_xq", "w_xo", "w_xkv")])])
    w_out, w_xq, w_xo = [a.reshape(D, D) for a in (w_out, w_xq, w_xo)]
    o_f, sd_f, o_b, sd_b, w_up_t, w_down = _gla_fwd(
        z, waf_pad, vec["b_af"], wab_pad, vec["b_ab"], [_gather_rider([shard["w_up"], shard["w_down"]])])
    w_up_t, w_down = w_up_t.reshape(FF, D), w_down.reshape(FF, D)
    kv, memn = _kv_proj(mem, vec["mem_norm"], w_xkv)
    kb, vb = kv[:, :D].astype(BF16), kv[:, D:].astype(BF16)
    x1, x2, xn1, qb, attb, yb, o_pre = _attn_fwd(x, z, o_f, o_b, p["conv_w"], vec["conv_norm"], gla_norm4, w_out,
                                                 vec["xa_norm"], w_xq, kb, vb, w_xo)
    h1b, xn2, dx3, dx3b, loss8, dfinal = _mlp_fwd(x2, vec["mlp_norm"], w_up_t, w_down, vec["final_norm"], target)

    ab, dh1b, dx2, dx2b, dmlp = _mlp_bwd(dx3, dx3b, h1b, w_down, w_up_t, x2, vec["mlp_norm"])
    g_mlp = [by_dest(_matmul_tn(ab, dx3b, "dw_down")[0], "w_down"),
             by_dest(_matmul_tn(dh1b, xn2, "dw_up")[0], "w_up")]
    dx1, dx1b, dy, dqb, dkv, dxa = _attn_bwd(x1, dx2, dx2b, qb, kb, vb, w_xo, w_xq, w_out, vec["xa_norm"])
    dw_xo, *s_mlp = _matmul_tn(attb, dx2b, "dw_xo", riders=[_sibling_rider(g_mlp)])
    pb_mlp = pair_sums(("w_down", "w_up"), g_mlp, s_mlp)
    dw_xkv, dmemn = _kv_bwd(dkv, memn, mem, vec["mem_norm"], w_xkv)
    att_names = ("w_xo", "w_xq", "w_out", "w_xkv")
    g_att = [by_dest(g, n) for g, n in zip(
        (dw_xo, _matmul_tn(xn1, dqb, "dw_xq")[0], _matmul_tn(yb, dx1b, "dw_out")[0], dw_xkv), att_names)]
    res = _gla_bwd_first(z, dy, o_pre, sd_f, waf_pad, vec["b_af"], p["conv_w"], vec["conv_norm"], gla_norm4,
                         riders=[_chips_rider(pb_mlp), _sibling_rider(g_att)])
    do, dqa, dka, dva, dlra, dzg, dzcb, dconv, dwaf, dbaf, dcw, dcn, dgn = res[:13]
    from_chips["w_down"], from_chips["w_up"] = res[13:15]
    pb_att = pair_sums(att_names, g_att, res[15:])
    dz, dwab, dbab, *c_att = _gla_bwd_second(z, do, sd_b, wab_pad, vec["b_ab"], dqa, dka, dva, dlra, dzg, dzcb, dconv,
                                             p["conv_w"], riders=[_chips_rider(pb_att)])
    from_chips.update(zip(att_names, c_att))
    g_in = [by_dest(_matmul_tn(dz, hb, "dw_in", rows=ZW)[0], "w_in")]
    pb_in = pair_sums(("w_in",), g_in, _exchange(_sibling_rider(g_in), "grads_to_sibling_w_in"))
    grad_x, dmix, from_chips["w_in"] = _inproj_bwd(dz, w_in, x, dx1, vec["mix_norm"], riders=[_chips_rider(pb_in)])

    small_acc = dict(mix_norm=dmix, conv_w=dcw, conv_norm=dcn, w_af=dwaf, b_af=dbaf, w_ab=dwab, b_ab=dbab,
                     gla_norm=dgn, xa_norm=dxa, mem_norm=dmemn, mlp_norm=dmlp, final_norm=dfinal)
    return loss8, grad_x, small_acc, own, from_chips


def _place():
    return lax.axis_index("x"), lax.axis_index("y"), lax.axis_index("c")


class _Rider:
    def __init__(self, arrays, out_shape, scratch, start, finish):
        self.arrays, self.out_shape, self.scratch, self.start, self.finish = arrays, out_shape, scratch, start, finish


def _gather_rider(blks):
    n = len(blks)

    def plan(in_refs, out_refs, sems):
        send_sems, recv_sems, local_sems = sems
        x, y, c = _place()
        me, sibling = (x, y, c), (x, y, 1 - c)
        chips = [(1 - x, y, c), (x, 1 - y, c), (1 - x, 1 - y, c)]

        def copy(a, k, block, to, own=False):
            px, py, pc = block
            dst = out_refs[a].at[4 * px + 2 * py + pc]
            return pltpu.make_async_remote_copy(
                src_ref=in_refs[a] if own else dst, dst_ref=dst, send_sem=send_sems.at[k, a],
                recv_sem=recv_sems.at[k, a], device_id=to, device_id_type=MESH)

        def local(a):
            return pltpu.make_async_copy(in_refs[a], out_refs[a].at[4 * x + 2 * y + c], local_sems.at[a])

        def own_sends(a):
            return [copy(a, 0, me, sibling, own=True)] + [copy(a, 1 + j, me, chip, own=True)
                                                          for j, chip in enumerate(chips)]

        return copy, local, own_sends, me, sibling, chips

    def start(in_refs, out_refs, sems):
        _, local, own_sends, _, _, _ = plan(in_refs, out_refs, sems)
        for a in range(n):
            local(a).start()
            for cp in own_sends(a):
                cp.start()

    def finish(in_refs, out_refs, sems):
        copy, local, own_sends, me, sibling, chips = plan(in_refs, out_refs, sems)
        for j, chip in enumerate(chips):
            for a in range(n):
                copy(a, 1 + j, chip, me).wait_recv()
                copy(a, 4 + j, chip, sibling).start()
        for a in range(n):
            copy(a, 0, sibling, me).wait_recv()
            for j, (px, py, pc) in enumerate(chips):
                copy(a, 4 + j, (px, py, 1 - pc), me).wait_recv()
            for cp in own_sends(a) + [copy(a, 4 + j, chip, sibling) for j, chip in enumerate(chips)]:
                cp.wait_send()
            local(a).wait()

    return _Rider(blks, [jax.ShapeDtypeStruct((NDEV,) + b.shape, b.dtype) for b in blks],
                  [pltpu.SemaphoreType.DMA((7, n)), pltpu.SemaphoreType.DMA((7, n)), pltpu.SemaphoreType.DMA((n,))],
                  start, finish)


def _sibling_rider(g4s):
    n = len(g4s)

    def copies(in_refs, out_refs, sems):
        send_sems, recv_sems = sems
        x, y, c = _place()
        return [pltpu.make_async_remote_copy(
            src_ref=in_refs[a].at[k, 1 - c], dst_ref=out_refs[a].at[k], send_sem=send_sems.at[k, a],
            recv_sem=recv_sems.at[k, a], device_id=(x, y, 1 - c), device_id_type=MESH)
            for a in range(n) for k in range(4)]

    def start(in_refs, out_refs, sems):
        for cp in copies(in_refs, out_refs, sems):
            cp.start()

    def finish(in_refs, out_refs, sems):
        for cp in copies(in_refs, out_refs, sems):
            cp.wait()

    return _Rider(g4s, [jax.ShapeDtypeStruct((4,) + g.shape[2:], g.dtype) for g in g4s],
                  [pltpu.SemaphoreType.DMA((4, n)), pltpu.SemaphoreType.DMA((4, n))], start, finish)


def _chips_rider(pbs):
    n = len(pbs)

    def copies(in_refs, out_refs, sems):
        send_sems, recv_sems = sems
        x, y, c = _place()
        peers = [(1 - x, y), (x, 1 - y), (1 - x, 1 - y)]
        return [pltpu.make_async_remote_copy(
            src_ref=in_refs[a].at[2 * px + py], dst_ref=out_refs[a].at[k], send_sem=send_sems.at[k, a],
            recv_sem=recv_sems.at[k, a], device_id=(px, py, c), device_id_type=MESH)
            for a in range(n) for k, (px, py) in enumerate(peers)]

    def start(in_refs, out_refs, sems):
        for cp in copies(in_refs, out_refs, sems):
            cp.start()

    def finish(in_refs, out_refs, sems):
        for cp in copies(in_refs, out_refs, sems):
            cp.wait()

    return _Rider(pbs, [jax.ShapeDtypeStruct((3,) + p.shape[1:], p.dtype) for p in pbs],
                  [pltpu.SemaphoreType.DMA((3, n)), pltpu.SemaphoreType.DMA((3, n))], start, finish)


def _exchange(rider, name):
    n_in, n_out = len(rider.arrays), len(rider.out_shape)

    def body(*refs):
        ins, outs, sems = refs[:n_in], refs[n_in:n_in + n_out], refs[n_in + n_out:]
        rider.start(ins, outs, sems)
        rider.finish(ins, outs, sems)

    hbm = pl.BlockSpec(memory_space=pltpu.HBM)
    return pl.pallas_call(body, name=name, out_shape=rider.out_shape, in_specs=[hbm] * n_in,
                          out_specs=[hbm] * n_out, scratch_shapes=rider.scratch)(*rider.arrays)


def _rs_pair_sum(place, g4, r1, name):
    rows, cols = g4.shape[2:]
    tr = min(rows, 512)

    def body(pl_ref, g_ref, r_ref, pb_ref, own_ref):
        s = g_ref[0, 0] + r_ref[0]
        pb_ref[0] = s.astype(BF16)

        @pl.when(pl.program_id(1) == pl_ref[0])
        def _():
            own_ref[...] = s

    grid_spec = pltpu.PrefetchScalarGridSpec(
        num_scalar_prefetch=1, grid=(rows // tr, 4),
        in_specs=[pl.BlockSpec((1, 1, tr, cols), lambda r, k, p: (k, p[1], r, 0)),
                  pl.BlockSpec((1, tr, cols), lambda r, k, p: (k, r, 0))],
        out_specs=[pl.BlockSpec((1, tr, cols), lambda r, k, p: (k, r, 0)),
                   pl.BlockSpec((tr, cols), lambda r, k, p: (r, 0))])
    return pl.pallas_call(
        body, name=name, grid_spec=grid_spec,
        out_shape=[jax.ShapeDtypeStruct((4, rows, cols), BF16), jax.ShapeDtypeStruct((rows, cols), F32)],
        compiler_params=_cparams(("arbitrary", "arbitrary")))(place, g4, r1)


PACK_ROWS = 32
VEC_ROW = {"mix_norm": 0, "conv_norm": 1, "b_af": 2, "b_ab": 3, "gla_norm": 4, "xa_norm": 5, "mem_norm": 6,
           "mlp_norm": 7, "final_norm": 8}
LOSS_ROW, MAT_ROW = 9, 16
MAT_LANE = {"w_af": 0, "w_ab": GK, "conv_w": 2 * GK}
MAT_SRC_ROW = {"w_af": 0, "w_ab": LR, "conv_w": 0}


SMALL_WIDTH = {"mix_norm": D, "conv_w": 64, "conv_norm": CW, "w_af": 32, "b_af": GK, "w_ab": 32, "b_ab": GK,
               "gla_norm": 128, "xa_norm": D, "mem_norm": D, "mlp_norm": D, "final_norm": D}


def _small_reduce(acc, loss8):
    names = list(SMALL)
    n = len(names)
    widths = SMALL_WIDTH

    def body(*refs):
        acc_refs = dict(zip(names, refs[:n]))
        loss_ref, tot = refs[n], refs[n + 1]
        pk, all_ref, send_sems, recv_sems, local_sem = refs[n + 2:]

        pk[...] = jnp.zeros_like(pk)
        for k, row in VEC_ROW.items():
            if k == "gla_norm":
                g = functools.reduce(lambda a, b: a + b, [acc_refs[k][pl.ds(0, 1), pl.ds(h * 128, 128)]
                                                          for h in range(NH)])
            else:
                g = acc_refs[k][pl.ds(0, 1), :]
            pk[pl.ds(row, 1), pl.ds(0, widths[k])] = g
        pk[pl.ds(LOSS_ROW, 1), pl.ds(0, 128)] = loss_ref[pl.ds(0, 1), :]
        for k, lane in MAT_LANE.items():
            rows, cols = (3, CW) if k == "conv_w" else (LR, GK)
            pk[pl.ds(MAT_ROW, rows), pl.ds(lane, cols)] = acc_refs[k][pl.ds(MAT_SRC_ROW[k], rows), :]

        x, y, c = _place()
        me, sibling = (x, y, c), (x, y, 1 - c)
        chips = [(1 - x, y, c), (x, 1 - y, c), (1 - x, 1 - y, c)]

        def copy(k, block, to, own=False):
            px, py, pc = block
            dst = all_ref.at[4 * px + 2 * py + pc]
            return pltpu.make_async_remote_copy(
                src_ref=pk if own else dst, dst_ref=dst, send_sem=send_sems.at[k], recv_sem=recv_sems.at[k],
                device_id=to, device_id_type=MESH)

        mine = pltpu.make_async_copy(pk, all_ref.at[4 * x + 2 * y + c], local_sem)
        mine.start()
        first = [copy(0, me, sibling, own=True)] + [copy(1 + j, me, chip, own=True) for j, chip in enumerate(chips)]
        for cp in first:
            cp.start()
        passed = [copy(4 + j, chip, sibling) for j, chip in enumerate(chips)]
        for j, chip in enumerate(chips):
            copy(1 + j, chip, me).wait_recv()
            passed[j].start()
        copy(0, sibling, me).wait_recv()
        for j, (px, py, pc) in enumerate(chips):
            copy(4 + j, (px, py, 1 - pc), me).wait_recv()
        for cp in first + passed:
            cp.wait_send()
        mine.wait()
        total = all_ref[0]
        for d in range(1, NDEV):
            total = total + all_ref[d]
        tot[...] = total

    return pl.pallas_call(
        body, name="small_reduce", out_shape=jax.ShapeDtypeStruct((PACK_ROWS, D), F32),
        scratch_shapes=[pltpu.VMEM((PACK_ROWS, D), F32), pltpu.VMEM((NDEV, PACK_ROWS, D), F32),
                        pltpu.SemaphoreType.DMA((7,)), pltpu.SemaphoreType.DMA((7,)), pltpu.SemaphoreType.DMA],
    )(*[acc[k] for k in names], loss8)


def _small_adamw(tot, ws, ms, vs):
    names = list(SMALL)
    n = len(names)
    widths = SMALL_WIDTH

    def body(*refs):
        tot = refs[0]
        w_refs, m_refs, v_refs = [dict(zip(names, refs[1 + q * n:1 + (q + 1) * n])) for q in range(3)]
        outs = refs[1 + 3 * n:1 + 7 * n]
        g_out, d_out, m_out, v_out = [dict(zip(names, outs[q * n:(q + 1) * n])) for q in range(4)]
        cut = refs[1 + 7 * n]
        x, y, c = _place()
        dev = 4 * x + 2 * y + c
        for k in names:
            if k in VEC_ROW:
                g = tot[pl.ds(VEC_ROW[k], 1), pl.ds(0, widths[k])]
            else:
                rows, cols = (3, CW) if k == "conv_w" else (LR, GK)
                wd = widths[k]
                sel = jnp.where(_iota((cols, wd), 0) == dev * wd + _iota((cols, wd), 1), 1.0, 0.0).astype(BF16)
                cut[:, pl.ds(0, wd)] = _dot_exact_rhs(tot[pl.ds(MAT_ROW, LR), pl.ds(MAT_LANE[k], cols)], sel, 3)
                g = cut[pl.ds(0, rows), pl.ds(0, wd)]
            g_out[k][...] = g
            d_out[k][...], m_out[k][...], v_out[k][...] = _adamw_math(w_refs[k][...], g, m_refs[k][...],
                                                                       v_refs[k][...])

    shapes = [jax.ShapeDtypeStruct(ws[k].shape, F32) for k in names]
    res = pl.pallas_call(
        body, name="small_adamw", out_shape=shapes * 4, scratch_shapes=[pltpu.VMEM((LR, 128), F32)],
    )(tot, *[ws[k] for k in names], *[ms[k] for k in names], *[vs[k] for k in names])
    return {k: tuple(res[q * n + i] for q in range(4)) for i, k in enumerate(names)}


def _adamw_math(w, g, m, v):
    m = ADAM_B1 * m + (1.0 - ADAM_B1) * g
    v = ADAM_B2 * v + (1.0 - ADAM_B2) * (g * g)
    m_hat = m / (1.0 - ADAM_B1 ** ADAM_STEP)
    v_hat = v / (1.0 - ADAM_B2 ** ADAM_STEP)
    delta = -ADAM_LR * (m_hat / (jnp.sqrt(v_hat) + ADAM_EPS) + ADAM_WD * w)
    return delta, m, v


def _adamw(w, m, v, own, r2, name):
    _, r, c = w.shape
    tr = 256 if r % 256 == 0 else r

    def body(w_ref, m_ref, v_ref, o_ref, r_ref, g_ref, d_ref, nm_ref, nv_ref):
        g = ((o_ref[...] + r_ref[0].astype(F32)) + r_ref[1].astype(F32)) + r_ref[2].astype(F32)
        g_ref[...] = g
        d_ref[...], nm_ref[...], nv_ref[...] = _adamw_math(w_ref[...], g, m_ref[...], v_ref[...])

    spec = pl.BlockSpec((None, tr, c), lambda i: (0, i, 0))
    return pl.pallas_call(
        body, name=name, grid=(r // tr,),
        in_specs=[spec, spec, spec, pl.BlockSpec((tr, c), lambda i: (i, 0)),
                  pl.BlockSpec((3, tr, c), lambda i: (0, i, 0))],
        out_specs=[spec] * 4, out_shape=[jax.ShapeDtypeStruct((1, r, c), F32)] * 4,
        compiler_params=_cparams(("arbitrary",)))(w, m, v, own, r2)


MATS = ("w_in", "w_out", "w_xq", "w_xo", "w_xkv", "w_up", "w_down")
SMALL = ("mix_norm", "conv_w", "conv_norm", "w_af", "b_af", "w_ab", "b_ab", "gla_norm", "xa_norm", "mem_norm",
         "mlp_norm", "final_norm")
WEIGHTS = ("mix_norm", "w_in", "conv_w", "conv_norm", "w_af", "b_af", "w_ab", "b_ab", "gla_norm", "w_out", "xa_norm",
           "mem_norm", "w_xq", "w_xkv", "w_xo", "mlp_norm", "w_up", "w_down", "final_norm")
SMALL_SHARDED = {"conv_w": (3, 64), "w_af": (LR, 32), "w_ab": (LR, 32)}
SMALL_PACK_ROWS = 16


def kernel(x, mem, mix_norm, w_in, conv_w, conv_norm, w_af, b_af, w_ab, b_ab, gla_norm, w_out, xa_norm, mem_norm, w_xq, w_xkv, w_xo, mlp_norm, w_up, w_down, final_norm, loss_target, m_mix_norm, m_w_in, m_conv_w, m_conv_norm, m_w_af, m_b_af, m_w_ab, m_b_ab, m_gla_norm, m_w_out, m_xa_norm, m_mem_norm, m_w_xq, m_w_xkv, m_w_xo, m_mlp_norm, m_w_up, m_w_down, m_final_norm, v_mix_norm, v_w_in, v_conv_w, v_conv_norm, v_w_af, v_b_af, v_w_ab, v_b_ab, v_gla_norm, v_w_out, v_xa_norm, v_mem_norm, v_w_xq, v_w_xkv, v_w_xo, v_mlp_norm, v_w_up, v_w_down, v_final_norm):
    w = dict(mix_norm=mix_norm, w_in=w_in, conv_w=conv_w, conv_norm=conv_norm, w_af=w_af, b_af=b_af, w_ab=w_ab,
             b_ab=b_ab, gla_norm=gla_norm, w_out=w_out, xa_norm=xa_norm, mem_norm=mem_norm, w_xq=w_xq, w_xkv=w_xkv,
             w_xo=w_xo, mlp_norm=mlp_norm, w_up=w_up, w_down=w_down, final_norm=final_norm)
    mom = dict(mix_norm=m_mix_norm, w_in=m_w_in, conv_w=m_conv_w, conv_norm=m_conv_norm, w_af=m_w_af, b_af=m_b_af,
               w_ab=m_w_ab, b_ab=m_b_ab, gla_norm=m_gla_norm, w_out=m_w_out, xa_norm=m_xa_norm, mem_norm=m_mem_norm,
               w_xq=m_w_xq, w_xkv=m_w_xkv, w_xo=m_w_xo, mlp_norm=m_mlp_norm, w_up=m_w_up, w_down=m_w_down,
               final_norm=m_final_norm)
    var = dict(mix_norm=v_mix_norm, w_in=v_w_in, conv_w=v_conv_w, conv_norm=v_conv_norm, w_af=v_w_af, b_af=v_b_af,
               w_ab=v_w_ab, b_ab=v_b_ab, gla_norm=v_gla_norm, w_out=v_w_out, xa_norm=v_xa_norm, mem_norm=v_mem_norm,
               w_xq=v_w_xq, w_xkv=v_w_xkv, w_xo=v_w_xo, mlp_norm=v_mlp_norm, w_up=v_w_up, w_down=v_w_down,
               final_norm=v_final_norm)
    xi, yi, ci = _place()
    two_d = lambda a: a.reshape(a.shape[-2:]) if a.ndim == 3 else a.reshape(1, a.shape[-1])

    small = jnp.concatenate([w[n].reshape(-1) for n in SMALL_SHARDED])
    small = jnp.pad(small, (0, SMALL_PACK_ROWS * 128 - small.shape[0])).reshape(SMALL_PACK_ROWS, 128)
    shard = {n: two_d(w[n]).astype(BF16) for n in MATS}
    for n in ("w_in", "w_up"):
        shard[n] = shard[n].T
    vec = {n: two_d(w[n]) for n in SMALL if n not in SMALL_SHARDED}
    place = jnp.stack([2 * xi + yi, ci]).astype(jnp.int32)
    loss8, grad_x, small_acc, own, from_chips = _step(x[0], mem[0], loss_target[0], shard, small, vec, place)

    tot = _small_reduce(small_acc, loss8)
    small_out = _small_adamw(tot, *[{n: two_d(d[n]) for n in SMALL} for d in (w, mom, var)])
    loss = tot[LOSS_ROW, 0]

    out_g, out_d, out_m, out_v = {}, {}, {}, {}
    own["w_up"], from_chips["w_up"] = own["w_up"].T, from_chips["w_up"].transpose(0, 2, 1)
    for n in MATS:
        if n == "w_in":
            res = _adamw(*[a.transpose(0, 2, 1) for a in (w[n], mom[n], var[n])], own[n], from_chips[n], "adamw_" + n)
            res = [a.transpose(0, 2, 1) for a in res]
        else:
            res = _adamw(w[n], mom[n], var[n], own[n], from_chips[n], "adamw_" + n)
        out_g[n], out_d[n], out_m[n], out_v[n] = res
    for n in SMALL:
        out_g[n], out_d[n], out_m[n], out_v[n] = [a.reshape(w[n].shape) for a in small_out[n]]

    return (loss, grad_x[None], *[out_g[n] for n in WEIGHTS], *[out_d[n] for n in WEIGHTS],
            *[out_m[n] for n in WEIGHTS], *[out_v[n] for n in WEIGHTS])
```

```python
import functools
import itertools

import jax
import jax.numpy as jnp
from jax import lax
from jax.experimental import pallas as pl
from jax.experimental.pallas import tpu as pltpu

F32 = jnp.float32
BF16 = jnp.bfloat16

D = 1024
CW = 512
GK = 256
GV = 512
NH = 4
CH = 64
LR = 16
NMEM = 256
XD = 256
FF = 4096
ZW = 3104
ZC = 3200
EPS = 1e-6
NDEV = 8

ZB_CB, ZB_CC, ZB_CU, ZB_V, ZB_G = 0, 1, 2, 4, 5
ZB_Q, ZB_K = 6, 7
ZB_LR = 24

TM = 512
TM_MLP = 256
TM_MLP_FWD = 512
TF = 512
TB = 512
TB_BWD = 512
TT = 2048
VMEM_LIMIT = 56 * 1024 * 1024

ADAM_LR, ADAM_B1, ADAM_B2, ADAM_EPS, ADAM_WD, ADAM_STEP = 0.001, 0.9, 0.999, 1e-08, 0.01, 10

XKV_SHARD = 2 * D // NDEV

MESH = pl.DeviceIdType.MESH


def _cparams(sem):
    return pltpu.CompilerParams(dimension_semantics=sem, vmem_limit_bytes=VMEM_LIMIT)


def _call(body, name, grid, in_specs, out_specs, out_shape, scratch, args, riders=()):
    n_in, n_out, n_scr = len(in_specs), len(out_specs), len(scratch)
    counts = [(len(r.arrays), len(r.out_shape), len(r.scratch)) for r in riders]

    def take(refs, pos, sizes):
        groups = []
        for size in sizes:
            groups.append(refs[pos:pos + size])
            pos += size
        return groups, pos

    def wrapped(*refs):
        ins, pos = refs[:n_in], n_in
        r_ins, pos = take(refs, pos, [c[0] for c in counts])
        outs, pos = refs[pos:pos + n_out], pos + n_out
        r_outs, pos = take(refs, pos, [c[1] for c in counts])
        scr, pos = refs[pos:pos + n_scr], pos + n_scr
        r_scr, pos = take(refs, pos, [c[2] for c in counts])
        ids = [pl.program_id(d) for d in range(len(grid))]
        first = functools.reduce(lambda a, b: a & b, [i == 0 for i in ids])
        last = functools.reduce(lambda a, b: a & b, [i == g - 1 for i, g in zip(ids, grid)])

        @pl.when(first)
        def _():
            for r, a, b, c in zip(riders, r_ins, r_outs, r_scr):
                r.start(a, b, c)

        body(*ins, *outs, *scr)

        @pl.when(last)
        def _():
            for r, a, b, c in zip(riders, r_ins, r_outs, r_scr):
                r.finish(a, b, c)

    hbm = pl.BlockSpec(memory_space=pltpu.HBM)
    r_args = [a for r in riders for a in r.arrays]
    r_shapes = [s for r in riders for s in r.out_shape]
    return pl.pallas_call(
        wrapped if riders else body, name=name, grid=grid, in_specs=list(in_specs) + [hbm] * len(r_args),
        out_specs=list(out_specs) + [hbm] * len(r_shapes), out_shape=list(out_shape) + r_shapes,
        scratch_shapes=list(scratch) + [s for r in riders for s in r.scratch],
        compiler_params=_cparams(("arbitrary",) * len(grid)))(*args, *r_args)


def _dot(a, b):
    return jnp.dot(a.astype(BF16), b.astype(BF16), preferred_element_type=F32)


def _dot_nt(a, b):
    return lax.dot_general(a.astype(BF16), b.astype(BF16), (((1,), (1,)), ((), ())), preferred_element_type=F32)


def _dot_tn(a, b):
    return lax.dot_general(a.astype(BF16), b.astype(BF16), (((0,), (0,)), ((), ())), preferred_element_type=F32)


def _split(x, n):
    parts = []
    for _ in range(n):
        p = x.astype(BF16)
        parts.append(p)
        x = x - p.astype(F32)
    return parts


def _dot_exact_lhs(m, x, n):
    return functools.reduce(lambda a, b: a + b, [jnp.dot(m, p, preferred_element_type=F32) for p in _split(x, n)])


def _dot_exact_rhs(x, m, n):
    return functools.reduce(lambda a, b: a + b, [jnp.dot(p, m, preferred_element_type=F32) for p in _split(x, n)])


def _rms(x, g):
    r = lax.rsqrt(jnp.mean(x * x, axis=-1, keepdims=True) + EPS)
    return x * r * g, r


def _rms_bwd(x, r, g, dy):
    xr = x * r
    u = dy * g
    dx = r * (u - xr * jnp.mean(u * xr, axis=-1, keepdims=True))
    return dx, jnp.sum(dy * xr, axis=0, keepdims=True)


def _iota(shape, dim):
    return lax.broadcasted_iota(jnp.int32, shape, dim)


def _sigmoid(x):
    return 1.0 / (1.0 + jnp.exp(-x))


def _acc_rows(ref, row):
    ref[...] += jnp.broadcast_to(row, ref.shape)


def _inproj(x, g, w_t, riders=()):
    t = x.shape[0]
    tm = min(TM, t)

    def body(x_ref, g_ref, w_ref, z_ref, h_ref):
        h, _ = _rms(x_ref[...], g_ref[...])
        hb = h.astype(BF16)
        h_ref[...] = hb
        z_ref[...] = _dot_nt(hb, w_ref[...])

    return _call(
        body, "inproj", (t // tm,),
        [pl.BlockSpec((tm, D), lambda i: (i, 0)), pl.BlockSpec((1, D), lambda i: (0, 0)),
         pl.BlockSpec((ZC, D), lambda i: (0, 0))],
        [pl.BlockSpec((tm, ZC), lambda i: (i, 0)), pl.BlockSpec((tm, D), lambda i: (i, 0))],
        [jax.ShapeDtypeStruct((t, ZC), F32), jax.ShapeDtypeStruct((t, D), BF16)], [], (x, g, w_t), riders)


def _kv_proj(mem, g, w):
    def body(m_ref, g_ref, w_ref, kv_ref, mn_ref):
        mn, _ = _rms(m_ref[...], g_ref[...])
        mb = mn.astype(BF16)
        mn_ref[...] = mb
        for j in range(NDEV):
            kv_ref[:, j * XKV_SHARD:(j + 1) * XKV_SHARD] = jnp.dot(mb, w_ref[j], preferred_element_type=F32)

    return pl.pallas_call(
        body, name="kv_proj",
        out_shape=[jax.ShapeDtypeStruct((NMEM, 2 * D), F32), jax.ShapeDtypeStruct((NMEM, D), BF16)],
        compiler_params=pltpu.CompilerParams(vmem_limit_bytes=VMEM_LIMIT))(mem, g, w)


def _softmax_head(qb, kb):
    s = _dot_nt(qb, kb) * (1.0 / 16.0)
    e = jnp.exp(s - jnp.max(s, axis=-1, keepdims=True))
    return e / jnp.sum(e, axis=-1, keepdims=True)


def _attn_fwd(x, z, o_f, o_b, conv_w, conv_norm, gla_norm4, w_out, g, w_xq, kb, vb, w_xo, riders=()):
    t = x.shape[0]
    tm = min(TM, t)
    nblk = t // tm
    jmap = lambda i: i

    def body(x_ref, zq_ref, zk_ref, zv_ref, zg_ref, cb_ref, cc_ref, cu_ref, ccp_ref, ccn_ref, cup_ref, cun_ref,
             of_ref, ob_ref, cw_ref, cn_ref, gn_ref, wo_ref, g_ref, wq_ref, k_ref, v_ref, wx_ref,
             x1_ref, x2_ref, xn_ref, q_ref, a_ref, y_ref, opre_ref):
        j = pl.program_id(0)
        zv = zv_ref[...]
        sb = _head_sum((zq_ref[...] * 0.125) * zk_ref[...], 64, 128)
        o_pre = of_ref[...] + ob_ref[...] - sb * zv
        opre_ref[...] = o_pre
        on, _ = _head_norm(o_pre)
        zg = zg_ref[...]
        y_ref[:, CW:] = (on * gn_ref[...] * (zg * _sigmoid(zg))).astype(BF16)
        cb = cb_ref[...]
        _, _, _, conv = _conv_parts(cb, cc_ref[...], cu_ref[...], ccp_ref[pl.ds(7, 1), :], cup_ref[pl.ds(7, 1), :],
                                    ccn_ref[pl.ds(0, 1), :], cun_ref[pl.ds(0, 1), :], cw_ref, j == 0,
                                    j == nblk - 1, tm)
        yc = cb * conv
        gm = _group_sum(yc * yc) * (1.0 / 64.0)
        y_ref[:, :CW] = (yc * lax.rsqrt(gm + EPS) * cn_ref[...]).astype(BF16)

        x1 = x_ref[...] + jnp.dot(y_ref[...], wo_ref[...], preferred_element_type=F32)
        x1_ref[...] = x1
        xn, _ = _rms(x1, g_ref[...])
        xb = xn.astype(BF16)
        xn_ref[...] = xb
        qb = jnp.dot(xb, wq_ref[...], preferred_element_type=F32).astype(BF16)
        q_ref[...] = qb
        for h in range(NH):
            hs = slice(h * XD, (h + 1) * XD)
            p = _softmax_head(qb[:, hs], k_ref[:, hs])
            a_ref[:, hs] = _dot(p, v_ref[:, hs]).astype(BF16)
        x2_ref[...] = x1 + jnp.dot(a_ref[...], wx_ref[...], preferred_element_type=F32)

    tok = lambda i: (i, 0)
    full = lambda i: (0, 0)
    once = pl.Buffered(1)
    tokd, tokv = pl.BlockSpec((tm, D), tok), pl.BlockSpec((tm, GV), tok)
    weight = pl.BlockSpec((D, D), full, pipeline_mode=once)
    ccp, ccn = _halo_specs(tm, nblk, t, ZB_CC, jmap)
    cup, cun = _halo_specs(tm, nblk, t, ZB_CU, jmap)
    in_specs = [tokd, _zspec(tm, GK, ZB_Q, jmap), _zspec(tm, GK, ZB_K, jmap), _zspec(tm, GV, ZB_V, jmap),
                _zspec(tm, GV, ZB_G, jmap), _zspec(tm, CW, ZB_CB, jmap), _zspec(tm, CW, ZB_CC, jmap),
                _zspec(tm, CW, ZB_CU, jmap), ccp, ccn, cup, cun, tokv, tokv,
                pl.BlockSpec((3, CW), full), pl.BlockSpec((1, CW), full), pl.BlockSpec((1, GV), full),
                weight, pl.BlockSpec((1, D), full), weight, pl.BlockSpec((NMEM, D), full),
                pl.BlockSpec((NMEM, D), full), weight]
    return _call(
        body, "attn_fwd", (nblk,), in_specs, [tokd] * 6 + [tokv],
        [jax.ShapeDtypeStruct((t, D), F32), jax.ShapeDtypeStruct((t, D), F32),
         jax.ShapeDtypeStruct((t, D), BF16), jax.ShapeDtypeStruct((t, D), BF16),
         jax.ShapeDtypeStruct((t, D), BF16), jax.ShapeDtypeStruct((t, D), BF16),
         jax.ShapeDtypeStruct((t, GV), F32)], [],
        (x, z, z, z, z, z, z, z, z, z, z, z, o_f, o_b, conv_w, conv_norm, gla_norm4, w_out, g, w_xq, kb, vb, w_xo),
        riders)


def _mlp_fwd(x2, g, w_up_t, w_down, fg, target):
    t = x2.shape[0]
    tm = min(TM_MLP_FWD, t)

    def body(x_ref, g_ref, wu_ref, wd_ref, fg_ref, t_ref, h1_ref, xn_ref, dx_ref, dxb_ref, loss_ref, dfg_ref, ab):
        @pl.when(pl.program_id(0) == 0)
        def _():
            loss_ref[...] = jnp.zeros_like(loss_ref)
            dfg_ref[...] = jnp.zeros_like(dfg_ref)

        x = x_ref[...]
        xn, _ = _rms(x, g_ref[...])
        xnb = xn.astype(BF16)
        xn_ref[...] = xnb
        for q in range(FF // TF):
            cols = slice(q * TF, (q + 1) * TF)
            h1 = _dot_nt(xnb, wu_ref[cols, :])
            h1_ref[:, cols] = h1.astype(BF16)
            hr = jnp.maximum(h1, 0.0)
            ab[:, cols] = (hr * hr).astype(BF16)
        x3 = x + jnp.dot(ab[...], wd_ref[...], preferred_element_type=F32)
        y, r = _rms(x3, fg_ref[...])
        e = y - t_ref[...]
        row = jnp.mean(e * e, axis=-1, keepdims=True)
        _acc_rows(loss_ref, 0.5 * jnp.sum(row, axis=0, keepdims=True))
        dx, dfg = _rms_bwd(x3, r, fg_ref[...], e * (1.0 / D))
        dx_ref[...] = dx
        dxb_ref[...] = dx.astype(BF16)
        _acc_rows(dfg_ref, dfg)

    tok = lambda i: (i, 0)
    full = lambda i: (0, 0)
    once = pl.Buffered(1)
    return pl.pallas_call(
        body, name="mlp_fwd", grid=(t // tm,),
        in_specs=[pl.BlockSpec((tm, D), tok), pl.BlockSpec((1, D), full),
                  pl.BlockSpec((FF, D), full, pipeline_mode=once), pl.BlockSpec((FF, D), full, pipeline_mode=once),
                  pl.BlockSpec((1, D), full), pl.BlockSpec((tm, D), tok)],
        out_specs=[pl.BlockSpec((tm, FF), tok), pl.BlockSpec((tm, D), tok), pl.BlockSpec((tm, D), tok),
                   pl.BlockSpec((tm, D), tok), pl.BlockSpec((8, 128), full), pl.BlockSpec((8, D), full)],
        out_shape=[jax.ShapeDtypeStruct((t, FF), BF16), jax.ShapeDtypeStruct((t, D), BF16),
                   jax.ShapeDtypeStruct((t, D), F32), jax.ShapeDtypeStruct((t, D), BF16),
                   jax.ShapeDtypeStruct((8, 128), F32), jax.ShapeDtypeStruct((8, D), F32)],
        scratch_shapes=[pltpu.VMEM((tm, FF), BF16)],
        compiler_params=_cparams(("arbitrary",)))(x2, g, w_up_t, w_down, fg, target)


def _mlp_bwd(dx3, dx3b, h1b, w_down, w_up_t, x2, g):
    t = x2.shape[0]
    tm = min(TM_MLP, t)

    def body(dx_ref, dxb_ref, h1_ref, wd_ref, wu_ref, x_ref, g_ref, a_ref, dh_ref, dx2_ref, dx2b_ref, dg_ref):
        @pl.when(pl.program_id(0) == 0)
        def _():
            dg_ref[...] = jnp.zeros_like(dg_ref)

        for q in range(FF // TF):
            cols = slice(q * TF, (q + 1) * TF)
            hr = jnp.maximum(h1_ref[:, cols].astype(F32), 0.0)
            da = _dot_nt(dxb_ref[...], wd_ref[cols, :])
            a_ref[:, cols] = (hr * hr).astype(BF16)
            dh_ref[:, cols] = (da * 2.0 * hr).astype(BF16)
        dxn = jnp.dot(dh_ref[...], wu_ref[...], preferred_element_type=F32)
        x = x_ref[...]
        r = lax.rsqrt(jnp.mean(x * x, axis=-1, keepdims=True) + EPS)
        dx, dg = _rms_bwd(x, r, g_ref[...], dxn)
        dx2 = dx_ref[...] + dx
        dx2_ref[...] = dx2
        dx2b_ref[...] = dx2.astype(BF16)
        _acc_rows(dg_ref, dg)

    tok = lambda i: (i, 0)
    full = lambda i: (0, 0)
    once = pl.Buffered(1)
    return pl.pallas_call(
        body, name="mlp_bwd", grid=(t // tm,),
        in_specs=[pl.BlockSpec((tm, D), tok), pl.BlockSpec((tm, D), tok), pl.BlockSpec((tm, FF), tok),
                  pl.BlockSpec((FF, D), full, pipeline_mode=once), pl.BlockSpec((FF, D), full, pipeline_mode=once),
                  pl.BlockSpec((tm, D), tok), pl.BlockSpec((1, D), full)],
        out_specs=[pl.BlockSpec((tm, FF), tok), pl.BlockSpec((tm, FF), tok), pl.BlockSpec((tm, D), tok),
                   pl.BlockSpec((tm, D), tok), pl.BlockSpec((8, D), full)],
        out_shape=[jax.ShapeDtypeStruct((t, FF), BF16), jax.ShapeDtypeStruct((t, FF), BF16),
                   jax.ShapeDtypeStruct((t, D), F32), jax.ShapeDtypeStruct((t, D), BF16),
                   jax.ShapeDtypeStruct((8, D), F32)],
        compiler_params=_cparams(("arbitrary",)))(dx3, dx3b, h1b, w_down, w_up_t, x2, g)


def _attn_bwd(x1, dx2, dx2b, qb, kb, vb, w_xo, w_xq, w_out, g, z, o_pre, conv_w, conv_norm, gla_norm4):
    t = x1.shape[0]
    tm = min(TM, t)
    nblk = t // tm
    jmap = lambda i: i

    def body(x_ref, dx2_ref, dx2b_ref, q_ref, k_ref, v_ref, wx_ref, wq_ref, wo_ref, g_ref,
             zg_ref, cb_ref, cc_ref, cu_ref, ccp_ref, ccn_ref, cup_ref, cun_ref, opre_ref, cw_ref, cn_ref, gn_ref,
             dx1_ref, dx1b_ref, dq_ref, dkv_ref, dg_ref,
             do_ref, dzg_ref, dzcb_ref, dconv_ref, dcw_ref, dcn_ref, dgn_ref):
        j = pl.program_id(0)

        @pl.when(j == 0)
        def _():
            for ref in (dkv_ref, dg_ref, dcw_ref, dcn_ref, dgn_ref):
                ref[...] = jnp.zeros_like(ref)

        datt = _dot_nt(dx2b_ref[...], wx_ref[...]).astype(BF16)
        for h in range(NH):
            hs = slice(h * XD, (h + 1) * XD)
            q_h, k_h, v_h, da_h = q_ref[:, hs], k_ref[:, hs], v_ref[:, hs], datt[:, hs]
            p = _softmax_head(q_h, k_h)
            dp = _dot_nt(da_h, v_h)
            ds = (p * (dp - jnp.sum(dp * p, axis=-1, keepdims=True)) * (1.0 / 16.0)).astype(BF16)
            dq_ref[:, hs] = _dot(ds, k_h).astype(BF16)
            dkv_ref[:, hs] += _dot_tn(ds, q_h)
            dkv_ref[:, D + h * XD:D + (h + 1) * XD] += _dot_tn(p, da_h)
        dxn = _dot_nt(dq_ref[...], wq_ref[...])
        x = x_ref[...]
        r = lax.rsqrt(jnp.mean(x * x, axis=-1, keepdims=True) + EPS)
        dx, dg = _rms_bwd(x, r, g_ref[...], dxn)
        dx1 = dx2_ref[...] + dx
        dx1_ref[...] = dx1
        dx1b = dx1.astype(BF16)
        dx1b_ref[...] = dx1b
        _acc_rows(dg_ref, dg)
        dy = _dot_nt(dx1b, wo_ref[...])

        dyg = dy[:, CW:]
        zg = zg_ref[...]
        sig = _sigmoid(zg)
        on, rr = _head_norm(opre_ref[...])
        gn = gn_ref[...]
        dzg_ref[...] = (dyg * on * gn * (sig * (1.0 + zg * (1.0 - sig)))).astype(BF16)
        don = dyg * (zg * sig)
        _acc_rows(dgn_ref, jnp.sum(don * on, axis=0, keepdims=True))
        u = don * gn
        uo = u * on
        mean_uo = jnp.concatenate(
            [jnp.broadcast_to(jnp.mean(uo[:, h * 128:(h + 1) * 128], axis=-1, keepdims=True), (tm, 128))
             for h in range(NH)], axis=1)
        do_ref[...] = rr * (u - on * mean_uo)

        cb = cb_ref[...]
        hh, h_m1, h_p1, conv = _conv_parts(cb, cc_ref[...], cu_ref[...], ccp_ref[pl.ds(7, 1), :],
                                           cup_ref[pl.ds(7, 1), :], ccn_ref[pl.ds(0, 1), :], cun_ref[pl.ds(0, 1), :],
                                           cw_ref, j == 0, j == nblk - 1, tm)
        yc = cb * conv
        rc = lax.rsqrt(_group_sum(yc * yc) * (1.0 / 64.0) + EPS)
        ycr = yc * rc
        dyn = dy[:, :CW]
        _acc_rows(dcn_ref, jnp.sum(dyn * ycr, axis=0, keepdims=True))
        uc = dyn * cn_ref[...]
        dyc = rc * (uc - ycr * (_group_sum(uc * ycr) * (1.0 / 64.0)))
        dzcb_ref[...] = (dyc * conv).astype(BF16)
        dconv = dyc * cb
        dconv_ref[...] = dconv
        dcw_ref[pl.ds(0, 1), :] += jnp.sum(dconv * h_m1, axis=0, keepdims=True)
        dcw_ref[pl.ds(1, 1), :] += jnp.sum(dconv * hh, axis=0, keepdims=True)
        dcw_ref[pl.ds(2, 1), :] += jnp.sum(dconv * h_p1, axis=0, keepdims=True)

    tok = lambda i: (i, 0)
    full = lambda i: (0, 0)
    tokd, tokv = pl.BlockSpec((tm, D), tok), pl.BlockSpec((tm, GV), tok)
    weight = pl.BlockSpec((D, D), full, pipeline_mode=pl.Buffered(1))
    ccp, ccn = _halo_specs(tm, nblk, t, ZB_CC, jmap)
    cup, cun = _halo_specs(tm, nblk, t, ZB_CU, jmap)
    acc = pl.BlockSpec((8, CW), full)
    return pl.pallas_call(
        body, name="attn_bwd", grid=(nblk,),
        in_specs=[tokd, tokd, tokd, tokd, pl.BlockSpec((NMEM, D), full), pl.BlockSpec((NMEM, D), full),
                  weight, weight, weight, pl.BlockSpec((1, D), full),
                  _zspec(tm, GV, ZB_G, jmap), _zspec(tm, CW, ZB_CB, jmap), _zspec(tm, CW, ZB_CC, jmap),
                  _zspec(tm, CW, ZB_CU, jmap), ccp, ccn, cup, cun, tokv, pl.BlockSpec((3, CW), full),
                  pl.BlockSpec((1, CW), full), pl.BlockSpec((1, GV), full)],
        out_specs=[tokd, tokd, tokd, pl.BlockSpec((NMEM, 2 * D), full), pl.BlockSpec((8, D), full),
                   tokv, tokv, tokv, tokv, acc, acc, acc],
        out_shape=[jax.ShapeDtypeStruct((t, D), F32), jax.ShapeDtypeStruct((t, D), BF16),
                   jax.ShapeDtypeStruct((t, D), BF16), jax.ShapeDtypeStruct((NMEM, 2 * D), F32),
                   jax.ShapeDtypeStruct((8, D), F32),
                   jax.ShapeDtypeStruct((t, GV), F32), jax.ShapeDtypeStruct((t, GV), BF16),
                   jax.ShapeDtypeStruct((t, CW), BF16), jax.ShapeDtypeStruct((t, CW), F32),
                   jax.ShapeDtypeStruct((8, CW), F32), jax.ShapeDtypeStruct((8, CW), F32),
                   jax.ShapeDtypeStruct((8, GV), F32)],
        compiler_params=_cparams(("arbitrary",)))(
            x1, dx2, dx2b, qb, kb, vb, w_xo, w_xq, w_out, g, z, z, z, z, z, z, z, z, o_pre, conv_w, conv_norm,
            gla_norm4)


def _kv_bwd(dkv, memn, mem, g, w):
    def body(dkv_ref, mn_ref, m_ref, g_ref, w_ref, dw_ref, dg_ref):
        dkvb = dkv_ref[...].astype(BF16)
        dmn = jnp.zeros((NMEM, D), F32)
        for j in range(NDEV):
            cols = slice(j * XKV_SHARD, (j + 1) * XKV_SHARD)
            dw_ref[j] = _dot_tn(mn_ref[...], dkvb[:, cols])
            dmn += _dot_nt(dkvb[:, cols], w_ref[j])
        m = m_ref[...]
        r = lax.rsqrt(jnp.mean(m * m, axis=-1, keepdims=True) + EPS)
        dg_ref[...] = jnp.broadcast_to(jnp.sum(dmn * m * r, axis=0, keepdims=True), dg_ref.shape)

    return pl.pallas_call(
        body, name="kv_bwd",
        out_shape=[jax.ShapeDtypeStruct((NDEV, D, XKV_SHARD), F32), jax.ShapeDtypeStruct((8, D), F32)],
        compiler_params=pltpu.CompilerParams(vmem_limit_bytes=VMEM_LIMIT))(dkv, memn, mem, g, w)


def _inproj_bwd(dz, w_t, x, dx1, g, riders=()):
    t = x.shape[0]
    tm = min(TM, t)

    def body(dz_ref, w_ref, x_ref, dx1_ref, g_ref, gx_ref, dg_ref):
        @pl.when(pl.program_id(0) == 0)
        def _():
            dg_ref[...] = jnp.zeros_like(dg_ref)

        dh = jnp.dot(dz_ref[...], w_ref[...], preferred_element_type=F32)
        x = x_ref[...]
        r = lax.rsqrt(jnp.mean(x * x, axis=-1, keepdims=True) + EPS)
        dx, dg = _rms_bwd(x, r, g_ref[...], dh)
        gx_ref[...] = dx1_ref[...] + dx
        _acc_rows(dg_ref, dg)

    tok = lambda i: (i, 0)
    full = lambda i: (0, 0)
    return _call(
        body, "inproj_bwd", (t // tm,),
        [pl.BlockSpec((tm, ZC), tok), pl.BlockSpec((ZC, D), full), pl.BlockSpec((tm, D), tok),
         pl.BlockSpec((tm, D), tok), pl.BlockSpec((1, D), full)],
        [pl.BlockSpec((tm, D), tok), pl.BlockSpec((8, D), full)],
        [jax.ShapeDtypeStruct((t, D), F32), jax.ShapeDtypeStruct((8, D), F32)], [], (dz, w_t, x, dx1, g), riders)


def _matmul_tn(a, b, name, rows=None, riders=()):
    t, k = a.shape
    n = b.shape[1]
    tk, tn = [1024 if size % 1024 == 0 else 640 for size in (k, n)]
    tt = min(TT, t)
    rows = rows or k

    def body(a_ref, b_ref, o_ref):
        @pl.when(pl.program_id(2) == 0)
        def _():
            o_ref[...] = jnp.zeros_like(o_ref)

        o_ref[...] += _dot_tn(a_ref[...], b_ref[...])

    return _call(
        body, name, (k // tk, n // tn, t // tt),
        [pl.BlockSpec((tt, tk), lambda i, j, s: (s, i)), pl.BlockSpec((tt, tn), lambda i, j, s: (s, j))],
        [pl.BlockSpec((tk, tn), lambda i, j, s: (i, j))], [jax.ShapeDtypeStruct((rows, n), F32)], [], (a, b), riders)


def _lane_head(shape, dim, shift):
    return _iota(shape, dim) >> shift


CUM_ROWS = 128


def _chunk_cumsum(x, upper, n):
    r, c = _iota((CUM_ROWS, CUM_ROWS), 0), _iota((CUM_ROWS, CUM_ROWS), 1)
    tri = (c >= r) if upper else (c <= r)
    cum = jnp.where(((r >> 6) == (c >> 6)) & tri, 1.0, 0.0).astype(BF16)
    return jnp.concatenate([_dot_exact_lhs(cum, x[g:g + CUM_ROWS], n) for g in range(0, x.shape[0], CUM_ROWS)],
                           axis=0)


def _gla_recompute(q_raw, k, lr, wpad, bias, rev, tb):
    pre = _dot(lr, wpad) + bias
    la = (jnp.minimum(pre, 0.0) - jnp.log(1.0 + jnp.exp(-jnp.abs(pre)))) * (1.0 / 16.0)
    b = _chunk_cumsum(la, rev, 3)
    e, ei = jnp.exp(b), jnp.exp(-b)
    qt = (q_raw * 0.125) * e
    kt = k * ei
    return pre, b, e, ei, qt, kt


def _stack_heads(x, shift):
    head = _lane_head(x.shape, 1, shift)
    return jnp.concatenate([jnp.where(head == h, x, 0.0) for h in range(NH)], axis=0).astype(BF16)


def _fold_heads(x, shift):
    head = _lane_head((CH, x.shape[1]), 1, shift)
    return functools.reduce(lambda a, b: a + b,
                            [jnp.where(head == h, x[h * CH:(h + 1) * CH], 0.0) for h in range(NH)])


def _wide_mask(rev):
    r, s = _iota((CH, NH * CH), 0), _iota((CH, NH * CH), 1) & (CH - 1)
    return (s >= r) if rev else (s <= r)


def _rows_by_head(x):
    w = x.shape[1] // NH
    return jnp.concatenate([x[:, h * w:(h + 1) * w] for h in range(NH)], axis=0)


def _lanes_by_head(x):
    return jnp.concatenate([x[h * CH:(h + 1) * CH] for h in range(NH)], axis=1)


def _state_compact(xt):
    head = _lane_head((128, GK), 1, 6)
    return functools.reduce(lambda a, b: a + b,
                            [jnp.where(head == h, xt[h * 128:(h + 1) * 128], 0.0) for h in range(NH)])


def _conv_parts(cb, cc, cu, ccp, cup, ccn, cun, cw_ref, first, last, tb):
    h = cc * cu
    hp = jnp.where(first, 0.0, ccp * cup)
    hn = jnp.where(last, 0.0, ccn * cun)
    rows = _iota(h.shape, 0)
    h_m1 = jnp.where(rows == 0, hp, pltpu.roll(h, 1, 0))
    h_p1 = jnp.where(rows == tb - 1, hn, pltpu.roll(h, tb - 1, 0))
    conv = cw_ref[pl.ds(0, 1), :] * h_m1 + cw_ref[pl.ds(1, 1), :] * h + cw_ref[pl.ds(2, 1), :] * h_p1
    return h, h_m1, h_p1, conv


def _head_sum(x, w_in, w_out):
    shape, sh_in, sh_out = (2 * w_in, 2 * w_out), w_in.bit_length() - 1, w_out.bit_length() - 1
    sel = jnp.where((_iota(shape, 0) >> sh_in) == (_iota(shape, 1) >> sh_out), 1.0, 0.0).astype(BF16)
    return jnp.concatenate([_dot_exact_rhs(x[:, s:s + 2 * w_in], sel, 2) for s in range(0, NH * w_in, 2 * w_in)],
                           axis=1)


def _group_sum(x):
    ones = jnp.where((_iota((128, 128), 0) >> 6) == (_iota((128, 128), 1) >> 6), 1.0, 0.0).astype(BF16)
    return jnp.concatenate([_dot_exact_rhs(x[:, s:s + 128], ones, 2) for s in range(0, x.shape[1], 128)], axis=1)


def _head_norm(o):
    ons, rs = [], []
    for h in range(NH):
        slab = o[:, h * 128:(h + 1) * 128]
        r = lax.rsqrt(jnp.mean(slab * slab, axis=-1, keepdims=True) + EPS)
        ons.append(slab * r)
        rs.append(jnp.broadcast_to(r, slab.shape))
    return jnp.concatenate(ons, axis=1), jnp.concatenate(rs, axis=1)


def _zspec(tb, width, blk, jmap):
    return pl.BlockSpec((tb, width), lambda i: (jmap(i), blk))


def _halo_specs(tb, nblk, t, blk, jmap):
    prev = pl.BlockSpec((8, CW), lambda i: (jnp.maximum(jmap(i) * (tb // 8) - 1, 0), blk))
    nxt = pl.BlockSpec((8, CW), lambda i: (jnp.minimum((jmap(i) + 1) * (tb // 8), t // 8 - 1), blk))
    return prev, nxt


def _gla_fwd_block(q_ref, k_ref, v_ref, lr_ref, w_ref, bias_ref, o_ref, sd_ref, st, b_scr, rev, tb):
    nb = tb // CH
    _, b, _, _, qt, kt = _gla_recompute(q_ref[...], k_ref[...], lr_ref[...], w_ref[...], bias_ref[...], rev, tb)
    v = v_ref[...]
    b_scr[...] = b
    yield
    maskw = _wide_mask(rev)
    order = list(reversed(range(nb))) if rev else list(range(nb))
    rows = [slice(c * CH, (c + 1) * CH) for c in range(nb)]
    state = st[...]
    for c in order:
        gdec = jnp.exp(b_scr[pl.ds(c * CH + (0 if rev else CH - 1), 1), :])
        sd_ref[c] = state
        a = jnp.where(maskw, _dot_nt(qt[rows[c]], _stack_heads(kt[rows[c]], 6)), 0.0)
        o_inter = _lanes_by_head(_dot_nt(_stack_heads(qt[rows[c]], 6), state))
        o_ref[pl.ds(c * CH, CH), :] = _dot(a, _stack_heads(v[rows[c]], 7)) + o_inter
        state = state * gdec + _state_compact(_dot_tn(v[rows[c]], kt[rows[c]] * gdec))
        yield
    st[...] = state
    yield


def _gla_fwd(z, waf_pad, b_af, wab_pad, b_ab, riders=()):
    t = z.shape[0]
    tb = min(TB, t)
    nblk, nb = t // tb, tb // CH
    jmaps = (lambda i: i, lambda i: nblk - 1 - i)

    def body(qf, kf, vf, lrf, qr, kr, vr, lrr, wf, bf, wr, br, of_ref, sdf_ref, or_ref, sdr_ref,
             st_f, st_r, b_f, b_r):
        @pl.when(pl.program_id(0) == 0)
        def _():
            st_f[...] = jnp.zeros_like(st_f)
            st_r[...] = jnp.zeros_like(st_r)

        for _ in zip(_gla_fwd_block(qf, kf, vf, lrf, wf, bf, of_ref, sdf_ref, st_f, b_f, False, tb),
                     _gla_fwd_block(qr, kr, vr, lrr, wr, br, or_ref, sdr_ref, st_r, b_r, True, tb)):
            pass

    full = lambda i: (0, 0)
    zspecs = [s for jm in jmaps for s in (_zspec(tb, GK, ZB_Q, jm), _zspec(tb, GK, ZB_K, jm),
                                         _zspec(tb, GV, ZB_V, jm), _zspec(tb, 128, ZB_LR, jm))]
    wspecs = [pl.BlockSpec((128, GK), full), pl.BlockSpec((1, GK), full)] * 2
    out_specs = [s for jm in jmaps for s in (pl.BlockSpec((tb, GV), lambda i, jm=jm: (jm(i), 0)),
                                             pl.BlockSpec((nb, 128, GK), lambda i, jm=jm: (jm(i), 0, 0)))]
    out_shape = [jax.ShapeDtypeStruct((t, GV), F32), jax.ShapeDtypeStruct((t // CH, 128, GK), F32)] * 2
    scratch = [pltpu.VMEM((128, GK), F32), pltpu.VMEM((128, GK), F32), pltpu.VMEM((tb, GK), F32),
               pltpu.VMEM((tb, GK), F32)]
    return _call(body, "gla_fwd", (nblk,), zspecs + wspecs, out_specs, out_shape, scratch,
                 [z] * 8 + [waf_pad, b_af, wab_pad, b_ab], riders)


def _gla_bwd_chunks(do_ref, sd_ref, dst, b_scr, db_scr, dq_ref, dk_ref, dv_ref, qt, kt, e, ei, v, rev, nb):
    maskw = _wide_mask(rev)
    for c in (range(nb) if rev else reversed(range(nb))):
        sl = slice(c * CH, (c + 1) * CH)
        grow = c * CH + (0 if rev else CH - 1)
        gdec = jnp.exp(b_scr[pl.ds(grow, 1), :])
        qt_c, kt_c, v_c, do_c = qt[sl], kt[sl], v[sl], do_ref[pl.ds(c * CH, CH), :]
        s_in, ds_out = sd_ref[c], dst[...]
        kbd, vbd = _stack_heads(kt_c, 6), _stack_heads(v_c, 7)
        a = jnp.where(maskw, _dot_nt(qt_c, kbd), 0.0)
        da = jnp.where(maskw, _dot_nt(do_c, vbd), 0.0)
        dv_ref[pl.ds(c * CH, CH), :] = (_fold_heads(_dot_tn(a, do_c), 7)
                                        + _lanes_by_head(_dot_nt(_stack_heads(kt_c * gdec, 6), ds_out)))
        dqt = _dot(da, kbd) + _fold_heads(_dot(_rows_by_head(do_c), s_in), 6)
        dkh = _fold_heads(_dot(_rows_by_head(v_c), ds_out), 6)
        da_do = jnp.concatenate([da.astype(BF16), do_c.astype(BF16)], axis=1)
        both = _dot_tn(da_do, qt_c)
        dkt = _fold_heads(both[:NH * CH], 6) + dkh * gdec
        dg = jnp.sum(ds_out * s_in, axis=0, keepdims=True) + jnp.sum(kt_c * dkh, axis=0, keepdims=True)
        db_scr[pl.ds(c * CH, CH), :] = dqt * qt_c - dkt * kt_c
        db_scr[pl.ds(grow, 1), :] += dg * gdec
        dq_ref[pl.ds(c * CH, CH), :] = dqt * e[sl] * 0.125
        dk_ref[pl.ds(c * CH, CH), :] = dkt * ei[sl]
        dst[...] = ds_out * gdec + _state_compact(both[NH * CH:])
        yield


def _gate_bwd(db, pre, lr, wpad, rev, tb):
    dla = _chunk_cumsum(db, not rev, 2)
    dpre = dla * (1.0 / 16.0) / (1.0 + jnp.exp(pre))
    return dpre, _dot_nt(dpre, wpad), _dot_tn(lr, dpre)


def _gla_bwd_first(z, do, sd, wpad, bias, riders=()):
    t = z.shape[0]
    tb = min(TB_BWD, t)
    nblk, nb = t // tb, tb // CH
    jmap = lambda i: nblk - 1 - i

    def body(q_ref, k_ref, v_ref, lr_ref, do_ref, sd_ref, w_ref, bias_ref,
             dq_ref, dk_ref, dv_ref, dlr_ref, dw_ref, dbias_ref, dst, b_scr, db_scr):
        @pl.when(pl.program_id(0) == 0)
        def _():
            dst[...] = jnp.zeros_like(dst)
            dw_ref[...] = jnp.zeros_like(dw_ref)
            dbias_ref[...] = jnp.zeros_like(dbias_ref)

        lr, wp = lr_ref[...], w_ref[...]
        pre, b, e, ei, qt, kt = _gla_recompute(q_ref[...], k_ref[...], lr, wp, bias_ref[...], False, tb)
        b_scr[...] = b
        for _ in _gla_bwd_chunks(do_ref, sd_ref, dst, b_scr, db_scr, dq_ref, dk_ref, dv_ref, qt, kt, e, ei, v_ref[...],
                                 False, nb):
            pass
        dpre, dlr, dw = _gate_bwd(db_scr[...], pre, lr, wp, False, tb)
        dlr_ref[...] = dlr
        dw_ref[...] += dw
        _acc_rows(dbias_ref, jnp.sum(dpre, axis=0, keepdims=True))

    full = lambda i: (0, 0)
    tokv = pl.BlockSpec((tb, GV), lambda i: (jmap(i), 0))
    tokk = pl.BlockSpec((tb, GK), lambda i: (jmap(i), 0))
    in_specs = [_zspec(tb, GK, ZB_Q, jmap), _zspec(tb, GK, ZB_K, jmap), _zspec(tb, GV, ZB_V, jmap),
                _zspec(tb, 128, ZB_LR, jmap), tokv, pl.BlockSpec((nb, 128, GK), lambda i: (jmap(i), 0, 0)),
                pl.BlockSpec((128, GK), full), pl.BlockSpec((1, GK), full)]
    out_specs = [tokk, tokk, tokv, pl.BlockSpec((tb, 128), lambda i: (jmap(i), 0)),
                 pl.BlockSpec((128, GK), full), pl.BlockSpec((8, GK), full)]
    out_shape = [jax.ShapeDtypeStruct((t, GK), F32), jax.ShapeDtypeStruct((t, GK), F32),
                 jax.ShapeDtypeStruct((t, GV), F32), jax.ShapeDtypeStruct((t, 128), F32),
                 jax.ShapeDtypeStruct((128, GK), F32), jax.ShapeDtypeStruct((8, GK), F32)]
    return _call(
        body, "gla_bwd_first", (nblk,), in_specs, out_specs, out_shape,
        [pltpu.VMEM((128, GK), F32), pltpu.VMEM((tb, GK), F32), pltpu.VMEM((tb, GK), F32)],
        (z, z, z, z, do, sd, wpad, bias), riders)


def _gla_bwd_second(z, do, sd, wpad, bias, dqa, dka, dva, dlra, dzg, dzcb, dconv, conv_w, riders=()):
    t = z.shape[0]
    tb = min(TB_BWD, t)
    nblk, nb = t // tb, tb // CH
    jmap = lambda i: i

    def body(q_ref, k_ref, v_ref, lr_ref, cc_ref, cu_ref, do_ref, sd_ref, w_ref, bias_ref, dqa_ref, dka_ref,
             dva_ref, dlra_ref, dzg_ref, dzcb_ref, dc_ref, dcp_ref, dcn_ref, cw_ref,
             dz_ref, dw_ref, dbias_ref, dst, b_scr, db_scr, dq_scr, dk_scr, dv_scr, sb_scr, dsk_scr):
        i = pl.program_id(0)

        @pl.when(i == 0)
        def _():
            dst[...] = jnp.zeros_like(dst)
            dw_ref[...] = jnp.zeros_like(dw_ref)
            dbias_ref[...] = jnp.zeros_like(dbias_ref)

        q_raw, k, v, lr, wp = q_ref[...], k_ref[...], v_ref[...], lr_ref[...], w_ref[...]
        pre, b, e, ei, qt, kt = _gla_recompute(q_raw, k, lr, wp, bias_ref[...], True, tb)
        b_scr[...] = b

        def token_local():
            dc = dc_ref[...]
            rows = _iota(dc.shape, 0)
            dprev = jnp.where(i == 0, 0.0, dcp_ref[pl.ds(7, 1), :])
            dnext = jnp.where(i == nblk - 1, 0.0, dcn_ref[pl.ds(0, 1), :])
            dc_m1 = jnp.where(rows == 0, dprev, pltpu.roll(dc, 1, 0))
            dc_p1 = jnp.where(rows == tb - 1, dnext, pltpu.roll(dc, tb - 1, 0))
            yield
            dh = cw_ref[pl.ds(0, 1), :] * dc_p1 + cw_ref[pl.ds(1, 1), :] * dc + cw_ref[pl.ds(2, 1), :] * dc_m1
            dz_ref[:, 0:512] = dzcb_ref[...]
            yield
            dz_ref[:, 512:1024] = (dh * cu_ref[...]).astype(BF16)
            dz_ref[:, 1024:1536] = (dh * cc_ref[...]).astype(BF16)
            dz_ref[:, 2560:3072] = dzg_ref[...]
            yield
            sb_scr[...] = _head_sum((q_raw * 0.125) * k, 64, 128)
            yield
            dsk_scr[...] = _head_sum(do_ref[...] * v, 128, 64)
            yield

        for _ in itertools.zip_longest(
                _gla_bwd_chunks(do_ref, sd_ref, dst, b_scr, db_scr, dq_scr, dk_scr, dv_scr, qt, kt, e, ei, v, True, nb),
                token_local()):
            pass
        dpre, dlr, dw = _gate_bwd(db_scr[...], pre, lr, wp, True, tb)
        dw_ref[...] += dw
        _acc_rows(dbias_ref, jnp.sum(dpre, axis=0, keepdims=True))
        dsk = dsk_scr[...]
        dz_ref[:, 1536:1792] = (dqa_ref[...] + dq_scr[...] - dsk * k * 0.125).astype(BF16)
        dz_ref[:, 1792:2048] = (dka_ref[...] + dk_scr[...] - dsk * (q_raw * 0.125)).astype(BF16)
        dz_ref[:, 2048:2560] = (dva_ref[...] + dv_scr[...] - sb_scr[...] * do_ref[...]).astype(BF16)
        dz_ref[:, 3072:3200] = (dlra_ref[...] + dlr).astype(BF16)

    full = lambda i: (0, 0)
    tokv = pl.BlockSpec((tb, GV), lambda i: (i, 0))
    tokk = pl.BlockSpec((tb, GK), lambda i: (i, 0))
    dcp = pl.BlockSpec((8, CW), lambda i: (jnp.maximum(i * (tb // 8) - 1, 0), 0))
    dcn = pl.BlockSpec((8, CW), lambda i: (jnp.minimum((i + 1) * (tb // 8), t // 8 - 1), 0))
    in_specs = [_zspec(tb, GK, ZB_Q, jmap), _zspec(tb, GK, ZB_K, jmap), _zspec(tb, GV, ZB_V, jmap),
                _zspec(tb, 128, ZB_LR, jmap), _zspec(tb, CW, ZB_CC, jmap), _zspec(tb, CW, ZB_CU, jmap), tokv,
                pl.BlockSpec((nb, 128, GK), lambda i: (i, 0, 0)), pl.BlockSpec((128, GK), full),
                pl.BlockSpec((1, GK), full), tokk, tokk, tokv, pl.BlockSpec((tb, 128), lambda i: (i, 0)), tokv, tokv,
                tokv, dcp, dcn, pl.BlockSpec((3, CW), full)]
    out_specs = [pl.BlockSpec((tb, ZC), lambda i: (i, 0)), pl.BlockSpec((128, GK), full), pl.BlockSpec((8, GK), full)]
    out_shape = [jax.ShapeDtypeStruct((t, ZC), BF16), jax.ShapeDtypeStruct((128, GK), F32),
                 jax.ShapeDtypeStruct((8, GK), F32)]
    return _call(
        body, "gla_bwd_second", (nblk,), in_specs, out_specs, out_shape,
        [pltpu.VMEM((128, GK), F32), pltpu.VMEM((tb, GK), F32), pltpu.VMEM((tb, GK), F32),
         pltpu.VMEM((tb, GK), F32), pltpu.VMEM((tb, GK), F32), pltpu.VMEM((tb, GV), F32),
         pltpu.VMEM((tb, GV), F32), pltpu.VMEM((tb, GK), F32)],
        (z, z, z, z, z, z, do, sd, wpad, bias, dqa, dka, dva, dlra, dzg, dzcb, dconv, dconv, dconv, conv_w), riders)


def _step(x, mem, target, shard, small_pack, vec, place):
    own, from_chips = {}, {}

    def pair_sums(names, g4, from_sibling):
        pbs = []
        for n, g, s in zip(names, g4, from_sibling):
            pb, own[n] = _rs_pair_sum(place, g, s, "pair_sum_" + n)
            pbs.append(pb)
        return pbs

    def by_dest(g, n):
        return g.reshape((4, 2) + shard[n].shape)

    w_in, small_all = _exchange(_gather_rider([shard["w_in"], small_pack]), "gather_w_in")
    w_in = jnp.pad(w_in.reshape(ZW, D), ((0, ZC - ZW), (0, 0)))
    small_all = small_all.reshape(NDEV, -1)
    p, off = {}, 0
    for n, (r, c) in SMALL_SHARDED.items():
        p[n] = small_all[:, off:off + r * c].reshape(NDEV, r, c).transpose(1, 0, 2).reshape(r, NDEV * c)
        off += r * c
    zeros_lr = jnp.zeros((128 - LR, GK), BF16)
    waf_pad = jnp.concatenate([p["w_af"].astype(BF16), zeros_lr], axis=0)
    wab_pad = jnp.concatenate([jnp.zeros((LR, GK), BF16), p["w_ab"].astype(BF16), zeros_lr[:128 - 2 * LR]], axis=0)
    gla_norm4 = jnp.tile(vec["gla_norm"], (1, NH))

    z, hb, w_out, w_xq, w_xo, w_xkv = _inproj(
        x, vec["mix_norm"], w_in, [_gather_rider([shard[n] for n in ("w_out", "w_xq", "w_xo", "w_xkv")])])
    w_out, w_xq, w_xo = [a.reshape(D, D) for a in (w_out, w_xq, w_xo)]
    o_f, sd_f, o_b, sd_b, w_up_t = _gla_fwd(
        z, waf_pad, vec["b_af"], wab_pad, vec["b_ab"], [_gather_rider([shard["w_up"]])])
    kv, memn = _kv_proj(mem, vec["mem_norm"], w_xkv)
    kb, vb = kv[:, :D].astype(BF16), kv[:, D:].astype(BF16)
    x1, x2, xn1, qb, attb, yb, o_pre, w_down = _attn_fwd(
        x, z, o_f, o_b, p["conv_w"], vec["conv_norm"], gla_norm4, w_out, vec["xa_norm"], w_xq, kb, vb, w_xo,
        riders=[_gather_rider([shard["w_down"]])])
    w_up_t, w_down = w_up_t.reshape(FF, D), w_down.reshape(FF, D)
    h1b, xn2, dx3, dx3b, loss8, dfinal = _mlp_fwd(x2, vec["mlp_norm"], w_up_t, w_down, vec["final_norm"], target)

    ab, dh1b, dx2, dx2b, dmlp = _mlp_bwd(dx3, dx3b, h1b, w_down, w_up_t, x2, vec["mlp_norm"])
    g_mlp = [by_dest(_matmul_tn(ab, dx3b, "dw_down")[0], "w_down"),
             by_dest(_matmul_tn(dh1b, xn2, "dw_up")[0], "w_up")]
    dx1, dx1b, dqb, dkv, dxa, do, dzg, dzcb, dconv, dcw, dcn, dgn = _attn_bwd(
        x1, dx2, dx2b, qb, kb, vb, w_xo, w_xq, w_out, vec["xa_norm"], z, o_pre, p["conv_w"], vec["conv_norm"],
        gla_norm4)
    dw_xo, *s_mlp = _matmul_tn(attb, dx2b, "dw_xo", riders=[_sibling_rider(g_mlp)])
    pb_mlp = pair_sums(("w_down", "w_up"), g_mlp, s_mlp)
    dw_xkv, dmemn = _kv_bwd(dkv, memn, mem, vec["mem_norm"], w_xkv)
    att_names = ("w_xo", "w_xq", "w_out", "w_xkv")
    g_att = [by_dest(g, n) for g, n in zip(
        (dw_xo, _matmul_tn(xn1, dqb, "dw_xq")[0], _matmul_tn(yb, dx1b, "dw_out")[0], dw_xkv), att_names)]
    res = _gla_bwd_first(z, do, sd_f, waf_pad, vec["b_af"], riders=[_chips_rider(pb_mlp), _sibling_rider(g_att)])
    dqa, dka, dva, dlra, dwaf, dbaf = res[:6]
    from_chips["w_down"], from_chips["w_up"] = res[6:8]
    pb_att = pair_sums(att_names, g_att, res[8:])
    dz, dwab, dbab, *c_att = _gla_bwd_second(z, do, sd_b, wab_pad, vec["b_ab"], dqa, dka, dva, dlra, dzg, dzcb, dconv,
                                             p["conv_w"], riders=[_chips_rider(pb_att)])
    from_chips.update(zip(att_names, c_att))
    g_in = [by_dest(_matmul_tn(dz, hb, "dw_in", rows=ZW)[0], "w_in")]
    pb_in = pair_sums(("w_in",), g_in, _exchange(_sibling_rider(g_in), "grads_to_sibling_w_in"))
    grad_x, dmix, from_chips["w_in"] = _inproj_bwd(dz, w_in, x, dx1, vec["mix_norm"], riders=[_chips_rider(pb_in)])

    small_acc = dict(mix_norm=dmix, conv_w=dcw, conv_norm=dcn, w_af=dwaf, b_af=dbaf, w_ab=dwab, b_ab=dbab,
                     gla_norm=dgn, xa_norm=dxa, mem_norm=dmemn, mlp_norm=dmlp, final_norm=dfinal)
    return loss8, grad_x, small_acc, own, from_chips


def _place():
    return lax.axis_index("x"), lax.axis_index("y"), lax.axis_index("c")


class _Rider:
    def __init__(self, arrays, out_shape, scratch, start, finish):
        self.arrays, self.out_shape, self.scratch, self.start, self.finish = arrays, out_shape, scratch, start, finish


def _gather_rider(blks):
    n = len(blks)

    def plan(in_refs, out_refs, sems):
        send_sems, recv_sems, local_sems = sems
        x, y, c = _place()
        me, sibling = (x, y, c), (x, y, 1 - c)
        chips = [(1 - x, y, c), (x, 1 - y, c), (1 - x, 1 - y, c)]

        def copy(a, k, block, to, own=False):
            px, py, pc = block
            dst = out_refs[a].at[4 * px + 2 * py + pc]
            return pltpu.make_async_remote_copy(
                src_ref=in_refs[a] if own else dst, dst_ref=dst, send_sem=send_sems.at[k, a],
                recv_sem=recv_sems.at[k, a], device_id=to, device_id_type=MESH)

        def local(a):
            return pltpu.make_async_copy(in_refs[a], out_refs[a].at[4 * x + 2 * y + c], local_sems.at[a])

        def own_sends(a):
            return [copy(a, 0, me, sibling, own=True)] + [copy(a, 1 + j, me, chip, own=True)
                                                          for j, chip in enumerate(chips)]

        return copy, local, own_sends, me, sibling, chips

    def start(in_refs, out_refs, sems):
        _, local, own_sends, _, _, _ = plan(in_refs, out_refs, sems)
        for a in range(n):
            local(a).start()
            for cp in own_sends(a):
                cp.start()

    def finish(in_refs, out_refs, sems):
        copy, local, own_sends, me, sibling, chips = plan(in_refs, out_refs, sems)
        for j, chip in enumerate(chips):
            for a in range(n):
                copy(a, 1 + j, chip, me).wait_recv()
                copy(a, 4 + j, chip, sibling).start()
        for a in range(n):
            copy(a, 0, sibling, me).wait_recv()
            for j, (px, py, pc) in enumerate(chips):
                copy(a, 4 + j, (px, py, 1 - pc), me).wait_recv()
            for cp in own_sends(a) + [copy(a, 4 + j, chip, sibling) for j, chip in enumerate(chips)]:
                cp.wait_send()
            local(a).wait()

    return _Rider(blks, [jax.ShapeDtypeStruct((NDEV,) + b.shape, b.dtype) for b in blks],
                  [pltpu.SemaphoreType.DMA((7, n)), pltpu.SemaphoreType.DMA((7, n)), pltpu.SemaphoreType.DMA((n,))],
                  start, finish)


def _sibling_rider(g4s):
    n = len(g4s)

    def copies(in_refs, out_refs, sems):
        send_sems, recv_sems = sems
        x, y, c = _place()
        return [pltpu.make_async_remote_copy(
            src_ref=in_refs[a].at[k, 1 - c], dst_ref=out_refs[a].at[k], send_sem=send_sems.at[k, a],
            recv_sem=recv_sems.at[k, a], device_id=(x, y, 1 - c), device_id_type=MESH)
            for a in range(n) for k in range(4)]

    def start(in_refs, out_refs, sems):
        for cp in copies(in_refs, out_refs, sems):
            cp.start()

    def finish(in_refs, out_refs, sems):
        for cp in copies(in_refs, out_refs, sems):
            cp.wait()

    return _Rider(g4s, [jax.ShapeDtypeStruct((4,) + g.shape[2:], g.dtype) for g in g4s],
                  [pltpu.SemaphoreType.DMA((4, n)), pltpu.SemaphoreType.DMA((4, n))], start, finish)


def _chips_rider(pbs):
    n = len(pbs)

    def copies(in_refs, out_refs, sems):
        send_sems, recv_sems = sems
        x, y, c = _place()
        peers = [(1 - x, y), (x, 1 - y), (1 - x, 1 - y)]
        return [pltpu.make_async_remote_copy(
            src_ref=in_refs[a].at[2 * px + py], dst_ref=out_refs[a].at[k], send_sem=send_sems.at[k, a],
            recv_sem=recv_sems.at[k, a], device_id=(px, py, c), device_id_type=MESH)
            for a in range(n) for k, (px, py) in enumerate(peers)]

    def start(in_refs, out_refs, sems):
        for cp in copies(in_refs, out_refs, sems):
            cp.start()

    def finish(in_refs, out_refs, sems):
        for cp in copies(in_refs, out_refs, sems):
            cp.wait()

    return _Rider(pbs, [jax.ShapeDtypeStruct((3,) + p.shape[1:], p.dtype) for p in pbs],
                  [pltpu.SemaphoreType.DMA((3, n)), pltpu.SemaphoreType.DMA((3, n))], start, finish)


def _exchange(rider, name):
    n_in, n_out = len(rider.arrays), len(rider.out_shape)

    def body(*refs):
        ins, outs, sems = refs[:n_in], refs[n_in:n_in + n_out], refs[n_in + n_out:]
        rider.start(ins, outs, sems)
        rider.finish(ins, outs, sems)

    hbm = pl.BlockSpec(memory_space=pltpu.HBM)
    return pl.pallas_call(body, name=name, out_shape=rider.out_shape, in_specs=[hbm] * n_in,
                          out_specs=[hbm] * n_out, scratch_shapes=rider.scratch)(*rider.arrays)


def _rs_pair_sum(place, g4, r1, name):
    rows, cols = g4.shape[2:]
    tr = min(rows, 512)

    def body(pl_ref, g_ref, r_ref, pb_ref, own_ref):
        s = g_ref[0, 0] + r_ref[0]
        pb_ref[0] = s.astype(BF16)

        @pl.when(pl.program_id(1) == pl_ref[0])
        def _():
            own_ref[...] = s

    grid_spec = pltpu.PrefetchScalarGridSpec(
        num_scalar_prefetch=1, grid=(rows // tr, 4),
        in_specs=[pl.BlockSpec((1, 1, tr, cols), lambda r, k, p: (k, p[1], r, 0)),
                  pl.BlockSpec((1, tr, cols), lambda r, k, p: (k, r, 0))],
        out_specs=[pl.BlockSpec((1, tr, cols), lambda r, k, p: (k, r, 0)),
                   pl.BlockSpec((tr, cols), lambda r, k, p: (r, 0))])
    return pl.pallas_call(
        body, name=name, grid_spec=grid_spec,
        out_shape=[jax.ShapeDtypeStruct((4, rows, cols), BF16), jax.ShapeDtypeStruct((rows, cols), F32)],
        compiler_params=_cparams(("arbitrary", "arbitrary")))(place, g4, r1)


PACK_ROWS = 32
VEC_ROW = {"mix_norm": 0, "conv_norm": 1, "b_af": 2, "b_ab": 3, "gla_norm": 4, "xa_norm": 5, "mem_norm": 6,
           "mlp_norm": 7, "final_norm": 8}
LOSS_ROW, MAT_ROW = 9, 16
MAT_LANE = {"w_af": 0, "w_ab": GK, "conv_w": 2 * GK}
MAT_SRC_ROW = {"w_af": 0, "w_ab": LR, "conv_w": 0}


SMALL_WIDTH = {"mix_norm": D, "conv_w": 64, "conv_norm": CW, "w_af": 32, "b_af": GK, "w_ab": 32, "b_ab": GK,
               "gla_norm": 128, "xa_norm": D, "mem_norm": D, "mlp_norm": D, "final_norm": D}


def _small_reduce(acc, loss8):
    names = list(SMALL)
    n = len(names)
    widths = SMALL_WIDTH

    def body(*refs):
        acc_refs = dict(zip(names, refs[:n]))
        loss_ref, tot = refs[n], refs[n + 1]
        pk, all_ref, send_sems, recv_sems, local_sem = refs[n + 2:]

        pk[...] = jnp.zeros_like(pk)
        for k, row in VEC_ROW.items():
            if k == "gla_norm":
                g = functools.reduce(lambda a, b: a + b, [acc_refs[k][pl.ds(0, 1), pl.ds(h * 128, 128)]
                                                          for h in range(NH)])
            else:
                g = acc_refs[k][pl.ds(0, 1), :]
            pk[pl.ds(row, 1), pl.ds(0, widths[k])] = g
        pk[pl.ds(LOSS_ROW, 1), pl.ds(0, 128)] = loss_ref[pl.ds(0, 1), :]
        for k, lane in MAT_LANE.items():
            rows, cols = (3, CW) if k == "conv_w" else (LR, GK)
            pk[pl.ds(MAT_ROW, rows), pl.ds(lane, cols)] = acc_refs[k][pl.ds(MAT_SRC_ROW[k], rows), :]

        x, y, c = _place()
        me, sibling = (x, y, c), (x, y, 1 - c)
        chips = [(1 - x, y, c), (x, 1 - y, c), (1 - x, 1 - y, c)]

        def copy(k, block, to, own=False):
            px, py, pc = block
            dst = all_ref.at[4 * px + 2 * py + pc]
            return pltpu.make_async_remote_copy(
                src_ref=pk if own else dst, dst_ref=dst, send_sem=send_sems.at[k], recv_sem=recv_sems.at[k],
                device_id=to, device_id_type=MESH)

        mine = pltpu.make_async_copy(pk, all_ref.at[4 * x + 2 * y + c], local_sem)
        mine.start()
        first = [copy(0, me, sibling, own=True)] + [copy(1 + j, me, chip, own=True) for j, chip in enumerate(chips)]
        for cp in first:
            cp.start()
        passed = [copy(4 + j, chip, sibling) for j, chip in enumerate(chips)]
        for j, chip in enumerate(chips):
            copy(1 + j, chip, me).wait_recv()
            passed[j].start()
        copy(0, sibling, me).wait_recv()
        for j, (px, py, pc) in enumerate(chips):
            copy(4 + j, (px, py, 1 - pc), me).wait_recv()
        for cp in first + passed:
            cp.wait_send()
        mine.wait()
        total = all_ref[0]
        for d in range(1, NDEV):
            total = total + all_ref[d]
        tot[...] = total

    return pl.pallas_call(
        body, name="small_reduce", out_shape=jax.ShapeDtypeStruct((PACK_ROWS, D), F32),
        scratch_shapes=[pltpu.VMEM((PACK_ROWS, D), F32), pltpu.VMEM((NDEV, PACK_ROWS, D), F32),
                        pltpu.SemaphoreType.DMA((7,)), pltpu.SemaphoreType.DMA((7,)), pltpu.SemaphoreType.DMA],
    )(*[acc[k] for k in names], loss8)


def _small_adamw(tot, ws, ms, vs):
    names = list(SMALL)
    n = len(names)
    widths = SMALL_WIDTH

    def body(*refs):
        tot = refs[0]
        w_refs, m_refs, v_refs = [dict(zip(names, refs[1 + q * n:1 + (q + 1) * n])) for q in range(3)]
        outs = refs[1 + 3 * n:1 + 7 * n]
        g_out, d_out, m_out, v_out = [dict(zip(names, outs[q * n:(q + 1) * n])) for q in range(4)]
        cut = refs[1 + 7 * n]
        x, y, c = _place()
        dev = 4 * x + 2 * y + c
        for k in names:
            if k in VEC_ROW:
                g = tot[pl.ds(VEC_ROW[k], 1), pl.ds(0, widths[k])]
            else:
                rows, cols = (3, CW) if k == "conv_w" else (LR, GK)
                wd = widths[k]
                sel = jnp.where(_iota((cols, wd), 0) == dev * wd + _iota((cols, wd), 1), 1.0, 0.0).astype(BF16)
                cut[:, pl.ds(0, wd)] = _dot_exact_rhs(tot[pl.ds(MAT_ROW, LR), pl.ds(MAT_LANE[k], cols)], sel, 3)
                g = cut[pl.ds(0, rows), pl.ds(0, wd)]
            g_out[k][...] = g
            d_out[k][...], m_out[k][...], v_out[k][...] = _adamw_math(w_refs[k][...], g, m_refs[k][...],
                                                                       v_refs[k][...])

    shapes = [jax.ShapeDtypeStruct(ws[k].shape, F32) for k in names]
    res = pl.pallas_call(
        body, name="small_adamw", out_shape=shapes * 4, scratch_shapes=[pltpu.VMEM((LR, 128), F32)],
    )(tot, *[ws[k] for k in names], *[ms[k] for k in names], *[vs[k] for k in names])
    return {k: tuple(res[q * n + i] for q in range(4)) for i, k in enumerate(names)}


def _adamw_math(w, g, m, v):
    m = ADAM_B1 * m + (1.0 - ADAM_B1) * g
    v = ADAM_B2 * v + (1.0 - ADAM_B2) * (g * g)
    m_hat = m / (1.0 - ADAM_B1 ** ADAM_STEP)
    v_hat = v / (1.0 - ADAM_B2 ** ADAM_STEP)
    delta = -ADAM_LR * (m_hat / (jnp.sqrt(v_hat) + ADAM_EPS) + ADAM_WD * w)
    return delta, m, v


def _adamw(w, m, v, own, r2, name):
    _, r, c = w.shape
    tr = 256 if r % 256 == 0 else r

    def body(w_ref, m_ref, v_ref, o_ref, r_ref, g_ref, d_ref, nm_ref, nv_ref):
        g = ((o_ref[...] + r_ref[0].astype(F32)) + r_ref[1].astype(F32)) + r_ref[2].astype(F32)
        g_ref[...] = g
        d_ref[...], nm_ref[...], nv_ref[...] = _adamw_math(w_ref[...], g, m_ref[...], v_ref[...])

    spec = pl.BlockSpec((None, tr, c), lambda i: (0, i, 0))
    return pl.pallas_call(
        body, name=name, grid=(r // tr,),
        in_specs=[spec, spec, spec, pl.BlockSpec((tr, c), lambda i: (i, 0)),
                  pl.BlockSpec((3, tr, c), lambda i: (0, i, 0))],
        out_specs=[spec] * 4, out_shape=[jax.ShapeDtypeStruct((1, r, c), F32)] * 4,
        compiler_params=_cparams(("arbitrary",)))(w, m, v, own, r2)


MATS = ("w_in", "w_out", "w_xq", "w_xo", "w_xkv", "w_up", "w_down")
SMALL = ("mix_norm", "conv_w", "conv_norm", "w_af", "b_af", "w_ab", "b_ab", "gla_norm", "xa_norm", "mem_norm",
         "mlp_norm", "final_norm")
WEIGHTS = ("mix_norm", "w_in", "conv_w", "conv_norm", "w_af", "b_af", "w_ab", "b_ab", "gla_norm", "w_out", "xa_norm",
           "mem_norm", "w_xq", "w_xkv", "w_xo", "mlp_norm", "w_up", "w_down", "final_norm")
SMALL_SHARDED = {"conv_w": (3, 64), "w_af": (LR, 32), "w_ab": (LR, 32)}
SMALL_PACK_ROWS = 16


def kernel(x, mem, mix_norm, w_in, conv_w, conv_norm, w_af, b_af, w_ab, b_ab, gla_norm, w_out, xa_norm, mem_norm, w_xq, w_xkv, w_xo, mlp_norm, w_up, w_down, final_norm, loss_target, m_mix_norm, m_w_in, m_conv_w, m_conv_norm, m_w_af, m_b_af, m_w_ab, m_b_ab, m_gla_norm, m_w_out, m_xa_norm, m_mem_norm, m_w_xq, m_w_xkv, m_w_xo, m_mlp_norm, m_w_up, m_w_down, m_final_norm, v_mix_norm, v_w_in, v_conv_w, v_conv_norm, v_w_af, v_b_af, v_w_ab, v_b_ab, v_gla_norm, v_w_out, v_xa_norm, v_mem_norm, v_w_xq, v_w_xkv, v_w_xo, v_mlp_norm, v_w_up, v_w_down, v_final_norm):
    w = dict(mix_norm=mix_norm, w_in=w_in, conv_w=conv_w, conv_norm=conv_norm, w_af=w_af, b_af=b_af, w_ab=w_ab,
             b_ab=b_ab, gla_norm=gla_norm, w_out=w_out, xa_norm=xa_norm, mem_norm=mem_norm, w_xq=w_xq, w_xkv=w_xkv,
             w_xo=w_xo, mlp_norm=mlp_norm, w_up=w_up, w_down=w_down, final_norm=final_norm)
    mom = dict(mix_norm=m_mix_norm, w_in=m_w_in, conv_w=m_conv_w, conv_norm=m_conv_norm, w_af=m_w_af, b_af=m_b_af,
               w_ab=m_w_ab, b_ab=m_b_ab, gla_norm=m_gla_norm, w_out=m_w_out, xa_norm=m_xa_norm, mem_norm=m_mem_norm,
               w_xq=m_w_xq, w_xkv=m_w_xkv, w_xo=m_w_xo, mlp_norm=m_mlp_norm, w_up=m_w_up, w_down=m_w_down,
               final_norm=m_final_norm)
    var = dict(mix_norm=v_mix_norm, w_in=v_w_in, conv_w=v_conv_w, conv_norm=v_conv_norm, w_af=v_w_af, b_af=v_b_af,
               w_ab=v_w_ab, b_ab=v_b_ab, gla_norm=v_gla_norm, w_out=v_w_out, xa_norm=v_xa_norm, mem_norm=v_mem_norm,
               w_xq=v_w_xq, w_xkv=v_w_xkv, w_xo=v_w_xo, mlp_norm=v_mlp_norm, w_up=v_w_up, w_down=v_w_down,
               final_norm=v_final_norm)
    xi, yi, ci = _place()
    two_d = lambda a: a.reshape(a.shape[-2:]) if a.ndim == 3 else a.reshape(1, a.shape[-1])

    small = jnp.concatenate([w[n].reshape(-1) for n in SMALL_SHARDED])
    small = jnp.pad(small, (0, SMALL_PACK_ROWS * 128 - small.shape[0])).reshape(SMALL_PACK_ROWS, 128)
    shard = {n: two_d(w[n]).astype(BF16) for n in MATS}
    for n in ("w_in", "w_up"):
        shard[n] = shard[n].T
    vec = {n: two_d(w[n]) for n in SMALL if n not in SMALL_SHARDED}
    place = jnp.stack([2 * xi + yi, ci]).astype(jnp.int32)
    loss8, grad_x, small_acc, own, from_chips = _step(x[0], mem[0], loss_target[0], shard, small, vec, place)

    tot = _small_reduce(small_acc, loss8)
    small_out = _small_adamw(tot, *[{n: two_d(d[n]) for n in SMALL} for d in (w, mom, var)])
    loss = tot[LOSS_ROW, 0]

    out_g, out_d, out_m, out_v = {}, {}, {}, {}
    own["w_up"], from_chips["w_up"] = own["w_up"].T, from_chips["w_up"].transpose(0, 2, 1)
    for n in MATS:
        if n == "w_in":
            res = _adamw(*[a.transpose(0, 2, 1) for a in (w[n], mom[n], var[n])], own[n], from_chips[n], "adamw_" + n)
            res = [a.transpose(0, 2, 1) for a in res]
        else:
            res = _adamw(w[n], mom[n], var[n], own[n], from_chips[n], "adamw_" + n)
        out_g[n], out_d[n], out_m[n], out_v[n] = res
    for n in SMALL:
        out_g[n], out_d[n], out_m[n], out_v[n] = [a.reshape(w[n].shape) for a in small_out[n]]

    return (loss, grad_x[None], *[out_g[n] for n in WEIGHTS], *[out_d[n] for n in WEIGHTS],
            *[out_m[n] for n in WEIGHTS], *[out_v[n] for n in WEIGHTS])
```

```python
import functools
import itertools

import jax
import jax.numpy as jnp
from jax import lax
from jax.experimental import pallas as pl
from jax.experimental.pallas import tpu as pltpu

F32 = jnp.float32
BF16 = jnp.bfloat16

D = 1024
CW = 512
GK = 256
GV = 512
NH = 4
CH = 64
LR = 16
NMEM = 256
XD = 256
FF = 4096
ZW = 3104
ZC = 3200
EPS = 1e-6
NDEV = 8

ZB_CB, ZB_CC, ZB_CU, ZB_V, ZB_G = 0, 1, 2, 4, 5
ZB_Q, ZB_K = 6, 7
ZB_LR = 24

TM = 512
TM_MLP = 256
TM_MLP_FWD = 512
TF = 512
TB = 512
TB_BWD = 512
TT = 2048
VMEM_LIMIT = 56 * 1024 * 1024

ADAM_LR, ADAM_B1, ADAM_B2, ADAM_EPS, ADAM_WD, ADAM_STEP = 0.001, 0.9, 0.999, 1e-08, 0.01, 10

XKV_SHARD = 2 * D // NDEV

MESH = pl.DeviceIdType.MESH


def _cparams(sem):
    return pltpu.CompilerParams(dimension_semantics=sem, vmem_limit_bytes=VMEM_LIMIT)


def _call(body, name, grid, in_specs, out_specs, out_shape, scratch, args, riders=()):
    n_in, n_out, n_scr = len(in_specs), len(out_specs), len(scratch)
    counts = [(len(r.arrays), len(r.out_shape), len(r.scratch)) for r in riders]

    def take(refs, pos, sizes):
        groups = []
        for size in sizes:
            groups.append(refs[pos:pos + size])
            pos += size
        return groups, pos

    def wrapped(*refs):
        ins, pos = refs[:n_in], n_in
        r_ins, pos = take(refs, pos, [c[0] for c in counts])
        outs, pos = refs[pos:pos + n_out], pos + n_out
        r_outs, pos = take(refs, pos, [c[1] for c in counts])
        scr, pos = refs[pos:pos + n_scr], pos + n_scr
        r_scr, pos = take(refs, pos, [c[2] for c in counts])
        ids = [pl.program_id(d) for d in range(len(grid))]
        first = functools.reduce(lambda a, b: a & b, [i == 0 for i in ids])
        last = functools.reduce(lambda a, b: a & b, [i == g - 1 for i, g in zip(ids, grid)])

        @pl.when(first)
        def _():
            for r, a, b, c in zip(riders, r_ins, r_outs, r_scr):
                r.start(a, b, c)

        body(*ins, *outs, *scr)

        if any(r.relay for r in riders):
            at = [max(g - 2, 0) for g in grid]

            @pl.when(functools.reduce(lambda a, b: a & b, [i == s for i, s in zip(ids, at)]))
            def _():
                for r, a, b, c in zip(riders, r_ins, r_outs, r_scr):
                    if r.relay:
                        r.relay(a, b, c)

        @pl.when(last)
        def _():
            for r, a, b, c in zip(riders, r_ins, r_outs, r_scr):
                r.finish(a, b, c)

    hbm = pl.BlockSpec(memory_space=pltpu.HBM)
    r_args = [a for r in riders for a in r.arrays]
    r_shapes = [s for r in riders for s in r.out_shape]
    return pl.pallas_call(
        wrapped if riders else body, name=name, grid=grid, in_specs=list(in_specs) + [hbm] * len(r_args),
        out_specs=list(out_specs) + [hbm] * len(r_shapes), out_shape=list(out_shape) + r_shapes,
        scratch_shapes=list(scratch) + [s for r in riders for s in r.scratch],
        compiler_params=_cparams(("arbitrary",) * len(grid)))(*args, *r_args)


def _dot(a, b):
    return jnp.dot(a.astype(BF16), b.astype(BF16), preferred_element_type=F32)


def _dot_nt(a, b):
    return lax.dot_general(a.astype(BF16), b.astype(BF16), (((1,), (1,)), ((), ())), preferred_element_type=F32)


def _dot_tn(a, b):
    return lax.dot_general(a.astype(BF16), b.astype(BF16), (((0,), (0,)), ((), ())), preferred_element_type=F32)


def _split(x, n):
    parts = []
    for _ in range(n):
        p = x.astype(BF16)
        parts.append(p)
        x = x - p.astype(F32)
    return parts


def _dot_exact_lhs(m, x, n):
    return functools.reduce(lambda a, b: a + b, [jnp.dot(m, p, preferred_element_type=F32) for p in _split(x, n)])


def _dot_exact_rhs(x, m, n):
    return functools.reduce(lambda a, b: a + b, [jnp.dot(p, m, preferred_element_type=F32) for p in _split(x, n)])


def _rms(x, g):
    r = lax.rsqrt(jnp.mean(x * x, axis=-1, keepdims=True) + EPS)
    return x * r * g, r


def _rms_bwd(x, r, g, dy):
    xr = x * r
    u = dy * g
    dx = r * (u - xr * jnp.mean(u * xr, axis=-1, keepdims=True))
    return dx, jnp.sum(dy * xr, axis=0, keepdims=True)


def _iota(shape, dim):
    return lax.broadcasted_iota(jnp.int32, shape, dim)


def _sigmoid(x):
    return 1.0 / (1.0 + jnp.exp(-x))


def _acc_rows(ref, row):
    ref[...] += jnp.broadcast_to(row, ref.shape)


def _inproj(x, g, w_t, riders=()):
    t = x.shape[0]
    tm = min(TM, t)

    def body(x_ref, g_ref, w_ref, z_ref, h_ref):
        h, _ = _rms(x_ref[...], g_ref[...])
        hb = h.astype(BF16)
        h_ref[...] = hb
        z_ref[...] = _dot_nt(hb, w_ref[...])

    return _call(
        body, "inproj", (t // tm,),
        [pl.BlockSpec((tm, D), lambda i: (i, 0)), pl.BlockSpec((1, D), lambda i: (0, 0)),
         pl.BlockSpec((ZC, D), lambda i: (0, 0))],
        [pl.BlockSpec((tm, ZC), lambda i: (i, 0)), pl.BlockSpec((tm, D), lambda i: (i, 0))],
        [jax.ShapeDtypeStruct((t, ZC), F32), jax.ShapeDtypeStruct((t, D), BF16)], [], (x, g, w_t), riders)


def _kv_proj(mem, g, w):
    def body(m_ref, g_ref, w_ref, kv_ref, mn_ref):
        mn, _ = _rms(m_ref[...], g_ref[...])
        mb = mn.astype(BF16)
        mn_ref[...] = mb
        for j in range(NDEV):
            kv_ref[:, j * XKV_SHARD:(j + 1) * XKV_SHARD] = jnp.dot(mb, w_ref[j], preferred_element_type=F32)

    return pl.pallas_call(
        body, name="kv_proj",
        out_shape=[jax.ShapeDtypeStruct((NMEM, 2 * D), F32), jax.ShapeDtypeStruct((NMEM, D), BF16)],
        compiler_params=pltpu.CompilerParams(vmem_limit_bytes=VMEM_LIMIT))(mem, g, w)


def _softmax_head(qb, kb):
    s = _dot_nt(qb, kb) * (1.0 / 16.0)
    e = jnp.exp(s - jnp.max(s, axis=-1, keepdims=True))
    return e / jnp.sum(e, axis=-1, keepdims=True)


def _attn_fwd(x, z, o_f, o_b, conv_w, conv_norm, gla_norm4, w_out, g, w_xq, kb, vb, w_xo):
    t = x.shape[0]
    tm = min(TM, t)
    nblk = t // tm
    jmap = lambda i: i

    def body(x_ref, zq_ref, zk_ref, zv_ref, zg_ref, cb_ref, cc_ref, cu_ref, ccp_ref, ccn_ref, cup_ref, cun_ref,
             of_ref, ob_ref, cw_ref, cn_ref, gn_ref, wo_ref, g_ref, wq_ref, k_ref, v_ref, wx_ref,
             x1_ref, x2_ref, xn_ref, q_ref, a_ref, y_ref, opre_ref):
        j = pl.program_id(0)
        zv = zv_ref[...]
        sb = _head_sum((zq_ref[...] * 0.125) * zk_ref[...], 64, 128)
        o_pre = of_ref[...] + ob_ref[...] - sb * zv
        opre_ref[...] = o_pre
        on, _ = _head_norm(o_pre)
        zg = zg_ref[...]
        y_ref[:, CW:] = (on * gn_ref[...] * (zg * _sigmoid(zg))).astype(BF16)
        cb = cb_ref[...]
        _, _, _, conv = _conv_parts(cb, cc_ref[...], cu_ref[...], ccp_ref[pl.ds(7, 1), :], cup_ref[pl.ds(7, 1), :],
                                    ccn_ref[pl.ds(0, 1), :], cun_ref[pl.ds(0, 1), :], cw_ref, j == 0,
                                    j == nblk - 1, tm)
        yc = cb * conv
        gm = _group_sum(yc * yc) * (1.0 / 64.0)
        y_ref[:, :CW] = (yc * lax.rsqrt(gm + EPS) * cn_ref[...]).astype(BF16)

        x1 = x_ref[...] + jnp.dot(y_ref[...], wo_ref[...], preferred_element_type=F32)
        x1_ref[...] = x1
        xn, _ = _rms(x1, g_ref[...])
        xb = xn.astype(BF16)
        xn_ref[...] = xb
        qb = jnp.dot(xb, wq_ref[...], preferred_element_type=F32).astype(BF16)
        q_ref[...] = qb
        for h in range(NH):
            hs = slice(h * XD, (h + 1) * XD)
            p = _softmax_head(qb[:, hs], k_ref[:, hs])
            a_ref[:, hs] = _dot(p, v_ref[:, hs]).astype(BF16)
        x2_ref[...] = x1 + jnp.dot(a_ref[...], wx_ref[...], preferred_element_type=F32)

    tok = lambda i: (i, 0)
    full = lambda i: (0, 0)
    once = pl.Buffered(1)
    tokd, tokv = pl.BlockSpec((tm, D), tok), pl.BlockSpec((tm, GV), tok)
    weight = pl.BlockSpec((D, D), full, pipeline_mode=once)
    ccp, ccn = _halo_specs(tm, nblk, t, ZB_CC, jmap)
    cup, cun = _halo_specs(tm, nblk, t, ZB_CU, jmap)
    in_specs = [tokd, _zspec(tm, GK, ZB_Q, jmap), _zspec(tm, GK, ZB_K, jmap), _zspec(tm, GV, ZB_V, jmap),
                _zspec(tm, GV, ZB_G, jmap), _zspec(tm, CW, ZB_CB, jmap), _zspec(tm, CW, ZB_CC, jmap),
                _zspec(tm, CW, ZB_CU, jmap), ccp, ccn, cup, cun, tokv, tokv,
                pl.BlockSpec((3, CW), full), pl.BlockSpec((1, CW), full), pl.BlockSpec((1, GV), full),
                weight, pl.BlockSpec((1, D), full), weight, pl.BlockSpec((NMEM, D), full),
                pl.BlockSpec((NMEM, D), full), weight]
    return pl.pallas_call(
        body, name="attn_fwd", grid=(nblk,), in_specs=in_specs, out_specs=[tokd] * 6 + [tokv],
        out_shape=[jax.ShapeDtypeStruct((t, D), F32), jax.ShapeDtypeStruct((t, D), F32),
                   jax.ShapeDtypeStruct((t, D), BF16), jax.ShapeDtypeStruct((t, D), BF16),
                   jax.ShapeDtypeStruct((t, D), BF16), jax.ShapeDtypeStruct((t, D), BF16),
                   jax.ShapeDtypeStruct((t, GV), F32)],
        compiler_params=_cparams(("arbitrary",)))(
            x, z, z, z, z, z, z, z, z, z, z, z, o_f, o_b, conv_w, conv_norm, gla_norm4, w_out, g, w_xq, kb, vb, w_xo)


def _mlp_fwd(x2, g, w_up_t, w_down, fg, target):
    t = x2.shape[0]
    tm = min(TM_MLP_FWD, t)

    def body(x_ref, g_ref, wu_ref, wd_ref, fg_ref, t_ref, h1_ref, xn_ref, dx_ref, dxb_ref, loss_ref, dfg_ref, ab):
        @pl.when(pl.program_id(0) == 0)
        def _():
            loss_ref[...] = jnp.zeros_like(loss_ref)
            dfg_ref[...] = jnp.zeros_like(dfg_ref)

        x = x_ref[...]
        xn, _ = _rms(x, g_ref[...])
        xnb = xn.astype(BF16)
        xn_ref[...] = xnb
        for q in range(FF // TF):
            cols = slice(q * TF, (q + 1) * TF)
            h1 = _dot_nt(xnb, wu_ref[cols, :])
            h1_ref[:, cols] = h1.astype(BF16)
            hr = jnp.maximum(h1, 0.0)
            ab[:, cols] = (hr * hr).astype(BF16)
        x3 = x + jnp.dot(ab[...], wd_ref[...], preferred_element_type=F32)
        y, r = _rms(x3, fg_ref[...])
        e = y - t_ref[...]
        row = jnp.mean(e * e, axis=-1, keepdims=True)
        _acc_rows(loss_ref, 0.5 * jnp.sum(row, axis=0, keepdims=True))
        dx, dfg = _rms_bwd(x3, r, fg_ref[...], e * (1.0 / D))
        dx_ref[...] = dx
        dxb_ref[...] = dx.astype(BF16)
        _acc_rows(dfg_ref, dfg)

    tok = lambda i: (i, 0)
    full = lambda i: (0, 0)
    once = pl.Buffered(1)
    return pl.pallas_call(
        body, name="mlp_fwd", grid=(t // tm,),
        in_specs=[pl.BlockSpec((tm, D), tok), pl.BlockSpec((1, D), full),
                  pl.BlockSpec((FF, D), full, pipeline_mode=once), pl.BlockSpec((FF, D), full, pipeline_mode=once),
                  pl.BlockSpec((1, D), full), pl.BlockSpec((tm, D), tok)],
        out_specs=[pl.BlockSpec((tm, FF), tok), pl.BlockSpec((tm, D), tok), pl.BlockSpec((tm, D), tok),
                   pl.BlockSpec((tm, D), tok), pl.BlockSpec((8, 128), full), pl.BlockSpec((8, D), full)],
        out_shape=[jax.ShapeDtypeStruct((t, FF), BF16), jax.ShapeDtypeStruct((t, D), BF16),
                   jax.ShapeDtypeStruct((t, D), F32), jax.ShapeDtypeStruct((t, D), BF16),
                   jax.ShapeDtypeStruct((8, 128), F32), jax.ShapeDtypeStruct((8, D), F32)],
        scratch_shapes=[pltpu.VMEM((tm, FF), BF16)],
        compiler_params=_cparams(("arbitrary",)))(x2, g, w_up_t, w_down, fg, target)


def _mlp_bwd(dx3, dx3b, h1b, w_down, w_up_t, x2, g):
    t = x2.shape[0]
    tm = min(TM_MLP, t)

    def body(dx_ref, dxb_ref, h1_ref, wd_ref, wu_ref, x_ref, g_ref, a_ref, dh_ref, dx2_ref, dx2b_ref, dg_ref):
        @pl.when(pl.program_id(0) == 0)
        def _():
            dg_ref[...] = jnp.zeros_like(dg_ref)

        for q in range(FF // TF):
            cols = slice(q * TF, (q + 1) * TF)
            hr = jnp.maximum(h1_ref[:, cols].astype(F32), 0.0)
            da = _dot_nt(dxb_ref[...], wd_ref[cols, :])
            a_ref[:, cols] = (hr * hr).astype(BF16)
            dh_ref[:, cols] = (da * 2.0 * hr).astype(BF16)
        dxn = jnp.dot(dh_ref[...], wu_ref[...], preferred_element_type=F32)
        x = x_ref[...]
        r = lax.rsqrt(jnp.mean(x * x, axis=-1, keepdims=True) + EPS)
        dx, dg = _rms_bwd(x, r, g_ref[...], dxn)
        dx2 = dx_ref[...] + dx
        dx2_ref[...] = dx2
        dx2b_ref[...] = dx2.astype(BF16)
        _acc_rows(dg_ref, dg)

    tok = lambda i: (i, 0)
    full = lambda i: (0, 0)
    once = pl.Buffered(1)
    return pl.pallas_call(
        body, name="mlp_bwd", grid=(t // tm,),
        in_specs=[pl.BlockSpec((tm, D), tok), pl.BlockSpec((tm, D), tok), pl.BlockSpec((tm, FF), tok),
                  pl.BlockSpec((FF, D), full, pipeline_mode=once), pl.BlockSpec((FF, D), full, pipeline_mode=once),
                  pl.BlockSpec((tm, D), tok), pl.BlockSpec((1, D), full)],
        out_specs=[pl.BlockSpec((tm, FF), tok), pl.BlockSpec((tm, FF), tok), pl.BlockSpec((tm, D), tok),
                   pl.BlockSpec((tm, D), tok), pl.BlockSpec((8, D), full)],
        out_shape=[jax.ShapeDtypeStruct((t, FF), BF16), jax.ShapeDtypeStruct((t, FF), BF16),
                   jax.ShapeDtypeStruct((t, D), F32), jax.ShapeDtypeStruct((t, D), BF16),
                   jax.ShapeDtypeStruct((8, D), F32)],
        compiler_params=_cparams(("arbitrary",)))(dx3, dx3b, h1b, w_down, w_up_t, x2, g)


def _attn_bwd(x1, dx2, dx2b, qb, kb, vb, w_xo, w_xq, w_out, g, riders=()):
    t = x1.shape[0]
    tm = min(TM, t)

    def body(x_ref, dx2_ref, dx2b_ref, q_ref, k_ref, v_ref, wx_ref, wq_ref, wo_ref, g_ref,
             dx1_ref, dx1b_ref, dy_ref, dq_ref, dkv_ref, dg_ref):
        @pl.when(pl.program_id(0) == 0)
        def _():
            dkv_ref[...] = jnp.zeros_like(dkv_ref)
            dg_ref[...] = jnp.zeros_like(dg_ref)

        datt = _dot_nt(dx2b_ref[...], wx_ref[...]).astype(BF16)
        for h in range(NH):
            hs = slice(h * XD, (h + 1) * XD)
            q_h, k_h, v_h, da_h = q_ref[:, hs], k_ref[:, hs], v_ref[:, hs], datt[:, hs]
            p = _softmax_head(q_h, k_h)
            dp = _dot_nt(da_h, v_h)
            ds = (p * (dp - jnp.sum(dp * p, axis=-1, keepdims=True)) * (1.0 / 16.0)).astype(BF16)
            dq_ref[:, hs] = _dot(ds, k_h).astype(BF16)
            dkv_ref[:, hs] += _dot_tn(ds, q_h)
            dkv_ref[:, D + h * XD:D + (h + 1) * XD] += _dot_tn(p, da_h)
        dxn = _dot_nt(dq_ref[...], wq_ref[...])
        x = x_ref[...]
        r = lax.rsqrt(jnp.mean(x * x, axis=-1, keepdims=True) + EPS)
        dx, dg = _rms_bwd(x, r, g_ref[...], dxn)
        dx1 = dx2_ref[...] + dx
        dx1_ref[...] = dx1
        dx1b = dx1.astype(BF16)
        dx1b_ref[...] = dx1b
        dy_ref[...] = _dot_nt(dx1b, wo_ref[...])
        _acc_rows(dg_ref, dg)

    tok = lambda i: (i, 0)
    full = lambda i: (0, 0)
    return _call(
        body, "attn_bwd", (t // tm,),
        [pl.BlockSpec((tm, D), tok), pl.BlockSpec((tm, D), tok), pl.BlockSpec((tm, D), tok),
         pl.BlockSpec((tm, D), tok), pl.BlockSpec((NMEM, D), full), pl.BlockSpec((NMEM, D), full),
         pl.BlockSpec((D, D), full), pl.BlockSpec((D, D), full), pl.BlockSpec((D, D), full),
         pl.BlockSpec((1, D), full)],
        [pl.BlockSpec((tm, D), tok), pl.BlockSpec((tm, D), tok), pl.BlockSpec((tm, D), tok),
         pl.BlockSpec((tm, D), tok), pl.BlockSpec((NMEM, 2 * D), full), pl.BlockSpec((8, D), full)],
        [jax.ShapeDtypeStruct((t, D), F32), jax.ShapeDtypeStruct((t, D), BF16),
         jax.ShapeDtypeStruct((t, D), F32), jax.ShapeDtypeStruct((t, D), BF16),
         jax.ShapeDtypeStruct((NMEM, 2 * D), F32), jax.ShapeDtypeStruct((8, D), F32)], [],
        (x1, dx2, dx2b, qb, kb, vb, w_xo, w_xq, w_out, g), riders)


def _kv_bwd(dkv, memn, mem, g, w):
    def body(dkv_ref, mn_ref, m_ref, g_ref, w_ref, dw_ref, dg_ref):
        dkvb = dkv_ref[...].astype(BF16)
        dmn = jnp.zeros((NMEM, D), F32)
        for j in range(NDEV):
            cols = slice(j * XKV_SHARD, (j + 1) * XKV_SHARD)
            dw_ref[j] = _dot_tn(mn_ref[...], dkvb[:, cols])
            dmn += _dot_nt(dkvb[:, cols], w_ref[j])
        m = m_ref[...]
        r = lax.rsqrt(jnp.mean(m * m, axis=-1, keepdims=True) + EPS)
        dg_ref[...] = jnp.broadcast_to(jnp.sum(dmn * m * r, axis=0, keepdims=True), dg_ref.shape)

    return pl.pallas_call(
        body, name="kv_bwd",
        out_shape=[jax.ShapeDtypeStruct((NDEV, D, XKV_SHARD), F32), jax.ShapeDtypeStruct((8, D), F32)],
        compiler_params=pltpu.CompilerParams(vmem_limit_bytes=VMEM_LIMIT))(dkv, memn, mem, g, w)


def _inproj_bwd(dz, w_t, x, dx1, g, riders=()):
    t = x.shape[0]
    tm = min(TM, t)

    def body(dz_ref, w_ref, x_ref, dx1_ref, g_ref, gx_ref, dg_ref):
        @pl.when(pl.program_id(0) == 0)
        def _():
            dg_ref[...] = jnp.zeros_like(dg_ref)

        dh = jnp.dot(dz_ref[...], w_ref[...], preferred_element_type=F32)
        x = x_ref[...]
        r = lax.rsqrt(jnp.mean(x * x, axis=-1, keepdims=True) + EPS)
        dx, dg = _rms_bwd(x, r, g_ref[...], dh)
        gx_ref[...] = dx1_ref[...] + dx
        _acc_rows(dg_ref, dg)

    tok = lambda i: (i, 0)
    full = lambda i: (0, 0)
    return _call(
        body, "inproj_bwd", (t // tm,),
        [pl.BlockSpec((tm, ZC), tok), pl.BlockSpec((ZC, D), full), pl.BlockSpec((tm, D), tok),
         pl.BlockSpec((tm, D), tok), pl.BlockSpec((1, D), full)],
        [pl.BlockSpec((tm, D), tok), pl.BlockSpec((8, D), full)],
        [jax.ShapeDtypeStruct((t, D), F32), jax.ShapeDtypeStruct((8, D), F32)], [], (dz, w_t, x, dx1, g), riders)


def _matmul_tn(a, b, name, rows=None, riders=()):
    t, k = a.shape
    n = b.shape[1]
    tk, tn = [1024 if size % 1024 == 0 else 640 for size in (k, n)]
    tt = min(TT, t)
    rows = rows or k

    def body(a_ref, b_ref, o_ref):
        @pl.when(pl.program_id(2) == 0)
        def _():
            o_ref[...] = jnp.zeros_like(o_ref)

        o_ref[...] += _dot_tn(a_ref[...], b_ref[...])

    return _call(
        body, name, (k // tk, n // tn, t // tt),
        [pl.BlockSpec((tt, tk), lambda i, j, s: (s, i)), pl.BlockSpec((tt, tn), lambda i, j, s: (s, j))],
        [pl.BlockSpec((tk, tn), lambda i, j, s: (i, j))], [jax.ShapeDtypeStruct((rows, n), F32)], [], (a, b), riders)


def _lane_head(shape, dim, shift):
    return _iota(shape, dim) >> shift


CUM_ROWS = 128


def _chunk_cumsum(x, upper, n):
    r, c = _iota((CUM_ROWS, CUM_ROWS), 0), _iota((CUM_ROWS, CUM_ROWS), 1)
    tri = (c >= r) if upper else (c <= r)
    cum = jnp.where(((r >> 6) == (c >> 6)) & tri, 1.0, 0.0).astype(BF16)
    return jnp.concatenate([_dot_exact_lhs(cum, x[g:g + CUM_ROWS], n) for g in range(0, x.shape[0], CUM_ROWS)],
                           axis=0)


def _gla_recompute(q_raw, k, lr, wpad, bias, rev, tb):
    pre = _dot(lr, wpad) + bias
    la = (jnp.minimum(pre, 0.0) - jnp.log(1.0 + jnp.exp(-jnp.abs(pre)))) * (1.0 / 16.0)
    b = _chunk_cumsum(la, rev, 3)
    e, ei = jnp.exp(b), jnp.exp(-b)
    qt = (q_raw * 0.125) * e
    kt = k * ei
    return pre, b, e, ei, qt, kt


def _stack_heads(x, shift):
    head = _lane_head(x.shape, 1, shift)
    return jnp.concatenate([jnp.where(head == h, x, 0.0) for h in range(NH)], axis=0).astype(BF16)


def _fold_heads(x, shift):
    head = _lane_head((CH, x.shape[1]), 1, shift)
    return functools.reduce(lambda a, b: a + b,
                            [jnp.where(head == h, x[h * CH:(h + 1) * CH], 0.0) for h in range(NH)])


def _wide_mask(rev):
    r, s = _iota((CH, NH * CH), 0), _iota((CH, NH * CH), 1) & (CH - 1)
    return (s >= r) if rev else (s <= r)


def _rows_by_head(x):
    w = x.shape[1] // NH
    return jnp.concatenate([x[:, h * w:(h + 1) * w] for h in range(NH)], axis=0)


def _lanes_by_head(x):
    return jnp.concatenate([x[h * CH:(h + 1) * CH] for h in range(NH)], axis=1)


def _state_compact(xt):
    head = _lane_head((128, GK), 1, 6)
    return functools.reduce(lambda a, b: a + b,
                            [jnp.where(head == h, xt[h * 128:(h + 1) * 128], 0.0) for h in range(NH)])


def _conv_parts(cb, cc, cu, ccp, cup, ccn, cun, cw_ref, first, last, tb):
    h = cc * cu
    hp = jnp.where(first, 0.0, ccp * cup)
    hn = jnp.where(last, 0.0, ccn * cun)
    rows = _iota(h.shape, 0)
    h_m1 = jnp.where(rows == 0, hp, pltpu.roll(h, 1, 0))
    h_p1 = jnp.where(rows == tb - 1, hn, pltpu.roll(h, tb - 1, 0))
    conv = cw_ref[pl.ds(0, 1), :] * h_m1 + cw_ref[pl.ds(1, 1), :] * h + cw_ref[pl.ds(2, 1), :] * h_p1
    return h, h_m1, h_p1, conv


def _head_sum(x, w_in, w_out):
    shape, sh_in, sh_out = (2 * w_in, 2 * w_out), w_in.bit_length() - 1, w_out.bit_length() - 1
    sel = jnp.where((_iota(shape, 0) >> sh_in) == (_iota(shape, 1) >> sh_out), 1.0, 0.0).astype(BF16)
    return jnp.concatenate([_dot_exact_rhs(x[:, s:s + 2 * w_in], sel, 2) for s in range(0, NH * w_in, 2 * w_in)],
                           axis=1)


def _group_sum(x):
    ones = jnp.where((_iota((128, 128), 0) >> 6) == (_iota((128, 128), 1) >> 6), 1.0, 0.0).astype(BF16)
    return jnp.concatenate([_dot_exact_rhs(x[:, s:s + 128], ones, 2) for s in range(0, x.shape[1], 128)], axis=1)


def _head_norm(o):
    ons, rs = [], []
    for h in range(NH):
        slab = o[:, h * 128:(h + 1) * 128]
        r = lax.rsqrt(jnp.mean(slab * slab, axis=-1, keepdims=True) + EPS)
        ons.append(slab * r)
        rs.append(jnp.broadcast_to(r, slab.shape))
    return jnp.concatenate(ons, axis=1), jnp.concatenate(rs, axis=1)


def _zspec(tb, width, blk, jmap):
    return pl.BlockSpec((tb, width), lambda i: (jmap(i), blk))


def _halo_specs(tb, nblk, t, blk, jmap):
    prev = pl.BlockSpec((8, CW), lambda i: (jnp.maximum(jmap(i) * (tb // 8) - 1, 0), blk))
    nxt = pl.BlockSpec((8, CW), lambda i: (jnp.minimum((jmap(i) + 1) * (tb // 8), t // 8 - 1), blk))
    return prev, nxt


def _gla_fwd_block(q_ref, k_ref, v_ref, lr_ref, w_ref, bias_ref, o_ref, sd_ref, st, b_scr, rev, tb):
    nb = tb // CH
    _, b, _, _, qt, kt = _gla_recompute(q_ref[...], k_ref[...], lr_ref[...], w_ref[...], bias_ref[...], rev, tb)
    v = v_ref[...]
    b_scr[...] = b
    yield
    maskw = _wide_mask(rev)
    order = list(reversed(range(nb))) if rev else list(range(nb))
    rows = [slice(c * CH, (c + 1) * CH) for c in range(nb)]
    state = st[...]
    for c in order:
        gdec = jnp.exp(b_scr[pl.ds(c * CH + (0 if rev else CH - 1), 1), :])
        sd_ref[c] = state
        a = jnp.where(maskw, _dot_nt(qt[rows[c]], _stack_heads(kt[rows[c]], 6)), 0.0)
        o_inter = _lanes_by_head(_dot_nt(_stack_heads(qt[rows[c]], 6), state))
        o_ref[pl.ds(c * CH, CH), :] = _dot(a, _stack_heads(v[rows[c]], 7)) + o_inter
        state = state * gdec + _state_compact(_dot_tn(v[rows[c]], kt[rows[c]] * gdec))
        yield
    st[...] = state
    yield


def _gla_fwd(z, waf_pad, b_af, wab_pad, b_ab, riders=()):
    t = z.shape[0]
    tb = min(TB, t)
    nblk, nb = t // tb, tb // CH
    jmaps = (lambda i: i, lambda i: nblk - 1 - i)

    def body(qf, kf, vf, lrf, qr, kr, vr, lrr, wf, bf, wr, br, of_ref, sdf_ref, or_ref, sdr_ref,
             st_f, st_r, b_f, b_r):
        @pl.when(pl.program_id(0) == 0)
        def _():
            st_f[...] = jnp.zeros_like(st_f)
            st_r[...] = jnp.zeros_like(st_r)

        for _ in zip(_gla_fwd_block(qf, kf, vf, lrf, wf, bf, of_ref, sdf_ref, st_f, b_f, False, tb),
                     _gla_fwd_block(qr, kr, vr, lrr, wr, br, or_ref, sdr_ref, st_r, b_r, True, tb)):
            pass

    full = lambda i: (0, 0)
    zspecs = [s for jm in jmaps for s in (_zspec(tb, GK, ZB_Q, jm), _zspec(tb, GK, ZB_K, jm),
                                         _zspec(tb, GV, ZB_V, jm), _zspec(tb, 128, ZB_LR, jm))]
    wspecs = [pl.BlockSpec((128, GK), full), pl.BlockSpec((1, GK), full)] * 2
    out_specs = [s for jm in jmaps for s in (pl.BlockSpec((tb, GV), lambda i, jm=jm: (jm(i), 0)),
                                             pl.BlockSpec((nb, 128, GK), lambda i, jm=jm: (jm(i), 0, 0)))]
    out_shape = [jax.ShapeDtypeStruct((t, GV), F32), jax.ShapeDtypeStruct((t // CH, 128, GK), F32)] * 2
    scratch = [pltpu.VMEM((128, GK), F32), pltpu.VMEM((128, GK), F32), pltpu.VMEM((tb, GK), F32),
               pltpu.VMEM((tb, GK), F32)]
    return _call(body, "gla_fwd", (nblk,), zspecs + wspecs, out_specs, out_shape, scratch,
                 [z] * 8 + [waf_pad, b_af, wab_pad, b_ab], riders)


def _gla_bwd_chunks(do_ref, sd_ref, dst, b_scr, db_scr, dq_ref, dk_ref, dv_ref, qt, kt, e, ei, v, rev, nb):
    maskw = _wide_mask(rev)
    for c in (range(nb) if rev else reversed(range(nb))):
        sl = slice(c * CH, (c + 1) * CH)
        grow = c * CH + (0 if rev else CH - 1)
        gdec = jnp.exp(b_scr[pl.ds(grow, 1), :])
        qt_c, kt_c, v_c, do_c = qt[sl], kt[sl], v[sl], do_ref[pl.ds(c * CH, CH), :]
        s_in, ds_out = sd_ref[c], dst[...]
        kbd, vbd = _stack_heads(kt_c, 6), _stack_heads(v_c, 7)
        a = jnp.where(maskw, _dot_nt(qt_c, kbd), 0.0)
        da = jnp.where(maskw, _dot_nt(do_c, vbd), 0.0)
        dv_ref[pl.ds(c * CH, CH), :] = (_fold_heads(_dot_tn(a, do_c), 7)
                                        + _lanes_by_head(_dot_nt(_stack_heads(kt_c * gdec, 6), ds_out)))
        dqt = _dot(da, kbd) + _fold_heads(_dot(_rows_by_head(do_c), s_in), 6)
        dkh = _fold_heads(_dot(_rows_by_head(v_c), ds_out), 6)
        da_do = jnp.concatenate([da.astype(BF16), do_c.astype(BF16)], axis=1)
        both = _dot_tn(da_do, qt_c)
        dkt = _fold_heads(both[:NH * CH], 6) + dkh * gdec
        dg = jnp.sum(ds_out * s_in, axis=0, keepdims=True) + jnp.sum(kt_c * dkh, axis=0, keepdims=True)
        db_scr[pl.ds(c * CH, CH), :] = dqt * qt_c - dkt * kt_c
        db_scr[pl.ds(grow, 1), :] += dg * gdec
        dq_ref[pl.ds(c * CH, CH), :] = dqt * e[sl] * 0.125
        dk_ref[pl.ds(c * CH, CH), :] = dkt * ei[sl]
        dst[...] = ds_out * gdec + _state_compact(both[NH * CH:])
        yield


def _gate_bwd(db, pre, lr, wpad, rev, tb):
    dla = _chunk_cumsum(db, not rev, 2)
    dpre = dla * (1.0 / 16.0) / (1.0 + jnp.exp(pre))
    return dpre, _dot_nt(dpre, wpad), _dot_tn(lr, dpre)


def _gla_bwd_first(z, dy, o_pre, sd, wpad, bias, conv_w, conv_norm, gla_norm4, riders=()):
    t = z.shape[0]
    tb = min(TB_BWD, t)
    nblk, nb = t // tb, tb // CH
    jmap = lambda i: nblk - 1 - i

    def body(q_ref, k_ref, v_ref, lr_ref, g_ref, cb_ref, cc_ref, cu_ref, ccp_ref, ccn_ref, cup_ref, cun_ref,
             dy_ref, opre_ref, sd_ref, w_ref, bias_ref, cw_ref, cn_ref, gn_ref,
             do_ref, dq_ref, dk_ref, dv_ref, dlr_ref, dzg_ref, dzcb_ref, dconv_ref,
             dw_ref, dbias_ref, dcw_ref, dcn_ref, dgn_ref, dst, b_scr, db_scr):
        i = pl.program_id(0)
        j = jmap(i)

        @pl.when(i == 0)
        def _():
            dst[...] = jnp.zeros_like(dst)
            for ref in (dw_ref, dbias_ref, dcw_ref, dcn_ref, dgn_ref):
                ref[...] = jnp.zeros_like(ref)

        dyg = dy_ref[:, CW:]
        g = g_ref[...]
        sig = _sigmoid(g)
        on, rr = _head_norm(opre_ref[...])
        gn = gn_ref[...]
        dzg_ref[...] = (dyg * on * gn * (sig * (1.0 + g * (1.0 - sig)))).astype(BF16)
        don = dyg * (g * sig)
        _acc_rows(dgn_ref, jnp.sum(don * on, axis=0, keepdims=True))
        u = don * gn
        uo = u * on
        mean_uo = jnp.concatenate(
            [jnp.broadcast_to(jnp.mean(uo[:, h * 128:(h + 1) * 128], axis=-1, keepdims=True), (tb, 128))
             for h in range(NH)], axis=1)
        do_ref[...] = rr * (u - on * mean_uo)

        def conv_branch():
            cb = cb_ref[...]
            h, h_m1, h_p1, conv = _conv_parts(cb, cc_ref[...], cu_ref[...], ccp_ref[pl.ds(7, 1), :],
                                              cup_ref[pl.ds(7, 1), :], ccn_ref[pl.ds(0, 1), :],
                                              cun_ref[pl.ds(0, 1), :], cw_ref, j == 0, j == nblk - 1, tb)
            yc = cb * conv
            yield
            rc = lax.rsqrt(_group_sum(yc * yc) * (1.0 / 64.0) + EPS)
            ycr = yc * rc
            yield
            dyn = dy_ref[:, :CW]
            _acc_rows(dcn_ref, jnp.sum(dyn * ycr, axis=0, keepdims=True))
            uc = dyn * cn_ref[...]
            yield
            dyc = rc * (uc - ycr * (_group_sum(uc * ycr) * (1.0 / 64.0)))
            dzcb_ref[...] = (dyc * conv).astype(BF16)
            yield
            dconv = dyc * cb
            dconv_ref[...] = dconv
            yield
            dcw_ref[pl.ds(0, 1), :] += jnp.sum(dconv * h_m1, axis=0, keepdims=True)
            dcw_ref[pl.ds(1, 1), :] += jnp.sum(dconv * h, axis=0, keepdims=True)
            dcw_ref[pl.ds(2, 1), :] += jnp.sum(dconv * h_p1, axis=0, keepdims=True)
            yield

        lr, wp = lr_ref[...], w_ref[...]
        pre, b, e, ei, qt, kt = _gla_recompute(q_ref[...], k_ref[...], lr, wp, bias_ref[...], False, tb)
        b_scr[...] = b
        for _ in itertools.zip_longest(
                _gla_bwd_chunks(do_ref, sd_ref, dst, b_scr, db_scr, dq_ref, dk_ref, dv_ref, qt, kt, e, ei, v_ref[...],
                                False, nb), conv_branch()):
            pass
        dpre, dlr, dw = _gate_bwd(db_scr[...], pre, lr, wp, False, tb)
        dlr_ref[...] = dlr
        dw_ref[...] += dw
        _acc_rows(dbias_ref, jnp.sum(dpre, axis=0, keepdims=True))

    full = lambda i: (0, 0)
    tokv = pl.BlockSpec((tb, GV), lambda i: (jmap(i), 0))
    tokk = pl.BlockSpec((tb, GK), lambda i: (jmap(i), 0))
    ccp, ccn = _halo_specs(tb, nblk, t, ZB_CC, jmap)
    cup, cun = _halo_specs(tb, nblk, t, ZB_CU, jmap)
    in_specs = [_zspec(tb, GK, ZB_Q, jmap), _zspec(tb, GK, ZB_K, jmap), _zspec(tb, GV, ZB_V, jmap),
                _zspec(tb, 128, ZB_LR, jmap), _zspec(tb, GV, ZB_G, jmap), _zspec(tb, CW, ZB_CB, jmap),
                _zspec(tb, CW, ZB_CC, jmap), _zspec(tb, CW, ZB_CU, jmap), ccp, ccn, cup, cun,
                pl.BlockSpec((tb, D), lambda i: (jmap(i), 0)), tokv,
                pl.BlockSpec((nb, 128, GK), lambda i: (jmap(i), 0, 0)), pl.BlockSpec((128, GK), full),
                pl.BlockSpec((1, GK), full), pl.BlockSpec((3, CW), full), pl.BlockSpec((1, CW), full),
                pl.BlockSpec((1, GV), full)]
    out_specs = [tokv, tokk, tokk, tokv, pl.BlockSpec((tb, 128), lambda i: (jmap(i), 0)), tokv, tokv, tokv,
                 pl.BlockSpec((128, GK), full), pl.BlockSpec((8, GK), full), pl.BlockSpec((8, CW), full),
                 pl.BlockSpec((8, CW), full), pl.BlockSpec((8, GV), full)]
    out_shape = [jax.ShapeDtypeStruct((t, GV), F32), jax.ShapeDtypeStruct((t, GK), F32),
                 jax.ShapeDtypeStruct((t, GK), F32), jax.ShapeDtypeStruct((t, GV), F32),
                 jax.ShapeDtypeStruct((t, 128), F32), jax.ShapeDtypeStruct((t, GV), BF16),
                 jax.ShapeDtypeStruct((t, CW), BF16), jax.ShapeDtypeStruct((t, CW), F32),
                 jax.ShapeDtypeStruct((128, GK), F32), jax.ShapeDtypeStruct((8, GK), F32),
                 jax.ShapeDtypeStruct((8, CW), F32), jax.ShapeDtypeStruct((8, CW), F32),
                 jax.ShapeDtypeStruct((8, GV), F32)]
    return _call(
        body, "gla_bwd_first", (nblk,), in_specs, out_specs, out_shape,
        [pltpu.VMEM((128, GK), F32), pltpu.VMEM((tb, GK), F32), pltpu.VMEM((tb, GK), F32)],
        (z, z, z, z, z, z, z, z, z, z, z, z, dy, o_pre, sd, wpad, bias, conv_w, conv_norm, gla_norm4), riders)


def _gla_bwd_second(z, do, sd, wpad, bias, dqa, dka, dva, dlra, dzg, dzcb, dconv, conv_w, riders=()):
    t = z.shape[0]
    tb = min(TB_BWD, t)
    nblk, nb = t // tb, tb // CH
    jmap = lambda i: i

    def body(q_ref, k_ref, v_ref, lr_ref, cc_ref, cu_ref, do_ref, sd_ref, w_ref, bias_ref, dqa_ref, dka_ref,
             dva_ref, dlra_ref, dzg_ref, dzcb_ref, dc_ref, dcp_ref, dcn_ref, cw_ref,
             dz_ref, dw_ref, dbias_ref, dst, b_scr, db_scr, dq_scr, dk_scr, dv_scr, sb_scr, dsk_scr):
        i = pl.program_id(0)

        @pl.when(i == 0)
        def _():
            dst[...] = jnp.zeros_like(dst)
            dw_ref[...] = jnp.zeros_like(dw_ref)
            dbias_ref[...] = jnp.zeros_like(dbias_ref)

        q_raw, k, v, lr, wp = q_ref[...], k_ref[...], v_ref[...], lr_ref[...], w_ref[...]
        pre, b, e, ei, qt, kt = _gla_recompute(q_raw, k, lr, wp, bias_ref[...], True, tb)
        b_scr[...] = b

        def token_local():
            dc = dc_ref[...]
            rows = _iota(dc.shape, 0)
            dprev = jnp.where(i == 0, 0.0, dcp_ref[pl.ds(7, 1), :])
            dnext = jnp.where(i == nblk - 1, 0.0, dcn_ref[pl.ds(0, 1), :])
            dc_m1 = jnp.where(rows == 0, dprev, pltpu.roll(dc, 1, 0))
            dc_p1 = jnp.where(rows == tb - 1, dnext, pltpu.roll(dc, tb - 1, 0))
            yield
            dh = cw_ref[pl.ds(0, 1), :] * dc_p1 + cw_ref[pl.ds(1, 1), :] * dc + cw_ref[pl.ds(2, 1), :] * dc_m1
            dz_ref[:, 0:512] = dzcb_ref[...]
            yield
            dz_ref[:, 512:1024] = (dh * cu_ref[...]).astype(BF16)
            dz_ref[:, 1024:1536] = (dh * cc_ref[...]).astype(BF16)
            dz_ref[:, 2560:3072] = dzg_ref[...]
            yield
            sb_scr[...] = _head_sum((q_raw * 0.125) * k, 64, 128)
            yield
            dsk_scr[...] = _head_sum(do_ref[...] * v, 128, 64)
            yield

        for _ in itertools.zip_longest(
                _gla_bwd_chunks(do_ref, sd_ref, dst, b_scr, db_scr, dq_scr, dk_scr, dv_scr, qt, kt, e, ei, v, True, nb),
                token_local()):
            pass
        dpre, dlr, dw = _gate_bwd(db_scr[...], pre, lr, wp, True, tb)
        dw_ref[...] += dw
        _acc_rows(dbias_ref, jnp.sum(dpre, axis=0, keepdims=True))
        dsk = dsk_scr[...]
        dz_ref[:, 1536:1792] = (dqa_ref[...] + dq_scr[...] - dsk * k * 0.125).astype(BF16)
        dz_ref[:, 1792:2048] = (dka_ref[...] + dk_scr[...] - dsk * (q_raw * 0.125)).astype(BF16)
        dz_ref[:, 2048:2560] = (dva_ref[...] + dv_scr[...] - sb_scr[...] * do_ref[...]).astype(BF16)
        dz_ref[:, 3072:3200] = (dlra_ref[...] + dlr).astype(BF16)

    full = lambda i: (0, 0)
    tokv = pl.BlockSpec((tb, GV), lambda i: (i, 0))
    tokk = pl.BlockSpec((tb, GK), lambda i: (i, 0))
    dcp = pl.BlockSpec((8, CW), lambda i: (jnp.maximum(i * (tb // 8) - 1, 0), 0))
    dcn = pl.BlockSpec((8, CW), lambda i: (jnp.minimum((i + 1) * (tb // 8), t // 8 - 1), 0))
    in_specs = [_zspec(tb, GK, ZB_Q, jmap), _zspec(tb, GK, ZB_K, jmap), _zspec(tb, GV, ZB_V, jmap),
                _zspec(tb, 128, ZB_LR, jmap), _zspec(tb, CW, ZB_CC, jmap), _zspec(tb, CW, ZB_CU, jmap), tokv,
                pl.BlockSpec((nb, 128, GK), lambda i: (i, 0, 0)), pl.BlockSpec((128, GK), full),
                pl.BlockSpec((1, GK), full), tokk, tokk, tokv, pl.BlockSpec((tb, 128), lambda i: (i, 0)), tokv, tokv,
                tokv, dcp, dcn, pl.BlockSpec((3, CW), full)]
    out_specs = [pl.BlockSpec((tb, ZC), lambda i: (i, 0)), pl.BlockSpec((128, GK), full), pl.BlockSpec((8, GK), full)]
    out_shape = [jax.ShapeDtypeStruct((t, ZC), BF16), jax.ShapeDtypeStruct((128, GK), F32),
                 jax.ShapeDtypeStruct((8, GK), F32)]
    return _call(
        body, "gla_bwd_second", (nblk,), in_specs, out_specs, out_shape,
        [pltpu.VMEM((128, GK), F32), pltpu.VMEM((tb, GK), F32), pltpu.VMEM((tb, GK), F32),
         pltpu.VMEM((tb, GK), F32), pltpu.VMEM((tb, GK), F32), pltpu.VMEM((tb, GV), F32),
         pltpu.VMEM((tb, GV), F32), pltpu.VMEM((tb, GK), F32)],
        (z, z, z, z, z, z, do, sd, wpad, bias, dqa, dka, dva, dlra, dzg, dzcb, dconv, dconv, dconv, conv_w), riders)


def _step(x, mem, target, shard, small_pack, vec, place):
    own, from_chips = {}, {}

    def pair_sums(names, g4, from_sibling):
        pbs = []
        for n, g, s in zip(names, g4, from_sibling):
            pb, own[n] = _rs_pair_sum(place, g, s, "pair_sum_" + n)
            pbs.append(pb)
        return pbs

    def by_dest(g, n):
        return g.reshape((4, 2) + shard[n].shape)

    w_in, small_all = _exchange(_gather_rider([shard["w_in"], small_pack]), "gather_w_in")
    w_in = jnp.pad(w_in.reshape(ZW, D), ((0, ZC - ZW), (0, 0)))
    small_all = small_all.reshape(NDEV, -1)
    p, off = {}, 0
    for n, (r, c) in SMALL_SHARDED.items():
        p[n] = small_all[:, off:off + r * c].reshape(NDEV, r, c).transpose(1, 0, 2).reshape(r, NDEV * c)
        off += r * c
    zeros_lr = jnp.zeros((128 - LR, GK), BF16)
    waf_pad = jnp.concatenate([p["w_af"].astype(BF16), zeros_lr], axis=0)
    wab_pad = jnp.concatenate([jnp.zeros((LR, GK), BF16), p["w_ab"].astype(BF16), zeros_lr[:128 - 2 * LR]], axis=0)
    gla_norm4 = jnp.tile(vec["gla_norm"], (1, NH))

    z, hb, w_out, w_xq, w_xo, w_xkv = _inproj(
        x, vec["mix_norm"], w_in, [_gather_rider([shard[n] for n in ("w_out", "w_xq", "w_xo", "w_xkv")])])
    w_out, w_xq, w_xo = [a.reshape(D, D) for a in (w_out, w_xq, w_xo)]
    o_f, sd_f, o_b, sd_b, w_up_t, w_down = _gla_fwd(
        z, waf_pad, vec["b_af"], wab_pad, vec["b_ab"], [_gather_rider([shard["w_up"], shard["w_down"]])])
    w_up_t, w_down = w_up_t.reshape(FF, D), w_down.reshape(FF, D)
    kv, memn = _kv_proj(mem, vec["mem_norm"], w_xkv)
    kb, vb = kv[:, :D].astype(BF16), kv[:, D:].astype(BF16)
    x1, x2, xn1, qb, attb, yb, o_pre = _attn_fwd(x, z, o_f, o_b, p["conv_w"], vec["conv_norm"], gla_norm4, w_out,
                                                 vec["xa_norm"], w_xq, kb, vb, w_xo)
    h1b, xn2, dx3, dx3b, loss8, dfinal = _mlp_fwd(x2, vec["mlp_norm"], w_up_t, w_down, vec["final_norm"], target)

    ab, dh1b, dx2, dx2b, dmlp = _mlp_bwd(dx3, dx3b, h1b, w_down, w_up_t, x2, vec["mlp_norm"])
    g_mlp = [by_dest(_matmul_tn(ab, dx3b, "dw_down")[0], "w_down"),
             by_dest(_matmul_tn(dh1b, xn2, "dw_up")[0], "w_up")]
    dx1, dx1b, dy, dqb, dkv, dxa, *s_mlp = _attn_bwd(x1, dx2, dx2b, qb, kb, vb, w_xo, w_xq, w_out, vec["xa_norm"],
                                                     riders=[_sibling_rider(g_mlp)])
    pb_mlp = pair_sums(("w_down", "w_up"), g_mlp, s_mlp)
    dw_xo = _matmul_tn(attb, dx2b, "dw_xo")[0]
    dw_xkv, dmemn = _kv_bwd(dkv, memn, mem, vec["mem_norm"], w_xkv)
    att_names = ("w_xo", "w_xq", "w_out", "w_xkv")
    g_att = [by_dest(g, n) for g, n in zip(
        (dw_xo, _matmul_tn(xn1, dqb, "dw_xq")[0], _matmul_tn(yb, dx1b, "dw_out")[0], dw_xkv), att_names)]
    res = _gla_bwd_first(z, dy, o_pre, sd_f, waf_pad, vec["b_af"], p["conv_w"], vec["conv_norm"], gla_norm4,
                         riders=[_chips_rider(pb_mlp), _sibling_rider(g_att)])
    do, dqa, dka, dva, dlra, dzg, dzcb, dconv, dwaf, dbaf, dcw, dcn, dgn = res[:13]
    from_chips["w_down"], from_chips["w_up"] = res[13:15]
    pb_att = pair_sums(att_names, g_att, res[15:])
    dz, dwab, dbab, *c_att = _gla_bwd_second(z, do, sd_b, wab_pad, vec["b_ab"], dqa, dka, dva, dlra, dzg, dzcb, dconv,
                                             p["conv_w"], riders=[_chips_rider(pb_att)])
    from_chips.update(zip(att_names, c_att))
    g_in = [by_dest(_matmul_tn(dz, hb, "dw_in", rows=ZW)[0], "w_in")]
    pb_in = pair_sums(("w_in",), g_in, _exchange(_sibling_rider(g_in), "grads_to_sibling_w_in"))
    grad_x, dmix, from_chips["w_in"] = _inproj_bwd(dz, w_in, x, dx1, vec["mix_norm"], riders=[_chips_rider(pb_in)])

    small_acc = dict(mix_norm=dmix, conv_w=dcw, conv_norm=dcn, w_af=dwaf, b_af=dbaf, w_ab=dwab, b_ab=dbab,
                     gla_norm=dgn, xa_norm=dxa, mem_norm=dmemn, mlp_norm=dmlp, final_norm=dfinal)
    return loss8, grad_x, small_acc, own, from_chips


def _place():
    return lax.axis_index("x"), lax.axis_index("y"), lax.axis_index("c")


class _Rider:
    def __init__(self, arrays, out_shape, scratch, start, finish, relay=None):
        self.arrays, self.out_shape, self.scratch, self.start, self.finish = arrays, out_shape, scratch, start, finish
        self.relay = relay


def _gather_rider(blks):
    n = len(blks)

    def plan(in_refs, out_refs, sems):
        send_sems, recv_sems, local_sems = sems
        x, y, c = _place()
        me, sibling = (x, y, c), (x, y, 1 - c)
        chips = [(1 - x, y, c), (x, 1 - y, c), (1 - x, 1 - y, c)]

        def copy(a, k, block, to, own=False):
            px, py, pc = block
            dst = out_refs[a].at[4 * px + 2 * py + pc]
            return pltpu.make_async_remote_copy(
                src_ref=in_refs[a] if own else dst, dst_ref=dst, send_sem=send_sems.at[k, a],
                recv_sem=recv_sems.at[k, a], device_id=to, device_id_type=MESH)

        def local(a):
            return pltpu.make_async_copy(in_refs[a], out_refs[a].at[4 * x + 2 * y + c], local_sems.at[a])

        def own_sends(a):
            return [copy(a, 0, me, sibling, own=True)] + [copy(a, 1 + j, me, chip, own=True)
                                                          for j, chip in enumerate(chips)]

        return copy, local, own_sends, me, sibling, chips

    def start(in_refs, out_refs, sems):
        _, local, own_sends, _, _, _ = plan(in_refs, out_refs, sems)
        for a in range(n):
            local(a).start()
            for cp in own_sends(a):
                cp.start()

    def relay(in_refs, out_refs, sems):
        copy, _, _, me, sibling, chips = plan(in_refs, out_refs, sems)
        for j, chip in enumerate(chips):
            for a in range(n):
                copy(a, 1 + j, chip, me).wait_recv()
                copy(a, 4 + j, chip, sibling).start()

    def finish(in_refs, out_refs, sems):
        copy, local, own_sends, me, sibling, chips = plan(in_refs, out_refs, sems)
        for a in range(n):
            copy(a, 0, sibling, me).wait_recv()
            for j, (px, py, pc) in enumerate(chips):
                copy(a, 4 + j, (px, py, 1 - pc), me).wait_recv()
            for cp in own_sends(a) + [copy(a, 4 + j, chip, sibling) for j, chip in enumerate(chips)]:
                cp.wait_send()
            local(a).wait()

    return _Rider(blks, [jax.ShapeDtypeStruct((NDEV,) + b.shape, b.dtype) for b in blks],
                  [pltpu.SemaphoreType.DMA((7, n)), pltpu.SemaphoreType.DMA((7, n)), pltpu.SemaphoreType.DMA((n,))],
                  start, finish, relay)


def _sibling_rider(g4s):
    n = len(g4s)

    def copies(in_refs, out_refs, sems):
        send_sems, recv_sems = sems
        x, y, c = _place()
        return [pltpu.make_async_remote_copy(
            src_ref=in_refs[a].at[k, 1 - c], dst_ref=out_refs[a].at[k], send_sem=send_sems.at[k, a],
            recv_sem=recv_sems.at[k, a], device_id=(x, y, 1 - c), device_id_type=MESH)
            for a in range(n) for k in range(4)]

    def start(in_refs, out_refs, sems):
        for cp in copies(in_refs, out_refs, sems):
            cp.start()

    def finish(in_refs, out_refs, sems):
        for cp in copies(in_refs, out_refs, sems):
            cp.wait()

    return _Rider(g4s, [jax.ShapeDtypeStruct((4,) + g.shape[2:], g.dtype) for g in g4s],
                  [pltpu.SemaphoreType.DMA((4, n)), pltpu.SemaphoreType.DMA((4, n))], start, finish)


def _chips_rider(pbs):
    n = len(pbs)

    def copies(in_refs, out_refs, sems):
        send_sems, recv_sems = sems
        x, y, c = _place()
        peers = [(1 - x, y), (x, 1 - y), (1 - x, 1 - y)]
        return [pltpu.make_async_remote_copy(
            src_ref=in_refs[a].at[2 * px + py], dst_ref=out_refs[a].at[k], send_sem=send_sems.at[k, a],
            recv_sem=recv_sems.at[k, a], device_id=(px, py, c), device_id_type=MESH)
            for a in range(n) for k, (px, py) in enumerate(peers)]

    def start(in_refs, out_refs, sems):
        for cp in copies(in_refs, out_refs, sems):
            cp.start()

    def finish(in_refs, out_refs, sems):
        for cp in copies(in_refs, out_refs, sems):
            cp.wait()

    return _Rider(pbs, [jax.ShapeDtypeStruct((3,) + p.shape[1:], p.dtype) for p in pbs],
                  [pltpu.SemaphoreType.DMA((3, n)), pltpu.SemaphoreType.DMA((3, n))], start, finish)


def _exchange(rider, name):
    n_in, n_out = len(rider.arrays), len(rider.out_shape)

    def body(*refs):
        ins, outs, sems = refs[:n_in], refs[n_in:n_in + n_out], refs[n_in + n_out:]
        rider.start(ins, outs, sems)
        if rider.relay:
            rider.relay(ins, outs, sems)
        rider.finish(ins, outs, sems)

    hbm = pl.BlockSpec(memory_space=pltpu.HBM)
    return pl.pallas_call(body, name=name, out_shape=rider.out_shape, in_specs=[hbm] * n_in,
                          out_specs=[hbm] * n_out, scratch_shapes=rider.scratch)(*rider.arrays)


def _rs_pair_sum(place, g4, r1, name):
    rows, cols = g4.shape[2:]
    tr = min(rows, 512)

    def body(pl_ref, g_ref, r_ref, pb_ref, own_ref):
        s = g_ref[0, 0] + r_ref[0]
        pb_ref[0] = s.astype(BF16)

        @pl.when(pl.program_id(1) == pl_ref[0])
        def _():
            own_ref[...] = s

    grid_spec = pltpu.PrefetchScalarGridSpec(
        num_scalar_prefetch=1, grid=(rows // tr, 4),
        in_specs=[pl.BlockSpec((1, 1, tr, cols), lambda r, k, p: (k, p[1], r, 0)),
                  pl.BlockSpec((1, tr, cols), lambda r, k, p: (k, r, 0))],
        out_specs=[pl.BlockSpec((1, tr, cols), lambda r, k, p: (k, r, 0)),
                   pl.BlockSpec((tr, cols), lambda r, k, p: (r, 0))])
    return pl.pallas_call(
        body, name=name, grid_spec=grid_spec,
        out_shape=[jax.ShapeDtypeStruct((4, rows, cols), BF16), jax.ShapeDtypeStruct((rows, cols), F32)],
        compiler_params=_cparams(("arbitrary", "arbitrary")))(place, g4, r1)


PACK_ROWS = 32
VEC_ROW = {"mix_norm": 0, "conv_norm": 1, "b_af": 2, "b_ab": 3, "gla_norm": 4, "xa_norm": 5, "mem_norm": 6,
           "mlp_norm": 7, "final_norm": 8}
LOSS_ROW, MAT_ROW = 9, 16
MAT_LANE = {"w_af": 0, "w_ab": GK, "conv_w": 2 * GK}
MAT_SRC_ROW = {"w_af": 0, "w_ab": LR, "conv_w": 0}


SMALL_WIDTH = {"mix_norm": D, "conv_w": 64, "conv_norm": CW, "w_af": 32, "b_af": GK, "w_ab": 32, "b_ab": GK,
               "gla_norm": 128, "xa_norm": D, "mem_norm": D, "mlp_norm": D, "final_norm": D}


def _small_reduce(acc, loss8):
    names = list(SMALL)
    n = len(names)
    widths = SMALL_WIDTH

    def body(*refs):
        acc_refs = dict(zip(names, refs[:n]))
        loss_ref, tot = refs[n], refs[n + 1]
        pk, all_ref, send_sems, recv_sems, local_sem = refs[n + 2:]

        pk[...] = jnp.zeros_like(pk)
        for k, row in VEC_ROW.items():
            if k == "gla_norm":
                g = functools.reduce(lambda a, b: a + b, [acc_refs[k][pl.ds(0, 1), pl.ds(h * 128, 128)]
                                                          for h in range(NH)])
            else:
                g = acc_refs[k][pl.ds(0, 1), :]
            pk[pl.ds(row, 1), pl.ds(0, widths[k])] = g
        pk[pl.ds(LOSS_ROW, 1), pl.ds(0, 128)] = loss_ref[pl.ds(0, 1), :]
        for k, lane in MAT_LANE.items():
            rows, cols = (3, CW) if k == "conv_w" else (LR, GK)
            pk[pl.ds(MAT_ROW, rows), pl.ds(lane, cols)] = acc_refs[k][pl.ds(MAT_SRC_ROW[k], rows), :]

        x, y, c = _place()
        me, sibling = (x, y, c), (x, y, 1 - c)
        chips = [(1 - x, y, c), (x, 1 - y, c), (1 - x, 1 - y, c)]

        def copy(k, block, to, own=False):
            px, py, pc = block
            dst = all_ref.at[4 * px + 2 * py + pc]
            return pltpu.make_async_remote_copy(
                src_ref=pk if own else dst, dst_ref=dst, send_sem=send_sems.at[k], recv_sem=recv_sems.at[k],
                device_id=to, device_id_type=MESH)

        mine = pltpu.make_async_copy(pk, all_ref.at[4 * x + 2 * y + c], local_sem)
        mine.start()
        first = [copy(0, me, sibling, own=True)] + [copy(1 + j, me, chip, own=True) for j, chip in enumerate(chips)]
        for cp in first:
            cp.start()
        passed = [copy(4 + j, chip, sibling) for j, chip in enumerate(chips)]
        for j, chip in enumerate(chips):
            copy(1 + j, chip, me).wait_recv()
            passed[j].start()
        copy(0, sibling, me).wait_recv()
        for j, (px, py, pc) in enumerate(chips):
            copy(4 + j, (px, py, 1 - pc), me).wait_recv()
        for cp in first + passed:
            cp.wait_send()
        mine.wait()
        total = all_ref[0]
        for d in range(1, NDEV):
            total = total + all_ref[d]
        tot[...] = total

    return pl.pallas_call(
        body, name="small_reduce", out_shape=jax.ShapeDtypeStruct((PACK_ROWS, D), F32),
        scratch_shapes=[pltpu.VMEM((PACK_ROWS, D), F32), pltpu.VMEM((NDEV, PACK_ROWS, D), F32),
                        pltpu.SemaphoreType.DMA((7,)), pltpu.SemaphoreType.DMA((7,)), pltpu.SemaphoreType.DMA],
    )(*[acc[k] for k in names], loss8)


def _small_adamw(tot, ws, ms, vs):
    names = list(SMALL)
    n = len(names)
    widths = SMALL_WIDTH

    def body(*refs):
        tot = refs[0]
        w_refs, m_refs, v_refs = [dict(zip(names, refs[1 + q * n:1 + (q + 1) * n])) for q in range(3)]
        outs = refs[1 + 3 * n:1 + 7 * n]
        g_out, d_out, m_out, v_out = [dict(zip(names, outs[q * n:(q + 1) * n])) for q in range(4)]
        cut = refs[1 + 7 * n]
        x, y, c = _place()
        dev = 4 * x + 2 * y + c
        for k in names:
            if k in VEC_ROW:
                g = tot[pl.ds(VEC_ROW[k], 1), pl.ds(0, widths[k])]
            else:
                rows, cols = (3, CW) if k == "conv_w" else (LR, GK)
                wd = widths[k]
                sel = jnp.where(_iota((cols, wd), 0) == dev * wd + _iota((cols, wd), 1), 1.0, 0.0).astype(BF16)
                cut[:, pl.ds(0, wd)] = _dot_exact_rhs(tot[pl.ds(MAT_ROW, LR), pl.ds(MAT_LANE[k], cols)], sel, 3)
                g = cut[pl.ds(0, rows), pl.ds(0, wd)]
            g_out[k][...] = g
            d_out[k][...], m_out[k][...], v_out[k][...] = _adamw_math(w_refs[k][...], g, m_refs[k][...],
                                                                       v_refs[k][...])

    shapes = [jax.ShapeDtypeStruct(ws[k].shape, F32) for k in names]
    res = pl.pallas_call(
        body, name="small_adamw", out_shape=shapes * 4, scratch_shapes=[pltpu.VMEM((LR, 128), F32)],
    )(tot, *[ws[k] for k in names], *[ms[k] for k in names], *[vs[k] for k in names])
    return {k: tuple(res[q * n + i] for q in range(4)) for i, k in enumerate(names)}


def _adamw_math(w, g, m, v):
    m = ADAM_B1 * m + (1.0 - ADAM_B1) * g
    v = ADAM_B2 * v + (1.0 - ADAM_B2) * (g * g)
    m_hat = m / (1.0 - ADAM_B1 ** ADAM_STEP)
    v_hat = v / (1.0 - ADAM_B2 ** ADAM_STEP)
    delta = -ADAM_LR * (m_hat / (jnp.sqrt(v_hat) + ADAM_EPS) + ADAM_WD * w)
    return delta, m, v


def _adamw(w, m, v, own, r2, name):
    _, r, c = w.shape
    tr = 256 if r % 256 == 0 else r

    def body(w_ref, m_ref, v_ref, o_ref, r_ref, g_ref, d_ref, nm_ref, nv_ref):
        g = ((o_ref[...] + r_ref[0].astype(F32)) + r_ref[1].astype(F32)) + r_ref[2].astype(F32)
        g_ref[...] = g
        d_ref[...], nm_ref[...], nv_ref[...] = _adamw_math(w_ref[...], g, m_ref[...], v_ref[...])

    spec = pl.BlockSpec((None, tr, c), lambda i: (0, i, 0))
    return pl.pallas_call(
        body, name=name, grid=(r // tr,),
        in_specs=[spec, spec, spec, pl.BlockSpec((tr, c), lambda i: (i, 0)),
                  pl.BlockSpec((3, tr, c), lambda i: (0, i, 0))],
        out_specs=[spec] * 4, out_shape=[jax.ShapeDtypeStruct((1, r, c), F32)] * 4,
        compiler_params=_cparams(("arbitrary",)))(w, m, v, own, r2)


MATS = ("w_in", "w_out", "w_xq", "w_xo", "w_xkv", "w_up", "w_down")
SMALL = ("mix_norm", "conv_w", "conv_norm", "w_af", "b_af", "w_ab", "b_ab", "gla_norm", "xa_norm", "mem_norm",
         "mlp_norm", "final_norm")
WEIGHTS = ("mix_norm", "w_in", "conv_w", "conv_norm", "w_af", "b_af", "w_ab", "b_ab", "gla_norm", "w_out", "xa_norm",
           "mem_norm", "w_xq", "w_xkv", "w_xo", "mlp_norm", "w_up", "w_down", "final_norm")
SMALL_SHARDED = {"conv_w": (3, 64), "w_af": (LR, 32), "w_ab": (LR, 32)}
SMALL_PACK_ROWS = 16


def kernel(x, mem, mix_norm, w_in, conv_w, conv_norm, w_af, b_af, w_ab, b_ab, gla_norm, w_out, xa_norm, mem_norm, w_xq, w_xkv, w_xo, mlp_norm, w_up, w_down, final_norm, loss_target, m_mix_norm, m_w_in, m_conv_w, m_conv_norm, m_w_af, m_b_af, m_w_ab, m_b_ab, m_gla_norm, m_w_out, m_xa_norm, m_mem_norm, m_w_xq, m_w_xkv, m_w_xo, m_mlp_norm, m_w_up, m_w_down, m_final_norm, v_mix_norm, v_w_in, v_conv_w, v_conv_norm, v_w_af, v_b_af, v_w_ab, v_b_ab, v_gla_norm, v_w_out, v_xa_norm, v_mem_norm, v_w_xq, v_w_xkv, v_w_xo, v_mlp_norm, v_w_up, v_w_down, v_final_norm):
    w = dict(mix_norm=mix_norm, w_in=w_in, conv_w=conv_w, conv_norm=conv_norm, w_af=w_af, b_af=b_af, w_ab=w_ab,
             b_ab=b_ab, gla_norm=gla_norm, w_out=w_out, xa_norm=xa_norm, mem_norm=mem_norm, w_xq=w_xq, w_xkv=w_xkv,
             w_xo=w_xo, mlp_norm=mlp_norm, w_up=w_up, w_down=w_down, final_norm=final_norm)
    mom = dict(mix_norm=m_mix_norm, w_in=m_w_in, conv_w=m_conv_w, conv_norm=m_conv_norm, w_af=m_w_af, b_af=m_b_af,
               w_ab=m_w_ab, b_ab=m_b_ab, gla_norm=m_gla_norm, w_out=m_w_out, xa_norm=m_xa_norm, mem_norm=m_mem_norm,
               w_xq=m_w_xq, w_xkv=m_w_xkv, w_xo=m_w_xo, mlp_norm=m_mlp_norm, w_up=m_w_up, w_down=m_w_down,
               final_norm=m_final_norm)
    var = dict(mix_norm=v_mix_norm, w_in=v_w_in, conv_w=v_conv_w, conv_norm=v_conv_norm, w_af=v_w_af, b_af=v_b_af,
               w_ab=v_w_ab, b_ab=v_b_ab, gla_norm=v_gla_norm, w_out=v_w_out, xa_norm=v_xa_norm, mem_norm=v_mem_norm,
               w_xq=v_w_xq, w_xkv=v_w_xkv, w_xo=v_w_xo, mlp_norm=v_mlp_norm, w_up=v_w_up, w_down=v_w_down,
               final_norm=v_final_norm)
    xi, yi, ci = _place()
    two_d = lambda a: a.reshape(a.shape[-2:]) if a.ndim == 3 else a.reshape(1, a.shape[-1])

    small = jnp.concatenate([w[n].reshape(-1) for n in SMALL_SHARDED])
    small = jnp.pad(small, (0, SMALL_PACK_ROWS * 128 - small.shape[0])).reshape(SMALL_PACK_ROWS, 128)
    shard = {n: two_d(w[n]).astype(BF16) for n in MATS}
    for n in ("w_in", "w_up"):
        shard[n] = shard[n].T
    vec = {n: two_d(w[n]) for n in SMALL if n not in SMALL_SHARDED}
    place = jnp.stack([2 * xi + yi, ci]).astype(jnp.int32)
    loss8, grad_x, small_acc, own, from_chips = _step(x[0], mem[0], loss_target[0], shard, small, vec, place)

    tot = _small_reduce(small_acc, loss8)
    small_out = _small_adamw(tot, *[{n: two_d(d[n]) for n in SMALL} for d in (w, mom, var)])
    loss = tot[LOSS_ROW, 0]

    out_g, out_d, out_m, out_v = {}, {}, {}, {}
    own["w_up"], from_chips["w_up"] = own["w_up"].T, from_chips["w_up"].transpose(0, 2, 1)
    for n in MATS:
        if n == "w_in":
            res = _adamw(*[a.transpose(0, 2, 1) for a in (w[n], mom[n], var[n])], own[n], from_chips[n], "adamw_" + n)
            res = [a.transpose(0, 2, 1) for a in res]
        else:
            res = _adamw(w[n], mom[n], var[n], own[n], from_chips[n], "adamw_" + n)
        out_g[n], out_d[n], out_m[n], out_v[n] = res
    for n in SMALL:
        out_g[n], out_d[n], out_m[n], out_v[n] = [a.reshape(w[n].shape) for a in small_out[n]]

    return (loss, grad_x[None], *[out_g[n] for n in WEIGHTS], *[out_d[n] for n in WEIGHTS],
            *[out_m[n] for n in WEIGHTS], *[out_v[n] for n in WEIGHTS])
```

```python
import functools
import itertools

import jax
import jax.numpy as jnp
from jax import lax
from jax.experimental import pallas as pl
from jax.experimental.pallas import tpu as pltpu

F32 = jnp.float32
BF16 = jnp.bfloat16

D = 1024
CW = 512
GK = 256
GV = 512
NH = 4
CH = 64
LR = 16
NMEM = 256
XD = 256
FF = 4096
ZW = 3104
ZC = 3200
EPS = 1e-6
NDEV = 8

ZB_CB, ZB_CC, ZB_CU, ZB_V, ZB_G = 0, 1, 2, 4, 5
ZB_Q, ZB_K = 6, 7
ZB_LR = 24

TM = 512
TM_MLP = 256
TM_MLP_FWD = 512
TF = 512
TB = 512
TB_BWD = 512
TT = 2048
VMEM_LIMIT = 56 * 1024 * 1024

ADAM_LR, ADAM_B1, ADAM_B2, ADAM_EPS, ADAM_WD, ADAM_STEP = 0.001, 0.9, 0.999, 1e-08, 0.01, 10

XKV_SHARD = 2 * D // NDEV

MESH = pl.DeviceIdType.MESH


def _cparams(sem):
    return pltpu.CompilerParams(dimension_semantics=sem, vmem_limit_bytes=VMEM_LIMIT)


def _call(body, name, grid, in_specs, out_specs, out_shape, scratch, args, riders=()):
    n_in, n_out, n_scr = len(in_specs), len(out_specs), len(scratch)
    counts = [(len(r.arrays), len(r.out_shape), len(r.scratch)) for r in riders]

    def take(refs, pos, sizes):
        groups = []
        for size in sizes:
            groups.append(refs[pos:pos + size])
            pos += size
        return groups, pos

    def wrapped(*refs):
        ins, pos = refs[:n_in], n_in
        r_ins, pos = take(refs, pos, [c[0] for c in counts])
        outs, pos = refs[pos:pos + n_out], pos + n_out
        r_outs, pos = take(refs, pos, [c[1] for c in counts])
        scr, pos = refs[pos:pos + n_scr], pos + n_scr
        r_scr, pos = take(refs, pos, [c[2] for c in counts])
        ids = [pl.program_id(d) for d in range(len(grid))]
        first = functools.reduce(lambda a, b: a & b, [i == 0 for i in ids])
        last = functools.reduce(lambda a, b: a & b, [i == g - 1 for i, g in zip(ids, grid)])

        @pl.when(first)
        def _():
            for r, a, b, c in zip(riders, r_ins, r_outs, r_scr):
                r.start(a, b, c)

        body(*ins, *outs, *scr)

        if any(r.relay for r in riders):
            at = [max(g - 2, 0) for g in grid]

            @pl.when(functools.reduce(lambda a, b: a & b, [i == s for i, s in zip(ids, at)]))
            def _():
                for r, a, b, c in zip(riders, r_ins, r_outs, r_scr):
                    if r.relay:
                        r.relay(a, b, c)

        @pl.when(last)
        def _():
            for r, a, b, c in zip(riders, r_ins, r_outs, r_scr):
                r.finish(a, b, c)

    hbm = pl.BlockSpec(memory_space=pltpu.HBM)
    r_args = [a for r in riders for a in r.arrays]
    r_shapes = [s for r in riders for s in r.out_shape]
    return pl.pallas_call(
        wrapped if riders else body, name=name, grid=grid, in_specs=list(in_specs) + [hbm] * len(r_args),
        out_specs=list(out_specs) + [hbm] * len(r_shapes), out_shape=list(out_shape) + r_shapes,
        scratch_shapes=list(scratch) + [s for r in riders for s in r.scratch],
        compiler_params=_cparams(("arbitrary",) * len(grid)))(*args, *r_args)


def _dot(a, b):
    return jnp.dot(a.astype(BF16), b.astype(BF16), preferred_element_type=F32)


def _dot_nt(a, b):
    return lax.dot_general(a.astype(BF16), b.astype(BF16), (((1,), (1,)), ((), ())), preferred_element_type=F32)


def _dot_tn(a, b):
    return lax.dot_general(a.astype(BF16), b.astype(BF16), (((0,), (0,)), ((), ())), preferred_element_type=F32)


def _split(x, n):
    parts = []
    for _ in range(n):
        p = x.astype(BF16)
        parts.append(p)
        x = x - p.astype(F32)
    return parts


def _dot_exact_lhs(m, x, n):
    return functools.reduce(lambda a, b: a + b, [jnp.dot(m, p, preferred_element_type=F32) for p in _split(x, n)])


def _dot_exact_rhs(x, m, n):
    return functools.reduce(lambda a, b: a + b, [jnp.dot(p, m, preferred_element_type=F32) for p in _split(x, n)])


def _rms(x, g):
    r = lax.rsqrt(jnp.mean(x * x, axis=-1, keepdims=True) + EPS)
    return x * r * g, r


def _rms_bwd(x, r, g, dy):
    xr = x * r
    u = dy * g
    dx = r * (u - xr * jnp.mean(u * xr, axis=-1, keepdims=True))
    return dx, jnp.sum(dy * xr, axis=0, keepdims=True)


def _iota(shape, dim):
    return lax.broadcasted_iota(jnp.int32, shape, dim)


def _sigmoid(x):
    return 1.0 / (1.0 + jnp.exp(-x))


def _acc_rows(ref, row):
    ref[...] += jnp.broadcast_to(row, ref.shape)


def _inproj(x, g, w_t, riders=()):
    t = x.shape[0]
    tm = min(TM, t)

    def body(x_ref, g_ref, w_ref, z_ref, h_ref):
        h, _ = _rms(x_ref[...], g_ref[...])
        hb = h.astype(BF16)
        h_ref[...] = hb
        z_ref[...] = _dot_nt(hb, w_ref[...])

    return _call(
        body, "inproj", (t // tm,),
        [pl.BlockSpec((tm, D), lambda i: (i, 0)), pl.BlockSpec((1, D), lambda i: (0, 0)),
         pl.BlockSpec((ZC, D), lambda i: (0, 0))],
        [pl.BlockSpec((tm, ZC), lambda i: (i, 0)), pl.BlockSpec((tm, D), lambda i: (i, 0))],
        [jax.ShapeDtypeStruct((t, ZC), F32), jax.ShapeDtypeStruct((t, D), BF16)], [], (x, g, w_t), riders)


def _kv_proj(mem, g, w):
    def body(m_ref, g_ref, w_ref, kv_ref, mn_ref):
        mn, _ = _rms(m_ref[...], g_ref[...])
        mb = mn.astype(BF16)
        mn_ref[...] = mb
        for j in range(NDEV):
            kv_ref[:, j * XKV_SHARD:(j + 1) * XKV_SHARD] = jnp.dot(mb, w_ref[j], preferred_element_type=F32)

    return pl.pallas_call(
        body, name="kv_proj",
        out_shape=[jax.ShapeDtypeStruct((NMEM, 2 * D), F32), jax.ShapeDtypeStruct((NMEM, D), BF16)],
        compiler_params=pltpu.CompilerParams(vmem_limit_bytes=VMEM_LIMIT))(mem, g, w)


def _softmax_head(qb, kb):
    s = _dot_nt(qb, kb) * (1.0 / 16.0)
    e = jnp.exp(s - jnp.max(s, axis=-1, keepdims=True))
    return e / jnp.sum(e, axis=-1, keepdims=True)


def _attn_fwd(x, z, o_f, o_b, conv_w, conv_norm, gla_norm4, w_out, g, w_xq, kb, vb, w_xo):
    t = x.shape[0]
    tm = min(TM, t)
    nblk = t // tm
    jmap = lambda i: i

    def body(x_ref, zq_ref, zk_ref, zv_ref, zg_ref, cb_ref, cc_ref, cu_ref, ccp_ref, ccn_ref, cup_ref, cun_ref,
             of_ref, ob_ref, cw_ref, cn_ref, gn_ref, wo_ref, g_ref, wq_ref, k_ref, v_ref, wx_ref,
             x1_ref, x2_ref, xn_ref, q_ref, a_ref, y_ref, opre_ref):
        j = pl.program_id(0)
        zv = zv_ref[...]
        sb = _head_sum((zq_ref[...] * 0.125) * zk_ref[...], 64, 128)
        o_pre = of_ref[...] + ob_ref[...] - sb * zv
        opre_ref[...] = o_pre
        on, _ = _head_norm(o_pre)
        zg = zg_ref[...]
        y_ref[:, CW:] = (on * gn_ref[...] * (zg * _sigmoid(zg))).astype(BF16)
        cb = cb_ref[...]
        _, _, _, conv = _conv_parts(cb, cc_ref[...], cu_ref[...], ccp_ref[pl.ds(7, 1), :], cup_ref[pl.ds(7, 1), :],
                                    ccn_ref[pl.ds(0, 1), :], cun_ref[pl.ds(0, 1), :], cw_ref, j == 0,
                                    j == nblk - 1, tm)
        yc = cb * conv
        gm = _group_sum(yc * yc) * (1.0 / 64.0)
        y_ref[:, :CW] = (yc * lax.rsqrt(gm + EPS) * cn_ref[...]).astype(BF16)

        x1 = x_ref[...] + jnp.dot(y_ref[...], wo_ref[...], preferred_element_type=F32)
        x1_ref[...] = x1
        xn, _ = _rms(x1, g_ref[...])
        xb = xn.astype(BF16)
        xn_ref[...] = xb
        qb = jnp.dot(xb, wq_ref[...], preferred_element_type=F32).astype(BF16)
        q_ref[...] = qb
        for h in range(NH):
            hs = slice(h * XD, (h + 1) * XD)
            p = _softmax_head(qb[:, hs], k_ref[:, hs])
            a_ref[:, hs] = _dot(p, v_ref[:, hs]).astype(BF16)
        x2_ref[...] = x1 + jnp.dot(a_ref[...], wx_ref[...], preferred_element_type=F32)

    tok = lambda i: (i, 0)
    full = lambda i: (0, 0)
    once = pl.Buffered(1)
    tokd, tokv = pl.BlockSpec((tm, D), tok), pl.BlockSpec((tm, GV), tok)
    weight = pl.BlockSpec((D, D), full, pipeline_mode=once)
    ccp, ccn = _halo_specs(tm, nblk, t, ZB_CC, jmap)
    cup, cun = _halo_specs(tm, nblk, t, ZB_CU, jmap)
    in_specs = [tokd, _zspec(tm, GK, ZB_Q, jmap), _zspec(tm, GK, ZB_K, jmap), _zspec(tm, GV, ZB_V, jmap),
                _zspec(tm, GV, ZB_G, jmap), _zspec(tm, CW, ZB_CB, jmap), _zspec(tm, CW, ZB_CC, jmap),
                _zspec(tm, CW, ZB_CU, jmap), ccp, ccn, cup, cun, tokv, tokv,
                pl.BlockSpec((3, CW), full), pl.BlockSpec((1, CW), full), pl.BlockSpec((1, GV), full),
                weight, pl.BlockSpec((1, D), full), weight, pl.BlockSpec((NMEM, D), full),
                pl.BlockSpec((NMEM, D), full), weight]
    return pl.pallas_call(
        body, name="attn_fwd", grid=(nblk,), in_specs=in_specs, out_specs=[tokd] * 6 + [tokv],
        out_shape=[jax.ShapeDtypeStruct((t, D), F32), jax.ShapeDtypeStruct((t, D), F32),
                   jax.ShapeDtypeStruct((t, D), BF16), jax.ShapeDtypeStruct((t, D), BF16),
                   jax.ShapeDtypeStruct((t, D), BF16), jax.ShapeDtypeStruct((t, D), BF16),
                   jax.ShapeDtypeStruct((t, GV), F32)],
        compiler_params=_cparams(("arbitrary",)))(
            x, z, z, z, z, z, z, z, z, z, z, z, o_f, o_b, conv_w, conv_norm, gla_norm4, w_out, g, w_xq, kb, vb, w_xo)


def _mlp_fwd(x2, g, w_up_t, w_down, fg, target):
    t = x2.shape[0]
    tm = min(TM_MLP_FWD, t)

    def body(x_ref, g_ref, wu_ref, wd_ref, fg_ref, t_ref, h1_ref, xn_ref, dx_ref, dxb_ref, loss_ref, dfg_ref, ab):
        @pl.when(pl.program_id(0) == 0)
        def _():
            loss_ref[...] = jnp.zeros_like(loss_ref)
            dfg_ref[...] = jnp.zeros_like(dfg_ref)

        x = x_ref[...]
        xn, _ = _rms(x, g_ref[...])
        xnb = xn.astype(BF16)
        xn_ref[...] = xnb
        for q in range(FF // TF):
            cols = slice(q * TF, (q + 1) * TF)
            h1 = _dot_nt(xnb, wu_ref[cols, :])
            h1_ref[:, cols] = h1.astype(BF16)
            hr = jnp.maximum(h1, 0.0)
            ab[:, cols] = (hr * hr).astype(BF16)
        x3 = x + jnp.dot(ab[...], wd_ref[...], preferred_element_type=F32)
        y, r = _rms(x3, fg_ref[...])
        e = y - t_ref[...]
        row = jnp.mean(e * e, axis=-1, keepdims=True)
        _acc_rows(loss_ref, 0.5 * jnp.sum(row, axis=0, keepdims=True))
        dx, dfg = _rms_bwd(x3, r, fg_ref[...], e * (1.0 / D))
        dx_ref[...] = dx
        dxb_ref[...] = dx.astype(BF16)
        _acc_rows(dfg_ref, dfg)

    tok = lambda i: (i, 0)
    full = lambda i: (0, 0)
    once = pl.Buffered(1)
    return pl.pallas_call(
        body, name="mlp_fwd", grid=(t // tm,),
        in_specs=[pl.BlockSpec((tm, D), tok), pl.BlockSpec((1, D), full),
                  pl.BlockSpec((FF, D), full, pipeline_mode=once), pl.BlockSpec((FF, D), full, pipeline_mode=once),
                  pl.BlockSpec((1, D), full), pl.BlockSpec((tm, D), tok)],
        out_specs=[pl.BlockSpec((tm, FF), tok), pl.BlockSpec((tm, D), tok), pl.BlockSpec((tm, D), tok),
                   pl.BlockSpec((tm, D), tok), pl.BlockSpec((8, 128), full), pl.BlockSpec((8, D), full)],
        out_shape=[jax.ShapeDtypeStruct((t, FF), BF16), jax.ShapeDtypeStruct((t, D), BF16),
                   jax.ShapeDtypeStruct((t, D), F32), jax.ShapeDtypeStruct((t, D), BF16),
                   jax.ShapeDtypeStruct((8, 128), F32), jax.ShapeDtypeStruct((8, D), F32)],
        scratch_shapes=[pltpu.VMEM((tm, FF), BF16)],
        compiler_params=_cparams(("arbitrary",)))(x2, g, w_up_t, w_down, fg, target)


def _mlp_bwd(dx3, dx3b, h1b, w_down, w_up_t, x2, g):
    t = x2.shape[0]
    tm = min(TM_MLP, t)

    def body(dx_ref, dxb_ref, h1_ref, wd_ref, wu_ref, x_ref, g_ref, a_ref, dh_ref, dx2_ref, dx2b_ref, dg_ref):
        @pl.when(pl.program_id(0) == 0)
        def _():
            dg_ref[...] = jnp.zeros_like(dg_ref)

        for q in range(FF // TF):
            cols = slice(q * TF, (q + 1) * TF)
            hr = jnp.maximum(h1_ref[:, cols].astype(F32), 0.0)
            da = _dot_nt(dxb_ref[...], wd_ref[cols, :])
            a_ref[:, cols] = (hr * hr).astype(BF16)
            dh_ref[:, cols] = (da * 2.0 * hr).astype(BF16)
        dxn = jnp.dot(dh_ref[...], wu_ref[...], preferred_element_type=F32)
        x = x_ref[...]
        r = lax.rsqrt(jnp.mean(x * x, axis=-1, keepdims=True) + EPS)
        dx, dg = _rms_bwd(x, r, g_ref[...], dxn)
        dx2 = dx_ref[...] + dx
        dx2_ref[...] = dx2
        dx2b_ref[...] = dx2.astype(BF16)
        _acc_rows(dg_ref, dg)

    tok = lambda i: (i, 0)
    full = lambda i: (0, 0)
    once = pl.Buffered(1)
    return pl.pallas_call(
        body, name="mlp_bwd", grid=(t // tm,),
        in_specs=[pl.BlockSpec((tm, D), tok), pl.BlockSpec((tm, D), tok), pl.BlockSpec((tm, FF), tok),
                  pl.BlockSpec((FF, D), full, pipeline_mode=once), pl.BlockSpec((FF, D), full, pipeline_mode=once),
                  pl.BlockSpec((tm, D), tok), pl.BlockSpec((1, D), full)],
        out_specs=[pl.BlockSpec((tm, FF), tok), pl.BlockSpec((tm, FF), tok), pl.BlockSpec((tm, D), tok),
                   pl.BlockSpec((tm, D), tok), pl.BlockSpec((8, D), full)],
        out_shape=[jax.ShapeDtypeStruct((t, FF), BF16), jax.ShapeDtypeStruct((t, FF), BF16),
                   jax.ShapeDtypeStruct((t, D), F32), jax.ShapeDtypeStruct((t, D), BF16),
                   jax.ShapeDtypeStruct((8, D), F32)],
        compiler_params=_cparams(("arbitrary",)))(dx3, dx3b, h1b, w_down, w_up_t, x2, g)


def _attn_bwd(x1, dx2, dx2b, qb, kb, vb, w_xo, w_xq, w_out, g, riders=()):
    t = x1.shape[0]
    tm = min(TM, t)

    def body(x_ref, dx2_ref, dx2b_ref, q_ref, k_ref, v_ref, wx_ref, wq_ref, wo_ref, g_ref,
             dx1_ref, dx1b_ref, dy_ref, dq_ref, dkv_ref, dg_ref):
        @pl.when(pl.program_id(0) == 0)
        def _():
            dkv_ref[...] = jnp.zeros_like(dkv_ref)
            dg_ref[...] = jnp.zeros_like(dg_ref)

        datt = _dot_nt(dx2b_ref[...], wx_ref[...]).astype(BF16)
        for h in range(NH):
            hs = slice(h * XD, (h + 1) * XD)
            q_h, k_h, v_h, da_h = q_ref[:, hs], k_ref[:, hs], v_ref[:, hs], datt[:, hs]
            p = _softmax_head(q_h, k_h)
            dp = _dot_nt(da_h, v_h)
            ds = (p * (dp - jnp.sum(dp * p, axis=-1, keepdims=True)) * (1.0 / 16.0)).astype(BF16)
            dq_ref[:, hs] = _dot(ds, k_h).astype(BF16)
            dkv_ref[:, hs] += _dot_tn(ds, q_h)
            dkv_ref[:, D + h * XD:D + (h + 1) * XD] += _dot_tn(p, da_h)
        dxn = _dot_nt(dq_ref[...], wq_ref[...])
        x = x_ref[...]
        r = lax.rsqrt(jnp.mean(x * x, axis=-1, keepdims=True) + EPS)
        dx, dg = _rms_bwd(x, r, g_ref[...], dxn)
        dx1 = dx2_ref[...] + dx
        dx1_ref[...] = dx1
        dx1b = dx1.astype(BF16)
        dx1b_ref[...] = dx1b
        dy_ref[...] = _dot_nt(dx1b, wo_ref[...])
        _acc_rows(dg_ref, dg)

    tok = lambda i: (i, 0)
    full = lambda i: (0, 0)
    return _call(
        body, "attn_bwd", (t // tm,),
        [pl.BlockSpec((tm, D), tok), pl.BlockSpec((tm, D), tok), pl.BlockSpec((tm, D), tok),
         pl.BlockSpec((tm, D), tok), pl.BlockSpec((NMEM, D), full), pl.BlockSpec((NMEM, D), full),
         pl.BlockSpec((D, D), full), pl.BlockSpec((D, D), full), pl.BlockSpec((D, D), full),
         pl.BlockSpec((1, D), full)],
        [pl.BlockSpec((tm, D), tok), pl.BlockSpec((tm, D), tok), pl.BlockSpec((tm, D), tok),
         pl.BlockSpec((tm, D), tok), pl.BlockSpec((NMEM, 2 * D), full), pl.BlockSpec((8, D), full)],
        [jax.ShapeDtypeStruct((t, D), F32), jax.ShapeDtypeStruct((t, D), BF16),
         jax.ShapeDtypeStruct((t, D), F32), jax.ShapeDtypeStruct((t, D), BF16),
         jax.ShapeDtypeStruct((NMEM, 2 * D), F32), jax.ShapeDtypeStruct((8, D), F32)], [],
        (x1, dx2, dx2b, qb, kb, vb, w_xo, w_xq, w_out, g), riders)


def _kv_bwd(dkv, memn, mem, g, w):
    def body(dkv_ref, mn_ref, m_ref, g_ref, w_ref, dw_ref, dg_ref):
        dkvb = dkv_ref[...].astype(BF16)
        dmn = jnp.zeros((NMEM, D), F32)
        for j in range(NDEV):
            cols = slice(j * XKV_SHARD, (j + 1) * XKV_SHARD)
            dw_ref[j] = _dot_tn(mn_ref[...], dkvb[:, cols])
            dmn += _dot_nt(dkvb[:, cols], w_ref[j])
        m = m_ref[...]
        r = lax.rsqrt(jnp.mean(m * m, axis=-1, keepdims=True) + EPS)
        dg_ref[...] = jnp.broadcast_to(jnp.sum(dmn * m * r, axis=0, keepdims=True), dg_ref.shape)

    return pl.pallas_call(
        body, name="kv_bwd",
        out_shape=[jax.ShapeDtypeStruct((NDEV, D, XKV_SHARD), F32), jax.ShapeDtypeStruct((8, D), F32)],
        compiler_params=pltpu.CompilerParams(vmem_limit_bytes=VMEM_LIMIT))(dkv, memn, mem, g, w)


def _inproj_bwd(dz, w_t, x, dx1, g, riders=()):
    t = x.shape[0]
    tm = min(TM, t)

    def body(dz_ref, w_ref, x_ref, dx1_ref, g_ref, gx_ref, dg_ref):
        @pl.when(pl.program_id(0) == 0)
        def _():
            dg_ref[...] = jnp.zeros_like(dg_ref)

        dh = jnp.dot(dz_ref[...], w_ref[...], preferred_element_type=F32)
        x = x_ref[...]
        r = lax.rsqrt(jnp.mean(x * x, axis=-1, keepdims=True) + EPS)
        dx, dg = _rms_bwd(x, r, g_ref[...], dh)
        gx_ref[...] = dx1_ref[...] + dx
        _acc_rows(dg_ref, dg)

    tok = lambda i: (i, 0)
    full = lambda i: (0, 0)
    return _call(
        body, "inproj_bwd", (t // tm,),
        [pl.BlockSpec((tm, ZC), tok), pl.BlockSpec((ZC, D), full), pl.BlockSpec((tm, D), tok),
         pl.BlockSpec((tm, D), tok), pl.BlockSpec((1, D), full)],
        [pl.BlockSpec((tm, D), tok), pl.BlockSpec((8, D), full)],
        [jax.ShapeDtypeStruct((t, D), F32), jax.ShapeDtypeStruct((8, D), F32)], [], (dz, w_t, x, dx1, g), riders)


def _matmul_tn(a, b, name, rows=None, riders=()):
    t, k = a.shape
    n = b.shape[1]
    tk, tn = [1024 if size % 1024 == 0 else 640 for size in (k, n)]
    tt = min(TT, t)
    rows = rows or k

    def body(a_ref, b_ref, o_ref):
        @pl.when(pl.program_id(2) == 0)
        def _():
            o_ref[...] = jnp.zeros_like(o_ref)

        o_ref[...] += _dot_tn(a_ref[...], b_ref[...])

    return _call(
        body, name, (k // tk, n // tn, t // tt),
        [pl.BlockSpec((tt, tk), lambda i, j, s: (s, i)), pl.BlockSpec((tt, tn), lambda i, j, s: (s, j))],
        [pl.BlockSpec((tk, tn), lambda i, j, s: (i, j))], [jax.ShapeDtypeStruct((rows, n), F32)], [], (a, b), riders)


def _lane_head(shape, dim, shift):
    return _iota(shape, dim) >> shift


CUM_ROWS = 128


def _chunk_cumsum(x, upper, n):
    r, c = _iota((CUM_ROWS, CUM_ROWS), 0), _iota((CUM_ROWS, CUM_ROWS), 1)
    tri = (c >= r) if upper else (c <= r)
    cum = jnp.where(((r >> 6) == (c >> 6)) & tri, 1.0, 0.0).astype(BF16)
    return jnp.concatenate([_dot_exact_lhs(cum, x[g:g + CUM_ROWS], n) for g in range(0, x.shape[0], CUM_ROWS)],
                           axis=0)


def _gla_recompute(q_raw, k, lr, wpad, bias, rev, tb):
    pre = _dot(lr, wpad) + bias
    la = (jnp.minimum(pre, 0.0) - jnp.log(1.0 + jnp.exp(-jnp.abs(pre)))) * (1.0 / 16.0)
    b = _chunk_cumsum(la, rev, 3)
    e, ei = jnp.exp(b), jnp.exp(-b)
    qt = (q_raw * 0.125) * e
    kt = k * ei
    return pre, b, e, ei, qt, kt


def _stack_heads(x, shift):
    head = _lane_head(x.shape, 1, shift)
    return jnp.concatenate([jnp.where(head == h, x, 0.0) for h in range(NH)], axis=0).astype(BF16)


def _fold_heads(x, shift):
    head = _lane_head((CH, x.shape[1]), 1, shift)
    return functools.reduce(lambda a, b: a + b,
                            [jnp.where(head == h, x[h * CH:(h + 1) * CH], 0.0) for h in range(NH)])


def _wide_mask(rev):
    r, s = _iota((CH, NH * CH), 0), _iota((CH, NH * CH), 1) & (CH - 1)
    return (s >= r) if rev else (s <= r)


def _rows_by_head(x):
    w = x.shape[1] // NH
    return jnp.concatenate([x[:, h * w:(h + 1) * w] for h in range(NH)], axis=0)


def _lanes_by_head(x):
    return jnp.concatenate([x[h * CH:(h + 1) * CH] for h in range(NH)], axis=1)


def _state_compact(xt):
    head = _lane_head((128, GK), 1, 6)
    return functools.reduce(lambda a, b: a + b,
                            [jnp.where(head == h, xt[h * 128:(h + 1) * 128], 0.0) for h in range(NH)])


def _conv_parts(cb, cc, cu, ccp, cup, ccn, cun, cw_ref, first, last, tb):
    h = cc * cu
    hp = jnp.where(first, 0.0, ccp * cup)
    hn = jnp.where(last, 0.0, ccn * cun)
    rows = _iota(h.shape, 0)
    h_m1 = jnp.where(rows == 0, hp, pltpu.roll(h, 1, 0))
    h_p1 = jnp.where(rows == tb - 1, hn, pltpu.roll(h, tb - 1, 0))
    conv = cw_ref[pl.ds(0, 1), :] * h_m1 + cw_ref[pl.ds(1, 1), :] * h + cw_ref[pl.ds(2, 1), :] * h_p1
    return h, h_m1, h_p1, conv


def _head_sum(x, w_in, w_out):
    shape, sh_in, sh_out = (2 * w_in, 2 * w_out), w_in.bit_length() - 1, w_out.bit_length() - 1
    sel = jnp.where((_iota(shape, 0) >> sh_in) == (_iota(shape, 1) >> sh_out), 1.0, 0.0).astype(BF16)
    return jnp.concatenate([_dot_exact_rhs(x[:, s:s + 2 * w_in], sel, 2) for s in range(0, NH * w_in, 2 * w_in)],
                           axis=1)


def _group_sum(x):
    ones = jnp.where((_iota((128, 128), 0) >> 6) == (_iota((128, 128), 1) >> 6), 1.0, 0.0).astype(BF16)
    return jnp.concatenate([_dot_exact_rhs(x[:, s:s + 128], ones, 2) for s in range(0, x.shape[1], 128)], axis=1)


def _head_norm(o):
    ons, rs = [], []
    for h in range(NH):
        slab = o[:, h * 128:(h + 1) * 128]
        r = lax.rsqrt(jnp.mean(slab * slab, axis=-1, keepdims=True) + EPS)
        ons.append(slab * r)
        rs.append(jnp.broadcast_to(r, slab.shape))
    return jnp.concatenate(ons, axis=1), jnp.concatenate(rs, axis=1)


def _zspec(tb, width, blk, jmap):
    return pl.BlockSpec((tb, width), lambda i: (jmap(i), blk))


def _halo_specs(tb, nblk, t, blk, jmap):
    prev = pl.BlockSpec((8, CW), lambda i: (jnp.maximum(jmap(i) * (tb // 8) - 1, 0), blk))
    nxt = pl.BlockSpec((8, CW), lambda i: (jnp.minimum((jmap(i) + 1) * (tb // 8), t // 8 - 1), blk))
    return prev, nxt


def _gla_fwd_block(q_ref, k_ref, v_ref, lr_ref, w_ref, bias_ref, o_ref, sd_ref, st, b_scr, rev, tb):
    nb = tb // CH
    _, b, _, _, qt, kt = _gla_recompute(q_ref[...], k_ref[...], lr_ref[...], w_ref[...], bias_ref[...], rev, tb)
    v = v_ref[...]
    b_scr[...] = b
    yield
    maskw = _wide_mask(rev)
    order = list(reversed(range(nb))) if rev else list(range(nb))
    rows = [slice(c * CH, (c + 1) * CH) for c in range(nb)]
    state = st[...]
    for c in order:
        gdec = jnp.exp(b_scr[pl.ds(c * CH + (0 if rev else CH - 1), 1), :])
        sd_ref[c] = state
        a = jnp.where(maskw, _dot_nt(qt[rows[c]], _stack_heads(kt[rows[c]], 6)), 0.0)
        o_inter = _lanes_by_head(_dot_nt(_stack_heads(qt[rows[c]], 6), state))
        o_ref[pl.ds(c * CH, CH), :] = _dot(a, _stack_heads(v[rows[c]], 7)) + o_inter
        state = state * gdec + _state_compact(_dot_tn(v[rows[c]], kt[rows[c]] * gdec))
        yield
    st[...] = state
    yield


def _gla_fwd(z, waf_pad, b_af, wab_pad, b_ab, riders=()):
    t = z.shape[0]
    tb = min(TB, t)
    nblk, nb = t // tb, tb // CH
    jmaps = (lambda i: i, lambda i: nblk - 1 - i)

    def body(qf, kf, vf, lrf, qr, kr, vr, lrr, wf, bf, wr, br, of_ref, sdf_ref, or_ref, sdr_ref,
             st_f, st_r, b_f, b_r):
        @pl.when(pl.program_id(0) == 0)
        def _():
            st_f[...] = jnp.zeros_like(st_f)
            st_r[...] = jnp.zeros_like(st_r)

        for _ in zip(_gla_fwd_block(qf, kf, vf, lrf, wf, bf, of_ref, sdf_ref, st_f, b_f, False, tb),
                     _gla_fwd_block(qr, kr, vr, lrr, wr, br, or_ref, sdr_ref, st_r, b_r, True, tb)):
            pass

    full = lambda i: (0, 0)
    zspecs = [s for jm in jmaps for s in (_zspec(tb, GK, ZB_Q, jm), _zspec(tb, GK, ZB_K, jm),
                                         _zspec(tb, GV, ZB_V, jm), _zspec(tb, 128, ZB_LR, jm))]
    wspecs = [pl.BlockSpec((128, GK), full), pl.BlockSpec((1, GK), full)] * 2
    out_specs = [s for jm in jmaps for s in (pl.BlockSpec((tb, GV), lambda i, jm=jm: (jm(i), 0)),
                                             pl.BlockSpec((nb, 128, GK), lambda i, jm=jm: (jm(i), 0, 0)))]
    out_shape = [jax.ShapeDtypeStruct((t, GV), F32), jax.ShapeDtypeStruct((t // CH, 128, GK), F32)] * 2
    scratch = [pltpu.VMEM((128, GK), F32), pltpu.VMEM((128, GK), F32), pltpu.VMEM((tb, GK), F32),
               pltpu.VMEM((tb, GK), F32)]
    return _call(body, "gla_fwd", (nblk,), zspecs + wspecs, out_specs, out_shape, scratch,
                 [z] * 8 + [waf_pad, b_af, wab_pad, b_ab], riders)


def _gla_bwd_chunks(do_ref, sd_ref, dst, b_scr, db_scr, dq_ref, dk_ref, dv_ref, qt, kt, e, ei, v, rev, nb):
    maskw = _wide_mask(rev)
    for c in (range(nb) if rev else reversed(range(nb))):
        sl = slice(c * CH, (c + 1) * CH)
        grow = c * CH + (0 if rev else CH - 1)
        gdec = jnp.exp(b_scr[pl.ds(grow, 1), :])
        qt_c, kt_c, v_c, do_c = qt[sl], kt[sl], v[sl], do_ref[pl.ds(c * CH, CH), :]
        s_in, ds_out = sd_ref[c], dst[...]
        kbd, vbd = _stack_heads(kt_c, 6), _stack_heads(v_c, 7)
        a = jnp.where(maskw, _dot_nt(qt_c, kbd), 0.0)
        da = jnp.where(maskw, _dot_nt(do_c, vbd), 0.0)
        dv_ref[pl.ds(c * CH, CH), :] = (_fold_heads(_dot_tn(a, do_c), 7)
                                        + _lanes_by_head(_dot_nt(_stack_heads(kt_c * gdec, 6), ds_out)))
        dqt = _dot(da, kbd) + _fold_heads(_dot(_rows_by_head(do_c), s_in), 6)
        dkh = _fold_heads(_dot(_rows_by_head(v_c), ds_out), 6)
        da_do = jnp.concatenate([da.astype(BF16), do_c.astype(BF16)], axis=1)
        both = _dot_tn(da_do, qt_c)
        dkt = _fold_heads(both[:NH * CH], 6) + dkh * gdec
        dg = jnp.sum(ds_out * s_in, axis=0, keepdims=True) + jnp.sum(kt_c * dkh, axis=0, keepdims=True)
        db_scr[pl.ds(c * CH, CH), :] = dqt * qt_c - dkt * kt_c
        db_scr[pl.ds(grow, 1), :] += dg * gdec
        dq_ref[pl.ds(c * CH, CH), :] = dqt * e[sl] * 0.125
        dk_ref[pl.ds(c * CH, CH), :] = dkt * ei[sl]
        dst[...] = ds_out * gdec + _state_compact(both[NH * CH:])
        yield


def _gate_bwd(db, pre, lr, wpad, rev, tb):
    dla = _chunk_cumsum(db, not rev, 2)
    dpre = dla * (1.0 / 16.0) / (1.0 + jnp.exp(pre))
    return dpre, _dot_nt(dpre, wpad), _dot_tn(lr, dpre)


def _gla_bwd_first(z, dy, o_pre, sd, wpad, bias, conv_w, conv_norm, gla_norm4, riders=()):
    t = z.shape[0]
    tb = min(TB_BWD, t)
    nblk, nb = t // tb, tb // CH
    jmap = lambda i: nblk - 1 - i

    def body(q_ref, k_ref, v_ref, lr_ref, g_ref, cb_ref, cc_ref, cu_ref, ccp_ref, ccn_ref, cup_ref, cun_ref,
             dy_ref, opre_ref, sd_ref, w_ref, bias_ref, cw_ref, cn_ref, gn_ref,
             do_ref, dq_ref, dk_ref, dv_ref, dlr_ref, dzg_ref, dzcb_ref, dconv_ref,
             dw_ref, dbias_ref, dcw_ref, dcn_ref, dgn_ref, dst, b_scr, db_scr):
        i = pl.program_id(0)
        j = jmap(i)

        @pl.when(i == 0)
        def _():
            dst[...] = jnp.zeros_like(dst)
            for ref in (dw_ref, dbias_ref, dcw_ref, dcn_ref, dgn_ref):
                ref[...] = jnp.zeros_like(ref)

        dyg = dy_ref[:, CW:]
        g = g_ref[...]
        sig = _sigmoid(g)
        on, rr = _head_norm(opre_ref[...])
        gn = gn_ref[...]
        dzg_ref[...] = (dyg * on * gn * (sig * (1.0 + g * (1.0 - sig)))).astype(BF16)
        don = dyg * (g * sig)
        _acc_rows(dgn_ref, jnp.sum(don * on, axis=0, keepdims=True))
        u = don * gn
        uo = u * on
        mean_uo = jnp.concatenate(
            [jnp.broadcast_to(jnp.mean(uo[:, h * 128:(h + 1) * 128], axis=-1, keepdims=True), (tb, 128))
             for h in range(NH)], axis=1)
        do_ref[...] = rr * (u - on * mean_uo)

        def conv_branch():
            cb = cb_ref[...]
            h, h_m1, h_p1, conv = _conv_parts(cb, cc_ref[...], cu_ref[...], ccp_ref[pl.ds(7, 1), :],
                                              cup_ref[pl.ds(7, 1), :], ccn_ref[pl.ds(0, 1), :],
                                              cun_ref[pl.ds(0, 1), :], cw_ref, j == 0, j == nblk - 1, tb)
            yc = cb * conv
            yield
            rc = lax.rsqrt(_group_sum(yc * yc) * (1.0 / 64.0) + EPS)
            ycr = yc * rc
            yield
            dyn = dy_ref[:, :CW]
            _acc_rows(dcn_ref, jnp.sum(dyn * ycr, axis=0, keepdims=True))
            uc = dyn * cn_ref[...]
            yield
            dyc = rc * (uc - ycr * (_group_sum(uc * ycr) * (1.0 / 64.0)))
            dzcb_ref[...] = (dyc * conv).astype(BF16)
            yield
            dconv = dyc * cb
            dconv_ref[...] = dconv
            yield
            dcw_ref[pl.ds(0, 1), :] += jnp.sum(dconv * h_m1, axis=0, keepdims=True)
            dcw_ref[pl.ds(1, 1), :] += jnp.sum(dconv * h, axis=0, keepdims=True)
            dcw_ref[pl.ds(2, 1), :] += jnp.sum(dconv * h_p1, axis=0, keepdims=True)
            yield

        lr, wp = lr_ref[...], w_ref[...]
        pre, b, e, ei, qt, kt = _gla_recompute(q_ref[...], k_ref[...], lr, wp, bias_ref[...], False, tb)
        b_scr[...] = b
        for _ in itertools.zip_longest(
                _gla_bwd_chunks(do_ref, sd_ref, dst, b_scr, db_scr, dq_ref, dk_ref, dv_ref, qt, kt, e, ei, v_ref[...],
                                False, nb), conv_branch()):
            pass
        dpre, dlr, dw = _gate_bwd(db_scr[...], pre, lr, wp, False, tb)
        dlr_ref[...] = dlr
        dw_ref[...] += dw
        _acc_rows(dbias_ref, jnp.sum(dpre, axis=0, keepdims=True))

    full = lambda i: (0, 0)
    tokv = pl.BlockSpec((tb, GV), lambda i: (jmap(i), 0))
    tokk = pl.BlockSpec((tb, GK), lambda i: (jmap(i), 0))
    ccp, ccn = _halo_specs(tb, nblk, t, ZB_CC, jmap)
    cup, cun = _halo_specs(tb, nblk, t, ZB_CU, jmap)
    in_specs = [_zspec(tb, GK, ZB_Q, jmap), _zspec(tb, GK, ZB_K, jmap), _zspec(tb, GV, ZB_V, jmap),
                _zspec(tb, 128, ZB_LR, jmap), _zspec(tb, GV, ZB_G, jmap), _zspec(tb, CW, ZB_CB, jmap),
                _zspec(tb, CW, ZB_CC, jmap), _zspec(tb, CW, ZB_CU, jmap), ccp, ccn, cup, cun,
                pl.BlockSpec((tb, D), lambda i: (jmap(i), 0)), tokv,
                pl.BlockSpec((nb, 128, GK), lambda i: (jmap(i), 0, 0)), pl.BlockSpec((128, GK), full),
                pl.BlockSpec((1, GK), full), pl.BlockSpec((3, CW), full), pl.BlockSpec((1, CW), full),
                pl.BlockSpec((1, GV), full)]
    out_specs = [tokv, tokk, tokk, tokv, pl.BlockSpec((tb, 128), lambda i: (jmap(i), 0)), tokv, tokv, tokv,
                 pl.BlockSpec((128, GK), full), pl.BlockSpec((8, GK), full), pl.BlockSpec((8, CW), full),
                 pl.BlockSpec((8, CW), full), pl.BlockSpec((8, GV), full)]
    out_shape = [jax.ShapeDtypeStruct((t, GV), F32), jax.ShapeDtypeStruct((t, GK), F32),
                 jax.ShapeDtypeStruct((t, GK), F32), jax.ShapeDtypeStruct((t, GV), F32),
                 jax.ShapeDtypeStruct((t, 128), F32), jax.ShapeDtypeStruct((t, GV), BF16),
                 jax.ShapeDtypeStruct((t, CW), BF16), jax.ShapeDtypeStruct((t, CW), F32),
                 jax.ShapeDtypeStruct((128, GK), F32), jax.ShapeDtypeStruct((8, GK), F32),
                 jax.ShapeDtypeStruct((8, CW), F32), jax.ShapeDtypeStruct((8, CW), F32),
                 jax.ShapeDtypeStruct((8, GV), F32)]
    return _call(
        body, "gla_bwd_first", (nblk,), in_specs, out_specs, out_shape,
        [pltpu.VMEM((128, GK), F32), pltpu.VMEM((tb, GK), F32), pltpu.VMEM((tb, GK), F32)],
        (z, z, z, z, z, z, z, z, z, z, z, z, dy, o_pre, sd, wpad, bias, conv_w, conv_norm, gla_norm4), riders)


def _gla_bwd_second(z, do, sd, wpad, bias, dqa, dka, dva, dlra, dzg, dzcb, dconv, conv_w, riders=()):
    t = z.shape[0]
    tb = min(TB_BWD, t)
    nblk, nb = t // tb, tb // CH
    jmap = lambda i: i

    def body(q_ref, k_ref, v_ref, lr_ref, cc_ref, cu_ref, do_ref, sd_ref, w_ref, bias_ref, dqa_ref, dka_ref,
             dva_ref, dlra_ref, dzg_ref, dzcb_ref, dc_ref, dcp_ref, dcn_ref, cw_ref,
             dz_ref, dw_ref, dbias_ref, dst, b_scr, db_scr, dq_scr, dk_scr, dv_scr, sb_scr, dsk_scr):
        i = pl.program_id(0)

        @pl.when(i == 0)
        def _():
            dst[...] = jnp.zeros_like(dst)
            dw_ref[...] = jnp.zeros_like(dw_ref)
            dbias_ref[...] = jnp.zeros_like(dbias_ref)

        q_raw, k, v, lr, wp = q_ref[...], k_ref[...], v_ref[...], lr_ref[...], w_ref[...]
        pre, b, e, ei, qt, kt = _gla_recompute(q_raw, k, lr, wp, bias_ref[...], True, tb)
        b_scr[...] = b

        def token_local():
            dc = dc_ref[...]
            rows = _iota(dc.shape, 0)
            dprev = jnp.where(i == 0, 0.0, dcp_ref[pl.ds(7, 1), :])
            dnext = jnp.where(i == nblk - 1, 0.0, dcn_ref[pl.ds(0, 1), :])
            dc_m1 = jnp.where(rows == 0, dprev, pltpu.roll(dc, 1, 0))
            dc_p1 = jnp.where(rows == tb - 1, dnext, pltpu.roll(dc, tb - 1, 0))
            yield
            dh = cw_ref[pl.ds(0, 1), :] * dc_p1 + cw_ref[pl.ds(1, 1), :] * dc + cw_ref[pl.ds(2, 1), :] * dc_m1
            dz_ref[:, 0:512] = dzcb_ref[...]
            yield
            dz_ref[:, 512:1024] = (dh * cu_ref[...]).astype(BF16)
            dz_ref[:, 1024:1536] = (dh * cc_ref[...]).astype(BF16)
            dz_ref[:, 2560:3072] = dzg_ref[...]
            yield
            sb_scr[...] = _head_sum((q_raw * 0.125) * k, 64, 128)
            yield
            dsk_scr[...] = _head_sum(do_ref[...] * v, 128, 64)
            yield

        for _ in itertools.zip_longest(
                _gla_bwd_chunks(do_ref, sd_ref, dst, b_scr, db_scr, dq_scr, dk_scr, dv_scr, qt, kt, e, ei, v, True, nb),
                token_local()):
            pass
        dpre, dlr, dw = _gate_bwd(db_scr[...], pre, lr, wp, True, tb)
        dw_ref[...] += dw
        _acc_rows(dbias_ref, jnp.sum(dpre, axis=0, keepdims=True))
        dsk = dsk_scr[...]
        dz_ref[:, 1536:1792] = (dqa_ref[...] + dq_scr[...] - dsk * k * 0.125).astype(BF16)
        dz_ref[:, 1792:2048] = (dka_ref[...] + dk_scr[...] - dsk * (q_raw * 0.125)).astype(BF16)
        dz_ref[:, 2048:2560] = (dva_ref[...] + dv_scr[...] - sb_scr[...] * do_ref[...]).astype(BF16)
        dz_ref[:, 3072:3200] = (dlra_ref[...] + dlr).astype(BF16)

    full = lambda i: (0, 0)
    tokv = pl.BlockSpec((tb, GV), lambda i: (i, 0))
    tokk = pl.BlockSpec((tb, GK), lambda i: (i, 0))
    dcp = pl.BlockSpec((8, CW), lambda i: (jnp.maximum(i * (tb // 8) - 1, 0), 0))
    dcn = pl.BlockSpec((8, CW), lambda i: (jnp.minimum((i + 1) * (tb // 8), t // 8 - 1), 0))
    in_specs = [_zspec(tb, GK, ZB_Q, jmap), _zspec(tb, GK, ZB_K, jmap), _zspec(tb, GV, ZB_V, jmap),
                _zspec(tb, 128, ZB_LR, jmap), _zspec(tb, CW, ZB_CC, jmap), _zspec(tb, CW, ZB_CU, jmap), tokv,
                pl.BlockSpec((nb, 128, GK), lambda i: (i, 0, 0)), pl.BlockSpec((128, GK), full),
                pl.BlockSpec((1, GK), full), tokk, tokk, tokv, pl.BlockSpec((tb, 128), lambda i: (i, 0)), tokv, tokv,
                tokv, dcp, dcn, pl.BlockSpec((3, CW), full)]
    out_specs = [pl.BlockSpec((tb, ZC), lambda i: (i, 0)), pl.BlockSpec((128, GK), full), pl.BlockSpec((8, GK), full)]
    out_shape = [jax.ShapeDtypeStruct((t, ZC), BF16), jax.ShapeDtypeStruct((128, GK), F32),
                 jax.ShapeDtypeStruct((8, GK), F32)]
    return _call(
        body, "gla_bwd_second", (nblk,), in_specs, out_specs, out_shape,
        [pltpu.VMEM((128, GK), F32), pltpu.VMEM((tb, GK), F32), pltpu.VMEM((tb, GK), F32),
         pltpu.VMEM((tb, GK), F32), pltpu.VMEM((tb, GK), F32), pltpu.VMEM((tb, GV), F32),
         pltpu.VMEM((tb, GV), F32), pltpu.VMEM((tb, GK), F32)],
        (z, z, z, z, z, z, do, sd, wpad, bias, dqa, dka, dva, dlra, dzg, dzcb, dconv, dconv, dconv, conv_w), riders)


def _step(x, mem, target, shard, small_pack, vec, place):
    own, from_chips = {}, {}

    def pair_sums(names, g4, from_sibling):
        pbs = {}
        for shape in dict.fromkeys(g.shape for g in g4):
            idx = [i for i, g in enumerate(g4) if g.shape == shape]
            pb, mine = _rs_pair_sum(place, [g4[i] for i in idx], [from_sibling[i] for i in idx],
                                    "pair_sum_" + "_".join(names[i] for i in idx))
            for i, b, o in zip(idx, pb, mine):
                pbs[i], own[names[i]] = b, o
        return [pbs[i] for i in range(len(g4))]

    def by_dest(g, n):
        return g.reshape((4, 2) + shard[n].shape)

    w_in, small_all = _exchange(_gather_rider([shard["w_in"], small_pack]), "gather_w_in")
    w_in = jnp.pad(w_in.reshape(ZW, D), ((0, ZC - ZW), (0, 0)))
    small_all = small_all.reshape(NDEV, -1)
    p, off = {}, 0
    for n, (r, c) in SMALL_SHARDED.items():
        p[n] = small_all[:, off:off + r * c].reshape(NDEV, r, c).transpose(1, 0, 2).reshape(r, NDEV * c)
        off += r * c
    zeros_lr = jnp.zeros((128 - LR, GK), BF16)
    waf_pad = jnp.concatenate([p["w_af"].astype(BF16), zeros_lr], axis=0)
    wab_pad = jnp.concatenate([jnp.zeros((LR, GK), BF16), p["w_ab"].astype(BF16), zeros_lr[:128 - 2 * LR]], axis=0)
    gla_norm4 = jnp.tile(vec["gla_norm"], (1, NH))

    z, hb, w_out, w_xq, w_xo, w_xkv = _inproj(
        x, vec["mix_norm"], w_in, [_gather_rider([shard[n] for n in ("w_out", "w_xq", "w_xo", "w_xkv")])])
    w_out, w_xq, w_xo = [a.reshape(D, D) for a in (w_out, w_xq, w_xo)]
    o_f, sd_f, o_b, sd_b, w_up_t, w_down = _gla_fwd(
        z, waf_pad, vec["b_af"], wab_pad, vec["b_ab"],
        [_gather_rider([shard["w_up"], shard["w_down"]], early_relay=False)])
    w_up_t, w_down = w_up_t.reshape(FF, D), w_down.reshape(FF, D)
    kv, memn = _kv_proj(mem, vec["mem_norm"], w_xkv)
    kb, vb = kv[:, :D].astype(BF16), kv[:, D:].astype(BF16)
    x1, x2, xn1, qb, attb, yb, o_pre = _attn_fwd(x, z, o_f, o_b, p["conv_w"], vec["conv_norm"], gla_norm4, w_out,
                                                 vec["xa_norm"], w_xq, kb, vb, w_xo)
    h1b, xn2, dx3, dx3b, loss8, dfinal = _mlp_fwd(x2, vec["mlp_norm"], w_up_t, w_down, vec["final_norm"], target)

    ab, dh1b, dx2, dx2b, dmlp = _mlp_bwd(dx3, dx3b, h1b, w_down, w_up_t, x2, vec["mlp_norm"])
    g_mlp = [by_dest(_matmul_tn(ab, dx3b, "dw_down")[0], "w_down"),
             by_dest(_matmul_tn(dh1b, xn2, "dw_up")[0], "w_up")]
    dx1, dx1b, dy, dqb, dkv, dxa, *s_mlp = _attn_bwd(x1, dx2, dx2b, qb, kb, vb, w_xo, w_xq, w_out, vec["xa_norm"],
                                                     riders=[_sibling_rider(g_mlp)])
    pb_mlp = pair_sums(("w_down", "w_up"), g_mlp, s_mlp)
    dw_xo = _matmul_tn(attb, dx2b, "dw_xo")[0]
    dw_xkv, dmemn = _kv_bwd(dkv, memn, mem, vec["mem_norm"], w_xkv)
    att_names = ("w_xo", "w_xq", "w_out", "w_xkv")
    g_att = [by_dest(g, n) for g, n in zip(
        (dw_xo, _matmul_tn(xn1, dqb, "dw_xq")[0], _matmul_tn(yb, dx1b, "dw_out")[0], dw_xkv), att_names)]
    res = _gla_bwd_first(z, dy, o_pre, sd_f, waf_pad, vec["b_af"], p["conv_w"], vec["conv_norm"], gla_norm4,
                         riders=[_chips_rider(pb_mlp), _sibling_rider(g_att)])
    do, dqa, dka, dva, dlra, dzg, dzcb, dconv, dwaf, dbaf, dcw, dcn, dgn = res[:13]
    from_chips["w_down"], from_chips["w_up"] = res[13:15]
    pb_att = pair_sums(att_names, g_att, res[15:])
    dz, dwab, dbab, *c_att = _gla_bwd_second(z, do, sd_b, wab_pad, vec["b_ab"], dqa, dka, dva, dlra, dzg, dzcb, dconv,
                                             p["conv_w"], riders=[_chips_rider(pb_att)])
    from_chips.update(zip(att_names, c_att))
    g_in = [by_dest(_matmul_tn(dz, hb, "dw_in", rows=ZW)[0], "w_in")]
    pb_in = pair_sums(("w_in",), g_in, _exchange(_sibling_rider(g_in), "grads_to_sibling_w_in"))
    grad_x, dmix, from_chips["w_in"] = _inproj_bwd(dz, w_in, x, dx1, vec["mix_norm"], riders=[_chips_rider(pb_in)])

    small_acc = dict(mix_norm=dmix, conv_w=dcw, conv_norm=dcn, w_af=dwaf, b_af=dbaf, w_ab=dwab, b_ab=dbab,
                     gla_norm=dgn, xa_norm=dxa, mem_norm=dmemn, mlp_norm=dmlp, final_norm=dfinal)
    return loss8, grad_x, small_acc, own, from_chips


def _place():
    return lax.axis_index("x"), lax.axis_index("y"), lax.axis_index("c")


class _Rider:
    def __init__(self, arrays, out_shape, scratch, start, finish, relay=None):
        self.arrays, self.out_shape, self.scratch, self.start, self.finish = arrays, out_shape, scratch, start, finish
        self.relay = relay


def _gather_rider(blks, early_relay=True):
    n = len(blks)

    def plan(in_refs, out_refs, sems):
        send_sems, recv_sems, local_sems = sems
        x, y, c = _place()
        me, sibling = (x, y, c), (x, y, 1 - c)
        chips = [(1 - x, y, c), (x, 1 - y, c), (1 - x, 1 - y, c)]

        def copy(a, k, block, to, own=False):
            px, py, pc = block
            dst = out_refs[a].at[4 * px + 2 * py + pc]
            return pltpu.make_async_remote_copy(
                src_ref=in_refs[a] if own else dst, dst_ref=dst, send_sem=send_sems.at[k, a],
                recv_sem=recv_sems.at[k, a], device_id=to, device_id_type=MESH)

        def local(a):
            return pltpu.make_async_copy(in_refs[a], out_refs[a].at[4 * x + 2 * y + c], local_sems.at[a])

        def own_sends(a):
            return [copy(a, 0, me, sibling, own=True)] + [copy(a, 1 + j, me, chip, own=True)
                                                          for j, chip in enumerate(chips)]

        return copy, local, own_sends, me, sibling, chips

    def start(in_refs, out_refs, sems):
        _, local, own_sends, _, _, _ = plan(in_refs, out_refs, sems)
        for a in range(n):
            local(a).start()
            for cp in own_sends(a):
                cp.start()

    def relay(in_refs, out_refs, sems):
        copy, _, _, me, sibling, chips = plan(in_refs, out_refs, sems)
        for j, chip in enumerate(chips):
            for a in range(n):
                copy(a, 1 + j, chip, me).wait_recv()
                copy(a, 4 + j, chip, sibling).start()

    def finish(in_refs, out_refs, sems):
        if not early_relay:
            relay(in_refs, out_refs, sems)
        copy, local, own_sends, me, sibling, chips = plan(in_refs, out_refs, sems)
        for a in range(n):
            copy(a, 0, sibling, me).wait_recv()
            for j, (px, py, pc) in enumerate(chips):
                copy(a, 4 + j, (px, py, 1 - pc), me).wait_recv()
            for cp in own_sends(a) + [copy(a, 4 + j, chip, sibling) for j, chip in enumerate(chips)]:
                cp.wait_send()
            local(a).wait()

    return _Rider(blks, [jax.ShapeDtypeStruct((NDEV,) + b.shape, b.dtype) for b in blks],
                  [pltpu.SemaphoreType.DMA((7, n)), pltpu.SemaphoreType.DMA((7, n)), pltpu.SemaphoreType.DMA((n,))],
                  start, finish, relay if early_relay else None)


def _sibling_rider(g4s):
    n = len(g4s)

    def copies(in_refs, out_refs, sems):
        send_sems, recv_sems = sems
        x, y, c = _place()
        return [pltpu.make_async_remote_copy(
            src_ref=in_refs[a].at[k, 1 - c], dst_ref=out_refs[a].at[k], send_sem=send_sems.at[k, a],
            recv_sem=recv_sems.at[k, a], device_id=(x, y, 1 - c), device_id_type=MESH)
            for a in range(n) for k in range(4)]

    def start(in_refs, out_refs, sems):
        for cp in copies(in_refs, out_refs, sems):
            cp.start()

    def finish(in_refs, out_refs, sems):
        for cp in copies(in_refs, out_refs, sems):
            cp.wait()

    return _Rider(g4s, [jax.ShapeDtypeStruct((4,) + g.shape[2:], g.dtype) for g in g4s],
                  [pltpu.SemaphoreType.DMA((4, n)), pltpu.SemaphoreType.DMA((4, n))], start, finish)


def _chips_rider(pbs):
    n = len(pbs)

    def copies(in_refs, out_refs, sems):
        send_sems, recv_sems = sems
        x, y, c = _place()
        peers = [(1 - x, y), (x, 1 - y), (1 - x, 1 - y)]
        return [pltpu.make_async_remote_copy(
            src_ref=in_refs[a].at[2 * px + py], dst_ref=out_refs[a].at[k], send_sem=send_sems.at[k, a],
            recv_sem=recv_sems.at[k, a], device_id=(px, py, c), device_id_type=MESH)
            for a in range(n) for k, (px, py) in enumerate(peers)]

    def start(in_refs, out_refs, sems):
        for cp in copies(in_refs, out_refs, sems):
            cp.start()

    def finish(in_refs, out_refs, sems):
        for cp in copies(in_refs, out_refs, sems):
            cp.wait()

    return _Rider(pbs, [jax.ShapeDtypeStruct((3,) + p.shape[1:], p.dtype) for p in pbs],
                  [pltpu.SemaphoreType.DMA((3, n)), pltpu.SemaphoreType.DMA((3, n))], start, finish)


def _exchange(rider, name):
    n_in, n_out = len(rider.arrays), len(rider.out_shape)

    def body(*refs):
        ins, outs, sems = refs[:n_in], refs[n_in:n_in + n_out], refs[n_in + n_out:]
        rider.start(ins, outs, sems)
        if rider.relay:
            rider.relay(ins, outs, sems)
        rider.finish(ins, outs, sems)

    hbm = pl.BlockSpec(memory_space=pltpu.HBM)
    return pl.pallas_call(body, name=name, out_shape=rider.out_shape, in_specs=[hbm] * n_in,
                          out_specs=[hbm] * n_out, scratch_shapes=rider.scratch)(*rider.arrays)


def _rs_pair_sum(place, g4s, r1s, name):
    n = len(g4s)
    rows, cols = g4s[0].shape[2:]
    tr = min(rows, 512)

    def body(pl_ref, *refs):
        for g_ref, r_ref, pb_ref, own_ref in zip(refs[:n], refs[n:2 * n], refs[2 * n:3 * n], refs[3 * n:]):
            s = g_ref[0, 0] + r_ref[0]
            pb_ref[0] = s.astype(BF16)

            @pl.when(pl.program_id(1) == pl_ref[0])
            def _():
                own_ref[...] = s

    grid_spec = pltpu.PrefetchScalarGridSpec(
        num_scalar_prefetch=1, grid=(rows // tr, 4),
        in_specs=[pl.BlockSpec((1, 1, tr, cols), lambda r, k, p: (k, p[1], r, 0))] * n
        + [pl.BlockSpec((1, tr, cols), lambda r, k, p: (k, r, 0))] * n,
        out_specs=[pl.BlockSpec((1, tr, cols), lambda r, k, p: (k, r, 0))] * n
        + [pl.BlockSpec((tr, cols), lambda r, k, p: (r, 0))] * n)
    res = pl.pallas_call(
        body, name=name, grid_spec=grid_spec,
        out_shape=[jax.ShapeDtypeStruct((4, rows, cols), BF16)] * n + [jax.ShapeDtypeStruct((rows, cols), F32)] * n,
        compiler_params=_cparams(("arbitrary", "arbitrary")))(place, *g4s, *r1s)
    return res[:n], res[n:]


PACK_ROWS = 32
VEC_ROW = {"mix_norm": 0, "conv_norm": 1, "b_af": 2, "b_ab": 3, "gla_norm": 4, "xa_norm": 5, "mem_norm": 6,
           "mlp_norm": 7, "final_norm": 8}
LOSS_ROW, MAT_ROW = 9, 16
MAT_LANE = {"w_af": 0, "w_ab": GK, "conv_w": 2 * GK}
MAT_SRC_ROW = {"w_af": 0, "w_ab": LR, "conv_w": 0}


SMALL_WIDTH = {"mix_norm": D, "conv_w": 64, "conv_norm": CW, "w_af": 32, "b_af": GK, "w_ab": 32, "b_ab": GK,
               "gla_norm": 128, "xa_norm": D, "mem_norm": D, "mlp_norm": D, "final_norm": D}


def _small_reduce(acc, loss8):
    names = list(SMALL)
    n = len(names)
    widths = SMALL_WIDTH

    def body(*refs):
        acc_refs = dict(zip(names, refs[:n]))
        loss_ref, tot = refs[n], refs[n + 1]
        pk, all_ref, send_sems, recv_sems, local_sem = refs[n + 2:]

        pk[...] = jnp.zeros_like(pk)
        for k, row in VEC_ROW.items():
            if k == "gla_norm":
                g = functools.reduce(lambda a, b: a + b, [acc_refs[k][pl.ds(0, 1), pl.ds(h * 128, 128)]
                                                          for h in range(NH)])
            else:
                g = acc_refs[k][pl.ds(0, 1), :]
            pk[pl.ds(row, 1), pl.ds(0, widths[k])] = g
        pk[pl.ds(LOSS_ROW, 1), pl.ds(0, 128)] = loss_ref[pl.ds(0, 1), :]
        for k, lane in MAT_LANE.items():
            rows, cols = (3, CW) if k == "conv_w" else (LR, GK)
            pk[pl.ds(MAT_ROW, rows), pl.ds(lane, cols)] = acc_refs[k][pl.ds(MAT_SRC_ROW[k], rows), :]

        x, y, c = _place()
        me, sibling = (x, y, c), (x, y, 1 - c)
        chips = [(1 - x, y, c), (x, 1 - y, c), (1 - x, 1 - y, c)]

        def copy(k, block, to, own=False):
            px, py, pc = block
            dst = all_ref.at[4 * px + 2 * py + pc]
            return pltpu.make_async_remote_copy(
                src_ref=pk if own else dst, dst_ref=dst, send_sem=send_sems.at[k], recv_sem=recv_sems.at[k],
                device_id=to, device_id_type=MESH)

        mine = pltpu.make_async_copy(pk, all_ref.at[4 * x + 2 * y + c], local_sem)
        mine.start()
        first = [copy(0, me, sibling, own=True)] + [copy(1 + j, me, chip, own=True) for j, chip in enumerate(chips)]
        for cp in first:
            cp.start()
        passed = [copy(4 + j, chip, sibling) for j, chip in enumerate(chips)]
        for j, chip in enumerate(chips):
            copy(1 + j, chip, me).wait_recv()
            passed[j].start()
        copy(0, sibling, me).wait_recv()
        for j, (px, py, pc) in enumerate(chips):
            copy(4 + j, (px, py, 1 - pc), me).wait_recv()
        for cp in first + passed:
            cp.wait_send()
        mine.wait()
        total = all_ref[0]
        for d in range(1, NDEV):
            total = total + all_ref[d]
        tot[...] = total

    return pl.pallas_call(
        body, name="small_reduce", out_shape=jax.ShapeDtypeStruct((PACK_ROWS, D), F32),
        scratch_shapes=[pltpu.VMEM((PACK_ROWS, D), F32), pltpu.VMEM((NDEV, PACK_ROWS, D), F32),
                        pltpu.SemaphoreType.DMA((7,)), pltpu.SemaphoreType.DMA((7,)), pltpu.SemaphoreType.DMA],
    )(*[acc[k] for k in names], loss8)


def _small_adamw(tot, ws, ms, vs):
    names = list(SMALL)
    n = len(names)
    widths = SMALL_WIDTH

    def body(*refs):
        tot = refs[0]
        w_refs, m_refs, v_refs = [dict(zip(names, refs[1 + q * n:1 + (q + 1) * n])) for q in range(3)]
        outs = refs[1 + 3 * n:1 + 7 * n]
        g_out, d_out, m_out, v_out = [dict(zip(names, outs[q * n:(q + 1) * n])) for q in range(4)]
        cut = refs[1 + 7 * n]
        x, y, c = _place()
        dev = 4 * x + 2 * y + c
        for k in names:
            if k in VEC_ROW:
                g = tot[pl.ds(VEC_ROW[k], 1), pl.ds(0, widths[k])]
            else:
                rows, cols = (3, CW) if k == "conv_w" else (LR, GK)
                wd = widths[k]
                sel = jnp.where(_iota((cols, wd), 0) == dev * wd + _iota((cols, wd), 1), 1.0, 0.0).astype(BF16)
                cut[:, pl.ds(0, wd)] = _dot_exact_rhs(tot[pl.ds(MAT_ROW, LR), pl.ds(MAT_LANE[k], cols)], sel, 3)
                g = cut[pl.ds(0, rows), pl.ds(0, wd)]
            g_out[k][...] = g
            d_out[k][...], m_out[k][...], v_out[k][...] = _adamw_math(w_refs[k][...], g, m_refs[k][...],
                                                                       v_refs[k][...])

    shapes = [jax.ShapeDtypeStruct(ws[k].shape, F32) for k in names]
    res = pl.pallas_call(
        body, name="small_adamw", out_shape=shapes * 4, scratch_shapes=[pltpu.VMEM((LR, 128), F32)],
    )(tot, *[ws[k] for k in names], *[ms[k] for k in names], *[vs[k] for k in names])
    return {k: tuple(res[q * n + i] for q in range(4)) for i, k in enumerate(names)}


def _adamw_math(w, g, m, v):
    m = ADAM_B1 * m + (1.0 - ADAM_B1) * g
    v = ADAM_B2 * v + (1.0 - ADAM_B2) * (g * g)
    m_hat = m / (1.0 - ADAM_B1 ** ADAM_STEP)
    v_hat = v / (1.0 - ADAM_B2 ** ADAM_STEP)
    delta = -ADAM_LR * (m_hat / (jnp.sqrt(v_hat) + ADAM_EPS) + ADAM_WD * w)
    return delta, m, v


def _adamw(ws, ms, vs, owns, r2s, name):
    n = len(ws)
    _, r, c = ws[0].shape
    tr = 256 if r % 256 == 0 else r

    def body(*refs):
        ins, outs = refs[:5 * n], refs[5 * n:]
        for q in range(n):
            w_ref, m_ref, v_ref, o_ref, r_ref = [ins[k * n + q] for k in range(5)]
            g_ref, d_ref, nm_ref, nv_ref = [outs[k * n + q] for k in range(4)]
            g = ((o_ref[...] + r_ref[0].astype(F32)) + r_ref[1].astype(F32)) + r_ref[2].astype(F32)
            g_ref[...] = g
            d_ref[...], nm_ref[...], nv_ref[...] = _adamw_math(w_ref[...], g, m_ref[...], v_ref[...])

    spec = pl.BlockSpec((None, tr, c), lambda i: (0, i, 0))
    res = pl.pallas_call(
        body, name=name, grid=(r // tr,),
        in_specs=[spec] * (3 * n) + [pl.BlockSpec((tr, c), lambda i: (i, 0))] * n
        + [pl.BlockSpec((3, tr, c), lambda i: (0, i, 0))] * n,
        out_specs=[spec] * (4 * n), out_shape=[jax.ShapeDtypeStruct((1, r, c), F32)] * (4 * n),
        compiler_params=_cparams(("arbitrary",)))(*ws, *ms, *vs, *owns, *r2s)
    return [tuple(res[k * n + q] for k in range(4)) for q in range(n)]


MATS = ("w_in", "w_out", "w_xq", "w_xo", "w_xkv", "w_up", "w_down")
SMALL = ("mix_norm", "conv_w", "conv_norm", "w_af", "b_af", "w_ab", "b_ab", "gla_norm", "xa_norm", "mem_norm",
         "mlp_norm", "final_norm")
WEIGHTS = ("mix_norm", "w_in", "conv_w", "conv_norm", "w_af", "b_af", "w_ab", "b_ab", "gla_norm", "w_out", "xa_norm",
           "mem_norm", "w_xq", "w_xkv", "w_xo", "mlp_norm", "w_up", "w_down", "final_norm")
SMALL_SHARDED = {"conv_w": (3, 64), "w_af": (LR, 32), "w_ab": (LR, 32)}
SMALL_PACK_ROWS = 16


def kernel(x, mem, mix_norm, w_in, conv_w, conv_norm, w_af, b_af, w_ab, b_ab, gla_norm, w_out, xa_norm, mem_norm, w_xq, w_xkv, w_xo, mlp_norm, w_up, w_down, final_norm, loss_target, m_mix_norm, m_w_in, m_conv_w, m_conv_norm, m_w_af, m_b_af, m_w_ab, m_b_ab, m_gla_norm, m_w_out, m_xa_norm, m_mem_norm, m_w_xq, m_w_xkv, m_w_xo, m_mlp_norm, m_w_up, m_w_down, m_final_norm, v_mix_norm, v_w_in, v_conv_w, v_conv_norm, v_w_af, v_b_af, v_w_ab, v_b_ab, v_gla_norm, v_w_out, v_xa_norm, v_mem_norm, v_w_xq, v_w_xkv, v_w_xo, v_mlp_norm, v_w_up, v_w_down, v_final_norm):
    w = dict(mix_norm=mix_norm, w_in=w_in, conv_w=conv_w, conv_norm=conv_norm, w_af=w_af, b_af=b_af, w_ab=w_ab,
             b_ab=b_ab, gla_norm=gla_norm, w_out=w_out, xa_norm=xa_norm, mem_norm=mem_norm, w_xq=w_xq, w_xkv=w_xkv,
             w_xo=w_xo, mlp_norm=mlp_norm, w_up=w_up, w_down=w_down, final_norm=final_norm)
    mom = dict(mix_norm=m_mix_norm, w_in=m_w_in, conv_w=m_conv_w, conv_norm=m_conv_norm, w_af=m_w_af, b_af=m_b_af,
               w_ab=m_w_ab, b_ab=m_b_ab, gla_norm=m_gla_norm, w_out=m_w_out, xa_norm=m_xa_norm, mem_norm=m_mem_norm,
               w_xq=m_w_xq, w_xkv=m_w_xkv, w_xo=m_w_xo, mlp_norm=m_mlp_norm, w_up=m_w_up, w_down=m_w_down,
               final_norm=m_final_norm)
    var = dict(mix_norm=v_mix_norm, w_in=v_w_in, conv_w=v_conv_w, conv_norm=v_conv_norm, w_af=v_w_af, b_af=v_b_af,
               w_ab=v_w_ab, b_ab=v_b_ab, gla_norm=v_gla_norm, w_out=v_w_out, xa_norm=v_xa_norm, mem_norm=v_mem_norm,
               w_xq=v_w_xq, w_xkv=v_w_xkv, w_xo=v_w_xo, mlp_norm=v_mlp_norm, w_up=v_w_up, w_down=v_w_down,
               final_norm=v_final_norm)
    xi, yi, ci = _place()
    two_d = lambda a: a.reshape(a.shape[-2:]) if a.ndim == 3 else a.reshape(1, a.shape[-1])

    small = jnp.concatenate([w[n].reshape(-1) for n in SMALL_SHARDED])
    small = jnp.pad(small, (0, SMALL_PACK_ROWS * 128 - small.shape[0])).reshape(SMALL_PACK_ROWS, 128)
    shard = {n: two_d(w[n]).astype(BF16) for n in MATS}
    for n in ("w_in", "w_up"):
        shard[n] = shard[n].T
    vec = {n: two_d(w[n]) for n in SMALL if n not in SMALL_SHARDED}
    place = jnp.stack([2 * xi + yi, ci]).astype(jnp.int32)
    loss8, grad_x, small_acc, own, from_chips = _step(x[0], mem[0], loss_target[0], shard, small, vec, place)

    tot = _small_reduce(small_acc, loss8)
    small_out = _small_adamw(tot, *[{n: two_d(d[n]) for n in SMALL} for d in (w, mom, var)])
    loss = tot[LOSS_ROW, 0]

    out_g, out_d, out_m, out_v = {}, {}, {}, {}
    own["w_up"], from_chips["w_up"] = own["w_up"].T, from_chips["w_up"].transpose(0, 2, 1)
    wmv = {n: [a.transpose(0, 2, 1) if n == "w_in" else a for a in (w[n], mom[n], var[n])] for n in MATS}
    for shape in dict.fromkeys(wmv[n][0].shape for n in MATS):
        names = [n for n in MATS if wmv[n][0].shape == shape]
        res = _adamw(*[[wmv[n][k] for n in names] for k in range(3)], [own[n] for n in names],
                     [from_chips[n] for n in names], "adamw_" + "_".join(names))
        for n, r in zip(names, res):
            out_g[n], out_d[n], out_m[n], out_v[n] = [a.transpose(0, 2, 1) for a in r] if n == "w_in" else r
    for n in SMALL:
        out_g[n], out_d[n], out_m[n], out_v[n] = [a.reshape(w[n].shape) for a in small_out[n]]

    return (loss, grad_x[None], *[out_g[n] for n in WEIGHTS], *[out_d[n] for n in WEIGHTS],
            *[out_m[n] for n in WEIGHTS], *[out_v[n] for n in WEIGHTS])
```

```python
import functools
import itertools

import jax
import jax.numpy as jnp
from jax import lax
from jax.experimental import pallas as pl
from jax.experimental.pallas import tpu as pltpu

F32 = jnp.float32
BF16 = jnp.bfloat16

D = 1024
CW = 512
GK = 256
GV = 512
NH = 4
CH = 64
LR = 16
NMEM = 256
XD = 256
FF = 4096
ZW = 3104
ZC = 3200
EPS = 1e-6
NDEV = 8

ZB_CB, ZB_CC, ZB_CU, ZB_V, ZB_G = 0, 1, 2, 4, 5
ZB_Q, ZB_K = 6, 7
ZB_LR = 24

TM = 512
TM_MLP = 256
TM_MLP_FWD = 512
TF = 512
TB = 512
TB_BWD = 512
TT = 2048
VMEM_LIMIT = 56 * 1024 * 1024

ADAM_LR, ADAM_B1, ADAM_B2, ADAM_EPS, ADAM_WD, ADAM_STEP = 0.001, 0.9, 0.999, 1e-08, 0.01, 10

XKV_SHARD = 2 * D // NDEV

MESH = pl.DeviceIdType.MESH


def _cparams(sem):
    return pltpu.CompilerParams(dimension_semantics=sem, vmem_limit_bytes=VMEM_LIMIT)


def _call(body, name, grid, in_specs, out_specs, out_shape, scratch, args, riders=()):
    n_in, n_out, n_scr = len(in_specs), len(out_specs), len(scratch)
    counts = [(len(r.arrays), len(r.out_shape), len(r.scratch)) for r in riders]

    def take(refs, pos, sizes):
        groups = []
        for size in sizes:
            groups.append(refs[pos:pos + size])
            pos += size
        return groups, pos

    def wrapped(*refs):
        ins, pos = refs[:n_in], n_in
        r_ins, pos = take(refs, pos, [c[0] for c in counts])
        outs, pos = refs[pos:pos + n_out], pos + n_out
        r_outs, pos = take(refs, pos, [c[1] for c in counts])
        scr, pos = refs[pos:pos + n_scr], pos + n_scr
        r_scr, pos = take(refs, pos, [c[2] for c in counts])
        ids = [pl.program_id(d) for d in range(len(grid))]
        first = functools.reduce(lambda a, b: a & b, [i == 0 for i in ids])
        last = functools.reduce(lambda a, b: a & b, [i == g - 1 for i, g in zip(ids, grid)])

        @pl.when(first)
        def _():
            for r, a, b, c in zip(riders, r_ins, r_outs, r_scr):
                r.start(a, b, c)

        body(*ins, *outs, *scr)

        if any(r.relay for r in riders):
            at = [max(g - 2, 0) for g in grid]

            @pl.when(functools.reduce(lambda a, b: a & b, [i == s for i, s in zip(ids, at)]))
            def _():
                for r, a, b, c in zip(riders, r_ins, r_outs, r_scr):
                    if r.relay:
                        r.relay(a, b, c)

        @pl.when(last)
        def _():
            for r, a, b, c in zip(riders, r_ins, r_outs, r_scr):
                r.finish(a, b, c)

    hbm = pl.BlockSpec(memory_space=pltpu.HBM)
    r_args = [a for r in riders for a in r.arrays]
    r_shapes = [s for r in riders for s in r.out_shape]
    return pl.pallas_call(
        wrapped if riders else body, name=name, grid=grid, in_specs=list(in_specs) + [hbm] * len(r_args),
        out_specs=list(out_specs) + [hbm] * len(r_shapes), out_shape=list(out_shape) + r_shapes,
        scratch_shapes=list(scratch) + [s for r in riders for s in r.scratch],
        compiler_params=_cparams(("arbitrary",) * len(grid)))(*args, *r_args)


def _dot(a, b):
    return jnp.dot(a.astype(BF16), b.astype(BF16), preferred_element_type=F32)


def _dot_nt(a, b):
    return lax.dot_general(a.astype(BF16), b.astype(BF16), (((1,), (1,)), ((), ())), preferred_element_type=F32)


def _dot_tn(a, b):
    return lax.dot_general(a.astype(BF16), b.astype(BF16), (((0,), (0,)), ((), ())), preferred_element_type=F32)


def _split(x, n):
    parts = []
    for _ in range(n):
        p = x.astype(BF16)
        parts.append(p)
        x = x - p.astype(F32)
    return parts


def _dot_exact_lhs(m, x, n):
    return functools.reduce(lambda a, b: a + b, [jnp.dot(m, p, preferred_element_type=F32) for p in _split(x, n)])


def _dot_exact_rhs(x, m, n):
    return functools.reduce(lambda a, b: a + b, [jnp.dot(p, m, preferred_element_type=F32) for p in _split(x, n)])


def _rms(x, g):
    r = lax.rsqrt(jnp.mean(x * x, axis=-1, keepdims=True) + EPS)
    return x * r * g, r


def _rms_bwd(x, r, g, dy):
    xr = x * r
    u = dy * g
    dx = r * (u - xr * jnp.mean(u * xr, axis=-1, keepdims=True))
    return dx, jnp.sum(dy * xr, axis=0, keepdims=True)


def _iota(shape, dim):
    return lax.broadcasted_iota(jnp.int32, shape, dim)


def _sigmoid(x):
    return 1.0 / (1.0 + jnp.exp(-x))


def _acc_rows(ref, row):
    ref[...] += jnp.broadcast_to(row, ref.shape)


def _prenorm(x, g, riders=()):
    t = x.shape[0]
    tm = min(2 * TM, t)

    def body(x_ref, g_ref, h_ref):
        h_ref[...] = _rms(x_ref[...], g_ref[...])[0].astype(BF16)

    return _call(
        body, "prenorm", (t // tm,), [pl.BlockSpec((tm, D), lambda i: (i, 0)), pl.BlockSpec((1, D), lambda i: (0, 0))],
        [pl.BlockSpec((tm, D), lambda i: (i, 0))], [jax.ShapeDtypeStruct((t, D), BF16)], [], (x, g), riders)


def _inproj(hb, w_t, riders=()):
    t = hb.shape[0]
    tm = min(TM, t)

    def body(h_ref, w_ref, z_ref):
        z_ref[...] = _dot_nt(h_ref[...], w_ref[...])

    return _call(
        body, "inproj", (t // tm,),
        [pl.BlockSpec((tm, D), lambda i: (i, 0)), pl.BlockSpec((ZC, D), lambda i: (0, 0))],
        [pl.BlockSpec((tm, ZC), lambda i: (i, 0))], [jax.ShapeDtypeStruct((t, ZC), F32)], [], (hb, w_t), riders)


def _kv_proj(mem, g, w):
    def body(m_ref, g_ref, w_ref, kv_ref, mn_ref):
        mn, _ = _rms(m_ref[...], g_ref[...])
        mb = mn.astype(BF16)
        mn_ref[...] = mb
        for j in range(NDEV):
            kv_ref[:, j * XKV_SHARD:(j + 1) * XKV_SHARD] = jnp.dot(mb, w_ref[j], preferred_element_type=F32)

    return pl.pallas_call(
        body, name="kv_proj",
        out_shape=[jax.ShapeDtypeStruct((NMEM, 2 * D), F32), jax.ShapeDtypeStruct((NMEM, D), BF16)],
        compiler_params=pltpu.CompilerParams(vmem_limit_bytes=VMEM_LIMIT))(mem, g, w)


def _softmax_head(qb, kb):
    s = _dot_nt(qb, kb) * (1.0 / 16.0)
    e = jnp.exp(s - jnp.max(s, axis=-1, keepdims=True))
    return e / jnp.sum(e, axis=-1, keepdims=True)


def _attn_fwd(x, z, o_f, o_b, conv_w, conv_norm, gla_norm4, w_out, g, w_xq, kb, vb, w_xo):
    t = x.shape[0]
    tm = min(TM, t)
    nblk = t // tm
    jmap = lambda i: i

    def body(x_ref, zq_ref, zk_ref, zv_ref, zg_ref, cb_ref, cc_ref, cu_ref, ccp_ref, ccn_ref, cup_ref, cun_ref,
             of_ref, ob_ref, cw_ref, cn_ref, gn_ref, wo_ref, g_ref, wq_ref, k_ref, v_ref, wx_ref,
             x1_ref, x2_ref, xn_ref, q_ref, a_ref, y_ref, opre_ref):
        j = pl.program_id(0)
        zv = zv_ref[...]
        sb = _head_sum((zq_ref[...] * 0.125) * zk_ref[...], 64, 128)
        o_pre = of_ref[...] + ob_ref[...] - sb * zv
        opre_ref[...] = o_pre
        on, _ = _head_norm(o_pre)
        zg = zg_ref[...]
        y_ref[:, CW:] = (on * gn_ref[...] * (zg * _sigmoid(zg))).astype(BF16)
        cb = cb_ref[...]
        _, _, _, conv = _conv_parts(cb, cc_ref[...], cu_ref[...], ccp_ref[pl.ds(7, 1), :], cup_ref[pl.ds(7, 1), :],
                                    ccn_ref[pl.ds(0, 1), :], cun_ref[pl.ds(0, 1), :], cw_ref, j == 0,
                                    j == nblk - 1, tm)
        yc = cb * conv
        gm = _group_sum(yc * yc) * (1.0 / 64.0)
        y_ref[:, :CW] = (yc * lax.rsqrt(gm + EPS) * cn_ref[...]).astype(BF16)

        x1 = x_ref[...] + jnp.dot(y_ref[...], wo_ref[...], preferred_element_type=F32)
        x1_ref[...] = x1
        xn, _ = _rms(x1, g_ref[...])
        xb = xn.astype(BF16)
        xn_ref[...] = xb
        qb = jnp.dot(xb, wq_ref[...], preferred_element_type=F32).astype(BF16)
        q_ref[...] = qb
        for h in range(NH):
            hs = slice(h * XD, (h + 1) * XD)
            p = _softmax_head(qb[:, hs], k_ref[:, hs])
            a_ref[:, hs] = _dot(p, v_ref[:, hs]).astype(BF16)
        x2_ref[...] = x1 + jnp.dot(a_ref[...], wx_ref[...], preferred_element_type=F32)

    tok = lambda i: (i, 0)
    full = lambda i: (0, 0)
    once = pl.Buffered(1)
    tokd, tokv = pl.BlockSpec((tm, D), tok), pl.BlockSpec((tm, GV), tok)
    weight = pl.BlockSpec((D, D), full, pipeline_mode=once)
    ccp, ccn = _halo_specs(tm, nblk, t, ZB_CC, jmap)
    cup, cun = _halo_specs(tm, nblk, t, ZB_CU, jmap)
    in_specs = [tokd, _zspec(tm, GK, ZB_Q, jmap), _zspec(tm, GK, ZB_K, jmap), _zspec(tm, GV, ZB_V, jmap),
                _zspec(tm, GV, ZB_G, jmap), _zspec(tm, CW, ZB_CB, jmap), _zspec(tm, CW, ZB_CC, jmap),
                _zspec(tm, CW, ZB_CU, jmap), ccp, ccn, cup, cun, tokv, tokv,
                pl.BlockSpec((3, CW), full), pl.BlockSpec((1, CW), full), pl.BlockSpec((1, GV), full),
                weight, pl.BlockSpec((1, D), full), weight, pl.BlockSpec((NMEM, D), full),
                pl.BlockSpec((NMEM, D), full), weight]
    return pl.pallas_call(
        body, name="attn_fwd", grid=(nblk,), in_specs=in_specs, out_specs=[tokd] * 6 + [tokv],
        out_shape=[jax.ShapeDtypeStruct((t, D), F32), jax.ShapeDtypeStruct((t, D), F32),
                   jax.ShapeDtypeStruct((t, D), BF16), jax.ShapeDtypeStruct((t, D), BF16),
                   jax.ShapeDtypeStruct((t, D), BF16), jax.ShapeDtypeStruct((t, D), BF16),
                   jax.ShapeDtypeStruct((t, GV), F32)],
        compiler_params=_cparams(("arbitrary",)))(
            x, z, z, z, z, z, z, z, z, z, z, z, o_f, o_b, conv_w, conv_norm, gla_norm4, w_out, g, w_xq, kb, vb, w_xo)


def _mlp_fwd(x2, g, w_up_t, w_down, fg, target):
    t = x2.shape[0]
    tm = min(TM_MLP_FWD, t)

    def body(x_ref, g_ref, wu_ref, wd_ref, fg_ref, t_ref, h1_ref, xn_ref, dx_ref, dxb_ref, loss_ref, dfg_ref, ab):
        @pl.when(pl.program_id(0) == 0)
        def _():
            loss_ref[...] = jnp.zeros_like(loss_ref)
            dfg_ref[...] = jnp.zeros_like(dfg_ref)

        x = x_ref[...]
        xn, _ = _rms(x, g_ref[...])
        xnb = xn.astype(BF16)
        xn_ref[...] = xnb
        for q in range(FF // TF):
            cols = slice(q * TF, (q + 1) * TF)
            h1 = _dot_nt(xnb, wu_ref[cols, :])
            h1_ref[:, cols] = h1.astype(BF16)
            hr = jnp.maximum(h1, 0.0)
            ab[:, cols] = (hr * hr).astype(BF16)
        x3 = x + jnp.dot(ab[...], wd_ref[...], preferred_element_type=F32)
        y, r = _rms(x3, fg_ref[...])
        e = y - t_ref[...]
        row = jnp.mean(e * e, axis=-1, keepdims=True)
        _acc_rows(loss_ref, 0.5 * jnp.sum(row, axis=0, keepdims=True))
        dx, dfg = _rms_bwd(x3, r, fg_ref[...], e * (1.0 / D))
        dx_ref[...] = dx
        dxb_ref[...] = dx.astype(BF16)
        _acc_rows(dfg_ref, dfg)

    tok = lambda i: (i, 0)
    full = lambda i: (0, 0)
    once = pl.Buffered(1)
    return pl.pallas_call(
        body, name="mlp_fwd", grid=(t // tm,),
        in_specs=[pl.BlockSpec((tm, D), tok), pl.BlockSpec((1, D), full),
                  pl.BlockSpec((FF, D), full, pipeline_mode=once), pl.BlockSpec((FF, D), full, pipeline_mode=once),
                  pl.BlockSpec((1, D), full), pl.BlockSpec((tm, D), tok)],
        out_specs=[pl.BlockSpec((tm, FF), tok), pl.BlockSpec((tm, D), tok), pl.BlockSpec((tm, D), tok),
                   pl.BlockSpec((tm, D), tok), pl.BlockSpec((8, 128), full), pl.BlockSpec((8, D), full)],
        out_shape=[jax.ShapeDtypeStruct((t, FF), BF16), jax.ShapeDtypeStruct((t, D), BF16),
                   jax.ShapeDtypeStruct((t, D), F32), jax.ShapeDtypeStruct((t, D), BF16),
                   jax.ShapeDtypeStruct((8, 128), F32), jax.ShapeDtypeStruct((8, D), F32)],
        scratch_shapes=[pltpu.VMEM((tm, FF), BF16)],
        compiler_params=_cparams(("arbitrary",)))(x2, g, w_up_t, w_down, fg, target)


def _mlp_bwd(dx3, dx3b, h1b, w_down, w_up_t, x2, g):
    t = x2.shape[0]
    tm = min(TM_MLP, t)

    def body(dx_ref, dxb_ref, h1_ref, wd_ref, wu_ref, x_ref, g_ref, a_ref, dh_ref, dx2_ref, dx2b_ref, dg_ref):
        @pl.when(pl.program_id(0) == 0)
        def _():
            dg_ref[...] = jnp.zeros_like(dg_ref)

        for q in range(FF // TF):
            cols = slice(q * TF, (q + 1) * TF)
            hr = jnp.maximum(h1_ref[:, cols].astype(F32), 0.0)
            da = _dot_nt(dxb_ref[...], wd_ref[cols, :])
            a_ref[:, cols] = (hr * hr).astype(BF16)
            dh_ref[:, cols] = (da * 2.0 * hr).astype(BF16)
        dxn = jnp.dot(dh_ref[...], wu_ref[...], preferred_element_type=F32)
        x = x_ref[...]
        r = lax.rsqrt(jnp.mean(x * x, axis=-1, keepdims=True) + EPS)
        dx, dg = _rms_bwd(x, r, g_ref[...], dxn)
        dx2 = dx_ref[...] + dx
        dx2_ref[...] = dx2
        dx2b_ref[...] = dx2.astype(BF16)
        _acc_rows(dg_ref, dg)

    tok = lambda i: (i, 0)
    full = lambda i: (0, 0)
    once = pl.Buffered(1)
    return pl.pallas_call(
        body, name="mlp_bwd", grid=(t // tm,),
        in_specs=[pl.BlockSpec((tm, D), tok), pl.BlockSpec((tm, D), tok), pl.BlockSpec((tm, FF), tok),
                  pl.BlockSpec((FF, D), full, pipeline_mode=once), pl.BlockSpec((FF, D), full, pipeline_mode=once),
                  pl.BlockSpec((tm, D), tok), pl.BlockSpec((1, D), full)],
        out_specs=[pl.BlockSpec((tm, FF), tok), pl.BlockSpec((tm, FF), tok), pl.BlockSpec((tm, D), tok),
                   pl.BlockSpec((tm, D), tok), pl.BlockSpec((8, D), full)],
        out_shape=[jax.ShapeDtypeStruct((t, FF), BF16), jax.ShapeDtypeStruct((t, FF), BF16),
                   jax.ShapeDtypeStruct((t, D), F32), jax.ShapeDtypeStruct((t, D), BF16),
                   jax.ShapeDtypeStruct((8, D), F32)],
        compiler_params=_cparams(("arbitrary",)))(dx3, dx3b, h1b, w_down, w_up_t, x2, g)


def _attn_bwd(x1, dx2, dx2b, qb, kb, vb, w_xo, w_xq, w_out, g, riders=()):
    t = x1.shape[0]
    tm = min(TM, t)

    def body(x_ref, dx2_ref, dx2b_ref, q_ref, k_ref, v_ref, wx_ref, wq_ref, wo_ref, g_ref,
             dx1_ref, dx1b_ref, dy_ref, dq_ref, dkv_ref, dg_ref):
        @pl.when(pl.program_id(0) == 0)
        def _():
            dkv_ref[...] = jnp.zeros_like(dkv_ref)
            dg_ref[...] = jnp.zeros_like(dg_ref)

        datt = _dot_nt(dx2b_ref[...], wx_ref[...]).astype(BF16)
        for h in range(NH):
            hs = slice(h * XD, (h + 1) * XD)
            q_h, k_h, v_h, da_h = q_ref[:, hs], k_ref[:, hs], v_ref[:, hs], datt[:, hs]
            p = _softmax_head(q_h, k_h)
            dp = _dot_nt(da_h, v_h)
            ds = (p * (dp - jnp.sum(dp * p, axis=-1, keepdims=True)) * (1.0 / 16.0)).astype(BF16)
            dq_ref[:, hs] = _dot(ds, k_h).astype(BF16)
            dkv_ref[:, hs] += _dot_tn(ds, q_h)
            dkv_ref[:, D + h * XD:D + (h + 1) * XD] += _dot_tn(p, da_h)
        dxn = _dot_nt(dq_ref[...], wq_ref[...])
        x = x_ref[...]
        r = lax.rsqrt(jnp.mean(x * x, axis=-1, keepdims=True) + EPS)
        dx, dg = _rms_bwd(x, r, g_ref[...], dxn)
        dx1 = dx2_ref[...] + dx
        dx1_ref[...] = dx1
        dx1b = dx1.astype(BF16)
        dx1b_ref[...] = dx1b
        dy_ref[...] = _dot_nt(dx1b, wo_ref[...])
        _acc_rows(dg_ref, dg)

    tok = lambda i: (i, 0)
    full = lambda i: (0, 0)
    return _call(
        body, "attn_bwd", (t // tm,),
        [pl.BlockSpec((tm, D), tok), pl.BlockSpec((tm, D), tok), pl.BlockSpec((tm, D), tok),
         pl.BlockSpec((tm, D), tok), pl.BlockSpec((NMEM, D), full), pl.BlockSpec((NMEM, D), full),
         pl.BlockSpec((D, D), full), pl.BlockSpec((D, D), full), pl.BlockSpec((D, D), full),
         pl.BlockSpec((1, D), full)],
        [pl.BlockSpec((tm, D), tok), pl.BlockSpec((tm, D), tok), pl.BlockSpec((tm, D), tok),
         pl.BlockSpec((tm, D), tok), pl.BlockSpec((NMEM, 2 * D), full), pl.BlockSpec((8, D), full)],
        [jax.ShapeDtypeStruct((t, D), F32), jax.ShapeDtypeStruct((t, D), BF16),
         jax.ShapeDtypeStruct((t, D), F32), jax.ShapeDtypeStruct((t, D), BF16),
         jax.ShapeDtypeStruct((NMEM, 2 * D), F32), jax.ShapeDtypeStruct((8, D), F32)], [],
        (x1, dx2, dx2b, qb, kb, vb, w_xo, w_xq, w_out, g), riders)


def _kv_bwd(dkv, memn, mem, g, w):
    def body(dkv_ref, mn_ref, m_ref, g_ref, w_ref, dw_ref, dg_ref):
        dkvb = dkv_ref[...].astype(BF16)
        dmn = jnp.zeros((NMEM, D), F32)
        for j in range(NDEV):
            cols = slice(j * XKV_SHARD, (j + 1) * XKV_SHARD)
            dw_ref[j] = _dot_tn(mn_ref[...], dkvb[:, cols])
            dmn += _dot_nt(dkvb[:, cols], w_ref[j])
        m = m_ref[...]
        r = lax.rsqrt(jnp.mean(m * m, axis=-1, keepdims=True) + EPS)
        dg_ref[...] = jnp.broadcast_to(jnp.sum(dmn * m * r, axis=0, keepdims=True), dg_ref.shape)

    return pl.pallas_call(
        body, name="kv_bwd",
        out_shape=[jax.ShapeDtypeStruct((NDEV, D, XKV_SHARD), F32), jax.ShapeDtypeStruct((8, D), F32)],
        compiler_params=pltpu.CompilerParams(vmem_limit_bytes=VMEM_LIMIT))(dkv, memn, mem, g, w)


def _inproj_bwd(dz, w_t, x, dx1, g, riders=()):
    t = x.shape[0]
    tm = min(TM, t)

    def body(dz_ref, w_ref, x_ref, dx1_ref, g_ref, gx_ref, dg_ref):
        @pl.when(pl.program_id(0) == 0)
        def _():
            dg_ref[...] = jnp.zeros_like(dg_ref)

        dh = jnp.dot(dz_ref[...], w_ref[...], preferred_element_type=F32)
        x = x_ref[...]
        r = lax.rsqrt(jnp.mean(x * x, axis=-1, keepdims=True) + EPS)
        dx, dg = _rms_bwd(x, r, g_ref[...], dh)
        gx_ref[...] = dx1_ref[...] + dx
        _acc_rows(dg_ref, dg)

    tok = lambda i: (i, 0)
    full = lambda i: (0, 0)
    return _call(
        body, "inproj_bwd", (t // tm,),
        [pl.BlockSpec((tm, ZC), tok), pl.BlockSpec((ZC, D), full), pl.BlockSpec((tm, D), tok),
         pl.BlockSpec((tm, D), tok), pl.BlockSpec((1, D), full)],
        [pl.BlockSpec((tm, D), tok), pl.BlockSpec((8, D), full)],
        [jax.ShapeDtypeStruct((t, D), F32), jax.ShapeDtypeStruct((8, D), F32)], [], (dz, w_t, x, dx1, g), riders)


def _matmul_tn(a, b, name, rows=None, riders=()):
    t, k = a.shape
    n = b.shape[1]
    tk, tn = [1024 if size % 1024 == 0 else 640 for size in (k, n)]
    tt = min(TT, t)
    rows = rows or k

    def body(a_ref, b_ref, o_ref):
        @pl.when(pl.program_id(2) == 0)
        def _():
            o_ref[...] = jnp.zeros_like(o_ref)

        o_ref[...] += _dot_tn(a_ref[...], b_ref[...])

    return _call(
        body, name, (k // tk, n // tn, t // tt),
        [pl.BlockSpec((tt, tk), lambda i, j, s: (s, i)), pl.BlockSpec((tt, tn), lambda i, j, s: (s, j))],
        [pl.BlockSpec((tk, tn), lambda i, j, s: (i, j))], [jax.ShapeDtypeStruct((rows, n), F32)], [], (a, b), riders)


def _lane_head(shape, dim, shift):
    return _iota(shape, dim) >> shift


CUM_ROWS = 128


def _chunk_cumsum(x, upper, n):
    r, c = _iota((CUM_ROWS, CUM_ROWS), 0), _iota((CUM_ROWS, CUM_ROWS), 1)
    tri = (c >= r) if upper else (c <= r)
    cum = jnp.where(((r >> 6) == (c >> 6)) & tri, 1.0, 0.0).astype(BF16)
    return jnp.concatenate([_dot_exact_lhs(cum, x[g:g + CUM_ROWS], n) for g in range(0, x.shape[0], CUM_ROWS)],
                           axis=0)


def _gla_recompute(q_raw, k, lr, wpad, bias, rev, tb):
    pre = _dot(lr, wpad) + bias
    la = (jnp.minimum(pre, 0.0) - jnp.log(1.0 + jnp.exp(-jnp.abs(pre)))) * (1.0 / 16.0)
    b = _chunk_cumsum(la, rev, 3)
    e, ei = jnp.exp(b), jnp.exp(-b)
    qt = (q_raw * 0.125) * e
    kt = k * ei
    return pre, b, e, ei, qt, kt


def _stack_heads(x, shift):
    head = _lane_head(x.shape, 1, shift)
    return jnp.concatenate([jnp.where(head == h, x, 0.0) for h in range(NH)], axis=0).astype(BF16)


def _fold_heads(x, shift):
    head = _lane_head((CH, x.shape[1]), 1, shift)
    return functools.reduce(lambda a, b: a + b,
                            [jnp.where(head == h, x[h * CH:(h + 1) * CH], 0.0) for h in range(NH)])


def _wide_mask(rev):
    r, s = _iota((CH, NH * CH), 0), _iota((CH, NH * CH), 1) & (CH - 1)
    return (s >= r) if rev else (s <= r)


def _rows_by_head(x):
    w = x.shape[1] // NH
    return jnp.concatenate([x[:, h * w:(h + 1) * w] for h in range(NH)], axis=0)


def _lanes_by_head(x):
    return jnp.concatenate([x[h * CH:(h + 1) * CH] for h in range(NH)], axis=1)


def _state_compact(xt):
    head = _lane_head((128, GK), 1, 6)
    return functools.reduce(lambda a, b: a + b,
                            [jnp.where(head == h, xt[h * 128:(h + 1) * 128], 0.0) for h in range(NH)])


def _conv_parts(cb, cc, cu, ccp, cup, ccn, cun, cw_ref, first, last, tb):
    h = cc * cu
    hp = jnp.where(first, 0.0, ccp * cup)
    hn = jnp.where(last, 0.0, ccn * cun)
    rows = _iota(h.shape, 0)
    h_m1 = jnp.where(rows == 0, hp, pltpu.roll(h, 1, 0))
    h_p1 = jnp.where(rows == tb - 1, hn, pltpu.roll(h, tb - 1, 0))
    conv = cw_ref[pl.ds(0, 1), :] * h_m1 + cw_ref[pl.ds(1, 1), :] * h + cw_ref[pl.ds(2, 1), :] * h_p1
    return h, h_m1, h_p1, conv


def _head_sum(x, w_in, w_out):
    shape, sh_in, sh_out = (2 * w_in, 2 * w_out), w_in.bit_length() - 1, w_out.bit_length() - 1
    sel = jnp.where((_iota(shape, 0) >> sh_in) == (_iota(shape, 1) >> sh_out), 1.0, 0.0).astype(BF16)
    return jnp.concatenate([_dot_exact_rhs(x[:, s:s + 2 * w_in], sel, 2) for s in range(0, NH * w_in, 2 * w_in)],
                           axis=1)


def _group_sum(x):
    ones = jnp.where((_iota((128, 128), 0) >> 6) == (_iota((128, 128), 1) >> 6), 1.0, 0.0).astype(BF16)
    return jnp.concatenate([_dot_exact_rhs(x[:, s:s + 128], ones, 2) for s in range(0, x.shape[1], 128)], axis=1)


def _head_norm(o):
    ons, rs = [], []
    for h in range(NH):
        slab = o[:, h * 128:(h + 1) * 128]
        r = lax.rsqrt(jnp.mean(slab * slab, axis=-1, keepdims=True) + EPS)
        ons.append(slab * r)
        rs.append(jnp.broadcast_to(r, slab.shape))
    return jnp.concatenate(ons, axis=1), jnp.concatenate(rs, axis=1)


def _zspec(tb, width, blk, jmap):
    return pl.BlockSpec((tb, width), lambda i: (jmap(i), blk))


def _halo_specs(tb, nblk, t, blk, jmap):
    prev = pl.BlockSpec((8, CW), lambda i: (jnp.maximum(jmap(i) * (tb // 8) - 1, 0), blk))
    nxt = pl.BlockSpec((8, CW), lambda i: (jnp.minimum((jmap(i) + 1) * (tb // 8), t // 8 - 1), blk))
    return prev, nxt


def _gla_fwd_block(q_ref, k_ref, v_ref, lr_ref, w_ref, bias_ref, o_ref, sd_ref, st, b_scr, rev, tb):
    nb = tb // CH
    _, b, _, _, qt, kt = _gla_recompute(q_ref[...], k_ref[...], lr_ref[...], w_ref[...], bias_ref[...], rev, tb)
    v = v_ref[...]
    b_scr[...] = b
    yield
    maskw = _wide_mask(rev)
    order = list(reversed(range(nb))) if rev else list(range(nb))
    rows = [slice(c * CH, (c + 1) * CH) for c in range(nb)]
    state = st[...]
    for c in order:
        gdec = jnp.exp(b_scr[pl.ds(c * CH + (0 if rev else CH - 1), 1), :])
        sd_ref[c] = state
        a = jnp.where(maskw, _dot_nt(qt[rows[c]], _stack_heads(kt[rows[c]], 6)), 0.0)
        o_inter = _lanes_by_head(_dot_nt(_stack_heads(qt[rows[c]], 6), state))
        o_ref[pl.ds(c * CH, CH), :] = _dot(a, _stack_heads(v[rows[c]], 7)) + o_inter
        state = state * gdec + _state_compact(_dot_tn(v[rows[c]], kt[rows[c]] * gdec))
        yield
    st[...] = state
    yield


def _gla_fwd(z, waf_pad, b_af, wab_pad, b_ab, riders=()):
    t = z.shape[0]
    tb = min(TB, t)
    nblk, nb = t // tb, tb // CH
    jmaps = (lambda i: i, lambda i: nblk - 1 - i)

    def body(qf, kf, vf, lrf, qr, kr, vr, lrr, wf, bf, wr, br, of_ref, sdf_ref, or_ref, sdr_ref,
             st_f, st_r, b_f, b_r):
        @pl.when(pl.program_id(0) == 0)
        def _():
            st_f[...] = jnp.zeros_like(st_f)
            st_r[...] = jnp.zeros_like(st_r)

        for _ in zip(_gla_fwd_block(qf, kf, vf, lrf, wf, bf, of_ref, sdf_ref, st_f, b_f, False, tb),
                     _gla_fwd_block(qr, kr, vr, lrr, wr, br, or_ref, sdr_ref, st_r, b_r, True, tb)):
            pass

    full = lambda i: (0, 0)
    zspecs = [s for jm in jmaps for s in (_zspec(tb, GK, ZB_Q, jm), _zspec(tb, GK, ZB_K, jm),
                                         _zspec(tb, GV, ZB_V, jm), _zspec(tb, 128, ZB_LR, jm))]
    wspecs = [pl.BlockSpec((128, GK), full), pl.BlockSpec((1, GK), full)] * 2
    out_specs = [s for jm in jmaps for s in (pl.BlockSpec((tb, GV), lambda i, jm=jm: (jm(i), 0)),
                                             pl.BlockSpec((nb, 128, GK), lambda i, jm=jm: (jm(i), 0, 0)))]
    out_shape = [jax.ShapeDtypeStruct((t, GV), F32), jax.ShapeDtypeStruct((t // CH, 128, GK), F32)] * 2
    scratch = [pltpu.VMEM((128, GK), F32), pltpu.VMEM((128, GK), F32), pltpu.VMEM((tb, GK), F32),
               pltpu.VMEM((tb, GK), F32)]
    return _call(body, "gla_fwd", (nblk,), zspecs + wspecs, out_specs, out_shape, scratch,
                 [z] * 8 + [waf_pad, b_af, wab_pad, b_ab], riders)


def _gla_bwd_chunks(do_ref, sd_ref, dst, b_scr, db_scr, dq_ref, dk_ref, dv_ref, qt, kt, e, ei, v, rev, nb):
    maskw = _wide_mask(rev)
    for c in (range(nb) if rev else reversed(range(nb))):
        sl = slice(c * CH, (c + 1) * CH)
        grow = c * CH + (0 if rev else CH - 1)
        gdec = jnp.exp(b_scr[pl.ds(grow, 1), :])
        qt_c, kt_c, v_c, do_c = qt[sl], kt[sl], v[sl], do_ref[pl.ds(c * CH, CH), :]
        s_in, ds_out = sd_ref[c], dst[...]
        kbd, vbd = _stack_heads(kt_c, 6), _stack_heads(v_c, 7)
        a = jnp.where(maskw, _dot_nt(qt_c, kbd), 0.0)
        da = jnp.where(maskw, _dot_nt(do_c, vbd), 0.0)
        dv_ref[pl.ds(c * CH, CH), :] = (_fold_heads(_dot_tn(a, do_c), 7)
                                        + _lanes_by_head(_dot_nt(_stack_heads(kt_c * gdec, 6), ds_out)))
        dqt = _dot(da, kbd) + _fold_heads(_dot(_rows_by_head(do_c), s_in), 6)
        dkh = _fold_heads(_dot(_rows_by_head(v_c), ds_out), 6)
        da_do = jnp.concatenate([da.astype(BF16), do_c.astype(BF16)], axis=1)
        both = _dot_tn(da_do, qt_c)
        dkt = _fold_heads(both[:NH * CH], 6) + dkh * gdec
        dg = jnp.sum(ds_out * s_in, axis=0, keepdims=True) + jnp.sum(kt_c * dkh, axis=0, keepdims=True)
        db_scr[pl.ds(c * CH, CH), :] = dqt * qt_c - dkt * kt_c
        db_scr[pl.ds(grow, 1), :] += dg * gdec
        dq_ref[pl.ds(c * CH, CH), :] = dqt * e[sl] * 0.125
        dk_ref[pl.ds(c * CH, CH), :] = dkt * ei[sl]
        dst[...] = ds_out * gdec + _state_compact(both[NH * CH:])
        yield


def _gate_bwd(db, pre, lr, wpad, rev, tb):
    dla = _chunk_cumsum(db, not rev, 2)
    dpre = dla * (1.0 / 16.0) / (1.0 + jnp.exp(pre))
    return dpre, _dot_nt(dpre, wpad), _dot_tn(lr, dpre)


def _gla_bwd_first(z, dy, o_pre, sd, wpad, bias, conv_w, conv_norm, gla_norm4, riders=()):
    t = z.shape[0]
    tb = min(TB_BWD, t)
    nblk, nb = t // tb, tb // CH
    jmap = lambda i: nblk - 1 - i

    def body(q_ref, k_ref, v_ref, lr_ref, g_ref, cb_ref, cc_ref, cu_ref, ccp_ref, ccn_ref, cup_ref, cun_ref,
             dy_ref, opre_ref, sd_ref, w_ref, bias_ref, cw_ref, cn_ref, gn_ref,
             do_ref, dq_ref, dk_ref, dv_ref, dlr_ref, dzg_ref, dzcb_ref, dconv_ref,
             dw_ref, dbias_ref, dcw_ref, dcn_ref, dgn_ref, dst, b_scr, db_scr):
        i = pl.program_id(0)
        j = jmap(i)

        @pl.when(i == 0)
        def _():
            dst[...] = jnp.zeros_like(dst)
            for ref in (dw_ref, dbias_ref, dcw_ref, dcn_ref, dgn_ref):
                ref[...] = jnp.zeros_like(ref)

        dyg = dy_ref[:, CW:]
        g = g_ref[...]
        sig = _sigmoid(g)
        on, rr = _head_norm(opre_ref[...])
        gn = gn_ref[...]
        dzg_ref[...] = (dyg * on * gn * (sig * (1.0 + g * (1.0 - sig)))).astype(BF16)
        don = dyg * (g * sig)
        _acc_rows(dgn_ref, jnp.sum(don * on, axis=0, keepdims=True))
        u = don * gn
        uo = u * on
        mean_uo = jnp.concatenate(
            [jnp.broadcast_to(jnp.mean(uo[:, h * 128:(h + 1) * 128], axis=-1, keepdims=True), (tb, 128))
             for h in range(NH)], axis=1)
        do_ref[...] = rr * (u - on * mean_uo)

        def conv_branch():
            cb = cb_ref[...]
            h, h_m1, h_p1, conv = _conv_parts(cb, cc_ref[...], cu_ref[...], ccp_ref[pl.ds(7, 1), :],
                                              cup_ref[pl.ds(7, 1), :], ccn_ref[pl.ds(0, 1), :],
                                              cun_ref[pl.ds(0, 1), :], cw_ref, j == 0, j == nblk - 1, tb)
            yc = cb * conv
            yield
            rc = lax.rsqrt(_group_sum(yc * yc) * (1.0 / 64.0) + EPS)
            ycr = yc * rc
            yield
            dyn = dy_ref[:, :CW]
            _acc_rows(dcn_ref, jnp.sum(dyn * ycr, axis=0, keepdims=True))
            uc = dyn * cn_ref[...]
            yield
            dyc = rc * (uc - ycr * (_group_sum(uc * ycr) * (1.0 / 64.0)))
            dzcb_ref[...] = (dyc * conv).astype(BF16)
            yield
            dconv = dyc * cb
            dconv_ref[...] = dconv
            yield
            dcw_ref[pl.ds(0, 1), :] += jnp.sum(dconv * h_m1, axis=0, keepdims=True)
            dcw_ref[pl.ds(1, 1), :] += jnp.sum(dconv * h, axis=0, keepdims=True)
            dcw_ref[pl.ds(2, 1), :] += jnp.sum(dconv * h_p1, axis=0, keepdims=True)
            yield

        lr, wp = lr_ref[...], w_ref[...]
        pre, b, e, ei, qt, kt = _gla_recompute(q_ref[...], k_ref[...], lr, wp, bias_ref[...], False, tb)
        b_scr[...] = b
        for _ in itertools.zip_longest(
                _gla_bwd_chunks(do_ref, sd_ref, dst, b_scr, db_scr, dq_ref, dk_ref, dv_ref, qt, kt, e, ei, v_ref[...],
                                False, nb), conv_branch()):
            pass
        dpre, dlr, dw = _gate_bwd(db_scr[...], pre, lr, wp, False, tb)
        dlr_ref[...] = dlr
        dw_ref[...] += dw
        _acc_rows(dbias_ref, jnp.sum(dpre, axis=0, keepdims=True))

    full = lambda i: (0, 0)
    tokv = pl.BlockSpec((tb, GV), lambda i: (jmap(i), 0))
    tokk = pl.BlockSpec((tb, GK), lambda i: (jmap(i), 0))
    ccp, ccn = _halo_specs(tb, nblk, t, ZB_CC, jmap)
    cup, cun = _halo_specs(tb, nblk, t, ZB_CU, jmap)
    in_specs = [_zspec(tb, GK, ZB_Q, jmap), _zspec(tb, GK, ZB_K, jmap), _zspec(tb, GV, ZB_V, jmap),
                _zspec(tb, 128, ZB_LR, jmap), _zspec(tb, GV, ZB_G, jmap), _zspec(tb, CW, ZB_CB, jmap),
                _zspec(tb, CW, ZB_CC, jmap), _zspec(tb, CW, ZB_CU, jmap), ccp, ccn, cup, cun,
                pl.BlockSpec((tb, D), lambda i: (jmap(i), 0)), tokv,
                pl.BlockSpec((nb, 128, GK), lambda i: (jmap(i), 0, 0)), pl.BlockSpec((128, GK), full),
                pl.BlockSpec((1, GK), full), pl.BlockSpec((3, CW), full), pl.BlockSpec((1, CW), full),
                pl.BlockSpec((1, GV), full)]
    out_specs = [tokv, tokk, tokk, tokv, pl.BlockSpec((tb, 128), lambda i: (jmap(i), 0)), tokv, tokv, tokv,
                 pl.BlockSpec((128, GK), full), pl.BlockSpec((8, GK), full), pl.BlockSpec((8, CW), full),
                 pl.BlockSpec((8, CW), full), pl.BlockSpec((8, GV), full)]
    out_shape = [jax.ShapeDtypeStruct((t, GV), F32), jax.ShapeDtypeStruct((t, GK), F32),
                 jax.ShapeDtypeStruct((t, GK), F32), jax.ShapeDtypeStruct((t, GV), F32),
                 jax.ShapeDtypeStruct((t, 128), F32), jax.ShapeDtypeStruct((t, GV), BF16),
                 jax.ShapeDtypeStruct((t, CW), BF16), jax.ShapeDtypeStruct((t, CW), F32),
                 jax.ShapeDtypeStruct((128, GK), F32), jax.ShapeDtypeStruct((8, GK), F32),
                 jax.ShapeDtypeStruct((8, CW), F32), jax.ShapeDtypeStruct((8, CW), F32),
                 jax.ShapeDtypeStruct((8, GV), F32)]
    return _call(
        body, "gla_bwd_first", (nblk,), in_specs, out_specs, out_shape,
        [pltpu.VMEM((128, GK), F32), pltpu.VMEM((tb, GK), F32), pltpu.VMEM((tb, GK), F32)],
        (z, z, z, z, z, z, z, z, z, z, z, z, dy, o_pre, sd, wpad, bias, conv_w, conv_norm, gla_norm4), riders)


def _gla_bwd_second(z, do, sd, wpad, bias, dqa, dka, dva, dlra, dzg, dzcb, dconv, conv_w, riders=()):
    t = z.shape[0]
    tb = min(TB_BWD, t)
    nblk, nb = t // tb, tb // CH
    jmap = lambda i: i

    def body(q_ref, k_ref, v_ref, lr_ref, cc_ref, cu_ref, do_ref, sd_ref, w_ref, bias_ref, dqa_ref, dka_ref,
             dva_ref, dlra_ref, dzg_ref, dzcb_ref, dc_ref, dcp_ref, dcn_ref, cw_ref,
             dz_ref, dw_ref, dbias_ref, dst, b_scr, db_scr, dq_scr, dk_scr, dv_scr, sb_scr, dsk_scr):
        i = pl.program_id(0)

        @pl.when(i == 0)
        def _():
            dst[...] = jnp.zeros_like(dst)
            dw_ref[...] = jnp.zeros_like(dw_ref)
            dbias_ref[...] = jnp.zeros_like(dbias_ref)

        q_raw, k, v, lr, wp = q_ref[...], k_ref[...], v_ref[...], lr_ref[...], w_ref[...]
        pre, b, e, ei, qt, kt = _gla_recompute(q_raw, k, lr, wp, bias_ref[...], True, tb)
        b_scr[...] = b

        def token_local():
            dc = dc_ref[...]
            rows = _iota(dc.shape, 0)
            dprev = jnp.where(i == 0, 0.0, dcp_ref[pl.ds(7, 1), :])
            dnext = jnp.where(i == nblk - 1, 0.0, dcn_ref[pl.ds(0, 1), :])
            dc_m1 = jnp.where(rows == 0, dprev, pltpu.roll(dc, 1, 0))
            dc_p1 = jnp.where(rows == tb - 1, dnext, pltpu.roll(dc, tb - 1, 0))
            yield
            dh = cw_ref[pl.ds(0, 1), :] * dc_p1 + cw_ref[pl.ds(1, 1), :] * dc + cw_ref[pl.ds(2, 1), :] * dc_m1
            dz_ref[:, 0:512] = dzcb_ref[...]
            yield
            dz_ref[:, 512:1024] = (dh * cu_ref[...]).astype(BF16)
            dz_ref[:, 1024:1536] = (dh * cc_ref[...]).astype(BF16)
            dz_ref[:, 2560:3072] = dzg_ref[...]
            yield
            sb_scr[...] = _head_sum((q_raw * 0.125) * k, 64, 128)
            yield
            dsk_scr[...] = _head_sum(do_ref[...] * v, 128, 64)
            yield

        for _ in itertools.zip_longest(
                _gla_bwd_chunks(do_ref, sd_ref, dst, b_scr, db_scr, dq_scr, dk_scr, dv_scr, qt, kt, e, ei, v, True, nb),
                token_local()):
            pass
        dpre, dlr, dw = _gate_bwd(db_scr[...], pre, lr, wp, True, tb)
        dw_ref[...] += dw
        _acc_rows(dbias_ref, jnp.sum(dpre, axis=0, keepdims=True))
        dsk = dsk_scr[...]
        dz_ref[:, 1536:1792] = (dqa_ref[...] + dq_scr[...] - dsk * k * 0.125).astype(BF16)
        dz_ref[:, 1792:2048] = (dka_ref[...] + dk_scr[...] - dsk * (q_raw * 0.125)).astype(BF16)
        dz_ref[:, 2048:2560] = (dva_ref[...] + dv_scr[...] - sb_scr[...] * do_ref[...]).astype(BF16)
        dz_ref[:, 3072:3200] = (dlra_ref[...] + dlr).astype(BF16)

    full = lambda i: (0, 0)
    tokv = pl.BlockSpec((tb, GV), lambda i: (i, 0))
    tokk = pl.BlockSpec((tb, GK), lambda i: (i, 0))
    dcp = pl.BlockSpec((8, CW), lambda i: (jnp.maximum(i * (tb // 8) - 1, 0), 0))
    dcn = pl.BlockSpec((8, CW), lambda i: (jnp.minimum((i + 1) * (tb // 8), t // 8 - 1), 0))
    in_specs = [_zspec(tb, GK, ZB_Q, jmap), _zspec(tb, GK, ZB_K, jmap), _zspec(tb, GV, ZB_V, jmap),
                _zspec(tb, 128, ZB_LR, jmap), _zspec(tb, CW, ZB_CC, jmap), _zspec(tb, CW, ZB_CU, jmap), tokv,
                pl.BlockSpec((nb, 128, GK), lambda i: (i, 0, 0)), pl.BlockSpec((128, GK), full),
                pl.BlockSpec((1, GK), full), tokk, tokk, tokv, pl.BlockSpec((tb, 128), lambda i: (i, 0)), tokv, tokv,
                tokv, dcp, dcn, pl.BlockSpec((3, CW), full)]
    out_specs = [pl.BlockSpec((tb, ZC), lambda i: (i, 0)), pl.BlockSpec((128, GK), full), pl.BlockSpec((8, GK), full)]
    out_shape = [jax.ShapeDtypeStruct((t, ZC), BF16), jax.ShapeDtypeStruct((128, GK), F32),
                 jax.ShapeDtypeStruct((8, GK), F32)]
    return _call(
        body, "gla_bwd_second", (nblk,), in_specs, out_specs, out_shape,
        [pltpu.VMEM((128, GK), F32), pltpu.VMEM((tb, GK), F32), pltpu.VMEM((tb, GK), F32),
         pltpu.VMEM((tb, GK), F32), pltpu.VMEM((tb, GK), F32), pltpu.VMEM((tb, GV), F32),
         pltpu.VMEM((tb, GV), F32), pltpu.VMEM((tb, GK), F32)],
        (z, z, z, z, z, z, do, sd, wpad, bias, dqa, dka, dva, dlra, dzg, dzcb, dconv, dconv, dconv, conv_w), riders)


def _step(x, mem, target, shard, small_pack, vec, place):
    own, from_chips = {}, {}

    def pair_sums(names, g4, from_sibling):
        pbs = {}
        for shape in dict.fromkeys(g.shape for g in g4):
            idx = [i for i, g in enumerate(g4) if g.shape == shape]
            pb, mine = _rs_pair_sum(place, [g4[i] for i in idx], [from_sibling[i] for i in idx],
                                    "pair_sum_" + "_".join(names[i] for i in idx))
            for i, b, o in zip(idx, pb, mine):
                pbs[i], own[names[i]] = b, o
        return [pbs[i] for i in range(len(g4))]

    def by_dest(g, n):
        return g.reshape((4, 2) + shard[n].shape)

    hb, w_in, small_all = _prenorm(x, vec["mix_norm"],
                                   [_gather_rider([shard["w_in"], small_pack], early_relay=False)])
    w_in = jnp.pad(w_in.reshape(ZW, D), ((0, ZC - ZW), (0, 0)))
    small_all = small_all.reshape(NDEV, -1)
    p, off = {}, 0
    for n, (r, c) in SMALL_SHARDED.items():
        p[n] = small_all[:, off:off + r * c].reshape(NDEV, r, c).transpose(1, 0, 2).reshape(r, NDEV * c)
        off += r * c
    zeros_lr = jnp.zeros((128 - LR, GK), BF16)
    waf_pad = jnp.concatenate([p["w_af"].astype(BF16), zeros_lr], axis=0)
    wab_pad = jnp.concatenate([jnp.zeros((LR, GK), BF16), p["w_ab"].astype(BF16), zeros_lr[:128 - 2 * LR]], axis=0)
    gla_norm4 = jnp.tile(vec["gla_norm"], (1, NH))

    z, w_out, w_xq, w_xo, w_xkv = _inproj(
        hb, w_in, [_gather_rider([shard[n] for n in ("w_out", "w_xq", "w_xo", "w_xkv")])])
    w_out, w_xq, w_xo = [a.reshape(D, D) for a in (w_out, w_xq, w_xo)]
    o_f, sd_f, o_b, sd_b, w_up_t, w_down = _gla_fwd(
        z, waf_pad, vec["b_af"], wab_pad, vec["b_ab"],
        [_gather_rider([shard["w_up"], shard["w_down"]], early_relay=False)])
    w_up_t, w_down = w_up_t.reshape(FF, D), w_down.reshape(FF, D)
    kv, memn = _kv_proj(mem, vec["mem_norm"], w_xkv)
    kb, vb = kv[:, :D].astype(BF16), kv[:, D:].astype(BF16)
    x1, x2, xn1, qb, attb, yb, o_pre = _attn_fwd(x, z, o_f, o_b, p["conv_w"], vec["conv_norm"], gla_norm4, w_out,
                                                 vec["xa_norm"], w_xq, kb, vb, w_xo)
    h1b, xn2, dx3, dx3b, loss8, dfinal = _mlp_fwd(x2, vec["mlp_norm"], w_up_t, w_down, vec["final_norm"], target)

    ab, dh1b, dx2, dx2b, dmlp = _mlp_bwd(dx3, dx3b, h1b, w_down, w_up_t, x2, vec["mlp_norm"])
    g_mlp = [by_dest(_matmul_tn(ab, dx3b, "dw_down")[0], "w_down"),
             by_dest(_matmul_tn(dh1b, xn2, "dw_up")[0], "w_up")]
    dx1, dx1b, dy, dqb, dkv, dxa, *s_mlp = _attn_bwd(x1, dx2, dx2b, qb, kb, vb, w_xo, w_xq, w_out, vec["xa_norm"],
                                                     riders=[_sibling_rider(g_mlp)])
    pb_mlp = pair_sums(("w_down", "w_up"), g_mlp, s_mlp)
    dw_xo = _matmul_tn(attb, dx2b, "dw_xo")[0]
    dw_xkv, dmemn = _kv_bwd(dkv, memn, mem, vec["mem_norm"], w_xkv)
    att_names = ("w_xo", "w_xq", "w_out", "w_xkv")
    g_att = [by_dest(g, n) for g, n in zip(
        (dw_xo, _matmul_tn(xn1, dqb, "dw_xq")[0], _matmul_tn(yb, dx1b, "dw_out")[0], dw_xkv), att_names)]
    res = _gla_bwd_first(z, dy, o_pre, sd_f, waf_pad, vec["b_af"], p["conv_w"], vec["conv_norm"], gla_norm4,
                         riders=[_chips_rider(pb_mlp), _sibling_rider(g_att)])
    do, dqa, dka, dva, dlra, dzg, dzcb, dconv, dwaf, dbaf, dcw, dcn, dgn = res[:13]
    from_chips["w_down"], from_chips["w_up"] = res[13:15]
    pb_att = pair_sums(att_names, g_att, res[15:])
    dz, dwab, dbab, *c_att = _gla_bwd_second(z, do, sd_b, wab_pad, vec["b_ab"], dqa, dka, dva, dlra, dzg, dzcb, dconv,
                                             p["conv_w"], riders=[_chips_rider(pb_att)])
    from_chips.update(zip(att_names, c_att))
    g_in = [by_dest(_matmul_tn(dz, hb, "dw_in", rows=ZW)[0], "w_in")]
    pb_in = pair_sums(("w_in",), g_in, _exchange(_sibling_rider(g_in), "grads_to_sibling_w_in"))
    grad_x, dmix, from_chips["w_in"] = _inproj_bwd(dz, w_in, x, dx1, vec["mix_norm"], riders=[_chips_rider(pb_in)])

    small_acc = dict(mix_norm=dmix, conv_w=dcw, conv_norm=dcn, w_af=dwaf, b_af=dbaf, w_ab=dwab, b_ab=dbab,
                     gla_norm=dgn, xa_norm=dxa, mem_norm=dmemn, mlp_norm=dmlp, final_norm=dfinal)
    return loss8, grad_x, small_acc, own, from_chips


def _place():
    return lax.axis_index("x"), lax.axis_index("y"), lax.axis_index("c")


class _Rider:
    def __init__(self, arrays, out_shape, scratch, start, finish, relay=None):
        self.arrays, self.out_shape, self.scratch, self.start, self.finish = arrays, out_shape, scratch, start, finish
        self.relay = relay


def _gather_rider(blks, early_relay=True):
    n = len(blks)

    def plan(in_refs, out_refs, sems):
        send_sems, recv_sems, local_sems = sems
        x, y, c = _place()
        me, sibling = (x, y, c), (x, y, 1 - c)
        chips = [(1 - x, y, c), (x, 1 - y, c), (1 - x, 1 - y, c)]

        def copy(a, k, block, to, own=False):
            px, py, pc = block
            dst = out_refs[a].at[4 * px + 2 * py + pc]
            return pltpu.make_async_remote_copy(
                src_ref=in_refs[a] if own else dst, dst_ref=dst, send_sem=send_sems.at[k, a],
                recv_sem=recv_sems.at[k, a], device_id=to, device_id_type=MESH)

        def local(a):
            return pltpu.make_async_copy(in_refs[a], out_refs[a].at[4 * x + 2 * y + c], local_sems.at[a])

        def own_sends(a):
            return [copy(a, 0, me, sibling, own=True)] + [copy(a, 1 + j, me, chip, own=True)
                                                          for j, chip in enumerate(chips)]

        return copy, local, own_sends, me, sibling, chips

    def start(in_refs, out_refs, sems):
        _, local, own_sends, _, _, _ = plan(in_refs, out_refs, sems)
        for a in range(n):
            local(a).start()
            for cp in own_sends(a):
                cp.start()

    def relay(in_refs, out_refs, sems):
        copy, _, _, me, sibling, chips = plan(in_refs, out_refs, sems)
        for j, chip in enumerate(chips):
            for a in range(n):
                copy(a, 1 + j, chip, me).wait_recv()
                copy(a, 4 + j, chip, sibling).start()

    def finish(in_refs, out_refs, sems):
        if not early_relay:
            relay(in_refs, out_refs, sems)
        copy, local, own_sends, me, sibling, chips = plan(in_refs, out_refs, sems)
        for a in range(n):
            copy(a, 0, sibling, me).wait_recv()
            for j, (px, py, pc) in enumerate(chips):
                copy(a, 4 + j, (px, py, 1 - pc), me).wait_recv()
            for cp in own_sends(a) + [copy(a, 4 + j, chip, sibling) for j, chip in enumerate(chips)]:
                cp.wait_send()
            local(a).wait()

    return _Rider(blks, [jax.ShapeDtypeStruct((NDEV,) + b.shape, b.dtype) for b in blks],
                  [pltpu.SemaphoreType.DMA((7, n)), pltpu.SemaphoreType.DMA((7, n)), pltpu.SemaphoreType.DMA((n,))],
                  start, finish, relay if early_relay else None)


def _sibling_rider(g4s):
    n = len(g4s)

    def copies(in_refs, out_refs, sems):
        send_sems, recv_sems = sems
        x, y, c = _place()
        return [pltpu.make_async_remote_copy(
            src_ref=in_refs[a].at[k, 1 - c], dst_ref=out_refs[a].at[k], send_sem=send_sems.at[k, a],
            recv_sem=recv_sems.at[k, a], device_id=(x, y, 1 - c), device_id_type=MESH)
            for a in range(n) for k in range(4)]

    def start(in_refs, out_refs, sems):
        for cp in copies(in_refs, out_refs, sems):
            cp.start()

    def finish(in_refs, out_refs, sems):
        for cp in copies(in_refs, out_refs, sems):
            cp.wait()

    return _Rider(g4s, [jax.ShapeDtypeStruct((4,) + g.shape[2:], g.dtype) for g in g4s],
                  [pltpu.SemaphoreType.DMA((4, n)), pltpu.SemaphoreType.DMA((4, n))], start, finish)


def _chips_rider(pbs):
    n = len(pbs)

    def copies(in_refs, out_refs, sems):
        send_sems, recv_sems = sems
        x, y, c = _place()
        peers = [(1 - x, y), (x, 1 - y), (1 - x, 1 - y)]
        return [pltpu.make_async_remote_copy(
            src_ref=in_refs[a].at[2 * px + py], dst_ref=out_refs[a].at[k], send_sem=send_sems.at[k, a],
            recv_sem=recv_sems.at[k, a], device_id=(px, py, c), device_id_type=MESH)
            for a in range(n) for k, (px, py) in enumerate(peers)]

    def start(in_refs, out_refs, sems):
        for cp in copies(in_refs, out_refs, sems):
            cp.start()

    def finish(in_refs, out_refs, sems):
        for cp in copies(in_refs, out_refs, sems):
            cp.wait()

    return _Rider(pbs, [jax.ShapeDtypeStruct((3,) + p.shape[1:], p.dtype) for p in pbs],
                  [pltpu.SemaphoreType.DMA((3, n)), pltpu.SemaphoreType.DMA((3, n))], start, finish)


def _exchange(rider, name):
    n_in, n_out = len(rider.arrays), len(rider.out_shape)

    def body(*refs):
        ins, outs, sems = refs[:n_in], refs[n_in:n_in + n_out], refs[n_in + n_out:]
        rider.start(ins, outs, sems)
        if rider.relay:
            rider.relay(ins, outs, sems)
        rider.finish(ins, outs, sems)

    hbm = pl.BlockSpec(memory_space=pltpu.HBM)
    return pl.pallas_call(body, name=name, out_shape=rider.out_shape, in_specs=[hbm] * n_in,
                          out_specs=[hbm] * n_out, scratch_shapes=rider.scratch)(*rider.arrays)


def _rs_pair_sum(place, g4s, r1s, name):
    n = len(g4s)
    rows, cols = g4s[0].shape[2:]
    tr = min(rows, 512)

    def body(pl_ref, *refs):
        for g_ref, r_ref, pb_ref, own_ref in zip(refs[:n], refs[n:2 * n], refs[2 * n:3 * n], refs[3 * n:]):
            s = g_ref[0, 0] + r_ref[0]
            pb_ref[0] = s.astype(BF16)

            @pl.when(pl.program_id(1) == pl_ref[0])
            def _():
                own_ref[...] = s

    grid_spec = pltpu.PrefetchScalarGridSpec(
        num_scalar_prefetch=1, grid=(rows // tr, 4),
        in_specs=[pl.BlockSpec((1, 1, tr, cols), lambda r, k, p: (k, p[1], r, 0))] * n
        + [pl.BlockSpec((1, tr, cols), lambda r, k, p: (k, r, 0))] * n,
        out_specs=[pl.BlockSpec((1, tr, cols), lambda r, k, p: (k, r, 0))] * n
        + [pl.BlockSpec((tr, cols), lambda r, k, p: (r, 0))] * n)
    res = pl.pallas_call(
        body, name=name, grid_spec=grid_spec,
        out_shape=[jax.ShapeDtypeStruct((4, rows, cols), BF16)] * n + [jax.ShapeDtypeStruct((rows, cols), F32)] * n,
        compiler_params=_cparams(("arbitrary", "arbitrary")))(place, *g4s, *r1s)
    return res[:n], res[n:]


PACK_ROWS = 32
VEC_ROW = {"mix_norm": 0, "conv_norm": 1, "b_af": 2, "b_ab": 3, "gla_norm": 4, "xa_norm": 5, "mem_norm": 6,
           "mlp_norm": 7, "final_norm": 8}
LOSS_ROW, MAT_ROW = 9, 16
MAT_LANE = {"w_af": 0, "w_ab": GK, "conv_w": 2 * GK}
MAT_SRC_ROW = {"w_af": 0, "w_ab": LR, "conv_w": 0}


SMALL_WIDTH = {"mix_norm": D, "conv_w": 64, "conv_norm": CW, "w_af": 32, "b_af": GK, "w_ab": 32, "b_ab": GK,
               "gla_norm": 128, "xa_norm": D, "mem_norm": D, "mlp_norm": D, "final_norm": D}


def _small_reduce(acc, loss8):
    names = list(SMALL)
    n = len(names)
    widths = SMALL_WIDTH

    def body(*refs):
        acc_refs = dict(zip(names, refs[:n]))
        loss_ref, tot = refs[n], refs[n + 1]
        pk, all_ref, send_sems, recv_sems, local_sem = refs[n + 2:]

        pk[...] = jnp.zeros_like(pk)
        for k, row in VEC_ROW.items():
            if k == "gla_norm":
                g = functools.reduce(lambda a, b: a + b, [acc_refs[k][pl.ds(0, 1), pl.ds(h * 128, 128)]
                                                          for h in range(NH)])
            else:
                g = acc_refs[k][pl.ds(0, 1), :]
            pk[pl.ds(row, 1), pl.ds(0, widths[k])] = g
        pk[pl.ds(LOSS_ROW, 1), pl.ds(0, 128)] = loss_ref[pl.ds(0, 1), :]
        for k, lane in MAT_LANE.items():
            rows, cols = (3, CW) if k == "conv_w" else (LR, GK)
            pk[pl.ds(MAT_ROW, rows), pl.ds(lane, cols)] = acc_refs[k][pl.ds(MAT_SRC_ROW[k], rows), :]

        x, y, c = _place()
        me, sibling = (x, y, c), (x, y, 1 - c)
        chips = [(1 - x, y, c), (x, 1 - y, c), (1 - x, 1 - y, c)]

        def copy(k, block, to, own=False):
            px, py, pc = block
            dst = all_ref.at[4 * px + 2 * py + pc]
            return pltpu.make_async_remote_copy(
                src_ref=pk if own else dst, dst_ref=dst, send_sem=send_sems.at[k], recv_sem=recv_sems.at[k],
                device_id=to, device_id_type=MESH)

        mine = pltpu.make_async_copy(pk, all_ref.at[4 * x + 2 * y + c], local_sem)
        mine.start()
        first = [copy(0, me, sibling, own=True)] + [copy(1 + j, me, chip, own=True) for j, chip in enumerate(chips)]
        for cp in first:
            cp.start()
        passed = [copy(4 + j, chip, sibling) for j, chip in enumerate(chips)]
        for j, chip in enumerate(chips):
            copy(1 + j, chip, me).wait_recv()
            passed[j].start()
        copy(0, sibling, me).wait_recv()
        for j, (px, py, pc) in enumerate(chips):
            copy(4 + j, (px, py, 1 - pc), me).wait_recv()
        for cp in first + passed:
            cp.wait_send()
        mine.wait()
        total = all_ref[0]
        for d in range(1, NDEV):
            total = total + all_ref[d]
        tot[...] = total

    return pl.pallas_call(
        body, name="small_reduce", out_shape=jax.ShapeDtypeStruct((PACK_ROWS, D), F32),
        scratch_shapes=[pltpu.VMEM((PACK_ROWS, D), F32), pltpu.VMEM((NDEV, PACK_ROWS, D), F32),
                        pltpu.SemaphoreType.DMA((7,)), pltpu.SemaphoreType.DMA((7,)), pltpu.SemaphoreType.DMA],
    )(*[acc[k] for k in names], loss8)


def _small_adamw(tot, ws, ms, vs):
    names = list(SMALL)
    n = len(names)
    widths = SMALL_WIDTH

    def body(*refs):
        tot = refs[0]
        w_refs, m_refs, v_refs = [dict(zip(names, refs[1 + q * n:1 + (q + 1) * n])) for q in range(3)]
        outs = refs[1 + 3 * n:1 + 7 * n]
        g_out, d_out, m_out, v_out = [dict(zip(names, outs[q * n:(q + 1) * n])) for q in range(4)]
        cut = refs[1 + 7 * n]
        x, y, c = _place()
        dev = 4 * x + 2 * y + c
        for k in names:
            if k in VEC_ROW:
                g = tot[pl.ds(VEC_ROW[k], 1), pl.ds(0, widths[k])]
            else:
                rows, cols = (3, CW) if k == "conv_w" else (LR, GK)
                wd = widths[k]
                sel = jnp.where(_iota((cols, wd), 0) == dev * wd + _iota((cols, wd), 1), 1.0, 0.0).astype(BF16)
                cut[:, pl.ds(0, wd)] = _dot_exact_rhs(tot[pl.ds(MAT_ROW, LR), pl.ds(MAT_LANE[k], cols)], sel, 3)
                g = cut[pl.ds(0, rows), pl.ds(0, wd)]
            g_out[k][...] = g
            d_out[k][...], m_out[k][...], v_out[k][...] = _adamw_math(w_refs[k][...], g, m_refs[k][...],
                                                                       v_refs[k][...])

    shapes = [jax.ShapeDtypeStruct(ws[k].shape, F32) for k in names]
    res = pl.pallas_call(
        body, name="small_adamw", out_shape=shapes * 4, scratch_shapes=[pltpu.VMEM((LR, 128), F32)],
    )(tot, *[ws[k] for k in names], *[ms[k] for k in names], *[vs[k] for k in names])
    return {k: tuple(res[q * n + i] for q in range(4)) for i, k in enumerate(names)}


def _adamw_math(w, g, m, v):
    m = ADAM_B1 * m + (1.0 - ADAM_B1) * g
    v = ADAM_B2 * v + (1.0 - ADAM_B2) * (g * g)
    m_hat = m / (1.0 - ADAM_B1 ** ADAM_STEP)
    v_hat = v / (1.0 - ADAM_B2 ** ADAM_STEP)
    delta = -ADAM_LR * (m_hat / (jnp.sqrt(v_hat) + ADAM_EPS) + ADAM_WD * w)
    return delta, m, v


def _adamw(ws, ms, vs, owns, r2s, name, grads_transposed=False):
    n = len(ws)
    _, r, c = ws[0].shape
    tr = 256 if r % 256 == 0 else r

    def body(*refs):
        ins, outs = refs[:5 * n], refs[5 * n:]
        for q in range(n):
            w_ref, m_ref, v_ref, o_ref, r_ref = [ins[k * n + q] for k in range(5)]
            g_ref, d_ref, nm_ref, nv_ref = [outs[k * n + q] for k in range(4)]
            g = ((o_ref[...] + r_ref[0].astype(F32)) + r_ref[1].astype(F32)) + r_ref[2].astype(F32)
            g = g.T if grads_transposed else g
            g_ref[...] = g
            d_ref[...], nm_ref[...], nv_ref[...] = _adamw_math(w_ref[...], g, m_ref[...], v_ref[...])

    spec = pl.BlockSpec((None, tr, c), lambda i: (0, i, 0))
    if grads_transposed:
        own_spec, r2_spec = pl.BlockSpec((c, tr), lambda i: (0, i)), pl.BlockSpec((3, c, tr), lambda i: (0, 0, i))
    else:
        own_spec, r2_spec = pl.BlockSpec((tr, c), lambda i: (i, 0)), pl.BlockSpec((3, tr, c), lambda i: (0, i, 0))
    res = pl.pallas_call(
        body, name=name, grid=(r // tr,),
        in_specs=[spec] * (3 * n) + [own_spec] * n + [r2_spec] * n,
        out_specs=[spec] * (4 * n), out_shape=[jax.ShapeDtypeStruct((1, r, c), F32)] * (4 * n),
        compiler_params=_cparams(("arbitrary",)))(*ws, *ms, *vs, *owns, *r2s)
    return [tuple(res[k * n + q] for k in range(4)) for q in range(n)]


MATS = ("w_in", "w_out", "w_xq", "w_xo", "w_xkv", "w_up", "w_down")
SMALL = ("mix_norm", "conv_w", "conv_norm", "w_af", "b_af", "w_ab", "b_ab", "gla_norm", "xa_norm", "mem_norm",
         "mlp_norm", "final_norm")
WEIGHTS = ("mix_norm", "w_in", "conv_w", "conv_norm", "w_af", "b_af", "w_ab", "b_ab", "gla_norm", "w_out", "xa_norm",
           "mem_norm", "w_xq", "w_xkv", "w_xo", "mlp_norm", "w_up", "w_down", "final_norm")
SMALL_SHARDED = {"conv_w": (3, 64), "w_af": (LR, 32), "w_ab": (LR, 32)}
SMALL_PACK_ROWS = 16


def kernel(x, mem, mix_norm, w_in, conv_w, conv_norm, w_af, b_af, w_ab, b_ab, gla_norm, w_out, xa_norm, mem_norm, w_xq, w_xkv, w_xo, mlp_norm, w_up, w_down, final_norm, loss_target, m_mix_norm, m_w_in, m_conv_w, m_conv_norm, m_w_af, m_b_af, m_w_ab, m_b_ab, m_gla_norm, m_w_out, m_xa_norm, m_mem_norm, m_w_xq, m_w_xkv, m_w_xo, m_mlp_norm, m_w_up, m_w_down, m_final_norm, v_mix_norm, v_w_in, v_conv_w, v_conv_norm, v_w_af, v_b_af, v_w_ab, v_b_ab, v_gla_norm, v_w_out, v_xa_norm, v_mem_norm, v_w_xq, v_w_xkv, v_w_xo, v_mlp_norm, v_w_up, v_w_down, v_final_norm):
    w = dict(mix_norm=mix_norm, w_in=w_in, conv_w=conv_w, conv_norm=conv_norm, w_af=w_af, b_af=b_af, w_ab=w_ab,
             b_ab=b_ab, gla_norm=gla_norm, w_out=w_out, xa_norm=xa_norm, mem_norm=mem_norm, w_xq=w_xq, w_xkv=w_xkv,
             w_xo=w_xo, mlp_norm=mlp_norm, w_up=w_up, w_down=w_down, final_norm=final_norm)
    mom = dict(mix_norm=m_mix_norm, w_in=m_w_in, conv_w=m_conv_w, conv_norm=m_conv_norm, w_af=m_w_af, b_af=m_b_af,
               w_ab=m_w_ab, b_ab=m_b_ab, gla_norm=m_gla_norm, w_out=m_w_out, xa_norm=m_xa_norm, mem_norm=m_mem_norm,
               w_xq=m_w_xq, w_xkv=m_w_xkv, w_xo=m_w_xo, mlp_norm=m_mlp_norm, w_up=m_w_up, w_down=m_w_down,
               final_norm=m_final_norm)
    var = dict(mix_norm=v_mix_norm, w_in=v_w_in, conv_w=v_conv_w, conv_norm=v_conv_norm, w_af=v_w_af, b_af=v_b_af,
               w_ab=v_w_ab, b_ab=v_b_ab, gla_norm=v_gla_norm, w_out=v_w_out, xa_norm=v_xa_norm, mem_norm=v_mem_norm,
               w_xq=v_w_xq, w_xkv=v_w_xkv, w_xo=v_w_xo, mlp_norm=v_mlp_norm, w_up=v_w_up, w_down=v_w_down,
               final_norm=v_final_norm)
    xi, yi, ci = _place()
    two_d = lambda a: a.reshape(a.shape[-2:]) if a.ndim == 3 else a.reshape(1, a.shape[-1])

    small = jnp.concatenate([w[n].reshape(-1) for n in SMALL_SHARDED])
    small = jnp.pad(small, (0, SMALL_PACK_ROWS * 128 - small.shape[0])).reshape(SMALL_PACK_ROWS, 128)
    shard = {n: two_d(w[n]).astype(BF16) for n in MATS}
    for n in ("w_in", "w_up"):
        shard[n] = shard[n].T
    vec = {n: two_d(w[n]) for n in SMALL if n not in SMALL_SHARDED}
    place = jnp.stack([2 * xi + yi, ci]).astype(jnp.int32)
    loss8, grad_x, small_acc, own, from_chips = _step(x[0], mem[0], loss_target[0], shard, small, vec, place)

    tot = _small_reduce(small_acc, loss8)
    small_out = _small_adamw(tot, *[{n: two_d(d[n]) for n in SMALL} for d in (w, mom, var)])
    loss = tot[LOSS_ROW, 0]

    out_g, out_d, out_m, out_v = {}, {}, {}, {}
    wmv = {n: [a.transpose(0, 2, 1) if n == "w_in" else a for a in (w[n], mom[n], var[n])] for n in MATS}
    for shape in dict.fromkeys(wmv[n][0].shape for n in MATS):
        names = [n for n in MATS if wmv[n][0].shape == shape]
        res = _adamw(*[[wmv[n][k] for n in names] for k in range(3)], [own[n] for n in names],
                     [from_chips[n] for n in names], "adamw_" + "_".join(names), grads_transposed=names == ["w_up"])
        for n, r in zip(names, res):
            out_g[n], out_d[n], out_m[n], out_v[n] = [a.transpose(0, 2, 1) for a in r] if n == "w_in" else r
    for n in SMALL:
        out_g[n], out_d[n], out_m[n], out_v[n] = [a.reshape(w[n].shape) for a in small_out[n]]

    return (loss, grad_x[None], *[out_g[n] for n in WEIGHTS], *[out_d[n] for n in WEIGHTS],
            *[out_m[n] for n in WEIGHTS], *[out_v[n] for n in WEIGHTS])
```

```python
import functools
import itertools

import jax
import jax.numpy as jnp
from jax import lax
from jax.experimental import pallas as pl
from jax.experimental.pallas import tpu as pltpu

F32 = jnp.float32
BF16 = jnp.bfloat16

D = 1024
CW = 512
GK = 256
GV = 512
NH = 4
CH = 64
LR = 16
NMEM = 256
XD = 256
FF = 4096
ZW = 3104
ZC = 3200
EPS = 1e-6
NDEV = 8

ZB_CB, ZB_CC, ZB_CU, ZB_V, ZB_G = 0, 1, 2, 4, 5
ZB_Q, ZB_K = 6, 7
ZB_LR = 24

TM = 512
TM_MLP = 256
TM_MLP_FWD = 512
TF = 512
TB = 512
TB_BWD = 512
TT = 2048
VMEM_LIMIT = 56 * 1024 * 1024

ADAM_LR, ADAM_B1, ADAM_B2, ADAM_EPS, ADAM_WD, ADAM_STEP = 0.001, 0.9, 0.999, 1e-08, 0.01, 10

XKV_SHARD = 2 * D // NDEV

MESH = pl.DeviceIdType.MESH


def _cparams(sem):
    return pltpu.CompilerParams(dimension_semantics=sem, vmem_limit_bytes=VMEM_LIMIT)


def _call(body, name, grid, in_specs, out_specs, out_shape, scratch, args, riders=()):
    n_in, n_out, n_scr = len(in_specs), len(out_specs), len(scratch)
    counts = [(len(r.arrays), len(r.out_shape), len(r.scratch)) for r in riders]

    def take(refs, pos, sizes):
        groups = []
        for size in sizes:
            groups.append(refs[pos:pos + size])
            pos += size
        return groups, pos

    def wrapped(*refs):
        ins, pos = refs[:n_in], n_in
        r_ins, pos = take(refs, pos, [c[0] for c in counts])
        outs, pos = refs[pos:pos + n_out], pos + n_out
        r_outs, pos = take(refs, pos, [c[1] for c in counts])
        scr, pos = refs[pos:pos + n_scr], pos + n_scr
        r_scr, pos = take(refs, pos, [c[2] for c in counts])
        ids = [pl.program_id(d) for d in range(len(grid))]
        first = functools.reduce(lambda a, b: a & b, [i == 0 for i in ids])
        last = functools.reduce(lambda a, b: a & b, [i == g - 1 for i, g in zip(ids, grid)])

        @pl.when(first)
        def _():
            for r, a, b, c in zip(riders, r_ins, r_outs, r_scr):
                r.start(a, b, c)

        body(*ins, *outs, *scr)

        if any(r.relay for r in riders):
            at = [max(g - 2, 0) for g in grid]

            @pl.when(functools.reduce(lambda a, b: a & b, [i == s for i, s in zip(ids, at)]))
            def _():
                for r, a, b, c in zip(riders, r_ins, r_outs, r_scr):
                    if r.relay:
                        r.relay(a, b, c)

        @pl.when(last)
        def _():
            for r, a, b, c in zip(riders, r_ins, r_outs, r_scr):
                r.finish(a, b, c)

    hbm = pl.BlockSpec(memory_space=pltpu.HBM)
    r_args = [a for r in riders for a in r.arrays]
    r_shapes = [s for r in riders for s in r.out_shape]
    return pl.pallas_call(
        wrapped if riders else body, name=name, grid=grid, in_specs=list(in_specs) + [hbm] * len(r_args),
        out_specs=list(out_specs) + [hbm] * len(r_shapes), out_shape=list(out_shape) + r_shapes,
        scratch_shapes=list(scratch) + [s for r in riders for s in r.scratch],
        compiler_params=_cparams(("arbitrary",) * len(grid)))(*args, *r_args)


def _dot(a, b):
    return jnp.dot(a.astype(BF16), b.astype(BF16), preferred_element_type=F32)


def _dot_nt(a, b):
    return lax.dot_general(a.astype(BF16), b.astype(BF16), (((1,), (1,)), ((), ())), preferred_element_type=F32)


def _dot_tn(a, b):
    return lax.dot_general(a.astype(BF16), b.astype(BF16), (((0,), (0,)), ((), ())), preferred_element_type=F32)


def _split(x, n):
    parts = []
    for _ in range(n):
        p = x.astype(BF16)
        parts.append(p)
        x = x - p.astype(F32)
    return parts


def _dot_exact_lhs(m, x, n):
    return functools.reduce(lambda a, b: a + b, [jnp.dot(m, p, preferred_element_type=F32) for p in _split(x, n)])


def _dot_exact_rhs(x, m, n):
    return functools.reduce(lambda a, b: a + b, [jnp.dot(p, m, preferred_element_type=F32) for p in _split(x, n)])


def _rms(x, g):
    r = lax.rsqrt(jnp.mean(x * x, axis=-1, keepdims=True) + EPS)
    return x * r * g, r


def _rms_bwd(x, r, g, dy):
    xr = x * r
    u = dy * g
    dx = r * (u - xr * jnp.mean(u * xr, axis=-1, keepdims=True))
    return dx, jnp.sum(dy * xr, axis=0, keepdims=True)


def _iota(shape, dim):
    return lax.broadcasted_iota(jnp.int32, shape, dim)


def _sigmoid(x):
    return 1.0 / (1.0 + jnp.exp(-x))


def _acc_rows(ref, row):
    ref[...] += jnp.broadcast_to(row, ref.shape)


def _inproj(x, g, w_t, riders=()):
    t = x.shape[0]
    tm = min(TM, t)

    def body(x_ref, g_ref, w_ref, z_ref, h_ref):
        h, _ = _rms(x_ref[...], g_ref[...])
        hb = h.astype(BF16)
        h_ref[...] = hb
        z_ref[...] = _dot_nt(hb, w_ref[...])

    return _call(
        body, "inproj", (t // tm,),
        [pl.BlockSpec((tm, D), lambda i: (i, 0)), pl.BlockSpec((1, D), lambda i: (0, 0)),
         pl.BlockSpec((ZC, D), lambda i: (0, 0))],
        [pl.BlockSpec((tm, ZC), lambda i: (i, 0)), pl.BlockSpec((tm, D), lambda i: (i, 0))],
        [jax.ShapeDtypeStruct((t, ZC), F32), jax.ShapeDtypeStruct((t, D), BF16)], [], (x, g, w_t), riders)


def _kv_proj(mem, g, w):
    def body(m_ref, g_ref, w_ref, kv_ref, mn_ref):
        mn, _ = _rms(m_ref[...], g_ref[...])
        mb = mn.astype(BF16)
        mn_ref[...] = mb
        for j in range(NDEV):
            kv_ref[:, j * XKV_SHARD:(j + 1) * XKV_SHARD] = jnp.dot(mb, w_ref[j], preferred_element_type=F32)

    return pl.pallas_call(
        body, name="kv_proj",
        out_shape=[jax.ShapeDtypeStruct((NMEM, 2 * D), F32), jax.ShapeDtypeStruct((NMEM, D), BF16)],
        compiler_params=pltpu.CompilerParams(vmem_limit_bytes=VMEM_LIMIT))(mem, g, w)


def _softmax_head(qb, kb):
    s = _dot_nt(qb, kb) * (1.0 / 16.0)
    e = jnp.exp(s - jnp.max(s, axis=-1, keepdims=True))
    return e / jnp.sum(e, axis=-1, keepdims=True)


def _attn_fwd(x, z, o_f, o_b, conv_w, conv_norm, gla_norm4, w_out, g, w_xq, kb, vb, w_xo):
    t = x.shape[0]
    tm = min(TM, t)
    nblk = t // tm
    jmap = lambda i: i

    def body(x_ref, zq_ref, zk_ref, zv_ref, zg_ref, cb_ref, cc_ref, cu_ref, ccp_ref, ccn_ref, cup_ref, cun_ref,
             of_ref, ob_ref, cw_ref, cn_ref, gn_ref, wo_ref, g_ref, wq_ref, k_ref, v_ref, wx_ref,
             x1_ref, x2_ref, xn_ref, q_ref, a_ref, y_ref, opre_ref):
        j = pl.program_id(0)
        zv = zv_ref[...]
        sb = _head_sum((zq_ref[...] * 0.125) * zk_ref[...], 64, 128)
        o_pre = of_ref[...] + ob_ref[...] - sb * zv
        opre_ref[...] = o_pre
        on, _ = _head_norm(o_pre)
        zg = zg_ref[...]
        y_ref[:, CW:] = (on * gn_ref[...] * (zg * _sigmoid(zg))).astype(BF16)
        cb = cb_ref[...]
        _, _, _, conv = _conv_parts(cb, cc_ref[...], cu_ref[...], ccp_ref[pl.ds(7, 1), :], cup_ref[pl.ds(7, 1), :],
                                    ccn_ref[pl.ds(0, 1), :], cun_ref[pl.ds(0, 1), :], cw_ref, j == 0,
                                    j == nblk - 1, tm)
        yc = cb * conv
        gm = _group_sum(yc * yc) * (1.0 / 64.0)
        y_ref[:, :CW] = (yc * lax.rsqrt(gm + EPS) * cn_ref[...]).astype(BF16)

        x1 = x_ref[...] + jnp.dot(y_ref[...], wo_ref[...], preferred_element_type=F32)
        x1_ref[...] = x1
        xn, _ = _rms(x1, g_ref[...])
        xb = xn.astype(BF16)
        xn_ref[...] = xb
        qb = jnp.dot(xb, wq_ref[...], preferred_element_type=F32).astype(BF16)
        q_ref[...] = qb
        heads = [slice(h * XD, (h + 1) * XD) for h in range(NH)]
        ps = [_softmax_head(qb[:, hs], k_ref[:, hs]) for hs in heads]
        for hs, p in zip(heads, ps):
            a_ref[:, hs] = _dot(p, v_ref[:, hs]).astype(BF16)
        x2_ref[...] = x1 + jnp.dot(a_ref[...], wx_ref[...], preferred_element_type=F32)

    tok = lambda i: (i, 0)
    full = lambda i: (0, 0)
    once = pl.Buffered(1)
    tokd, tokv = pl.BlockSpec((tm, D), tok), pl.BlockSpec((tm, GV), tok)
    weight = pl.BlockSpec((D, D), full, pipeline_mode=once)
    ccp, ccn = _halo_specs(tm, nblk, t, ZB_CC, jmap)
    cup, cun = _halo_specs(tm, nblk, t, ZB_CU, jmap)
    in_specs = [tokd, _zspec(tm, GK, ZB_Q, jmap), _zspec(tm, GK, ZB_K, jmap), _zspec(tm, GV, ZB_V, jmap),
                _zspec(tm, GV, ZB_G, jmap), _zspec(tm, CW, ZB_CB, jmap), _zspec(tm, CW, ZB_CC, jmap),
                _zspec(tm, CW, ZB_CU, jmap), ccp, ccn, cup, cun, tokv, tokv,
                pl.BlockSpec((3, CW), full), pl.BlockSpec((1, CW), full), pl.BlockSpec((1, GV), full),
                weight, pl.BlockSpec((1, D), full), weight, pl.BlockSpec((NMEM, D), full),
                pl.BlockSpec((NMEM, D), full), weight]
    return pl.pallas_call(
        body, name="attn_fwd", grid=(nblk,), in_specs=in_specs, out_specs=[tokd] * 6 + [tokv],
        out_shape=[jax.ShapeDtypeStruct((t, D), F32), jax.ShapeDtypeStruct((t, D), F32),
                   jax.ShapeDtypeStruct((t, D), BF16), jax.ShapeDtypeStruct((t, D), BF16),
                   jax.ShapeDtypeStruct((t, D), BF16), jax.ShapeDtypeStruct((t, D), BF16),
                   jax.ShapeDtypeStruct((t, GV), F32)],
        compiler_params=_cparams(("arbitrary",)))(
            x, z, z, z, z, z, z, z, z, z, z, z, o_f, o_b, conv_w, conv_norm, gla_norm4, w_out, g, w_xq, kb, vb, w_xo)


def _mlp_fwd(x2, g, w_up_t, w_down, fg, target):
    t = x2.shape[0]
    tm = min(TM_MLP_FWD, t)

    def body(x_ref, g_ref, wu_ref, wd_ref, fg_ref, t_ref, h1_ref, xn_ref, dx_ref, dxb_ref, loss_ref, dfg_ref, ab):
        @pl.when(pl.program_id(0) == 0)
        def _():
            loss_ref[...] = jnp.zeros_like(loss_ref)
            dfg_ref[...] = jnp.zeros_like(dfg_ref)

        x = x_ref[...]
        xn, _ = _rms(x, g_ref[...])
        xnb = xn.astype(BF16)
        xn_ref[...] = xnb
        for q in range(FF // TF):
            cols = slice(q * TF, (q + 1) * TF)
            h1 = _dot_nt(xnb, wu_ref[cols, :])
            h1_ref[:, cols] = h1.astype(BF16)
            hr = jnp.maximum(h1, 0.0)
            ab[:, cols] = (hr * hr).astype(BF16)
        x3 = x + jnp.dot(ab[...], wd_ref[...], preferred_element_type=F32)
        y, r = _rms(x3, fg_ref[...])
        e = y - t_ref[...]
        row = jnp.mean(e * e, axis=-1, keepdims=True)
        _acc_rows(loss_ref, 0.5 * jnp.sum(row, axis=0, keepdims=True))
        dx, dfg = _rms_bwd(x3, r, fg_ref[...], e * (1.0 / D))
        dx_ref[...] = dx
        dxb_ref[...] = dx.astype(BF16)
        _acc_rows(dfg_ref, dfg)

    tok = lambda i: (i, 0)
    full = lambda i: (0, 0)
    once = pl.Buffered(1)
    return pl.pallas_call(
        body, name="mlp_fwd", grid=(t // tm,),
        in_specs=[pl.BlockSpec((tm, D), tok), pl.BlockSpec((1, D), full),
                  pl.BlockSpec((FF, D), full, pipeline_mode=once), pl.BlockSpec((FF, D), full, pipeline_mode=once),
                  pl.BlockSpec((1, D), full), pl.BlockSpec((tm, D), tok)],
        out_specs=[pl.BlockSpec((tm, FF), tok), pl.BlockSpec((tm, D), tok), pl.BlockSpec((tm, D), tok),
                   pl.BlockSpec((tm, D), tok), pl.BlockSpec((8, 128), full), pl.BlockSpec((8, D), full)],
        out_shape=[jax.ShapeDtypeStruct((t, FF), BF16), jax.ShapeDtypeStruct((t, D), BF16),
                   jax.ShapeDtypeStruct((t, D), F32), jax.ShapeDtypeStruct((t, D), BF16),
                   jax.ShapeDtypeStruct((8, 128), F32), jax.ShapeDtypeStruct((8, D), F32)],
        scratch_shapes=[pltpu.VMEM((tm, FF), BF16)],
        compiler_params=_cparams(("arbitrary",)))(x2, g, w_up_t, w_down, fg, target)


def _mlp_bwd(dx3, dx3b, h1b, w_down, w_up_t, x2, g):
    t = x2.shape[0]
    tm = min(TM_MLP, t)

    def body(dx_ref, dxb_ref, h1_ref, wd_ref, wu_ref, x_ref, g_ref, a_ref, dh_ref, dx2_ref, dx2b_ref, dg_ref):
        @pl.when(pl.program_id(0) == 0)
        def _():
            dg_ref[...] = jnp.zeros_like(dg_ref)

        for q in range(FF // TF):
            cols = slice(q * TF, (q + 1) * TF)
            hr = jnp.maximum(h1_ref[:, cols].astype(F32), 0.0)
            da = _dot_nt(dxb_ref[...], wd_ref[cols, :])
            a_ref[:, cols] = (hr * hr).astype(BF16)
            dh_ref[:, cols] = (da * 2.0 * hr).astype(BF16)
        dxn = jnp.dot(dh_ref[...], wu_ref[...], preferred_element_type=F32)
        x = x_ref[...]
        r = lax.rsqrt(jnp.mean(x * x, axis=-1, keepdims=True) + EPS)
        dx, dg = _rms_bwd(x, r, g_ref[...], dxn)
        dx2 = dx_ref[...] + dx
        dx2_ref[...] = dx2
        dx2b_ref[...] = dx2.astype(BF16)
        _acc_rows(dg_ref, dg)

    tok = lambda i: (i, 0)
    full = lambda i: (0, 0)
    once = pl.Buffered(1)
    return pl.pallas_call(
        body, name="mlp_bwd", grid=(t // tm,),
        in_specs=[pl.BlockSpec((tm, D), tok), pl.BlockSpec((tm, D), tok), pl.BlockSpec((tm, FF), tok),
                  pl.BlockSpec((FF, D), full, pipeline_mode=once), pl.BlockSpec((FF, D), full, pipeline_mode=once),
                  pl.BlockSpec((tm, D), tok), pl.BlockSpec((1, D), full)],
        out_specs=[pl.BlockSpec((tm, FF), tok), pl.BlockSpec((tm, FF), tok), pl.BlockSpec((tm, D), tok),
                   pl.BlockSpec((tm, D), tok), pl.BlockSpec((8, D), full)],
        out_shape=[jax.ShapeDtypeStruct((t, FF), BF16), jax.ShapeDtypeStruct((t, FF), BF16),
                   jax.ShapeDtypeStruct((t, D), F32), jax.ShapeDtypeStruct((t, D), BF16),
                   jax.ShapeDtypeStruct((8, D), F32)],
        compiler_params=_cparams(("arbitrary",)))(dx3, dx3b, h1b, w_down, w_up_t, x2, g)


def _attn_bwd(x1, dx2, dx2b, qb, kb, vb, w_xo, w_xq, w_out, g, riders=()):
    t = x1.shape[0]
    tm = min(TM, t)

    def body(x_ref, dx2_ref, dx2b_ref, q_ref, k_ref, v_ref, wx_ref, wq_ref, wo_ref, g_ref,
             dx1_ref, dx1b_ref, dy_ref, dq_ref, dkv_ref, dg_ref):
        @pl.when(pl.program_id(0) == 0)
        def _():
            dkv_ref[...] = jnp.zeros_like(dkv_ref)
            dg_ref[...] = jnp.zeros_like(dg_ref)

        datt = _dot_nt(dx2b_ref[...], wx_ref[...]).astype(BF16)
        heads = [slice(h * XD, (h + 1) * XD) for h in range(NH)]
        ps = [_softmax_head(q_ref[:, hs], k_ref[:, hs]) for hs in heads]
        dps = [_dot_nt(datt[:, hs], v_ref[:, hs]) for hs in heads]
        dss = [(p * (dp - jnp.sum(dp * p, axis=-1, keepdims=True)) * (1.0 / 16.0)).astype(BF16)
               for p, dp in zip(ps, dps)]
        for h, (hs, p, ds) in enumerate(zip(heads, ps, dss)):
            dq_ref[:, hs] = _dot(ds, k_ref[:, hs]).astype(BF16)
            dkv_ref[:, hs] += _dot_tn(ds, q_ref[:, hs])
            dkv_ref[:, D + h * XD:D + (h + 1) * XD] += _dot_tn(p, datt[:, hs])
        dxn = _dot_nt(dq_ref[...], wq_ref[...])
        x = x_ref[...]
        r = lax.rsqrt(jnp.mean(x * x, axis=-1, keepdims=True) + EPS)
        dx, dg = _rms_bwd(x, r, g_ref[...], dxn)
        dx1 = dx2_ref[...] + dx
        dx1_ref[...] = dx1
        dx1b = dx1.astype(BF16)
        dx1b_ref[...] = dx1b
        dy_ref[...] = _dot_nt(dx1b, wo_ref[...])
        _acc_rows(dg_ref, dg)

    tok = lambda i: (i, 0)
    full = lambda i: (0, 0)
    return _call(
        body, "attn_bwd", (t // tm,),
        [pl.BlockSpec((tm, D), tok), pl.BlockSpec((tm, D), tok), pl.BlockSpec((tm, D), tok),
         pl.BlockSpec((tm, D), tok), pl.BlockSpec((NMEM, D), full), pl.BlockSpec((NMEM, D), full),
         pl.BlockSpec((D, D), full), pl.BlockSpec((D, D), full), pl.BlockSpec((D, D), full),
         pl.BlockSpec((1, D), full)],
        [pl.BlockSpec((tm, D), tok), pl.BlockSpec((tm, D), tok), pl.BlockSpec((tm, D), tok),
         pl.BlockSpec((tm, D), tok), pl.BlockSpec((NMEM, 2 * D), full), pl.BlockSpec((8, D), full)],
        [jax.ShapeDtypeStruct((t, D), F32), jax.ShapeDtypeStruct((t, D), BF16),
         jax.ShapeDtypeStruct((t, D), F32), jax.ShapeDtypeStruct((t, D), BF16),
         jax.ShapeDtypeStruct((NMEM, 2 * D), F32), jax.ShapeDtypeStruct((8, D), F32)], [],
        (x1, dx2, dx2b, qb, kb, vb, w_xo, w_xq, w_out, g), riders)


def _kv_bwd(dkv, memn, mem, g, w):
    def body(dkv_ref, mn_ref, m_ref, g_ref, w_ref, dw_ref, dg_ref):
        dkvb = dkv_ref[...].astype(BF16)
        dmn = jnp.zeros((NMEM, D), F32)
        for j in range(NDEV):
            cols = slice(j * XKV_SHARD, (j + 1) * XKV_SHARD)
            dw_ref[j] = _dot_tn(mn_ref[...], dkvb[:, cols])
            dmn += _dot_nt(dkvb[:, cols], w_ref[j])
        m = m_ref[...]
        r = lax.rsqrt(jnp.mean(m * m, axis=-1, keepdims=True) + EPS)
        dg_ref[...] = jnp.broadcast_to(jnp.sum(dmn * m * r, axis=0, keepdims=True), dg_ref.shape)

    return pl.pallas_call(
        body, name="kv_bwd",
        out_shape=[jax.ShapeDtypeStruct((NDEV, D, XKV_SHARD), F32), jax.ShapeDtypeStruct((8, D), F32)],
        compiler_params=pltpu.CompilerParams(vmem_limit_bytes=VMEM_LIMIT))(dkv, memn, mem, g, w)


def _inproj_bwd(dz, w_t, x, dx1, g, riders=()):
    t = x.shape[0]
    tm = min(TM, t)

    def body(dz_ref, w_ref, x_ref, dx1_ref, g_ref, gx_ref, dg_ref):
        @pl.when(pl.program_id(0) == 0)
        def _():
            dg_ref[...] = jnp.zeros_like(dg_ref)

        dh = jnp.dot(dz_ref[...], w_ref[...], preferred_element_type=F32)
        x = x_ref[...]
        r = lax.rsqrt(jnp.mean(x * x, axis=-1, keepdims=True) + EPS)
        dx, dg = _rms_bwd(x, r, g_ref[...], dh)
        gx_ref[...] = dx1_ref[...] + dx
        _acc_rows(dg_ref, dg)

    tok = lambda i: (i, 0)
    full = lambda i: (0, 0)
    return _call(
        body, "inproj_bwd", (t // tm,),
        [pl.BlockSpec((tm, ZC), tok), pl.BlockSpec((ZC, D), full), pl.BlockSpec((tm, D), tok),
         pl.BlockSpec((tm, D), tok), pl.BlockSpec((1, D), full)],
        [pl.BlockSpec((tm, D), tok), pl.BlockSpec((8, D), full)],
        [jax.ShapeDtypeStruct((t, D), F32), jax.ShapeDtypeStruct((8, D), F32)], [], (dz, w_t, x, dx1, g), riders)


def _matmul_tn(a, b, name, rows=None, riders=()):
    t, k = a.shape
    n = b.shape[1]
    tk, tn = [1024 if size % 1024 == 0 else 640 for size in (k, n)]
    tt = min(TT, t)
    rows = rows or k

    def body(a_ref, b_ref, o_ref):
        @pl.when(pl.program_id(2) == 0)
        def _():
            o_ref[...] = jnp.zeros_like(o_ref)

        o_ref[...] += _dot_tn(a_ref[...], b_ref[...])

    return _call(
        body, name, (k // tk, n // tn, t // tt),
        [pl.BlockSpec((tt, tk), lambda i, j, s: (s, i)), pl.BlockSpec((tt, tn), lambda i, j, s: (s, j))],
        [pl.BlockSpec((tk, tn), lambda i, j, s: (i, j))], [jax.ShapeDtypeStruct((rows, n), F32)], [], (a, b), riders)


def _lane_head(shape, dim, shift):
    return _iota(shape, dim) >> shift


CUM_ROWS = 128


def _chunk_cumsum(x, upper, n):
    r, c = _iota((CUM_ROWS, CUM_ROWS), 0), _iota((CUM_ROWS, CUM_ROWS), 1)
    tri = (c >= r) if upper else (c <= r)
    cum = jnp.where(((r >> 6) == (c >> 6)) & tri, 1.0, 0.0).astype(BF16)
    return jnp.concatenate([_dot_exact_lhs(cum, x[g:g + CUM_ROWS], n) for g in range(0, x.shape[0], CUM_ROWS)],
                           axis=0)


def _gla_recompute(q_raw, k, lr, wpad, bias, rev, tb):
    pre = _dot(lr, wpad) + bias
    la = (jnp.minimum(pre, 0.0) - jnp.log(1.0 + jnp.exp(-jnp.abs(pre)))) * (1.0 / 16.0)
    b = _chunk_cumsum(la, rev, 3)
    e, ei = jnp.exp(b), jnp.exp(-b)
    qt = (q_raw * 0.125) * e
    kt = k * ei
    return pre, b, e, ei, qt, kt


def _stack_heads(x, shift):
    head = _lane_head(x.shape, 1, shift)
    return jnp.concatenate([jnp.where(head == h, x, 0.0) for h in range(NH)], axis=0).astype(BF16)


def _fold_heads(x, shift):
    head = _lane_head((CH, x.shape[1]), 1, shift)
    return functools.reduce(lambda a, b: a + b,
                            [jnp.where(head == h, x[h * CH:(h + 1) * CH], 0.0) for h in range(NH)])


def _wide_mask(rev):
    r, s = _iota((CH, NH * CH), 0), _iota((CH, NH * CH), 1) & (CH - 1)
    return (s >= r) if rev else (s <= r)


def _rows_by_head(x):
    w = x.shape[1] // NH
    return jnp.concatenate([x[:, h * w:(h + 1) * w] for h in range(NH)], axis=0)


def _lanes_by_head(x):
    return jnp.concatenate([x[h * CH:(h + 1) * CH] for h in range(NH)], axis=1)


def _state_compact(xt):
    head = _lane_head((128, GK), 1, 6)
    return functools.reduce(lambda a, b: a + b,
                            [jnp.where(head == h, xt[h * 128:(h + 1) * 128], 0.0) for h in range(NH)])


def _conv_parts(cb, cc, cu, ccp, cup, ccn, cun, cw_ref, first, last, tb):
    h = cc * cu
    hp = jnp.where(first, 0.0, ccp * cup)
    hn = jnp.where(last, 0.0, ccn * cun)
    rows = _iota(h.shape, 0)
    h_m1 = jnp.where(rows == 0, hp, pltpu.roll(h, 1, 0))
    h_p1 = jnp.where(rows == tb - 1, hn, pltpu.roll(h, tb - 1, 0))
    conv = cw_ref[pl.ds(0, 1), :] * h_m1 + cw_ref[pl.ds(1, 1), :] * h + cw_ref[pl.ds(2, 1), :] * h_p1
    return h, h_m1, h_p1, conv


def _head_sum(x, w_in, w_out):
    shape, sh_in, sh_out = (2 * w_in, 2 * w_out), w_in.bit_length() - 1, w_out.bit_length() - 1
    sel = jnp.where((_iota(shape, 0) >> sh_in) == (_iota(shape, 1) >> sh_out), 1.0, 0.0).astype(BF16)
    return jnp.concatenate([_dot_exact_rhs(x[:, s:s + 2 * w_in], sel, 2) for s in range(0, NH * w_in, 2 * w_in)],
                           axis=1)


def _group_sum(x):
    ones = jnp.where((_iota((128, 128), 0) >> 6) == (_iota((128, 128), 1) >> 6), 1.0, 0.0).astype(BF16)
    return jnp.concatenate([_dot_exact_rhs(x[:, s:s + 128], ones, 2) for s in range(0, x.shape[1], 128)], axis=1)


def _head_norm(o):
    ons, rs = [], []
    for h in range(NH):
        slab = o[:, h * 128:(h + 1) * 128]
        r = lax.rsqrt(jnp.mean(slab * slab, axis=-1, keepdims=True) + EPS)
        ons.append(slab * r)
        rs.append(jnp.broadcast_to(r, slab.shape))
    return jnp.concatenate(ons, axis=1), jnp.concatenate(rs, axis=1)


def _zspec(tb, width, blk, jmap):
    return pl.BlockSpec((tb, width), lambda i: (jmap(i), blk))


def _halo_specs(tb, nblk, t, blk, jmap):
    prev = pl.BlockSpec((8, CW), lambda i: (jnp.maximum(jmap(i) * (tb // 8) - 1, 0), blk))
    nxt = pl.BlockSpec((8, CW), lambda i: (jnp.minimum((jmap(i) + 1) * (tb // 8), t // 8 - 1), blk))
    return prev, nxt


def _gla_fwd_block(q_ref, k_ref, v_ref, lr_ref, w_ref, bias_ref, o_ref, sd_ref, st, b_scr, rev, tb):
    nb = tb // CH
    _, b, _, _, qt, kt = _gla_recompute(q_ref[...], k_ref[...], lr_ref[...], w_ref[...], bias_ref[...], rev, tb)
    v = v_ref[...]
    b_scr[...] = b
    yield
    maskw = _wide_mask(rev)
    order = list(reversed(range(nb))) if rev else list(range(nb))
    rows = [slice(c * CH, (c + 1) * CH) for c in range(nb)]
    state = st[...]
    for c in order:
        gdec = jnp.exp(b_scr[pl.ds(c * CH + (0 if rev else CH - 1), 1), :])
        sd_ref[c] = state
        a = jnp.where(maskw, _dot_nt(qt[rows[c]], _stack_heads(kt[rows[c]], 6)), 0.0)
        o_inter = _lanes_by_head(_dot_nt(_stack_heads(qt[rows[c]], 6), state))
        o_ref[pl.ds(c * CH, CH), :] = _dot(a, _stack_heads(v[rows[c]], 7)) + o_inter
        state = state * gdec + _state_compact(_dot_tn(v[rows[c]], kt[rows[c]] * gdec))
        yield
    st[...] = state
    yield


def _gla_fwd(z, waf_pad, b_af, wab_pad, b_ab, riders=()):
    t = z.shape[0]
    tb = min(TB, t)
    nblk, nb = t // tb, tb // CH
    jmaps = (lambda i: i, lambda i: nblk - 1 - i)

    def body(qf, kf, vf, lrf, qr, kr, vr, lrr, wf, bf, wr, br, of_ref, sdf_ref, or_ref, sdr_ref,
             st_f, st_r, b_f, b_r):
        @pl.when(pl.program_id(0) == 0)
        def _():
            st_f[...] = jnp.zeros_like(st_f)
            st_r[...] = jnp.zeros_like(st_r)

        for _ in zip(_gla_fwd_block(qf, kf, vf, lrf, wf, bf, of_ref, sdf_ref, st_f, b_f, False, tb),
                     _gla_fwd_block(qr, kr, vr, lrr, wr, br, or_ref, sdr_ref, st_r, b_r, True, tb)):
            pass

    full = lambda i: (0, 0)
    zspecs = [s for jm in jmaps for s in (_zspec(tb, GK, ZB_Q, jm), _zspec(tb, GK, ZB_K, jm),
                                         _zspec(tb, GV, ZB_V, jm), _zspec(tb, 128, ZB_LR, jm))]
    wspecs = [pl.BlockSpec((128, GK), full), pl.BlockSpec((1, GK), full)] * 2
    out_specs = [s for jm in jmaps for s in (pl.BlockSpec((tb, GV), lambda i, jm=jm: (jm(i), 0)),
                                             pl.BlockSpec((nb, 128, GK), lambda i, jm=jm: (jm(i), 0, 0)))]
    out_shape = [jax.ShapeDtypeStruct((t, GV), F32), jax.ShapeDtypeStruct((t // CH, 128, GK), F32)] * 2
    scratch = [pltpu.VMEM((128, GK), F32), pltpu.VMEM((128, GK), F32), pltpu.VMEM((tb, GK), F32),
               pltpu.VMEM((tb, GK), F32)]
    return _call(body, "gla_fwd", (nblk,), zspecs + wspecs, out_specs, out_shape, scratch,
                 [z] * 8 + [waf_pad, b_af, wab_pad, b_ab], riders)


def _gla_bwd_chunks(do_ref, sd_ref, dst, b_scr, db_scr, dq_ref, dk_ref, dv_ref, qt, kt, e, ei, v, rev, nb):
    maskw = _wide_mask(rev)
    for c in (range(nb) if rev else reversed(range(nb))):
        sl = slice(c * CH, (c + 1) * CH)
        grow = c * CH + (0 if rev else CH - 1)
        gdec = jnp.exp(b_scr[pl.ds(grow, 1), :])
        qt_c, kt_c, v_c, do_c = qt[sl], kt[sl], v[sl], do_ref[pl.ds(c * CH, CH), :]
        s_in, ds_out = sd_ref[c], dst[...]
        kbd, vbd = _stack_heads(kt_c, 6), _stack_heads(v_c, 7)
        a = jnp.where(maskw, _dot_nt(qt_c, kbd), 0.0)
        da = jnp.where(maskw, _dot_nt(do_c, vbd), 0.0)
        dv_ref[pl.ds(c * CH, CH), :] = (_fold_heads(_dot_tn(a, do_c), 7)
                                        + _lanes_by_head(_dot_nt(_stack_heads(kt_c * gdec, 6), ds_out)))
        dqt = _dot(da, kbd) + _fold_heads(_dot(_rows_by_head(do_c), s_in), 6)
        dkh = _fold_heads(_dot(_rows_by_head(v_c), ds_out), 6)
        da_do = jnp.concatenate([da.astype(BF16), do_c.astype(BF16)], axis=1)
        both = _dot_tn(da_do, qt_c)
        dkt = _fold_heads(both[:NH * CH], 6) + dkh * gdec
        dg = jnp.sum(ds_out * s_in, axis=0, keepdims=True) + jnp.sum(kt_c * dkh, axis=0, keepdims=True)
        db_scr[pl.ds(c * CH, CH), :] = dqt * qt_c - dkt * kt_c
        db_scr[pl.ds(grow, 1), :] += dg * gdec
        dq_ref[pl.ds(c * CH, CH), :] = dqt * e[sl] * 0.125
        dk_ref[pl.ds(c * CH, CH), :] = dkt * ei[sl]
        dst[...] = ds_out * gdec + _state_compact(both[NH * CH:])
        yield


def _gate_bwd(db, pre, lr, wpad, rev, tb):
    dla = _chunk_cumsum(db, not rev, 2)
    dpre = dla * (1.0 / 16.0) / (1.0 + jnp.exp(pre))
    return dpre, _dot_nt(dpre, wpad), _dot_tn(lr, dpre)


def _gla_bwd_first(z, dy, o_pre, sd, wpad, bias, conv_w, conv_norm, gla_norm4, riders=()):
    t = z.shape[0]
    tb = min(TB_BWD, t)
    nblk, nb = t // tb, tb // CH
    jmap = lambda i: nblk - 1 - i

    def body(q_ref, k_ref, v_ref, lr_ref, g_ref, cb_ref, cc_ref, cu_ref, ccp_ref, ccn_ref, cup_ref, cun_ref,
             dy_ref, opre_ref, sd_ref, w_ref, bias_ref, cw_ref, cn_ref, gn_ref,
             do_ref, dq_ref, dk_ref, dv_ref, dlr_ref, dzg_ref, dzcb_ref, dconv_ref,
             dw_ref, dbias_ref, dcw_ref, dcn_ref, dgn_ref, dst, b_scr, db_scr):
        i = pl.program_id(0)
        j = jmap(i)

        @pl.when(i == 0)
        def _():
            dst[...] = jnp.zeros_like(dst)
            for ref in (dw_ref, dbias_ref, dcw_ref, dcn_ref, dgn_ref):
                ref[...] = jnp.zeros_like(ref)

        dyg = dy_ref[:, CW:]
        g = g_ref[...]
        sig = _sigmoid(g)
        on, rr = _head_norm(opre_ref[...])
        gn = gn_ref[...]
        dzg_ref[...] = (dyg * on * gn * (sig * (1.0 + g * (1.0 - sig)))).astype(BF16)
        don = dyg * (g * sig)
        _acc_rows(dgn_ref, jnp.sum(don * on, axis=0, keepdims=True))
        u = don * gn
        uo = u * on
        mean_uo = jnp.concatenate(
            [jnp.broadcast_to(jnp.mean(uo[:, h * 128:(h + 1) * 128], axis=-1, keepdims=True), (tb, 128))
             for h in range(NH)], axis=1)
        do_ref[...] = rr * (u - on * mean_uo)

        def conv_branch():
            cb = cb_ref[...]
            h, h_m1, h_p1, conv = _conv_parts(cb, cc_ref[...], cu_ref[...], ccp_ref[pl.ds(7, 1), :],
                                              cup_ref[pl.ds(7, 1), :], ccn_ref[pl.ds(0, 1), :],
                                              cun_ref[pl.ds(0, 1), :], cw_ref, j == 0, j == nblk - 1, tb)
            yc = cb * conv
            yield
            rc = lax.rsqrt(_group_sum(yc * yc) * (1.0 / 64.0) + EPS)
            ycr = yc * rc
            yield
            dyn = dy_ref[:, :CW]
            _acc_rows(dcn_ref, jnp.sum(dyn * ycr, axis=0, keepdims=True))
            uc = dyn * cn_ref[...]
            yield
            dyc = rc * (uc - ycr * (_group_sum(uc * ycr) * (1.0 / 64.0)))
            dzcb_ref[...] = (dyc * conv).astype(BF16)
            yield
            dconv = dyc * cb
            dconv_ref[...] = dconv
            yield
            dcw_ref[pl.ds(0, 1), :] += jnp.sum(dconv * h_m1, axis=0, keepdims=True)
            dcw_ref[pl.ds(1, 1), :] += jnp.sum(dconv * h, axis=0, keepdims=True)
            dcw_ref[pl.ds(2, 1), :] += jnp.sum(dconv * h_p1, axis=0, keepdims=True)
            yield

        lr, wp = lr_ref[...], w_ref[...]
        pre, b, e, ei, qt, kt = _gla_recompute(q_ref[...], k_ref[...], lr, wp, bias_ref[...], False, tb)
        b_scr[...] = b
        for _ in itertools.zip_longest(
                _gla_bwd_chunks(do_ref, sd_ref, dst, b_scr, db_scr, dq_ref, dk_ref, dv_ref, qt, kt, e, ei, v_ref[...],
                                False, nb), conv_branch()):
            pass
        dpre, dlr, dw = _gate_bwd(db_scr[...], pre, lr, wp, False, tb)
        dlr_ref[...] = dlr
        dw_ref[...] += dw
        _acc_rows(dbias_ref, jnp.sum(dpre, axis=0, keepdims=True))

    full = lambda i: (0, 0)
    tokv = pl.BlockSpec((tb, GV), lambda i: (jmap(i), 0))
    tokk = pl.BlockSpec((tb, GK), lambda i: (jmap(i), 0))
    ccp, ccn = _halo_specs(tb, nblk, t, ZB_CC, jmap)
    cup, cun = _halo_specs(tb, nblk, t, ZB_CU, jmap)
    in_specs = [_zspec(tb, GK, ZB_Q, jmap), _zspec(tb, GK, ZB_K, jmap), _zspec(tb, GV, ZB_V, jmap),
                _zspec(tb, 128, ZB_LR, jmap), _zspec(tb, GV, ZB_G, jmap), _zspec(tb, CW, ZB_CB, jmap),
                _zspec(tb, CW, ZB_CC, jmap), _zspec(tb, CW, ZB_CU, jmap), ccp, ccn, cup, cun,
                pl.BlockSpec((tb, D), lambda i: (jmap(i), 0)), tokv,
                pl.BlockSpec((nb, 128, GK), lambda i: (jmap(i), 0, 0)), pl.BlockSpec((128, GK), full),
                pl.BlockSpec((1, GK), full), pl.BlockSpec((3, CW), full), pl.BlockSpec((1, CW), full),
                pl.BlockSpec((1, GV), full)]
    out_specs = [tokv, tokk, tokk, tokv, pl.BlockSpec((tb, 128), lambda i: (jmap(i), 0)), tokv, tokv, tokv,
                 pl.BlockSpec((128, GK), full), pl.BlockSpec((8, GK), full), pl.BlockSpec((8, CW), full),
                 pl.BlockSpec((8, CW), full), pl.BlockSpec((8, GV), full)]
    out_shape = [jax.ShapeDtypeStruct((t, GV), F32), jax.ShapeDtypeStruct((t, GK), F32),
                 jax.ShapeDtypeStruct((t, GK), F32), jax.ShapeDtypeStruct((t, GV), F32),
                 jax.ShapeDtypeStruct((t, 128), F32), jax.ShapeDtypeStruct((t, GV), BF16),
                 jax.ShapeDtypeStruct((t, CW), BF16), jax.ShapeDtypeStruct((t, CW), F32),
                 jax.ShapeDtypeStruct((128, GK), F32), jax.ShapeDtypeStruct((8, GK), F32),
                 jax.ShapeDtypeStruct((8, CW), F32), jax.ShapeDtypeStruct((8, CW), F32),
                 jax.ShapeDtypeStruct((8, GV), F32)]
    return _call(
        body, "gla_bwd_first", (nblk,), in_specs, out_specs, out_shape,
        [pltpu.VMEM((128, GK), F32), pltpu.VMEM((tb, GK), F32), pltpu.VMEM((tb, GK), F32)],
        (z, z, z, z, z, z, z, z, z, z, z, z, dy, o_pre, sd, wpad, bias, conv_w, conv_norm, gla_norm4), riders)


def _gla_bwd_second(z, do, sd, wpad, bias, dqa, dka, dva, dlra, dzg, dzcb, dconv, conv_w, riders=()):
    t = z.shape[0]
    tb = min(TB_BWD, t)
    nblk, nb = t // tb, tb // CH
    jmap = lambda i: i

    def body(q_ref, k_ref, v_ref, lr_ref, cc_ref, cu_ref, do_ref, sd_ref, w_ref, bias_ref, dqa_ref, dka_ref,
             dva_ref, dlra_ref, dzg_ref, dzcb_ref, dc_ref, dcp_ref, dcn_ref, cw_ref,
             dz_ref, dw_ref, dbias_ref, dst, b_scr, db_scr, dq_scr, dk_scr, dv_scr, sb_scr, dsk_scr):
        i = pl.program_id(0)

        @pl.when(i == 0)
        def _():
            dst[...] = jnp.zeros_like(dst)
            dw_ref[...] = jnp.zeros_like(dw_ref)
            dbias_ref[...] = jnp.zeros_like(dbias_ref)

        q_raw, k, v, lr, wp = q_ref[...], k_ref[...], v_ref[...], lr_ref[...], w_ref[...]
        pre, b, e, ei, qt, kt = _gla_recompute(q_raw, k, lr, wp, bias_ref[...], True, tb)
        b_scr[...] = b

        def token_local():
            dc = dc_ref[...]
            rows = _iota(dc.shape, 0)
            dprev = jnp.where(i == 0, 0.0, dcp_ref[pl.ds(7, 1), :])
            dnext = jnp.where(i == nblk - 1, 0.0, dcn_ref[pl.ds(0, 1), :])
            dc_m1 = jnp.where(rows == 0, dprev, pltpu.roll(dc, 1, 0))
            dc_p1 = jnp.where(rows == tb - 1, dnext, pltpu.roll(dc, tb - 1, 0))
            yield
            dh = cw_ref[pl.ds(0, 1), :] * dc_p1 + cw_ref[pl.ds(1, 1), :] * dc + cw_ref[pl.ds(2, 1), :] * dc_m1
            dz_ref[:, 0:512] = dzcb_ref[...]
            yield
            dz_ref[:, 512:1024] = (dh * cu_ref[...]).astype(BF16)
            dz_ref[:, 1024:1536] = (dh * cc_ref[...]).astype(BF16)
            dz_ref[:, 2560:3072] = dzg_ref[...]
            yield
            sb_scr[...] = _head_sum((q_raw * 0.125) * k, 64, 128)
            yield
            dsk_scr[...] = _head_sum(do_ref[...] * v, 128, 64)
            yield

        for _ in itertools.zip_longest(
                _gla_bwd_chunks(do_ref, sd_ref, dst, b_scr, db_scr, dq_scr, dk_scr, dv_scr, qt, kt, e, ei, v, True, nb),
                token_local()):
            pass
        dpre, dlr, dw = _gate_bwd(db_scr[...], pre, lr, wp, True, tb)
        dw_ref[...] += dw
        _acc_rows(dbias_ref, jnp.sum(dpre, axis=0, keepdims=True))
        dsk = dsk_scr[...]
        dz_ref[:, 1536:1792] = (dqa_ref[...] + dq_scr[...] - dsk * k * 0.125).astype(BF16)
        dz_ref[:, 1792:2048] = (dka_ref[...] + dk_scr[...] - dsk * (q_raw * 0.125)).astype(BF16)
        dz_ref[:, 2048:2560] = (dva_ref[...] + dv_scr[...] - sb_scr[...] * do_ref[...]).astype(BF16)
        dz_ref[:, 3072:3200] = (dlra_ref[...] + dlr).astype(BF16)

    full = lambda i: (0, 0)
    tokv = pl.BlockSpec((tb, GV), lambda i: (i, 0))
    tokk = pl.BlockSpec((tb, GK), lambda i: (i, 0))
    dcp = pl.BlockSpec((8, CW), lambda i: (jnp.maximum(i * (tb // 8) - 1, 0), 0))
    dcn = pl.BlockSpec((8, CW), lambda i: (jnp.minimum((i + 1) * (tb // 8), t // 8 - 1), 0))
    in_specs = [_zspec(tb, GK, ZB_Q, jmap), _zspec(tb, GK, ZB_K, jmap), _zspec(tb, GV, ZB_V, jmap),
                _zspec(tb, 128, ZB_LR, jmap), _zspec(tb, CW, ZB_CC, jmap), _zspec(tb, CW, ZB_CU, jmap), tokv,
                pl.BlockSpec((nb, 128, GK), lambda i: (i, 0, 0)), pl.BlockSpec((128, GK), full),
                pl.BlockSpec((1, GK), full), tokk, tokk, tokv, pl.BlockSpec((tb, 128), lambda i: (i, 0)), tokv, tokv,
                tokv, dcp, dcn, pl.BlockSpec((3, CW), full)]
    out_specs = [pl.BlockSpec((tb, ZC), lambda i: (i, 0)), pl.BlockSpec((128, GK), full), pl.BlockSpec((8, GK), full)]
    out_shape = [jax.ShapeDtypeStruct((t, ZC), BF16), jax.ShapeDtypeStruct((128, GK), F32),
                 jax.ShapeDtypeStruct((8, GK), F32)]
    return _call(
        body, "gla_bwd_second", (nblk,), in_specs, out_specs, out_shape,
        [pltpu.VMEM((128, GK), F32), pltpu.VMEM((tb, GK), F32), pltpu.VMEM((tb, GK), F32),
         pltpu.VMEM((tb, GK), F32), pltpu.VMEM((tb, GK), F32), pltpu.VMEM((tb, GV), F32),
         pltpu.VMEM((tb, GV), F32), pltpu.VMEM((tb, GK), F32)],
        (z, z, z, z, z, z, do, sd, wpad, bias, dqa, dka, dva, dlra, dzg, dzcb, dconv, dconv, dconv, conv_w), riders)


def _step(x, mem, target, shard, small_pack, vec, place):
    own, from_chips = {}, {}

    def pair_sums(names, g4, from_sibling):
        pbs = {}
        for shape in dict.fromkeys(g.shape for g in g4):
            idx = [i for i, g in enumerate(g4) if g.shape == shape]
            pb, mine = _rs_pair_sum(place, [g4[i] for i in idx], [from_sibling[i] for i in idx],
                                    "pair_sum_" + "_".join(names[i] for i in idx))
            for i, b, o in zip(idx, pb, mine):
                pbs[i], own[names[i]] = b, o
        return [pbs[i] for i in range(len(g4))]

    def by_dest(g, n):
        return g.reshape((4, 2) + shard[n].shape)

    w_in, small_all = _exchange(_gather_rider([shard["w_in"], small_pack]), "gather_w_in")
    w_in = jnp.pad(w_in.reshape(ZW, D), ((0, ZC - ZW), (0, 0)))
    small_all = small_all.reshape(NDEV, -1)
    p, off = {}, 0
    for n, (r, c) in SMALL_SHARDED.items():
        p[n] = small_all[:, off:off + r * c].reshape(NDEV, r, c).transpose(1, 0, 2).reshape(r, NDEV * c)
        off += r * c
    zeros_lr = jnp.zeros((128 - LR, GK), BF16)
    waf_pad = jnp.concatenate([p["w_af"].astype(BF16), zeros_lr], axis=0)
    wab_pad = jnp.concatenate([jnp.zeros((LR, GK), BF16), p["w_ab"].astype(BF16), zeros_lr[:128 - 2 * LR]], axis=0)
    gla_norm4 = jnp.tile(vec["gla_norm"], (1, NH))

    z, hb, w_out, w_xq, w_xo, w_xkv = _inproj(
        x, vec["mix_norm"], w_in, [_gather_rider([shard[n] for n in ("w_out", "w_xq", "w_xo", "w_xkv")])])
    w_out, w_xq, w_xo = [a.reshape(D, D) for a in (w_out, w_xq, w_xo)]
    o_f, sd_f, o_b, sd_b, w_up_t, w_down = _gla_fwd(
        z, waf_pad, vec["b_af"], wab_pad, vec["b_ab"],
        [_gather_rider([shard["w_up"], shard["w_down"]], early_relay=False)])
    w_up_t, w_down = w_up_t.reshape(FF, D), w_down.reshape(FF, D)
    kv, memn = _kv_proj(mem, vec["mem_norm"], w_xkv)
    kb, vb = kv[:, :D].astype(BF16), kv[:, D:].astype(BF16)
    x1, x2, xn1, qb, attb, yb, o_pre = _attn_fwd(x, z, o_f, o_b, p["conv_w"], vec["conv_norm"], gla_norm4, w_out,
                                                 vec["xa_norm"], w_xq, kb, vb, w_xo)
    h1b, xn2, dx3, dx3b, loss8, dfinal = _mlp_fwd(x2, vec["mlp_norm"], w_up_t, w_down, vec["final_norm"], target)

    ab, dh1b, dx2, dx2b, dmlp = _mlp_bwd(dx3, dx3b, h1b, w_down, w_up_t, x2, vec["mlp_norm"])
    g_mlp = [by_dest(_matmul_tn(ab, dx3b, "dw_down")[0], "w_down"),
             by_dest(_matmul_tn(dh1b, xn2, "dw_up")[0], "w_up")]
    dx1, dx1b, dy, dqb, dkv, dxa, *s_mlp = _attn_bwd(x1, dx2, dx2b, qb, kb, vb, w_xo, w_xq, w_out, vec["xa_norm"],
                                                     riders=[_sibling_rider(g_mlp)])
    pb_mlp = pair_sums(("w_down", "w_up"), g_mlp, s_mlp)
    dw_xo = _matmul_tn(attb, dx2b, "dw_xo")[0]
    dw_xkv, dmemn = _kv_bwd(dkv, memn, mem, vec["mem_norm"], w_xkv)
    att_names = ("w_xo", "w_xq", "w_out", "w_xkv")
    g_att = [by_dest(g, n) for g, n in zip(
        (dw_xo, _matmul_tn(xn1, dqb, "dw_xq")[0], _matmul_tn(yb, dx1b, "dw_out")[0], dw_xkv), att_names)]
    res = _gla_bwd_first(z, dy, o_pre, sd_f, waf_pad, vec["b_af"], p["conv_w"], vec["conv_norm"], gla_norm4,
                         riders=[_chips_rider(pb_mlp), _sibling_rider(g_att)])
    do, dqa, dka, dva, dlra, dzg, dzcb, dconv, dwaf, dbaf, dcw, dcn, dgn = res[:13]
    from_chips["w_down"], from_chips["w_up"] = res[13:15]
    pb_att = pair_sums(att_names, g_att, res[15:])
    dz, dwab, dbab, *c_att = _gla_bwd_second(z, do, sd_b, wab_pad, vec["b_ab"], dqa, dka, dva, dlra, dzg, dzcb, dconv,
                                             p["conv_w"], riders=[_chips_rider(pb_att)])
    from_chips.update(zip(att_names, c_att))
    g_in = [by_dest(_matmul_tn(dz, hb, "dw_in", rows=ZW)[0], "w_in")]
    pb_in = pair_sums(("w_in",), g_in, _exchange(_sibling_rider(g_in), "grads_to_sibling_w_in"))
    grad_x, dmix, from_chips["w_in"] = _inproj_bwd(dz, w_in, x, dx1, vec["mix_norm"], riders=[_chips_rider(pb_in)])

    small_acc = dict(mix_norm=dmix, conv_w=dcw, conv_norm=dcn, w_af=dwaf, b_af=dbaf, w_ab=dwab, b_ab=dbab,
                     gla_norm=dgn, xa_norm=dxa, mem_norm=dmemn, mlp_norm=dmlp, final_norm=dfinal)
    return loss8, grad_x, small_acc, own, from_chips


def _place():
    return lax.axis_index("x"), lax.axis_index("y"), lax.axis_index("c")


class _Rider:
    def __init__(self, arrays, out_shape, scratch, start, finish, relay=None):
        self.arrays, self.out_shape, self.scratch, self.start, self.finish = arrays, out_shape, scratch, start, finish
        self.relay = relay


def _gather_rider(blks, early_relay=True):
    n = len(blks)

    def plan(in_refs, out_refs, sems):
        send_sems, recv_sems, local_sems = sems
        x, y, c = _place()
        me, sibling = (x, y, c), (x, y, 1 - c)
        chips = [(1 - x, y, c), (x, 1 - y, c), (1 - x, 1 - y, c)]

        def copy(a, k, block, to, own=False):
            px, py, pc = block
            dst = out_refs[a].at[4 * px + 2 * py + pc]
            return pltpu.make_async_remote_copy(
                src_ref=in_refs[a] if own else dst, dst_ref=dst, send_sem=send_sems.at[k, a],
                recv_sem=recv_sems.at[k, a], device_id=to, device_id_type=MESH)

        def local(a):
            return pltpu.make_async_copy(in_refs[a], out_refs[a].at[4 * x + 2 * y + c], local_sems.at[a])

        def own_sends(a):
            return [copy(a, 0, me, sibling, own=True)] + [copy(a, 1 + j, me, chip, own=True)
                                                          for j, chip in enumerate(chips)]

        return copy, local, own_sends, me, sibling, chips

    def start(in_refs, out_refs, sems):
        _, local, own_sends, _, _, _ = plan(in_refs, out_refs, sems)
        for a in range(n):
            local(a).start()
            for cp in own_sends(a):
                cp.start()

    def relay(in_refs, out_refs, sems):
        copy, _, _, me, sibling, chips = plan(in_refs, out_refs, sems)
        for j, chip in enumerate(chips):
            for a in range(n):
                copy(a, 1 + j, chip, me).wait_recv()
                copy(a, 4 + j, chip, sibling).start()

    def finish(in_refs, out_refs, sems):
        if not early_relay:
            relay(in_refs, out_refs, sems)
        copy, local, own_sends, me, sibling, chips = plan(in_refs, out_refs, sems)
        for a in range(n):
            copy(a, 0, sibling, me).wait_recv()
            for j, (px, py, pc) in enumerate(chips):
                copy(a, 4 + j, (px, py, 1 - pc), me).wait_recv()
            for cp in own_sends(a) + [copy(a, 4 + j, chip, sibling) for j, chip in enumerate(chips)]:
                cp.wait_send()
            local(a).wait()

    return _Rider(blks, [jax.ShapeDtypeStruct((NDEV,) + b.shape, b.dtype) for b in blks],
                  [pltpu.SemaphoreType.DMA((7, n)), pltpu.SemaphoreType.DMA((7, n)), pltpu.SemaphoreType.DMA((n,))],
                  start, finish, relay if early_relay else None)


def _sibling_rider(g4s):
    n = len(g4s)

    def copies(in_refs, out_refs, sems):
        send_sems, recv_sems = sems
        x, y, c = _place()
        return [pltpu.make_async_remote_copy(
            src_ref=in_refs[a].at[k, 1 - c], dst_ref=out_refs[a].at[k], send_sem=send_sems.at[k, a],
            recv_sem=recv_sems.at[k, a], device_id=(x, y, 1 - c), device_id_type=MESH)
            for a in range(n) for k in range(4)]

    def start(in_refs, out_refs, sems):
        for cp in copies(in_refs, out_refs, sems):
            cp.start()

    def finish(in_refs, out_refs, sems):
        for cp in copies(in_refs, out_refs, sems):
            cp.wait()

    return _Rider(g4s, [jax.ShapeDtypeStruct((4,) + g.shape[2:], g.dtype) for g in g4s],
                  [pltpu.SemaphoreType.DMA((4, n)), pltpu.SemaphoreType.DMA((4, n))], start, finish)


def _chips_rider(pbs):
    n = len(pbs)

    def copies(in_refs, out_refs, sems):
        send_sems, recv_sems = sems
        x, y, c = _place()
        peers = [(1 - x, y), (x, 1 - y), (1 - x, 1 - y)]
        return [pltpu.make_async_remote_copy(
            src_ref=in_refs[a].at[2 * px + py], dst_ref=out_refs[a].at[k], send_sem=send_sems.at[k, a],
            recv_sem=recv_sems.at[k, a], device_id=(px, py, c), device_id_type=MESH)
            for a in range(n) for k, (px, py) in enumerate(peers)]

    def start(in_refs, out_refs, sems):
        for cp in copies(in_refs, out_refs, sems):
            cp.start()

    def finish(in_refs, out_refs, sems):
        for cp in copies(in_refs, out_refs, sems):
            cp.wait()

    return _Rider(pbs, [jax.ShapeDtypeStruct((3,) + p.shape[1:], p.dtype) for p in pbs],
                  [pltpu.SemaphoreType.DMA((3, n)), pltpu.SemaphoreType.DMA((3, n))], start, finish)


def _exchange(rider, name):
    n_in, n_out = len(rider.arrays), len(rider.out_shape)

    def body(*refs):
        ins, outs, sems = refs[:n_in], refs[n_in:n_in + n_out], refs[n_in + n_out:]
        rider.start(ins, outs, sems)
        if rider.relay:
            rider.relay(ins, outs, sems)
        rider.finish(ins, outs, sems)

    hbm = pl.BlockSpec(memory_space=pltpu.HBM)
    return pl.pallas_call(body, name=name, out_shape=rider.out_shape, in_specs=[hbm] * n_in,
                          out_specs=[hbm] * n_out, scratch_shapes=rider.scratch)(*rider.arrays)


def _rs_pair_sum(place, g4s, r1s, name):
    n = len(g4s)
    rows, cols = g4s[0].shape[2:]
    tr = min(rows, 512)

    def body(pl_ref, *refs):
        for g_ref, r_ref, pb_ref, own_ref in zip(refs[:n], refs[n:2 * n], refs[2 * n:3 * n], refs[3 * n:]):
            s = g_ref[0, 0] + r_ref[0]
            pb_ref[0] = s.astype(BF16)

            @pl.when(pl.program_id(1) == pl_ref[0])
            def _():
                own_ref[...] = s

    grid_spec = pltpu.PrefetchScalarGridSpec(
        num_scalar_prefetch=1, grid=(rows // tr, 4),
        in_specs=[pl.BlockSpec((1, 1, tr, cols), lambda r, k, p: (k, p[1], r, 0))] * n
        + [pl.BlockSpec((1, tr, cols), lambda r, k, p: (k, r, 0))] * n,
        out_specs=[pl.BlockSpec((1, tr, cols), lambda r, k, p: (k, r, 0))] * n
        + [pl.BlockSpec((tr, cols), lambda r, k, p: (r, 0))] * n)
    res = pl.pallas_call(
        body, name=name, grid_spec=grid_spec,
        out_shape=[jax.ShapeDtypeStruct((4, rows, cols), BF16)] * n + [jax.ShapeDtypeStruct((rows, cols), F32)] * n,
        compiler_params=_cparams(("arbitrary", "arbitrary")))(place, *g4s, *r1s)
    return res[:n], res[n:]


PACK_ROWS = 32
VEC_ROW = {"mix_norm": 0, "conv_norm": 1, "b_af": 2, "b_ab": 3, "gla_norm": 4, "xa_norm": 5, "mem_norm": 6,
           "mlp_norm": 7, "final_norm": 8}
LOSS_ROW, MAT_ROW = 9, 16
MAT_LANE = {"w_af": 0, "w_ab": GK, "conv_w": 2 * GK}
MAT_SRC_ROW = {"w_af": 0, "w_ab": LR, "conv_w": 0}


SMALL_WIDTH = {"mix_norm": D, "conv_w": 64, "conv_norm": CW, "w_af": 32, "b_af": GK, "w_ab": 32, "b_ab": GK,
               "gla_norm": 128, "xa_norm": D, "mem_norm": D, "mlp_norm": D, "final_norm": D}


def _small_reduce(acc, loss8):
    names = list(SMALL)
    n = len(names)
    widths = SMALL_WIDTH

    def body(*refs):
        acc_refs = dict(zip(names, refs[:n]))
        loss_ref, tot = refs[n], refs[n + 1]
        pk, all_ref, send_sems, recv_sems, local_sem = refs[n + 2:]

        pk[...] = jnp.zeros_like(pk)
        for k, row in VEC_ROW.items():
            if k == "gla_norm":
                g = functools.reduce(lambda a, b: a + b, [acc_refs[k][pl.ds(0, 1), pl.ds(h * 128, 128)]
                                                          for h in range(NH)])
            else:
                g = acc_refs[k][pl.ds(0, 1), :]
            pk[pl.ds(row, 1), pl.ds(0, widths[k])] = g
        pk[pl.ds(LOSS_ROW, 1), pl.ds(0, 128)] = loss_ref[pl.ds(0, 1), :]
        for k, lane in MAT_LANE.items():
            rows, cols = (3, CW) if k == "conv_w" else (LR, GK)
            pk[pl.ds(MAT_ROW, rows), pl.ds(lane, cols)] = acc_refs[k][pl.ds(MAT_SRC_ROW[k], rows), :]

        x, y, c = _place()
        me, sibling = (x, y, c), (x, y, 1 - c)
        chips = [(1 - x, y, c), (x, 1 - y, c), (1 - x, 1 - y, c)]

        def copy(k, block, to, own=False):
            px, py, pc = block
            dst = all_ref.at[4 * px + 2 * py + pc]
            return pltpu.make_async_remote_copy(
                src_ref=pk if own else dst, dst_ref=dst, send_sem=send_sems.at[k], recv_sem=recv_sems.at[k],
                device_id=to, device_id_type=MESH)

        mine = pltpu.make_async_copy(pk, all_ref.at[4 * x + 2 * y + c], local_sem)
        mine.start()
        first = [copy(0, me, sibling, own=True)] + [copy(1 + j, me, chip, own=True) for j, chip in enumerate(chips)]
        for cp in first:
            cp.start()
        passed = [copy(4 + j, chip, sibling) for j, chip in enumerate(chips)]
        for j, chip in enumerate(chips):
            copy(1 + j, chip, me).wait_recv()
            passed[j].start()
        copy(0, sibling, me).wait_recv()
        for j, (px, py, pc) in enumerate(chips):
            copy(4 + j, (px, py, 1 - pc), me).wait_recv()
        for cp in first + passed:
            cp.wait_send()
        mine.wait()
        total = all_ref[0]
        for d in range(1, NDEV):
            total = total + all_ref[d]
        tot[...] = total

    return pl.pallas_call(
        body, name="small_reduce", out_shape=jax.ShapeDtypeStruct((PACK_ROWS, D), F32),
        scratch_shapes=[pltpu.VMEM((PACK_ROWS, D), F32), pltpu.VMEM((NDEV, PACK_ROWS, D), F32),
                        pltpu.SemaphoreType.DMA((7,)), pltpu.SemaphoreType.DMA((7,)), pltpu.SemaphoreType.DMA],
    )(*[acc[k] for k in names], loss8)


def _small_adamw(tot, ws, ms, vs):
    names = list(SMALL)
    n = len(names)
    widths = SMALL_WIDTH

    def body(*refs):
        tot = refs[0]
        w_refs, m_refs, v_refs = [dict(zip(names, refs[1 + q * n:1 + (q + 1) * n])) for q in range(3)]
        outs = refs[1 + 3 * n:1 + 7 * n]
        g_out, d_out, m_out, v_out = [dict(zip(names, outs[q * n:(q + 1) * n])) for q in range(4)]
        cut = refs[1 + 7 * n]
        x, y, c = _place()
        dev = 4 * x + 2 * y + c
        for k in names:
            if k in VEC_ROW:
                g = tot[pl.ds(VEC_ROW[k], 1), pl.ds(0, widths[k])]
            else:
                rows, cols = (3, CW) if k == "conv_w" else (LR, GK)
                wd = widths[k]
                sel = jnp.where(_iota((cols, wd), 0) == dev * wd + _iota((cols, wd), 1), 1.0, 0.0).astype(BF16)
                cut[:, pl.ds(0, wd)] = _dot_exact_rhs(tot[pl.ds(MAT_ROW, LR), pl.ds(MAT_LANE[k], cols)], sel, 3)
                g = cut[pl.ds(0, rows), pl.ds(0, wd)]
            g_out[k][...] = g
            d_out[k][...], m_out[k][...], v_out[k][...] = _adamw_math(w_refs[k][...], g, m_refs[k][...],
                                                                       v_refs[k][...])

    shapes = [jax.ShapeDtypeStruct(ws[k].shape, F32) for k in names]
    res = pl.pallas_call(
        body, name="small_adamw", out_shape=shapes * 4, scratch_shapes=[pltpu.VMEM((LR, 128), F32)],
    )(tot, *[ws[k] for k in names], *[ms[k] for k in names], *[vs[k] for k in names])
    return {k: tuple(res[q * n + i] for q in range(4)) for i, k in enumerate(names)}


def _adamw_math(w, g, m, v):
    m = ADAM_B1 * m + (1.0 - ADAM_B1) * g
    v = ADAM_B2 * v + (1.0 - ADAM_B2) * (g * g)
    m_hat = m / (1.0 - ADAM_B1 ** ADAM_STEP)
    v_hat = v / (1.0 - ADAM_B2 ** ADAM_STEP)
    delta = -ADAM_LR * (m_hat / (jnp.sqrt(v_hat) + ADAM_EPS) + ADAM_WD * w)
    return delta, m, v


def _adamw(ws, ms, vs, owns, r2s, name, grads_transposed=False):
    n = len(ws)
    _, r, c = ws[0].shape
    tr = 256 if r % 256 == 0 else r

    def body(*refs):
        ins, outs = refs[:5 * n], refs[5 * n:]
        for q in range(n):
            w_ref, m_ref, v_ref, o_ref, r_ref = [ins[k * n + q] for k in range(5)]
            g_ref, d_ref, nm_ref, nv_ref = [outs[k * n + q] for k in range(4)]
            g = ((o_ref[...] + r_ref[0].astype(F32)) + r_ref[1].astype(F32)) + r_ref[2].astype(F32)
            g = g.T if grads_transposed else g
            g_ref[...] = g
            d_ref[...], nm_ref[...], nv_ref[...] = _adamw_math(w_ref[...], g, m_ref[...], v_ref[...])

    spec = pl.BlockSpec((None, tr, c), lambda i: (0, i, 0))
    if grads_transposed:
        own_spec, r2_spec = pl.BlockSpec((c, tr), lambda i: (0, i)), pl.BlockSpec((3, c, tr), lambda i: (0, 0, i))
    else:
        own_spec, r2_spec = pl.BlockSpec((tr, c), lambda i: (i, 0)), pl.BlockSpec((3, tr, c), lambda i: (0, i, 0))
    res = pl.pallas_call(
        body, name=name, grid=(r // tr,),
        in_specs=[spec] * (3 * n) + [own_spec] * n + [r2_spec] * n,
        out_specs=[spec] * (4 * n), out_shape=[jax.ShapeDtypeStruct((1, r, c), F32)] * (4 * n),
        compiler_params=_cparams(("arbitrary",)))(*ws, *ms, *vs, *owns, *r2s)
    return [tuple(res[k * n + q] for k in range(4)) for q in range(n)]


MATS = ("w_in", "w_out", "w_xq", "w_xo", "w_xkv", "w_up", "w_down")
SMALL = ("mix_norm", "conv_w", "conv_norm", "w_af", "b_af", "w_ab", "b_ab", "gla_norm", "xa_norm", "mem_norm",
         "mlp_norm", "final_norm")
WEIGHTS = ("mix_norm", "w_in", "conv_w", "conv_norm", "w_af", "b_af", "w_ab", "b_ab", "gla_norm", "w_out", "xa_norm",
           "mem_norm", "w_xq", "w_xkv", "w_xo", "mlp_norm", "w_up", "w_down", "final_norm")
SMALL_SHARDED = {"conv_w": (3, 64), "w_af": (LR, 32), "w_ab": (LR, 32)}
SMALL_PACK_ROWS = 16


def kernel(x, mem, mix_norm, w_in, conv_w, conv_norm, w_af, b_af, w_ab, b_ab, gla_norm, w_out, xa_norm, mem_norm, w_xq, w_xkv, w_xo, mlp_norm, w_up, w_down, final_norm, loss_target, m_mix_norm, m_w_in, m_conv_w, m_conv_norm, m_w_af, m_b_af, m_w_ab, m_b_ab, m_gla_norm, m_w_out, m_xa_norm, m_mem_norm, m_w_xq, m_w_xkv, m_w_xo, m_mlp_norm, m_w_up, m_w_down, m_final_norm, v_mix_norm, v_w_in, v_conv_w, v_conv_norm, v_w_af, v_b_af, v_w_ab, v_b_ab, v_gla_norm, v_w_out, v_xa_norm, v_mem_norm, v_w_xq, v_w_xkv, v_w_xo, v_mlp_norm, v_w_up, v_w_down, v_final_norm):
    w = dict(mix_norm=mix_norm, w_in=w_in, conv_w=conv_w, conv_norm=conv_norm, w_af=w_af, b_af=b_af, w_ab=w_ab,
             b_ab=b_ab, gla_norm=gla_norm, w_out=w_out, xa_norm=xa_norm, mem_norm=mem_norm, w_xq=w_xq, w_xkv=w_xkv,
             w_xo=w_xo, mlp_norm=mlp_norm, w_up=w_up, w_down=w_down, final_norm=final_norm)
    mom = dict(mix_norm=m_mix_norm, w_in=m_w_in, conv_w=m_conv_w, conv_norm=m_conv_norm, w_af=m_w_af, b_af=m_b_af,
               w_ab=m_w_ab, b_ab=m_b_ab, gla_norm=m_gla_norm, w_out=m_w_out, xa_norm=m_xa_norm, mem_norm=m_mem_norm,
               w_xq=m_w_xq, w_xkv=m_w_xkv, w_xo=m_w_xo, mlp_norm=m_mlp_norm, w_up=m_w_up, w_down=m_w_down,
               final_norm=m_final_norm)
    var = dict(mix_norm=v_mix_norm, w_in=v_w_in, conv_w=v_conv_w, conv_norm=v_conv_norm, w_af=v_w_af, b_af=v_b_af,
               w_ab=v_w_ab, b_ab=v_b_ab, gla_norm=v_gla_norm, w_out=v_w_out, xa_norm=v_xa_norm, mem_norm=v_mem_norm,
               w_xq=v_w_xq, w_xkv=v_w_xkv, w_xo=v_w_xo, mlp_norm=v_mlp_norm, w_up=v_w_up, w_down=v_w_down,
               final_norm=v_final_norm)
    xi, yi, ci = _place()
    two_d = lambda a: a.reshape(a.shape[-2:]) if a.ndim == 3 else a.reshape(1, a.shape[-1])

    small = jnp.concatenate([w[n].reshape(-1) for n in SMALL_SHARDED])
    small = jnp.pad(small, (0, SMALL_PACK_ROWS * 128 - small.shape[0])).reshape(SMALL_PACK_ROWS, 128)
    shard = {n: two_d(w[n]).astype(BF16) for n in MATS}
    for n in ("w_in", "w_up"):
        shard[n] = shard[n].T
    vec = {n: two_d(w[n]) for n in SMALL if n not in SMALL_SHARDED}
    place = jnp.stack([2 * xi + yi, ci]).astype(jnp.int32)
    loss8, grad_x, small_acc, own, from_chips = _step(x[0], mem[0], loss_target[0], shard, small, vec, place)

    tot = _small_reduce(small_acc, loss8)
    small_out = _small_adamw(tot, *[{n: two_d(d[n]) for n in SMALL} for d in (w, mom, var)])
    loss = tot[LOSS_ROW, 0]

    out_g, out_d, out_m, out_v = {}, {}, {}, {}
    wmv = {n: [a.transpose(0, 2, 1) if n == "w_in" else a for a in (w[n], mom[n], var[n])] for n in MATS}
    for shape in dict.fromkeys(wmv[n][0].shape for n in MATS):
        names = [n for n in MATS if wmv[n][0].shape == shape]
        res = _adamw(*[[wmv[n][k] for n in names] for k in range(3)], [own[n] for n in names],
                     [from_chips[n] for n in names], "adamw_" + "_".join(names), grads_transposed=names == ["w_up"])
        for n, r in zip(names, res):
            out_g[n], out_d[n], out_m[n], out_v[n] = [a.transpose(0, 2, 1) for a in r] if n == "w_in" else r
    for n in SMALL:
        out_g[n], out_d[n], out_m[n], out_v[n] = [a.reshape(w[n].shape) for a in small_out[n]]

    return (loss, grad_x[None], *[out_g[n] for n in WEIGHTS], *[out_d[n] for n in WEIGHTS],
            *[out_m[n] for n in WEIGHTS], *[out_v[n] for n in WEIGHTS])
```

```python
import functools
import itertools

import jax
import jax.numpy as jnp
from jax import lax
from jax.experimental import pallas as pl
from jax.experimental.pallas import tpu as pltpu

F32 = jnp.float32
BF16 = jnp.bfloat16

D = 1024
CW = 512
GK = 256
GV = 512
NH = 4
CH = 64
LR = 16
NMEM = 256
XD = 256
FF = 4096
ZW = 3104
ZC = 3200
EPS = 1e-6
NDEV = 8

ZB_CB, ZB_CC, ZB_CU, ZB_V, ZB_G = 0, 1, 2, 4, 5
ZB_Q, ZB_K = 6, 7
ZB_LR = 24

TM = 512
TM_MLP = 256
TM_MLP_FWD = 512
TF = 512
TB = 512
TB_BWD = 512
TT = 2048
VMEM_LIMIT = 56 * 1024 * 1024

ADAM_LR, ADAM_B1, ADAM_B2, ADAM_EPS, ADAM_WD, ADAM_STEP = 0.001, 0.9, 0.999, 1e-08, 0.01, 10

XKV_SHARD = 2 * D // NDEV

MESH = pl.DeviceIdType.MESH


def _cparams(sem):
    return pltpu.CompilerParams(dimension_semantics=sem, vmem_limit_bytes=VMEM_LIMIT)


def _call(body, name, grid, in_specs, out_specs, out_shape, scratch, args, riders=()):
    n_in, n_out, n_scr = len(in_specs), len(out_specs), len(scratch)
    counts = [(len(r.arrays), len(r.out_shape), len(r.scratch)) for r in riders]

    def take(refs, pos, sizes):
        groups = []
        for size in sizes:
            groups.append(refs[pos:pos + size])
            pos += size
        return groups, pos

    def wrapped(*refs):
        ins, pos = refs[:n_in], n_in
        r_ins, pos = take(refs, pos, [c[0] for c in counts])
        outs, pos = refs[pos:pos + n_out], pos + n_out
        r_outs, pos = take(refs, pos, [c[1] for c in counts])
        scr, pos = refs[pos:pos + n_scr], pos + n_scr
        r_scr, pos = take(refs, pos, [c[2] for c in counts])
        ids = [pl.program_id(d) for d in range(len(grid))]
        first = functools.reduce(lambda a, b: a & b, [i == 0 for i in ids])
        last = functools.reduce(lambda a, b: a & b, [i == g - 1 for i, g in zip(ids, grid)])

        @pl.when(first)
        def _():
            for r, a, b, c in zip(riders, r_ins, r_outs, r_scr):
                r.start(a, b, c)

        body(*ins, *outs, *scr)

        if any(r.relay for r in riders):
            at = [max(g - 2, 0) for g in grid]

            @pl.when(functools.reduce(lambda a, b: a & b, [i == s for i, s in zip(ids, at)]))
            def _():
                for r, a, b, c in zip(riders, r_ins, r_outs, r_scr):
                    if r.relay:
                        r.relay(a, b, c)

        @pl.when(last)
        def _():
            for r, a, b, c in zip(riders, r_ins, r_outs, r_scr):
                r.finish(a, b, c)

    hbm = pl.BlockSpec(memory_space=pltpu.HBM)
    r_args = [a for r in riders for a in r.arrays]
    r_shapes = [s for r in riders for s in r.out_shape]
    return pl.pallas_call(
        wrapped if riders else body, name=name, grid=grid, in_specs=list(in_specs) + [hbm] * len(r_args),
        out_specs=list(out_specs) + [hbm] * len(r_shapes), out_shape=list(out_shape) + r_shapes,
        scratch_shapes=list(scratch) + [s for r in riders for s in r.scratch],
        compiler_params=_cparams(("arbitrary",) * len(grid)))(*args, *r_args)


def _dot(a, b):
    return jnp.dot(a.astype(BF16), b.astype(BF16), preferred_element_type=F32)


def _dot_nt(a, b):
    return lax.dot_general(a.astype(BF16), b.astype(BF16), (((1,), (1,)), ((), ())), preferred_element_type=F32)


def _dot_tn(a, b):
    return lax.dot_general(a.astype(BF16), b.astype(BF16), (((0,), (0,)), ((), ())), preferred_element_type=F32)


def _split(x, n):
    parts = []
    for _ in range(n):
        p = x.astype(BF16)
        parts.append(p)
        x = x - p.astype(F32)
    return parts


def _dot_exact_lhs(m, x, n):
    return functools.reduce(lambda a, b: a + b, [jnp.dot(m, p, preferred_element_type=F32) for p in _split(x, n)])


def _dot_exact_rhs(x, m, n):
    return functools.reduce(lambda a, b: a + b, [jnp.dot(p, m, preferred_element_type=F32) for p in _split(x, n)])


def _rms(x, g):
    r = lax.rsqrt(jnp.mean(x * x, axis=-1, keepdims=True) + EPS)
    return x * r * g, r


def _rms_bwd(x, r, g, dy):
    xr = x * r
    u = dy * g
    dx = r * (u - xr * jnp.mean(u * xr, axis=-1, keepdims=True))
    return dx, jnp.sum(dy * xr, axis=0, keepdims=True)


def _iota(shape, dim):
    return lax.broadcasted_iota(jnp.int32, shape, dim)


def _sigmoid(x):
    return 1.0 / (1.0 + jnp.exp(-x))


def _acc_rows(ref, row):
    ref[...] += jnp.broadcast_to(row, ref.shape)


def _inproj(x, g, w_t, riders=()):
    t = x.shape[0]
    tm = min(TM, t)

    def body(x_ref, g_ref, w_ref, z_ref, h_ref):
        h, _ = _rms(x_ref[...], g_ref[...])
        hb = h.astype(BF16)
        h_ref[...] = hb
        z_ref[...] = _dot_nt(hb, w_ref[...])

    return _call(
        body, "inproj", (t // tm,),
        [pl.BlockSpec((tm, D), lambda i: (i, 0)), pl.BlockSpec((1, D), lambda i: (0, 0)),
         pl.BlockSpec((ZC, D), lambda i: (0, 0))],
        [pl.BlockSpec((tm, ZC), lambda i: (i, 0)), pl.BlockSpec((tm, D), lambda i: (i, 0))],
        [jax.ShapeDtypeStruct((t, ZC), F32), jax.ShapeDtypeStruct((t, D), BF16)], [], (x, g, w_t), riders)


def _kv_proj(mem, g, w):
    def body(m_ref, g_ref, w_ref, kv_ref, mn_ref):
        mn, _ = _rms(m_ref[...], g_ref[...])
        mb = mn.astype(BF16)
        mn_ref[...] = mb
        for j in range(NDEV):
            kv_ref[:, j * XKV_SHARD:(j + 1) * XKV_SHARD] = jnp.dot(mb, w_ref[j], preferred_element_type=F32)

    return pl.pallas_call(
        body, name="kv_proj",
        out_shape=[jax.ShapeDtypeStruct((NMEM, 2 * D), F32), jax.ShapeDtypeStruct((NMEM, D), BF16)],
        compiler_params=pltpu.CompilerParams(vmem_limit_bytes=VMEM_LIMIT))(mem, g, w)


def _softmax_head(qb, kb):
    s = _dot_nt(qb, kb) * (1.0 / 16.0)
    e = jnp.exp(s - jnp.max(s, axis=-1, keepdims=True))
    return e / jnp.sum(e, axis=-1, keepdims=True)


def _attn_fwd(x, z, o_f, o_b, conv_w, conv_norm, gla_norm4, w_out, g, w_xq, kb, vb, w_xo):
    t = x.shape[0]
    tm = min(TM, t)
    nblk = t // tm
    jmap = lambda i: i

    def body(x_ref, zq_ref, zk_ref, zv_ref, zg_ref, cb_ref, cc_ref, cu_ref, ccp_ref, ccn_ref, cup_ref, cun_ref,
             of_ref, ob_ref, cw_ref, cn_ref, gn_ref, wo_ref, g_ref, wq_ref, k_ref, v_ref, wx_ref,
             x1_ref, x2_ref, xn_ref, q_ref, a_ref, y_ref, opre_ref):
        j = pl.program_id(0)
        zv = zv_ref[...]
        sb = _head_sum((zq_ref[...] * 0.125) * zk_ref[...], 64, 128)
        o_pre = of_ref[...] + ob_ref[...] - sb * zv
        opre_ref[...] = o_pre
        on, _ = _head_norm(o_pre)
        zg = zg_ref[...]
        y_ref[:, CW:] = (on * gn_ref[...] * (zg * _sigmoid(zg))).astype(BF16)
        cb = cb_ref[...]
        _, _, _, conv = _conv_parts(cb, cc_ref[...], cu_ref[...], ccp_ref[pl.ds(7, 1), :], cup_ref[pl.ds(7, 1), :],
                                    ccn_ref[pl.ds(0, 1), :], cun_ref[pl.ds(0, 1), :], cw_ref, j == 0,
                                    j == nblk - 1, tm)
        yc = cb * conv
        gm = _group_sum(yc * yc) * (1.0 / 64.0)
        y_ref[:, :CW] = (yc * lax.rsqrt(gm + EPS) * cn_ref[...]).astype(BF16)

        x1 = x_ref[...] + jnp.dot(y_ref[...], wo_ref[...], preferred_element_type=F32)
        x1_ref[...] = x1
        xn, _ = _rms(x1, g_ref[...])
        xb = xn.astype(BF16)
        xn_ref[...] = xb
        qb = jnp.dot(xb, wq_ref[...], preferred_element_type=F32).astype(BF16)
        q_ref[...] = qb
        heads = [slice(h * XD, (h + 1) * XD) for h in range(NH)]
        ps = [_softmax_head(qb[:, hs], k_ref[:, hs]) for hs in heads]
        for hs, p in zip(heads, ps):
            a_ref[:, hs] = _dot(p, v_ref[:, hs]).astype(BF16)
        x2_ref[...] = x1 + jnp.dot(a_ref[...], wx_ref[...], preferred_element_type=F32)

    tok = lambda i: (i, 0)
    full = lambda i: (0, 0)
    once = pl.Buffered(1)
    tokd, tokv = pl.BlockSpec((tm, D), tok), pl.BlockSpec((tm, GV), tok)
    weight = pl.BlockSpec((D, D), full, pipeline_mode=once)
    ccp, ccn = _halo_specs(tm, nblk, t, ZB_CC, jmap)
    cup, cun = _halo_specs(tm, nblk, t, ZB_CU, jmap)
    in_specs = [tokd, _zspec(tm, GK, ZB_Q, jmap), _zspec(tm, GK, ZB_K, jmap), _zspec(tm, GV, ZB_V, jmap),
                _zspec(tm, GV, ZB_G, jmap), _zspec(tm, CW, ZB_CB, jmap), _zspec(tm, CW, ZB_CC, jmap),
                _zspec(tm, CW, ZB_CU, jmap), ccp, ccn, cup, cun, tokv, tokv,
                pl.BlockSpec((3, CW), full), pl.BlockSpec((1, CW), full), pl.BlockSpec((1, GV), full),
                weight, pl.BlockSpec((1, D), full), weight, pl.BlockSpec((NMEM, D), full),
                pl.BlockSpec((NMEM, D), full), weight]
    return pl.pallas_call(
        body, name="attn_fwd", grid=(nblk,), in_specs=in_specs, out_specs=[tokd] * 6 + [tokv],
        out_shape=[jax.ShapeDtypeStruct((t, D), F32), jax.ShapeDtypeStruct((t, D), F32),
                   jax.ShapeDtypeStruct((t, D), BF16), jax.ShapeDtypeStruct((t, D), BF16),
                   jax.ShapeDtypeStruct((t, D), BF16), jax.ShapeDtypeStruct((t, D), BF16),
                   jax.ShapeDtypeStruct((t, GV), F32)],
        compiler_params=_cparams(("arbitrary",)))(
            x, z, z, z, z, z, z, z, z, z, z, z, o_f, o_b, conv_w, conv_norm, gla_norm4, w_out, g, w_xq, kb, vb, w_xo)


def _mlp_fwd(x2, g, w_up_t, w_down, fg, target):
    t = x2.shape[0]
    tm = min(TM_MLP_FWD, t)

    def body(x_ref, g_ref, wu_ref, wd_ref, fg_ref, t_ref, h1_ref, xn_ref, dx_ref, dxb_ref, loss_ref, dfg_ref, ab):
        @pl.when(pl.program_id(0) == 0)
        def _():
            loss_ref[...] = jnp.zeros_like(loss_ref)
            dfg_ref[...] = jnp.zeros_like(dfg_ref)

        x = x_ref[...]
        xn, _ = _rms(x, g_ref[...])
        xnb = xn.astype(BF16)
        xn_ref[...] = xnb
        for q in range(FF // TF):
            cols = slice(q * TF, (q + 1) * TF)
            h1 = _dot_nt(xnb, wu_ref[cols, :])
            h1_ref[:, cols] = h1.astype(BF16)
            hr = jnp.maximum(h1, 0.0)
            ab[:, cols] = (hr * hr).astype(BF16)
        x3 = x + jnp.dot(ab[...], wd_ref[...], preferred_element_type=F32)
        y, r = _rms(x3, fg_ref[...])
        e = y - t_ref[...]
        row = jnp.mean(e * e, axis=-1, keepdims=True)
        _acc_rows(loss_ref, 0.5 * jnp.sum(row, axis=0, keepdims=True))
        dx, dfg = _rms_bwd(x3, r, fg_ref[...], e * (1.0 / D))
        dx_ref[...] = dx
        dxb_ref[...] = dx.astype(BF16)
        _acc_rows(dfg_ref, dfg)

    tok = lambda i: (i, 0)
    full = lambda i: (0, 0)
    once = pl.Buffered(1)
    return pl.pallas_call(
        body, name="mlp_fwd", grid=(t // tm,),
        in_specs=[pl.BlockSpec((tm, D), tok), pl.BlockSpec((1, D), full),
                  pl.BlockSpec((FF, D), full, pipeline_mode=once), pl.BlockSpec((FF, D), full, pipeline_mode=once),
                  pl.BlockSpec((1, D), full), pl.BlockSpec((tm, D), tok)],
        out_specs=[pl.BlockSpec((tm, FF), tok), pl.BlockSpec((tm, D), tok), pl.BlockSpec((tm, D), tok),
                   pl.BlockSpec((tm, D), tok), pl.BlockSpec((8, 128), full), pl.BlockSpec((8, D), full)],
        out_shape=[jax.ShapeDtypeStruct((t, FF), BF16), jax.ShapeDtypeStruct((t, D), BF16),
                   jax.ShapeDtypeStruct((t, D), F32), jax.ShapeDtypeStruct((t, D), BF16),
                   jax.ShapeDtypeStruct((8, 128), F32), jax.ShapeDtypeStruct((8, D), F32)],
        scratch_shapes=[pltpu.VMEM((tm, FF), BF16)],
        compiler_params=_cparams(("arbitrary",)))(x2, g, w_up_t, w_down, fg, target)


def _mlp_bwd(dx3, dx3b, h1b, w_down, w_up_t, x2, g):
    t = x2.shape[0]
    tm = min(TM_MLP, t)

    def body(dx_ref, dxb_ref, h1_ref, wd_ref, wu_ref, x_ref, g_ref, a_ref, dh_ref, dx2_ref, dx2b_ref, dg_ref):
        @pl.when(pl.program_id(0) == 0)
        def _():
            dg_ref[...] = jnp.zeros_like(dg_ref)

        for q in range(FF // TF):
            cols = slice(q * TF, (q + 1) * TF)
            hr = jnp.maximum(h1_ref[:, cols].astype(F32), 0.0)
            da = _dot_nt(dxb_ref[...], wd_ref[cols, :])
            a_ref[:, cols] = (hr * hr).astype(BF16)
            dh_ref[:, cols] = (da * 2.0 * hr).astype(BF16)
        dxn = jnp.dot(dh_ref[...], wu_ref[...], preferred_element_type=F32)
        x = x_ref[...]
        r = lax.rsqrt(jnp.mean(x * x, axis=-1, keepdims=True) + EPS)
        dx, dg = _rms_bwd(x, r, g_ref[...], dxn)
        dx2 = dx_ref[...] + dx
        dx2_ref[...] = dx2
        dx2b_ref[...] = dx2.astype(BF16)
        _acc_rows(dg_ref, dg)

    tok = lambda i: (i, 0)
    full = lambda i: (0, 0)
    once = pl.Buffered(1)
    return pl.pallas_call(
        body, name="mlp_bwd", grid=(t // tm,),
        in_specs=[pl.BlockSpec((tm, D), tok), pl.BlockSpec((tm, D), tok), pl.BlockSpec((tm, FF), tok),
                  pl.BlockSpec((FF, D), full, pipeline_mode=once), pl.BlockSpec((FF, D), full, pipeline_mode=once),
                  pl.BlockSpec((tm, D), tok), pl.BlockSpec((1, D), full)],
        out_specs=[pl.BlockSpec((tm, FF), tok), pl.BlockSpec((tm, FF), tok), pl.BlockSpec((tm, D), tok),
                   pl.BlockSpec((tm, D), tok), pl.BlockSpec((8, D), full)],
        out_shape=[jax.ShapeDtypeStruct((t, FF), BF16), jax.ShapeDtypeStruct((t, FF), BF16),
                   jax.ShapeDtypeStruct((t, D), F32), jax.ShapeDtypeStruct((t, D), BF16),
                   jax.ShapeDtypeStruct((8, D), F32)],
        compiler_params=_cparams(("arbitrary",)))(dx3, dx3b, h1b, w_down, w_up_t, x2, g)


def _attn_bwd(x1, dx2, dx2b, qb, kb, vb, w_xo, w_xq, w_out, g, riders=()):
    t = x1.shape[0]
    tm = min(TM, t)

    def body(x_ref, dx2_ref, dx2b_ref, q_ref, k_ref, v_ref, wx_ref, wq_ref, wo_ref, g_ref,
             dx1_ref, dx1b_ref, dy_ref, dq_ref, dkv_ref, dg_ref):
        @pl.when(pl.program_id(0) == 0)
        def _():
            dkv_ref[...] = jnp.zeros_like(dkv_ref)
            dg_ref[...] = jnp.zeros_like(dg_ref)

        datt = _dot_nt(dx2b_ref[...], wx_ref[...]).astype(BF16)
        heads = [slice(h * XD, (h + 1) * XD) for h in range(NH)]
        ps = [_softmax_head(q_ref[:, hs], k_ref[:, hs]) for hs in heads]
        dps = [_dot_nt(datt[:, hs], v_ref[:, hs]) for hs in heads]
        dss = [(p * (dp - jnp.sum(dp * p, axis=-1, keepdims=True)) * (1.0 / 16.0)).astype(BF16)
               for p, dp in zip(ps, dps)]
        for h, (hs, p, ds) in enumerate(zip(heads, ps, dss)):
            dq_ref[:, hs] = _dot(ds, k_ref[:, hs]).astype(BF16)
            dkv_ref[:, hs] += _dot_tn(ds, q_ref[:, hs])
            dkv_ref[:, D + h * XD:D + (h + 1) * XD] += _dot_tn(p, datt[:, hs])
        dxn = _dot_nt(dq_ref[...], wq_ref[...])
        x = x_ref[...]
        r = lax.rsqrt(jnp.mean(x * x, axis=-1, keepdims=True) + EPS)
        dx, dg = _rms_bwd(x, r, g_ref[...], dxn)
        dx1 = dx2_ref[...] + dx
        dx1_ref[...] = dx1
        dx1b = dx1.astype(BF16)
        dx1b_ref[...] = dx1b
        dy_ref[...] = _dot_nt(dx1b, wo_ref[...])
        _acc_rows(dg_ref, dg)

    tok = lambda i: (i, 0)
    full = lambda i: (0, 0)
    return _call(
        body, "attn_bwd", (t // tm,),
        [pl.BlockSpec((tm, D), tok), pl.BlockSpec((tm, D), tok), pl.BlockSpec((tm, D), tok),
         pl.BlockSpec((tm, D), tok), pl.BlockSpec((NMEM, D), full), pl.BlockSpec((NMEM, D), full),
         pl.BlockSpec((D, D), full), pl.BlockSpec((D, D), full), pl.BlockSpec((D, D), full),
         pl.BlockSpec((1, D), full)],
        [pl.BlockSpec((tm, D), tok), pl.BlockSpec((tm, D), tok), pl.BlockSpec((tm, D), tok),
         pl.BlockSpec((tm, D), tok), pl.BlockSpec((NMEM, 2 * D), full), pl.BlockSpec((8, D), full)],
        [jax.ShapeDtypeStruct((t, D), F32), jax.ShapeDtypeStruct((t, D), BF16),
         jax.ShapeDtypeStruct((t, D), F32), jax.ShapeDtypeStruct((t, D), BF16),
         jax.ShapeDtypeStruct((NMEM, 2 * D), F32), jax.ShapeDtypeStruct((8, D), F32)], [],
        (x1, dx2, dx2b, qb, kb, vb, w_xo, w_xq, w_out, g), riders)


def _kv_bwd(dkv, memn, mem, g, w):
    def body(dkv_ref, mn_ref, m_ref, g_ref, w_ref, dw_ref, dg_ref):
        dkvb = dkv_ref[...].astype(BF16)
        dmn = jnp.zeros((NMEM, D), F32)
        for j in range(NDEV):
            cols = slice(j * XKV_SHARD, (j + 1) * XKV_SHARD)
            dw_ref[j] = _dot_tn(mn_ref[...], dkvb[:, cols])
            dmn += _dot_nt(dkvb[:, cols], w_ref[j])
        m = m_ref[...]
        r = lax.rsqrt(jnp.mean(m * m, axis=-1, keepdims=True) + EPS)
        dg_ref[...] = jnp.broadcast_to(jnp.sum(dmn * m * r, axis=0, keepdims=True), dg_ref.shape)

    return pl.pallas_call(
        body, name="kv_bwd",
        out_shape=[jax.ShapeDtypeStruct((NDEV, D, XKV_SHARD), F32), jax.ShapeDtypeStruct((8, D), F32)],
        compiler_params=pltpu.CompilerParams(vmem_limit_bytes=VMEM_LIMIT))(dkv, memn, mem, g, w)


def _inproj_bwd(dz, w_t, x, dx1, g, riders=()):
    t = x.shape[0]
    tm = min(TM, t)

    def body(dz_ref, w_ref, x_ref, dx1_ref, g_ref, gx_ref, dg_ref):
        @pl.when(pl.program_id(0) == 0)
        def _():
            dg_ref[...] = jnp.zeros_like(dg_ref)

        dh = jnp.dot(dz_ref[...], w_ref[...], preferred_element_type=F32)
        x = x_ref[...]
        r = lax.rsqrt(jnp.mean(x * x, axis=-1, keepdims=True) + EPS)
        dx, dg = _rms_bwd(x, r, g_ref[...], dh)
        gx_ref[...] = dx1_ref[...] + dx
        _acc_rows(dg_ref, dg)

    tok = lambda i: (i, 0)
    full = lambda i: (0, 0)
    return _call(
        body, "inproj_bwd", (t // tm,),
        [pl.BlockSpec((tm, ZC), tok), pl.BlockSpec((ZC, D), full), pl.BlockSpec((tm, D), tok),
         pl.BlockSpec((tm, D), tok), pl.BlockSpec((1, D), full)],
        [pl.BlockSpec((tm, D), tok), pl.BlockSpec((8, D), full)],
        [jax.ShapeDtypeStruct((t, D), F32), jax.ShapeDtypeStruct((8, D), F32)], [], (dz, w_t, x, dx1, g), riders)


def _matmul_tn(a, b, name, rows=None, riders=()):
    t, k = a.shape
    n = b.shape[1]
    tk, tn = [1024 if size % 1024 == 0 else 640 for size in (k, n)]
    tt = min(TT, t)
    rows = rows or k

    def body(a_ref, b_ref, o_ref):
        @pl.when(pl.program_id(2) == 0)
        def _():
            o_ref[...] = jnp.zeros_like(o_ref)

        o_ref[...] += _dot_tn(a_ref[...], b_ref[...])

    return _call(
        body, name, (k // tk, n // tn, t // tt),
        [pl.BlockSpec((tt, tk), lambda i, j, s: (s, i)), pl.BlockSpec((tt, tn), lambda i, j, s: (s, j))],
        [pl.BlockSpec((tk, tn), lambda i, j, s: (i, j))], [jax.ShapeDtypeStruct((rows, n), F32)], [], (a, b), riders)


def _lane_head(shape, dim, shift):
    return _iota(shape, dim) >> shift


CUM_ROWS = 128


def _chunk_cumsum(x, upper, n):
    r, c = _iota((CUM_ROWS, CUM_ROWS), 0), _iota((CUM_ROWS, CUM_ROWS), 1)
    tri = (c >= r) if upper else (c <= r)
    cum = jnp.where(((r >> 6) == (c >> 6)) & tri, 1.0, 0.0).astype(BF16)
    return jnp.concatenate([_dot_exact_lhs(cum, x[g:g + CUM_ROWS], n) for g in range(0, x.shape[0], CUM_ROWS)],
                           axis=0)


def _gla_recompute(q_raw, k, lr, wpad, bias, rev, tb):
    pre = _dot(lr, wpad) + bias
    la = (jnp.minimum(pre, 0.0) - jnp.log(1.0 + jnp.exp(-jnp.abs(pre)))) * (1.0 / 16.0)
    b = _chunk_cumsum(la, rev, 3)
    e, ei = jnp.exp(b), jnp.exp(-b)
    qt = (q_raw * 0.125) * e
    kt = k * ei
    return pre, b, e, ei, qt, kt


def _stack_heads(x, shift):
    if shift == 7:
        zero = jnp.zeros((x.shape[0], 128), BF16)
        xb = x.astype(BF16)
        return jnp.concatenate([jnp.concatenate([xb[:, h * 128:(h + 1) * 128] if g == h else zero
                                                 for g in range(NH)], axis=1) for h in range(NH)], axis=0)
    head = _lane_head(x.shape, 1, shift)
    return jnp.concatenate([jnp.where(head == h, x, 0.0) for h in range(NH)], axis=0).astype(BF16)


def _fold_heads(x, shift):
    if shift == 7:
        return jnp.concatenate([x[h * CH:(h + 1) * CH, h * 128:(h + 1) * 128] for h in range(NH)], axis=1)
    head = _lane_head((CH, x.shape[1]), 1, shift)
    return functools.reduce(lambda a, b: a + b,
                            [jnp.where(head == h, x[h * CH:(h + 1) * CH], 0.0) for h in range(NH)])


def _wide_mask(rev):
    r, s = _iota((CH, NH * CH), 0), _iota((CH, NH * CH), 1) & (CH - 1)
    return (s >= r) if rev else (s <= r)


def _rows_by_head(x):
    w = x.shape[1] // NH
    return jnp.concatenate([x[:, h * w:(h + 1) * w] for h in range(NH)], axis=0)


def _lanes_by_head(x):
    return jnp.concatenate([x[h * CH:(h + 1) * CH] for h in range(NH)], axis=1)


def _state_compact(xt):
    head = _lane_head((128, GK), 1, 6)
    return functools.reduce(lambda a, b: a + b,
                            [jnp.where(head == h, xt[h * 128:(h + 1) * 128], 0.0) for h in range(NH)])


def _conv_parts(cb, cc, cu, ccp, cup, ccn, cun, cw_ref, first, last, tb):
    h = cc * cu
    hp = jnp.where(first, 0.0, ccp * cup)
    hn = jnp.where(last, 0.0, ccn * cun)
    rows = _iota(h.shape, 0)
    h_m1 = jnp.where(rows == 0, hp, pltpu.roll(h, 1, 0))
    h_p1 = jnp.where(rows == tb - 1, hn, pltpu.roll(h, tb - 1, 0))
    conv = cw_ref[pl.ds(0, 1), :] * h_m1 + cw_ref[pl.ds(1, 1), :] * h + cw_ref[pl.ds(2, 1), :] * h_p1
    return h, h_m1, h_p1, conv


def _head_sum(x, w_in, w_out):
    shape, sh_in, sh_out = (2 * w_in, 2 * w_out), w_in.bit_length() - 1, w_out.bit_length() - 1
    sel = jnp.where((_iota(shape, 0) >> sh_in) == (_iota(shape, 1) >> sh_out), 1.0, 0.0).astype(BF16)
    return jnp.concatenate([_dot_exact_rhs(x[:, s:s + 2 * w_in], sel, 2) for s in range(0, NH * w_in, 2 * w_in)],
                           axis=1)


def _group_sum(x):
    ones = jnp.where((_iota((128, 128), 0) >> 6) == (_iota((128, 128), 1) >> 6), 1.0, 0.0).astype(BF16)
    return jnp.concatenate([_dot_exact_rhs(x[:, s:s + 128], ones, 2) for s in range(0, x.shape[1], 128)], axis=1)


def _head_norm(o):
    ons, rs = [], []
    for h in range(NH):
        slab = o[:, h * 128:(h + 1) * 128]
        r = lax.rsqrt(jnp.mean(slab * slab, axis=-1, keepdims=True) + EPS)
        ons.append(slab * r)
        rs.append(jnp.broadcast_to(r, slab.shape))
    return jnp.concatenate(ons, axis=1), jnp.concatenate(rs, axis=1)


def _zspec(tb, width, blk, jmap):
    return pl.BlockSpec((tb, width), lambda i: (jmap(i), blk))


def _halo_specs(tb, nblk, t, blk, jmap):
    prev = pl.BlockSpec((8, CW), lambda i: (jnp.maximum(jmap(i) * (tb // 8) - 1, 0), blk))
    nxt = pl.BlockSpec((8, CW), lambda i: (jnp.minimum((jmap(i) + 1) * (tb // 8), t // 8 - 1), blk))
    return prev, nxt


def _gla_fwd_block(q_ref, k_ref, v_ref, lr_ref, w_ref, bias_ref, o_ref, sd_ref, st, b_scr, rev, tb):
    nb = tb // CH
    _, b, _, _, qt, kt = _gla_recompute(q_ref[...], k_ref[...], lr_ref[...], w_ref[...], bias_ref[...], rev, tb)
    v = v_ref[...]
    b_scr[...] = b
    yield
    maskw = _wide_mask(rev)
    order = list(reversed(range(nb))) if rev else list(range(nb))
    rows = [slice(c * CH, (c + 1) * CH) for c in range(nb)]
    state = st[...]
    for c in order:
        gdec = jnp.exp(b_scr[pl.ds(c * CH + (0 if rev else CH - 1), 1), :])
        sd_ref[c] = state
        a = jnp.where(maskw, _dot_nt(qt[rows[c]], _stack_heads(kt[rows[c]], 6)), 0.0)
        o_inter = _lanes_by_head(_dot_nt(_stack_heads(qt[rows[c]], 6), state))
        o_ref[pl.ds(c * CH, CH), :] = _dot(a, _stack_heads(v[rows[c]], 7)) + o_inter
        state = state * gdec + _state_compact(_dot_tn(v[rows[c]], kt[rows[c]] * gdec))
        yield
    st[...] = state
    yield


def _gla_fwd(z, waf_pad, b_af, wab_pad, b_ab, riders=()):
    t = z.shape[0]
    tb = min(TB, t)
    nblk, nb = t // tb, tb // CH
    jmaps = (lambda i: i, lambda i: nblk - 1 - i)

    def body(qf, kf, vf, lrf, qr, kr, vr, lrr, wf, bf, wr, br, of_ref, sdf_ref, or_ref, sdr_ref,
             st_f, st_r, b_f, b_r):
        @pl.when(pl.program_id(0) == 0)
        def _():
            st_f[...] = jnp.zeros_like(st_f)
            st_r[...] = jnp.zeros_like(st_r)

        for _ in zip(_gla_fwd_block(qf, kf, vf, lrf, wf, bf, of_ref, sdf_ref, st_f, b_f, False, tb),
                     _gla_fwd_block(qr, kr, vr, lrr, wr, br, or_ref, sdr_ref, st_r, b_r, True, tb)):
            pass

    full = lambda i: (0, 0)
    zspecs = [s for jm in jmaps for s in (_zspec(tb, GK, ZB_Q, jm), _zspec(tb, GK, ZB_K, jm),
                                         _zspec(tb, GV, ZB_V, jm), _zspec(tb, 128, ZB_LR, jm))]
    wspecs = [pl.BlockSpec((128, GK), full), pl.BlockSpec((1, GK), full)] * 2
    out_specs = [s for jm in jmaps for s in (pl.BlockSpec((tb, GV), lambda i, jm=jm: (jm(i), 0)),
                                             pl.BlockSpec((nb, 128, GK), lambda i, jm=jm: (jm(i), 0, 0)))]
    out_shape = [jax.ShapeDtypeStruct((t, GV), F32), jax.ShapeDtypeStruct((t // CH, 128, GK), F32)] * 2
    scratch = [pltpu.VMEM((128, GK), F32), pltpu.VMEM((128, GK), F32), pltpu.VMEM((tb, GK), F32),
               pltpu.VMEM((tb, GK), F32)]
    return _call(body, "gla_fwd", (nblk,), zspecs + wspecs, out_specs, out_shape, scratch,
                 [z] * 8 + [waf_pad, b_af, wab_pad, b_ab], riders)


def _gla_bwd_chunks(do_ref, sd_ref, dst, b_scr, db_scr, dq_ref, dk_ref, dv_ref, qt, kt, e, ei, v, rev, nb):
    maskw = _wide_mask(rev)
    for c in (range(nb) if rev else reversed(range(nb))):
        sl = slice(c * CH, (c + 1) * CH)
        grow = c * CH + (0 if rev else CH - 1)
        gdec = jnp.exp(b_scr[pl.ds(grow, 1), :])
        qt_c, kt_c, v_c, do_c = qt[sl], kt[sl], v[sl], do_ref[pl.ds(c * CH, CH), :]
        s_in, ds_out = sd_ref[c], dst[...]
        kbd, vbd = _stack_heads(kt_c, 6), _stack_heads(v_c, 7)
        a = jnp.where(maskw, _dot_nt(qt_c, kbd), 0.0)
        da = jnp.where(maskw, _dot_nt(do_c, vbd), 0.0)
        dv_ref[pl.ds(c * CH, CH), :] = (_fold_heads(_dot_tn(a, do_c), 7)
                                        + _lanes_by_head(_dot_nt(_stack_heads(kt_c * gdec, 6), ds_out)))
        dqt = _dot(da, kbd) + _fold_heads(_dot(_rows_by_head(do_c), s_in), 6)
        dkh = _fold_heads(_dot(_rows_by_head(v_c), ds_out), 6)
        da_do = jnp.concatenate([da.astype(BF16), do_c.astype(BF16)], axis=1)
        both = _dot_tn(da_do, qt_c)
        dkt = _fold_heads(both[:NH * CH], 6) + dkh * gdec
        dg = jnp.sum(ds_out * s_in, axis=0, keepdims=True) + jnp.sum(kt_c * dkh, axis=0, keepdims=True)
        db_scr[pl.ds(c * CH, CH), :] = dqt * qt_c - dkt * kt_c
        db_scr[pl.ds(grow, 1), :] += dg * gdec
        dq_ref[pl.ds(c * CH, CH), :] = dqt * e[sl] * 0.125
        dk_ref[pl.ds(c * CH, CH), :] = dkt * ei[sl]
        dst[...] = ds_out * gdec + _state_compact(both[NH * CH:])
        yield


def _gate_bwd(db, pre, lr, wpad, rev, tb):
    dla = _chunk_cumsum(db, not rev, 2)
    dpre = dla * (1.0 / 16.0) / (1.0 + jnp.exp(pre))
    return dpre, _dot_nt(dpre, wpad), _dot_tn(lr, dpre)


def _gla_bwd_first(z, dy, o_pre, sd, wpad, bias, conv_w, conv_norm, gla_norm4, riders=()):
    t = z.shape[0]
    tb = min(TB_BWD, t)
    nblk, nb = t // tb, tb // CH
    jmap = lambda i: nblk - 1 - i

    def body(q_ref, k_ref, v_ref, lr_ref, g_ref, cb_ref, cc_ref, cu_ref, ccp_ref, ccn_ref, cup_ref, cun_ref,
             dy_ref, opre_ref, sd_ref, w_ref, bias_ref, cw_ref, cn_ref, gn_ref,
             do_ref, dq_ref, dk_ref, dv_ref, dlr_ref, dzg_ref, dzcb_ref, dconv_ref,
             dw_ref, dbias_ref, dcw_ref, dcn_ref, dgn_ref, dst, b_scr, db_scr):
        i = pl.program_id(0)
        j = jmap(i)

        @pl.when(i == 0)
        def _():
            dst[...] = jnp.zeros_like(dst)
            for ref in (dw_ref, dbias_ref, dcw_ref, dcn_ref, dgn_ref):
                ref[...] = jnp.zeros_like(ref)

        dyg = dy_ref[:, CW:]
        g = g_ref[...]
        sig = _sigmoid(g)
        on, rr = _head_norm(opre_ref[...])
        gn = gn_ref[...]
        dzg_ref[...] = (dyg * on * gn * (sig * (1.0 + g * (1.0 - sig)))).astype(BF16)
        don = dyg * (g * sig)
        _acc_rows(dgn_ref, jnp.sum(don * on, axis=0, keepdims=True))
        u = don * gn
        uo = u * on
        mean_uo = jnp.concatenate(
            [jnp.broadcast_to(jnp.mean(uo[:, h * 128:(h + 1) * 128], axis=-1, keepdims=True), (tb, 128))
             for h in range(NH)], axis=1)
        do_ref[...] = rr * (u - on * mean_uo)

        def conv_branch():
            cb = cb_ref[...]
            h, h_m1, h_p1, conv = _conv_parts(cb, cc_ref[...], cu_ref[...], ccp_ref[pl.ds(7, 1), :],
                                              cup_ref[pl.ds(7, 1), :], ccn_ref[pl.ds(0, 1), :],
                                              cun_ref[pl.ds(0, 1), :], cw_ref, j == 0, j == nblk - 1, tb)
            yc = cb * conv
            yield
            rc = lax.rsqrt(_group_sum(yc * yc) * (1.0 / 64.0) + EPS)
            ycr = yc * rc
            yield
            dyn = dy_ref[:, :CW]
            _acc_rows(dcn_ref, jnp.sum(dyn * ycr, axis=0, keepdims=True))
            uc = dyn * cn_ref[...]
            yield
            dyc = rc * (uc - ycr * (_group_sum(uc * ycr) * (1.0 / 64.0)))
            dzcb_ref[...] = (dyc * conv).astype(BF16)
            yield
            dconv = dyc * cb
            dconv_ref[...] = dconv
            yield
            dcw_ref[pl.ds(0, 1), :] += jnp.sum(dconv * h_m1, axis=0, keepdims=True)
            dcw_ref[pl.ds(1, 1), :] += jnp.sum(dconv * h, axis=0, keepdims=True)
            dcw_ref[pl.ds(2, 1), :] += jnp.sum(dconv * h_p1, axis=0, keepdims=True)
            yield

        lr, wp = lr_ref[...], w_ref[...]
        pre, b, e, ei, qt, kt = _gla_recompute(q_ref[...], k_ref[...], lr, wp, bias_ref[...], False, tb)
        b_scr[...] = b
        for _ in itertools.zip_longest(
                _gla_bwd_chunks(do_ref, sd_ref, dst, b_scr, db_scr, dq_ref, dk_ref, dv_ref, qt, kt, e, ei, v_ref[...],
                                False, nb), conv_branch()):
            pass
        dpre, dlr, dw = _gate_bwd(db_scr[...], pre, lr, wp, False, tb)
        dlr_ref[...] = dlr
        dw_ref[...] += dw
        _acc_rows(dbias_ref, jnp.sum(dpre, axis=0, keepdims=True))

    full = lambda i: (0, 0)
    tokv = pl.BlockSpec((tb, GV), lambda i: (jmap(i), 0))
    tokk = pl.BlockSpec((tb, GK), lambda i: (jmap(i), 0))
    ccp, ccn = _halo_specs(tb, nblk, t, ZB_CC, jmap)
    cup, cun = _halo_specs(tb, nblk, t, ZB_CU, jmap)
    in_specs = [_zspec(tb, GK, ZB_Q, jmap), _zspec(tb, GK, ZB_K, jmap), _zspec(tb, GV, ZB_V, jmap),
                _zspec(tb, 128, ZB_LR, jmap), _zspec(tb, GV, ZB_G, jmap), _zspec(tb, CW, ZB_CB, jmap),
                _zspec(tb, CW, ZB_CC, jmap), _zspec(tb, CW, ZB_CU, jmap), ccp, ccn, cup, cun,
                pl.BlockSpec((tb, D), lambda i: (jmap(i), 0)), tokv,
                pl.BlockSpec((nb, 128, GK), lambda i: (jmap(i), 0, 0)), pl.BlockSpec((128, GK), full),
                pl.BlockSpec((1, GK), full), pl.BlockSpec((3, CW), full), pl.BlockSpec((1, CW), full),
                pl.BlockSpec((1, GV), full)]
    out_specs = [tokv, tokk, tokk, tokv, pl.BlockSpec((tb, 128), lambda i: (jmap(i), 0)), tokv, tokv, tokv,
                 pl.BlockSpec((128, GK), full), pl.BlockSpec((8, GK), full), pl.BlockSpec((8, CW), full),
                 pl.BlockSpec((8, CW), full), pl.BlockSpec((8, GV), full)]
    out_shape = [jax.ShapeDtypeStruct((t, GV), F32), jax.ShapeDtypeStruct((t, GK), F32),
                 jax.ShapeDtypeStruct((t, GK), F32), jax.ShapeDtypeStruct((t, GV), F32),
                 jax.ShapeDtypeStruct((t, 128), F32), jax.ShapeDtypeStruct((t, GV), BF16),
                 jax.ShapeDtypeStruct((t, CW), BF16), jax.ShapeDtypeStruct((t, CW), F32),
                 jax.ShapeDtypeStruct((128, GK), F32), jax.ShapeDtypeStruct((8, GK), F32),
                 jax.ShapeDtypeStruct((8, CW), F32), jax.ShapeDtypeStruct((8, CW), F32),
                 jax.ShapeDtypeStruct((8, GV), F32)]
    return _call(
        body, "gla_bwd_first", (nblk,), in_specs, out_specs, out_shape,
        [pltpu.VMEM((128, GK), F32), pltpu.VMEM((tb, GK), F32), pltpu.VMEM((tb, GK), F32)],
        (z, z, z, z, z, z, z, z, z, z, z, z, dy, o_pre, sd, wpad, bias, conv_w, conv_norm, gla_norm4), riders)


def _gla_bwd_second(z, do, sd, wpad, bias, dqa, dka, dva, dlra, dzg, dzcb, dconv, conv_w, riders=()):
    t = z.shape[0]
    tb = min(TB_BWD, t)
    nblk, nb = t // tb, tb // CH
    jmap = lambda i: i

    def body(q_ref, k_ref, v_ref, lr_ref, cc_ref, cu_ref, do_ref, sd_ref, w_ref, bias_ref, dqa_ref, dka_ref,
             dva_ref, dlra_ref, dzg_ref, dzcb_ref, dc_ref, dcp_ref, dcn_ref, cw_ref,
             dz_ref, dw_ref, dbias_ref, dst, b_scr, db_scr, dq_scr, dk_scr, dv_scr, sb_scr, dsk_scr):
        i = pl.program_id(0)

        @pl.when(i == 0)
        def _():
            dst[...] = jnp.zeros_like(dst)
            dw_ref[...] = jnp.zeros_like(dw_ref)
            dbias_ref[...] = jnp.zeros_like(dbias_ref)

        q_raw, k, v, lr, wp = q_ref[...], k_ref[...], v_ref[...], lr_ref[...], w_ref[...]
        pre, b, e, ei, qt, kt = _gla_recompute(q_raw, k, lr, wp, bias_ref[...], True, tb)
        b_scr[...] = b

        def token_local():
            dc = dc_ref[...]
            rows = _iota(dc.shape, 0)
            dprev = jnp.where(i == 0, 0.0, dcp_ref[pl.ds(7, 1), :])
            dnext = jnp.where(i == nblk - 1, 0.0, dcn_ref[pl.ds(0, 1), :])
            dc_m1 = jnp.where(rows == 0, dprev, pltpu.roll(dc, 1, 0))
            dc_p1 = jnp.where(rows == tb - 1, dnext, pltpu.roll(dc, tb - 1, 0))
            yield
            dh = cw_ref[pl.ds(0, 1), :] * dc_p1 + cw_ref[pl.ds(1, 1), :] * dc + cw_ref[pl.ds(2, 1), :] * dc_m1
            dz_ref[:, 0:512] = dzcb_ref[...]
            yield
            dz_ref[:, 512:1024] = (dh * cu_ref[...]).astype(BF16)
            dz_ref[:, 1024:1536] = (dh * cc_ref[...]).astype(BF16)
            dz_ref[:, 2560:3072] = dzg_ref[...]
            yield
            sb_scr[...] = _head_sum((q_raw * 0.125) * k, 64, 128)
            yield
            dsk_scr[...] = _head_sum(do_ref[...] * v, 128, 64)
            yield

        for _ in itertools.zip_longest(
                _gla_bwd_chunks(do_ref, sd_ref, dst, b_scr, db_scr, dq_scr, dk_scr, dv_scr, qt, kt, e, ei, v, True, nb),
                token_local()):
            pass
        dpre, dlr, dw = _gate_bwd(db_scr[...], pre, lr, wp, True, tb)
        dw_ref[...] += dw
        _acc_rows(dbias_ref, jnp.sum(dpre, axis=0, keepdims=True))
        dsk = dsk_scr[...]
        dz_ref[:, 1536:1792] = (dqa_ref[...] + dq_scr[...] - dsk * k * 0.125).astype(BF16)
        dz_ref[:, 1792:2048] = (dka_ref[...] + dk_scr[...] - dsk * (q_raw * 0.125)).astype(BF16)
        dz_ref[:, 2048:2560] = (dva_ref[...] + dv_scr[...] - sb_scr[...] * do_ref[...]).astype(BF16)
        dz_ref[:, 3072:3200] = (dlra_ref[...] + dlr).astype(BF16)

    full = lambda i: (0, 0)
    tokv = pl.BlockSpec((tb, GV), lambda i: (i, 0))
    tokk = pl.BlockSpec((tb, GK), lambda i: (i, 0))
    dcp = pl.BlockSpec((8, CW), lambda i: (jnp.maximum(i * (tb // 8) - 1, 0), 0))
    dcn = pl.BlockSpec((8, CW), lambda i: (jnp.minimum((i + 1) * (tb // 8), t // 8 - 1), 0))
    in_specs = [_zspec(tb, GK, ZB_Q, jmap), _zspec(tb, GK, ZB_K, jmap), _zspec(tb, GV, ZB_V, jmap),
                _zspec(tb, 128, ZB_LR, jmap), _zspec(tb, CW, ZB_CC, jmap), _zspec(tb, CW, ZB_CU, jmap), tokv,
                pl.BlockSpec((nb, 128, GK), lambda i: (i, 0, 0)), pl.BlockSpec((128, GK), full),
                pl.BlockSpec((1, GK), full), tokk, tokk, tokv, pl.BlockSpec((tb, 128), lambda i: (i, 0)), tokv, tokv,
                tokv, dcp, dcn, pl.BlockSpec((3, CW), full)]
    out_specs = [pl.BlockSpec((tb, ZC), lambda i: (i, 0)), pl.BlockSpec((128, GK), full), pl.BlockSpec((8, GK), full)]
    out_shape = [jax.ShapeDtypeStruct((t, ZC), BF16), jax.ShapeDtypeStruct((128, GK), F32),
                 jax.ShapeDtypeStruct((8, GK), F32)]
    return _call(
        body, "gla_bwd_second", (nblk,), in_specs, out_specs, out_shape,
        [pltpu.VMEM((128, GK), F32), pltpu.VMEM((tb, GK), F32), pltpu.VMEM((tb, GK), F32),
         pltpu.VMEM((tb, GK), F32), pltpu.VMEM((tb, GK), F32), pltpu.VMEM((tb, GV), F32),
         pltpu.VMEM((tb, GV), F32), pltpu.VMEM((tb, GK), F32)],
        (z, z, z, z, z, z, do, sd, wpad, bias, dqa, dka, dva, dlra, dzg, dzcb, dconv, dconv, dconv, conv_w), riders)


def _step(x, mem, target, shard, small_pack, vec, place):
    own, from_chips = {}, {}

    def pair_sums(names, g4, from_sibling):
        pbs = {}
        for shape in dict.fromkeys(g.shape for g in g4):
            idx = [i for i, g in enumerate(g4) if g.shape == shape]
            pb, mine = _rs_pair_sum(place, [g4[i] for i in idx], [from_sibling[i] for i in idx],
                                    "pair_sum_" + "_".join(names[i] for i in idx))
            for i, b, o in zip(idx, pb, mine):
                pbs[i], own[names[i]] = b, o
        return [pbs[i] for i in range(len(g4))]

    def by_dest(g, n):
        return g.reshape((4, 2) + shard[n].shape)

    w_in, small_all = _exchange(_gather_rider([shard["w_in"], small_pack]), "gather_w_in")
    w_in = jnp.pad(w_in.reshape(ZW, D), ((0, ZC - ZW), (0, 0)))
    small_all = small_all.reshape(NDEV, -1)
    p, off = {}, 0
    for n, (r, c) in SMALL_SHARDED.items():
        p[n] = small_all[:, off:off + r * c].reshape(NDEV, r, c).transpose(1, 0, 2).reshape(r, NDEV * c)
        off += r * c
    zeros_lr = jnp.zeros((128 - LR, GK), BF16)
    waf_pad = jnp.concatenate([p["w_af"].astype(BF16), zeros_lr], axis=0)
    wab_pad = jnp.concatenate([jnp.zeros((LR, GK), BF16), p["w_ab"].astype(BF16), zeros_lr[:128 - 2 * LR]], axis=0)
    gla_norm4 = jnp.tile(vec["gla_norm"], (1, NH))

    z, hb, w_out, w_xq, w_xo, w_xkv = _inproj(
        x, vec["mix_norm"], w_in, [_gather_rider([shard[n] for n in ("w_out", "w_xq", "w_xo", "w_xkv")])])
    w_out, w_xq, w_xo = [a.reshape(D, D) for a in (w_out, w_xq, w_xo)]
    o_f, sd_f, o_b, sd_b, w_up_t, w_down = _gla_fwd(
        z, waf_pad, vec["b_af"], wab_pad, vec["b_ab"],
        [_gather_rider([shard["w_up"], shard["w_down"]], early_relay=False)])
    w_up_t, w_down = w_up_t.reshape(FF, D), w_down.reshape(FF, D)
    kv, memn = _kv_proj(mem, vec["mem_norm"], w_xkv)
    kb, vb = kv[:, :D].astype(BF16), kv[:, D:].astype(BF16)
    x1, x2, xn1, qb, attb, yb, o_pre = _attn_fwd(x, z, o_f, o_b, p["conv_w"], vec["conv_norm"], gla_norm4, w_out,
                                                 vec["xa_norm"], w_xq, kb, vb, w_xo)
    h1b, xn2, dx3, dx3b, loss8, dfinal = _mlp_fwd(x2, vec["mlp_norm"], w_up_t, w_down, vec["final_norm"], target)

    ab, dh1b, dx2, dx2b, dmlp = _mlp_bwd(dx3, dx3b, h1b, w_down, w_up_t, x2, vec["mlp_norm"])
    g_mlp = [by_dest(_matmul_tn(ab, dx3b, "dw_down")[0], "w_down"),
             by_dest(_matmul_tn(dh1b, xn2, "dw_up")[0], "w_up")]
    dx1, dx1b, dy, dqb, dkv, dxa, *s_mlp = _attn_bwd(x1, dx2, dx2b, qb, kb, vb, w_xo, w_xq, w_out, vec["xa_norm"],
                                                     riders=[_sibling_rider(g_mlp)])
    pb_mlp = pair_sums(("w_down", "w_up"), g_mlp, s_mlp)
    dw_xo = _matmul_tn(attb, dx2b, "dw_xo")[0]
    dw_xkv, dmemn = _kv_bwd(dkv, memn, mem, vec["mem_norm"], w_xkv)
    att_names = ("w_xo", "w_xq", "w_out", "w_xkv")
    g_att = [by_dest(g, n) for g, n in zip(
        (dw_xo, _matmul_tn(xn1, dqb, "dw_xq")[0], _matmul_tn(yb, dx1b, "dw_out")[0], dw_xkv), att_names)]
    res = _gla_bwd_first(z, dy, o_pre, sd_f, waf_pad, vec["b_af"], p["conv_w"], vec["conv_norm"], gla_norm4,
                         riders=[_chips_rider(pb_mlp), _sibling_rider(g_att)])
    do, dqa, dka, dva, dlra, dzg, dzcb, dconv, dwaf, dbaf, dcw, dcn, dgn = res[:13]
    from_chips["w_down"], from_chips["w_up"] = res[13:15]
    pb_att = pair_sums(att_names, g_att, res[15:])
    dz, dwab, dbab, *c_att = _gla_bwd_second(z, do, sd_b, wab_pad, vec["b_ab"], dqa, dka, dva, dlra, dzg, dzcb, dconv,
                                             p["conv_w"], riders=[_chips_rider(pb_att)])
    from_chips.update(zip(att_names, c_att))
    g_in = [by_dest(_matmul_tn(dz, hb, "dw_in", rows=ZW)[0], "w_in")]
    pb_in = pair_sums(("w_in",), g_in, _exchange(_sibling_rider(g_in), "grads_to_sibling_w_in"))
    grad_x, dmix, from_chips["w_in"] = _inproj_bwd(dz, w_in, x, dx1, vec["mix_norm"], riders=[_chips_rider(pb_in)])

    small_acc = dict(mix_norm=dmix, conv_w=dcw, conv_norm=dcn, w_af=dwaf, b_af=dbaf, w_ab=dwab, b_ab=dbab,
                     gla_norm=dgn, xa_norm=dxa, mem_norm=dmemn, mlp_norm=dmlp, final_norm=dfinal)
    return loss8, grad_x, small_acc, own, from_chips


def _place():
    return lax.axis_index("x"), lax.axis_index("y"), lax.axis_index("c")


class _Rider:
    def __init__(self, arrays, out_shape, scratch, start, finish, relay=None):
        self.arrays, self.out_shape, self.scratch, self.start, self.finish = arrays, out_shape, scratch, start, finish
        self.relay = relay


def _gather_rider(blks, early_relay=True):
    n = len(blks)

    def plan(in_refs, out_refs, sems):
        send_sems, recv_sems, local_sems = sems
        x, y, c = _place()
        me, sibling = (x, y, c), (x, y, 1 - c)
        chips = [(1 - x, y, c), (x, 1 - y, c), (1 - x, 1 - y, c)]

        def copy(a, k, block, to, own=False):
            px, py, pc = block
            dst = out_refs[a].at[4 * px + 2 * py + pc]
            return pltpu.make_async_remote_copy(
                src_ref=in_refs[a] if own else dst, dst_ref=dst, send_sem=send_sems.at[k, a],
                recv_sem=recv_sems.at[k, a], device_id=to, device_id_type=MESH)

        def local(a):
            return pltpu.make_async_copy(in_refs[a], out_refs[a].at[4 * x + 2 * y + c], local_sems.at[a])

        def own_sends(a):
            return [copy(a, 0, me, sibling, own=True)] + [copy(a, 1 + j, me, chip, own=True)
                                                          for j, chip in enumerate(chips)]

        return copy, local, own_sends, me, sibling, chips

    def start(in_refs, out_refs, sems):
        _, local, own_sends, _, _, _ = plan(in_refs, out_refs, sems)
        for a in range(n):
            local(a).start()
            for cp in own_sends(a):
                cp.start()

    def relay(in_refs, out_refs, sems):
        copy, _, _, me, sibling, chips = plan(in_refs, out_refs, sems)
        for j, chip in enumerate(chips):
            for a in range(n):
                copy(a, 1 + j, chip, me).wait_recv()
                copy(a, 4 + j, chip, sibling).start()

    def finish(in_refs, out_refs, sems):
        if not early_relay:
            relay(in_refs, out_refs, sems)
        copy, local, own_sends, me, sibling, chips = plan(in_refs, out_refs, sems)
        for a in range(n):
            copy(a, 0, sibling, me).wait_recv()
            for j, (px, py, pc) in enumerate(chips):
                copy(a, 4 + j, (px, py, 1 - pc), me).wait_recv()
            for cp in own_sends(a) + [copy(a, 4 + j, chip, sibling) for j, chip in enumerate(chips)]:
                cp.wait_send()
            local(a).wait()

    return _Rider(blks, [jax.ShapeDtypeStruct((NDEV,) + b.shape, b.dtype) for b in blks],
                  [pltpu.SemaphoreType.DMA((7, n)), pltpu.SemaphoreType.DMA((7, n)), pltpu.SemaphoreType.DMA((n,))],
                  start, finish, relay if early_relay else None)


def _sibling_rider(g4s):
    n = len(g4s)

    def copies(in_refs, out_refs, sems):
        send_sems, recv_sems = sems
        x, y, c = _place()
        return [pltpu.make_async_remote_copy(
            src_ref=in_refs[a].at[k, 1 - c], dst_ref=out_refs[a].at[k], send_sem=send_sems.at[k, a],
            recv_sem=recv_sems.at[k, a], device_id=(x, y, 1 - c), device_id_type=MESH)
            for a in range(n) for k in range(4)]

    def start(in_refs, out_refs, sems):
        for cp in copies(in_refs, out_refs, sems):
            cp.start()

    def finish(in_refs, out_refs, sems):
        for cp in copies(in_refs, out_refs, sems):
            cp.wait()

    return _Rider(g4s, [jax.ShapeDtypeStruct((4,) + g.shape[2:], g.dtype) for g in g4s],
                  [pltpu.SemaphoreType.DMA((4, n)), pltpu.SemaphoreType.DMA((4, n))], start, finish)


def _chips_rider(pbs):
    n = len(pbs)

    def copies(in_refs, out_refs, sems):
        send_sems, recv_sems = sems
        x, y, c = _place()
        peers = [(1 - x, y), (x, 1 - y), (1 - x, 1 - y)]
        return [pltpu.make_async_remote_copy(
            src_ref=in_refs[a].at[2 * px + py], dst_ref=out_refs[a].at[k], send_sem=send_sems.at[k, a],
            recv_sem=recv_sems.at[k, a], device_id=(px, py, c), device_id_type=MESH)
            for a in range(n) for k, (px, py) in enumerate(peers)]

    def start(in_refs, out_refs, sems):
        for cp in copies(in_refs, out_refs, sems):
            cp.start()

    def finish(in_refs, out_refs, sems):
        for cp in copies(in_refs, out_refs, sems):
            cp.wait()

    return _Rider(pbs, [jax.ShapeDtypeStruct((3,) + p.shape[1:], p.dtype) for p in pbs],
                  [pltpu.SemaphoreType.DMA((3, n)), pltpu.SemaphoreType.DMA((3, n))], start, finish)


def _exchange(rider, name):
    n_in, n_out = len(rider.arrays), len(rider.out_shape)

    def body(*refs):
        ins, outs, sems = refs[:n_in], refs[n_in:n_in + n_out], refs[n_in + n_out:]
        rider.start(ins, outs, sems)
        if rider.relay:
            rider.relay(ins, outs, sems)
        rider.finish(ins, outs, sems)

    hbm = pl.BlockSpec(memory_space=pltpu.HBM)
    return pl.pallas_call(body, name=name, out_shape=rider.out_shape, in_specs=[hbm] * n_in,
                          out_specs=[hbm] * n_out, scratch_shapes=rider.scratch)(*rider.arrays)


def _rs_pair_sum(place, g4s, r1s, name):
    n = len(g4s)
    rows, cols = g4s[0].shape[2:]
    tr = min(rows, 512)

    def body(pl_ref, *refs):
        for g_ref, r_ref, pb_ref, own_ref in zip(refs[:n], refs[n:2 * n], refs[2 * n:3 * n], refs[3 * n:]):
            s = g_ref[0, 0] + r_ref[0]
            pb_ref[0] = s.astype(BF16)

            @pl.when(pl.program_id(1) == pl_ref[0])
            def _():
                own_ref[...] = s

    grid_spec = pltpu.PrefetchScalarGridSpec(
        num_scalar_prefetch=1, grid=(rows // tr, 4),
        in_specs=[pl.BlockSpec((1, 1, tr, cols), lambda r, k, p: (k, p[1], r, 0))] * n
        + [pl.BlockSpec((1, tr, cols), lambda r, k, p: (k, r, 0))] * n,
        out_specs=[pl.BlockSpec((1, tr, cols), lambda r, k, p: (k, r, 0))] * n
        + [pl.BlockSpec((tr, cols), lambda r, k, p: (r, 0))] * n)
    res = pl.pallas_call(
        body, name=name, grid_spec=grid_spec,
        out_shape=[jax.ShapeDtypeStruct((4, rows, cols), BF16)] * n + [jax.ShapeDtypeStruct((rows, cols), F32)] * n,
        compiler_params=_cparams(("arbitrary", "arbitrary")))(place, *g4s, *r1s)
    return res[:n], res[n:]


PACK_ROWS = 32
VEC_ROW = {"mix_norm": 0, "conv_norm": 1, "b_af": 2, "b_ab": 3, "gla_norm": 4, "xa_norm": 5, "mem_norm": 6,
           "mlp_norm": 7, "final_norm": 8}
LOSS_ROW, MAT_ROW = 9, 16
MAT_LANE = {"w_af": 0, "w_ab": GK, "conv_w": 2 * GK}
MAT_SRC_ROW = {"w_af": 0, "w_ab": LR, "conv_w": 0}


SMALL_WIDTH = {"mix_norm": D, "conv_w": 64, "conv_norm": CW, "w_af": 32, "b_af": GK, "w_ab": 32, "b_ab": GK,
               "gla_norm": 128, "xa_norm": D, "mem_norm": D, "mlp_norm": D, "final_norm": D}


def _small_reduce(acc, loss8):
    names = list(SMALL)
    n = len(names)
    widths = SMALL_WIDTH

    def body(*refs):
        acc_refs = dict(zip(names, refs[:n]))
        loss_ref, tot = refs[n], refs[n + 1]
        pk, all_ref, send_sems, recv_sems, local_sem = refs[n + 2:]

        pk[...] = jnp.zeros_like(pk)
        for k, row in VEC_ROW.items():
            if k == "gla_norm":
                g = functools.reduce(lambda a, b: a + b, [acc_refs[k][pl.ds(0, 1), pl.ds(h * 128, 128)]
                                                          for h in range(NH)])
            else:
                g = acc_refs[k][pl.ds(0, 1), :]
            pk[pl.ds(row, 1), pl.ds(0, widths[k])] = g
        pk[pl.ds(LOSS_ROW, 1), pl.ds(0, 128)] = loss_ref[pl.ds(0, 1), :]
        for k, lane in MAT_LANE.items():
            rows, cols = (3, CW) if k == "conv_w" else (LR, GK)
            pk[pl.ds(MAT_ROW, rows), pl.ds(lane, cols)] = acc_refs[k][pl.ds(MAT_SRC_ROW[k], rows), :]

        x, y, c = _place()
        me, sibling = (x, y, c), (x, y, 1 - c)
        chips = [(1 - x, y, c), (x, 1 - y, c), (1 - x, 1 - y, c)]

        def copy(k, block, to, own=False):
            px, py, pc = block
            dst = all_ref.at[4 * px + 2 * py + pc]
            return pltpu.make_async_remote_copy(
                src_ref=pk if own else dst, dst_ref=dst, send_sem=send_sems.at[k], recv_sem=recv_sems.at[k],
                device_id=to, device_id_type=MESH)

        mine = pltpu.make_async_copy(pk, all_ref.at[4 * x + 2 * y + c], local_sem)
        mine.start()
        first = [copy(0, me, sibling, own=True)] + [copy(1 + j, me, chip, own=True) for j, chip in enumerate(chips)]
        for cp in first:
            cp.start()
        passed = [copy(4 + j, chip, sibling) for j, chip in enumerate(chips)]
        for j, chip in enumerate(chips):
            copy(1 + j, chip, me).wait_recv()
            passed[j].start()
        copy(0, sibling, me).wait_recv()
        for j, (px, py, pc) in enumerate(chips):
            copy(4 + j, (px, py, 1 - pc), me).wait_recv()
        for cp in first + passed:
            cp.wait_send()
        mine.wait()
        total = all_ref[0]
        for d in range(1, NDEV):
            total = total + all_ref[d]
        tot[...] = total

    return pl.pallas_call(
        body, name="small_reduce", out_shape=jax.ShapeDtypeStruct((PACK_ROWS, D), F32),
        scratch_shapes=[pltpu.VMEM((PACK_ROWS, D), F32), pltpu.VMEM((NDEV, PACK_ROWS, D), F32),
                        pltpu.SemaphoreType.DMA((7,)), pltpu.SemaphoreType.DMA((7,)), pltpu.SemaphoreType.DMA],
    )(*[acc[k] for k in names], loss8)


def _small_adamw(tot, ws, ms, vs):
    names = list(SMALL)
    n = len(names)
    widths = SMALL_WIDTH

    def body(*refs):
        tot = refs[0]
        w_refs, m_refs, v_refs = [dict(zip(names, refs[1 + q * n:1 + (q + 1) * n])) for q in range(3)]
        outs = refs[1 + 3 * n:1 + 7 * n]
        g_out, d_out, m_out, v_out = [dict(zip(names, outs[q * n:(q + 1) * n])) for q in range(4)]
        cut = refs[1 + 7 * n]
        x, y, c = _place()
        dev = 4 * x + 2 * y + c
        for k in names:
            if k in VEC_ROW:
                g = tot[pl.ds(VEC_ROW[k], 1), pl.ds(0, widths[k])]
            else:
                rows, cols = (3, CW) if k == "conv_w" else (LR, GK)
                wd = widths[k]
                sel = jnp.where(_iota((cols, wd), 0) == dev * wd + _iota((cols, wd), 1), 1.0, 0.0).astype(BF16)
                cut[:, pl.ds(0, wd)] = _dot_exact_rhs(tot[pl.ds(MAT_ROW, LR), pl.ds(MAT_LANE[k], cols)], sel, 3)
                g = cut[pl.ds(0, rows), pl.ds(0, wd)]
            g_out[k][...] = g
            d_out[k][...], m_out[k][...], v_out[k][...] = _adamw_math(w_refs[k][...], g, m_refs[k][...],
                                                                       v_refs[k][...])

    shapes = [jax.ShapeDtypeStruct(ws[k].shape, F32) for k in names]
    res = pl.pallas_call(
        body, name="small_adamw", out_shape=shapes * 4, scratch_shapes=[pltpu.VMEM((LR, 128), F32)],
    )(tot, *[ws[k] for k in names], *[ms[k] for k in names], *[vs[k] for k in names])
    return {k: tuple(res[q * n + i] for q in range(4)) for i, k in enumerate(names)}


def _adamw_math(w, g, m, v):
    m = ADAM_B1 * m + (1.0 - ADAM_B1) * g
    v = ADAM_B2 * v + (1.0 - ADAM_B2) * (g * g)
    m_hat = m / (1.0 - ADAM_B1 ** ADAM_STEP)
    v_hat = v / (1.0 - ADAM_B2 ** ADAM_STEP)
    delta = -ADAM_LR * (m_hat / (jnp.sqrt(v_hat) + ADAM_EPS) + ADAM_WD * w)
    return delta, m, v


def _adamw(ws, ms, vs, owns, r2s, name, grads_transposed=False):
    n = len(ws)
    _, r, c = ws[0].shape
    tr = 256 if r % 256 == 0 else r

    def body(*refs):
        ins, outs = refs[:5 * n], refs[5 * n:]
        for q in range(n):
            w_ref, m_ref, v_ref, o_ref, r_ref = [ins[k * n + q] for k in range(5)]
            g_ref, d_ref, nm_ref, nv_ref = [outs[k * n + q] for k in range(4)]
            g = ((o_ref[...] + r_ref[0].astype(F32)) + r_ref[1].astype(F32)) + r_ref[2].astype(F32)
            g = g.T if grads_transposed else g
            g_ref[...] = g
            d_ref[...], nm_ref[...], nv_ref[...] = _adamw_math(w_ref[...], g, m_ref[...], v_ref[...])

    spec = pl.BlockSpec((None, tr, c), lambda i: (0, i, 0))
    if grads_transposed:
        own_spec, r2_spec = pl.BlockSpec((c, tr), lambda i: (0, i)), pl.BlockSpec((3, c, tr), lambda i: (0, 0, i))
    else:
        own_spec, r2_spec = pl.BlockSpec((tr, c), lambda i: (i, 0)), pl.BlockSpec((3, tr, c), lambda i: (0, i, 0))
    res = pl.pallas_call(
        body, name=name, grid=(r // tr,),
        in_specs=[spec] * (3 * n) + [own_spec] * n + [r2_spec] * n,
        out_specs=[spec] * (4 * n), out_shape=[jax.ShapeDtypeStruct((1, r, c), F32)] * (4 * n),
        compiler_params=_cparams(("arbitrary",)))(*ws, *ms, *vs, *owns, *r2s)
    return [tuple(res[k * n + q] for k in range(4)) for q in range(n)]


MATS = ("w_in", "w_out", "w_xq", "w_xo", "w_xkv", "w_up", "w_down")
SMALL = ("mix_norm", "conv_w", "conv_norm", "w_af", "b_af", "w_ab", "b_ab", "gla_norm", "xa_norm", "mem_norm",
         "mlp_norm", "final_norm")
WEIGHTS = ("mix_norm", "w_in", "conv_w", "conv_norm", "w_af", "b_af", "w_ab", "b_ab", "gla_norm", "w_out", "xa_norm",
           "mem_norm", "w_xq", "w_xkv", "w_xo", "mlp_norm", "w_up", "w_down", "final_norm")
SMALL_SHARDED = {"conv_w": (3, 64), "w_af": (LR, 32), "w_ab": (LR, 32)}
SMALL_PACK_ROWS = 16


def kernel(x, mem, mix_norm, w_in, conv_w, conv_norm, w_af, b_af, w_ab, b_ab, gla_norm, w_out, xa_norm, mem_norm, w_xq, w_xkv, w_xo, mlp_norm, w_up, w_down, final_norm, loss_target, m_mix_norm, m_w_in, m_conv_w, m_conv_norm, m_w_af, m_b_af, m_w_ab, m_b_ab, m_gla_norm, m_w_out, m_xa_norm, m_mem_norm, m_w_xq, m_w_xkv, m_w_xo, m_mlp_norm, m_w_up, m_w_down, m_final_norm, v_mix_norm, v_w_in, v_conv_w, v_conv_norm, v_w_af, v_b_af, v_w_ab, v_b_ab, v_gla_norm, v_w_out, v_xa_norm, v_mem_norm, v_w_xq, v_w_xkv, v_w_xo, v_mlp_norm, v_w_up, v_w_down, v_final_norm):
    w = dict(mix_norm=mix_norm, w_in=w_in, conv_w=conv_w, conv_norm=conv_norm, w_af=w_af, b_af=b_af, w_ab=w_ab,
             b_ab=b_ab, gla_norm=gla_norm, w_out=w_out, xa_norm=xa_norm, mem_norm=mem_norm, w_xq=w_xq, w_xkv=w_xkv,
             w_xo=w_xo, mlp_norm=mlp_norm, w_up=w_up, w_down=w_down, final_norm=final_norm)
    mom = dict(mix_norm=m_mix_norm, w_in=m_w_in, conv_w=m_conv_w, conv_norm=m_conv_norm, w_af=m_w_af, b_af=m_b_af,
               w_ab=m_w_ab, b_ab=m_b_ab, gla_norm=m_gla_norm, w_out=m_w_out, xa_norm=m_xa_norm, mem_norm=m_mem_norm,
               w_xq=m_w_xq, w_xkv=m_w_xkv, w_xo=m_w_xo, mlp_norm=m_mlp_norm, w_up=m_w_up, w_down=m_w_down,
               final_norm=m_final_norm)
    var = dict(mix_norm=v_mix_norm, w_in=v_w_in, conv_w=v_conv_w, conv_norm=v_conv_norm, w_af=v_w_af, b_af=v_b_af,
               w_ab=v_w_ab, b_ab=v_b_ab, gla_norm=v_gla_norm, w_out=v_w_out, xa_norm=v_xa_norm, mem_norm=v_mem_norm,
               w_xq=v_w_xq, w_xkv=v_w_xkv, w_xo=v_w_xo, mlp_norm=v_mlp_norm, w_up=v_w_up, w_down=v_w_down,
               final_norm=v_final_norm)
    xi, yi, ci = _place()
    two_d = lambda a: a.reshape(a.shape[-2:]) if a.ndim == 3 else a.reshape(1, a.shape[-1])

    small = jnp.concatenate([w[n].reshape(-1) for n in SMALL_SHARDED])
    small = jnp.pad(small, (0, SMALL_PACK_ROWS * 128 - small.shape[0])).reshape(SMALL_PACK_ROWS, 128)
    shard = {n: two_d(w[n]).astype(BF16) for n in MATS}
    for n in ("w_in", "w_up"):
        shard[n] = shard[n].T
    vec = {n: two_d(w[n]) for n in SMALL if n not in SMALL_SHARDED}
    place = jnp.stack([2 * xi + yi, ci]).astype(jnp.int32)
    loss8, grad_x, small_acc, own, from_chips = _step(x[0], mem[0], loss_target[0], shard, small, vec, place)

    tot = _small_reduce(small_acc, loss8)
    small_out = _small_adamw(tot, *[{n: two_d(d[n]) for n in SMALL} for d in (w, mom, var)])
    loss = tot[LOSS_ROW, 0]

    out_g, out_d, out_m, out_v = {}, {}, {}, {}
    wmv = {n: [a.transpose(0, 2, 1) if n == "w_in" else a for a in (w[n], mom[n], var[n])] for n in MATS}
    for shape in dict.fromkeys(wmv[n][0].shape for n in MATS):
        names = [n for n in MATS if wmv[n][0].shape == shape]
        res = _adamw(*[[wmv[n][k] for n in names] for k in range(3)], [own[n] for n in names],
                     [from_chips[n] for n in names], "adamw_" + "_".join(names), grads_transposed=names == ["w_up"])
        for n, r in zip(names, res):
            out_g[n], out_d[n], out_m[n], out_v[n] = [a.transpose(0, 2, 1) for a in r] if n == "w_in" else r
    for n in SMALL:
        out_g[n], out_d[n], out_m[n], out_v[n] = [a.reshape(w[n].shape) for a in small_out[n]]

    return (loss, grad_x[None], *[out_g[n] for n in WEIGHTS], *[out_d[n] for n in WEIGHTS],
            *[out_m[n] for n in WEIGHTS], *[out_v[n] for n in WEIGHTS])
```

```python
import functools
import itertools

import jax
import jax.numpy as jnp
from jax import lax
from jax.experimental import pallas as pl
from jax.experimental.pallas import tpu as pltpu

F32 = jnp.float32
BF16 = jnp.bfloat16

D = 1024
CW = 512
GK = 256
GV = 512
NH = 4
CH = 64
LR = 16
NMEM = 256
XD = 256
FF = 4096
ZW = 3104
ZC = 3200
EPS = 1e-6
NDEV = 8

ZB_CB, ZB_CC, ZB_CU, ZB_V, ZB_G = 0, 1, 2, 4, 5
ZB_Q, ZB_K = 6, 7
ZB_LR = 24

TM = 512
TM_PROJ = 1024
TM_MLP = 256
TM_MLP_FWD = 512
TF = 512
TB = 512
TB_BWD = 512
TT = 2048
VMEM_LIMIT = 56 * 1024 * 1024

ADAM_LR, ADAM_B1, ADAM_B2, ADAM_EPS, ADAM_WD, ADAM_STEP = 0.001, 0.9, 0.999, 1e-08, 0.01, 10

XKV_SHARD = 2 * D // NDEV

MESH = pl.DeviceIdType.MESH


def _cparams(sem):
    return pltpu.CompilerParams(dimension_semantics=sem, vmem_limit_bytes=VMEM_LIMIT)


def _call(body, name, grid, in_specs, out_specs, out_shape, scratch, args, riders=()):
    n_in, n_out, n_scr = len(in_specs), len(out_specs), len(scratch)
    counts = [(len(r.arrays), len(r.out_shape), len(r.scratch)) for r in riders]

    def take(refs, pos, sizes):
        groups = []
        for size in sizes:
            groups.append(refs[pos:pos + size])
            pos += size
        return groups, pos

    def wrapped(*refs):
        ins, pos = refs[:n_in], n_in
        r_ins, pos = take(refs, pos, [c[0] for c in counts])
        outs, pos = refs[pos:pos + n_out], pos + n_out
        r_outs, pos = take(refs, pos, [c[1] for c in counts])
        scr, pos = refs[pos:pos + n_scr], pos + n_scr
        r_scr, pos = take(refs, pos, [c[2] for c in counts])
        ids = [pl.program_id(d) for d in range(len(grid))]
        first = functools.reduce(lambda a, b: a & b, [i == 0 for i in ids])
        last = functools.reduce(lambda a, b: a & b, [i == g - 1 for i, g in zip(ids, grid)])

        @pl.when(first)
        def _():
            for r, a, b, c in zip(riders, r_ins, r_outs, r_scr):
                r.start(a, b, c)

        body(*ins, *outs, *scr)

        if any(r.relay for r in riders):
            at = [max(g - 2, 0) for g in grid]

            @pl.when(functools.reduce(lambda a, b: a & b, [i == s for i, s in zip(ids, at)]))
            def _():
                for r, a, b, c in zip(riders, r_ins, r_outs, r_scr):
                    if r.relay:
                        r.relay(a, b, c)

        @pl.when(last)
        def _():
            for r, a, b, c in zip(riders, r_ins, r_outs, r_scr):
                r.finish(a, b, c)

    hbm = pl.BlockSpec(memory_space=pltpu.HBM)
    r_args = [a for r in riders for a in r.arrays]
    r_shapes = [s for r in riders for s in r.out_shape]
    return pl.pallas_call(
        wrapped if riders else body, name=name, grid=grid, in_specs=list(in_specs) + [hbm] * len(r_args),
        out_specs=list(out_specs) + [hbm] * len(r_shapes), out_shape=list(out_shape) + r_shapes,
        scratch_shapes=list(scratch) + [s for r in riders for s in r.scratch],
        compiler_params=_cparams(("arbitrary",) * len(grid)))(*args, *r_args)


def _dot(a, b):
    return jnp.dot(a.astype(BF16), b.astype(BF16), preferred_element_type=F32)


def _dot_nt(a, b):
    return lax.dot_general(a.astype(BF16), b.astype(BF16), (((1,), (1,)), ((), ())), preferred_element_type=F32)


def _dot_tn(a, b):
    return lax.dot_general(a.astype(BF16), b.astype(BF16), (((0,), (0,)), ((), ())), preferred_element_type=F32)


def _split(x, n):
    parts = []
    for _ in range(n):
        p = x.astype(BF16)
        parts.append(p)
        x = x - p.astype(F32)
    return parts


def _dot_exact_lhs(m, x, n):
    return functools.reduce(lambda a, b: a + b, [jnp.dot(m, p, preferred_element_type=F32) for p in _split(x, n)])


def _dot_exact_rhs(x, m, n):
    return functools.reduce(lambda a, b: a + b, [jnp.dot(p, m, preferred_element_type=F32) for p in _split(x, n)])


def _rms(x, g):
    r = lax.rsqrt(jnp.mean(x * x, axis=-1, keepdims=True) + EPS)
    return x * r * g, r


def _rms_bwd(x, r, g, dy):
    xr = x * r
    u = dy * g
    dx = r * (u - xr * jnp.mean(u * xr, axis=-1, keepdims=True))
    return dx, jnp.sum(dy * xr, axis=0, keepdims=True)


def _iota(shape, dim):
    return lax.broadcasted_iota(jnp.int32, shape, dim)


def _sigmoid(x):
    return 1.0 / (1.0 + jnp.exp(-x))


def _acc_rows(ref, row):
    ref[...] += jnp.broadcast_to(row, ref.shape)


def _inproj(x, g, w_t, riders=()):
    t = x.shape[0]
    tm = min(TM_PROJ, t)

    def body(x_ref, g_ref, w_ref, z_ref, h_ref):
        h, _ = _rms(x_ref[...], g_ref[...])
        hb = h.astype(BF16)
        h_ref[...] = hb
        z_ref[...] = _dot_nt(hb, w_ref[...])

    return _call(
        body, "inproj", (t // tm,),
        [pl.BlockSpec((tm, D), lambda i: (i, 0)), pl.BlockSpec((1, D), lambda i: (0, 0)),
         pl.BlockSpec((ZC, D), lambda i: (0, 0), pipeline_mode=pl.Buffered(1))],
        [pl.BlockSpec((tm, ZC), lambda i: (i, 0)), pl.BlockSpec((tm, D), lambda i: (i, 0))],
        [jax.ShapeDtypeStruct((t, ZC), F32), jax.ShapeDtypeStruct((t, D), BF16)], [], (x, g, w_t), riders)


def _kv_proj(mem, g, w):
    def body(m_ref, g_ref, w_ref, kv_ref, mn_ref):
        mn, _ = _rms(m_ref[...], g_ref[...])
        mb = mn.astype(BF16)
        mn_ref[...] = mb
        for j in range(NDEV):
            kv_ref[:, j * XKV_SHARD:(j + 1) * XKV_SHARD] = jnp.dot(mb, w_ref[j], preferred_element_type=F32)

    return pl.pallas_call(
        body, name="kv_proj",
        out_shape=[jax.ShapeDtypeStruct((NMEM, 2 * D), F32), jax.ShapeDtypeStruct((NMEM, D), BF16)],
        compiler_params=pltpu.CompilerParams(vmem_limit_bytes=VMEM_LIMIT))(mem, g, w)


def _softmax_head(qb, kb):
    s = _dot_nt(qb, kb) * (1.0 / 16.0)
    e = jnp.exp(s - jnp.max(s, axis=-1, keepdims=True))
    return e / jnp.sum(e, axis=-1, keepdims=True)


def _attn_fwd(x, z, o_f, o_b, conv_w, conv_norm, gla_norm4, w_out, g, w_xq, kb, vb, w_xo):
    t = x.shape[0]
    tm = min(TM, t)
    nblk = t // tm
    jmap = lambda i: i

    def body(x_ref, zq_ref, zk_ref, zv_ref, zg_ref, cb_ref, cc_ref, cu_ref, ccp_ref, ccn_ref, cup_ref, cun_ref,
             of_ref, ob_ref, cw_ref, cn_ref, gn_ref, wo_ref, g_ref, wq_ref, k_ref, v_ref, wx_ref,
             x1_ref, x2_ref, xn_ref, q_ref, a_ref, y_ref, opre_ref):
        j = pl.program_id(0)
        zv = zv_ref[...]
        sb = _head_sum((zq_ref[...] * 0.125) * zk_ref[...], 64, 128)
        o_pre = of_ref[...] + ob_ref[...] - sb * zv
        opre_ref[...] = o_pre
        on, _ = _head_norm(o_pre)
        zg = zg_ref[...]
        y_ref[:, CW:] = (on * gn_ref[...] * (zg * _sigmoid(zg))).astype(BF16)
        cb = cb_ref[...]
        _, _, _, conv = _conv_parts(cb, cc_ref[...], cu_ref[...], ccp_ref[pl.ds(7, 1), :], cup_ref[pl.ds(7, 1), :],
                                    ccn_ref[pl.ds(0, 1), :], cun_ref[pl.ds(0, 1), :], cw_ref, j == 0,
                                    j == nblk - 1, tm)
        yc = cb * conv
        gm = _group_sum(yc * yc) * (1.0 / 64.0)
        y_ref[:, :CW] = (yc * lax.rsqrt(gm + EPS) * cn_ref[...]).astype(BF16)

        x1 = x_ref[...] + jnp.dot(y_ref[...], wo_ref[...], preferred_element_type=F32)
        x1_ref[...] = x1
        xn, _ = _rms(x1, g_ref[...])
        xb = xn.astype(BF16)
        xn_ref[...] = xb
        qb = jnp.dot(xb, wq_ref[...], preferred_element_type=F32).astype(BF16)
        q_ref[...] = qb
        heads = [slice(h * XD, (h + 1) * XD) for h in range(NH)]
        ps = [_softmax_head(qb[:, hs], k_ref[:, hs]) for hs in heads]
        for hs, p in zip(heads, ps):
            a_ref[:, hs] = _dot(p, v_ref[:, hs]).astype(BF16)
        x2_ref[...] = x1 + jnp.dot(a_ref[...], wx_ref[...], preferred_element_type=F32)

    tok = lambda i: (i, 0)
    full = lambda i: (0, 0)
    once = pl.Buffered(1)
    tokd, tokv = pl.BlockSpec((tm, D), tok), pl.BlockSpec((tm, GV), tok)
    weight = pl.BlockSpec((D, D), full, pipeline_mode=once)
    ccp, ccn = _halo_specs(tm, nblk, t, ZB_CC, jmap)
    cup, cun = _halo_specs(tm, nblk, t, ZB_CU, jmap)
    in_specs = [tokd, _zspec(tm, GK, ZB_Q, jmap), _zspec(tm, GK, ZB_K, jmap), _zspec(tm, GV, ZB_V, jmap),
                _zspec(tm, GV, ZB_G, jmap), _zspec(tm, CW, ZB_CB, jmap), _zspec(tm, CW, ZB_CC, jmap),
                _zspec(tm, CW, ZB_CU, jmap), ccp, ccn, cup, cun, tokv, tokv,
                pl.BlockSpec((3, CW), full), pl.BlockSpec((1, CW), full), pl.BlockSpec((1, GV), full),
                weight, pl.BlockSpec((1, D), full), weight, pl.BlockSpec((NMEM, D), full),
                pl.BlockSpec((NMEM, D), full), weight]
    return pl.pallas_call(
        body, name="attn_fwd", grid=(nblk,), in_specs=in_specs, out_specs=[tokd] * 6 + [tokv],
        out_shape=[jax.ShapeDtypeStruct((t, D), F32), jax.ShapeDtypeStruct((t, D), F32),
                   jax.ShapeDtypeStruct((t, D), BF16), jax.ShapeDtypeStruct((t, D), BF16),
                   jax.ShapeDtypeStruct((t, D), BF16), jax.ShapeDtypeStruct((t, D), BF16),
                   jax.ShapeDtypeStruct((t, GV), F32)],
        compiler_params=_cparams(("arbitrary",)))(
            x, z, z, z, z, z, z, z, z, z, z, z, o_f, o_b, conv_w, conv_norm, gla_norm4, w_out, g, w_xq, kb, vb, w_xo)


def _mlp_fwd(x2, g, w_up_t, w_down, fg, target):
    t = x2.shape[0]
    tm = min(TM_MLP_FWD, t)

    def body(x_ref, g_ref, wu_ref, wd_ref, fg_ref, t_ref, h1_ref, xn_ref, dx_ref, dxb_ref, loss_ref, dfg_ref, ab):
        @pl.when(pl.program_id(0) == 0)
        def _():
            loss_ref[...] = jnp.zeros_like(loss_ref)
            dfg_ref[...] = jnp.zeros_like(dfg_ref)

        x = x_ref[...]
        xn, _ = _rms(x, g_ref[...])
        xnb = xn.astype(BF16)
        xn_ref[...] = xnb
        for q in range(FF // TF):
            cols = slice(q * TF, (q + 1) * TF)
            h1 = _dot_nt(xnb, wu_ref[cols, :])
            h1_ref[:, cols] = h1.astype(BF16)
            hr = jnp.maximum(h1, 0.0)
            ab[:, cols] = (hr * hr).astype(BF16)
        x3 = x + jnp.dot(ab[...], wd_ref[...], preferred_element_type=F32)
        y, r = _rms(x3, fg_ref[...])
        e = y - t_ref[...]
        row = jnp.mean(e * e, axis=-1, keepdims=True)
        _acc_rows(loss_ref, 0.5 * jnp.sum(row, axis=0, keepdims=True))
        dx, dfg = _rms_bwd(x3, r, fg_ref[...], e * (1.0 / D))
        dx_ref[...] = dx
        dxb_ref[...] = dx.astype(BF16)
        _acc_rows(dfg_ref, dfg)

    tok = lambda i: (i, 0)
    full = lambda i: (0, 0)
    once = pl.Buffered(1)
    return pl.pallas_call(
        body, name="mlp_fwd", grid=(t // tm,),
        in_specs=[pl.BlockSpec((tm, D), tok), pl.BlockSpec((1, D), full),
                  pl.BlockSpec((FF, D), full, pipeline_mode=once), pl.BlockSpec((FF, D), full, pipeline_mode=once),
                  pl.BlockSpec((1, D), full), pl.BlockSpec((tm, D), tok)],
        out_specs=[pl.BlockSpec((tm, FF), tok), pl.BlockSpec((tm, D), tok), pl.BlockSpec((tm, D), tok),
                   pl.BlockSpec((tm, D), tok), pl.BlockSpec((8, 128), full), pl.BlockSpec((8, D), full)],
        out_shape=[jax.ShapeDtypeStruct((t, FF), BF16), jax.ShapeDtypeStruct((t, D), BF16),
                   jax.ShapeDtypeStruct((t, D), F32), jax.ShapeDtypeStruct((t, D), BF16),
                   jax.ShapeDtypeStruct((8, 128), F32), jax.ShapeDtypeStruct((8, D), F32)],
        scratch_shapes=[pltpu.VMEM((tm, FF), BF16)],
        compiler_params=_cparams(("arbitrary",)))(x2, g, w_up_t, w_down, fg, target)


def _mlp_bwd(dx3, dx3b, h1b, w_down, w_up_t, x2, g):
    t = x2.shape[0]
    tm = min(TM_MLP, t)

    def body(dx_ref, dxb_ref, h1_ref, wd_ref, wu_ref, x_ref, g_ref, a_ref, dh_ref, dx2_ref, dx2b_ref, dg_ref):
        @pl.when(pl.program_id(0) == 0)
        def _():
            dg_ref[...] = jnp.zeros_like(dg_ref)

        for q in range(FF // TF):
            cols = slice(q * TF, (q + 1) * TF)
            hr = jnp.maximum(h1_ref[:, cols].astype(F32), 0.0)
            da = _dot_nt(dxb_ref[...], wd_ref[cols, :])
            a_ref[:, cols] = (hr * hr).astype(BF16)
            dh_ref[:, cols] = (da * 2.0 * hr).astype(BF16)
        dxn = jnp.dot(dh_ref[...], wu_ref[...], preferred_element_type=F32)
        x = x_ref[...]
        r = lax.rsqrt(jnp.mean(x * x, axis=-1, keepdims=True) + EPS)
        dx, dg = _rms_bwd(x, r, g_ref[...], dxn)
        dx2 = dx_ref[...] + dx
        dx2_ref[...] = dx2
        dx2b_ref[...] = dx2.astype(BF16)
        _acc_rows(dg_ref, dg)

    tok = lambda i: (i, 0)
    full = lambda i: (0, 0)
    once = pl.Buffered(1)
    return pl.pallas_call(
        body, name="mlp_bwd", grid=(t // tm,),
        in_specs=[pl.BlockSpec((tm, D), tok), pl.BlockSpec((tm, D), tok), pl.BlockSpec((tm, FF), tok),
                  pl.BlockSpec((FF, D), full, pipeline_mode=once), pl.BlockSpec((FF, D), full, pipeline_mode=once),
                  pl.BlockSpec((tm, D), tok), pl.BlockSpec((1, D), full)],
        out_specs=[pl.BlockSpec((tm, FF), tok), pl.BlockSpec((tm, FF), tok), pl.BlockSpec((tm, D), tok),
                   pl.BlockSpec((tm, D), tok), pl.BlockSpec((8, D), full)],
        out_shape=[jax.ShapeDtypeStruct((t, FF), BF16), jax.ShapeDtypeStruct((t, FF), BF16),
                   jax.ShapeDtypeStruct((t, D), F32), jax.ShapeDtypeStruct((t, D), BF16),
                   jax.ShapeDtypeStruct((8, D), F32)],
        compiler_params=_cparams(("arbitrary",)))(dx3, dx3b, h1b, w_down, w_up_t, x2, g)


def _attn_bwd(x1, dx2, dx2b, qb, kb, vb, w_xo, w_xq, w_out, g, riders=()):
    t = x1.shape[0]
    tm = min(TM, t)

    def body(x_ref, dx2_ref, dx2b_ref, q_ref, k_ref, v_ref, wx_ref, wq_ref, wo_ref, g_ref,
             dx1_ref, dx1b_ref, dy_ref, dq_ref, dkv_ref, dg_ref):
        @pl.when(pl.program_id(0) == 0)
        def _():
            dkv_ref[...] = jnp.zeros_like(dkv_ref)
            dg_ref[...] = jnp.zeros_like(dg_ref)

        datt = _dot_nt(dx2b_ref[...], wx_ref[...]).astype(BF16)
        heads = [slice(h * XD, (h + 1) * XD) for h in range(NH)]
        ps = [_softmax_head(q_ref[:, hs], k_ref[:, hs]) for hs in heads]
        dps = [_dot_nt(datt[:, hs], v_ref[:, hs]) for hs in heads]
        dss = [(p * (dp - jnp.sum(dp * p, axis=-1, keepdims=True)) * (1.0 / 16.0)).astype(BF16)
               for p, dp in zip(ps, dps)]
        for h, (hs, p, ds) in enumerate(zip(heads, ps, dss)):
            dq_ref[:, hs] = _dot(ds, k_ref[:, hs]).astype(BF16)
            dkv_ref[:, hs] += _dot_tn(ds, q_ref[:, hs])
            dkv_ref[:, D + h * XD:D + (h + 1) * XD] += _dot_tn(p, datt[:, hs])
        dxn = _dot_nt(dq_ref[...], wq_ref[...])
        x = x_ref[...]
        r = lax.rsqrt(jnp.mean(x * x, axis=-1, keepdims=True) + EPS)
        dx, dg = _rms_bwd(x, r, g_ref[...], dxn)
        dx1 = dx2_ref[...] + dx
        dx1_ref[...] = dx1
        dx1b = dx1.astype(BF16)
        dx1b_ref[...] = dx1b
        dy_ref[...] = _dot_nt(dx1b, wo_ref[...])
        _acc_rows(dg_ref, dg)

    tok = lambda i: (i, 0)
    full = lambda i: (0, 0)
    return _call(
        body, "attn_bwd", (t // tm,),
        [pl.BlockSpec((tm, D), tok), pl.BlockSpec((tm, D), tok), pl.BlockSpec((tm, D), tok),
         pl.BlockSpec((tm, D), tok), pl.BlockSpec((NMEM, D), full), pl.BlockSpec((NMEM, D), full),
         pl.BlockSpec((D, D), full), pl.BlockSpec((D, D), full), pl.BlockSpec((D, D), full),
         pl.BlockSpec((1, D), full)],
        [pl.BlockSpec((tm, D), tok), pl.BlockSpec((tm, D), tok), pl.BlockSpec((tm, D), tok),
         pl.BlockSpec((tm, D), tok), pl.BlockSpec((NMEM, 2 * D), full), pl.BlockSpec((8, D), full)],
        [jax.ShapeDtypeStruct((t, D), F32), jax.ShapeDtypeStruct((t, D), BF16),
         jax.ShapeDtypeStruct((t, D), F32), jax.ShapeDtypeStruct((t, D), BF16),
         jax.ShapeDtypeStruct((NMEM, 2 * D), F32), jax.ShapeDtypeStruct((8, D), F32)], [],
        (x1, dx2, dx2b, qb, kb, vb, w_xo, w_xq, w_out, g), riders)


def _kv_bwd(dkv, memn, mem, g, w):
    def body(dkv_ref, mn_ref, m_ref, g_ref, w_ref, dw_ref, dg_ref):
        dkvb = dkv_ref[...].astype(BF16)
        dmn = jnp.zeros((NMEM, D), F32)
        for j in range(NDEV):
            cols = slice(j * XKV_SHARD, (j + 1) * XKV_SHARD)
            dw_ref[j] = _dot_tn(mn_ref[...], dkvb[:, cols])
            dmn += _dot_nt(dkvb[:, cols], w_ref[j])
        m = m_ref[...]
        r = lax.rsqrt(jnp.mean(m * m, axis=-1, keepdims=True) + EPS)
        dg_ref[...] = jnp.broadcast_to(jnp.sum(dmn * m * r, axis=0, keepdims=True), dg_ref.shape)

    return pl.pallas_call(
        body, name="kv_bwd",
        out_shape=[jax.ShapeDtypeStruct((NDEV, D, XKV_SHARD), F32), jax.ShapeDtypeStruct((8, D), F32)],
        compiler_params=pltpu.CompilerParams(vmem_limit_bytes=VMEM_LIMIT))(dkv, memn, mem, g, w)


def _inproj_bwd(dz, w_t, x, dx1, g, riders=()):
    t = x.shape[0]
    tm = min(TM_PROJ, t)

    def body(dz_ref, w_ref, x_ref, dx1_ref, g_ref, gx_ref, dg_ref):
        @pl.when(pl.program_id(0) == 0)
        def _():
            dg_ref[...] = jnp.zeros_like(dg_ref)

        dh = jnp.dot(dz_ref[...], w_ref[...], preferred_element_type=F32)
        x = x_ref[...]
        r = lax.rsqrt(jnp.mean(x * x, axis=-1, keepdims=True) + EPS)
        dx, dg = _rms_bwd(x, r, g_ref[...], dh)
        gx_ref[...] = dx1_ref[...] + dx
        _acc_rows(dg_ref, dg)

    tok = lambda i: (i, 0)
    full = lambda i: (0, 0)
    return _call(
        body, "inproj_bwd", (t // tm,),
        [pl.BlockSpec((tm, ZC), tok), pl.BlockSpec((ZC, D), full, pipeline_mode=pl.Buffered(1)),
         pl.BlockSpec((tm, D), tok),
         pl.BlockSpec((tm, D), tok), pl.BlockSpec((1, D), full)],
        [pl.BlockSpec((tm, D), tok), pl.BlockSpec((8, D), full)],
        [jax.ShapeDtypeStruct((t, D), F32), jax.ShapeDtypeStruct((8, D), F32)], [], (dz, w_t, x, dx1, g), riders)


def _matmul_tn(a, b, name, rows=None, riders=()):
    t, k = a.shape
    n = b.shape[1]
    tk, tn = [1024 if size % 1024 == 0 else 640 for size in (k, n)]
    tt = min(TT, t)
    rows = rows or k

    def body(a_ref, b_ref, o_ref):
        @pl.when(pl.program_id(2) == 0)
        def _():
            o_ref[...] = jnp.zeros_like(o_ref)

        o_ref[...] += _dot_tn(a_ref[...], b_ref[...])

    return _call(
        body, name, (k // tk, n // tn, t // tt),
        [pl.BlockSpec((tt, tk), lambda i, j, s: (s, i)), pl.BlockSpec((tt, tn), lambda i, j, s: (s, j))],
        [pl.BlockSpec((tk, tn), lambda i, j, s: (i, j))], [jax.ShapeDtypeStruct((rows, n), F32)], [], (a, b), riders)


def _lane_head(shape, dim, shift):
    return _iota(shape, dim) >> shift


CUM_ROWS = 128


def _chunk_cumsum(x, upper, n):
    r, c = _iota((CUM_ROWS, CUM_ROWS), 0), _iota((CUM_ROWS, CUM_ROWS), 1)
    tri = (c >= r) if upper else (c <= r)
    cum = jnp.where(((r >> 6) == (c >> 6)) & tri, 1.0, 0.0).astype(BF16)
    return jnp.concatenate([_dot_exact_lhs(cum, x[g:g + CUM_ROWS], n) for g in range(0, x.shape[0], CUM_ROWS)],
                           axis=0)


def _gla_recompute(q_raw, k, lr, wpad, bias, rev, tb):
    pre = _dot(lr, wpad) + bias
    la = (jnp.minimum(pre, 0.0) - jnp.log(1.0 + jnp.exp(-jnp.abs(pre)))) * (1.0 / 16.0)
    b = _chunk_cumsum(la, rev, 3)
    e, ei = jnp.exp(b), jnp.exp(-b)
    qt = (q_raw * 0.125) * e
    kt = k * ei
    return pre, b, e, ei, qt, kt


def _stack_heads(x, shift):
    head = _lane_head(x.shape, 1, shift)
    return jnp.concatenate([jnp.where(head == h, x, 0.0) for h in range(NH)], axis=0).astype(BF16)


def _fold_heads(x, shift):
    head = _lane_head((CH, x.shape[1]), 1, shift)
    return functools.reduce(lambda a, b: a + b,
                            [jnp.where(head == h, x[h * CH:(h + 1) * CH], 0.0) for h in range(NH)])


def _wide_mask(rev):
    r, s = _iota((CH, NH * CH), 0), _iota((CH, NH * CH), 1) & (CH - 1)
    return (s >= r) if rev else (s <= r)


def _rows_by_head(x):
    w = x.shape[1] // NH
    return jnp.concatenate([x[:, h * w:(h + 1) * w] for h in range(NH)], axis=0)


def _lanes_by_head(x):
    return jnp.concatenate([x[h * CH:(h + 1) * CH] for h in range(NH)], axis=1)


def _state_compact(xt):
    head = _lane_head((128, GK), 1, 6)
    return functools.reduce(lambda a, b: a + b,
                            [jnp.where(head == h, xt[h * 128:(h + 1) * 128], 0.0) for h in range(NH)])


def _conv_parts(cb, cc, cu, ccp, cup, ccn, cun, cw_ref, first, last, tb):
    h = cc * cu
    hp = jnp.where(first, 0.0, ccp * cup)
    hn = jnp.where(last, 0.0, ccn * cun)
    rows = _iota(h.shape, 0)
    h_m1 = jnp.where(rows == 0, hp, pltpu.roll(h, 1, 0))
    h_p1 = jnp.where(rows == tb - 1, hn, pltpu.roll(h, tb - 1, 0))
    conv = cw_ref[pl.ds(0, 1), :] * h_m1 + cw_ref[pl.ds(1, 1), :] * h + cw_ref[pl.ds(2, 1), :] * h_p1
    return h, h_m1, h_p1, conv


def _head_sum(x, w_in, w_out):
    shape, sh_in, sh_out = (2 * w_in, 2 * w_out), w_in.bit_length() - 1, w_out.bit_length() - 1
    sel = jnp.where((_iota(shape, 0) >> sh_in) == (_iota(shape, 1) >> sh_out), 1.0, 0.0).astype(BF16)
    return jnp.concatenate([_dot_exact_rhs(x[:, s:s + 2 * w_in], sel, 2) for s in range(0, NH * w_in, 2 * w_in)],
                           axis=1)


def _group_sum(x):
    ones = jnp.where((_iota((128, 128), 0) >> 6) == (_iota((128, 128), 1) >> 6), 1.0, 0.0).astype(BF16)
    return jnp.concatenate([_dot_exact_rhs(x[:, s:s + 128], ones, 2) for s in range(0, x.shape[1], 128)], axis=1)


def _head_norm(o):
    ons, rs = [], []
    for h in range(NH):
        slab = o[:, h * 128:(h + 1) * 128]
        r = lax.rsqrt(jnp.mean(slab * slab, axis=-1, keepdims=True) + EPS)
        ons.append(slab * r)
        rs.append(jnp.broadcast_to(r, slab.shape))
    return jnp.concatenate(ons, axis=1), jnp.concatenate(rs, axis=1)


def _zspec(tb, width, blk, jmap):
    return pl.BlockSpec((tb, width), lambda i: (jmap(i), blk))


def _halo_specs(tb, nblk, t, blk, jmap):
    prev = pl.BlockSpec((8, CW), lambda i: (jnp.maximum(jmap(i) * (tb // 8) - 1, 0), blk))
    nxt = pl.BlockSpec((8, CW), lambda i: (jnp.minimum((jmap(i) + 1) * (tb // 8), t // 8 - 1), blk))
    return prev, nxt


def _gla_fwd_block(q_ref, k_ref, v_ref, lr_ref, w_ref, bias_ref, o_ref, sd_ref, st, b_scr, rev, tb):
    nb = tb // CH
    _, b, _, _, qt, kt = _gla_recompute(q_ref[...], k_ref[...], lr_ref[...], w_ref[...], bias_ref[...], rev, tb)
    v = v_ref[...]
    b_scr[...] = b
    yield
    maskw = _wide_mask(rev)
    order = list(reversed(range(nb))) if rev else list(range(nb))
    rows = [slice(c * CH, (c + 1) * CH) for c in range(nb)]
    state = st[...]
    for c in order:
        gdec = jnp.exp(b_scr[pl.ds(c * CH + (0 if rev else CH - 1), 1), :])
        sd_ref[c] = state
        a = jnp.where(maskw, _dot_nt(qt[rows[c]], _stack_heads(kt[rows[c]], 6)), 0.0)
        o_inter = _lanes_by_head(_dot_nt(_stack_heads(qt[rows[c]], 6), state))
        o_ref[pl.ds(c * CH, CH), :] = _dot(a, _stack_heads(v[rows[c]], 7)) + o_inter
        state = state * gdec + _state_compact(_dot_tn(v[rows[c]], kt[rows[c]] * gdec))
        yield
    st[...] = state
    yield


def _gla_fwd(z, waf_pad, b_af, wab_pad, b_ab, riders=()):
    t = z.shape[0]
    tb = min(TB, t)
    nblk, nb = t // tb, tb // CH
    jmaps = (lambda i: i, lambda i: nblk - 1 - i)

    def body(qf, kf, vf, lrf, qr, kr, vr, lrr, wf, bf, wr, br, of_ref, sdf_ref, or_ref, sdr_ref,
             st_f, st_r, b_f, b_r):
        @pl.when(pl.program_id(0) == 0)
        def _():
            st_f[...] = jnp.zeros_like(st_f)
            st_r[...] = jnp.zeros_like(st_r)

        for _ in zip(_gla_fwd_block(qf, kf, vf, lrf, wf, bf, of_ref, sdf_ref, st_f, b_f, False, tb),
                     _gla_fwd_block(qr, kr, vr, lrr, wr, br, or_ref, sdr_ref, st_r, b_r, True, tb)):
            pass

    full = lambda i: (0, 0)
    zspecs = [s for jm in jmaps for s in (_zspec(tb, GK, ZB_Q, jm), _zspec(tb, GK, ZB_K, jm),
                                         _zspec(tb, GV, ZB_V, jm), _zspec(tb, 128, ZB_LR, jm))]
    wspecs = [pl.BlockSpec((128, GK), full), pl.BlockSpec((1, GK), full)] * 2
    out_specs = [s for jm in jmaps for s in (pl.BlockSpec((tb, GV), lambda i, jm=jm: (jm(i), 0)),
                                             pl.BlockSpec((nb, 128, GK), lambda i, jm=jm: (jm(i), 0, 0)))]
    out_shape = [jax.ShapeDtypeStruct((t, GV), F32), jax.ShapeDtypeStruct((t // CH, 128, GK), F32)] * 2
    scratch = [pltpu.VMEM((128, GK), F32), pltpu.VMEM((128, GK), F32), pltpu.VMEM((tb, GK), F32),
               pltpu.VMEM((tb, GK), F32)]
    return _call(body, "gla_fwd", (nblk,), zspecs + wspecs, out_specs, out_shape, scratch,
                 [z] * 8 + [waf_pad, b_af, wab_pad, b_ab], riders)


def _gla_bwd_chunks(do_ref, sd_ref, dst, b_scr, db_scr, dq_ref, dk_ref, dv_ref, qt, kt, e, ei, v, rev, nb):
    maskw = _wide_mask(rev)
    for c in (range(nb) if rev else reversed(range(nb))):
        sl = slice(c * CH, (c + 1) * CH)
        grow = c * CH + (0 if rev else CH - 1)
        gdec = jnp.exp(b_scr[pl.ds(grow, 1), :])
        qt_c, kt_c, v_c, do_c = qt[sl], kt[sl], v[sl], do_ref[pl.ds(c * CH, CH), :]
        s_in, ds_out = sd_ref[c], dst[...]
        kbd, vbd = _stack_heads(kt_c, 6), _stack_heads(v_c, 7)
        a = jnp.where(maskw, _dot_nt(qt_c, kbd), 0.0)
        da = jnp.where(maskw, _dot_nt(do_c, vbd), 0.0)
        dv_ref[pl.ds(c * CH, CH), :] = (_fold_heads(_dot_tn(a, do_c), 7)
                                        + _lanes_by_head(_dot_nt(_stack_heads(kt_c * gdec, 6), ds_out)))
        dqt = _dot(da, kbd) + _fold_heads(_dot(_rows_by_head(do_c), s_in), 6)
        dkh = _fold_heads(_dot(_rows_by_head(v_c), ds_out), 6)
        da_do = jnp.concatenate([da.astype(BF16), do_c.astype(BF16)], axis=1)
        both = _dot_tn(da_do, qt_c)
        dkt = _fold_heads(both[:NH * CH], 6) + dkh * gdec
        dg = jnp.sum(ds_out * s_in, axis=0, keepdims=True) + jnp.sum(kt_c * dkh, axis=0, keepdims=True)
        db_scr[pl.ds(c * CH, CH), :] = dqt * qt_c - dkt * kt_c
        db_scr[pl.ds(grow, 1), :] += dg * gdec
        dq_ref[pl.ds(c * CH, CH), :] = dqt * e[sl] * 0.125
        dk_ref[pl.ds(c * CH, CH), :] = dkt * ei[sl]
        dst[...] = ds_out * gdec + _state_compact(both[NH * CH:])
        yield


def _gate_bwd(db, pre, lr, wpad, rev, tb):
    dla = _chunk_cumsum(db, not rev, 2)
    dpre = dla * (1.0 / 16.0) / (1.0 + jnp.exp(pre))
    return dpre, _dot_nt(dpre, wpad), _dot_tn(lr, dpre)


def _gla_bwd_first(z, dy, o_pre, sd, wpad, bias, conv_w, conv_norm, gla_norm4, riders=()):
    t = z.shape[0]
    tb = min(TB_BWD, t)
    nblk, nb = t // tb, tb // CH
    jmap = lambda i: nblk - 1 - i

    def body(q_ref, k_ref, v_ref, lr_ref, g_ref, cb_ref, cc_ref, cu_ref, ccp_ref, ccn_ref, cup_ref, cun_ref,
             dy_ref, opre_ref, sd_ref, w_ref, bias_ref, cw_ref, cn_ref, gn_ref,
             do_ref, dq_ref, dk_ref, dv_ref, dlr_ref, dzg_ref, dzcb_ref, dconv_ref,
             dw_ref, dbias_ref, dcw_ref, dcn_ref, dgn_ref, dst, b_scr, db_scr):
        i = pl.program_id(0)
        j = jmap(i)

        @pl.when(i == 0)
        def _():
            dst[...] = jnp.zeros_like(dst)
            for ref in (dw_ref, dbias_ref, dcw_ref, dcn_ref, dgn_ref):
                ref[...] = jnp.zeros_like(ref)

        dyg = dy_ref[:, CW:]
        g = g_ref[...]
        sig = _sigmoid(g)
        on, rr = _head_norm(opre_ref[...])
        gn = gn_ref[...]
        dzg_ref[...] = (dyg * on * gn * (sig * (1.0 + g * (1.0 - sig)))).astype(BF16)
        don = dyg * (g * sig)
        _acc_rows(dgn_ref, jnp.sum(don * on, axis=0, keepdims=True))
        u = don * gn
        uo = u * on
        mean_uo = jnp.concatenate(
            [jnp.broadcast_to(jnp.mean(uo[:, h * 128:(h + 1) * 128], axis=-1, keepdims=True), (tb, 128))
             for h in range(NH)], axis=1)
        do_ref[...] = rr * (u - on * mean_uo)

        def conv_branch():
            cb = cb_ref[...]
            h, h_m1, h_p1, conv = _conv_parts(cb, cc_ref[...], cu_ref[...], ccp_ref[pl.ds(7, 1), :],
                                              cup_ref[pl.ds(7, 1), :], ccn_ref[pl.ds(0, 1), :],
                                              cun_ref[pl.ds(0, 1), :], cw_ref, j == 0, j == nblk - 1, tb)
            yc = cb * conv
            yield
            rc = lax.rsqrt(_group_sum(yc * yc) * (1.0 / 64.0) + EPS)
            ycr = yc * rc
            yield
            dyn = dy_ref[:, :CW]
            _acc_rows(dcn_ref, jnp.sum(dyn * ycr, axis=0, keepdims=True))
            uc = dyn * cn_ref[...]
            yield
            dyc = rc * (uc - ycr * (_group_sum(uc * ycr) * (1.0 / 64.0)))
            dzcb_ref[...] = (dyc * conv).astype(BF16)
            yield
            dconv = dyc * cb
            dconv_ref[...] = dconv
            yield
            dcw_ref[pl.ds(0, 1), :] += jnp.sum(dconv * h_m1, axis=0, keepdims=True)
            dcw_ref[pl.ds(1, 1), :] += jnp.sum(dconv * h, axis=0, keepdims=True)
            dcw_ref[pl.ds(2, 1), :] += jnp.sum(dconv * h_p1, axis=0, keepdims=True)
            yield

        lr, wp = lr_ref[...], w_ref[...]
        pre, b, e, ei, qt, kt = _gla_recompute(q_ref[...], k_ref[...], lr, wp, bias_ref[...], False, tb)
        b_scr[...] = b
        for _ in itertools.zip_longest(
                _gla_bwd_chunks(do_ref, sd_ref, dst, b_scr, db_scr, dq_ref, dk_ref, dv_ref, qt, kt, e, ei, v_ref[...],
                                False, nb), conv_branch()):
            pass
        dpre, dlr, dw = _gate_bwd(db_scr[...], pre, lr, wp, False, tb)
        dlr_ref[...] = dlr
        dw_ref[...] += dw
        _acc_rows(dbias_ref, jnp.sum(dpre, axis=0, keepdims=True))

    full = lambda i: (0, 0)
    tokv = pl.BlockSpec((tb, GV), lambda i: (jmap(i), 0))
    tokk = pl.BlockSpec((tb, GK), lambda i: (jmap(i), 0))
    ccp, ccn = _halo_specs(tb, nblk, t, ZB_CC, jmap)
    cup, cun = _halo_specs(tb, nblk, t, ZB_CU, jmap)
    in_specs = [_zspec(tb, GK, ZB_Q, jmap), _zspec(tb, GK, ZB_K, jmap), _zspec(tb, GV, ZB_V, jmap),
                _zspec(tb, 128, ZB_LR, jmap), _zspec(tb, GV, ZB_G, jmap), _zspec(tb, CW, ZB_CB, jmap),
                _zspec(tb, CW, ZB_CC, jmap), _zspec(tb, CW, ZB_CU, jmap), ccp, ccn, cup, cun,
                pl.BlockSpec((tb, D), lambda i: (jmap(i), 0)), tokv,
                pl.BlockSpec((nb, 128, GK), lambda i: (jmap(i), 0, 0)), pl.BlockSpec((128, GK), full),
                pl.BlockSpec((1, GK), full), pl.BlockSpec((3, CW), full), pl.BlockSpec((1, CW), full),
                pl.BlockSpec((1, GV), full)]
    out_specs = [tokv, tokk, tokk, tokv, pl.BlockSpec((tb, 128), lambda i: (jmap(i), 0)), tokv, tokv, tokv,
                 pl.BlockSpec((128, GK), full), pl.BlockSpec((8, GK), full), pl.BlockSpec((8, CW), full),
                 pl.BlockSpec((8, CW), full), pl.BlockSpec((8, GV), full)]
    out_shape = [jax.ShapeDtypeStruct((t, GV), F32), jax.ShapeDtypeStruct((t, GK), F32),
                 jax.ShapeDtypeStruct((t, GK), F32), jax.ShapeDtypeStruct((t, GV), F32),
                 jax.ShapeDtypeStruct((t, 128), F32), jax.ShapeDtypeStruct((t, GV), BF16),
                 jax.ShapeDtypeStruct((t, CW), BF16), jax.ShapeDtypeStruct((t, CW), F32),
                 jax.ShapeDtypeStruct((128, GK), F32), jax.ShapeDtypeStruct((8, GK), F32),
                 jax.ShapeDtypeStruct((8, CW), F32), jax.ShapeDtypeStruct((8, CW), F32),
                 jax.ShapeDtypeStruct((8, GV), F32)]
    return _call(
        body, "gla_bwd_first", (nblk,), in_specs, out_specs, out_shape,
        [pltpu.VMEM((128, GK), F32), pltpu.VMEM((tb, GK), F32), pltpu.VMEM((tb, GK), F32)],
        (z, z, z, z, z, z, z, z, z, z, z, z, dy, o_pre, sd, wpad, bias, conv_w, conv_norm, gla_norm4), riders)


def _gla_bwd_second(z, do, sd, wpad, bias, dqa, dka, dva, dlra, dzg, dzcb, dconv, conv_w, riders=()):
    t = z.shape[0]
    tb = min(TB_BWD, t)
    nblk, nb = t // tb, tb // CH
    jmap = lambda i: i

    def body(q_ref, k_ref, v_ref, lr_ref, cc_ref, cu_ref, do_ref, sd_ref, w_ref, bias_ref, dqa_ref, dka_ref,
             dva_ref, dlra_ref, dzg_ref, dzcb_ref, dc_ref, dcp_ref, dcn_ref, cw_ref,
             dz_ref, dw_ref, dbias_ref, dst, b_scr, db_scr, dq_scr, dk_scr, dv_scr, sb_scr, dsk_scr):
        i = pl.program_id(0)

        @pl.when(i == 0)
        def _():
            dst[...] = jnp.zeros_like(dst)
            dw_ref[...] = jnp.zeros_like(dw_ref)
            dbias_ref[...] = jnp.zeros_like(dbias_ref)

        q_raw, k, v, lr, wp = q_ref[...], k_ref[...], v_ref[...], lr_ref[...], w_ref[...]
        pre, b, e, ei, qt, kt = _gla_recompute(q_raw, k, lr, wp, bias_ref[...], True, tb)
        b_scr[...] = b

        def token_local():
            dc = dc_ref[...]
            rows = _iota(dc.shape, 0)
            dprev = jnp.where(i == 0, 0.0, dcp_ref[pl.ds(7, 1), :])
            dnext = jnp.where(i == nblk - 1, 0.0, dcn_ref[pl.ds(0, 1), :])
            dc_m1 = jnp.where(rows == 0, dprev, pltpu.roll(dc, 1, 0))
            dc_p1 = jnp.where(rows == tb - 1, dnext, pltpu.roll(dc, tb - 1, 0))
            yield
            dh = cw_ref[pl.ds(0, 1), :] * dc_p1 + cw_ref[pl.ds(1, 1), :] * dc + cw_ref[pl.ds(2, 1), :] * dc_m1
            dz_ref[:, 0:512] = dzcb_ref[...]
            yield
            dz_ref[:, 512:1024] = (dh * cu_ref[...]).astype(BF16)
            dz_ref[:, 1024:1536] = (dh * cc_ref[...]).astype(BF16)
            dz_ref[:, 2560:3072] = dzg_ref[...]
            yield
            sb_scr[...] = _head_sum((q_raw * 0.125) * k, 64, 128)
            yield
            dsk_scr[...] = _head_sum(do_ref[...] * v, 128, 64)
            yield

        for _ in itertools.zip_longest(
                _gla_bwd_chunks(do_ref, sd_ref, dst, b_scr, db_scr, dq_scr, dk_scr, dv_scr, qt, kt, e, ei, v, True, nb),
                token_local()):
            pass
        dpre, dlr, dw = _gate_bwd(db_scr[...], pre, lr, wp, True, tb)
        dw_ref[...] += dw
        _acc_rows(dbias_ref, jnp.sum(dpre, axis=0, keepdims=True))
        dsk = dsk_scr[...]
        dz_ref[:, 1536:1792] = (dqa_ref[...] + dq_scr[...] - dsk * k * 0.125).astype(BF16)
        dz_ref[:, 1792:2048] = (dka_ref[...] + dk_scr[...] - dsk * (q_raw * 0.125)).astype(BF16)
        dz_ref[:, 2048:2560] = (dva_ref[...] + dv_scr[...] - sb_scr[...] * do_ref[...]).astype(BF16)
        dz_ref[:, 3072:3200] = (dlra_ref[...] + dlr).astype(BF16)

    full = lambda i: (0, 0)
    tokv = pl.BlockSpec((tb, GV), lambda i: (i, 0))
    tokk = pl.BlockSpec((tb, GK), lambda i: (i, 0))
    dcp = pl.BlockSpec((8, CW), lambda i: (jnp.maximum(i * (tb // 8) - 1, 0), 0))
    dcn = pl.BlockSpec((8, CW), lambda i: (jnp.minimum((i + 1) * (tb // 8), t // 8 - 1), 0))
    in_specs = [_zspec(tb, GK, ZB_Q, jmap), _zspec(tb, GK, ZB_K, jmap), _zspec(tb, GV, ZB_V, jmap),
                _zspec(tb, 128, ZB_LR, jmap), _zspec(tb, CW, ZB_CC, jmap), _zspec(tb, CW, ZB_CU, jmap), tokv,
                pl.BlockSpec((nb, 128, GK), lambda i: (i, 0, 0)), pl.BlockSpec((128, GK), full),
                pl.BlockSpec((1, GK), full), tokk, tokk, tokv, pl.BlockSpec((tb, 128), lambda i: (i, 0)), tokv, tokv,
                tokv, dcp, dcn, pl.BlockSpec((3, CW), full)]
    out_specs = [pl.BlockSpec((tb, ZC), lambda i: (i, 0)), pl.BlockSpec((128, GK), full), pl.BlockSpec((8, GK), full)]
    out_shape = [jax.ShapeDtypeStruct((t, ZC), BF16), jax.ShapeDtypeStruct((128, GK), F32),
                 jax.ShapeDtypeStruct((8, GK), F32)]
    return _call(
        body, "gla_bwd_second", (nblk,), in_specs, out_specs, out_shape,
        [pltpu.VMEM((128, GK), F32), pltpu.VMEM((tb, GK), F32), pltpu.VMEM((tb, GK), F32),
         pltpu.VMEM((tb, GK), F32), pltpu.VMEM((tb, GK), F32), pltpu.VMEM((tb, GV), F32),
         pltpu.VMEM((tb, GV), F32), pltpu.VMEM((tb, GK), F32)],
        (z, z, z, z, z, z, do, sd, wpad, bias, dqa, dka, dva, dlra, dzg, dzcb, dconv, dconv, dconv, conv_w), riders)


def _step(x, mem, target, shard, small_pack, vec, place):
    own, from_chips = {}, {}

    def pair_sums(names, g4, from_sibling):
        pbs = {}
        for shape in dict.fromkeys(g.shape for g in g4):
            idx = [i for i, g in enumerate(g4) if g.shape == shape]
            pb, mine = _rs_pair_sum(place, [g4[i] for i in idx], [from_sibling[i] for i in idx],
                                    "pair_sum_" + "_".join(names[i] for i in idx))
            for i, b, o in zip(idx, pb, mine):
                pbs[i], own[names[i]] = b, o
        return [pbs[i] for i in range(len(g4))]

    def by_dest(g, n):
        return g.reshape((4, 2) + shard[n].shape)

    w_in, small_all = _exchange(_gather_rider([shard["w_in"], small_pack]), "gather_w_in")
    w_in = jnp.pad(w_in.reshape(ZW, D), ((0, ZC - ZW), (0, 0)))
    small_all = small_all.reshape(NDEV, -1)
    p, off = {}, 0
    for n, (r, c) in SMALL_SHARDED.items():
        p[n] = small_all[:, off:off + r * c].reshape(NDEV, r, c).transpose(1, 0, 2).reshape(r, NDEV * c)
        off += r * c
    zeros_lr = jnp.zeros((128 - LR, GK), BF16)
    waf_pad = jnp.concatenate([p["w_af"].astype(BF16), zeros_lr], axis=0)
    wab_pad = jnp.concatenate([jnp.zeros((LR, GK), BF16), p["w_ab"].astype(BF16), zeros_lr[:128 - 2 * LR]], axis=0)
    gla_norm4 = jnp.tile(vec["gla_norm"], (1, NH))

    z, hb, w_out, w_xq, w_xo, w_xkv = _inproj(
        x, vec["mix_norm"], w_in, [_gather_rider([shard[n] for n in ("w_out", "w_xq", "w_xo", "w_xkv")])])
    w_out, w_xq, w_xo = [a.reshape(D, D) for a in (w_out, w_xq, w_xo)]
    o_f, sd_f, o_b, sd_b, w_up_t, w_down = _gla_fwd(
        z, waf_pad, vec["b_af"], wab_pad, vec["b_ab"],
        [_gather_rider([shard["w_up"], shard["w_down"]], early_relay=False)])
    w_up_t, w_down = w_up_t.reshape(FF, D), w_down.reshape(FF, D)
    kv, memn = _kv_proj(mem, vec["mem_norm"], w_xkv)
    kb, vb = kv[:, :D].astype(BF16), kv[:, D:].astype(BF16)
    x1, x2, xn1, qb, attb, yb, o_pre = _attn_fwd(x, z, o_f, o_b, p["conv_w"], vec["conv_norm"], gla_norm4, w_out,
                                                 vec["xa_norm"], w_xq, kb, vb, w_xo)
    h1b, xn2, dx3, dx3b, loss8, dfinal = _mlp_fwd(x2, vec["mlp_norm"], w_up_t, w_down, vec["final_norm"], target)

    ab, dh1b, dx2, dx2b, dmlp = _mlp_bwd(dx3, dx3b, h1b, w_down, w_up_t, x2, vec["mlp_norm"])
    g_mlp = [by_dest(_matmul_tn(ab, dx3b, "dw_down")[0], "w_down"),
             by_dest(_matmul_tn(dh1b, xn2, "dw_up")[0], "w_up")]
    dx1, dx1b, dy, dqb, dkv, dxa, *s_mlp = _attn_bwd(x1, dx2, dx2b, qb, kb, vb, w_xo, w_xq, w_out, vec["xa_norm"],
                                                     riders=[_sibling_rider(g_mlp)])
    pb_mlp = pair_sums(("w_down", "w_up"), g_mlp, s_mlp)
    dw_xo = _matmul_tn(attb, dx2b, "dw_xo")[0]
    dw_xkv, dmemn = _kv_bwd(dkv, memn, mem, vec["mem_norm"], w_xkv)
    att_names = ("w_xo", "w_xq", "w_out", "w_xkv")
    g_att = [by_dest(g, n) for g, n in zip(
        (dw_xo, _matmul_tn(xn1, dqb, "dw_xq")[0], _matmul_tn(yb, dx1b, "dw_out")[0], dw_xkv), att_names)]
    res = _gla_bwd_first(z, dy, o_pre, sd_f, waf_pad, vec["b_af"], p["conv_w"], vec["conv_norm"], gla_norm4,
                         riders=[_chips_rider(pb_mlp), _sibling_rider(g_att)])
    do, dqa, dka, dva, dlra, dzg, dzcb, dconv, dwaf, dbaf, dcw, dcn, dgn = res[:13]
    from_chips["w_down"], from_chips["w_up"] = res[13:15]
    pb_att = pair_sums(att_names, g_att, res[15:])
    dz, dwab, dbab, *c_att = _gla_bwd_second(z, do, sd_b, wab_pad, vec["b_ab"], dqa, dka, dva, dlra, dzg, dzcb, dconv,
                                             p["conv_w"], riders=[_chips_rider(pb_att)])
    from_chips.update(zip(att_names, c_att))
    g_in = [by_dest(_matmul_tn(dz, hb, "dw_in", rows=ZW)[0], "w_in")]
    pb_in = pair_sums(("w_in",), g_in, _exchange(_sibling_rider(g_in), "grads_to_sibling_w_in"))
    grad_x, dmix, from_chips["w_in"] = _inproj_bwd(dz, w_in, x, dx1, vec["mix_norm"], riders=[_chips_rider(pb_in)])

    small_acc = dict(mix_norm=dmix, conv_w=dcw, conv_norm=dcn, w_af=dwaf, b_af=dbaf, w_ab=dwab, b_ab=dbab,
                     gla_norm=dgn, xa_norm=dxa, mem_norm=dmemn, mlp_norm=dmlp, final_norm=dfinal)
    return loss8, grad_x, small_acc, own, from_chips


def _place():
    return lax.axis_index("x"), lax.axis_index("y"), lax.axis_index("c")


class _Rider:
    def __init__(self, arrays, out_shape, scratch, start, finish, relay=None):
        self.arrays, self.out_shape, self.scratch, self.start, self.finish = arrays, out_shape, scratch, start, finish
        self.relay = relay


def _gather_rider(blks, early_relay=True):
    n = len(blks)

    def plan(in_refs, out_refs, sems):
        send_sems, recv_sems, local_sems = sems
        x, y, c = _place()
        me, sibling = (x, y, c), (x, y, 1 - c)
        chips = [(1 - x, y, c), (x, 1 - y, c), (1 - x, 1 - y, c)]

        def copy(a, k, block, to, own=False):
            px, py, pc = block
            dst = out_refs[a].at[4 * px + 2 * py + pc]
            return pltpu.make_async_remote_copy(
                src_ref=in_refs[a] if own else dst, dst_ref=dst, send_sem=send_sems.at[k, a],
                recv_sem=recv_sems.at[k, a], device_id=to, device_id_type=MESH)

        def local(a):
            return pltpu.make_async_copy(in_refs[a], out_refs[a].at[4 * x + 2 * y + c], local_sems.at[a])

        def own_sends(a):
            return [copy(a, 0, me, sibling, own=True)] + [copy(a, 1 + j, me, chip, own=True)
                                                          for j, chip in enumerate(chips)]

        return copy, local, own_sends, me, sibling, chips

    def start(in_refs, out_refs, sems):
        _, local, own_sends, _, _, _ = plan(in_refs, out_refs, sems)
        for a in range(n):
            local(a).start()
            for cp in own_sends(a):
                cp.start()

    def relay(in_refs, out_refs, sems):
        copy, _, _, me, sibling, chips = plan(in_refs, out_refs, sems)
        for j, chip in enumerate(chips):
            for a in range(n):
                copy(a, 1 + j, chip, me).wait_recv()
                copy(a, 4 + j, chip, sibling).start()

    def finish(in_refs, out_refs, sems):
        if not early_relay:
            relay(in_refs, out_refs, sems)
        copy, local, own_sends, me, sibling, chips = plan(in_refs, out_refs, sems)
        for a in range(n):
            copy(a, 0, sibling, me).wait_recv()
            for j, (px, py, pc) in enumerate(chips):
                copy(a, 4 + j, (px, py, 1 - pc), me).wait_recv()
            for cp in own_sends(a) + [copy(a, 4 + j, chip, sibling) for j, chip in enumerate(chips)]:
                cp.wait_send()
            local(a).wait()

    return _Rider(blks, [jax.ShapeDtypeStruct((NDEV,) + b.shape, b.dtype) for b in blks],
                  [pltpu.SemaphoreType.DMA((7, n)), pltpu.SemaphoreType.DMA((7, n)), pltpu.SemaphoreType.DMA((n,))],
                  start, finish, relay if early_relay else None)


def _sibling_rider(g4s):
    n = len(g4s)

    def copies(in_refs, out_refs, sems):
        send_sems, recv_sems = sems
        x, y, c = _place()
        return [pltpu.make_async_remote_copy(
            src_ref=in_refs[a].at[k, 1 - c], dst_ref=out_refs[a].at[k], send_sem=send_sems.at[k, a],
            recv_sem=recv_sems.at[k, a], device_id=(x, y, 1 - c), device_id_type=MESH)
            for a in range(n) for k in range(4)]

    def start(in_refs, out_refs, sems):
        for cp in copies(in_refs, out_refs, sems):
            cp.start()

    def finish(in_refs, out_refs, sems):
        for cp in copies(in_refs, out_refs, sems):
            cp.wait()

    return _Rider(g4s, [jax.ShapeDtypeStruct((4,) + g.shape[2:], g.dtype) for g in g4s],
                  [pltpu.SemaphoreType.DMA((4, n)), pltpu.SemaphoreType.DMA((4, n))], start, finish)


def _chips_rider(pbs):
    n = len(pbs)

    def copies(in_refs, out_refs, sems):
        send_sems, recv_sems = sems
        x, y, c = _place()
        peers = [(1 - x, y), (x, 1 - y), (1 - x, 1 - y)]
        return [pltpu.make_async_remote_copy(
            src_ref=in_refs[a].at[2 * px + py], dst_ref=out_refs[a].at[k], send_sem=send_sems.at[k, a],
            recv_sem=recv_sems.at[k, a], device_id=(px, py, c), device_id_type=MESH)
            for a in range(n) for k, (px, py) in enumerate(peers)]

    def start(in_refs, out_refs, sems):
        for cp in copies(in_refs, out_refs, sems):
            cp.start()

    def finish(in_refs, out_refs, sems):
        for cp in copies(in_refs, out_refs, sems):
            cp.wait()

    return _Rider(pbs, [jax.ShapeDtypeStruct((3,) + p.shape[1:], p.dtype) for p in pbs],
                  [pltpu.SemaphoreType.DMA((3, n)), pltpu.SemaphoreType.DMA((3, n))], start, finish)


def _exchange(rider, name):
    n_in, n_out = len(rider.arrays), len(rider.out_shape)

    def body(*refs):
        ins, outs, sems = refs[:n_in], refs[n_in:n_in + n_out], refs[n_in + n_out:]
        rider.start(ins, outs, sems)
        if rider.relay:
            rider.relay(ins, outs, sems)
        rider.finish(ins, outs, sems)

    hbm = pl.BlockSpec(memory_space=pltpu.HBM)
    return pl.pallas_call(body, name=name, out_shape=rider.out_shape, in_specs=[hbm] * n_in,
                          out_specs=[hbm] * n_out, scratch_shapes=rider.scratch)(*rider.arrays)


def _rs_pair_sum(place, g4s, r1s, name):
    n = len(g4s)
    rows, cols = g4s[0].shape[2:]
    tr = min(rows, 512)

    def body(pl_ref, *refs):
        for g_ref, r_ref, pb_ref, own_ref in zip(refs[:n], refs[n:2 * n], refs[2 * n:3 * n], refs[3 * n:]):
            s = g_ref[0, 0] + r_ref[0]
            pb_ref[0] = s.astype(BF16)

            @pl.when(pl.program_id(1) == pl_ref[0])
            def _():
                own_ref[...] = s

    grid_spec = pltpu.PrefetchScalarGridSpec(
        num_scalar_prefetch=1, grid=(rows // tr, 4),
        in_specs=[pl.BlockSpec((1, 1, tr, cols), lambda r, k, p: (k, p[1], r, 0))] * n
        + [pl.BlockSpec((1, tr, cols), lambda r, k, p: (k, r, 0))] * n,
        out_specs=[pl.BlockSpec((1, tr, cols), lambda r, k, p: (k, r, 0))] * n
        + [pl.BlockSpec((tr, cols), lambda r, k, p: (r, 0))] * n)
    res = pl.pallas_call(
        body, name=name, grid_spec=grid_spec,
        out_shape=[jax.ShapeDtypeStruct((4, rows, cols), BF16)] * n + [jax.ShapeDtypeStruct((rows, cols), F32)] * n,
        compiler_params=_cparams(("arbitrary", "arbitrary")))(place, *g4s, *r1s)
    return res[:n], res[n:]


PACK_ROWS = 32
VEC_ROW = {"mix_norm": 0, "conv_norm": 1, "b_af": 2, "b_ab": 3, "gla_norm": 4, "xa_norm": 5, "mem_norm": 6,
           "mlp_norm": 7, "final_norm": 8}
LOSS_ROW, MAT_ROW = 9, 16
MAT_LANE = {"w_af": 0, "w_ab": GK, "conv_w": 2 * GK}
MAT_SRC_ROW = {"w_af": 0, "w_ab": LR, "conv_w": 0}


SMALL_WIDTH = {"mix_norm": D, "conv_w": 64, "conv_norm": CW, "w_af": 32, "b_af": GK, "w_ab": 32, "b_ab": GK,
               "gla_norm": 128, "xa_norm": D, "mem_norm": D, "mlp_norm": D, "final_norm": D}


def _small_reduce(acc, loss8):
    names = list(SMALL)
    n = len(names)
    widths = SMALL_WIDTH

    def body(*refs):
        acc_refs = dict(zip(names, refs[:n]))
        loss_ref, tot = refs[n], refs[n + 1]
        pk, all_ref, send_sems, recv_sems, local_sem = refs[n + 2:]

        pk[...] = jnp.zeros_like(pk)
        for k, row in VEC_ROW.items():
            if k == "gla_norm":
                g = functools.reduce(lambda a, b: a + b, [acc_refs[k][pl.ds(0, 1), pl.ds(h * 128, 128)]
                                                          for h in range(NH)])
            else:
                g = acc_refs[k][pl.ds(0, 1), :]
            pk[pl.ds(row, 1), pl.ds(0, widths[k])] = g
        pk[pl.ds(LOSS_ROW, 1), pl.ds(0, 128)] = loss_ref[pl.ds(0, 1), :]
        for k, lane in MAT_LANE.items():
            rows, cols = (3, CW) if k == "conv_w" else (LR, GK)
            pk[pl.ds(MAT_ROW, rows), pl.ds(lane, cols)] = acc_refs[k][pl.ds(MAT_SRC_ROW[k], rows), :]

        x, y, c = _place()
        me, sibling = (x, y, c), (x, y, 1 - c)
        chips = [(1 - x, y, c), (x, 1 - y, c), (1 - x, 1 - y, c)]

        def copy(k, block, to, own=False):
            px, py, pc = block
            dst = all_ref.at[4 * px + 2 * py + pc]
            return pltpu.make_async_remote_copy(
                src_ref=pk if own else dst, dst_ref=dst, send_sem=send_sems.at[k], recv_sem=recv_sems.at[k],
                device_id=to, device_id_type=MESH)

        mine = pltpu.make_async_copy(pk, all_ref.at[4 * x + 2 * y + c], local_sem)
        mine.start()
        first = [copy(0, me, sibling, own=True)] + [copy(1 + j, me, chip, own=True) for j, chip in enumerate(chips)]
        for cp in first:
            cp.start()
        passed = [copy(4 + j, chip, sibling) for j, chip in enumerate(chips)]
        for j, chip in enumerate(chips):
            copy(1 + j, chip, me).wait_recv()
            passed[j].start()
        copy(0, sibling, me).wait_recv()
        for j, (px, py, pc) in enumerate(chips):
            copy(4 + j, (px, py, 1 - pc), me).wait_recv()
        for cp in first + passed:
            cp.wait_send()
        mine.wait()
        total = all_ref[0]
        for d in range(1, NDEV):
            total = total + all_ref[d]
        tot[...] = total

    return pl.pallas_call(
        body, name="small_reduce", out_shape=jax.ShapeDtypeStruct((PACK_ROWS, D), F32),
        scratch_shapes=[pltpu.VMEM((PACK_ROWS, D), F32), pltpu.VMEM((NDEV, PACK_ROWS, D), F32),
                        pltpu.SemaphoreType.DMA((7,)), pltpu.SemaphoreType.DMA((7,)), pltpu.SemaphoreType.DMA],
    )(*[acc[k] for k in names], loss8)


def _small_adamw(tot, ws, ms, vs):
    names = list(SMALL)
    n = len(names)
    widths = SMALL_WIDTH

    def body(*refs):
        tot = refs[0]
        w_refs, m_refs, v_refs = [dict(zip(names, refs[1 + q * n:1 + (q + 1) * n])) for q in range(3)]
        outs = refs[1 + 3 * n:1 + 7 * n]
        g_out, d_out, m_out, v_out = [dict(zip(names, outs[q * n:(q + 1) * n])) for q in range(4)]
        cut = refs[1 + 7 * n]
        x, y, c = _place()
        dev = 4 * x + 2 * y + c
        for k in names:
            if k in VEC_ROW:
                g = tot[pl.ds(VEC_ROW[k], 1), pl.ds(0, widths[k])]
            else:
                rows, cols = (3, CW) if k == "conv_w" else (LR, GK)
                wd = widths[k]
                sel = jnp.where(_iota((cols, wd), 0) == dev * wd + _iota((cols, wd), 1), 1.0, 0.0).astype(BF16)
                cut[:, pl.ds(0, wd)] = _dot_exact_rhs(tot[pl.ds(MAT_ROW, LR), pl.ds(MAT_LANE[k], cols)], sel, 3)
                g = cut[pl.ds(0, rows), pl.ds(0, wd)]
            g_out[k][...] = g
            d_out[k][...], m_out[k][...], v_out[k][...] = _adamw_math(w_refs[k][...], g, m_refs[k][...],
                                                                       v_refs[k][...])

    shapes = [jax.ShapeDtypeStruct(ws[k].shape, F32) for k in names]
    res = pl.pallas_call(
        body, name="small_adamw", out_shape=shapes * 4, scratch_shapes=[pltpu.VMEM((LR, 128), F32)],
    )(tot, *[ws[k] for k in names], *[ms[k] for k in names], *[vs[k] for k in names])
    return {k: tuple(res[q * n + i] for q in range(4)) for i, k in enumerate(names)}


def _adamw_math(w, g, m, v):
    m = ADAM_B1 * m + (1.0 - ADAM_B1) * g
    v = ADAM_B2 * v + (1.0 - ADAM_B2) * (g * g)
    m_hat = m / (1.0 - ADAM_B1 ** ADAM_STEP)
    v_hat = v / (1.0 - ADAM_B2 ** ADAM_STEP)
    delta = -ADAM_LR * (m_hat / (jnp.sqrt(v_hat) + ADAM_EPS) + ADAM_WD * w)
    return delta, m, v


def _adamw(ws, ms, vs, owns, r2s, name, grads_transposed=False):
    n = len(ws)
    _, r, c = ws[0].shape
    tr = 256 if r % 256 == 0 else r

    def body(*refs):
        ins, outs = refs[:5 * n], refs[5 * n:]
        for q in range(n):
            w_ref, m_ref, v_ref, o_ref, r_ref = [ins[k * n + q] for k in range(5)]
            g_ref, d_ref, nm_ref, nv_ref = [outs[k * n + q] for k in range(4)]
            g = ((o_ref[...] + r_ref[0].astype(F32)) + r_ref[1].astype(F32)) + r_ref[2].astype(F32)
            g = g.T if grads_transposed else g
            g_ref[...] = g
            d_ref[...], nm_ref[...], nv_ref[...] = _adamw_math(w_ref[...], g, m_ref[...], v_ref[...])

    spec = pl.BlockSpec((None, tr, c), lambda i: (0, i, 0))
    if grads_transposed:
        own_spec, r2_spec = pl.BlockSpec((c, tr), lambda i: (0, i)), pl.BlockSpec((3, c, tr), lambda i: (0, 0, i))
    else:
        own_spec, r2_spec = pl.BlockSpec((tr, c), lambda i: (i, 0)), pl.BlockSpec((3, tr, c), lambda i: (0, i, 0))
    res = pl.pallas_call(
        body, name=name, grid=(r // tr,),
        in_specs=[spec] * (3 * n) + [own_spec] * n + [r2_spec] * n,
        out_specs=[spec] * (4 * n), out_shape=[jax.ShapeDtypeStruct((1, r, c), F32)] * (4 * n),
        compiler_params=_cparams(("arbitrary",)))(*ws, *ms, *vs, *owns, *r2s)
    return [tuple(res[k * n + q] for k in range(4)) for q in range(n)]


MATS = ("w_in", "w_out", "w_xq", "w_xo", "w_xkv", "w_up", "w_down")
SMALL = ("mix_norm", "conv_w", "conv_norm", "w_af", "b_af", "w_ab", "b_ab", "gla_norm", "xa_norm", "mem_norm",
         "mlp_norm", "final_norm")
WEIGHTS = ("mix_norm", "w_in", "conv_w", "conv_norm", "w_af", "b_af", "w_ab", "b_ab", "gla_norm", "w_out", "xa_norm",
           "mem_norm", "w_xq", "w_xkv", "w_xo", "mlp_norm", "w_up", "w_down", "final_norm")
SMALL_SHARDED = {"conv_w": (3, 64), "w_af": (LR, 32), "w_ab": (LR, 32)}
SMALL_PACK_ROWS = 16


def kernel(x, mem, mix_norm, w_in, conv_w, conv_norm, w_af, b_af, w_ab, b_ab, gla_norm, w_out, xa_norm, mem_norm, w_xq, w_xkv, w_xo, mlp_norm, w_up, w_down, final_norm, loss_target, m_mix_norm, m_w_in, m_conv_w, m_conv_norm, m_w_af, m_b_af, m_w_ab, m_b_ab, m_gla_norm, m_w_out, m_xa_norm, m_mem_norm, m_w_xq, m_w_xkv, m_w_xo, m_mlp_norm, m_w_up, m_w_down, m_final_norm, v_mix_norm, v_w_in, v_conv_w, v_conv_norm, v_w_af, v_b_af, v_w_ab, v_b_ab, v_gla_norm, v_w_out, v_xa_norm, v_mem_norm, v_w_xq, v_w_xkv, v_w_xo, v_mlp_norm, v_w_up, v_w_down, v_final_norm):
    w = dict(mix_norm=mix_norm, w_in=w_in, conv_w=conv_w, conv_norm=conv_norm, w_af=w_af, b_af=b_af, w_ab=w_ab,
             b_ab=b_ab, gla_norm=gla_norm, w_out=w_out, xa_norm=xa_norm, mem_norm=mem_norm, w_xq=w_xq, w_xkv=w_xkv,
             w_xo=w_xo, mlp_norm=mlp_norm, w_up=w_up, w_down=w_down, final_norm=final_norm)
    mom = dict(mix_norm=m_mix_norm, w_in=m_w_in, conv_w=m_conv_w, conv_norm=m_conv_norm, w_af=m_w_af, b_af=m_b_af,
               w_ab=m_w_ab, b_ab=m_b_ab, gla_norm=m_gla_norm, w_out=m_w_out, xa_norm=m_xa_norm, mem_norm=m_mem_norm,
               w_xq=m_w_xq, w_xkv=m_w_xkv, w_xo=m_w_xo, mlp_norm=m_mlp_norm, w_up=m_w_up, w_down=m_w_down,
               final_norm=m_final_norm)
    var = dict(mix_norm=v_mix_norm, w_in=v_w_in, conv_w=v_conv_w, conv_norm=v_conv_norm, w_af=v_w_af, b_af=v_b_af,
               w_ab=v_w_ab, b_ab=v_b_ab, gla_norm=v_gla_norm, w_out=v_w_out, xa_norm=v_xa_norm, mem_norm=v_mem_norm,
               w_xq=v_w_xq, w_xkv=v_w_xkv, w_xo=v_w_xo, mlp_norm=v_mlp_norm, w_up=v_w_up, w_down=v_w_down,
               final_norm=v_final_norm)
    xi, yi, ci = _place()
    two_d = lambda a: a.reshape(a.shape[-2:]) if a.ndim == 3 else a.reshape(1, a.shape[-1])

    small = jnp.concatenate([w[n].reshape(-1) for n in SMALL_SHARDED])
    small = jnp.pad(small, (0, SMALL_PACK_ROWS * 128 - small.shape[0])).reshape(SMALL_PACK_ROWS, 128)
    shard = {n: two_d(w[n]).astype(BF16) for n in MATS}
    for n in ("w_in", "w_up"):
        shard[n] = shard[n].T
    vec = {n: two_d(w[n]) for n in SMALL if n not in SMALL_SHARDED}
    place = jnp.stack([2 * xi + yi, ci]).astype(jnp.int32)
    loss8, grad_x, small_acc, own, from_chips = _step(x[0], mem[0], loss_target[0], shard, small, vec, place)

    tot = _small_reduce(small_acc, loss8)
    small_out = _small_adamw(tot, *[{n: two_d(d[n]) for n in SMALL} for d in (w, mom, var)])
    loss = tot[LOSS_ROW, 0]

    out_g, out_d, out_m, out_v = {}, {}, {}, {}
    wmv = {n: [a.transpose(0, 2, 1) if n == "w_in" else a for a in (w[n], mom[n], var[n])] for n in MATS}
    for shape in dict.fromkeys(wmv[n][0].shape for n in MATS):
        names = [n for n in MATS if wmv[n][0].shape == shape]
        res = _adamw(*[[wmv[n][k] for n in names] for k in range(3)], [own[n] for n in names],
                     [from_chips[n] for n in names], "adamw_" + "_".join(names), grads_transposed=names == ["w_up"])
        for n, r in zip(names, res):
            out_g[n], out_d[n], out_m[n], out_v[n] = [a.transpose(0, 2, 1) for a in r] if n == "w_in" else r
    for n in SMALL:
        out_g[n], out_d[n], out_m[n], out_v[n] = [a.reshape(w[n].shape) for a in small_out[n]]

    return (loss, grad_x[None], *[out_g[n] for n in WEIGHTS], *[out_d[n] for n in WEIGHTS],
            *[out_m[n] for n in WEIGHTS], *[out_v[n] for n in WEIGHTS])
```

```python
import functools
import itertools

import jax
import jax.numpy as jnp
from jax import lax
from jax.experimental import pallas as pl
from jax.experimental.pallas import tpu as pltpu

F32 = jnp.float32
BF16 = jnp.bfloat16

D = 1024
CW = 512
GK = 256
GV = 512
NH = 4
CH = 64
LR = 16
NMEM = 256
XD = 256
FF = 4096
ZW = 3104
ZC = 3200
EPS = 1e-6
NDEV = 8

ZB_CB, ZB_CC, ZB_CU, ZB_V, ZB_G = 0, 1, 2, 4, 5
ZB_Q, ZB_K = 6, 7
ZB_LR = 24

TM = 512
TM_MLP = 256
TM_MLP_FWD = 512
TF = 512
TB = 512
TB_BWD = 512
TT = 2048
VMEM_LIMIT = 56 * 1024 * 1024

ADAM_LR, ADAM_B1, ADAM_B2, ADAM_EPS, ADAM_WD, ADAM_STEP = 0.001, 0.9, 0.999, 1e-08, 0.01, 10

XKV_SHARD = 2 * D // NDEV

MESH = pl.DeviceIdType.MESH


def _cparams(sem):
    return pltpu.CompilerParams(dimension_semantics=sem, vmem_limit_bytes=VMEM_LIMIT)


def _call(body, name, grid, in_specs, out_specs, out_shape, scratch, args, riders=()):
    n_in, n_out, n_scr = len(in_specs), len(out_specs), len(scratch)
    counts = [(len(r.arrays), len(r.out_shape), len(r.scratch)) for r in riders]

    def take(refs, pos, sizes):
        groups = []
        for size in sizes:
            groups.append(refs[pos:pos + size])
            pos += size
        return groups, pos

    def wrapped(*refs):
        ins, pos = refs[:n_in], n_in
        r_ins, pos = take(refs, pos, [c[0] for c in counts])
        outs, pos = refs[pos:pos + n_out], pos + n_out
        r_outs, pos = take(refs, pos, [c[1] for c in counts])
        scr, pos = refs[pos:pos + n_scr], pos + n_scr
        r_scr, pos = take(refs, pos, [c[2] for c in counts])
        ids = [pl.program_id(d) for d in range(len(grid))]
        first = functools.reduce(lambda a, b: a & b, [i == 0 for i in ids])
        last = functools.reduce(lambda a, b: a & b, [i == g - 1 for i, g in zip(ids, grid)])

        @pl.when(first)
        def _():
            for r, a, b, c in zip(riders, r_ins, r_outs, r_scr):
                r.start(a, b, c)

        body(*ins, *outs, *scr)

        if any(r.relay for r in riders):
            at = [max(g - 2, 0) for g in grid]

            @pl.when(functools.reduce(lambda a, b: a & b, [i == s for i, s in zip(ids, at)]))
            def _():
                for r, a, b, c in zip(riders, r_ins, r_outs, r_scr):
                    if r.relay:
                        r.relay(a, b, c)

        @pl.when(last)
        def _():
            for r, a, b, c in zip(riders, r_ins, r_outs, r_scr):
                r.finish(a, b, c)

    hbm = pl.BlockSpec(memory_space=pltpu.HBM)
    r_args = [a for r in riders for a in r.arrays]
    r_shapes = [s for r in riders for s in r.out_shape]
    return pl.pallas_call(
        wrapped if riders else body, name=name, grid=grid, in_specs=list(in_specs) + [hbm] * len(r_args),
        out_specs=list(out_specs) + [hbm] * len(r_shapes), out_shape=list(out_shape) + r_shapes,
        scratch_shapes=list(scratch) + [s for r in riders for s in r.scratch],
        compiler_params=_cparams(("arbitrary",) * len(grid)))(*args, *r_args)


def _dot(a, b):
    return jnp.dot(a.astype(BF16), b.astype(BF16), preferred_element_type=F32)


def _dot_nt(a, b):
    return lax.dot_general(a.astype(BF16), b.astype(BF16), (((1,), (1,)), ((), ())), preferred_element_type=F32)


def _dot_tn(a, b):
    return lax.dot_general(a.astype(BF16), b.astype(BF16), (((0,), (0,)), ((), ())), preferred_element_type=F32)


def _split(x, n):
    parts = []
    for _ in range(n):
        p = x.astype(BF16)
        parts.append(p)
        x = x - p.astype(F32)
    return parts


def _dot_exact_lhs(m, x, n):
    return functools.reduce(lambda a, b: a + b, [jnp.dot(m, p, preferred_element_type=F32) for p in _split(x, n)])


def _dot_exact_rhs(x, m, n):
    return functools.reduce(lambda a, b: a + b, [jnp.dot(p, m, preferred_element_type=F32) for p in _split(x, n)])


def _rms(x, g):
    r = lax.rsqrt(jnp.mean(x * x, axis=-1, keepdims=True) + EPS)
    return x * r * g, r


def _rms_bwd(x, r, g, dy):
    xr = x * r
    u = dy * g
    dx = r * (u - xr * jnp.mean(u * xr, axis=-1, keepdims=True))
    return dx, jnp.sum(dy * xr, axis=0, keepdims=True)


def _iota(shape, dim):
    return lax.broadcasted_iota(jnp.int32, shape, dim)


def _sigmoid(x):
    return 1.0 / (1.0 + jnp.exp(-x))


def _acc_rows(ref, row):
    ref[...] += jnp.broadcast_to(row, ref.shape)


def _inproj(x, g, w_t, riders=()):
    t = x.shape[0]
    tm = min(TM, t)

    def body(x_ref, g_ref, w_ref, z_ref, h_ref):
        h, _ = _rms(x_ref[...], g_ref[...])
        hb = h.astype(BF16)
        h_ref[...] = hb
        z_ref[...] = _dot_nt(hb, w_ref[...])

    return _call(
        body, "inproj", (t // tm,),
        [pl.BlockSpec((tm, D), lambda i: (i, 0)), pl.BlockSpec((1, D), lambda i: (0, 0)),
         pl.BlockSpec((ZC, D), lambda i: (0, 0))],
        [pl.BlockSpec((tm, ZC), lambda i: (i, 0)), pl.BlockSpec((tm, D), lambda i: (i, 0))],
        [jax.ShapeDtypeStruct((t, ZC), F32), jax.ShapeDtypeStruct((t, D), BF16)], [], (x, g, w_t), riders)


def _kv_proj(mem, g, w):
    def body(m_ref, g_ref, w_ref, kv_ref, mn_ref):
        mn, _ = _rms(m_ref[...], g_ref[...])
        mb = mn.astype(BF16)
        mn_ref[...] = mb
        for j in range(NDEV):
            kv_ref[:, j * XKV_SHARD:(j + 1) * XKV_SHARD] = jnp.dot(mb, w_ref[j], preferred_element_type=F32)

    return pl.pallas_call(
        body, name="kv_proj",
        out_shape=[jax.ShapeDtypeStruct((NMEM, 2 * D), F32), jax.ShapeDtypeStruct((NMEM, D), BF16)],
        compiler_params=pltpu.CompilerParams(vmem_limit_bytes=VMEM_LIMIT))(mem, g, w)


def _softmax_head(qb, kb):
    s = _dot_nt(qb, kb) * (1.0 / 16.0)
    e = jnp.exp(s - jnp.max(s, axis=-1, keepdims=True))
    return e / jnp.sum(e, axis=-1, keepdims=True)


def _attn_fwd(x, z, o_f, o_b, conv_w, conv_norm, gla_norm4, w_out, g, w_xq, kb, vb, w_xo):
    t = x.shape[0]
    tm = min(TM, t)
    nblk = t // tm
    jmap = lambda i: i

    def body(x_ref, zq_ref, zk_ref, zv_ref, zg_ref, cb_ref, cc_ref, cu_ref, ccp_ref, ccn_ref, cup_ref, cun_ref,
             of_ref, ob_ref, cw_ref, cn_ref, gn_ref, wo_ref, g_ref, wq_ref, k_ref, v_ref, wx_ref,
             x1_ref, x2_ref, xn_ref, q_ref, a_ref, y_ref, opre_ref):
        j = pl.program_id(0)
        zv = zv_ref[...]
        sb = _head_sum((zq_ref[...] * 0.125) * zk_ref[...], 64, 128)
        o_pre = of_ref[...] + ob_ref[...] - sb * zv
        opre_ref[...] = o_pre
        on, _ = _head_norm(o_pre)
        zg = zg_ref[...]
        y_ref[:, CW:] = (on * gn_ref[...] * (zg * _sigmoid(zg))).astype(BF16)
        cb = cb_ref[...]
        _, _, _, conv = _conv_parts(cb, cc_ref[...], cu_ref[...], ccp_ref[pl.ds(7, 1), :], cup_ref[pl.ds(7, 1), :],
                                    ccn_ref[pl.ds(0, 1), :], cun_ref[pl.ds(0, 1), :], cw_ref, j == 0,
                                    j == nblk - 1, tm)
        yc = cb * conv
        gm = _group_sum(yc * yc) * (1.0 / 64.0)
        y_ref[:, :CW] = (yc * lax.rsqrt(gm + EPS) * cn_ref[...]).astype(BF16)

        x1 = x_ref[...] + jnp.dot(y_ref[...], wo_ref[...], preferred_element_type=F32)
        x1_ref[...] = x1
        xn, _ = _rms(x1, g_ref[...])
        xb = xn.astype(BF16)
        xn_ref[...] = xb
        qb = jnp.dot(xb, wq_ref[...], preferred_element_type=F32).astype(BF16)
        q_ref[...] = qb
        heads = [slice(h * XD, (h + 1) * XD) for h in range(NH)]
        ps = [_softmax_head(qb[:, hs], k_ref[:, hs]) for hs in heads]
        for hs, p in zip(heads, ps):
            a_ref[:, hs] = _dot(p, v_ref[:, hs]).astype(BF16)
        x2_ref[...] = x1 + jnp.dot(a_ref[...], wx_ref[...], preferred_element_type=F32)

    tok = lambda i: (i, 0)
    full = lambda i: (0, 0)
    once = pl.Buffered(1)
    tokd, tokv = pl.BlockSpec((tm, D), tok), pl.BlockSpec((tm, GV), tok)
    weight = pl.BlockSpec((D, D), full, pipeline_mode=once)
    ccp, ccn = _halo_specs(tm, nblk, t, ZB_CC, jmap)
    cup, cun = _halo_specs(tm, nblk, t, ZB_CU, jmap)
    in_specs = [tokd, _zspec(tm, GK, ZB_Q, jmap), _zspec(tm, GK, ZB_K, jmap), _zspec(tm, GV, ZB_V, jmap),
                _zspec(tm, GV, ZB_G, jmap), _zspec(tm, CW, ZB_CB, jmap), _zspec(tm, CW, ZB_CC, jmap),
                _zspec(tm, CW, ZB_CU, jmap), ccp, ccn, cup, cun, tokv, tokv,
                pl.BlockSpec((3, CW), full), pl.BlockSpec((1, CW), full), pl.BlockSpec((1, GV), full),
                weight, pl.BlockSpec((1, D), full), weight, pl.BlockSpec((NMEM, D), full),
                pl.BlockSpec((NMEM, D), full), weight]
    return pl.pallas_call(
        body, name="attn_fwd", grid=(nblk,), in_specs=in_specs, out_specs=[tokd] * 6 + [tokv],
        out_shape=[jax.ShapeDtypeStruct((t, D), F32), jax.ShapeDtypeStruct((t, D), F32),
                   jax.ShapeDtypeStruct((t, D), BF16), jax.ShapeDtypeStruct((t, D), BF16),
                   jax.ShapeDtypeStruct((t, D), BF16), jax.ShapeDtypeStruct((t, D), BF16),
                   jax.ShapeDtypeStruct((t, GV), F32)],
        compiler_params=_cparams(("arbitrary",)))(
            x, z, z, z, z, z, z, z, z, z, z, z, o_f, o_b, conv_w, conv_norm, gla_norm4, w_out, g, w_xq, kb, vb, w_xo)


def _mlp_fwd(x2, g, w_up_t, w_down, fg, target):
    t = x2.shape[0]
    tm = min(TM_MLP_FWD, t)

    def body(x_ref, g_ref, wu_ref, wd_ref, fg_ref, t_ref, h1_ref, xn_ref, dx_ref, dxb_ref, loss_ref, dfg_ref, ab):
        @pl.when(pl.program_id(0) == 0)
        def _():
            loss_ref[...] = jnp.zeros_like(loss_ref)
            dfg_ref[...] = jnp.zeros_like(dfg_ref)

        x = x_ref[...]
        xn, _ = _rms(x, g_ref[...])
        xnb = xn.astype(BF16)
        xn_ref[...] = xnb
        for q in range(FF // TF):
            cols = slice(q * TF, (q + 1) * TF)
            h1 = _dot_nt(xnb, wu_ref[cols, :])
            h1_ref[:, cols] = h1.astype(BF16)
            hr = jnp.maximum(h1, 0.0)
            ab[:, cols] = (hr * hr).astype(BF16)
        x3 = x + jnp.dot(ab[...], wd_ref[...], preferred_element_type=F32)
        y, r = _rms(x3, fg_ref[...])
        e = y - t_ref[...]
        row = jnp.mean(e * e, axis=-1, keepdims=True)
        _acc_rows(loss_ref, 0.5 * jnp.sum(row, axis=0, keepdims=True))
        dx, dfg = _rms_bwd(x3, r, fg_ref[...], e * (1.0 / D))
        dx_ref[...] = dx
        dxb_ref[...] = dx.astype(BF16)
        _acc_rows(dfg_ref, dfg)

    tok = lambda i: (i, 0)
    full = lambda i: (0, 0)
    once = pl.Buffered(1)
    return pl.pallas_call(
        body, name="mlp_fwd", grid=(t // tm,),
        in_specs=[pl.BlockSpec((tm, D), tok), pl.BlockSpec((1, D), full),
                  pl.BlockSpec((FF, D), full, pipeline_mode=once), pl.BlockSpec((FF, D), full, pipeline_mode=once),
                  pl.BlockSpec((1, D), full), pl.BlockSpec((tm, D), tok)],
        out_specs=[pl.BlockSpec((tm, FF), tok), pl.BlockSpec((tm, D), tok), pl.BlockSpec((tm, D), tok),
                   pl.BlockSpec((tm, D), tok), pl.BlockSpec((8, 128), full), pl.BlockSpec((8, D), full)],
        out_shape=[jax.ShapeDtypeStruct((t, FF), BF16), jax.ShapeDtypeStruct((t, D), BF16),
                   jax.ShapeDtypeStruct((t, D), F32), jax.ShapeDtypeStruct((t, D), BF16),
                   jax.ShapeDtypeStruct((8, 128), F32), jax.ShapeDtypeStruct((8, D), F32)],
        scratch_shapes=[pltpu.VMEM((tm, FF), BF16)],
        compiler_params=_cparams(("arbitrary",)))(x2, g, w_up_t, w_down, fg, target)


def _mlp_bwd(dx3, dx3b, h1b, w_down, w_up_t, x2, g):
    t = x2.shape[0]
    tm = min(TM_MLP, t)

    def body(dx_ref, dxb_ref, h1_ref, wd_ref, wu_ref, x_ref, g_ref, a_ref, dh_ref, dx2_ref, dx2b_ref, dg_ref):
        @pl.when(pl.program_id(0) == 0)
        def _():
            dg_ref[...] = jnp.zeros_like(dg_ref)

        for q in range(FF // TF):
            cols = slice(q * TF, (q + 1) * TF)
            hr = jnp.maximum(h1_ref[:, cols].astype(F32), 0.0)
            da = _dot_nt(dxb_ref[...], wd_ref[cols, :])
            a_ref[:, cols] = (hr * hr).astype(BF16)
            dh_ref[:, cols] = (da * 2.0 * hr).astype(BF16)
        dxn = jnp.dot(dh_ref[...], wu_ref[...], preferred_element_type=F32)
        x = x_ref[...]
        r = lax.rsqrt(jnp.mean(x * x, axis=-1, keepdims=True) + EPS)
        dx, dg = _rms_bwd(x, r, g_ref[...], dxn)
        dx2 = dx_ref[...] + dx
        dx2_ref[...] = dx2
        dx2b_ref[...] = dx2.astype(BF16)
        _acc_rows(dg_ref, dg)

    tok = lambda i: (i, 0)
    full = lambda i: (0, 0)
    once = pl.Buffered(1)
    return pl.pallas_call(
        body, name="mlp_bwd", grid=(t // tm,),
        in_specs=[pl.BlockSpec((tm, D), tok), pl.BlockSpec((tm, D), tok), pl.BlockSpec((tm, FF), tok),
                  pl.BlockSpec((FF, D), full, pipeline_mode=once), pl.BlockSpec((FF, D), full, pipeline_mode=once),
                  pl.BlockSpec((tm, D), tok), pl.BlockSpec((1, D), full)],
        out_specs=[pl.BlockSpec((tm, FF), tok), pl.BlockSpec((tm, FF), tok), pl.BlockSpec((tm, D), tok),
                   pl.BlockSpec((tm, D), tok), pl.BlockSpec((8, D), full)],
        out_shape=[jax.ShapeDtypeStruct((t, FF), BF16), jax.ShapeDtypeStruct((t, FF), BF16),
                   jax.ShapeDtypeStruct((t, D), F32), jax.ShapeDtypeStruct((t, D), BF16),
                   jax.ShapeDtypeStruct((8, D), F32)],
        compiler_params=_cparams(("arbitrary",)))(dx3, dx3b, h1b, w_down, w_up_t, x2, g)


def _attn_bwd(x1, dx2, dx2b, qb, kb, vb, w_xo, w_xq, w_out, g, riders=()):
    t = x1.shape[0]
    tm = min(TM, t)

    def body(x_ref, dx2_ref, dx2b_ref, q_ref, k_ref, v_ref, wx_ref, wq_ref, wo_ref, g_ref,
             dx1_ref, dx1b_ref, dy_ref, dq_ref, dkv_ref, dg_ref):
        @pl.when(pl.program_id(0) == 0)
        def _():
            dkv_ref[...] = jnp.zeros_like(dkv_ref)
            dg_ref[...] = jnp.zeros_like(dg_ref)

        datt = _dot_nt(dx2b_ref[...], wx_ref[...]).astype(BF16)
        heads = [slice(h * XD, (h + 1) * XD) for h in range(NH)]
        ps = [_softmax_head(q_ref[:, hs], k_ref[:, hs]) for hs in heads]
        dps = [_dot_nt(datt[:, hs], v_ref[:, hs]) for hs in heads]
        dss = [(p * (dp - jnp.sum(dp * p, axis=-1, keepdims=True)) * (1.0 / 16.0)).astype(BF16)
               for p, dp in zip(ps, dps)]
        for h, (hs, p, ds) in enumerate(zip(heads, ps, dss)):
            dq_ref[:, hs] = _dot(ds, k_ref[:, hs]).astype(BF16)
            dkv_ref[:, hs] += _dot_tn(ds, q_ref[:, hs])
            dkv_ref[:, D + h * XD:D + (h + 1) * XD] += _dot_tn(p, datt[:, hs])
        dxn = _dot_nt(dq_ref[...], wq_ref[...])
        x = x_ref[...]
        r = lax.rsqrt(jnp.mean(x * x, axis=-1, keepdims=True) + EPS)
        dx, dg = _rms_bwd(x, r, g_ref[...], dxn)
        dx1 = dx2_ref[...] + dx
        dx1_ref[...] = dx1
        dx1b = dx1.astype(BF16)
        dx1b_ref[...] = dx1b
        dy_ref[...] = _dot_nt(dx1b, wo_ref[...])
        _acc_rows(dg_ref, dg)

    tok = lambda i: (i, 0)
    full = lambda i: (0, 0)
    return _call(
        body, "attn_bwd", (t // tm,),
        [pl.BlockSpec((tm, D), tok), pl.BlockSpec((tm, D), tok), pl.BlockSpec((tm, D), tok),
         pl.BlockSpec((tm, D), tok), pl.BlockSpec((NMEM, D), full), pl.BlockSpec((NMEM, D), full),
         pl.BlockSpec((D, D), full), pl.BlockSpec((D, D), full), pl.BlockSpec((D, D), full),
         pl.BlockSpec((1, D), full)],
        [pl.BlockSpec((tm, D), tok), pl.BlockSpec((tm, D), tok), pl.BlockSpec((tm, D), tok),
         pl.BlockSpec((tm, D), tok), pl.BlockSpec((NMEM, 2 * D), full), pl.BlockSpec((8, D), full)],
        [jax.ShapeDtypeStruct((t, D), F32), jax.ShapeDtypeStruct((t, D), BF16),
         jax.ShapeDtypeStruct((t, D), F32), jax.ShapeDtypeStruct((t, D), BF16),
         jax.ShapeDtypeStruct((NMEM, 2 * D), F32), jax.ShapeDtypeStruct((8, D), F32)], [],
        (x1, dx2, dx2b, qb, kb, vb, w_xo, w_xq, w_out, g), riders)


def _kv_bwd(dkv, memn, mem, g, w):
    def body(dkv_ref, mn_ref, m_ref, g_ref, w_ref, dw_ref, dg_ref):
        dkvb = dkv_ref[...].astype(BF16)
        dmn = jnp.zeros((NMEM, D), F32)
        for j in range(NDEV):
            cols = slice(j * XKV_SHARD, (j + 1) * XKV_SHARD)
            dw_ref[j] = _dot_tn(mn_ref[...], dkvb[:, cols])
            dmn += _dot_nt(dkvb[:, cols], w_ref[j])
        m = m_ref[...]
        r = lax.rsqrt(jnp.mean(m * m, axis=-1, keepdims=True) + EPS)
        dg_ref[...] = jnp.broadcast_to(jnp.sum(dmn * m * r, axis=0, keepdims=True), dg_ref.shape)

    return pl.pallas_call(
        body, name="kv_bwd",
        out_shape=[jax.ShapeDtypeStruct((NDEV, D, XKV_SHARD), F32), jax.ShapeDtypeStruct((8, D), F32)],
        compiler_params=pltpu.CompilerParams(vmem_limit_bytes=VMEM_LIMIT))(dkv, memn, mem, g, w)


def _inproj_bwd(dz, w_t, x, dx1, g, riders=()):
    t = x.shape[0]
    tm = min(TM, t)

    def body(dz_ref, w_ref, x_ref, dx1_ref, g_ref, gx_ref, dg_ref):
        @pl.when(pl.program_id(0) == 0)
        def _():
            dg_ref[...] = jnp.zeros_like(dg_ref)

        dh = jnp.dot(dz_ref[...], w_ref[...], preferred_element_type=F32)
        x = x_ref[...]
        r = lax.rsqrt(jnp.mean(x * x, axis=-1, keepdims=True) + EPS)
        dx, dg = _rms_bwd(x, r, g_ref[...], dh)
        gx_ref[...] = dx1_ref[...] + dx
        _acc_rows(dg_ref, dg)

    tok = lambda i: (i, 0)
    full = lambda i: (0, 0)
    return _call(
        body, "inproj_bwd", (t // tm,),
        [pl.BlockSpec((tm, ZC), tok), pl.BlockSpec((ZC, D), full), pl.BlockSpec((tm, D), tok),
         pl.BlockSpec((tm, D), tok), pl.BlockSpec((1, D), full)],
        [pl.BlockSpec((tm, D), tok), pl.BlockSpec((8, D), full)],
        [jax.ShapeDtypeStruct((t, D), F32), jax.ShapeDtypeStruct((8, D), F32)], [], (dz, w_t, x, dx1, g), riders)


def _matmul_tn(a, b, name, rows=None, riders=()):
    t, k = a.shape
    n = b.shape[1]
    tk, tn = [1024 if size % 1024 == 0 else 640 for size in (k, n)]
    tt = min(TT, t)
    rows = rows or k

    def body(a_ref, b_ref, o_ref):
        @pl.when(pl.program_id(2) == 0)
        def _():
            o_ref[...] = jnp.zeros_like(o_ref)

        o_ref[...] += _dot_tn(a_ref[...], b_ref[...])

    return _call(
        body, name, (k // tk, n // tn, t // tt),
        [pl.BlockSpec((tt, tk), lambda i, j, s: (s, i)), pl.BlockSpec((tt, tn), lambda i, j, s: (s, j))],
        [pl.BlockSpec((tk, tn), lambda i, j, s: (i, j))], [jax.ShapeDtypeStruct((rows, n), F32)], [], (a, b), riders)


def _lane_head(shape, dim, shift):
    return _iota(shape, dim) >> shift


CUM_ROWS = 128


def _chunk_cumsum(x, upper, n):
    r, c = _iota((CUM_ROWS, CUM_ROWS), 0), _iota((CUM_ROWS, CUM_ROWS), 1)
    tri = (c >= r) if upper else (c <= r)
    cum = jnp.where(((r >> 6) == (c >> 6)) & tri, 1.0, 0.0).astype(BF16)
    return jnp.concatenate([_dot_exact_lhs(cum, x[g:g + CUM_ROWS], n) for g in range(0, x.shape[0], CUM_ROWS)],
                           axis=0)


def _gla_recompute(q_raw, k, lr, wpad, bias, rev, tb):
    pre = _dot(lr, wpad) + bias
    la = (jnp.minimum(pre, 0.0) - jnp.log(1.0 + jnp.exp(-jnp.abs(pre)))) * (1.0 / 16.0)
    b = _chunk_cumsum(la, rev, 3)
    e, ei = jnp.exp(b), jnp.exp(-b)
    qt = (q_raw * 0.125) * e
    kt = k * ei
    return pre, b, e, ei, qt, kt


def _stack_heads(x, shift):
    head = _lane_head(x.shape, 1, shift)
    return jnp.concatenate([jnp.where(head == h, x, 0.0) for h in range(NH)], axis=0).astype(BF16)


def _fold_heads(x, shift):
    head = _lane_head((CH, x.shape[1]), 1, shift)
    return functools.reduce(lambda a, b: a + b,
                            [jnp.where(head == h, x[h * CH:(h + 1) * CH], 0.0) for h in range(NH)])


def _wide_mask(rev):
    r, s = _iota((CH, NH * CH), 0), _iota((CH, NH * CH), 1) & (CH - 1)
    return (s >= r) if rev else (s <= r)


def _rows_by_head(x):
    w = x.shape[1] // NH
    return jnp.concatenate([x[:, h * w:(h + 1) * w] for h in range(NH)], axis=0)


def _lanes_by_head(x):
    return jnp.concatenate([x[h * CH:(h + 1) * CH] for h in range(NH)], axis=1)


def _state_compact(xt):
    head = _lane_head((128, GK), 1, 6)
    return functools.reduce(lambda a, b: a + b,
                            [jnp.where(head == h, xt[h * 128:(h + 1) * 128], 0.0) for h in range(NH)])


def _conv_parts(cb, cc, cu, ccp, cup, ccn, cun, cw_ref, first, last, tb):
    h = cc * cu
    hp = jnp.where(first, 0.0, ccp * cup)
    hn = jnp.where(last, 0.0, ccn * cun)
    rows = _iota(h.shape, 0)
    h_m1 = jnp.where(rows == 0, hp, pltpu.roll(h, 1, 0))
    h_p1 = jnp.where(rows == tb - 1, hn, pltpu.roll(h, tb - 1, 0))
    conv = cw_ref[pl.ds(0, 1), :] * h_m1 + cw_ref[pl.ds(1, 1), :] * h + cw_ref[pl.ds(2, 1), :] * h_p1
    return h, h_m1, h_p1, conv


def _head_sum(x, w_in, w_out):
    shape, sh_in, sh_out = (2 * w_in, 2 * w_out), w_in.bit_length() - 1, w_out.bit_length() - 1
    sel = jnp.where((_iota(shape, 0) >> sh_in) == (_iota(shape, 1) >> sh_out), 1.0, 0.0).astype(BF16)
    return jnp.concatenate([_dot_exact_rhs(x[:, s:s + 2 * w_in], sel, 2) for s in range(0, NH * w_in, 2 * w_in)],
                           axis=1)


def _group_sum(x):
    ones = jnp.where((_iota((128, 128), 0) >> 6) == (_iota((128, 128), 1) >> 6), 1.0, 0.0).astype(BF16)
    return jnp.concatenate([_dot_exact_rhs(x[:, s:s + 128], ones, 2) for s in range(0, x.shape[1], 128)], axis=1)


def _head_norm(o):
    ons, rs = [], []
    for h in range(NH):
        slab = o[:, h * 128:(h + 1) * 128]
        r = lax.rsqrt(jnp.mean(slab * slab, axis=-1, keepdims=True) + EPS)
        ons.append(slab * r)
        rs.append(jnp.broadcast_to(r, slab.shape))
    return jnp.concatenate(ons, axis=1), jnp.concatenate(rs, axis=1)


def _zspec(tb, width, blk, jmap):
    return pl.BlockSpec((tb, width), lambda i: (jmap(i), blk))


def _halo_specs(tb, nblk, t, blk, jmap):
    prev = pl.BlockSpec((8, CW), lambda i: (jnp.maximum(jmap(i) * (tb // 8) - 1, 0), blk))
    nxt = pl.BlockSpec((8, CW), lambda i: (jnp.minimum((jmap(i) + 1) * (tb // 8), t // 8 - 1), blk))
    return prev, nxt


def _gla_fwd_block(q_ref, k_ref, v_ref, lr_ref, w_ref, bias_ref, o_ref, sd_ref, st, b_scr, rev, tb):
    nb = tb // CH
    _, b, _, _, qt, kt = _gla_recompute(q_ref[...], k_ref[...], lr_ref[...], w_ref[...], bias_ref[...], rev, tb)
    v = v_ref[...]
    b_scr[...] = b
    yield
    maskw = _wide_mask(rev)
    order = list(reversed(range(nb))) if rev else list(range(nb))
    rows = [slice(c * CH, (c + 1) * CH) for c in range(nb)]
    state = st[...]
    for c in order:
        gdec = jnp.exp(b_scr[pl.ds(c * CH + (0 if rev else CH - 1), 1), :])
        sd_ref[c] = state
        a = jnp.where(maskw, _dot_nt(qt[rows[c]], _stack_heads(kt[rows[c]], 6)), 0.0)
        o_inter = _lanes_by_head(_dot_nt(_stack_heads(qt[rows[c]], 6), state))
        o_ref[pl.ds(c * CH, CH), :] = _dot(a, _stack_heads(v[rows[c]], 7)) + o_inter
        state = state * gdec + _state_compact(_dot_tn(v[rows[c]], kt[rows[c]] * gdec))
        yield
    st[...] = state
    yield


def _gla_fwd(z, waf_pad, b_af, wab_pad, b_ab, riders=()):
    t = z.shape[0]
    tb = min(TB, t)
    nblk, nb = t // tb, tb // CH
    jmaps = (lambda i: i, lambda i: nblk - 1 - i)

    def body(qf, kf, vf, lrf, qr, kr, vr, lrr, wf, bf, wr, br, of_ref, sdf_ref, or_ref, sdr_ref,
             st_f, st_r, b_f, b_r):
        @pl.when(pl.program_id(0) == 0)
        def _():
            st_f[...] = jnp.zeros_like(st_f)
            st_r[...] = jnp.zeros_like(st_r)

        for _ in zip(_gla_fwd_block(qf, kf, vf, lrf, wf, bf, of_ref, sdf_ref, st_f, b_f, False, tb),
                     _gla_fwd_block(qr, kr, vr, lrr, wr, br, or_ref, sdr_ref, st_r, b_r, True, tb)):
            pass

    full = lambda i: (0, 0)
    zspecs = [s for jm in jmaps for s in (_zspec(tb, GK, ZB_Q, jm), _zspec(tb, GK, ZB_K, jm),
                                         _zspec(tb, GV, ZB_V, jm), _zspec(tb, 128, ZB_LR, jm))]
    wspecs = [pl.BlockSpec((128, GK), full), pl.BlockSpec((1, GK), full)] * 2
    out_specs = [s for jm in jmaps for s in (pl.BlockSpec((tb, GV), lambda i, jm=jm: (jm(i), 0)),
                                             pl.BlockSpec((nb, 128, GK), lambda i, jm=jm: (jm(i), 0, 0)))]
    out_shape = [jax.ShapeDtypeStruct((t, GV), F32), jax.ShapeDtypeStruct((t // CH, 128, GK), F32)] * 2
    scratch = [pltpu.VMEM((128, GK), F32), pltpu.VMEM((128, GK), F32), pltpu.VMEM((tb, GK), F32),
               pltpu.VMEM((tb, GK), F32)]
    return _call(body, "gla_fwd", (nblk,), zspecs + wspecs, out_specs, out_shape, scratch,
                 [z] * 8 + [waf_pad, b_af, wab_pad, b_ab], riders)


def _gla_bwd_chunks(do_ref, sd_ref, dst, b_scr, db_scr, dq_ref, dk_ref, dv_ref, qt, kt, e, ei, v, rev, nb):
    maskw = _wide_mask(rev)
    for c in (range(nb) if rev else reversed(range(nb))):
        sl = slice(c * CH, (c + 1) * CH)
        grow = c * CH + (0 if rev else CH - 1)
        gdec = jnp.exp(b_scr[pl.ds(grow, 1), :])
        qt_c, kt_c, v_c, do_c = qt[sl], kt[sl], v[sl], do_ref[pl.ds(c * CH, CH), :]
        s_in, ds_out = sd_ref[c], dst[...]
        kbd, vbd = _stack_heads(kt_c, 6), _stack_heads(v_c, 7)
        a = jnp.where(maskw, _dot_nt(qt_c, kbd), 0.0)
        da = jnp.where(maskw, _dot_nt(do_c, vbd), 0.0)
        dv_ref[pl.ds(c * CH, CH), :] = (_fold_heads(_dot_tn(a, do_c), 7)
                                        + _lanes_by_head(_dot_nt(_stack_heads(kt_c * gdec, 6), ds_out)))
        dqt = _dot(da, kbd) + _fold_heads(_dot(_rows_by_head(do_c), s_in), 6)
        dkh = _fold_heads(_dot(_rows_by_head(v_c), ds_out), 6)
        da_do = jnp.concatenate([da.astype(BF16), do_c.astype(BF16)], axis=1)
        both = _dot_tn(da_do, qt_c)
        dkt = _fold_heads(both[:NH * CH], 6) + dkh * gdec
        dg = jnp.sum(ds_out * s_in, axis=0, keepdims=True) + jnp.sum(kt_c * dkh, axis=0, keepdims=True)
        db_scr[pl.ds(c * CH, CH), :] = dqt * qt_c - dkt * kt_c
        db_scr[pl.ds(grow, 1), :] += dg * gdec
        dq_ref[pl.ds(c * CH, CH), :] = dqt * e[sl] * 0.125
        dk_ref[pl.ds(c * CH, CH), :] = dkt * ei[sl]
        dst[...] = ds_out * gdec + _state_compact(both[NH * CH:])
        yield


def _gate_bwd(db, pre, lr, wpad, rev, tb):
    dla = _chunk_cumsum(db, not rev, 2)
    dpre = dla * (1.0 / 16.0) / (1.0 + jnp.exp(pre))
    return dpre, _dot_nt(dpre, wpad), _dot_tn(lr, dpre)


def _gla_bwd_first(z, dy, o_pre, sd, wpad, bias, conv_w, conv_norm, gla_norm4, riders=()):
    t = z.shape[0]
    tb = min(TB_BWD, t)
    nblk, nb = t // tb, tb // CH
    jmap = lambda i: nblk - 1 - i

    def body(q_ref, k_ref, v_ref, lr_ref, g_ref, cb_ref, cc_ref, cu_ref, ccp_ref, ccn_ref, cup_ref, cun_ref,
             dy_ref, opre_ref, sd_ref, w_ref, bias_ref, cw_ref, cn_ref, gn_ref,
             do_ref, dq_ref, dk_ref, dv_ref, dlr_ref, dzg_ref, dzcb_ref, dconv_ref,
             dw_ref, dbias_ref, dcw_ref, dcn_ref, dgn_ref, dst, b_scr, db_scr):
        i = pl.program_id(0)
        j = jmap(i)

        @pl.when(i == 0)
        def _():
            dst[...] = jnp.zeros_like(dst)
            for ref in (dw_ref, dbias_ref, dcw_ref, dcn_ref, dgn_ref):
                ref[...] = jnp.zeros_like(ref)

        dyg = dy_ref[:, CW:]
        g = g_ref[...]
        sig = _sigmoid(g)
        on, rr = _head_norm(opre_ref[...])
        gn = gn_ref[...]
        dzg_ref[...] = (dyg * on * gn * (sig * (1.0 + g * (1.0 - sig)))).astype(BF16)
        don = dyg * (g * sig)
        _acc_rows(dgn_ref, jnp.sum(don * on, axis=0, keepdims=True))
        u = don * gn
        uo = u * on
        mean_uo = jnp.concatenate(
            [jnp.broadcast_to(jnp.mean(uo[:, h * 128:(h + 1) * 128], axis=-1, keepdims=True), (tb, 128))
             for h in range(NH)], axis=1)
        do_ref[...] = rr * (u - on * mean_uo)

        def conv_branch():
            cb = cb_ref[...]
            h, h_m1, h_p1, conv = _conv_parts(cb, cc_ref[...], cu_ref[...], ccp_ref[pl.ds(7, 1), :],
                                              cup_ref[pl.ds(7, 1), :], ccn_ref[pl.ds(0, 1), :],
                                              cun_ref[pl.ds(0, 1), :], cw_ref, j == 0, j == nblk - 1, tb)
            yc = cb * conv
            yield
            rc = lax.rsqrt(_group_sum(yc * yc) * (1.0 / 64.0) + EPS)
            ycr = yc * rc
            yield
            dyn = dy_ref[:, :CW]
            _acc_rows(dcn_ref, jnp.sum(dyn * ycr, axis=0, keepdims=True))
            uc = dyn * cn_ref[...]
            yield
            dyc = rc * (uc - ycr * (_group_sum(uc * ycr) * (1.0 / 64.0)))
            dzcb_ref[...] = (dyc * conv).astype(BF16)
            yield
            dconv = dyc * cb
            dconv_ref[...] = dconv
            yield
            dcw_ref[pl.ds(0, 1), :] += jnp.sum(dconv * h_m1, axis=0, keepdims=True)
            dcw_ref[pl.ds(1, 1), :] += jnp.sum(dconv * h, axis=0, keepdims=True)
            dcw_ref[pl.ds(2, 1), :] += jnp.sum(dconv * h_p1, axis=0, keepdims=True)
            yield

        lr, wp = lr_ref[...], w_ref[...]
        pre, b, e, ei, qt, kt = _gla_recompute(q_ref[...], k_ref[...], lr, wp, bias_ref[...], False, tb)
        b_scr[...] = b
        for _ in itertools.zip_longest(
                _gla_bwd_chunks(do_ref, sd_ref, dst, b_scr, db_scr, dq_ref, dk_ref, dv_ref, qt, kt, e, ei, v_ref[...],
                                False, nb), conv_branch()):
            pass
        dpre, dlr, dw = _gate_bwd(db_scr[...], pre, lr, wp, False, tb)
        dlr_ref[...] = dlr
        dw_ref[...] += dw
        _acc_rows(dbias_ref, jnp.sum(dpre, axis=0, keepdims=True))

    full = lambda i: (0, 0)
    tokv = pl.BlockSpec((tb, GV), lambda i: (jmap(i), 0))
    tokk = pl.BlockSpec((tb, GK), lambda i: (jmap(i), 0))
    ccp, ccn = _halo_specs(tb, nblk, t, ZB_CC, jmap)
    cup, cun = _halo_specs(tb, nblk, t, ZB_CU, jmap)
    in_specs = [_zspec(tb, GK, ZB_Q, jmap), _zspec(tb, GK, ZB_K, jmap), _zspec(tb, GV, ZB_V, jmap),
                _zspec(tb, 128, ZB_LR, jmap), _zspec(tb, GV, ZB_G, jmap), _zspec(tb, CW, ZB_CB, jmap),
                _zspec(tb, CW, ZB_CC, jmap), _zspec(tb, CW, ZB_CU, jmap), ccp, ccn, cup, cun,
                pl.BlockSpec((tb, D), lambda i: (jmap(i), 0)), tokv,
                pl.BlockSpec((nb, 128, GK), lambda i: (jmap(i), 0, 0)), pl.BlockSpec((128, GK), full),
                pl.BlockSpec((1, GK), full), pl.BlockSpec((3, CW), full), pl.BlockSpec((1, CW), full),
                pl.BlockSpec((1, GV), full)]
    out_specs = [tokv, tokk, tokk, tokv, pl.BlockSpec((tb, 128), lambda i: (jmap(i), 0)), tokv, tokv, tokv,
                 pl.BlockSpec((128, GK), full), pl.BlockSpec((8, GK), full), pl.BlockSpec((8, CW), full),
                 pl.BlockSpec((8, CW), full), pl.BlockSpec((8, GV), full)]
    out_shape = [jax.ShapeDtypeStruct((t, GV), F32), jax.ShapeDtypeStruct((t, GK), F32),
                 jax.ShapeDtypeStruct((t, GK), F32), jax.ShapeDtypeStruct((t, GV), F32),
                 jax.ShapeDtypeStruct((t, 128), F32), jax.ShapeDtypeStruct((t, GV), BF16),
                 jax.ShapeDtypeStruct((t, CW), BF16), jax.ShapeDtypeStruct((t, CW), F32),
                 jax.ShapeDtypeStruct((128, GK), F32), jax.ShapeDtypeStruct((8, GK), F32),
                 jax.ShapeDtypeStruct((8, CW), F32), jax.ShapeDtypeStruct((8, CW), F32),
                 jax.ShapeDtypeStruct((8, GV), F32)]
    return _call(
        body, "gla_bwd_first", (nblk,), in_specs, out_specs, out_shape,
        [pltpu.VMEM((128, GK), F32), pltpu.VMEM((tb, GK), F32), pltpu.VMEM((tb, GK), F32)],
        (z, z, z, z, z, z, z, z, z, z, z, z, dy, o_pre, sd, wpad, bias, conv_w, conv_norm, gla_norm4), riders)


def _gla_bwd_second(z, do, sd, wpad, bias, dqa, dka, dva, dlra, dzg, dzcb, dconv, conv_w, riders=()):
    t = z.shape[0]
    tb = min(TB_BWD, t)
    nblk, nb = t // tb, tb // CH
    jmap = lambda i: i

    def body(q_ref, k_ref, v_ref, lr_ref, cc_ref, cu_ref, do_ref, sd_ref, w_ref, bias_ref, dqa_ref, dka_ref,
             dva_ref, dlra_ref, dzg_ref, dzcb_ref, dc_ref, dcp_ref, dcn_ref, cw_ref,
             dz_ref, dw_ref, dbias_ref, dst, b_scr, db_scr, dq_scr, dk_scr, dv_scr, sb_scr, dsk_scr):
        i = pl.program_id(0)

        @pl.when(i == 0)
        def _():
            dst[...] = jnp.zeros_like(dst)
            dw_ref[...] = jnp.zeros_like(dw_ref)
            dbias_ref[...] = jnp.zeros_like(dbias_ref)

        q_raw, k, v, lr, wp = q_ref[...], k_ref[...], v_ref[...], lr_ref[...], w_ref[...]
        pre, b, e, ei, qt, kt = _gla_recompute(q_raw, k, lr, wp, bias_ref[...], True, tb)
        b_scr[...] = b

        def token_local():
            dc = dc_ref[...]
            rows = _iota(dc.shape, 0)
            dprev = jnp.where(i == 0, 0.0, dcp_ref[pl.ds(7, 1), :])
            dnext = jnp.where(i == nblk - 1, 0.0, dcn_ref[pl.ds(0, 1), :])
            dc_m1 = jnp.where(rows == 0, dprev, pltpu.roll(dc, 1, 0))
            dc_p1 = jnp.where(rows == tb - 1, dnext, pltpu.roll(dc, tb - 1, 0))
            yield
            dh = cw_ref[pl.ds(0, 1), :] * dc_p1 + cw_ref[pl.ds(1, 1), :] * dc + cw_ref[pl.ds(2, 1), :] * dc_m1
            dz_ref[:, 0:512] = dzcb_ref[...]
            yield
            dz_ref[:, 512:1024] = (dh * cu_ref[...]).astype(BF16)
            dz_ref[:, 1024:1536] = (dh * cc_ref[...]).astype(BF16)
            dz_ref[:, 2560:3072] = dzg_ref[...]
            yield
            sb_scr[...] = _head_sum((q_raw * 0.125) * k, 64, 128)
            yield
            dsk_scr[...] = _head_sum(do_ref[...] * v, 128, 64)
            yield

        for _ in itertools.zip_longest(
                _gla_bwd_chunks(do_ref, sd_ref, dst, b_scr, db_scr, dq_scr, dk_scr, dv_scr, qt, kt, e, ei, v, True, nb),
                token_local()):
            pass
        dpre, dlr, dw = _gate_bwd(db_scr[...], pre, lr, wp, True, tb)
        dw_ref[...] += dw
        _acc_rows(dbias_ref, jnp.sum(dpre, axis=0, keepdims=True))
        dsk = dsk_scr[...]
        dz_ref[:, 1536:1792] = (dqa_ref[...] + dq_scr[...] - dsk * k * 0.125).astype(BF16)
        dz_ref[:, 1792:2048] = (dka_ref[...] + dk_scr[...] - dsk * (q_raw * 0.125)).astype(BF16)
        dz_ref[:, 2048:2560] = (dva_ref[...] + dv_scr[...] - sb_scr[...] * do_ref[...]).astype(BF16)
        dz_ref[:, 3072:3200] = (dlra_ref[...] + dlr).astype(BF16)

    full = lambda i: (0, 0)
    tokv = pl.BlockSpec((tb, GV), lambda i: (i, 0))
    tokk = pl.BlockSpec((tb, GK), lambda i: (i, 0))
    dcp = pl.BlockSpec((8, CW), lambda i: (jnp.maximum(i * (tb // 8) - 1, 0), 0))
    dcn = pl.BlockSpec((8, CW), lambda i: (jnp.minimum((i + 1) * (tb // 8), t // 8 - 1), 0))
    in_specs = [_zspec(tb, GK, ZB_Q, jmap), _zspec(tb, GK, ZB_K, jmap), _zspec(tb, GV, ZB_V, jmap),
                _zspec(tb, 128, ZB_LR, jmap), _zspec(tb, CW, ZB_CC, jmap), _zspec(tb, CW, ZB_CU, jmap), tokv,
                pl.BlockSpec((nb, 128, GK), lambda i: (i, 0, 0)), pl.BlockSpec((128, GK), full),
                pl.BlockSpec((1, GK), full), tokk, tokk, tokv, pl.BlockSpec((tb, 128), lambda i: (i, 0)), tokv, tokv,
                tokv, dcp, dcn, pl.BlockSpec((3, CW), full)]
    out_specs = [pl.BlockSpec((tb, ZC), lambda i: (i, 0)), pl.BlockSpec((128, GK), full), pl.BlockSpec((8, GK), full)]
    out_shape = [jax.ShapeDtypeStruct((t, ZC), BF16), jax.ShapeDtypeStruct((128, GK), F32),
                 jax.ShapeDtypeStruct((8, GK), F32)]
    return _call(
        body, "gla_bwd_second", (nblk,), in_specs, out_specs, out_shape,
        [pltpu.VMEM((128, GK), F32), pltpu.VMEM((tb, GK), F32), pltpu.VMEM((tb, GK), F32),
         pltpu.VMEM((tb, GK), F32), pltpu.VMEM((tb, GK), F32), pltpu.VMEM((tb, GV), F32),
         pltpu.VMEM((tb, GV), F32), pltpu.VMEM((tb, GK), F32)],
        (z, z, z, z, z, z, do, sd, wpad, bias, dqa, dka, dva, dlra, dzg, dzcb, dconv, dconv, dconv, conv_w), riders)


def _step(x, mem, target, shard, small_pack, vec, place):
    own, from_chips = {}, {}

    def pair_sums(names, g4, from_sibling):
        pbs = {}
        for shape in dict.fromkeys(g.shape for g in g4):
            idx = [i for i, g in enumerate(g4) if g.shape == shape]
            pb, mine = _rs_pair_sum(place, [g4[i] for i in idx], [from_sibling[i] for i in idx],
                                    "pair_sum_" + "_".join(names[i] for i in idx))
            for i, b, o in zip(idx, pb, mine):
                pbs[i], own[names[i]] = b, o
        return [pbs[i] for i in range(len(g4))]

    def by_dest(g, n):
        return g.reshape((4, 2) + shard[n].shape)

    w_in, small_all = _exchange(_gather_rider([shard["w_in"], small_pack]), "gather_w_in")
    w_in = jnp.pad(w_in.reshape(ZW, D), ((0, ZC - ZW), (0, 0)))
    small_all = small_all.reshape(NDEV, -1)
    p, off = {}, 0
    for n, (r, c) in SMALL_SHARDED.items():
        p[n] = small_all[:, off:off + r * c].reshape(NDEV, r, c).transpose(1, 0, 2).reshape(r, NDEV * c)
        off += r * c
    zeros_lr = jnp.zeros((128 - LR, GK), BF16)
    waf_pad = jnp.concatenate([p["w_af"].astype(BF16), zeros_lr], axis=0)
    wab_pad = jnp.concatenate([jnp.zeros((LR, GK), BF16), p["w_ab"].astype(BF16), zeros_lr[:128 - 2 * LR]], axis=0)
    gla_norm4 = jnp.tile(vec["gla_norm"], (1, NH))

    z, hb, w_out, w_xq, w_xo, w_xkv = _inproj(
        x, vec["mix_norm"], w_in, [_gather_rider([shard[n] for n in ("w_out", "w_xq", "w_xo", "w_xkv")])])
    w_out, w_xq, w_xo = [a.reshape(D, D) for a in (w_out, w_xq, w_xo)]
    o_f, sd_f, o_b, sd_b, w_up_t, w_down = _gla_fwd(
        z, waf_pad, vec["b_af"], wab_pad, vec["b_ab"],
        [_gather_rider([shard["w_up"], shard["w_down"]], early_relay=False)])
    w_up_t, w_down = w_up_t.reshape(FF, D), w_down.reshape(FF, D)
    kv, memn = _kv_proj(mem, vec["mem_norm"], w_xkv)
    kb, vb = kv[:, :D].astype(BF16), kv[:, D:].astype(BF16)
    x1, x2, xn1, qb, attb, yb, o_pre = _attn_fwd(x, z, o_f, o_b, p["conv_w"], vec["conv_norm"], gla_norm4, w_out,
                                                 vec["xa_norm"], w_xq, kb, vb, w_xo)
    h1b, xn2, dx3, dx3b, loss8, dfinal = _mlp_fwd(x2, vec["mlp_norm"], w_up_t, w_down, vec["final_norm"], target)

    ab, dh1b, dx2, dx2b, dmlp = _mlp_bwd(dx3, dx3b, h1b, w_down, w_up_t, x2, vec["mlp_norm"])
    g_mlp = [by_dest(_matmul_tn(ab, dx3b, "dw_down")[0], "w_down"),
             by_dest(_matmul_tn(dh1b, xn2, "dw_up")[0], "w_up")]
    dx1, dx1b, dy, dqb, dkv, dxa, *s_mlp = _attn_bwd(x1, dx2, dx2b, qb, kb, vb, w_xo, w_xq, w_out, vec["xa_norm"],
                                                     riders=[_sibling_rider(g_mlp)])
    pb_mlp = pair_sums(("w_down", "w_up"), g_mlp, s_mlp)
    dw_xo = _matmul_tn(attb, dx2b, "dw_xo")[0]
    dw_xkv, dmemn = _kv_bwd(dkv, memn, mem, vec["mem_norm"], w_xkv)
    att_names = ("w_xo", "w_xq", "w_out", "w_xkv")
    g_att = [by_dest(g, n) for g, n in zip(
        (dw_xo, _matmul_tn(xn1, dqb, "dw_xq")[0], _matmul_tn(yb, dx1b, "dw_out")[0], dw_xkv), att_names)]
    res = _gla_bwd_first(z, dy, o_pre, sd_f, waf_pad, vec["b_af"], p["conv_w"], vec["conv_norm"], gla_norm4,
                         riders=[_chips_rider(pb_mlp), _sibling_rider(g_att)])
    do, dqa, dka, dva, dlra, dzg, dzcb, dconv, dwaf, dbaf, dcw, dcn, dgn = res[:13]
    from_chips["w_down"], from_chips["w_up"] = res[13:15]
    pb_att = pair_sums(att_names, g_att, res[15:])
    dz, dwab, dbab, *c_att = _gla_bwd_second(z, do, sd_b, wab_pad, vec["b_ab"], dqa, dka, dva, dlra, dzg, dzcb, dconv,
                                             p["conv_w"], riders=[_chips_rider(pb_att)])
    from_chips.update(zip(att_names, c_att))
    g_in = [by_dest(_matmul_tn(dz, hb, "dw_in", rows=ZW)[0], "w_in")]
    pb_in = pair_sums(("w_in",), g_in, _exchange(_sibling_rider(g_in), "grads_to_sibling_w_in"))
    grad_x, dmix, from_chips["w_in"] = _inproj_bwd(dz, w_in, x, dx1, vec["mix_norm"], riders=[_chips_rider(pb_in)])

    small_acc = dict(mix_norm=dmix, conv_w=dcw, conv_norm=dcn, w_af=dwaf, b_af=dbaf, w_ab=dwab, b_ab=dbab,
                     gla_norm=dgn, xa_norm=dxa, mem_norm=dmemn, mlp_norm=dmlp, final_norm=dfinal)
    return loss8, grad_x, small_acc, own, from_chips


def _place():
    return lax.axis_index("x"), lax.axis_index("y"), lax.axis_index("c")


class _Rider:
    def __init__(self, arrays, out_shape, scratch, start, finish, relay=None):
        self.arrays, self.out_shape, self.scratch, self.start, self.finish = arrays, out_shape, scratch, start, finish
        self.relay = relay


def _gather_rider(blks, early_relay=True):
    n = len(blks)

    def plan(in_refs, out_refs, sems):
        send_sems, recv_sems, local_sems = sems
        x, y, c = _place()
        me, sibling = (x, y, c), (x, y, 1 - c)
        xn, yn, dg = (1 - x, y, c), (x, 1 - y, c), (1 - x, 1 - y, c)
        via = (x + (1 - c) * (1 - 2 * x), y + c * (1 - 2 * y), c)
        onto = (x + c * (1 - 2 * x), y + (1 - c) * (1 - 2 * y), c)
        other = onto

        def copy(a, k, block, to, own=False):
            px, py, pc = block
            dst = out_refs[a].at[4 * px + 2 * py + pc]
            return pltpu.make_async_remote_copy(
                src_ref=in_refs[a] if own else dst, dst_ref=dst, send_sem=send_sems.at[k, a],
                recv_sem=recv_sems.at[k, a], device_id=to, device_id_type=MESH)

        def local(a):
            return pltpu.make_async_copy(in_refs[a], out_refs[a].at[4 * x + 2 * y + c], local_sems.at[a])

        def sends(a):
            return ([copy(a, 0, me, sibling, own=True), copy(a, 1, me, xn, own=True), copy(a, 2, me, yn, own=True),
                     copy(a, 3, via, onto), copy(a, 4 + c, via, sibling), copy(a, 5 - c, other, sibling),
                     copy(a, 6, dg, sibling)])

        return copy, local, sends, me, sibling, (xn, yn, dg), via, other

    def start(in_refs, out_refs, sems):
        _, local, sends, _, _, _, _, _ = plan(in_refs, out_refs, sems)
        for a in range(n):
            local(a).start()
            for cp in sends(a)[:3]:
                cp.start()

    def relay(in_refs, out_refs, sems):
        copy, _, sends, me, _, (_, _, dg), via, other = plan(in_refs, out_refs, sems)
        c = me[2]
        for a in range(n):
            copy(a, 1 + c, via, me).wait_recv()
            sends(a)[3].start()
            sends(a)[4].start()
        for a in range(n):
            copy(a, 2 - c, other, me).wait_recv()
            sends(a)[5].start()
        for a in range(n):
            copy(a, 3, dg, me).wait_recv()
            sends(a)[6].start()

    def finish(in_refs, out_refs, sems):
        if not early_relay:
            relay(in_refs, out_refs, sems)
        copy, local, sends, me, sibling, chips, _, _ = plan(in_refs, out_refs, sems)
        for a in range(n):
            copy(a, 0, sibling, me).wait_recv()
            for j, (px, py, pc) in enumerate(chips):
                copy(a, 4 + j, (px, py, 1 - pc), me).wait_recv()
            for cp in sends(a):
                cp.wait_send()
            local(a).wait()

    return _Rider(blks, [jax.ShapeDtypeStruct((NDEV,) + b.shape, b.dtype) for b in blks],
                  [pltpu.SemaphoreType.DMA((7, n)), pltpu.SemaphoreType.DMA((7, n)), pltpu.SemaphoreType.DMA((n,))],
                  start, finish, relay if early_relay else None)


def _sibling_rider(g4s):
    n = len(g4s)

    def copies(in_refs, out_refs, sems):
        send_sems, recv_sems = sems
        x, y, c = _place()
        return [pltpu.make_async_remote_copy(
            src_ref=in_refs[a].at[k, 1 - c], dst_ref=out_refs[a].at[k], send_sem=send_sems.at[k, a],
            recv_sem=recv_sems.at[k, a], device_id=(x, y, 1 - c), device_id_type=MESH)
            for a in range(n) for k in range(4)]

    def start(in_refs, out_refs, sems):
        for cp in copies(in_refs, out_refs, sems):
            cp.start()

    def finish(in_refs, out_refs, sems):
        for cp in copies(in_refs, out_refs, sems):
            cp.wait()

    return _Rider(g4s, [jax.ShapeDtypeStruct((4,) + g.shape[2:], g.dtype) for g in g4s],
                  [pltpu.SemaphoreType.DMA((4, n)), pltpu.SemaphoreType.DMA((4, n))], start, finish)


def _chips_rider(pbs):
    n = len(pbs)

    def copies(in_refs, out_refs, sems):
        send_sems, recv_sems = sems
        x, y, c = _place()
        peers = [(1 - x, y), (x, 1 - y), (1 - x, 1 - y)]
        return [pltpu.make_async_remote_copy(
            src_ref=in_refs[a].at[2 * px + py], dst_ref=out_refs[a].at[k], send_sem=send_sems.at[k, a],
            recv_sem=recv_sems.at[k, a], device_id=(px, py, c), device_id_type=MESH)
            for a in range(n) for k, (px, py) in enumerate(peers)]

    def start(in_refs, out_refs, sems):
        for cp in copies(in_refs, out_refs, sems):
            cp.start()

    def finish(in_refs, out_refs, sems):
        for cp in copies(in_refs, out_refs, sems):
            cp.wait()

    return _Rider(pbs, [jax.ShapeDtypeStruct((3,) + p.shape[1:], p.dtype) for p in pbs],
                  [pltpu.SemaphoreType.DMA((3, n)), pltpu.SemaphoreType.DMA((3, n))], start, finish)


def _exchange(rider, name):
    n_in, n_out = len(rider.arrays), len(rider.out_shape)

    def body(*refs):
        ins, outs, sems = refs[:n_in], refs[n_in:n_in + n_out], refs[n_in + n_out:]
        rider.start(ins, outs, sems)
        if rider.relay:
            rider.relay(ins, outs, sems)
        rider.finish(ins, outs, sems)

    hbm = pl.BlockSpec(memory_space=pltpu.HBM)
    return pl.pallas_call(body, name=name, out_shape=rider.out_shape, in_specs=[hbm] * n_in,
                          out_specs=[hbm] * n_out, scratch_shapes=rider.scratch)(*rider.arrays)


def _rs_pair_sum(place, g4s, r1s, name):
    n = len(g4s)
    rows, cols = g4s[0].shape[2:]
    tr = min(rows, 512)

    def body(pl_ref, *refs):
        for g_ref, r_ref, pb_ref, own_ref in zip(refs[:n], refs[n:2 * n], refs[2 * n:3 * n], refs[3 * n:]):
            s = g_ref[0, 0] + r_ref[0]
            pb_ref[0] = s.astype(BF16)

            @pl.when(pl.program_id(1) == pl_ref[0])
            def _():
                own_ref[...] = s

    grid_spec = pltpu.PrefetchScalarGridSpec(
        num_scalar_prefetch=1, grid=(rows // tr, 4),
        in_specs=[pl.BlockSpec((1, 1, tr, cols), lambda r, k, p: (k, p[1], r, 0))] * n
        + [pl.BlockSpec((1, tr, cols), lambda r, k, p: (k, r, 0))] * n,
        out_specs=[pl.BlockSpec((1, tr, cols), lambda r, k, p: (k, r, 0))] * n
        + [pl.BlockSpec((tr, cols), lambda r, k, p: (r, 0))] * n)
    res = pl.pallas_call(
        body, name=name, grid_spec=grid_spec,
        out_shape=[jax.ShapeDtypeStruct((4, rows, cols), BF16)] * n + [jax.ShapeDtypeStruct((rows, cols), F32)] * n,
        compiler_params=_cparams(("arbitrary", "arbitrary")))(place, *g4s, *r1s)
    return res[:n], res[n:]


PACK_ROWS = 32
VEC_ROW = {"mix_norm": 0, "conv_norm": 1, "b_af": 2, "b_ab": 3, "gla_norm": 4, "xa_norm": 5, "mem_norm": 6,
           "mlp_norm": 7, "final_norm": 8}
LOSS_ROW, MAT_ROW = 9, 16
MAT_LANE = {"w_af": 0, "w_ab": GK, "conv_w": 2 * GK}
MAT_SRC_ROW = {"w_af": 0, "w_ab": LR, "conv_w": 0}


SMALL_WIDTH = {"mix_norm": D, "conv_w": 64, "conv_norm": CW, "w_af": 32, "b_af": GK, "w_ab": 32, "b_ab": GK,
               "gla_norm": 128, "xa_norm": D, "mem_norm": D, "mlp_norm": D, "final_norm": D}


def _small_reduce(acc, loss8):
    names = list(SMALL)
    n = len(names)
    widths = SMALL_WIDTH

    def body(*refs):
        acc_refs = dict(zip(names, refs[:n]))
        loss_ref, tot = refs[n], refs[n + 1]
        pk, all_ref, send_sems, recv_sems, local_sem = refs[n + 2:]

        pk[...] = jnp.zeros_like(pk)
        for k, row in VEC_ROW.items():
            if k == "gla_norm":
                g = functools.reduce(lambda a, b: a + b, [acc_refs[k][pl.ds(0, 1), pl.ds(h * 128, 128)]
                                                          for h in range(NH)])
            else:
                g = acc_refs[k][pl.ds(0, 1), :]
            pk[pl.ds(row, 1), pl.ds(0, widths[k])] = g
        pk[pl.ds(LOSS_ROW, 1), pl.ds(0, 128)] = loss_ref[pl.ds(0, 1), :]
        for k, lane in MAT_LANE.items():
            rows, cols = (3, CW) if k == "conv_w" else (LR, GK)
            pk[pl.ds(MAT_ROW, rows), pl.ds(lane, cols)] = acc_refs[k][pl.ds(MAT_SRC_ROW[k], rows), :]

        x, y, c = _place()
        me, sibling = (x, y, c), (x, y, 1 - c)
        chips = [(1 - x, y, c), (x, 1 - y, c), (1 - x, 1 - y, c)]

        def copy(k, block, to, own=False):
            px, py, pc = block
            dst = all_ref.at[4 * px + 2 * py + pc]
            return pltpu.make_async_remote_copy(
                src_ref=pk if own else dst, dst_ref=dst, send_sem=send_sems.at[k], recv_sem=recv_sems.at[k],
                device_id=to, device_id_type=MESH)

        mine = pltpu.make_async_copy(pk, all_ref.at[4 * x + 2 * y + c], local_sem)
        mine.start()
        first = [copy(0, me, sibling, own=True)] + [copy(1 + j, me, chip, own=True) for j, chip in enumerate(chips)]
        for cp in first:
            cp.start()
        passed = [copy(4 + j, chip, sibling) for j, chip in enumerate(chips)]
        for j, chip in enumerate(chips):
            copy(1 + j, chip, me).wait_recv()
            passed[j].start()
        copy(0, sibling, me).wait_recv()
        for j, (px, py, pc) in enumerate(chips):
            copy(4 + j, (px, py, 1 - pc), me).wait_recv()
        for cp in first + passed:
            cp.wait_send()
        mine.wait()
        total = all_ref[0]
        for d in range(1, NDEV):
            total = total + all_ref[d]
        tot[...] = total

    return pl.pallas_call(
        body, name="small_reduce", out_shape=jax.ShapeDtypeStruct((PACK_ROWS, D), F32),
        scratch_shapes=[pltpu.VMEM((PACK_ROWS, D), F32), pltpu.VMEM((NDEV, PACK_ROWS, D), F32),
                        pltpu.SemaphoreType.DMA((7,)), pltpu.SemaphoreType.DMA((7,)), pltpu.SemaphoreType.DMA],
    )(*[acc[k] for k in names], loss8)


def _small_adamw(tot, ws, ms, vs):
    names = list(SMALL)
    n = len(names)
    widths = SMALL_WIDTH

    def body(*refs):
        tot = refs[0]
        w_refs, m_refs, v_refs = [dict(zip(names, refs[1 + q * n:1 + (q + 1) * n])) for q in range(3)]
        outs = refs[1 + 3 * n:1 + 7 * n]
        g_out, d_out, m_out, v_out = [dict(zip(names, outs[q * n:(q + 1) * n])) for q in range(4)]
        cut = refs[1 + 7 * n]
        x, y, c = _place()
        dev = 4 * x + 2 * y + c
        for k in names:
            if k in VEC_ROW:
                g = tot[pl.ds(VEC_ROW[k], 1), pl.ds(0, widths[k])]
            else:
                rows, cols = (3, CW) if k == "conv_w" else (LR, GK)
                wd = widths[k]
                sel = jnp.where(_iota((cols, wd), 0) == dev * wd + _iota((cols, wd), 1), 1.0, 0.0).astype(BF16)
                cut[:, pl.ds(0, wd)] = _dot_exact_rhs(tot[pl.ds(MAT_ROW, LR), pl.ds(MAT_LANE[k], cols)], sel, 3)
                g = cut[pl.ds(0, rows), pl.ds(0, wd)]
            g_out[k][...] = g
            d_out[k][...], m_out[k][...], v_out[k][...] = _adamw_math(w_refs[k][...], g, m_refs[k][...],
                                                                       v_refs[k][...])

    shapes = [jax.ShapeDtypeStruct(ws[k].shape, F32) for k in names]
    res = pl.pallas_call(
        body, name="small_adamw", out_shape=shapes * 4, scratch_shapes=[pltpu.VMEM((LR, 128), F32)],
    )(tot, *[ws[k] for k in names], *[ms[k] for k in names], *[vs[k] for k in names])
    return {k: tuple(res[q * n + i] for q in range(4)) for i, k in enumerate(names)}


def _adamw_math(w, g, m, v):
    m = ADAM_B1 * m + (1.0 - ADAM_B1) * g
    v = ADAM_B2 * v + (1.0 - ADAM_B2) * (g * g)
    m_hat = m / (1.0 - ADAM_B1 ** ADAM_STEP)
    v_hat = v / (1.0 - ADAM_B2 ** ADAM_STEP)
    delta = -ADAM_LR * (m_hat / (jnp.sqrt(v_hat) + ADAM_EPS) + ADAM_WD * w)
    return delta, m, v


def _adamw(ws, ms, vs, owns, r2s, name, grads_transposed=False):
    n = len(ws)
    _, r, c = ws[0].shape
    tr = 256 if r % 256 == 0 else r

    def body(*refs):
        ins, outs = refs[:5 * n], refs[5 * n:]
        for q in range(n):
            w_ref, m_ref, v_ref, o_ref, r_ref = [ins[k * n + q] for k in range(5)]
            g_ref, d_ref, nm_ref, nv_ref = [outs[k * n + q] for k in range(4)]
            g = ((o_ref[...] + r_ref[0].astype(F32)) + r_ref[1].astype(F32)) + r_ref[2].astype(F32)
            g = g.T if grads_transposed else g
            g_ref[...] = g
            d_ref[...], nm_ref[...], nv_ref[...] = _adamw_math(w_ref[...], g, m_ref[...], v_ref[...])

    spec = pl.BlockSpec((None, tr, c), lambda i: (0, i, 0))
    if grads_transposed:
        own_spec, r2_spec = pl.BlockSpec((c, tr), lambda i: (0, i)), pl.BlockSpec((3, c, tr), lambda i: (0, 0, i))
    else:
        own_spec, r2_spec = pl.BlockSpec((tr, c), lambda i: (i, 0)), pl.BlockSpec((3, tr, c), lambda i: (0, i, 0))
    res = pl.pallas_call(
        body, name=name, grid=(r // tr,),
        in_specs=[spec] * (3 * n) + [own_spec] * n + [r2_spec] * n,
        out_specs=[spec] * (4 * n), out_shape=[jax.ShapeDtypeStruct((1, r, c), F32)] * (4 * n),
        compiler_params=_cparams(("arbitrary",)))(*ws, *ms, *vs, *owns, *r2s)
    return [tuple(res[k * n + q] for k in range(4)) for q in range(n)]


MATS = ("w_in", "w_out", "w_xq", "w_xo", "w_xkv", "w_up", "w_down")
SMALL = ("mix_norm", "conv_w", "conv_norm", "w_af", "b_af", "w_ab", "b_ab", "gla_norm", "xa_norm", "mem_norm",
         "mlp_norm", "final_norm")
WEIGHTS = ("mix_norm", "w_in", "conv_w", "conv_norm", "w_af", "b_af", "w_ab", "b_ab", "gla_norm", "w_out", "xa_norm",
           "mem_norm", "w_xq", "w_xkv", "w_xo", "mlp_norm", "w_up", "w_down", "final_norm")
SMALL_SHARDED = {"conv_w": (3, 64), "w_af": (LR, 32), "w_ab": (LR, 32)}
SMALL_PACK_ROWS = 16


def kernel(x, mem, mix_norm, w_in, conv_w, conv_norm, w_af, b_af, w_ab, b_ab, gla_norm, w_out, xa_norm, mem_norm, w_xq, w_xkv, w_xo, mlp_norm, w_up, w_down, final_norm, loss_target, m_mix_norm, m_w_in, m_conv_w, m_conv_norm, m_w_af, m_b_af, m_w_ab, m_b_ab, m_gla_norm, m_w_out, m_xa_norm, m_mem_norm, m_w_xq, m_w_xkv, m_w_xo, m_mlp_norm, m_w_up, m_w_down, m_final_norm, v_mix_norm, v_w_in, v_conv_w, v_conv_norm, v_w_af, v_b_af, v_w_ab, v_b_ab, v_gla_norm, v_w_out, v_xa_norm, v_mem_norm, v_w_xq, v_w_xkv, v_w_xo, v_mlp_norm, v_w_up, v_w_down, v_final_norm):
    w = dict(mix_norm=mix_norm, w_in=w_in, conv_w=conv_w, conv_norm=conv_norm, w_af=w_af, b_af=b_af, w_ab=w_ab,
             b_ab=b_ab, gla_norm=gla_norm, w_out=w_out, xa_norm=xa_norm, mem_norm=mem_norm, w_xq=w_xq, w_xkv=w_xkv,
             w_xo=w_xo, mlp_norm=mlp_norm, w_up=w_up, w_down=w_down, final_norm=final_norm)
    mom = dict(mix_norm=m_mix_norm, w_in=m_w_in, conv_w=m_conv_w, conv_norm=m_conv_norm, w_af=m_w_af, b_af=m_b_af,
               w_ab=m_w_ab, b_ab=m_b_ab, gla_norm=m_gla_norm, w_out=m_w_out, xa_norm=m_xa_norm, mem_norm=m_mem_norm,
               w_xq=m_w_xq, w_xkv=m_w_xkv, w_xo=m_w_xo, mlp_norm=m_mlp_norm, w_up=m_w_up, w_down=m_w_down,
               final_norm=m_final_norm)
    var = dict(mix_norm=v_mix_norm, w_in=v_w_in, conv_w=v_conv_w, conv_norm=v_conv_norm, w_af=v_w_af, b_af=v_b_af,
               w_ab=v_w_ab, b_ab=v_b_ab, gla_norm=v_gla_norm, w_out=v_w_out, xa_norm=v_xa_norm, mem_norm=v_mem_norm,
               w_xq=v_w_xq, w_xkv=v_w_xkv, w_xo=v_w_xo, mlp_norm=v_mlp_norm, w_up=v_w_up, w_down=v_w_down,
               final_norm=v_final_norm)
    xi, yi, ci = _place()
    two_d = lambda a: a.reshape(a.shape[-2:]) if a.ndim == 3 else a.reshape(1, a.shape[-1])

    small = jnp.concatenate([w[n].reshape(-1) for n in SMALL_SHARDED])
    small = jnp.pad(small, (0, SMALL_PACK_ROWS * 128 - small.shape[0])).reshape(SMALL_PACK_ROWS, 128)
    shard = {n: two_d(w[n]).astype(BF16) for n in MATS}
    for n in ("w_in", "w_up"):
        shard[n] = shard[n].T
    vec = {n: two_d(w[n]) for n in SMALL if n not in SMALL_SHARDED}
    place = jnp.stack([2 * xi + yi, ci]).astype(jnp.int32)
    loss8, grad_x, small_acc, own, from_chips = _step(x[0], mem[0], loss_target[0], shard, small, vec, place)

    tot = _small_reduce(small_acc, loss8)
    small_out = _small_adamw(tot, *[{n: two_d(d[n]) for n in SMALL} for d in (w, mom, var)])
    loss = tot[LOSS_ROW, 0]

    out_g, out_d, out_m, out_v = {}, {}, {}, {}
    wmv = {n: [a.transpose(0, 2, 1) if n == "w_in" else a for a in (w[n], mom[n], var[n])] for n in MATS}
    for shape in dict.fromkeys(wmv[n][0].shape for n in MATS):
        names = [n for n in MATS if wmv[n][0].shape == shape]
        res = _adamw(*[[wmv[n][k] for n in names] for k in range(3)], [own[n] for n in names],
                     [from_chips[n] for n in names], "adamw_" + "_".join(names), grads_transposed=names == ["w_up"])
        for n, r in zip(names, res):
            out_g[n], out_d[n], out_m[n], out_v[n] = [a.transpose(0, 2, 1) for a in r] if n == "w_in" else r
    for n in SMALL:
        out_g[n], out_d[n], out_m[n], out_v[n] = [a.reshape(w[n].shape) for a in small_out[n]]

    return (loss, grad_x[None], *[out_g[n] for n in WEIGHTS], *[out_d[n] for n in WEIGHTS],
            *[out_m[n] for n in WEIGHTS], *[out_v[n] for n in WEIGHTS])
```

```python
import functools
import itertools

import jax
import jax.numpy as jnp
from jax import lax
from jax.experimental import pallas as pl
from jax.experimental.pallas import tpu as pltpu

F32 = jnp.float32
BF16 = jnp.bfloat16

D = 1024
CW = 512
GK = 256
GV = 512
NH = 4
CH = 64
LR = 16
NMEM = 256
XD = 256
FF = 4096
ZW = 3104
ZC = 3200
EPS = 1e-6
NDEV = 8

ZB_CB, ZB_CC, ZB_CU, ZB_V, ZB_G = 0, 1, 2, 4, 5
ZB_Q, ZB_K = 6, 7
ZB_LR = 24

TM = 512
TM_MLP = 256
TM_MLP_FWD = 512
TF = 512
TB = 512
TB_BWD = 512
TT = 2048
VMEM_LIMIT = 56 * 1024 * 1024

ADAM_LR, ADAM_B1, ADAM_B2, ADAM_EPS, ADAM_WD, ADAM_STEP = 0.001, 0.9, 0.999, 1e-08, 0.01, 10

XKV_SHARD = 2 * D // NDEV

MESH = pl.DeviceIdType.MESH


def _cparams(sem):
    return pltpu.CompilerParams(dimension_semantics=sem, vmem_limit_bytes=VMEM_LIMIT)


def _call(body, name, grid, in_specs, out_specs, out_shape, scratch, args, riders=()):
    n_in, n_out, n_scr = len(in_specs), len(out_specs), len(scratch)
    counts = [(len(r.arrays), len(r.out_shape), len(r.scratch)) for r in riders]

    def take(refs, pos, sizes):
        groups = []
        for size in sizes:
            groups.append(refs[pos:pos + size])
            pos += size
        return groups, pos

    def wrapped(*refs):
        ins, pos = refs[:n_in], n_in
        r_ins, pos = take(refs, pos, [c[0] for c in counts])
        outs, pos = refs[pos:pos + n_out], pos + n_out
        r_outs, pos = take(refs, pos, [c[1] for c in counts])
        scr, pos = refs[pos:pos + n_scr], pos + n_scr
        r_scr, pos = take(refs, pos, [c[2] for c in counts])
        ids = [pl.program_id(d) for d in range(len(grid))]
        first = functools.reduce(lambda a, b: a & b, [i == 0 for i in ids])
        last = functools.reduce(lambda a, b: a & b, [i == g - 1 for i, g in zip(ids, grid)])

        @pl.when(first)
        def _():
            for r, a, b, c in zip(riders, r_ins, r_outs, r_scr):
                r.start(a, b, c)

        body(*ins, *outs, *scr)

        for hook, at in (("forward", [g // 2 for g in grid]), ("relay", [max(g - 2, 0) for g in grid])):
            if any(getattr(r, hook) for r in riders):
                @pl.when(functools.reduce(lambda a, b: a & b, [i == s for i, s in zip(ids, at)]))
                def _(hook=hook):
                    for r, a, b, c in zip(riders, r_ins, r_outs, r_scr):
                        if getattr(r, hook):
                            getattr(r, hook)(a, b, c)

        @pl.when(last)
        def _():
            for r, a, b, c in zip(riders, r_ins, r_outs, r_scr):
                r.finish(a, b, c)

    hbm = pl.BlockSpec(memory_space=pltpu.HBM)
    r_args = [a for r in riders for a in r.arrays]
    r_shapes = [s for r in riders for s in r.out_shape]
    return pl.pallas_call(
        wrapped if riders else body, name=name, grid=grid, in_specs=list(in_specs) + [hbm] * len(r_args),
        out_specs=list(out_specs) + [hbm] * len(r_shapes), out_shape=list(out_shape) + r_shapes,
        scratch_shapes=list(scratch) + [s for r in riders for s in r.scratch],
        compiler_params=_cparams(("arbitrary",) * len(grid)))(*args, *r_args)


def _dot(a, b):
    return jnp.dot(a.astype(BF16), b.astype(BF16), preferred_element_type=F32)


def _dot_nt(a, b):
    return lax.dot_general(a.astype(BF16), b.astype(BF16), (((1,), (1,)), ((), ())), preferred_element_type=F32)


def _dot_tn(a, b):
    return lax.dot_general(a.astype(BF16), b.astype(BF16), (((0,), (0,)), ((), ())), preferred_element_type=F32)


def _split(x, n):
    parts = []
    for _ in range(n):
        p = x.astype(BF16)
        parts.append(p)
        x = x - p.astype(F32)
    return parts


def _dot_exact_lhs(m, x, n):
    return functools.reduce(lambda a, b: a + b, [jnp.dot(m, p, preferred_element_type=F32) for p in _split(x, n)])


def _dot_exact_rhs(x, m, n):
    return functools.reduce(lambda a, b: a + b, [jnp.dot(p, m, preferred_element_type=F32) for p in _split(x, n)])


def _rms(x, g):
    r = lax.rsqrt(jnp.mean(x * x, axis=-1, keepdims=True) + EPS)
    return x * r * g, r


def _rms_bwd(x, r, g, dy):
    xr = x * r
    u = dy * g
    dx = r * (u - xr * jnp.mean(u * xr, axis=-1, keepdims=True))
    return dx, jnp.sum(dy * xr, axis=0, keepdims=True)


def _iota(shape, dim):
    return lax.broadcasted_iota(jnp.int32, shape, dim)


def _sigmoid(x):
    return 1.0 / (1.0 + jnp.exp(-x))


def _acc_rows(ref, row):
    ref[...] += jnp.broadcast_to(row, ref.shape)


def _inproj(x, g, w_t, riders=()):
    t = x.shape[0]
    tm = min(TM, t)

    def body(x_ref, g_ref, w_ref, z_ref, h_ref):
        h, _ = _rms(x_ref[...], g_ref[...])
        hb = h.astype(BF16)
        h_ref[...] = hb
        z_ref[...] = _dot_nt(hb, w_ref[...])

    return _call(
        body, "inproj", (t // tm,),
        [pl.BlockSpec((tm, D), lambda i: (i, 0)), pl.BlockSpec((1, D), lambda i: (0, 0)),
         pl.BlockSpec((ZC, D), lambda i: (0, 0))],
        [pl.BlockSpec((tm, ZC), lambda i: (i, 0)), pl.BlockSpec((tm, D), lambda i: (i, 0))],
        [jax.ShapeDtypeStruct((t, ZC), F32), jax.ShapeDtypeStruct((t, D), BF16)], [], (x, g, w_t), riders)


def _kv_proj(mem, g, w):
    def body(m_ref, g_ref, w_ref, kv_ref, mn_ref):
        mn, _ = _rms(m_ref[...], g_ref[...])
        mb = mn.astype(BF16)
        mn_ref[...] = mb
        for j in range(NDEV):
            kv_ref[:, j * XKV_SHARD:(j + 1) * XKV_SHARD] = jnp.dot(mb, w_ref[j], preferred_element_type=F32)

    return pl.pallas_call(
        body, name="kv_proj",
        out_shape=[jax.ShapeDtypeStruct((NMEM, 2 * D), F32), jax.ShapeDtypeStruct((NMEM, D), BF16)],
        compiler_params=pltpu.CompilerParams(vmem_limit_bytes=VMEM_LIMIT))(mem, g, w)


def _softmax_head(qb, kb):
    s = _dot_nt(qb, kb) * (1.0 / 16.0)
    e = jnp.exp(s - jnp.max(s, axis=-1, keepdims=True))
    return e / jnp.sum(e, axis=-1, keepdims=True)


def _attn_fwd(x, z, o_f, o_b, conv_w, conv_norm, gla_norm4, w_out, g, w_xq, kb, vb, w_xo):
    t = x.shape[0]
    tm = min(TM, t)
    nblk = t // tm
    jmap = lambda i: i

    def body(x_ref, zq_ref, zk_ref, zv_ref, zg_ref, cb_ref, cc_ref, cu_ref, ccp_ref, ccn_ref, cup_ref, cun_ref,
             of_ref, ob_ref, cw_ref, cn_ref, gn_ref, wo_ref, g_ref, wq_ref, k_ref, v_ref, wx_ref,
             x1_ref, x2_ref, xn_ref, q_ref, a_ref, y_ref, opre_ref):
        j = pl.program_id(0)
        zv = zv_ref[...]
        sb = _head_sum((zq_ref[...] * 0.125) * zk_ref[...], 64, 128)
        o_pre = of_ref[...] + ob_ref[...] - sb * zv
        opre_ref[...] = o_pre
        on, _ = _head_norm(o_pre)
        zg = zg_ref[...]
        y_ref[:, CW:] = (on * gn_ref[...] * (zg * _sigmoid(zg))).astype(BF16)
        cb = cb_ref[...]
        _, _, _, conv = _conv_parts(cb, cc_ref[...], cu_ref[...], ccp_ref[pl.ds(7, 1), :], cup_ref[pl.ds(7, 1), :],
                                    ccn_ref[pl.ds(0, 1), :], cun_ref[pl.ds(0, 1), :], cw_ref, j == 0,
                                    j == nblk - 1, tm)
        yc = cb * conv
        gm = _group_sum(yc * yc) * (1.0 / 64.0)
        y_ref[:, :CW] = (yc * lax.rsqrt(gm + EPS) * cn_ref[...]).astype(BF16)

        x1 = x_ref[...] + jnp.dot(y_ref[...], wo_ref[...], preferred_element_type=F32)
        x1_ref[...] = x1
        xn, _ = _rms(x1, g_ref[...])
        xb = xn.astype(BF16)
        xn_ref[...] = xb
        qb = jnp.dot(xb, wq_ref[...], preferred_element_type=F32).astype(BF16)
        q_ref[...] = qb
        heads = [slice(h * XD, (h + 1) * XD) for h in range(NH)]
        ps = [_softmax_head(qb[:, hs], k_ref[:, hs]) for hs in heads]
        for hs, p in zip(heads, ps):
            a_ref[:, hs] = _dot(p, v_ref[:, hs]).astype(BF16)
        x2_ref[...] = x1 + jnp.dot(a_ref[...], wx_ref[...], preferred_element_type=F32)

    tok = lambda i: (i, 0)
    full = lambda i: (0, 0)
    once = pl.Buffered(1)
    tokd, tokv = pl.BlockSpec((tm, D), tok), pl.BlockSpec((tm, GV), tok)
    weight = pl.BlockSpec((D, D), full, pipeline_mode=once)
    ccp, ccn = _halo_specs(tm, nblk, t, ZB_CC, jmap)
    cup, cun = _halo_specs(tm, nblk, t, ZB_CU, jmap)
    in_specs = [tokd, _zspec(tm, GK, ZB_Q, jmap), _zspec(tm, GK, ZB_K, jmap), _zspec(tm, GV, ZB_V, jmap),
                _zspec(tm, GV, ZB_G, jmap), _zspec(tm, CW, ZB_CB, jmap), _zspec(tm, CW, ZB_CC, jmap),
                _zspec(tm, CW, ZB_CU, jmap), ccp, ccn, cup, cun, tokv, tokv,
                pl.BlockSpec((3, CW), full), pl.BlockSpec((1, CW), full), pl.BlockSpec((1, GV), full),
                weight, pl.BlockSpec((1, D), full), weight, pl.BlockSpec((NMEM, D), full),
                pl.BlockSpec((NMEM, D), full), weight]
    return pl.pallas_call(
        body, name="attn_fwd", grid=(nblk,), in_specs=in_specs, out_specs=[tokd] * 6 + [tokv],
        out_shape=[jax.ShapeDtypeStruct((t, D), F32), jax.ShapeDtypeStruct((t, D), F32),
                   jax.ShapeDtypeStruct((t, D), BF16), jax.ShapeDtypeStruct((t, D), BF16),
                   jax.ShapeDtypeStruct((t, D), BF16), jax.ShapeDtypeStruct((t, D), BF16),
                   jax.ShapeDtypeStruct((t, GV), F32)],
        compiler_params=_cparams(("arbitrary",)))(
            x, z, z, z, z, z, z, z, z, z, z, z, o_f, o_b, conv_w, conv_norm, gla_norm4, w_out, g, w_xq, kb, vb, w_xo)


def _mlp_fwd(x2, g, w_up_t, w_down, fg, target):
    t = x2.shape[0]
    tm = min(TM_MLP_FWD, t)

    def body(x_ref, g_ref, wu_ref, wd_ref, fg_ref, t_ref, h1_ref, xn_ref, dx_ref, dxb_ref, loss_ref, dfg_ref, ab):
        @pl.when(pl.program_id(0) == 0)
        def _():
            loss_ref[...] = jnp.zeros_like(loss_ref)
            dfg_ref[...] = jnp.zeros_like(dfg_ref)

        x = x_ref[...]
        xn, _ = _rms(x, g_ref[...])
        xnb = xn.astype(BF16)
        xn_ref[...] = xnb
        for q in range(FF // TF):
            cols = slice(q * TF, (q + 1) * TF)
            h1 = _dot_nt(xnb, wu_ref[cols, :])
            h1_ref[:, cols] = h1.astype(BF16)
            hr = jnp.maximum(h1, 0.0)
            ab[:, cols] = (hr * hr).astype(BF16)
        x3 = x + jnp.dot(ab[...], wd_ref[...], preferred_element_type=F32)
        y, r = _rms(x3, fg_ref[...])
        e = y - t_ref[...]
        row = jnp.mean(e * e, axis=-1, keepdims=True)
        _acc_rows(loss_ref, 0.5 * jnp.sum(row, axis=0, keepdims=True))
        dx, dfg = _rms_bwd(x3, r, fg_ref[...], e * (1.0 / D))
        dx_ref[...] = dx
        dxb_ref[...] = dx.astype(BF16)
        _acc_rows(dfg_ref, dfg)

    tok = lambda i: (i, 0)
    full = lambda i: (0, 0)
    once = pl.Buffered(1)
    return pl.pallas_call(
        body, name="mlp_fwd", grid=(t // tm,),
        in_specs=[pl.BlockSpec((tm, D), tok), pl.BlockSpec((1, D), full),
                  pl.BlockSpec((FF, D), full, pipeline_mode=once), pl.BlockSpec((FF, D), full, pipeline_mode=once),
                  pl.BlockSpec((1, D), full), pl.BlockSpec((tm, D), tok)],
        out_specs=[pl.BlockSpec((tm, FF), tok), pl.BlockSpec((tm, D), tok), pl.BlockSpec((tm, D), tok),
                   pl.BlockSpec((tm, D), tok), pl.BlockSpec((8, 128), full), pl.BlockSpec((8, D), full)],
        out_shape=[jax.ShapeDtypeStruct((t, FF), BF16), jax.ShapeDtypeStruct((t, D), BF16),
                   jax.ShapeDtypeStruct((t, D), F32), jax.ShapeDtypeStruct((t, D), BF16),
                   jax.ShapeDtypeStruct((8, 128), F32), jax.ShapeDtypeStruct((8, D), F32)],
        scratch_shapes=[pltpu.VMEM((tm, FF), BF16)],
        compiler_params=_cparams(("arbitrary",)))(x2, g, w_up_t, w_down, fg, target)


def _mlp_bwd(dx3, dx3b, h1b, w_down, w_up_t, x2, g):
    t = x2.shape[0]
    tm = min(TM_MLP, t)

    def body(dx_ref, dxb_ref, h1_ref, wd_ref, wu_ref, x_ref, g_ref, a_ref, dh_ref, dx2_ref, dx2b_ref, dg_ref):
        @pl.when(pl.program_id(0) == 0)
        def _():
            dg_ref[...] = jnp.zeros_like(dg_ref)

        for q in range(FF // TF):
            cols = slice(q * TF, (q + 1) * TF)
            hr = jnp.maximum(h1_ref[:, cols].astype(F32), 0.0)
            da = _dot_nt(dxb_ref[...], wd_ref[cols, :])
            a_ref[:, cols] = (hr * hr).astype(BF16)
            dh_ref[:, cols] = (da * 2.0 * hr).astype(BF16)
        dxn = jnp.dot(dh_ref[...], wu_ref[...], preferred_element_type=F32)
        x = x_ref[...]
        r = lax.rsqrt(jnp.mean(x * x, axis=-1, keepdims=True) + EPS)
        dx, dg = _rms_bwd(x, r, g_ref[...], dxn)
        dx2 = dx_ref[...] + dx
        dx2_ref[...] = dx2
        dx2b_ref[...] = dx2.astype(BF16)
        _acc_rows(dg_ref, dg)

    tok = lambda i: (i, 0)
    full = lambda i: (0, 0)
    once = pl.Buffered(1)
    return pl.pallas_call(
        body, name="mlp_bwd", grid=(t // tm,),
        in_specs=[pl.BlockSpec((tm, D), tok), pl.BlockSpec((tm, D), tok), pl.BlockSpec((tm, FF), tok),
                  pl.BlockSpec((FF, D), full, pipeline_mode=once), pl.BlockSpec((FF, D), full, pipeline_mode=once),
                  pl.BlockSpec((tm, D), tok), pl.BlockSpec((1, D), full)],
        out_specs=[pl.BlockSpec((tm, FF), tok), pl.BlockSpec((tm, FF), tok), pl.BlockSpec((tm, D), tok),
                   pl.BlockSpec((tm, D), tok), pl.BlockSpec((8, D), full)],
        out_shape=[jax.ShapeDtypeStruct((t, FF), BF16), jax.ShapeDtypeStruct((t, FF), BF16),
                   jax.ShapeDtypeStruct((t, D), F32), jax.ShapeDtypeStruct((t, D), BF16),
                   jax.ShapeDtypeStruct((8, D), F32)],
        compiler_params=_cparams(("arbitrary",)))(dx3, dx3b, h1b, w_down, w_up_t, x2, g)


def _attn_bwd(x1, dx2, dx2b, qb, kb, vb, w_xo, w_xq, w_out, g, riders=()):
    t = x1.shape[0]
    tm = min(TM, t)

    def body(x_ref, dx2_ref, dx2b_ref, q_ref, k_ref, v_ref, wx_ref, wq_ref, wo_ref, g_ref,
             dx1_ref, dx1b_ref, dy_ref, dq_ref, dkv_ref, dg_ref):
        @pl.when(pl.program_id(0) == 0)
        def _():
            dkv_ref[...] = jnp.zeros_like(dkv_ref)
            dg_ref[...] = jnp.zeros_like(dg_ref)

        datt = _dot_nt(dx2b_ref[...], wx_ref[...]).astype(BF16)
        heads = [slice(h * XD, (h + 1) * XD) for h in range(NH)]
        ps = [_softmax_head(q_ref[:, hs], k_ref[:, hs]) for hs in heads]
        dps = [_dot_nt(datt[:, hs], v_ref[:, hs]) for hs in heads]
        dss = [(p * (dp - jnp.sum(dp * p, axis=-1, keepdims=True)) * (1.0 / 16.0)).astype(BF16)
               for p, dp in zip(ps, dps)]
        for h, (hs, p, ds) in enumerate(zip(heads, ps, dss)):
            dq_ref[:, hs] = _dot(ds, k_ref[:, hs]).astype(BF16)
            dkv_ref[:, hs] += _dot_tn(ds, q_ref[:, hs])
            dkv_ref[:, D + h * XD:D + (h + 1) * XD] += _dot_tn(p, datt[:, hs])
        dxn = _dot_nt(dq_ref[...], wq_ref[...])
        x = x_ref[...]
        r = lax.rsqrt(jnp.mean(x * x, axis=-1, keepdims=True) + EPS)
        dx, dg = _rms_bwd(x, r, g_ref[...], dxn)
        dx1 = dx2_ref[...] + dx
        dx1_ref[...] = dx1
        dx1b = dx1.astype(BF16)
        dx1b_ref[...] = dx1b
        dy_ref[...] = _dot_nt(dx1b, wo_ref[...])
        _acc_rows(dg_ref, dg)

    tok = lambda i: (i, 0)
    full = lambda i: (0, 0)
    return _call(
        body, "attn_bwd", (t // tm,),
        [pl.BlockSpec((tm, D), tok), pl.BlockSpec((tm, D), tok), pl.BlockSpec((tm, D), tok),
         pl.BlockSpec((tm, D), tok), pl.BlockSpec((NMEM, D), full), pl.BlockSpec((NMEM, D), full),
         pl.BlockSpec((D, D), full), pl.BlockSpec((D, D), full), pl.BlockSpec((D, D), full),
         pl.BlockSpec((1, D), full)],
        [pl.BlockSpec((tm, D), tok), pl.BlockSpec((tm, D), tok), pl.BlockSpec((tm, D), tok),
         pl.BlockSpec((tm, D), tok), pl.BlockSpec((NMEM, 2 * D), full), pl.BlockSpec((8, D), full)],
        [jax.ShapeDtypeStruct((t, D), F32), jax.ShapeDtypeStruct((t, D), BF16),
         jax.ShapeDtypeStruct((t, D), F32), jax.ShapeDtypeStruct((t, D), BF16),
         jax.ShapeDtypeStruct((NMEM, 2 * D), F32), jax.ShapeDtypeStruct((8, D), F32)], [],
        (x1, dx2, dx2b, qb, kb, vb, w_xo, w_xq, w_out, g), riders)


def _kv_bwd(dkv, memn, mem, g, w):
    def body(dkv_ref, mn_ref, m_ref, g_ref, w_ref, dw_ref, dg_ref):
        dkvb = dkv_ref[...].astype(BF16)
        dmn = jnp.zeros((NMEM, D), F32)
        for j in range(NDEV):
            cols = slice(j * XKV_SHARD, (j + 1) * XKV_SHARD)
            dw_ref[j] = _dot_tn(mn_ref[...], dkvb[:, cols])
            dmn += _dot_nt(dkvb[:, cols], w_ref[j])
        m = m_ref[...]
        r = lax.rsqrt(jnp.mean(m * m, axis=-1, keepdims=True) + EPS)
        dg_ref[...] = jnp.broadcast_to(jnp.sum(dmn * m * r, axis=0, keepdims=True), dg_ref.shape)

    return pl.pallas_call(
        body, name="kv_bwd",
        out_shape=[jax.ShapeDtypeStruct((NDEV, D, XKV_SHARD), F32), jax.ShapeDtypeStruct((8, D), F32)],
        compiler_params=pltpu.CompilerParams(vmem_limit_bytes=VMEM_LIMIT))(dkv, memn, mem, g, w)


def _inproj_bwd(dz, w_t, x, dx1, g, riders=()):
    t = x.shape[0]
    tm = min(TM, t)

    def body(dz_ref, w_ref, x_ref, dx1_ref, g_ref, gx_ref, dg_ref):
        @pl.when(pl.program_id(0) == 0)
        def _():
            dg_ref[...] = jnp.zeros_like(dg_ref)

        dh = jnp.dot(dz_ref[...], w_ref[...], preferred_element_type=F32)
        x = x_ref[...]
        r = lax.rsqrt(jnp.mean(x * x, axis=-1, keepdims=True) + EPS)
        dx, dg = _rms_bwd(x, r, g_ref[...], dh)
        gx_ref[...] = dx1_ref[...] + dx
        _acc_rows(dg_ref, dg)

    tok = lambda i: (i, 0)
    full = lambda i: (0, 0)
    return _call(
        body, "inproj_bwd", (t // tm,),
        [pl.BlockSpec((tm, ZC), tok), pl.BlockSpec((ZC, D), full), pl.BlockSpec((tm, D), tok),
         pl.BlockSpec((tm, D), tok), pl.BlockSpec((1, D), full)],
        [pl.BlockSpec((tm, D), tok), pl.BlockSpec((8, D), full)],
        [jax.ShapeDtypeStruct((t, D), F32), jax.ShapeDtypeStruct((8, D), F32)], [], (dz, w_t, x, dx1, g), riders)


def _matmul_tn(a, b, name, rows=None, riders=()):
    t, k = a.shape
    n = b.shape[1]
    tk, tn = [1024 if size % 1024 == 0 else 640 for size in (k, n)]
    tt = min(TT, t)
    rows = rows or k

    def body(a_ref, b_ref, o_ref):
        @pl.when(pl.program_id(2) == 0)
        def _():
            o_ref[...] = jnp.zeros_like(o_ref)

        o_ref[...] += _dot_tn(a_ref[...], b_ref[...])

    return _call(
        body, name, (k // tk, n // tn, t // tt),
        [pl.BlockSpec((tt, tk), lambda i, j, s: (s, i)), pl.BlockSpec((tt, tn), lambda i, j, s: (s, j))],
        [pl.BlockSpec((tk, tn), lambda i, j, s: (i, j))], [jax.ShapeDtypeStruct((rows, n), F32)], [], (a, b), riders)


def _lane_head(shape, dim, shift):
    return _iota(shape, dim) >> shift


CUM_ROWS = 128


def _chunk_cumsum(x, upper, n):
    r, c = _iota((CUM_ROWS, CUM_ROWS), 0), _iota((CUM_ROWS, CUM_ROWS), 1)
    tri = (c >= r) if upper else (c <= r)
    cum = jnp.where(((r >> 6) == (c >> 6)) & tri, 1.0, 0.0).astype(BF16)
    return jnp.concatenate([_dot_exact_lhs(cum, x[g:g + CUM_ROWS], n) for g in range(0, x.shape[0], CUM_ROWS)],
                           axis=0)


def _gla_recompute(q_raw, k, lr, wpad, bias, rev, tb):
    pre = _dot(lr, wpad) + bias
    la = (jnp.minimum(pre, 0.0) - jnp.log(1.0 + jnp.exp(-jnp.abs(pre)))) * (1.0 / 16.0)
    b = _chunk_cumsum(la, rev, 3)
    e, ei = jnp.exp(b), jnp.exp(-b)
    qt = (q_raw * 0.125) * e
    kt = k * ei
    return pre, b, e, ei, qt, kt


def _stack_heads(x, shift):
    head = _lane_head(x.shape, 1, shift)
    return jnp.concatenate([jnp.where(head == h, x, 0.0) for h in range(NH)], axis=0).astype(BF16)


def _fold_heads(x, shift):
    head = _lane_head((CH, x.shape[1]), 1, shift)
    return functools.reduce(lambda a, b: a + b,
                            [jnp.where(head == h, x[h * CH:(h + 1) * CH], 0.0) for h in range(NH)])


def _wide_mask(rev):
    r, s = _iota((CH, NH * CH), 0), _iota((CH, NH * CH), 1) & (CH - 1)
    return (s >= r) if rev else (s <= r)


def _rows_by_head(x):
    w = x.shape[1] // NH
    return jnp.concatenate([x[:, h * w:(h + 1) * w] for h in range(NH)], axis=0)


def _lanes_by_head(x):
    return jnp.concatenate([x[h * CH:(h + 1) * CH] for h in range(NH)], axis=1)


def _state_compact(xt):
    head = _lane_head((128, GK), 1, 6)
    return functools.reduce(lambda a, b: a + b,
                            [jnp.where(head == h, xt[h * 128:(h + 1) * 128], 0.0) for h in range(NH)])


def _conv_parts(cb, cc, cu, ccp, cup, ccn, cun, cw_ref, first, last, tb):
    h = cc * cu
    hp = jnp.where(first, 0.0, ccp * cup)
    hn = jnp.where(last, 0.0, ccn * cun)
    rows = _iota(h.shape, 0)
    h_m1 = jnp.where(rows == 0, hp, pltpu.roll(h, 1, 0))
    h_p1 = jnp.where(rows == tb - 1, hn, pltpu.roll(h, tb - 1, 0))
    conv = cw_ref[pl.ds(0, 1), :] * h_m1 + cw_ref[pl.ds(1, 1), :] * h + cw_ref[pl.ds(2, 1), :] * h_p1
    return h, h_m1, h_p1, conv


def _head_sum(x, w_in, w_out):
    shape, sh_in, sh_out = (2 * w_in, 2 * w_out), w_in.bit_length() - 1, w_out.bit_length() - 1
    sel = jnp.where((_iota(shape, 0) >> sh_in) == (_iota(shape, 1) >> sh_out), 1.0, 0.0).astype(BF16)
    return jnp.concatenate([_dot_exact_rhs(x[:, s:s + 2 * w_in], sel, 2) for s in range(0, NH * w_in, 2 * w_in)],
                           axis=1)


def _group_sum(x):
    ones = jnp.where((_iota((128, 128), 0) >> 6) == (_iota((128, 128), 1) >> 6), 1.0, 0.0).astype(BF16)
    return jnp.concatenate([_dot_exact_rhs(x[:, s:s + 128], ones, 2) for s in range(0, x.shape[1], 128)], axis=1)


def _head_norm(o):
    ons, rs = [], []
    for h in range(NH):
        slab = o[:, h * 128:(h + 1) * 128]
        r = lax.rsqrt(jnp.mean(slab * slab, axis=-1, keepdims=True) + EPS)
        ons.append(slab * r)
        rs.append(jnp.broadcast_to(r, slab.shape))
    return jnp.concatenate(ons, axis=1), jnp.concatenate(rs, axis=1)


def _zspec(tb, width, blk, jmap):
    return pl.BlockSpec((tb, width), lambda i: (jmap(i), blk))


def _halo_specs(tb, nblk, t, blk, jmap):
    prev = pl.BlockSpec((8, CW), lambda i: (jnp.maximum(jmap(i) * (tb // 8) - 1, 0), blk))
    nxt = pl.BlockSpec((8, CW), lambda i: (jnp.minimum((jmap(i) + 1) * (tb // 8), t // 8 - 1), blk))
    return prev, nxt


def _gla_fwd_block(q_ref, k_ref, v_ref, lr_ref, w_ref, bias_ref, o_ref, sd_ref, st, b_scr, rev, tb):
    nb = tb // CH
    _, b, _, _, qt, kt = _gla_recompute(q_ref[...], k_ref[...], lr_ref[...], w_ref[...], bias_ref[...], rev, tb)
    v = v_ref[...]
    b_scr[...] = b
    yield
    maskw = _wide_mask(rev)
    order = list(reversed(range(nb))) if rev else list(range(nb))
    rows = [slice(c * CH, (c + 1) * CH) for c in range(nb)]
    state = st[...]
    for c in order:
        gdec = jnp.exp(b_scr[pl.ds(c * CH + (0 if rev else CH - 1), 1), :])
        sd_ref[c] = state
        a = jnp.where(maskw, _dot_nt(qt[rows[c]], _stack_heads(kt[rows[c]], 6)), 0.0)
        o_inter = _lanes_by_head(_dot_nt(_stack_heads(qt[rows[c]], 6), state))
        o_ref[pl.ds(c * CH, CH), :] = _dot(a, _stack_heads(v[rows[c]], 7)) + o_inter
        state = state * gdec + _state_compact(_dot_tn(v[rows[c]], kt[rows[c]] * gdec))
        yield
    st[...] = state
    yield


def _gla_fwd(z, waf_pad, b_af, wab_pad, b_ab, riders=()):
    t = z.shape[0]
    tb = min(TB, t)
    nblk, nb = t // tb, tb // CH
    jmaps = (lambda i: i, lambda i: nblk - 1 - i)

    def body(qf, kf, vf, lrf, qr, kr, vr, lrr, wf, bf, wr, br, of_ref, sdf_ref, or_ref, sdr_ref,
             st_f, st_r, b_f, b_r):
        @pl.when(pl.program_id(0) == 0)
        def _():
            st_f[...] = jnp.zeros_like(st_f)
            st_r[...] = jnp.zeros_like(st_r)

        for _ in zip(_gla_fwd_block(qf, kf, vf, lrf, wf, bf, of_ref, sdf_ref, st_f, b_f, False, tb),
                     _gla_fwd_block(qr, kr, vr, lrr, wr, br, or_ref, sdr_ref, st_r, b_r, True, tb)):
            pass

    full = lambda i: (0, 0)
    zspecs = [s for jm in jmaps for s in (_zspec(tb, GK, ZB_Q, jm), _zspec(tb, GK, ZB_K, jm),
                                         _zspec(tb, GV, ZB_V, jm), _zspec(tb, 128, ZB_LR, jm))]
    wspecs = [pl.BlockSpec((128, GK), full), pl.BlockSpec((1, GK), full)] * 2
    out_specs = [s for jm in jmaps for s in (pl.BlockSpec((tb, GV), lambda i, jm=jm: (jm(i), 0)),
                                             pl.BlockSpec((nb, 128, GK), lambda i, jm=jm: (jm(i), 0, 0)))]
    out_shape = [jax.ShapeDtypeStruct((t, GV), F32), jax.ShapeDtypeStruct((t // CH, 128, GK), F32)] * 2
    scratch = [pltpu.VMEM((128, GK), F32), pltpu.VMEM((128, GK), F32), pltpu.VMEM((tb, GK), F32),
               pltpu.VMEM((tb, GK), F32)]
    return _call(body, "gla_fwd", (nblk,), zspecs + wspecs, out_specs, out_shape, scratch,
                 [z] * 8 + [waf_pad, b_af, wab_pad, b_ab], riders)


def _gla_bwd_chunks(do_ref, sd_ref, dst, b_scr, db_scr, dq_ref, dk_ref, dv_ref, qt, kt, e, ei, v, rev, nb):
    maskw = _wide_mask(rev)
    for c in (range(nb) if rev else reversed(range(nb))):
        sl = slice(c * CH, (c + 1) * CH)
        grow = c * CH + (0 if rev else CH - 1)
        gdec = jnp.exp(b_scr[pl.ds(grow, 1), :])
        qt_c, kt_c, v_c, do_c = qt[sl], kt[sl], v[sl], do_ref[pl.ds(c * CH, CH), :]
        s_in, ds_out = sd_ref[c], dst[...]
        kbd, vbd = _stack_heads(kt_c, 6), _stack_heads(v_c, 7)
        a = jnp.where(maskw, _dot_nt(qt_c, kbd), 0.0)
        da = jnp.where(maskw, _dot_nt(do_c, vbd), 0.0)
        dv_ref[pl.ds(c * CH, CH), :] = (_fold_heads(_dot_tn(a, do_c), 7)
                                        + _lanes_by_head(_dot_nt(_stack_heads(kt_c * gdec, 6), ds_out)))
        dqt = _dot(da, kbd) + _fold_heads(_dot(_rows_by_head(do_c), s_in), 6)
        dkh = _fold_heads(_dot(_rows_by_head(v_c), ds_out), 6)
        da_do = jnp.concatenate([da.astype(BF16), do_c.astype(BF16)], axis=1)
        both = _dot_tn(da_do, qt_c)
        dkt = _fold_heads(both[:NH * CH], 6) + dkh * gdec
        dg = jnp.sum(ds_out * s_in, axis=0, keepdims=True) + jnp.sum(kt_c * dkh, axis=0, keepdims=True)
        db_scr[pl.ds(c * CH, CH), :] = dqt * qt_c - dkt * kt_c
        db_scr[pl.ds(grow, 1), :] += dg * gdec
        dq_ref[pl.ds(c * CH, CH), :] = dqt * e[sl] * 0.125
        dk_ref[pl.ds(c * CH, CH), :] = dkt * ei[sl]
        dst[...] = ds_out * gdec + _state_compact(both[NH * CH:])
        yield


def _gate_bwd(db, pre, lr, wpad, rev, tb):
    dla = _chunk_cumsum(db, not rev, 2)
    dpre = dla * (1.0 / 16.0) / (1.0 + jnp.exp(pre))
    return dpre, _dot_nt(dpre, wpad), _dot_tn(lr, dpre)


def _gla_bwd_first(z, dy, o_pre, sd, wpad, bias, conv_w, conv_norm, gla_norm4, riders=()):
    t = z.shape[0]
    tb = min(TB_BWD, t)
    nblk, nb = t // tb, tb // CH
    jmap = lambda i: nblk - 1 - i

    def body(q_ref, k_ref, v_ref, lr_ref, g_ref, cb_ref, cc_ref, cu_ref, ccp_ref, ccn_ref, cup_ref, cun_ref,
             dy_ref, opre_ref, sd_ref, w_ref, bias_ref, cw_ref, cn_ref, gn_ref,
             do_ref, dq_ref, dk_ref, dv_ref, dlr_ref, dzg_ref, dzcb_ref, dconv_ref,
             dw_ref, dbias_ref, dcw_ref, dcn_ref, dgn_ref, dst, b_scr, db_scr):
        i = pl.program_id(0)
        j = jmap(i)

        @pl.when(i == 0)
        def _():
            dst[...] = jnp.zeros_like(dst)
            for ref in (dw_ref, dbias_ref, dcw_ref, dcn_ref, dgn_ref):
                ref[...] = jnp.zeros_like(ref)

        dyg = dy_ref[:, CW:]
        g = g_ref[...]
        sig = _sigmoid(g)
        on, rr = _head_norm(opre_ref[...])
        gn = gn_ref[...]
        dzg_ref[...] = (dyg * on * gn * (sig * (1.0 + g * (1.0 - sig)))).astype(BF16)
        don = dyg * (g * sig)
        _acc_rows(dgn_ref, jnp.sum(don * on, axis=0, keepdims=True))
        u = don * gn
        uo = u * on
        mean_uo = jnp.concatenate(
            [jnp.broadcast_to(jnp.mean(uo[:, h * 128:(h + 1) * 128], axis=-1, keepdims=True), (tb, 128))
             for h in range(NH)], axis=1)
        do_ref[...] = rr * (u - on * mean_uo)

        def conv_branch():
            cb = cb_ref[...]
            h, h_m1, h_p1, conv = _conv_parts(cb, cc_ref[...], cu_ref[...], ccp_ref[pl.ds(7, 1), :],
                                              cup_ref[pl.ds(7, 1), :], ccn_ref[pl.ds(0, 1), :],
                                              cun_ref[pl.ds(0, 1), :], cw_ref, j == 0, j == nblk - 1, tb)
            yc = cb * conv
            yield
            rc = lax.rsqrt(_group_sum(yc * yc) * (1.0 / 64.0) + EPS)
            ycr = yc * rc
            yield
            dyn = dy_ref[:, :CW]
            _acc_rows(dcn_ref, jnp.sum(dyn * ycr, axis=0, keepdims=True))
            uc = dyn * cn_ref[...]
            yield
            dyc = rc * (uc - ycr * (_group_sum(uc * ycr) * (1.0 / 64.0)))
            dzcb_ref[...] = (dyc * conv).astype(BF16)
            yield
            dconv = dyc * cb
            dconv_ref[...] = dconv
            yield
            dcw_ref[pl.ds(0, 1), :] += jnp.sum(dconv * h_m1, axis=0, keepdims=True)
            dcw_ref[pl.ds(1, 1), :] += jnp.sum(dconv * h, axis=0, keepdims=True)
            dcw_ref[pl.ds(2, 1), :] += jnp.sum(dconv * h_p1, axis=0, keepdims=True)
            yield

        lr, wp = lr_ref[...], w_ref[...]
        pre, b, e, ei, qt, kt = _gla_recompute(q_ref[...], k_ref[...], lr, wp, bias_ref[...], False, tb)
        b_scr[...] = b
        for _ in itertools.zip_longest(
                _gla_bwd_chunks(do_ref, sd_ref, dst, b_scr, db_scr, dq_ref, dk_ref, dv_ref, qt, kt, e, ei, v_ref[...],
                                False, nb), conv_branch()):
            pass
        dpre, dlr, dw = _gate_bwd(db_scr[...], pre, lr, wp, False, tb)
        dlr_ref[...] = dlr
        dw_ref[...] += dw
        _acc_rows(dbias_ref, jnp.sum(dpre, axis=0, keepdims=True))

    full = lambda i: (0, 0)
    tokv = pl.BlockSpec((tb, GV), lambda i: (jmap(i), 0))
    tokk = pl.BlockSpec((tb, GK), lambda i: (jmap(i), 0))
    ccp, ccn = _halo_specs(tb, nblk, t, ZB_CC, jmap)
    cup, cun = _halo_specs(tb, nblk, t, ZB_CU, jmap)
    in_specs = [_zspec(tb, GK, ZB_Q, jmap), _zspec(tb, GK, ZB_K, jmap), _zspec(tb, GV, ZB_V, jmap),
                _zspec(tb, 128, ZB_LR, jmap), _zspec(tb, GV, ZB_G, jmap), _zspec(tb, CW, ZB_CB, jmap),
                _zspec(tb, CW, ZB_CC, jmap), _zspec(tb, CW, ZB_CU, jmap), ccp, ccn, cup, cun,
                pl.BlockSpec((tb, D), lambda i: (jmap(i), 0)), tokv,
                pl.BlockSpec((nb, 128, GK), lambda i: (jmap(i), 0, 0)), pl.BlockSpec((128, GK), full),
                pl.BlockSpec((1, GK), full), pl.BlockSpec((3, CW), full), pl.BlockSpec((1, CW), full),
                pl.BlockSpec((1, GV), full)]
    out_specs = [tokv, tokk, tokk, tokv, pl.BlockSpec((tb, 128), lambda i: (jmap(i), 0)), tokv, tokv, tokv,
                 pl.BlockSpec((128, GK), full), pl.BlockSpec((8, GK), full), pl.BlockSpec((8, CW), full),
                 pl.BlockSpec((8, CW), full), pl.BlockSpec((8, GV), full)]
    out_shape = [jax.ShapeDtypeStruct((t, GV), F32), jax.ShapeDtypeStruct((t, GK), F32),
                 jax.ShapeDtypeStruct((t, GK), F32), jax.ShapeDtypeStruct((t, GV), F32),
                 jax.ShapeDtypeStruct((t, 128), F32), jax.ShapeDtypeStruct((t, GV), BF16),
                 jax.ShapeDtypeStruct((t, CW), BF16), jax.ShapeDtypeStruct((t, CW), F32),
                 jax.ShapeDtypeStruct((128, GK), F32), jax.ShapeDtypeStruct((8, GK), F32),
                 jax.ShapeDtypeStruct((8, CW), F32), jax.ShapeDtypeStruct((8, CW), F32),
                 jax.ShapeDtypeStruct((8, GV), F32)]
    return _call(
        body, "gla_bwd_first", (nblk,), in_specs, out_specs, out_shape,
        [pltpu.VMEM((128, GK), F32), pltpu.VMEM((tb, GK), F32), pltpu.VMEM((tb, GK), F32)],
        (z, z, z, z, z, z, z, z, z, z, z, z, dy, o_pre, sd, wpad, bias, conv_w, conv_norm, gla_norm4), riders)


def _gla_bwd_second(z, do, sd, wpad, bias, dqa, dka, dva, dlra, dzg, dzcb, dconv, conv_w, riders=()):
    t = z.shape[0]
    tb = min(TB_BWD, t)
    nblk, nb = t // tb, tb // CH
    jmap = lambda i: i

    def body(q_ref, k_ref, v_ref, lr_ref, cc_ref, cu_ref, do_ref, sd_ref, w_ref, bias_ref, dqa_ref, dka_ref,
             dva_ref, dlra_ref, dzg_ref, dzcb_ref, dc_ref, dcp_ref, dcn_ref, cw_ref,
             dz_ref, dw_ref, dbias_ref, dst, b_scr, db_scr, dq_scr, dk_scr, dv_scr, sb_scr, dsk_scr):
        i = pl.program_id(0)

        @pl.when(i == 0)
        def _():
            dst[...] = jnp.zeros_like(dst)
            dw_ref[...] = jnp.zeros_like(dw_ref)
            dbias_ref[...] = jnp.zeros_like(dbias_ref)

        q_raw, k, v, lr, wp = q_ref[...], k_ref[...], v_ref[...], lr_ref[...], w_ref[...]
        pre, b, e, ei, qt, kt = _gla_recompute(q_raw, k, lr, wp, bias_ref[...], True, tb)
        b_scr[...] = b

        def token_local():
            dc = dc_ref[...]
            rows = _iota(dc.shape, 0)
            dprev = jnp.where(i == 0, 0.0, dcp_ref[pl.ds(7, 1), :])
            dnext = jnp.where(i == nblk - 1, 0.0, dcn_ref[pl.ds(0, 1), :])
            dc_m1 = jnp.where(rows == 0, dprev, pltpu.roll(dc, 1, 0))
            dc_p1 = jnp.where(rows == tb - 1, dnext, pltpu.roll(dc, tb - 1, 0))
            yield
            dh = cw_ref[pl.ds(0, 1), :] * dc_p1 + cw_ref[pl.ds(1, 1), :] * dc + cw_ref[pl.ds(2, 1), :] * dc_m1
            dz_ref[:, 0:512] = dzcb_ref[...]
            yield
            dz_ref[:, 512:1024] = (dh * cu_ref[...]).astype(BF16)
            dz_ref[:, 1024:1536] = (dh * cc_ref[...]).astype(BF16)
            dz_ref[:, 2560:3072] = dzg_ref[...]
            yield
            sb_scr[...] = _head_sum((q_raw * 0.125) * k, 64, 128)
            yield
            dsk_scr[...] = _head_sum(do_ref[...] * v, 128, 64)
            yield

        for _ in itertools.zip_longest(
                _gla_bwd_chunks(do_ref, sd_ref, dst, b_scr, db_scr, dq_scr, dk_scr, dv_scr, qt, kt, e, ei, v, True, nb),
                token_local()):
            pass
        dpre, dlr, dw = _gate_bwd(db_scr[...], pre, lr, wp, True, tb)
        dw_ref[...] += dw
        _acc_rows(dbias_ref, jnp.sum(dpre, axis=0, keepdims=True))
        dsk = dsk_scr[...]
        dz_ref[:, 1536:1792] = (dqa_ref[...] + dq_scr[...] - dsk * k * 0.125).astype(BF16)
        dz_ref[:, 1792:2048] = (dka_ref[...] + dk_scr[...] - dsk * (q_raw * 0.125)).astype(BF16)
        dz_ref[:, 2048:2560] = (dva_ref[...] + dv_scr[...] - sb_scr[...] * do_ref[...]).astype(BF16)
        dz_ref[:, 3072:3200] = (dlra_ref[...] + dlr).astype(BF16)

    full = lambda i: (0, 0)
    tokv = pl.BlockSpec((tb, GV), lambda i: (i, 0))
    tokk = pl.BlockSpec((tb, GK), lambda i: (i, 0))
    dcp = pl.BlockSpec((8, CW), lambda i: (jnp.maximum(i * (tb // 8) - 1, 0), 0))
    dcn = pl.BlockSpec((8, CW), lambda i: (jnp.minimum((i + 1) * (tb // 8), t // 8 - 1), 0))
    in_specs = [_zspec(tb, GK, ZB_Q, jmap), _zspec(tb, GK, ZB_K, jmap), _zspec(tb, GV, ZB_V, jmap),
                _zspec(tb, 128, ZB_LR, jmap), _zspec(tb, CW, ZB_CC, jmap), _zspec(tb, CW, ZB_CU, jmap), tokv,
                pl.BlockSpec((nb, 128, GK), lambda i: (i, 0, 0)), pl.BlockSpec((128, GK), full),
                pl.BlockSpec((1, GK), full), tokk, tokk, tokv, pl.BlockSpec((tb, 128), lambda i: (i, 0)), tokv, tokv,
                tokv, dcp, dcn, pl.BlockSpec((3, CW), full)]
    out_specs = [pl.BlockSpec((tb, ZC), lambda i: (i, 0)), pl.BlockSpec((128, GK), full), pl.BlockSpec((8, GK), full)]
    out_shape = [jax.ShapeDtypeStruct((t, ZC), BF16), jax.ShapeDtypeStruct((128, GK), F32),
                 jax.ShapeDtypeStruct((8, GK), F32)]
    return _call(
        body, "gla_bwd_second", (nblk,), in_specs, out_specs, out_shape,
        [pltpu.VMEM((128, GK), F32), pltpu.VMEM((tb, GK), F32), pltpu.VMEM((tb, GK), F32),
         pltpu.VMEM((tb, GK), F32), pltpu.VMEM((tb, GK), F32), pltpu.VMEM((tb, GV), F32),
         pltpu.VMEM((tb, GV), F32), pltpu.VMEM((tb, GK), F32)],
        (z, z, z, z, z, z, do, sd, wpad, bias, dqa, dka, dva, dlra, dzg, dzcb, dconv, dconv, dconv, conv_w), riders)


def _step(x, mem, target, shard, small_pack, vec, place):
    own, from_chips = {}, {}

    def pair_sums(names, g4, from_sibling):
        pbs = {}
        for shape in dict.fromkeys(g.shape for g in g4):
            idx = [i for i, g in enumerate(g4) if g.shape == shape]
            pb, mine = _rs_pair_sum(place, [g4[i] for i in idx], [from_sibling[i] for i in idx],
                                    "pair_sum_" + "_".join(names[i] for i in idx))
            for i, b, o in zip(idx, pb, mine):
                pbs[i], own[names[i]] = b, o
        return [pbs[i] for i in range(len(g4))]

    def by_dest(g, n):
        return g.reshape((4, 2) + shard[n].shape)

    w_in, small_all = _exchange(_gather_rider([shard["w_in"], small_pack]), "gather_w_in")
    w_in = jnp.pad(w_in.reshape(ZW, D), ((0, ZC - ZW), (0, 0)))
    small_all = small_all.reshape(NDEV, -1)
    p, off = {}, 0
    for n, (r, c) in SMALL_SHARDED.items():
        p[n] = small_all[:, off:off + r * c].reshape(NDEV, r, c).transpose(1, 0, 2).reshape(r, NDEV * c)
        off += r * c
    zeros_lr = jnp.zeros((128 - LR, GK), BF16)
    waf_pad = jnp.concatenate([p["w_af"].astype(BF16), zeros_lr], axis=0)
    wab_pad = jnp.concatenate([jnp.zeros((LR, GK), BF16), p["w_ab"].astype(BF16), zeros_lr[:128 - 2 * LR]], axis=0)
    gla_norm4 = jnp.tile(vec["gla_norm"], (1, NH))

    z, hb, w_out, w_xq, w_xo, w_xkv = _inproj(
        x, vec["mix_norm"], w_in, [_gather_rider([shard[n] for n in ("w_out", "w_xq", "w_xo", "w_xkv")])])
    w_out, w_xq, w_xo = [a.reshape(D, D) for a in (w_out, w_xq, w_xo)]
    o_f, sd_f, o_b, sd_b, w_up_t, w_down = _gla_fwd(
        z, waf_pad, vec["b_af"], wab_pad, vec["b_ab"], [_gather_rider([shard["w_up"], shard["w_down"]])])
    w_up_t, w_down = w_up_t.reshape(FF, D), w_down.reshape(FF, D)
    kv, memn = _kv_proj(mem, vec["mem_norm"], w_xkv)
    kb, vb = kv[:, :D].astype(BF16), kv[:, D:].astype(BF16)
    x1, x2, xn1, qb, attb, yb, o_pre = _attn_fwd(x, z, o_f, o_b, p["conv_w"], vec["conv_norm"], gla_norm4, w_out,
                                                 vec["xa_norm"], w_xq, kb, vb, w_xo)
    h1b, xn2, dx3, dx3b, loss8, dfinal = _mlp_fwd(x2, vec["mlp_norm"], w_up_t, w_down, vec["final_norm"], target)

    ab, dh1b, dx2, dx2b, dmlp = _mlp_bwd(dx3, dx3b, h1b, w_down, w_up_t, x2, vec["mlp_norm"])
    g_mlp = [by_dest(_matmul_tn(ab, dx3b, "dw_down")[0], "w_down"),
             by_dest(_matmul_tn(dh1b, xn2, "dw_up")[0], "w_up")]
    dx1, dx1b, dy, dqb, dkv, dxa, *s_mlp = _attn_bwd(x1, dx2, dx2b, qb, kb, vb, w_xo, w_xq, w_out, vec["xa_norm"],
                                                     riders=[_sibling_rider(g_mlp)])
    pb_mlp = pair_sums(("w_down", "w_up"), g_mlp, s_mlp)
    dw_xo = _matmul_tn(attb, dx2b, "dw_xo")[0]
    dw_xkv, dmemn = _kv_bwd(dkv, memn, mem, vec["mem_norm"], w_xkv)
    att_names = ("w_xo", "w_xq", "w_out", "w_xkv")
    g_att = [by_dest(g, n) for g, n in zip(
        (dw_xo, _matmul_tn(xn1, dqb, "dw_xq")[0], _matmul_tn(yb, dx1b, "dw_out")[0], dw_xkv), att_names)]
    res = _gla_bwd_first(z, dy, o_pre, sd_f, waf_pad, vec["b_af"], p["conv_w"], vec["conv_norm"], gla_norm4,
                         riders=[_chips_rider(pb_mlp), _sibling_rider(g_att)])
    do, dqa, dka, dva, dlra, dzg, dzcb, dconv, dwaf, dbaf, dcw, dcn, dgn = res[:13]
    from_chips["w_down"], from_chips["w_up"] = res[13:15]
    pb_att = pair_sums(att_names, g_att, res[15:])
    dz, dwab, dbab, *c_att = _gla_bwd_second(z, do, sd_b, wab_pad, vec["b_ab"], dqa, dka, dva, dlra, dzg, dzcb, dconv,
                                             p["conv_w"], riders=[_chips_rider(pb_att)])
    from_chips.update(zip(att_names, c_att))
    g_in = [by_dest(_matmul_tn(dz, hb, "dw_in", rows=ZW)[0], "w_in")]
    pb_in = pair_sums(("w_in",), g_in, _exchange(_sibling_rider(g_in), "grads_to_sibling_w_in"))
    grad_x, dmix, from_chips["w_in"] = _inproj_bwd(dz, w_in, x, dx1, vec["mix_norm"], riders=[_chips_rider(pb_in)])

    small_acc = dict(mix_norm=dmix, conv_w=dcw, conv_norm=dcn, w_af=dwaf, b_af=dbaf, w_ab=dwab, b_ab=dbab,
                     gla_norm=dgn, xa_norm=dxa, mem_norm=dmemn, mlp_norm=dmlp, final_norm=dfinal)
    return loss8, grad_x, small_acc, own, from_chips


def _place():
    return lax.axis_index("x"), lax.axis_index("y"), lax.axis_index("c")


class _Rider:
    def __init__(self, arrays, out_shape, scratch, start, finish, forward=None, relay=None):
        self.arrays, self.out_shape, self.scratch, self.start, self.finish = arrays, out_shape, scratch, start, finish
        self.forward, self.relay = forward, relay


def _gather_rider(blks, early_relay=True):
    n = len(blks)

    def plan(in_refs, out_refs, sems):
        send_sems, recv_sems, local_sems = sems
        x, y, c = _place()
        me, sibling = (x, y, c), (x, y, 1 - c)
        xn, yn, dg = (1 - x, y, c), (x, 1 - y, c), (1 - x, 1 - y, c)
        via = (x + (1 - c) * (1 - 2 * x), y + c * (1 - 2 * y), c)
        onto = (x + c * (1 - 2 * x), y + (1 - c) * (1 - 2 * y), c)
        other = onto

        def copy(a, k, block, to, own=False):
            px, py, pc = block
            dst = out_refs[a].at[4 * px + 2 * py + pc]
            return pltpu.make_async_remote_copy(
                src_ref=in_refs[a] if own else dst, dst_ref=dst, send_sem=send_sems.at[k, a],
                recv_sem=recv_sems.at[k, a], device_id=to, device_id_type=MESH)

        def local(a):
            return pltpu.make_async_copy(in_refs[a], out_refs[a].at[4 * x + 2 * y + c], local_sems.at[a])

        def sends(a):
            return ([copy(a, 0, me, sibling, own=True), copy(a, 1, me, xn, own=True), copy(a, 2, me, yn, own=True),
                     copy(a, 3, via, onto), copy(a, 4 + c, via, sibling), copy(a, 5 - c, other, sibling),
                     copy(a, 6, dg, sibling)])

        return copy, local, sends, me, sibling, (xn, yn, dg), via, other

    def start(in_refs, out_refs, sems):
        _, local, sends, _, _, _, _, _ = plan(in_refs, out_refs, sems)
        for a in range(n):
            local(a).start()
            for cp in sends(a)[:3]:
                cp.start()

    def forward(in_refs, out_refs, sems):
        copy, _, sends, me, _, _, via, _ = plan(in_refs, out_refs, sems)
        for a in range(n):
            copy(a, 1 + me[2], via, me).wait_recv()
            sends(a)[3].start()
            sends(a)[4].start()

    def relay(in_refs, out_refs, sems):
        copy, _, sends, me, _, (_, _, dg), _, other = plan(in_refs, out_refs, sems)
        for a in range(n):
            copy(a, 2 - me[2], other, me).wait_recv()
            sends(a)[5].start()
        for a in range(n):
            copy(a, 3, dg, me).wait_recv()
            sends(a)[6].start()

    def finish(in_refs, out_refs, sems):
        if not early_relay:
            forward(in_refs, out_refs, sems)
            relay(in_refs, out_refs, sems)
        copy, local, sends, me, sibling, chips, _, _ = plan(in_refs, out_refs, sems)
        for a in range(n):
            copy(a, 0, sibling, me).wait_recv()
            for j, (px, py, pc) in enumerate(chips):
                copy(a, 4 + j, (px, py, 1 - pc), me).wait_recv()
            for cp in sends(a):
                cp.wait_send()
            local(a).wait()

    return _Rider(blks, [jax.ShapeDtypeStruct((NDEV,) + b.shape, b.dtype) for b in blks],
                  [pltpu.SemaphoreType.DMA((7, n)), pltpu.SemaphoreType.DMA((7, n)), pltpu.SemaphoreType.DMA((n,))],
                  start, finish, forward if early_relay else None, relay if early_relay else None)


def _sibling_rider(g4s):
    n = len(g4s)

    def copies(in_refs, out_refs, sems):
        send_sems, recv_sems = sems
        x, y, c = _place()
        return [pltpu.make_async_remote_copy(
            src_ref=in_refs[a].at[k, 1 - c], dst_ref=out_refs[a].at[k], send_sem=send_sems.at[k, a],
            recv_sem=recv_sems.at[k, a], device_id=(x, y, 1 - c), device_id_type=MESH)
            for a in range(n) for k in range(4)]

    def start(in_refs, out_refs, sems):
        for cp in copies(in_refs, out_refs, sems):
            cp.start()

    def finish(in_refs, out_refs, sems):
        for cp in copies(in_refs, out_refs, sems):
            cp.wait()

    return _Rider(g4s, [jax.ShapeDtypeStruct((4,) + g.shape[2:], g.dtype) for g in g4s],
                  [pltpu.SemaphoreType.DMA((4, n)), pltpu.SemaphoreType.DMA((4, n))], start, finish)


def _chips_rider(pbs):
    n = len(pbs)

    def copies(in_refs, out_refs, sems):
        send_sems, recv_sems = sems
        x, y, c = _place()
        peers = [(1 - x, y), (x, 1 - y), (1 - x, 1 - y)]
        return [pltpu.make_async_remote_copy(
            src_ref=in_refs[a].at[2 * px + py], dst_ref=out_refs[a].at[k], send_sem=send_sems.at[k, a],
            recv_sem=recv_sems.at[k, a], device_id=(px, py, c), device_id_type=MESH)
            for a in range(n) for k, (px, py) in enumerate(peers)]

    def start(in_refs, out_refs, sems):
        for cp in copies(in_refs, out_refs, sems):
            cp.start()

    def finish(in_refs, out_refs, sems):
        for cp in copies(in_refs, out_refs, sems):
            cp.wait()

    return _Rider(pbs, [jax.ShapeDtypeStruct((3,) + p.shape[1:], p.dtype) for p in pbs],
                  [pltpu.SemaphoreType.DMA((3, n)), pltpu.SemaphoreType.DMA((3, n))], start, finish)


def _exchange(rider, name):
    n_in, n_out = len(rider.arrays), len(rider.out_shape)

    def body(*refs):
        ins, outs, sems = refs[:n_in], refs[n_in:n_in + n_out], refs[n_in + n_out:]
        rider.start(ins, outs, sems)
        for hook in (rider.forward, rider.relay):
            if hook:
                hook(ins, outs, sems)
        rider.finish(ins, outs, sems)

    hbm = pl.BlockSpec(memory_space=pltpu.HBM)
    return pl.pallas_call(body, name=name, out_shape=rider.out_shape, in_specs=[hbm] * n_in,
                          out_specs=[hbm] * n_out, scratch_shapes=rider.scratch)(*rider.arrays)


def _rs_pair_sum(place, g4s, r1s, name):
    n = len(g4s)
    rows, cols = g4s[0].shape[2:]
    tr = min(rows, 512)

    def body(pl_ref, *refs):
        for g_ref, r_ref, pb_ref, own_ref in zip(refs[:n], refs[n:2 * n], refs[2 * n:3 * n], refs[3 * n:]):
            s = g_ref[0, 0] + r_ref[0]
            pb_ref[0] = s.astype(BF16)

            @pl.when(pl.program_id(1) == pl_ref[0])
            def _():
                own_ref[...] = s

    grid_spec = pltpu.PrefetchScalarGridSpec(
        num_scalar_prefetch=1, grid=(rows // tr, 4),
        in_specs=[pl.BlockSpec((1, 1, tr, cols), lambda r, k, p: (k, p[1], r, 0))] * n
        + [pl.BlockSpec((1, tr, cols), lambda r, k, p: (k, r, 0))] * n,
        out_specs=[pl.BlockSpec((1, tr, cols), lambda r, k, p: (k, r, 0))] * n
        + [pl.BlockSpec((tr, cols), lambda r, k, p: (r, 0))] * n)
    res = pl.pallas_call(
        body, name=name, grid_spec=grid_spec,
        out_shape=[jax.ShapeDtypeStruct((4, rows, cols), BF16)] * n + [jax.ShapeDtypeStruct((rows, cols), F32)] * n,
        compiler_params=_cparams(("arbitrary", "arbitrary")))(place, *g4s, *r1s)
    return res[:n], res[n:]


PACK_ROWS = 32
VEC_ROW = {"mix_norm": 0, "conv_norm": 1, "b_af": 2, "b_ab": 3, "gla_norm": 4, "xa_norm": 5, "mem_norm": 6,
           "mlp_norm": 7, "final_norm": 8}
LOSS_ROW, MAT_ROW = 9, 16
MAT_LANE = {"w_af": 0, "w_ab": GK, "conv_w": 2 * GK}
MAT_SRC_ROW = {"w_af": 0, "w_ab": LR, "conv_w": 0}


SMALL_WIDTH = {"mix_norm": D, "conv_w": 64, "conv_norm": CW, "w_af": 32, "b_af": GK, "w_ab": 32, "b_ab": GK,
               "gla_norm": 128, "xa_norm": D, "mem_norm": D, "mlp_norm": D, "final_norm": D}


def _small_reduce(acc, loss8):
    names = list(SMALL)
    n = len(names)
    widths = SMALL_WIDTH

    def body(*refs):
        acc_refs = dict(zip(names, refs[:n]))
        loss_ref, tot = refs[n], refs[n + 1]
        pk, all_ref, send_sems, recv_sems, local_sem = refs[n + 2:]

        pk[...] = jnp.zeros_like(pk)
        for k, row in VEC_ROW.items():
            if k == "gla_norm":
                g = functools.reduce(lambda a, b: a + b, [acc_refs[k][pl.ds(0, 1), pl.ds(h * 128, 128)]
                                                          for h in range(NH)])
            else:
                g = acc_refs[k][pl.ds(0, 1), :]
            pk[pl.ds(row, 1), pl.ds(0, widths[k])] = g
        pk[pl.ds(LOSS_ROW, 1), pl.ds(0, 128)] = loss_ref[pl.ds(0, 1), :]
        for k, lane in MAT_LANE.items():
            rows, cols = (3, CW) if k == "conv_w" else (LR, GK)
            pk[pl.ds(MAT_ROW, rows), pl.ds(lane, cols)] = acc_refs[k][pl.ds(MAT_SRC_ROW[k], rows), :]

        x, y, c = _place()
        me, sibling = (x, y, c), (x, y, 1 - c)
        chips = [(1 - x, y, c), (x, 1 - y, c), (1 - x, 1 - y, c)]

        def copy(k, block, to, own=False):
            px, py, pc = block
            dst = all_ref.at[4 * px + 2 * py + pc]
            return pltpu.make_async_remote_copy(
                src_ref=pk if own else dst, dst_ref=dst, send_sem=send_sems.at[k], recv_sem=recv_sems.at[k],
                device_id=to, device_id_type=MESH)

        mine = pltpu.make_async_copy(pk, all_ref.at[4 * x + 2 * y + c], local_sem)
        mine.start()
        first = [copy(0, me, sibling, own=True)] + [copy(1 + j, me, chip, own=True) for j, chip in enumerate(chips)]
        for cp in first:
            cp.start()
        passed = [copy(4 + j, chip, sibling) for j, chip in enumerate(chips)]
        for j, chip in enumerate(chips):
            copy(1 + j, chip, me).wait_recv()
            passed[j].start()
        copy(0, sibling, me).wait_recv()
        for j, (px, py, pc) in enumerate(chips):
            copy(4 + j, (px, py, 1 - pc), me).wait_recv()
        for cp in first + passed:
            cp.wait_send()
        mine.wait()
        total = all_ref[0]
        for d in range(1, NDEV):
            total = total + all_ref[d]
        tot[...] = total

    return pl.pallas_call(
        body, name="small_reduce", out_shape=jax.ShapeDtypeStruct((PACK_ROWS, D), F32),
        scratch_shapes=[pltpu.VMEM((PACK_ROWS, D), F32), pltpu.VMEM((NDEV, PACK_ROWS, D), F32),
                        pltpu.SemaphoreType.DMA((7,)), pltpu.SemaphoreType.DMA((7,)), pltpu.SemaphoreType.DMA],
    )(*[acc[k] for k in names], loss8)


def _small_adamw(tot, ws, ms, vs):
    names = list(SMALL)
    n = len(names)
    widths = SMALL_WIDTH

    def body(*refs):
        tot = refs[0]
        w_refs, m_refs, v_refs = [dict(zip(names, refs[1 + q * n:1 + (q + 1) * n])) for q in range(3)]
        outs = refs[1 + 3 * n:1 + 7 * n]
        g_out, d_out, m_out, v_out = [dict(zip(names, outs[q * n:(q + 1) * n])) for q in range(4)]
        cut = refs[1 + 7 * n]
        x, y, c = _place()
        dev = 4 * x + 2 * y + c
        for k in names:
            if k in VEC_ROW:
                g = tot[pl.ds(VEC_ROW[k], 1), pl.ds(0, widths[k])]
            else:
                rows, cols = (3, CW) if k == "conv_w" else (LR, GK)
                wd = widths[k]
                sel = jnp.where(_iota((cols, wd), 0) == dev * wd + _iota((cols, wd), 1), 1.0, 0.0).astype(BF16)
                cut[:, pl.ds(0, wd)] = _dot_exact_rhs(tot[pl.ds(MAT_ROW, LR), pl.ds(MAT_LANE[k], cols)], sel, 3)
                g = cut[pl.ds(0, rows), pl.ds(0, wd)]
            g_out[k][...] = g
            d_out[k][...], m_out[k][...], v_out[k][...] = _adamw_math(w_refs[k][...], g, m_refs[k][...],
                                                                       v_refs[k][...])

    shapes = [jax.ShapeDtypeStruct(ws[k].shape, F32) for k in names]
    res = pl.pallas_call(
        body, name="small_adamw", out_shape=shapes * 4, scratch_shapes=[pltpu.VMEM((LR, 128), F32)],
    )(tot, *[ws[k] for k in names], *[ms[k] for k in names], *[vs[k] for k in names])
    return {k: tuple(res[q * n + i] for q in range(4)) for i, k in enumerate(names)}


def _adamw_math(w, g, m, v):
    m = ADAM_B1 * m + (1.0 - ADAM_B1) * g
    v = ADAM_B2 * v + (1.0 - ADAM_B2) * (g * g)
    m_hat = m / (1.0 - ADAM_B1 ** ADAM_STEP)
    v_hat = v / (1.0 - ADAM_B2 ** ADAM_STEP)
    delta = -ADAM_LR * (m_hat / (jnp.sqrt(v_hat) + ADAM_EPS) + ADAM_WD * w)
    return delta, m, v


def _adamw(ws, ms, vs, owns, r2s, name, grads_transposed=False):
    n = len(ws)
    _, r, c = ws[0].shape
    tr = 256 if r % 256 == 0 else r

    def body(*refs):
        ins, outs = refs[:5 * n], refs[5 * n:]
        for q in range(n):
            w_ref, m_ref, v_ref, o_ref, r_ref = [ins[k * n + q] for k in range(5)]
            g_ref, d_ref, nm_ref, nv_ref = [outs[k * n + q] for k in range(4)]
            g = ((o_ref[...] + r_ref[0].astype(F32)) + r_ref[1].astype(F32)) + r_ref[2].astype(F32)
            g = g.T if grads_transposed else g
            g_ref[...] = g
            d_ref[...], nm_ref[...], nv_ref[...] = _adamw_math(w_ref[...], g, m_ref[...], v_ref[...])

    spec = pl.BlockSpec((None, tr, c), lambda i: (0, i, 0))
    if grads_transposed:
        own_spec, r2_spec = pl.BlockSpec((c, tr), lambda i: (0, i)), pl.BlockSpec((3, c, tr), lambda i: (0, 0, i))
    else:
        own_spec, r2_spec = pl.BlockSpec((tr, c), lambda i: (i, 0)), pl.BlockSpec((3, tr, c), lambda i: (0, i, 0))
    res = pl.pallas_call(
        body, name=name, grid=(r // tr,),
        in_specs=[spec] * (3 * n) + [own_spec] * n + [r2_spec] * n,
        out_specs=[spec] * (4 * n), out_shape=[jax.ShapeDtypeStruct((1, r, c), F32)] * (4 * n),
        compiler_params=_cparams(("arbitrary",)))(*ws, *ms, *vs, *owns, *r2s)
    return [tuple(res[k * n + q] for k in range(4)) for q in range(n)]


MATS = ("w_in", "w_out", "w_xq", "w_xo", "w_xkv", "w_up", "w_down")
SMALL = ("mix_norm", "conv_w", "conv_norm", "w_af", "b_af", "w_ab", "b_ab", "gla_norm", "xa_norm", "mem_norm",
         "mlp_norm", "final_norm")
WEIGHTS = ("mix_norm", "w_in", "conv_w", "conv_norm", "w_af", "b_af", "w_ab", "b_ab", "gla_norm", "w_out", "xa_norm",
           "mem_norm", "w_xq", "w_xkv", "w_xo", "mlp_norm", "w_up", "w_down", "final_norm")
SMALL_SHARDED = {"conv_w": (3, 64), "w_af": (LR, 32), "w_ab": (LR, 32)}
SMALL_PACK_ROWS = 16


def kernel(x, mem, mix_norm, w_in, conv_w, conv_norm, w_af, b_af, w_ab, b_ab, gla_norm, w_out, xa_norm, mem_norm, w_xq, w_xkv, w_xo, mlp_norm, w_up, w_down, final_norm, loss_target, m_mix_norm, m_w_in, m_conv_w, m_conv_norm, m_w_af, m_b_af, m_w_ab, m_b_ab, m_gla_norm, m_w_out, m_xa_norm, m_mem_norm, m_w_xq, m_w_xkv, m_w_xo, m_mlp_norm, m_w_up, m_w_down, m_final_norm, v_mix_norm, v_w_in, v_conv_w, v_conv_norm, v_w_af, v_b_af, v_w_ab, v_b_ab, v_gla_norm, v_w_out, v_xa_norm, v_mem_norm, v_w_xq, v_w_xkv, v_w_xo, v_mlp_norm, v_w_up, v_w_down, v_final_norm):
    w = dict(mix_norm=mix_norm, w_in=w_in, conv_w=conv_w, conv_norm=conv_norm, w_af=w_af, b_af=b_af, w_ab=w_ab,
             b_ab=b_ab, gla_norm=gla_norm, w_out=w_out, xa_norm=xa_norm, mem_norm=mem_norm, w_xq=w_xq, w_xkv=w_xkv,
             w_xo=w_xo, mlp_norm=mlp_norm, w_up=w_up, w_down=w_down, final_norm=final_norm)
    mom = dict(mix_norm=m_mix_norm, w_in=m_w_in, conv_w=m_conv_w, conv_norm=m_conv_norm, w_af=m_w_af, b_af=m_b_af,
               w_ab=m_w_ab, b_ab=m_b_ab, gla_norm=m_gla_norm, w_out=m_w_out, xa_norm=m_xa_norm, mem_norm=m_mem_norm,
               w_xq=m_w_xq, w_xkv=m_w_xkv, w_xo=m_w_xo, mlp_norm=m_mlp_norm, w_up=m_w_up, w_down=m_w_down,
               final_norm=m_final_norm)
    var = dict(mix_norm=v_mix_norm, w_in=v_w_in, conv_w=v_conv_w, conv_norm=v_conv_norm, w_af=v_w_af, b_af=v_b_af,
               w_ab=v_w_ab, b_ab=v_b_ab, gla_norm=v_gla_norm, w_out=v_w_out, xa_norm=v_xa_norm, mem_norm=v_mem_norm,
               w_xq=v_w_xq, w_xkv=v_w_xkv, w_xo=v_w_xo, mlp_norm=v_mlp_norm, w_up=v_w_up, w_down=v_w_down,
               final_norm=v_final_norm)
    xi, yi, ci = _place()
    two_d = lambda a: a.reshape(a.shape[-2:]) if a.ndim == 3 else a.reshape(1, a.shape[-1])

    small = jnp.concatenate([w[n].reshape(-1) for n in SMALL_SHARDED])
    small = jnp.pad(small, (0, SMALL_PACK_ROWS * 128 - small.shape[0])).reshape(SMALL_PACK_ROWS, 128)
    shard = {n: two_d(w[n]).astype(BF16) for n in MATS}
    for n in ("w_in", "w_up"):
        shard[n] = shard[n].T
    vec = {n: two_d(w[n]) for n in SMALL if n not in SMALL_SHARDED}
    place = jnp.stack([2 * xi + yi, ci]).astype(jnp.int32)
    loss8, grad_x, small_acc, own, from_chips = _step(x[0], mem[0], loss_target[0], shard, small, vec, place)

    tot = _small_reduce(small_acc, loss8)
    small_out = _small_adamw(tot, *[{n: two_d(d[n]) for n in SMALL} for d in (w, mom, var)])
    loss = tot[LOSS_ROW, 0]

    out_g, out_d, out_m, out_v = {}, {}, {}, {}
    wmv = {n: [a.transpose(0, 2, 1) if n == "w_in" else a for a in (w[n], mom[n], var[n])] for n in MATS}
    for shape in dict.fromkeys(wmv[n][0].shape for n in MATS):
        names = [n for n in MATS if wmv[n][0].shape == shape]
        res = _adamw(*[[wmv[n][k] for n in names] for k in range(3)], [own[n] for n in names],
                     [from_chips[n] for n in names], "adamw_" + "_".join(names), grads_transposed=names == ["w_up"])
        for n, r in zip(names, res):
            out_g[n], out_d[n], out_m[n], out_v[n] = [a.transpose(0, 2, 1) for a in r] if n == "w_in" else r
    for n in SMALL:
        out_g[n], out_d[n], out_m[n], out_v[n] = [a.reshape(w[n].shape) for a in small_out[n]]

    return (loss, grad_x[None], *[out_g[n] for n in WEIGHTS], *[out_d[n] for n in WEIGHTS],
            *[out_m[n] for n in WEIGHTS], *[out_v[n] for n in WEIGHTS])
```

```python
import functools
import itertools

import jax
import jax.numpy as jnp
from jax import lax
from jax.experimental import pallas as pl
from jax.experimental.pallas import tpu as pltpu

F32 = jnp.float32
BF16 = jnp.bfloat16

D = 1024
CW = 512
GK = 256
GV = 512
NH = 4
CH = 64
LR = 16
NMEM = 256
XD = 256
FF = 4096
ZW = 3104
ZC = 3200
EPS = 1e-6
NDEV = 8

ZB_CB, ZB_CC, ZB_CU, ZB_V, ZB_G = 0, 1, 2, 4, 5
ZB_Q, ZB_K = 6, 7
ZB_LR = 24

TM = 512
TM_MLP = 256
TM_MLP_FWD = 512
TF = 512
TB = 512
TB_BWD = 512
TT = 2048
VMEM_LIMIT = 56 * 1024 * 1024

ADAM_LR, ADAM_B1, ADAM_B2, ADAM_EPS, ADAM_WD, ADAM_STEP = 0.001, 0.9, 0.999, 1e-08, 0.01, 10

XKV_SHARD = 2 * D // NDEV

MESH = pl.DeviceIdType.MESH


def _cparams(sem):
    return pltpu.CompilerParams(dimension_semantics=sem, vmem_limit_bytes=VMEM_LIMIT)


def _call(body, name, grid, in_specs, out_specs, out_shape, scratch, args, riders=()):
    n_in, n_out, n_scr = len(in_specs), len(out_specs), len(scratch)
    counts = [(len(r.arrays), len(r.out_shape), len(r.scratch)) for r in riders]

    def take(refs, pos, sizes):
        groups = []
        for size in sizes:
            groups.append(refs[pos:pos + size])
            pos += size
        return groups, pos

    def wrapped(*refs):
        ins, pos = refs[:n_in], n_in
        r_ins, pos = take(refs, pos, [c[0] for c in counts])
        outs, pos = refs[pos:pos + n_out], pos + n_out
        r_outs, pos = take(refs, pos, [c[1] for c in counts])
        scr, pos = refs[pos:pos + n_scr], pos + n_scr
        r_scr, pos = take(refs, pos, [c[2] for c in counts])
        ids = [pl.program_id(d) for d in range(len(grid))]
        first = functools.reduce(lambda a, b: a & b, [i == 0 for i in ids])
        last = functools.reduce(lambda a, b: a & b, [i == g - 1 for i, g in zip(ids, grid)])

        @pl.when(first)
        def _():
            for r, a, b, c in zip(riders, r_ins, r_outs, r_scr):
                r.start(a, b, c)

        body(*ins, *outs, *scr)

        for hook, at in (("forward", [g // 2 for g in grid]), ("relay", [max(g - 2, 0) for g in grid])):
            if any(getattr(r, hook) for r in riders):
                @pl.when(functools.reduce(lambda a, b: a & b, [i == s for i, s in zip(ids, at)]))
                def _(hook=hook):
                    for r, a, b, c in zip(riders, r_ins, r_outs, r_scr):
                        if getattr(r, hook):
                            getattr(r, hook)(a, b, c)

        @pl.when(last)
        def _():
            for r, a, b, c in zip(riders, r_ins, r_outs, r_scr):
                r.finish(a, b, c)

    hbm = pl.BlockSpec(memory_space=pltpu.HBM)
    r_args = [a for r in riders for a in r.arrays]
    r_shapes = [s for r in riders for s in r.out_shape]
    return pl.pallas_call(
        wrapped if riders else body, name=name, grid=grid, in_specs=list(in_specs) + [hbm] * len(r_args),
        out_specs=list(out_specs) + [hbm] * len(r_shapes), out_shape=list(out_shape) + r_shapes,
        scratch_shapes=list(scratch) + [s for r in riders for s in r.scratch],
        compiler_params=_cparams(("arbitrary",) * len(grid)))(*args, *r_args)


def _dot(a, b):
    return jnp.dot(a.astype(BF16), b.astype(BF16), preferred_element_type=F32)


def _dot_nt(a, b):
    return lax.dot_general(a.astype(BF16), b.astype(BF16), (((1,), (1,)), ((), ())), preferred_element_type=F32)


def _dot_tn(a, b):
    return lax.dot_general(a.astype(BF16), b.astype(BF16), (((0,), (0,)), ((), ())), preferred_element_type=F32)


def _split(x, n):
    parts = []
    for _ in range(n):
        p = x.astype(BF16)
        parts.append(p)
        x = x - p.astype(F32)
    return parts


def _dot_exact_lhs(m, x, n):
    return functools.reduce(lambda a, b: a + b, [jnp.dot(m, p, preferred_element_type=F32) for p in _split(x, n)])


def _dot_exact_rhs(x, m, n):
    return functools.reduce(lambda a, b: a + b, [jnp.dot(p, m, preferred_element_type=F32) for p in _split(x, n)])


def _rms(x, g):
    r = lax.rsqrt(jnp.mean(x * x, axis=-1, keepdims=True) + EPS)
    return x * r * g, r


def _rms_bwd(x, r, g, dy):
    xr = x * r
    u = dy * g
    dx = r * (u - xr * jnp.mean(u * xr, axis=-1, keepdims=True))
    return dx, jnp.sum(dy * xr, axis=0, keepdims=True)


def _iota(shape, dim):
    return lax.broadcasted_iota(jnp.int32, shape, dim)


def _sigmoid(x):
    return 1.0 / (1.0 + jnp.exp(-x))


def _acc_rows(ref, row):
    ref[...] += jnp.broadcast_to(row, ref.shape)


def _inproj(x, g, w_t, riders=()):
    t = x.shape[0]
    tm = min(TM, t)

    def body(x_ref, g_ref, w_ref, z_ref, h_ref):
        h, _ = _rms(x_ref[...], g_ref[...])
        hb = h.astype(BF16)
        h_ref[...] = hb
        z_ref[...] = _dot_nt(hb, w_ref[...])

    return _call(
        body, "inproj", (t // tm,),
        [pl.BlockSpec((tm, D), lambda i: (i, 0)), pl.BlockSpec((1, D), lambda i: (0, 0)),
         pl.BlockSpec((ZC, D), lambda i: (0, 0))],
        [pl.BlockSpec((tm, ZC), lambda i: (i, 0)), pl.BlockSpec((tm, D), lambda i: (i, 0))],
        [jax.ShapeDtypeStruct((t, ZC), F32), jax.ShapeDtypeStruct((t, D), BF16)], [], (x, g, w_t), riders)


def _kv_proj(mem, g, w):
    def body(m_ref, g_ref, w_ref, kv_ref, mn_ref):
        mn, _ = _rms(m_ref[...], g_ref[...])
        mb = mn.astype(BF16)
        mn_ref[...] = mb
        for j in range(NDEV):
            kv_ref[:, j * XKV_SHARD:(j + 1) * XKV_SHARD] = jnp.dot(mb, w_ref[j], preferred_element_type=F32)

    return pl.pallas_call(
        body, name="kv_proj",
        out_shape=[jax.ShapeDtypeStruct((NMEM, 2 * D), F32), jax.ShapeDtypeStruct((NMEM, D), BF16)],
        compiler_params=pltpu.CompilerParams(vmem_limit_bytes=VMEM_LIMIT))(mem, g, w)


def _softmax_head(qb, kb):
    s = _dot_nt(qb, kb) * (1.0 / 16.0)
    e = jnp.exp(s - jnp.max(s, axis=-1, keepdims=True))
    return e / jnp.sum(e, axis=-1, keepdims=True)


def _attn_fwd(x, z, o_f, o_b, conv_w, conv_norm, gla_norm4, w_out, g, w_xq, kb, vb, w_xo, riders=()):
    t = x.shape[0]
    tm = min(TM, t)
    nblk = t // tm
    jmap = lambda i: i

    def body(x_ref, zq_ref, zk_ref, zv_ref, zg_ref, cb_ref, cc_ref, cu_ref, ccp_ref, ccn_ref, cup_ref, cun_ref,
             of_ref, ob_ref, cw_ref, cn_ref, gn_ref, wo_ref, g_ref, wq_ref, k_ref, v_ref, wx_ref,
             x1_ref, x2_ref, xn_ref, q_ref, a_ref, y_ref, opre_ref):
        j = pl.program_id(0)
        zv = zv_ref[...]
        sb = _head_sum((zq_ref[...] * 0.125) * zk_ref[...], 64, 128)
        o_pre = of_ref[...] + ob_ref[...] - sb * zv
        opre_ref[...] = o_pre
        on, _ = _head_norm(o_pre)
        zg = zg_ref[...]
        y_ref[:, CW:] = (on * gn_ref[...] * (zg * _sigmoid(zg))).astype(BF16)
        cb = cb_ref[...]
        _, _, _, conv = _conv_parts(cb, cc_ref[...], cu_ref[...], ccp_ref[pl.ds(7, 1), :], cup_ref[pl.ds(7, 1), :],
                                    ccn_ref[pl.ds(0, 1), :], cun_ref[pl.ds(0, 1), :], cw_ref, j == 0,
                                    j == nblk - 1, tm)
        yc = cb * conv
        gm = _group_sum(yc * yc) * (1.0 / 64.0)
        y_ref[:, :CW] = (yc * lax.rsqrt(gm + EPS) * cn_ref[...]).astype(BF16)

        x1 = x_ref[...] + jnp.dot(y_ref[...], wo_ref[...], preferred_element_type=F32)
        x1_ref[...] = x1
        xn, _ = _rms(x1, g_ref[...])
        xb = xn.astype(BF16)
        xn_ref[...] = xb
        qb = jnp.dot(xb, wq_ref[...], preferred_element_type=F32).astype(BF16)
        q_ref[...] = qb
        heads = [slice(h * XD, (h + 1) * XD) for h in range(NH)]
        ps = [_softmax_head(qb[:, hs], k_ref[:, hs]) for hs in heads]
        for hs, p in zip(heads, ps):
            a_ref[:, hs] = _dot(p, v_ref[:, hs]).astype(BF16)
        x2_ref[...] = x1 + jnp.dot(a_ref[...], wx_ref[...], preferred_element_type=F32)

    tok = lambda i: (i, 0)
    full = lambda i: (0, 0)
    once = pl.Buffered(1)
    tokd, tokv = pl.BlockSpec((tm, D), tok), pl.BlockSpec((tm, GV), tok)
    weight = pl.BlockSpec((D, D), full, pipeline_mode=once)
    ccp, ccn = _halo_specs(tm, nblk, t, ZB_CC, jmap)
    cup, cun = _halo_specs(tm, nblk, t, ZB_CU, jmap)
    in_specs = [tokd, _zspec(tm, GK, ZB_Q, jmap), _zspec(tm, GK, ZB_K, jmap), _zspec(tm, GV, ZB_V, jmap),
                _zspec(tm, GV, ZB_G, jmap), _zspec(tm, CW, ZB_CB, jmap), _zspec(tm, CW, ZB_CC, jmap),
                _zspec(tm, CW, ZB_CU, jmap), ccp, ccn, cup, cun, tokv, tokv,
                pl.BlockSpec((3, CW), full), pl.BlockSpec((1, CW), full), pl.BlockSpec((1, GV), full),
                weight, pl.BlockSpec((1, D), full), weight, pl.BlockSpec((NMEM, D), full),
                pl.BlockSpec((NMEM, D), full), weight]
    return _call(
        body, "attn_fwd", (nblk,), in_specs, [tokd] * 6 + [tokv],
        [jax.ShapeDtypeStruct((t, D), F32), jax.ShapeDtypeStruct((t, D), F32),
         jax.ShapeDtypeStruct((t, D), BF16), jax.ShapeDtypeStruct((t, D), BF16),
         jax.ShapeDtypeStruct((t, D), BF16), jax.ShapeDtypeStruct((t, D), BF16),
         jax.ShapeDtypeStruct((t, GV), F32)], [],
        (x, z, z, z, z, z, z, z, z, z, z, z, o_f, o_b, conv_w, conv_norm, gla_norm4, w_out, g, w_xq, kb, vb, w_xo),
        riders)


def _mlp_fwd(x2, g, w_up_t, w_down, fg, target):
    t = x2.shape[0]
    tm = min(TM_MLP_FWD, t)

    def body(x_ref, g_ref, wu_ref, wd_ref, fg_ref, t_ref, h1_ref, xn_ref, dx_ref, dxb_ref, loss_ref, dfg_ref, ab):
        @pl.when(pl.program_id(0) == 0)
        def _():
            loss_ref[...] = jnp.zeros_like(loss_ref)
            dfg_ref[...] = jnp.zeros_like(dfg_ref)

        x = x_ref[...]
        xn, _ = _rms(x, g_ref[...])
        xnb = xn.astype(BF16)
        xn_ref[...] = xnb
        for q in range(FF // TF):
            cols = slice(q * TF, (q + 1) * TF)
            h1 = _dot_nt(xnb, wu_ref[cols, :])
            h1_ref[:, cols] = h1.astype(BF16)
            hr = jnp.maximum(h1, 0.0)
            ab[:, cols] = (hr * hr).astype(BF16)
        x3 = x + jnp.dot(ab[...], wd_ref[...], preferred_element_type=F32)
        y, r = _rms(x3, fg_ref[...])
        e = y - t_ref[...]
        row = jnp.mean(e * e, axis=-1, keepdims=True)
        _acc_rows(loss_ref, 0.5 * jnp.sum(row, axis=0, keepdims=True))
        dx, dfg = _rms_bwd(x3, r, fg_ref[...], e * (1.0 / D))
        dx_ref[...] = dx
        dxb_ref[...] = dx.astype(BF16)
        _acc_rows(dfg_ref, dfg)

    tok = lambda i: (i, 0)
    full = lambda i: (0, 0)
    once = pl.Buffered(1)
    return pl.pallas_call(
        body, name="mlp_fwd", grid=(t // tm,),
        in_specs=[pl.BlockSpec((tm, D), tok), pl.BlockSpec((1, D), full),
                  pl.BlockSpec((FF, D), full, pipeline_mode=once), pl.BlockSpec((FF, D), full, pipeline_mode=once),
                  pl.BlockSpec((1, D), full), pl.BlockSpec((tm, D), tok)],
        out_specs=[pl.BlockSpec((tm, FF), tok), pl.BlockSpec((tm, D), tok), pl.BlockSpec((tm, D), tok),
                   pl.BlockSpec((tm, D), tok), pl.BlockSpec((8, 128), full), pl.BlockSpec((8, D), full)],
        out_shape=[jax.ShapeDtypeStruct((t, FF), BF16), jax.ShapeDtypeStruct((t, D), BF16),
                   jax.ShapeDtypeStruct((t, D), F32), jax.ShapeDtypeStruct((t, D), BF16),
                   jax.ShapeDtypeStruct((8, 128), F32), jax.ShapeDtypeStruct((8, D), F32)],
        scratch_shapes=[pltpu.VMEM((tm, FF), BF16)],
        compiler_params=_cparams(("arbitrary",)))(x2, g, w_up_t, w_down, fg, target)


def _mlp_bwd(dx3, dx3b, h1b, w_down, w_up_t, x2, g):
    t = x2.shape[0]
    tm = min(TM_MLP, t)

    def body(dx_ref, dxb_ref, h1_ref, wd_ref, wu_ref, x_ref, g_ref, a_ref, dh_ref, dx2_ref, dx2b_ref, dg_ref):
        @pl.when(pl.program_id(0) == 0)
        def _():
            dg_ref[...] = jnp.zeros_like(dg_ref)

        for q in range(FF // TF):
            cols = slice(q * TF, (q + 1) * TF)
            hr = jnp.maximum(h1_ref[:, cols].astype(F32), 0.0)
            da = _dot_nt(dxb_ref[...], wd_ref[cols, :])
            a_ref[:, cols] = (hr * hr).astype(BF16)
            dh_ref[:, cols] = (da * 2.0 * hr).astype(BF16)
        dxn = jnp.dot(dh_ref[...], wu_ref[...], preferred_element_type=F32)
        x = x_ref[...]
        r = lax.rsqrt(jnp.mean(x * x, axis=-1, keepdims=True) + EPS)
        dx, dg = _rms_bwd(x, r, g_ref[...], dxn)
        dx2 = dx_ref[...] + dx
        dx2_ref[...] = dx2
        dx2b_ref[...] = dx2.astype(BF16)
        _acc_rows(dg_ref, dg)

    tok = lambda i: (i, 0)
    full = lambda i: (0, 0)
    once = pl.Buffered(1)
    return pl.pallas_call(
        body, name="mlp_bwd", grid=(t // tm,),
        in_specs=[pl.BlockSpec((tm, D), tok), pl.BlockSpec((tm, D), tok), pl.BlockSpec((tm, FF), tok),
                  pl.BlockSpec((FF, D), full, pipeline_mode=once), pl.BlockSpec((FF, D), full, pipeline_mode=once),
                  pl.BlockSpec((tm, D), tok), pl.BlockSpec((1, D), full)],
        out_specs=[pl.BlockSpec((tm, FF), tok), pl.BlockSpec((tm, FF), tok), pl.BlockSpec((tm, D), tok),
                   pl.BlockSpec((tm, D), tok), pl.BlockSpec((8, D), full)],
        out_shape=[jax.ShapeDtypeStruct((t, FF), BF16), jax.ShapeDtypeStruct((t, FF), BF16),
                   jax.ShapeDtypeStruct((t, D), F32), jax.ShapeDtypeStruct((t, D), BF16),
                   jax.ShapeDtypeStruct((8, D), F32)],
        compiler_params=_cparams(("arbitrary",)))(dx3, dx3b, h1b, w_down, w_up_t, x2, g)


def _attn_bwd(x1, dx2, dx2b, qb, kb, vb, w_xo, w_xq, w_out, g, riders=()):
    t = x1.shape[0]
    tm = min(TM, t)

    def body(x_ref, dx2_ref, dx2b_ref, q_ref, k_ref, v_ref, wx_ref, wq_ref, wo_ref, g_ref,
             dx1_ref, dx1b_ref, dy_ref, dq_ref, dkv_ref, dg_ref):
        @pl.when(pl.program_id(0) == 0)
        def _():
            dkv_ref[...] = jnp.zeros_like(dkv_ref)
            dg_ref[...] = jnp.zeros_like(dg_ref)

        datt = _dot_nt(dx2b_ref[...], wx_ref[...]).astype(BF16)
        heads = [slice(h * XD, (h + 1) * XD) for h in range(NH)]
        ps = [_softmax_head(q_ref[:, hs], k_ref[:, hs]) for hs in heads]
        dps = [_dot_nt(datt[:, hs], v_ref[:, hs]) for hs in heads]
        dss = [(p * (dp - jnp.sum(dp * p, axis=-1, keepdims=True)) * (1.0 / 16.0)).astype(BF16)
               for p, dp in zip(ps, dps)]
        for h, (hs, p, ds) in enumerate(zip(heads, ps, dss)):
            dq_ref[:, hs] = _dot(ds, k_ref[:, hs]).astype(BF16)
            dkv_ref[:, hs] += _dot_tn(ds, q_ref[:, hs])
            dkv_ref[:, D + h * XD:D + (h + 1) * XD] += _dot_tn(p, datt[:, hs])
        dxn = _dot_nt(dq_ref[...], wq_ref[...])
        x = x_ref[...]
        r = lax.rsqrt(jnp.mean(x * x, axis=-1, keepdims=True) + EPS)
        dx, dg = _rms_bwd(x, r, g_ref[...], dxn)
        dx1 = dx2_ref[...] + dx
        dx1_ref[...] = dx1
        dx1b = dx1.astype(BF16)
        dx1b_ref[...] = dx1b
        dy_ref[...] = _dot_nt(dx1b, wo_ref[...])
        _acc_rows(dg_ref, dg)

    tok = lambda i: (i, 0)
    full = lambda i: (0, 0)
    return _call(
        body, "attn_bwd", (t // tm,),
        [pl.BlockSpec((tm, D), tok), pl.BlockSpec((tm, D), tok), pl.BlockSpec((tm, D), tok),
         pl.BlockSpec((tm, D), tok), pl.BlockSpec((NMEM, D), full), pl.BlockSpec((NMEM, D), full),
         pl.BlockSpec((D, D), full), pl.BlockSpec((D, D), full), pl.BlockSpec((D, D), full),
         pl.BlockSpec((1, D), full)],
        [pl.BlockSpec((tm, D), tok), pl.BlockSpec((tm, D), tok), pl.BlockSpec((tm, D), tok),
         pl.BlockSpec((tm, D), tok), pl.BlockSpec((NMEM, 2 * D), full), pl.BlockSpec((8, D), full)],
        [jax.ShapeDtypeStruct((t, D), F32), jax.ShapeDtypeStruct((t, D), BF16),
         jax.ShapeDtypeStruct((t, D), F32), jax.ShapeDtypeStruct((t, D), BF16),
         jax.ShapeDtypeStruct((NMEM, 2 * D), F32), jax.ShapeDtypeStruct((8, D), F32)], [],
        (x1, dx2, dx2b, qb, kb, vb, w_xo, w_xq, w_out, g), riders)


def _kv_bwd(dkv, memn, mem, g, w):
    def body(dkv_ref, mn_ref, m_ref, g_ref, w_ref, dw_ref, dg_ref):
        dkvb = dkv_ref[...].astype(BF16)
        dmn = jnp.zeros((NMEM, D), F32)
        for j in range(NDEV):
            cols = slice(j * XKV_SHARD, (j + 1) * XKV_SHARD)
            dw_ref[j] = _dot_tn(mn_ref[...], dkvb[:, cols])
            dmn += _dot_nt(dkvb[:, cols], w_ref[j])
        m = m_ref[...]
        r = lax.rsqrt(jnp.mean(m * m, axis=-1, keepdims=True) + EPS)
        dg_ref[...] = jnp.broadcast_to(jnp.sum(dmn * m * r, axis=0, keepdims=True), dg_ref.shape)

    return pl.pallas_call(
        body, name="kv_bwd",
        out_shape=[jax.ShapeDtypeStruct((NDEV, D, XKV_SHARD), F32), jax.ShapeDtypeStruct((8, D), F32)],
        compiler_params=pltpu.CompilerParams(vmem_limit_bytes=VMEM_LIMIT))(dkv, memn, mem, g, w)


def _inproj_bwd(dz, w_t, x, dx1, g, riders=()):
    t = x.shape[0]
    tm = min(TM, t)

    def body(dz_ref, w_ref, x_ref, dx1_ref, g_ref, gx_ref, dg_ref):
        @pl.when(pl.program_id(0) == 0)
        def _():
            dg_ref[...] = jnp.zeros_like(dg_ref)

        dh = jnp.dot(dz_ref[...], w_ref[...], preferred_element_type=F32)
        x = x_ref[...]
        r = lax.rsqrt(jnp.mean(x * x, axis=-1, keepdims=True) + EPS)
        dx, dg = _rms_bwd(x, r, g_ref[...], dh)
        gx_ref[...] = dx1_ref[...] + dx
        _acc_rows(dg_ref, dg)

    tok = lambda i: (i, 0)
    full = lambda i: (0, 0)
    return _call(
        body, "inproj_bwd", (t // tm,),
        [pl.BlockSpec((tm, ZC), tok), pl.BlockSpec((ZC, D), full), pl.BlockSpec((tm, D), tok),
         pl.BlockSpec((tm, D), tok), pl.BlockSpec((1, D), full)],
        [pl.BlockSpec((tm, D), tok), pl.BlockSpec((8, D), full)],
        [jax.ShapeDtypeStruct((t, D), F32), jax.ShapeDtypeStruct((8, D), F32)], [], (dz, w_t, x, dx1, g), riders)


def _matmul_tn(a, b, name, rows=None, riders=()):
    t, k = a.shape
    n = b.shape[1]
    tk, tn = [1024 if size % 1024 == 0 else 640 for size in (k, n)]
    tt = min(TT, t)
    rows = rows or k

    def body(a_ref, b_ref, o_ref):
        @pl.when(pl.program_id(2) == 0)
        def _():
            o_ref[...] = jnp.zeros_like(o_ref)

        o_ref[...] += _dot_tn(a_ref[...], b_ref[...])

    return _call(
        body, name, (k // tk, n // tn, t // tt),
        [pl.BlockSpec((tt, tk), lambda i, j, s: (s, i)), pl.BlockSpec((tt, tn), lambda i, j, s: (s, j))],
        [pl.BlockSpec((tk, tn), lambda i, j, s: (i, j))], [jax.ShapeDtypeStruct((rows, n), F32)], [], (a, b), riders)


def _lane_head(shape, dim, shift):
    return _iota(shape, dim) >> shift


CUM_ROWS = 128


def _chunk_cumsum(x, upper, n):
    r, c = _iota((CUM_ROWS, CUM_ROWS), 0), _iota((CUM_ROWS, CUM_ROWS), 1)
    tri = (c >= r) if upper else (c <= r)
    cum = jnp.where(((r >> 6) == (c >> 6)) & tri, 1.0, 0.0).astype(BF16)
    return jnp.concatenate([_dot_exact_lhs(cum, x[g:g + CUM_ROWS], n) for g in range(0, x.shape[0], CUM_ROWS)],
                           axis=0)


def _gla_recompute(q_raw, k, lr, wpad, bias, rev, tb):
    pre = _dot(lr, wpad) + bias
    la = (jnp.minimum(pre, 0.0) - jnp.log(1.0 + jnp.exp(-jnp.abs(pre)))) * (1.0 / 16.0)
    b = _chunk_cumsum(la, rev, 3)
    e, ei = jnp.exp(b), jnp.exp(-b)
    qt = (q_raw * 0.125) * e
    kt = k * ei
    return pre, b, e, ei, qt, kt


def _stack_heads(x, shift):
    head = _lane_head(x.shape, 1, shift)
    return jnp.concatenate([jnp.where(head == h, x, 0.0) for h in range(NH)], axis=0).astype(BF16)


def _fold_heads(x, shift):
    head = _lane_head((CH, x.shape[1]), 1, shift)
    return functools.reduce(lambda a, b: a + b,
                            [jnp.where(head == h, x[h * CH:(h + 1) * CH], 0.0) for h in range(NH)])


def _wide_mask(rev):
    r, s = _iota((CH, NH * CH), 0), _iota((CH, NH * CH), 1) & (CH - 1)
    return (s >= r) if rev else (s <= r)


def _rows_by_head(x):
    w = x.shape[1] // NH
    return jnp.concatenate([x[:, h * w:(h + 1) * w] for h in range(NH)], axis=0)


def _lanes_by_head(x):
    return jnp.concatenate([x[h * CH:(h + 1) * CH] for h in range(NH)], axis=1)


def _state_compact(xt):
    head = _lane_head((128, GK), 1, 6)
    return functools.reduce(lambda a, b: a + b,
                            [jnp.where(head == h, xt[h * 128:(h + 1) * 128], 0.0) for h in range(NH)])


def _conv_parts(cb, cc, cu, ccp, cup, ccn, cun, cw_ref, first, last, tb):
    h = cc * cu
    hp = jnp.where(first, 0.0, ccp * cup)
    hn = jnp.where(last, 0.0, ccn * cun)
    rows = _iota(h.shape, 0)
    h_m1 = jnp.where(rows == 0, hp, pltpu.roll(h, 1, 0))
    h_p1 = jnp.where(rows == tb - 1, hn, pltpu.roll(h, tb - 1, 0))
    conv = cw_ref[pl.ds(0, 1), :] * h_m1 + cw_ref[pl.ds(1, 1), :] * h + cw_ref[pl.ds(2, 1), :] * h_p1
    return h, h_m1, h_p1, conv


def _head_sum(x, w_in, w_out):
    shape, sh_in, sh_out = (2 * w_in, 2 * w_out), w_in.bit_length() - 1, w_out.bit_length() - 1
    sel = jnp.where((_iota(shape, 0) >> sh_in) == (_iota(shape, 1) >> sh_out), 1.0, 0.0).astype(BF16)
    return jnp.concatenate([_dot_exact_rhs(x[:, s:s + 2 * w_in], sel, 2) for s in range(0, NH * w_in, 2 * w_in)],
                           axis=1)


def _group_sum(x):
    ones = jnp.where((_iota((128, 128), 0) >> 6) == (_iota((128, 128), 1) >> 6), 1.0, 0.0).astype(BF16)
    return jnp.concatenate([_dot_exact_rhs(x[:, s:s + 128], ones, 2) for s in range(0, x.shape[1], 128)], axis=1)


def _head_norm(o):
    ons, rs = [], []
    for h in range(NH):
        slab = o[:, h * 128:(h + 1) * 128]
        r = lax.rsqrt(jnp.mean(slab * slab, axis=-1, keepdims=True) + EPS)
        ons.append(slab * r)
        rs.append(jnp.broadcast_to(r, slab.shape))
    return jnp.concatenate(ons, axis=1), jnp.concatenate(rs, axis=1)


def _zspec(tb, width, blk, jmap):
    return pl.BlockSpec((tb, width), lambda i: (jmap(i), blk))


def _halo_specs(tb, nblk, t, blk, jmap):
    prev = pl.BlockSpec((8, CW), lambda i: (jnp.maximum(jmap(i) * (tb // 8) - 1, 0), blk))
    nxt = pl.BlockSpec((8, CW), lambda i: (jnp.minimum((jmap(i) + 1) * (tb // 8), t // 8 - 1), blk))
    return prev, nxt


def _gla_fwd_block(q_ref, k_ref, v_ref, lr_ref, w_ref, bias_ref, o_ref, sd_ref, st, b_scr, rev, tb):
    nb = tb // CH
    _, b, _, _, qt, kt = _gla_recompute(q_ref[...], k_ref[...], lr_ref[...], w_ref[...], bias_ref[...], rev, tb)
    v = v_ref[...]
    b_scr[...] = b
    yield
    maskw = _wide_mask(rev)
    order = list(reversed(range(nb))) if rev else list(range(nb))
    rows = [slice(c * CH, (c + 1) * CH) for c in range(nb)]
    state = st[...]
    for c in order:
        gdec = jnp.exp(b_scr[pl.ds(c * CH + (0 if rev else CH - 1), 1), :])
        sd_ref[c] = state
        a = jnp.where(maskw, _dot_nt(qt[rows[c]], _stack_heads(kt[rows[c]], 6)), 0.0)
        o_inter = _lanes_by_head(_dot_nt(_stack_heads(qt[rows[c]], 6), state))
        o_ref[pl.ds(c * CH, CH), :] = _dot(a, _stack_heads(v[rows[c]], 7)) + o_inter
        state = state * gdec + _state_compact(_dot_tn(v[rows[c]], kt[rows[c]] * gdec))
        yield
    st[...] = state
    yield


def _gla_fwd(z, waf_pad, b_af, wab_pad, b_ab, riders=()):
    t = z.shape[0]
    tb = min(TB, t)
    nblk, nb = t // tb, tb // CH
    jmaps = (lambda i: i, lambda i: nblk - 1 - i)

    def body(qf, kf, vf, lrf, qr, kr, vr, lrr, wf, bf, wr, br, of_ref, sdf_ref, or_ref, sdr_ref,
             st_f, st_r, b_f, b_r):
        @pl.when(pl.program_id(0) == 0)
        def _():
            st_f[...] = jnp.zeros_like(st_f)
            st_r[...] = jnp.zeros_like(st_r)

        for _ in zip(_gla_fwd_block(qf, kf, vf, lrf, wf, bf, of_ref, sdf_ref, st_f, b_f, False, tb),
                     _gla_fwd_block(qr, kr, vr, lrr, wr, br, or_ref, sdr_ref, st_r, b_r, True, tb)):
            pass

    full = lambda i: (0, 0)
    zspecs = [s for jm in jmaps for s in (_zspec(tb, GK, ZB_Q, jm), _zspec(tb, GK, ZB_K, jm),
                                         _zspec(tb, GV, ZB_V, jm), _zspec(tb, 128, ZB_LR, jm))]
    wspecs = [pl.BlockSpec((128, GK), full), pl.BlockSpec((1, GK), full)] * 2
    out_specs = [s for jm in jmaps for s in (pl.BlockSpec((tb, GV), lambda i, jm=jm: (jm(i), 0)),
                                             pl.BlockSpec((nb, 128, GK), lambda i, jm=jm: (jm(i), 0, 0)))]
    out_shape = [jax.ShapeDtypeStruct((t, GV), F32), jax.ShapeDtypeStruct((t // CH, 128, GK), F32)] * 2
    scratch = [pltpu.VMEM((128, GK), F32), pltpu.VMEM((128, GK), F32), pltpu.VMEM((tb, GK), F32),
               pltpu.VMEM((tb, GK), F32)]
    return _call(body, "gla_fwd", (nblk,), zspecs + wspecs, out_specs, out_shape, scratch,
                 [z] * 8 + [waf_pad, b_af, wab_pad, b_ab], riders)


def _gla_bwd_chunks(do_ref, sd_ref, dst, b_scr, db_scr, dq_ref, dk_ref, dv_ref, qt, kt, e, ei, v, rev, nb):
    maskw = _wide_mask(rev)
    for c in (range(nb) if rev else reversed(range(nb))):
        sl = slice(c * CH, (c + 1) * CH)
        grow = c * CH + (0 if rev else CH - 1)
        gdec = jnp.exp(b_scr[pl.ds(grow, 1), :])
        qt_c, kt_c, v_c, do_c = qt[sl], kt[sl], v[sl], do_ref[pl.ds(c * CH, CH), :]
        s_in, ds_out = sd_ref[c], dst[...]
        kbd, vbd = _stack_heads(kt_c, 6), _stack_heads(v_c, 7)
        a = jnp.where(maskw, _dot_nt(qt_c, kbd), 0.0)
        da = jnp.where(maskw, _dot_nt(do_c, vbd), 0.0)
        dv_ref[pl.ds(c * CH, CH), :] = (_fold_heads(_dot_tn(a, do_c), 7)
                                        + _lanes_by_head(_dot_nt(_stack_heads(kt_c * gdec, 6), ds_out)))
        dqt = _dot(da, kbd) + _fold_heads(_dot(_rows_by_head(do_c), s_in), 6)
        dkh = _fold_heads(_dot(_rows_by_head(v_c), ds_out), 6)
        da_do = jnp.concatenate([da.astype(BF16), do_c.astype(BF16)], axis=1)
        both = _dot_tn(da_do, qt_c)
        dkt = _fold_heads(both[:NH * CH], 6) + dkh * gdec
        dg = jnp.sum(ds_out * s_in, axis=0, keepdims=True) + jnp.sum(kt_c * dkh, axis=0, keepdims=True)
        db_scr[pl.ds(c * CH, CH), :] = dqt * qt_c - dkt * kt_c
        db_scr[pl.ds(grow, 1), :] += dg * gdec
        dq_ref[pl.ds(c * CH, CH), :] = dqt * e[sl] * 0.125
        dk_ref[pl.ds(c * CH, CH), :] = dkt * ei[sl]
        dst[...] = ds_out * gdec + _state_compact(both[NH * CH:])
        yield


def _gate_bwd(db, pre, lr, wpad, rev, tb):
    dla = _chunk_cumsum(db, not rev, 2)
    dpre = dla * (1.0 / 16.0) / (1.0 + jnp.exp(pre))
    return dpre, _dot_nt(dpre, wpad), _dot_tn(lr, dpre)


def _gla_bwd_first(z, dy, o_pre, sd, wpad, bias, conv_w, conv_norm, gla_norm4, riders=()):
    t = z.shape[0]
    tb = min(TB_BWD, t)
    nblk, nb = t // tb, tb // CH
    jmap = lambda i: nblk - 1 - i

    def body(q_ref, k_ref, v_ref, lr_ref, g_ref, cb_ref, cc_ref, cu_ref, ccp_ref, ccn_ref, cup_ref, cun_ref,
             dy_ref, opre_ref, sd_ref, w_ref, bias_ref, cw_ref, cn_ref, gn_ref,
             do_ref, dq_ref, dk_ref, dv_ref, dlr_ref, dzg_ref, dzcb_ref, dconv_ref,
             dw_ref, dbias_ref, dcw_ref, dcn_ref, dgn_ref, dst, b_scr, db_scr):
        i = pl.program_id(0)
        j = jmap(i)

        @pl.when(i == 0)
        def _():
            dst[...] = jnp.zeros_like(dst)
            for ref in (dw_ref, dbias_ref, dcw_ref, dcn_ref, dgn_ref):
                ref[...] = jnp.zeros_like(ref)

        dyg = dy_ref[:, CW:]
        g = g_ref[...]
        sig = _sigmoid(g)
        on, rr = _head_norm(opre_ref[...])
        gn = gn_ref[...]
        dzg_ref[...] = (dyg * on * gn * (sig * (1.0 + g * (1.0 - sig)))).astype(BF16)
        don = dyg * (g * sig)
        _acc_rows(dgn_ref, jnp.sum(don * on, axis=0, keepdims=True))
        u = don * gn
        uo = u * on
        mean_uo = jnp.concatenate(
            [jnp.broadcast_to(jnp.mean(uo[:, h * 128:(h + 1) * 128], axis=-1, keepdims=True), (tb, 128))
             for h in range(NH)], axis=1)
        do_ref[...] = rr * (u - on * mean_uo)

        def conv_branch():
            cb = cb_ref[...]
            h, h_m1, h_p1, conv = _conv_parts(cb, cc_ref[...], cu_ref[...], ccp_ref[pl.ds(7, 1), :],
                                              cup_ref[pl.ds(7, 1), :], ccn_ref[pl.ds(0, 1), :],
                                              cun_ref[pl.ds(0, 1), :], cw_ref, j == 0, j == nblk - 1, tb)
            yc = cb * conv
            yield
            rc = lax.rsqrt(_group_sum(yc * yc) * (1.0 / 64.0) + EPS)
            ycr = yc * rc
            yield
            dyn = dy_ref[:, :CW]
            _acc_rows(dcn_ref, jnp.sum(dyn * ycr, axis=0, keepdims=True))
            uc = dyn * cn_ref[...]
            yield
            dyc = rc * (uc - ycr * (_group_sum(uc * ycr) * (1.0 / 64.0)))
            dzcb_ref[...] = (dyc * conv).astype(BF16)
            yield
            dconv = dyc * cb
            dconv_ref[...] = dconv
            yield
            dcw_ref[pl.ds(0, 1), :] += jnp.sum(dconv * h_m1, axis=0, keepdims=True)
            dcw_ref[pl.ds(1, 1), :] += jnp.sum(dconv * h, axis=0, keepdims=True)
            dcw_ref[pl.ds(2, 1), :] += jnp.sum(dconv * h_p1, axis=0, keepdims=True)
            yield

        lr, wp = lr_ref[...], w_ref[...]
        pre, b, e, ei, qt, kt = _gla_recompute(q_ref[...], k_ref[...], lr, wp, bias_ref[...], False, tb)
        b_scr[...] = b
        for _ in itertools.zip_longest(
                _gla_bwd_chunks(do_ref, sd_ref, dst, b_scr, db_scr, dq_ref, dk_ref, dv_ref, qt, kt, e, ei, v_ref[...],
                                False, nb), conv_branch()):
            pass
        dpre, dlr, dw = _gate_bwd(db_scr[...], pre, lr, wp, False, tb)
        dlr_ref[...] = dlr
        dw_ref[...] += dw
        _acc_rows(dbias_ref, jnp.sum(dpre, axis=0, keepdims=True))

    full = lambda i: (0, 0)
    tokv = pl.BlockSpec((tb, GV), lambda i: (jmap(i), 0))
    tokk = pl.BlockSpec((tb, GK), lambda i: (jmap(i), 0))
    ccp, ccn = _halo_specs(tb, nblk, t, ZB_CC, jmap)
    cup, cun = _halo_specs(tb, nblk, t, ZB_CU, jmap)
    in_specs = [_zspec(tb, GK, ZB_Q, jmap), _zspec(tb, GK, ZB_K, jmap), _zspec(tb, GV, ZB_V, jmap),
                _zspec(tb, 128, ZB_LR, jmap), _zspec(tb, GV, ZB_G, jmap), _zspec(tb, CW, ZB_CB, jmap),
                _zspec(tb, CW, ZB_CC, jmap), _zspec(tb, CW, ZB_CU, jmap), ccp, ccn, cup, cun,
                pl.BlockSpec((tb, D), lambda i: (jmap(i), 0)), tokv,
                pl.BlockSpec((nb, 128, GK), lambda i: (jmap(i), 0, 0)), pl.BlockSpec((128, GK), full),
                pl.BlockSpec((1, GK), full), pl.BlockSpec((3, CW), full), pl.BlockSpec((1, CW), full),
                pl.BlockSpec((1, GV), full)]
    out_specs = [tokv, tokk, tokk, tokv, pl.BlockSpec((tb, 128), lambda i: (jmap(i), 0)), tokv, tokv, tokv,
                 pl.BlockSpec((128, GK), full), pl.BlockSpec((8, GK), full), pl.BlockSpec((8, CW), full),
                 pl.BlockSpec((8, CW), full), pl.BlockSpec((8, GV), full)]
    out_shape = [jax.ShapeDtypeStruct((t, GV), F32), jax.ShapeDtypeStruct((t, GK), F32),
                 jax.ShapeDtypeStruct((t, GK), F32), jax.ShapeDtypeStruct((t, GV), F32),
                 jax.ShapeDtypeStruct((t, 128), F32), jax.ShapeDtypeStruct((t, GV), BF16),
                 jax.ShapeDtypeStruct((t, CW), BF16), jax.ShapeDtypeStruct((t, CW), F32),
                 jax.ShapeDtypeStruct((128, GK), F32), jax.ShapeDtypeStruct((8, GK), F32),
                 jax.ShapeDtypeStruct((8, CW), F32), jax.ShapeDtypeStruct((8, CW), F32),
                 jax.ShapeDtypeStruct((8, GV), F32)]
    return _call(
        body, "gla_bwd_first", (nblk,), in_specs, out_specs, out_shape,
        [pltpu.VMEM((128, GK), F32), pltpu.VMEM((tb, GK), F32), pltpu.VMEM((tb, GK), F32)],
        (z, z, z, z, z, z, z, z, z, z, z, z, dy, o_pre, sd, wpad, bias, conv_w, conv_norm, gla_norm4), riders)


def _gla_bwd_second(z, do, sd, wpad, bias, dqa, dka, dva, dlra, dzg, dzcb, dconv, conv_w, riders=()):
    t = z.shape[0]
    tb = min(TB_BWD, t)
    nblk, nb = t // tb, tb // CH
    jmap = lambda i: i

    def body(q_ref, k_ref, v_ref, lr_ref, cc_ref, cu_ref, do_ref, sd_ref, w_ref, bias_ref, dqa_ref, dka_ref,
             dva_ref, dlra_ref, dzg_ref, dzcb_ref, dc_ref, dcp_ref, dcn_ref, cw_ref,
             dz_ref, dw_ref, dbias_ref, dst, b_scr, db_scr, dq_scr, dk_scr, dv_scr, sb_scr, dsk_scr):
        i = pl.program_id(0)

        @pl.when(i == 0)
        def _():
            dst[...] = jnp.zeros_like(dst)
            dw_ref[...] = jnp.zeros_like(dw_ref)
            dbias_ref[...] = jnp.zeros_like(dbias_ref)

        q_raw, k, v, lr, wp = q_ref[...], k_ref[...], v_ref[...], lr_ref[...], w_ref[...]
        pre, b, e, ei, qt, kt = _gla_recompute(q_raw, k, lr, wp, bias_ref[...], True, tb)
        b_scr[...] = b

        def token_local():
            dc = dc_ref[...]
            rows = _iota(dc.shape, 0)
            dprev = jnp.where(i == 0, 0.0, dcp_ref[pl.ds(7, 1), :])
            dnext = jnp.where(i == nblk - 1, 0.0, dcn_ref[pl.ds(0, 1), :])
            dc_m1 = jnp.where(rows == 0, dprev, pltpu.roll(dc, 1, 0))
            dc_p1 = jnp.where(rows == tb - 1, dnext, pltpu.roll(dc, tb - 1, 0))
            yield
            dh = cw_ref[pl.ds(0, 1), :] * dc_p1 + cw_ref[pl.ds(1, 1), :] * dc + cw_ref[pl.ds(2, 1), :] * dc_m1
            dz_ref[:, 0:512] = dzcb_ref[...]
            yield
            dz_ref[:, 512:1024] = (dh * cu_ref[...]).astype(BF16)
            dz_ref[:, 1024:1536] = (dh * cc_ref[...]).astype(BF16)
            dz_ref[:, 2560:3072] = dzg_ref[...]
            yield
            sb_scr[...] = _head_sum((q_raw * 0.125) * k, 64, 128)
            yield
            dsk_scr[...] = _head_sum(do_ref[...] * v, 128, 64)
            yield

        for _ in itertools.zip_longest(
                _gla_bwd_chunks(do_ref, sd_ref, dst, b_scr, db_scr, dq_scr, dk_scr, dv_scr, qt, kt, e, ei, v, True, nb),
                token_local()):
            pass
        dpre, dlr, dw = _gate_bwd(db_scr[...], pre, lr, wp, True, tb)
        dw_ref[...] += dw
        _acc_rows(dbias_ref, jnp.sum(dpre, axis=0, keepdims=True))
        dsk = dsk_scr[...]
        dz_ref[:, 1536:1792] = (dqa_ref[...] + dq_scr[...] - dsk * k * 0.125).astype(BF16)
        dz_ref[:, 1792:2048] = (dka_ref[...] + dk_scr[...] - dsk * (q_raw * 0.125)).astype(BF16)
        dz_ref[:, 2048:2560] = (dva_ref[...] + dv_scr[...] - sb_scr[...] * do_ref[...]).astype(BF16)
        dz_ref[:, 3072:3200] = (dlra_ref[...] + dlr).astype(BF16)

    full = lambda i: (0, 0)
    tokv = pl.BlockSpec((tb, GV), lambda i: (i, 0))
    tokk = pl.BlockSpec((tb, GK), lambda i: (i, 0))
    dcp = pl.BlockSpec((8, CW), lambda i: (jnp.maximum(i * (tb // 8) - 1, 0), 0))
    dcn = pl.BlockSpec((8, CW), lambda i: (jnp.minimum((i + 1) * (tb // 8), t // 8 - 1), 0))
    in_specs = [_zspec(tb, GK, ZB_Q, jmap), _zspec(tb, GK, ZB_K, jmap), _zspec(tb, GV, ZB_V, jmap),
                _zspec(tb, 128, ZB_LR, jmap), _zspec(tb, CW, ZB_CC, jmap), _zspec(tb, CW, ZB_CU, jmap), tokv,
                pl.BlockSpec((nb, 128, GK), lambda i: (i, 0, 0)), pl.BlockSpec((128, GK), full),
                pl.BlockSpec((1, GK), full), tokk, tokk, tokv, pl.BlockSpec((tb, 128), lambda i: (i, 0)), tokv, tokv,
                tokv, dcp, dcn, pl.BlockSpec((3, CW), full)]
    out_specs = [pl.BlockSpec((tb, ZC), lambda i: (i, 0)), pl.BlockSpec((128, GK), full), pl.BlockSpec((8, GK), full)]
    out_shape = [jax.ShapeDtypeStruct((t, ZC), BF16), jax.ShapeDtypeStruct((128, GK), F32),
                 jax.ShapeDtypeStruct((8, GK), F32)]
    return _call(
        body, "gla_bwd_second", (nblk,), in_specs, out_specs, out_shape,
        [pltpu.VMEM((128, GK), F32), pltpu.VMEM((tb, GK), F32), pltpu.VMEM((tb, GK), F32),
         pltpu.VMEM((tb, GK), F32), pltpu.VMEM((tb, GK), F32), pltpu.VMEM((tb, GV), F32),
         pltpu.VMEM((tb, GV), F32), pltpu.VMEM((tb, GK), F32)],
        (z, z, z, z, z, z, do, sd, wpad, bias, dqa, dka, dva, dlra, dzg, dzcb, dconv, dconv, dconv, conv_w), riders)


def _step(x, mem, target, shard, small_pack, vec, place):
    own, from_chips = {}, {}

    def pair_sums(names, g4, from_sibling):
        pbs = {}
        for shape in dict.fromkeys(g.shape for g in g4):
            idx = [i for i, g in enumerate(g4) if g.shape == shape]
            pb, mine = _rs_pair_sum(place, [g4[i] for i in idx], [from_sibling[i] for i in idx],
                                    "pair_sum_" + "_".join(names[i] for i in idx))
            for i, b, o in zip(idx, pb, mine):
                pbs[i], own[names[i]] = b, o
        return [pbs[i] for i in range(len(g4))]

    def by_dest(g, n):
        return g.reshape((4, 2) + shard[n].shape)

    w_in, small_all = _exchange(_gather_rider([shard["w_in"], small_pack]), "gather_w_in")
    w_in = jnp.pad(w_in.reshape(ZW, D), ((0, ZC - ZW), (0, 0)))
    small_all = small_all.reshape(NDEV, -1)
    p, off = {}, 0
    for n, (r, c) in SMALL_SHARDED.items():
        p[n] = small_all[:, off:off + r * c].reshape(NDEV, r, c).transpose(1, 0, 2).reshape(r, NDEV * c)
        off += r * c
    zeros_lr = jnp.zeros((128 - LR, GK), BF16)
    waf_pad = jnp.concatenate([p["w_af"].astype(BF16), zeros_lr], axis=0)
    wab_pad = jnp.concatenate([jnp.zeros((LR, GK), BF16), p["w_ab"].astype(BF16), zeros_lr[:128 - 2 * LR]], axis=0)
    gla_norm4 = jnp.tile(vec["gla_norm"], (1, NH))

    z, hb, w_out, w_xq, w_xo, w_xkv = _inproj(
        x, vec["mix_norm"], w_in, [_gather_rider([shard[n] for n in ("w_out", "w_xq", "w_xo", "w_xkv")])])
    w_out, w_xq, w_xo = [a.reshape(D, D) for a in (w_out, w_xq, w_xo)]
    o_f, sd_f, o_b, sd_b, w_up_t = _gla_fwd(
        z, waf_pad, vec["b_af"], wab_pad, vec["b_ab"], [_gather_rider([shard["w_up"]])])
    kv, memn = _kv_proj(mem, vec["mem_norm"], w_xkv)
    kb, vb = kv[:, :D].astype(BF16), kv[:, D:].astype(BF16)
    x1, x2, xn1, qb, attb, yb, o_pre, w_down = _attn_fwd(
        x, z, o_f, o_b, p["conv_w"], vec["conv_norm"], gla_norm4, w_out, vec["xa_norm"], w_xq, kb, vb, w_xo,
        riders=[_gather_rider([shard["w_down"]])])
    w_up_t, w_down = w_up_t.reshape(FF, D), w_down.reshape(FF, D)
    h1b, xn2, dx3, dx3b, loss8, dfinal = _mlp_fwd(x2, vec["mlp_norm"], w_up_t, w_down, vec["final_norm"], target)

    ab, dh1b, dx2, dx2b, dmlp = _mlp_bwd(dx3, dx3b, h1b, w_down, w_up_t, x2, vec["mlp_norm"])
    g_mlp = [by_dest(_matmul_tn(ab, dx3b, "dw_down")[0], "w_down"),
             by_dest(_matmul_tn(dh1b, xn2, "dw_up")[0], "w_up")]
    dx1, dx1b, dy, dqb, dkv, dxa, *s_mlp = _attn_bwd(x1, dx2, dx2b, qb, kb, vb, w_xo, w_xq, w_out, vec["xa_norm"],
                                                     riders=[_sibling_rider(g_mlp)])
    pb_mlp = pair_sums(("w_down", "w_up"), g_mlp, s_mlp)
    dw_xo = _matmul_tn(attb, dx2b, "dw_xo")[0]
    dw_xkv, dmemn = _kv_bwd(dkv, memn, mem, vec["mem_norm"], w_xkv)
    att_names = ("w_xo", "w_xq", "w_out", "w_xkv")
    g_att = [by_dest(g, n) for g, n in zip(
        (dw_xo, _matmul_tn(xn1, dqb, "dw_xq")[0], _matmul_tn(yb, dx1b, "dw_out")[0], dw_xkv), att_names)]
    res = _gla_bwd_first(z, dy, o_pre, sd_f, waf_pad, vec["b_af"], p["conv_w"], vec["conv_norm"], gla_norm4,
                         riders=[_chips_rider(pb_mlp), _sibling_rider(g_att)])
    do, dqa, dka, dva, dlra, dzg, dzcb, dconv, dwaf, dbaf, dcw, dcn, dgn = res[:13]
    from_chips["w_down"], from_chips["w_up"] = res[13:15]
    pb_att = pair_sums(att_names, g_att, res[15:])
    dz, dwab, dbab, *c_att = _gla_bwd_second(z, do, sd_b, wab_pad, vec["b_ab"], dqa, dka, dva, dlra, dzg, dzcb, dconv,
                                             p["conv_w"], riders=[_chips_rider(pb_att)])
    from_chips.update(zip(att_names, c_att))
    g_in = [by_dest(_matmul_tn(dz, hb, "dw_in", rows=ZW)[0], "w_in")]
    pb_in = pair_sums(("w_in",), g_in, _exchange(_sibling_rider(g_in), "grads_to_sibling_w_in"))
    grad_x, dmix, from_chips["w_in"] = _inproj_bwd(dz, w_in, x, dx1, vec["mix_norm"], riders=[_chips_rider(pb_in)])

    small_acc = dict(mix_norm=dmix, conv_w=dcw, conv_norm=dcn, w_af=dwaf, b_af=dbaf, w_ab=dwab, b_ab=dbab,
                     gla_norm=dgn, xa_norm=dxa, mem_norm=dmemn, mlp_norm=dmlp, final_norm=dfinal)
    return loss8, grad_x, small_acc, own, from_chips


def _place():
    return lax.axis_index("x"), lax.axis_index("y"), lax.axis_index("c")


class _Rider:
    def __init__(self, arrays, out_shape, scratch, start, finish, forward=None, relay=None):
        self.arrays, self.out_shape, self.scratch, self.start, self.finish = arrays, out_shape, scratch, start, finish
        self.forward, self.relay = forward, relay


def _gather_rider(blks, early_relay=True):
    n = len(blks)

    def plan(in_refs, out_refs, sems):
        send_sems, recv_sems, local_sems = sems
        x, y, c = _place()
        me, sibling = (x, y, c), (x, y, 1 - c)
        xn, yn, dg = (1 - x, y, c), (x, 1 - y, c), (1 - x, 1 - y, c)
        via = (x + (1 - c) * (1 - 2 * x), y + c * (1 - 2 * y), c)
        onto = (x + c * (1 - 2 * x), y + (1 - c) * (1 - 2 * y), c)
        other = onto

        def copy(a, k, block, to, own=False):
            px, py, pc = block
            dst = out_refs[a].at[4 * px + 2 * py + pc]
            return pltpu.make_async_remote_copy(
                src_ref=in_refs[a] if own else dst, dst_ref=dst, send_sem=send_sems.at[k, a],
                recv_sem=recv_sems.at[k, a], device_id=to, device_id_type=MESH)

        def local(a):
            return pltpu.make_async_copy(in_refs[a], out_refs[a].at[4 * x + 2 * y + c], local_sems.at[a])

        def sends(a):
            return ([copy(a, 0, me, sibling, own=True), copy(a, 1, me, xn, own=True), copy(a, 2, me, yn, own=True),
                     copy(a, 3, via, onto), copy(a, 4 + c, via, sibling), copy(a, 5 - c, other, sibling),
                     copy(a, 6, dg, sibling)])

        return copy, local, sends, me, sibling, (xn, yn, dg), via, other

    def start(in_refs, out_refs, sems):
        _, local, sends, _, _, _, _, _ = plan(in_refs, out_refs, sems)
        for a in range(n):
            local(a).start()
            for cp in sends(a)[:3]:
                cp.start()

    def forward(in_refs, out_refs, sems):
        copy, _, sends, me, _, _, via, _ = plan(in_refs, out_refs, sems)
        for a in range(n):
            copy(a, 1 + me[2], via, me).wait_recv()
            sends(a)[3].start()
            sends(a)[4].start()

    def relay(in_refs, out_refs, sems):
        copy, _, sends, me, _, (_, _, dg), _, other = plan(in_refs, out_refs, sems)
        for a in range(n):
            copy(a, 2 - me[2], other, me).wait_recv()
            sends(a)[5].start()
        for a in range(n):
            copy(a, 3, dg, me).wait_recv()
            sends(a)[6].start()

    def finish(in_refs, out_refs, sems):
        if not early_relay:
            forward(in_refs, out_refs, sems)
            relay(in_refs, out_refs, sems)
        copy, local, sends, me, sibling, chips, _, _ = plan(in_refs, out_refs, sems)
        for a in range(n):
            copy(a, 0, sibling, me).wait_recv()
            for j, (px, py, pc) in enumerate(chips):
                copy(a, 4 + j, (px, py, 1 - pc), me).wait_recv()
            for cp in sends(a):
                cp.wait_send()
            local(a).wait()

    return _Rider(blks, [jax.ShapeDtypeStruct((NDEV,) + b.shape, b.dtype) for b in blks],
                  [pltpu.SemaphoreType.DMA((7, n)), pltpu.SemaphoreType.DMA((7, n)), pltpu.SemaphoreType.DMA((n,))],
                  start, finish, forward if early_relay else None, relay if early_relay else None)


def _sibling_rider(g4s):
    n = len(g4s)

    def copies(in_refs, out_refs, sems):
        send_sems, recv_sems = sems
        x, y, c = _place()
        return [pltpu.make_async_remote_copy(
            src_ref=in_refs[a].at[k, 1 - c], dst_ref=out_refs[a].at[k], send_sem=send_sems.at[k, a],
            recv_sem=recv_sems.at[k, a], device_id=(x, y, 1 - c), device_id_type=MESH)
            for a in range(n) for k in range(4)]

    def start(in_refs, out_refs, sems):
        for cp in copies(in_refs, out_refs, sems):
            cp.start()

    def finish(in_refs, out_refs, sems):
        for cp in copies(in_refs, out_refs, sems):
            cp.wait()

    return _Rider(g4s, [jax.ShapeDtypeStruct((4,) + g.shape[2:], g.dtype) for g in g4s],
                  [pltpu.SemaphoreType.DMA((4, n)), pltpu.SemaphoreType.DMA((4, n))], start, finish)


def _chips_rider(pbs):
    n = len(pbs)

    def copies(in_refs, out_refs, sems):
        send_sems, recv_sems = sems
        x, y, c = _place()
        peers = [(1 - x, y), (x, 1 - y), (1 - x, 1 - y)]
        return [pltpu.make_async_remote_copy(
            src_ref=in_refs[a].at[2 * px + py], dst_ref=out_refs[a].at[k], send_sem=send_sems.at[k, a],
            recv_sem=recv_sems.at[k, a], device_id=(px, py, c), device_id_type=MESH)
            for a in range(n) for k, (px, py) in enumerate(peers)]

    def start(in_refs, out_refs, sems):
        for cp in copies(in_refs, out_refs, sems):
            cp.start()

    def finish(in_refs, out_refs, sems):
        for cp in copies(in_refs, out_refs, sems):
            cp.wait()

    return _Rider(pbs, [jax.ShapeDtypeStruct((3,) + p.shape[1:], p.dtype) for p in pbs],
                  [pltpu.SemaphoreType.DMA((3, n)), pltpu.SemaphoreType.DMA((3, n))], start, finish)


def _exchange(rider, name):
    n_in, n_out = len(rider.arrays), len(rider.out_shape)

    def body(*refs):
        ins, outs, sems = refs[:n_in], refs[n_in:n_in + n_out], refs[n_in + n_out:]
        rider.start(ins, outs, sems)
        for hook in (rider.forward, rider.relay):
            if hook:
                hook(ins, outs, sems)
        rider.finish(ins, outs, sems)

    hbm = pl.BlockSpec(memory_space=pltpu.HBM)
    return pl.pallas_call(body, name=name, out_shape=rider.out_shape, in_specs=[hbm] * n_in,
                          out_specs=[hbm] * n_out, scratch_shapes=rider.scratch)(*rider.arrays)


def _rs_pair_sum(place, g4s, r1s, name):
    n = len(g4s)
    rows, cols = g4s[0].shape[2:]
    tr = min(rows, 512)

    def body(pl_ref, *refs):
        for g_ref, r_ref, pb_ref, own_ref in zip(refs[:n], refs[n:2 * n], refs[2 * n:3 * n], refs[3 * n:]):
            s = g_ref[0, 0] + r_ref[0]
            pb_ref[0] = s.astype(BF16)

            @pl.when(pl.program_id(1) == pl_ref[0])
            def _():
                own_ref[...] = s

    grid_spec = pltpu.PrefetchScalarGridSpec(
        num_scalar_prefetch=1, grid=(rows // tr, 4),
        in_specs=[pl.BlockSpec((1, 1, tr, cols), lambda r, k, p: (k, p[1], r, 0))] * n
        + [pl.BlockSpec((1, tr, cols), lambda r, k, p: (k, r, 0))] * n,
        out_specs=[pl.BlockSpec((1, tr, cols), lambda r, k, p: (k, r, 0))] * n
        + [pl.BlockSpec((tr, cols), lambda r, k, p: (r, 0))] * n)
    res = pl.pallas_call(
        body, name=name, grid_spec=grid_spec,
        out_shape=[jax.ShapeDtypeStruct((4, rows, cols), BF16)] * n + [jax.ShapeDtypeStruct((rows, cols), F32)] * n,
        compiler_params=_cparams(("arbitrary", "arbitrary")))(place, *g4s, *r1s)
    return res[:n], res[n:]


PACK_ROWS = 32
VEC_ROW = {"mix_norm": 0, "conv_norm": 1, "b_af": 2, "b_ab": 3, "gla_norm": 4, "xa_norm": 5, "mem_norm": 6,
           "mlp_norm": 7, "final_norm": 8}
LOSS_ROW, MAT_ROW = 9, 16
MAT_LANE = {"w_af": 0, "w_ab": GK, "conv_w": 2 * GK}
MAT_SRC_ROW = {"w_af": 0, "w_ab": LR, "conv_w": 0}


SMALL_WIDTH = {"mix_norm": D, "conv_w": 64, "conv_norm": CW, "w_af": 32, "b_af": GK, "w_ab": 32, "b_ab": GK,
               "gla_norm": 128, "xa_norm": D, "mem_norm": D, "mlp_norm": D, "final_norm": D}


def _small_reduce(acc, loss8):
    names = list(SMALL)
    n = len(names)
    widths = SMALL_WIDTH

    def body(*refs):
        acc_refs = dict(zip(names, refs[:n]))
        loss_ref, tot = refs[n], refs[n + 1]
        pk, all_ref, send_sems, recv_sems, local_sem = refs[n + 2:]

        pk[...] = jnp.zeros_like(pk)
        for k, row in VEC_ROW.items():
            if k == "gla_norm":
                g = functools.reduce(lambda a, b: a + b, [acc_refs[k][pl.ds(0, 1), pl.ds(h * 128, 128)]
                                                          for h in range(NH)])
            else:
                g = acc_refs[k][pl.ds(0, 1), :]
            pk[pl.ds(row, 1), pl.ds(0, widths[k])] = g
        pk[pl.ds(LOSS_ROW, 1), pl.ds(0, 128)] = loss_ref[pl.ds(0, 1), :]
        for k, lane in MAT_LANE.items():
            rows, cols = (3, CW) if k == "conv_w" else (LR, GK)
            pk[pl.ds(MAT_ROW, rows), pl.ds(lane, cols)] = acc_refs[k][pl.ds(MAT_SRC_ROW[k], rows), :]

        x, y, c = _place()
        me, sibling = (x, y, c), (x, y, 1 - c)
        chips = [(1 - x, y, c), (x, 1 - y, c), (1 - x, 1 - y, c)]

        def copy(k, block, to, own=False):
            px, py, pc = block
            dst = all_ref.at[4 * px + 2 * py + pc]
            return pltpu.make_async_remote_copy(
                src_ref=pk if own else dst, dst_ref=dst, send_sem=send_sems.at[k], recv_sem=recv_sems.at[k],
                device_id=to, device_id_type=MESH)

        mine = pltpu.make_async_copy(pk, all_ref.at[4 * x + 2 * y + c], local_sem)
        mine.start()
        first = [copy(0, me, sibling, own=True)] + [copy(1 + j, me, chip, own=True) for j, chip in enumerate(chips)]
        for cp in first:
            cp.start()
        passed = [copy(4 + j, chip, sibling) for j, chip in enumerate(chips)]
        for j, chip in enumerate(chips):
            copy(1 + j, chip, me).wait_recv()
            passed[j].start()
        copy(0, sibling, me).wait_recv()
        for j, (px, py, pc) in enumerate(chips):
            copy(4 + j, (px, py, 1 - pc), me).wait_recv()
        for cp in first + passed:
            cp.wait_send()
        mine.wait()
        total = all_ref[0]
        for d in range(1, NDEV):
            total = total + all_ref[d]
        tot[...] = total

    return pl.pallas_call(
        body, name="small_reduce", out_shape=jax.ShapeDtypeStruct((PACK_ROWS, D), F32),
        scratch_shapes=[pltpu.VMEM((PACK_ROWS, D), F32), pltpu.VMEM((NDEV, PACK_ROWS, D), F32),
                        pltpu.SemaphoreType.DMA((7,)), pltpu.SemaphoreType.DMA((7,)), pltpu.SemaphoreType.DMA],
    )(*[acc[k] for k in names], loss8)


def _small_adamw(tot, ws, ms, vs):
    names = list(SMALL)
    n = len(names)
    widths = SMALL_WIDTH

    def body(*refs):
        tot = refs[0]
        w_refs, m_refs, v_refs = [dict(zip(names, refs[1 + q * n:1 + (q + 1) * n])) for q in range(3)]
        outs = refs[1 + 3 * n:1 + 7 * n]
        g_out, d_out, m_out, v_out = [dict(zip(names, outs[q * n:(q + 1) * n])) for q in range(4)]
        cut = refs[1 + 7 * n]
        x, y, c = _place()
        dev = 4 * x + 2 * y + c
        for k in names:
            if k in VEC_ROW:
                g = tot[pl.ds(VEC_ROW[k], 1), pl.ds(0, widths[k])]
            else:
                rows, cols = (3, CW) if k == "conv_w" else (LR, GK)
                wd = widths[k]
                sel = jnp.where(_iota((cols, wd), 0) == dev * wd + _iota((cols, wd), 1), 1.0, 0.0).astype(BF16)
                cut[:, pl.ds(0, wd)] = _dot_exact_rhs(tot[pl.ds(MAT_ROW, LR), pl.ds(MAT_LANE[k], cols)], sel, 3)
                g = cut[pl.ds(0, rows), pl.ds(0, wd)]
            g_out[k][...] = g
            d_out[k][...], m_out[k][...], v_out[k][...] = _adamw_math(w_refs[k][...], g, m_refs[k][...],
                                                                       v_refs[k][...])

    shapes = [jax.ShapeDtypeStruct(ws[k].shape, F32) for k in names]
    res = pl.pallas_call(
        body, name="small_adamw", out_shape=shapes * 4, scratch_shapes=[pltpu.VMEM((LR, 128), F32)],
    )(tot, *[ws[k] for k in names], *[ms[k] for k in names], *[vs[k] for k in names])
    return {k: tuple(res[q * n + i] for q in range(4)) for i, k in enumerate(names)}


def _adamw_math(w, g, m, v):
    m = ADAM_B1 * m + (1.0 - ADAM_B1) * g
    v = ADAM_B2 * v + (1.0 - ADAM_B2) * (g * g)
    m_hat = m / (1.0 - ADAM_B1 ** ADAM_STEP)
    v_hat = v / (1.0 - ADAM_B2 ** ADAM_STEP)
    delta = -ADAM_LR * (m_hat / (jnp.sqrt(v_hat) + ADAM_EPS) + ADAM_WD * w)
    return delta, m, v


def _adamw(ws, ms, vs, owns, r2s, name, grads_transposed=False):
    n = len(ws)
    _, r, c = ws[0].shape
    tr = 256 if r % 256 == 0 else r

    def body(*refs):
        ins, outs = refs[:5 * n], refs[5 * n:]
        for q in range(n):
            w_ref, m_ref, v_ref, o_ref, r_ref = [ins[k * n + q] for k in range(5)]
            g_ref, d_ref, nm_ref, nv_ref = [outs[k * n + q] for k in range(4)]
            g = ((o_ref[...] + r_ref[0].astype(F32)) + r_ref[1].astype(F32)) + r_ref[2].astype(F32)
            g = g.T if grads_transposed else g
            g_ref[...] = g
            d_ref[...], nm_ref[...], nv_ref[...] = _adamw_math(w_ref[...], g, m_ref[...], v_ref[...])

    spec = pl.BlockSpec((None, tr, c), lambda i: (0, i, 0))
    if grads_transposed:
        own_spec, r2_spec = pl.BlockSpec((c, tr), lambda i: (0, i)), pl.BlockSpec((3, c, tr), lambda i: (0, 0, i))
    else:
        own_spec, r2_spec = pl.BlockSpec((tr, c), lambda i: (i, 0)), pl.BlockSpec((3, tr, c), lambda i: (0, i, 0))
    res = pl.pallas_call(
        body, name=name, grid=(r // tr,),
        in_specs=[spec] * (3 * n) + [own_spec] * n + [r2_spec] * n,
        out_specs=[spec] * (4 * n), out_shape=[jax.ShapeDtypeStruct((1, r, c), F32)] * (4 * n),
        compiler_params=_cparams(("arbitrary",)))(*ws, *ms, *vs, *owns, *r2s)
    return [tuple(res[k * n + q] for k in range(4)) for q in range(n)]


MATS = ("w_in", "w_out", "w_xq", "w_xo", "w_xkv", "w_up", "w_down")
SMALL = ("mix_norm", "conv_w", "conv_norm", "w_af", "b_af", "w_ab", "b_ab", "gla_norm", "xa_norm", "mem_norm",
         "mlp_norm", "final_norm")
WEIGHTS = ("mix_norm", "w_in", "conv_w", "conv_norm", "w_af", "b_af", "w_ab", "b_ab", "gla_norm", "w_out", "xa_norm",
           "mem_norm", "w_xq", "w_xkv", "w_xo", "mlp_norm", "w_up", "w_down", "final_norm")
SMALL_SHARDED = {"conv_w": (3, 64), "w_af": (LR, 32), "w_ab": (LR, 32)}
SMALL_PACK_ROWS = 16


def kernel(x, mem, mix_norm, w_in, conv_w, conv_norm, w_af, b_af, w_ab, b_ab, gla_norm, w_out, xa_norm, mem_norm, w_xq, w_xkv, w_xo, mlp_norm, w_up, w_down, final_norm, loss_target, m_mix_norm, m_w_in, m_conv_w, m_conv_norm, m_w_af, m_b_af, m_w_ab, m_b_ab, m_gla_norm, m_w_out, m_xa_norm, m_mem_norm, m_w_xq, m_w_xkv, m_w_xo, m_mlp_norm, m_w_up, m_w_down, m_final_norm, v_mix_norm, v_w_in, v_conv_w, v_conv_norm, v_w_af, v_b_af, v_w_ab, v_b_ab, v_gla_norm, v_w_out, v_xa_norm, v_mem_norm, v_w_xq, v_w_xkv, v_w_xo, v_mlp_norm, v_w_up, v_w_down, v_final_norm):
    w = dict(mix_norm=mix_norm, w_in=w_in, conv_w=conv_w, conv_norm=conv_norm, w_af=w_af, b_af=b_af, w_ab=w_ab,
             b_ab=b_ab, gla_norm=gla_norm, w_out=w_out, xa_norm=xa_norm, mem_norm=mem_norm, w_xq=w_xq, w_xkv=w_xkv,
             w_xo=w_xo, mlp_norm=mlp_norm, w_up=w_up, w_down=w_down, final_norm=final_norm)
    mom = dict(mix_norm=m_mix_norm, w_in=m_w_in, conv_w=m_conv_w, conv_norm=m_conv_norm, w_af=m_w_af, b_af=m_b_af,
               w_ab=m_w_ab, b_ab=m_b_ab, gla_norm=m_gla_norm, w_out=m_w_out, xa_norm=m_xa_norm, mem_norm=m_mem_norm,
               w_xq=m_w_xq, w_xkv=m_w_xkv, w_xo=m_w_xo, mlp_norm=m_mlp_norm, w_up=m_w_up, w_down=m_w_down,
               final_norm=m_final_norm)
    var = dict(mix_norm=v_mix_norm, w_in=v_w_in, conv_w=v_conv_w, conv_norm=v_conv_norm, w_af=v_w_af, b_af=v_b_af,
               w_ab=v_w_ab, b_ab=v_b_ab, gla_norm=v_gla_norm, w_out=v_w_out, xa_norm=v_xa_norm, mem_norm=v_mem_norm,
               w_xq=v_w_xq, w_xkv=v_w_xkv, w_xo=v_w_xo, mlp_norm=v_mlp_norm, w_up=v_w_up, w_down=v_w_down,
               final_norm=v_final_norm)
    xi, yi, ci = _place()
    two_d = lambda a: a.reshape(a.shape[-2:]) if a.ndim == 3 else a.reshape(1, a.shape[-1])

    small = jnp.concatenate([w[n].reshape(-1) for n in SMALL_SHARDED])
    small = jnp.pad(small, (0, SMALL_PACK_ROWS * 128 - small.shape[0])).reshape(SMALL_PACK_ROWS, 128)
    shard = {n: two_d(w[n]).astype(BF16) for n in MATS}
    for n in ("w_in", "w_up"):
        shard[n] = shard[n].T
    vec = {n: two_d(w[n]) for n in SMALL if n not in SMALL_SHARDED}
    place = jnp.stack([2 * xi + yi, ci]).astype(jnp.int32)
    loss8, grad_x, small_acc, own, from_chips = _step(x[0], mem[0], loss_target[0], shard, small, vec, place)

    tot = _small_reduce(small_acc, loss8)
    small_out = _small_adamw(tot, *[{n: two_d(d[n]) for n in SMALL} for d in (w, mom, var)])
    loss = tot[LOSS_ROW, 0]

    out_g, out_d, out_m, out_v = {}, {}, {}, {}
    wmv = {n: [a.transpose(0, 2, 1) if n == "w_in" else a for a in (w[n], mom[n], var[n])] for n in MATS}
    for shape in dict.fromkeys(wmv[n][0].shape for n in MATS):
        names = [n for n in MATS if wmv[n][0].shape == shape]
        res = _adamw(*[[wmv[n][k] for n in names] for k in range(3)], [own[n] for n in names],
                     [from_chips[n] for n in names], "adamw_" + "_".join(names), grads_transposed=names == ["w_up"])
        for n, r in zip(names, res):
            out_g[n], out_d[n], out_m[n], out_v[n] = [a.transpose(0, 2, 1) for a in r] if n == "w_in" else r
    for n in SMALL:
        out_g[n], out_d[n], out_m[n], out_v[n] = [a.reshape(w[n].shape) for a in small_out[n]]

    return (loss, grad_x[None], *[out_g[n] for n in WEIGHTS], *[out_d[n] for n in WEIGHTS],
            *[out_m[n] for n in WEIGHTS], *[out_v[n] for n in WEIGHTS])
```

```python
import functools
import itertools

import jax
import jax.numpy as jnp
from jax import lax
from jax.experimental import pallas as pl
from jax.experimental.pallas import tpu as pltpu

F32 = jnp.float32
BF16 = jnp.bfloat16

D = 1024
CW = 512
GK = 256
GV = 512
NH = 4
CH = 64
LR = 16
NMEM = 256
XD = 256
FF = 4096
ZW = 3104
ZC = 3200
EPS = 1e-6
NDEV = 8

ZB_CB, ZB_CC, ZB_CU, ZB_V, ZB_G = 0, 1, 2, 4, 5
ZB_Q, ZB_K = 6, 7
ZB_LR = 24

TM = 512
TM_MLP = 256
TM_MLP_FWD = 512
TF = 512
TB = 512
TB_BWD = 512
TT = 2048
VMEM_LIMIT = 56 * 1024 * 1024

ADAM_LR, ADAM_B1, ADAM_B2, ADAM_EPS, ADAM_WD, ADAM_STEP = 0.001, 0.9, 0.999, 1e-08, 0.01, 10

XKV_SHARD = 2 * D // NDEV

MESH = pl.DeviceIdType.MESH


def _cparams(sem):
    return pltpu.CompilerParams(dimension_semantics=sem, vmem_limit_bytes=VMEM_LIMIT)


def _call(body, name, grid, in_specs, out_specs, out_shape, scratch, args, riders=()):
    n_in, n_out, n_scr = len(in_specs), len(out_specs), len(scratch)
    counts = [(len(r.arrays), len(r.out_shape), len(r.scratch)) for r in riders]

    def take(refs, pos, sizes):
        groups = []
        for size in sizes:
            groups.append(refs[pos:pos + size])
            pos += size
        return groups, pos

    def wrapped(*refs):
        ins, pos = refs[:n_in], n_in
        r_ins, pos = take(refs, pos, [c[0] for c in counts])
        outs, pos = refs[pos:pos + n_out], pos + n_out
        r_outs, pos = take(refs, pos, [c[1] for c in counts])
        scr, pos = refs[pos:pos + n_scr], pos + n_scr
        r_scr, pos = take(refs, pos, [c[2] for c in counts])
        ids = [pl.program_id(d) for d in range(len(grid))]
        first = functools.reduce(lambda a, b: a & b, [i == 0 for i in ids])
        last = functools.reduce(lambda a, b: a & b, [i == g - 1 for i, g in zip(ids, grid)])

        @pl.when(first)
        def _():
            for r, a, b, c in zip(riders, r_ins, r_outs, r_scr):
                r.start(a, b, c)

        body(*ins, *outs, *scr)

        for hook, at in (("forward", [g // 2 for g in grid]), ("relay", [max(g - 2, 0) for g in grid])):
            if any(getattr(r, hook) for r in riders):
                @pl.when(functools.reduce(lambda a, b: a & b, [i == s for i, s in zip(ids, at)]))
                def _(hook=hook):
                    for r, a, b, c in zip(riders, r_ins, r_outs, r_scr):
                        if getattr(r, hook):
                            getattr(r, hook)(a, b, c)

        @pl.when(last)
        def _():
            for r, a, b, c in zip(riders, r_ins, r_outs, r_scr):
                r.finish(a, b, c)

    hbm = pl.BlockSpec(memory_space=pltpu.HBM)
    r_args = [a for r in riders for a in r.arrays]
    r_shapes = [s for r in riders for s in r.out_shape]
    return pl.pallas_call(
        wrapped if riders else body, name=name, grid=grid, in_specs=list(in_specs) + [hbm] * len(r_args),
        out_specs=list(out_specs) + [hbm] * len(r_shapes), out_shape=list(out_shape) + r_shapes,
        scratch_shapes=list(scratch) + [s for r in riders for s in r.scratch],
        compiler_params=_cparams(("arbitrary",) * len(grid)))(*args, *r_args)


def _dot(a, b):
    return jnp.dot(a.astype(BF16), b.astype(BF16), preferred_element_type=F32)


def _dot_nt(a, b):
    return lax.dot_general(a.astype(BF16), b.astype(BF16), (((1,), (1,)), ((), ())), preferred_element_type=F32)


def _dot_tn(a, b):
    return lax.dot_general(a.astype(BF16), b.astype(BF16), (((0,), (0,)), ((), ())), preferred_element_type=F32)


def _split(x, n):
    parts = []
    for _ in range(n):
        p = x.astype(BF16)
        parts.append(p)
        x = x - p.astype(F32)
    return parts


def _dot_exact_lhs(m, x, n):
    return functools.reduce(lambda a, b: a + b, [jnp.dot(m, p, preferred_element_type=F32) for p in _split(x, n)])


def _dot_exact_rhs(x, m, n):
    return functools.reduce(lambda a, b: a + b, [jnp.dot(p, m, preferred_element_type=F32) for p in _split(x, n)])


def _rms(x, g):
    r = lax.rsqrt(jnp.mean(x * x, axis=-1, keepdims=True) + EPS)
    return x * r * g, r


def _rms_bwd(x, r, g, dy):
    xr = x * r
    u = dy * g
    dx = r * (u - xr * jnp.mean(u * xr, axis=-1, keepdims=True))
    return dx, jnp.sum(dy * xr, axis=0, keepdims=True)


def _iota(shape, dim):
    return lax.broadcasted_iota(jnp.int32, shape, dim)


def _sigmoid(x):
    return 1.0 / (1.0 + jnp.exp(-x))


def _acc_rows(ref, row):
    ref[...] += jnp.broadcast_to(row, ref.shape)


ZA = ZW // 128 * 128


def _w_in_tail(w_ref):
    return jnp.concatenate([w_ref[ZA:ZW], jnp.zeros((ZC - ZW, D), BF16)], axis=0)


def _inproj(x, g, w_t, riders=()):
    t = x.shape[0]
    tm = min(TM, t)

    def body(x_ref, g_ref, w_ref, z_ref, h_ref):
        h, _ = _rms(x_ref[...], g_ref[...])
        hb = h.astype(BF16)
        h_ref[...] = hb
        z_ref[:, :ZA] = _dot_nt(hb, w_ref[:ZA])
        z_ref[:, ZA:] = _dot_nt(hb, _w_in_tail(w_ref))

    return _call(
        body, "inproj", (t // tm,),
        [pl.BlockSpec((tm, D), lambda i: (i, 0)), pl.BlockSpec((1, D), lambda i: (0, 0)),
         pl.BlockSpec((ZW, D), lambda i: (0, 0))],
        [pl.BlockSpec((tm, ZC), lambda i: (i, 0)), pl.BlockSpec((tm, D), lambda i: (i, 0))],
        [jax.ShapeDtypeStruct((t, ZC), F32), jax.ShapeDtypeStruct((t, D), BF16)], [], (x, g, w_t), riders)


def _kv_proj(mem, g, w):
    def body(m_ref, g_ref, w_ref, kv_ref, mn_ref):
        mn, _ = _rms(m_ref[...], g_ref[...])
        mb = mn.astype(BF16)
        mn_ref[...] = mb
        for j in range(NDEV):
            kv_ref[:, j * XKV_SHARD:(j + 1) * XKV_SHARD] = jnp.dot(mb, w_ref[j], preferred_element_type=F32)

    return pl.pallas_call(
        body, name="kv_proj",
        out_shape=[jax.ShapeDtypeStruct((NMEM, 2 * D), F32), jax.ShapeDtypeStruct((NMEM, D), BF16)],
        compiler_params=pltpu.CompilerParams(vmem_limit_bytes=VMEM_LIMIT))(mem, g, w)


def _softmax_head(qb, kb):
    s = _dot_nt(qb, kb) * (1.0 / 16.0)
    e = jnp.exp(s - jnp.max(s, axis=-1, keepdims=True))
    return e / jnp.sum(e, axis=-1, keepdims=True)


def _attn_fwd(x, z, o_f, o_b, conv_w, conv_norm, gla_norm4, w_out, g, w_xq, kb, vb, w_xo):
    t = x.shape[0]
    tm = min(TM, t)
    nblk = t // tm
    jmap = lambda i: i

    def body(x_ref, zq_ref, zk_ref, zv_ref, zg_ref, cb_ref, cc_ref, cu_ref, ccp_ref, ccn_ref, cup_ref, cun_ref,
             of_ref, ob_ref, cw_ref, cn_ref, gn_ref, wo_ref, g_ref, wq_ref, k_ref, v_ref, wx_ref,
             x1_ref, x2_ref, xn_ref, q_ref, a_ref, y_ref, opre_ref):
        j = pl.program_id(0)
        zv = zv_ref[...]
        sb = _head_sum((zq_ref[...] * 0.125) * zk_ref[...], 64, 128)
        o_pre = of_ref[...] + ob_ref[...] - sb * zv
        opre_ref[...] = o_pre
        on, _ = _head_norm(o_pre)
        zg = zg_ref[...]
        y_ref[:, CW:] = (on * gn_ref[...] * (zg * _sigmoid(zg))).astype(BF16)
        cb = cb_ref[...]
        _, _, _, conv = _conv_parts(cb, cc_ref[...], cu_ref[...], ccp_ref[pl.ds(7, 1), :], cup_ref[pl.ds(7, 1), :],
                                    ccn_ref[pl.ds(0, 1), :], cun_ref[pl.ds(0, 1), :], cw_ref, j == 0,
                                    j == nblk - 1, tm)
        yc = cb * conv
        gm = _group_sum(yc * yc) * (1.0 / 64.0)
        y_ref[:, :CW] = (yc * lax.rsqrt(gm + EPS) * cn_ref[...]).astype(BF16)

        x1 = x_ref[...] + jnp.dot(y_ref[...], wo_ref[...], preferred_element_type=F32)
        x1_ref[...] = x1
        xn, _ = _rms(x1, g_ref[...])
        xb = xn.astype(BF16)
        xn_ref[...] = xb
        qb = jnp.dot(xb, wq_ref[...], preferred_element_type=F32).astype(BF16)
        q_ref[...] = qb
        heads = [slice(h * XD, (h + 1) * XD) for h in range(NH)]
        ps = [_softmax_head(qb[:, hs], k_ref[:, hs]) for hs in heads]
        for hs, p in zip(heads, ps):
            a_ref[:, hs] = _dot(p, v_ref[:, hs]).astype(BF16)
        x2_ref[...] = x1 + jnp.dot(a_ref[...], wx_ref[...], preferred_element_type=F32)

    tok = lambda i: (i, 0)
    full = lambda i: (0, 0)
    once = pl.Buffered(1)
    tokd, tokv = pl.BlockSpec((tm, D), tok), pl.BlockSpec((tm, GV), tok)
    weight = pl.BlockSpec((D, D), full, pipeline_mode=once)
    ccp, ccn = _halo_specs(tm, nblk, t, ZB_CC, jmap)
    cup, cun = _halo_specs(tm, nblk, t, ZB_CU, jmap)
    in_specs = [tokd, _zspec(tm, GK, ZB_Q, jmap), _zspec(tm, GK, ZB_K, jmap), _zspec(tm, GV, ZB_V, jmap),
                _zspec(tm, GV, ZB_G, jmap), _zspec(tm, CW, ZB_CB, jmap), _zspec(tm, CW, ZB_CC, jmap),
                _zspec(tm, CW, ZB_CU, jmap), ccp, ccn, cup, cun, tokv, tokv,
                pl.BlockSpec((3, CW), full), pl.BlockSpec((1, CW), full), pl.BlockSpec((1, GV), full),
                weight, pl.BlockSpec((1, D), full), weight, pl.BlockSpec((NMEM, D), full),
                pl.BlockSpec((NMEM, D), full), weight]
    return pl.pallas_call(
        body, name="attn_fwd", grid=(nblk,), in_specs=in_specs, out_specs=[tokd] * 6 + [tokv],
        out_shape=[jax.ShapeDtypeStruct((t, D), F32), jax.ShapeDtypeStruct((t, D), F32),
                   jax.ShapeDtypeStruct((t, D), BF16), jax.ShapeDtypeStruct((t, D), BF16),
                   jax.ShapeDtypeStruct((t, D), BF16), jax.ShapeDtypeStruct((t, D), BF16),
                   jax.ShapeDtypeStruct((t, GV), F32)],
        compiler_params=_cparams(("arbitrary",)))(
            x, z, z, z, z, z, z, z, z, z, z, z, o_f, o_b, conv_w, conv_norm, gla_norm4, w_out, g, w_xq, kb, vb, w_xo)


def _mlp_fwd(x2, g, w_up_t, w_down, fg, target):
    t = x2.shape[0]
    tm = min(TM_MLP_FWD, t)

    def body(x_ref, g_ref, wu_ref, wd_ref, fg_ref, t_ref, h1_ref, xn_ref, dx_ref, dxb_ref, loss_ref, dfg_ref, ab):
        @pl.when(pl.program_id(0) == 0)
        def _():
            loss_ref[...] = jnp.zeros_like(loss_ref)
            dfg_ref[...] = jnp.zeros_like(dfg_ref)

        x = x_ref[...]
        xn, _ = _rms(x, g_ref[...])
        xnb = xn.astype(BF16)
        xn_ref[...] = xnb
        for q in range(FF // TF):
            cols = slice(q * TF, (q + 1) * TF)
            h1 = _dot_nt(xnb, wu_ref[cols, :])
            h1_ref[:, cols] = h1.astype(BF16)
            hr = jnp.maximum(h1, 0.0)
            ab[:, cols] = (hr * hr).astype(BF16)
        x3 = x + jnp.dot(ab[...], wd_ref[...], preferred_element_type=F32)
        y, r = _rms(x3, fg_ref[...])
        e = y - t_ref[...]
        row = jnp.mean(e * e, axis=-1, keepdims=True)
        _acc_rows(loss_ref, 0.5 * jnp.sum(row, axis=0, keepdims=True))
        dx, dfg = _rms_bwd(x3, r, fg_ref[...], e * (1.0 / D))
        dx_ref[...] = dx
        dxb_ref[...] = dx.astype(BF16)
        _acc_rows(dfg_ref, dfg)

    tok = lambda i: (i, 0)
    full = lambda i: (0, 0)
    once = pl.Buffered(1)
    return pl.pallas_call(
        body, name="mlp_fwd", grid=(t // tm,),
        in_specs=[pl.BlockSpec((tm, D), tok), pl.BlockSpec((1, D), full),
                  pl.BlockSpec((FF, D), full, pipeline_mode=once), pl.BlockSpec((FF, D), full, pipeline_mode=once),
                  pl.BlockSpec((1, D), full), pl.BlockSpec((tm, D), tok)],
        out_specs=[pl.BlockSpec((tm, FF), tok), pl.BlockSpec((tm, D), tok), pl.BlockSpec((tm, D), tok),
                   pl.BlockSpec((tm, D), tok), pl.BlockSpec((8, 128), full), pl.BlockSpec((8, D), full)],
        out_shape=[jax.ShapeDtypeStruct((t, FF), BF16), jax.ShapeDtypeStruct((t, D), BF16),
                   jax.ShapeDtypeStruct((t, D), F32), jax.ShapeDtypeStruct((t, D), BF16),
                   jax.ShapeDtypeStruct((8, 128), F32), jax.ShapeDtypeStruct((8, D), F32)],
        scratch_shapes=[pltpu.VMEM((tm, FF), BF16)],
        compiler_params=_cparams(("arbitrary",)))(x2, g, w_up_t, w_down, fg, target)


def _mlp_bwd(dx3, dx3b, h1b, w_down, w_up_t, x2, g):
    t = x2.shape[0]
    tm = min(TM_MLP, t)

    def body(dx_ref, dxb_ref, h1_ref, wd_ref, wu_ref, x_ref, g_ref, a_ref, dh_ref, dx2_ref, dx2b_ref, dg_ref):
        @pl.when(pl.program_id(0) == 0)
        def _():
            dg_ref[...] = jnp.zeros_like(dg_ref)

        for q in range(FF // TF):
            cols = slice(q * TF, (q + 1) * TF)
            hr = jnp.maximum(h1_ref[:, cols].astype(F32), 0.0)
            da = _dot_nt(dxb_ref[...], wd_ref[cols, :])
            a_ref[:, cols] = (hr * hr).astype(BF16)
            dh_ref[:, cols] = (da * 2.0 * hr).astype(BF16)
        dxn = jnp.dot(dh_ref[...], wu_ref[...], preferred_element_type=F32)
        x = x_ref[...]
        r = lax.rsqrt(jnp.mean(x * x, axis=-1, keepdims=True) + EPS)
        dx, dg = _rms_bwd(x, r, g_ref[...], dxn)
        dx2 = dx_ref[...] + dx
        dx2_ref[...] = dx2
        dx2b_ref[...] = dx2.astype(BF16)
        _acc_rows(dg_ref, dg)

    tok = lambda i: (i, 0)
    full = lambda i: (0, 0)
    once = pl.Buffered(1)
    return pl.pallas_call(
        body, name="mlp_bwd", grid=(t // tm,),
        in_specs=[pl.BlockSpec((tm, D), tok), pl.BlockSpec((tm, D), tok), pl.BlockSpec((tm, FF), tok),
                  pl.BlockSpec((FF, D), full, pipeline_mode=once), pl.BlockSpec((FF, D), full, pipeline_mode=once),
                  pl.BlockSpec((tm, D), tok), pl.BlockSpec((1, D), full)],
        out_specs=[pl.BlockSpec((tm, FF), tok), pl.BlockSpec((tm, FF), tok), pl.BlockSpec((tm, D), tok),
                   pl.BlockSpec((tm, D), tok), pl.BlockSpec((8, D), full)],
        out_shape=[jax.ShapeDtypeStruct((t, FF), BF16), jax.ShapeDtypeStruct((t, FF), BF16),
                   jax.ShapeDtypeStruct((t, D), F32), jax.ShapeDtypeStruct((t, D), BF16),
                   jax.ShapeDtypeStruct((8, D), F32)],
        compiler_params=_cparams(("arbitrary",)))(dx3, dx3b, h1b, w_down, w_up_t, x2, g)


def _attn_bwd(x1, dx2, dx2b, qb, kb, vb, w_xo, w_xq, w_out, g, riders=()):
    t = x1.shape[0]
    tm = min(TM, t)

    def body(x_ref, dx2_ref, dx2b_ref, q_ref, k_ref, v_ref, wx_ref, wq_ref, wo_ref, g_ref,
             dx1_ref, dx1b_ref, dy_ref, dq_ref, dkv_ref, dg_ref):
        @pl.when(pl.program_id(0) == 0)
        def _():
            dkv_ref[...] = jnp.zeros_like(dkv_ref)
            dg_ref[...] = jnp.zeros_like(dg_ref)

        datt = _dot_nt(dx2b_ref[...], wx_ref[...]).astype(BF16)
        heads = [slice(h * XD, (h + 1) * XD) for h in range(NH)]
        ps = [_softmax_head(q_ref[:, hs], k_ref[:, hs]) for hs in heads]
        dps = [_dot_nt(datt[:, hs], v_ref[:, hs]) for hs in heads]
        dss = [(p * (dp - jnp.sum(dp * p, axis=-1, keepdims=True)) * (1.0 / 16.0)).astype(BF16)
               for p, dp in zip(ps, dps)]
        for h, (hs, p, ds) in enumerate(zip(heads, ps, dss)):
            dq_ref[:, hs] = _dot(ds, k_ref[:, hs]).astype(BF16)
            dkv_ref[:, hs] += _dot_tn(ds, q_ref[:, hs])
            dkv_ref[:, D + h * XD:D + (h + 1) * XD] += _dot_tn(p, datt[:, hs])
        dxn = _dot_nt(dq_ref[...], wq_ref[...])
        x = x_ref[...]
        r = lax.rsqrt(jnp.mean(x * x, axis=-1, keepdims=True) + EPS)
        dx, dg = _rms_bwd(x, r, g_ref[...], dxn)
        dx1 = dx2_ref[...] + dx
        dx1_ref[...] = dx1
        dx1b = dx1.astype(BF16)
        dx1b_ref[...] = dx1b
        dy_ref[...] = _dot_nt(dx1b, wo_ref[...])
        _acc_rows(dg_ref, dg)

    tok = lambda i: (i, 0)
    full = lambda i: (0, 0)
    return _call(
        body, "attn_bwd", (t // tm,),
        [pl.BlockSpec((tm, D), tok), pl.BlockSpec((tm, D), tok), pl.BlockSpec((tm, D), tok),
         pl.BlockSpec((tm, D), tok), pl.BlockSpec((NMEM, D), full), pl.BlockSpec((NMEM, D), full),
         pl.BlockSpec((D, D), full), pl.BlockSpec((D, D), full), pl.BlockSpec((D, D), full),
         pl.BlockSpec((1, D), full)],
        [pl.BlockSpec((tm, D), tok), pl.BlockSpec((tm, D), tok), pl.BlockSpec((tm, D), tok),
         pl.BlockSpec((tm, D), tok), pl.BlockSpec((NMEM, 2 * D), full), pl.BlockSpec((8, D), full)],
        [jax.ShapeDtypeStruct((t, D), F32), jax.ShapeDtypeStruct((t, D), BF16),
         jax.ShapeDtypeStruct((t, D), F32), jax.ShapeDtypeStruct((t, D), BF16),
         jax.ShapeDtypeStruct((NMEM, 2 * D), F32), jax.ShapeDtypeStruct((8, D), F32)], [],
        (x1, dx2, dx2b, qb, kb, vb, w_xo, w_xq, w_out, g), riders)


def _kv_bwd(dkv, memn, mem, g, w):
    def body(dkv_ref, mn_ref, m_ref, g_ref, w_ref, dw_ref, dg_ref):
        dkvb = dkv_ref[...].astype(BF16)
        dmn = jnp.zeros((NMEM, D), F32)
        for j in range(NDEV):
            cols = slice(j * XKV_SHARD, (j + 1) * XKV_SHARD)
            dw_ref[j] = _dot_tn(mn_ref[...], dkvb[:, cols])
            dmn += _dot_nt(dkvb[:, cols], w_ref[j])
        m = m_ref[...]
        r = lax.rsqrt(jnp.mean(m * m, axis=-1, keepdims=True) + EPS)
        dg_ref[...] = jnp.broadcast_to(jnp.sum(dmn * m * r, axis=0, keepdims=True), dg_ref.shape)

    return pl.pallas_call(
        body, name="kv_bwd",
        out_shape=[jax.ShapeDtypeStruct((NDEV, D, XKV_SHARD), F32), jax.ShapeDtypeStruct((8, D), F32)],
        compiler_params=pltpu.CompilerParams(vmem_limit_bytes=VMEM_LIMIT))(dkv, memn, mem, g, w)


def _inproj_bwd(dz, w_t, x, dx1, g, riders=()):
    t = x.shape[0]
    tm = min(TM, t)

    def body(dz_ref, w_ref, x_ref, dx1_ref, g_ref, gx_ref, dg_ref):
        @pl.when(pl.program_id(0) == 0)
        def _():
            dg_ref[...] = jnp.zeros_like(dg_ref)

        dh = (jnp.dot(dz_ref[:, :ZA], w_ref[:ZA], preferred_element_type=F32)
              + jnp.dot(dz_ref[:, ZA:], _w_in_tail(w_ref), preferred_element_type=F32))
        x = x_ref[...]
        r = lax.rsqrt(jnp.mean(x * x, axis=-1, keepdims=True) + EPS)
        dx, dg = _rms_bwd(x, r, g_ref[...], dh)
        gx_ref[...] = dx1_ref[...] + dx
        _acc_rows(dg_ref, dg)

    tok = lambda i: (i, 0)
    full = lambda i: (0, 0)
    return _call(
        body, "inproj_bwd", (t // tm,),
        [pl.BlockSpec((tm, ZC), tok), pl.BlockSpec((ZW, D), full), pl.BlockSpec((tm, D), tok),
         pl.BlockSpec((tm, D), tok), pl.BlockSpec((1, D), full)],
        [pl.BlockSpec((tm, D), tok), pl.BlockSpec((8, D), full)],
        [jax.ShapeDtypeStruct((t, D), F32), jax.ShapeDtypeStruct((8, D), F32)], [], (dz, w_t, x, dx1, g), riders)


def _matmul_tn(a, b, name, rows=None, riders=()):
    t, k = a.shape
    n = b.shape[1]
    tk, tn = [1024 if size % 1024 == 0 else 640 for size in (k, n)]
    tt = min(TT, t)
    rows = rows or k

    def body(a_ref, b_ref, o_ref):
        @pl.when(pl.program_id(2) == 0)
        def _():
            o_ref[...] = jnp.zeros_like(o_ref)

        o_ref[...] += _dot_tn(a_ref[...], b_ref[...])

    return _call(
        body, name, (k // tk, n // tn, t // tt),
        [pl.BlockSpec((tt, tk), lambda i, j, s: (s, i)), pl.BlockSpec((tt, tn), lambda i, j, s: (s, j))],
        [pl.BlockSpec((tk, tn), lambda i, j, s: (i, j))], [jax.ShapeDtypeStruct((rows, n), F32)], [], (a, b), riders)


def _lane_head(shape, dim, shift):
    return _iota(shape, dim) >> shift


CUM_ROWS = 128


def _chunk_cumsum(x, upper, n):
    r, c = _iota((CUM_ROWS, CUM_ROWS), 0), _iota((CUM_ROWS, CUM_ROWS), 1)
    tri = (c >= r) if upper else (c <= r)
    cum = jnp.where(((r >> 6) == (c >> 6)) & tri, 1.0, 0.0).astype(BF16)
    return jnp.concatenate([_dot_exact_lhs(cum, x[g:g + CUM_ROWS], n) for g in range(0, x.shape[0], CUM_ROWS)],
                           axis=0)


def _gla_recompute(q_raw, k, lr, wpad, bias, rev, tb):
    pre = _dot(lr, wpad) + bias
    la = (jnp.minimum(pre, 0.0) - jnp.log(1.0 + jnp.exp(-jnp.abs(pre)))) * (1.0 / 16.0)
    b = _chunk_cumsum(la, rev, 3)
    e, ei = jnp.exp(b), jnp.exp(-b)
    qt = (q_raw * 0.125) * e
    kt = k * ei
    return pre, b, e, ei, qt, kt


def _stack_heads(x, shift):
    head = _lane_head(x.shape, 1, shift)
    return jnp.concatenate([jnp.where(head == h, x, 0.0) for h in range(NH)], axis=0).astype(BF16)


def _fold_heads(x, shift):
    head = _lane_head((CH, x.shape[1]), 1, shift)
    return functools.reduce(lambda a, b: a + b,
                            [jnp.where(head == h, x[h * CH:(h + 1) * CH], 0.0) for h in range(NH)])


def _wide_mask(rev):
    r, s = _iota((CH, NH * CH), 0), _iota((CH, NH * CH), 1) & (CH - 1)
    return (s >= r) if rev else (s <= r)


def _rows_by_head(x):
    w = x.shape[1] // NH
    return jnp.concatenate([x[:, h * w:(h + 1) * w] for h in range(NH)], axis=0)


def _lanes_by_head(x):
    return jnp.concatenate([x[h * CH:(h + 1) * CH] for h in range(NH)], axis=1)


def _state_compact(xt):
    head = _lane_head((128, GK), 1, 6)
    return functools.reduce(lambda a, b: a + b,
                            [jnp.where(head == h, xt[h * 128:(h + 1) * 128], 0.0) for h in range(NH)])


def _conv_parts(cb, cc, cu, ccp, cup, ccn, cun, cw_ref, first, last, tb):
    h = cc * cu
    hp = jnp.where(first, 0.0, ccp * cup)
    hn = jnp.where(last, 0.0, ccn * cun)
    rows = _iota(h.shape, 0)
    h_m1 = jnp.where(rows == 0, hp, pltpu.roll(h, 1, 0))
    h_p1 = jnp.where(rows == tb - 1, hn, pltpu.roll(h, tb - 1, 0))
    conv = cw_ref[pl.ds(0, 1), :] * h_m1 + cw_ref[pl.ds(1, 1), :] * h + cw_ref[pl.ds(2, 1), :] * h_p1
    return h, h_m1, h_p1, conv


def _head_sum(x, w_in, w_out):
    shape, sh_in, sh_out = (2 * w_in, 2 * w_out), w_in.bit_length() - 1, w_out.bit_length() - 1
    sel = jnp.where((_iota(shape, 0) >> sh_in) == (_iota(shape, 1) >> sh_out), 1.0, 0.0).astype(BF16)
    return jnp.concatenate([_dot_exact_rhs(x[:, s:s + 2 * w_in], sel, 2) for s in range(0, NH * w_in, 2 * w_in)],
                           axis=1)


def _group_sum(x):
    ones = jnp.where((_iota((128, 128), 0) >> 6) == (_iota((128, 128), 1) >> 6), 1.0, 0.0).astype(BF16)
    return jnp.concatenate([_dot_exact_rhs(x[:, s:s + 128], ones, 2) for s in range(0, x.shape[1], 128)], axis=1)


def _head_norm(o):
    ons, rs = [], []
    for h in range(NH):
        slab = o[:, h * 128:(h + 1) * 128]
        r = lax.rsqrt(jnp.mean(slab * slab, axis=-1, keepdims=True) + EPS)
        ons.append(slab * r)
        rs.append(jnp.broadcast_to(r, slab.shape))
    return jnp.concatenate(ons, axis=1), jnp.concatenate(rs, axis=1)


def _zspec(tb, width, blk, jmap):
    return pl.BlockSpec((tb, width), lambda i: (jmap(i), blk))


def _halo_specs(tb, nblk, t, blk, jmap):
    prev = pl.BlockSpec((8, CW), lambda i: (jnp.maximum(jmap(i) * (tb // 8) - 1, 0), blk))
    nxt = pl.BlockSpec((8, CW), lambda i: (jnp.minimum((jmap(i) + 1) * (tb // 8), t // 8 - 1), blk))
    return prev, nxt


def _gla_fwd_block(q_ref, k_ref, v_ref, lr_ref, w_ref, bias_ref, o_ref, sd_ref, st, b_scr, rev, tb):
    nb = tb // CH
    _, b, _, _, qt, kt = _gla_recompute(q_ref[...], k_ref[...], lr_ref[...], w_ref[...], bias_ref[...], rev, tb)
    v = v_ref[...]
    b_scr[...] = b
    yield
    maskw = _wide_mask(rev)
    order = list(reversed(range(nb))) if rev else list(range(nb))
    rows = [slice(c * CH, (c + 1) * CH) for c in range(nb)]
    state = st[...]
    for c in order:
        gdec = jnp.exp(b_scr[pl.ds(c * CH + (0 if rev else CH - 1), 1), :])
        sd_ref[c] = state
        a = jnp.where(maskw, _dot_nt(qt[rows[c]], _stack_heads(kt[rows[c]], 6)), 0.0)
        o_inter = _lanes_by_head(_dot_nt(_stack_heads(qt[rows[c]], 6), state))
        o_ref[pl.ds(c * CH, CH), :] = _dot(a, _stack_heads(v[rows[c]], 7)) + o_inter
        state = state * gdec + _state_compact(_dot_tn(v[rows[c]], kt[rows[c]] * gdec))
        yield
    st[...] = state
    yield


def _gla_fwd(z, waf_pad, b_af, wab_pad, b_ab, riders=()):
    t = z.shape[0]
    tb = min(TB, t)
    nblk, nb = t // tb, tb // CH
    jmaps = (lambda i: i, lambda i: nblk - 1 - i)

    def body(qf, kf, vf, lrf, qr, kr, vr, lrr, wf, bf, wr, br, of_ref, sdf_ref, or_ref, sdr_ref,
             st_f, st_r, b_f, b_r):
        @pl.when(pl.program_id(0) == 0)
        def _():
            st_f[...] = jnp.zeros_like(st_f)
            st_r[...] = jnp.zeros_like(st_r)

        for _ in zip(_gla_fwd_block(qf, kf, vf, lrf, wf, bf, of_ref, sdf_ref, st_f, b_f, False, tb),
                     _gla_fwd_block(qr, kr, vr, lrr, wr, br, or_ref, sdr_ref, st_r, b_r, True, tb)):
            pass

    full = lambda i: (0, 0)
    zspecs = [s for jm in jmaps for s in (_zspec(tb, GK, ZB_Q, jm), _zspec(tb, GK, ZB_K, jm),
                                         _zspec(tb, GV, ZB_V, jm), _zspec(tb, 128, ZB_LR, jm))]
    wspecs = [pl.BlockSpec((128, GK), full), pl.BlockSpec((1, GK), full)] * 2
    out_specs = [s for jm in jmaps for s in (pl.BlockSpec((tb, GV), lambda i, jm=jm: (jm(i), 0)),
                                             pl.BlockSpec((nb, 128, GK), lambda i, jm=jm: (jm(i), 0, 0)))]
    out_shape = [jax.ShapeDtypeStruct((t, GV), F32), jax.ShapeDtypeStruct((t // CH, 128, GK), F32)] * 2
    scratch = [pltpu.VMEM((128, GK), F32), pltpu.VMEM((128, GK), F32), pltpu.VMEM((tb, GK), F32),
               pltpu.VMEM((tb, GK), F32)]
    return _call(body, "gla_fwd", (nblk,), zspecs + wspecs, out_specs, out_shape, scratch,
                 [z] * 8 + [waf_pad, b_af, wab_pad, b_ab], riders)


def _gla_bwd_chunks(do_ref, sd_ref, dst, b_scr, db_scr, dq_ref, dk_ref, dv_ref, qt, kt, e, ei, v, rev, nb):
    maskw = _wide_mask(rev)
    for c in (range(nb) if rev else reversed(range(nb))):
        sl = slice(c * CH, (c + 1) * CH)
        grow = c * CH + (0 if rev else CH - 1)
        gdec = jnp.exp(b_scr[pl.ds(grow, 1), :])
        qt_c, kt_c, v_c, do_c = qt[sl], kt[sl], v[sl], do_ref[pl.ds(c * CH, CH), :]
        s_in, ds_out = sd_ref[c], dst[...]
        kbd, vbd = _stack_heads(kt_c, 6), _stack_heads(v_c, 7)
        a = jnp.where(maskw, _dot_nt(qt_c, kbd), 0.0)
        da = jnp.where(maskw, _dot_nt(do_c, vbd), 0.0)
        dv_ref[pl.ds(c * CH, CH), :] = (_fold_heads(_dot_tn(a, do_c), 7)
                                        + _lanes_by_head(_dot_nt(_stack_heads(kt_c * gdec, 6), ds_out)))
        dqt = _dot(da, kbd) + _fold_heads(_dot(_rows_by_head(do_c), s_in), 6)
        dkh = _fold_heads(_dot(_rows_by_head(v_c), ds_out), 6)
        da_do = jnp.concatenate([da.astype(BF16), do_c.astype(BF16)], axis=1)
        both = _dot_tn(da_do, qt_c)
        dkt = _fold_heads(both[:NH * CH], 6) + dkh * gdec
        dg = jnp.sum(ds_out * s_in, axis=0, keepdims=True) + jnp.sum(kt_c * dkh, axis=0, keepdims=True)
        db_scr[pl.ds(c * CH, CH), :] = dqt * qt_c - dkt * kt_c
        db_scr[pl.ds(grow, 1), :] += dg * gdec
        dq_ref[pl.ds(c * CH, CH), :] = dqt * e[sl] * 0.125
        dk_ref[pl.ds(c * CH, CH), :] = dkt * ei[sl]
        dst[...] = ds_out * gdec + _state_compact(both[NH * CH:])
        yield


def _gate_bwd(db, pre, lr, wpad, rev, tb):
    dla = _chunk_cumsum(db, not rev, 2)
    dpre = dla * (1.0 / 16.0) / (1.0 + jnp.exp(pre))
    return dpre, _dot_nt(dpre, wpad), _dot_tn(lr, dpre)


def _gla_bwd_first(z, dy, o_pre, sd, wpad, bias, conv_w, conv_norm, gla_norm4, riders=()):
    t = z.shape[0]
    tb = min(TB_BWD, t)
    nblk, nb = t // tb, tb // CH
    jmap = lambda i: nblk - 1 - i

    def body(q_ref, k_ref, v_ref, lr_ref, g_ref, cb_ref, cc_ref, cu_ref, ccp_ref, ccn_ref, cup_ref, cun_ref,
             dy_ref, opre_ref, sd_ref, w_ref, bias_ref, cw_ref, cn_ref, gn_ref,
             do_ref, dq_ref, dk_ref, dv_ref, dlr_ref, dzg_ref, dzcb_ref, dconv_ref,
             dw_ref, dbias_ref, dcw_ref, dcn_ref, dgn_ref, dst, b_scr, db_scr):
        i = pl.program_id(0)
        j = jmap(i)

        @pl.when(i == 0)
        def _():
            dst[...] = jnp.zeros_like(dst)
            for ref in (dw_ref, dbias_ref, dcw_ref, dcn_ref, dgn_ref):
                ref[...] = jnp.zeros_like(ref)

        dyg = dy_ref[:, CW:]
        g = g_ref[...]
        sig = _sigmoid(g)
        on, rr = _head_norm(opre_ref[...])
        gn = gn_ref[...]
        dzg_ref[...] = (dyg * on * gn * (sig * (1.0 + g * (1.0 - sig)))).astype(BF16)
        don = dyg * (g * sig)
        _acc_rows(dgn_ref, jnp.sum(don * on, axis=0, keepdims=True))
        u = don * gn
        uo = u * on
        mean_uo = jnp.concatenate(
            [jnp.broadcast_to(jnp.mean(uo[:, h * 128:(h + 1) * 128], axis=-1, keepdims=True), (tb, 128))
             for h in range(NH)], axis=1)
        do_ref[...] = rr * (u - on * mean_uo)

        def conv_branch():
            cb = cb_ref[...]
            h, h_m1, h_p1, conv = _conv_parts(cb, cc_ref[...], cu_ref[...], ccp_ref[pl.ds(7, 1), :],
                                              cup_ref[pl.ds(7, 1), :], ccn_ref[pl.ds(0, 1), :],
                                              cun_ref[pl.ds(0, 1), :], cw_ref, j == 0, j == nblk - 1, tb)
            yc = cb * conv
            yield
            rc = lax.rsqrt(_group_sum(yc * yc) * (1.0 / 64.0) + EPS)
            ycr = yc * rc
            yield
            dyn = dy_ref[:, :CW]
            _acc_rows(dcn_ref, jnp.sum(dyn * ycr, axis=0, keepdims=True))
            uc = dyn * cn_ref[...]
            yield
            dyc = rc * (uc - ycr * (_group_sum(uc * ycr) * (1.0 / 64.0)))
            dzcb_ref[...] = (dyc * conv).astype(BF16)
            yield
            dconv = dyc * cb
            dconv_ref[...] = dconv
            yield
            dcw_ref[pl.ds(0, 1), :] += jnp.sum(dconv * h_m1, axis=0, keepdims=True)
            dcw_ref[pl.ds(1, 1), :] += jnp.sum(dconv * h, axis=0, keepdims=True)
            dcw_ref[pl.ds(2, 1), :] += jnp.sum(dconv * h_p1, axis=0, keepdims=True)
            yield

        lr, wp = lr_ref[...], w_ref[...]
        pre, b, e, ei, qt, kt = _gla_recompute(q_ref[...], k_ref[...], lr, wp, bias_ref[...], False, tb)
        b_scr[...] = b
        for _ in itertools.zip_longest(
                _gla_bwd_chunks(do_ref, sd_ref, dst, b_scr, db_scr, dq_ref, dk_ref, dv_ref, qt, kt, e, ei, v_ref[...],
                                False, nb), conv_branch()):
            pass
        dpre, dlr, dw = _gate_bwd(db_scr[...], pre, lr, wp, False, tb)
        dlr_ref[...] = dlr
        dw_ref[...] += dw
        _acc_rows(dbias_ref, jnp.sum(dpre, axis=0, keepdims=True))

    full = lambda i: (0, 0)
    tokv = pl.BlockSpec((tb, GV), lambda i: (jmap(i), 0))
    tokk = pl.BlockSpec((tb, GK), lambda i: (jmap(i), 0))
    ccp, ccn = _halo_specs(tb, nblk, t, ZB_CC, jmap)
    cup, cun = _halo_specs(tb, nblk, t, ZB_CU, jmap)
    in_specs = [_zspec(tb, GK, ZB_Q, jmap), _zspec(tb, GK, ZB_K, jmap), _zspec(tb, GV, ZB_V, jmap),
                _zspec(tb, 128, ZB_LR, jmap), _zspec(tb, GV, ZB_G, jmap), _zspec(tb, CW, ZB_CB, jmap),
                _zspec(tb, CW, ZB_CC, jmap), _zspec(tb, CW, ZB_CU, jmap), ccp, ccn, cup, cun,
                pl.BlockSpec((tb, D), lambda i: (jmap(i), 0)), tokv,
                pl.BlockSpec((nb, 128, GK), lambda i: (jmap(i), 0, 0)), pl.BlockSpec((128, GK), full),
                pl.BlockSpec((1, GK), full), pl.BlockSpec((3, CW), full), pl.BlockSpec((1, CW), full),
                pl.BlockSpec((1, GV), full)]
    out_specs = [tokv, tokk, tokk, tokv, pl.BlockSpec((tb, 128), lambda i: (jmap(i), 0)), tokv, tokv, tokv,
                 pl.BlockSpec((128, GK), full), pl.BlockSpec((8, GK), full), pl.BlockSpec((8, CW), full),
                 pl.BlockSpec((8, CW), full), pl.BlockSpec((8, GV), full)]
    out_shape = [jax.ShapeDtypeStruct((t, GV), F32), jax.ShapeDtypeStruct((t, GK), F32),
                 jax.ShapeDtypeStruct((t, GK), F32), jax.ShapeDtypeStruct((t, GV), F32),
                 jax.ShapeDtypeStruct((t, 128), F32), jax.ShapeDtypeStruct((t, GV), BF16),
                 jax.ShapeDtypeStruct((t, CW), BF16), jax.ShapeDtypeStruct((t, CW), F32),
                 jax.ShapeDtypeStruct((128, GK), F32), jax.ShapeDtypeStruct((8, GK), F32),
                 jax.ShapeDtypeStruct((8, CW), F32), jax.ShapeDtypeStruct((8, CW), F32),
                 jax.ShapeDtypeStruct((8, GV), F32)]
    return _call(
        body, "gla_bwd_first", (nblk,), in_specs, out_specs, out_shape,
        [pltpu.VMEM((128, GK), F32), pltpu.VMEM((tb, GK), F32), pltpu.VMEM((tb, GK), F32)],
        (z, z, z, z, z, z, z, z, z, z, z, z, dy, o_pre, sd, wpad, bias, conv_w, conv_norm, gla_norm4), riders)


def _gla_bwd_second(z, do, sd, wpad, bias, dqa, dka, dva, dlra, dzg, dzcb, dconv, conv_w, riders=()):
    t = z.shape[0]
    tb = min(TB_BWD, t)
    nblk, nb = t // tb, tb // CH
    jmap = lambda i: i

    def body(q_ref, k_ref, v_ref, lr_ref, cc_ref, cu_ref, do_ref, sd_ref, w_ref, bias_ref, dqa_ref, dka_ref,
             dva_ref, dlra_ref, dzg_ref, dzcb_ref, dc_ref, dcp_ref, dcn_ref, cw_ref,
             dz_ref, dw_ref, dbias_ref, dst, b_scr, db_scr, dq_scr, dk_scr, dv_scr, sb_scr, dsk_scr):
        i = pl.program_id(0)

        @pl.when(i == 0)
        def _():
            dst[...] = jnp.zeros_like(dst)
            dw_ref[...] = jnp.zeros_like(dw_ref)
            dbias_ref[...] = jnp.zeros_like(dbias_ref)

        q_raw, k, v, lr, wp = q_ref[...], k_ref[...], v_ref[...], lr_ref[...], w_ref[...]
        pre, b, e, ei, qt, kt = _gla_recompute(q_raw, k, lr, wp, bias_ref[...], True, tb)
        b_scr[...] = b

        def token_local():
            dc = dc_ref[...]
            rows = _iota(dc.shape, 0)
            dprev = jnp.where(i == 0, 0.0, dcp_ref[pl.ds(7, 1), :])
            dnext = jnp.where(i == nblk - 1, 0.0, dcn_ref[pl.ds(0, 1), :])
            dc_m1 = jnp.where(rows == 0, dprev, pltpu.roll(dc, 1, 0))
            dc_p1 = jnp.where(rows == tb - 1, dnext, pltpu.roll(dc, tb - 1, 0))
            yield
            dh = cw_ref[pl.ds(0, 1), :] * dc_p1 + cw_ref[pl.ds(1, 1), :] * dc + cw_ref[pl.ds(2, 1), :] * dc_m1
            dz_ref[:, 0:512] = dzcb_ref[...]
            yield
            dz_ref[:, 512:1024] = (dh * cu_ref[...]).astype(BF16)
            dz_ref[:, 1024:1536] = (dh * cc_ref[...]).astype(BF16)
            dz_ref[:, 2560:3072] = dzg_ref[...]
            yield
            sb_scr[...] = _head_sum((q_raw * 0.125) * k, 64, 128)
            yield
            dsk_scr[...] = _head_sum(do_ref[...] * v, 128, 64)
            yield

        for _ in itertools.zip_longest(
                _gla_bwd_chunks(do_ref, sd_ref, dst, b_scr, db_scr, dq_scr, dk_scr, dv_scr, qt, kt, e, ei, v, True, nb),
                token_local()):
            pass
        dpre, dlr, dw = _gate_bwd(db_scr[...], pre, lr, wp, True, tb)
        dw_ref[...] += dw
        _acc_rows(dbias_ref, jnp.sum(dpre, axis=0, keepdims=True))
        dsk = dsk_scr[...]
        dz_ref[:, 1536:1792] = (dqa_ref[...] + dq_scr[...] - dsk * k * 0.125).astype(BF16)
        dz_ref[:, 1792:2048] = (dka_ref[...] + dk_scr[...] - dsk * (q_raw * 0.125)).astype(BF16)
        dz_ref[:, 2048:2560] = (dva_ref[...] + dv_scr[...] - sb_scr[...] * do_ref[...]).astype(BF16)
        dz_ref[:, 3072:3200] = (dlra_ref[...] + dlr).astype(BF16)

    full = lambda i: (0, 0)
    tokv = pl.BlockSpec((tb, GV), lambda i: (i, 0))
    tokk = pl.BlockSpec((tb, GK), lambda i: (i, 0))
    dcp = pl.BlockSpec((8, CW), lambda i: (jnp.maximum(i * (tb // 8) - 1, 0), 0))
    dcn = pl.BlockSpec((8, CW), lambda i: (jnp.minimum((i + 1) * (tb // 8), t // 8 - 1), 0))
    in_specs = [_zspec(tb, GK, ZB_Q, jmap), _zspec(tb, GK, ZB_K, jmap), _zspec(tb, GV, ZB_V, jmap),
                _zspec(tb, 128, ZB_LR, jmap), _zspec(tb, CW, ZB_CC, jmap), _zspec(tb, CW, ZB_CU, jmap), tokv,
                pl.BlockSpec((nb, 128, GK), lambda i: (i, 0, 0)), pl.BlockSpec((128, GK), full),
                pl.BlockSpec((1, GK), full), tokk, tokk, tokv, pl.BlockSpec((tb, 128), lambda i: (i, 0)), tokv, tokv,
                tokv, dcp, dcn, pl.BlockSpec((3, CW), full)]
    out_specs = [pl.BlockSpec((tb, ZC), lambda i: (i, 0)), pl.BlockSpec((128, GK), full), pl.BlockSpec((8, GK), full)]
    out_shape = [jax.ShapeDtypeStruct((t, ZC), BF16), jax.ShapeDtypeStruct((128, GK), F32),
                 jax.ShapeDtypeStruct((8, GK), F32)]
    return _call(
        body, "gla_bwd_second", (nblk,), in_specs, out_specs, out_shape,
        [pltpu.VMEM((128, GK), F32), pltpu.VMEM((tb, GK), F32), pltpu.VMEM((tb, GK), F32),
         pltpu.VMEM((tb, GK), F32), pltpu.VMEM((tb, GK), F32), pltpu.VMEM((tb, GV), F32),
         pltpu.VMEM((tb, GV), F32), pltpu.VMEM((tb, GK), F32)],
        (z, z, z, z, z, z, do, sd, wpad, bias, dqa, dka, dva, dlra, dzg, dzcb, dconv, dconv, dconv, conv_w), riders)


def _step(x, mem, target, shard, small_pack, vec, place):
    own, from_chips = {}, {}

    def pair_sums(names, g4, from_sibling):
        pbs = {}
        for shape in dict.fromkeys(g.shape for g in g4):
            idx = [i for i, g in enumerate(g4) if g.shape == shape]
            pb, mine = _rs_pair_sum(place, [g4[i] for i in idx], [from_sibling[i] for i in idx],
                                    "pair_sum_" + "_".join(names[i] for i in idx))
            for i, b, o in zip(idx, pb, mine):
                pbs[i], own[names[i]] = b, o
        return [pbs[i] for i in range(len(g4))]

    def by_dest(g, n):
        return g.reshape((4, 2) + shard[n].shape)

    w_in, small_all = _exchange(_gather_rider([shard["w_in"], small_pack]), "gather_w_in")
    w_in = w_in.reshape(ZW, D)
    small_all = small_all.reshape(NDEV, -1)
    p, off = {}, 0
    for n, (r, c) in SMALL_SHARDED.items():
        p[n] = small_all[:, off:off + r * c].reshape(NDEV, r, c).transpose(1, 0, 2).reshape(r, NDEV * c)
        off += r * c
    zeros_lr = jnp.zeros((128 - LR, GK), BF16)
    waf_pad = jnp.concatenate([p["w_af"].astype(BF16), zeros_lr], axis=0)
    wab_pad = jnp.concatenate([jnp.zeros((LR, GK), BF16), p["w_ab"].astype(BF16), zeros_lr[:128 - 2 * LR]], axis=0)
    gla_norm4 = jnp.tile(vec["gla_norm"], (1, NH))

    z, hb, w_out, w_xq, w_xo, w_xkv = _inproj(
        x, vec["mix_norm"], w_in, [_gather_rider([shard[n] for n in ("w_out", "w_xq", "w_xo", "w_xkv")])])
    w_out, w_xq, w_xo = [a.reshape(D, D) for a in (w_out, w_xq, w_xo)]
    o_f, sd_f, o_b, sd_b, w_up_t, w_down = _gla_fwd(
        z, waf_pad, vec["b_af"], wab_pad, vec["b_ab"], [_gather_rider([shard["w_up"], shard["w_down"]])])
    w_up_t, w_down = w_up_t.reshape(FF, D), w_down.reshape(FF, D)
    kv, memn = _kv_proj(mem, vec["mem_norm"], w_xkv)
    kb, vb = kv[:, :D].astype(BF16), kv[:, D:].astype(BF16)
    x1, x2, xn1, qb, attb, yb, o_pre = _attn_fwd(x, z, o_f, o_b, p["conv_w"], vec["conv_norm"], gla_norm4, w_out,
                                                 vec["xa_norm"], w_xq, kb, vb, w_xo)
    h1b, xn2, dx3, dx3b, loss8, dfinal = _mlp_fwd(x2, vec["mlp_norm"], w_up_t, w_down, vec["final_norm"], target)

    ab, dh1b, dx2, dx2b, dmlp = _mlp_bwd(dx3, dx3b, h1b, w_down, w_up_t, x2, vec["mlp_norm"])
    g_mlp = [by_dest(_matmul_tn(ab, dx3b, "dw_down")[0], "w_down"),
             by_dest(_matmul_tn(dh1b, xn2, "dw_up")[0], "w_up")]
    dx1, dx1b, dy, dqb, dkv, dxa, *s_mlp = _attn_bwd(x1, dx2, dx2b, qb, kb, vb, w_xo, w_xq, w_out, vec["xa_norm"],
                                                     riders=[_sibling_rider(g_mlp)])
    pb_mlp = pair_sums(("w_down", "w_up"), g_mlp, s_mlp)
    dw_xo = _matmul_tn(attb, dx2b, "dw_xo")[0]
    dw_xkv, dmemn = _kv_bwd(dkv, memn, mem, vec["mem_norm"], w_xkv)
    att_names = ("w_xo", "w_xq", "w_out", "w_xkv")
    g_att = [by_dest(g, n) for g, n in zip(
        (dw_xo, _matmul_tn(xn1, dqb, "dw_xq")[0], _matmul_tn(yb, dx1b, "dw_out")[0], dw_xkv), att_names)]
    res = _gla_bwd_first(z, dy, o_pre, sd_f, waf_pad, vec["b_af"], p["conv_w"], vec["conv_norm"], gla_norm4,
                         riders=[_chips_rider(pb_mlp), _sibling_rider(g_att)])
    do, dqa, dka, dva, dlra, dzg, dzcb, dconv, dwaf, dbaf, dcw, dcn, dgn = res[:13]
    from_chips["w_down"], from_chips["w_up"] = res[13:15]
    pb_att = pair_sums(att_names, g_att, res[15:])
    dz, dwab, dbab, *c_att = _gla_bwd_second(z, do, sd_b, wab_pad, vec["b_ab"], dqa, dka, dva, dlra, dzg, dzcb, dconv,
                                             p["conv_w"], riders=[_chips_rider(pb_att)])
    from_chips.update(zip(att_names, c_att))
    g_in = [by_dest(_matmul_tn(dz, hb, "dw_in", rows=ZW)[0], "w_in")]
    pb_in = pair_sums(("w_in",), g_in, _exchange(_sibling_rider(g_in), "grads_to_sibling_w_in"))
    grad_x, dmix, from_chips["w_in"] = _inproj_bwd(dz, w_in, x, dx1, vec["mix_norm"], riders=[_chips_rider(pb_in)])

    small_acc = dict(mix_norm=dmix, conv_w=dcw, conv_norm=dcn, w_af=dwaf, b_af=dbaf, w_ab=dwab, b_ab=dbab,
                     gla_norm=dgn, xa_norm=dxa, mem_norm=dmemn, mlp_norm=dmlp, final_norm=dfinal)
    return loss8, grad_x, small_acc, own, from_chips


def _place():
    return lax.axis_index("x"), lax.axis_index("y"), lax.axis_index("c")


class _Rider:
    def __init__(self, arrays, out_shape, scratch, start, finish, forward=None, relay=None):
        self.arrays, self.out_shape, self.scratch, self.start, self.finish = arrays, out_shape, scratch, start, finish
        self.forward, self.relay = forward, relay


def _gather_rider(blks, early_relay=True):
    n = len(blks)

    def plan(in_refs, out_refs, sems):
        send_sems, recv_sems, local_sems = sems
        x, y, c = _place()
        me, sibling = (x, y, c), (x, y, 1 - c)
        xn, yn, dg = (1 - x, y, c), (x, 1 - y, c), (1 - x, 1 - y, c)
        via = (x + (1 - c) * (1 - 2 * x), y + c * (1 - 2 * y), c)
        onto = (x + c * (1 - 2 * x), y + (1 - c) * (1 - 2 * y), c)
        other = onto

        def copy(a, k, block, to, own=False):
            px, py, pc = block
            dst = out_refs[a].at[4 * px + 2 * py + pc]
            return pltpu.make_async_remote_copy(
                src_ref=in_refs[a] if own else dst, dst_ref=dst, send_sem=send_sems.at[k, a],
                recv_sem=recv_sems.at[k, a], device_id=to, device_id_type=MESH)

        def local(a):
            return pltpu.make_async_copy(in_refs[a], out_refs[a].at[4 * x + 2 * y + c], local_sems.at[a])

        def sends(a):
            return ([copy(a, 0, me, sibling, own=True), copy(a, 1, me, xn, own=True), copy(a, 2, me, yn, own=True),
                     copy(a, 3, via, onto), copy(a, 4 + c, via, sibling), copy(a, 5 - c, other, sibling),
                     copy(a, 6, dg, sibling)])

        return copy, local, sends, me, sibling, (xn, yn, dg), via, other

    def start(in_refs, out_refs, sems):
        _, local, sends, _, _, _, _, _ = plan(in_refs, out_refs, sems)
        for a in range(n):
            local(a).start()
            for cp in sends(a)[:3]:
                cp.start()

    def forward(in_refs, out_refs, sems):
        copy, _, sends, me, _, _, via, _ = plan(in_refs, out_refs, sems)
        for a in range(n):
            copy(a, 1 + me[2], via, me).wait_recv()
            sends(a)[3].start()
            sends(a)[4].start()

    def relay(in_refs, out_refs, sems):
        copy, _, sends, me, _, (_, _, dg), _, other = plan(in_refs, out_refs, sems)
        for a in range(n):
            copy(a, 2 - me[2], other, me).wait_recv()
            sends(a)[5].start()
        for a in range(n):
            copy(a, 3, dg, me).wait_recv()
            sends(a)[6].start()

    def finish(in_refs, out_refs, sems):
        if not early_relay:
            forward(in_refs, out_refs, sems)
            relay(in_refs, out_refs, sems)
        copy, local, sends, me, sibling, chips, _, _ = plan(in_refs, out_refs, sems)
        for a in range(n):
            copy(a, 0, sibling, me).wait_recv()
            for j, (px, py, pc) in enumerate(chips):
                copy(a, 4 + j, (px, py, 1 - pc), me).wait_recv()
            for cp in sends(a):
                cp.wait_send()
            local(a).wait()

    return _Rider(blks, [jax.ShapeDtypeStruct((NDEV,) + b.shape, b.dtype) for b in blks],
                  [pltpu.SemaphoreType.DMA((7, n)), pltpu.SemaphoreType.DMA((7, n)), pltpu.SemaphoreType.DMA((n,))],
                  start, finish, forward if early_relay else None, relay if early_relay else None)


def _sibling_rider(g4s):
    n = len(g4s)

    def copies(in_refs, out_refs, sems):
        send_sems, recv_sems = sems
        x, y, c = _place()
        return [pltpu.make_async_remote_copy(
            src_ref=in_refs[a].at[k, 1 - c], dst_ref=out_refs[a].at[k], send_sem=send_sems.at[k, a],
            recv_sem=recv_sems.at[k, a], device_id=(x, y, 1 - c), device_id_type=MESH)
            for a in range(n) for k in range(4)]

    def start(in_refs, out_refs, sems):
        for cp in copies(in_refs, out_refs, sems):
            cp.start()

    def finish(in_refs, out_refs, sems):
        for cp in copies(in_refs, out_refs, sems):
            cp.wait()

    return _Rider(g4s, [jax.ShapeDtypeStruct((4,) + g.shape[2:], g.dtype) for g in g4s],
                  [pltpu.SemaphoreType.DMA((4, n)), pltpu.SemaphoreType.DMA((4, n))], start, finish)


def _chips_rider(pbs):
    n = len(pbs)

    def copies(in_refs, out_refs, sems):
        send_sems, recv_sems = sems
        x, y, c = _place()
        peers = [(1 - x, y), (x, 1 - y), (1 - x, 1 - y)]
        return [pltpu.make_async_remote_copy(
            src_ref=in_refs[a].at[2 * px + py], dst_ref=out_refs[a].at[k], send_sem=send_sems.at[k, a],
            recv_sem=recv_sems.at[k, a], device_id=(px, py, c), device_id_type=MESH)
            for a in range(n) for k, (px, py) in enumerate(peers)]

    def start(in_refs, out_refs, sems):
        for cp in copies(in_refs, out_refs, sems):
            cp.start()

    def finish(in_refs, out_refs, sems):
        for cp in copies(in_refs, out_refs, sems):
            cp.wait()

    return _Rider(pbs, [jax.ShapeDtypeStruct((3,) + p.shape[1:], p.dtype) for p in pbs],
                  [pltpu.SemaphoreType.DMA((3, n)), pltpu.SemaphoreType.DMA((3, n))], start, finish)


def _exchange(rider, name):
    n_in, n_out = len(rider.arrays), len(rider.out_shape)

    def body(*refs):
        ins, outs, sems = refs[:n_in], refs[n_in:n_in + n_out], refs[n_in + n_out:]
        rider.start(ins, outs, sems)
        for hook in (rider.forward, rider.relay):
            if hook:
                hook(ins, outs, sems)
        rider.finish(ins, outs, sems)

    hbm = pl.BlockSpec(memory_space=pltpu.HBM)
    return pl.pallas_call(body, name=name, out_shape=rider.out_shape, in_specs=[hbm] * n_in,
                          out_specs=[hbm] * n_out, scratch_shapes=rider.scratch)(*rider.arrays)


def _rs_pair_sum(place, g4s, r1s, name):
    n = len(g4s)
    rows, cols = g4s[0].shape[2:]
    tr = min(rows, 512)

    def body(pl_ref, *refs):
        for g_ref, r_ref, pb_ref, own_ref in zip(refs[:n], refs[n:2 * n], refs[2 * n:3 * n], refs[3 * n:]):
            s = g_ref[0, 0] + r_ref[0]
            pb_ref[0] = s.astype(BF16)

            @pl.when(pl.program_id(1) == pl_ref[0])
            def _():
                own_ref[...] = s

    grid_spec = pltpu.PrefetchScalarGridSpec(
        num_scalar_prefetch=1, grid=(rows // tr, 4),
        in_specs=[pl.BlockSpec((1, 1, tr, cols), lambda r, k, p: (k, p[1], r, 0))] * n
        + [pl.BlockSpec((1, tr, cols), lambda r, k, p: (k, r, 0))] * n,
        out_specs=[pl.BlockSpec((1, tr, cols), lambda r, k, p: (k, r, 0))] * n
        + [pl.BlockSpec((tr, cols), lambda r, k, p: (r, 0))] * n)
    res = pl.pallas_call(
        body, name=name, grid_spec=grid_spec,
        out_shape=[jax.ShapeDtypeStruct((4, rows, cols), BF16)] * n + [jax.ShapeDtypeStruct((rows, cols), F32)] * n,
        compiler_params=_cparams(("arbitrary", "arbitrary")))(place, *g4s, *r1s)
    return res[:n], res[n:]


PACK_ROWS = 32
VEC_ROW = {"mix_norm": 0, "conv_norm": 1, "b_af": 2, "b_ab": 3, "gla_norm": 4, "xa_norm": 5, "mem_norm": 6,
           "mlp_norm": 7, "final_norm": 8}
LOSS_ROW, MAT_ROW = 9, 16
MAT_LANE = {"w_af": 0, "w_ab": GK, "conv_w": 2 * GK}
MAT_SRC_ROW = {"w_af": 0, "w_ab": LR, "conv_w": 0}


SMALL_WIDTH = {"mix_norm": D, "conv_w": 64, "conv_norm": CW, "w_af": 32, "b_af": GK, "w_ab": 32, "b_ab": GK,
               "gla_norm": 128, "xa_norm": D, "mem_norm": D, "mlp_norm": D, "final_norm": D}


def _small_reduce(acc, loss8):
    names = list(SMALL)
    n = len(names)
    widths = SMALL_WIDTH

    def body(*refs):
        acc_refs = dict(zip(names, refs[:n]))
        loss_ref, tot = refs[n], refs[n + 1]
        pk, all_ref, send_sems, recv_sems, local_sem = refs[n + 2:]

        pk[...] = jnp.zeros_like(pk)
        for k, row in VEC_ROW.items():
            if k == "gla_norm":
                g = functools.reduce(lambda a, b: a + b, [acc_refs[k][pl.ds(0, 1), pl.ds(h * 128, 128)]
                                                          for h in range(NH)])
            else:
                g = acc_refs[k][pl.ds(0, 1), :]
            pk[pl.ds(row, 1), pl.ds(0, widths[k])] = g
        pk[pl.ds(LOSS_ROW, 1), pl.ds(0, 128)] = loss_ref[pl.ds(0, 1), :]
        for k, lane in MAT_LANE.items():
            rows, cols = (3, CW) if k == "conv_w" else (LR, GK)
            pk[pl.ds(MAT_ROW, rows), pl.ds(lane, cols)] = acc_refs[k][pl.ds(MAT_SRC_ROW[k], rows), :]

        x, y, c = _place()
        me, sibling = (x, y, c), (x, y, 1 - c)
        chips = [(1 - x, y, c), (x, 1 - y, c), (1 - x, 1 - y, c)]

        def copy(k, block, to, own=False):
            px, py, pc = block
            dst = all_ref.at[4 * px + 2 * py + pc]
            return pltpu.make_async_remote_copy(
                src_ref=pk if own else dst, dst_ref=dst, send_sem=send_sems.at[k], recv_sem=recv_sems.at[k],
                device_id=to, device_id_type=MESH)

        mine = pltpu.make_async_copy(pk, all_ref.at[4 * x + 2 * y + c], local_sem)
        mine.start()
        first = [copy(0, me, sibling, own=True)] + [copy(1 + j, me, chip, own=True) for j, chip in enumerate(chips)]
        for cp in first:
            cp.start()
        passed = [copy(4 + j, chip, sibling) for j, chip in enumerate(chips)]
        for j, chip in enumerate(chips):
            copy(1 + j, chip, me).wait_recv()
            passed[j].start()
        copy(0, sibling, me).wait_recv()
        for j, (px, py, pc) in enumerate(chips):
            copy(4 + j, (px, py, 1 - pc), me).wait_recv()
        for cp in first + passed:
            cp.wait_send()
        mine.wait()
        total = all_ref[0]
        for d in range(1, NDEV):
            total = total + all_ref[d]
        tot[...] = total

    return pl.pallas_call(
        body, name="small_reduce", out_shape=jax.ShapeDtypeStruct((PACK_ROWS, D), F32),
        scratch_shapes=[pltpu.VMEM((PACK_ROWS, D), F32), pltpu.VMEM((NDEV, PACK_ROWS, D), F32),
                        pltpu.SemaphoreType.DMA((7,)), pltpu.SemaphoreType.DMA((7,)), pltpu.SemaphoreType.DMA],
    )(*[acc[k] for k in names], loss8)


def _small_adamw(tot, ws, ms, vs):
    names = list(SMALL)
    n = len(names)
    widths = SMALL_WIDTH

    def body(*refs):
        tot = refs[0]
        w_refs, m_refs, v_refs = [dict(zip(names, refs[1 + q * n:1 + (q + 1) * n])) for q in range(3)]
        outs = refs[1 + 3 * n:1 + 7 * n]
        g_out, d_out, m_out, v_out = [dict(zip(names, outs[q * n:(q + 1) * n])) for q in range(4)]
        cut = refs[1 + 7 * n]
        x, y, c = _place()
        dev = 4 * x + 2 * y + c
        for k in names:
            if k in VEC_ROW:
                g = tot[pl.ds(VEC_ROW[k], 1), pl.ds(0, widths[k])]
            else:
                rows, cols = (3, CW) if k == "conv_w" else (LR, GK)
                wd = widths[k]
                sel = jnp.where(_iota((cols, wd), 0) == dev * wd + _iota((cols, wd), 1), 1.0, 0.0).astype(BF16)
                cut[:, pl.ds(0, wd)] = _dot_exact_rhs(tot[pl.ds(MAT_ROW, LR), pl.ds(MAT_LANE[k], cols)], sel, 3)
                g = cut[pl.ds(0, rows), pl.ds(0, wd)]
            g_out[k][...] = g
            d_out[k][...], m_out[k][...], v_out[k][...] = _adamw_math(w_refs[k][...], g, m_refs[k][...],
                                                                       v_refs[k][...])

    shapes = [jax.ShapeDtypeStruct(ws[k].shape, F32) for k in names]
    res = pl.pallas_call(
        body, name="small_adamw", out_shape=shapes * 4, scratch_shapes=[pltpu.VMEM((LR, 128), F32)],
    )(tot, *[ws[k] for k in names], *[ms[k] for k in names], *[vs[k] for k in names])
    return {k: tuple(res[q * n + i] for q in range(4)) for i, k in enumerate(names)}


def _adamw_math(w, g, m, v):
    m = ADAM_B1 * m + (1.0 - ADAM_B1) * g
    v = ADAM_B2 * v + (1.0 - ADAM_B2) * (g * g)
    m_hat = m / (1.0 - ADAM_B1 ** ADAM_STEP)
    v_hat = v / (1.0 - ADAM_B2 ** ADAM_STEP)
    delta = -ADAM_LR * (m_hat / (jnp.sqrt(v_hat) + ADAM_EPS) + ADAM_WD * w)
    return delta, m, v


def _adamw(ws, ms, vs, owns, r2s, name, grads_transposed=False):
    n = len(ws)
    _, r, c = ws[0].shape
    tr = 256 if r % 256 == 0 else r

    def body(*refs):
        ins, outs = refs[:5 * n], refs[5 * n:]
        for q in range(n):
            w_ref, m_ref, v_ref, o_ref, r_ref = [ins[k * n + q] for k in range(5)]
            g_ref, d_ref, nm_ref, nv_ref = [outs[k * n + q] for k in range(4)]
            g = ((o_ref[...] + r_ref[0].astype(F32)) + r_ref[1].astype(F32)) + r_ref[2].astype(F32)
            g = g.T if grads_transposed else g
            g_ref[...] = g
            d_ref[...], nm_ref[...], nv_ref[...] = _adamw_math(w_ref[...], g, m_ref[...], v_ref[...])

    spec = pl.BlockSpec((None, tr, c), lambda i: (0, i, 0))
    if grads_transposed:
        own_spec, r2_spec = pl.BlockSpec((c, tr), lambda i: (0, i)), pl.BlockSpec((3, c, tr), lambda i: (0, 0, i))
    else:
        own_spec, r2_spec = pl.BlockSpec((tr, c), lambda i: (i, 0)), pl.BlockSpec((3, tr, c), lambda i: (0, i, 0))
    res = pl.pallas_call(
        body, name=name, grid=(r // tr,),
        in_specs=[spec] * (3 * n) + [own_spec] * n + [r2_spec] * n,
        out_specs=[spec] * (4 * n), out_shape=[jax.ShapeDtypeStruct((1, r, c), F32)] * (4 * n),
        compiler_params=_cparams(("arbitrary",)))(*ws, *ms, *vs, *owns, *r2s)
    return [tuple(res[k * n + q] for k in range(4)) for q in range(n)]


MATS = ("w_in", "w_out", "w_xq", "w_xo", "w_xkv", "w_up", "w_down")
SMALL = ("mix_norm", "conv_w", "conv_norm", "w_af", "b_af", "w_ab", "b_ab", "gla_norm", "xa_norm", "mem_norm",
         "mlp_norm", "final_norm")
WEIGHTS = ("mix_norm", "w_in", "conv_w", "conv_norm", "w_af", "b_af", "w_ab", "b_ab", "gla_norm", "w_out", "xa_norm",
           "mem_norm", "w_xq", "w_xkv", "w_xo", "mlp_norm", "w_up", "w_down", "final_norm")
SMALL_SHARDED = {"conv_w": (3, 64), "w_af": (LR, 32), "w_ab": (LR, 32)}
SMALL_PACK_ROWS = 16


def kernel(x, mem, mix_norm, w_in, conv_w, conv_norm, w_af, b_af, w_ab, b_ab, gla_norm, w_out, xa_norm, mem_norm, w_xq, w_xkv, w_xo, mlp_norm, w_up, w_down, final_norm, loss_target, m_mix_norm, m_w_in, m_conv_w, m_conv_norm, m_w_af, m_b_af, m_w_ab, m_b_ab, m_gla_norm, m_w_out, m_xa_norm, m_mem_norm, m_w_xq, m_w_xkv, m_w_xo, m_mlp_norm, m_w_up, m_w_down, m_final_norm, v_mix_norm, v_w_in, v_conv_w, v_conv_norm, v_w_af, v_b_af, v_w_ab, v_b_ab, v_gla_norm, v_w_out, v_xa_norm, v_mem_norm, v_w_xq, v_w_xkv, v_w_xo, v_mlp_norm, v_w_up, v_w_down, v_final_norm):
    w = dict(mix_norm=mix_norm, w_in=w_in, conv_w=conv_w, conv_norm=conv_norm, w_af=w_af, b_af=b_af, w_ab=w_ab,
             b_ab=b_ab, gla_norm=gla_norm, w_out=w_out, xa_norm=xa_norm, mem_norm=mem_norm, w_xq=w_xq, w_xkv=w_xkv,
             w_xo=w_xo, mlp_norm=mlp_norm, w_up=w_up, w_down=w_down, final_norm=final_norm)
    mom = dict(mix_norm=m_mix_norm, w_in=m_w_in, conv_w=m_conv_w, conv_norm=m_conv_norm, w_af=m_w_af, b_af=m_b_af,
               w_ab=m_w_ab, b_ab=m_b_ab, gla_norm=m_gla_norm, w_out=m_w_out, xa_norm=m_xa_norm, mem_norm=m_mem_norm,
               w_xq=m_w_xq, w_xkv=m_w_xkv, w_xo=m_w_xo, mlp_norm=m_mlp_norm, w_up=m_w_up, w_down=m_w_down,
               final_norm=m_final_norm)
    var = dict(mix_norm=v_mix_norm, w_in=v_w_in, conv_w=v_conv_w, conv_norm=v_conv_norm, w_af=v_w_af, b_af=v_b_af,
               w_ab=v_w_ab, b_ab=v_b_ab, gla_norm=v_gla_norm, w_out=v_w_out, xa_norm=v_xa_norm, mem_norm=v_mem_norm,
               w_xq=v_w_xq, w_xkv=v_w_xkv, w_xo=v_w_xo, mlp_norm=v_mlp_norm, w_up=v_w_up, w_down=v_w_down,
               final_norm=v_final_norm)
    xi, yi, ci = _place()
    two_d = lambda a: a.reshape(a.shape[-2:]) if a.ndim == 3 else a.reshape(1, a.shape[-1])

    small = jnp.concatenate([w[n].reshape(-1) for n in SMALL_SHARDED])
    small = jnp.pad(small, (0, SMALL_PACK_ROWS * 128 - small.shape[0])).reshape(SMALL_PACK_ROWS, 128)
    shard = {n: two_d(w[n]).astype(BF16) for n in MATS}
    for n in ("w_in", "w_up"):
        shard[n] = shard[n].T
    vec = {n: two_d(w[n]) for n in SMALL if n not in SMALL_SHARDED}
    place = jnp.stack([2 * xi + yi, ci]).astype(jnp.int32)
    loss8, grad_x, small_acc, own, from_chips = _step(x[0], mem[0], loss_target[0], shard, small, vec, place)

    tot = _small_reduce(small_acc, loss8)
    small_out = _small_adamw(tot, *[{n: two_d(d[n]) for n in SMALL} for d in (w, mom, var)])
    loss = tot[LOSS_ROW, 0]

    out_g, out_d, out_m, out_v = {}, {}, {}, {}
    wmv = {n: [a.transpose(0, 2, 1) if n == "w_in" else a for a in (w[n], mom[n], var[n])] for n in MATS}
    for shape in dict.fromkeys(wmv[n][0].shape for n in MATS):
        names = [n for n in MATS if wmv[n][0].shape == shape]
        res = _adamw(*[[wmv[n][k] for n in names] for k in range(3)], [own[n] for n in names],
                     [from_chips[n] for n in names], "adamw_" + "_".join(names), grads_transposed=names == ["w_up"])
        for n, r in zip(names, res):
            out_g[n], out_d[n], out_m[n], out_v[n] = [a.transpose(0, 2, 1) for a in r] if n == "w_in" else r
    for n in SMALL:
        out_g[n], out_d[n], out_m[n], out_v[n] = [a.reshape(w[n].shape) for a in small_out[n]]

    return (loss, grad_x[None], *[out_g[n] for n in WEIGHTS], *[out_d[n] for n in WEIGHTS],
            *[out_m[n] for n in WEIGHTS], *[out_v[n] for n in WEIGHTS])
```

```python
import functools
import itertools

import jax
import jax.numpy as jnp
from jax import lax
from jax.experimental import pallas as pl
from jax.experimental.pallas import tpu as pltpu

F32 = jnp.float32
BF16 = jnp.bfloat16

D = 1024
CW = 512
GK = 256
GV = 512
NH = 4
CH = 64
LR = 16
NMEM = 256
XD = 256
FF = 4096
ZW = 3104
ZC = 3200
EPS = 1e-6
NDEV = 8

ZB_CB, ZB_CC, ZB_CU, ZB_V, ZB_G = 0, 1, 2, 4, 5
ZB_Q, ZB_K = 6, 7
ZB_LR = 24

TM = 512
TM_MLP = 256
TM_MLP_FWD = 512
TF = 512
TB = 512
TB_BWD = 512
TT = 2048
VMEM_LIMIT = 56 * 1024 * 1024

ADAM_LR, ADAM_B1, ADAM_B2, ADAM_EPS, ADAM_WD, ADAM_STEP = 0.001, 0.9, 0.999, 1e-08, 0.01, 10

XKV_SHARD = 2 * D // NDEV

MESH = pl.DeviceIdType.MESH


def _cparams(sem):
    return pltpu.CompilerParams(dimension_semantics=sem, vmem_limit_bytes=VMEM_LIMIT)


def _call(body, name, grid, in_specs, out_specs, out_shape, scratch, args, riders=(), aliases=None):
    n_in, n_out, n_scr = len(in_specs), len(out_specs), len(scratch)
    counts = [(len(r.arrays), len(r.out_shape), len(r.scratch)) for r in riders]

    def take(refs, pos, sizes):
        groups = []
        for size in sizes:
            groups.append(refs[pos:pos + size])
            pos += size
        return groups, pos

    def wrapped(*refs):
        ins, pos = refs[:n_in], n_in
        r_ins, pos = take(refs, pos, [c[0] for c in counts])
        outs, pos = refs[pos:pos + n_out], pos + n_out
        r_outs, pos = take(refs, pos, [c[1] for c in counts])
        scr, pos = refs[pos:pos + n_scr], pos + n_scr
        r_scr, pos = take(refs, pos, [c[2] for c in counts])
        ids = [pl.program_id(d) for d in range(len(grid))]
        first = functools.reduce(lambda a, b: a & b, [i == 0 for i in ids])
        last = functools.reduce(lambda a, b: a & b, [i == g - 1 for i, g in zip(ids, grid)])

        @pl.when(first)
        def _():
            for r, a, b, c in zip(riders, r_ins, r_outs, r_scr):
                r.start(a, b, c)

        body(*ins, *outs, *scr)

        for hook, at in (("forward", [g // 2 for g in grid]), ("relay", [max(g - 2, 0) for g in grid])):
            if any(getattr(r, hook) for r in riders):
                @pl.when(functools.reduce(lambda a, b: a & b, [i == s for i, s in zip(ids, at)]))
                def _(hook=hook):
                    for r, a, b, c in zip(riders, r_ins, r_outs, r_scr):
                        if getattr(r, hook):
                            getattr(r, hook)(a, b, c)

        @pl.when(last)
        def _():
            for r, a, b, c in zip(riders, r_ins, r_outs, r_scr):
                r.finish(a, b, c)

    hbm = pl.BlockSpec(memory_space=pltpu.HBM)
    r_args = [a for r in riders for a in r.arrays]
    r_shapes = [s for r in riders for s in r.out_shape]
    return pl.pallas_call(
        wrapped if riders else body, name=name, grid=grid, in_specs=list(in_specs) + [hbm] * len(r_args),
        out_specs=list(out_specs) + [hbm] * len(r_shapes), out_shape=list(out_shape) + r_shapes,
        scratch_shapes=list(scratch) + [s for r in riders for s in r.scratch], input_output_aliases=aliases or {},
        compiler_params=_cparams(("arbitrary",) * len(grid)))(*args, *r_args)


def _dot(a, b):
    return jnp.dot(a.astype(BF16), b.astype(BF16), preferred_element_type=F32)


def _dot_nt(a, b):
    return lax.dot_general(a.astype(BF16), b.astype(BF16), (((1,), (1,)), ((), ())), preferred_element_type=F32)


def _dot_tn(a, b):
    return lax.dot_general(a.astype(BF16), b.astype(BF16), (((0,), (0,)), ((), ())), preferred_element_type=F32)


def _split(x, n):
    parts = []
    for _ in range(n):
        p = x.astype(BF16)
        parts.append(p)
        x = x - p.astype(F32)
    return parts


def _dot_exact_lhs(m, x, n):
    return functools.reduce(lambda a, b: a + b, [jnp.dot(m, p, preferred_element_type=F32) for p in _split(x, n)])


def _dot_exact_rhs(x, m, n):
    return functools.reduce(lambda a, b: a + b, [jnp.dot(p, m, preferred_element_type=F32) for p in _split(x, n)])


def _rms(x, g):
    r = lax.rsqrt(jnp.mean(x * x, axis=-1, keepdims=True) + EPS)
    return x * r * g, r


def _rms_bwd(x, r, g, dy):
    xr = x * r
    u = dy * g
    dx = r * (u - xr * jnp.mean(u * xr, axis=-1, keepdims=True))
    return dx, jnp.sum(dy * xr, axis=0, keepdims=True)


def _iota(shape, dim):
    return lax.broadcasted_iota(jnp.int32, shape, dim)


def _sigmoid(x):
    return 1.0 / (1.0 + jnp.exp(-x))


def _acc_rows(ref, row):
    ref[...] += jnp.broadcast_to(row, ref.shape)


ZA = ZW // 128 * 128


def _w_in_tail(w_ref):
    return jnp.concatenate([w_ref[ZA:ZW], jnp.zeros((ZC - ZW, D), BF16)], axis=0)


def _inproj(x, g, w_t, riders=()):
    t = x.shape[0]
    tm = min(TM, t)

    def body(x_ref, g_ref, w_ref, z_ref, h_ref):
        h, _ = _rms(x_ref[...], g_ref[...])
        hb = h.astype(BF16)
        h_ref[...] = hb
        z_ref[:, :ZA] = _dot_nt(hb, w_ref[:ZA])
        z_ref[:, ZA:] = _dot_nt(hb, _w_in_tail(w_ref))

    return _call(
        body, "inproj", (t // tm,),
        [pl.BlockSpec((tm, D), lambda i: (i, 0)), pl.BlockSpec((1, D), lambda i: (0, 0)),
         pl.BlockSpec((ZW, D), lambda i: (0, 0))],
        [pl.BlockSpec((tm, ZC), lambda i: (i, 0)), pl.BlockSpec((tm, D), lambda i: (i, 0))],
        [jax.ShapeDtypeStruct((t, ZC), F32), jax.ShapeDtypeStruct((t, D), BF16)], [], (x, g, w_t), riders)


def _kv_proj(mem, g, w):
    def body(m_ref, g_ref, w_ref, kv_ref, mn_ref):
        mn, _ = _rms(m_ref[...], g_ref[...])
        mb = mn.astype(BF16)
        mn_ref[...] = mb
        for j in range(NDEV):
            kv_ref[:, j * XKV_SHARD:(j + 1) * XKV_SHARD] = jnp.dot(mb, w_ref[j], preferred_element_type=F32)

    return pl.pallas_call(
        body, name="kv_proj",
        out_shape=[jax.ShapeDtypeStruct((NMEM, 2 * D), F32), jax.ShapeDtypeStruct((NMEM, D), BF16)],
        compiler_params=pltpu.CompilerParams(vmem_limit_bytes=VMEM_LIMIT))(mem, g, w)


def _softmax_head(qb, kb):
    s = _dot_nt(qb, kb) * (1.0 / 16.0)
    e = jnp.exp(s - jnp.max(s, axis=-1, keepdims=True))
    return e / jnp.sum(e, axis=-1, keepdims=True)


def _attn_fwd(x, z, o_f, o_b, conv_w, conv_norm, gla_norm4, w_out, g, w_xq, kb, vb, w_xo):
    t = x.shape[0]
    tm = min(TM, t)
    nblk = t // tm
    jmap = lambda i: i

    def body(x_ref, zq_ref, zk_ref, zv_ref, zg_ref, cb_ref, cc_ref, cu_ref, ccp_ref, ccn_ref, cup_ref, cun_ref,
             of_ref, ob_ref, cw_ref, cn_ref, gn_ref, wo_ref, g_ref, wq_ref, k_ref, v_ref, wx_ref,
             x1_ref, x2_ref, xn_ref, q_ref, a_ref, y_ref, opre_ref):
        j = pl.program_id(0)
        zv = zv_ref[...]
        sb = _head_sum((zq_ref[...] * 0.125) * zk_ref[...], 64, 128)
        o_pre = of_ref[...] + ob_ref[...] - sb * zv
        opre_ref[...] = o_pre
        on, _ = _head_norm(o_pre)
        zg = zg_ref[...]
        y_ref[:, CW:] = (on * gn_ref[...] * (zg * _sigmoid(zg))).astype(BF16)
        cb = cb_ref[...]
        _, _, _, conv = _conv_parts(cb, cc_ref[...], cu_ref[...], ccp_ref[pl.ds(7, 1), :], cup_ref[pl.ds(7, 1), :],
                                    ccn_ref[pl.ds(0, 1), :], cun_ref[pl.ds(0, 1), :], cw_ref, j == 0,
                                    j == nblk - 1, tm)
        yc = cb * conv
        gm = _group_sum(yc * yc) * (1.0 / 64.0)
        y_ref[:, :CW] = (yc * lax.rsqrt(gm + EPS) * cn_ref[...]).astype(BF16)

        x1 = x_ref[...] + jnp.dot(y_ref[...], wo_ref[...], preferred_element_type=F32)
        x1_ref[...] = x1
        xn, _ = _rms(x1, g_ref[...])
        xb = xn.astype(BF16)
        xn_ref[...] = xb
        qb = jnp.dot(xb, wq_ref[...], preferred_element_type=F32).astype(BF16)
        q_ref[...] = qb
        heads = [slice(h * XD, (h + 1) * XD) for h in range(NH)]
        ps = [_softmax_head(qb[:, hs], k_ref[:, hs]) for hs in heads]
        for hs, p in zip(heads, ps):
            a_ref[:, hs] = _dot(p, v_ref[:, hs]).astype(BF16)
        x2_ref[...] = x1 + jnp.dot(a_ref[...], wx_ref[...], preferred_element_type=F32)

    tok = lambda i: (i, 0)
    full = lambda i: (0, 0)
    once = pl.Buffered(1)
    tokd, tokv = pl.BlockSpec((tm, D), tok), pl.BlockSpec((tm, GV), tok)
    weight = pl.BlockSpec((D, D), full, pipeline_mode=once)
    ccp, ccn = _halo_specs(tm, nblk, t, ZB_CC, jmap)
    cup, cun = _halo_specs(tm, nblk, t, ZB_CU, jmap)
    in_specs = [tokd, _zspec(tm, GK, ZB_Q, jmap), _zspec(tm, GK, ZB_K, jmap), _zspec(tm, GV, ZB_V, jmap),
                _zspec(tm, GV, ZB_G, jmap), _zspec(tm, CW, ZB_CB, jmap), _zspec(tm, CW, ZB_CC, jmap),
                _zspec(tm, CW, ZB_CU, jmap), ccp, ccn, cup, cun, tokv, tokv,
                pl.BlockSpec((3, CW), full), pl.BlockSpec((1, CW), full), pl.BlockSpec((1, GV), full),
                weight, pl.BlockSpec((1, D), full), weight, pl.BlockSpec((NMEM, D), full),
                pl.BlockSpec((NMEM, D), full), weight]
    return pl.pallas_call(
        body, name="attn_fwd", grid=(nblk,), in_specs=in_specs, out_specs=[tokd] * 6 + [tokv],
        out_shape=[jax.ShapeDtypeStruct((t, D), F32), jax.ShapeDtypeStruct((t, D), F32),
                   jax.ShapeDtypeStruct((t, D), BF16), jax.ShapeDtypeStruct((t, D), BF16),
                   jax.ShapeDtypeStruct((t, D), BF16), jax.ShapeDtypeStruct((t, D), BF16),
                   jax.ShapeDtypeStruct((t, GV), F32)],
        compiler_params=_cparams(("arbitrary",)))(
            x, z, z, z, z, z, z, z, z, z, z, z, o_f, o_b, conv_w, conv_norm, gla_norm4, w_out, g, w_xq, kb, vb, w_xo)


def _mlp_fwd(x2, g, w_up_t, w_down, fg, target):
    t = x2.shape[0]
    tm = min(TM_MLP_FWD, t)

    def body(x_ref, g_ref, wu_ref, wd_ref, fg_ref, t_ref, h1_ref, xn_ref, dx_ref, dxb_ref, loss_ref, dfg_ref, ab):
        @pl.when(pl.program_id(0) == 0)
        def _():
            loss_ref[...] = jnp.zeros_like(loss_ref)
            dfg_ref[...] = jnp.zeros_like(dfg_ref)

        x = x_ref[...]
        xn, _ = _rms(x, g_ref[...])
        xnb = xn.astype(BF16)
        xn_ref[...] = xnb
        for q in range(FF // TF):
            cols = slice(q * TF, (q + 1) * TF)
            h1 = _dot_nt(xnb, wu_ref[cols, :])
            h1_ref[:, cols] = h1.astype(BF16)
            hr = jnp.maximum(h1, 0.0)
            ab[:, cols] = (hr * hr).astype(BF16)
        x3 = x + jnp.dot(ab[...], wd_ref[...], preferred_element_type=F32)
        y, r = _rms(x3, fg_ref[...])
        e = y - t_ref[...]
        row = jnp.mean(e * e, axis=-1, keepdims=True)
        _acc_rows(loss_ref, 0.5 * jnp.sum(row, axis=0, keepdims=True))
        dx, dfg = _rms_bwd(x3, r, fg_ref[...], e * (1.0 / D))
        dx_ref[...] = dx
        dxb_ref[...] = dx.astype(BF16)
        _acc_rows(dfg_ref, dfg)

    tok = lambda i: (i, 0)
    full = lambda i: (0, 0)
    once = pl.Buffered(1)
    return pl.pallas_call(
        body, name="mlp_fwd", grid=(t // tm,),
        in_specs=[pl.BlockSpec((tm, D), tok), pl.BlockSpec((1, D), full),
                  pl.BlockSpec((FF, D), full, pipeline_mode=once), pl.BlockSpec((FF, D), full, pipeline_mode=once),
                  pl.BlockSpec((1, D), full), pl.BlockSpec((tm, D), tok)],
        out_specs=[pl.BlockSpec((tm, FF), tok), pl.BlockSpec((tm, D), tok), pl.BlockSpec((tm, D), tok),
                   pl.BlockSpec((tm, D), tok), pl.BlockSpec((8, 128), full), pl.BlockSpec((8, D), full)],
        out_shape=[jax.ShapeDtypeStruct((t, FF), BF16), jax.ShapeDtypeStruct((t, D), BF16),
                   jax.ShapeDtypeStruct((t, D), F32), jax.ShapeDtypeStruct((t, D), BF16),
                   jax.ShapeDtypeStruct((8, 128), F32), jax.ShapeDtypeStruct((8, D), F32)],
        scratch_shapes=[pltpu.VMEM((tm, FF), BF16)],
        compiler_params=_cparams(("arbitrary",)))(x2, g, w_up_t, w_down, fg, target)


def _mlp_bwd(dx3, dx3b, h1b, w_down, w_up_t, x2, g):
    t = x2.shape[0]
    tm = min(TM_MLP, t)

    def body(dx_ref, dxb_ref, h1_ref, wd_ref, wu_ref, x_ref, g_ref, a_ref, dh_ref, dx2_ref, dx2b_ref, dg_ref):
        @pl.when(pl.program_id(0) == 0)
        def _():
            dg_ref[...] = jnp.zeros_like(dg_ref)

        for q in range(FF // TF):
            cols = slice(q * TF, (q + 1) * TF)
            hr = jnp.maximum(h1_ref[:, cols].astype(F32), 0.0)
            da = _dot_nt(dxb_ref[...], wd_ref[cols, :])
            a_ref[:, cols] = (hr * hr).astype(BF16)
            dh_ref[:, cols] = (da * 2.0 * hr).astype(BF16)
        dxn = jnp.dot(dh_ref[...], wu_ref[...], preferred_element_type=F32)
        x = x_ref[...]
        r = lax.rsqrt(jnp.mean(x * x, axis=-1, keepdims=True) + EPS)
        dx, dg = _rms_bwd(x, r, g_ref[...], dxn)
        dx2 = dx_ref[...] + dx
        dx2_ref[...] = dx2
        dx2b_ref[...] = dx2.astype(BF16)
        _acc_rows(dg_ref, dg)

    tok = lambda i: (i, 0)
    full = lambda i: (0, 0)
    once = pl.Buffered(1)
    return pl.pallas_call(
        body, name="mlp_bwd", grid=(t // tm,),
        in_specs=[pl.BlockSpec((tm, D), tok), pl.BlockSpec((tm, D), tok), pl.BlockSpec((tm, FF), tok),
                  pl.BlockSpec((FF, D), full, pipeline_mode=once), pl.BlockSpec((FF, D), full, pipeline_mode=once),
                  pl.BlockSpec((tm, D), tok), pl.BlockSpec((1, D), full)],
        out_specs=[pl.BlockSpec((tm, FF), tok), pl.BlockSpec((tm, FF), tok), pl.BlockSpec((tm, D), tok),
                   pl.BlockSpec((tm, D), tok), pl.BlockSpec((8, D), full)],
        out_shape=[jax.ShapeDtypeStruct((t, FF), BF16), jax.ShapeDtypeStruct((t, FF), BF16),
                   jax.ShapeDtypeStruct((t, D), F32), jax.ShapeDtypeStruct((t, D), BF16),
                   jax.ShapeDtypeStruct((8, D), F32)],
        compiler_params=_cparams(("arbitrary",)))(dx3, dx3b, h1b, w_down, w_up_t, x2, g)


def _attn_bwd(x1, dx2, dx2b, qb, kb, vb, w_xo, w_xq, w_out, g, riders=()):
    t = x1.shape[0]
    tm = min(TM, t)

    def body(x_ref, dx2_ref, dx2b_ref, q_ref, k_ref, v_ref, wx_ref, wq_ref, wo_ref, g_ref,
             dx1_ref, dx1b_ref, dy_ref, dq_ref, dkv_ref, dg_ref):
        @pl.when(pl.program_id(0) == 0)
        def _():
            dkv_ref[...] = jnp.zeros_like(dkv_ref)
            dg_ref[...] = jnp.zeros_like(dg_ref)

        datt = _dot_nt(dx2b_ref[...], wx_ref[...]).astype(BF16)
        heads = [slice(h * XD, (h + 1) * XD) for h in range(NH)]
        ps = [_softmax_head(q_ref[:, hs], k_ref[:, hs]) for hs in heads]
        dps = [_dot_nt(datt[:, hs], v_ref[:, hs]) for hs in heads]
        dss = [(p * (dp - jnp.sum(dp * p, axis=-1, keepdims=True)) * (1.0 / 16.0)).astype(BF16)
               for p, dp in zip(ps, dps)]
        for h, (hs, p, ds) in enumerate(zip(heads, ps, dss)):
            dq_ref[:, hs] = _dot(ds, k_ref[:, hs]).astype(BF16)
            dkv_ref[:, hs] += _dot_tn(ds, q_ref[:, hs])
            dkv_ref[:, D + h * XD:D + (h + 1) * XD] += _dot_tn(p, datt[:, hs])
        dxn = _dot_nt(dq_ref[...], wq_ref[...])
        x = x_ref[...]
        r = lax.rsqrt(jnp.mean(x * x, axis=-1, keepdims=True) + EPS)
        dx, dg = _rms_bwd(x, r, g_ref[...], dxn)
        dx1 = dx2_ref[...] + dx
        dx1_ref[...] = dx1
        dx1b = dx1.astype(BF16)
        dx1b_ref[...] = dx1b
        dy_ref[...] = _dot_nt(dx1b, wo_ref[...])
        _acc_rows(dg_ref, dg)

    tok = lambda i: (i, 0)
    full = lambda i: (0, 0)
    return _call(
        body, "attn_bwd", (t // tm,),
        [pl.BlockSpec((tm, D), tok), pl.BlockSpec((tm, D), tok), pl.BlockSpec((tm, D), tok),
         pl.BlockSpec((tm, D), tok), pl.BlockSpec((NMEM, D), full), pl.BlockSpec((NMEM, D), full),
         pl.BlockSpec((D, D), full), pl.BlockSpec((D, D), full), pl.BlockSpec((D, D), full),
         pl.BlockSpec((1, D), full)],
        [pl.BlockSpec((tm, D), tok), pl.BlockSpec((tm, D), tok), pl.BlockSpec((tm, D), tok),
         pl.BlockSpec((tm, D), tok), pl.BlockSpec((NMEM, 2 * D), full), pl.BlockSpec((8, D), full)],
        [jax.ShapeDtypeStruct((t, D), F32), jax.ShapeDtypeStruct((t, D), BF16),
         jax.ShapeDtypeStruct((t, D), F32), jax.ShapeDtypeStruct((t, D), BF16),
         jax.ShapeDtypeStruct((NMEM, 2 * D), F32), jax.ShapeDtypeStruct((8, D), F32)], [],
        (x1, dx2, dx2b, qb, kb, vb, w_xo, w_xq, w_out, g), riders)


def _kv_bwd(dkv, memn, mem, g, w):
    def body(dkv_ref, mn_ref, m_ref, g_ref, w_ref, dw_ref, dg_ref):
        dkvb = dkv_ref[...].astype(BF16)
        dmn = jnp.zeros((NMEM, D), F32)
        for j in range(NDEV):
            cols = slice(j * XKV_SHARD, (j + 1) * XKV_SHARD)
            dw_ref[j] = _dot_tn(mn_ref[...], dkvb[:, cols])
            dmn += _dot_nt(dkvb[:, cols], w_ref[j])
        m = m_ref[...]
        r = lax.rsqrt(jnp.mean(m * m, axis=-1, keepdims=True) + EPS)
        dg_ref[...] = jnp.broadcast_to(jnp.sum(dmn * m * r, axis=0, keepdims=True), dg_ref.shape)

    return pl.pallas_call(
        body, name="kv_bwd",
        out_shape=[jax.ShapeDtypeStruct((NDEV, D, XKV_SHARD), F32), jax.ShapeDtypeStruct((8, D), F32)],
        compiler_params=pltpu.CompilerParams(vmem_limit_bytes=VMEM_LIMIT))(dkv, memn, mem, g, w)


def _inproj_bwd(dz, w_t, x, dx1, g, riders=()):
    t = x.shape[0]
    tm = min(TM, t)

    def body(dz_ref, w_ref, x_ref, dx1_ref, g_ref, gx_ref, dg_ref):
        @pl.when(pl.program_id(0) == 0)
        def _():
            dg_ref[...] = jnp.zeros_like(dg_ref)

        dh = (jnp.dot(dz_ref[:, :ZA], w_ref[:ZA], preferred_element_type=F32)
              + jnp.dot(dz_ref[:, ZA:], _w_in_tail(w_ref), preferred_element_type=F32))
        x = x_ref[...]
        r = lax.rsqrt(jnp.mean(x * x, axis=-1, keepdims=True) + EPS)
        dx, dg = _rms_bwd(x, r, g_ref[...], dh)
        gx_ref[...] = dx1_ref[...] + dx
        _acc_rows(dg_ref, dg)

    tok = lambda i: (i, 0)
    full = lambda i: (0, 0)
    return _call(
        body, "inproj_bwd", (t // tm,),
        [pl.BlockSpec((tm, ZC), tok), pl.BlockSpec((ZW, D), full), pl.BlockSpec((tm, D), tok),
         pl.BlockSpec((tm, D), tok), pl.BlockSpec((1, D), full)],
        [pl.BlockSpec((tm, D), tok), pl.BlockSpec((8, D), full)],
        [jax.ShapeDtypeStruct((t, D), F32), jax.ShapeDtypeStruct((8, D), F32)], [], (dz, w_t, x, dx1, g), riders,
        aliases={3: 0})


def _matmul_tn(a, b, name, rows=None, riders=()):
    t, k = a.shape
    n = b.shape[1]
    tk, tn = [1024 if size % 1024 == 0 else 640 for size in (k, n)]
    tt = min(TT, t)
    rows = rows or k

    def body(a_ref, b_ref, o_ref):
        @pl.when(pl.program_id(2) == 0)
        def _():
            o_ref[...] = jnp.zeros_like(o_ref)

        o_ref[...] += _dot_tn(a_ref[...], b_ref[...])

    return _call(
        body, name, (k // tk, n // tn, t // tt),
        [pl.BlockSpec((tt, tk), lambda i, j, s: (s, i)), pl.BlockSpec((tt, tn), lambda i, j, s: (s, j))],
        [pl.BlockSpec((tk, tn), lambda i, j, s: (i, j))], [jax.ShapeDtypeStruct((rows, n), F32)], [], (a, b), riders)


def _lane_head(shape, dim, shift):
    return _iota(shape, dim) >> shift


CUM_ROWS = 128


def _chunk_cumsum(x, upper, n):
    r, c = _iota((CUM_ROWS, CUM_ROWS), 0), _iota((CUM_ROWS, CUM_ROWS), 1)
    tri = (c >= r) if upper else (c <= r)
    cum = jnp.where(((r >> 6) == (c >> 6)) & tri, 1.0, 0.0).astype(BF16)
    return jnp.concatenate([_dot_exact_lhs(cum, x[g:g + CUM_ROWS], n) for g in range(0, x.shape[0], CUM_ROWS)],
                           axis=0)


def _gla_recompute(q_raw, k, lr, wpad, bias, rev, tb):
    pre = _dot(lr, wpad) + bias
    la = (jnp.minimum(pre, 0.0) - jnp.log(1.0 + jnp.exp(-jnp.abs(pre)))) * (1.0 / 16.0)
    b = _chunk_cumsum(la, rev, 3)
    e, ei = jnp.exp(b), jnp.exp(-b)
    qt = (q_raw * 0.125) * e
    kt = k * ei
    return pre, b, e, ei, qt, kt


def _stack_heads(x, shift):
    head = _lane_head(x.shape, 1, shift)
    return jnp.concatenate([jnp.where(head == h, x, 0.0) for h in range(NH)], axis=0).astype(BF16)


def _fold_heads(x, shift):
    head = _lane_head((CH, x.shape[1]), 1, shift)
    return functools.reduce(lambda a, b: a + b,
                            [jnp.where(head == h, x[h * CH:(h + 1) * CH], 0.0) for h in range(NH)])


def _wide_mask(rev):
    r, s = _iota((CH, NH * CH), 0), _iota((CH, NH * CH), 1) & (CH - 1)
    return (s >= r) if rev else (s <= r)


def _rows_by_head(x):
    w = x.shape[1] // NH
    return jnp.concatenate([x[:, h * w:(h + 1) * w] for h in range(NH)], axis=0)


def _lanes_by_head(x):
    return jnp.concatenate([x[h * CH:(h + 1) * CH] for h in range(NH)], axis=1)


def _state_compact(xt):
    head = _lane_head((128, GK), 1, 6)
    return functools.reduce(lambda a, b: a + b,
                            [jnp.where(head == h, xt[h * 128:(h + 1) * 128], 0.0) for h in range(NH)])


def _conv_parts(cb, cc, cu, ccp, cup, ccn, cun, cw_ref, first, last, tb):
    h = cc * cu
    hp = jnp.where(first, 0.0, ccp * cup)
    hn = jnp.where(last, 0.0, ccn * cun)
    rows = _iota(h.shape, 0)
    h_m1 = jnp.where(rows == 0, hp, pltpu.roll(h, 1, 0))
    h_p1 = jnp.where(rows == tb - 1, hn, pltpu.roll(h, tb - 1, 0))
    conv = cw_ref[pl.ds(0, 1), :] * h_m1 + cw_ref[pl.ds(1, 1), :] * h + cw_ref[pl.ds(2, 1), :] * h_p1
    return h, h_m1, h_p1, conv


def _head_sum(x, w_in, w_out):
    shape, sh_in, sh_out = (2 * w_in, 2 * w_out), w_in.bit_length() - 1, w_out.bit_length() - 1
    sel = jnp.where((_iota(shape, 0) >> sh_in) == (_iota(shape, 1) >> sh_out), 1.0, 0.0).astype(BF16)
    return jnp.concatenate([_dot_exact_rhs(x[:, s:s + 2 * w_in], sel, 2) for s in range(0, NH * w_in, 2 * w_in)],
                           axis=1)


def _group_sum(x):
    ones = jnp.where((_iota((128, 128), 0) >> 6) == (_iota((128, 128), 1) >> 6), 1.0, 0.0).astype(BF16)
    return jnp.concatenate([_dot_exact_rhs(x[:, s:s + 128], ones, 2) for s in range(0, x.shape[1], 128)], axis=1)


def _head_norm(o):
    ons, rs = [], []
    for h in range(NH):
        slab = o[:, h * 128:(h + 1) * 128]
        r = lax.rsqrt(jnp.mean(slab * slab, axis=-1, keepdims=True) + EPS)
        ons.append(slab * r)
        rs.append(jnp.broadcast_to(r, slab.shape))
    return jnp.concatenate(ons, axis=1), jnp.concatenate(rs, axis=1)


def _zspec(tb, width, blk, jmap):
    return pl.BlockSpec((tb, width), lambda i: (jmap(i), blk))


def _halo_specs(tb, nblk, t, blk, jmap):
    prev = pl.BlockSpec((8, CW), lambda i: (jnp.maximum(jmap(i) * (tb // 8) - 1, 0), blk))
    nxt = pl.BlockSpec((8, CW), lambda i: (jnp.minimum((jmap(i) + 1) * (tb // 8), t // 8 - 1), blk))
    return prev, nxt


def _gla_fwd_block(q_ref, k_ref, v_ref, lr_ref, w_ref, bias_ref, o_ref, sd_ref, st, b_scr, rev, tb):
    nb = tb // CH
    _, b, _, _, qt, kt = _gla_recompute(q_ref[...], k_ref[...], lr_ref[...], w_ref[...], bias_ref[...], rev, tb)
    v = v_ref[...]
    b_scr[...] = b
    yield
    maskw = _wide_mask(rev)
    order = list(reversed(range(nb))) if rev else list(range(nb))
    rows = [slice(c * CH, (c + 1) * CH) for c in range(nb)]
    state = st[...]
    for c in order:
        gdec = jnp.exp(b_scr[pl.ds(c * CH + (0 if rev else CH - 1), 1), :])
        sd_ref[c] = state
        a = jnp.where(maskw, _dot_nt(qt[rows[c]], _stack_heads(kt[rows[c]], 6)), 0.0)
        o_inter = _lanes_by_head(_dot_nt(_stack_heads(qt[rows[c]], 6), state))
        o_ref[pl.ds(c * CH, CH), :] = _dot(a, _stack_heads(v[rows[c]], 7)) + o_inter
        state = state * gdec + _state_compact(_dot_tn(v[rows[c]], kt[rows[c]] * gdec))
        yield
    st[...] = state
    yield


def _gla_fwd(z, waf_pad, b_af, wab_pad, b_ab, riders=()):
    t = z.shape[0]
    tb = min(TB, t)
    nblk, nb = t // tb, tb // CH
    jmaps = (lambda i: i, lambda i: nblk - 1 - i)

    def body(qf, kf, vf, lrf, qr, kr, vr, lrr, wf, bf, wr, br, of_ref, sdf_ref, or_ref, sdr_ref,
             st_f, st_r, b_f, b_r):
        @pl.when(pl.program_id(0) == 0)
        def _():
            st_f[...] = jnp.zeros_like(st_f)
            st_r[...] = jnp.zeros_like(st_r)

        for _ in zip(_gla_fwd_block(qf, kf, vf, lrf, wf, bf, of_ref, sdf_ref, st_f, b_f, False, tb),
                     _gla_fwd_block(qr, kr, vr, lrr, wr, br, or_ref, sdr_ref, st_r, b_r, True, tb)):
            pass

    full = lambda i: (0, 0)
    zspecs = [s for jm in jmaps for s in (_zspec(tb, GK, ZB_Q, jm), _zspec(tb, GK, ZB_K, jm),
                                         _zspec(tb, GV, ZB_V, jm), _zspec(tb, 128, ZB_LR, jm))]
    wspecs = [pl.BlockSpec((128, GK), full), pl.BlockSpec((1, GK), full)] * 2
    out_specs = [s for jm in jmaps for s in (pl.BlockSpec((tb, GV), lambda i, jm=jm: (jm(i), 0)),
                                             pl.BlockSpec((nb, 128, GK), lambda i, jm=jm: (jm(i), 0, 0)))]
    out_shape = [jax.ShapeDtypeStruct((t, GV), F32), jax.ShapeDtypeStruct((t // CH, 128, GK), F32)] * 2
    scratch = [pltpu.VMEM((128, GK), F32), pltpu.VMEM((128, GK), F32), pltpu.VMEM((tb, GK), F32),
               pltpu.VMEM((tb, GK), F32)]
    return _call(body, "gla_fwd", (nblk,), zspecs + wspecs, out_specs, out_shape, scratch,
                 [z] * 8 + [waf_pad, b_af, wab_pad, b_ab], riders)


def _gla_bwd_chunks(do_ref, sd_ref, dst, b_scr, db_scr, dq_ref, dk_ref, dv_ref, qt, kt, e, ei, v, rev, nb):
    maskw = _wide_mask(rev)
    for c in (range(nb) if rev else reversed(range(nb))):
        sl = slice(c * CH, (c + 1) * CH)
        grow = c * CH + (0 if rev else CH - 1)
        gdec = jnp.exp(b_scr[pl.ds(grow, 1), :])
        qt_c, kt_c, v_c, do_c = qt[sl], kt[sl], v[sl], do_ref[pl.ds(c * CH, CH), :]
        s_in, ds_out = sd_ref[c], dst[...]
        kbd, vbd = _stack_heads(kt_c, 6), _stack_heads(v_c, 7)
        a = jnp.where(maskw, _dot_nt(qt_c, kbd), 0.0)
        da = jnp.where(maskw, _dot_nt(do_c, vbd), 0.0)
        dv_ref[pl.ds(c * CH, CH), :] = (_fold_heads(_dot_tn(a, do_c), 7)
                                        + _lanes_by_head(_dot_nt(_stack_heads(kt_c * gdec, 6), ds_out)))
        dqt = _dot(da, kbd) + _fold_heads(_dot(_rows_by_head(do_c), s_in), 6)
        dkh = _fold_heads(_dot(_rows_by_head(v_c), ds_out), 6)
        da_do = jnp.concatenate([da.astype(BF16), do_c.astype(BF16)], axis=1)
        both = _dot_tn(da_do, qt_c)
        dkt = _fold_heads(both[:NH * CH], 6) + dkh * gdec
        dg = jnp.sum(ds_out * s_in, axis=0, keepdims=True) + jnp.sum(kt_c * dkh, axis=0, keepdims=True)
        db_scr[pl.ds(c * CH, CH), :] = dqt * qt_c - dkt * kt_c
        db_scr[pl.ds(grow, 1), :] += dg * gdec
        dq_ref[pl.ds(c * CH, CH), :] = dqt * e[sl] * 0.125
        dk_ref[pl.ds(c * CH, CH), :] = dkt * ei[sl]
        dst[...] = ds_out * gdec + _state_compact(both[NH * CH:])
        yield


def _gate_bwd(db, pre, lr, wpad, rev, tb):
    dla = _chunk_cumsum(db, not rev, 2)
    dpre = dla * (1.0 / 16.0) / (1.0 + jnp.exp(pre))
    return dpre, _dot_nt(dpre, wpad), _dot_tn(lr, dpre)


def _gla_bwd_first(z, dy, o_pre, sd, wpad, bias, conv_w, conv_norm, gla_norm4, riders=()):
    t = z.shape[0]
    tb = min(TB_BWD, t)
    nblk, nb = t // tb, tb // CH
    jmap = lambda i: nblk - 1 - i

    def body(q_ref, k_ref, v_ref, lr_ref, g_ref, cb_ref, cc_ref, cu_ref, ccp_ref, ccn_ref, cup_ref, cun_ref,
             dy_ref, opre_ref, sd_ref, w_ref, bias_ref, cw_ref, cn_ref, gn_ref,
             do_ref, dq_ref, dk_ref, dv_ref, dlr_ref, dzg_ref, dzcb_ref, dconv_ref,
             dw_ref, dbias_ref, dcw_ref, dcn_ref, dgn_ref, dst, b_scr, db_scr):
        i = pl.program_id(0)
        j = jmap(i)

        @pl.when(i == 0)
        def _():
            dst[...] = jnp.zeros_like(dst)
            for ref in (dw_ref, dbias_ref, dcw_ref, dcn_ref, dgn_ref):
                ref[...] = jnp.zeros_like(ref)

        dyg = dy_ref[:, CW:]
        g = g_ref[...]
        sig = _sigmoid(g)
        on, rr = _head_norm(opre_ref[...])
        gn = gn_ref[...]
        dzg_ref[...] = (dyg * on * gn * (sig * (1.0 + g * (1.0 - sig)))).astype(BF16)
        don = dyg * (g * sig)
        _acc_rows(dgn_ref, jnp.sum(don * on, axis=0, keepdims=True))
        u = don * gn
        uo = u * on
        mean_uo = jnp.concatenate(
            [jnp.broadcast_to(jnp.mean(uo[:, h * 128:(h + 1) * 128], axis=-1, keepdims=True), (tb, 128))
             for h in range(NH)], axis=1)
        do_ref[...] = rr * (u - on * mean_uo)

        def conv_branch():
            cb = cb_ref[...]
            h, h_m1, h_p1, conv = _conv_parts(cb, cc_ref[...], cu_ref[...], ccp_ref[pl.ds(7, 1), :],
                                              cup_ref[pl.ds(7, 1), :], ccn_ref[pl.ds(0, 1), :],
                                              cun_ref[pl.ds(0, 1), :], cw_ref, j == 0, j == nblk - 1, tb)
            yc = cb * conv
            yield
            rc = lax.rsqrt(_group_sum(yc * yc) * (1.0 / 64.0) + EPS)
            ycr = yc * rc
            yield
            dyn = dy_ref[:, :CW]
            _acc_rows(dcn_ref, jnp.sum(dyn * ycr, axis=0, keepdims=True))
            uc = dyn * cn_ref[...]
            yield
            dyc = rc * (uc - ycr * (_group_sum(uc * ycr) * (1.0 / 64.0)))
            dzcb_ref[...] = (dyc * conv).astype(BF16)
            yield
            dconv = dyc * cb
            dconv_ref[...] = dconv
            yield
            dcw_ref[pl.ds(0, 1), :] += jnp.sum(dconv * h_m1, axis=0, keepdims=True)
            dcw_ref[pl.ds(1, 1), :] += jnp.sum(dconv * h, axis=0, keepdims=True)
            dcw_ref[pl.ds(2, 1), :] += jnp.sum(dconv * h_p1, axis=0, keepdims=True)
            yield

        lr, wp = lr_ref[...], w_ref[...]
        pre, b, e, ei, qt, kt = _gla_recompute(q_ref[...], k_ref[...], lr, wp, bias_ref[...], False, tb)
        b_scr[...] = b
        for _ in itertools.zip_longest(
                _gla_bwd_chunks(do_ref, sd_ref, dst, b_scr, db_scr, dq_ref, dk_ref, dv_ref, qt, kt, e, ei, v_ref[...],
                                False, nb), conv_branch()):
            pass
        dpre, dlr, dw = _gate_bwd(db_scr[...], pre, lr, wp, False, tb)
        dlr_ref[...] = dlr
        dw_ref[...] += dw
        _acc_rows(dbias_ref, jnp.sum(dpre, axis=0, keepdims=True))

    full = lambda i: (0, 0)
    tokv = pl.BlockSpec((tb, GV), lambda i: (jmap(i), 0))
    tokk = pl.BlockSpec((tb, GK), lambda i: (jmap(i), 0))
    ccp, ccn = _halo_specs(tb, nblk, t, ZB_CC, jmap)
    cup, cun = _halo_specs(tb, nblk, t, ZB_CU, jmap)
    in_specs = [_zspec(tb, GK, ZB_Q, jmap), _zspec(tb, GK, ZB_K, jmap), _zspec(tb, GV, ZB_V, jmap),
                _zspec(tb, 128, ZB_LR, jmap), _zspec(tb, GV, ZB_G, jmap), _zspec(tb, CW, ZB_CB, jmap),
                _zspec(tb, CW, ZB_CC, jmap), _zspec(tb, CW, ZB_CU, jmap), ccp, ccn, cup, cun,
                pl.BlockSpec((tb, D), lambda i: (jmap(i), 0)), tokv,
                pl.BlockSpec((nb, 128, GK), lambda i: (jmap(i), 0, 0)), pl.BlockSpec((128, GK), full),
                pl.BlockSpec((1, GK), full), pl.BlockSpec((3, CW), full), pl.BlockSpec((1, CW), full),
                pl.BlockSpec((1, GV), full)]
    out_specs = [tokv, tokk, tokk, tokv, pl.BlockSpec((tb, 128), lambda i: (jmap(i), 0)), tokv, tokv, tokv,
                 pl.BlockSpec((128, GK), full), pl.BlockSpec((8, GK), full), pl.BlockSpec((8, CW), full),
                 pl.BlockSpec((8, CW), full), pl.BlockSpec((8, GV), full)]
    out_shape = [jax.ShapeDtypeStruct((t, GV), F32), jax.ShapeDtypeStruct((t, GK), F32),
                 jax.ShapeDtypeStruct((t, GK), F32), jax.ShapeDtypeStruct((t, GV), F32),
                 jax.ShapeDtypeStruct((t, 128), F32), jax.ShapeDtypeStruct((t, GV), BF16),
                 jax.ShapeDtypeStruct((t, CW), BF16), jax.ShapeDtypeStruct((t, CW), F32),
                 jax.ShapeDtypeStruct((128, GK), F32), jax.ShapeDtypeStruct((8, GK), F32),
                 jax.ShapeDtypeStruct((8, CW), F32), jax.ShapeDtypeStruct((8, CW), F32),
                 jax.ShapeDtypeStruct((8, GV), F32)]
    return _call(
        body, "gla_bwd_first", (nblk,), in_specs, out_specs, out_shape,
        [pltpu.VMEM((128, GK), F32), pltpu.VMEM((tb, GK), F32), pltpu.VMEM((tb, GK), F32)],
        (z, z, z, z, z, z, z, z, z, z, z, z, dy, o_pre, sd, wpad, bias, conv_w, conv_norm, gla_norm4), riders)


def _gla_bwd_second(z, do, sd, wpad, bias, dqa, dka, dva, dlra, dzg, dzcb, dconv, conv_w, riders=()):
    t = z.shape[0]
    tb = min(TB_BWD, t)
    nblk, nb = t // tb, tb // CH
    jmap = lambda i: i

    def body(q_ref, k_ref, v_ref, lr_ref, cc_ref, cu_ref, do_ref, sd_ref, w_ref, bias_ref, dqa_ref, dka_ref,
             dva_ref, dlra_ref, dzg_ref, dzcb_ref, dc_ref, dcp_ref, dcn_ref, cw_ref,
             dz_ref, dw_ref, dbias_ref, dst, b_scr, db_scr, dq_scr, dk_scr, dv_scr, sb_scr, dsk_scr):
        i = pl.program_id(0)

        @pl.when(i == 0)
        def _():
            dst[...] = jnp.zeros_like(dst)
            dw_ref[...] = jnp.zeros_like(dw_ref)
            dbias_ref[...] = jnp.zeros_like(dbias_ref)

        q_raw, k, v, lr, wp = q_ref[...], k_ref[...], v_ref[...], lr_ref[...], w_ref[...]
        pre, b, e, ei, qt, kt = _gla_recompute(q_raw, k, lr, wp, bias_ref[...], True, tb)
        b_scr[...] = b

        def token_local():
            dc = dc_ref[...]
            rows = _iota(dc.shape, 0)
            dprev = jnp.where(i == 0, 0.0, dcp_ref[pl.ds(7, 1), :])
            dnext = jnp.where(i == nblk - 1, 0.0, dcn_ref[pl.ds(0, 1), :])
            dc_m1 = jnp.where(rows == 0, dprev, pltpu.roll(dc, 1, 0))
            dc_p1 = jnp.where(rows == tb - 1, dnext, pltpu.roll(dc, tb - 1, 0))
            yield
            dh = cw_ref[pl.ds(0, 1), :] * dc_p1 + cw_ref[pl.ds(1, 1), :] * dc + cw_ref[pl.ds(2, 1), :] * dc_m1
            dz_ref[:, 0:512] = dzcb_ref[...]
            yield
            dz_ref[:, 512:1024] = (dh * cu_ref[...]).astype(BF16)
            dz_ref[:, 1024:1536] = (dh * cc_ref[...]).astype(BF16)
            dz_ref[:, 2560:3072] = dzg_ref[...]
            yield
            sb_scr[...] = _head_sum((q_raw * 0.125) * k, 64, 128)
            yield
            dsk_scr[...] = _head_sum(do_ref[...] * v, 128, 64)
            yield

        for _ in itertools.zip_longest(
                _gla_bwd_chunks(do_ref, sd_ref, dst, b_scr, db_scr, dq_scr, dk_scr, dv_scr, qt, kt, e, ei, v, True, nb),
                token_local()):
            pass
        dpre, dlr, dw = _gate_bwd(db_scr[...], pre, lr, wp, True, tb)
        dw_ref[...] += dw
        _acc_rows(dbias_ref, jnp.sum(dpre, axis=0, keepdims=True))
        dsk = dsk_scr[...]
        dz_ref[:, 1536:1792] = (dqa_ref[...] + dq_scr[...] - dsk * k * 0.125).astype(BF16)
        dz_ref[:, 1792:2048] = (dka_ref[...] + dk_scr[...] - dsk * (q_raw * 0.125)).astype(BF16)
        dz_ref[:, 2048:2560] = (dva_ref[...] + dv_scr[...] - sb_scr[...] * do_ref[...]).astype(BF16)
        dz_ref[:, 3072:3200] = (dlra_ref[...] + dlr).astype(BF16)

    full = lambda i: (0, 0)
    tokv = pl.BlockSpec((tb, GV), lambda i: (i, 0))
    tokk = pl.BlockSpec((tb, GK), lambda i: (i, 0))
    dcp = pl.BlockSpec((8, CW), lambda i: (jnp.maximum(i * (tb // 8) - 1, 0), 0))
    dcn = pl.BlockSpec((8, CW), lambda i: (jnp.minimum((i + 1) * (tb // 8), t // 8 - 1), 0))
    in_specs = [_zspec(tb, GK, ZB_Q, jmap), _zspec(tb, GK, ZB_K, jmap), _zspec(tb, GV, ZB_V, jmap),
                _zspec(tb, 128, ZB_LR, jmap), _zspec(tb, CW, ZB_CC, jmap), _zspec(tb, CW, ZB_CU, jmap), tokv,
                pl.BlockSpec((nb, 128, GK), lambda i: (i, 0, 0)), pl.BlockSpec((128, GK), full),
                pl.BlockSpec((1, GK), full), tokk, tokk, tokv, pl.BlockSpec((tb, 128), lambda i: (i, 0)), tokv, tokv,
                tokv, dcp, dcn, pl.BlockSpec((3, CW), full)]
    out_specs = [pl.BlockSpec((tb, ZC), lambda i: (i, 0)), pl.BlockSpec((128, GK), full), pl.BlockSpec((8, GK), full)]
    out_shape = [jax.ShapeDtypeStruct((t, ZC), BF16), jax.ShapeDtypeStruct((128, GK), F32),
                 jax.ShapeDtypeStruct((8, GK), F32)]
    return _call(
        body, "gla_bwd_second", (nblk,), in_specs, out_specs, out_shape,
        [pltpu.VMEM((128, GK), F32), pltpu.VMEM((tb, GK), F32), pltpu.VMEM((tb, GK), F32),
         pltpu.VMEM((tb, GK), F32), pltpu.VMEM((tb, GK), F32), pltpu.VMEM((tb, GV), F32),
         pltpu.VMEM((tb, GV), F32), pltpu.VMEM((tb, GK), F32)],
        (z, z, z, z, z, z, do, sd, wpad, bias, dqa, dka, dva, dlra, dzg, dzcb, dconv, dconv, dconv, conv_w), riders)


def _step(x, mem, target, shard, small_pack, vec, place):
    own, from_chips = {}, {}

    def pair_sums(names, g4, from_sibling):
        pbs = {}
        for shape in dict.fromkeys(g.shape for g in g4):
            idx = [i for i, g in enumerate(g4) if g.shape == shape]
            pb, mine = _rs_pair_sum(place, [g4[i] for i in idx], [from_sibling[i] for i in idx],
                                    "pair_sum_" + "_".join(names[i] for i in idx))
            for i, b, o in zip(idx, pb, mine):
                pbs[i], own[names[i]] = b, o
        return [pbs[i] for i in range(len(g4))]

    def by_dest(g, n):
        return g.reshape((4, 2) + shard[n].shape)

    w_in, small_all = _exchange(_gather_rider([shard["w_in"], small_pack]), "gather_w_in")
    w_in = w_in.reshape(ZW, D)
    small_all = small_all.reshape(NDEV, -1)
    p, off = {}, 0
    for n, (r, c) in SMALL_SHARDED.items():
        p[n] = small_all[:, off:off + r * c].reshape(NDEV, r, c).transpose(1, 0, 2).reshape(r, NDEV * c)
        off += r * c
    zeros_lr = jnp.zeros((128 - LR, GK), BF16)
    waf_pad = jnp.concatenate([p["w_af"].astype(BF16), zeros_lr], axis=0)
    wab_pad = jnp.concatenate([jnp.zeros((LR, GK), BF16), p["w_ab"].astype(BF16), zeros_lr[:128 - 2 * LR]], axis=0)
    gla_norm4 = jnp.tile(vec["gla_norm"], (1, NH))

    z, hb, w_out, w_xq, w_xo, w_xkv = _inproj(
        x, vec["mix_norm"], w_in, [_gather_rider([shard[n] for n in ("w_out", "w_xq", "w_xo", "w_xkv")])])
    w_out, w_xq, w_xo = [a.reshape(D, D) for a in (w_out, w_xq, w_xo)]
    o_f, sd_f, o_b, sd_b, w_up_t, w_down = _gla_fwd(
        z, waf_pad, vec["b_af"], wab_pad, vec["b_ab"], [_gather_rider([shard["w_up"], shard["w_down"]])])
    w_up_t, w_down = w_up_t.reshape(FF, D), w_down.reshape(FF, D)
    kv, memn = _kv_proj(mem, vec["mem_norm"], w_xkv)
    kb, vb = kv[:, :D].astype(BF16), kv[:, D:].astype(BF16)
    x1, x2, xn1, qb, attb, yb, o_pre = _attn_fwd(x, z, o_f, o_b, p["conv_w"], vec["conv_norm"], gla_norm4, w_out,
                                                 vec["xa_norm"], w_xq, kb, vb, w_xo)
    h1b, xn2, dx3, dx3b, loss8, dfinal = _mlp_fwd(x2, vec["mlp_norm"], w_up_t, w_down, vec["final_norm"], target)

    ab, dh1b, dx2, dx2b, dmlp = _mlp_bwd(dx3, dx3b, h1b, w_down, w_up_t, x2, vec["mlp_norm"])
    g_mlp = [by_dest(_matmul_tn(ab, dx3b, "dw_down")[0], "w_down"),
             by_dest(_matmul_tn(dh1b, xn2, "dw_up")[0], "w_up")]
    dx1, dx1b, dy, dqb, dkv, dxa, *s_mlp = _attn_bwd(x1, dx2, dx2b, qb, kb, vb, w_xo, w_xq, w_out, vec["xa_norm"],
                                                     riders=[_sibling_rider(g_mlp)])
    pb_mlp = pair_sums(("w_down", "w_up"), g_mlp, s_mlp)
    dw_xo = _matmul_tn(attb, dx2b, "dw_xo")[0]
    dw_xkv, dmemn = _kv_bwd(dkv, memn, mem, vec["mem_norm"], w_xkv)
    att_names = ("w_xo", "w_xq", "w_out", "w_xkv")
    g_att = [by_dest(g, n) for g, n in zip(
        (dw_xo, _matmul_tn(xn1, dqb, "dw_xq")[0], _matmul_tn(yb, dx1b, "dw_out")[0], dw_xkv), att_names)]
    res = _gla_bwd_first(z, dy, o_pre, sd_f, waf_pad, vec["b_af"], p["conv_w"], vec["conv_norm"], gla_norm4,
                         riders=[_chips_rider(pb_mlp), _sibling_rider(g_att)])
    do, dqa, dka, dva, dlra, dzg, dzcb, dconv, dwaf, dbaf, dcw, dcn, dgn = res[:13]
    from_chips["w_down"], from_chips["w_up"] = res[13:15]
    pb_att = pair_sums(att_names, g_att, res[15:])
    dz, dwab, dbab, *c_att = _gla_bwd_second(z, do, sd_b, wab_pad, vec["b_ab"], dqa, dka, dva, dlra, dzg, dzcb, dconv,
                                             p["conv_w"], riders=[_chips_rider(pb_att)])
    from_chips.update(zip(att_names, c_att))
    g_in = [by_dest(_matmul_tn(dz, hb, "dw_in", rows=ZW)[0], "w_in")]
    pb_in = pair_sums(("w_in",), g_in, _exchange(_sibling_rider(g_in), "grads_to_sibling_w_in"))
    grad_x, dmix, from_chips["w_in"] = _inproj_bwd(dz, w_in, x, dx1, vec["mix_norm"], riders=[_chips_rider(pb_in)])

    small_acc = dict(mix_norm=dmix, conv_w=dcw, conv_norm=dcn, w_af=dwaf, b_af=dbaf, w_ab=dwab, b_ab=dbab,
                     gla_norm=dgn, xa_norm=dxa, mem_norm=dmemn, mlp_norm=dmlp, final_norm=dfinal)
    return loss8, grad_x, small_acc, own, from_chips


def _place():
    return lax.axis_index("x"), lax.axis_index("y"), lax.axis_index("c")


class _Rider:
    def __init__(self, arrays, out_shape, scratch, start, finish, forward=None, relay=None):
        self.arrays, self.out_shape, self.scratch, self.start, self.finish = arrays, out_shape, scratch, start, finish
        self.forward, self.relay = forward, relay


def _gather_rider(blks, early_relay=True):
    n = len(blks)

    def plan(in_refs, out_refs, sems):
        send_sems, recv_sems, local_sems = sems
        x, y, c = _place()
        me, sibling = (x, y, c), (x, y, 1 - c)
        xn, yn, dg = (1 - x, y, c), (x, 1 - y, c), (1 - x, 1 - y, c)
        via = (x + (1 - c) * (1 - 2 * x), y + c * (1 - 2 * y), c)
        onto = (x + c * (1 - 2 * x), y + (1 - c) * (1 - 2 * y), c)
        other = onto

        def copy(a, k, block, to, own=False):
            px, py, pc = block
            dst = out_refs[a].at[4 * px + 2 * py + pc]
            return pltpu.make_async_remote_copy(
                src_ref=in_refs[a] if own else dst, dst_ref=dst, send_sem=send_sems.at[k, a],
                recv_sem=recv_sems.at[k, a], device_id=to, device_id_type=MESH)

        def local(a):
            return pltpu.make_async_copy(in_refs[a], out_refs[a].at[4 * x + 2 * y + c], local_sems.at[a])

        def sends(a):
            return ([copy(a, 0, me, sibling, own=True), copy(a, 1, me, xn, own=True), copy(a, 2, me, yn, own=True),
                     copy(a, 3, via, onto), copy(a, 4 + c, via, sibling), copy(a, 5 - c, other, sibling),
                     copy(a, 6, dg, sibling)])

        return copy, local, sends, me, sibling, (xn, yn, dg), via, other

    def start(in_refs, out_refs, sems):
        _, local, sends, _, _, _, _, _ = plan(in_refs, out_refs, sems)
        for a in range(n):
            local(a).start()
            for cp in sends(a)[:3]:
                cp.start()

    def forward(in_refs, out_refs, sems):
        copy, _, sends, me, _, _, via, _ = plan(in_refs, out_refs, sems)
        for a in range(n):
            copy(a, 1 + me[2], via, me).wait_recv()
            sends(a)[3].start()
            sends(a)[4].start()

    def relay(in_refs, out_refs, sems):
        copy, _, sends, me, _, (_, _, dg), _, other = plan(in_refs, out_refs, sems)
        for a in range(n):
            copy(a, 2 - me[2], other, me).wait_recv()
            sends(a)[5].start()
        for a in range(n):
            copy(a, 3, dg, me).wait_recv()
            sends(a)[6].start()

    def finish(in_refs, out_refs, sems):
        if not early_relay:
            forward(in_refs, out_refs, sems)
            relay(in_refs, out_refs, sems)
        copy, local, sends, me, sibling, chips, _, _ = plan(in_refs, out_refs, sems)
        for a in range(n):
            copy(a, 0, sibling, me).wait_recv()
            for j, (px, py, pc) in enumerate(chips):
                copy(a, 4 + j, (px, py, 1 - pc), me).wait_recv()
            for cp in sends(a):
                cp.wait_send()
            local(a).wait()

    return _Rider(blks, [jax.ShapeDtypeStruct((NDEV,) + b.shape, b.dtype) for b in blks],
                  [pltpu.SemaphoreType.DMA((7, n)), pltpu.SemaphoreType.DMA((7, n)), pltpu.SemaphoreType.DMA((n,))],
                  start, finish, forward if early_relay else None, relay if early_relay else None)


def _sibling_rider(g4s):
    n = len(g4s)

    def copies(in_refs, out_refs, sems):
        send_sems, recv_sems = sems
        x, y, c = _place()
        return [pltpu.make_async_remote_copy(
            src_ref=in_refs[a].at[k, 1 - c], dst_ref=out_refs[a].at[k], send_sem=send_sems.at[k, a],
            recv_sem=recv_sems.at[k, a], device_id=(x, y, 1 - c), device_id_type=MESH)
            for a in range(n) for k in range(4)]

    def start(in_refs, out_refs, sems):
        for cp in copies(in_refs, out_refs, sems):
            cp.start()

    def finish(in_refs, out_refs, sems):
        for cp in copies(in_refs, out_refs, sems):
            cp.wait()

    return _Rider(g4s, [jax.ShapeDtypeStruct((4,) + g.shape[2:], g.dtype) for g in g4s],
                  [pltpu.SemaphoreType.DMA((4, n)), pltpu.SemaphoreType.DMA((4, n))], start, finish)


def _chips_rider(pbs):
    n = len(pbs)

    def copies(in_refs, out_refs, sems):
        send_sems, recv_sems = sems
        x, y, c = _place()
        peers = [(1 - x, y), (x, 1 - y), (1 - x, 1 - y)]
        return [pltpu.make_async_remote_copy(
            src_ref=in_refs[a].at[2 * px + py], dst_ref=out_refs[a].at[k], send_sem=send_sems.at[k, a],
            recv_sem=recv_sems.at[k, a], device_id=(px, py, c), device_id_type=MESH)
            for a in range(n) for k, (px, py) in enumerate(peers)]

    def start(in_refs, out_refs, sems):
        for cp in copies(in_refs, out_refs, sems):
            cp.start()

    def finish(in_refs, out_refs, sems):
        for cp in copies(in_refs, out_refs, sems):
            cp.wait()

    return _Rider(pbs, [jax.ShapeDtypeStruct((3,) + p.shape[1:], p.dtype) for p in pbs],
                  [pltpu.SemaphoreType.DMA((3, n)), pltpu.SemaphoreType.DMA((3, n))], start, finish)


def _exchange(rider, name):
    n_in, n_out = len(rider.arrays), len(rider.out_shape)

    def body(*refs):
        ins, outs, sems = refs[:n_in], refs[n_in:n_in + n_out], refs[n_in + n_out:]
        rider.start(ins, outs, sems)
        for hook in (rider.forward, rider.relay):
            if hook:
                hook(ins, outs, sems)
        rider.finish(ins, outs, sems)

    hbm = pl.BlockSpec(memory_space=pltpu.HBM)
    return pl.pallas_call(body, name=name, out_shape=rider.out_shape, in_specs=[hbm] * n_in,
                          out_specs=[hbm] * n_out, scratch_shapes=rider.scratch)(*rider.arrays)


def _rs_pair_sum(place, g4s, r1s, name):
    n = len(g4s)
    rows, cols = g4s[0].shape[2:]
    tr = min(rows, 512)

    def body(pl_ref, *refs):
        for g_ref, r_ref, pb_ref, own_ref in zip(refs[:n], refs[n:2 * n], refs[2 * n:3 * n], refs[3 * n:]):
            s = g_ref[0, 0] + r_ref[0]
            pb_ref[0] = s.astype(BF16)

            @pl.when(pl.program_id(1) == pl_ref[0])
            def _():
                own_ref[...] = s

    grid_spec = pltpu.PrefetchScalarGridSpec(
        num_scalar_prefetch=1, grid=(rows // tr, 4),
        in_specs=[pl.BlockSpec((1, 1, tr, cols), lambda r, k, p: (k, p[1], r, 0))] * n
        + [pl.BlockSpec((1, tr, cols), lambda r, k, p: (k, r, 0))] * n,
        out_specs=[pl.BlockSpec((1, tr, cols), lambda r, k, p: (k, r, 0))] * n
        + [pl.BlockSpec((tr, cols), lambda r, k, p: (r, 0))] * n)
    res = pl.pallas_call(
        body, name=name, grid_spec=grid_spec,
        out_shape=[jax.ShapeDtypeStruct((4, rows, cols), BF16)] * n + [jax.ShapeDtypeStruct((rows, cols), F32)] * n,
        compiler_params=_cparams(("arbitrary", "arbitrary")))(place, *g4s, *r1s)
    return res[:n], res[n:]


PACK_ROWS = 32
VEC_ROW = {"mix_norm": 0, "conv_norm": 1, "b_af": 2, "b_ab": 3, "gla_norm": 4, "xa_norm": 5, "mem_norm": 6,
           "mlp_norm": 7, "final_norm": 8}
LOSS_ROW, MAT_ROW = 9, 16
MAT_LANE = {"w_af": 0, "w_ab": GK, "conv_w": 2 * GK}
MAT_SRC_ROW = {"w_af": 0, "w_ab": LR, "conv_w": 0}


SMALL_WIDTH = {"mix_norm": D, "conv_w": 64, "conv_norm": CW, "w_af": 32, "b_af": GK, "w_ab": 32, "b_ab": GK,
               "gla_norm": 128, "xa_norm": D, "mem_norm": D, "mlp_norm": D, "final_norm": D}


def _small_reduce(acc, loss8):
    names = list(SMALL)
    n = len(names)
    widths = SMALL_WIDTH

    def body(*refs):
        acc_refs = dict(zip(names, refs[:n]))
        loss_ref, tot = refs[n], refs[n + 1]
        pk, all_ref, send_sems, recv_sems, local_sem = refs[n + 2:]

        pk[...] = jnp.zeros_like(pk)
        for k, row in VEC_ROW.items():
            if k == "gla_norm":
                g = functools.reduce(lambda a, b: a + b, [acc_refs[k][pl.ds(0, 1), pl.ds(h * 128, 128)]
                                                          for h in range(NH)])
            else:
                g = acc_refs[k][pl.ds(0, 1), :]
            pk[pl.ds(row, 1), pl.ds(0, widths[k])] = g
        pk[pl.ds(LOSS_ROW, 1), pl.ds(0, 128)] = loss_ref[pl.ds(0, 1), :]
        for k, lane in MAT_LANE.items():
            rows, cols = (3, CW) if k == "conv_w" else (LR, GK)
            pk[pl.ds(MAT_ROW, rows), pl.ds(lane, cols)] = acc_refs[k][pl.ds(MAT_SRC_ROW[k], rows), :]

        x, y, c = _place()
        me, sibling = (x, y, c), (x, y, 1 - c)
        chips = [(1 - x, y, c), (x, 1 - y, c), (1 - x, 1 - y, c)]

        def copy(k, block, to, own=False):
            px, py, pc = block
            dst = all_ref.at[4 * px + 2 * py + pc]
            return pltpu.make_async_remote_copy(
                src_ref=pk if own else dst, dst_ref=dst, send_sem=send_sems.at[k], recv_sem=recv_sems.at[k],
                device_id=to, device_id_type=MESH)

        mine = pltpu.make_async_copy(pk, all_ref.at[4 * x + 2 * y + c], local_sem)
        mine.start()
        first = [copy(0, me, sibling, own=True)] + [copy(1 + j, me, chip, own=True) for j, chip in enumerate(chips)]
        for cp in first:
            cp.start()
        passed = [copy(4 + j, chip, sibling) for j, chip in enumerate(chips)]
        for j, chip in enumerate(chips):
            copy(1 + j, chip, me).wait_recv()
            passed[j].start()
        copy(0, sibling, me).wait_recv()
        for j, (px, py, pc) in enumerate(chips):
            copy(4 + j, (px, py, 1 - pc), me).wait_recv()
        for cp in first + passed:
            cp.wait_send()
        mine.wait()
        total = all_ref[0]
        for d in range(1, NDEV):
            total = total + all_ref[d]
        tot[...] = total

    return pl.pallas_call(
        body, name="small_reduce", out_shape=jax.ShapeDtypeStruct((PACK_ROWS, D), F32),
        scratch_shapes=[pltpu.VMEM((PACK_ROWS, D), F32), pltpu.VMEM((NDEV, PACK_ROWS, D), F32),
                        pltpu.SemaphoreType.DMA((7,)), pltpu.SemaphoreType.DMA((7,)), pltpu.SemaphoreType.DMA],
    )(*[acc[k] for k in names], loss8)


def _small_adamw(tot, ws, ms, vs):
    names = list(SMALL)
    n = len(names)
    widths = SMALL_WIDTH

    def body(*refs):
        tot = refs[0]
        w_refs, m_refs, v_refs = [dict(zip(names, refs[1 + q * n:1 + (q + 1) * n])) for q in range(3)]
        outs = refs[1 + 3 * n:1 + 7 * n]
        g_out, d_out, m_out, v_out = [dict(zip(names, outs[q * n:(q + 1) * n])) for q in range(4)]
        cut = refs[1 + 7 * n]
        x, y, c = _place()
        dev = 4 * x + 2 * y + c
        for k in names:
            if k in VEC_ROW:
                g = tot[pl.ds(VEC_ROW[k], 1), pl.ds(0, widths[k])]
            else:
                rows, cols = (3, CW) if k == "conv_w" else (LR, GK)
                wd = widths[k]
                sel = jnp.where(_iota((cols, wd), 0) == dev * wd + _iota((cols, wd), 1), 1.0, 0.0).astype(BF16)
                cut[:, pl.ds(0, wd)] = _dot_exact_rhs(tot[pl.ds(MAT_ROW, LR), pl.ds(MAT_LANE[k], cols)], sel, 3)
                g = cut[pl.ds(0, rows), pl.ds(0, wd)]
            g_out[k][...] = g
            d_out[k][...], m_out[k][...], v_out[k][...] = _adamw_math(w_refs[k][...], g, m_refs[k][...],
                                                                       v_refs[k][...])

    shapes = [jax.ShapeDtypeStruct(ws[k].shape, F32) for k in names]
    res = pl.pallas_call(
        body, name="small_adamw", out_shape=shapes * 4, scratch_shapes=[pltpu.VMEM((LR, 128), F32)],
    )(tot, *[ws[k] for k in names], *[ms[k] for k in names], *[vs[k] for k in names])
    return {k: tuple(res[q * n + i] for q in range(4)) for i, k in enumerate(names)}


def _adamw_math(w, g, m, v):
    m = ADAM_B1 * m + (1.0 - ADAM_B1) * g
    v = ADAM_B2 * v + (1.0 - ADAM_B2) * (g * g)
    m_hat = m / (1.0 - ADAM_B1 ** ADAM_STEP)
    v_hat = v / (1.0 - ADAM_B2 ** ADAM_STEP)
    delta = -ADAM_LR * (m_hat / (jnp.sqrt(v_hat) + ADAM_EPS) + ADAM_WD * w)
    return delta, m, v


def _adamw(ws, ms, vs, owns, r2s, name, grads_transposed=False):
    n = len(ws)
    _, r, c = ws[0].shape
    tr = 256 if r % 256 == 0 else r

    def body(*refs):
        ins, outs = refs[:5 * n], refs[5 * n:]
        for q in range(n):
            w_ref, m_ref, v_ref, o_ref, r_ref = [ins[k * n + q] for k in range(5)]
            g_ref, d_ref, nm_ref, nv_ref = [outs[k * n + q] for k in range(4)]
            g = ((o_ref[...] + r_ref[0].astype(F32)) + r_ref[1].astype(F32)) + r_ref[2].astype(F32)
            g = g.T if grads_transposed else g
            g_ref[...] = g
            d_ref[...], nm_ref[...], nv_ref[...] = _adamw_math(w_ref[...], g, m_ref[...], v_ref[...])

    spec = pl.BlockSpec((None, tr, c), lambda i: (0, i, 0))
    if grads_transposed:
        own_spec, r2_spec = pl.BlockSpec((c, tr), lambda i: (0, i)), pl.BlockSpec((3, c, tr), lambda i: (0, 0, i))
    else:
        own_spec, r2_spec = pl.BlockSpec((tr, c), lambda i: (i, 0)), pl.BlockSpec((3, tr, c), lambda i: (0, i, 0))
    res = pl.pallas_call(
        body, name=name, grid=(r // tr,),
        in_specs=[spec] * (3 * n) + [own_spec] * n + [r2_spec] * n,
        out_specs=[spec] * (4 * n), out_shape=[jax.ShapeDtypeStruct((1, r, c), F32)] * (4 * n),
        compiler_params=_cparams(("arbitrary",)))(*ws, *ms, *vs, *owns, *r2s)
    return [tuple(res[k * n + q] for k in range(4)) for q in range(n)]


MATS = ("w_in", "w_out", "w_xq", "w_xo", "w_xkv", "w_up", "w_down")
SMALL = ("mix_norm", "conv_w", "conv_norm", "w_af", "b_af", "w_ab", "b_ab", "gla_norm", "xa_norm", "mem_norm",
         "mlp_norm", "final_norm")
WEIGHTS = ("mix_norm", "w_in", "conv_w", "conv_norm", "w_af", "b_af", "w_ab", "b_ab", "gla_norm", "w_out", "xa_norm",
           "mem_norm", "w_xq", "w_xkv", "w_xo", "mlp_norm", "w_up", "w_down", "final_norm")
SMALL_SHARDED = {"conv_w": (3, 64), "w_af": (LR, 32), "w_ab": (LR, 32)}
SMALL_PACK_ROWS = 16


def kernel(x, mem, mix_norm, w_in, conv_w, conv_norm, w_af, b_af, w_ab, b_ab, gla_norm, w_out, xa_norm, mem_norm, w_xq, w_xkv, w_xo, mlp_norm, w_up, w_down, final_norm, loss_target, m_mix_norm, m_w_in, m_conv_w, m_conv_norm, m_w_af, m_b_af, m_w_ab, m_b_ab, m_gla_norm, m_w_out, m_xa_norm, m_mem_norm, m_w_xq, m_w_xkv, m_w_xo, m_mlp_norm, m_w_up, m_w_down, m_final_norm, v_mix_norm, v_w_in, v_conv_w, v_conv_norm, v_w_af, v_b_af, v_w_ab, v_b_ab, v_gla_norm, v_w_out, v_xa_norm, v_mem_norm, v_w_xq, v_w_xkv, v_w_xo, v_mlp_norm, v_w_up, v_w_down, v_final_norm):
    w = dict(mix_norm=mix_norm, w_in=w_in, conv_w=conv_w, conv_norm=conv_norm, w_af=w_af, b_af=b_af, w_ab=w_ab,
             b_ab=b_ab, gla_norm=gla_norm, w_out=w_out, xa_norm=xa_norm, mem_norm=mem_norm, w_xq=w_xq, w_xkv=w_xkv,
             w_xo=w_xo, mlp_norm=mlp_norm, w_up=w_up, w_down=w_down, final_norm=final_norm)
    mom = dict(mix_norm=m_mix_norm, w_in=m_w_in, conv_w=m_conv_w, conv_norm=m_conv_norm, w_af=m_w_af, b_af=m_b_af,
               w_ab=m_w_ab, b_ab=m_b_ab, gla_norm=m_gla_norm, w_out=m_w_out, xa_norm=m_xa_norm, mem_norm=m_mem_norm,
               w_xq=m_w_xq, w_xkv=m_w_xkv, w_xo=m_w_xo, mlp_norm=m_mlp_norm, w_up=m_w_up, w_down=m_w_down,
               final_norm=m_final_norm)
    var = dict(mix_norm=v_mix_norm, w_in=v_w_in, conv_w=v_conv_w, conv_norm=v_conv_norm, w_af=v_w_af, b_af=v_b_af,
               w_ab=v_w_ab, b_ab=v_b_ab, gla_norm=v_gla_norm, w_out=v_w_out, xa_norm=v_xa_norm, mem_norm=v_mem_norm,
               w_xq=v_w_xq, w_xkv=v_w_xkv, w_xo=v_w_xo, mlp_norm=v_mlp_norm, w_up=v_w_up, w_down=v_w_down,
               final_norm=v_final_norm)
    xi, yi, ci = _place()
    two_d = lambda a: a.reshape(a.shape[-2:]) if a.ndim == 3 else a.reshape(1, a.shape[-1])

    small = jnp.concatenate([w[n].reshape(-1) for n in SMALL_SHARDED])
    small = jnp.pad(small, (0, SMALL_PACK_ROWS * 128 - small.shape[0])).reshape(SMALL_PACK_ROWS, 128)
    shard = {n: two_d(w[n]).astype(BF16) for n in MATS}
    for n in ("w_in", "w_up"):
        shard[n] = shard[n].T
    vec = {n: two_d(w[n]) for n in SMALL if n not in SMALL_SHARDED}
    place = jnp.stack([2 * xi + yi, ci]).astype(jnp.int32)
    loss8, grad_x, small_acc, own, from_chips = _step(x[0], mem[0], loss_target[0], shard, small, vec, place)

    tot = _small_reduce(small_acc, loss8)
    small_out = _small_adamw(tot, *[{n: two_d(d[n]) for n in SMALL} for d in (w, mom, var)])
    loss = tot[LOSS_ROW, 0]

    out_g, out_d, out_m, out_v = {}, {}, {}, {}
    wmv = {n: [a.transpose(0, 2, 1) if n == "w_in" else a for a in (w[n], mom[n], var[n])] for n in MATS}
    for shape in dict.fromkeys(wmv[n][0].shape for n in MATS):
        names = [n for n in MATS if wmv[n][0].shape == shape]
        res = _adamw(*[[wmv[n][k] for n in names] for k in range(3)], [own[n] for n in names],
                     [from_chips[n] for n in names], "adamw_" + "_".join(names), grads_transposed=names == ["w_up"])
        for n, r in zip(names, res):
            out_g[n], out_d[n], out_m[n], out_v[n] = [a.transpose(0, 2, 1) for a in r] if n == "w_in" else r
    for n in SMALL:
        out_g[n], out_d[n], out_m[n], out_v[n] = [a.reshape(w[n].shape) for a in small_out[n]]

    return (loss, grad_x[None], *[out_g[n] for n in WEIGHTS], *[out_d[n] for n in WEIGHTS],
            *[out_m[n] for n in WEIGHTS], *[out_v[n] for n in WEIGHTS])
```
